```python
import jax, jax.numpy as jnp
from jax import lax
import numpy as np

D_MODEL = 2048
BATCH = 8
SEQ = 2048
DEPTH = 1

HEAD_DIM = 128
N_Q_HEADS = 8
N_KV_HEADS = 2
Q_GROUP = N_Q_HEADS // N_KV_HEADS
ATTN_WIDTH = N_Q_HEADS * HEAD_DIM
KV_WIDTH = N_KV_HEADS * HEAD_DIM
WINDOW = 128
BLOCK = 128
ROPE_THETA = 10000.0
CONV_WIDTH = D_MODEL // 2
CONV_K = 3
N_BRANCH = 2
N_MEM = 256
MEM_HEADS = 4
MEM_HEAD_DIM = 128
MEM_WIDTH = MEM_HEADS * MEM_HEAD_DIM
D_FF = -(-8 * D_MODEL // (3 * 256)) * 256
RMS_EPS = 1e-6
NEG_INF = -1e30

IN_SIZES = (ATTN_WIDTH, KV_WIDTH, KV_WIDTH, CONV_WIDTH, CONV_WIDTH, CONV_WIDTH, N_BRANCH * D_MODEL)
IN_WIDTH = sum(IN_SIZES)
IN_SPLITS = tuple(int(i) for i in np.cumsum(IN_SIZES)[:-1])

kernel_name = "hybrid_gated_swa_shortconv_encoder"


def rms_norm(t, g):
    tf = t.astype(jnp.float32)
    y = tf * lax.rsqrt(jnp.mean(tf * tf, axis=-1, keepdims=True) + RMS_EPS)
    return (y * g.astype(jnp.float32)).astype(t.dtype)


def rope_tables(s):
    inv = 1.0 / (ROPE_THETA ** (jnp.arange(0, HEAD_DIM, 2, dtype=jnp.float32) / HEAD_DIM))
    ang = jnp.arange(s, dtype=jnp.float32)[:, None] * inv[None, :]
    return jnp.cos(ang), jnp.sin(ang)


def apply_rope(t, cos, sin):
    half = HEAD_DIM // 2
    t1, t2 = t[..., :half], t[..., half:]
    c = cos[None, :, None, :].astype(t.dtype)
    s = sin[None, :, None, :].astype(t.dtype)
    return jnp.concatenate([t1 * c - t2 * s, t2 * c + t1 * s], axis=-1)


def windowed_gqa_sink(q, k, v, sink):
    b, s = q.shape[0], q.shape[1]
    nb = s // BLOCK
    scale = HEAD_DIM ** -0.5
    qb = q.reshape(b, nb, BLOCK, N_KV_HEADS, Q_GROUP, HEAD_DIM)

    def band(t):
        tb = t.reshape(b, nb, BLOCK, N_KV_HEADS, HEAD_DIM)
        tp = jnp.pad(tb, ((0, 0), (1, 1), (0, 0), (0, 0), (0, 0)))
        return jnp.concatenate([tp[:, :-2], tp[:, 1:-1], tp[:, 2:]], axis=2)

    kb, vb = band(k), band(v)
    q_pos = jnp.arange(BLOCK)[:, None]
    k_off = jnp.arange(3 * BLOCK)[None, :] - BLOCK
    in_window = jnp.abs(k_off - q_pos) <= WINDOW
    k_abs = jnp.arange(nb)[:, None] * BLOCK + k_off
    in_range = (k_abs >= 0) & (k_abs < s)
    valid = in_window[None] & in_range[:, None, :]

    scores = jnp.einsum('bnqhgd,bnkhd->bnhgqk', qb, kb).astype(jnp.float32) * scale
    scores = jnp.where(valid[None, :, None, None], scores, NEG_INF)
    sink_col = jnp.broadcast_to(
        sink.astype(jnp.float32).reshape(1, 1, N_KV_HEADS, Q_GROUP, 1, 1),
        scores.shape[:-1] + (1,))
    probs = jax.nn.softmax(jnp.concatenate([scores, sink_col], axis=-1), axis=-1)[..., :-1]
    out = jnp.einsum('bnhgqk,bnkhd->bnqhgd', probs.astype(v.dtype), vb)
    return out.reshape(b, s, ATTN_WIDTH)


def short_conv_centred(u, w):
    up = jnp.pad(u, ((0, 0), (1, 1), (0, 0)))
    return up[:, :-2] * w[0] + up[:, 1:-1] * w[1] + up[:, 2:] * w[2]


def memory_cross_attention(h, mem_n, w_cq, w_ckv, w_co):
    b, s = h.shape[0], h.shape[1]
    m = mem_n.shape[1]
    q = (h @ w_cq).reshape(b, s, MEM_HEADS, MEM_HEAD_DIM)
    k, v = jnp.split(mem_n @ w_ckv, 2, axis=-1)
    k = k.reshape(b, m, MEM_HEADS, MEM_HEAD_DIM)
    v = v.reshape(b, m, MEM_HEADS, MEM_HEAD_DIM)
    scores = jnp.einsum('bshd,bmhd->bhsm', q, k).astype(jnp.float32) * (MEM_HEAD_DIM ** -0.5)
    probs = jax.nn.softmax(scores, axis=-1).astype(v.dtype)
    out = jnp.einsum('bhsm,bmhd->bshd', probs, v).reshape(b, s, MEM_WIDTH)
    return out @ w_co


def _fwd_setup_inputs(seed: int = 0) -> dict:
    key = jax.random.key(seed)
    ks = jax.random.split(key, 24)
    f32 = jnp.float32

    def w(k, shape, fan_in):
        return jax.random.normal(k, shape, f32) * (fan_in ** -0.5)

    def gain(k, shape):
        return 1.0 + 0.02 * jax.random.normal(k, shape, f32)

    L = DEPTH
    return {
        "x": jax.random.normal(ks[0], (BATCH, SEQ, D_MODEL), f32),
        "mem": jax.random.normal(ks[1], (BATCH, N_MEM, D_MODEL), f32),
        "g_mix": gain(ks[2], (L, D_MODEL)),
        "w_in": w(ks[3], (L, D_MODEL, IN_WIDTH), D_MODEL),
        "sink": 0.5 * jax.random.normal(ks[4], (L, N_Q_HEADS), f32),
        "conv_w": w(ks[5], (L, CONV_K, CONV_WIDTH), CONV_K),
        "b_gate": 0.1 * jax.random.normal(ks[6], (L, N_BRANCH * D_MODEL), f32),
        "w_attn_out": w(ks[7], (L, ATTN_WIDTH, D_MODEL), ATTN_WIDTH),
        "w_conv_out": w(ks[8], (L, CONV_WIDTH, D_MODEL), CONV_WIDTH),
        "w_o": w(ks[9], (L, D_MODEL, D_MODEL), D_MODEL),
        "g_cross": gain(ks[10], (L, D_MODEL)),
        "g_mem": gain(ks[11], (L, D_MODEL)),
        "w_cq": w(ks[12], (L, D_MODEL, MEM_WIDTH), D_MODEL),
        "w_ckv": w(ks[13], (L, D_MODEL, 2 * MEM_WIDTH), D_MODEL),
        "w_co": w(ks[14], (L, MEM_WIDTH, D_MODEL), MEM_WIDTH),
        "g_ffn": gain(ks[15], (L, D_MODEL)),
        "w_gate": w(ks[16], (L, D_MODEL, D_FF), D_MODEL),
        "w_up": w(ks[17], (L, D_MODEL, D_FF), D_MODEL),
        "w_down": w(ks[18], (L, D_FF, D_MODEL), D_FF),
        "g_final": gain(ks[19], (D_MODEL,)),
    }


def _fwd_reference(x, mem, g_mix, w_in, sink, conv_w, b_gate, w_attn_out, w_conv_out, w_o,
              g_cross, g_mem, w_cq, w_ckv, w_co, g_ffn, w_gate, w_up, w_down, g_final):
    b, s = x.shape[0], x.shape[1]
    cos, sin = rope_tables(s)
    for l in range(DEPTH):
        h = rms_norm(x, g_mix[l])
        z = h @ w_in[l]
        q, k, v, cu, cb, cc, gl = jnp.split(z, IN_SPLITS, axis=-1)

        q = apply_rope(q.reshape(b, s, N_Q_HEADS, HEAD_DIM), cos, sin)
        k = apply_rope(k.reshape(b, s, N_KV_HEADS, HEAD_DIM), cos, sin)
        v = v.reshape(b, s, N_KV_HEADS, HEAD_DIM)
        y_attn = windowed_gqa_sink(q, k, v, sink[l]) @ w_attn_out[l]

        y_conv = (cb * short_conv_centred(cc * cu, conv_w[l])) @ w_conv_out[l]

        g_a, g_c = jnp.split(jax.nn.sigmoid(gl + b_gate[l]), 2, axis=-1)
        x = x + (g_a * y_attn + g_c * y_conv) @ w_o[l]

        x = x + memory_cross_attention(rms_norm(x, g_cross[l]), rms_norm(mem, g_mem[l]),
                                       w_cq[l], w_ckv[l], w_co[l])

        hf = rms_norm(x, g_ffn[l])
        x = x + (jax.nn.silu(hf @ w_gate[l]) * (hf @ w_up[l])) @ w_down[l]
    return rms_norm(x, g_final)


import jax as _jax
import jax.numpy as _jnp

TWIN_FORMAT = 'train_step'
FWD_PARAMS = ['x', 'mem', 'g_mix', 'w_in', 'sink', 'conv_w', 'b_gate', 'w_attn_out', 'w_conv_out', 'w_o', 'g_cross', 'g_mem', 'w_cq', 'w_ckv', 'w_co', 'g_ffn', 'w_gate', 'w_up', 'w_down', 'g_final']
TWIN_WEIGHTS = ['g_mix', 'w_in', 'sink', 'conv_w', 'b_gate', 'w_attn_out', 'w_conv_out', 'w_o', 'g_cross', 'g_mem', 'w_cq', 'w_ckv', 'w_co', 'g_ffn', 'w_gate', 'w_up', 'w_down', 'g_final']
TWIN_DIFF_INPUT = 'x'
TWIN_INPUTS = ['x', 'mem', 'g_mix', 'w_in', 'sink', 'conv_w', 'b_gate', 'w_attn_out', 'w_conv_out', 'w_o', 'g_cross', 'g_mem', 'w_cq', 'w_ckv', 'w_co', 'g_ffn', 'w_gate', 'w_up', 'w_down', 'g_final', 'loss_target', 'm_g_mix', 'm_w_in', 'm_sink', 'm_conv_w', 'm_b_gate', 'm_w_attn_out', 'm_w_conv_out', 'm_w_o', 'm_g_cross', 'm_g_mem', 'm_w_cq', 'm_w_ckv', 'm_w_co', 'm_g_ffn', 'm_w_gate', 'm_w_up', 'm_w_down', 'm_g_final', 'v_g_mix', 'v_w_in', 'v_sink', 'v_conv_w', 'v_b_gate', 'v_w_attn_out', 'v_w_conv_out', 'v_w_o', 'v_g_cross', 'v_g_mem', 'v_w_cq', 'v_w_ckv', 'v_w_co', 'v_g_ffn', 'v_w_gate', 'v_w_up', 'v_w_down', 'v_g_final']
TWIN_OUTPUTS = ['loss', 'grad_x', 'grad_g_mix', 'grad_w_in', 'grad_sink', 'grad_conv_w', 'grad_b_gate', 'grad_w_attn_out', 'grad_w_conv_out', 'grad_w_o', 'grad_g_cross', 'grad_g_mem', 'grad_w_cq', 'grad_w_ckv', 'grad_w_co', 'grad_g_ffn', 'grad_w_gate', 'grad_w_up', 'grad_w_down', 'grad_g_final', 'delta_g_mix', 'delta_w_in', 'delta_sink', 'delta_conv_w', 'delta_b_gate', 'delta_w_attn_out', 'delta_w_conv_out', 'delta_w_o', 'delta_g_cross', 'delta_g_mem', 'delta_w_cq', 'delta_w_ckv', 'delta_w_co', 'delta_g_ffn', 'delta_w_gate', 'delta_w_up', 'delta_w_down', 'delta_g_final', 'new_m_g_mix', 'new_m_w_in', 'new_m_sink', 'new_m_conv_w', 'new_m_b_gate', 'new_m_w_attn_out', 'new_m_w_conv_out', 'new_m_w_o', 'new_m_g_cross', 'new_m_g_mem', 'new_m_w_cq', 'new_m_w_ckv', 'new_m_w_co', 'new_m_g_ffn', 'new_m_w_gate', 'new_m_w_up', 'new_m_w_down', 'new_m_g_final', 'new_v_g_mix', 'new_v_w_in', 'new_v_sink', 'new_v_conv_w', 'new_v_b_gate', 'new_v_w_attn_out', 'new_v_w_conv_out', 'new_v_w_o', 'new_v_g_cross', 'new_v_g_mem', 'new_v_w_cq', 'new_v_w_ckv', 'new_v_w_co', 'new_v_g_ffn', 'new_v_w_gate', 'new_v_w_up', 'new_v_w_down', 'new_v_g_final']
TWIN_LEAF_KINDS = {'loss': 'loss', 'grad_x': 'grad_x', 'grad_g_mix': 'grad_w', 'grad_w_in': 'grad_w', 'grad_sink': 'grad_w', 'grad_conv_w': 'grad_w', 'grad_b_gate': 'grad_w', 'grad_w_attn_out': 'grad_w', 'grad_w_conv_out': 'grad_w', 'grad_w_o': 'grad_w', 'grad_g_cross': 'grad_w', 'grad_g_mem': 'grad_w', 'grad_w_cq': 'grad_w', 'grad_w_ckv': 'grad_w', 'grad_w_co': 'grad_w', 'grad_g_ffn': 'grad_w', 'grad_w_gate': 'grad_w', 'grad_w_up': 'grad_w', 'grad_w_down': 'grad_w', 'grad_g_final': 'grad_w', 'delta_g_mix': 'delta_w', 'delta_w_in': 'delta_w', 'delta_sink': 'delta_w', 'delta_conv_w': 'delta_w', 'delta_b_gate': 'delta_w', 'delta_w_attn_out': 'delta_w', 'delta_w_conv_out': 'delta_w', 'delta_w_o': 'delta_w', 'delta_g_cross': 'delta_w', 'delta_g_mem': 'delta_w', 'delta_w_cq': 'delta_w', 'delta_w_ckv': 'delta_w', 'delta_w_co': 'delta_w', 'delta_g_ffn': 'delta_w', 'delta_w_gate': 'delta_w', 'delta_w_up': 'delta_w', 'delta_w_down': 'delta_w', 'delta_g_final': 'delta_w', 'new_m_g_mix': 'new_m', 'new_m_w_in': 'new_m', 'new_m_sink': 'new_m', 'new_m_conv_w': 'new_m', 'new_m_b_gate': 'new_m', 'new_m_w_attn_out': 'new_m', 'new_m_w_conv_out': 'new_m', 'new_m_w_o': 'new_m', 'new_m_g_cross': 'new_m', 'new_m_g_mem': 'new_m', 'new_m_w_cq': 'new_m', 'new_m_w_ckv': 'new_m', 'new_m_w_co': 'new_m', 'new_m_g_ffn': 'new_m', 'new_m_w_gate': 'new_m', 'new_m_w_up': 'new_m', 'new_m_w_down': 'new_m', 'new_m_g_final': 'new_m', 'new_v_g_mix': 'new_v', 'new_v_w_in': 'new_v', 'new_v_sink': 'new_v', 'new_v_conv_w': 'new_v', 'new_v_b_gate': 'new_v', 'new_v_w_attn_out': 'new_v', 'new_v_w_conv_out': 'new_v', 'new_v_w_o': 'new_v', 'new_v_g_cross': 'new_v', 'new_v_g_mem': 'new_v', 'new_v_w_cq': 'new_v', 'new_v_w_ckv': 'new_v', 'new_v_w_co': 'new_v', 'new_v_g_ffn': 'new_v', 'new_v_w_gate': 'new_v', 'new_v_w_up': 'new_v', 'new_v_w_down': 'new_v', 'new_v_g_final': 'new_v'}


def _forward(args):
    return _fwd_reference(*[args[k] for k in FWD_PARAMS])


def _output_shape():
    out = _jax.eval_shape(lambda: _forward(_fwd_setup_inputs(0)))
    return out.shape, out.dtype

N_MICROBATCH = 1
ADAM_LR = 0.001
ADAM_B1 = 0.9
ADAM_B2 = 0.999
ADAM_EPS = 1e-08
ADAM_WD = 0.01
ADAM_STEP = 10
PER_EXAMPLE_BATCH_AXIS = {'x': 0, 'mem': 0, 'loss_target': 0}
SHARED_INPUTS = []
_WEIGHT_DTYPES = {'g_mix': _jnp.float32, 'w_in': _jnp.float32, 'sink': _jnp.float32, 'conv_w': _jnp.float32, 'b_gate': _jnp.float32, 'w_attn_out': _jnp.float32, 'w_conv_out': _jnp.float32, 'w_o': _jnp.float32, 'g_cross': _jnp.float32, 'g_mem': _jnp.float32, 'w_cq': _jnp.float32, 'w_ckv': _jnp.float32, 'w_co': _jnp.float32, 'g_ffn': _jnp.float32, 'w_gate': _jnp.float32, 'w_up': _jnp.float32, 'w_down': _jnp.float32, 'g_final': _jnp.float32}
MOMENT_SCALE = {'g_mix': 6.094245e-02, 'w_in': 2.842377e-02, 'sink': 2.648620e-04, 'conv_w': 4.741601e-02, 'b_gate': 9.172088e-03, 'w_attn_out': 3.936506e-03, 'w_conv_out': 3.294636e-02, 'w_o': 3.297455e-02, 'g_cross': 6.326466e-03, 'g_mem': 9.181070e-03, 'w_cq': 1.243410e-02, 'w_ckv': 1.261009e-02, 'w_co': 6.433400e-03, 'g_ffn': 4.226638e-02, 'w_gate': 1.805940e-02, 'w_up': 1.745238e-02, 'w_down': 2.896027e-02, 'g_final': 8.000216e+00}


def _to_microbatches(a, axis):
    t = _jnp.moveaxis(a, axis, 0)
    t = t.reshape((N_MICROBATCH, t.shape[0] // N_MICROBATCH) + t.shape[1:])
    return _jnp.moveaxis(t, 1, axis + 1)


def setup_inputs(seed: int = 0) -> dict:
    inp = _fwd_setup_inputs(seed)
    key = _jax.random.fold_in(_jax.random.key(seed), 7919)
    shape, _ = _output_shape()
    out = dict(inp)
    out["loss_target"] = _jax.random.normal(_jax.random.fold_in(key, 0), shape, _jnp.float32)
    for i, name in enumerate(TWIN_WEIGHTS):
        w = inp[name].astype(_jnp.float32)
        if MOMENT_SCALE is None:
            s = _jnp.sqrt(_jnp.mean(_jnp.square(w)) + 1e-30)
        else:
            s = MOMENT_SCALE[name]
        km, kv = _jax.random.split(_jax.random.fold_in(key, i + 1))
        out[name] = w
        out["m_" + name] = s * _jax.random.normal(km, w.shape, _jnp.float32)
        out["v_" + name] = (s * s) * _jax.random.uniform(kv, w.shape, _jnp.float32, 0.5, 1.5)
    if N_MICROBATCH > 1:
        for name, axis in PER_EXAMPLE_BATCH_AXIS.items():
            out[name] = _to_microbatches(out[name], axis)
    return {'x': out['x'], 'mem': out['mem'], 'g_mix': out['g_mix'], 'w_in': out['w_in'], 'sink': out['sink'], 'conv_w': out['conv_w'], 'b_gate': out['b_gate'], 'w_attn_out': out['w_attn_out'], 'w_conv_out': out['w_conv_out'], 'w_o': out['w_o'], 'g_cross': out['g_cross'], 'g_mem': out['g_mem'], 'w_cq': out['w_cq'], 'w_ckv': out['w_ckv'], 'w_co': out['w_co'], 'g_ffn': out['g_ffn'], 'w_gate': out['w_gate'], 'w_up': out['w_up'], 'w_down': out['w_down'], 'g_final': out['g_final'], 'loss_target': out['loss_target'], 'm_g_mix': out['m_g_mix'], 'm_w_in': out['m_w_in'], 'm_sink': out['m_sink'], 'm_conv_w': out['m_conv_w'], 'm_b_gate': out['m_b_gate'], 'm_w_attn_out': out['m_w_attn_out'], 'm_w_conv_out': out['m_w_conv_out'], 'm_w_o': out['m_w_o'], 'm_g_cross': out['m_g_cross'], 'm_g_mem': out['m_g_mem'], 'm_w_cq': out['m_w_cq'], 'm_w_ckv': out['m_w_ckv'], 'm_w_co': out['m_w_co'], 'm_g_ffn': out['m_g_ffn'], 'm_w_gate': out['m_w_gate'], 'm_w_up': out['m_w_up'], 'm_w_down': out['m_w_down'], 'm_g_final': out['m_g_final'], 'v_g_mix': out['v_g_mix'], 'v_w_in': out['v_w_in'], 'v_sink': out['v_sink'], 'v_conv_w': out['v_conv_w'], 'v_b_gate': out['v_b_gate'], 'v_w_attn_out': out['v_w_attn_out'], 'v_w_conv_out': out['v_w_conv_out'], 'v_w_o': out['v_w_o'], 'v_g_cross': out['v_g_cross'], 'v_g_mem': out['v_g_mem'], 'v_w_cq': out['v_w_cq'], 'v_w_ckv': out['v_w_ckv'], 'v_w_co': out['v_w_co'], 'v_g_ffn': out['v_g_ffn'], 'v_w_gate': out['v_w_gate'], 'v_w_up': out['v_w_up'], 'v_w_down': out['v_w_down'], 'v_g_final': out['v_g_final']}


def _loss(weights, diff, rest, loss_target):
    with _jax.named_scope("forward"):
        args = {**rest, TWIN_DIFF_INPUT: diff, **{k: w.astype(_WEIGHT_DTYPES[k]) for k, w in weights.items()}}
        y = _forward(args)
    with _jax.named_scope("loss_head"):
        err = _jnp.square(y.astype(_jnp.float32) - loss_target)
        return 0.5 * _jnp.sum(_jnp.mean(err, axis=-1)) if err.ndim else 0.5 * err


def _adamw(w, g, m, v):
    m = ADAM_B1 * m + (1.0 - ADAM_B1) * g
    v = ADAM_B2 * v + (1.0 - ADAM_B2) * _jnp.square(g)
    m_hat = m / (1.0 - ADAM_B1 ** ADAM_STEP)
    v_hat = v / (1.0 - ADAM_B2 ** ADAM_STEP)
    delta = -ADAM_LR * (m_hat / (_jnp.sqrt(v_hat) + ADAM_EPS) + ADAM_WD * w)
    return delta, m, v


def reference(x, mem, g_mix, w_in, sink, conv_w, b_gate, w_attn_out, w_conv_out, w_o, g_cross, g_mem, w_cq, w_ckv, w_co, g_ffn, w_gate, w_up, w_down, g_final, loss_target, m_g_mix, m_w_in, m_sink, m_conv_w, m_b_gate, m_w_attn_out, m_w_conv_out, m_w_o, m_g_cross, m_g_mem, m_w_cq, m_w_ckv, m_w_co, m_g_ffn, m_w_gate, m_w_up, m_w_down, m_g_final, v_g_mix, v_w_in, v_sink, v_conv_w, v_b_gate, v_w_attn_out, v_w_conv_out, v_w_o, v_g_cross, v_g_mem, v_w_cq, v_w_ckv, v_w_co, v_g_ffn, v_w_gate, v_w_up, v_w_down, v_g_final):
    given = dict(x=x, mem=mem, g_mix=g_mix, w_in=w_in, sink=sink, conv_w=conv_w, b_gate=b_gate, w_attn_out=w_attn_out, w_conv_out=w_conv_out, w_o=w_o, g_cross=g_cross, g_mem=g_mem, w_cq=w_cq, w_ckv=w_ckv, w_co=w_co, g_ffn=g_ffn, w_gate=w_gate, w_up=w_up, w_down=w_down, g_final=g_final, loss_target=loss_target, m_g_mix=m_g_mix, m_w_in=m_w_in, m_sink=m_sink, m_conv_w=m_conv_w, m_b_gate=m_b_gate, m_w_attn_out=m_w_attn_out, m_w_conv_out=m_w_conv_out, m_w_o=m_w_o, m_g_cross=m_g_cross, m_g_mem=m_g_mem, m_w_cq=m_w_cq, m_w_ckv=m_w_ckv, m_w_co=m_w_co, m_g_ffn=m_g_ffn, m_w_gate=m_w_gate, m_w_up=m_w_up, m_w_down=m_w_down, m_g_final=m_g_final, v_g_mix=v_g_mix, v_w_in=v_w_in, v_sink=v_sink, v_conv_w=v_conv_w, v_b_gate=v_b_gate, v_w_attn_out=v_w_attn_out, v_w_conv_out=v_w_conv_out, v_w_o=v_w_o, v_g_cross=v_g_cross, v_g_mem=v_g_mem, v_w_cq=v_w_cq, v_w_ckv=v_w_ckv, v_w_co=v_w_co, v_g_ffn=v_g_ffn, v_w_gate=v_w_gate, v_w_up=v_w_up, v_w_down=v_w_down, v_g_final=v_g_final)
    weights = {n: given[n] for n in TWIN_WEIGHTS}
    shared = {n: given[n] for n in SHARED_INPUTS}
    per_example = {n: given[n] for n in ['x', 'mem']}
    grad_fn = _jax.value_and_grad(_loss, argnums=(0, 1))

    def one_microbatch(ex, loss_target):
        ex = dict(ex)
        diff = ex.pop(TWIN_DIFF_INPUT)
        return grad_fn(weights, diff, {**shared, **ex}, loss_target)

    if N_MICROBATCH == 1:
        loss, (grad_w, grad_x) = one_microbatch(per_example, given["loss_target"])
    else:
        def body(carry, xs):
            loss_sum, grad_sum = carry
            l_k, (gw_k, gx_k) = one_microbatch(xs[0], xs[1])
            with _jax.named_scope("update"):
                return (loss_sum + l_k, _jax.tree.map(_jnp.add, grad_sum, gw_k)), gx_k

        init = (_jnp.zeros((), _jnp.float32), _jax.tree.map(_jnp.zeros_like, weights))
        (loss, grad_w), grad_x = _jax.lax.scan(body, init, (per_example, given["loss_target"]))
    with _jax.named_scope("update"):
        delta_w, new_m, new_v = {}, {}, {}
        for n in TWIN_WEIGHTS:
            delta_w[n], new_m[n], new_v[n] = _adamw(weights[n], grad_w[n], given["m_" + n], given["v_" + n])
    return (loss, grad_x, *[grad_w[n] for n in TWIN_WEIGHTS], *[delta_w[n] for n in TWIN_WEIGHTS],
            *[new_m[n] for n in TWIN_WEIGHTS], *[new_v[n] for n in TWIN_WEIGHTS])
```

```python
import functools

import jax
import jax.numpy as jnp
from jax import lax
from jax.experimental import pallas as pl
from jax.experimental.pallas import tpu as pltpu

F32 = jnp.float32
BF16 = jnp.bfloat16
MESH = pl.DeviceIdType.MESH
ANY = pl.BlockSpec(memory_space=pl.ANY)

VMEM_LIMIT_BYTES = 56 * 1024 * 1024

N_CHIPS = 4
HEAD_DIM = 128
N_Q_HEADS = 8
N_KV_HEADS = 2
Q_GROUP = N_Q_HEADS // N_KV_HEADS
ATTN_WIDTH = N_Q_HEADS * HEAD_DIM
KV_WIDTH = N_KV_HEADS * HEAD_DIM
WINDOW = 128
BLOCK = 128
BAND = 3 * BLOCK
ROPE_THETA = 10000.0
CONV_WIDTH = 1024
MEM_HEADS = 4
MEM_WIDTH = MEM_HEADS * HEAD_DIM
RMS_EPS = 1e-6
NEG_INF = -1e30
ATTN_SCALE = HEAD_DIM ** -0.5

Q_OFF, K_OFF, V_OFF, CU_OFF, CB_OFF, CC_OFF, GL_OFF = 0, 1024, 1280, 1536, 2560, 3584, 4608

ADAM_LR = 0.001
ADAM_B1 = 0.9
ADAM_B2 = 0.999
ADAM_EPS = 1e-08
ADAM_WD = 0.01
ADAM_STEP = 10
ADAM_C1 = 1.0 - ADAM_B1 ** ADAM_STEP
ADAM_C2 = 1.0 - ADAM_B2 ** ADAM_STEP


def _params(n_grid_axes):
    return pltpu.CompilerParams(dimension_semantics=("arbitrary",) * n_grid_axes, vmem_limit_bytes=VMEM_LIMIT_BYTES)


def _row_tile(rows, want):
    t = min(want, rows)
    while rows % t:
        t //= 2
    return t


def _matmul(a, b, *, mode, tm, tn, tk, out_dtypes, name, extras=(), epilogue=None, b_blocks=1, out_blocks=1):
    if mode == "tn":
        kdim, m = a.shape
    else:
        m, kdim = a.shape
    if b_blocks > 1:
        nb, brows, bcols = b.shape
        assert nb == b_blocks
        if mode == "nn":
            n = bcols * nb
            assert brows == kdim
        else:
            assert mode == "nt" and bcols * nb == kdim
            n = brows
    else:
        n = b.shape[0] if mode == "nt" else b.shape[1]
    tm, tn, tk = min(tm, m), min(tn, n), min(tk, kdim)
    assert m % tm == 0 and n % tn == 0 and kdim % tk == 0, (name, m, n, kdim, tm, tn, tk)
    nk = kdim // tk
    n_extra, n_out = len(extras), len(out_dtypes)

    if mode == "tn":
        a_spec = pl.BlockSpec((tk, tm), lambda j, i, k: (k, i))
        dims = (((0,), (0,)), ((), ()))
    else:
        a_spec = pl.BlockSpec((tm, tk), lambda j, i, k: (i, k))
        dims = (((1,), (0,)), ((), ())) if mode == "nn" else (((1,), (1,)), ((), ()))

    if b_blocks > 1 and mode == "nn":
        per = b.shape[2] // tn
        assert b.shape[2] % tn == 0
        b_spec = pl.BlockSpec((None, tk, tn), lambda j, i, k: (j // per, k, j % per))
    elif b_blocks > 1:
        per = b.shape[2] // tk
        assert b.shape[2] % tk == 0
        b_spec = pl.BlockSpec((None, tn, tk), lambda j, i, k: (k // per, j, k % per))
    elif mode == "nt":
        b_spec = pl.BlockSpec((tn, tk), lambda j, i, k: (j, k))
    else:
        b_spec = pl.BlockSpec((tk, tn), lambda j, i, k: (k, j))

    tile_spec = pl.BlockSpec((tm, tn), lambda j, i, k: (i, j))
    if out_blocks > 1:
        ncols = n // out_blocks
        assert ncols % tn == 0
        oper = ncols // tn
        out_spec = pl.BlockSpec((None, tm, tn), lambda j, i, k: (j // oper, i, j % oper))
        out_shape = [jax.ShapeDtypeStruct((out_blocks, m, ncols), dt) for dt in out_dtypes]
    else:
        out_spec = tile_spec
        out_shape = [jax.ShapeDtypeStruct((m, n), dt) for dt in out_dtypes]

    def body(a_ref, b_ref, *rest):
        extra_refs = rest[:n_extra]
        out_refs = rest[n_extra:n_extra + n_out]

        def finish(acc):
            if epilogue is None:
                tiles = (acc,)
            else:
                tiles = epilogue(acc, *[r[...] for r in extra_refs])
            for o_ref, t in zip(out_refs, tiles, strict=True):
                o_ref[...] = t.astype(o_ref.dtype)

        part = lax.dot_general(a_ref[...].astype(BF16), b_ref[...].astype(BF16), dims, preferred_element_type=F32)
        if nk == 1:
            finish(part)
        else:
            acc_ref = rest[-1]
            k = pl.program_id(2)

            @pl.when(k == 0)
            def _():
                acc_ref[...] = part

            @pl.when(k > 0)
            def _():
                acc_ref[...] += part

            @pl.when(k == nk - 1)
            def _():
                finish(acc_ref[...])

    outs = pl.pallas_call(
        body,
        name=name,
        grid=(n // tn, m // tm, nk),
        in_specs=[a_spec, b_spec] + [tile_spec] * n_extra,
        out_specs=[out_spec] * n_out,
        out_shape=out_shape,
        scratch_shapes=[pltpu.VMEM((tm, tn), F32)] if nk > 1 else [],
        compiler_params=_params(3),
    )(a, b, *extras)
    return outs[0] if n_out == 1 else outs


def _add_residual(acc, res):
    return (acc + res,)


def _rstd(x):
    return lax.rsqrt(jnp.mean(x * x, axis=-1, keepdims=True) + RMS_EPS)


def _rmsnorm(x, g, name):
    s, d = x.shape
    tr = _row_tile(s, 256)

    def body(x_ref, g_ref, o_ref):
        xv = x_ref[...]
        o_ref[...] = (xv * _rstd(xv) * g_ref[...]).astype(BF16)

    return pl.pallas_call(
        body, name=name, grid=(s // tr,),
        in_specs=[pl.BlockSpec((tr, d), lambda i: (i, 0)), pl.BlockSpec((1, d), lambda i: (0, 0))],
        out_specs=pl.BlockSpec((tr, d), lambda i: (i, 0)),
        out_shape=jax.ShapeDtypeStruct((s, d), BF16),
        compiler_params=_params(1),
    )(x, g)


def _rmsnorm_bwd(dh, x, g, dres, name):
    s, d = x.shape
    tr = _row_tile(s, 256)
    has_res = dres is not None

    def body(*refs):
        if has_res:
            dh_ref, x_ref, g_ref, res_ref, dx_ref, dxb_ref, dg_ref = refs
        else:
            dh_ref, x_ref, g_ref, dx_ref, dxb_ref, dg_ref = refs
        xv = x_ref[...]
        dhv = dh_ref[...].astype(F32)
        r = _rstd(xv)
        xn = xv * r
        dhg = dhv * g_ref[...]
        dx = r * (dhg - xn * jnp.mean(dhg * xn, axis=-1, keepdims=True))
        if has_res:
            dx = dx + res_ref[...]
        dx_ref[...] = dx
        dxb_ref[...] = dx.astype(BF16)
        part = jnp.sum(dhv * xn, axis=0, keepdims=True)

        @pl.when(pl.program_id(0) == 0)
        def _():
            dg_ref[...] = part

        @pl.when(pl.program_id(0) > 0)
        def _():
            dg_ref[...] += part

    row = pl.BlockSpec((tr, d), lambda i: (i, 0))
    vec = pl.BlockSpec((1, d), lambda i: (0, 0))
    return pl.pallas_call(
        body, name=name, grid=(s // tr,),
        in_specs=[row, row, vec] + ([row] if has_res else []),
        out_specs=[row, row, vec],
        out_shape=[jax.ShapeDtypeStruct((s, d), F32), jax.ShapeDtypeStruct((s, d), BF16), jax.ShapeDtypeStruct((1, d), F32)],
        compiler_params=_params(1),
    )(*([dh, x, g] + ([dres] if has_res else [])))


def _loss_head(x3, g, target):
    s, d = x3.shape
    tr = _row_tile(s, 256)

    def body(x_ref, g_ref, t_ref, dx_ref, dxb_ref, sq_ref, dg_ref):
        xv = x_ref[...]
        gv = g_ref[...]
        r = _rstd(xv)
        xn = xv * r
        err = xn * gv - t_ref[...]
        dy = err * (1.0 / d)
        dyg = dy * gv
        dx = r * (dyg - xn * jnp.mean(dyg * xn, axis=-1, keepdims=True))
        dx_ref[...] = dx
        dxb_ref[...] = dx.astype(BF16)
        sq = jnp.sum(jnp.sum(err * err, axis=1, keepdims=True), axis=0, keepdims=True)
        sq = jnp.broadcast_to(sq, (1, 128))
        part = jnp.sum(dy * xn, axis=0, keepdims=True)

        @pl.when(pl.program_id(0) == 0)
        def _():
            sq_ref[...] = sq
            dg_ref[...] = part

        @pl.when(pl.program_id(0) > 0)
        def _():
            sq_ref[...] += sq
            dg_ref[...] += part

    row = pl.BlockSpec((tr, d), lambda i: (i, 0))
    vec = pl.BlockSpec((1, d), lambda i: (0, 0))
    return pl.pallas_call(
        body, name="loss_head", grid=(s // tr,),
        in_specs=[row, vec, row],
        out_specs=[row, row, pl.BlockSpec((1, 128), lambda i: (0, 0)), vec],
        out_shape=[jax.ShapeDtypeStruct((s, d), F32), jax.ShapeDtypeStruct((s, d), BF16),
                   jax.ShapeDtypeStruct((1, 128), F32), jax.ShapeDtypeStruct((1, d), F32)],
        compiler_params=_params(1),
    )(x3, g, target)


def _rope_tables(s):
    inv = 1.0 / (ROPE_THETA ** (jnp.arange(0, HEAD_DIM, 2, dtype=F32) / HEAD_DIM))
    ang = jnp.arange(s, dtype=F32)[:, None] * inv[None, :]
    cos, sin = jnp.cos(ang), jnp.sin(ang)
    return jnp.concatenate([cos, cos], axis=1), jnp.concatenate([-sin, sin], axis=1)


def _swap_halves(t):
    return pltpu.roll(t, HEAD_DIM // 2, 1)


def _rope_fwd(z, cos_t, sin_t):
    s = z.shape[0]
    tr = _row_tile(s, 256)

    def body(zq_ref, zk_ref, zv_ref, c_ref, s_ref, q_ref, k_ref, v_ref):
        c, sn = c_ref[...], s_ref[...]
        for hd in range(N_Q_HEADS):
            cols = slice(hd * HEAD_DIM, (hd + 1) * HEAD_DIM)
            t = zq_ref[:, cols]
            q_ref[:, cols] = (t * c + _swap_halves(t) * sn).astype(BF16)
        for hd in range(N_KV_HEADS):
            cols = slice(hd * HEAD_DIM, (hd + 1) * HEAD_DIM)
            t = zk_ref[:, cols]
            k_ref[:, cols] = (t * c + _swap_halves(t) * sn).astype(BF16)
        v_ref[...] = zv_ref[...].astype(BF16)

    tab = pl.BlockSpec((tr, HEAD_DIM), lambda i: (i, 0))
    return pl.pallas_call(
        body, name="rope_fwd", grid=(s // tr,),
        in_specs=[pl.BlockSpec((tr, ATTN_WIDTH), lambda i: (i, Q_OFF // ATTN_WIDTH)),
                  pl.BlockSpec((tr, KV_WIDTH), lambda i: (i, K_OFF // KV_WIDTH)),
                  pl.BlockSpec((tr, KV_WIDTH), lambda i: (i, V_OFF // KV_WIDTH)), tab, tab],
        out_specs=[pl.BlockSpec((tr, ATTN_WIDTH), lambda i: (i, 0)), pl.BlockSpec((tr, KV_WIDTH), lambda i: (i, 0)),
                   pl.BlockSpec((tr, KV_WIDTH), lambda i: (i, 0))],
        out_shape=[jax.ShapeDtypeStruct((s, ATTN_WIDTH), BF16), jax.ShapeDtypeStruct((s, KV_WIDTH), BF16),
                   jax.ShapeDtypeStruct((s, KV_WIDTH), BF16)],
        compiler_params=_params(1),
    )(z, z, z, cos_t, sin_t)


def _rope_bwd(dq_rot, dk_rot, dv, cos_t, sin_t):
    s = dq_rot.shape[0]
    tr = _row_tile(s, 256)

    def body(dq_ref, dk_ref, dv_ref, c_ref, s_ref, oq_ref, ok_ref, ov_ref):
        c, sn = c_ref[...], s_ref[...]
        for hd in range(N_Q_HEADS):
            cols = slice(hd * HEAD_DIM, (hd + 1) * HEAD_DIM)
            t = dq_ref[:, cols]
            oq_ref[:, cols] = (t * c + _swap_halves(t * sn)).astype(BF16)
        for hd in range(N_KV_HEADS):
            cols = slice(hd * HEAD_DIM, (hd + 1) * HEAD_DIM)
            t = dk_ref[:, cols]
            ok_ref[:, cols] = (t * c + _swap_halves(t * sn)).astype(BF16)
        ov_ref[...] = dv_ref[...].astype(BF16)

    tab = pl.BlockSpec((tr, HEAD_DIM), lambda i: (i, 0))
    wide = pl.BlockSpec((tr, ATTN_WIDTH), lambda i: (i, 0))
    narrow = pl.BlockSpec((tr, KV_WIDTH), lambda i: (i, 0))
    return pl.pallas_call(
        body, name="rope_bwd", grid=(s // tr,),
        in_specs=[wide, narrow, narrow, tab, tab],
        out_specs=[wide, narrow, narrow],
        out_shape=[jax.ShapeDtypeStruct((s, ATTN_WIDTH), BF16), jax.ShapeDtypeStruct((s, KV_WIDTH), BF16),
                   jax.ShapeDtypeStruct((s, KV_WIDTH), BF16)],
        compiler_params=_params(1),
    )(dq_rot, dk_rot, dv, cos_t, sin_t)


def _swa_band(i, s):
    return pl.multiple_of(jnp.clip((i - 1) * BLOCK, 0, s - BAND), BLOCK)


def _swa_probs(q_ref, k_ref, sink_ref, kv, start, valid):
    cols = slice(kv * HEAD_DIM, (kv + 1) * HEAD_DIM)
    kb = k_ref[pl.ds(start, BAND), cols]
    heads = [kv * Q_GROUP + g for g in range(Q_GROUP)]
    qg = jnp.concatenate([q_ref[:, hd * HEAD_DIM:(hd + 1) * HEAD_DIM] for hd in heads], axis=0)
    sc = lax.dot_general(qg, kb, (((1,), (1,)), ((), ())), preferred_element_type=F32) * ATTN_SCALE
    sc = jnp.where(valid, sc, NEG_INF)
    sk = jnp.concatenate([jnp.full((BLOCK, 1), sink_ref[hd], F32) for hd in heads], axis=0)
    mx = jnp.maximum(jnp.max(sc, axis=1, keepdims=True), sk)
    e = jnp.exp(sc - mx)
    es = jnp.exp(sk - mx)
    inv = 1.0 / (jnp.sum(e, axis=1, keepdims=True) + es)
    return qg, kb, e * inv, es * inv


def _swa_valid(i, start):
    q_pos = i * BLOCK + lax.broadcasted_iota(jnp.int32, (BLOCK, 1), 0)
    q_pos = jnp.concatenate([q_pos] * Q_GROUP, axis=0)
    k_pos = start + lax.broadcasted_iota(jnp.int32, (1, BAND), 1)
    return jnp.abs(k_pos - q_pos) <= WINDOW


def _swa_fwd(q, k, v, sink):
    s = q.shape[0]
    assert s % BLOCK == 0 and s >= BAND

    def body(sink_ref, q_ref, k_ref, v_ref, o_ref):
        i = pl.program_id(0)
        start = _swa_band(i, s)
        valid = _swa_valid(i, start)
        for kv in range(N_KV_HEADS):
            _, _, p, _ = _swa_probs(q_ref, k_ref, sink_ref, kv, start, valid)
            vb = v_ref[pl.ds(start, BAND), kv * HEAD_DIM:(kv + 1) * HEAD_DIM]
            o = jnp.dot(p.astype(BF16), vb, preferred_element_type=F32)
            for g in range(Q_GROUP):
                hd = kv * Q_GROUP + g
                o_ref[:, hd * HEAD_DIM:(hd + 1) * HEAD_DIM] = o[g * BLOCK:(g + 1) * BLOCK].astype(BF16)

    whole = pl.BlockSpec((s, KV_WIDTH), lambda i: (0, 0))
    blk = pl.BlockSpec((BLOCK, ATTN_WIDTH), lambda i: (i, 0))
    return pl.pallas_call(
        body, name="swa_fwd", grid=(s // BLOCK,),
        in_specs=[pl.BlockSpec(memory_space=pltpu.SMEM), blk, whole, whole],
        out_specs=blk,
        out_shape=jax.ShapeDtypeStruct((s, ATTN_WIDTH), BF16),
        compiler_params=_params(1),
    )(sink, q, k, v)


def _swa_bwd(q, k, v, d_out, sink):
    s = q.shape[0]

    def body(sink_ref, q_ref, k_ref, v_ref, do_ref, dq_ref, dk_ref, dv_ref, dsink_ref):
        i = pl.program_id(0)

        @pl.when(i == 0)
        def _():
            dk_ref[...] = jnp.zeros_like(dk_ref)
            dv_ref[...] = jnp.zeros_like(dv_ref)
            dsink_ref[...] = jnp.zeros_like(dsink_ref)

        start = _swa_band(i, s)
        valid = _swa_valid(i, start)
        for kv in range(N_KV_HEADS):
            cols = slice(kv * HEAD_DIM, (kv + 1) * HEAD_DIM)
            qg, kb, p, p_sink = _swa_probs(q_ref, k_ref, sink_ref, kv, start, valid)
            vb = v_ref[pl.ds(start, BAND), cols]
            heads = [kv * Q_GROUP + g for g in range(Q_GROUP)]
            dog = jnp.concatenate([do_ref[:, hd * HEAD_DIM:(hd + 1) * HEAD_DIM] for hd in heads], axis=0)
            dp = lax.dot_general(dog, vb, (((1,), (1,)), ((), ())), preferred_element_type=F32)
            delta = jnp.sum(p * dp, axis=1, keepdims=True)
            ds = (p * (dp - delta) * ATTN_SCALE).astype(BF16)
            dqg = jnp.dot(ds, kb, preferred_element_type=F32)
            dk_ref[pl.ds(start, BAND), cols] += lax.dot_general(ds, qg, (((0,), (0,)), ((), ())), preferred_element_type=F32)
            dv_ref[pl.ds(start, BAND), cols] += lax.dot_general(p.astype(BF16), dog, (((0,), (0,)), ((), ())),
                                                                 preferred_element_type=F32)
            dsk = p_sink * delta
            for g, hd in enumerate(heads):
                dq_ref[:, hd * HEAD_DIM:(hd + 1) * HEAD_DIM] = dqg[g * BLOCK:(g + 1) * BLOCK]
                tot = jnp.sum(dsk[g * BLOCK:(g + 1) * BLOCK], axis=0, keepdims=True)
                dsink_ref[hd:hd + 1, :] -= jnp.broadcast_to(tot, (1, 128))

    whole = pl.BlockSpec((s, KV_WIDTH), lambda i: (0, 0))
    blk = pl.BlockSpec((BLOCK, ATTN_WIDTH), lambda i: (i, 0))
    return pl.pallas_call(
        body, name="swa_bwd", grid=(s // BLOCK,),
        in_specs=[pl.BlockSpec(memory_space=pltpu.SMEM), blk, whole, whole, blk],
        out_specs=[blk, whole, whole, pl.BlockSpec((N_Q_HEADS, 128), lambda i: (0, 0))],
        out_shape=[jax.ShapeDtypeStruct((s, ATTN_WIDTH), F32), jax.ShapeDtypeStruct((s, KV_WIDTH), F32),
                   jax.ShapeDtypeStruct((s, KV_WIDTH), F32), jax.ShapeDtypeStruct((N_Q_HEADS, 128), F32)],
        compiler_params=_params(1),
    )(sink, q, k, v, d_out)


CONV_CHUNK = 256


def _shift_rows(t, rows, down):
    n = t.shape[0]
    rolled = pltpu.roll(t, 1 if down else n - 1, 0)
    edge = 0 if down else n - 1
    return jnp.where(rows == edge, 0.0, rolled)


def _conv_specs(s):
    def z_spec(off):
        return pl.BlockSpec((s, CONV_CHUNK), lambda j, off=off: (0, off // CONV_CHUNK + j))
    chunk = pl.BlockSpec((s, CONV_CHUNK), lambda j: (0, j))
    w_spec = pl.BlockSpec((3, CONV_CHUNK), lambda j: (0, j))
    return z_spec(CU_OFF), z_spec(CB_OFF), z_spec(CC_OFF), chunk, w_spec


def _conv_fwd(z, conv_w):
    s = z.shape[0]
    cu_spec, cb_spec, cc_spec, chunk, w_spec = _conv_specs(s)

    def body(cu_ref, cb_ref, cc_ref, w_ref, o_ref):
        rows = lax.broadcasted_iota(jnp.int32, (s, 1), 0)
        t = cc_ref[...] * cu_ref[...]
        c3 = _shift_rows(t, rows, True) * w_ref[0:1, :] + t * w_ref[1:2, :] + _shift_rows(t, rows, False) * w_ref[2:3, :]
        o_ref[...] = (cb_ref[...] * c3).astype(BF16)

    return pl.pallas_call(
        body, name="conv_fwd", grid=(CONV_WIDTH // CONV_CHUNK,),
        in_specs=[cu_spec, cb_spec, cc_spec, w_spec],
        out_specs=chunk,
        out_shape=jax.ShapeDtypeStruct((s, CONV_WIDTH), BF16),
        compiler_params=_params(1),
    )(z, z, z, conv_w)


def _conv_bwd(z, conv_w, d_co):
    s = z.shape[0]
    cu_spec, cb_spec, cc_spec, chunk, w_spec = _conv_specs(s)

    def body(cu_ref, cb_ref, cc_ref, w_ref, d_ref, dcu_ref, dcb_ref, dcc_ref, dw_ref):
        rows = lax.broadcasted_iota(jnp.int32, (s, 1), 0)
        cu, cc = cu_ref[...], cc_ref[...]
        t = cc * cu
        t_dn, t_up = _shift_rows(t, rows, True), _shift_rows(t, rows, False)
        c3 = t_dn * w_ref[0:1, :] + t * w_ref[1:2, :] + t_up * w_ref[2:3, :]
        d = d_ref[...]
        dcb_ref[...] = (d * c3).astype(BF16)
        dc3 = d * cb_ref[...]
        dw_ref[0:1, :] = jnp.sum(dc3 * t_dn, axis=0, keepdims=True)
        dw_ref[1:2, :] = jnp.sum(dc3 * t, axis=0, keepdims=True)
        dw_ref[2:3, :] = jnp.sum(dc3 * t_up, axis=0, keepdims=True)
        dt = _shift_rows(dc3, rows, False) * w_ref[0:1, :] + dc3 * w_ref[1:2, :] + _shift_rows(dc3, rows, True) * w_ref[2:3, :]
        dcc_ref[...] = (dt * cu).astype(BF16)
        dcu_ref[...] = (dt * cc).astype(BF16)

    return pl.pallas_call(
        body, name="conv_bwd", grid=(CONV_WIDTH // CONV_CHUNK,),
        in_specs=[cu_spec, cb_spec, cc_spec, w_spec, chunk],
        out_specs=[chunk, chunk, chunk, w_spec],
        out_shape=[jax.ShapeDtypeStruct((s, CONV_WIDTH), BF16)] * 3 + [jax.ShapeDtypeStruct((3, CONV_WIDTH), F32)],
        compiler_params=_params(1),
    )(z, z, z, conv_w, d_co)


GATE_CHUNK = 512


def _gate_specs(s, d, tr):
    n_chunks = d // GATE_CHUNK
    za = pl.BlockSpec((tr, GATE_CHUNK), lambda j, i: (i, GL_OFF // GATE_CHUNK + j))
    zc = pl.BlockSpec((tr, GATE_CHUNK), lambda j, i: (i, GL_OFF // GATE_CHUNK + n_chunks + j))
    ba = pl.BlockSpec((1, GATE_CHUNK), lambda j, i: (0, j))
    bc = pl.BlockSpec((1, GATE_CHUNK), lambda j, i: (0, n_chunks + j))
    tile = pl.BlockSpec((tr, GATE_CHUNK), lambda j, i: (i, j))
    return za, zc, ba, bc, tile


def _gate_fwd(z, b_gate, ya, yc):
    s, d = ya.shape
    tr = _row_tile(s, 512)
    za, zc, ba, bc, tile = _gate_specs(s, d, tr)

    def body(za_ref, zc_ref, ba_ref, bc_ref, ya_ref, yc_ref, o_ref):
        ga = jax.nn.sigmoid(za_ref[...] + ba_ref[...])
        gc = jax.nn.sigmoid(zc_ref[...] + bc_ref[...])
        o_ref[...] = (ga * ya_ref[...] + gc * yc_ref[...]).astype(BF16)

    return pl.pallas_call(
        body, name="gate_fwd", grid=(d // GATE_CHUNK, s // tr),
        in_specs=[za, zc, ba, bc, tile, tile],
        out_specs=tile,
        out_shape=jax.ShapeDtypeStruct((s, d), BF16),
        compiler_params=_params(2),
    )(z, z, b_gate, b_gate, ya, yc)


def _gate_bwd(z, b_gate, ya, yc, dmix):
    s, d = ya.shape
    tr = _row_tile(s, 512)
    za, zc, ba, bc, tile = _gate_specs(s, d, tr)
    vec = pl.BlockSpec((1, GATE_CHUNK), lambda j, i: (0, j))

    def body(za_ref, zc_ref, ba_ref, bc_ref, ya_ref, yc_ref, dm_ref, dya_ref, dyc_ref, dla_ref, dlc_ref, dba_ref, dbc_ref):
        ga = jax.nn.sigmoid(za_ref[...] + ba_ref[...])
        gc = jax.nn.sigmoid(zc_ref[...] + bc_ref[...])
        dm = dm_ref[...]
        dya_ref[...] = (dm * ga).astype(BF16)
        dyc_ref[...] = (dm * gc).astype(BF16)
        dla = dm * ya_ref[...] * ga * (1.0 - ga)
        dlc = dm * yc_ref[...] * gc * (1.0 - gc)
        dla_ref[...] = dla.astype(BF16)
        dlc_ref[...] = dlc.astype(BF16)
        pa = jnp.sum(dla, axis=0, keepdims=True)
        pc = jnp.sum(dlc, axis=0, keepdims=True)

        @pl.when(pl.program_id(1) == 0)
        def _():
            dba_ref[...] = pa
            dbc_ref[...] = pc

        @pl.when(pl.program_id(1) > 0)
        def _():
            dba_ref[...] += pa
            dbc_ref[...] += pc

    big = jax.ShapeDtypeStruct((s, d), BF16)
    small = jax.ShapeDtypeStruct((1, d), F32)
    return pl.pallas_call(
        body, name="gate_bwd", grid=(d // GATE_CHUNK, s // tr),
        in_specs=[za, zc, ba, bc, tile, tile, tile],
        out_specs=[tile, tile, tile, tile, vec, vec],
        out_shape=[big, big, big, big, small, small],
        compiler_params=_params(2),
    )(z, z, b_gate, b_gate, ya, yc, dmix)


def _cross_probs(q_ref, kv_ref, hd):
    cols = slice(hd * HEAD_DIM, (hd + 1) * HEAD_DIM)
    qh = q_ref[:, cols]
    kh = kv_ref[:, cols]
    sc = lax.dot_general(qh, kh, (((1,), (1,)), ((), ())), preferred_element_type=F32) * ATTN_SCALE
    e = jnp.exp(sc - jnp.max(sc, axis=1, keepdims=True))
    return qh, kh, e * (1.0 / jnp.sum(e, axis=1, keepdims=True))


def _cross_fwd(qc, kvc):
    s = qc.shape[0]
    n_mem = kvc.shape[0]
    tq = _row_tile(s, 256)

    def body(q_ref, kv_ref, o_ref):
        for hd in range(MEM_HEADS):
            _, _, p = _cross_probs(q_ref, kv_ref, hd)
            vh = kv_ref[:, MEM_WIDTH + hd * HEAD_DIM:MEM_WIDTH + (hd + 1) * HEAD_DIM]
            o_ref[:, hd * HEAD_DIM:(hd + 1) * HEAD_DIM] = jnp.dot(p.astype(BF16), vh, preferred_element_type=F32).astype(BF16)

    return pl.pallas_call(
        body, name="cross_fwd", grid=(s // tq,),
        in_specs=[pl.BlockSpec((tq, MEM_WIDTH), lambda i: (i, 0)), pl.BlockSpec((n_mem, 2 * MEM_WIDTH), lambda i: (0, 0))],
        out_specs=pl.BlockSpec((tq, MEM_WIDTH), lambda i: (i, 0)),
        out_shape=jax.ShapeDtypeStruct((s, MEM_WIDTH), BF16),
        compiler_params=_params(1),
    )(qc, kvc)


def _cross_bwd(qc, kvc, d_out):
    s = qc.shape[0]
    n_mem = kvc.shape[0]
    tq = _row_tile(s, 256)

    def body(q_ref, kv_ref, do_ref, dq_ref, dkv_ref):
        @pl.when(pl.program_id(0) == 0)
        def _():
            dkv_ref[...] = jnp.zeros_like(dkv_ref)

        for hd in range(MEM_HEADS):
            cols = slice(hd * HEAD_DIM, (hd + 1) * HEAD_DIM)
            vcols = slice(MEM_WIDTH + hd * HEAD_DIM, MEM_WIDTH + (hd + 1) * HEAD_DIM)
            qh, kh, p = _cross_probs(q_ref, kv_ref, hd)
            doh = do_ref[:, cols]
            dp = lax.dot_general(doh, kv_ref[:, vcols], (((1,), (1,)), ((), ())), preferred_element_type=F32)
            ds = (p * (dp - jnp.sum(p * dp, axis=1, keepdims=True)) * ATTN_SCALE).astype(BF16)
            dq_ref[:, cols] = jnp.dot(ds, kh, preferred_element_type=F32).astype(BF16)
            dkv_ref[:, cols] += lax.dot_general(ds, qh, (((0,), (0,)), ((), ())), preferred_element_type=F32)
            dkv_ref[:, vcols] += lax.dot_general(p.astype(BF16), doh, (((0,), (0,)), ((), ())), preferred_element_type=F32)

    qspec = pl.BlockSpec((tq, MEM_WIDTH), lambda i: (i, 0))
    kvspec = pl.BlockSpec((n_mem, 2 * MEM_WIDTH), lambda i: (0, 0))
    return pl.pallas_call(
        body, name="cross_bwd", grid=(s // tq,),
        in_specs=[qspec, kvspec, qspec],
        out_specs=[qspec, kvspec],
        out_shape=[jax.ShapeDtypeStruct((s, MEM_WIDTH), BF16), jax.ShapeDtypeStruct((n_mem, 2 * MEM_WIDTH), F32)],
        compiler_params=_params(1),
    )(qc, kvc, d_out)


def _swiglu_fwd(up, gate):
    return up, (gate * jax.nn.sigmoid(gate)) * up


def _swiglu_bwd(d_act, gate, up):
    sg = jax.nn.sigmoid(gate)
    silu = gate * sg
    return d_act * up * (sg * (1.0 + gate * (1.0 - sg))), d_act * silu


def _local_step(xs, mems, target, small, conv_w, w4):
    s, d = xs.shape
    w_o = w4["w_o"].reshape(-1, w4["w_o"].shape[-1])
    w_cq = w4["w_cq"].reshape(-1, w4["w_cq"].shape[-1])
    w_ckv = w4["w_ckv"].reshape(-1, w4["w_ckv"].shape[-1])
    w_down = w4["w_down"].reshape(-1, w4["w_down"].shape[-1])
    c_in = w4["w_in"].shape[2]
    c_ff = w4["w_gate"].shape[2]
    c_d = w4["w_attn_out"].shape[2]
    cos_t, sin_t = _rope_tables(s)

    h = _rmsnorm(xs, small["g_mix"], "norm_mix")
    z = _matmul(h, w4["w_in"], mode="nn", tm=512, tn=c_in, tk=d, out_dtypes=[F32], name="in_proj", b_blocks=N_CHIPS)
    q_rot, k_rot, v_b = _rope_fwd(z, cos_t, sin_t)
    attn = _swa_fwd(q_rot, k_rot, v_b, small["sink"])
    co = _conv_fwd(z, conv_w)
    ya = _matmul(attn, w4["w_attn_out"], mode="nn", tm=1024, tn=c_d, tk=ATTN_WIDTH, out_dtypes=[F32], name="attn_out_proj",
                 b_blocks=N_CHIPS)
    yc = _matmul(co, w4["w_conv_out"], mode="nn", tm=1024, tn=c_d, tk=CONV_WIDTH, out_dtypes=[F32], name="conv_out_proj",
                 b_blocks=N_CHIPS)
    mix = _gate_fwd(z, small["b_gate"], ya, yc)
    x1 = _matmul(mix, w_o, mode="nn", tm=512, tn=1024, tk=d, out_dtypes=[F32], name="mix_out_proj", extras=[xs],
                 epilogue=_add_residual)
    hc = _rmsnorm(x1, small["g_cross"], "norm_cross")
    memn = _rmsnorm(mems, small["g_mem"], "norm_mem")
    qc = _matmul(hc, w_cq, mode="nn", tm=1024, tn=MEM_WIDTH, tk=d, out_dtypes=[BF16], name="cross_q_proj")
    kvc = _matmul(memn, w_ckv, mode="nn", tm=256, tn=2 * MEM_WIDTH, tk=d, out_dtypes=[BF16], name="cross_kv_proj")
    oc = _cross_fwd(qc, kvc)
    x2 = _matmul(oc, w4["w_co"], mode="nn", tm=1024, tn=c_d, tk=MEM_WIDTH, out_dtypes=[F32], name="cross_out_proj",
                 extras=[x1], epilogue=_add_residual, b_blocks=N_CHIPS)
    hf = _rmsnorm(x2, small["g_ffn"], "norm_ffn")
    gate = _matmul(hf, w4["w_gate"], mode="nn", tm=512, tn=c_ff, tk=d, out_dtypes=[F32], name="ffn_gate_proj", b_blocks=N_CHIPS)
    up, act = _matmul(hf, w4["w_up"], mode="nn", tm=512, tn=c_ff, tk=d, out_dtypes=[F32, BF16], name="ffn_up_proj",
                      extras=[gate], epilogue=_swiglu_fwd, b_blocks=N_CHIPS)
    x3 = _matmul(act, w_down, mode="nn", tm=512, tn=1024, tk=c_ff, out_dtypes=[F32], name="ffn_down_proj", extras=[x2],
                 epilogue=_add_residual)
    dx3, dx3b, sq, dg_final = _loss_head(x3, small["g_final"], target)

    grads = {}
    da, du = _matmul(dx3b, w_down, mode="nt", tm=512, tn=c_ff, tk=d, out_dtypes=[BF16, BF16], name="ffn_down_bwd",
                     extras=[gate, up], epilogue=_swiglu_bwd)
    grads["w_down"] = _matmul(act, dx3b, mode="tn", tm=c_ff, tn=1024, tk=s, out_dtypes=[BF16], name="ffn_down_wgrad")
    grads["w_gate"] = _matmul(hf, da, mode="tn", tm=512, tn=c_ff, tk=s, out_dtypes=[BF16], name="ffn_gate_wgrad",
                              out_blocks=N_CHIPS)
    grads["w_up"] = _matmul(hf, du, mode="tn", tm=512, tn=c_ff, tk=s, out_dtypes=[BF16], name="ffn_up_wgrad", out_blocks=N_CHIPS)
    dhf = _matmul(da, w4["w_gate"], mode="nt", tm=512, tn=1024, tk=c_ff, out_dtypes=[F32], name="ffn_gate_bwd", b_blocks=N_CHIPS)
    dhf = _matmul(du, w4["w_up"], mode="nt", tm=512, tn=1024, tk=c_ff, out_dtypes=[F32], name="ffn_up_bwd", extras=[dhf],
                  epilogue=_add_residual, b_blocks=N_CHIPS)
    dx2, dx2b, dg_ffn = _rmsnorm_bwd(dhf, x2, small["g_ffn"], dx3, "norm_ffn_bwd")

    d_oc = _matmul(dx2b, w4["w_co"], mode="nt", tm=1024, tn=MEM_WIDTH, tk=c_d, out_dtypes=[BF16], name="cross_out_bwd",
                   b_blocks=N_CHIPS)
    grads["w_co"] = _matmul(oc, dx2b, mode="tn", tm=MEM_WIDTH, tn=c_d, tk=s, out_dtypes=[BF16], name="cross_out_wgrad",
                            out_blocks=N_CHIPS)
    dqc, dkvc = _cross_bwd(qc, kvc, d_oc)
    grads["w_cq"] = _matmul(hc, dqc, mode="tn", tm=1024, tn=MEM_WIDTH, tk=s, out_dtypes=[BF16], name="cross_q_wgrad")
    dhc = _matmul(dqc, w_cq, mode="nt", tm=1024, tn=1024, tk=MEM_WIDTH, out_dtypes=[F32], name="cross_q_bwd")
    grads["w_ckv"] = _matmul(memn, dkvc, mode="tn", tm=1024, tn=2 * MEM_WIDTH, tk=mems.shape[0], out_dtypes=[BF16],
                             name="cross_kv_wgrad")
    dmemn = _matmul(dkvc, w_ckv, mode="nt", tm=256, tn=1024, tk=2 * MEM_WIDTH, out_dtypes=[F32], name="cross_kv_bwd")
    _, _, dg_mem = _rmsnorm_bwd(dmemn, mems, small["g_mem"], None, "norm_mem_bwd")
    dx1, dx1b, dg_cross = _rmsnorm_bwd(dhc, x1, small["g_cross"], dx2, "norm_cross_bwd")

    dmix = _matmul(dx1b, w_o, mode="nt", tm=512, tn=1024, tk=d, out_dtypes=[F32], name="mix_out_bwd")
    grads["w_o"] = _matmul(mix, dx1b, mode="tn", tm=1024, tn=1024, tk=s, out_dtypes=[BF16], name="mix_out_wgrad")
    dya, dyc, dgl_a, dgl_c, db_a, db_c = _gate_bwd(z, small["b_gate"], ya, yc, dmix)
    d_attn = _matmul(dya, w4["w_attn_out"], mode="nt", tm=1024, tn=ATTN_WIDTH, tk=c_d, out_dtypes=[BF16], name="attn_out_bwd",
                     b_blocks=N_CHIPS)
    grads["w_attn_out"] = _matmul(attn, dya, mode="tn", tm=ATTN_WIDTH, tn=c_d, tk=s, out_dtypes=[BF16], name="attn_out_wgrad",
                                  out_blocks=N_CHIPS)
    d_co = _matmul(dyc, w4["w_conv_out"], mode="nt", tm=1024, tn=CONV_WIDTH, tk=c_d, out_dtypes=[F32], name="conv_out_bwd",
                   b_blocks=N_CHIPS)
    grads["w_conv_out"] = _matmul(co, dyc, mode="tn", tm=CONV_WIDTH, tn=c_d, tk=s, out_dtypes=[BF16], name="conv_out_wgrad",
                                  out_blocks=N_CHIPS)
    dcu, dcb, dcc, d_conv_w = _conv_bwd(z, conv_w, d_co)
    dq_rot, dk_rot, dv, dsink = _swa_bwd(q_rot, k_rot, v_b, d_attn, small["sink"])
    dq, dk, dvb = _rope_bwd(dq_rot, dk_rot, dv, cos_t, sin_t)
    dz = jnp.concatenate([dq, dk, dvb, dcu, dcb, dcc, dgl_a, dgl_c], axis=1)
    grads["w_in"] = _matmul(h, dz, mode="tn", tm=512, tn=c_in, tk=s, out_dtypes=[BF16], name="in_proj_wgrad", out_blocks=N_CHIPS)
    dh = _matmul(dz, w4["w_in"], mode="nt", tm=512, tn=1024, tk=c_in, out_dtypes=[F32], name="in_proj_bwd", b_blocks=N_CHIPS)
    grad_x, _, dg_mix = _rmsnorm_bwd(dh, xs, small["g_mix"], dx1, "norm_mix_bwd")

    small_grads = {
        "g_mix": dg_mix, "sink": dsink[:, 0], "b_gate": jnp.concatenate([db_a, db_c], axis=1), "g_cross": dg_cross,
        "g_mem": dg_mem, "g_ffn": dg_ffn, "g_final": dg_final, "conv_w": d_conv_w,
    }
    return sq, grad_x, small_grads, grads


def _pair_sum(g4, ra, core, name):
    nb, rs, cs = g4.shape
    rh = rs // 2
    tr = _row_tile(rh, 256)
    per = rh // tr

    def body(c_ref, g_ref, r_ref, o_ref):
        o_ref[...] = (g_ref[...].astype(F32) + r_ref[...].astype(F32)).astype(BF16)

    plain = pl.BlockSpec((None, tr, cs), lambda j, i, c: (j, i, 0))
    return pl.pallas_call(
        body, name=name,
        grid_spec=pltpu.PrefetchScalarGridSpec(
            num_scalar_prefetch=1, grid=(nb, per),
            in_specs=[pl.BlockSpec((None, tr, cs), lambda j, i, c: (j, c[0] * per + i, 0)), plain],
            out_specs=plain),
        out_shape=jax.ShapeDtypeStruct((nb, rh, cs), BF16),
        compiler_params=_params(2),
    )(core, g4, ra)


def _quad_sum(rc, name):
    nb, rh, cs = rc.shape
    tr = _row_tile(rh, 256)

    def body(r_ref, o_ref):
        acc = r_ref[0].astype(F32)
        for j in range(1, nb):
            acc = acc + r_ref[j].astype(F32)
        o_ref[...] = acc

    return pl.pallas_call(
        body, name=name, grid=(rh // tr,),
        in_specs=[pl.BlockSpec((nb, tr, cs), lambda i: (0, i, 0))],
        out_specs=pl.BlockSpec((tr, cs), lambda i: (i, 0)),
        out_shape=jax.ShapeDtypeStruct((rh, cs), F32),
        compiler_params=_params(1),
    )(rc)


def _adamw(w, g, m, v, name):
    rows, cols = w.shape
    tr = _row_tile(rows, 256)

    def body(w_ref, g_ref, m_ref, v_ref, d_ref, nm_ref, nv_ref):
        gv = g_ref[...]
        nm = ADAM_B1 * m_ref[...] + (1.0 - ADAM_B1) * gv
        nv = ADAM_B2 * v_ref[...] + (1.0 - ADAM_B2) * (gv * gv)
        m_hat = nm / ADAM_C1
        v_hat = nv / ADAM_C2
        d_ref[...] = -ADAM_LR * (m_hat / (jnp.sqrt(v_hat) + ADAM_EPS) + ADAM_WD * w_ref[...])
        nm_ref[...] = nm
        nv_ref[...] = nv

    tile = pl.BlockSpec((tr, cols), lambda i: (i, 0))
    shape = jax.ShapeDtypeStruct((rows, cols), F32)
    return pl.pallas_call(
        body, name=name, grid=(rows // tr,),
        in_specs=[tile] * 4, out_specs=[tile] * 3, out_shape=[shape] * 3,
        compiler_params=_params(1),
    )(w, g, m, v)


def _mesh_pos():
    return lax.axis_index("x"), lax.axis_index("y"), lax.axis_index("c")


def _other_chips(x, y):
    return [(1 - x, y), (x, 1 - y), (1 - x, 1 - y)]


def _half_rows(ref, which):
    rh = ref.shape[-2] // 2
    return ref.at[pl.ds(which * rh, rh), :]


def _remote(src, dst, send_sems, recv_sems, sem, to):
    return pltpu.make_async_remote_copy(src_ref=src, dst_ref=dst, send_sem=send_sems.at[sem], recv_sem=recv_sems.at[sem],
                                        device_id=to, device_id_type=MESH)


def _all_gather(shards):
    n = len(shards)

    def body(*refs):
        src, dst = refs[:n], refs[n:2 * n]
        send_sems, recv_sems, local_sems = refs[2 * n:]
        x, y, c = _mesh_pos()
        me = 2 * x + y
        chips = _other_chips(x, y)
        local = [pltpu.make_async_copy(src[w], dst[w].at[me], local_sems.at[w]) for w in range(n)]
        for cp in local:
            cp.start()
        sends = []
        for w in range(n):
            for k, (px, py) in enumerate(chips):
                cp = _remote(_half_rows(src[w], c), _half_rows(dst[w].at[me], c), send_sems, recv_sems, 6 * w + k, (px, py, c))
                cp.start()
                sends.append(cp)
        for w in range(n):
            for k, (px, py) in enumerate(chips):
                landed = _half_rows(dst[w].at[2 * px + py], c)
                _remote(landed, landed, send_sems, recv_sems, 6 * w + k, (px, py, c)).wait_recv()
                cp = _remote(landed, landed, send_sems, recv_sems, 6 * w + 3 + k, (x, y, 1 - c))
                cp.start()
                sends.append(cp)
        for w in range(n):
            for k, (px, py) in enumerate(chips):
                passed = _half_rows(dst[w].at[2 * px + py], 1 - c)
                _remote(passed, passed, send_sems, recv_sems, 6 * w + 3 + k, (x, y, 1 - c)).wait_recv()
        for cp in sends:
            cp.wait_send()
        for cp in local:
            cp.wait()

    return pl.pallas_call(
        body, name="weights_all_gather",
        in_specs=[ANY] * n, out_specs=[ANY] * n,
        out_shape=[jax.ShapeDtypeStruct((N_CHIPS,) + s.shape, s.dtype) for s in shards],
        scratch_shapes=[pltpu.SemaphoreType.DMA((6 * n,)), pltpu.SemaphoreType.DMA((6 * n,)), pltpu.SemaphoreType.DMA((n,))],
    )(*shards)


def _sibling_exchange_halves(g4s):
    n = len(g4s)

    def body(*refs):
        src, dst = refs[:n], refs[n:2 * n]
        send_sems, recv_sems = refs[2 * n:]
        x, y, c = _mesh_pos()
        copies = []
        for w in range(n):
            rh = src[w].shape[1] // 2
            cp = _remote(src[w].at[:, pl.ds((1 - c) * rh, rh), :], dst[w], send_sems, recv_sems, w, (x, y, 1 - c))
            cp.start()
            copies.append(cp)
        for cp in copies:
            cp.wait_recv()
        for cp in copies:
            cp.wait_send()

    return pl.pallas_call(
        body, name="grads_sibling_exchange",
        in_specs=[ANY] * n, out_specs=[ANY] * n,
        out_shape=[jax.ShapeDtypeStruct((g.shape[0], g.shape[1] // 2, g.shape[2]), g.dtype) for g in g4s],
        scratch_shapes=[pltpu.SemaphoreType.DMA((n,)), pltpu.SemaphoreType.DMA((n,))],
    )(*g4s)


def _chip_exchange(parts):
    n = len(parts)

    def body(*refs):
        src, dst = refs[:n], refs[n:2 * n]
        send_sems, recv_sems, local_sems = refs[2 * n:]
        x, y, c = _mesh_pos()
        me = 2 * x + y
        chips = _other_chips(x, y)
        local = [pltpu.make_async_copy(src[w].at[me], dst[w].at[me], local_sems.at[w]) for w in range(n)]
        for cp in local:
            cp.start()
        sends = []
        for w in range(n):
            for k, (px, py) in enumerate(chips):
                cp = _remote(src[w].at[2 * px + py], dst[w].at[me], send_sems, recv_sems, 3 * w + k, (px, py, c))
                cp.start()
                sends.append(cp)
        for w in range(n):
            for k, (px, py) in enumerate(chips):
                landed = dst[w].at[2 * px + py]
                _remote(landed, landed, send_sems, recv_sems, 3 * w + k, (px, py, c)).wait_recv()
        for cp in sends:
            cp.wait_send()
        for cp in local:
            cp.wait()

    return pl.pallas_call(
        body, name="grads_chip_exchange",
        in_specs=[ANY] * n, out_specs=[ANY] * n,
        out_shape=[jax.ShapeDtypeStruct(p.shape, p.dtype) for p in parts],
        scratch_shapes=[pltpu.SemaphoreType.DMA((3 * n,)), pltpu.SemaphoreType.DMA((3 * n,)), pltpu.SemaphoreType.DMA((n,))],
    )(*parts)


def _sibling_join_halves(halves):
    n = len(halves)

    def body(*refs):
        src, dst = refs[:n], refs[n:2 * n]
        send_sems, recv_sems, local_sems = refs[2 * n:]
        x, y, c = _mesh_pos()
        local = [pltpu.make_async_copy(src[w], _half_rows(dst[w], c), local_sems.at[w]) for w in range(n)]
        for cp in local:
            cp.start()
        sends = []
        for w in range(n):
            cp = _remote(src[w], _half_rows(dst[w], c), send_sems, recv_sems, w, (x, y, 1 - c))
            cp.start()
            sends.append(cp)
        for w in range(n):
            other = _half_rows(dst[w], 1 - c)
            _remote(other, other, send_sems, recv_sems, w, (x, y, 1 - c)).wait_recv()
        for cp in sends:
            cp.wait_send()
        for cp in local:
            cp.wait()

    return pl.pallas_call(
        body, name="grads_sibling_join",
        in_specs=[ANY] * n, out_specs=[ANY] * n,
        out_shape=[jax.ShapeDtypeStruct((2 * h.shape[0], h.shape[1]), h.dtype) for h in halves],
        scratch_shapes=[pltpu.SemaphoreType.DMA((n,)), pltpu.SemaphoreType.DMA((n,)), pltpu.SemaphoreType.DMA((n,))],
    )(*halves)


N_DEV = 8


def _all_reduce_small(v):
    def body(v_ref, o_ref, slots, send_sems, recv_sems):
        x, y, c = _mesh_pos()
        me = 4 * x + 2 * y + c
        slots[me] = v_ref[...]
        peers = []
        for r in range(1, N_DEV):
            fx, fy, fc = (r >> 2) & 1, (r >> 1) & 1, r & 1
            peers.append((x + fx - 2 * x * fx, y + fy - 2 * y * fy, c + fc - 2 * c * fc))
        sends = []
        for r, peer in enumerate(peers):
            cp = _remote(v_ref, slots.at[me], send_sems, recv_sems, r, peer)
            cp.start()
            sends.append(cp)
        for r, (px, py, pc) in enumerate(peers):
            landed = slots.at[4 * px + 2 * py + pc]
            _remote(landed, landed, send_sems, recv_sems, r, (px, py, pc)).wait_recv()
        for cp in sends:
            cp.wait_send()
        acc = slots[0]
        for i in range(1, N_DEV):
            acc = acc + slots[i]
        o_ref[...] = acc

    vm = pl.BlockSpec(memory_space=pltpu.VMEM)
    return pl.pallas_call(
        body, name="small_grads_all_reduce",
        in_specs=[vm], out_specs=vm,
        out_shape=jax.ShapeDtypeStruct(v.shape, v.dtype),
        scratch_shapes=[pltpu.VMEM((N_DEV,) + v.shape, v.dtype), pltpu.SemaphoreType.DMA((N_DEV - 1,)),
                        pltpu.SemaphoreType.DMA((N_DEV - 1,))],
    )(v)


MATRICES = ("w_in", "w_attn_out", "w_conv_out", "w_o", "w_cq", "w_ckv", "w_co", "w_gate", "w_up", "w_down")
VECTORS = ("g_mix", "b_gate", "g_cross", "g_mem", "g_ffn", "g_final", "conv_w", "sink")
WEIGHT_ORDER = ("g_mix", "w_in", "sink", "conv_w", "b_gate", "w_attn_out", "w_conv_out", "w_o", "g_cross", "g_mem", "w_cq",
                "w_ckv", "w_co", "g_ffn", "w_gate", "w_up", "w_down", "g_final")
CONV_PAD_ROWS = 16
SMALL_ROWS = 8


def _pack(pieces):
    flat = jnp.concatenate([p.reshape(-1) for p in pieces])
    lane_group = SMALL_ROWS * 128
    total = -(-flat.shape[0] // lane_group) * lane_group
    flat = jnp.pad(flat, (0, total - flat.shape[0]))
    return flat.reshape(SMALL_ROWS, total // SMALL_ROWS), [p.size for p in pieces]


def _unpack(packed, pieces):
    flat = packed.reshape(-1)
    out, off = [], 0
    for p in pieces:
        out.append(flat[off:off + p.size].reshape(p.shape))
        off += p.size
    return out


def kernel(x, mem, g_mix, w_in, sink, conv_w, b_gate, w_attn_out, w_conv_out, w_o, g_cross, g_mem, w_cq, w_ckv, w_co, g_ffn, w_gate, w_up, w_down, g_final, loss_target, m_g_mix, m_w_in, m_sink, m_conv_w, m_b_gate, m_w_attn_out, m_w_conv_out, m_w_o, m_g_cross, m_g_mem, m_w_cq, m_w_ckv, m_w_co, m_g_ffn, m_w_gate, m_w_up, m_w_down, m_g_final, v_g_mix, v_w_in, v_sink, v_conv_w, v_b_gate, v_w_attn_out, v_w_conv_out, v_w_o, v_g_cross, v_g_mem, v_w_cq, v_w_ckv, v_w_co, v_g_ffn, v_w_gate, v_w_up, v_w_down, v_g_final):
    given = dict(g_mix=g_mix, w_in=w_in, sink=sink, conv_w=conv_w, b_gate=b_gate, w_attn_out=w_attn_out, w_conv_out=w_conv_out,
                 w_o=w_o, g_cross=g_cross, g_mem=g_mem, w_cq=w_cq, w_ckv=w_ckv, w_co=w_co, g_ffn=g_ffn, w_gate=w_gate, w_up=w_up,
                 w_down=w_down, g_final=g_final)
    mom_m = dict(g_mix=m_g_mix, w_in=m_w_in, sink=m_sink, conv_w=m_conv_w, b_gate=m_b_gate, w_attn_out=m_w_attn_out,
                 w_conv_out=m_w_conv_out, w_o=m_w_o, g_cross=m_g_cross, g_mem=m_g_mem, w_cq=m_w_cq, w_ckv=m_w_ckv, w_co=m_w_co,
                 g_ffn=m_g_ffn, w_gate=m_w_gate, w_up=m_w_up, w_down=m_w_down, g_final=m_g_final)
    mom_v = dict(g_mix=v_g_mix, w_in=v_w_in, sink=v_sink, conv_w=v_conv_w, b_gate=v_b_gate, w_attn_out=v_w_attn_out,
                 w_conv_out=v_w_conv_out, w_o=v_w_o, g_cross=v_g_cross, g_mem=v_g_mem, w_cq=v_w_cq, w_ckv=v_w_ckv, w_co=v_w_co,
                 g_ffn=v_g_ffn, w_gate=v_w_gate, w_up=v_w_up, w_down=v_w_down, g_final=v_g_final)
    xs, mems, target = x[0], mem[0], loss_target[0]
    d_model = xs.shape[1]
    chip = 2 * lax.axis_index("x") + lax.axis_index("y")
    core = jnp.reshape(lax.axis_index("c"), (1,)).astype(jnp.int32)

    shards = {n: given[n][0] for n in MATRICES}
    conv_cols = conv_w.shape[2]
    conv_pad = jnp.pad(conv_w[0], ((0, CONV_PAD_ROWS - conv_w.shape[1]), (0, 0)))
    gathered = _all_gather([shards[n].astype(BF16) for n in MATRICES] + [conv_pad])
    w4 = dict(zip(MATRICES, gathered[:-1]))
    conv_full = gathered[-1][:, :conv_w.shape[1], :].transpose(1, 0, 2).reshape(conv_w.shape[1], N_CHIPS * conv_cols)
    small = {n: given[n] for n in ("g_mix", "b_gate", "g_cross", "g_mem", "g_ffn")}
    small["g_final"] = g_final[None]
    small["sink"] = sink[0]

    sq, grad_x, small_grads, grads = _local_step(xs, mems, target, small, conv_full, w4)
    loss = lax.psum(0.5 * sq[0, 0] / d_model, ("x", "y", "c"))

    g4 = [grads[n].reshape((N_CHIPS, -1, grads[n].shape[-1])) if grads[n].ndim == 2 else grads[n] for n in MATRICES]
    from_sibling = _sibling_exchange_halves(g4)
    chip_parts = [_pair_sum(g, r, core, "pair_sum_" + n) for g, r, n in zip(g4, from_sibling, MATRICES)]
    from_chips = _chip_exchange(chip_parts)
    halves = [_quad_sum(r, "quad_sum_" + n) for r, n in zip(from_chips, MATRICES)]
    reduced = dict(zip(MATRICES, _sibling_join_halves(halves)))

    packed, _ = _pack([small_grads[n] for n in VECTORS])
    summed = _unpack(_all_reduce_small(packed), [small_grads[n] for n in VECTORS])
    small_sum = dict(zip(VECTORS, summed))
    small_sum["conv_w"] = lax.dynamic_slice_in_dim(small_sum["conv_w"], chip * conv_cols, conv_cols, axis=1)

    grad_out, delta, new_m, new_v = {}, {}, {}, {}
    for n in MATRICES:
        g = reduced[n]
        grad_out[n] = g[None]
        d, nm, nv = _adamw(shards[n], g, mom_m[n][0], mom_v[n][0], "adamw_" + n)
        delta[n], new_m[n], new_v[n] = d[None], nm[None], nv[None]
    like = [given[n] for n in VECTORS]
    pw, _ = _pack(like)
    pg, _ = _pack([small_sum[n] for n in VECTORS])
    pm, _ = _pack([mom_m[n] for n in VECTORS])
    pv, _ = _pack([mom_v[n] for n in VECTORS])
    pd, pnm, pnv = _adamw(pw, pg, pm, pv, "adamw_small")
    for n, g, d, nm, nv in zip(VECTORS, [small_sum[n] for n in VECTORS], _unpack(pd, like), _unpack(pnm, like), _unpack(pnv, like)):
        grad_out[n] = g.reshape(given[n].shape)
        delta[n], new_m[n], new_v[n] = d, nm, nv

    return (loss, grad_x[None], *[grad_out[n] for n in WEIGHT_ORDER], *[delta[n] for n in WEIGHT_ORDER],
            *[new_m[n] for n in WEIGHT_ORDER], *[new_v[n] for n in WEIGHT_ORDER])
```

```python
import functools

import jax
import jax.numpy as jnp
from jax import lax
from jax.experimental import pallas as pl
from jax.experimental.pallas import tpu as pltpu

F32 = jnp.float32
BF16 = jnp.bfloat16
MESH = pl.DeviceIdType.MESH
ANY = pl.BlockSpec(memory_space=pl.ANY)

VMEM_LIMIT_BYTES = 56 * 1024 * 1024

N_CHIPS = 4
HEAD_DIM = 128
N_Q_HEADS = 8
N_KV_HEADS = 2
Q_GROUP = N_Q_HEADS // N_KV_HEADS
ATTN_WIDTH = N_Q_HEADS * HEAD_DIM
KV_WIDTH = N_KV_HEADS * HEAD_DIM
WINDOW = 128
BLOCK = 128
BAND = 3 * BLOCK
ROPE_THETA = 10000.0
CONV_WIDTH = 1024
MEM_HEADS = 4
MEM_WIDTH = MEM_HEADS * HEAD_DIM
RMS_EPS = 1e-6
NEG_INF = -1e30
ATTN_SCALE = HEAD_DIM ** -0.5

Q_OFF, K_OFF, V_OFF, CU_OFF, CB_OFF, CC_OFF, GL_OFF = 0, 1024, 1280, 1536, 2560, 3584, 4608

ADAM_LR = 0.001
ADAM_B1 = 0.9
ADAM_B2 = 0.999
ADAM_EPS = 1e-08
ADAM_WD = 0.01
ADAM_STEP = 10
ADAM_C1 = 1.0 - ADAM_B1 ** ADAM_STEP
ADAM_C2 = 1.0 - ADAM_B2 ** ADAM_STEP


def _params(n_grid_axes):
    return pltpu.CompilerParams(dimension_semantics=("arbitrary",) * n_grid_axes, vmem_limit_bytes=VMEM_LIMIT_BYTES)


def _row_tile(rows, want):
    t = min(want, rows)
    while rows % t:
        t //= 2
    return t


def _matmul(a, b, *, mode, tm, tn, tk, out_dtypes, name, extras=(), epilogue=None, b_blocks=1, out_blocks=1):
    if mode == "tn":
        kdim, m = a.shape
    else:
        m, kdim = a.shape
    if b_blocks > 1:
        nb, brows, bcols = b.shape
        assert nb == b_blocks
        if mode == "nn":
            n = bcols * nb
            assert brows == kdim
        else:
            assert mode == "nt" and bcols * nb == kdim
            n = brows
    else:
        n = b.shape[0] if mode == "nt" else b.shape[1]
    tm, tn, tk = min(tm, m), min(tn, n), min(tk, kdim)
    assert m % tm == 0 and n % tn == 0 and kdim % tk == 0, (name, m, n, kdim, tm, tn, tk)
    nk = kdim // tk
    n_extra, n_out = len(extras), len(out_dtypes)

    if mode == "tn":
        a_spec = pl.BlockSpec((tk, tm), lambda j, i, k: (k, i))
        dims = (((0,), (0,)), ((), ()))
    else:
        a_spec = pl.BlockSpec((tm, tk), lambda j, i, k: (i, k))
        dims = (((1,), (0,)), ((), ())) if mode == "nn" else (((1,), (1,)), ((), ()))

    if b_blocks > 1 and mode == "nn":
        per = b.shape[2] // tn
        assert b.shape[2] % tn == 0
        b_spec = pl.BlockSpec((None, tk, tn), lambda j, i, k: (j // per, k, j % per))
    elif b_blocks > 1:
        per = b.shape[2] // tk
        assert b.shape[2] % tk == 0
        b_spec = pl.BlockSpec((None, tn, tk), lambda j, i, k: (k // per, j, k % per))
    elif mode == "nt":
        b_spec = pl.BlockSpec((tn, tk), lambda j, i, k: (j, k))
    else:
        b_spec = pl.BlockSpec((tk, tn), lambda j, i, k: (k, j))

    tile_spec = pl.BlockSpec((tm, tn), lambda j, i, k: (i, j))
    if out_blocks > 1:
        ncols = n // out_blocks
        assert ncols % tn == 0
        oper = ncols // tn
        out_spec = pl.BlockSpec((None, tm, tn), lambda j, i, k: (j // oper, i, j % oper))
        out_shape = [jax.ShapeDtypeStruct((out_blocks, m, ncols), dt) for dt in out_dtypes]
    else:
        out_spec = tile_spec
        out_shape = [jax.ShapeDtypeStruct((m, n), dt) for dt in out_dtypes]

    def body(a_ref, b_ref, *rest):
        extra_refs = rest[:n_extra]
        out_refs = rest[n_extra:n_extra + n_out]

        def finish(acc):
            if epilogue is None:
                tiles = (acc,)
            else:
                tiles = epilogue(acc, *[r[...] for r in extra_refs])
            for o_ref, t in zip(out_refs, tiles, strict=True):
                o_ref[...] = t.astype(o_ref.dtype)

        part = lax.dot_general(a_ref[...].astype(BF16), b_ref[...].astype(BF16), dims, preferred_element_type=F32)
        if nk == 1:
            finish(part)
        else:
            acc_ref = rest[-1]
            k = pl.program_id(2)

            @pl.when(k == 0)
            def _():
                acc_ref[...] = part

            @pl.when(k > 0)
            def _():
                acc_ref[...] += part

            @pl.when(k == nk - 1)
            def _():
                finish(acc_ref[...])

    outs = pl.pallas_call(
        body,
        name=name,
        grid=(n // tn, m // tm, nk),
        in_specs=[a_spec, b_spec] + [tile_spec] * n_extra,
        out_specs=[out_spec] * n_out,
        out_shape=out_shape,
        scratch_shapes=[pltpu.VMEM((tm, tn), F32)] if nk > 1 else [],
        compiler_params=_params(3),
    )(a, b, *extras)
    return outs[0] if n_out == 1 else outs


def _add_residual(acc, res):
    return (acc + res,)


def _rstd(x):
    return lax.rsqrt(jnp.mean(x * x, axis=-1, keepdims=True) + RMS_EPS)


def _rmsnorm(x, g, name):
    s, d = x.shape
    tr = _row_tile(s, 256)

    def body(x_ref, g_ref, o_ref):
        xv = x_ref[...]
        o_ref[...] = (xv * _rstd(xv) * g_ref[...]).astype(BF16)

    return pl.pallas_call(
        body, name=name, grid=(s // tr,),
        in_specs=[pl.BlockSpec((tr, d), lambda i: (i, 0)), pl.BlockSpec((1, d), lambda i: (0, 0))],
        out_specs=pl.BlockSpec((tr, d), lambda i: (i, 0)),
        out_shape=jax.ShapeDtypeStruct((s, d), BF16),
        compiler_params=_params(1),
    )(x, g)


def _rmsnorm_bwd(dh, x, g, dres, name):
    s, d = x.shape
    tr = _row_tile(s, 256)
    has_res = dres is not None

    def body(*refs):
        if has_res:
            dh_ref, x_ref, g_ref, res_ref, dx_ref, dxb_ref, dg_ref = refs
        else:
            dh_ref, x_ref, g_ref, dx_ref, dxb_ref, dg_ref = refs
        xv = x_ref[...]
        dhv = dh_ref[...].astype(F32)
        r = _rstd(xv)
        xn = xv * r
        dhg = dhv * g_ref[...]
        dx = r * (dhg - xn * jnp.mean(dhg * xn, axis=-1, keepdims=True))
        if has_res:
            dx = dx + res_ref[...]
        dx_ref[...] = dx
        dxb_ref[...] = dx.astype(BF16)
        part = jnp.sum(dhv * xn, axis=0, keepdims=True)

        @pl.when(pl.program_id(0) == 0)
        def _():
            dg_ref[...] = part

        @pl.when(pl.program_id(0) > 0)
        def _():
            dg_ref[...] += part

    row = pl.BlockSpec((tr, d), lambda i: (i, 0))
    vec = pl.BlockSpec((1, d), lambda i: (0, 0))
    return pl.pallas_call(
        body, name=name, grid=(s // tr,),
        in_specs=[row, row, vec] + ([row] if has_res else []),
        out_specs=[row, row, vec],
        out_shape=[jax.ShapeDtypeStruct((s, d), F32), jax.ShapeDtypeStruct((s, d), BF16), jax.ShapeDtypeStruct((1, d), F32)],
        compiler_params=_params(1),
    )(*([dh, x, g] + ([dres] if has_res else [])))


def _loss_head(x3, g, target):
    s, d = x3.shape
    tr = _row_tile(s, 256)

    def body(x_ref, g_ref, t_ref, dx_ref, dxb_ref, sq_ref, dg_ref):
        xv = x_ref[...]
        gv = g_ref[...]
        r = _rstd(xv)
        xn = xv * r
        err = xn * gv - t_ref[...]
        dy = err * (1.0 / d)
        dyg = dy * gv
        dx = r * (dyg - xn * jnp.mean(dyg * xn, axis=-1, keepdims=True))
        dx_ref[...] = dx
        dxb_ref[...] = dx.astype(BF16)
        sq = jnp.sum(jnp.sum(err * err, axis=1, keepdims=True), axis=0, keepdims=True)
        sq = jnp.broadcast_to(sq, (1, 128))
        part = jnp.sum(dy * xn, axis=0, keepdims=True)

        @pl.when(pl.program_id(0) == 0)
        def _():
            sq_ref[...] = sq
            dg_ref[...] = part

        @pl.when(pl.program_id(0) > 0)
        def _():
            sq_ref[...] += sq
            dg_ref[...] += part

    row = pl.BlockSpec((tr, d), lambda i: (i, 0))
    vec = pl.BlockSpec((1, d), lambda i: (0, 0))
    return pl.pallas_call(
        body, name="loss_head", grid=(s // tr,),
        in_specs=[row, vec, row],
        out_specs=[row, row, pl.BlockSpec((1, 128), lambda i: (0, 0)), vec],
        out_shape=[jax.ShapeDtypeStruct((s, d), F32), jax.ShapeDtypeStruct((s, d), BF16),
                   jax.ShapeDtypeStruct((1, 128), F32), jax.ShapeDtypeStruct((1, d), F32)],
        compiler_params=_params(1),
    )(x3, g, target)


def _rope_tables(s):
    inv = 1.0 / (ROPE_THETA ** (jnp.arange(0, HEAD_DIM, 2, dtype=F32) / HEAD_DIM))
    ang = jnp.arange(s, dtype=F32)[:, None] * inv[None, :]
    cos, sin = jnp.cos(ang), jnp.sin(ang)
    return jnp.concatenate([cos, cos], axis=1), jnp.concatenate([-sin, sin], axis=1)


def _swap_halves(t):
    return pltpu.roll(t, HEAD_DIM // 2, 1)


def _rope_fwd(z, cos_t, sin_t):
    s = z.shape[0]
    tr = _row_tile(s, 256)

    def body(zq_ref, zk_ref, zv_ref, c_ref, s_ref, q_ref, k_ref, v_ref):
        c, sn = c_ref[...], s_ref[...]
        for hd in range(N_Q_HEADS):
            cols = slice(hd * HEAD_DIM, (hd + 1) * HEAD_DIM)
            t = zq_ref[:, cols]
            q_ref[:, cols] = (t * c + _swap_halves(t) * sn).astype(BF16)
        for hd in range(N_KV_HEADS):
            cols = slice(hd * HEAD_DIM, (hd + 1) * HEAD_DIM)
            t = zk_ref[:, cols]
            k_ref[:, cols] = (t * c + _swap_halves(t) * sn).astype(BF16)
        v_ref[...] = zv_ref[...].astype(BF16)

    tab = pl.BlockSpec((tr, HEAD_DIM), lambda i: (i, 0))
    return pl.pallas_call(
        body, name="rope_fwd", grid=(s // tr,),
        in_specs=[pl.BlockSpec((tr, ATTN_WIDTH), lambda i: (i, Q_OFF // ATTN_WIDTH)),
                  pl.BlockSpec((tr, KV_WIDTH), lambda i: (i, K_OFF // KV_WIDTH)),
                  pl.BlockSpec((tr, KV_WIDTH), lambda i: (i, V_OFF // KV_WIDTH)), tab, tab],
        out_specs=[pl.BlockSpec((tr, ATTN_WIDTH), lambda i: (i, 0)), pl.BlockSpec((tr, KV_WIDTH), lambda i: (i, 0)),
                   pl.BlockSpec((tr, KV_WIDTH), lambda i: (i, 0))],
        out_shape=[jax.ShapeDtypeStruct((s, ATTN_WIDTH), BF16), jax.ShapeDtypeStruct((s, KV_WIDTH), BF16),
                   jax.ShapeDtypeStruct((s, KV_WIDTH), BF16)],
        compiler_params=_params(1),
    )(z, z, z, cos_t, sin_t)


def _rope_bwd(dq_rot, dk_rot, dv, cos_t, sin_t):
    s = dq_rot.shape[0]
    tr = _row_tile(s, 256)

    def body(dq_ref, dk_ref, dv_ref, c_ref, s_ref, oq_ref, ok_ref, ov_ref):
        c, sn = c_ref[...], s_ref[...]
        for hd in range(N_Q_HEADS):
            cols = slice(hd * HEAD_DIM, (hd + 1) * HEAD_DIM)
            t = dq_ref[:, cols]
            oq_ref[:, cols] = (t * c + _swap_halves(t * sn)).astype(BF16)
        for hd in range(N_KV_HEADS):
            cols = slice(hd * HEAD_DIM, (hd + 1) * HEAD_DIM)
            t = dk_ref[:, cols]
            ok_ref[:, cols] = (t * c + _swap_halves(t * sn)).astype(BF16)
        ov_ref[...] = dv_ref[...].astype(BF16)

    tab = pl.BlockSpec((tr, HEAD_DIM), lambda i: (i, 0))
    wide = pl.BlockSpec((tr, ATTN_WIDTH), lambda i: (i, 0))
    narrow = pl.BlockSpec((tr, KV_WIDTH), lambda i: (i, 0))
    return pl.pallas_call(
        body, name="rope_bwd", grid=(s // tr,),
        in_specs=[wide, narrow, narrow, tab, tab],
        out_specs=[wide, narrow, narrow],
        out_shape=[jax.ShapeDtypeStruct((s, ATTN_WIDTH), BF16), jax.ShapeDtypeStruct((s, KV_WIDTH), BF16),
                   jax.ShapeDtypeStruct((s, KV_WIDTH), BF16)],
        compiler_params=_params(1),
    )(dq_rot, dk_rot, dv, cos_t, sin_t)


def _swa_band(i, s):
    return pl.multiple_of(jnp.clip((i - 1) * BLOCK, 0, s - BAND), BLOCK)


def _swa_probs(q_ref, k_ref, sink_ref, kv, start, valid):
    cols = slice(kv * HEAD_DIM, (kv + 1) * HEAD_DIM)
    kb = k_ref[pl.ds(start, BAND), cols]
    heads = [kv * Q_GROUP + g for g in range(Q_GROUP)]
    qg = jnp.concatenate([q_ref[:, hd * HEAD_DIM:(hd + 1) * HEAD_DIM] for hd in heads], axis=0)
    sc = lax.dot_general(qg, kb, (((1,), (1,)), ((), ())), preferred_element_type=F32) * ATTN_SCALE
    sc = jnp.where(valid, sc, NEG_INF)
    sk = jnp.concatenate([jnp.full((BLOCK, 1), sink_ref[hd], F32) for hd in heads], axis=0)
    mx = jnp.maximum(jnp.max(sc, axis=1, keepdims=True), sk)
    e = jnp.exp(sc - mx)
    es = jnp.exp(sk - mx)
    inv = 1.0 / (jnp.sum(e, axis=1, keepdims=True) + es)
    return qg, kb, e * inv, es * inv


def _swa_valid(i, start):
    q_pos = i * BLOCK + lax.broadcasted_iota(jnp.int32, (BLOCK, 1), 0)
    q_pos = jnp.concatenate([q_pos] * Q_GROUP, axis=0)
    k_pos = start + lax.broadcasted_iota(jnp.int32, (1, BAND), 1)
    return jnp.abs(k_pos - q_pos) <= WINDOW


def _swa_fwd(q, k, v, sink):
    s = q.shape[0]
    assert s % BLOCK == 0 and s >= BAND

    def body(sink_ref, q_ref, k_ref, v_ref, o_ref):
        i = pl.program_id(0)
        start = _swa_band(i, s)
        valid = _swa_valid(i, start)
        for kv in range(N_KV_HEADS):
            _, _, p, _ = _swa_probs(q_ref, k_ref, sink_ref, kv, start, valid)
            vb = v_ref[pl.ds(start, BAND), kv * HEAD_DIM:(kv + 1) * HEAD_DIM]
            o = jnp.dot(p.astype(BF16), vb, preferred_element_type=F32)
            for g in range(Q_GROUP):
                hd = kv * Q_GROUP + g
                o_ref[:, hd * HEAD_DIM:(hd + 1) * HEAD_DIM] = o[g * BLOCK:(g + 1) * BLOCK].astype(BF16)

    whole = pl.BlockSpec((s, KV_WIDTH), lambda i: (0, 0))
    blk = pl.BlockSpec((BLOCK, ATTN_WIDTH), lambda i: (i, 0))
    return pl.pallas_call(
        body, name="swa_fwd", grid=(s // BLOCK,),
        in_specs=[pl.BlockSpec(memory_space=pltpu.SMEM), blk, whole, whole],
        out_specs=blk,
        out_shape=jax.ShapeDtypeStruct((s, ATTN_WIDTH), BF16),
        compiler_params=_params(1),
    )(sink, q, k, v)


def _swa_bwd(q, k, v, d_out, sink):
    s = q.shape[0]

    def body(sink_ref, q_ref, k_ref, v_ref, do_ref, dq_ref, dk_ref, dv_ref, dsink_ref):
        i = pl.program_id(0)

        @pl.when(i == 0)
        def _():
            dk_ref[...] = jnp.zeros_like(dk_ref)
            dv_ref[...] = jnp.zeros_like(dv_ref)
            dsink_ref[...] = jnp.zeros_like(dsink_ref)

        start = _swa_band(i, s)
        valid = _swa_valid(i, start)
        for kv in range(N_KV_HEADS):
            cols = slice(kv * HEAD_DIM, (kv + 1) * HEAD_DIM)
            qg, kb, p, p_sink = _swa_probs(q_ref, k_ref, sink_ref, kv, start, valid)
            vb = v_ref[pl.ds(start, BAND), cols]
            heads = [kv * Q_GROUP + g for g in range(Q_GROUP)]
            dog = jnp.concatenate([do_ref[:, hd * HEAD_DIM:(hd + 1) * HEAD_DIM] for hd in heads], axis=0)
            dp = lax.dot_general(dog, vb, (((1,), (1,)), ((), ())), preferred_element_type=F32)
            delta = jnp.sum(p * dp, axis=1, keepdims=True)
            ds = (p * (dp - delta) * ATTN_SCALE).astype(BF16)
            dqg = jnp.dot(ds, kb, preferred_element_type=F32)
            dk_ref[pl.ds(start, BAND), cols] += lax.dot_general(ds, qg, (((0,), (0,)), ((), ())), preferred_element_type=F32)
            dv_ref[pl.ds(start, BAND), cols] += lax.dot_general(p.astype(BF16), dog, (((0,), (0,)), ((), ())),
                                                                 preferred_element_type=F32)
            dsk = p_sink * delta
            for g, hd in enumerate(heads):
                dq_ref[:, hd * HEAD_DIM:(hd + 1) * HEAD_DIM] = dqg[g * BLOCK:(g + 1) * BLOCK]
                tot = jnp.sum(dsk[g * BLOCK:(g + 1) * BLOCK], axis=0, keepdims=True)
                dsink_ref[hd:hd + 1, :] -= jnp.broadcast_to(tot, (1, 128))

    whole = pl.BlockSpec((s, KV_WIDTH), lambda i: (0, 0))
    blk = pl.BlockSpec((BLOCK, ATTN_WIDTH), lambda i: (i, 0))
    return pl.pallas_call(
        body, name="swa_bwd", grid=(s // BLOCK,),
        in_specs=[pl.BlockSpec(memory_space=pltpu.SMEM), blk, whole, whole, blk],
        out_specs=[blk, whole, whole, pl.BlockSpec((N_Q_HEADS, 128), lambda i: (0, 0))],
        out_shape=[jax.ShapeDtypeStruct((s, ATTN_WIDTH), F32), jax.ShapeDtypeStruct((s, KV_WIDTH), F32),
                   jax.ShapeDtypeStruct((s, KV_WIDTH), F32), jax.ShapeDtypeStruct((N_Q_HEADS, 128), F32)],
        compiler_params=_params(1),
    )(sink, q, k, v, d_out)


CONV_CHUNK = 256


def _shift_rows(t, rows, down):
    n = t.shape[0]
    rolled = pltpu.roll(t, 1 if down else n - 1, 0)
    edge = 0 if down else n - 1
    return jnp.where(rows == edge, 0.0, rolled)


def _conv_specs(s):
    def z_spec(off):
        return pl.BlockSpec((s, CONV_CHUNK), lambda j, off=off: (0, off // CONV_CHUNK + j))
    chunk = pl.BlockSpec((s, CONV_CHUNK), lambda j: (0, j))
    w_spec = pl.BlockSpec((3, CONV_CHUNK), lambda j: (0, j))
    return z_spec(CU_OFF), z_spec(CB_OFF), z_spec(CC_OFF), chunk, w_spec


def _conv_fwd(z, conv_w):
    s = z.shape[0]
    cu_spec, cb_spec, cc_spec, chunk, w_spec = _conv_specs(s)

    def body(cu_ref, cb_ref, cc_ref, w_ref, o_ref):
        rows = lax.broadcasted_iota(jnp.int32, (s, 1), 0)
        t = cc_ref[...] * cu_ref[...]
        c3 = _shift_rows(t, rows, True) * w_ref[0:1, :] + t * w_ref[1:2, :] + _shift_rows(t, rows, False) * w_ref[2:3, :]
        o_ref[...] = (cb_ref[...] * c3).astype(BF16)

    return pl.pallas_call(
        body, name="conv_fwd", grid=(CONV_WIDTH // CONV_CHUNK,),
        in_specs=[cu_spec, cb_spec, cc_spec, w_spec],
        out_specs=chunk,
        out_shape=jax.ShapeDtypeStruct((s, CONV_WIDTH), BF16),
        compiler_params=_params(1),
    )(z, z, z, conv_w)


def _conv_bwd(z, conv_w, d_co):
    s = z.shape[0]
    cu_spec, cb_spec, cc_spec, chunk, w_spec = _conv_specs(s)

    def body(cu_ref, cb_ref, cc_ref, w_ref, d_ref, dcu_ref, dcb_ref, dcc_ref, dw_ref):
        rows = lax.broadcasted_iota(jnp.int32, (s, 1), 0)
        cu, cc = cu_ref[...], cc_ref[...]
        t = cc * cu
        t_dn, t_up = _shift_rows(t, rows, True), _shift_rows(t, rows, False)
        c3 = t_dn * w_ref[0:1, :] + t * w_ref[1:2, :] + t_up * w_ref[2:3, :]
        d = d_ref[...]
        dcb_ref[...] = (d * c3).astype(BF16)
        dc3 = d * cb_ref[...]
        dw_ref[0:1, :] = jnp.sum(dc3 * t_dn, axis=0, keepdims=True)
        dw_ref[1:2, :] = jnp.sum(dc3 * t, axis=0, keepdims=True)
        dw_ref[2:3, :] = jnp.sum(dc3 * t_up, axis=0, keepdims=True)
        dt = _shift_rows(dc3, rows, False) * w_ref[0:1, :] + dc3 * w_ref[1:2, :] + _shift_rows(dc3, rows, True) * w_ref[2:3, :]
        dcc_ref[...] = (dt * cu).astype(BF16)
        dcu_ref[...] = (dt * cc).astype(BF16)

    return pl.pallas_call(
        body, name="conv_bwd", grid=(CONV_WIDTH // CONV_CHUNK,),
        in_specs=[cu_spec, cb_spec, cc_spec, w_spec, chunk],
        out_specs=[chunk, chunk, chunk, w_spec],
        out_shape=[jax.ShapeDtypeStruct((s, CONV_WIDTH), BF16)] * 3 + [jax.ShapeDtypeStruct((3, CONV_WIDTH), F32)],
        compiler_params=_params(1),
    )(z, z, z, conv_w, d_co)


GATE_CHUNK = 512


def _gate_specs(s, d, tr):
    n_chunks = d // GATE_CHUNK
    za = pl.BlockSpec((tr, GATE_CHUNK), lambda j, i: (i, GL_OFF // GATE_CHUNK + j))
    zc = pl.BlockSpec((tr, GATE_CHUNK), lambda j, i: (i, GL_OFF // GATE_CHUNK + n_chunks + j))
    ba = pl.BlockSpec((1, GATE_CHUNK), lambda j, i: (0, j))
    bc = pl.BlockSpec((1, GATE_CHUNK), lambda j, i: (0, n_chunks + j))
    tile = pl.BlockSpec((tr, GATE_CHUNK), lambda j, i: (i, j))
    return za, zc, ba, bc, tile


def _gate_fwd(z, b_gate, ya, yc):
    s, d = ya.shape
    tr = _row_tile(s, 512)
    za, zc, ba, bc, tile = _gate_specs(s, d, tr)

    def body(za_ref, zc_ref, ba_ref, bc_ref, ya_ref, yc_ref, o_ref):
        ga = jax.nn.sigmoid(za_ref[...] + ba_ref[...])
        gc = jax.nn.sigmoid(zc_ref[...] + bc_ref[...])
        o_ref[...] = (ga * ya_ref[...] + gc * yc_ref[...]).astype(BF16)

    return pl.pallas_call(
        body, name="gate_fwd", grid=(d // GATE_CHUNK, s // tr),
        in_specs=[za, zc, ba, bc, tile, tile],
        out_specs=tile,
        out_shape=jax.ShapeDtypeStruct((s, d), BF16),
        compiler_params=_params(2),
    )(z, z, b_gate, b_gate, ya, yc)


def _gate_bwd(z, b_gate, ya, yc, dmix):
    s, d = ya.shape
    tr = _row_tile(s, 512)
    za, zc, ba, bc, tile = _gate_specs(s, d, tr)
    vec = pl.BlockSpec((1, GATE_CHUNK), lambda j, i: (0, j))

    def body(za_ref, zc_ref, ba_ref, bc_ref, ya_ref, yc_ref, dm_ref, dya_ref, dyc_ref, dla_ref, dlc_ref, dba_ref, dbc_ref):
        ga = jax.nn.sigmoid(za_ref[...] + ba_ref[...])
        gc = jax.nn.sigmoid(zc_ref[...] + bc_ref[...])
        dm = dm_ref[...]
        dya_ref[...] = (dm * ga).astype(BF16)
        dyc_ref[...] = (dm * gc).astype(BF16)
        dla = dm * ya_ref[...] * ga * (1.0 - ga)
        dlc = dm * yc_ref[...] * gc * (1.0 - gc)
        dla_ref[...] = dla.astype(BF16)
        dlc_ref[...] = dlc.astype(BF16)
        pa = jnp.sum(dla, axis=0, keepdims=True)
        pc = jnp.sum(dlc, axis=0, keepdims=True)

        @pl.when(pl.program_id(1) == 0)
        def _():
            dba_ref[...] = pa
            dbc_ref[...] = pc

        @pl.when(pl.program_id(1) > 0)
        def _():
            dba_ref[...] += pa
            dbc_ref[...] += pc

    big = jax.ShapeDtypeStruct((s, d), BF16)
    small = jax.ShapeDtypeStruct((1, d), F32)
    return pl.pallas_call(
        body, name="gate_bwd", grid=(d // GATE_CHUNK, s // tr),
        in_specs=[za, zc, ba, bc, tile, tile, tile],
        out_specs=[tile, tile, tile, tile, vec, vec],
        out_shape=[big, big, big, big, small, small],
        compiler_params=_params(2),
    )(z, z, b_gate, b_gate, ya, yc, dmix)


def _cross_probs(q_ref, kv_ref, hd):
    cols = slice(hd * HEAD_DIM, (hd + 1) * HEAD_DIM)
    qh = q_ref[:, cols]
    kh = kv_ref[:, cols]
    sc = lax.dot_general(qh, kh, (((1,), (1,)), ((), ())), preferred_element_type=F32) * ATTN_SCALE
    e = jnp.exp(sc - jnp.max(sc, axis=1, keepdims=True))
    return qh, kh, e * (1.0 / jnp.sum(e, axis=1, keepdims=True))


def _cross_fwd(qc, kvc):
    s = qc.shape[0]
    n_mem = kvc.shape[0]
    tq = _row_tile(s, 256)

    def body(q_ref, kv_ref, o_ref):
        for hd in range(MEM_HEADS):
            _, _, p = _cross_probs(q_ref, kv_ref, hd)
            vh = kv_ref[:, MEM_WIDTH + hd * HEAD_DIM:MEM_WIDTH + (hd + 1) * HEAD_DIM]
            o_ref[:, hd * HEAD_DIM:(hd + 1) * HEAD_DIM] = jnp.dot(p.astype(BF16), vh, preferred_element_type=F32).astype(BF16)

    return pl.pallas_call(
        body, name="cross_fwd", grid=(s // tq,),
        in_specs=[pl.BlockSpec((tq, MEM_WIDTH), lambda i: (i, 0)), pl.BlockSpec((n_mem, 2 * MEM_WIDTH), lambda i: (0, 0))],
        out_specs=pl.BlockSpec((tq, MEM_WIDTH), lambda i: (i, 0)),
        out_shape=jax.ShapeDtypeStruct((s, MEM_WIDTH), BF16),
        compiler_params=_params(1),
    )(qc, kvc)


def _cross_bwd(qc, kvc, d_out):
    s = qc.shape[0]
    n_mem = kvc.shape[0]
    tq = _row_tile(s, 256)

    def body(q_ref, kv_ref, do_ref, dq_ref, dkv_ref):
        @pl.when(pl.program_id(0) == 0)
        def _():
            dkv_ref[...] = jnp.zeros_like(dkv_ref)

        for hd in range(MEM_HEADS):
            cols = slice(hd * HEAD_DIM, (hd + 1) * HEAD_DIM)
            vcols = slice(MEM_WIDTH + hd * HEAD_DIM, MEM_WIDTH + (hd + 1) * HEAD_DIM)
            qh, kh, p = _cross_probs(q_ref, kv_ref, hd)
            doh = do_ref[:, cols]
            dp = lax.dot_general(doh, kv_ref[:, vcols], (((1,), (1,)), ((), ())), preferred_element_type=F32)
            ds = (p * (dp - jnp.sum(p * dp, axis=1, keepdims=True)) * ATTN_SCALE).astype(BF16)
            dq_ref[:, cols] = jnp.dot(ds, kh, preferred_element_type=F32).astype(BF16)
            dkv_ref[:, cols] += lax.dot_general(ds, qh, (((0,), (0,)), ((), ())), preferred_element_type=F32)
            dkv_ref[:, vcols] += lax.dot_general(p.astype(BF16), doh, (((0,), (0,)), ((), ())), preferred_element_type=F32)

    qspec = pl.BlockSpec((tq, MEM_WIDTH), lambda i: (i, 0))
    kvspec = pl.BlockSpec((n_mem, 2 * MEM_WIDTH), lambda i: (0, 0))
    return pl.pallas_call(
        body, name="cross_bwd", grid=(s // tq,),
        in_specs=[qspec, kvspec, qspec],
        out_specs=[qspec, kvspec],
        out_shape=[jax.ShapeDtypeStruct((s, MEM_WIDTH), BF16), jax.ShapeDtypeStruct((n_mem, 2 * MEM_WIDTH), F32)],
        compiler_params=_params(1),
    )(qc, kvc, d_out)


def _swiglu_fwd(up, gate):
    return up, (gate * jax.nn.sigmoid(gate)) * up


def _swiglu_bwd(d_act, gate, up):
    sg = jax.nn.sigmoid(gate)
    silu = gate * sg
    return d_act * up * (sg * (1.0 + gate * (1.0 - sg))), d_act * silu


def _local_step(xs, mems, target, small, conv_w, w4):
    s, d = xs.shape
    w_o = w4["w_o"].reshape(-1, w4["w_o"].shape[-1])
    w_cq = w4["w_cq"].reshape(-1, w4["w_cq"].shape[-1])
    w_ckv = w4["w_ckv"].reshape(-1, w4["w_ckv"].shape[-1])
    w_down = w4["w_down"].reshape(-1, w4["w_down"].shape[-1])
    c_in = w4["w_in"].shape[2]
    c_ff = w4["w_gate"].shape[2]
    c_d = w4["w_attn_out"].shape[2]
    cos_t, sin_t = _rope_tables(s)

    h = _rmsnorm(xs, small["g_mix"], "norm_mix")
    z = _matmul(h, w4["w_in"], mode="nn", tm=512, tn=c_in, tk=d, out_dtypes=[F32], name="in_proj", b_blocks=N_CHIPS)
    q_rot, k_rot, v_b = _rope_fwd(z, cos_t, sin_t)
    attn = _swa_fwd(q_rot, k_rot, v_b, small["sink"])
    co = _conv_fwd(z, conv_w)
    ya = _matmul(attn, w4["w_attn_out"], mode="nn", tm=1024, tn=c_d, tk=ATTN_WIDTH, out_dtypes=[F32], name="attn_out_proj",
                 b_blocks=N_CHIPS)
    yc = _matmul(co, w4["w_conv_out"], mode="nn", tm=1024, tn=c_d, tk=CONV_WIDTH, out_dtypes=[F32], name="conv_out_proj",
                 b_blocks=N_CHIPS)
    mix = _gate_fwd(z, small["b_gate"], ya, yc)
    x1 = _matmul(mix, w_o, mode="nn", tm=512, tn=1024, tk=d, out_dtypes=[F32], name="mix_out_proj", extras=[xs],
                 epilogue=_add_residual)
    hc = _rmsnorm(x1, small["g_cross"], "norm_cross")
    memn = _rmsnorm(mems, small["g_mem"], "norm_mem")
    qc = _matmul(hc, w_cq, mode="nn", tm=1024, tn=MEM_WIDTH, tk=d, out_dtypes=[BF16], name="cross_q_proj")
    kvc = _matmul(memn, w_ckv, mode="nn", tm=256, tn=2 * MEM_WIDTH, tk=d, out_dtypes=[BF16], name="cross_kv_proj")
    oc = _cross_fwd(qc, kvc)
    x2 = _matmul(oc, w4["w_co"], mode="nn", tm=1024, tn=c_d, tk=MEM_WIDTH, out_dtypes=[F32], name="cross_out_proj",
                 extras=[x1], epilogue=_add_residual, b_blocks=N_CHIPS)
    hf = _rmsnorm(x2, small["g_ffn"], "norm_ffn")
    gate = _matmul(hf, w4["w_gate"], mode="nn", tm=512, tn=c_ff, tk=d, out_dtypes=[F32], name="ffn_gate_proj", b_blocks=N_CHIPS)
    up, act = _matmul(hf, w4["w_up"], mode="nn", tm=512, tn=c_ff, tk=d, out_dtypes=[F32, BF16], name="ffn_up_proj",
                      extras=[gate], epilogue=_swiglu_fwd, b_blocks=N_CHIPS)
    x3 = _matmul(act, w_down, mode="nn", tm=512, tn=1024, tk=c_ff, out_dtypes=[F32], name="ffn_down_proj", extras=[x2],
                 epilogue=_add_residual)
    dx3, dx3b, sq, dg_final = _loss_head(x3, small["g_final"], target)

    grads = {}
    da, du = _matmul(dx3b, w_down, mode="nt", tm=512, tn=c_ff, tk=d, out_dtypes=[BF16, BF16], name="ffn_down_bwd",
                     extras=[gate, up], epilogue=_swiglu_bwd)
    grads["w_down"] = _matmul(act, dx3b, mode="tn", tm=c_ff, tn=1024, tk=s, out_dtypes=[BF16], name="ffn_down_wgrad")
    grads["w_gate"] = _matmul(hf, da, mode="tn", tm=512, tn=c_ff, tk=s, out_dtypes=[BF16], name="ffn_gate_wgrad",
                              out_blocks=N_CHIPS)
    grads["w_up"] = _matmul(hf, du, mode="tn", tm=512, tn=c_ff, tk=s, out_dtypes=[BF16], name="ffn_up_wgrad", out_blocks=N_CHIPS)
    dhf = _matmul(da, w4["w_gate"], mode="nt", tm=512, tn=1024, tk=c_ff, out_dtypes=[F32], name="ffn_gate_bwd", b_blocks=N_CHIPS)
    dhf = _matmul(du, w4["w_up"], mode="nt", tm=512, tn=1024, tk=c_ff, out_dtypes=[F32], name="ffn_up_bwd", extras=[dhf],
                  epilogue=_add_residual, b_blocks=N_CHIPS)
    dx2, dx2b, dg_ffn = _rmsnorm_bwd(dhf, x2, small["g_ffn"], dx3, "norm_ffn_bwd")

    d_oc = _matmul(dx2b, w4["w_co"], mode="nt", tm=1024, tn=MEM_WIDTH, tk=c_d, out_dtypes=[BF16], name="cross_out_bwd",
                   b_blocks=N_CHIPS)
    grads["w_co"] = _matmul(oc, dx2b, mode="tn", tm=MEM_WIDTH, tn=c_d, tk=s, out_dtypes=[BF16], name="cross_out_wgrad",
                            out_blocks=N_CHIPS)
    dqc, dkvc = _cross_bwd(qc, kvc, d_oc)
    grads["w_cq"] = _matmul(hc, dqc, mode="tn", tm=1024, tn=MEM_WIDTH, tk=s, out_dtypes=[BF16], name="cross_q_wgrad")
    dhc = _matmul(dqc, w_cq, mode="nt", tm=1024, tn=1024, tk=MEM_WIDTH, out_dtypes=[F32], name="cross_q_bwd")
    grads["w_ckv"] = _matmul(memn, dkvc, mode="tn", tm=1024, tn=2 * MEM_WIDTH, tk=mems.shape[0], out_dtypes=[BF16],
                             name="cross_kv_wgrad")
    dmemn = _matmul(dkvc, w_ckv, mode="nt", tm=256, tn=1024, tk=2 * MEM_WIDTH, out_dtypes=[F32], name="cross_kv_bwd")
    _, _, dg_mem = _rmsnorm_bwd(dmemn, mems, small["g_mem"], None, "norm_mem_bwd")
    dx1, dx1b, dg_cross = _rmsnorm_bwd(dhc, x1, small["g_cross"], dx2, "norm_cross_bwd")

    dmix = _matmul(dx1b, w_o, mode="nt", tm=512, tn=1024, tk=d, out_dtypes=[F32], name="mix_out_bwd")
    grads["w_o"] = _matmul(mix, dx1b, mode="tn", tm=1024, tn=1024, tk=s, out_dtypes=[BF16], name="mix_out_wgrad")
    dya, dyc, dgl_a, dgl_c, db_a, db_c = _gate_bwd(z, small["b_gate"], ya, yc, dmix)
    d_attn = _matmul(dya, w4["w_attn_out"], mode="nt", tm=1024, tn=ATTN_WIDTH, tk=c_d, out_dtypes=[BF16], name="attn_out_bwd",
                     b_blocks=N_CHIPS)
    grads["w_attn_out"] = _matmul(attn, dya, mode="tn", tm=ATTN_WIDTH, tn=c_d, tk=s, out_dtypes=[BF16], name="attn_out_wgrad",
                                  out_blocks=N_CHIPS)
    d_co = _matmul(dyc, w4["w_conv_out"], mode="nt", tm=1024, tn=CONV_WIDTH, tk=c_d, out_dtypes=[F32], name="conv_out_bwd",
                   b_blocks=N_CHIPS)
    grads["w_conv_out"] = _matmul(co, dyc, mode="tn", tm=CONV_WIDTH, tn=c_d, tk=s, out_dtypes=[BF16], name="conv_out_wgrad",
                                  out_blocks=N_CHIPS)
    dcu, dcb, dcc, d_conv_w = _conv_bwd(z, conv_w, d_co)
    dq_rot, dk_rot, dv, dsink = _swa_bwd(q_rot, k_rot, v_b, d_attn, small["sink"])
    dq, dk, dvb = _rope_bwd(dq_rot, dk_rot, dv, cos_t, sin_t)
    dz = jnp.concatenate([dq, dk, dvb, dcu, dcb, dcc, dgl_a, dgl_c], axis=1)
    grads["w_in"] = _matmul(h, dz, mode="tn", tm=512, tn=c_in, tk=s, out_dtypes=[BF16], name="in_proj_wgrad", out_blocks=N_CHIPS)
    dh = _matmul(dz, w4["w_in"], mode="nt", tm=512, tn=1024, tk=c_in, out_dtypes=[F32], name="in_proj_bwd", b_blocks=N_CHIPS)
    grad_x, _, dg_mix = _rmsnorm_bwd(dh, xs, small["g_mix"], dx1, "norm_mix_bwd")

    small_grads = {
        "g_mix": dg_mix, "sink": dsink[:, 0], "b_gate": jnp.concatenate([db_a, db_c], axis=1), "g_cross": dg_cross,
        "g_mem": dg_mem, "g_ffn": dg_ffn, "g_final": dg_final, "conv_w": d_conv_w,
    }
    return sq, grad_x, small_grads, grads


def _pair_sum(g4, ra, core, name):
    nb, rs, cs = g4.shape
    rh = rs // 2
    tr = _row_tile(rh, 256)
    per = rh // tr

    def body(c_ref, g_ref, r_ref, o_ref):
        o_ref[...] = (g_ref[...].astype(F32) + r_ref[...].astype(F32)).astype(BF16)

    plain = pl.BlockSpec((None, tr, cs), lambda j, i, c: (j, i, 0))
    return pl.pallas_call(
        body, name=name,
        grid_spec=pltpu.PrefetchScalarGridSpec(
            num_scalar_prefetch=1, grid=(nb, per),
            in_specs=[pl.BlockSpec((None, tr, cs), lambda j, i, c: (j, c[0] * per + i, 0)), plain],
            out_specs=plain),
        out_shape=jax.ShapeDtypeStruct((nb, rh, cs), BF16),
        compiler_params=_params(2),
    )(core, g4, ra)


def _quad_sum(parts, rc, place, name):
    _, rh, cs = parts.shape
    tr = _row_tile(rh, 256)
    per = rh // tr

    def body(p_ref, own_ref, r_ref, o_ref):
        acc = own_ref[...].astype(F32)
        for j in range(rc.shape[0]):
            acc = acc + r_ref[j].astype(F32)
        o_ref[...] = acc

    return pl.pallas_call(
        body, name=name,
        grid_spec=pltpu.PrefetchScalarGridSpec(
            num_scalar_prefetch=1, grid=(per,),
            in_specs=[pl.BlockSpec((None, tr, cs), lambda i, p: (p[0], i, 0)),
                      pl.BlockSpec((rc.shape[0], tr, cs), lambda i, p: (0, i, 0))],
            out_specs=pl.BlockSpec((tr, cs), lambda i, p: (p[1] * per + i, 0))),
        out_shape=jax.ShapeDtypeStruct((2 * rh, cs), F32),
        compiler_params=_params(1),
    )(place, parts, rc)


def _cast_to_slot(w, place, dtype, name):
    rows, cols = w.shape
    tr = _row_tile(rows, 256)

    def body(p_ref, w_ref, o_ref):
        o_ref[...] = w_ref[...].astype(dtype)

    return pl.pallas_call(
        body, name=name,
        grid_spec=pltpu.PrefetchScalarGridSpec(
            num_scalar_prefetch=1, grid=(rows // tr,),
            in_specs=[pl.BlockSpec((tr, cols), lambda i, p: (i, 0))],
            out_specs=pl.BlockSpec((None, tr, cols), lambda i, p: (p[0], i, 0))),
        out_shape=jax.ShapeDtypeStruct((N_CHIPS, rows, cols), dtype),
        compiler_params=_params(1),
    )(place, w)


def _adamw(w, g, m, v, name):
    rows, cols = w.shape
    tr = _row_tile(rows, 256)

    def body(w_ref, g_ref, m_ref, v_ref, go_ref, d_ref, nm_ref, nv_ref):
        gv = g_ref[...]
        go_ref[...] = gv
        nm = ADAM_B1 * m_ref[...] + (1.0 - ADAM_B1) * gv
        nv = ADAM_B2 * v_ref[...] + (1.0 - ADAM_B2) * (gv * gv)
        m_hat = nm / ADAM_C1
        v_hat = nv / ADAM_C2
        d_ref[...] = -ADAM_LR * (m_hat / (jnp.sqrt(v_hat) + ADAM_EPS) + ADAM_WD * w_ref[...])
        nm_ref[...] = nm
        nv_ref[...] = nv

    tile = pl.BlockSpec((tr, cols), lambda i: (i, 0))
    shape = jax.ShapeDtypeStruct((rows, cols), F32)
    return pl.pallas_call(
        body, name=name, grid=(rows // tr,),
        in_specs=[tile] * 4, out_specs=[tile] * 4, out_shape=[shape] * 4,
        compiler_params=_params(1),
    )(w, g, m, v)


def _mesh_pos():
    return lax.axis_index("x"), lax.axis_index("y"), lax.axis_index("c")


def _other_chips(x, y):
    return [(1 - x, y), (x, 1 - y), (1 - x, 1 - y)]


def _half_rows(ref, which):
    rh = ref.shape[-2] // 2
    return ref.at[pl.ds(which * rh, rh), :]


def _remote(src, dst, send_sems, recv_sems, sem, to):
    return pltpu.make_async_remote_copy(src_ref=src, dst_ref=dst, send_sem=send_sems.at[sem], recv_sem=recv_sems.at[sem],
                                        device_id=to, device_id_type=MESH)


def _all_gather(slotted):
    n = len(slotted)

    def body(*refs):
        buf = refs[n:2 * n]
        send_sems, recv_sems = refs[2 * n:]
        x, y, c = _mesh_pos()
        me = 2 * x + y
        chips = _other_chips(x, y)
        sends = []
        for w in range(n):
            mine = _half_rows(buf[w].at[me], c)
            for k, (px, py) in enumerate(chips):
                cp = _remote(mine, mine, send_sems, recv_sems, 6 * w + k, (px, py, c))
                cp.start()
                sends.append(cp)
        for w in range(n):
            for k, (px, py) in enumerate(chips):
                landed = _half_rows(buf[w].at[2 * px + py], c)
                _remote(landed, landed, send_sems, recv_sems, 6 * w + k, (px, py, c)).wait_recv()
                cp = _remote(landed, landed, send_sems, recv_sems, 6 * w + 3 + k, (x, y, 1 - c))
                cp.start()
                sends.append(cp)
        for w in range(n):
            for k, (px, py) in enumerate(chips):
                passed = _half_rows(buf[w].at[2 * px + py], 1 - c)
                _remote(passed, passed, send_sems, recv_sems, 6 * w + 3 + k, (x, y, 1 - c)).wait_recv()
        for cp in sends:
            cp.wait_send()

    return pl.pallas_call(
        body, name="weights_all_gather",
        in_specs=[ANY] * n, out_specs=[ANY] * n,
        out_shape=[jax.ShapeDtypeStruct(s.shape, s.dtype) for s in slotted],
        input_output_aliases={i: i for i in range(n)},
        scratch_shapes=[pltpu.SemaphoreType.DMA((6 * n,)), pltpu.SemaphoreType.DMA((6 * n,))],
    )(*slotted)


def _sibling_exchange_halves(g4s):
    n = len(g4s)

    def body(*refs):
        src, dst = refs[:n], refs[n:2 * n]
        send_sems, recv_sems = refs[2 * n:]
        x, y, c = _mesh_pos()
        copies = []
        for w in range(n):
            rh = src[w].shape[1] // 2
            cp = _remote(src[w].at[:, pl.ds((1 - c) * rh, rh), :], dst[w], send_sems, recv_sems, w, (x, y, 1 - c))
            cp.start()
            copies.append(cp)
        for cp in copies:
            cp.wait_recv()
        for cp in copies:
            cp.wait_send()

    return pl.pallas_call(
        body, name="grads_sibling_exchange",
        in_specs=[ANY] * n, out_specs=[ANY] * n,
        out_shape=[jax.ShapeDtypeStruct((g.shape[0], g.shape[1] // 2, g.shape[2]), g.dtype) for g in g4s],
        scratch_shapes=[pltpu.SemaphoreType.DMA((n,)), pltpu.SemaphoreType.DMA((n,))],
    )(*g4s)


def _chip_exchange(parts):
    n = len(parts)

    def body(*refs):
        src, dst = refs[:n], refs[n:2 * n]
        send_sems, recv_sems = refs[2 * n:]
        x, y, c = _mesh_pos()
        chips = _other_chips(x, y)
        sends = []
        for w in range(n):
            for k, (px, py) in enumerate(chips):
                cp = _remote(src[w].at[2 * px + py], dst[w].at[k], send_sems, recv_sems, 3 * w + k, (px, py, c))
                cp.start()
                sends.append(cp)
        for cp in sends:
            cp.wait_recv()
        for cp in sends:
            cp.wait_send()

    return pl.pallas_call(
        body, name="grads_chip_exchange",
        in_specs=[ANY] * n, out_specs=[ANY] * n,
        out_shape=[jax.ShapeDtypeStruct((N_CHIPS - 1,) + p.shape[1:], p.dtype) for p in parts],
        scratch_shapes=[pltpu.SemaphoreType.DMA((3 * n,)), pltpu.SemaphoreType.DMA((3 * n,))],
    )(*parts)


def _sibling_join_halves(wholes):
    n = len(wholes)

    def body(*refs):
        buf = refs[n:2 * n]
        send_sems, recv_sems = refs[2 * n:]
        x, y, c = _mesh_pos()
        sends = []
        for w in range(n):
            mine = _half_rows(buf[w], c)
            cp = _remote(mine, mine, send_sems, recv_sems, w, (x, y, 1 - c))
            cp.start()
            sends.append(cp)
        for w in range(n):
            other = _half_rows(buf[w], 1 - c)
            _remote(other, other, send_sems, recv_sems, w, (x, y, 1 - c)).wait_recv()
        for cp in sends:
            cp.wait_send()

    return pl.pallas_call(
        body, name="grads_sibling_join",
        in_specs=[ANY] * n, out_specs=[ANY] * n,
        out_shape=[jax.ShapeDtypeStruct(h.shape, h.dtype) for h in wholes],
        input_output_aliases={i: i for i in range(n)},
        scratch_shapes=[pltpu.SemaphoreType.DMA((n,)), pltpu.SemaphoreType.DMA((n,))],
    )(*wholes)


N_DEV = 8


def _all_reduce_small(v):
    def body(v_ref, o_ref, slots, send_sems, recv_sems):
        x, y, c = _mesh_pos()
        me = 4 * x + 2 * y + c
        slots[me] = v_ref[...]
        peers = []
        for r in range(1, N_DEV):
            fx, fy, fc = (r >> 2) & 1, (r >> 1) & 1, r & 1
            peers.append((x + fx - 2 * x * fx, y + fy - 2 * y * fy, c + fc - 2 * c * fc))
        sends = []
        for r, peer in enumerate(peers):
            cp = _remote(v_ref, slots.at[me], send_sems, recv_sems, r, peer)
            cp.start()
            sends.append(cp)
        for r, (px, py, pc) in enumerate(peers):
            landed = slots.at[4 * px + 2 * py + pc]
            _remote(landed, landed, send_sems, recv_sems, r, (px, py, pc)).wait_recv()
        for cp in sends:
            cp.wait_send()
        acc = slots[0]
        for i in range(1, N_DEV):
            acc = acc + slots[i]
        o_ref[...] = acc

    vm = pl.BlockSpec(memory_space=pltpu.VMEM)
    return pl.pallas_call(
        body, name="small_grads_all_reduce",
        in_specs=[vm], out_specs=vm,
        out_shape=jax.ShapeDtypeStruct(v.shape, v.dtype),
        scratch_shapes=[pltpu.VMEM((N_DEV,) + v.shape, v.dtype), pltpu.SemaphoreType.DMA((N_DEV - 1,)),
                        pltpu.SemaphoreType.DMA((N_DEV - 1,))],
    )(v)


MATRICES = ("w_in", "w_attn_out", "w_conv_out", "w_o", "w_cq", "w_ckv", "w_co", "w_gate", "w_up", "w_down")
VECTORS = ("g_mix", "b_gate", "g_cross", "g_mem", "g_ffn", "g_final", "conv_w", "sink")
WEIGHT_ORDER = ("g_mix", "w_in", "sink", "conv_w", "b_gate", "w_attn_out", "w_conv_out", "w_o", "g_cross", "g_mem", "w_cq",
                "w_ckv", "w_co", "g_ffn", "w_gate", "w_up", "w_down", "g_final")
CONV_PAD_ROWS = 16
SMALL_ROWS = 8


def _pack(pieces):
    flat = jnp.concatenate([p.reshape(-1) for p in pieces])
    lane_group = SMALL_ROWS * 128
    total = -(-flat.shape[0] // lane_group) * lane_group
    flat = jnp.pad(flat, (0, total - flat.shape[0]))
    return flat.reshape(SMALL_ROWS, total // SMALL_ROWS), [p.size for p in pieces]


def _unpack(packed, pieces):
    flat = packed.reshape(-1)
    out, off = [], 0
    for p in pieces:
        out.append(flat[off:off + p.size].reshape(p.shape))
        off += p.size
    return out


def kernel(x, mem, g_mix, w_in, sink, conv_w, b_gate, w_attn_out, w_conv_out, w_o, g_cross, g_mem, w_cq, w_ckv, w_co, g_ffn, w_gate, w_up, w_down, g_final, loss_target, m_g_mix, m_w_in, m_sink, m_conv_w, m_b_gate, m_w_attn_out, m_w_conv_out, m_w_o, m_g_cross, m_g_mem, m_w_cq, m_w_ckv, m_w_co, m_g_ffn, m_w_gate, m_w_up, m_w_down, m_g_final, v_g_mix, v_w_in, v_sink, v_conv_w, v_b_gate, v_w_attn_out, v_w_conv_out, v_w_o, v_g_cross, v_g_mem, v_w_cq, v_w_ckv, v_w_co, v_g_ffn, v_w_gate, v_w_up, v_w_down, v_g_final):
    given = dict(g_mix=g_mix, w_in=w_in, sink=sink, conv_w=conv_w, b_gate=b_gate, w_attn_out=w_attn_out, w_conv_out=w_conv_out,
                 w_o=w_o, g_cross=g_cross, g_mem=g_mem, w_cq=w_cq, w_ckv=w_ckv, w_co=w_co, g_ffn=g_ffn, w_gate=w_gate, w_up=w_up,
                 w_down=w_down, g_final=g_final)
    mom_m = dict(g_mix=m_g_mix, w_in=m_w_in, sink=m_sink, conv_w=m_conv_w, b_gate=m_b_gate, w_attn_out=m_w_attn_out,
                 w_conv_out=m_w_conv_out, w_o=m_w_o, g_cross=m_g_cross, g_mem=m_g_mem, w_cq=m_w_cq, w_ckv=m_w_ckv, w_co=m_w_co,
                 g_ffn=m_g_ffn, w_gate=m_w_gate, w_up=m_w_up, w_down=m_w_down, g_final=m_g_final)
    mom_v = dict(g_mix=v_g_mix, w_in=v_w_in, sink=v_sink, conv_w=v_conv_w, b_gate=v_b_gate, w_attn_out=v_w_attn_out,
                 w_conv_out=v_w_conv_out, w_o=v_w_o, g_cross=v_g_cross, g_mem=v_g_mem, w_cq=v_w_cq, w_ckv=v_w_ckv, w_co=v_w_co,
                 g_ffn=v_g_ffn, w_gate=v_w_gate, w_up=v_w_up, w_down=v_w_down, g_final=v_g_final)
    xs, mems, target = x[0], mem[0], loss_target[0]
    d_model = xs.shape[1]
    chip = 2 * lax.axis_index("x") + lax.axis_index("y")
    core = jnp.reshape(lax.axis_index("c"), (1,)).astype(jnp.int32)
    place = jnp.stack([chip, lax.axis_index("c")]).astype(jnp.int32)

    shards = {n: given[n][0] for n in MATRICES}
    conv_cols = conv_w.shape[2]
    conv_pad = jnp.pad(conv_w[0], ((0, CONV_PAD_ROWS - conv_w.shape[1]), (0, 0)))
    slotted = [_cast_to_slot(shards[n], place, BF16, "to_slot_" + n) for n in MATRICES]
    slotted.append(_cast_to_slot(conv_pad, place, F32, "to_slot_conv_w"))
    gathered = _all_gather(slotted)
    w4 = dict(zip(MATRICES, gathered[:-1]))
    conv_full = gathered[-1][:, :conv_w.shape[1], :].transpose(1, 0, 2).reshape(conv_w.shape[1], N_CHIPS * conv_cols)
    small = {n: given[n] for n in ("g_mix", "b_gate", "g_cross", "g_mem", "g_ffn")}
    small["g_final"] = g_final[None]
    small["sink"] = sink[0]

    sq, grad_x, small_grads, grads = _local_step(xs, mems, target, small, conv_full, w4)

    g4 = [grads[n].reshape((N_CHIPS, -1, grads[n].shape[-1])) if grads[n].ndim == 2 else grads[n] for n in MATRICES]
    from_sibling = _sibling_exchange_halves(g4)
    chip_parts = [_pair_sum(g, r, core, "pair_sum_" + n) for g, r, n in zip(g4, from_sibling, MATRICES)]
    from_chips = _chip_exchange(chip_parts)
    halves = [_quad_sum(p, r, place, "quad_sum_" + n) for p, r, n in zip(chip_parts, from_chips, MATRICES)]
    reduced = dict(zip(MATRICES, _sibling_join_halves(halves)))

    loss_part = 0.5 * sq[0:1, 0:1] / d_model
    pieces = [small_grads[n] for n in VECTORS] + [loss_part]
    packed, _ = _pack(pieces)
    summed = _unpack(_all_reduce_small(packed), pieces)
    loss = summed[-1][0, 0]
    small_sum = dict(zip(VECTORS, summed[:-1]))
    small_sum["conv_w"] = lax.dynamic_slice_in_dim(small_sum["conv_w"], chip * conv_cols, conv_cols, axis=1)

    grad_out, delta, new_m, new_v = {}, {}, {}, {}
    for n in MATRICES:
        g, d, nm, nv = _adamw(shards[n], reduced[n], mom_m[n][0], mom_v[n][0], "adamw_" + n)
        grad_out[n], delta[n], new_m[n], new_v[n] = g[None], d[None], nm[None], nv[None]
    like = [given[n] for n in VECTORS]
    pw, _ = _pack(like)
    pg, _ = _pack([small_sum[n] for n in VECTORS])
    pm, _ = _pack([mom_m[n] for n in VECTORS])
    pv, _ = _pack([mom_v[n] for n in VECTORS])
    _, pd, pnm, pnv = _adamw(pw, pg, pm, pv, "adamw_small")
    for n, g, d, nm, nv in zip(VECTORS, [small_sum[n] for n in VECTORS], _unpack(pd, like), _unpack(pnm, like), _unpack(pnv, like)):
        grad_out[n] = g.reshape(given[n].shape)
        delta[n], new_m[n], new_v[n] = d, nm, nv

    return (loss, grad_x[None], *[grad_out[n] for n in WEIGHT_ORDER], *[delta[n] for n in WEIGHT_ORDER],
            *[new_m[n] for n in WEIGHT_ORDER], *[new_v[n] for n in WEIGHT_ORDER])
```

```python
import functools

import jax
import jax.numpy as jnp
from jax import lax
from jax.experimental import pallas as pl
from jax.experimental.pallas import tpu as pltpu

F32 = jnp.float32
BF16 = jnp.bfloat16
MESH = pl.DeviceIdType.MESH
ANY = pl.BlockSpec(memory_space=pl.ANY)

VMEM_LIMIT_BYTES = 56 * 1024 * 1024

N_CHIPS = 4
HEAD_DIM = 128
N_Q_HEADS = 8
N_KV_HEADS = 2
Q_GROUP = N_Q_HEADS // N_KV_HEADS
ATTN_WIDTH = N_Q_HEADS * HEAD_DIM
KV_WIDTH = N_KV_HEADS * HEAD_DIM
WINDOW = 128
BLOCK = 128
BAND = 3 * BLOCK
ROPE_THETA = 10000.0
CONV_WIDTH = 1024
MEM_HEADS = 4
MEM_WIDTH = MEM_HEADS * HEAD_DIM
RMS_EPS = 1e-6
NEG_INF = -1e30
ATTN_SCALE = HEAD_DIM ** -0.5

Q_OFF, K_OFF, V_OFF, CU_OFF, CB_OFF, CC_OFF, GL_OFF = 0, 1024, 1280, 1536, 2560, 3584, 4608

ADAM_LR = 0.001
ADAM_B1 = 0.9
ADAM_B2 = 0.999
ADAM_EPS = 1e-08
ADAM_WD = 0.01
ADAM_STEP = 10
ADAM_C1 = 1.0 - ADAM_B1 ** ADAM_STEP
ADAM_C2 = 1.0 - ADAM_B2 ** ADAM_STEP


def _params(n_grid_axes):
    return pltpu.CompilerParams(dimension_semantics=("arbitrary",) * n_grid_axes, vmem_limit_bytes=VMEM_LIMIT_BYTES)


def _row_tile(rows, want):
    t = min(want, rows)
    while rows % t:
        t //= 2
    return t


def _matmul(a, b, *, mode, tm, tn, tk, out_dtypes, name, extras=(), epilogue=None, b_blocks=1, out_blocks=1):
    if mode == "tn":
        kdim, m = a.shape
    else:
        m, kdim = a.shape
    if b_blocks > 1:
        nb, brows, bcols = b.shape
        assert nb == b_blocks
        if mode == "nn":
            n = bcols * nb
            assert brows == kdim
        else:
            assert mode == "nt" and bcols * nb == kdim
            n = brows
    else:
        n = b.shape[0] if mode == "nt" else b.shape[1]
    tm, tn, tk = min(tm, m), min(tn, n), min(tk, kdim)
    assert m % tm == 0 and n % tn == 0 and kdim % tk == 0, (name, m, n, kdim, tm, tn, tk)
    nk = kdim // tk
    n_extra, n_out = len(extras), len(out_dtypes)

    if mode == "tn":
        a_spec = pl.BlockSpec((tk, tm), lambda j, i, k: (k, i))
        dims = (((0,), (0,)), ((), ()))
    else:
        a_spec = pl.BlockSpec((tm, tk), lambda j, i, k: (i, k))
        dims = (((1,), (0,)), ((), ())) if mode == "nn" else (((1,), (1,)), ((), ()))

    if b_blocks > 1 and mode == "nn":
        per = b.shape[2] // tn
        assert b.shape[2] % tn == 0
        b_spec = pl.BlockSpec((None, tk, tn), lambda j, i, k: (j // per, k, j % per))
    elif b_blocks > 1:
        per = b.shape[2] // tk
        assert b.shape[2] % tk == 0
        b_spec = pl.BlockSpec((None, tn, tk), lambda j, i, k: (k // per, j, k % per))
    elif mode == "nt":
        b_spec = pl.BlockSpec((tn, tk), lambda j, i, k: (j, k))
    else:
        b_spec = pl.BlockSpec((tk, tn), lambda j, i, k: (k, j))

    tile_spec = pl.BlockSpec((tm, tn), lambda j, i, k: (i, j))
    if out_blocks > 1:
        ncols = n // out_blocks
        assert ncols % tn == 0
        oper = ncols // tn
        out_spec = pl.BlockSpec((None, tm, tn), lambda j, i, k: (j // oper, i, j % oper))
        out_shape = [jax.ShapeDtypeStruct((out_blocks, m, ncols), dt) for dt in out_dtypes]
    else:
        out_spec = tile_spec
        out_shape = [jax.ShapeDtypeStruct((m, n), dt) for dt in out_dtypes]

    def body(a_ref, b_ref, *rest):
        extra_refs = rest[:n_extra]
        out_refs = rest[n_extra:n_extra + n_out]

        def finish(acc):
            if epilogue is None:
                tiles = (acc,)
            else:
                tiles = epilogue(acc, *[r[...] for r in extra_refs])
            for o_ref, t in zip(out_refs, tiles, strict=True):
                o_ref[...] = t.astype(o_ref.dtype)

        part = lax.dot_general(a_ref[...].astype(BF16), b_ref[...].astype(BF16), dims, preferred_element_type=F32)
        if nk == 1:
            finish(part)
        else:
            acc_ref = rest[-1]
            k = pl.program_id(2)

            @pl.when(k == 0)
            def _():
                acc_ref[...] = part

            @pl.when(k > 0)
            def _():
                acc_ref[...] += part

            @pl.when(k == nk - 1)
            def _():
                finish(acc_ref[...])

    outs = pl.pallas_call(
        body,
        name=name,
        grid=(n // tn, m // tm, nk),
        in_specs=[a_spec, b_spec] + [tile_spec] * n_extra,
        out_specs=[out_spec] * n_out,
        out_shape=out_shape,
        scratch_shapes=[pltpu.VMEM((tm, tn), F32)] if nk > 1 else [],
        compiler_params=_params(3),
    )(a, b, *extras)
    return outs[0] if n_out == 1 else outs


def _add_residual(acc, res):
    return (acc + res,)


def _rstd(x):
    return lax.rsqrt(jnp.mean(x * x, axis=-1, keepdims=True) + RMS_EPS)


def _rmsnorm(x, g, name):
    s, d = x.shape
    tr = _row_tile(s, 256)

    def body(x_ref, g_ref, o_ref):
        xv = x_ref[...]
        o_ref[...] = (xv * _rstd(xv) * g_ref[...]).astype(BF16)

    return pl.pallas_call(
        body, name=name, grid=(s // tr,),
        in_specs=[pl.BlockSpec((tr, d), lambda i: (i, 0)), pl.BlockSpec((1, d), lambda i: (0, 0))],
        out_specs=pl.BlockSpec((tr, d), lambda i: (i, 0)),
        out_shape=jax.ShapeDtypeStruct((s, d), BF16),
        compiler_params=_params(1),
    )(x, g)


def _rmsnorm_bwd(dh, x, g, dres, name):
    s, d = x.shape
    tr = _row_tile(s, 256)
    has_res = dres is not None

    def body(*refs):
        if has_res:
            dh_ref, x_ref, g_ref, res_ref, dx_ref, dxb_ref, dg_ref = refs
        else:
            dh_ref, x_ref, g_ref, dx_ref, dxb_ref, dg_ref = refs
        xv = x_ref[...]
        dhv = dh_ref[...].astype(F32)
        r = _rstd(xv)
        xn = xv * r
        dhg = dhv * g_ref[...]
        dx = r * (dhg - xn * jnp.mean(dhg * xn, axis=-1, keepdims=True))
        if has_res:
            dx = dx + res_ref[...]
        dx_ref[...] = dx
        dxb_ref[...] = dx.astype(BF16)
        part = jnp.sum(dhv * xn, axis=0, keepdims=True)

        @pl.when(pl.program_id(0) == 0)
        def _():
            dg_ref[...] = part

        @pl.when(pl.program_id(0) > 0)
        def _():
            dg_ref[...] += part

    row = pl.BlockSpec((tr, d), lambda i: (i, 0))
    vec = pl.BlockSpec((1, d), lambda i: (0, 0))
    return pl.pallas_call(
        body, name=name, grid=(s // tr,),
        in_specs=[row, row, vec] + ([row] if has_res else []),
        out_specs=[row, row, vec],
        out_shape=[jax.ShapeDtypeStruct((s, d), F32), jax.ShapeDtypeStruct((s, d), BF16), jax.ShapeDtypeStruct((1, d), F32)],
        compiler_params=_params(1),
    )(*([dh, x, g] + ([dres] if has_res else [])))


def _loss_head(x3, g, target):
    s, d = x3.shape
    tr = _row_tile(s, 256)

    def body(x_ref, g_ref, t_ref, dx_ref, dxb_ref, sq_ref, dg_ref):
        xv = x_ref[...]
        gv = g_ref[...]
        r = _rstd(xv)
        xn = xv * r
        err = xn * gv - t_ref[...]
        dy = err * (1.0 / d)
        dyg = dy * gv
        dx = r * (dyg - xn * jnp.mean(dyg * xn, axis=-1, keepdims=True))
        dx_ref[...] = dx
        dxb_ref[...] = dx.astype(BF16)
        sq = jnp.sum(jnp.sum(err * err, axis=1, keepdims=True), axis=0, keepdims=True)
        sq = jnp.broadcast_to(sq, (1, 128))
        part = jnp.sum(dy * xn, axis=0, keepdims=True)

        @pl.when(pl.program_id(0) == 0)
        def _():
            sq_ref[...] = sq
            dg_ref[...] = part

        @pl.when(pl.program_id(0) > 0)
        def _():
            sq_ref[...] += sq
            dg_ref[...] += part

    row = pl.BlockSpec((tr, d), lambda i: (i, 0))
    vec = pl.BlockSpec((1, d), lambda i: (0, 0))
    return pl.pallas_call(
        body, name="loss_head", grid=(s // tr,),
        in_specs=[row, vec, row],
        out_specs=[row, row, pl.BlockSpec((1, 128), lambda i: (0, 0)), vec],
        out_shape=[jax.ShapeDtypeStruct((s, d), F32), jax.ShapeDtypeStruct((s, d), BF16),
                   jax.ShapeDtypeStruct((1, 128), F32), jax.ShapeDtypeStruct((1, d), F32)],
        compiler_params=_params(1),
    )(x3, g, target)


def _rope_tables(s):
    inv = 1.0 / (ROPE_THETA ** (jnp.arange(0, HEAD_DIM, 2, dtype=F32) / HEAD_DIM))
    ang = jnp.arange(s, dtype=F32)[:, None] * inv[None, :]
    cos, sin = jnp.cos(ang), jnp.sin(ang)
    return jnp.concatenate([cos, cos], axis=1), jnp.concatenate([-sin, sin], axis=1)


def _swap_halves(t):
    return pltpu.roll(t, HEAD_DIM // 2, 1)


def _rope_fwd(z, cos_t, sin_t):
    s = z.shape[0]
    tr = _row_tile(s, 256)

    def body(zq_ref, zk_ref, zv_ref, c_ref, s_ref, q_ref, k_ref, v_ref):
        c, sn = c_ref[...], s_ref[...]
        for hd in range(N_Q_HEADS):
            cols = slice(hd * HEAD_DIM, (hd + 1) * HEAD_DIM)
            t = zq_ref[:, cols]
            q_ref[:, cols] = (t * c + _swap_halves(t) * sn).astype(BF16)
        for hd in range(N_KV_HEADS):
            cols = slice(hd * HEAD_DIM, (hd + 1) * HEAD_DIM)
            t = zk_ref[:, cols]
            k_ref[:, cols] = (t * c + _swap_halves(t) * sn).astype(BF16)
        v_ref[...] = zv_ref[...].astype(BF16)

    tab = pl.BlockSpec((tr, HEAD_DIM), lambda i: (i, 0))
    return pl.pallas_call(
        body, name="rope_fwd", grid=(s // tr,),
        in_specs=[pl.BlockSpec((tr, ATTN_WIDTH), lambda i: (i, Q_OFF // ATTN_WIDTH)),
                  pl.BlockSpec((tr, KV_WIDTH), lambda i: (i, K_OFF // KV_WIDTH)),
                  pl.BlockSpec((tr, KV_WIDTH), lambda i: (i, V_OFF // KV_WIDTH)), tab, tab],
        out_specs=[pl.BlockSpec((tr, ATTN_WIDTH), lambda i: (i, 0)), pl.BlockSpec((tr, KV_WIDTH), lambda i: (i, 0)),
                   pl.BlockSpec((tr, KV_WIDTH), lambda i: (i, 0))],
        out_shape=[jax.ShapeDtypeStruct((s, ATTN_WIDTH), BF16), jax.ShapeDtypeStruct((s, KV_WIDTH), BF16),
                   jax.ShapeDtypeStruct((s, KV_WIDTH), BF16)],
        compiler_params=_params(1),
    )(z, z, z, cos_t, sin_t)


def _rope_bwd(dq_rot, dk_rot, dv, cos_t, sin_t):
    s = dq_rot.shape[0]
    tr = _row_tile(s, 256)

    def body(dq_ref, dk_ref, dv_ref, c_ref, s_ref, oq_ref, ok_ref, ov_ref):
        c, sn = c_ref[...], s_ref[...]
        for hd in range(N_Q_HEADS):
            cols = slice(hd * HEAD_DIM, (hd + 1) * HEAD_DIM)
            t = dq_ref[:, cols]
            oq_ref[:, cols] = (t * c + _swap_halves(t * sn)).astype(BF16)
        for hd in range(N_KV_HEADS):
            cols = slice(hd * HEAD_DIM, (hd + 1) * HEAD_DIM)
            t = dk_ref[:, cols]
            ok_ref[:, cols] = (t * c + _swap_halves(t * sn)).astype(BF16)
        ov_ref[...] = dv_ref[...].astype(BF16)

    tab = pl.BlockSpec((tr, HEAD_DIM), lambda i: (i, 0))
    wide = pl.BlockSpec((tr, ATTN_WIDTH), lambda i: (i, 0))
    narrow = pl.BlockSpec((tr, KV_WIDTH), lambda i: (i, 0))
    return pl.pallas_call(
        body, name="rope_bwd", grid=(s // tr,),
        in_specs=[wide, narrow, narrow, tab, tab],
        out_specs=[wide, narrow, narrow],
        out_shape=[jax.ShapeDtypeStruct((s, ATTN_WIDTH), BF16), jax.ShapeDtypeStruct((s, KV_WIDTH), BF16),
                   jax.ShapeDtypeStruct((s, KV_WIDTH), BF16)],
        compiler_params=_params(1),
    )(dq_rot, dk_rot, dv, cos_t, sin_t)


def _swa_band(i, s):
    return pl.multiple_of(jnp.clip((i - 1) * BLOCK, 0, s - BAND), BLOCK)


def _swa_probs(q_ref, k_ref, sink_ref, kv, start, valid):
    cols = slice(kv * HEAD_DIM, (kv + 1) * HEAD_DIM)
    kb = k_ref[pl.ds(start, BAND), cols]
    heads = [kv * Q_GROUP + g for g in range(Q_GROUP)]
    qg = jnp.concatenate([q_ref[:, hd * HEAD_DIM:(hd + 1) * HEAD_DIM] for hd in heads], axis=0)
    sc = lax.dot_general(qg, kb, (((1,), (1,)), ((), ())), preferred_element_type=F32) * ATTN_SCALE
    sc = jnp.where(valid, sc, NEG_INF)
    sk = jnp.concatenate([jnp.full((BLOCK, 1), sink_ref[hd], F32) for hd in heads], axis=0)
    mx = jnp.maximum(jnp.max(sc, axis=1, keepdims=True), sk)
    e = jnp.exp(sc - mx)
    es = jnp.exp(sk - mx)
    inv = 1.0 / (jnp.sum(e, axis=1, keepdims=True) + es)
    return qg, kb, e * inv, es * inv


def _swa_valid(i, start):
    q_pos = i * BLOCK + lax.broadcasted_iota(jnp.int32, (BLOCK, 1), 0)
    q_pos = jnp.concatenate([q_pos] * Q_GROUP, axis=0)
    k_pos = start + lax.broadcasted_iota(jnp.int32, (1, BAND), 1)
    return jnp.abs(k_pos - q_pos) <= WINDOW


def _swa_fwd(q, k, v, sink):
    s = q.shape[0]
    assert s % BLOCK == 0 and s >= BAND

    def body(sink_ref, q_ref, k_ref, v_ref, o_ref):
        i = pl.program_id(0)
        start = _swa_band(i, s)
        valid = _swa_valid(i, start)
        for kv in range(N_KV_HEADS):
            _, _, p, _ = _swa_probs(q_ref, k_ref, sink_ref, kv, start, valid)
            vb = v_ref[pl.ds(start, BAND), kv * HEAD_DIM:(kv + 1) * HEAD_DIM]
            o = jnp.dot(p.astype(BF16), vb, preferred_element_type=F32)
            for g in range(Q_GROUP):
                hd = kv * Q_GROUP + g
                o_ref[:, hd * HEAD_DIM:(hd + 1) * HEAD_DIM] = o[g * BLOCK:(g + 1) * BLOCK].astype(BF16)

    whole = pl.BlockSpec((s, KV_WIDTH), lambda i: (0, 0))
    blk = pl.BlockSpec((BLOCK, ATTN_WIDTH), lambda i: (i, 0))
    return pl.pallas_call(
        body, name="swa_fwd", grid=(s // BLOCK,),
        in_specs=[pl.BlockSpec(memory_space=pltpu.SMEM), blk, whole, whole],
        out_specs=blk,
        out_shape=jax.ShapeDtypeStruct((s, ATTN_WIDTH), BF16),
        compiler_params=_params(1),
    )(sink, q, k, v)


def _swa_bwd(q, k, v, d_out, sink):
    s = q.shape[0]

    def body(sink_ref, q_ref, k_ref, v_ref, do_ref, dq_ref, dk_ref, dv_ref, dsink_ref):
        i = pl.program_id(0)

        @pl.when(i == 0)
        def _():
            dk_ref[...] = jnp.zeros_like(dk_ref)
            dv_ref[...] = jnp.zeros_like(dv_ref)
            dsink_ref[...] = jnp.zeros_like(dsink_ref)

        start = _swa_band(i, s)
        valid = _swa_valid(i, start)
        for kv in range(N_KV_HEADS):
            cols = slice(kv * HEAD_DIM, (kv + 1) * HEAD_DIM)
            qg, kb, p, p_sink = _swa_probs(q_ref, k_ref, sink_ref, kv, start, valid)
            vb = v_ref[pl.ds(start, BAND), cols]
            heads = [kv * Q_GROUP + g for g in range(Q_GROUP)]
            dog = jnp.concatenate([do_ref[:, hd * HEAD_DIM:(hd + 1) * HEAD_DIM] for hd in heads], axis=0)
            dp = lax.dot_general(dog, vb, (((1,), (1,)), ((), ())), preferred_element_type=F32)
            delta = jnp.sum(p * dp, axis=1, keepdims=True)
            ds = (p * (dp - delta) * ATTN_SCALE).astype(BF16)
            dqg = jnp.dot(ds, kb, preferred_element_type=F32)
            dk_ref[pl.ds(start, BAND), cols] += lax.dot_general(ds, qg, (((0,), (0,)), ((), ())), preferred_element_type=F32)
            dv_ref[pl.ds(start, BAND), cols] += lax.dot_general(p.astype(BF16), dog, (((0,), (0,)), ((), ())),
                                                                 preferred_element_type=F32)
            dsk = p_sink * delta
            for g, hd in enumerate(heads):
                dq_ref[:, hd * HEAD_DIM:(hd + 1) * HEAD_DIM] = dqg[g * BLOCK:(g + 1) * BLOCK]
                tot = jnp.sum(dsk[g * BLOCK:(g + 1) * BLOCK], axis=0, keepdims=True)
                dsink_ref[hd:hd + 1, :] -= jnp.broadcast_to(tot, (1, 128))

    whole = pl.BlockSpec((s, KV_WIDTH), lambda i: (0, 0))
    blk = pl.BlockSpec((BLOCK, ATTN_WIDTH), lambda i: (i, 0))
    return pl.pallas_call(
        body, name="swa_bwd", grid=(s // BLOCK,),
        in_specs=[pl.BlockSpec(memory_space=pltpu.SMEM), blk, whole, whole, blk],
        out_specs=[blk, whole, whole, pl.BlockSpec((N_Q_HEADS, 128), lambda i: (0, 0))],
        out_shape=[jax.ShapeDtypeStruct((s, ATTN_WIDTH), F32), jax.ShapeDtypeStruct((s, KV_WIDTH), F32),
                   jax.ShapeDtypeStruct((s, KV_WIDTH), F32), jax.ShapeDtypeStruct((N_Q_HEADS, 128), F32)],
        compiler_params=_params(1),
    )(sink, q, k, v, d_out)


CONV_CHUNK = 256


def _shift_rows(t, rows, down):
    n = t.shape[0]
    rolled = pltpu.roll(t, 1 if down else n - 1, 0)
    edge = 0 if down else n - 1
    return jnp.where(rows == edge, 0.0, rolled)


def _conv_specs(s):
    def z_spec(off):
        return pl.BlockSpec((s, CONV_CHUNK), lambda j, off=off: (0, off // CONV_CHUNK + j))
    chunk = pl.BlockSpec((s, CONV_CHUNK), lambda j: (0, j))
    w_spec = pl.BlockSpec((3, CONV_CHUNK), lambda j: (0, j))
    return z_spec(CU_OFF), z_spec(CB_OFF), z_spec(CC_OFF), chunk, w_spec


def _conv_fwd(z, conv_w):
    s = z.shape[0]
    cu_spec, cb_spec, cc_spec, chunk, w_spec = _conv_specs(s)

    def body(cu_ref, cb_ref, cc_ref, w_ref, o_ref):
        rows = lax.broadcasted_iota(jnp.int32, (s, 1), 0)
        t = cc_ref[...] * cu_ref[...]
        c3 = _shift_rows(t, rows, True) * w_ref[0:1, :] + t * w_ref[1:2, :] + _shift_rows(t, rows, False) * w_ref[2:3, :]
        o_ref[...] = (cb_ref[...] * c3).astype(BF16)

    return pl.pallas_call(
        body, name="conv_fwd", grid=(CONV_WIDTH // CONV_CHUNK,),
        in_specs=[cu_spec, cb_spec, cc_spec, w_spec],
        out_specs=chunk,
        out_shape=jax.ShapeDtypeStruct((s, CONV_WIDTH), BF16),
        compiler_params=_params(1),
    )(z, z, z, conv_w)


def _conv_bwd(z, conv_w, d_co):
    s = z.shape[0]
    cu_spec, cb_spec, cc_spec, chunk, w_spec = _conv_specs(s)

    def body(cu_ref, cb_ref, cc_ref, w_ref, d_ref, dcu_ref, dcb_ref, dcc_ref, dw_ref):
        rows = lax.broadcasted_iota(jnp.int32, (s, 1), 0)
        cu, cc = cu_ref[...], cc_ref[...]
        t = cc * cu
        t_dn, t_up = _shift_rows(t, rows, True), _shift_rows(t, rows, False)
        c3 = t_dn * w_ref[0:1, :] + t * w_ref[1:2, :] + t_up * w_ref[2:3, :]
        d = d_ref[...]
        dcb_ref[...] = (d * c3).astype(BF16)
        dc3 = d * cb_ref[...]
        dw_ref[0:1, :] = jnp.sum(dc3 * t_dn, axis=0, keepdims=True)
        dw_ref[1:2, :] = jnp.sum(dc3 * t, axis=0, keepdims=True)
        dw_ref[2:3, :] = jnp.sum(dc3 * t_up, axis=0, keepdims=True)
        dt = _shift_rows(dc3, rows, False) * w_ref[0:1, :] + dc3 * w_ref[1:2, :] + _shift_rows(dc3, rows, True) * w_ref[2:3, :]
        dcc_ref[...] = (dt * cu).astype(BF16)
        dcu_ref[...] = (dt * cc).astype(BF16)

    return pl.pallas_call(
        body, name="conv_bwd", grid=(CONV_WIDTH // CONV_CHUNK,),
        in_specs=[cu_spec, cb_spec, cc_spec, w_spec, chunk],
        out_specs=[chunk, chunk, chunk, w_spec],
        out_shape=[jax.ShapeDtypeStruct((s, CONV_WIDTH), BF16)] * 3 + [jax.ShapeDtypeStruct((3, CONV_WIDTH), F32)],
        compiler_params=_params(1),
    )(z, z, z, conv_w, d_co)


GATE_CHUNK = 512


def _gate_specs(s, d, tr):
    n_chunks = d // GATE_CHUNK
    za = pl.BlockSpec((tr, GATE_CHUNK), lambda j, i: (i, GL_OFF // GATE_CHUNK + j))
    zc = pl.BlockSpec((tr, GATE_CHUNK), lambda j, i: (i, GL_OFF // GATE_CHUNK + n_chunks + j))
    ba = pl.BlockSpec((1, GATE_CHUNK), lambda j, i: (0, j))
    bc = pl.BlockSpec((1, GATE_CHUNK), lambda j, i: (0, n_chunks + j))
    tile = pl.BlockSpec((tr, GATE_CHUNK), lambda j, i: (i, j))
    return za, zc, ba, bc, tile


def _gate_fwd(z, b_gate, ya, yc):
    s, d = ya.shape
    tr = _row_tile(s, 512)
    za, zc, ba, bc, tile = _gate_specs(s, d, tr)

    def body(za_ref, zc_ref, ba_ref, bc_ref, ya_ref, yc_ref, o_ref):
        ga = jax.nn.sigmoid(za_ref[...] + ba_ref[...])
        gc = jax.nn.sigmoid(zc_ref[...] + bc_ref[...])
        o_ref[...] = (ga * ya_ref[...] + gc * yc_ref[...]).astype(BF16)

    return pl.pallas_call(
        body, name="gate_fwd", grid=(d // GATE_CHUNK, s // tr),
        in_specs=[za, zc, ba, bc, tile, tile],
        out_specs=tile,
        out_shape=jax.ShapeDtypeStruct((s, d), BF16),
        compiler_params=_params(2),
    )(z, z, b_gate, b_gate, ya, yc)


def _gate_bwd(z, b_gate, ya, yc, dmix):
    s, d = ya.shape
    tr = _row_tile(s, 512)
    za, zc, ba, bc, tile = _gate_specs(s, d, tr)
    vec = pl.BlockSpec((1, GATE_CHUNK), lambda j, i: (0, j))

    def body(za_ref, zc_ref, ba_ref, bc_ref, ya_ref, yc_ref, dm_ref, dya_ref, dyc_ref, dla_ref, dlc_ref, dba_ref, dbc_ref):
        ga = jax.nn.sigmoid(za_ref[...] + ba_ref[...])
        gc = jax.nn.sigmoid(zc_ref[...] + bc_ref[...])
        dm = dm_ref[...]
        dya_ref[...] = (dm * ga).astype(BF16)
        dyc_ref[...] = (dm * gc).astype(BF16)
        dla = dm * ya_ref[...] * ga * (1.0 - ga)
        dlc = dm * yc_ref[...] * gc * (1.0 - gc)
        dla_ref[...] = dla.astype(BF16)
        dlc_ref[...] = dlc.astype(BF16)
        pa = jnp.sum(dla, axis=0, keepdims=True)
        pc = jnp.sum(dlc, axis=0, keepdims=True)

        @pl.when(pl.program_id(1) == 0)
        def _():
            dba_ref[...] = pa
            dbc_ref[...] = pc

        @pl.when(pl.program_id(1) > 0)
        def _():
            dba_ref[...] += pa
            dbc_ref[...] += pc

    big = jax.ShapeDtypeStruct((s, d), BF16)
    small = jax.ShapeDtypeStruct((1, d), F32)
    return pl.pallas_call(
        body, name="gate_bwd", grid=(d // GATE_CHUNK, s // tr),
        in_specs=[za, zc, ba, bc, tile, tile, tile],
        out_specs=[tile, tile, tile, tile, vec, vec],
        out_shape=[big, big, big, big, small, small],
        compiler_params=_params(2),
    )(z, z, b_gate, b_gate, ya, yc, dmix)


def _cross_probs(q_ref, kv_ref, hd):
    cols = slice(hd * HEAD_DIM, (hd + 1) * HEAD_DIM)
    qh = q_ref[:, cols]
    kh = kv_ref[:, cols]
    sc = lax.dot_general(qh, kh, (((1,), (1,)), ((), ())), preferred_element_type=F32) * ATTN_SCALE
    e = jnp.exp(sc - jnp.max(sc, axis=1, keepdims=True))
    return qh, kh, e * (1.0 / jnp.sum(e, axis=1, keepdims=True))


def _cross_fwd(qc, kvc):
    s = qc.shape[0]
    n_mem = kvc.shape[0]
    tq = _row_tile(s, 256)

    def body(q_ref, kv_ref, o_ref):
        for hd in range(MEM_HEADS):
            _, _, p = _cross_probs(q_ref, kv_ref, hd)
            vh = kv_ref[:, MEM_WIDTH + hd * HEAD_DIM:MEM_WIDTH + (hd + 1) * HEAD_DIM]
            o_ref[:, hd * HEAD_DIM:(hd + 1) * HEAD_DIM] = jnp.dot(p.astype(BF16), vh, preferred_element_type=F32).astype(BF16)

    return pl.pallas_call(
        body, name="cross_fwd", grid=(s // tq,),
        in_specs=[pl.BlockSpec((tq, MEM_WIDTH), lambda i: (i, 0)), pl.BlockSpec((n_mem, 2 * MEM_WIDTH), lambda i: (0, 0))],
        out_specs=pl.BlockSpec((tq, MEM_WIDTH), lambda i: (i, 0)),
        out_shape=jax.ShapeDtypeStruct((s, MEM_WIDTH), BF16),
        compiler_params=_params(1),
    )(qc, kvc)


def _cross_bwd(qc, kvc, d_out):
    s = qc.shape[0]
    n_mem = kvc.shape[0]
    tq = _row_tile(s, 256)

    def body(q_ref, kv_ref, do_ref, dq_ref, dkv_ref):
        @pl.when(pl.program_id(0) == 0)
        def _():
            dkv_ref[...] = jnp.zeros_like(dkv_ref)

        for hd in range(MEM_HEADS):
            cols = slice(hd * HEAD_DIM, (hd + 1) * HEAD_DIM)
            vcols = slice(MEM_WIDTH + hd * HEAD_DIM, MEM_WIDTH + (hd + 1) * HEAD_DIM)
            qh, kh, p = _cross_probs(q_ref, kv_ref, hd)
            doh = do_ref[:, cols]
            dp = lax.dot_general(doh, kv_ref[:, vcols], (((1,), (1,)), ((), ())), preferred_element_type=F32)
            ds = (p * (dp - jnp.sum(p * dp, axis=1, keepdims=True)) * ATTN_SCALE).astype(BF16)
            dq_ref[:, cols] = jnp.dot(ds, kh, preferred_element_type=F32).astype(BF16)
            dkv_ref[:, cols] += lax.dot_general(ds, qh, (((0,), (0,)), ((), ())), preferred_element_type=F32)
            dkv_ref[:, vcols] += lax.dot_general(p.astype(BF16), doh, (((0,), (0,)), ((), ())), preferred_element_type=F32)

    qspec = pl.BlockSpec((tq, MEM_WIDTH), lambda i: (i, 0))
    kvspec = pl.BlockSpec((n_mem, 2 * MEM_WIDTH), lambda i: (0, 0))
    return pl.pallas_call(
        body, name="cross_bwd", grid=(s // tq,),
        in_specs=[qspec, kvspec, qspec],
        out_specs=[qspec, kvspec],
        out_shape=[jax.ShapeDtypeStruct((s, MEM_WIDTH), BF16), jax.ShapeDtypeStruct((n_mem, 2 * MEM_WIDTH), F32)],
        compiler_params=_params(1),
    )(qc, kvc, d_out)


def _swiglu_fwd(up, gate):
    return up, (gate * jax.nn.sigmoid(gate)) * up


def _swiglu_bwd(d_act, gate, up):
    sg = jax.nn.sigmoid(gate)
    silu = gate * sg
    return d_act * up * (sg * (1.0 + gate * (1.0 - sg))), d_act * silu


GATHER_GROUPS = {"in": ("w_in", "conv_w"), "mix": ("w_attn_out", "w_conv_out", "w_o"), "cross": ("w_cq", "w_ckv", "w_co"),
                 "gate": ("w_gate",), "up": ("w_up",), "down": ("w_down",)}


def _local_step(xs, mems, target, small, fetch):
    s, d = xs.shape
    w4 = {}
    cos_t, sin_t = _rope_tables(s)

    h = _rmsnorm(xs, small["g_mix"], "norm_mix")
    w4.update(fetch.get("in", h))
    conv4 = w4["conv_w"]
    conv_w = conv4[:, :3, :].transpose(1, 0, 2).reshape(3, N_CHIPS * conv4.shape[2])
    c_in = w4["w_in"].shape[2]
    z = _matmul(h, w4["w_in"], mode="nn", tm=512, tn=c_in, tk=d, out_dtypes=[F32], name="in_proj", b_blocks=N_CHIPS)
    fetch.begin("mix", z)
    q_rot, k_rot, v_b = _rope_fwd(z, cos_t, sin_t)
    attn = _swa_fwd(q_rot, k_rot, v_b, small["sink"])
    co = _conv_fwd(z, conv_w)
    w4.update(fetch.get("mix", co))
    fetch.begin("cross", co)
    w_o = w4["w_o"].reshape(-1, w4["w_o"].shape[-1])
    c_d = w4["w_attn_out"].shape[2]
    ya = _matmul(attn, w4["w_attn_out"], mode="nn", tm=1024, tn=c_d, tk=ATTN_WIDTH, out_dtypes=[F32], name="attn_out_proj",
                 b_blocks=N_CHIPS)
    yc = _matmul(co, w4["w_conv_out"], mode="nn", tm=1024, tn=c_d, tk=CONV_WIDTH, out_dtypes=[F32], name="conv_out_proj",
                 b_blocks=N_CHIPS)
    mix = _gate_fwd(z, small["b_gate"], ya, yc)
    x1 = _matmul(mix, w_o, mode="nn", tm=512, tn=1024, tk=d, out_dtypes=[F32], name="mix_out_proj", extras=[xs],
                 epilogue=_add_residual)
    w4.update(fetch.get("cross", x1))
    fetch.begin("gate", x1)
    w_cq = w4["w_cq"].reshape(-1, w4["w_cq"].shape[-1])
    w_ckv = w4["w_ckv"].reshape(-1, w4["w_ckv"].shape[-1])
    hc = _rmsnorm(x1, small["g_cross"], "norm_cross")
    memn = _rmsnorm(mems, small["g_mem"], "norm_mem")
    qc = _matmul(hc, w_cq, mode="nn", tm=1024, tn=MEM_WIDTH, tk=d, out_dtypes=[BF16], name="cross_q_proj")
    kvc = _matmul(memn, w_ckv, mode="nn", tm=256, tn=2 * MEM_WIDTH, tk=d, out_dtypes=[BF16], name="cross_kv_proj")
    oc = _cross_fwd(qc, kvc)
    x2 = _matmul(oc, w4["w_co"], mode="nn", tm=1024, tn=c_d, tk=MEM_WIDTH, out_dtypes=[F32], name="cross_out_proj",
                 extras=[x1], epilogue=_add_residual, b_blocks=N_CHIPS)
    hf = _rmsnorm(x2, small["g_ffn"], "norm_ffn")
    w4.update(fetch.get("gate", hf))
    c_ff = w4["w_gate"].shape[2]
    gate = _matmul(hf, w4["w_gate"], mode="nn", tm=512, tn=c_ff, tk=d, out_dtypes=[F32], name="ffn_gate_proj", b_blocks=N_CHIPS)
    w4.update(fetch.get("up", gate))
    up, act = _matmul(hf, w4["w_up"], mode="nn", tm=512, tn=c_ff, tk=d, out_dtypes=[F32, BF16], name="ffn_up_proj",
                      extras=[gate], epilogue=_swiglu_fwd, b_blocks=N_CHIPS)
    w4.update(fetch.get("down", act))
    w_down = w4["w_down"].reshape(-1, w4["w_down"].shape[-1])
    x3 = _matmul(act, w_down, mode="nn", tm=512, tn=1024, tk=c_ff, out_dtypes=[F32], name="ffn_down_proj", extras=[x2],
                 epilogue=_add_residual)
    dx3, dx3b, sq, dg_final = _loss_head(x3, small["g_final"], target)

    grads = {}
    da, du = _matmul(dx3b, w_down, mode="nt", tm=512, tn=c_ff, tk=d, out_dtypes=[BF16, BF16], name="ffn_down_bwd",
                     extras=[gate, up], epilogue=_swiglu_bwd)
    grads["w_down"] = _matmul(act, dx3b, mode="tn", tm=c_ff, tn=1024, tk=s, out_dtypes=[BF16], name="ffn_down_wgrad")
    grads["w_gate"] = _matmul(hf, da, mode="tn", tm=512, tn=c_ff, tk=s, out_dtypes=[BF16], name="ffn_gate_wgrad",
                              out_blocks=N_CHIPS)
    grads["w_up"] = _matmul(hf, du, mode="tn", tm=512, tn=c_ff, tk=s, out_dtypes=[BF16], name="ffn_up_wgrad", out_blocks=N_CHIPS)
    dhf = _matmul(da, w4["w_gate"], mode="nt", tm=512, tn=1024, tk=c_ff, out_dtypes=[F32], name="ffn_gate_bwd", b_blocks=N_CHIPS)
    dhf = _matmul(du, w4["w_up"], mode="nt", tm=512, tn=1024, tk=c_ff, out_dtypes=[F32], name="ffn_up_bwd", extras=[dhf],
                  epilogue=_add_residual, b_blocks=N_CHIPS)
    dx2, dx2b, dg_ffn = _rmsnorm_bwd(dhf, x2, small["g_ffn"], dx3, "norm_ffn_bwd")

    d_oc = _matmul(dx2b, w4["w_co"], mode="nt", tm=1024, tn=MEM_WIDTH, tk=c_d, out_dtypes=[BF16], name="cross_out_bwd",
                   b_blocks=N_CHIPS)
    grads["w_co"] = _matmul(oc, dx2b, mode="tn", tm=MEM_WIDTH, tn=c_d, tk=s, out_dtypes=[BF16], name="cross_out_wgrad",
                            out_blocks=N_CHIPS)
    dqc, dkvc = _cross_bwd(qc, kvc, d_oc)
    grads["w_cq"] = _matmul(hc, dqc, mode="tn", tm=1024, tn=MEM_WIDTH, tk=s, out_dtypes=[BF16], name="cross_q_wgrad")
    dhc = _matmul(dqc, w_cq, mode="nt", tm=1024, tn=1024, tk=MEM_WIDTH, out_dtypes=[F32], name="cross_q_bwd")
    grads["w_ckv"] = _matmul(memn, dkvc, mode="tn", tm=1024, tn=2 * MEM_WIDTH, tk=mems.shape[0], out_dtypes=[BF16],
                             name="cross_kv_wgrad")
    dmemn = _matmul(dkvc, w_ckv, mode="nt", tm=256, tn=1024, tk=2 * MEM_WIDTH, out_dtypes=[F32], name="cross_kv_bwd")
    _, _, dg_mem = _rmsnorm_bwd(dmemn, mems, small["g_mem"], None, "norm_mem_bwd")
    dx1, dx1b, dg_cross = _rmsnorm_bwd(dhc, x1, small["g_cross"], dx2, "norm_cross_bwd")

    dmix = _matmul(dx1b, w_o, mode="nt", tm=512, tn=1024, tk=d, out_dtypes=[F32], name="mix_out_bwd")
    grads["w_o"] = _matmul(mix, dx1b, mode="tn", tm=1024, tn=1024, tk=s, out_dtypes=[BF16], name="mix_out_wgrad")
    dya, dyc, dgl_a, dgl_c, db_a, db_c = _gate_bwd(z, small["b_gate"], ya, yc, dmix)
    d_attn = _matmul(dya, w4["w_attn_out"], mode="nt", tm=1024, tn=ATTN_WIDTH, tk=c_d, out_dtypes=[BF16], name="attn_out_bwd",
                     b_blocks=N_CHIPS)
    grads["w_attn_out"] = _matmul(attn, dya, mode="tn", tm=ATTN_WIDTH, tn=c_d, tk=s, out_dtypes=[BF16], name="attn_out_wgrad",
                                  out_blocks=N_CHIPS)
    d_co = _matmul(dyc, w4["w_conv_out"], mode="nt", tm=1024, tn=CONV_WIDTH, tk=c_d, out_dtypes=[F32], name="conv_out_bwd",
                   b_blocks=N_CHIPS)
    grads["w_conv_out"] = _matmul(co, dyc, mode="tn", tm=CONV_WIDTH, tn=c_d, tk=s, out_dtypes=[BF16], name="conv_out_wgrad",
                                  out_blocks=N_CHIPS)
    dcu, dcb, dcc, d_conv_w = _conv_bwd(z, conv_w, d_co)
    dq_rot, dk_rot, dv, dsink = _swa_bwd(q_rot, k_rot, v_b, d_attn, small["sink"])
    dq, dk, dvb = _rope_bwd(dq_rot, dk_rot, dv, cos_t, sin_t)
    dz = jnp.concatenate([dq, dk, dvb, dcu, dcb, dcc, dgl_a, dgl_c], axis=1)
    grads["w_in"] = _matmul(h, dz, mode="tn", tm=512, tn=c_in, tk=s, out_dtypes=[BF16], name="in_proj_wgrad", out_blocks=N_CHIPS)
    dh = _matmul(dz, w4["w_in"], mode="nt", tm=512, tn=1024, tk=c_in, out_dtypes=[F32], name="in_proj_bwd", b_blocks=N_CHIPS)
    grad_x, _, dg_mix = _rmsnorm_bwd(dh, xs, small["g_mix"], dx1, "norm_mix_bwd")

    small_grads = {
        "g_mix": dg_mix, "sink": dsink[:, 0], "b_gate": jnp.concatenate([db_a, db_c], axis=1), "g_cross": dg_cross,
        "g_mem": dg_mem, "g_ffn": dg_ffn, "g_final": dg_final, "conv_w": d_conv_w,
    }
    return sq, grad_x, small_grads, grads


def _pair_sum(g4, ra, core, name):
    nb, rs, cs = g4.shape
    rh = rs // 2
    tr = _row_tile(rh, 256)
    per = rh // tr

    def body(c_ref, g_ref, r_ref, o_ref):
        o_ref[...] = (g_ref[...].astype(F32) + r_ref[...].astype(F32)).astype(BF16)

    plain = pl.BlockSpec((None, tr, cs), lambda j, i, c: (j, i, 0))
    return pl.pallas_call(
        body, name=name,
        grid_spec=pltpu.PrefetchScalarGridSpec(
            num_scalar_prefetch=1, grid=(nb, per),
            in_specs=[pl.BlockSpec((None, tr, cs), lambda j, i, c: (j, c[0] * per + i, 0)), plain],
            out_specs=plain),
        out_shape=jax.ShapeDtypeStruct((nb, rh, cs), BF16),
        compiler_params=_params(2),
    )(core, g4, ra)


def _quad_sum(parts, rc, place, name):
    _, rh, cs = parts.shape
    tr = _row_tile(rh, 256)
    per = rh // tr

    def body(p_ref, own_ref, r_ref, o_ref):
        acc = own_ref[...].astype(F32)
        for j in range(rc.shape[0]):
            acc = acc + r_ref[j].astype(F32)
        o_ref[...] = acc

    return pl.pallas_call(
        body, name=name,
        grid_spec=pltpu.PrefetchScalarGridSpec(
            num_scalar_prefetch=1, grid=(per,),
            in_specs=[pl.BlockSpec((None, tr, cs), lambda i, p: (p[0], i, 0)),
                      pl.BlockSpec((rc.shape[0], tr, cs), lambda i, p: (0, i, 0))],
            out_specs=pl.BlockSpec((tr, cs), lambda i, p: (p[1] * per + i, 0))),
        out_shape=jax.ShapeDtypeStruct((2 * rh, cs), F32),
        compiler_params=_params(1),
    )(place, parts, rc)


def _cast_to_slot(w, place, dtype, name):
    rows, cols = w.shape
    tr = _row_tile(rows, 256)

    def body(p_ref, w_ref, o_ref):
        o_ref[...] = w_ref[...].astype(dtype)

    return pl.pallas_call(
        body, name=name,
        grid_spec=pltpu.PrefetchScalarGridSpec(
            num_scalar_prefetch=1, grid=(rows // tr,),
            in_specs=[pl.BlockSpec((tr, cols), lambda i, p: (i, 0))],
            out_specs=pl.BlockSpec((None, tr, cols), lambda i, p: (p[0], i, 0))),
        out_shape=jax.ShapeDtypeStruct((N_CHIPS, rows, cols), dtype),
        compiler_params=_params(1),
    )(place, w)


def _adamw(w, g, m, v, name):
    rows, cols = w.shape
    tr = _row_tile(rows, 256)

    def body(w_ref, g_ref, m_ref, v_ref, go_ref, d_ref, nm_ref, nv_ref):
        gv = g_ref[...]
        go_ref[...] = gv
        nm = ADAM_B1 * m_ref[...] + (1.0 - ADAM_B1) * gv
        nv = ADAM_B2 * v_ref[...] + (1.0 - ADAM_B2) * (gv * gv)
        m_hat = nm / ADAM_C1
        v_hat = nv / ADAM_C2
        d_ref[...] = -ADAM_LR * (m_hat / (jnp.sqrt(v_hat) + ADAM_EPS) + ADAM_WD * w_ref[...])
        nm_ref[...] = nm
        nv_ref[...] = nv

    tile = pl.BlockSpec((tr, cols), lambda i: (i, 0))
    shape = jax.ShapeDtypeStruct((rows, cols), F32)
    return pl.pallas_call(
        body, name=name, grid=(rows // tr,),
        in_specs=[tile] * 4, out_specs=[tile] * 4, out_shape=[shape] * 4,
        compiler_params=_params(1),
    )(w, g, m, v)


def _mesh_pos():
    return lax.axis_index("x"), lax.axis_index("y"), lax.axis_index("c")


def _other_chips(x, y):
    return [(1 - x, y), (x, 1 - y), (1 - x, 1 - y)]


def _half_rows(ref, which):
    rh = ref.shape[-2] // 2
    return ref.at[pl.ds(which * rh, rh), :]


def _remote(src, dst, send_sems, recv_sems, sem, to):
    return pltpu.make_async_remote_copy(src_ref=src, dst_ref=dst, send_sem=send_sems.at[sem], recv_sem=recv_sems.at[sem],
                                        device_id=to, device_id_type=MESH)


HBM = pl.BlockSpec(memory_space=pltpu.HBM)
SEM = pl.BlockSpec(memory_space=pltpu.SEMAPHORE)
DATAFLOW_EFFECT = pltpu.SideEffectType.DATAFLOW_SIDE_EFFECTING


def _in_hbm(arrays):
    return [pltpu.with_memory_space_constraint(a, pltpu.HBM) for a in arrays]


def _hbm_like(arrays):
    return [pltpu.HBM(a.shape, a.dtype) for a in arrays]


def _gather_start(bufs, groups):
    n, ng = len(bufs), len(groups)

    def body(*refs):
        ins = refs[:n]
        send, recv = refs[n:n + ng], refs[n + ng:n + 2 * ng]
        x, y, c = _mesh_pos()
        me = 2 * x + y
        for g, members in enumerate(groups):
            for i, w in enumerate(members):
                mine = _half_rows(ins[w].at[me], c)
                for k, (px, py) in enumerate(_other_chips(x, y)):
                    _remote(mine, mine, send[g], recv[g], 3 * i + k, (px, py, c)).start()

    sems = [pltpu.SemaphoreType.DMA((3 * len(m),)) for m in groups]
    outs = pl.pallas_call(
        body, name="gather_start",
        in_specs=[HBM] * n, out_specs=[SEM] * (2 * ng) + [HBM] * n,
        out_shape=sems + sems + _hbm_like(bufs),
        input_output_aliases={i: 2 * ng + i for i in range(n)},
        compiler_params=pltpu.CompilerParams(has_side_effects=DATAFLOW_EFFECT),
    )(*_in_hbm(bufs))
    return outs[:ng], outs[ng:2 * ng], outs[2 * ng:]


def _gather_pass(bufs, send, recv, after, name):
    m = len(bufs)

    def body(*refs):
        ins, send_in, recv_in = refs[:m], refs[m], refs[m + 1]
        send_out, recv_out = refs[m + 3], refs[m + 4]
        x, y, c = _mesh_pos()
        for i in range(m):
            for k, (px, py) in enumerate(_other_chips(x, y)):
                landed = _half_rows(ins[i].at[2 * px + py], c)
                came = _remote(landed, landed, send_in, recv_in, 3 * i + k, (px, py, c))
                came.wait_recv()
                came.wait_send()
                _remote(landed, landed, send_out, recv_out, 3 * i + k, (x, y, 1 - c)).start()

    sems = [pltpu.SemaphoreType.DMA((3 * m,))] * 2
    outs = pl.pallas_call(
        body, name=name,
        in_specs=[HBM] * m + [SEM, SEM, ANY], out_specs=[SEM, SEM] + [HBM] * m,
        out_shape=sems + _hbm_like(bufs),
        input_output_aliases={i: 2 + i for i in range(m)},
        compiler_params=pltpu.CompilerParams(has_side_effects=DATAFLOW_EFFECT),
    )(*_in_hbm(bufs), send, recv, after)
    return outs[0], outs[1], outs[2:]


def _gather_done(bufs, send, recv, after, name):
    m = len(bufs)

    def body(*refs):
        ins, send_in, recv_in = refs[:m], refs[m], refs[m + 1]
        x, y, c = _mesh_pos()
        for i in range(m):
            for k, (px, py) in enumerate(_other_chips(x, y)):
                passed = _half_rows(ins[i].at[2 * px + py], 1 - c)
                came = _remote(passed, passed, send_in, recv_in, 3 * i + k, (x, y, 1 - c))
                came.wait_send()
                came.wait_recv()

    return pl.pallas_call(
        body, name=name,
        in_specs=[HBM] * m + [SEM, SEM, ANY], out_specs=[HBM] * m,
        out_shape=_hbm_like(bufs),
        input_output_aliases={i: i for i in range(m)},
        compiler_params=pltpu.CompilerParams(has_side_effects=DATAFLOW_EFFECT),
    )(*_in_hbm(bufs), send, recv, after)


class _Gather:
    def __init__(self, slotted, groups):
        self.groups = groups
        names = [n for members in groups.values() for n in members]
        index = {n: i for i, n in enumerate(names)}
        send, recv, bufs = _gather_start([slotted[n] for n in names], [[index[n] for n in m] for m in groups.values()])
        self.landing = {g: (send[j], recv[j], [bufs[index[n]] for n in m]) for j, (g, m) in enumerate(groups.items())}
        self.passing = {}

    def begin(self, group, after):
        send, recv, bufs = self.landing.pop(group)
        self.passing[group] = _gather_pass(bufs, send, recv, after, "gather_pass_" + group)

    def get(self, group, after):
        if group not in self.passing:
            self.begin(group, after)
        send, recv, bufs = self.passing.pop(group)
        return dict(zip(self.groups[group], _gather_done(bufs, send, recv, after, "gather_done_" + group)))


def _sibling_exchange_halves(g4s):
    n = len(g4s)

    def body(*refs):
        src, dst = refs[:n], refs[n:2 * n]
        send_sems, recv_sems = refs[2 * n:]
        x, y, c = _mesh_pos()
        copies = []
        for w in range(n):
            rh = src[w].shape[1] // 2
            cp = _remote(src[w].at[:, pl.ds((1 - c) * rh, rh), :], dst[w], send_sems, recv_sems, w, (x, y, 1 - c))
            cp.start()
            copies.append(cp)
        for cp in copies:
            cp.wait_recv()
        for cp in copies:
            cp.wait_send()

    return pl.pallas_call(
        body, name="grads_sibling_exchange",
        in_specs=[ANY] * n, out_specs=[ANY] * n,
        out_shape=[jax.ShapeDtypeStruct((g.shape[0], g.shape[1] // 2, g.shape[2]), g.dtype) for g in g4s],
        scratch_shapes=[pltpu.SemaphoreType.DMA((n,)), pltpu.SemaphoreType.DMA((n,))],
    )(*g4s)


def _chip_exchange(parts):
    n = len(parts)

    def body(*refs):
        src, dst = refs[:n], refs[n:2 * n]
        send_sems, recv_sems = refs[2 * n:]
        x, y, c = _mesh_pos()
        chips = _other_chips(x, y)
        sends = []
        for w in range(n):
            for k, (px, py) in enumerate(chips):
                cp = _remote(src[w].at[2 * px + py], dst[w].at[k], send_sems, recv_sems, 3 * w + k, (px, py, c))
                cp.start()
                sends.append(cp)
        for cp in sends:
            cp.wait_recv()
        for cp in sends:
            cp.wait_send()

    return pl.pallas_call(
        body, name="grads_chip_exchange",
        in_specs=[ANY] * n, out_specs=[ANY] * n,
        out_shape=[jax.ShapeDtypeStruct((N_CHIPS - 1,) + p.shape[1:], p.dtype) for p in parts],
        scratch_shapes=[pltpu.SemaphoreType.DMA((3 * n,)), pltpu.SemaphoreType.DMA((3 * n,))],
    )(*parts)


def _sibling_join_halves(wholes):
    n = len(wholes)

    def body(*refs):
        buf = refs[n:2 * n]
        send_sems, recv_sems = refs[2 * n:]
        x, y, c = _mesh_pos()
        sends = []
        for w in range(n):
            mine = _half_rows(buf[w], c)
            cp = _remote(mine, mine, send_sems, recv_sems, w, (x, y, 1 - c))
            cp.start()
            sends.append(cp)
        for w in range(n):
            other = _half_rows(buf[w], 1 - c)
            _remote(other, other, send_sems, recv_sems, w, (x, y, 1 - c)).wait_recv()
        for cp in sends:
            cp.wait_send()

    return pl.pallas_call(
        body, name="grads_sibling_join",
        in_specs=[ANY] * n, out_specs=[ANY] * n,
        out_shape=[jax.ShapeDtypeStruct(h.shape, h.dtype) for h in wholes],
        input_output_aliases={i: i for i in range(n)},
        scratch_shapes=[pltpu.SemaphoreType.DMA((n,)), pltpu.SemaphoreType.DMA((n,))],
    )(*wholes)


N_DEV = 8


def _all_reduce_small(v):
    def body(v_ref, o_ref, slots, send_sems, recv_sems):
        x, y, c = _mesh_pos()
        me = 4 * x + 2 * y + c
        slots[me] = v_ref[...]
        peers = []
        for r in range(1, N_DEV):
            fx, fy, fc = (r >> 2) & 1, (r >> 1) & 1, r & 1
            peers.append((x + fx - 2 * x * fx, y + fy - 2 * y * fy, c + fc - 2 * c * fc))
        sends = []
        for r, peer in enumerate(peers):
            cp = _remote(v_ref, slots.at[me], send_sems, recv_sems, r, peer)
            cp.start()
            sends.append(cp)
        for r, (px, py, pc) in enumerate(peers):
            landed = slots.at[4 * px + 2 * py + pc]
            _remote(landed, landed, send_sems, recv_sems, r, (px, py, pc)).wait_recv()
        for cp in sends:
            cp.wait_send()
        acc = slots[0]
        for i in range(1, N_DEV):
            acc = acc + slots[i]
        o_ref[...] = acc

    vm = pl.BlockSpec(memory_space=pltpu.VMEM)
    return pl.pallas_call(
        body, name="small_grads_all_reduce",
        in_specs=[vm], out_specs=vm,
        out_shape=jax.ShapeDtypeStruct(v.shape, v.dtype),
        scratch_shapes=[pltpu.VMEM((N_DEV,) + v.shape, v.dtype), pltpu.SemaphoreType.DMA((N_DEV - 1,)),
                        pltpu.SemaphoreType.DMA((N_DEV - 1,))],
    )(v)


MATRICES = ("w_in", "w_attn_out", "w_conv_out", "w_o", "w_cq", "w_ckv", "w_co", "w_gate", "w_up", "w_down")
VECTORS = ("g_mix", "b_gate", "g_cross", "g_mem", "g_ffn", "g_final", "conv_w", "sink")
WEIGHT_ORDER = ("g_mix", "w_in", "sink", "conv_w", "b_gate", "w_attn_out", "w_conv_out", "w_o", "g_cross", "g_mem", "w_cq",
                "w_ckv", "w_co", "g_ffn", "w_gate", "w_up", "w_down", "g_final")
CONV_PAD_ROWS = 16
SMALL_ROWS = 8


def _pack(pieces):
    flat = jnp.concatenate([p.reshape(-1) for p in pieces])
    lane_group = SMALL_ROWS * 128
    total = -(-flat.shape[0] // lane_group) * lane_group
    flat = jnp.pad(flat, (0, total - flat.shape[0]))
    return flat.reshape(SMALL_ROWS, total // SMALL_ROWS), [p.size for p in pieces]


def _unpack(packed, pieces):
    flat = packed.reshape(-1)
    out, off = [], 0
    for p in pieces:
        out.append(flat[off:off + p.size].reshape(p.shape))
        off += p.size
    return out


def kernel(x, mem, g_mix, w_in, sink, conv_w, b_gate, w_attn_out, w_conv_out, w_o, g_cross, g_mem, w_cq, w_ckv, w_co, g_ffn, w_gate, w_up, w_down, g_final, loss_target, m_g_mix, m_w_in, m_sink, m_conv_w, m_b_gate, m_w_attn_out, m_w_conv_out, m_w_o, m_g_cross, m_g_mem, m_w_cq, m_w_ckv, m_w_co, m_g_ffn, m_w_gate, m_w_up, m_w_down, m_g_final, v_g_mix, v_w_in, v_sink, v_conv_w, v_b_gate, v_w_attn_out, v_w_conv_out, v_w_o, v_g_cross, v_g_mem, v_w_cq, v_w_ckv, v_w_co, v_g_ffn, v_w_gate, v_w_up, v_w_down, v_g_final):
    given = dict(g_mix=g_mix, w_in=w_in, sink=sink, conv_w=conv_w, b_gate=b_gate, w_attn_out=w_attn_out, w_conv_out=w_conv_out,
                 w_o=w_o, g_cross=g_cross, g_mem=g_mem, w_cq=w_cq, w_ckv=w_ckv, w_co=w_co, g_ffn=g_ffn, w_gate=w_gate, w_up=w_up,
                 w_down=w_down, g_final=g_final)
    mom_m = dict(g_mix=m_g_mix, w_in=m_w_in, sink=m_sink, conv_w=m_conv_w, b_gate=m_b_gate, w_attn_out=m_w_attn_out,
                 w_conv_out=m_w_conv_out, w_o=m_w_o, g_cross=m_g_cross, g_mem=m_g_mem, w_cq=m_w_cq, w_ckv=m_w_ckv, w_co=m_w_co,
                 g_ffn=m_g_ffn, w_gate=m_w_gate, w_up=m_w_up, w_down=m_w_down, g_final=m_g_final)
    mom_v = dict(g_mix=v_g_mix, w_in=v_w_in, sink=v_sink, conv_w=v_conv_w, b_gate=v_b_gate, w_attn_out=v_w_attn_out,
                 w_conv_out=v_w_conv_out, w_o=v_w_o, g_cross=v_g_cross, g_mem=v_g_mem, w_cq=v_w_cq, w_ckv=v_w_ckv, w_co=v_w_co,
                 g_ffn=v_g_ffn, w_gate=v_w_gate, w_up=v_w_up, w_down=v_w_down, g_final=v_g_final)
    xs, mems, target = x[0], mem[0], loss_target[0]
    d_model = xs.shape[1]
    chip = 2 * lax.axis_index("x") + lax.axis_index("y")
    core = jnp.reshape(lax.axis_index("c"), (1,)).astype(jnp.int32)
    place = jnp.stack([chip, lax.axis_index("c")]).astype(jnp.int32)

    shards = {n: given[n][0] for n in MATRICES}
    conv_cols = conv_w.shape[2]
    conv_pad = jnp.pad(conv_w[0], ((0, CONV_PAD_ROWS - conv_w.shape[1]), (0, 0)))
    slotted = {n: _cast_to_slot(shards[n], place, BF16, "to_slot_" + n) for n in MATRICES}
    slotted["conv_w"] = _cast_to_slot(conv_pad, place, F32, "to_slot_conv_w")
    fetch = _Gather(slotted, GATHER_GROUPS)
    small = {n: given[n] for n in ("g_mix", "b_gate", "g_cross", "g_mem", "g_ffn")}
    small["g_final"] = g_final[None]
    small["sink"] = sink[0]

    sq, grad_x, small_grads, grads = _local_step(xs, mems, target, small, fetch)

    g4 = [grads[n].reshape((N_CHIPS, -1, grads[n].shape[-1])) if grads[n].ndim == 2 else grads[n] for n in MATRICES]
    from_sibling = _sibling_exchange_halves(g4)
    chip_parts = [_pair_sum(g, r, core, "pair_sum_" + n) for g, r, n in zip(g4, from_sibling, MATRICES)]
    from_chips = _chip_exchange(chip_parts)
    halves = [_quad_sum(p, r, place, "quad_sum_" + n) for p, r, n in zip(chip_parts, from_chips, MATRICES)]
    reduced = dict(zip(MATRICES, _sibling_join_halves(halves)))

    loss_part = 0.5 * sq[0:1, 0:1] / d_model
    pieces = [small_grads[n] for n in VECTORS] + [loss_part]
    packed, _ = _pack(pieces)
    summed = _unpack(_all_reduce_small(packed), pieces)
    loss = summed[-1][0, 0]
    small_sum = dict(zip(VECTORS, summed[:-1]))
    small_sum["conv_w"] = lax.dynamic_slice_in_dim(small_sum["conv_w"], chip * conv_cols, conv_cols, axis=1)

    grad_out, delta, new_m, new_v = {}, {}, {}, {}
    for n in MATRICES:
        g, d, nm, nv = _adamw(shards[n], reduced[n], mom_m[n][0], mom_v[n][0], "adamw_" + n)
        grad_out[n], delta[n], new_m[n], new_v[n] = g[None], d[None], nm[None], nv[None]
    like = [given[n] for n in VECTORS]
    pw, _ = _pack(like)
    pg, _ = _pack([small_sum[n] for n in VECTORS])
    pm, _ = _pack([mom_m[n] for n in VECTORS])
    pv, _ = _pack([mom_v[n] for n in VECTORS])
    _, pd, pnm, pnv = _adamw(pw, pg, pm, pv, "adamw_small")
    for n, g, d, nm, nv in zip(VECTORS, [small_sum[n] for n in VECTORS], _unpack(pd, like), _unpack(pnm, like), _unpack(pnv, like)):
        grad_out[n] = g.reshape(given[n].shape)
        delta[n], new_m[n], new_v[n] = d, nm, nv

    return (loss, grad_x[None], *[grad_out[n] for n in WEIGHT_ORDER], *[delta[n] for n in WEIGHT_ORDER],
            *[new_m[n] for n in WEIGHT_ORDER], *[new_v[n] for n in WEIGHT_ORDER])
```

```python
import functools

import jax
import jax.numpy as jnp
from jax import lax
from jax.experimental import pallas as pl
from jax.experimental.pallas import tpu as pltpu

F32 = jnp.float32
BF16 = jnp.bfloat16
MESH = pl.DeviceIdType.MESH
ANY = pl.BlockSpec(memory_space=pl.ANY)

VMEM_LIMIT_BYTES = 56 * 1024 * 1024

N_CHIPS = 4
HEAD_DIM = 128
N_Q_HEADS = 8
N_KV_HEADS = 2
Q_GROUP = N_Q_HEADS // N_KV_HEADS
ATTN_WIDTH = N_Q_HEADS * HEAD_DIM
KV_WIDTH = N_KV_HEADS * HEAD_DIM
WINDOW = 128
BLOCK = 128
BAND = 3 * BLOCK
ROPE_THETA = 10000.0
CONV_WIDTH = 1024
MEM_HEADS = 4
MEM_WIDTH = MEM_HEADS * HEAD_DIM
RMS_EPS = 1e-6
NEG_INF = -1e30
ATTN_SCALE = HEAD_DIM ** -0.5

Q_OFF, K_OFF, V_OFF, CU_OFF, CB_OFF, CC_OFF, GL_OFF = 0, 1024, 1280, 1536, 2560, 3584, 4608

ADAM_LR = 0.001
ADAM_B1 = 0.9
ADAM_B2 = 0.999
ADAM_EPS = 1e-08
ADAM_WD = 0.01
ADAM_STEP = 10
ADAM_C1 = 1.0 - ADAM_B1 ** ADAM_STEP
ADAM_C2 = 1.0 - ADAM_B2 ** ADAM_STEP


def _params(n_grid_axes):
    return pltpu.CompilerParams(dimension_semantics=("arbitrary",) * n_grid_axes, vmem_limit_bytes=VMEM_LIMIT_BYTES)


def _row_tile(rows, want):
    t = min(want, rows)
    while rows % t:
        t //= 2
    return t


def _matmul(a, b, *, mode, tm, tn, tk, out_dtypes, name, extras=(), epilogue=None, b_blocks=1, out_blocks=1, after=None):
    if mode == "tn":
        kdim, m = a.shape
    else:
        m, kdim = a.shape
    if b_blocks > 1:
        nb, brows, bcols = b.shape
        assert nb == b_blocks
        if mode == "nn":
            n = bcols * nb
            assert brows == kdim
        else:
            assert mode == "nt" and bcols * nb == kdim
            n = brows
    else:
        n = b.shape[0] if mode == "nt" else b.shape[1]
    tm, tn, tk = min(tm, m), min(tn, n), min(tk, kdim)
    assert m % tm == 0 and n % tn == 0 and kdim % tk == 0, (name, m, n, kdim, tm, tn, tk)
    nk = kdim // tk
    n_extra, n_out = len(extras), len(out_dtypes)
    n_after = 0 if after is None else 1

    if mode == "tn":
        a_spec = pl.BlockSpec((tk, tm), lambda j, i, k: (k, i))
        dims = (((0,), (0,)), ((), ()))
    else:
        a_spec = pl.BlockSpec((tm, tk), lambda j, i, k: (i, k))
        dims = (((1,), (0,)), ((), ())) if mode == "nn" else (((1,), (1,)), ((), ()))

    if b_blocks > 1 and mode == "nn":
        per = b.shape[2] // tn
        assert b.shape[2] % tn == 0
        b_spec = pl.BlockSpec((None, tk, tn), lambda j, i, k: (j // per, k, j % per))
    elif b_blocks > 1:
        per = b.shape[2] // tk
        assert b.shape[2] % tk == 0
        b_spec = pl.BlockSpec((None, tn, tk), lambda j, i, k: (k // per, j, k % per))
    elif mode == "nt":
        b_spec = pl.BlockSpec((tn, tk), lambda j, i, k: (j, k))
    else:
        b_spec = pl.BlockSpec((tk, tn), lambda j, i, k: (k, j))

    tile_spec = pl.BlockSpec((tm, tn), lambda j, i, k: (i, j))
    if out_blocks > 1:
        ncols = n // out_blocks
        assert ncols % tn == 0
        oper = ncols // tn
        out_spec = pl.BlockSpec((None, tm, tn), lambda j, i, k: (j // oper, i, j % oper))
        out_shape = [jax.ShapeDtypeStruct((out_blocks, m, ncols), dt) for dt in out_dtypes]
    else:
        out_spec = tile_spec
        out_shape = [jax.ShapeDtypeStruct((m, n), dt) for dt in out_dtypes]

    def body(a_ref, b_ref, *rest):
        extra_refs = rest[:n_extra]
        out_refs = rest[n_extra + n_after:n_extra + n_after + n_out]

        def finish(acc):
            if epilogue is None:
                tiles = (acc,)
            else:
                tiles = epilogue(acc, *[r[...] for r in extra_refs])
            for o_ref, t in zip(out_refs, tiles, strict=True):
                o_ref[...] = t.astype(o_ref.dtype)

        part = lax.dot_general(a_ref[...].astype(BF16), b_ref[...].astype(BF16), dims, preferred_element_type=F32)
        if nk == 1:
            finish(part)
        else:
            acc_ref = rest[-1]
            k = pl.program_id(2)

            @pl.when(k == 0)
            def _():
                acc_ref[...] = part

            @pl.when(k > 0)
            def _():
                acc_ref[...] += part

            @pl.when(k == nk - 1)
            def _():
                finish(acc_ref[...])

    outs = pl.pallas_call(
        body,
        name=name,
        grid=(n // tn, m // tm, nk),
        in_specs=[a_spec, b_spec] + [tile_spec] * n_extra + [ANY] * n_after,
        out_specs=[out_spec] * n_out,
        out_shape=out_shape,
        scratch_shapes=[pltpu.VMEM((tm, tn), F32)] if nk > 1 else [],
        compiler_params=_params(3),
    )(a, b, *extras, *([] if after is None else [after]))
    return outs[0] if n_out == 1 else outs


def _add_residual(acc, res):
    return (acc + res,)


def _rstd(x):
    return lax.rsqrt(jnp.mean(x * x, axis=-1, keepdims=True) + RMS_EPS)


def _rmsnorm(x, g, name):
    s, d = x.shape
    tr = _row_tile(s, 256)

    def body(x_ref, g_ref, o_ref):
        xv = x_ref[...]
        o_ref[...] = (xv * _rstd(xv) * g_ref[...]).astype(BF16)

    return pl.pallas_call(
        body, name=name, grid=(s // tr,),
        in_specs=[pl.BlockSpec((tr, d), lambda i: (i, 0)), pl.BlockSpec((1, d), lambda i: (0, 0))],
        out_specs=pl.BlockSpec((tr, d), lambda i: (i, 0)),
        out_shape=jax.ShapeDtypeStruct((s, d), BF16),
        compiler_params=_params(1),
    )(x, g)


def _rmsnorm_bwd(dh, x, g, dres, name):
    s, d = x.shape
    tr = _row_tile(s, 256)
    has_res = dres is not None

    def body(*refs):
        if has_res:
            dh_ref, x_ref, g_ref, res_ref, dx_ref, dxb_ref, dg_ref = refs
        else:
            dh_ref, x_ref, g_ref, dx_ref, dxb_ref, dg_ref = refs
        xv = x_ref[...]
        dhv = dh_ref[...].astype(F32)
        r = _rstd(xv)
        xn = xv * r
        dhg = dhv * g_ref[...]
        dx = r * (dhg - xn * jnp.mean(dhg * xn, axis=-1, keepdims=True))
        if has_res:
            dx = dx + res_ref[...]
        dx_ref[...] = dx
        dxb_ref[...] = dx.astype(BF16)
        part = jnp.sum(dhv * xn, axis=0, keepdims=True)

        @pl.when(pl.program_id(0) == 0)
        def _():
            dg_ref[...] = part

        @pl.when(pl.program_id(0) > 0)
        def _():
            dg_ref[...] += part

    row = pl.BlockSpec((tr, d), lambda i: (i, 0))
    vec = pl.BlockSpec((1, d), lambda i: (0, 0))
    return pl.pallas_call(
        body, name=name, grid=(s // tr,),
        in_specs=[row, row, vec] + ([row] if has_res else []),
        out_specs=[row, row, vec],
        out_shape=[jax.ShapeDtypeStruct((s, d), F32), jax.ShapeDtypeStruct((s, d), BF16), jax.ShapeDtypeStruct((1, d), F32)],
        compiler_params=_params(1),
    )(*([dh, x, g] + ([dres] if has_res else [])))


def _loss_head(x3, g, target):
    s, d = x3.shape
    tr = _row_tile(s, 256)

    def body(x_ref, g_ref, t_ref, dx_ref, dxb_ref, sq_ref, dg_ref):
        xv = x_ref[...]
        gv = g_ref[...]
        r = _rstd(xv)
        xn = xv * r
        err = xn * gv - t_ref[...]
        dy = err * (1.0 / d)
        dyg = dy * gv
        dx = r * (dyg - xn * jnp.mean(dyg * xn, axis=-1, keepdims=True))
        dx_ref[...] = dx
        dxb_ref[...] = dx.astype(BF16)
        sq = jnp.sum(jnp.sum(err * err, axis=1, keepdims=True), axis=0, keepdims=True)
        sq = jnp.broadcast_to(sq, (1, 128))
        part = jnp.sum(dy * xn, axis=0, keepdims=True)

        @pl.when(pl.program_id(0) == 0)
        def _():
            sq_ref[...] = sq
            dg_ref[...] = part

        @pl.when(pl.program_id(0) > 0)
        def _():
            sq_ref[...] += sq
            dg_ref[...] += part

    row = pl.BlockSpec((tr, d), lambda i: (i, 0))
    vec = pl.BlockSpec((1, d), lambda i: (0, 0))
    return pl.pallas_call(
        body, name="loss_head", grid=(s // tr,),
        in_specs=[row, vec, row],
        out_specs=[row, row, pl.BlockSpec((1, 128), lambda i: (0, 0)), vec],
        out_shape=[jax.ShapeDtypeStruct((s, d), F32), jax.ShapeDtypeStruct((s, d), BF16),
                   jax.ShapeDtypeStruct((1, 128), F32), jax.ShapeDtypeStruct((1, d), F32)],
        compiler_params=_params(1),
    )(x3, g, target)


def _rope_tables(s):
    inv = 1.0 / (ROPE_THETA ** (jnp.arange(0, HEAD_DIM, 2, dtype=F32) / HEAD_DIM))
    ang = jnp.arange(s, dtype=F32)[:, None] * inv[None, :]
    cos, sin = jnp.cos(ang), jnp.sin(ang)
    return jnp.concatenate([cos, cos], axis=1), jnp.concatenate([-sin, sin], axis=1)


def _swap_halves(t):
    return pltpu.roll(t, HEAD_DIM // 2, 1)


def _rope_fwd(z, cos_t, sin_t):
    s = z.shape[0]
    tr = _row_tile(s, 256)

    def body(zq_ref, zk_ref, zv_ref, c_ref, s_ref, q_ref, k_ref, v_ref):
        c, sn = c_ref[...], s_ref[...]
        for hd in range(N_Q_HEADS):
            cols = slice(hd * HEAD_DIM, (hd + 1) * HEAD_DIM)
            t = zq_ref[:, cols]
            q_ref[:, cols] = (t * c + _swap_halves(t) * sn).astype(BF16)
        for hd in range(N_KV_HEADS):
            cols = slice(hd * HEAD_DIM, (hd + 1) * HEAD_DIM)
            t = zk_ref[:, cols]
            k_ref[:, cols] = (t * c + _swap_halves(t) * sn).astype(BF16)
        v_ref[...] = zv_ref[...].astype(BF16)

    tab = pl.BlockSpec((tr, HEAD_DIM), lambda i: (i, 0))
    return pl.pallas_call(
        body, name="rope_fwd", grid=(s // tr,),
        in_specs=[pl.BlockSpec((tr, ATTN_WIDTH), lambda i: (i, Q_OFF // ATTN_WIDTH)),
                  pl.BlockSpec((tr, KV_WIDTH), lambda i: (i, K_OFF // KV_WIDTH)),
                  pl.BlockSpec((tr, KV_WIDTH), lambda i: (i, V_OFF // KV_WIDTH)), tab, tab],
        out_specs=[pl.BlockSpec((tr, ATTN_WIDTH), lambda i: (i, 0)), pl.BlockSpec((tr, KV_WIDTH), lambda i: (i, 0)),
                   pl.BlockSpec((tr, KV_WIDTH), lambda i: (i, 0))],
        out_shape=[jax.ShapeDtypeStruct((s, ATTN_WIDTH), BF16), jax.ShapeDtypeStruct((s, KV_WIDTH), BF16),
                   jax.ShapeDtypeStruct((s, KV_WIDTH), BF16)],
        compiler_params=_params(1),
    )(z, z, z, cos_t, sin_t)


def _rope_bwd(dq_rot, dk_rot, dv, cos_t, sin_t):
    s = dq_rot.shape[0]
    tr = _row_tile(s, 256)

    def body(dq_ref, dk_ref, dv_ref, c_ref, s_ref, oq_ref, ok_ref, ov_ref):
        c, sn = c_ref[...], s_ref[...]
        for hd in range(N_Q_HEADS):
            cols = slice(hd * HEAD_DIM, (hd + 1) * HEAD_DIM)
            t = dq_ref[:, cols]
            oq_ref[:, cols] = (t * c + _swap_halves(t * sn)).astype(BF16)
        for hd in range(N_KV_HEADS):
            cols = slice(hd * HEAD_DIM, (hd + 1) * HEAD_DIM)
            t = dk_ref[:, cols]
            ok_ref[:, cols] = (t * c + _swap_halves(t * sn)).astype(BF16)
        ov_ref[...] = dv_ref[...].astype(BF16)

    tab = pl.BlockSpec((tr, HEAD_DIM), lambda i: (i, 0))
    wide = pl.BlockSpec((tr, ATTN_WIDTH), lambda i: (i, 0))
    narrow = pl.BlockSpec((tr, KV_WIDTH), lambda i: (i, 0))
    return pl.pallas_call(
        body, name="rope_bwd", grid=(s // tr,),
        in_specs=[wide, narrow, narrow, tab, tab],
        out_specs=[wide, narrow, narrow],
        out_shape=[jax.ShapeDtypeStruct((s, ATTN_WIDTH), BF16), jax.ShapeDtypeStruct((s, KV_WIDTH), BF16),
                   jax.ShapeDtypeStruct((s, KV_WIDTH), BF16)],
        compiler_params=_params(1),
    )(dq_rot, dk_rot, dv, cos_t, sin_t)


def _swa_band(i, s):
    return pl.multiple_of(jnp.clip((i - 1) * BLOCK, 0, s - BAND), BLOCK)


def _swa_probs(q_ref, k_ref, sink_ref, kv, start, valid):
    cols = slice(kv * HEAD_DIM, (kv + 1) * HEAD_DIM)
    kb = k_ref[pl.ds(start, BAND), cols]
    heads = [kv * Q_GROUP + g for g in range(Q_GROUP)]
    qg = jnp.concatenate([q_ref[:, hd * HEAD_DIM:(hd + 1) * HEAD_DIM] for hd in heads], axis=0)
    sc = lax.dot_general(qg, kb, (((1,), (1,)), ((), ())), preferred_element_type=F32) * ATTN_SCALE
    sc = jnp.where(valid, sc, NEG_INF)
    sk = jnp.concatenate([jnp.full((BLOCK, 1), sink_ref[hd], F32) for hd in heads], axis=0)
    mx = jnp.maximum(jnp.max(sc, axis=1, keepdims=True), sk)
    e = jnp.exp(sc - mx)
    es = jnp.exp(sk - mx)
    inv = 1.0 / (jnp.sum(e, axis=1, keepdims=True) + es)
    return qg, kb, e * inv, es * inv


def _swa_valid(i, start):
    q_pos = i * BLOCK + lax.broadcasted_iota(jnp.int32, (BLOCK, 1), 0)
    q_pos = jnp.concatenate([q_pos] * Q_GROUP, axis=0)
    k_pos = start + lax.broadcasted_iota(jnp.int32, (1, BAND), 1)
    return jnp.abs(k_pos - q_pos) <= WINDOW


def _swa_fwd(q, k, v, sink):
    s = q.shape[0]
    assert s % BLOCK == 0 and s >= BAND

    def body(sink_ref, q_ref, k_ref, v_ref, o_ref):
        i = pl.program_id(0)
        start = _swa_band(i, s)
        valid = _swa_valid(i, start)
        for kv in range(N_KV_HEADS):
            _, _, p, _ = _swa_probs(q_ref, k_ref, sink_ref, kv, start, valid)
            vb = v_ref[pl.ds(start, BAND), kv * HEAD_DIM:(kv + 1) * HEAD_DIM]
            o = jnp.dot(p.astype(BF16), vb, preferred_element_type=F32)
            for g in range(Q_GROUP):
                hd = kv * Q_GROUP + g
                o_ref[:, hd * HEAD_DIM:(hd + 1) * HEAD_DIM] = o[g * BLOCK:(g + 1) * BLOCK].astype(BF16)

    whole = pl.BlockSpec((s, KV_WIDTH), lambda i: (0, 0))
    blk = pl.BlockSpec((BLOCK, ATTN_WIDTH), lambda i: (i, 0))
    return pl.pallas_call(
        body, name="swa_fwd", grid=(s // BLOCK,),
        in_specs=[pl.BlockSpec(memory_space=pltpu.SMEM), blk, whole, whole],
        out_specs=blk,
        out_shape=jax.ShapeDtypeStruct((s, ATTN_WIDTH), BF16),
        compiler_params=_params(1),
    )(sink, q, k, v)


def _swa_bwd(q, k, v, d_out, sink):
    s = q.shape[0]

    def body(sink_ref, q_ref, k_ref, v_ref, do_ref, dq_ref, dk_ref, dv_ref, dsink_ref):
        i = pl.program_id(0)

        @pl.when(i == 0)
        def _():
            dk_ref[...] = jnp.zeros_like(dk_ref)
            dv_ref[...] = jnp.zeros_like(dv_ref)
            dsink_ref[...] = jnp.zeros_like(dsink_ref)

        start = _swa_band(i, s)
        valid = _swa_valid(i, start)
        for kv in range(N_KV_HEADS):
            cols = slice(kv * HEAD_DIM, (kv + 1) * HEAD_DIM)
            qg, kb, p, p_sink = _swa_probs(q_ref, k_ref, sink_ref, kv, start, valid)
            vb = v_ref[pl.ds(start, BAND), cols]
            heads = [kv * Q_GROUP + g for g in range(Q_GROUP)]
            dog = jnp.concatenate([do_ref[:, hd * HEAD_DIM:(hd + 1) * HEAD_DIM] for hd in heads], axis=0)
            dp = lax.dot_general(dog, vb, (((1,), (1,)), ((), ())), preferred_element_type=F32)
            delta = jnp.sum(p * dp, axis=1, keepdims=True)
            ds = (p * (dp - delta) * ATTN_SCALE).astype(BF16)
            dqg = jnp.dot(ds, kb, preferred_element_type=F32)
            dk_ref[pl.ds(start, BAND), cols] += lax.dot_general(ds, qg, (((0,), (0,)), ((), ())), preferred_element_type=F32)
            dv_ref[pl.ds(start, BAND), cols] += lax.dot_general(p.astype(BF16), dog, (((0,), (0,)), ((), ())),
                                                                 preferred_element_type=F32)
            dsk = p_sink * delta
            for g, hd in enumerate(heads):
                dq_ref[:, hd * HEAD_DIM:(hd + 1) * HEAD_DIM] = dqg[g * BLOCK:(g + 1) * BLOCK]
                tot = jnp.sum(dsk[g * BLOCK:(g + 1) * BLOCK], axis=0, keepdims=True)
                dsink_ref[hd:hd + 1, :] -= jnp.broadcast_to(tot, (1, 128))

    whole = pl.BlockSpec((s, KV_WIDTH), lambda i: (0, 0))
    blk = pl.BlockSpec((BLOCK, ATTN_WIDTH), lambda i: (i, 0))
    return pl.pallas_call(
        body, name="swa_bwd", grid=(s // BLOCK,),
        in_specs=[pl.BlockSpec(memory_space=pltpu.SMEM), blk, whole, whole, blk],
        out_specs=[blk, whole, whole, pl.BlockSpec((N_Q_HEADS, 128), lambda i: (0, 0))],
        out_shape=[jax.ShapeDtypeStruct((s, ATTN_WIDTH), F32), jax.ShapeDtypeStruct((s, KV_WIDTH), F32),
                   jax.ShapeDtypeStruct((s, KV_WIDTH), F32), jax.ShapeDtypeStruct((N_Q_HEADS, 128), F32)],
        compiler_params=_params(1),
    )(sink, q, k, v, d_out)


CONV_CHUNK = 256


def _shift_rows(t, rows, down):
    n = t.shape[0]
    rolled = pltpu.roll(t, 1 if down else n - 1, 0)
    edge = 0 if down else n - 1
    return jnp.where(rows == edge, 0.0, rolled)


def _conv_specs(s):
    def z_spec(off):
        return pl.BlockSpec((s, CONV_CHUNK), lambda j, off=off: (0, off // CONV_CHUNK + j))
    chunk = pl.BlockSpec((s, CONV_CHUNK), lambda j: (0, j))
    w_spec = pl.BlockSpec((3, CONV_CHUNK), lambda j: (0, j))
    return z_spec(CU_OFF), z_spec(CB_OFF), z_spec(CC_OFF), chunk, w_spec


def _conv_fwd(z, conv_w):
    s = z.shape[0]
    cu_spec, cb_spec, cc_spec, chunk, w_spec = _conv_specs(s)

    def body(cu_ref, cb_ref, cc_ref, w_ref, o_ref):
        rows = lax.broadcasted_iota(jnp.int32, (s, 1), 0)
        t = cc_ref[...] * cu_ref[...]
        c3 = _shift_rows(t, rows, True) * w_ref[0:1, :] + t * w_ref[1:2, :] + _shift_rows(t, rows, False) * w_ref[2:3, :]
        o_ref[...] = (cb_ref[...] * c3).astype(BF16)

    return pl.pallas_call(
        body, name="conv_fwd", grid=(CONV_WIDTH // CONV_CHUNK,),
        in_specs=[cu_spec, cb_spec, cc_spec, w_spec],
        out_specs=chunk,
        out_shape=jax.ShapeDtypeStruct((s, CONV_WIDTH), BF16),
        compiler_params=_params(1),
    )(z, z, z, conv_w)


def _conv_bwd(z, conv_w, d_co):
    s = z.shape[0]
    cu_spec, cb_spec, cc_spec, chunk, w_spec = _conv_specs(s)

    def body(cu_ref, cb_ref, cc_ref, w_ref, d_ref, dcu_ref, dcb_ref, dcc_ref, dw_ref):
        rows = lax.broadcasted_iota(jnp.int32, (s, 1), 0)
        cu, cc = cu_ref[...], cc_ref[...]
        t = cc * cu
        t_dn, t_up = _shift_rows(t, rows, True), _shift_rows(t, rows, False)
        c3 = t_dn * w_ref[0:1, :] + t * w_ref[1:2, :] + t_up * w_ref[2:3, :]
        d = d_ref[...]
        dcb_ref[...] = (d * c3).astype(BF16)
        dc3 = d * cb_ref[...]
        dw_ref[0:1, :] = jnp.sum(dc3 * t_dn, axis=0, keepdims=True)
        dw_ref[1:2, :] = jnp.sum(dc3 * t, axis=0, keepdims=True)
        dw_ref[2:3, :] = jnp.sum(dc3 * t_up, axis=0, keepdims=True)
        dt = _shift_rows(dc3, rows, False) * w_ref[0:1, :] + dc3 * w_ref[1:2, :] + _shift_rows(dc3, rows, True) * w_ref[2:3, :]
        dcc_ref[...] = (dt * cu).astype(BF16)
        dcu_ref[...] = (dt * cc).astype(BF16)

    return pl.pallas_call(
        body, name="conv_bwd", grid=(CONV_WIDTH // CONV_CHUNK,),
        in_specs=[cu_spec, cb_spec, cc_spec, w_spec, chunk],
        out_specs=[chunk, chunk, chunk, w_spec],
        out_shape=[jax.ShapeDtypeStruct((s, CONV_WIDTH), BF16)] * 3 + [jax.ShapeDtypeStruct((3, CONV_WIDTH), F32)],
        compiler_params=_params(1),
    )(z, z, z, conv_w, d_co)


GATE_CHUNK = 512


def _gate_specs(s, d, tr):
    n_chunks = d // GATE_CHUNK
    za = pl.BlockSpec((tr, GATE_CHUNK), lambda j, i: (i, GL_OFF // GATE_CHUNK + j))
    zc = pl.BlockSpec((tr, GATE_CHUNK), lambda j, i: (i, GL_OFF // GATE_CHUNK + n_chunks + j))
    ba = pl.BlockSpec((1, GATE_CHUNK), lambda j, i: (0, j))
    bc = pl.BlockSpec((1, GATE_CHUNK), lambda j, i: (0, n_chunks + j))
    tile = pl.BlockSpec((tr, GATE_CHUNK), lambda j, i: (i, j))
    return za, zc, ba, bc, tile


def _gate_fwd(z, b_gate, ya, yc):
    s, d = ya.shape
    tr = _row_tile(s, 512)
    za, zc, ba, bc, tile = _gate_specs(s, d, tr)

    def body(za_ref, zc_ref, ba_ref, bc_ref, ya_ref, yc_ref, o_ref):
        ga = jax.nn.sigmoid(za_ref[...] + ba_ref[...])
        gc = jax.nn.sigmoid(zc_ref[...] + bc_ref[...])
        o_ref[...] = (ga * ya_ref[...] + gc * yc_ref[...]).astype(BF16)

    return pl.pallas_call(
        body, name="gate_fwd", grid=(d // GATE_CHUNK, s // tr),
        in_specs=[za, zc, ba, bc, tile, tile],
        out_specs=tile,
        out_shape=jax.ShapeDtypeStruct((s, d), BF16),
        compiler_params=_params(2),
    )(z, z, b_gate, b_gate, ya, yc)


def _gate_bwd(z, b_gate, ya, yc, dmix):
    s, d = ya.shape
    tr = _row_tile(s, 512)
    za, zc, ba, bc, tile = _gate_specs(s, d, tr)
    vec = pl.BlockSpec((1, GATE_CHUNK), lambda j, i: (0, j))

    def body(za_ref, zc_ref, ba_ref, bc_ref, ya_ref, yc_ref, dm_ref, dya_ref, dyc_ref, dla_ref, dlc_ref, dba_ref, dbc_ref):
        ga = jax.nn.sigmoid(za_ref[...] + ba_ref[...])
        gc = jax.nn.sigmoid(zc_ref[...] + bc_ref[...])
        dm = dm_ref[...]
        dya_ref[...] = (dm * ga).astype(BF16)
        dyc_ref[...] = (dm * gc).astype(BF16)
        dla = dm * ya_ref[...] * ga * (1.0 - ga)
        dlc = dm * yc_ref[...] * gc * (1.0 - gc)
        dla_ref[...] = dla.astype(BF16)
        dlc_ref[...] = dlc.astype(BF16)
        pa = jnp.sum(dla, axis=0, keepdims=True)
        pc = jnp.sum(dlc, axis=0, keepdims=True)

        @pl.when(pl.program_id(1) == 0)
        def _():
            dba_ref[...] = pa
            dbc_ref[...] = pc

        @pl.when(pl.program_id(1) > 0)
        def _():
            dba_ref[...] += pa
            dbc_ref[...] += pc

    big = jax.ShapeDtypeStruct((s, d), BF16)
    small = jax.ShapeDtypeStruct((1, d), F32)
    return pl.pallas_call(
        body, name="gate_bwd", grid=(d // GATE_CHUNK, s // tr),
        in_specs=[za, zc, ba, bc, tile, tile, tile],
        out_specs=[tile, tile, tile, tile, vec, vec],
        out_shape=[big, big, big, big, small, small],
        compiler_params=_params(2),
    )(z, z, b_gate, b_gate, ya, yc, dmix)


def _cross_probs(q_ref, kv_ref, hd):
    cols = slice(hd * HEAD_DIM, (hd + 1) * HEAD_DIM)
    qh = q_ref[:, cols]
    kh = kv_ref[:, cols]
    sc = lax.dot_general(qh, kh, (((1,), (1,)), ((), ())), preferred_element_type=F32) * ATTN_SCALE
    e = jnp.exp(sc - jnp.max(sc, axis=1, keepdims=True))
    return qh, kh, e * (1.0 / jnp.sum(e, axis=1, keepdims=True))


def _cross_fwd(qc, kvc):
    s = qc.shape[0]
    n_mem = kvc.shape[0]
    tq = _row_tile(s, 256)

    def body(q_ref, kv_ref, o_ref):
        for hd in range(MEM_HEADS):
            _, _, p = _cross_probs(q_ref, kv_ref, hd)
            vh = kv_ref[:, MEM_WIDTH + hd * HEAD_DIM:MEM_WIDTH + (hd + 1) * HEAD_DIM]
            o_ref[:, hd * HEAD_DIM:(hd + 1) * HEAD_DIM] = jnp.dot(p.astype(BF16), vh, preferred_element_type=F32).astype(BF16)

    return pl.pallas_call(
        body, name="cross_fwd", grid=(s // tq,),
        in_specs=[pl.BlockSpec((tq, MEM_WIDTH), lambda i: (i, 0)), pl.BlockSpec((n_mem, 2 * MEM_WIDTH), lambda i: (0, 0))],
        out_specs=pl.BlockSpec((tq, MEM_WIDTH), lambda i: (i, 0)),
        out_shape=jax.ShapeDtypeStruct((s, MEM_WIDTH), BF16),
        compiler_params=_params(1),
    )(qc, kvc)


def _cross_bwd(qc, kvc, d_out):
    s = qc.shape[0]
    n_mem = kvc.shape[0]
    tq = _row_tile(s, 256)

    def body(q_ref, kv_ref, do_ref, dq_ref, dkv_ref):
        @pl.when(pl.program_id(0) == 0)
        def _():
            dkv_ref[...] = jnp.zeros_like(dkv_ref)

        for hd in range(MEM_HEADS):
            cols = slice(hd * HEAD_DIM, (hd + 1) * HEAD_DIM)
            vcols = slice(MEM_WIDTH + hd * HEAD_DIM, MEM_WIDTH + (hd + 1) * HEAD_DIM)
            qh, kh, p = _cross_probs(q_ref, kv_ref, hd)
            doh = do_ref[:, cols]
            dp = lax.dot_general(doh, kv_ref[:, vcols], (((1,), (1,)), ((), ())), preferred_element_type=F32)
            ds = (p * (dp - jnp.sum(p * dp, axis=1, keepdims=True)) * ATTN_SCALE).astype(BF16)
            dq_ref[:, cols] = jnp.dot(ds, kh, preferred_element_type=F32).astype(BF16)
            dkv_ref[:, cols] += lax.dot_general(ds, qh, (((0,), (0,)), ((), ())), preferred_element_type=F32)
            dkv_ref[:, vcols] += lax.dot_general(p.astype(BF16), doh, (((0,), (0,)), ((), ())), preferred_element_type=F32)

    qspec = pl.BlockSpec((tq, MEM_WIDTH), lambda i: (i, 0))
    kvspec = pl.BlockSpec((n_mem, 2 * MEM_WIDTH), lambda i: (0, 0))
    return pl.pallas_call(
        body, name="cross_bwd", grid=(s // tq,),
        in_specs=[qspec, kvspec, qspec],
        out_specs=[qspec, kvspec],
        out_shape=[jax.ShapeDtypeStruct((s, MEM_WIDTH), BF16), jax.ShapeDtypeStruct((n_mem, 2 * MEM_WIDTH), F32)],
        compiler_params=_params(1),
    )(qc, kvc, d_out)


def _swiglu_fwd(up, gate):
    return up, (gate * jax.nn.sigmoid(gate)) * up


def _swiglu_bwd(d_act, gate, up):
    sg = jax.nn.sigmoid(gate)
    silu = gate * sg
    return d_act * up * (sg * (1.0 + gate * (1.0 - sg))), d_act * silu


GATHER_GROUPS = {"in": ("w_in", "conv_w"), "mix": ("w_attn_out", "w_conv_out", "w_o"), "cross": ("w_cq", "w_ckv", "w_co"),
                 "gate": ("w_gate",), "up": ("w_up",), "down": ("w_down",)}


def _local_step(xs, mems, target, small, fetch, reduce):
    s, d = xs.shape
    w4 = {}
    cos_t, sin_t = _rope_tables(s)

    h = _rmsnorm(xs, small["g_mix"], "norm_mix")
    w4.update(fetch.get("in", h))
    conv4 = w4["conv_w"]
    conv_w = conv4[:, :3, :].transpose(1, 0, 2).reshape(3, N_CHIPS * conv4.shape[2])
    c_in = w4["w_in"].shape[2]
    z = _matmul(h, w4["w_in"], mode="nn", tm=512, tn=c_in, tk=d, out_dtypes=[F32], name="in_proj", b_blocks=N_CHIPS)
    fetch.begin("mix", z)
    q_rot, k_rot, v_b = _rope_fwd(z, cos_t, sin_t)
    attn = _swa_fwd(q_rot, k_rot, v_b, small["sink"])
    co = _conv_fwd(z, conv_w)
    w4.update(fetch.get("mix", co))
    fetch.begin("cross", co)
    w_o = w4["w_o"].reshape(-1, w4["w_o"].shape[-1])
    c_d = w4["w_attn_out"].shape[2]
    ya = _matmul(attn, w4["w_attn_out"], mode="nn", tm=1024, tn=c_d, tk=ATTN_WIDTH, out_dtypes=[F32], name="attn_out_proj",
                 b_blocks=N_CHIPS)
    yc = _matmul(co, w4["w_conv_out"], mode="nn", tm=1024, tn=c_d, tk=CONV_WIDTH, out_dtypes=[F32], name="conv_out_proj",
                 b_blocks=N_CHIPS)
    mix = _gate_fwd(z, small["b_gate"], ya, yc)
    x1 = _matmul(mix, w_o, mode="nn", tm=512, tn=1024, tk=d, out_dtypes=[F32], name="mix_out_proj", extras=[xs],
                 epilogue=_add_residual)
    w4.update(fetch.get("cross", x1))
    fetch.begin("gate", x1)
    w_cq = w4["w_cq"].reshape(-1, w4["w_cq"].shape[-1])
    w_ckv = w4["w_ckv"].reshape(-1, w4["w_ckv"].shape[-1])
    hc = _rmsnorm(x1, small["g_cross"], "norm_cross")
    memn = _rmsnorm(mems, small["g_mem"], "norm_mem")
    qc = _matmul(hc, w_cq, mode="nn", tm=1024, tn=MEM_WIDTH, tk=d, out_dtypes=[BF16], name="cross_q_proj")
    kvc = _matmul(memn, w_ckv, mode="nn", tm=256, tn=2 * MEM_WIDTH, tk=d, out_dtypes=[BF16], name="cross_kv_proj")
    oc = _cross_fwd(qc, kvc)
    x2 = _matmul(oc, w4["w_co"], mode="nn", tm=1024, tn=c_d, tk=MEM_WIDTH, out_dtypes=[F32], name="cross_out_proj",
                 extras=[x1], epilogue=_add_residual, b_blocks=N_CHIPS)
    hf = _rmsnorm(x2, small["g_ffn"], "norm_ffn")
    w4.update(fetch.get("gate", hf))
    c_ff = w4["w_gate"].shape[2]
    gate = _matmul(hf, w4["w_gate"], mode="nn", tm=512, tn=c_ff, tk=d, out_dtypes=[F32], name="ffn_gate_proj", b_blocks=N_CHIPS)
    w4.update(fetch.get("up", gate))
    up, act = _matmul(hf, w4["w_up"], mode="nn", tm=512, tn=c_ff, tk=d, out_dtypes=[F32, BF16], name="ffn_up_proj",
                      extras=[gate], epilogue=_swiglu_fwd, b_blocks=N_CHIPS)
    w4.update(fetch.get("down", act))
    w_down = w4["w_down"].reshape(-1, w4["w_down"].shape[-1])
    x3 = _matmul(act, w_down, mode="nn", tm=512, tn=1024, tk=c_ff, out_dtypes=[F32], name="ffn_down_proj", extras=[x2],
                 epilogue=_add_residual)
    dx3, dx3b, sq, dg_final = _loss_head(x3, small["g_final"], target)

    da, du = _matmul(dx3b, w_down, mode="nt", tm=512, tn=c_ff, tk=d, out_dtypes=[BF16, BF16], name="ffn_down_bwd",
                     extras=[gate, up], epilogue=_swiglu_bwd)
    g_down = _matmul(act, dx3b, mode="tn", tm=c_ff, tn=1024, tk=s, out_dtypes=[BF16], name="ffn_down_wgrad")
    tok = reduce.add("down", {"w_down": g_down}, da)
    g_gate = _matmul(hf, da, mode="tn", tm=512, tn=c_ff, tk=s, out_dtypes=[BF16], name="ffn_gate_wgrad", out_blocks=N_CHIPS,
                     after=tok)
    tok = reduce.step("down", g_gate)
    tok = reduce.add("gate", {"w_gate": g_gate}, tok)
    g_up = _matmul(hf, du, mode="tn", tm=512, tn=c_ff, tk=s, out_dtypes=[BF16], name="ffn_up_wgrad", out_blocks=N_CHIPS, after=tok)
    tok = reduce.step("gate", g_up)
    tok = reduce.add("up", {"w_up": g_up}, tok)
    dhf = _matmul(da, w4["w_gate"], mode="nt", tm=512, tn=1024, tk=c_ff, out_dtypes=[F32], name="ffn_gate_bwd", b_blocks=N_CHIPS,
                  after=tok)
    tok = reduce.step("up", dhf)
    dhf = _matmul(du, w4["w_up"], mode="nt", tm=512, tn=1024, tk=c_ff, out_dtypes=[F32], name="ffn_up_bwd", extras=[dhf],
                  epilogue=_add_residual, b_blocks=N_CHIPS, after=tok)
    tok = reduce.step("down", dhf)
    dx2, dx2b, dg_ffn = _rmsnorm_bwd(dhf, x2, small["g_ffn"], dx3, "norm_ffn_bwd")

    d_oc = _matmul(dx2b, w4["w_co"], mode="nt", tm=1024, tn=MEM_WIDTH, tk=c_d, out_dtypes=[BF16], name="cross_out_bwd",
                   b_blocks=N_CHIPS, after=tok)
    g_co = _matmul(oc, dx2b, mode="tn", tm=MEM_WIDTH, tn=c_d, tk=s, out_dtypes=[BF16], name="cross_out_wgrad", out_blocks=N_CHIPS)
    tok = reduce.step("gate", g_co)
    tok = reduce.step("down", tok)
    dqc, dkvc = _cross_bwd(qc, kvc, d_oc)
    g_cq = _matmul(hc, dqc, mode="tn", tm=1024, tn=MEM_WIDTH, tk=s, out_dtypes=[BF16], name="cross_q_wgrad", after=tok)
    dhc = _matmul(dqc, w_cq, mode="nt", tm=1024, tn=1024, tk=MEM_WIDTH, out_dtypes=[F32], name="cross_q_bwd")
    g_ckv = _matmul(memn, dkvc, mode="tn", tm=1024, tn=2 * MEM_WIDTH, tk=mems.shape[0], out_dtypes=[BF16], name="cross_kv_wgrad")
    dmemn = _matmul(dkvc, w_ckv, mode="nt", tm=256, tn=1024, tk=2 * MEM_WIDTH, out_dtypes=[F32], name="cross_kv_bwd")
    _, _, dg_mem = _rmsnorm_bwd(dmemn, mems, small["g_mem"], None, "norm_mem_bwd")
    dx1, dx1b, dg_cross = _rmsnorm_bwd(dhc, x1, small["g_cross"], dx2, "norm_cross_bwd")
    tok = reduce.add("cross", {"w_co": g_co, "w_cq": g_cq, "w_ckv": g_ckv}, dx1b)

    dmix = _matmul(dx1b, w_o, mode="nt", tm=512, tn=1024, tk=d, out_dtypes=[F32], name="mix_out_bwd", after=tok)
    tok = reduce.step("up", dmix)
    tok = reduce.step("gate", tok)
    g_o = _matmul(mix, dx1b, mode="tn", tm=1024, tn=1024, tk=s, out_dtypes=[BF16], name="mix_out_wgrad", after=tok)
    tok = reduce.step("cross", g_o)
    dya, dyc, dgl_a, dgl_c, db_a, db_c = _gate_bwd(z, small["b_gate"], ya, yc, dmix)
    d_attn = _matmul(dya, w4["w_attn_out"], mode="nt", tm=1024, tn=ATTN_WIDTH, tk=c_d, out_dtypes=[BF16], name="attn_out_bwd",
                     b_blocks=N_CHIPS, after=tok)
    g_ao = _matmul(attn, dya, mode="tn", tm=ATTN_WIDTH, tn=c_d, tk=s, out_dtypes=[BF16], name="attn_out_wgrad", out_blocks=N_CHIPS)
    d_co = _matmul(dyc, w4["w_conv_out"], mode="nt", tm=1024, tn=CONV_WIDTH, tk=c_d, out_dtypes=[F32], name="conv_out_bwd",
                   b_blocks=N_CHIPS)
    g_cvo = _matmul(co, dyc, mode="tn", tm=CONV_WIDTH, tn=c_d, tk=s, out_dtypes=[BF16], name="conv_out_wgrad", out_blocks=N_CHIPS)
    tok = reduce.add("mix", {"w_o": g_o, "w_attn_out": g_ao, "w_conv_out": g_cvo}, d_co)
    tok = reduce.step("up", tok)
    dcu, dcb, dcc, d_conv_w = _conv_bwd(z, conv_w, d_co)
    dq_rot, dk_rot, dv, dsink = _swa_bwd(q_rot, k_rot, v_b, d_attn, small["sink"])
    dq, dk, dvb = _rope_bwd(dq_rot, dk_rot, dv, cos_t, sin_t)
    dz = jnp.concatenate([dq, dk, dvb, dcu, dcb, dcc, dgl_a, dgl_c], axis=1)
    g_in = _matmul(h, dz, mode="tn", tm=512, tn=c_in, tk=s, out_dtypes=[BF16], name="in_proj_wgrad", out_blocks=N_CHIPS, after=tok)
    tok = reduce.step("mix", g_in)
    tok = reduce.add("in", {"w_in": g_in}, tok)
    tok = reduce.step("cross", tok)
    dh = _matmul(dz, w4["w_in"], mode="nt", tm=512, tn=1024, tk=c_in, out_dtypes=[F32], name="in_proj_bwd", b_blocks=N_CHIPS,
                 after=tok)
    tok = reduce.step("in", dh)
    tok = reduce.step("cross", tok)
    grad_x, _, dg_mix = _rmsnorm_bwd(dh, xs, small["g_mix"], dx1, "norm_mix_bwd")
    tok = reduce.step("mix", grad_x)
    tok = reduce.step("mix", tok)
    tok = reduce.step("in", tok)
    reduce.step("in", tok)

    small_grads = {
        "g_mix": dg_mix, "sink": dsink[:, 0], "b_gate": jnp.concatenate([db_a, db_c], axis=1), "g_cross": dg_cross,
        "g_mem": dg_mem, "g_ffn": dg_ffn, "g_final": dg_final, "conv_w": d_conv_w,
    }
    return sq, grad_x, small_grads


def _pair_sum(g4, ra, core, name):
    nb, rs, cs = g4.shape
    rh = rs // 2
    tr = _row_tile(rh, 256)
    per = rh // tr

    def body(c_ref, g_ref, r_ref, o_ref):
        o_ref[...] = (g_ref[...].astype(F32) + r_ref[...].astype(F32)).astype(BF16)

    plain = pl.BlockSpec((None, tr, cs), lambda j, i, c: (j, i, 0))
    return pl.pallas_call(
        body, name=name,
        grid_spec=pltpu.PrefetchScalarGridSpec(
            num_scalar_prefetch=1, grid=(nb, per),
            in_specs=[pl.BlockSpec((None, tr, cs), lambda j, i, c: (j, c[0] * per + i, 0)), plain],
            out_specs=plain),
        out_shape=jax.ShapeDtypeStruct((nb, rh, cs), BF16),
        compiler_params=_params(2),
    )(core, g4, ra)


def _quad_sum(parts, rc, place, name):
    _, rh, cs = parts.shape
    tr = _row_tile(rh, 256)
    per = rh // tr

    def body(p_ref, own_ref, r_ref, o_ref):
        acc = own_ref[...].astype(F32)
        for j in range(rc.shape[0]):
            acc = acc + r_ref[j].astype(F32)
        o_ref[...] = acc

    return pl.pallas_call(
        body, name=name,
        grid_spec=pltpu.PrefetchScalarGridSpec(
            num_scalar_prefetch=1, grid=(per,),
            in_specs=[pl.BlockSpec((None, tr, cs), lambda i, p: (p[0], i, 0)),
                      pl.BlockSpec((rc.shape[0], tr, cs), lambda i, p: (0, i, 0))],
            out_specs=pl.BlockSpec((tr, cs), lambda i, p: (p[1] * per + i, 0))),
        out_shape=jax.ShapeDtypeStruct((2 * rh, cs), F32),
        compiler_params=_params(1),
    )(place, parts, rc)


def _cast_to_slot(w, place, dtype, name):
    rows, cols = w.shape
    tr = _row_tile(rows, 256)

    def body(p_ref, w_ref, o_ref):
        o_ref[...] = w_ref[...].astype(dtype)

    return pl.pallas_call(
        body, name=name,
        grid_spec=pltpu.PrefetchScalarGridSpec(
            num_scalar_prefetch=1, grid=(rows // tr,),
            in_specs=[pl.BlockSpec((tr, cols), lambda i, p: (i, 0))],
            out_specs=pl.BlockSpec((None, tr, cols), lambda i, p: (p[0], i, 0))),
        out_shape=jax.ShapeDtypeStruct((N_CHIPS, rows, cols), dtype),
        compiler_params=_params(1),
    )(place, w)


def _adamw(w, g, m, v, name):
    rows, cols = w.shape
    tr = _row_tile(rows, 256)

    def body(w_ref, g_ref, m_ref, v_ref, go_ref, d_ref, nm_ref, nv_ref):
        gv = g_ref[...]
        go_ref[...] = gv
        nm = ADAM_B1 * m_ref[...] + (1.0 - ADAM_B1) * gv
        nv = ADAM_B2 * v_ref[...] + (1.0 - ADAM_B2) * (gv * gv)
        m_hat = nm / ADAM_C1
        v_hat = nv / ADAM_C2
        d_ref[...] = -ADAM_LR * (m_hat / (jnp.sqrt(v_hat) + ADAM_EPS) + ADAM_WD * w_ref[...])
        nm_ref[...] = nm
        nv_ref[...] = nv

    tile = pl.BlockSpec((tr, cols), lambda i: (i, 0))
    shape = jax.ShapeDtypeStruct((rows, cols), F32)
    return pl.pallas_call(
        body, name=name, grid=(rows // tr,),
        in_specs=[tile] * 4, out_specs=[tile] * 4, out_shape=[shape] * 4,
        compiler_params=_params(1),
    )(w, g, m, v)


def _mesh_pos():
    return lax.axis_index("x"), lax.axis_index("y"), lax.axis_index("c")


def _other_chips(x, y):
    return [(1 - x, y), (x, 1 - y), (1 - x, 1 - y)]


def _half_rows(ref, which):
    rh = ref.shape[-2] // 2
    return ref.at[pl.ds(which * rh, rh), :]


def _remote(src, dst, send_sems, recv_sems, sem, to):
    return pltpu.make_async_remote_copy(src_ref=src, dst_ref=dst, send_sem=send_sems.at[sem], recv_sem=recv_sems.at[sem],
                                        device_id=to, device_id_type=MESH)


HBM = pl.BlockSpec(memory_space=pltpu.HBM)
SEM = pl.BlockSpec(memory_space=pltpu.SEMAPHORE)
DATAFLOW_EFFECT = pltpu.SideEffectType.DATAFLOW_SIDE_EFFECTING


def _in_hbm(arrays):
    return [pltpu.with_memory_space_constraint(a, pltpu.HBM) for a in arrays]


def _hbm_like(arrays):
    return [pltpu.HBM(a.shape, a.dtype) for a in arrays]


def _gather_start(bufs, groups):
    n, ng = len(bufs), len(groups)

    def body(*refs):
        ins = refs[:n]
        send, recv = refs[n:n + ng], refs[n + ng:n + 2 * ng]
        x, y, c = _mesh_pos()
        me = 2 * x + y
        for g, members in enumerate(groups):
            for i, w in enumerate(members):
                mine = _half_rows(ins[w].at[me], c)
                for k, (px, py) in enumerate(_other_chips(x, y)):
                    _remote(mine, mine, send[g], recv[g], 3 * i + k, (px, py, c)).start()

    sems = [pltpu.SemaphoreType.DMA((3 * len(m),)) for m in groups]
    outs = pl.pallas_call(
        body, name="gather_start",
        in_specs=[HBM] * n, out_specs=[SEM] * (2 * ng) + [HBM] * n,
        out_shape=sems + sems + _hbm_like(bufs),
        input_output_aliases={i: 2 * ng + i for i in range(n)},
        compiler_params=pltpu.CompilerParams(has_side_effects=DATAFLOW_EFFECT),
    )(*_in_hbm(bufs))
    return outs[:ng], outs[ng:2 * ng], outs[2 * ng:]


def _gather_pass(bufs, send, recv, after, name):
    m = len(bufs)

    def body(*refs):
        ins, send_in, recv_in = refs[:m], refs[m], refs[m + 1]
        send_out, recv_out = refs[m + 3], refs[m + 4]
        x, y, c = _mesh_pos()
        for i in range(m):
            for k, (px, py) in enumerate(_other_chips(x, y)):
                landed = _half_rows(ins[i].at[2 * px + py], c)
                came = _remote(landed, landed, send_in, recv_in, 3 * i + k, (px, py, c))
                came.wait_recv()
                came.wait_send()
                _remote(landed, landed, send_out, recv_out, 3 * i + k, (x, y, 1 - c)).start()

    sems = [pltpu.SemaphoreType.DMA((3 * m,))] * 2
    outs = pl.pallas_call(
        body, name=name,
        in_specs=[HBM] * m + [SEM, SEM, ANY], out_specs=[SEM, SEM] + [HBM] * m,
        out_shape=sems + _hbm_like(bufs),
        input_output_aliases={i: 2 + i for i in range(m)},
        compiler_params=pltpu.CompilerParams(has_side_effects=DATAFLOW_EFFECT),
    )(*_in_hbm(bufs), send, recv, after)
    return outs[0], outs[1], outs[2:]


def _gather_done(bufs, send, recv, after, name):
    m = len(bufs)

    def body(*refs):
        ins, send_in, recv_in = refs[:m], refs[m], refs[m + 1]
        x, y, c = _mesh_pos()
        for i in range(m):
            for k, (px, py) in enumerate(_other_chips(x, y)):
                passed = _half_rows(ins[i].at[2 * px + py], 1 - c)
                came = _remote(passed, passed, send_in, recv_in, 3 * i + k, (x, y, 1 - c))
                came.wait_send()
                came.wait_recv()

    return pl.pallas_call(
        body, name=name,
        in_specs=[HBM] * m + [SEM, SEM, ANY], out_specs=[HBM] * m,
        out_shape=_hbm_like(bufs),
        input_output_aliases={i: i for i in range(m)},
        compiler_params=pltpu.CompilerParams(has_side_effects=DATAFLOW_EFFECT),
    )(*_in_hbm(bufs), send, recv, after)


class _Gather:
    def __init__(self, slotted, groups):
        self.groups = groups
        names = [n for members in groups.values() for n in members]
        index = {n: i for i, n in enumerate(names)}
        send, recv, bufs = _gather_start([slotted[n] for n in names], [[index[n] for n in m] for m in groups.values()])
        self.landing = {g: (send[j], recv[j], [bufs[index[n]] for n in m]) for j, (g, m) in enumerate(groups.items())}
        self.passing = {}

    def begin(self, group, after):
        send, recv, bufs = self.landing.pop(group)
        self.passing[group] = _gather_pass(bufs, send, recv, after, "gather_pass_" + group)

    def get(self, group, after):
        if group not in self.passing:
            self.begin(group, after)
        send, recv, bufs = self.passing.pop(group)
        return dict(zip(self.groups[group], _gather_done(bufs, send, recv, after, "gather_done_" + group)))


def _sibling_halves_copies(srcs, dsts, x, y, c):
    out = []
    for s_ref, d_ref in zip(srcs, dsts, strict=True):
        rh = s_ref.shape[1] // 2
        out.append((s_ref.at[:, pl.ds((1 - c) * rh, rh), :], d_ref, (x, y, 1 - c)))
    return out


def _chip_copies(srcs, dsts, x, y, c):
    out = []
    for s_ref, d_ref in zip(srcs, dsts, strict=True):
        for k, (px, py) in enumerate(_other_chips(x, y)):
            out.append((s_ref.at[2 * px + py], d_ref.at[k], (px, py, c)))
    return out


def _join_copies(srcs, dsts, x, y, c):
    out = []
    for s_ref in srcs:
        mine = _half_rows(s_ref, c)
        out.append((mine, mine, (x, y, 1 - c)))
    return out


def _exchange_start(copies_fn, n_copies, srcs, fresh, after, name):
    ns, nb = len(srcs), len(srcs) + len(fresh)

    def body(*refs):
        bufs, send, recv, token = refs[:nb], refs[nb + 1], refs[nb + 2], refs[-1]
        x, y, c = _mesh_pos()
        for i, (s_ref, d_ref, to) in enumerate(copies_fn(bufs[:ns], bufs[ns:] if fresh else bufs[:ns], x, y, c)):
            _remote(s_ref, d_ref, send, recv, i, to).start()
        token[...] = jnp.zeros_like(token)

    sems = [pltpu.SemaphoreType.DMA((n_copies,))] * 2
    outs = pl.pallas_call(
        body, name=name,
        in_specs=[HBM] * nb + [ANY], out_specs=[SEM, SEM] + [HBM] * nb + [pl.BlockSpec(memory_space=pltpu.VMEM)],
        out_shape=sems + _hbm_like(list(srcs) + list(fresh)) + [jax.ShapeDtypeStruct((8, 128), F32)],
        input_output_aliases={i: 2 + i for i in range(nb)},
        compiler_params=pltpu.CompilerParams(has_side_effects=DATAFLOW_EFFECT),
    )(*_in_hbm(list(srcs) + list(fresh)), after)
    return outs[0], outs[1], outs[2:2 + ns], outs[2 + ns:2 + nb], outs[-1]


def _exchange_done(copies_fn, srcs, fresh, send, recv, after, name):
    ns, nb = len(srcs), len(srcs) + len(fresh)

    def body(*refs):
        bufs, send_in, recv_in = refs[:nb], refs[nb], refs[nb + 1]
        x, y, c = _mesh_pos()
        for i, (s_ref, d_ref, to) in enumerate(copies_fn(bufs[:ns], bufs[ns:] if fresh else bufs[:ns], x, y, c)):
            came = _remote(s_ref, d_ref, send_in, recv_in, i, to)
            came.wait_send()
            came.wait_recv()

    outs = pl.pallas_call(
        body, name=name,
        in_specs=[HBM] * nb + [SEM, SEM, ANY], out_specs=[HBM] * nb,
        out_shape=_hbm_like(list(srcs) + list(fresh)),
        input_output_aliases={i: i for i in range(nb)},
        compiler_params=pltpu.CompilerParams(has_side_effects=DATAFLOW_EFFECT),
    )(*_in_hbm(list(srcs) + list(fresh)), send, recv, after)
    return outs[:ns], outs[ns:]


class _Reduce:
    def __init__(self, place, core, shards, mom_m, mom_v):
        self.place, self.core = place, core
        self.shards, self.mom_m, self.mom_v = shards, mom_m, mom_v
        self.state = {}
        self.results = {}

    def add(self, group, grads, after):
        names = list(grads)
        g4s = [g.reshape((N_CHIPS, -1, g.shape[-1])) if g.ndim == 2 else g for g in grads.values()]
        fresh = [lax.empty((N_CHIPS, g.shape[1] // 2, g.shape[2]), BF16) for g in g4s]
        send, recv, g4s, fresh, token = _exchange_start(_sibling_halves_copies, len(names), g4s, fresh, after,
                                                        "pair_start_" + group)
        self.state[group] = (0, names, send, recv, g4s, fresh)
        return token

    def step(self, group, after):
        stage, names, send, recv, srcs, fresh = self.state[group]
        if stage == 0:
            g4s, ras = _exchange_done(_sibling_halves_copies, srcs, fresh, send, recv, after, "pair_done_" + group)
            parts = [_pair_sum(g, r, self.core, "pair_sum_" + n) for g, r, n in zip(g4s, ras, names)]
            fresh = [lax.empty((N_CHIPS - 1,) + p.shape[1:], BF16) for p in parts]
            send, recv, parts, fresh, token = _exchange_start(_chip_copies, 3 * len(names), parts, fresh, self.core,
                                                              "chips_start_" + group)
            self.state[group] = (1, names, send, recv, parts, fresh)
            return token
        if stage == 1:
            parts, rcs = _exchange_done(_chip_copies, srcs, fresh, send, recv, after, "chips_done_" + group)
            wholes = [_quad_sum(p, r, self.place, "quad_sum_" + n) for p, r, n in zip(parts, rcs, names)]
            send, recv, wholes, _, token = _exchange_start(_join_copies, len(names), wholes, [], self.core, "join_start_" + group)
            self.state[group] = (2, names, send, recv, wholes, [])
            return token
        assert stage == 2
        wholes, _ = _exchange_done(_join_copies, srcs, [], send, recv, after, "join_done_" + group)
        for n, g in zip(names, wholes):
            self.results[n] = _adamw(self.shards[n], g, self.mom_m[n], self.mom_v[n], "adamw_" + n)
        del self.state[group]
        return self.results[names[-1]][1]


N_DEV = 8


def _all_reduce_small(v):
    def body(v_ref, o_ref, slots, send_sems, recv_sems):
        x, y, c = _mesh_pos()
        me = 4 * x + 2 * y + c
        slots[me] = v_ref[...]
        peers = []
        for r in range(1, N_DEV):
            fx, fy, fc = (r >> 2) & 1, (r >> 1) & 1, r & 1
            peers.append((x + fx - 2 * x * fx, y + fy - 2 * y * fy, c + fc - 2 * c * fc))
        sends = []
        for r, peer in enumerate(peers):
            cp = _remote(v_ref, slots.at[me], send_sems, recv_sems, r, peer)
            cp.start()
            sends.append(cp)
        for r, (px, py, pc) in enumerate(peers):
            landed = slots.at[4 * px + 2 * py + pc]
            _remote(landed, landed, send_sems, recv_sems, r, (px, py, pc)).wait_recv()
        for cp in sends:
            cp.wait_send()
        acc = slots[0]
        for i in range(1, N_DEV):
            acc = acc + slots[i]
        o_ref[...] = acc

    vm = pl.BlockSpec(memory_space=pltpu.VMEM)
    return pl.pallas_call(
        body, name="small_grads_all_reduce",
        in_specs=[vm], out_specs=vm,
        out_shape=jax.ShapeDtypeStruct(v.shape, v.dtype),
        scratch_shapes=[pltpu.VMEM((N_DEV,) + v.shape, v.dtype), pltpu.SemaphoreType.DMA((N_DEV - 1,)),
                        pltpu.SemaphoreType.DMA((N_DEV - 1,))],
    )(v)


MATRICES = ("w_in", "w_attn_out", "w_conv_out", "w_o", "w_cq", "w_ckv", "w_co", "w_gate", "w_up", "w_down")
VECTORS = ("g_mix", "b_gate", "g_cross", "g_mem", "g_ffn", "g_final", "conv_w", "sink")
WEIGHT_ORDER = ("g_mix", "w_in", "sink", "conv_w", "b_gate", "w_attn_out", "w_conv_out", "w_o", "g_cross", "g_mem", "w_cq",
                "w_ckv", "w_co", "g_ffn", "w_gate", "w_up", "w_down", "g_final")
CONV_PAD_ROWS = 16
SMALL_ROWS = 8


def _pack(pieces):
    flat = jnp.concatenate([p.reshape(-1) for p in pieces])
    lane_group = SMALL_ROWS * 128
    total = -(-flat.shape[0] // lane_group) * lane_group
    flat = jnp.pad(flat, (0, total - flat.shape[0]))
    return flat.reshape(SMALL_ROWS, total // SMALL_ROWS), [p.size for p in pieces]


def _unpack(packed, pieces):
    flat = packed.reshape(-1)
    out, off = [], 0
    for p in pieces:
        out.append(flat[off:off + p.size].reshape(p.shape))
        off += p.size
    return out


def kernel(x, mem, g_mix, w_in, sink, conv_w, b_gate, w_attn_out, w_conv_out, w_o, g_cross, g_mem, w_cq, w_ckv, w_co, g_ffn, w_gate, w_up, w_down, g_final, loss_target, m_g_mix, m_w_in, m_sink, m_conv_w, m_b_gate, m_w_attn_out, m_w_conv_out, m_w_o, m_g_cross, m_g_mem, m_w_cq, m_w_ckv, m_w_co, m_g_ffn, m_w_gate, m_w_up, m_w_down, m_g_final, v_g_mix, v_w_in, v_sink, v_conv_w, v_b_gate, v_w_attn_out, v_w_conv_out, v_w_o, v_g_cross, v_g_mem, v_w_cq, v_w_ckv, v_w_co, v_g_ffn, v_w_gate, v_w_up, v_w_down, v_g_final):
    given = dict(g_mix=g_mix, w_in=w_in, sink=sink, conv_w=conv_w, b_gate=b_gate, w_attn_out=w_attn_out, w_conv_out=w_conv_out,
                 w_o=w_o, g_cross=g_cross, g_mem=g_mem, w_cq=w_cq, w_ckv=w_ckv, w_co=w_co, g_ffn=g_ffn, w_gate=w_gate, w_up=w_up,
                 w_down=w_down, g_final=g_final)
    mom_m = dict(g_mix=m_g_mix, w_in=m_w_in, sink=m_sink, conv_w=m_conv_w, b_gate=m_b_gate, w_attn_out=m_w_attn_out,
                 w_conv_out=m_w_conv_out, w_o=m_w_o, g_cross=m_g_cross, g_mem=m_g_mem, w_cq=m_w_cq, w_ckv=m_w_ckv, w_co=m_w_co,
                 g_ffn=m_g_ffn, w_gate=m_w_gate, w_up=m_w_up, w_down=m_w_down, g_final=m_g_final)
    mom_v = dict(g_mix=v_g_mix, w_in=v_w_in, sink=v_sink, conv_w=v_conv_w, b_gate=v_b_gate, w_attn_out=v_w_attn_out,
                 w_conv_out=v_w_conv_out, w_o=v_w_o, g_cross=v_g_cross, g_mem=v_g_mem, w_cq=v_w_cq, w_ckv=v_w_ckv, w_co=v_w_co,
                 g_ffn=v_g_ffn, w_gate=v_w_gate, w_up=v_w_up, w_down=v_w_down, g_final=v_g_final)
    xs, mems, target = x[0], mem[0], loss_target[0]
    d_model = xs.shape[1]
    chip = 2 * lax.axis_index("x") + lax.axis_index("y")
    core = jnp.reshape(lax.axis_index("c"), (1,)).astype(jnp.int32)
    place = jnp.stack([chip, lax.axis_index("c")]).astype(jnp.int32)

    shards = {n: given[n][0] for n in MATRICES}
    conv_cols = conv_w.shape[2]
    conv_pad = jnp.pad(conv_w[0], ((0, CONV_PAD_ROWS - conv_w.shape[1]), (0, 0)))
    slotted = {n: _cast_to_slot(shards[n], place, BF16, "to_slot_" + n) for n in MATRICES}
    slotted["conv_w"] = _cast_to_slot(conv_pad, place, F32, "to_slot_conv_w")
    fetch = _Gather(slotted, GATHER_GROUPS)
    small = {n: given[n] for n in ("g_mix", "b_gate", "g_cross", "g_mem", "g_ffn")}
    small["g_final"] = g_final[None]
    small["sink"] = sink[0]

    reduce = _Reduce(place, core, shards, {n: mom_m[n][0] for n in MATRICES}, {n: mom_v[n][0] for n in MATRICES})
    sq, grad_x, small_grads = _local_step(xs, mems, target, small, fetch, reduce)

    loss_part = 0.5 * sq[0:1, 0:1] / d_model
    pieces = [small_grads[n] for n in VECTORS] + [loss_part]
    packed, _ = _pack(pieces)
    summed = _unpack(_all_reduce_small(packed), pieces)
    loss = summed[-1][0, 0]
    small_sum = dict(zip(VECTORS, summed[:-1]))
    small_sum["conv_w"] = lax.dynamic_slice_in_dim(small_sum["conv_w"], chip * conv_cols, conv_cols, axis=1)

    grad_out, delta, new_m, new_v = {}, {}, {}, {}
    for n in MATRICES:
        g, d, nm, nv = reduce.results[n]
        grad_out[n], delta[n], new_m[n], new_v[n] = g[None], d[None], nm[None], nv[None]
    like = [given[n] for n in VECTORS]
    pw, _ = _pack(like)
    pg, _ = _pack([small_sum[n] for n in VECTORS])
    pm, _ = _pack([mom_m[n] for n in VECTORS])
    pv, _ = _pack([mom_v[n] for n in VECTORS])
    _, pd, pnm, pnv = _adamw(pw, pg, pm, pv, "adamw_small")
    for n, g, d, nm, nv in zip(VECTORS, [small_sum[n] for n in VECTORS], _unpack(pd, like), _unpack(pnm, like), _unpack(pnv, like)):
        grad_out[n] = g.reshape(given[n].shape)
        delta[n], new_m[n], new_v[n] = d, nm, nv

    return (loss, grad_x[None], *[grad_out[n] for n in WEIGHT_ORDER], *[delta[n] for n in WEIGHT_ORDER],
            *[new_m[n] for n in WEIGHT_ORDER], *[new_v[n] for n in WEIGHT_ORDER])
```

```python
import functools

import jax
import jax.numpy as jnp
from jax import lax
from jax.experimental import pallas as pl
from jax.experimental.pallas import tpu as pltpu

F32 = jnp.float32
BF16 = jnp.bfloat16
MESH = pl.DeviceIdType.MESH
ANY = pl.BlockSpec(memory_space=pl.ANY)

VMEM_LIMIT_BYTES = 56 * 1024 * 1024

N_CHIPS = 4
HEAD_DIM = 128
N_Q_HEADS = 8
N_KV_HEADS = 2
Q_GROUP = N_Q_HEADS // N_KV_HEADS
ATTN_WIDTH = N_Q_HEADS * HEAD_DIM
KV_WIDTH = N_KV_HEADS * HEAD_DIM
WINDOW = 128
BLOCK = 128
BAND = 3 * BLOCK
ROPE_THETA = 10000.0
CONV_WIDTH = 1024
MEM_HEADS = 4
MEM_WIDTH = MEM_HEADS * HEAD_DIM
RMS_EPS = 1e-6
NEG_INF = -1e30
ATTN_SCALE = HEAD_DIM ** -0.5

Q_OFF, K_OFF, V_OFF, CU_OFF, CB_OFF, CC_OFF, GL_OFF = 0, 1024, 1280, 1536, 2560, 3584, 4608

ADAM_LR = 0.001
ADAM_B1 = 0.9
ADAM_B2 = 0.999
ADAM_EPS = 1e-08
ADAM_WD = 0.01
ADAM_STEP = 10
ADAM_C1 = 1.0 - ADAM_B1 ** ADAM_STEP
ADAM_C2 = 1.0 - ADAM_B2 ** ADAM_STEP


def _params(n_grid_axes):
    return pltpu.CompilerParams(dimension_semantics=("arbitrary",) * n_grid_axes, vmem_limit_bytes=VMEM_LIMIT_BYTES)


BF16_SUBLANES = 16


def _row_tile(rows, want):
    if rows <= want:
        return rows
    for t in range(want, 0, -BF16_SUBLANES):
        if rows % t == 0:
            return t
    return rows


def _matmul(a, b, *, mode, tm, tn, tk, out_dtypes, name, extras=(), epilogue=None, b_blocks=1, out_blocks=1, after=None):
    if mode == "tn":
        kdim, m = a.shape
    else:
        m, kdim = a.shape
    if b_blocks > 1:
        nb, brows, bcols = b.shape
        assert nb == b_blocks
        if mode == "nn":
            n = bcols * nb
            assert brows == kdim
        else:
            assert mode == "nt" and bcols * nb == kdim
            n = brows
    else:
        n = b.shape[0] if mode == "nt" else b.shape[1]
    tm, tn, tk = min(tm, m), min(tn, n), min(tk, kdim)
    assert m % tm == 0 and n % tn == 0 and kdim % tk == 0, (name, m, n, kdim, tm, tn, tk)
    nk = kdim // tk
    n_extra, n_out = len(extras), len(out_dtypes)
    n_after = 0 if after is None else 1

    if mode == "tn":
        a_spec = pl.BlockSpec((tk, tm), lambda j, i, k: (k, i))
        dims = (((0,), (0,)), ((), ()))
    else:
        a_spec = pl.BlockSpec((tm, tk), lambda j, i, k: (i, k))
        dims = (((1,), (0,)), ((), ())) if mode == "nn" else (((1,), (1,)), ((), ()))

    if b_blocks > 1 and mode == "nn":
        per = b.shape[2] // tn
        assert b.shape[2] % tn == 0
        b_spec = pl.BlockSpec((None, tk, tn), lambda j, i, k: (j // per, k, j % per))
    elif b_blocks > 1:
        per = b.shape[2] // tk
        assert b.shape[2] % tk == 0
        b_spec = pl.BlockSpec((None, tn, tk), lambda j, i, k: (k // per, j, k % per))
    elif mode == "nt":
        b_spec = pl.BlockSpec((tn, tk), lambda j, i, k: (j, k))
    else:
        b_spec = pl.BlockSpec((tk, tn), lambda j, i, k: (k, j))

    tile_spec = pl.BlockSpec((tm, tn), lambda j, i, k: (i, j))
    if out_blocks > 1:
        ncols = n // out_blocks
        assert ncols % tn == 0
        oper = ncols // tn
        out_spec = pl.BlockSpec((None, tm, tn), lambda j, i, k: (j // oper, i, j % oper))
        out_shape = [jax.ShapeDtypeStruct((out_blocks, m, ncols), dt) for dt in out_dtypes]
    else:
        out_spec = tile_spec
        out_shape = [jax.ShapeDtypeStruct((m, n), dt) for dt in out_dtypes]

    def body(a_ref, b_ref, *rest):
        extra_refs = rest[:n_extra]
        out_refs = rest[n_extra + n_after:n_extra + n_after + n_out]

        def finish(acc):
            if epilogue is None:
                tiles = (acc,)
            else:
                tiles = epilogue(acc, *[r[...] for r in extra_refs])
            for o_ref, t in zip(out_refs, tiles, strict=True):
                o_ref[...] = t.astype(o_ref.dtype)

        part = lax.dot_general(a_ref[...].astype(BF16), b_ref[...].astype(BF16), dims, preferred_element_type=F32)
        if nk == 1:
            finish(part)
        else:
            acc_ref = rest[-1]
            k = pl.program_id(2)

            @pl.when(k == 0)
            def _():
                acc_ref[...] = part

            @pl.when(k > 0)
            def _():
                acc_ref[...] += part

            @pl.when(k == nk - 1)
            def _():
                finish(acc_ref[...])

    outs = pl.pallas_call(
        body,
        name=name,
        grid=(n // tn, m // tm, nk),
        in_specs=[a_spec, b_spec] + [tile_spec] * n_extra + [ANY] * n_after,
        out_specs=[out_spec] * n_out,
        out_shape=out_shape,
        scratch_shapes=[pltpu.VMEM((tm, tn), F32)] if nk > 1 else [],
        compiler_params=_params(3),
    )(a, b, *extras, *([] if after is None else [after]))
    return outs[0] if n_out == 1 else outs


def _add_residual(acc, res):
    return (acc + res,)


def _rstd(x):
    return lax.rsqrt(jnp.mean(x * x, axis=-1, keepdims=True) + RMS_EPS)


def _rmsnorm(x, g, name):
    s, d = x.shape
    tr = _row_tile(s, 256)

    def body(x_ref, g_ref, o_ref):
        xv = x_ref[...]
        o_ref[...] = (xv * _rstd(xv) * g_ref[...]).astype(BF16)

    return pl.pallas_call(
        body, name=name, grid=(s // tr,),
        in_specs=[pl.BlockSpec((tr, d), lambda i: (i, 0)), pl.BlockSpec((1, d), lambda i: (0, 0))],
        out_specs=pl.BlockSpec((tr, d), lambda i: (i, 0)),
        out_shape=jax.ShapeDtypeStruct((s, d), BF16),
        compiler_params=_params(1),
    )(x, g)


def _rmsnorm_bwd(dh, x, g, dres, name):
    s, d = x.shape
    tr = _row_tile(s, 256)
    has_res = dres is not None

    def body(*refs):
        if has_res:
            dh_ref, x_ref, g_ref, res_ref, dx_ref, dxb_ref, dg_ref = refs
        else:
            dh_ref, x_ref, g_ref, dx_ref, dxb_ref, dg_ref = refs
        xv = x_ref[...]
        dhv = dh_ref[...].astype(F32)
        r = _rstd(xv)
        xn = xv * r
        dhg = dhv * g_ref[...]
        dx = r * (dhg - xn * jnp.mean(dhg * xn, axis=-1, keepdims=True))
        if has_res:
            dx = dx + res_ref[...]
        dx_ref[...] = dx
        dxb_ref[...] = dx.astype(BF16)
        part = jnp.sum(dhv * xn, axis=0, keepdims=True)

        @pl.when(pl.program_id(0) == 0)
        def _():
            dg_ref[...] = part

        @pl.when(pl.program_id(0) > 0)
        def _():
            dg_ref[...] += part

    row = pl.BlockSpec((tr, d), lambda i: (i, 0))
    vec = pl.BlockSpec((1, d), lambda i: (0, 0))
    return pl.pallas_call(
        body, name=name, grid=(s // tr,),
        in_specs=[row, row, vec] + ([row] if has_res else []),
        out_specs=[row, row, vec],
        out_shape=[jax.ShapeDtypeStruct((s, d), F32), jax.ShapeDtypeStruct((s, d), BF16), jax.ShapeDtypeStruct((1, d), F32)],
        compiler_params=_params(1),
    )(*([dh, x, g] + ([dres] if has_res else [])))


def _loss_head(x3, g, target):
    s, d = x3.shape
    tr = _row_tile(s, 256)

    def body(x_ref, g_ref, t_ref, dx_ref, dxb_ref, sq_ref, dg_ref):
        xv = x_ref[...]
        gv = g_ref[...]
        r = _rstd(xv)
        xn = xv * r
        err = xn * gv - t_ref[...]
        dy = err * (1.0 / d)
        dyg = dy * gv
        dx = r * (dyg - xn * jnp.mean(dyg * xn, axis=-1, keepdims=True))
        dx_ref[...] = dx
        dxb_ref[...] = dx.astype(BF16)
        sq = jnp.sum(jnp.sum(err * err, axis=1, keepdims=True), axis=0, keepdims=True)
        sq = jnp.broadcast_to(sq, (1, 128))
        part = jnp.sum(dy * xn, axis=0, keepdims=True)

        @pl.when(pl.program_id(0) == 0)
        def _():
            sq_ref[...] = sq
            dg_ref[...] = part

        @pl.when(pl.program_id(0) > 0)
        def _():
            sq_ref[...] += sq
            dg_ref[...] += part

    row = pl.BlockSpec((tr, d), lambda i: (i, 0))
    vec = pl.BlockSpec((1, d), lambda i: (0, 0))
    return pl.pallas_call(
        body, name="loss_head", grid=(s // tr,),
        in_specs=[row, vec, row],
        out_specs=[row, row, pl.BlockSpec((1, 128), lambda i: (0, 0)), vec],
        out_shape=[jax.ShapeDtypeStruct((s, d), F32), jax.ShapeDtypeStruct((s, d), BF16),
                   jax.ShapeDtypeStruct((1, 128), F32), jax.ShapeDtypeStruct((1, d), F32)],
        compiler_params=_params(1),
    )(x3, g, target)


def _rope_tables(s):
    inv = 1.0 / (ROPE_THETA ** (jnp.arange(0, HEAD_DIM, 2, dtype=F32) / HEAD_DIM))
    ang = jnp.arange(s, dtype=F32)[:, None] * inv[None, :]
    cos, sin = jnp.cos(ang), jnp.sin(ang)
    return jnp.concatenate([cos, cos], axis=1), jnp.concatenate([-sin, sin], axis=1)


def _swap_halves(t):
    return pltpu.roll(t, HEAD_DIM // 2, 1)


def _rope_fwd(z, cos_t, sin_t):
    s = z.shape[0]
    tr = _row_tile(s, 256)

    def body(zq_ref, zk_ref, zv_ref, c_ref, s_ref, q_ref, k_ref, v_ref):
        c, sn = c_ref[...], s_ref[...]
        for hd in range(N_Q_HEADS):
            cols = slice(hd * HEAD_DIM, (hd + 1) * HEAD_DIM)
            t = zq_ref[:, cols]
            q_ref[:, cols] = (t * c + _swap_halves(t) * sn).astype(BF16)
        for hd in range(N_KV_HEADS):
            cols = slice(hd * HEAD_DIM, (hd + 1) * HEAD_DIM)
            t = zk_ref[:, cols]
            k_ref[:, cols] = (t * c + _swap_halves(t) * sn).astype(BF16)
        v_ref[...] = zv_ref[...].astype(BF16)

    tab = pl.BlockSpec((tr, HEAD_DIM), lambda i: (i, 0))
    return pl.pallas_call(
        body, name="rope_fwd", grid=(s // tr,),
        in_specs=[pl.BlockSpec((tr, ATTN_WIDTH), lambda i: (i, Q_OFF // ATTN_WIDTH)),
                  pl.BlockSpec((tr, KV_WIDTH), lambda i: (i, K_OFF // KV_WIDTH)),
                  pl.BlockSpec((tr, KV_WIDTH), lambda i: (i, V_OFF // KV_WIDTH)), tab, tab],
        out_specs=[pl.BlockSpec((tr, ATTN_WIDTH), lambda i: (i, 0)), pl.BlockSpec((tr, KV_WIDTH), lambda i: (i, 0)),
                   pl.BlockSpec((tr, KV_WIDTH), lambda i: (i, 0))],
        out_shape=[jax.ShapeDtypeStruct((s, ATTN_WIDTH), BF16), jax.ShapeDtypeStruct((s, KV_WIDTH), BF16),
                   jax.ShapeDtypeStruct((s, KV_WIDTH), BF16)],
        compiler_params=_params(1),
    )(z, z, z, cos_t, sin_t)


def _rope_bwd(dq_rot, dk_rot, dv, cos_t, sin_t):
    s = dq_rot.shape[0]
    tr = _row_tile(s, 256)

    def body(dq_ref, dk_ref, dv_ref, c_ref, s_ref, oq_ref, ok_ref, ov_ref):
        c, sn = c_ref[...], s_ref[...]
        for hd in range(N_Q_HEADS):
            cols = slice(hd * HEAD_DIM, (hd + 1) * HEAD_DIM)
            t = dq_ref[:, cols]
            oq_ref[:, cols] = (t * c + _swap_halves(t * sn)).astype(BF16)
        for hd in range(N_KV_HEADS):
            cols = slice(hd * HEAD_DIM, (hd + 1) * HEAD_DIM)
            t = dk_ref[:, cols]
            ok_ref[:, cols] = (t * c + _swap_halves(t * sn)).astype(BF16)
        ov_ref[...] = dv_ref[...].astype(BF16)

    tab = pl.BlockSpec((tr, HEAD_DIM), lambda i: (i, 0))
    wide = pl.BlockSpec((tr, ATTN_WIDTH), lambda i: (i, 0))
    narrow = pl.BlockSpec((tr, KV_WIDTH), lambda i: (i, 0))
    return pl.pallas_call(
        body, name="rope_bwd", grid=(s // tr,),
        in_specs=[wide, narrow, narrow, tab, tab],
        out_specs=[wide, narrow, narrow],
        out_shape=[jax.ShapeDtypeStruct((s, ATTN_WIDTH), BF16), jax.ShapeDtypeStruct((s, KV_WIDTH), BF16),
                   jax.ShapeDtypeStruct((s, KV_WIDTH), BF16)],
        compiler_params=_params(1),
    )(dq_rot, dk_rot, dv, cos_t, sin_t)


def _swa_band(i, s):
    return pl.multiple_of(jnp.clip((i - 1) * BLOCK, 0, s - BAND), BLOCK)


def _swa_probs(q_ref, k_ref, sink_ref, kv, start, valid):
    cols = slice(kv * HEAD_DIM, (kv + 1) * HEAD_DIM)
    kb = k_ref[pl.ds(start, BAND), cols]
    heads = [kv * Q_GROUP + g for g in range(Q_GROUP)]
    qg = jnp.concatenate([q_ref[:, hd * HEAD_DIM:(hd + 1) * HEAD_DIM] for hd in heads], axis=0)
    sc = lax.dot_general(qg, kb, (((1,), (1,)), ((), ())), preferred_element_type=F32) * ATTN_SCALE
    sc = jnp.where(valid, sc, NEG_INF)
    sk = jnp.concatenate([jnp.full((BLOCK, 1), sink_ref[hd], F32) for hd in heads], axis=0)
    mx = jnp.maximum(jnp.max(sc, axis=1, keepdims=True), sk)
    e = jnp.exp(sc - mx)
    es = jnp.exp(sk - mx)
    inv = 1.0 / (jnp.sum(e, axis=1, keepdims=True) + es)
    return qg, kb, e * inv, es * inv


def _swa_valid(i, start):
    q_pos = i * BLOCK + lax.broadcasted_iota(jnp.int32, (BLOCK, 1), 0)
    q_pos = jnp.concatenate([q_pos] * Q_GROUP, axis=0)
    k_pos = start + lax.broadcasted_iota(jnp.int32, (1, BAND), 1)
    return jnp.abs(k_pos - q_pos) <= WINDOW


def _swa_fwd(q, k, v, sink):
    s = q.shape[0]
    assert s % BLOCK == 0 and s >= BAND

    def body(sink_ref, q_ref, k_ref, v_ref, o_ref):
        i = pl.program_id(0)
        start = _swa_band(i, s)
        valid = _swa_valid(i, start)
        for kv in range(N_KV_HEADS):
            _, _, p, _ = _swa_probs(q_ref, k_ref, sink_ref, kv, start, valid)
            vb = v_ref[pl.ds(start, BAND), kv * HEAD_DIM:(kv + 1) * HEAD_DIM]
            o = jnp.dot(p.astype(BF16), vb, preferred_element_type=F32)
            for g in range(Q_GROUP):
                hd = kv * Q_GROUP + g
                o_ref[:, hd * HEAD_DIM:(hd + 1) * HEAD_DIM] = o[g * BLOCK:(g + 1) * BLOCK].astype(BF16)

    whole = pl.BlockSpec((s, KV_WIDTH), lambda i: (0, 0))
    blk = pl.BlockSpec((BLOCK, ATTN_WIDTH), lambda i: (i, 0))
    return pl.pallas_call(
        body, name="swa_fwd", grid=(s // BLOCK,),
        in_specs=[pl.BlockSpec(memory_space=pltpu.SMEM), blk, whole, whole],
        out_specs=blk,
        out_shape=jax.ShapeDtypeStruct((s, ATTN_WIDTH), BF16),
        compiler_params=_params(1),
    )(sink, q, k, v)


def _swa_bwd(q, k, v, d_out, sink):
    s = q.shape[0]

    def body(sink_ref, q_ref, k_ref, v_ref, do_ref, dq_ref, dk_ref, dv_ref, dsink_ref):
        i = pl.program_id(0)

        @pl.when(i == 0)
        def _():
            dk_ref[...] = jnp.zeros_like(dk_ref)
            dv_ref[...] = jnp.zeros_like(dv_ref)
            dsink_ref[...] = jnp.zeros_like(dsink_ref)

        start = _swa_band(i, s)
        valid = _swa_valid(i, start)
        for kv in range(N_KV_HEADS):
            cols = slice(kv * HEAD_DIM, (kv + 1) * HEAD_DIM)
            qg, kb, p, p_sink = _swa_probs(q_ref, k_ref, sink_ref, kv, start, valid)
            vb = v_ref[pl.ds(start, BAND), cols]
            heads = [kv * Q_GROUP + g for g in range(Q_GROUP)]
            dog = jnp.concatenate([do_ref[:, hd * HEAD_DIM:(hd + 1) * HEAD_DIM] for hd in heads], axis=0)
            dp = lax.dot_general(dog, vb, (((1,), (1,)), ((), ())), preferred_element_type=F32)
            delta = jnp.sum(p * dp, axis=1, keepdims=True)
            ds = (p * (dp - delta) * ATTN_SCALE).astype(BF16)
            dqg = jnp.dot(ds, kb, preferred_element_type=F32)
            dk_ref[pl.ds(start, BAND), cols] += lax.dot_general(ds, qg, (((0,), (0,)), ((), ())), preferred_element_type=F32)
            dv_ref[pl.ds(start, BAND), cols] += lax.dot_general(p.astype(BF16), dog, (((0,), (0,)), ((), ())),
                                                                 preferred_element_type=F32)
            dsk = p_sink * delta
            for g, hd in enumerate(heads):
                dq_ref[:, hd * HEAD_DIM:(hd + 1) * HEAD_DIM] = dqg[g * BLOCK:(g + 1) * BLOCK]
                tot = jnp.sum(dsk[g * BLOCK:(g + 1) * BLOCK], axis=0, keepdims=True)
                dsink_ref[hd:hd + 1, :] -= jnp.broadcast_to(tot, (1, 128))

    whole = pl.BlockSpec((s, KV_WIDTH), lambda i: (0, 0))
    blk = pl.BlockSpec((BLOCK, ATTN_WIDTH), lambda i: (i, 0))
    return pl.pallas_call(
        body, name="swa_bwd", grid=(s // BLOCK,),
        in_specs=[pl.BlockSpec(memory_space=pltpu.SMEM), blk, whole, whole, blk],
        out_specs=[blk, whole, whole, pl.BlockSpec((N_Q_HEADS, 128), lambda i: (0, 0))],
        out_shape=[jax.ShapeDtypeStruct((s, ATTN_WIDTH), F32), jax.ShapeDtypeStruct((s, KV_WIDTH), F32),
                   jax.ShapeDtypeStruct((s, KV_WIDTH), F32), jax.ShapeDtypeStruct((N_Q_HEADS, 128), F32)],
        compiler_params=_params(1),
    )(sink, q, k, v, d_out)


CONV_CHUNK = 256


def _shift_rows(t, rows, down):
    n = t.shape[0]
    rolled = pltpu.roll(t, 1 if down else n - 1, 0)
    edge = 0 if down else n - 1
    return jnp.where(rows == edge, 0.0, rolled)


def _conv_specs(s):
    def z_spec(off):
        return pl.BlockSpec((s, CONV_CHUNK), lambda j, off=off: (0, off // CONV_CHUNK + j))
    chunk = pl.BlockSpec((s, CONV_CHUNK), lambda j: (0, j))
    w_spec = pl.BlockSpec((3, CONV_CHUNK), lambda j: (0, j))
    return z_spec(CU_OFF), z_spec(CB_OFF), z_spec(CC_OFF), chunk, w_spec


def _conv_fwd(z, conv_w):
    s = z.shape[0]
    cu_spec, cb_spec, cc_spec, chunk, w_spec = _conv_specs(s)

    def body(cu_ref, cb_ref, cc_ref, w_ref, o_ref):
        rows = lax.broadcasted_iota(jnp.int32, (s, 1), 0)
        t = cc_ref[...] * cu_ref[...]
        c3 = _shift_rows(t, rows, True) * w_ref[0:1, :] + t * w_ref[1:2, :] + _shift_rows(t, rows, False) * w_ref[2:3, :]
        o_ref[...] = (cb_ref[...] * c3).astype(BF16)

    return pl.pallas_call(
        body, name="conv_fwd", grid=(CONV_WIDTH // CONV_CHUNK,),
        in_specs=[cu_spec, cb_spec, cc_spec, w_spec],
        out_specs=chunk,
        out_shape=jax.ShapeDtypeStruct((s, CONV_WIDTH), BF16),
        compiler_params=_params(1),
    )(z, z, z, conv_w)


def _conv_bwd(z, conv_w, d_co):
    s = z.shape[0]
    cu_spec, cb_spec, cc_spec, chunk, w_spec = _conv_specs(s)

    def body(cu_ref, cb_ref, cc_ref, w_ref, d_ref, dcu_ref, dcb_ref, dcc_ref, dw_ref):
        rows = lax.broadcasted_iota(jnp.int32, (s, 1), 0)
        cu, cc = cu_ref[...], cc_ref[...]
        t = cc * cu
        t_dn, t_up = _shift_rows(t, rows, True), _shift_rows(t, rows, False)
        c3 = t_dn * w_ref[0:1, :] + t * w_ref[1:2, :] + t_up * w_ref[2:3, :]
        d = d_ref[...]
        dcb_ref[...] = (d * c3).astype(BF16)
        dc3 = d * cb_ref[...]
        dw_ref[0:1, :] = jnp.sum(dc3 * t_dn, axis=0, keepdims=True)
        dw_ref[1:2, :] = jnp.sum(dc3 * t, axis=0, keepdims=True)
        dw_ref[2:3, :] = jnp.sum(dc3 * t_up, axis=0, keepdims=True)
        dt = _shift_rows(dc3, rows, False) * w_ref[0:1, :] + dc3 * w_ref[1:2, :] + _shift_rows(dc3, rows, True) * w_ref[2:3, :]
        dcc_ref[...] = (dt * cu).astype(BF16)
        dcu_ref[...] = (dt * cc).astype(BF16)

    return pl.pallas_call(
        body, name="conv_bwd", grid=(CONV_WIDTH // CONV_CHUNK,),
        in_specs=[cu_spec, cb_spec, cc_spec, w_spec, chunk],
        out_specs=[chunk, chunk, chunk, w_spec],
        out_shape=[jax.ShapeDtypeStruct((s, CONV_WIDTH), BF16)] * 3 + [jax.ShapeDtypeStruct((3, CONV_WIDTH), F32)],
        compiler_params=_params(1),
    )(z, z, z, conv_w, d_co)


GATE_CHUNK = 512


def _gate_specs(s, d, tr):
    n_chunks = d // GATE_CHUNK
    za = pl.BlockSpec((tr, GATE_CHUNK), lambda j, i: (i, GL_OFF // GATE_CHUNK + j))
    zc = pl.BlockSpec((tr, GATE_CHUNK), lambda j, i: (i, GL_OFF // GATE_CHUNK + n_chunks + j))
    ba = pl.BlockSpec((1, GATE_CHUNK), lambda j, i: (0, j))
    bc = pl.BlockSpec((1, GATE_CHUNK), lambda j, i: (0, n_chunks + j))
    tile = pl.BlockSpec((tr, GATE_CHUNK), lambda j, i: (i, j))
    return za, zc, ba, bc, tile


def _gate_fwd(z, b_gate, ya, yc):
    s, d = ya.shape
    tr = _row_tile(s, 512)
    za, zc, ba, bc, tile = _gate_specs(s, d, tr)

    def body(za_ref, zc_ref, ba_ref, bc_ref, ya_ref, yc_ref, o_ref):
        ga = jax.nn.sigmoid(za_ref[...] + ba_ref[...])
        gc = jax.nn.sigmoid(zc_ref[...] + bc_ref[...])
        o_ref[...] = (ga * ya_ref[...] + gc * yc_ref[...]).astype(BF16)

    return pl.pallas_call(
        body, name="gate_fwd", grid=(d // GATE_CHUNK, s // tr),
        in_specs=[za, zc, ba, bc, tile, tile],
        out_specs=tile,
        out_shape=jax.ShapeDtypeStruct((s, d), BF16),
        compiler_params=_params(2),
    )(z, z, b_gate, b_gate, ya, yc)


def _gate_bwd(z, b_gate, ya, yc, dmix):
    s, d = ya.shape
    tr = _row_tile(s, 512)
    za, zc, ba, bc, tile = _gate_specs(s, d, tr)
    vec = pl.BlockSpec((1, GATE_CHUNK), lambda j, i: (0, j))

    def body(za_ref, zc_ref, ba_ref, bc_ref, ya_ref, yc_ref, dm_ref, dya_ref, dyc_ref, dla_ref, dlc_ref, dba_ref, dbc_ref):
        ga = jax.nn.sigmoid(za_ref[...] + ba_ref[...])
        gc = jax.nn.sigmoid(zc_ref[...] + bc_ref[...])
        dm = dm_ref[...]
        dya_ref[...] = (dm * ga).astype(BF16)
        dyc_ref[...] = (dm * gc).astype(BF16)
        dla = dm * ya_ref[...] * ga * (1.0 - ga)
        dlc = dm * yc_ref[...] * gc * (1.0 - gc)
        dla_ref[...] = dla.astype(BF16)
        dlc_ref[...] = dlc.astype(BF16)
        pa = jnp.sum(dla, axis=0, keepdims=True)
        pc = jnp.sum(dlc, axis=0, keepdims=True)

        @pl.when(pl.program_id(1) == 0)
        def _():
            dba_ref[...] = pa
            dbc_ref[...] = pc

        @pl.when(pl.program_id(1) > 0)
        def _():
            dba_ref[...] += pa
            dbc_ref[...] += pc

    big = jax.ShapeDtypeStruct((s, d), BF16)
    small = jax.ShapeDtypeStruct((1, d), F32)
    return pl.pallas_call(
        body, name="gate_bwd", grid=(d // GATE_CHUNK, s // tr),
        in_specs=[za, zc, ba, bc, tile, tile, tile],
        out_specs=[tile, tile, tile, tile, vec, vec],
        out_shape=[big, big, big, big, small, small],
        compiler_params=_params(2),
    )(z, z, b_gate, b_gate, ya, yc, dmix)


def _cross_probs(q_ref, kv_ref, hd):
    cols = slice(hd * HEAD_DIM, (hd + 1) * HEAD_DIM)
    qh = q_ref[:, cols]
    kh = kv_ref[:, cols]
    sc = lax.dot_general(qh, kh, (((1,), (1,)), ((), ())), preferred_element_type=F32) * ATTN_SCALE
    e = jnp.exp(sc - jnp.max(sc, axis=1, keepdims=True))
    return qh, kh, e * (1.0 / jnp.sum(e, axis=1, keepdims=True))


def _cross_fwd(qc, kvc):
    s = qc.shape[0]
    n_mem = kvc.shape[0]
    tq = _row_tile(s, 256)

    def body(q_ref, kv_ref, o_ref):
        for hd in range(MEM_HEADS):
            _, _, p = _cross_probs(q_ref, kv_ref, hd)
            vh = kv_ref[:, MEM_WIDTH + hd * HEAD_DIM:MEM_WIDTH + (hd + 1) * HEAD_DIM]
            o_ref[:, hd * HEAD_DIM:(hd + 1) * HEAD_DIM] = jnp.dot(p.astype(BF16), vh, preferred_element_type=F32).astype(BF16)

    return pl.pallas_call(
        body, name="cross_fwd", grid=(s // tq,),
        in_specs=[pl.BlockSpec((tq, MEM_WIDTH), lambda i: (i, 0)), pl.BlockSpec((n_mem, 2 * MEM_WIDTH), lambda i: (0, 0))],
        out_specs=pl.BlockSpec((tq, MEM_WIDTH), lambda i: (i, 0)),
        out_shape=jax.ShapeDtypeStruct((s, MEM_WIDTH), BF16),
        compiler_params=_params(1),
    )(qc, kvc)


def _cross_bwd(qc, kvc, d_out):
    s = qc.shape[0]
    n_mem = kvc.shape[0]
    tq = _row_tile(s, 256)

    def body(q_ref, kv_ref, do_ref, dq_ref, dkv_ref):
        @pl.when(pl.program_id(0) == 0)
        def _():
            dkv_ref[...] = jnp.zeros_like(dkv_ref)

        for hd in range(MEM_HEADS):
            cols = slice(hd * HEAD_DIM, (hd + 1) * HEAD_DIM)
            vcols = slice(MEM_WIDTH + hd * HEAD_DIM, MEM_WIDTH + (hd + 1) * HEAD_DIM)
            qh, kh, p = _cross_probs(q_ref, kv_ref, hd)
            doh = do_ref[:, cols]
            dp = lax.dot_general(doh, kv_ref[:, vcols], (((1,), (1,)), ((), ())), preferred_element_type=F32)
            ds = (p * (dp - jnp.sum(p * dp, axis=1, keepdims=True)) * ATTN_SCALE).astype(BF16)
            dq_ref[:, cols] = jnp.dot(ds, kh, preferred_element_type=F32).astype(BF16)
            dkv_ref[:, cols] += lax.dot_general(ds, qh, (((0,), (0,)), ((), ())), preferred_element_type=F32)
            dkv_ref[:, vcols] += lax.dot_general(p.astype(BF16), doh, (((0,), (0,)), ((), ())), preferred_element_type=F32)

    qspec = pl.BlockSpec((tq, MEM_WIDTH), lambda i: (i, 0))
    kvspec = pl.BlockSpec((n_mem, 2 * MEM_WIDTH), lambda i: (0, 0))
    return pl.pallas_call(
        body, name="cross_bwd", grid=(s // tq,),
        in_specs=[qspec, kvspec, qspec],
        out_specs=[qspec, kvspec],
        out_shape=[jax.ShapeDtypeStruct((s, MEM_WIDTH), BF16), jax.ShapeDtypeStruct((n_mem, 2 * MEM_WIDTH), F32)],
        compiler_params=_params(1),
    )(qc, kvc, d_out)


def _swiglu_fwd(up, gate):
    return up, (gate * jax.nn.sigmoid(gate)) * up


def _swiglu_bwd(d_act, gate, up):
    sg = jax.nn.sigmoid(gate)
    silu = gate * sg
    return d_act * up * (sg * (1.0 + gate * (1.0 - sg))), d_act * silu


GATHER_GROUPS = {"in": ("w_in", "conv_w"), "mix": ("w_attn_out", "w_conv_out", "w_o"), "cross": ("w_cq", "w_ckv", "w_co"),
                 "gate": ("w_gate",), "up": ("w_up",), "down": ("w_down",)}


def _local_step(xs, mems, target, small, fetch, reduce):
    s, d = xs.shape
    w4 = {}
    cos_t, sin_t = _rope_tables(s)

    h = _rmsnorm(xs, small["g_mix"], "norm_mix")
    w4.update(fetch.get("in", h))
    conv4 = w4["conv_w"]
    conv_w = conv4[:, :3, :].transpose(1, 0, 2).reshape(3, N_CHIPS * conv4.shape[2])
    c_in = w4["w_in"].shape[2]
    z = _matmul(h, w4["w_in"], mode="nn", tm=512, tn=c_in, tk=d, out_dtypes=[F32], name="in_proj", b_blocks=N_CHIPS)
    fetch.begin("mix", z)
    q_rot, k_rot, v_b = _rope_fwd(z, cos_t, sin_t)
    attn = _swa_fwd(q_rot, k_rot, v_b, small["sink"])
    co = _conv_fwd(z, conv_w)
    w4.update(fetch.get("mix", co))
    fetch.begin("cross", co)
    w_o = w4["w_o"].reshape(-1, w4["w_o"].shape[-1])
    c_d = w4["w_attn_out"].shape[2]
    ya = _matmul(attn, w4["w_attn_out"], mode="nn", tm=1024, tn=c_d, tk=ATTN_WIDTH, out_dtypes=[F32], name="attn_out_proj",
                 b_blocks=N_CHIPS)
    yc = _matmul(co, w4["w_conv_out"], mode="nn", tm=1024, tn=c_d, tk=CONV_WIDTH, out_dtypes=[F32], name="conv_out_proj",
                 b_blocks=N_CHIPS)
    mix = _gate_fwd(z, small["b_gate"], ya, yc)
    x1 = _matmul(mix, w_o, mode="nn", tm=512, tn=1024, tk=d, out_dtypes=[F32], name="mix_out_proj", extras=[xs],
                 epilogue=_add_residual)
    w4.update(fetch.get("cross", x1))
    fetch.begin("gate", x1)
    w_cq = w4["w_cq"].reshape(-1, w4["w_cq"].shape[-1])
    w_ckv = w4["w_ckv"].reshape(-1, w4["w_ckv"].shape[-1])
    hc = _rmsnorm(x1, small["g_cross"], "norm_cross")
    memn = _rmsnorm(mems, small["g_mem"], "norm_mem")
    qc = _matmul(hc, w_cq, mode="nn", tm=1024, tn=MEM_WIDTH, tk=d, out_dtypes=[BF16], name="cross_q_proj")
    kvc = _matmul(memn, w_ckv, mode="nn", tm=256, tn=2 * MEM_WIDTH, tk=d, out_dtypes=[BF16], name="cross_kv_proj")
    oc = _cross_fwd(qc, kvc)
    x2 = _matmul(oc, w4["w_co"], mode="nn", tm=1024, tn=c_d, tk=MEM_WIDTH, out_dtypes=[F32], name="cross_out_proj",
                 extras=[x1], epilogue=_add_residual, b_blocks=N_CHIPS)
    hf = _rmsnorm(x2, small["g_ffn"], "norm_ffn")
    w4.update(fetch.get("gate", hf))
    fetch.begin("up", hf)
    c_ff = w4["w_gate"].shape[2]
    gate = _matmul(hf, w4["w_gate"], mode="nn", tm=512, tn=c_ff, tk=d, out_dtypes=[F32], name="ffn_gate_proj", b_blocks=N_CHIPS)
    w4.update(fetch.get("up", gate))
    up, act = _matmul(hf, w4["w_up"], mode="nn", tm=512, tn=c_ff, tk=d, out_dtypes=[F32, BF16], name="ffn_up_proj",
                      extras=[gate], epilogue=_swiglu_fwd, b_blocks=N_CHIPS)
    w4.update(fetch.get("down", act))
    w_down = w4["w_down"].reshape(-1, w4["w_down"].shape[-1])
    x3 = _matmul(act, w_down, mode="nn", tm=512, tn=1024, tk=c_ff, out_dtypes=[F32], name="ffn_down_proj", extras=[x2],
                 epilogue=_add_residual)
    dx3, dx3b, sq, dg_final = _loss_head(x3, small["g_final"], target)

    da, du = _matmul(dx3b, w_down, mode="nt", tm=512, tn=c_ff, tk=d, out_dtypes=[BF16, BF16], name="ffn_down_bwd",
                     extras=[gate, up], epilogue=_swiglu_bwd)
    g_down = _matmul(act, dx3b, mode="tn", tm=c_ff, tn=1024, tk=s, out_dtypes=[BF16], name="ffn_down_wgrad")
    tok = reduce.add("down", {"w_down": g_down}, da)
    g_gate = _matmul(hf, da, mode="tn", tm=512, tn=c_ff, tk=s, out_dtypes=[BF16], name="ffn_gate_wgrad", out_blocks=N_CHIPS,
                     after=tok)
    tok = reduce.step("down", g_gate)
    tok = reduce.add("gate", {"w_gate": g_gate}, tok)
    g_up = _matmul(hf, du, mode="tn", tm=512, tn=c_ff, tk=s, out_dtypes=[BF16], name="ffn_up_wgrad", out_blocks=N_CHIPS, after=tok)
    tok = reduce.step("gate", g_up)
    tok = reduce.add("up", {"w_up": g_up}, tok)
    dhf = _matmul(da, w4["w_gate"], mode="nt", tm=512, tn=1024, tk=c_ff, out_dtypes=[F32], name="ffn_gate_bwd", b_blocks=N_CHIPS,
                  after=tok)
    tok = reduce.step("up", dhf)
    dhf = _matmul(du, w4["w_up"], mode="nt", tm=512, tn=1024, tk=c_ff, out_dtypes=[F32], name="ffn_up_bwd", extras=[dhf],
                  epilogue=_add_residual, b_blocks=N_CHIPS, after=tok)
    tok = reduce.step("down", dhf)
    dx2, dx2b, dg_ffn = _rmsnorm_bwd(dhf, x2, small["g_ffn"], dx3, "norm_ffn_bwd")

    d_oc = _matmul(dx2b, w4["w_co"], mode="nt", tm=1024, tn=MEM_WIDTH, tk=c_d, out_dtypes=[BF16], name="cross_out_bwd",
                   b_blocks=N_CHIPS, after=tok)
    g_co = _matmul(oc, dx2b, mode="tn", tm=MEM_WIDTH, tn=c_d, tk=s, out_dtypes=[BF16], name="cross_out_wgrad", out_blocks=N_CHIPS)
    tok = reduce.step("gate", g_co)
    tok = reduce.step("down", tok)
    dqc, dkvc = _cross_bwd(qc, kvc, d_oc)
    g_cq = _matmul(hc, dqc, mode="tn", tm=1024, tn=MEM_WIDTH, tk=s, out_dtypes=[BF16], name="cross_q_wgrad", after=tok)
    dhc = _matmul(dqc, w_cq, mode="nt", tm=1024, tn=1024, tk=MEM_WIDTH, out_dtypes=[F32], name="cross_q_bwd")
    g_ckv = _matmul(memn, dkvc, mode="tn", tm=1024, tn=2 * MEM_WIDTH, tk=mems.shape[0], out_dtypes=[BF16], name="cross_kv_wgrad")
    dmemn = _matmul(dkvc, w_ckv, mode="nt", tm=256, tn=1024, tk=2 * MEM_WIDTH, out_dtypes=[F32], name="cross_kv_bwd")
    _, _, dg_mem = _rmsnorm_bwd(dmemn, mems, small["g_mem"], None, "norm_mem_bwd")
    dx1, dx1b, dg_cross = _rmsnorm_bwd(dhc, x1, small["g_cross"], dx2, "norm_cross_bwd")
    tok = reduce.add("cross", {"w_co": g_co, "w_cq": g_cq, "w_ckv": g_ckv}, dx1b)

    dmix = _matmul(dx1b, w_o, mode="nt", tm=512, tn=1024, tk=d, out_dtypes=[F32], name="mix_out_bwd", after=tok)
    tok = reduce.step("up", dmix)
    tok = reduce.step("gate", tok)
    g_o = _matmul(mix, dx1b, mode="tn", tm=1024, tn=1024, tk=s, out_dtypes=[BF16], name="mix_out_wgrad", after=tok)
    tok = reduce.step("cross", g_o)
    dya, dyc, dgl_a, dgl_c, db_a, db_c = _gate_bwd(z, small["b_gate"], ya, yc, dmix)
    d_attn = _matmul(dya, w4["w_attn_out"], mode="nt", tm=1024, tn=ATTN_WIDTH, tk=c_d, out_dtypes=[BF16], name="attn_out_bwd",
                     b_blocks=N_CHIPS, after=tok)
    g_ao = _matmul(attn, dya, mode="tn", tm=ATTN_WIDTH, tn=c_d, tk=s, out_dtypes=[BF16], name="attn_out_wgrad", out_blocks=N_CHIPS)
    d_co = _matmul(dyc, w4["w_conv_out"], mode="nt", tm=1024, tn=CONV_WIDTH, tk=c_d, out_dtypes=[F32], name="conv_out_bwd",
                   b_blocks=N_CHIPS)
    g_cvo = _matmul(co, dyc, mode="tn", tm=CONV_WIDTH, tn=c_d, tk=s, out_dtypes=[BF16], name="conv_out_wgrad", out_blocks=N_CHIPS)
    tok = reduce.add("mix", {"w_o": g_o, "w_attn_out": g_ao, "w_conv_out": g_cvo}, d_co)
    tok = reduce.step("up", tok)
    dcu, dcb, dcc, d_conv_w = _conv_bwd(z, conv_w, d_co)
    dq_rot, dk_rot, dv, dsink = _swa_bwd(q_rot, k_rot, v_b, d_attn, small["sink"])
    dq, dk, dvb = _rope_bwd(dq_rot, dk_rot, dv, cos_t, sin_t)
    dz = jnp.concatenate([dq, dk, dvb, dcu, dcb, dcc, dgl_a, dgl_c], axis=1)
    g_in = _matmul(h, dz, mode="tn", tm=512, tn=c_in, tk=s, out_dtypes=[BF16], name="in_proj_wgrad", out_blocks=N_CHIPS, after=tok)
    tok = reduce.add("in", {"w_in": g_in}, tok)
    tok = reduce.step("mix", tok)
    tok = reduce.step("cross", tok)
    tok = reduce.step("in", tok)
    dh = _matmul(dz, w4["w_in"], mode="nt", tm=512, tn=1024, tk=c_in, out_dtypes=[F32], name="in_proj_bwd", b_blocks=N_CHIPS,
                 after=tok)
    tok = reduce.step("cross", dh)
    grad_x, _, dg_mix = _rmsnorm_bwd(dh, xs, small["g_mix"], dx1, "norm_mix_bwd")
    tok = reduce.step("mix", grad_x)
    reduce.step("mix", tok)

    small_grads = {
        "g_mix": dg_mix, "sink": dsink[:, 0], "b_gate": jnp.concatenate([db_a, db_c], axis=1), "g_cross": dg_cross,
        "g_mem": dg_mem, "g_ffn": dg_ffn, "g_final": dg_final, "conv_w": d_conv_w,
    }
    return sq, grad_x, small_grads


def _pair_sum(g4, ra, core, name):
    nb, rs, cs = g4.shape
    rh = rs // 2
    tr = _row_tile(rh, 256)
    per = rh // tr

    def body(c_ref, g_ref, r_ref, o_ref):
        o_ref[...] = (g_ref[...].astype(F32) + r_ref[...].astype(F32)).astype(BF16)

    plain = pl.BlockSpec((None, tr, cs), lambda j, i, c: (j, i, 0))
    return pl.pallas_call(
        body, name=name,
        grid_spec=pltpu.PrefetchScalarGridSpec(
            num_scalar_prefetch=1, grid=(nb, per),
            in_specs=[pl.BlockSpec((None, tr, cs), lambda j, i, c: (j, c[0] * per + i, 0)), plain],
            out_specs=plain),
        out_shape=jax.ShapeDtypeStruct((nb, rh, cs), BF16),
        compiler_params=_params(2),
    )(core, g4, ra)


def _quad_sum(parts, rc, place, name):
    _, rh, cs = parts.shape
    tr = _row_tile(rh, 256)
    per = rh // tr

    def body(p_ref, own_ref, r_ref, o_ref):
        acc = own_ref[...].astype(F32)
        for j in range(rc.shape[0]):
            acc = acc + r_ref[j].astype(F32)
        o_ref[...] = acc

    return pl.pallas_call(
        body, name=name,
        grid_spec=pltpu.PrefetchScalarGridSpec(
            num_scalar_prefetch=1, grid=(per,),
            in_specs=[pl.BlockSpec((None, tr, cs), lambda i, p: (p[0], i, 0)),
                      pl.BlockSpec((rc.shape[0], tr, cs), lambda i, p: (0, i, 0))],
            out_specs=pl.BlockSpec((tr, cs), lambda i, p: (p[1] * per + i, 0))),
        out_shape=jax.ShapeDtypeStruct((2 * rh, cs), F32),
        compiler_params=_params(1),
    )(place, parts, rc)


def _cast_to_slot(w, place, dtype, name, after=None):
    rows, cols = w.shape
    tr = _row_tile(rows, 256)

    def body(p_ref, w_ref, *rest):
        o_ref = rest[-1]
        o_ref[...] = w_ref[...].astype(dtype)

    return pl.pallas_call(
        body, name=name,
        grid_spec=pltpu.PrefetchScalarGridSpec(
            num_scalar_prefetch=1, grid=(rows // tr,),
            in_specs=[pl.BlockSpec((tr, cols), lambda i, p: (i, 0))] + ([] if after is None else [ANY]),
            out_specs=pl.BlockSpec((None, tr, cols), lambda i, p: (p[0], i, 0))),
        out_shape=jax.ShapeDtypeStruct((N_CHIPS, rows, cols), dtype),
        compiler_params=_params(1),
    )(place, w, *([] if after is None else [after]))


def _adamw(w, g, m, v, name):
    rows, cols = w.shape
    tr = _row_tile(rows, 256)

    def body(w_ref, g_ref, m_ref, v_ref, go_ref, d_ref, nm_ref, nv_ref):
        gv = g_ref[...]
        go_ref[...] = gv
        nm = ADAM_B1 * m_ref[...] + (1.0 - ADAM_B1) * gv
        nv = ADAM_B2 * v_ref[...] + (1.0 - ADAM_B2) * (gv * gv)
        m_hat = nm / ADAM_C1
        v_hat = nv / ADAM_C2
        d_ref[...] = -ADAM_LR * (m_hat / (jnp.sqrt(v_hat) + ADAM_EPS) + ADAM_WD * w_ref[...])
        nm_ref[...] = nm
        nv_ref[...] = nv

    tile = pl.BlockSpec((tr, cols), lambda i: (i, 0))
    shape = jax.ShapeDtypeStruct((rows, cols), F32)
    return pl.pallas_call(
        body, name=name, grid=(rows // tr,),
        in_specs=[tile] * 4, out_specs=[tile] * 4, out_shape=[shape] * 4,
        compiler_params=_params(1),
    )(w, g, m, v)


def _mesh_pos():
    return lax.axis_index("x"), lax.axis_index("y"), lax.axis_index("c")


def _other_chips(x, y):
    return [(1 - x, y), (x, 1 - y), (1 - x, 1 - y)]


def _half_rows(ref, which):
    rh = ref.shape[-2] // 2
    return ref.at[pl.ds(which * rh, rh), :]


def _remote(src, dst, send_sems, recv_sems, sem, to):
    return pltpu.make_async_remote_copy(src_ref=src, dst_ref=dst, send_sem=send_sems.at[sem], recv_sem=recv_sems.at[sem],
                                        device_id=to, device_id_type=MESH)


HBM = pl.BlockSpec(memory_space=pltpu.HBM)
SEM = pl.BlockSpec(memory_space=pltpu.SEMAPHORE)
DATAFLOW_EFFECT = pltpu.SideEffectType.DATAFLOW_SIDE_EFFECTING


def _in_hbm(arrays):
    return [pltpu.with_memory_space_constraint(a, pltpu.HBM) for a in arrays]


def _hbm_like(arrays):
    return [pltpu.HBM(a.shape, a.dtype) for a in arrays]


def _gather_start(bufs, groups, name):
    n, ng = len(bufs), len(groups)

    def body(*refs):
        ins = refs[:n]
        send, recv, token = refs[n:n + ng], refs[n + ng:n + 2 * ng], refs[-1]
        x, y, c = _mesh_pos()
        me = 2 * x + y
        for g, members in enumerate(groups):
            for i, w in enumerate(members):
                mine = _half_rows(ins[w].at[me], c)
                for k, (px, py) in enumerate(_other_chips(x, y)):
                    _remote(mine, mine, send[g], recv[g], 3 * i + k, (px, py, c)).start()
        token[...] = jnp.zeros_like(token)

    sems = [pltpu.SemaphoreType.DMA((3 * len(m),)) for m in groups]
    outs = pl.pallas_call(
        body, name=name,
        in_specs=[HBM] * n, out_specs=[SEM] * (2 * ng) + [HBM] * n + [pl.BlockSpec(memory_space=pltpu.VMEM)],
        out_shape=sems + sems + _hbm_like(bufs) + [jax.ShapeDtypeStruct((8, 128), F32)],
        input_output_aliases={i: 2 * ng + i for i in range(n)},
        compiler_params=pltpu.CompilerParams(has_side_effects=DATAFLOW_EFFECT),
    )(*_in_hbm(bufs))
    return outs[:ng], outs[ng:2 * ng], outs[2 * ng:2 * ng + n], outs[-1]


def _gather_pass(bufs, send, recv, after, name):
    m = len(bufs)

    def body(*refs):
        ins, send_in, recv_in = refs[:m], refs[m], refs[m + 1]
        send_out, recv_out = refs[m + 3], refs[m + 4]
        x, y, c = _mesh_pos()
        for i in range(m):
            for k, (px, py) in enumerate(_other_chips(x, y)):
                landed = _half_rows(ins[i].at[2 * px + py], c)
                came = _remote(landed, landed, send_in, recv_in, 3 * i + k, (px, py, c))
                came.wait_recv()
                came.wait_send()
                _remote(landed, landed, send_out, recv_out, 3 * i + k, (x, y, 1 - c)).start()

    sems = [pltpu.SemaphoreType.DMA((3 * m,))] * 2
    outs = pl.pallas_call(
        body, name=name,
        in_specs=[HBM] * m + [SEM, SEM, ANY], out_specs=[SEM, SEM] + [HBM] * m,
        out_shape=sems + _hbm_like(bufs),
        input_output_aliases={i: 2 + i for i in range(m)},
        compiler_params=pltpu.CompilerParams(has_side_effects=DATAFLOW_EFFECT),
    )(*_in_hbm(bufs), send, recv, after)
    return outs[0], outs[1], outs[2:]


def _gather_done(bufs, send, recv, after, name):
    m = len(bufs)

    def body(*refs):
        ins, send_in, recv_in = refs[:m], refs[m], refs[m + 1]
        x, y, c = _mesh_pos()
        for i in range(m):
            for k, (px, py) in enumerate(_other_chips(x, y)):
                passed = _half_rows(ins[i].at[2 * px + py], 1 - c)
                came = _remote(passed, passed, send_in, recv_in, 3 * i + k, (x, y, 1 - c))
                came.wait_send()
                came.wait_recv()

    return pl.pallas_call(
        body, name=name,
        in_specs=[HBM] * m + [SEM, SEM, ANY], out_specs=[HBM] * m,
        out_shape=_hbm_like(bufs),
        input_output_aliases={i: i for i in range(m)},
        compiler_params=pltpu.CompilerParams(has_side_effects=DATAFLOW_EFFECT),
    )(*_in_hbm(bufs), send, recv, after)


class _Gather:
    def __init__(self, groups):
        self.groups = groups
        self.landing = {}
        self.passing = {}

    def start(self, slotted, group_names, name):
        names = [n for g in group_names for n in self.groups[g]]
        index = {n: i for i, n in enumerate(names)}
        members = [[index[n] for n in self.groups[g]] for g in group_names]
        send, recv, bufs, token = _gather_start([slotted[n] for n in names], members, name)
        for j, g in enumerate(group_names):
            self.landing[g] = (send[j], recv[j], [bufs[index[n]] for n in self.groups[g]])
        return token

    def begin(self, group, after):
        send, recv, bufs = self.landing.pop(group)
        self.passing[group] = _gather_pass(bufs, send, recv, after, "gather_pass_" + group)

    def get(self, group, after):
        if group not in self.passing:
            self.begin(group, after)
        send, recv, bufs = self.passing.pop(group)
        return dict(zip(self.groups[group], _gather_done(bufs, send, recv, after, "gather_done_" + group)))


def _sibling_halves_copies(srcs, dsts, x, y, c):
    out = []
    for s_ref, d_ref in zip(srcs, dsts, strict=True):
        rh = s_ref.shape[1] // 2
        out.append((s_ref.at[:, pl.ds((1 - c) * rh, rh), :], d_ref, (x, y, 1 - c)))
    return out


def _chip_copies(srcs, dsts, x, y, c):
    out = []
    for s_ref, d_ref in zip(srcs, dsts, strict=True):
        for k, (px, py) in enumerate(_other_chips(x, y)):
            out.append((s_ref.at[2 * px + py], d_ref.at[k], (px, py, c)))
    return out


def _join_copies(srcs, dsts, x, y, c):
    out = []
    for s_ref in srcs:
        mine = _half_rows(s_ref, c)
        out.append((mine, mine, (x, y, 1 - c)))
    return out


def _exchange_start(copies_fn, n_copies, srcs, fresh, after, name):
    ns, nb = len(srcs), len(srcs) + len(fresh)

    def body(*refs):
        bufs, send, recv, token = refs[:nb], refs[nb + 1], refs[nb + 2], refs[-1]
        x, y, c = _mesh_pos()
        for i, (s_ref, d_ref, to) in enumerate(copies_fn(bufs[:ns], bufs[ns:] if fresh else bufs[:ns], x, y, c)):
            _remote(s_ref, d_ref, send, recv, i, to).start()
        token[...] = jnp.zeros_like(token)

    sems = [pltpu.SemaphoreType.DMA((n_copies,))] * 2
    outs = pl.pallas_call(
        body, name=name,
        in_specs=[HBM] * nb + [ANY], out_specs=[SEM, SEM] + [HBM] * nb + [pl.BlockSpec(memory_space=pltpu.VMEM)],
        out_shape=sems + _hbm_like(list(srcs) + list(fresh)) + [jax.ShapeDtypeStruct((8, 128), F32)],
        input_output_aliases={i: 2 + i for i in range(nb)},
        compiler_params=pltpu.CompilerParams(has_side_effects=DATAFLOW_EFFECT),
    )(*_in_hbm(list(srcs) + list(fresh)), after)
    return outs[0], outs[1], outs[2:2 + ns], outs[2 + ns:2 + nb], outs[-1]


def _exchange_done(copies_fn, srcs, fresh, send, recv, after, name):
    ns, nb = len(srcs), len(srcs) + len(fresh)

    def body(*refs):
        bufs, send_in, recv_in = refs[:nb], refs[nb], refs[nb + 1]
        x, y, c = _mesh_pos()
        for i, (s_ref, d_ref, to) in enumerate(copies_fn(bufs[:ns], bufs[ns:] if fresh else bufs[:ns], x, y, c)):
            came = _remote(s_ref, d_ref, send_in, recv_in, i, to)
            came.wait_send()
            came.wait_recv()

    outs = pl.pallas_call(
        body, name=name,
        in_specs=[HBM] * nb + [SEM, SEM, ANY], out_specs=[HBM] * nb,
        out_shape=_hbm_like(list(srcs) + list(fresh)),
        input_output_aliases={i: i for i in range(nb)},
        compiler_params=pltpu.CompilerParams(has_side_effects=DATAFLOW_EFFECT),
    )(*_in_hbm(list(srcs) + list(fresh)), send, recv, after)
    return outs[:ns], outs[ns:]


class _Reduce:
    def __init__(self, place, core, shards, mom_m, mom_v):
        self.place, self.core = place, core
        self.shards, self.mom_m, self.mom_v = shards, mom_m, mom_v
        self.state = {}
        self.results = {}

    def add(self, group, grads, after):
        names = list(grads)
        g4s = [g.reshape((N_CHIPS, -1, g.shape[-1])) if g.ndim == 2 else g for g in grads.values()]
        fresh = [lax.empty((N_CHIPS, g.shape[1] // 2, g.shape[2]), BF16) for g in g4s]
        send, recv, g4s, fresh, token = _exchange_start(_sibling_halves_copies, len(names), g4s, fresh, after,
                                                        "pair_start_" + group)
        self.state[group] = (0, names, send, recv, g4s, fresh)
        return token

    def step(self, group, after):
        stage, names, send, recv, srcs, fresh = self.state[group]
        if stage == 0:
            g4s, ras = _exchange_done(_sibling_halves_copies, srcs, fresh, send, recv, after, "pair_done_" + group)
            parts = [_pair_sum(g, r, self.core, "pair_sum_" + n) for g, r, n in zip(g4s, ras, names)]
            fresh = [lax.empty((N_CHIPS - 1,) + p.shape[1:], BF16) for p in parts]
            send, recv, parts, fresh, token = _exchange_start(_chip_copies, 3 * len(names), parts, fresh, self.core,
                                                              "chips_start_" + group)
            self.state[group] = (1, names, send, recv, parts, fresh)
            return token
        if stage == 1:
            parts, rcs = _exchange_done(_chip_copies, srcs, fresh, send, recv, after, "chips_done_" + group)
            wholes = [_quad_sum(p, r, self.place, "quad_sum_" + n) for p, r, n in zip(parts, rcs, names)]
            send, recv, wholes, _, token = _exchange_start(_join_copies, len(names), wholes, [], self.core, "join_start_" + group)
            self.state[group] = (2, names, send, recv, wholes, [])
            return token
        assert stage == 2
        wholes, _ = _exchange_done(_join_copies, srcs, [], send, recv, after, "join_done_" + group)
        for n, g in zip(names, wholes):
            self.results[n] = _adamw(self.shards[n], g, self.mom_m[n], self.mom_v[n], "adamw_" + n)
        del self.state[group]
        return self.results[names[-1]][1]


N_DEV = 8


def _all_reduce_small(v):
    def body(v_ref, o_ref, slots, send_sems, recv_sems):
        x, y, c = _mesh_pos()
        me = 4 * x + 2 * y + c
        slots[me] = v_ref[...]
        peers = []
        for r in range(1, N_DEV):
            fx, fy, fc = (r >> 2) & 1, (r >> 1) & 1, r & 1
            peers.append((x + fx - 2 * x * fx, y + fy - 2 * y * fy, c + fc - 2 * c * fc))
        sends = []
        for r, peer in enumerate(peers):
            cp = _remote(v_ref, slots.at[me], send_sems, recv_sems, r, peer)
            cp.start()
            sends.append(cp)
        for r, (px, py, pc) in enumerate(peers):
            landed = slots.at[4 * px + 2 * py + pc]
            _remote(landed, landed, send_sems, recv_sems, r, (px, py, pc)).wait_recv()
        for cp in sends:
            cp.wait_send()
        acc = slots[0]
        for i in range(1, N_DEV):
            acc = acc + slots[i]
        o_ref[...] = acc

    vm = pl.BlockSpec(memory_space=pltpu.VMEM)
    return pl.pallas_call(
        body, name="small_grads_all_reduce",
        in_specs=[vm], out_specs=vm,
        out_shape=jax.ShapeDtypeStruct(v.shape, v.dtype),
        scratch_shapes=[pltpu.VMEM((N_DEV,) + v.shape, v.dtype), pltpu.SemaphoreType.DMA((N_DEV - 1,)),
                        pltpu.SemaphoreType.DMA((N_DEV - 1,))],
    )(v)


MATRICES = ("w_in", "w_attn_out", "w_conv_out", "w_o", "w_cq", "w_ckv", "w_co", "w_gate", "w_up", "w_down")
VECTORS = ("g_mix", "b_gate", "g_cross", "g_mem", "g_ffn", "g_final", "conv_w", "sink")
WEIGHT_ORDER = ("g_mix", "w_in", "sink", "conv_w", "b_gate", "w_attn_out", "w_conv_out", "w_o", "g_cross", "g_mem", "w_cq",
                "w_ckv", "w_co", "g_ffn", "w_gate", "w_up", "w_down", "g_final")
CONV_PAD_ROWS = 16
SMALL_ROWS = 8


def _pack(pieces):
    flat = jnp.concatenate([p.reshape(-1) for p in pieces])
    lane_group = SMALL_ROWS * 128
    total = -(-flat.shape[0] // lane_group) * lane_group
    flat = jnp.pad(flat, (0, total - flat.shape[0]))
    return flat.reshape(SMALL_ROWS, total // SMALL_ROWS), [p.size for p in pieces]


def _unpack(packed, pieces):
    flat = packed.reshape(-1)
    out, off = [], 0
    for p in pieces:
        out.append(flat[off:off + p.size].reshape(p.shape))
        off += p.size
    return out


def kernel(x, mem, g_mix, w_in, sink, conv_w, b_gate, w_attn_out, w_conv_out, w_o, g_cross, g_mem, w_cq, w_ckv, w_co, g_ffn, w_gate, w_up, w_down, g_final, loss_target, m_g_mix, m_w_in, m_sink, m_conv_w, m_b_gate, m_w_attn_out, m_w_conv_out, m_w_o, m_g_cross, m_g_mem, m_w_cq, m_w_ckv, m_w_co, m_g_ffn, m_w_gate, m_w_up, m_w_down, m_g_final, v_g_mix, v_w_in, v_sink, v_conv_w, v_b_gate, v_w_attn_out, v_w_conv_out, v_w_o, v_g_cross, v_g_mem, v_w_cq, v_w_ckv, v_w_co, v_g_ffn, v_w_gate, v_w_up, v_w_down, v_g_final):
    given = dict(g_mix=g_mix, w_in=w_in, sink=sink, conv_w=conv_w, b_gate=b_gate, w_attn_out=w_attn_out, w_conv_out=w_conv_out,
                 w_o=w_o, g_cross=g_cross, g_mem=g_mem, w_cq=w_cq, w_ckv=w_ckv, w_co=w_co, g_ffn=g_ffn, w_gate=w_gate, w_up=w_up,
                 w_down=w_down, g_final=g_final)
    mom_m = dict(g_mix=m_g_mix, w_in=m_w_in, sink=m_sink, conv_w=m_conv_w, b_gate=m_b_gate, w_attn_out=m_w_attn_out,
                 w_conv_out=m_w_conv_out, w_o=m_w_o, g_cross=m_g_cross, g_mem=m_g_mem, w_cq=m_w_cq, w_ckv=m_w_ckv, w_co=m_w_co,
                 g_ffn=m_g_ffn, w_gate=m_w_gate, w_up=m_w_up, w_down=m_w_down, g_final=m_g_final)
    mom_v = dict(g_mix=v_g_mix, w_in=v_w_in, sink=v_sink, conv_w=v_conv_w, b_gate=v_b_gate, w_attn_out=v_w_attn_out,
                 w_conv_out=v_w_conv_out, w_o=v_w_o, g_cross=v_g_cross, g_mem=v_g_mem, w_cq=v_w_cq, w_ckv=v_w_ckv, w_co=v_w_co,
                 g_ffn=v_g_ffn, w_gate=v_w_gate, w_up=v_w_up, w_down=v_w_down, g_final=v_g_final)
    xs, mems, target = x[0], mem[0], loss_target[0]
    d_model = xs.shape[1]
    chip = 2 * lax.axis_index("x") + lax.axis_index("y")
    core = jnp.reshape(lax.axis_index("c"), (1,)).astype(jnp.int32)
    place = jnp.stack([chip, lax.axis_index("c")]).astype(jnp.int32)

    shards = {n: given[n][0] for n in MATRICES}
    conv_cols = conv_w.shape[2]
    conv_pad = jnp.pad(conv_w[0], ((0, CONV_PAD_ROWS - conv_w.shape[1]), (0, 0)))
    fetch = _Gather(GATHER_GROUPS)
    first = {"w_in": _cast_to_slot(shards["w_in"], place, BF16, "to_slot_w_in"),
             "conv_w": _cast_to_slot(conv_pad, place, F32, "to_slot_conv_w")}
    tok = fetch.start(first, ["in"], "gather_start_in")
    rest = {n: _cast_to_slot(shards[n], place, BF16, "to_slot_" + n, after=tok) for n in MATRICES if n != "w_in"}
    fetch.start(rest, [g for g in GATHER_GROUPS if g != "in"], "gather_start_rest")
    small = {n: given[n] for n in ("g_mix", "b_gate", "g_cross", "g_mem", "g_ffn")}
    small["g_final"] = g_final[None]
    small["sink"] = sink[0]

    reduce = _Reduce(place, core, shards, {n: mom_m[n][0] for n in MATRICES}, {n: mom_v[n][0] for n in MATRICES})
    sq, grad_x, small_grads = _local_step(xs, mems, target, small, fetch, reduce)

    loss_part = 0.5 * sq[0:1, 0:1] / d_model
    pieces = [small_grads[n] for n in VECTORS] + [loss_part]
    packed, _ = _pack(pieces)
    summed = _unpack(_all_reduce_small(packed), pieces)
    loss = summed[-1][0, 0]
    small_sum = dict(zip(VECTORS, summed[:-1]))
    small_sum["conv_w"] = lax.dynamic_slice_in_dim(small_sum["conv_w"], chip * conv_cols, conv_cols, axis=1)

    grad_out, delta, new_m, new_v = {}, {}, {}, {}
    like = [given[n] for n in VECTORS]
    pw, _ = _pack(like)
    pg, _ = _pack([small_sum[n] for n in VECTORS])
    pm, _ = _pack([mom_m[n] for n in VECTORS])
    pv, _ = _pack([mom_v[n] for n in VECTORS])
    _, pd, pnm, pnv = _adamw(pw, pg, pm, pv, "adamw_small")
    for n, g, d, nm, nv in zip(VECTORS, [small_sum[n] for n in VECTORS], _unpack(pd, like), _unpack(pnm, like), _unpack(pnv, like)):
        grad_out[n] = g.reshape(given[n].shape)
        delta[n], new_m[n], new_v[n] = d, nm, nv
    tok = reduce.step("in", pd)
    reduce.step("in", tok)
    for n in MATRICES:
        g, d, nm, nv = reduce.results[n]
        grad_out[n], delta[n], new_m[n], new_v[n] = g[None], d[None], nm[None], nv[None]

    return (loss, grad_x[None], *[grad_out[n] for n in WEIGHT_ORDER], *[delta[n] for n in WEIGHT_ORDER],
            *[new_m[n] for n in WEIGHT_ORDER], *[new_v[n] for n in WEIGHT_ORDER])
```

```python
import functools

import jax
import jax.numpy as jnp
from jax import lax
from jax.experimental import pallas as pl
from jax.experimental.pallas import tpu as pltpu

F32 = jnp.float32
BF16 = jnp.bfloat16
MESH = pl.DeviceIdType.MESH
ANY = pl.BlockSpec(memory_space=pl.ANY)

VMEM_LIMIT_BYTES = 56 * 1024 * 1024

N_CHIPS = 4
HEAD_DIM = 128
N_Q_HEADS = 8
N_KV_HEADS = 2
Q_GROUP = N_Q_HEADS // N_KV_HEADS
ATTN_WIDTH = N_Q_HEADS * HEAD_DIM
KV_WIDTH = N_KV_HEADS * HEAD_DIM
WINDOW = 128
BLOCK = 128
BAND = 3 * BLOCK
ROPE_THETA = 10000.0
CONV_WIDTH = 1024
MEM_HEADS = 4
MEM_WIDTH = MEM_HEADS * HEAD_DIM
RMS_EPS = 1e-6
NEG_INF = -1e30
ATTN_SCALE = HEAD_DIM ** -0.5

Q_OFF, K_OFF, V_OFF, CU_OFF, CB_OFF, CC_OFF, GL_OFF = 0, 1024, 1280, 1536, 2560, 3584, 4608

ADAM_LR = 0.001
ADAM_B1 = 0.9
ADAM_B2 = 0.999
ADAM_EPS = 1e-08
ADAM_WD = 0.01
ADAM_STEP = 10
ADAM_C1 = 1.0 - ADAM_B1 ** ADAM_STEP
ADAM_C2 = 1.0 - ADAM_B2 ** ADAM_STEP


def _params(n_grid_axes):
    return pltpu.CompilerParams(dimension_semantics=("arbitrary",) * n_grid_axes, vmem_limit_bytes=VMEM_LIMIT_BYTES)


BF16_SUBLANES = 16


def _row_tile(rows, want):
    if rows <= want:
        return rows
    for t in range(want, 0, -BF16_SUBLANES):
        if rows % t == 0:
            return t
    return rows


def _matmul(a, b, *, mode, tm, tn, tk, out_dtypes, name, extras=(), epilogue=None, b_blocks=1, out_blocks=1, after=None):
    if mode == "tn":
        kdim, m = a.shape
    else:
        m, kdim = a.shape
    if b_blocks > 1:
        nb, brows, bcols = b.shape
        assert nb == b_blocks
        if mode == "nn":
            n = bcols * nb
            assert brows == kdim
        else:
            assert mode == "nt" and bcols * nb == kdim
            n = brows
    else:
        n = b.shape[0] if mode == "nt" else b.shape[1]
    tm, tn, tk = min(tm, m), min(tn, n), min(tk, kdim)
    assert m % tm == 0 and n % tn == 0 and kdim % tk == 0, (name, m, n, kdim, tm, tn, tk)
    nk = kdim // tk
    n_extra, n_out = len(extras), len(out_dtypes)
    n_after = 0 if after is None else 1

    if mode == "tn":
        a_spec = pl.BlockSpec((tk, tm), lambda j, i, k: (k, i))
        dims = (((0,), (0,)), ((), ()))
    else:
        a_spec = pl.BlockSpec((tm, tk), lambda j, i, k: (i, k))
        dims = (((1,), (0,)), ((), ())) if mode == "nn" else (((1,), (1,)), ((), ()))

    if b_blocks > 1 and mode == "nn":
        per = b.shape[2] // tn
        assert b.shape[2] % tn == 0
        b_spec = pl.BlockSpec((None, tk, tn), lambda j, i, k: (j // per, k, j % per))
    elif b_blocks > 1 and nk == 1:
        b_spec = pl.BlockSpec((b_blocks, tn, b.shape[2]), lambda j, i, k: (0, j, 0))
    elif b_blocks > 1:
        per = b.shape[2] // tk
        assert b.shape[2] % tk == 0
        b_spec = pl.BlockSpec((None, tn, tk), lambda j, i, k: (k // per, j, k % per))
    elif mode == "nt":
        b_spec = pl.BlockSpec((tn, tk), lambda j, i, k: (j, k))
    else:
        b_spec = pl.BlockSpec((tk, tn), lambda j, i, k: (k, j))

    tile_spec = pl.BlockSpec((tm, tn), lambda j, i, k: (i, j))
    if out_blocks > 1:
        ncols = n // out_blocks
        assert ncols % tn == 0
        oper = ncols // tn
        out_spec = pl.BlockSpec((None, tm, tn), lambda j, i, k: (j // oper, i, j % oper))
        out_shape = [jax.ShapeDtypeStruct((out_blocks, m, ncols), dt) for dt in out_dtypes]
    else:
        out_spec = tile_spec
        out_shape = [jax.ShapeDtypeStruct((m, n), dt) for dt in out_dtypes]

    def body(a_ref, b_ref, *rest):
        extra_refs = rest[:n_extra]
        out_refs = rest[n_extra + n_after:n_extra + n_after + n_out]

        def finish(acc):
            if epilogue is None:
                tiles = (acc,)
            else:
                tiles = epilogue(acc, *[r[...] for r in extra_refs])
            for o_ref, t in zip(out_refs, tiles, strict=True):
                o_ref[...] = t.astype(o_ref.dtype)

        if b_blocks > 1 and mode == "nt" and nk == 1:
            cs = b.shape[2]
            part = None
            for jb in range(b_blocks):
                prod = lax.dot_general(a_ref[:, jb * cs:(jb + 1) * cs].astype(BF16), b_ref[jb].astype(BF16), dims,
                                       preferred_element_type=F32)
                part = prod if part is None else part + prod
        else:
            part = lax.dot_general(a_ref[...].astype(BF16), b_ref[...].astype(BF16), dims, preferred_element_type=F32)
        if nk == 1:
            finish(part)
        else:
            acc_ref = rest[-1]
            k = pl.program_id(2)

            @pl.when(k == 0)
            def _():
                acc_ref[...] = part

            @pl.when(k > 0)
            def _():
                acc_ref[...] += part

            @pl.when(k == nk - 1)
            def _():
                finish(acc_ref[...])

    outs = pl.pallas_call(
        body,
        name=name,
        grid=(n // tn, m // tm, nk),
        in_specs=[a_spec, b_spec] + [tile_spec] * n_extra + [ANY] * n_after,
        out_specs=[out_spec] * n_out,
        out_shape=out_shape,
        scratch_shapes=[pltpu.VMEM((tm, tn), F32)] if nk > 1 else [],
        compiler_params=_params(3),
    )(a, b, *extras, *([] if after is None else [after]))
    return outs[0] if n_out == 1 else outs


def _add_residual(acc, res):
    return (acc + res,)


def _rstd(x):
    return lax.rsqrt(jnp.mean(x * x, axis=-1, keepdims=True) + RMS_EPS)


def _rmsnorm(x, g, name):
    s, d = x.shape
    tr = _row_tile(s, 256)

    def body(x_ref, g_ref, o_ref):
        xv = x_ref[...]
        o_ref[...] = (xv * _rstd(xv) * g_ref[...]).astype(BF16)

    return pl.pallas_call(
        body, name=name, grid=(s // tr,),
        in_specs=[pl.BlockSpec((tr, d), lambda i: (i, 0)), pl.BlockSpec((1, d), lambda i: (0, 0))],
        out_specs=pl.BlockSpec((tr, d), lambda i: (i, 0)),
        out_shape=jax.ShapeDtypeStruct((s, d), BF16),
        compiler_params=_params(1),
    )(x, g)


def _rmsnorm_bwd(dh, x, g, dres, name):
    s, d = x.shape
    tr = _row_tile(s, 256)
    has_res = dres is not None

    def body(*refs):
        if has_res:
            dh_ref, x_ref, g_ref, res_ref, dx_ref, dxb_ref, dg_ref = refs
        else:
            dh_ref, x_ref, g_ref, dx_ref, dxb_ref, dg_ref = refs
        xv = x_ref[...]
        dhv = dh_ref[...].astype(F32)
        r = _rstd(xv)
        xn = xv * r
        dhg = dhv * g_ref[...]
        dx = r * (dhg - xn * jnp.mean(dhg * xn, axis=-1, keepdims=True))
        if has_res:
            dx = dx + res_ref[...]
        dx_ref[...] = dx
        dxb_ref[...] = dx.astype(BF16)
        part = jnp.sum(dhv * xn, axis=0, keepdims=True)

        @pl.when(pl.program_id(0) == 0)
        def _():
            dg_ref[...] = part

        @pl.when(pl.program_id(0) > 0)
        def _():
            dg_ref[...] += part

    row = pl.BlockSpec((tr, d), lambda i: (i, 0))
    vec = pl.BlockSpec((1, d), lambda i: (0, 0))
    return pl.pallas_call(
        body, name=name, grid=(s // tr,),
        in_specs=[row, row, vec] + ([row] if has_res else []),
        out_specs=[row, row, vec],
        out_shape=[jax.ShapeDtypeStruct((s, d), F32), jax.ShapeDtypeStruct((s, d), BF16), jax.ShapeDtypeStruct((1, d), F32)],
        compiler_params=_params(1),
    )(*([dh, x, g] + ([dres] if has_res else [])))


def _loss_head(x3, g, target):
    s, d = x3.shape
    tr = _row_tile(s, 256)

    def body(x_ref, g_ref, t_ref, dx_ref, dxb_ref, sq_ref, dg_ref):
        xv = x_ref[...]
        gv = g_ref[...]
        r = _rstd(xv)
        xn = xv * r
        err = xn * gv - t_ref[...]
        dy = err * (1.0 / d)
        dyg = dy * gv
        dx = r * (dyg - xn * jnp.mean(dyg * xn, axis=-1, keepdims=True))
        dx_ref[...] = dx
        dxb_ref[...] = dx.astype(BF16)
        sq = jnp.sum(jnp.sum(err * err, axis=1, keepdims=True), axis=0, keepdims=True)
        sq = jnp.broadcast_to(sq, (1, 128))
        part = jnp.sum(dy * xn, axis=0, keepdims=True)

        @pl.when(pl.program_id(0) == 0)
        def _():
            sq_ref[...] = sq
            dg_ref[...] = part

        @pl.when(pl.program_id(0) > 0)
        def _():
            sq_ref[...] += sq
            dg_ref[...] += part

    row = pl.BlockSpec((tr, d), lambda i: (i, 0))
    vec = pl.BlockSpec((1, d), lambda i: (0, 0))
    return pl.pallas_call(
        body, name="loss_head", grid=(s // tr,),
        in_specs=[row, vec, row],
        out_specs=[row, row, pl.BlockSpec((1, 128), lambda i: (0, 0)), vec],
        out_shape=[jax.ShapeDtypeStruct((s, d), F32), jax.ShapeDtypeStruct((s, d), BF16),
                   jax.ShapeDtypeStruct((1, 128), F32), jax.ShapeDtypeStruct((1, d), F32)],
        compiler_params=_params(1),
    )(x3, g, target)


def _rope_tables(s):
    inv = 1.0 / (ROPE_THETA ** (jnp.arange(0, HEAD_DIM, 2, dtype=F32) / HEAD_DIM))
    ang = jnp.arange(s, dtype=F32)[:, None] * inv[None, :]
    cos, sin = jnp.cos(ang), jnp.sin(ang)
    return jnp.concatenate([cos, cos], axis=1), jnp.concatenate([-sin, sin], axis=1)


def _swap_halves(t):
    return pltpu.roll(t, HEAD_DIM // 2, 1)


def _rope_fwd(z, cos_t, sin_t):
    s = z.shape[0]
    tr = _row_tile(s, 256)

    def body(zq_ref, zk_ref, zv_ref, c_ref, s_ref, q_ref, k_ref, v_ref):
        c, sn = c_ref[...], s_ref[...]
        for hd in range(N_Q_HEADS):
            cols = slice(hd * HEAD_DIM, (hd + 1) * HEAD_DIM)
            t = zq_ref[:, cols]
            q_ref[:, cols] = (t * c + _swap_halves(t) * sn).astype(BF16)
        for hd in range(N_KV_HEADS):
            cols = slice(hd * HEAD_DIM, (hd + 1) * HEAD_DIM)
            t = zk_ref[:, cols]
            k_ref[:, cols] = (t * c + _swap_halves(t) * sn).astype(BF16)
        v_ref[...] = zv_ref[...].astype(BF16)

    tab = pl.BlockSpec((tr, HEAD_DIM), lambda i: (i, 0))
    return pl.pallas_call(
        body, name="rope_fwd", grid=(s // tr,),
        in_specs=[pl.BlockSpec((tr, ATTN_WIDTH), lambda i: (i, Q_OFF // ATTN_WIDTH)),
                  pl.BlockSpec((tr, KV_WIDTH), lambda i: (i, K_OFF // KV_WIDTH)),
                  pl.BlockSpec((tr, KV_WIDTH), lambda i: (i, V_OFF // KV_WIDTH)), tab, tab],
        out_specs=[pl.BlockSpec((tr, ATTN_WIDTH), lambda i: (i, 0)), pl.BlockSpec((tr, KV_WIDTH), lambda i: (i, 0)),
                   pl.BlockSpec((tr, KV_WIDTH), lambda i: (i, 0))],
        out_shape=[jax.ShapeDtypeStruct((s, ATTN_WIDTH), BF16), jax.ShapeDtypeStruct((s, KV_WIDTH), BF16),
                   jax.ShapeDtypeStruct((s, KV_WIDTH), BF16)],
        compiler_params=_params(1),
    )(z, z, z, cos_t, sin_t)


def _rope_bwd(dq_rot, dk_rot, dv, cos_t, sin_t):
    s = dq_rot.shape[0]
    tr = _row_tile(s, 256)

    def body(dq_ref, dk_ref, dv_ref, c_ref, s_ref, oq_ref, ok_ref, ov_ref):
        c, sn = c_ref[...], s_ref[...]
        for hd in range(N_Q_HEADS):
            cols = slice(hd * HEAD_DIM, (hd + 1) * HEAD_DIM)
            t = dq_ref[:, cols]
            oq_ref[:, cols] = (t * c + _swap_halves(t * sn)).astype(BF16)
        for hd in range(N_KV_HEADS):
            cols = slice(hd * HEAD_DIM, (hd + 1) * HEAD_DIM)
            t = dk_ref[:, cols]
            ok_ref[:, cols] = (t * c + _swap_halves(t * sn)).astype(BF16)
        ov_ref[...] = dv_ref[...].astype(BF16)

    tab = pl.BlockSpec((tr, HEAD_DIM), lambda i: (i, 0))
    wide = pl.BlockSpec((tr, ATTN_WIDTH), lambda i: (i, 0))
    narrow = pl.BlockSpec((tr, KV_WIDTH), lambda i: (i, 0))
    return pl.pallas_call(
        body, name="rope_bwd", grid=(s // tr,),
        in_specs=[wide, narrow, narrow, tab, tab],
        out_specs=[wide, narrow, narrow],
        out_shape=[jax.ShapeDtypeStruct((s, ATTN_WIDTH), BF16), jax.ShapeDtypeStruct((s, KV_WIDTH), BF16),
                   jax.ShapeDtypeStruct((s, KV_WIDTH), BF16)],
        compiler_params=_params(1),
    )(dq_rot, dk_rot, dv, cos_t, sin_t)


def _swa_band(i, s):
    return pl.multiple_of(jnp.clip((i - 1) * BLOCK, 0, s - BAND), BLOCK)


def _swa_probs(q_ref, k_ref, sink_ref, kv, start, valid):
    cols = slice(kv * HEAD_DIM, (kv + 1) * HEAD_DIM)
    kb = k_ref[pl.ds(start, BAND), cols]
    heads = [kv * Q_GROUP + g for g in range(Q_GROUP)]
    qg = jnp.concatenate([q_ref[:, hd * HEAD_DIM:(hd + 1) * HEAD_DIM] for hd in heads], axis=0)
    sc = lax.dot_general(qg, kb, (((1,), (1,)), ((), ())), preferred_element_type=F32) * ATTN_SCALE
    sc = jnp.where(valid, sc, NEG_INF)
    sk = jnp.concatenate([jnp.full((BLOCK, 1), sink_ref[hd], F32) for hd in heads], axis=0)
    mx = jnp.maximum(jnp.max(sc, axis=1, keepdims=True), sk)
    e = jnp.exp(sc - mx)
    es = jnp.exp(sk - mx)
    inv = 1.0 / (jnp.sum(e, axis=1, keepdims=True) + es)
    return qg, kb, e * inv, es * inv


def _swa_valid(i, start):
    q_pos = i * BLOCK + lax.broadcasted_iota(jnp.int32, (BLOCK, 1), 0)
    q_pos = jnp.concatenate([q_pos] * Q_GROUP, axis=0)
    k_pos = start + lax.broadcasted_iota(jnp.int32, (1, BAND), 1)
    return jnp.abs(k_pos - q_pos) <= WINDOW


def _swa_fwd(q, k, v, sink):
    s = q.shape[0]
    assert s % BLOCK == 0 and s >= BAND

    def body(sink_ref, q_ref, k_ref, v_ref, o_ref):
        i = pl.program_id(0)
        start = _swa_band(i, s)
        valid = _swa_valid(i, start)
        for kv in range(N_KV_HEADS):
            _, _, p, _ = _swa_probs(q_ref, k_ref, sink_ref, kv, start, valid)
            vb = v_ref[pl.ds(start, BAND), kv * HEAD_DIM:(kv + 1) * HEAD_DIM]
            o = jnp.dot(p.astype(BF16), vb, preferred_element_type=F32)
            for g in range(Q_GROUP):
                hd = kv * Q_GROUP + g
                o_ref[:, hd * HEAD_DIM:(hd + 1) * HEAD_DIM] = o[g * BLOCK:(g + 1) * BLOCK].astype(BF16)

    whole = pl.BlockSpec((s, KV_WIDTH), lambda i: (0, 0))
    blk = pl.BlockSpec((BLOCK, ATTN_WIDTH), lambda i: (i, 0))
    return pl.pallas_call(
        body, name="swa_fwd", grid=(s // BLOCK,),
        in_specs=[pl.BlockSpec(memory_space=pltpu.SMEM), blk, whole, whole],
        out_specs=blk,
        out_shape=jax.ShapeDtypeStruct((s, ATTN_WIDTH), BF16),
        compiler_params=_params(1),
    )(sink, q, k, v)


def _swa_bwd(q, k, v, d_out, sink):
    s = q.shape[0]

    def body(sink_ref, q_ref, k_ref, v_ref, do_ref, dq_ref, dk_ref, dv_ref, dsink_ref):
        i = pl.program_id(0)

        @pl.when(i == 0)
        def _():
            dk_ref[...] = jnp.zeros_like(dk_ref)
            dv_ref[...] = jnp.zeros_like(dv_ref)
            dsink_ref[...] = jnp.zeros_like(dsink_ref)

        start = _swa_band(i, s)
        valid = _swa_valid(i, start)
        for kv in range(N_KV_HEADS):
            cols = slice(kv * HEAD_DIM, (kv + 1) * HEAD_DIM)
            qg, kb, p, p_sink = _swa_probs(q_ref, k_ref, sink_ref, kv, start, valid)
            vb = v_ref[pl.ds(start, BAND), cols]
            heads = [kv * Q_GROUP + g for g in range(Q_GROUP)]
            dog = jnp.concatenate([do_ref[:, hd * HEAD_DIM:(hd + 1) * HEAD_DIM] for hd in heads], axis=0)
            dp = lax.dot_general(dog, vb, (((1,), (1,)), ((), ())), preferred_element_type=F32)
            delta = jnp.sum(p * dp, axis=1, keepdims=True)
            ds = (p * (dp - delta) * ATTN_SCALE).astype(BF16)
            dqg = jnp.dot(ds, kb, preferred_element_type=F32)
            dk_ref[pl.ds(start, BAND), cols] += lax.dot_general(ds, qg, (((0,), (0,)), ((), ())), preferred_element_type=F32)
            dv_ref[pl.ds(start, BAND), cols] += lax.dot_general(p.astype(BF16), dog, (((0,), (0,)), ((), ())),
                                                                 preferred_element_type=F32)
            dsk = p_sink * delta
            for g, hd in enumerate(heads):
                dq_ref[:, hd * HEAD_DIM:(hd + 1) * HEAD_DIM] = dqg[g * BLOCK:(g + 1) * BLOCK]
                tot = jnp.sum(dsk[g * BLOCK:(g + 1) * BLOCK], axis=0, keepdims=True)
                dsink_ref[hd:hd + 1, :] -= jnp.broadcast_to(tot, (1, 128))

    whole = pl.BlockSpec((s, KV_WIDTH), lambda i: (0, 0))
    blk = pl.BlockSpec((BLOCK, ATTN_WIDTH), lambda i: (i, 0))
    return pl.pallas_call(
        body, name="swa_bwd", grid=(s // BLOCK,),
        in_specs=[pl.BlockSpec(memory_space=pltpu.SMEM), blk, whole, whole, blk],
        out_specs=[blk, whole, whole, pl.BlockSpec((N_Q_HEADS, 128), lambda i: (0, 0))],
        out_shape=[jax.ShapeDtypeStruct((s, ATTN_WIDTH), F32), jax.ShapeDtypeStruct((s, KV_WIDTH), F32),
                   jax.ShapeDtypeStruct((s, KV_WIDTH), F32), jax.ShapeDtypeStruct((N_Q_HEADS, 128), F32)],
        compiler_params=_params(1),
    )(sink, q, k, v, d_out)


CONV_CHUNK = 256


def _shift_rows(t, rows, down):
    n = t.shape[0]
    rolled = pltpu.roll(t, 1 if down else n - 1, 0)
    edge = 0 if down else n - 1
    return jnp.where(rows == edge, 0.0, rolled)


def _conv_specs(s):
    def z_spec(off):
        return pl.BlockSpec((s, CONV_CHUNK), lambda j, off=off: (0, off // CONV_CHUNK + j))
    chunk = pl.BlockSpec((s, CONV_CHUNK), lambda j: (0, j))
    w_spec = pl.BlockSpec((3, CONV_CHUNK), lambda j: (0, j))
    return z_spec(CU_OFF), z_spec(CB_OFF), z_spec(CC_OFF), chunk, w_spec


def _conv_fwd(z, conv_w):
    s = z.shape[0]
    cu_spec, cb_spec, cc_spec, chunk, w_spec = _conv_specs(s)

    def body(cu_ref, cb_ref, cc_ref, w_ref, o_ref):
        rows = lax.broadcasted_iota(jnp.int32, (s, 1), 0)
        t = cc_ref[...] * cu_ref[...]
        c3 = _shift_rows(t, rows, True) * w_ref[0:1, :] + t * w_ref[1:2, :] + _shift_rows(t, rows, False) * w_ref[2:3, :]
        o_ref[...] = (cb_ref[...] * c3).astype(BF16)

    return pl.pallas_call(
        body, name="conv_fwd", grid=(CONV_WIDTH // CONV_CHUNK,),
        in_specs=[cu_spec, cb_spec, cc_spec, w_spec],
        out_specs=chunk,
        out_shape=jax.ShapeDtypeStruct((s, CONV_WIDTH), BF16),
        compiler_params=_params(1),
    )(z, z, z, conv_w)


def _conv_bwd(z, conv_w, d_co):
    s = z.shape[0]
    cu_spec, cb_spec, cc_spec, chunk, w_spec = _conv_specs(s)

    def body(cu_ref, cb_ref, cc_ref, w_ref, d_ref, dcu_ref, dcb_ref, dcc_ref, dw_ref):
        rows = lax.broadcasted_iota(jnp.int32, (s, 1), 0)
        cu, cc = cu_ref[...], cc_ref[...]
        t = cc * cu
        t_dn, t_up = _shift_rows(t, rows, True), _shift_rows(t, rows, False)
        c3 = t_dn * w_ref[0:1, :] + t * w_ref[1:2, :] + t_up * w_ref[2:3, :]
        d = d_ref[...]
        dcb_ref[...] = (d * c3).astype(BF16)
        dc3 = d * cb_ref[...]
        dw_ref[0:1, :] = jnp.sum(dc3 * t_dn, axis=0, keepdims=True)
        dw_ref[1:2, :] = jnp.sum(dc3 * t, axis=0, keepdims=True)
        dw_ref[2:3, :] = jnp.sum(dc3 * t_up, axis=0, keepdims=True)
        dt = _shift_rows(dc3, rows, False) * w_ref[0:1, :] + dc3 * w_ref[1:2, :] + _shift_rows(dc3, rows, True) * w_ref[2:3, :]
        dcc_ref[...] = (dt * cu).astype(BF16)
        dcu_ref[...] = (dt * cc).astype(BF16)

    return pl.pallas_call(
        body, name="conv_bwd", grid=(CONV_WIDTH // CONV_CHUNK,),
        in_specs=[cu_spec, cb_spec, cc_spec, w_spec, chunk],
        out_specs=[chunk, chunk, chunk, w_spec],
        out_shape=[jax.ShapeDtypeStruct((s, CONV_WIDTH), BF16)] * 3 + [jax.ShapeDtypeStruct((3, CONV_WIDTH), F32)],
        compiler_params=_params(1),
    )(z, z, z, conv_w, d_co)


GATE_CHUNK = 512


def _gate_specs(s, d, tr):
    n_chunks = d // GATE_CHUNK
    za = pl.BlockSpec((tr, GATE_CHUNK), lambda j, i: (i, GL_OFF // GATE_CHUNK + j))
    zc = pl.BlockSpec((tr, GATE_CHUNK), lambda j, i: (i, GL_OFF // GATE_CHUNK + n_chunks + j))
    ba = pl.BlockSpec((1, GATE_CHUNK), lambda j, i: (0, j))
    bc = pl.BlockSpec((1, GATE_CHUNK), lambda j, i: (0, n_chunks + j))
    tile = pl.BlockSpec((tr, GATE_CHUNK), lambda j, i: (i, j))
    return za, zc, ba, bc, tile


def _gate_fwd(z, b_gate, ya, yc):
    s, d = ya.shape
    tr = _row_tile(s, 512)
    za, zc, ba, bc, tile = _gate_specs(s, d, tr)

    def body(za_ref, zc_ref, ba_ref, bc_ref, ya_ref, yc_ref, o_ref):
        ga = jax.nn.sigmoid(za_ref[...] + ba_ref[...])
        gc = jax.nn.sigmoid(zc_ref[...] + bc_ref[...])
        o_ref[...] = (ga * ya_ref[...] + gc * yc_ref[...]).astype(BF16)

    return pl.pallas_call(
        body, name="gate_fwd", grid=(d // GATE_CHUNK, s // tr),
        in_specs=[za, zc, ba, bc, tile, tile],
        out_specs=tile,
        out_shape=jax.ShapeDtypeStruct((s, d), BF16),
        compiler_params=_params(2),
    )(z, z, b_gate, b_gate, ya, yc)


def _gate_bwd(z, b_gate, ya, yc, dmix):
    s, d = ya.shape
    tr = _row_tile(s, 512)
    za, zc, ba, bc, tile = _gate_specs(s, d, tr)
    vec = pl.BlockSpec((1, GATE_CHUNK), lambda j, i: (0, j))

    def body(za_ref, zc_ref, ba_ref, bc_ref, ya_ref, yc_ref, dm_ref, dya_ref, dyc_ref, dla_ref, dlc_ref, dba_ref, dbc_ref):
        ga = jax.nn.sigmoid(za_ref[...] + ba_ref[...])
        gc = jax.nn.sigmoid(zc_ref[...] + bc_ref[...])
        dm = dm_ref[...]
        dya_ref[...] = (dm * ga).astype(BF16)
        dyc_ref[...] = (dm * gc).astype(BF16)
        dla = dm * ya_ref[...] * ga * (1.0 - ga)
        dlc = dm * yc_ref[...] * gc * (1.0 - gc)
        dla_ref[...] = dla.astype(BF16)
        dlc_ref[...] = dlc.astype(BF16)
        pa = jnp.sum(dla, axis=0, keepdims=True)
        pc = jnp.sum(dlc, axis=0, keepdims=True)

        @pl.when(pl.program_id(1) == 0)
        def _():
            dba_ref[...] = pa
            dbc_ref[...] = pc

        @pl.when(pl.program_id(1) > 0)
        def _():
            dba_ref[...] += pa
            dbc_ref[...] += pc

    big = jax.ShapeDtypeStruct((s, d), BF16)
    small = jax.ShapeDtypeStruct((1, d), F32)
    return pl.pallas_call(
        body, name="gate_bwd", grid=(d // GATE_CHUNK, s // tr),
        in_specs=[za, zc, ba, bc, tile, tile, tile],
        out_specs=[tile, tile, tile, tile, vec, vec],
        out_shape=[big, big, big, big, small, small],
        compiler_params=_params(2),
    )(z, z, b_gate, b_gate, ya, yc, dmix)


def _cross_probs(q_ref, kv_ref, hd):
    cols = slice(hd * HEAD_DIM, (hd + 1) * HEAD_DIM)
    qh = q_ref[:, cols]
    kh = kv_ref[:, cols]
    sc = lax.dot_general(qh, kh, (((1,), (1,)), ((), ())), preferred_element_type=F32) * ATTN_SCALE
    e = jnp.exp(sc - jnp.max(sc, axis=1, keepdims=True))
    return qh, kh, e * (1.0 / jnp.sum(e, axis=1, keepdims=True))


def _cross_fwd(qc, kvc):
    s = qc.shape[0]
    n_mem = kvc.shape[0]
    tq = _row_tile(s, 256)

    def body(q_ref, kv_ref, o_ref):
        for hd in range(MEM_HEADS):
            _, _, p = _cross_probs(q_ref, kv_ref, hd)
            vh = kv_ref[:, MEM_WIDTH + hd * HEAD_DIM:MEM_WIDTH + (hd + 1) * HEAD_DIM]
            o_ref[:, hd * HEAD_DIM:(hd + 1) * HEAD_DIM] = jnp.dot(p.astype(BF16), vh, preferred_element_type=F32).astype(BF16)

    return pl.pallas_call(
        body, name="cross_fwd", grid=(s // tq,),
        in_specs=[pl.BlockSpec((tq, MEM_WIDTH), lambda i: (i, 0)), pl.BlockSpec((n_mem, 2 * MEM_WIDTH), lambda i: (0, 0))],
        out_specs=pl.BlockSpec((tq, MEM_WIDTH), lambda i: (i, 0)),
        out_shape=jax.ShapeDtypeStruct((s, MEM_WIDTH), BF16),
        compiler_params=_params(1),
    )(qc, kvc)


def _cross_bwd(qc, kvc, d_out):
    s = qc.shape[0]
    n_mem = kvc.shape[0]
    tq = _row_tile(s, 256)

    def body(q_ref, kv_ref, do_ref, dq_ref, dkv_ref):
        @pl.when(pl.program_id(0) == 0)
        def _():
            dkv_ref[...] = jnp.zeros_like(dkv_ref)

        for hd in range(MEM_HEADS):
            cols = slice(hd * HEAD_DIM, (hd + 1) * HEAD_DIM)
            vcols = slice(MEM_WIDTH + hd * HEAD_DIM, MEM_WIDTH + (hd + 1) * HEAD_DIM)
            qh, kh, p = _cross_probs(q_ref, kv_ref, hd)
            doh = do_ref[:, cols]
            dp = lax.dot_general(doh, kv_ref[:, vcols], (((1,), (1,)), ((), ())), preferred_element_type=F32)
            ds = (p * (dp - jnp.sum(p * dp, axis=1, keepdims=True)) * ATTN_SCALE).astype(BF16)
            dq_ref[:, cols] = jnp.dot(ds, kh, preferred_element_type=F32).astype(BF16)
            dkv_ref[:, cols] += lax.dot_general(ds, qh, (((0,), (0,)), ((), ())), preferred_element_type=F32)
            dkv_ref[:, vcols] += lax.dot_general(p.astype(BF16), doh, (((0,), (0,)), ((), ())), preferred_element_type=F32)

    qspec = pl.BlockSpec((tq, MEM_WIDTH), lambda i: (i, 0))
    kvspec = pl.BlockSpec((n_mem, 2 * MEM_WIDTH), lambda i: (0, 0))
    return pl.pallas_call(
        body, name="cross_bwd", grid=(s // tq,),
        in_specs=[qspec, kvspec, qspec],
        out_specs=[qspec, kvspec],
        out_shape=[jax.ShapeDtypeStruct((s, MEM_WIDTH), BF16), jax.ShapeDtypeStruct((n_mem, 2 * MEM_WIDTH), F32)],
        compiler_params=_params(1),
    )(qc, kvc, d_out)


def _swiglu_fwd(up, gate):
    return up, (gate * jax.nn.sigmoid(gate)) * up


def _swiglu_bwd(d_act, gate, up):
    sg = jax.nn.sigmoid(gate)
    silu = gate * sg
    return d_act * up * (sg * (1.0 + gate * (1.0 - sg))), d_act * silu


GATHER_GROUPS = {"in": ("w_in", "conv_w"), "mix": ("w_attn_out", "w_conv_out", "w_o"), "cross": ("w_cq", "w_ckv", "w_co"),
                 "gate": ("w_gate",), "up": ("w_up",), "down": ("w_down",)}


def _local_step(xs, mems, target, small, fetch, reduce):
    s, d = xs.shape
    w4 = {}
    cos_t, sin_t = _rope_tables(s)

    h = _rmsnorm(xs, small["g_mix"], "norm_mix")
    w4.update(fetch.get("in", h))
    conv4 = w4["conv_w"]
    conv_w = conv4[:, :3, :].transpose(1, 0, 2).reshape(3, N_CHIPS * conv4.shape[2])
    c_in = w4["w_in"].shape[2]
    z = _matmul(h, w4["w_in"], mode="nn", tm=512, tn=c_in, tk=d, out_dtypes=[F32], name="in_proj", b_blocks=N_CHIPS)
    fetch.begin("mix", z)
    q_rot, k_rot, v_b = _rope_fwd(z, cos_t, sin_t)
    attn = _swa_fwd(q_rot, k_rot, v_b, small["sink"])
    co = _conv_fwd(z, conv_w)
    w4.update(fetch.get("mix", co))
    tok = fetch.begin("cross", attn)
    w_o = w4["w_o"].reshape(-1, w4["w_o"].shape[-1])
    c_d = w4["w_attn_out"].shape[2]
    ya = _matmul(attn, w4["w_attn_out"], mode="nn", tm=1024, tn=c_d, tk=ATTN_WIDTH, out_dtypes=[F32], name="attn_out_proj",
                 b_blocks=N_CHIPS, after=tok)
    yc = _matmul(co, w4["w_conv_out"], mode="nn", tm=1024, tn=c_d, tk=CONV_WIDTH, out_dtypes=[F32], name="conv_out_proj",
                 b_blocks=N_CHIPS)
    mix = _gate_fwd(z, small["b_gate"], ya, yc)
    x1 = _matmul(mix, w_o, mode="nn", tm=512, tn=1024, tk=d, out_dtypes=[F32], name="mix_out_proj", extras=[xs],
                 epilogue=_add_residual)
    w4.update(fetch.get("cross", x1))
    tok = fetch.begin("gate", x1)
    w_cq = w4["w_cq"].reshape(-1, w4["w_cq"].shape[-1])
    w_ckv = w4["w_ckv"].reshape(-1, w4["w_ckv"].shape[-1])
    hc = _rmsnorm(x1, small["g_cross"], "norm_cross")
    memn = _rmsnorm(mems, small["g_mem"], "norm_mem")
    qc = _matmul(hc, w_cq, mode="nn", tm=1024, tn=MEM_WIDTH, tk=d, out_dtypes=[BF16], name="cross_q_proj", after=tok)
    kvc = _matmul(memn, w_ckv, mode="nn", tm=256, tn=2 * MEM_WIDTH, tk=d, out_dtypes=[BF16], name="cross_kv_proj")
    oc = _cross_fwd(qc, kvc)
    x2 = _matmul(oc, w4["w_co"], mode="nn", tm=1024, tn=c_d, tk=MEM_WIDTH, out_dtypes=[F32], name="cross_out_proj",
                 extras=[x1], epilogue=_add_residual, b_blocks=N_CHIPS)
    hf = _rmsnorm(x2, small["g_ffn"], "norm_ffn")
    w4.update(fetch.get("gate", hf))
    tok = fetch.begin("up", hf)
    c_ff = w4["w_gate"].shape[2]
    gate = _matmul(hf, w4["w_gate"], mode="nn", tm=512, tn=c_ff, tk=d, out_dtypes=[F32], name="ffn_gate_proj", b_blocks=N_CHIPS,
                   after=tok)
    w4.update(fetch.get("up", gate))
    up, act = _matmul(hf, w4["w_up"], mode="nn", tm=512, tn=c_ff, tk=d, out_dtypes=[F32, BF16], name="ffn_up_proj",
                      extras=[gate], epilogue=_swiglu_fwd, b_blocks=N_CHIPS)
    w4.update(fetch.get("down", act))
    w_down = w4["w_down"].reshape(-1, w4["w_down"].shape[-1])
    x3 = _matmul(act, w_down, mode="nn", tm=512, tn=512, tk=w_down.shape[0], out_dtypes=[F32], name="ffn_down_proj", extras=[x2],
                 epilogue=_add_residual)
    dx3, dx3b, sq, dg_final = _loss_head(x3, small["g_final"], target)

    da, du = _matmul(dx3b, w_down, mode="nt", tm=512, tn=c_ff, tk=d, out_dtypes=[BF16, BF16], name="ffn_down_bwd",
                     extras=[gate, up], epilogue=_swiglu_bwd)
    g_down = _matmul(act, dx3b, mode="tn", tm=c_ff, tn=1024, tk=s, out_dtypes=[BF16], name="ffn_down_wgrad")
    tok = reduce.add("down", {"w_down": g_down}, da)
    g_gate = _matmul(hf, da, mode="tn", tm=512, tn=c_ff, tk=s, out_dtypes=[BF16], name="ffn_gate_wgrad", out_blocks=N_CHIPS,
                     after=tok)
    tok = reduce.step("down", g_gate)
    g_up = _matmul(hf, du, mode="tn", tm=512, tn=c_ff, tk=s, out_dtypes=[BF16], name="ffn_up_wgrad", out_blocks=N_CHIPS, after=tok)
    tok = reduce.add("ffn", {"w_gate": g_gate, "w_up": g_up}, dx3b)
    dhf = _matmul(da, w4["w_gate"], mode="nt", tm=512, tn=1024, tk=N_CHIPS * c_ff, out_dtypes=[F32], name="ffn_gate_bwd", b_blocks=N_CHIPS,
                  after=tok)
    tok = reduce.step("ffn", dhf)
    dhf = _matmul(du, w4["w_up"], mode="nt", tm=512, tn=1024, tk=N_CHIPS * c_ff, out_dtypes=[F32], name="ffn_up_bwd", extras=[dhf],
                  epilogue=_add_residual, b_blocks=N_CHIPS, after=tok)
    tok = reduce.step("down", dhf)
    dx2, dx2b, dg_ffn = _rmsnorm_bwd(dhf, x2, small["g_ffn"], dx3, "norm_ffn_bwd")

    d_oc = _matmul(dx2b, w4["w_co"], mode="nt", tm=1024, tn=MEM_WIDTH, tk=d, out_dtypes=[BF16], name="cross_out_bwd",
                   b_blocks=N_CHIPS, after=tok)
    g_co = _matmul(oc, dx2b, mode="tn", tm=MEM_WIDTH, tn=c_d, tk=s, out_dtypes=[BF16], name="cross_out_wgrad", out_blocks=N_CHIPS)
    tok = reduce.step("down", g_co)
    dqc, dkvc = _cross_bwd(qc, kvc, d_oc)
    g_cq = _matmul(hc, dqc, mode="tn", tm=1024, tn=MEM_WIDTH, tk=s, out_dtypes=[BF16], name="cross_q_wgrad", after=tok)
    dhc = _matmul(dqc, w_cq, mode="nt", tm=1024, tn=1024, tk=MEM_WIDTH, out_dtypes=[F32], name="cross_q_bwd")
    g_ckv = _matmul(memn, dkvc, mode="tn", tm=1024, tn=2 * MEM_WIDTH, tk=mems.shape[0], out_dtypes=[BF16], name="cross_kv_wgrad")
    dmemn = _matmul(dkvc, w_ckv, mode="nt", tm=256, tn=1024, tk=2 * MEM_WIDTH, out_dtypes=[F32], name="cross_kv_bwd")
    _, _, dg_mem = _rmsnorm_bwd(dmemn, mems, small["g_mem"], None, "norm_mem_bwd")
    dx1, dx1b, dg_cross = _rmsnorm_bwd(dhc, x1, small["g_cross"], dx2, "norm_cross_bwd")

    dmix = _matmul(dx1b, w_o, mode="nt", tm=512, tn=1024, tk=d, out_dtypes=[F32], name="mix_out_bwd")
    tok = reduce.step("ffn", dmix)
    g_o = _matmul(mix, dx1b, mode="tn", tm=1024, tn=1024, tk=s, out_dtypes=[BF16], name="mix_out_wgrad", after=tok)
    tok = reduce.step("ffn", g_o)
    dya, dyc, dgl_a, dgl_c, db_a, db_c = _gate_bwd(z, small["b_gate"], ya, yc, dmix)
    d_attn = _matmul(dya, w4["w_attn_out"], mode="nt", tm=1024, tn=ATTN_WIDTH, tk=d, out_dtypes=[BF16], name="attn_out_bwd",
                     b_blocks=N_CHIPS, after=tok)
    g_ao = _matmul(attn, dya, mode="tn", tm=ATTN_WIDTH, tn=c_d, tk=s, out_dtypes=[BF16], name="attn_out_wgrad", out_blocks=N_CHIPS)
    d_co = _matmul(dyc, w4["w_conv_out"], mode="nt", tm=1024, tn=CONV_WIDTH, tk=d, out_dtypes=[F32], name="conv_out_bwd",
                   b_blocks=N_CHIPS)
    g_cvo = _matmul(co, dyc, mode="tn", tm=CONV_WIDTH, tn=c_d, tk=s, out_dtypes=[BF16], name="conv_out_wgrad", out_blocks=N_CHIPS)
    tok = reduce.add("mid", {"w_co": g_co, "w_cq": g_cq, "w_ckv": g_ckv, "w_o": g_o, "w_attn_out": g_ao, "w_conv_out": g_cvo}, d_co)
    dcu, dcb, dcc, d_conv_w = _conv_bwd(z, conv_w, d_co)
    dq_rot, dk_rot, dv, dsink = _swa_bwd(q_rot, k_rot, v_b, d_attn, small["sink"])
    tok = reduce.step("mid", dq_rot)
    dq, dk, dvb = _rope_bwd(dq_rot, dk_rot, dv, cos_t, sin_t)
    dz = jnp.concatenate([dq, dk, dvb, dcu, dcb, dcc, dgl_a, dgl_c], axis=1)
    g_in = _matmul(h, dz, mode="tn", tm=512, tn=c_in, tk=s, out_dtypes=[BF16], name="in_proj_wgrad", out_blocks=N_CHIPS, after=tok)
    tok = reduce.add("in", {"w_in": g_in}, dk)
    tok = reduce.step("mid", tok)
    tok = reduce.step("in", tok)
    dh = _matmul(dz, w4["w_in"], mode="nt", tm=512, tn=512, tk=N_CHIPS * c_in, out_dtypes=[F32], name="in_proj_bwd", b_blocks=N_CHIPS,
                 after=tok)
    tok = reduce.step("mid", dh)
    grad_x, _, dg_mix = _rmsnorm_bwd(dh, xs, small["g_mix"], dx1, "norm_mix_bwd")

    small_grads = {
        "g_mix": dg_mix, "sink": dsink[:, 0], "b_gate": jnp.concatenate([db_a, db_c], axis=1), "g_cross": dg_cross,
        "g_mem": dg_mem, "g_ffn": dg_ffn, "g_final": dg_final, "conv_w": d_conv_w,
    }
    return sq, grad_x, small_grads


def _pair_sum(g4, ra, core, name):
    nb, rs, cs = g4.shape
    rh = rs // 2
    tr = _row_tile(rh, 256)
    per = rh // tr

    def body(c_ref, g_ref, r_ref, o_ref):
        o_ref[...] = (g_ref[...].astype(F32) + r_ref[...].astype(F32)).astype(BF16)

    plain = pl.BlockSpec((None, tr, cs), lambda j, i, c: (j, i, 0))
    return pl.pallas_call(
        body, name=name,
        grid_spec=pltpu.PrefetchScalarGridSpec(
            num_scalar_prefetch=1, grid=(nb, per),
            in_specs=[pl.BlockSpec((None, tr, cs), lambda j, i, c: (j, c[0] * per + i, 0)), plain],
            out_specs=plain),
        out_shape=jax.ShapeDtypeStruct((nb, rh, cs), BF16),
        compiler_params=_params(2),
    )(core, g4, ra)


def _quad_sum(parts, rc, place, name):
    _, rh, cs = parts.shape
    tr = _row_tile(rh, 256)
    per = rh // tr

    def body(p_ref, own_ref, r_ref, o_ref):
        acc = own_ref[...].astype(F32)
        for j in range(rc.shape[0]):
            acc = acc + r_ref[j].astype(F32)
        o_ref[...] = acc

    return pl.pallas_call(
        body, name=name,
        grid_spec=pltpu.PrefetchScalarGridSpec(
            num_scalar_prefetch=1, grid=(per,),
            in_specs=[pl.BlockSpec((None, tr, cs), lambda i, p: (p[0], i, 0)),
                      pl.BlockSpec((rc.shape[0], tr, cs), lambda i, p: (0, i, 0))],
            out_specs=pl.BlockSpec((tr, cs), lambda i, p: (p[1] * per + i, 0))),
        out_shape=jax.ShapeDtypeStruct((2 * rh, cs), F32),
        compiler_params=_params(1),
    )(place, parts, rc)


def _cast_to_slot(w, place, dtype, name, after=None):
    rows, cols = w.shape
    tr = _row_tile(rows, 256)

    def body(p_ref, w_ref, *rest):
        o_ref = rest[-1]
        o_ref[...] = w_ref[...].astype(dtype)

    return pl.pallas_call(
        body, name=name,
        grid_spec=pltpu.PrefetchScalarGridSpec(
            num_scalar_prefetch=1, grid=(rows // tr,),
            in_specs=[pl.BlockSpec((tr, cols), lambda i, p: (i, 0))] + ([] if after is None else [ANY]),
            out_specs=pl.BlockSpec((None, tr, cols), lambda i, p: (p[0], i, 0))),
        out_shape=jax.ShapeDtypeStruct((N_CHIPS, rows, cols), dtype),
        compiler_params=_params(1),
    )(place, w, *([] if after is None else [after]))


def _adamw(w, g, m, v, name):
    rows, cols = w.shape
    tr = _row_tile(rows, 256)

    def body(w_ref, g_ref, m_ref, v_ref, go_ref, d_ref, nm_ref, nv_ref):
        gv = g_ref[...]
        go_ref[...] = gv
        nm = ADAM_B1 * m_ref[...] + (1.0 - ADAM_B1) * gv
        nv = ADAM_B2 * v_ref[...] + (1.0 - ADAM_B2) * (gv * gv)
        m_hat = nm / ADAM_C1
        v_hat = nv / ADAM_C2
        d_ref[...] = -ADAM_LR * (m_hat / (jnp.sqrt(v_hat) + ADAM_EPS) + ADAM_WD * w_ref[...])
        nm_ref[...] = nm
        nv_ref[...] = nv

    tile = pl.BlockSpec((tr, cols), lambda i: (i, 0))
    shape = jax.ShapeDtypeStruct((rows, cols), F32)
    return pl.pallas_call(
        body, name=name, grid=(rows // tr,),
        in_specs=[tile] * 4, out_specs=[tile] * 4, out_shape=[shape] * 4,
        compiler_params=_params(1),
    )(w, g, m, v)


def _mesh_pos():
    return lax.axis_index("x"), lax.axis_index("y"), lax.axis_index("c")


def _other_chips(x, y):
    return [(1 - x, y), (x, 1 - y), (1 - x, 1 - y)]


def _half_rows(ref, which):
    rh = ref.shape[-2] // 2
    return ref.at[pl.ds(which * rh, rh), :]


def _remote(src, dst, send_sems, recv_sems, sem, to):
    return pltpu.make_async_remote_copy(src_ref=src, dst_ref=dst, send_sem=send_sems.at[sem], recv_sem=recv_sems.at[sem],
                                        device_id=to, device_id_type=MESH)


HBM = pl.BlockSpec(memory_space=pltpu.HBM)
SEM = pl.BlockSpec(memory_space=pltpu.SEMAPHORE)
DATAFLOW_EFFECT = pltpu.SideEffectType.DATAFLOW_SIDE_EFFECTING


def _in_hbm(arrays):
    return [pltpu.with_memory_space_constraint(a, pltpu.HBM) for a in arrays]


def _hbm_like(arrays):
    return [pltpu.HBM(a.shape, a.dtype) for a in arrays]


def _gather_start(bufs, groups, name):
    n, ng = len(bufs), len(groups)

    def body(*refs):
        ins = refs[:n]
        send, recv, token = refs[n:n + ng], refs[n + ng:n + 2 * ng], refs[-1]
        x, y, c = _mesh_pos()
        me = 2 * x + y
        for g, members in enumerate(groups):
            for i, w in enumerate(members):
                mine = _half_rows(ins[w].at[me], c)
                for k, (px, py) in enumerate(_other_chips(x, y)):
                    _remote(mine, mine, send[g], recv[g], 3 * i + k, (px, py, c)).start()
        token[...] = jnp.zeros_like(token)

    sems = [pltpu.SemaphoreType.DMA((3 * len(m),)) for m in groups]
    outs = pl.pallas_call(
        body, name=name,
        in_specs=[HBM] * n, out_specs=[SEM] * (2 * ng) + [HBM] * n + [pl.BlockSpec(memory_space=pltpu.VMEM)],
        out_shape=sems + sems + _hbm_like(bufs) + [jax.ShapeDtypeStruct((8, 128), F32)],
        input_output_aliases={i: 2 * ng + i for i in range(n)},
        compiler_params=pltpu.CompilerParams(has_side_effects=DATAFLOW_EFFECT),
    )(*_in_hbm(bufs))
    return outs[:ng], outs[ng:2 * ng], outs[2 * ng:2 * ng + n], outs[-1]


def _gather_pass(bufs, send, recv, after, name):
    m = len(bufs)

    def body(*refs):
        ins, send_in, recv_in = refs[:m], refs[m], refs[m + 1]
        send_out, recv_out, token = refs[m + 3], refs[m + 4], refs[-1]
        x, y, c = _mesh_pos()
        for i in range(m):
            for k, (px, py) in enumerate(_other_chips(x, y)):
                landed = _half_rows(ins[i].at[2 * px + py], c)
                came = _remote(landed, landed, send_in, recv_in, 3 * i + k, (px, py, c))
                came.wait_recv()
                came.wait_send()
                _remote(landed, landed, send_out, recv_out, 3 * i + k, (x, y, 1 - c)).start()
        token[...] = jnp.zeros_like(token)

    sems = [pltpu.SemaphoreType.DMA((3 * m,))] * 2
    outs = pl.pallas_call(
        body, name=name,
        in_specs=[HBM] * m + [SEM, SEM, ANY], out_specs=[SEM, SEM] + [HBM] * m + [pl.BlockSpec(memory_space=pltpu.VMEM)],
        out_shape=sems + _hbm_like(bufs) + [jax.ShapeDtypeStruct((8, 128), F32)],
        input_output_aliases={i: 2 + i for i in range(m)},
        compiler_params=pltpu.CompilerParams(has_side_effects=DATAFLOW_EFFECT),
    )(*_in_hbm(bufs), send, recv, after)
    return outs[0], outs[1], outs[2:2 + m], outs[-1]


def _gather_done(bufs, send, recv, after, name):
    m = len(bufs)

    def body(*refs):
        ins, send_in, recv_in = refs[:m], refs[m], refs[m + 1]
        x, y, c = _mesh_pos()
        for i in range(m):
            for k, (px, py) in enumerate(_other_chips(x, y)):
                passed = _half_rows(ins[i].at[2 * px + py], 1 - c)
                came = _remote(passed, passed, send_in, recv_in, 3 * i + k, (x, y, 1 - c))
                came.wait_send()
                came.wait_recv()

    return pl.pallas_call(
        body, name=name,
        in_specs=[HBM] * m + [SEM, SEM, ANY], out_specs=[HBM] * m,
        out_shape=_hbm_like(bufs),
        input_output_aliases={i: i for i in range(m)},
        compiler_params=pltpu.CompilerParams(has_side_effects=DATAFLOW_EFFECT),
    )(*_in_hbm(bufs), send, recv, after)


class _Gather:
    def __init__(self, groups):
        self.groups = groups
        self.landing = {}
        self.passing = {}

    def start(self, slotted, group_names, name):
        names = [n for g in group_names for n in self.groups[g]]
        index = {n: i for i, n in enumerate(names)}
        members = [[index[n] for n in self.groups[g]] for g in group_names]
        send, recv, bufs, token = _gather_start([slotted[n] for n in names], members, name)
        for j, g in enumerate(group_names):
            self.landing[g] = (send[j], recv[j], [bufs[index[n]] for n in self.groups[g]])
        return token

    def begin(self, group, after):
        send, recv, bufs = self.landing.pop(group)
        send, recv, bufs, token = _gather_pass(bufs, send, recv, after, "gather_pass_" + group)
        self.passing[group] = (send, recv, bufs)
        return token

    def get(self, group, after):
        if group not in self.passing:
            self.begin(group, after)
        send, recv, bufs = self.passing.pop(group)
        return dict(zip(self.groups[group], _gather_done(bufs, send, recv, after, "gather_done_" + group)))


def _sibling_halves_copies(srcs, dsts, x, y, c):
    out = []
    for s_ref, d_ref in zip(srcs, dsts, strict=True):
        rh = s_ref.shape[1] // 2
        out.append((s_ref.at[:, pl.ds((1 - c) * rh, rh), :], d_ref, (x, y, 1 - c)))
    return out


def _chip_copies(srcs, dsts, x, y, c):
    out = []
    for s_ref, d_ref in zip(srcs, dsts, strict=True):
        for k, (px, py) in enumerate(_other_chips(x, y)):
            out.append((s_ref.at[2 * px + py], d_ref.at[k], (px, py, c)))
    return out


def _join_copies(srcs, dsts, x, y, c):
    out = []
    for s_ref in srcs:
        mine = _half_rows(s_ref, c)
        out.append((mine, mine, (x, y, 1 - c)))
    return out


def _exchange_start(copies_fn, n_copies, srcs, fresh, after, name):
    ns, nb = len(srcs), len(srcs) + len(fresh)

    def body(*refs):
        bufs, send, recv, token = refs[:nb], refs[nb + 1], refs[nb + 2], refs[-1]
        x, y, c = _mesh_pos()
        for i, (s_ref, d_ref, to) in enumerate(copies_fn(bufs[:ns], bufs[ns:] if fresh else bufs[:ns], x, y, c)):
            _remote(s_ref, d_ref, send, recv, i, to).start()
        token[...] = jnp.zeros_like(token)

    sems = [pltpu.SemaphoreType.DMA((n_copies,))] * 2
    outs = pl.pallas_call(
        body, name=name,
        in_specs=[HBM] * nb + [ANY], out_specs=[SEM, SEM] + [HBM] * nb + [pl.BlockSpec(memory_space=pltpu.VMEM)],
        out_shape=sems + _hbm_like(list(srcs) + list(fresh)) + [jax.ShapeDtypeStruct((8, 128), F32)],
        input_output_aliases={i: 2 + i for i in range(nb)},
        compiler_params=pltpu.CompilerParams(has_side_effects=DATAFLOW_EFFECT),
    )(*_in_hbm(list(srcs) + list(fresh)), after)
    return outs[0], outs[1], outs[2:2 + ns], outs[2 + ns:2 + nb], outs[-1]


def _exchange_done(copies_fn, srcs, fresh, send, recv, after, name):
    ns, nb = len(srcs), len(srcs) + len(fresh)

    def body(*refs):
        bufs, send_in, recv_in = refs[:nb], refs[nb], refs[nb + 1]
        x, y, c = _mesh_pos()
        for i, (s_ref, d_ref, to) in enumerate(copies_fn(bufs[:ns], bufs[ns:] if fresh else bufs[:ns], x, y, c)):
            came = _remote(s_ref, d_ref, send_in, recv_in, i, to)
            came.wait_send()
            came.wait_recv()

    outs = pl.pallas_call(
        body, name=name,
        in_specs=[HBM] * nb + [SEM, SEM, ANY], out_specs=[HBM] * nb,
        out_shape=_hbm_like(list(srcs) + list(fresh)),
        input_output_aliases={i: i for i in range(nb)},
        compiler_params=pltpu.CompilerParams(has_side_effects=DATAFLOW_EFFECT),
    )(*_in_hbm(list(srcs) + list(fresh)), send, recv, after)
    return outs[:ns], outs[ns:]


class _Reduce:
    def __init__(self, place, core, shards, mom_m, mom_v):
        self.place, self.core = place, core
        self.shards, self.mom_m, self.mom_v = shards, mom_m, mom_v
        self.state = {}
        self.results = {}

    def add(self, group, grads, after):
        names = list(grads)
        g4s = [g.reshape((N_CHIPS, -1, g.shape[-1])) if g.ndim == 2 else g for g in grads.values()]
        fresh = [lax.empty((N_CHIPS, g.shape[1] // 2, g.shape[2]), BF16) for g in g4s]
        send, recv, g4s, fresh, token = _exchange_start(_sibling_halves_copies, len(names), g4s, fresh, after,
                                                        "pair_start_" + group)
        self.state[group] = (0, names, send, recv, g4s, fresh)
        return token

    def step(self, group, after):
        stage, names, send, recv, srcs, fresh = self.state[group]
        if stage == 0:
            g4s, ras = _exchange_done(_sibling_halves_copies, srcs, fresh, send, recv, after, "pair_done_" + group)
            parts = [_pair_sum(g, r, self.core, "pair_sum_" + n) for g, r, n in zip(g4s, ras, names)]
            fresh = [lax.empty((N_CHIPS - 1,) + p.shape[1:], BF16) for p in parts]
            send, recv, parts, fresh, token = _exchange_start(_chip_copies, 3 * len(names), parts, fresh, self.core,
                                                              "chips_start_" + group)
            self.state[group] = (1, names, send, recv, parts, fresh)
            return token
        if stage == 1:
            parts, rcs = _exchange_done(_chip_copies, srcs, fresh, send, recv, after, "chips_done_" + group)
            wholes = [_quad_sum(p, r, self.place, "quad_sum_" + n) for p, r, n in zip(parts, rcs, names)]
            send, recv, wholes, _, token = _exchange_start(_join_copies, len(names), wholes, [], self.core, "join_start_" + group)
            self.state[group] = (2, names, send, recv, wholes, [])
            return token
        assert stage == 2
        wholes, _ = _exchange_done(_join_copies, srcs, [], send, recv, after, "join_done_" + group)
        for n, g in zip(names, wholes):
            self.results[n] = _adamw(self.shards[n], g, self.mom_m[n], self.mom_v[n], "adamw_" + n)
        del self.state[group]
        return self.results[names[-1]][1]


N_DEV = 8


def _all_reduce_small(v):
    def body(v_ref, o_ref, slots, send_sems, recv_sems):
        x, y, c = _mesh_pos()
        me = 4 * x + 2 * y + c
        slots[me] = v_ref[...]
        peers = []
        for r in range(1, N_DEV):
            fx, fy, fc = (r >> 2) & 1, (r >> 1) & 1, r & 1
            peers.append((x + fx - 2 * x * fx, y + fy - 2 * y * fy, c + fc - 2 * c * fc))
        sends = []
        for r, peer in enumerate(peers):
            cp = _remote(v_ref, slots.at[me], send_sems, recv_sems, r, peer)
            cp.start()
            sends.append(cp)
        for r, (px, py, pc) in enumerate(peers):
            landed = slots.at[4 * px + 2 * py + pc]
            _remote(landed, landed, send_sems, recv_sems, r, (px, py, pc)).wait_recv()
        for cp in sends:
            cp.wait_send()
        acc = slots[0]
        for i in range(1, N_DEV):
            acc = acc + slots[i]
        o_ref[...] = acc

    vm = pl.BlockSpec(memory_space=pltpu.VMEM)
    return pl.pallas_call(
        body, name="small_grads_all_reduce",
        in_specs=[vm], out_specs=vm,
        out_shape=jax.ShapeDtypeStruct(v.shape, v.dtype),
        scratch_shapes=[pltpu.VMEM((N_DEV,) + v.shape, v.dtype), pltpu.SemaphoreType.DMA((N_DEV - 1,)),
                        pltpu.SemaphoreType.DMA((N_DEV - 1,))],
    )(v)


MATRICES = ("w_in", "w_attn_out", "w_conv_out", "w_o", "w_cq", "w_ckv", "w_co", "w_gate", "w_up", "w_down")
VECTORS = ("g_mix", "b_gate", "g_cross", "g_mem", "g_ffn", "g_final", "conv_w", "sink")
WEIGHT_ORDER = ("g_mix", "w_in", "sink", "conv_w", "b_gate", "w_attn_out", "w_conv_out", "w_o", "g_cross", "g_mem", "w_cq",
                "w_ckv", "w_co", "g_ffn", "w_gate", "w_up", "w_down", "g_final")
CONV_PAD_ROWS = 16
SMALL_ROWS = 8


def _pack(pieces):
    flat = jnp.concatenate([p.reshape(-1) for p in pieces])
    lane_group = SMALL_ROWS * 128
    total = -(-flat.shape[0] // lane_group) * lane_group
    flat = jnp.pad(flat, (0, total - flat.shape[0]))
    return flat.reshape(SMALL_ROWS, total // SMALL_ROWS), [p.size for p in pieces]


def _unpack(packed, pieces):
    flat = packed.reshape(-1)
    out, off = [], 0
    for p in pieces:
        out.append(flat[off:off + p.size].reshape(p.shape))
        off += p.size
    return out


def kernel(x, mem, g_mix, w_in, sink, conv_w, b_gate, w_attn_out, w_conv_out, w_o, g_cross, g_mem, w_cq, w_ckv, w_co, g_ffn, w_gate, w_up, w_down, g_final, loss_target, m_g_mix, m_w_in, m_sink, m_conv_w, m_b_gate, m_w_attn_out, m_w_conv_out, m_w_o, m_g_cross, m_g_mem, m_w_cq, m_w_ckv, m_w_co, m_g_ffn, m_w_gate, m_w_up, m_w_down, m_g_final, v_g_mix, v_w_in, v_sink, v_conv_w, v_b_gate, v_w_attn_out, v_w_conv_out, v_w_o, v_g_cross, v_g_mem, v_w_cq, v_w_ckv, v_w_co, v_g_ffn, v_w_gate, v_w_up, v_w_down, v_g_final):
    given = dict(g_mix=g_mix, w_in=w_in, sink=sink, conv_w=conv_w, b_gate=b_gate, w_attn_out=w_attn_out, w_conv_out=w_conv_out,
                 w_o=w_o, g_cross=g_cross, g_mem=g_mem, w_cq=w_cq, w_ckv=w_ckv, w_co=w_co, g_ffn=g_ffn, w_gate=w_gate, w_up=w_up,
                 w_down=w_down, g_final=g_final)
    mom_m = dict(g_mix=m_g_mix, w_in=m_w_in, sink=m_sink, conv_w=m_conv_w, b_gate=m_b_gate, w_attn_out=m_w_attn_out,
                 w_conv_out=m_w_conv_out, w_o=m_w_o, g_cross=m_g_cross, g_mem=m_g_mem, w_cq=m_w_cq, w_ckv=m_w_ckv, w_co=m_w_co,
                 g_ffn=m_g_ffn, w_gate=m_w_gate, w_up=m_w_up, w_down=m_w_down, g_final=m_g_final)
    mom_v = dict(g_mix=v_g_mix, w_in=v_w_in, sink=v_sink, conv_w=v_conv_w, b_gate=v_b_gate, w_attn_out=v_w_attn_out,
                 w_conv_out=v_w_conv_out, w_o=v_w_o, g_cross=v_g_cross, g_mem=v_g_mem, w_cq=v_w_cq, w_ckv=v_w_ckv, w_co=v_w_co,
                 g_ffn=v_g_ffn, w_gate=v_w_gate, w_up=v_w_up, w_down=v_w_down, g_final=v_g_final)
    xs, mems, target = x[0], mem[0], loss_target[0]
    d_model = xs.shape[1]
    chip = 2 * lax.axis_index("x") + lax.axis_index("y")
    core = jnp.reshape(lax.axis_index("c"), (1,)).astype(jnp.int32)
    place = jnp.stack([chip, lax.axis_index("c")]).astype(jnp.int32)

    shards = {n: given[n][0] for n in MATRICES}
    conv_cols = conv_w.shape[2]
    conv_pad = jnp.pad(conv_w[0], ((0, CONV_PAD_ROWS - conv_w.shape[1]), (0, 0)))
    fetch = _Gather(GATHER_GROUPS)
    first = {"w_in": _cast_to_slot(shards["w_in"], place, BF16, "to_slot_w_in"),
             "conv_w": _cast_to_slot(conv_pad, place, F32, "to_slot_conv_w")}
    tok = fetch.start(first, ["in"], "gather_start_in")
    rest = {n: _cast_to_slot(shards[n], place, BF16, "to_slot_" + n, after=tok) for n in MATRICES if n != "w_in"}
    fetch.start(rest, [g for g in GATHER_GROUPS if g != "in"], "gather_start_rest")
    small = {n: given[n] for n in ("g_mix", "b_gate", "g_cross", "g_mem", "g_ffn")}
    small["g_final"] = g_final[None]
    small["sink"] = sink[0]

    reduce = _Reduce(place, core, shards, {n: mom_m[n][0] for n in MATRICES}, {n: mom_v[n][0] for n in MATRICES})
    sq, grad_x, small_grads = _local_step(xs, mems, target, small, fetch, reduce)

    loss_part = 0.5 * sq[0:1, 0:1] / d_model
    pieces = [small_grads[n] for n in VECTORS] + [loss_part]
    packed, _ = _pack(pieces)
    summed = _unpack(_all_reduce_small(packed), pieces)
    loss = summed[-1][0, 0]
    small_sum = dict(zip(VECTORS, summed[:-1]))
    small_sum["conv_w"] = lax.dynamic_slice_in_dim(small_sum["conv_w"], chip * conv_cols, conv_cols, axis=1)

    grad_out, delta, new_m, new_v = {}, {}, {}, {}
    like = [given[n] for n in VECTORS]
    pw, _ = _pack(like)
    pg, _ = _pack([small_sum[n] for n in VECTORS])
    pm, _ = _pack([mom_m[n] for n in VECTORS])
    pv, _ = _pack([mom_v[n] for n in VECTORS])
    _, pd, pnm, pnv = _adamw(pw, pg, pm, pv, "adamw_small")
    for n, g, d, nm, nv in zip(VECTORS, [small_sum[n] for n in VECTORS], _unpack(pd, like), _unpack(pnm, like), _unpack(pnv, like)):
        grad_out[n] = g.reshape(given[n].shape)
        delta[n], new_m[n], new_v[n] = d, nm, nv
    tok = reduce.step("in", pd)
    reduce.step("in", tok)
    for n in MATRICES:
        g, d, nm, nv = reduce.results[n]
        grad_out[n], delta[n], new_m[n], new_v[n] = g[None], d[None], nm[None], nv[None]

    return (loss, grad_x[None], *[grad_out[n] for n in WEIGHT_ORDER], *[delta[n] for n in WEIGHT_ORDER],
            *[new_m[n] for n in WEIGHT_ORDER], *[new_v[n] for n in WEIGHT_ORDER])
```

```python
import functools

import jax
import jax.numpy as jnp
from jax import lax
from jax.experimental import pallas as pl
from jax.experimental.pallas import tpu as pltpu

F32 = jnp.float32
BF16 = jnp.bfloat16
MESH = pl.DeviceIdType.MESH
ANY = pl.BlockSpec(memory_space=pl.ANY)

VMEM_LIMIT_BYTES = 56 * 1024 * 1024

N_CHIPS = 4
HEAD_DIM = 128
N_Q_HEADS = 8
N_KV_HEADS = 2
Q_GROUP = N_Q_HEADS // N_KV_HEADS
ATTN_WIDTH = N_Q_HEADS * HEAD_DIM
KV_WIDTH = N_KV_HEADS * HEAD_DIM
WINDOW = 128
BLOCK = 128
BAND = 3 * BLOCK
ROPE_THETA = 10000.0
CONV_WIDTH = 1024
MEM_HEADS = 4
MEM_WIDTH = MEM_HEADS * HEAD_DIM
RMS_EPS = 1e-6
NEG_INF = -1e30
ATTN_SCALE = HEAD_DIM ** -0.5

Q_OFF, K_OFF, V_OFF, CU_OFF, CB_OFF, CC_OFF, GL_OFF = 0, 1024, 1280, 1536, 2560, 3584, 4608

ADAM_LR = 0.001
ADAM_B1 = 0.9
ADAM_B2 = 0.999
ADAM_EPS = 1e-08
ADAM_WD = 0.01
ADAM_STEP = 10
ADAM_C1 = 1.0 - ADAM_B1 ** ADAM_STEP
ADAM_C2 = 1.0 - ADAM_B2 ** ADAM_STEP


def _params(n_grid_axes):
    return pltpu.CompilerParams(dimension_semantics=("arbitrary",) * n_grid_axes, vmem_limit_bytes=VMEM_LIMIT_BYTES)


BF16_SUBLANES = 16


def _row_tile(rows, want):
    if rows <= want:
        return rows
    for t in range(want, 0, -BF16_SUBLANES):
        if rows % t == 0:
            return t
    return rows


def _matmul(a, b, *, mode, tm, tn, tk, out_dtypes, name, extras=(), epilogue=None, b_blocks=1, out_blocks=1, after=None):
    if mode == "tn":
        kdim, m = a.shape
    else:
        m, kdim = a.shape
    if b_blocks > 1:
        nb, brows, bcols = b.shape
        assert nb == b_blocks
        if mode == "nn":
            n = bcols * nb
            assert brows == kdim
        else:
            assert mode == "nt" and bcols * nb == kdim
            n = brows
    else:
        n = b.shape[0] if mode == "nt" else b.shape[1]
    tm, tn, tk = min(tm, m), min(tn, n), min(tk, kdim)
    assert m % tm == 0 and n % tn == 0 and kdim % tk == 0, (name, m, n, kdim, tm, tn, tk)
    nk = kdim // tk
    n_extra, n_out = len(extras), len(out_dtypes)
    n_after = 0 if after is None else 1

    if mode == "tn":
        a_spec = pl.BlockSpec((tk, tm), lambda j, i, k: (k, i))
        dims = (((0,), (0,)), ((), ()))
    else:
        a_spec = pl.BlockSpec((tm, tk), lambda j, i, k: (i, k))
        dims = (((1,), (0,)), ((), ())) if mode == "nn" else (((1,), (1,)), ((), ()))

    if b_blocks > 1 and mode == "nn":
        per = b.shape[2] // tn
        assert b.shape[2] % tn == 0
        b_spec = pl.BlockSpec((None, tk, tn), lambda j, i, k: (j // per, k, j % per))
    elif b_blocks > 1 and nk == 1:
        b_spec = pl.BlockSpec((b_blocks, tn, b.shape[2]), lambda j, i, k: (0, j, 0))
    elif b_blocks > 1:
        per = b.shape[2] // tk
        assert b.shape[2] % tk == 0
        b_spec = pl.BlockSpec((None, tn, tk), lambda j, i, k: (k // per, j, k % per))
    elif mode == "nt":
        b_spec = pl.BlockSpec((tn, tk), lambda j, i, k: (j, k))
    else:
        b_spec = pl.BlockSpec((tk, tn), lambda j, i, k: (k, j))

    tile_spec = pl.BlockSpec((tm, tn), lambda j, i, k: (i, j))
    if out_blocks > 1:
        ncols = n // out_blocks
        assert ncols % tn == 0
        oper = ncols // tn
        out_spec = pl.BlockSpec((None, tm, tn), lambda j, i, k: (j // oper, i, j % oper))
        out_shape = [jax.ShapeDtypeStruct((out_blocks, m, ncols), dt) for dt in out_dtypes]
    else:
        out_spec = tile_spec
        out_shape = [jax.ShapeDtypeStruct((m, n), dt) for dt in out_dtypes]

    def body(a_ref, b_ref, *rest):
        extra_refs = rest[:n_extra]
        out_refs = rest[n_extra + n_after:n_extra + n_after + n_out]

        def finish(acc):
            if epilogue is None:
                tiles = (acc,)
            else:
                tiles = epilogue(acc, *[r[...] for r in extra_refs])
            for o_ref, t in zip(out_refs, tiles, strict=True):
                o_ref[...] = t.astype(o_ref.dtype)

        if b_blocks > 1 and mode == "nt" and nk == 1:
            cs = b.shape[2]
            part = None
            for jb in range(b_blocks):
                prod = lax.dot_general(a_ref[:, jb * cs:(jb + 1) * cs].astype(BF16), b_ref[jb].astype(BF16), dims,
                                       preferred_element_type=F32)
                part = prod if part is None else part + prod
        else:
            part = lax.dot_general(a_ref[...].astype(BF16), b_ref[...].astype(BF16), dims, preferred_element_type=F32)
        if nk == 1:
            finish(part)
        else:
            acc_ref = rest[-1]
            k = pl.program_id(2)

            @pl.when(k == 0)
            def _():
                acc_ref[...] = part

            @pl.when(k > 0)
            def _():
                acc_ref[...] += part

            @pl.when(k == nk - 1)
            def _():
                finish(acc_ref[...])

    outs = pl.pallas_call(
        body,
        name=name,
        grid=(n // tn, m // tm, nk),
        in_specs=[a_spec, b_spec] + [tile_spec] * n_extra + [ANY] * n_after,
        out_specs=[out_spec] * n_out,
        out_shape=out_shape,
        scratch_shapes=[pltpu.VMEM((tm, tn), F32)] if nk > 1 else [],
        compiler_params=_params(3),
    )(a, b, *extras, *([] if after is None else [after]))
    return outs[0] if n_out == 1 else outs


def _add_residual(acc, res):
    return (acc + res,)


def _wgrad_half(a, b, core, *, theirs, row_sharded, tm, tn, name, add=None, after=None):
    kdim, m = a.shape
    n = b.shape[1]
    rs, cs = (m // N_CHIPS, n) if row_sharded else (m, n // N_CHIPS)
    rh = rs // 2
    tm, tn = min(tm, rh), min(tn, cs)
    assert rh % tm == 0 and cs % tn == 0, (name, rh, cs, tm, tn)
    mh, per = rh // tm, cs // tn
    has_add = add is not None

    def half(c):
        return 1 - c[0] if theirs else c[0]

    if row_sharded:
        grid = (n // tn, N_CHIPS * mh)
        a_spec = pl.BlockSpec((kdim, tm), lambda j, r, c: (0, ((r // mh) * 2 + half(c)) * mh + r % mh))
        o_spec = pl.BlockSpec((None, tm, tn), lambda j, r, c: (r // mh, r % mh, j))
    else:
        grid = (n // tn, mh)
        a_spec = pl.BlockSpec((kdim, tm), lambda j, r, c: (0, half(c) * mh + r))
        o_spec = pl.BlockSpec((None, tm, tn), lambda j, r, c: (j // per, r, j % per))
    b_spec = pl.BlockSpec((kdim, tn), lambda j, r, c: (0, j))

    def body(c_ref, a_ref, b_ref, *rest):
        o_ref = rest[-1]
        acc = lax.dot_general(a_ref[...].astype(BF16), b_ref[...].astype(BF16), (((0,), (0,)), ((), ())),
                              preferred_element_type=F32)
        if has_add:
            acc = acc + rest[0][...].astype(F32)
        o_ref[...] = acc.astype(BF16)

    operands = [a, b] + ([add] if has_add else []) + ([] if after is None else [after])
    return pl.pallas_call(
        body, name=name,
        grid_spec=pltpu.PrefetchScalarGridSpec(
            num_scalar_prefetch=1, grid=grid,
            in_specs=[a_spec, b_spec] + ([o_spec] if has_add else []) + ([] if after is None else [ANY]),
            out_specs=o_spec),
        out_shape=jax.ShapeDtypeStruct((N_CHIPS, rh, cs), BF16),
        compiler_params=_params(2),
    )(core, *operands)


def _rstd(x):
    return lax.rsqrt(jnp.mean(x * x, axis=-1, keepdims=True) + RMS_EPS)


def _rmsnorm(x, g, name):
    s, d = x.shape
    tr = _row_tile(s, 256)

    def body(x_ref, g_ref, o_ref):
        xv = x_ref[...]
        o_ref[...] = (xv * _rstd(xv) * g_ref[...]).astype(BF16)

    return pl.pallas_call(
        body, name=name, grid=(s // tr,),
        in_specs=[pl.BlockSpec((tr, d), lambda i: (i, 0)), pl.BlockSpec((1, d), lambda i: (0, 0))],
        out_specs=pl.BlockSpec((tr, d), lambda i: (i, 0)),
        out_shape=jax.ShapeDtypeStruct((s, d), BF16),
        compiler_params=_params(1),
    )(x, g)


def _rmsnorm_bwd(dh, x, g, dres, name):
    s, d = x.shape
    tr = _row_tile(s, 256)
    has_res = dres is not None

    def body(*refs):
        if has_res:
            dh_ref, x_ref, g_ref, res_ref, dx_ref, dxb_ref, dg_ref = refs
        else:
            dh_ref, x_ref, g_ref, dx_ref, dxb_ref, dg_ref = refs
        xv = x_ref[...]
        dhv = dh_ref[...].astype(F32)
        r = _rstd(xv)
        xn = xv * r
        dhg = dhv * g_ref[...]
        dx = r * (dhg - xn * jnp.mean(dhg * xn, axis=-1, keepdims=True))
        if has_res:
            dx = dx + res_ref[...]
        dx_ref[...] = dx
        dxb_ref[...] = dx.astype(BF16)
        part = jnp.sum(dhv * xn, axis=0, keepdims=True)

        @pl.when(pl.program_id(0) == 0)
        def _():
            dg_ref[...] = part

        @pl.when(pl.program_id(0) > 0)
        def _():
            dg_ref[...] += part

    row = pl.BlockSpec((tr, d), lambda i: (i, 0))
    vec = pl.BlockSpec((1, d), lambda i: (0, 0))
    return pl.pallas_call(
        body, name=name, grid=(s // tr,),
        in_specs=[row, row, vec] + ([row] if has_res else []),
        out_specs=[row, row, vec],
        out_shape=[jax.ShapeDtypeStruct((s, d), F32), jax.ShapeDtypeStruct((s, d), BF16), jax.ShapeDtypeStruct((1, d), F32)],
        compiler_params=_params(1),
    )(*([dh, x, g] + ([dres] if has_res else [])))


def _loss_head(x3, g, target):
    s, d = x3.shape
    tr = _row_tile(s, 256)

    def body(x_ref, g_ref, t_ref, dx_ref, dxb_ref, sq_ref, dg_ref):
        xv = x_ref[...]
        gv = g_ref[...]
        r = _rstd(xv)
        xn = xv * r
        err = xn * gv - t_ref[...]
        dy = err * (1.0 / d)
        dyg = dy * gv
        dx = r * (dyg - xn * jnp.mean(dyg * xn, axis=-1, keepdims=True))
        dx_ref[...] = dx
        dxb_ref[...] = dx.astype(BF16)
        sq = jnp.sum(jnp.sum(err * err, axis=1, keepdims=True), axis=0, keepdims=True)
        sq = jnp.broadcast_to(sq, (1, 128))
        part = jnp.sum(dy * xn, axis=0, keepdims=True)

        @pl.when(pl.program_id(0) == 0)
        def _():
            sq_ref[...] = sq
            dg_ref[...] = part

        @pl.when(pl.program_id(0) > 0)
        def _():
            sq_ref[...] += sq
            dg_ref[...] += part

    row = pl.BlockSpec((tr, d), lambda i: (i, 0))
    vec = pl.BlockSpec((1, d), lambda i: (0, 0))
    return pl.pallas_call(
        body, name="loss_head", grid=(s // tr,),
        in_specs=[row, vec, row],
        out_specs=[row, row, pl.BlockSpec((1, 128), lambda i: (0, 0)), vec],
        out_shape=[jax.ShapeDtypeStruct((s, d), F32), jax.ShapeDtypeStruct((s, d), BF16),
                   jax.ShapeDtypeStruct((1, 128), F32), jax.ShapeDtypeStruct((1, d), F32)],
        compiler_params=_params(1),
    )(x3, g, target)


def _rope_tables(s):
    inv = 1.0 / (ROPE_THETA ** (jnp.arange(0, HEAD_DIM, 2, dtype=F32) / HEAD_DIM))
    ang = jnp.arange(s, dtype=F32)[:, None] * inv[None, :]
    cos, sin = jnp.cos(ang), jnp.sin(ang)
    return jnp.concatenate([cos, cos], axis=1), jnp.concatenate([-sin, sin], axis=1)


def _swap_halves(t):
    return pltpu.roll(t, HEAD_DIM // 2, 1)


def _rope_fwd(z, cos_t, sin_t):
    s = z.shape[0]
    tr = _row_tile(s, 256)

    def body(zq_ref, zk_ref, zv_ref, c_ref, s_ref, q_ref, k_ref, v_ref):
        c, sn = c_ref[...], s_ref[...]
        for hd in range(N_Q_HEADS):
            cols = slice(hd * HEAD_DIM, (hd + 1) * HEAD_DIM)
            t = zq_ref[:, cols]
            q_ref[:, cols] = (t * c + _swap_halves(t) * sn).astype(BF16)
        for hd in range(N_KV_HEADS):
            cols = slice(hd * HEAD_DIM, (hd + 1) * HEAD_DIM)
            t = zk_ref[:, cols]
            k_ref[:, cols] = (t * c + _swap_halves(t) * sn).astype(BF16)
        v_ref[...] = zv_ref[...].astype(BF16)

    tab = pl.BlockSpec((tr, HEAD_DIM), lambda i: (i, 0))
    return pl.pallas_call(
        body, name="rope_fwd", grid=(s // tr,),
        in_specs=[pl.BlockSpec((tr, ATTN_WIDTH), lambda i: (i, Q_OFF // ATTN_WIDTH)),
                  pl.BlockSpec((tr, KV_WIDTH), lambda i: (i, K_OFF // KV_WIDTH)),
                  pl.BlockSpec((tr, KV_WIDTH), lambda i: (i, V_OFF // KV_WIDTH)), tab, tab],
        out_specs=[pl.BlockSpec((tr, ATTN_WIDTH), lambda i: (i, 0)), pl.BlockSpec((tr, KV_WIDTH), lambda i: (i, 0)),
                   pl.BlockSpec((tr, KV_WIDTH), lambda i: (i, 0))],
        out_shape=[jax.ShapeDtypeStruct((s, ATTN_WIDTH), BF16), jax.ShapeDtypeStruct((s, KV_WIDTH), BF16),
                   jax.ShapeDtypeStruct((s, KV_WIDTH), BF16)],
        compiler_params=_params(1),
    )(z, z, z, cos_t, sin_t)


def _rope_bwd(dq_rot, dk_rot, dv, cos_t, sin_t):
    s = dq_rot.shape[0]
    tr = _row_tile(s, 256)

    def body(dq_ref, dk_ref, dv_ref, c_ref, s_ref, oq_ref, ok_ref, ov_ref):
        c, sn = c_ref[...], s_ref[...]
        for hd in range(N_Q_HEADS):
            cols = slice(hd * HEAD_DIM, (hd + 1) * HEAD_DIM)
            t = dq_ref[:, cols]
            oq_ref[:, cols] = (t * c + _swap_halves(t * sn)).astype(BF16)
        for hd in range(N_KV_HEADS):
            cols = slice(hd * HEAD_DIM, (hd + 1) * HEAD_DIM)
            t = dk_ref[:, cols]
            ok_ref[:, cols] = (t * c + _swap_halves(t * sn)).astype(BF16)
        ov_ref[...] = dv_ref[...].astype(BF16)

    tab = pl.BlockSpec((tr, HEAD_DIM), lambda i: (i, 0))
    wide = pl.BlockSpec((tr, ATTN_WIDTH), lambda i: (i, 0))
    narrow = pl.BlockSpec((tr, KV_WIDTH), lambda i: (i, 0))
    return pl.pallas_call(
        body, name="rope_bwd", grid=(s // tr,),
        in_specs=[wide, narrow, narrow, tab, tab],
        out_specs=[wide, narrow, narrow],
        out_shape=[jax.ShapeDtypeStruct((s, ATTN_WIDTH), BF16), jax.ShapeDtypeStruct((s, KV_WIDTH), BF16),
                   jax.ShapeDtypeStruct((s, KV_WIDTH), BF16)],
        compiler_params=_params(1),
    )(dq_rot, dk_rot, dv, cos_t, sin_t)


def _swa_band(i, s):
    return pl.multiple_of(jnp.clip((i - 1) * BLOCK, 0, s - BAND), BLOCK)


def _swa_probs(q_ref, k_ref, sink_ref, kv, start, valid):
    cols = slice(kv * HEAD_DIM, (kv + 1) * HEAD_DIM)
    kb = k_ref[pl.ds(start, BAND), cols]
    heads = [kv * Q_GROUP + g for g in range(Q_GROUP)]
    qg = jnp.concatenate([q_ref[:, hd * HEAD_DIM:(hd + 1) * HEAD_DIM] for hd in heads], axis=0)
    sc = lax.dot_general(qg, kb, (((1,), (1,)), ((), ())), preferred_element_type=F32) * ATTN_SCALE
    sc = jnp.where(valid, sc, NEG_INF)
    sk = jnp.concatenate([jnp.full((BLOCK, 1), sink_ref[hd], F32) for hd in heads], axis=0)
    mx = jnp.maximum(jnp.max(sc, axis=1, keepdims=True), sk)
    e = jnp.exp(sc - mx)
    es = jnp.exp(sk - mx)
    inv = 1.0 / (jnp.sum(e, axis=1, keepdims=True) + es)
    return qg, kb, e * inv, es * inv


def _swa_valid(i, start):
    q_pos = i * BLOCK + lax.broadcasted_iota(jnp.int32, (BLOCK, 1), 0)
    q_pos = jnp.concatenate([q_pos] * Q_GROUP, axis=0)
    k_pos = start + lax.broadcasted_iota(jnp.int32, (1, BAND), 1)
    return jnp.abs(k_pos - q_pos) <= WINDOW


def _swa_fwd(q, k, v, sink):
    s = q.shape[0]
    assert s % BLOCK == 0 and s >= BAND

    def body(sink_ref, q_ref, k_ref, v_ref, o_ref):
        i = pl.program_id(0)
        start = _swa_band(i, s)
        valid = _swa_valid(i, start)
        for kv in range(N_KV_HEADS):
            _, _, p, _ = _swa_probs(q_ref, k_ref, sink_ref, kv, start, valid)
            vb = v_ref[pl.ds(start, BAND), kv * HEAD_DIM:(kv + 1) * HEAD_DIM]
            o = jnp.dot(p.astype(BF16), vb, preferred_element_type=F32)
            for g in range(Q_GROUP):
                hd = kv * Q_GROUP + g
                o_ref[:, hd * HEAD_DIM:(hd + 1) * HEAD_DIM] = o[g * BLOCK:(g + 1) * BLOCK].astype(BF16)

    whole = pl.BlockSpec((s, KV_WIDTH), lambda i: (0, 0))
    blk = pl.BlockSpec((BLOCK, ATTN_WIDTH), lambda i: (i, 0))
    return pl.pallas_call(
        body, name="swa_fwd", grid=(s // BLOCK,),
        in_specs=[pl.BlockSpec(memory_space=pltpu.SMEM), blk, whole, whole],
        out_specs=blk,
        out_shape=jax.ShapeDtypeStruct((s, ATTN_WIDTH), BF16),
        compiler_params=_params(1),
    )(sink, q, k, v)


def _swa_bwd(q, k, v, d_out, sink):
    s = q.shape[0]

    def body(sink_ref, q_ref, k_ref, v_ref, do_ref, dq_ref, dk_ref, dv_ref, dsink_ref):
        i = pl.program_id(0)

        @pl.when(i == 0)
        def _():
            dk_ref[...] = jnp.zeros_like(dk_ref)
            dv_ref[...] = jnp.zeros_like(dv_ref)
            dsink_ref[...] = jnp.zeros_like(dsink_ref)

        start = _swa_band(i, s)
        valid = _swa_valid(i, start)
        for kv in range(N_KV_HEADS):
            cols = slice(kv * HEAD_DIM, (kv + 1) * HEAD_DIM)
            qg, kb, p, p_sink = _swa_probs(q_ref, k_ref, sink_ref, kv, start, valid)
            vb = v_ref[pl.ds(start, BAND), cols]
            heads = [kv * Q_GROUP + g for g in range(Q_GROUP)]
            dog = jnp.concatenate([do_ref[:, hd * HEAD_DIM:(hd + 1) * HEAD_DIM] for hd in heads], axis=0)
            dp = lax.dot_general(dog, vb, (((1,), (1,)), ((), ())), preferred_element_type=F32)
            delta = jnp.sum(p * dp, axis=1, keepdims=True)
            ds = (p * (dp - delta) * ATTN_SCALE).astype(BF16)
            dqg = jnp.dot(ds, kb, preferred_element_type=F32)
            dk_ref[pl.ds(start, BAND), cols] += lax.dot_general(ds, qg, (((0,), (0,)), ((), ())), preferred_element_type=F32)
            dv_ref[pl.ds(start, BAND), cols] += lax.dot_general(p.astype(BF16), dog, (((0,), (0,)), ((), ())),
                                                                 preferred_element_type=F32)
            dsk = p_sink * delta
            for g, hd in enumerate(heads):
                dq_ref[:, hd * HEAD_DIM:(hd + 1) * HEAD_DIM] = dqg[g * BLOCK:(g + 1) * BLOCK]
                tot = jnp.sum(dsk[g * BLOCK:(g + 1) * BLOCK], axis=0, keepdims=True)
                dsink_ref[hd:hd + 1, :] -= jnp.broadcast_to(tot, (1, 128))

    whole = pl.BlockSpec((s, KV_WIDTH), lambda i: (0, 0))
    blk = pl.BlockSpec((BLOCK, ATTN_WIDTH), lambda i: (i, 0))
    return pl.pallas_call(
        body, name="swa_bwd", grid=(s // BLOCK,),
        in_specs=[pl.BlockSpec(memory_space=pltpu.SMEM), blk, whole, whole, blk],
        out_specs=[blk, whole, whole, pl.BlockSpec((N_Q_HEADS, 128), lambda i: (0, 0))],
        out_shape=[jax.ShapeDtypeStruct((s, ATTN_WIDTH), F32), jax.ShapeDtypeStruct((s, KV_WIDTH), F32),
                   jax.ShapeDtypeStruct((s, KV_WIDTH), F32), jax.ShapeDtypeStruct((N_Q_HEADS, 128), F32)],
        compiler_params=_params(1),
    )(sink, q, k, v, d_out)


CONV_CHUNK = 256


def _shift_rows(t, rows, down):
    n = t.shape[0]
    rolled = pltpu.roll(t, 1 if down else n - 1, 0)
    edge = 0 if down else n - 1
    return jnp.where(rows == edge, 0.0, rolled)


def _conv_specs(s):
    def z_spec(off):
        return pl.BlockSpec((s, CONV_CHUNK), lambda j, off=off: (0, off // CONV_CHUNK + j))
    chunk = pl.BlockSpec((s, CONV_CHUNK), lambda j: (0, j))
    w_spec = pl.BlockSpec((3, CONV_CHUNK), lambda j: (0, j))
    return z_spec(CU_OFF), z_spec(CB_OFF), z_spec(CC_OFF), chunk, w_spec


def _conv_fwd(z, conv_w):
    s = z.shape[0]
    cu_spec, cb_spec, cc_spec, chunk, w_spec = _conv_specs(s)

    def body(cu_ref, cb_ref, cc_ref, w_ref, o_ref):
        rows = lax.broadcasted_iota(jnp.int32, (s, 1), 0)
        t = cc_ref[...] * cu_ref[...]
        c3 = _shift_rows(t, rows, True) * w_ref[0:1, :] + t * w_ref[1:2, :] + _shift_rows(t, rows, False) * w_ref[2:3, :]
        o_ref[...] = (cb_ref[...] * c3).astype(BF16)

    return pl.pallas_call(
        body, name="conv_fwd", grid=(CONV_WIDTH // CONV_CHUNK,),
        in_specs=[cu_spec, cb_spec, cc_spec, w_spec],
        out_specs=chunk,
        out_shape=jax.ShapeDtypeStruct((s, CONV_WIDTH), BF16),
        compiler_params=_params(1),
    )(z, z, z, conv_w)


def _conv_bwd(z, conv_w, d_co):
    s = z.shape[0]
    cu_spec, cb_spec, cc_spec, chunk, w_spec = _conv_specs(s)

    def body(cu_ref, cb_ref, cc_ref, w_ref, d_ref, dcu_ref, dcb_ref, dcc_ref, dw_ref):
        rows = lax.broadcasted_iota(jnp.int32, (s, 1), 0)
        cu, cc = cu_ref[...], cc_ref[...]
        t = cc * cu
        t_dn, t_up = _shift_rows(t, rows, True), _shift_rows(t, rows, False)
        c3 = t_dn * w_ref[0:1, :] + t * w_ref[1:2, :] + t_up * w_ref[2:3, :]
        d = d_ref[...]
        dcb_ref[...] = (d * c3).astype(BF16)
        dc3 = d * cb_ref[...]
        dw_ref[0:1, :] = jnp.sum(dc3 * t_dn, axis=0, keepdims=True)
        dw_ref[1:2, :] = jnp.sum(dc3 * t, axis=0, keepdims=True)
        dw_ref[2:3, :] = jnp.sum(dc3 * t_up, axis=0, keepdims=True)
        dt = _shift_rows(dc3, rows, False) * w_ref[0:1, :] + dc3 * w_ref[1:2, :] + _shift_rows(dc3, rows, True) * w_ref[2:3, :]
        dcc_ref[...] = (dt * cu).astype(BF16)
        dcu_ref[...] = (dt * cc).astype(BF16)

    return pl.pallas_call(
        body, name="conv_bwd", grid=(CONV_WIDTH // CONV_CHUNK,),
        in_specs=[cu_spec, cb_spec, cc_spec, w_spec, chunk],
        out_specs=[chunk, chunk, chunk, w_spec],
        out_shape=[jax.ShapeDtypeStruct((s, CONV_WIDTH), BF16)] * 3 + [jax.ShapeDtypeStruct((3, CONV_WIDTH), F32)],
        compiler_params=_params(1),
    )(z, z, z, conv_w, d_co)


GATE_CHUNK = 512


def _gate_specs(s, d, tr):
    n_chunks = d // GATE_CHUNK
    za = pl.BlockSpec((tr, GATE_CHUNK), lambda j, i: (i, GL_OFF // GATE_CHUNK + j))
    zc = pl.BlockSpec((tr, GATE_CHUNK), lambda j, i: (i, GL_OFF // GATE_CHUNK + n_chunks + j))
    ba = pl.BlockSpec((1, GATE_CHUNK), lambda j, i: (0, j))
    bc = pl.BlockSpec((1, GATE_CHUNK), lambda j, i: (0, n_chunks + j))
    tile = pl.BlockSpec((tr, GATE_CHUNK), lambda j, i: (i, j))
    return za, zc, ba, bc, tile


def _gate_fwd(z, b_gate, ya, yc):
    s, d = ya.shape
    tr = _row_tile(s, 512)
    za, zc, ba, bc, tile = _gate_specs(s, d, tr)

    def body(za_ref, zc_ref, ba_ref, bc_ref, ya_ref, yc_ref, o_ref):
        ga = jax.nn.sigmoid(za_ref[...] + ba_ref[...])
        gc = jax.nn.sigmoid(zc_ref[...] + bc_ref[...])
        o_ref[...] = (ga * ya_ref[...] + gc * yc_ref[...]).astype(BF16)

    return pl.pallas_call(
        body, name="gate_fwd", grid=(d // GATE_CHUNK, s // tr),
        in_specs=[za, zc, ba, bc, tile, tile],
        out_specs=tile,
        out_shape=jax.ShapeDtypeStruct((s, d), BF16),
        compiler_params=_params(2),
    )(z, z, b_gate, b_gate, ya, yc)


def _gate_bwd(z, b_gate, ya, yc, dmix):
    s, d = ya.shape
    tr = _row_tile(s, 512)
    za, zc, ba, bc, tile = _gate_specs(s, d, tr)
    vec = pl.BlockSpec((1, GATE_CHUNK), lambda j, i: (0, j))

    def body(za_ref, zc_ref, ba_ref, bc_ref, ya_ref, yc_ref, dm_ref, dya_ref, dyc_ref, dla_ref, dlc_ref, dba_ref, dbc_ref):
        ga = jax.nn.sigmoid(za_ref[...] + ba_ref[...])
        gc = jax.nn.sigmoid(zc_ref[...] + bc_ref[...])
        dm = dm_ref[...]
        dya_ref[...] = (dm * ga).astype(BF16)
        dyc_ref[...] = (dm * gc).astype(BF16)
        dla = dm * ya_ref[...] * ga * (1.0 - ga)
        dlc = dm * yc_ref[...] * gc * (1.0 - gc)
        dla_ref[...] = dla.astype(BF16)
        dlc_ref[...] = dlc.astype(BF16)
        pa = jnp.sum(dla, axis=0, keepdims=True)
        pc = jnp.sum(dlc, axis=0, keepdims=True)

        @pl.when(pl.program_id(1) == 0)
        def _():
            dba_ref[...] = pa
            dbc_ref[...] = pc

        @pl.when(pl.program_id(1) > 0)
        def _():
            dba_ref[...] += pa
            dbc_ref[...] += pc

    big = jax.ShapeDtypeStruct((s, d), BF16)
    small = jax.ShapeDtypeStruct((1, d), F32)
    return pl.pallas_call(
        body, name="gate_bwd", grid=(d // GATE_CHUNK, s // tr),
        in_specs=[za, zc, ba, bc, tile, tile, tile],
        out_specs=[tile, tile, tile, tile, vec, vec],
        out_shape=[big, big, big, big, small, small],
        compiler_params=_params(2),
    )(z, z, b_gate, b_gate, ya, yc, dmix)


def _cross_probs(q_ref, kv_ref, hd):
    cols = slice(hd * HEAD_DIM, (hd + 1) * HEAD_DIM)
    qh = q_ref[:, cols]
    kh = kv_ref[:, cols]
    sc = lax.dot_general(qh, kh, (((1,), (1,)), ((), ())), preferred_element_type=F32) * ATTN_SCALE
    e = jnp.exp(sc - jnp.max(sc, axis=1, keepdims=True))
    return qh, kh, e * (1.0 / jnp.sum(e, axis=1, keepdims=True))


def _cross_fwd(qc, kvc):
    s = qc.shape[0]
    n_mem = kvc.shape[0]
    tq = _row_tile(s, 256)

    def body(q_ref, kv_ref, o_ref):
        for hd in range(MEM_HEADS):
            _, _, p = _cross_probs(q_ref, kv_ref, hd)
            vh = kv_ref[:, MEM_WIDTH + hd * HEAD_DIM:MEM_WIDTH + (hd + 1) * HEAD_DIM]
            o_ref[:, hd * HEAD_DIM:(hd + 1) * HEAD_DIM] = jnp.dot(p.astype(BF16), vh, preferred_element_type=F32).astype(BF16)

    return pl.pallas_call(
        body, name="cross_fwd", grid=(s // tq,),
        in_specs=[pl.BlockSpec((tq, MEM_WIDTH), lambda i: (i, 0)), pl.BlockSpec((n_mem, 2 * MEM_WIDTH), lambda i: (0, 0))],
        out_specs=pl.BlockSpec((tq, MEM_WIDTH), lambda i: (i, 0)),
        out_shape=jax.ShapeDtypeStruct((s, MEM_WIDTH), BF16),
        compiler_params=_params(1),
    )(qc, kvc)


def _cross_bwd(qc, kvc, d_out):
    s = qc.shape[0]
    n_mem = kvc.shape[0]
    tq = _row_tile(s, 256)

    def body(q_ref, kv_ref, do_ref, dq_ref, dkv_ref):
        @pl.when(pl.program_id(0) == 0)
        def _():
            dkv_ref[...] = jnp.zeros_like(dkv_ref)

        for hd in range(MEM_HEADS):
            cols = slice(hd * HEAD_DIM, (hd + 1) * HEAD_DIM)
            vcols = slice(MEM_WIDTH + hd * HEAD_DIM, MEM_WIDTH + (hd + 1) * HEAD_DIM)
            qh, kh, p = _cross_probs(q_ref, kv_ref, hd)
            doh = do_ref[:, cols]
            dp = lax.dot_general(doh, kv_ref[:, vcols], (((1,), (1,)), ((), ())), preferred_element_type=F32)
            ds = (p * (dp - jnp.sum(p * dp, axis=1, keepdims=True)) * ATTN_SCALE).astype(BF16)
            dq_ref[:, cols] = jnp.dot(ds, kh, preferred_element_type=F32).astype(BF16)
            dkv_ref[:, cols] += lax.dot_general(ds, qh, (((0,), (0,)), ((), ())), preferred_element_type=F32)
            dkv_ref[:, vcols] += lax.dot_general(p.astype(BF16), doh, (((0,), (0,)), ((), ())), preferred_element_type=F32)

    qspec = pl.BlockSpec((tq, MEM_WIDTH), lambda i: (i, 0))
    kvspec = pl.BlockSpec((n_mem, 2 * MEM_WIDTH), lambda i: (0, 0))
    return pl.pallas_call(
        body, name="cross_bwd", grid=(s // tq,),
        in_specs=[qspec, kvspec, qspec],
        out_specs=[qspec, kvspec],
        out_shape=[jax.ShapeDtypeStruct((s, MEM_WIDTH), BF16), jax.ShapeDtypeStruct((n_mem, 2 * MEM_WIDTH), F32)],
        compiler_params=_params(1),
    )(qc, kvc, d_out)


def _swiglu_fwd(up, gate):
    return up, (gate * jax.nn.sigmoid(gate)) * up


def _swiglu_bwd(d_act, gate, up):
    sg = jax.nn.sigmoid(gate)
    silu = gate * sg
    return d_act * up * (sg * (1.0 + gate * (1.0 - sg))), d_act * silu


GATHER_GROUPS = {"in": ("w_in", "conv_w"), "mix": ("w_attn_out", "w_conv_out", "w_o"), "cross": ("w_cq", "w_ckv", "w_co"),
                 "gate": ("w_gate",), "up": ("w_up",), "down": ("w_down",)}


def _local_step(xs, mems, target, small, fetch, reduce):
    s, d = xs.shape
    w4 = {}
    cos_t, sin_t = _rope_tables(s)

    h = _rmsnorm(xs, small["g_mix"], "norm_mix")
    w4.update(fetch.get("in", h))
    conv4 = w4["conv_w"]
    conv_w = conv4[:, :3, :].transpose(1, 0, 2).reshape(3, N_CHIPS * conv4.shape[2])
    c_in = w4["w_in"].shape[2]
    z = _matmul(h, w4["w_in"], mode="nn", tm=512, tn=c_in, tk=d, out_dtypes=[F32], name="in_proj", b_blocks=N_CHIPS)
    fetch.begin("mix", z)
    q_rot, k_rot, v_b = _rope_fwd(z, cos_t, sin_t)
    attn = _swa_fwd(q_rot, k_rot, v_b, small["sink"])
    co = _conv_fwd(z, conv_w)
    w4.update(fetch.get("mix", co))
    tok = fetch.begin("cross", attn)
    w_o = w4["w_o"].reshape(-1, w4["w_o"].shape[-1])
    c_d = w4["w_attn_out"].shape[2]
    ya = _matmul(attn, w4["w_attn_out"], mode="nn", tm=1024, tn=c_d, tk=ATTN_WIDTH, out_dtypes=[F32], name="attn_out_proj",
                 b_blocks=N_CHIPS, after=tok)
    yc = _matmul(co, w4["w_conv_out"], mode="nn", tm=1024, tn=c_d, tk=CONV_WIDTH, out_dtypes=[F32], name="conv_out_proj",
                 b_blocks=N_CHIPS)
    mix = _gate_fwd(z, small["b_gate"], ya, yc)
    x1 = _matmul(mix, w_o, mode="nn", tm=512, tn=1024, tk=d, out_dtypes=[F32], name="mix_out_proj", extras=[xs],
                 epilogue=_add_residual)
    w4.update(fetch.get("cross", x1))
    tok = fetch.begin("gate", x1)
    w_cq = w4["w_cq"].reshape(-1, w4["w_cq"].shape[-1])
    w_ckv = w4["w_ckv"].reshape(-1, w4["w_ckv"].shape[-1])
    hc = _rmsnorm(x1, small["g_cross"], "norm_cross")
    memn = _rmsnorm(mems, small["g_mem"], "norm_mem")
    qc = _matmul(hc, w_cq, mode="nn", tm=1024, tn=MEM_WIDTH, tk=d, out_dtypes=[BF16], name="cross_q_proj", after=tok)
    kvc = _matmul(memn, w_ckv, mode="nn", tm=256, tn=2 * MEM_WIDTH, tk=d, out_dtypes=[BF16], name="cross_kv_proj")
    oc = _cross_fwd(qc, kvc)
    x2 = _matmul(oc, w4["w_co"], mode="nn", tm=1024, tn=c_d, tk=MEM_WIDTH, out_dtypes=[F32], name="cross_out_proj",
                 extras=[x1], epilogue=_add_residual, b_blocks=N_CHIPS)
    hf = _rmsnorm(x2, small["g_ffn"], "norm_ffn")
    w4.update(fetch.get("gate", hf))
    tok = fetch.begin("up", hf)
    c_ff = w4["w_gate"].shape[2]
    gate = _matmul(hf, w4["w_gate"], mode="nn", tm=512, tn=c_ff, tk=d, out_dtypes=[F32], name="ffn_gate_proj", b_blocks=N_CHIPS,
                   after=tok)
    w4.update(fetch.get("up", gate))
    up, act = _matmul(hf, w4["w_up"], mode="nn", tm=512, tn=c_ff, tk=d, out_dtypes=[F32, BF16], name="ffn_up_proj",
                      extras=[gate], epilogue=_swiglu_fwd, b_blocks=N_CHIPS)
    w4.update(fetch.get("down", act))
    w_down = w4["w_down"].reshape(-1, w4["w_down"].shape[-1])
    x3 = _matmul(act, w_down, mode="nn", tm=512, tn=512, tk=w_down.shape[0], out_dtypes=[F32], name="ffn_down_proj", extras=[x2],
                 epilogue=_add_residual)
    dx3, dx3b, sq, dg_final = _loss_head(x3, small["g_final"], target)

    da, du = _matmul(dx3b, w_down, mode="nt", tm=512, tn=c_ff, tk=d, out_dtypes=[BF16, BF16], name="ffn_down_bwd",
                     extras=[gate, up], epilogue=_swiglu_bwd)
    core = reduce.core
    ffn_shape = dict(row_sharded=False, tm=512, tn=c_ff)
    g_down = _matmul(act, dx3b, mode="tn", tm=c_ff, tn=1024, tk=s, out_dtypes=[BF16], name="ffn_down_wgrad")
    tok = reduce.add("down", {"w_down": g_down}, da)
    t_gate = _wgrad_half(hf, da, core, theirs=True, name="ffn_gate_wgrad_theirs", after=tok, **ffn_shape)
    tok = reduce.step("down", t_gate)
    t_up = _wgrad_half(hf, du, core, theirs=True, name="ffn_up_wgrad_theirs", after=tok, **ffn_shape)
    tok = reduce.send("ffn", {"w_gate": t_gate, "w_up": t_up}, dx3b)
    dhf = _matmul(da, w4["w_gate"], mode="nt", tm=512, tn=1024, tk=N_CHIPS * c_ff, out_dtypes=[F32], name="ffn_gate_bwd", b_blocks=N_CHIPS,
                  after=tok)
    got = reduce.received("ffn", dhf)
    p_gate = _wgrad_half(hf, da, core, theirs=False, name="ffn_gate_wgrad_mine", add=got["w_gate"], **ffn_shape)
    p_up = _wgrad_half(hf, du, core, theirs=False, name="ffn_up_wgrad_mine", add=got["w_up"], **ffn_shape)
    tok = reduce.add_parts("ffn", {"w_gate": p_gate, "w_up": p_up})
    dhf = _matmul(du, w4["w_up"], mode="nt", tm=512, tn=1024, tk=N_CHIPS * c_ff, out_dtypes=[F32], name="ffn_up_bwd", extras=[dhf],
                  epilogue=_add_residual, b_blocks=N_CHIPS, after=tok)
    tok = reduce.step("down", dhf)
    dx2, dx2b, dg_ffn = _rmsnorm_bwd(dhf, x2, small["g_ffn"], dx3, "norm_ffn_bwd")

    d_oc = _matmul(dx2b, w4["w_co"], mode="nt", tm=1024, tn=MEM_WIDTH, tk=d, out_dtypes=[BF16], name="cross_out_bwd",
                   b_blocks=N_CHIPS, after=tok)
    g_co = _matmul(oc, dx2b, mode="tn", tm=MEM_WIDTH, tn=c_d, tk=s, out_dtypes=[BF16], name="cross_out_wgrad", out_blocks=N_CHIPS)
    tok = reduce.step("down", g_co)
    dqc, dkvc = _cross_bwd(qc, kvc, d_oc)
    g_cq = _matmul(hc, dqc, mode="tn", tm=1024, tn=MEM_WIDTH, tk=s, out_dtypes=[BF16], name="cross_q_wgrad", after=tok)
    dhc = _matmul(dqc, w_cq, mode="nt", tm=1024, tn=1024, tk=MEM_WIDTH, out_dtypes=[F32], name="cross_q_bwd")
    g_ckv = _matmul(memn, dkvc, mode="tn", tm=1024, tn=2 * MEM_WIDTH, tk=mems.shape[0], out_dtypes=[BF16], name="cross_kv_wgrad")
    dmemn = _matmul(dkvc, w_ckv, mode="nt", tm=256, tn=1024, tk=2 * MEM_WIDTH, out_dtypes=[F32], name="cross_kv_bwd")
    _, _, dg_mem = _rmsnorm_bwd(dmemn, mems, small["g_mem"], None, "norm_mem_bwd")
    dx1, dx1b, dg_cross = _rmsnorm_bwd(dhc, x1, small["g_cross"], dx2, "norm_cross_bwd")

    dmix = _matmul(dx1b, w_o, mode="nt", tm=512, tn=1024, tk=d, out_dtypes=[F32], name="mix_out_bwd")
    tok = reduce.step("ffn", dmix)
    g_o = _matmul(mix, dx1b, mode="tn", tm=1024, tn=1024, tk=s, out_dtypes=[BF16], name="mix_out_wgrad", after=tok)
    tok = reduce.step("ffn", g_o)
    dya, dyc, dgl_a, dgl_c, db_a, db_c = _gate_bwd(z, small["b_gate"], ya, yc, dmix)
    d_attn = _matmul(dya, w4["w_attn_out"], mode="nt", tm=1024, tn=ATTN_WIDTH, tk=d, out_dtypes=[BF16], name="attn_out_bwd",
                     b_blocks=N_CHIPS, after=tok)
    g_ao = _matmul(attn, dya, mode="tn", tm=ATTN_WIDTH, tn=c_d, tk=s, out_dtypes=[BF16], name="attn_out_wgrad", out_blocks=N_CHIPS)
    d_co = _matmul(dyc, w4["w_conv_out"], mode="nt", tm=1024, tn=CONV_WIDTH, tk=d, out_dtypes=[F32], name="conv_out_bwd",
                   b_blocks=N_CHIPS)
    g_cvo = _matmul(co, dyc, mode="tn", tm=CONV_WIDTH, tn=c_d, tk=s, out_dtypes=[BF16], name="conv_out_wgrad", out_blocks=N_CHIPS)
    tok = reduce.add("mid", {"w_co": g_co, "w_cq": g_cq, "w_ckv": g_ckv, "w_o": g_o, "w_attn_out": g_ao, "w_conv_out": g_cvo}, d_co)
    dcu, dcb, dcc, d_conv_w = _conv_bwd(z, conv_w, d_co)
    dq_rot, dk_rot, dv, dsink = _swa_bwd(q_rot, k_rot, v_b, d_attn, small["sink"])
    tok = reduce.step("mid", dq_rot)
    dq, dk, dvb = _rope_bwd(dq_rot, dk_rot, dv, cos_t, sin_t)
    dz = jnp.concatenate([dq, dk, dvb, dcu, dcb, dcc, dgl_a, dgl_c], axis=1)
    in_shape = dict(row_sharded=False, tm=512, tn=c_in)
    t_in = _wgrad_half(h, dz, core, theirs=True, name="in_proj_wgrad_theirs", after=tok, **in_shape)
    tok = reduce.send("in", {"w_in": t_in}, dk)
    tok = reduce.step("mid", tok)
    got = reduce.received("in", tok)
    p_in = _wgrad_half(h, dz, core, theirs=False, name="in_proj_wgrad_mine", add=got["w_in"], **in_shape)
    tok = reduce.add_parts("in", {"w_in": p_in})
    dh = _matmul(dz, w4["w_in"], mode="nt", tm=512, tn=512, tk=N_CHIPS * c_in, out_dtypes=[F32], name="in_proj_bwd", b_blocks=N_CHIPS,
                 after=tok)
    tok = reduce.step("mid", dh)
    grad_x, _, dg_mix = _rmsnorm_bwd(dh, xs, small["g_mix"], dx1, "norm_mix_bwd")

    small_grads = {
        "g_mix": dg_mix, "sink": dsink[:, 0], "b_gate": jnp.concatenate([db_a, db_c], axis=1), "g_cross": dg_cross,
        "g_mem": dg_mem, "g_ffn": dg_ffn, "g_final": dg_final, "conv_w": d_conv_w,
    }
    return sq, grad_x, small_grads


def _pair_sum(g4, ra, core, name):
    nb, rs, cs = g4.shape
    rh = rs // 2
    tr = _row_tile(rh, 256)
    per = rh // tr

    def body(c_ref, g_ref, r_ref, o_ref):
        o_ref[...] = (g_ref[...].astype(F32) + r_ref[...].astype(F32)).astype(BF16)

    plain = pl.BlockSpec((None, tr, cs), lambda j, i, c: (j, i, 0))
    return pl.pallas_call(
        body, name=name,
        grid_spec=pltpu.PrefetchScalarGridSpec(
            num_scalar_prefetch=1, grid=(nb, per),
            in_specs=[pl.BlockSpec((None, tr, cs), lambda j, i, c: (j, c[0] * per + i, 0)), plain],
            out_specs=plain),
        out_shape=jax.ShapeDtypeStruct((nb, rh, cs), BF16),
        compiler_params=_params(2),
    )(core, g4, ra)


def _quad_sum(parts, rc, place, name):
    _, rh, cs = parts.shape
    tr = _row_tile(rh, 256)
    per = rh // tr

    def body(p_ref, own_ref, r_ref, o_ref):
        acc = own_ref[...].astype(F32)
        for j in range(rc.shape[0]):
            acc = acc + r_ref[j].astype(F32)
        o_ref[...] = acc

    return pl.pallas_call(
        body, name=name,
        grid_spec=pltpu.PrefetchScalarGridSpec(
            num_scalar_prefetch=1, grid=(per,),
            in_specs=[pl.BlockSpec((None, tr, cs), lambda i, p: (p[0], i, 0)),
                      pl.BlockSpec((rc.shape[0], tr, cs), lambda i, p: (0, i, 0))],
            out_specs=pl.BlockSpec((tr, cs), lambda i, p: (p[1] * per + i, 0))),
        out_shape=jax.ShapeDtypeStruct((2 * rh, cs), F32),
        compiler_params=_params(1),
    )(place, parts, rc)


def _cast_to_slot(w, place, dtype, name, after=None):
    rows, cols = w.shape
    tr = _row_tile(rows, 256)

    def body(p_ref, w_ref, *rest):
        o_ref = rest[-1]
        o_ref[...] = w_ref[...].astype(dtype)

    return pl.pallas_call(
        body, name=name,
        grid_spec=pltpu.PrefetchScalarGridSpec(
            num_scalar_prefetch=1, grid=(rows // tr,),
            in_specs=[pl.BlockSpec((tr, cols), lambda i, p: (i, 0))] + ([] if after is None else [ANY]),
            out_specs=pl.BlockSpec((None, tr, cols), lambda i, p: (p[0], i, 0))),
        out_shape=jax.ShapeDtypeStruct((N_CHIPS, rows, cols), dtype),
        compiler_params=_params(1),
    )(place, w, *([] if after is None else [after]))


def _adamw(w, g, m, v, name):
    rows, cols = w.shape
    tr = _row_tile(rows, 256)

    def body(w_ref, g_ref, m_ref, v_ref, go_ref, d_ref, nm_ref, nv_ref):
        gv = g_ref[...]
        go_ref[...] = gv
        nm = ADAM_B1 * m_ref[...] + (1.0 - ADAM_B1) * gv
        nv = ADAM_B2 * v_ref[...] + (1.0 - ADAM_B2) * (gv * gv)
        m_hat = nm / ADAM_C1
        v_hat = nv / ADAM_C2
        d_ref[...] = -ADAM_LR * (m_hat / (jnp.sqrt(v_hat) + ADAM_EPS) + ADAM_WD * w_ref[...])
        nm_ref[...] = nm
        nv_ref[...] = nv

    tile = pl.BlockSpec((tr, cols), lambda i: (i, 0))
    shape = jax.ShapeDtypeStruct((rows, cols), F32)
    return pl.pallas_call(
        body, name=name, grid=(rows // tr,),
        in_specs=[tile] * 4, out_specs=[tile] * 4, out_shape=[shape] * 4,
        compiler_params=_params(1),
    )(w, g, m, v)


def _mesh_pos():
    return lax.axis_index("x"), lax.axis_index("y"), lax.axis_index("c")


def _other_chips(x, y):
    return [(1 - x, y), (x, 1 - y), (1 - x, 1 - y)]


def _half_rows(ref, which):
    rh = ref.shape[-2] // 2
    return ref.at[pl.ds(which * rh, rh), :]


def _remote(src, dst, send_sems, recv_sems, sem, to):
    return pltpu.make_async_remote_copy(src_ref=src, dst_ref=dst, send_sem=send_sems.at[sem], recv_sem=recv_sems.at[sem],
                                        device_id=to, device_id_type=MESH)


HBM = pl.BlockSpec(memory_space=pltpu.HBM)
SEM = pl.BlockSpec(memory_space=pltpu.SEMAPHORE)
DATAFLOW_EFFECT = pltpu.SideEffectType.DATAFLOW_SIDE_EFFECTING


def _in_hbm(arrays):
    return [pltpu.with_memory_space_constraint(a, pltpu.HBM) for a in arrays]


def _hbm_like(arrays):
    return [pltpu.HBM(a.shape, a.dtype) for a in arrays]


def _gather_start(bufs, groups, name):
    n, ng = len(bufs), len(groups)

    def body(*refs):
        ins = refs[:n]
        send, recv, token = refs[n:n + ng], refs[n + ng:n + 2 * ng], refs[-1]
        x, y, c = _mesh_pos()
        me = 2 * x + y
        for g, members in enumerate(groups):
            for i, w in enumerate(members):
                mine = _half_rows(ins[w].at[me], c)
                for k, (px, py) in enumerate(_other_chips(x, y)):
                    _remote(mine, mine, send[g], recv[g], 3 * i + k, (px, py, c)).start()
        token[...] = jnp.zeros_like(token)

    sems = [pltpu.SemaphoreType.DMA((3 * len(m),)) for m in groups]
    outs = pl.pallas_call(
        body, name=name,
        in_specs=[HBM] * n, out_specs=[SEM] * (2 * ng) + [HBM] * n + [pl.BlockSpec(memory_space=pltpu.VMEM)],
        out_shape=sems + sems + _hbm_like(bufs) + [jax.ShapeDtypeStruct((8, 128), F32)],
        input_output_aliases={i: 2 * ng + i for i in range(n)},
        compiler_params=pltpu.CompilerParams(has_side_effects=DATAFLOW_EFFECT),
    )(*_in_hbm(bufs))
    return outs[:ng], outs[ng:2 * ng], outs[2 * ng:2 * ng + n], outs[-1]


def _gather_pass(bufs, send, recv, after, name):
    m = len(bufs)

    def body(*refs):
        ins, send_in, recv_in = refs[:m], refs[m], refs[m + 1]
        send_out, recv_out, token = refs[m + 3], refs[m + 4], refs[-1]
        x, y, c = _mesh_pos()
        for i in range(m):
            for k, (px, py) in enumerate(_other_chips(x, y)):
                landed = _half_rows(ins[i].at[2 * px + py], c)
                came = _remote(landed, landed, send_in, recv_in, 3 * i + k, (px, py, c))
                came.wait_recv()
                came.wait_send()
                _remote(landed, landed, send_out, recv_out, 3 * i + k, (x, y, 1 - c)).start()
        token[...] = jnp.zeros_like(token)

    sems = [pltpu.SemaphoreType.DMA((3 * m,))] * 2
    outs = pl.pallas_call(
        body, name=name,
        in_specs=[HBM] * m + [SEM, SEM, ANY], out_specs=[SEM, SEM] + [HBM] * m + [pl.BlockSpec(memory_space=pltpu.VMEM)],
        out_shape=sems + _hbm_like(bufs) + [jax.ShapeDtypeStruct((8, 128), F32)],
        input_output_aliases={i: 2 + i for i in range(m)},
        compiler_params=pltpu.CompilerParams(has_side_effects=DATAFLOW_EFFECT),
    )(*_in_hbm(bufs), send, recv, after)
    return outs[0], outs[1], outs[2:2 + m], outs[-1]


def _gather_done(bufs, send, recv, after, name):
    m = len(bufs)

    def body(*refs):
        ins, send_in, recv_in = refs[:m], refs[m], refs[m + 1]
        x, y, c = _mesh_pos()
        for i in range(m):
            for k, (px, py) in enumerate(_other_chips(x, y)):
                passed = _half_rows(ins[i].at[2 * px + py], 1 - c)
                came = _remote(passed, passed, send_in, recv_in, 3 * i + k, (x, y, 1 - c))
                came.wait_send()
                came.wait_recv()

    return pl.pallas_call(
        body, name=name,
        in_specs=[HBM] * m + [SEM, SEM, ANY], out_specs=[HBM] * m,
        out_shape=_hbm_like(bufs),
        input_output_aliases={i: i for i in range(m)},
        compiler_params=pltpu.CompilerParams(has_side_effects=DATAFLOW_EFFECT),
    )(*_in_hbm(bufs), send, recv, after)


class _Gather:
    def __init__(self, groups):
        self.groups = groups
        self.landing = {}
        self.passing = {}

    def start(self, slotted, group_names, name):
        names = [n for g in group_names for n in self.groups[g]]
        index = {n: i for i, n in enumerate(names)}
        members = [[index[n] for n in self.groups[g]] for g in group_names]
        send, recv, bufs, token = _gather_start([slotted[n] for n in names], members, name)
        for j, g in enumerate(group_names):
            self.landing[g] = (send[j], recv[j], [bufs[index[n]] for n in self.groups[g]])
        return token

    def begin(self, group, after):
        send, recv, bufs = self.landing.pop(group)
        send, recv, bufs, token = _gather_pass(bufs, send, recv, after, "gather_pass_" + group)
        self.passing[group] = (send, recv, bufs)
        return token

    def get(self, group, after):
        if group not in self.passing:
            self.begin(group, after)
        send, recv, bufs = self.passing.pop(group)
        return dict(zip(self.groups[group], _gather_done(bufs, send, recv, after, "gather_done_" + group)))


def _sibling_halves_copies(srcs, dsts, x, y, c):
    out = []
    for s_ref, d_ref in zip(srcs, dsts, strict=True):
        rh = s_ref.shape[1] // 2
        out.append((s_ref.at[:, pl.ds((1 - c) * rh, rh), :], d_ref, (x, y, 1 - c)))
    return out


def _to_sibling_copies(srcs, dsts, x, y, c):
    return [(s_ref, d_ref, (x, y, 1 - c)) for s_ref, d_ref in zip(srcs, dsts, strict=True)]


def _chip_copies(srcs, dsts, x, y, c):
    out = []
    for s_ref, d_ref in zip(srcs, dsts, strict=True):
        for k, (px, py) in enumerate(_other_chips(x, y)):
            out.append((s_ref.at[2 * px + py], d_ref.at[k], (px, py, c)))
    return out


def _join_copies(srcs, dsts, x, y, c):
    out = []
    for s_ref in srcs:
        mine = _half_rows(s_ref, c)
        out.append((mine, mine, (x, y, 1 - c)))
    return out


def _exchange_start(copies_fn, n_copies, srcs, fresh, after, name):
    ns, nb = len(srcs), len(srcs) + len(fresh)

    def body(*refs):
        bufs, send, recv, token = refs[:nb], refs[nb + 1], refs[nb + 2], refs[-1]
        x, y, c = _mesh_pos()
        for i, (s_ref, d_ref, to) in enumerate(copies_fn(bufs[:ns], bufs[ns:] if fresh else bufs[:ns], x, y, c)):
            _remote(s_ref, d_ref, send, recv, i, to).start()
        token[...] = jnp.zeros_like(token)

    sems = [pltpu.SemaphoreType.DMA((n_copies,))] * 2
    outs = pl.pallas_call(
        body, name=name,
        in_specs=[HBM] * nb + [ANY], out_specs=[SEM, SEM] + [HBM] * nb + [pl.BlockSpec(memory_space=pltpu.VMEM)],
        out_shape=sems + _hbm_like(list(srcs) + list(fresh)) + [jax.ShapeDtypeStruct((8, 128), F32)],
        input_output_aliases={i: 2 + i for i in range(nb)},
        compiler_params=pltpu.CompilerParams(has_side_effects=DATAFLOW_EFFECT),
    )(*_in_hbm(list(srcs) + list(fresh)), after)
    return outs[0], outs[1], outs[2:2 + ns], outs[2 + ns:2 + nb], outs[-1]


def _exchange_done(copies_fn, srcs, fresh, send, recv, after, name):
    ns, nb = len(srcs), len(srcs) + len(fresh)

    def body(*refs):
        bufs, send_in, recv_in = refs[:nb], refs[nb], refs[nb + 1]
        x, y, c = _mesh_pos()
        for i, (s_ref, d_ref, to) in enumerate(copies_fn(bufs[:ns], bufs[ns:] if fresh else bufs[:ns], x, y, c)):
            came = _remote(s_ref, d_ref, send_in, recv_in, i, to)
            came.wait_send()
            came.wait_recv()

    outs = pl.pallas_call(
        body, name=name,
        in_specs=[HBM] * nb + [SEM, SEM, ANY], out_specs=[HBM] * nb,
        out_shape=_hbm_like(list(srcs) + list(fresh)),
        input_output_aliases={i: i for i in range(nb)},
        compiler_params=pltpu.CompilerParams(has_side_effects=DATAFLOW_EFFECT),
    )(*_in_hbm(list(srcs) + list(fresh)), send, recv, after)
    return outs[:ns], outs[ns:]


class _Reduce:
    def __init__(self, place, core, shards, mom_m, mom_v):
        self.place, self.core = place, core
        self.shards, self.mom_m, self.mom_v = shards, mom_m, mom_v
        self.state = {}
        self.results = {}

    def add(self, group, grads, after):
        names = list(grads)
        g4s = [g.reshape((N_CHIPS, -1, g.shape[-1])) if g.ndim == 2 else g for g in grads.values()]
        fresh = [lax.empty((N_CHIPS, g.shape[1] // 2, g.shape[2]), BF16) for g in g4s]
        send, recv, g4s, fresh, token = _exchange_start(_sibling_halves_copies, len(names), g4s, fresh, after,
                                                        "pair_start_" + group)
        self.state[group] = (0, names, send, recv, g4s, fresh)
        return token

    def send(self, group, theirs, after):
        names, srcs = list(theirs), list(theirs.values())
        fresh = [lax.empty(s.shape, BF16) for s in srcs]
        send, recv, srcs, fresh, token = _exchange_start(_to_sibling_copies, len(names), srcs, fresh, after, "pair_start_" + group)
        self.state[group] = ("sent", names, send, recv, srcs, fresh)
        return token

    def received(self, group, after):
        stage, names, send, recv, srcs, fresh = self.state.pop(group)
        assert stage == "sent"
        _, got = _exchange_done(_to_sibling_copies, srcs, fresh, send, recv, after, "pair_done_" + group)
        return dict(zip(names, got))

    def add_parts(self, group, parts):
        names, srcs = list(parts), list(parts.values())
        fresh = [lax.empty((N_CHIPS - 1,) + p.shape[1:], BF16) for p in srcs]
        send, recv, srcs, fresh, token = _exchange_start(_chip_copies, 3 * len(names), srcs, fresh, self.core, "chips_start_" + group)
        self.state[group] = (1, names, send, recv, srcs, fresh)
        return token

    def step(self, group, after):
        stage, names, send, recv, srcs, fresh = self.state[group]
        if stage == 0:
            g4s, ras = _exchange_done(_sibling_halves_copies, srcs, fresh, send, recv, after, "pair_done_" + group)
            parts = [_pair_sum(g, r, self.core, "pair_sum_" + n) for g, r, n in zip(g4s, ras, names)]
            fresh = [lax.empty((N_CHIPS - 1,) + p.shape[1:], BF16) for p in parts]
            send, recv, parts, fresh, token = _exchange_start(_chip_copies, 3 * len(names), parts, fresh, self.core,
                                                              "chips_start_" + group)
            self.state[group] = (1, names, send, recv, parts, fresh)
            return token
        if stage == 1:
            parts, rcs = _exchange_done(_chip_copies, srcs, fresh, send, recv, after, "chips_done_" + group)
            wholes = [_quad_sum(p, r, self.place, "quad_sum_" + n) for p, r, n in zip(parts, rcs, names)]
            send, recv, wholes, _, token = _exchange_start(_join_copies, len(names), wholes, [], self.core, "join_start_" + group)
            self.state[group] = (2, names, send, recv, wholes, [])
            return token
        assert stage == 2
        wholes, _ = _exchange_done(_join_copies, srcs, [], send, recv, after, "join_done_" + group)
        for n, g in zip(names, wholes):
            self.results[n] = _adamw(self.shards[n], g, self.mom_m[n], self.mom_v[n], "adamw_" + n)
        del self.state[group]
        return self.results[names[-1]][1]


N_DEV = 8


def _all_reduce_small(v):
    def body(v_ref, o_ref, slots, send_sems, recv_sems):
        x, y, c = _mesh_pos()
        me = 4 * x + 2 * y + c
        slots[me] = v_ref[...]
        peers = []
        for r in range(1, N_DEV):
            fx, fy, fc = (r >> 2) & 1, (r >> 1) & 1, r & 1
            peers.append((x + fx - 2 * x * fx, y + fy - 2 * y * fy, c + fc - 2 * c * fc))
        sends = []
        for r, peer in enumerate(peers):
            cp = _remote(v_ref, slots.at[me], send_sems, recv_sems, r, peer)
            cp.start()
            sends.append(cp)
        for r, (px, py, pc) in enumerate(peers):
            landed = slots.at[4 * px + 2 * py + pc]
            _remote(landed, landed, send_sems, recv_sems, r, (px, py, pc)).wait_recv()
        for cp in sends:
            cp.wait_send()
        acc = slots[0]
        for i in range(1, N_DEV):
            acc = acc + slots[i]
        o_ref[...] = acc

    vm = pl.BlockSpec(memory_space=pltpu.VMEM)
    return pl.pallas_call(
        body, name="small_grads_all_reduce",
        in_specs=[vm], out_specs=vm,
        out_shape=jax.ShapeDtypeStruct(v.shape, v.dtype),
        scratch_shapes=[pltpu.VMEM((N_DEV,) + v.shape, v.dtype), pltpu.SemaphoreType.DMA((N_DEV - 1,)),
                        pltpu.SemaphoreType.DMA((N_DEV - 1,))],
    )(v)


MATRICES = ("w_in", "w_attn_out", "w_conv_out", "w_o", "w_cq", "w_ckv", "w_co", "w_gate", "w_up", "w_down")
VECTORS = ("g_mix", "b_gate", "g_cross", "g_mem", "g_ffn", "g_final", "conv_w", "sink")
WEIGHT_ORDER = ("g_mix", "w_in", "sink", "conv_w", "b_gate", "w_attn_out", "w_conv_out", "w_o", "g_cross", "g_mem", "w_cq",
                "w_ckv", "w_co", "g_ffn", "w_gate", "w_up", "w_down", "g_final")
CONV_PAD_ROWS = 16
SMALL_ROWS = 8


def _pack(pieces):
    flat = jnp.concatenate([p.reshape(-1) for p in pieces])
    lane_group = SMALL_ROWS * 128
    total = -(-flat.shape[0] // lane_group) * lane_group
    flat = jnp.pad(flat, (0, total - flat.shape[0]))
    return flat.reshape(SMALL_ROWS, total // SMALL_ROWS), [p.size for p in pieces]


def _unpack(packed, pieces):
    flat = packed.reshape(-1)
    out, off = [], 0
    for p in pieces:
        out.append(flat[off:off + p.size].reshape(p.shape))
        off += p.size
    return out


def kernel(x, mem, g_mix, w_in, sink, conv_w, b_gate, w_attn_out, w_conv_out, w_o, g_cross, g_mem, w_cq, w_ckv, w_co, g_ffn, w_gate, w_up, w_down, g_final, loss_target, m_g_mix, m_w_in, m_sink, m_conv_w, m_b_gate, m_w_attn_out, m_w_conv_out, m_w_o, m_g_cross, m_g_mem, m_w_cq, m_w_ckv, m_w_co, m_g_ffn, m_w_gate, m_w_up, m_w_down, m_g_final, v_g_mix, v_w_in, v_sink, v_conv_w, v_b_gate, v_w_attn_out, v_w_conv_out, v_w_o, v_g_cross, v_g_mem, v_w_cq, v_w_ckv, v_w_co, v_g_ffn, v_w_gate, v_w_up, v_w_down, v_g_final):
    given = dict(g_mix=g_mix, w_in=w_in, sink=sink, conv_w=conv_w, b_gate=b_gate, w_attn_out=w_attn_out, w_conv_out=w_conv_out,
                 w_o=w_o, g_cross=g_cross, g_mem=g_mem, w_cq=w_cq, w_ckv=w_ckv, w_co=w_co, g_ffn=g_ffn, w_gate=w_gate, w_up=w_up,
                 w_down=w_down, g_final=g_final)
    mom_m = dict(g_mix=m_g_mix, w_in=m_w_in, sink=m_sink, conv_w=m_conv_w, b_gate=m_b_gate, w_attn_out=m_w_attn_out,
                 w_conv_out=m_w_conv_out, w_o=m_w_o, g_cross=m_g_cross, g_mem=m_g_mem, w_cq=m_w_cq, w_ckv=m_w_ckv, w_co=m_w_co,
                 g_ffn=m_g_ffn, w_gate=m_w_gate, w_up=m_w_up, w_down=m_w_down, g_final=m_g_final)
    mom_v = dict(g_mix=v_g_mix, w_in=v_w_in, sink=v_sink, conv_w=v_conv_w, b_gate=v_b_gate, w_attn_out=v_w_attn_out,
                 w_conv_out=v_w_conv_out, w_o=v_w_o, g_cross=v_g_cross, g_mem=v_g_mem, w_cq=v_w_cq, w_ckv=v_w_ckv, w_co=v_w_co,
                 g_ffn=v_g_ffn, w_gate=v_w_gate, w_up=v_w_up, w_down=v_w_down, g_final=v_g_final)
    xs, mems, target = x[0], mem[0], loss_target[0]
    d_model = xs.shape[1]
    chip = 2 * lax.axis_index("x") + lax.axis_index("y")
    core = jnp.reshape(lax.axis_index("c"), (1,)).astype(jnp.int32)
    place = jnp.stack([chip, lax.axis_index("c")]).astype(jnp.int32)

    shards = {n: given[n][0] for n in MATRICES}
    conv_cols = conv_w.shape[2]
    conv_pad = jnp.pad(conv_w[0], ((0, CONV_PAD_ROWS - conv_w.shape[1]), (0, 0)))
    fetch = _Gather(GATHER_GROUPS)
    first = {"w_in": _cast_to_slot(shards["w_in"], place, BF16, "to_slot_w_in"),
             "conv_w": _cast_to_slot(conv_pad, place, F32, "to_slot_conv_w")}
    tok = fetch.start(first, ["in"], "gather_start_in")
    rest = {n: _cast_to_slot(shards[n], place, BF16, "to_slot_" + n, after=tok) for n in MATRICES if n != "w_in"}
    fetch.start(rest, [g for g in GATHER_GROUPS if g != "in"], "gather_start_rest")
    small = {n: given[n] for n in ("g_mix", "b_gate", "g_cross", "g_mem", "g_ffn")}
    small["g_final"] = g_final[None]
    small["sink"] = sink[0]

    reduce = _Reduce(place, core, shards, {n: mom_m[n][0] for n in MATRICES}, {n: mom_v[n][0] for n in MATRICES})
    sq, grad_x, small_grads = _local_step(xs, mems, target, small, fetch, reduce)

    loss_part = 0.5 * sq[0:1, 0:1] / d_model
    pieces = [small_grads[n] for n in VECTORS] + [loss_part]
    packed, _ = _pack(pieces)
    summed = _unpack(_all_reduce_small(packed), pieces)
    loss = summed[-1][0, 0]
    small_sum = dict(zip(VECTORS, summed[:-1]))
    small_sum["conv_w"] = lax.dynamic_slice_in_dim(small_sum["conv_w"], chip * conv_cols, conv_cols, axis=1)

    grad_out, delta, new_m, new_v = {}, {}, {}, {}
    like = [given[n] for n in VECTORS]
    pw, _ = _pack(like)
    pg, _ = _pack([small_sum[n] for n in VECTORS])
    pm, _ = _pack([mom_m[n] for n in VECTORS])
    pv, _ = _pack([mom_v[n] for n in VECTORS])
    _, pd, pnm, pnv = _adamw(pw, pg, pm, pv, "adamw_small")
    for n, g, d, nm, nv in zip(VECTORS, [small_sum[n] for n in VECTORS], _unpack(pd, like), _unpack(pnm, like), _unpack(pnv, like)):
        grad_out[n] = g.reshape(given[n].shape)
        delta[n], new_m[n], new_v[n] = d, nm, nv
    tok = reduce.step("in", pd)
    reduce.step("in", tok)
    for n in MATRICES:
        g, d, nm, nv = reduce.results[n]
        grad_out[n], delta[n], new_m[n], new_v[n] = g[None], d[None], nm[None], nv[None]

    return (loss, grad_x[None], *[grad_out[n] for n in WEIGHT_ORDER], *[delta[n] for n in WEIGHT_ORDER],
            *[new_m[n] for n in WEIGHT_ORDER], *[new_v[n] for n in WEIGHT_ORDER])
```

```python
import functools

import jax
import jax.numpy as jnp
from jax import lax
from jax.experimental import pallas as pl
from jax.experimental.pallas import tpu as pltpu

F32 = jnp.float32
BF16 = jnp.bfloat16
MESH = pl.DeviceIdType.MESH
ANY = pl.BlockSpec(memory_space=pl.ANY)

VMEM_LIMIT_BYTES = 56 * 1024 * 1024

N_CHIPS = 4
HEAD_DIM = 128
N_Q_HEADS = 8
N_KV_HEADS = 2
Q_GROUP = N_Q_HEADS // N_KV_HEADS
ATTN_WIDTH = N_Q_HEADS * HEAD_DIM
KV_WIDTH = N_KV_HEADS * HEAD_DIM
WINDOW = 128
BLOCK = 128
BAND = 3 * BLOCK
ROPE_THETA = 10000.0
CONV_WIDTH = 1024
MEM_HEADS = 4
MEM_WIDTH = MEM_HEADS * HEAD_DIM
RMS_EPS = 1e-6
NEG_INF = -1e30
ATTN_SCALE = HEAD_DIM ** -0.5

Q_OFF, K_OFF, V_OFF, CU_OFF, CB_OFF, CC_OFF, GL_OFF = 0, 1024, 1280, 1536, 2560, 3584, 4608

ADAM_LR = 0.001
ADAM_B1 = 0.9
ADAM_B2 = 0.999
ADAM_EPS = 1e-08
ADAM_WD = 0.01
ADAM_STEP = 10
ADAM_C1 = 1.0 - ADAM_B1 ** ADAM_STEP
ADAM_C2 = 1.0 - ADAM_B2 ** ADAM_STEP


def _params(n_grid_axes):
    return pltpu.CompilerParams(dimension_semantics=("arbitrary",) * n_grid_axes, vmem_limit_bytes=VMEM_LIMIT_BYTES)


BF16_SUBLANES = 16


def _row_tile(rows, want):
    if rows <= want:
        return rows
    for t in range(want, 0, -BF16_SUBLANES):
        if rows % t == 0:
            return t
    return rows


EPILOGUE_CHUNK_COLS = 512

def _matmul(a, b, *, mode, tm, tn, tk, out_dtypes, name, extras=(), epilogue=None, b_blocks=1, out_blocks=1, after=None):
    if mode == "tn":
        kdim, m = a.shape
    else:
        m, kdim = a.shape
    if b_blocks > 1:
        nb, brows, bcols = b.shape
        assert nb == b_blocks
        if mode == "nn":
            n = bcols * nb
            assert brows == kdim
        else:
            assert mode == "nt" and bcols * nb == kdim
            n = brows
    else:
        n = b.shape[0] if mode == "nt" else b.shape[1]
    tm, tn = min(tm, m), min(tn, n)
    assert m % tm == 0 and n % tn == 0 and tk == kdim, (name, m, n, kdim, tm, tn, tk)
    n_extra, n_out = len(extras), len(out_dtypes)
    n_after = 0 if after is None else 1

    if mode == "tn":
        a_spec = pl.BlockSpec((tk, tm), lambda j, i, k: (k, i))
        dims = (((0,), (0,)), ((), ()))
    else:
        a_spec = pl.BlockSpec((tm, tk), lambda j, i, k: (i, k))
        dims = (((1,), (0,)), ((), ())) if mode == "nn" else (((1,), (1,)), ((), ()))

    if b_blocks > 1 and mode == "nn":
        per = b.shape[2] // tn
        assert b.shape[2] % tn == 0
        b_spec = pl.BlockSpec((None, tk, tn), lambda j, i, k: (j // per, k, j % per))
    elif b_blocks > 1:
        b_spec = pl.BlockSpec((b_blocks, tn, b.shape[2]), lambda j, i, k: (0, j, 0))
    elif mode == "nt":
        b_spec = pl.BlockSpec((tn, tk), lambda j, i, k: (j, k))
    else:
        b_spec = pl.BlockSpec((tk, tn), lambda j, i, k: (k, j))

    tile_spec = pl.BlockSpec((tm, tn), lambda j, i, k: (i, j))
    if out_blocks > 1:
        ncols = n // out_blocks
        assert ncols % tn == 0
        oper = ncols // tn
        out_spec = pl.BlockSpec((None, tm, tn), lambda j, i, k: (j // oper, i, j % oper))
        out_shape = [jax.ShapeDtypeStruct((out_blocks, m, ncols), dt) for dt in out_dtypes]
    else:
        out_spec = tile_spec
        out_shape = [jax.ShapeDtypeStruct((m, n), dt) for dt in out_dtypes]

    chunk = EPILOGUE_CHUNK_COLS if epilogue is not None else tn
    chunks = [slice(c0, min(c0 + chunk, tn)) for c0 in range(0, tn, chunk)]

    def body(a_ref, b_ref, *rest):
        extra_refs = rest[:n_extra]
        out_refs = rest[n_extra + n_after:n_extra + n_after + n_out]

        def product(cols):
            if mode == "nn" or mode == "tn":
                return lax.dot_general(a_ref[...].astype(BF16), b_ref[:, cols].astype(BF16), dims, preferred_element_type=F32)
            if b_blocks == 1:
                return lax.dot_general(a_ref[...].astype(BF16), b_ref[cols, :].astype(BF16), dims, preferred_element_type=F32)
            cs = b.shape[2]
            acc = None
            for jb in range(b_blocks):
                prod = lax.dot_general(a_ref[:, jb * cs:(jb + 1) * cs].astype(BF16), b_ref[jb, cols, :].astype(BF16), dims,
                                       preferred_element_type=F32)
                acc = prod if acc is None else acc + prod
            return acc

        for cols in chunks:
            acc = product(cols)
            tiles = (acc,) if epilogue is None else epilogue(acc, *[r[:, cols] for r in extra_refs])
            for o_ref, t in zip(out_refs, tiles, strict=True):
                o_ref[:, cols] = t.astype(o_ref.dtype)

    outs = pl.pallas_call(
        body,
        name=name,
        grid=(n // tn, m // tm, 1),
        in_specs=[a_spec, b_spec] + [tile_spec] * n_extra + [ANY] * n_after,
        out_specs=[out_spec] * n_out,
        out_shape=out_shape,
        compiler_params=_params(3),
    )(a, b, *extras, *([] if after is None else [after]))
    return outs[0] if n_out == 1 else outs


def _add_residual(acc, res):
    return (acc + res,)


def _wgrad_half(a, b, core, *, theirs, row_sharded, tm, tn, name, add=None, after=None):
    kdim, m = a.shape
    n = b.shape[1]
    rs, cs = (m // N_CHIPS, n) if row_sharded else (m, n // N_CHIPS)
    rh = rs // 2
    tm, tn = min(tm, rh), min(tn, cs)
    assert rh % tm == 0 and cs % tn == 0, (name, rh, cs, tm, tn)
    mh, per = rh // tm, cs // tn
    has_add = add is not None

    def half(c):
        return 1 - c[0] if theirs else c[0]

    if row_sharded:
        grid = (n // tn, N_CHIPS * mh)
        a_spec = pl.BlockSpec((kdim, tm), lambda j, r, c: (0, ((r // mh) * 2 + half(c)) * mh + r % mh))
        o_spec = pl.BlockSpec((None, tm, tn), lambda j, r, c: (r // mh, r % mh, j))
    else:
        grid = (n // tn, mh)
        a_spec = pl.BlockSpec((kdim, tm), lambda j, r, c: (0, half(c) * mh + r))
        o_spec = pl.BlockSpec((None, tm, tn), lambda j, r, c: (j // per, r, j % per))
    b_spec = pl.BlockSpec((kdim, tn), lambda j, r, c: (0, j))

    def body(c_ref, a_ref, b_ref, *rest):
        o_ref = rest[-1]
        acc = lax.dot_general(a_ref[...].astype(BF16), b_ref[...].astype(BF16), (((0,), (0,)), ((), ())),
                              preferred_element_type=F32)
        if has_add:
            acc = acc + rest[0][...].astype(F32)
        o_ref[...] = acc.astype(BF16)

    operands = [a, b] + ([add] if has_add else []) + ([] if after is None else [after])
    return pl.pallas_call(
        body, name=name,
        grid_spec=pltpu.PrefetchScalarGridSpec(
            num_scalar_prefetch=1, grid=grid,
            in_specs=[a_spec, b_spec] + ([o_spec] if has_add else []) + ([] if after is None else [ANY]),
            out_specs=o_spec),
        out_shape=jax.ShapeDtypeStruct((N_CHIPS, rh, cs), BF16),
        compiler_params=_params(2),
    )(core, *operands)


def _rstd(x):
    return lax.rsqrt(jnp.mean(x * x, axis=-1, keepdims=True) + RMS_EPS)


def _rmsnorm(x, g, name):
    s, d = x.shape
    tr = _row_tile(s, 256)

    def body(x_ref, g_ref, o_ref):
        xv = x_ref[...]
        o_ref[...] = (xv * _rstd(xv) * g_ref[...]).astype(BF16)

    return pl.pallas_call(
        body, name=name, grid=(s // tr,),
        in_specs=[pl.BlockSpec((tr, d), lambda i: (i, 0)), pl.BlockSpec((1, d), lambda i: (0, 0))],
        out_specs=pl.BlockSpec((tr, d), lambda i: (i, 0)),
        out_shape=jax.ShapeDtypeStruct((s, d), BF16),
        compiler_params=_params(1),
    )(x, g)


def _rmsnorm_bwd(dh, x, g, dres, name):
    s, d = x.shape
    tr = _row_tile(s, 256)
    has_res = dres is not None

    def body(*refs):
        if has_res:
            dh_ref, x_ref, g_ref, res_ref, dx_ref, dxb_ref, dg_ref = refs
        else:
            dh_ref, x_ref, g_ref, dx_ref, dxb_ref, dg_ref = refs
        xv = x_ref[...]
        dhv = dh_ref[...].astype(F32)
        r = _rstd(xv)
        xn = xv * r
        dhg = dhv * g_ref[...]
        dx = r * (dhg - xn * jnp.mean(dhg * xn, axis=-1, keepdims=True))
        if has_res:
            dx = dx + res_ref[...]
        dx_ref[...] = dx
        dxb_ref[...] = dx.astype(BF16)
        part = jnp.sum(dhv * xn, axis=0, keepdims=True)

        @pl.when(pl.program_id(0) == 0)
        def _():
            dg_ref[...] = part

        @pl.when(pl.program_id(0) > 0)
        def _():
            dg_ref[...] += part

    row = pl.BlockSpec((tr, d), lambda i: (i, 0))
    vec = pl.BlockSpec((1, d), lambda i: (0, 0))
    return pl.pallas_call(
        body, name=name, grid=(s // tr,),
        in_specs=[row, row, vec] + ([row] if has_res else []),
        out_specs=[row, row, vec],
        out_shape=[jax.ShapeDtypeStruct((s, d), F32), jax.ShapeDtypeStruct((s, d), BF16), jax.ShapeDtypeStruct((1, d), F32)],
        compiler_params=_params(1),
    )(*([dh, x, g] + ([dres] if has_res else [])))


def _loss_head(x3, g, target):
    s, d = x3.shape
    tr = _row_tile(s, 256)

    def body(x_ref, g_ref, t_ref, dx_ref, dxb_ref, sq_ref, dg_ref):
        xv = x_ref[...]
        gv = g_ref[...]
        r = _rstd(xv)
        xn = xv * r
        err = xn * gv - t_ref[...]
        dy = err * (1.0 / d)
        dyg = dy * gv
        dx = r * (dyg - xn * jnp.mean(dyg * xn, axis=-1, keepdims=True))
        dx_ref[...] = dx
        dxb_ref[...] = dx.astype(BF16)
        sq = jnp.sum(jnp.sum(err * err, axis=1, keepdims=True), axis=0, keepdims=True)
        sq = jnp.broadcast_to(sq, (1, 128))
        part = jnp.sum(dy * xn, axis=0, keepdims=True)

        @pl.when(pl.program_id(0) == 0)
        def _():
            sq_ref[...] = sq
            dg_ref[...] = part

        @pl.when(pl.program_id(0) > 0)
        def _():
            sq_ref[...] += sq
            dg_ref[...] += part

    row = pl.BlockSpec((tr, d), lambda i: (i, 0))
    vec = pl.BlockSpec((1, d), lambda i: (0, 0))
    return pl.pallas_call(
        body, name="loss_head", grid=(s // tr,),
        in_specs=[row, vec, row],
        out_specs=[row, row, pl.BlockSpec((1, 128), lambda i: (0, 0)), vec],
        out_shape=[jax.ShapeDtypeStruct((s, d), F32), jax.ShapeDtypeStruct((s, d), BF16),
                   jax.ShapeDtypeStruct((1, 128), F32), jax.ShapeDtypeStruct((1, d), F32)],
        compiler_params=_params(1),
    )(x3, g, target)


def _rope_tables(s):
    inv = 1.0 / (ROPE_THETA ** (jnp.arange(0, HEAD_DIM, 2, dtype=F32) / HEAD_DIM))
    ang = jnp.arange(s, dtype=F32)[:, None] * inv[None, :]
    cos, sin = jnp.cos(ang), jnp.sin(ang)
    return jnp.concatenate([cos, cos], axis=1), jnp.concatenate([-sin, sin], axis=1)


def _swap_halves(t):
    return pltpu.roll(t, HEAD_DIM // 2, 1)


def _rope_fwd(z, cos_t, sin_t):
    s = z.shape[0]
    tr = _row_tile(s, 256)

    def body(zq_ref, zk_ref, zv_ref, c_ref, s_ref, q_ref, k_ref, v_ref):
        c, sn = c_ref[...], s_ref[...]
        for hd in range(N_Q_HEADS):
            cols = slice(hd * HEAD_DIM, (hd + 1) * HEAD_DIM)
            t = zq_ref[:, cols]
            q_ref[:, cols] = (t * c + _swap_halves(t) * sn).astype(BF16)
        for hd in range(N_KV_HEADS):
            cols = slice(hd * HEAD_DIM, (hd + 1) * HEAD_DIM)
            t = zk_ref[:, cols]
            k_ref[:, cols] = (t * c + _swap_halves(t) * sn).astype(BF16)
        v_ref[...] = zv_ref[...].astype(BF16)

    tab = pl.BlockSpec((tr, HEAD_DIM), lambda i: (i, 0))
    return pl.pallas_call(
        body, name="rope_fwd", grid=(s // tr,),
        in_specs=[pl.BlockSpec((tr, ATTN_WIDTH), lambda i: (i, Q_OFF // ATTN_WIDTH)),
                  pl.BlockSpec((tr, KV_WIDTH), lambda i: (i, K_OFF // KV_WIDTH)),
                  pl.BlockSpec((tr, KV_WIDTH), lambda i: (i, V_OFF // KV_WIDTH)), tab, tab],
        out_specs=[pl.BlockSpec((tr, ATTN_WIDTH), lambda i: (i, 0)), pl.BlockSpec((tr, KV_WIDTH), lambda i: (i, 0)),
                   pl.BlockSpec((tr, KV_WIDTH), lambda i: (i, 0))],
        out_shape=[jax.ShapeDtypeStruct((s, ATTN_WIDTH), BF16), jax.ShapeDtypeStruct((s, KV_WIDTH), BF16),
                   jax.ShapeDtypeStruct((s, KV_WIDTH), BF16)],
        compiler_params=_params(1),
    )(z, z, z, cos_t, sin_t)


def _rope_bwd(dq_rot, dk_rot, dv, cos_t, sin_t):
    s = dq_rot.shape[0]
    tr = _row_tile(s, 256)

    def body(dq_ref, dk_ref, dv_ref, c_ref, s_ref, oq_ref, ok_ref, ov_ref):
        c, sn = c_ref[...], s_ref[...]
        for hd in range(N_Q_HEADS):
            cols = slice(hd * HEAD_DIM, (hd + 1) * HEAD_DIM)
            t = dq_ref[:, cols]
            oq_ref[:, cols] = (t * c + _swap_halves(t * sn)).astype(BF16)
        for hd in range(N_KV_HEADS):
            cols = slice(hd * HEAD_DIM, (hd + 1) * HEAD_DIM)
            t = dk_ref[:, cols]
            ok_ref[:, cols] = (t * c + _swap_halves(t * sn)).astype(BF16)
        ov_ref[...] = dv_ref[...].astype(BF16)

    tab = pl.BlockSpec((tr, HEAD_DIM), lambda i: (i, 0))
    wide = pl.BlockSpec((tr, ATTN_WIDTH), lambda i: (i, 0))
    narrow = pl.BlockSpec((tr, KV_WIDTH), lambda i: (i, 0))
    return pl.pallas_call(
        body, name="rope_bwd", grid=(s // tr,),
        in_specs=[wide, narrow, narrow, tab, tab],
        out_specs=[wide, narrow, narrow],
        out_shape=[jax.ShapeDtypeStruct((s, ATTN_WIDTH), BF16), jax.ShapeDtypeStruct((s, KV_WIDTH), BF16),
                   jax.ShapeDtypeStruct((s, KV_WIDTH), BF16)],
        compiler_params=_params(1),
    )(dq_rot, dk_rot, dv, cos_t, sin_t)


def _swa_band(i, s):
    return pl.multiple_of(jnp.clip((i - 1) * BLOCK, 0, s - BAND), BLOCK)


def _swa_probs(q_ref, k_ref, sink_ref, kv, start, valid):
    cols = slice(kv * HEAD_DIM, (kv + 1) * HEAD_DIM)
    kb = k_ref[pl.ds(start, BAND), cols]
    heads = [kv * Q_GROUP + g for g in range(Q_GROUP)]
    qg = jnp.concatenate([q_ref[:, hd * HEAD_DIM:(hd + 1) * HEAD_DIM] for hd in heads], axis=0)
    sc = lax.dot_general(qg, kb, (((1,), (1,)), ((), ())), preferred_element_type=F32) * ATTN_SCALE
    sc = jnp.where(valid, sc, NEG_INF)
    sk = jnp.concatenate([jnp.full((BLOCK, 1), sink_ref[hd], F32) for hd in heads], axis=0)
    mx = jnp.maximum(jnp.max(sc, axis=1, keepdims=True), sk)
    e = jnp.exp(sc - mx)
    es = jnp.exp(sk - mx)
    inv = 1.0 / (jnp.sum(e, axis=1, keepdims=True) + es)
    return qg, kb, e * inv, es * inv


def _swa_valid(i, start):
    q_pos = i * BLOCK + lax.broadcasted_iota(jnp.int32, (BLOCK, 1), 0)
    q_pos = jnp.concatenate([q_pos] * Q_GROUP, axis=0)
    k_pos = start + lax.broadcasted_iota(jnp.int32, (1, BAND), 1)
    return jnp.abs(k_pos - q_pos) <= WINDOW


def _swa_fwd(q, k, v, sink):
    s = q.shape[0]
    assert s % BLOCK == 0 and s >= BAND

    def body(sink_ref, q_ref, k_ref, v_ref, o_ref):
        i = pl.program_id(0)
        start = _swa_band(i, s)
        valid = _swa_valid(i, start)
        for kv in range(N_KV_HEADS):
            _, _, p, _ = _swa_probs(q_ref, k_ref, sink_ref, kv, start, valid)
            vb = v_ref[pl.ds(start, BAND), kv * HEAD_DIM:(kv + 1) * HEAD_DIM]
            o = jnp.dot(p.astype(BF16), vb, preferred_element_type=F32)
            for g in range(Q_GROUP):
                hd = kv * Q_GROUP + g
                o_ref[:, hd * HEAD_DIM:(hd + 1) * HEAD_DIM] = o[g * BLOCK:(g + 1) * BLOCK].astype(BF16)

    whole = pl.BlockSpec((s, KV_WIDTH), lambda i: (0, 0))
    blk = pl.BlockSpec((BLOCK, ATTN_WIDTH), lambda i: (i, 0))
    return pl.pallas_call(
        body, name="swa_fwd", grid=(s // BLOCK,),
        in_specs=[pl.BlockSpec(memory_space=pltpu.SMEM), blk, whole, whole],
        out_specs=blk,
        out_shape=jax.ShapeDtypeStruct((s, ATTN_WIDTH), BF16),
        compiler_params=_params(1),
    )(sink, q, k, v)


def _swa_bwd(q, k, v, d_out, sink):
    s = q.shape[0]

    def body(sink_ref, q_ref, k_ref, v_ref, do_ref, dq_ref, dk_ref, dv_ref, dsink_ref):
        i = pl.program_id(0)

        @pl.when(i == 0)
        def _():
            dk_ref[...] = jnp.zeros_like(dk_ref)
            dv_ref[...] = jnp.zeros_like(dv_ref)
            dsink_ref[...] = jnp.zeros_like(dsink_ref)

        start = _swa_band(i, s)
        valid = _swa_valid(i, start)
        for kv in range(N_KV_HEADS):
            cols = slice(kv * HEAD_DIM, (kv + 1) * HEAD_DIM)
            qg, kb, p, p_sink = _swa_probs(q_ref, k_ref, sink_ref, kv, start, valid)
            vb = v_ref[pl.ds(start, BAND), cols]
            heads = [kv * Q_GROUP + g for g in range(Q_GROUP)]
            dog = jnp.concatenate([do_ref[:, hd * HEAD_DIM:(hd + 1) * HEAD_DIM] for hd in heads], axis=0)
            dp = lax.dot_general(dog, vb, (((1,), (1,)), ((), ())), preferred_element_type=F32)
            delta = jnp.sum(p * dp, axis=1, keepdims=True)
            ds = (p * (dp - delta) * ATTN_SCALE).astype(BF16)
            dqg = jnp.dot(ds, kb, preferred_element_type=F32)
            dk_ref[pl.ds(start, BAND), cols] += lax.dot_general(ds, qg, (((0,), (0,)), ((), ())), preferred_element_type=F32)
            dv_ref[pl.ds(start, BAND), cols] += lax.dot_general(p.astype(BF16), dog, (((0,), (0,)), ((), ())),
                                                                 preferred_element_type=F32)
            dsk = p_sink * delta
            for g, hd in enumerate(heads):
                dq_ref[:, hd * HEAD_DIM:(hd + 1) * HEAD_DIM] = dqg[g * BLOCK:(g + 1) * BLOCK]
                tot = jnp.sum(dsk[g * BLOCK:(g + 1) * BLOCK], axis=0, keepdims=True)
                dsink_ref[hd:hd + 1, :] -= jnp.broadcast_to(tot, (1, 128))

    whole = pl.BlockSpec((s, KV_WIDTH), lambda i: (0, 0))
    blk = pl.BlockSpec((BLOCK, ATTN_WIDTH), lambda i: (i, 0))
    return pl.pallas_call(
        body, name="swa_bwd", grid=(s // BLOCK,),
        in_specs=[pl.BlockSpec(memory_space=pltpu.SMEM), blk, whole, whole, blk],
        out_specs=[blk, whole, whole, pl.BlockSpec((N_Q_HEADS, 128), lambda i: (0, 0))],
        out_shape=[jax.ShapeDtypeStruct((s, ATTN_WIDTH), F32), jax.ShapeDtypeStruct((s, KV_WIDTH), F32),
                   jax.ShapeDtypeStruct((s, KV_WIDTH), F32), jax.ShapeDtypeStruct((N_Q_HEADS, 128), F32)],
        compiler_params=_params(1),
    )(sink, q, k, v, d_out)


CONV_CHUNK = 256


def _shift_rows(t, rows, down):
    n = t.shape[0]
    rolled = pltpu.roll(t, 1 if down else n - 1, 0)
    edge = 0 if down else n - 1
    return jnp.where(rows == edge, 0.0, rolled)


def _conv_specs(s):
    def z_spec(off):
        return pl.BlockSpec((s, CONV_CHUNK), lambda j, off=off: (0, off // CONV_CHUNK + j))
    chunk = pl.BlockSpec((s, CONV_CHUNK), lambda j: (0, j))
    w_spec = pl.BlockSpec((3, CONV_CHUNK), lambda j: (0, j))
    return z_spec(CU_OFF), z_spec(CB_OFF), z_spec(CC_OFF), chunk, w_spec


def _conv_fwd(z, conv_w):
    s = z.shape[0]
    cu_spec, cb_spec, cc_spec, chunk, w_spec = _conv_specs(s)

    def body(cu_ref, cb_ref, cc_ref, w_ref, o_ref):
        rows = lax.broadcasted_iota(jnp.int32, (s, 1), 0)
        t = cc_ref[...] * cu_ref[...]
        c3 = _shift_rows(t, rows, True) * w_ref[0:1, :] + t * w_ref[1:2, :] + _shift_rows(t, rows, False) * w_ref[2:3, :]
        o_ref[...] = (cb_ref[...] * c3).astype(BF16)

    return pl.pallas_call(
        body, name="conv_fwd", grid=(CONV_WIDTH // CONV_CHUNK,),
        in_specs=[cu_spec, cb_spec, cc_spec, w_spec],
        out_specs=chunk,
        out_shape=jax.ShapeDtypeStruct((s, CONV_WIDTH), BF16),
        compiler_params=_params(1),
    )(z, z, z, conv_w)


def _conv_bwd(z, conv_w, d_co):
    s = z.shape[0]
    cu_spec, cb_spec, cc_spec, chunk, w_spec = _conv_specs(s)

    def body(cu_ref, cb_ref, cc_ref, w_ref, d_ref, dcu_ref, dcb_ref, dcc_ref, dw_ref):
        rows = lax.broadcasted_iota(jnp.int32, (s, 1), 0)
        cu, cc = cu_ref[...], cc_ref[...]
        t = cc * cu
        t_dn, t_up = _shift_rows(t, rows, True), _shift_rows(t, rows, False)
        c3 = t_dn * w_ref[0:1, :] + t * w_ref[1:2, :] + t_up * w_ref[2:3, :]
        d = d_ref[...]
        dcb_ref[...] = (d * c3).astype(BF16)
        dc3 = d * cb_ref[...]
        dw_ref[0:1, :] = jnp.sum(dc3 * t_dn, axis=0, keepdims=True)
        dw_ref[1:2, :] = jnp.sum(dc3 * t, axis=0, keepdims=True)
        dw_ref[2:3, :] = jnp.sum(dc3 * t_up, axis=0, keepdims=True)
        dt = _shift_rows(dc3, rows, False) * w_ref[0:1, :] + dc3 * w_ref[1:2, :] + _shift_rows(dc3, rows, True) * w_ref[2:3, :]
        dcc_ref[...] = (dt * cu).astype(BF16)
        dcu_ref[...] = (dt * cc).astype(BF16)

    return pl.pallas_call(
        body, name="conv_bwd", grid=(CONV_WIDTH // CONV_CHUNK,),
        in_specs=[cu_spec, cb_spec, cc_spec, w_spec, chunk],
        out_specs=[chunk, chunk, chunk, w_spec],
        out_shape=[jax.ShapeDtypeStruct((s, CONV_WIDTH), BF16)] * 3 + [jax.ShapeDtypeStruct((3, CONV_WIDTH), F32)],
        compiler_params=_params(1),
    )(z, z, z, conv_w, d_co)


GATE_CHUNK = 512


def _gate_specs(s, d, tr):
    n_chunks = d // GATE_CHUNK
    za = pl.BlockSpec((tr, GATE_CHUNK), lambda j, i: (i, GL_OFF // GATE_CHUNK + j))
    zc = pl.BlockSpec((tr, GATE_CHUNK), lambda j, i: (i, GL_OFF // GATE_CHUNK + n_chunks + j))
    ba = pl.BlockSpec((1, GATE_CHUNK), lambda j, i: (0, j))
    bc = pl.BlockSpec((1, GATE_CHUNK), lambda j, i: (0, n_chunks + j))
    tile = pl.BlockSpec((tr, GATE_CHUNK), lambda j, i: (i, j))
    return za, zc, ba, bc, tile


def _gate_fwd(z, b_gate, ya, yc):
    s, d = ya.shape
    tr = _row_tile(s, 512)
    za, zc, ba, bc, tile = _gate_specs(s, d, tr)

    def body(za_ref, zc_ref, ba_ref, bc_ref, ya_ref, yc_ref, o_ref):
        ga = jax.nn.sigmoid(za_ref[...] + ba_ref[...])
        gc = jax.nn.sigmoid(zc_ref[...] + bc_ref[...])
        o_ref[...] = (ga * ya_ref[...] + gc * yc_ref[...]).astype(BF16)

    return pl.pallas_call(
        body, name="gate_fwd", grid=(d // GATE_CHUNK, s // tr),
        in_specs=[za, zc, ba, bc, tile, tile],
        out_specs=tile,
        out_shape=jax.ShapeDtypeStruct((s, d), BF16),
        compiler_params=_params(2),
    )(z, z, b_gate, b_gate, ya, yc)


def _gate_bwd(z, b_gate, ya, yc, dmix):
    s, d = ya.shape
    tr = _row_tile(s, 512)
    za, zc, ba, bc, tile = _gate_specs(s, d, tr)
    vec = pl.BlockSpec((1, GATE_CHUNK), lambda j, i: (0, j))

    def body(za_ref, zc_ref, ba_ref, bc_ref, ya_ref, yc_ref, dm_ref, dya_ref, dyc_ref, dla_ref, dlc_ref, dba_ref, dbc_ref):
        ga = jax.nn.sigmoid(za_ref[...] + ba_ref[...])
        gc = jax.nn.sigmoid(zc_ref[...] + bc_ref[...])
        dm = dm_ref[...]
        dya_ref[...] = (dm * ga).astype(BF16)
        dyc_ref[...] = (dm * gc).astype(BF16)
        dla = dm * ya_ref[...] * ga * (1.0 - ga)
        dlc = dm * yc_ref[...] * gc * (1.0 - gc)
        dla_ref[...] = dla.astype(BF16)
        dlc_ref[...] = dlc.astype(BF16)
        pa = jnp.sum(dla, axis=0, keepdims=True)
        pc = jnp.sum(dlc, axis=0, keepdims=True)

        @pl.when(pl.program_id(1) == 0)
        def _():
            dba_ref[...] = pa
            dbc_ref[...] = pc

        @pl.when(pl.program_id(1) > 0)
        def _():
            dba_ref[...] += pa
            dbc_ref[...] += pc

    big = jax.ShapeDtypeStruct((s, d), BF16)
    small = jax.ShapeDtypeStruct((1, d), F32)
    return pl.pallas_call(
        body, name="gate_bwd", grid=(d // GATE_CHUNK, s // tr),
        in_specs=[za, zc, ba, bc, tile, tile, tile],
        out_specs=[tile, tile, tile, tile, vec, vec],
        out_shape=[big, big, big, big, small, small],
        compiler_params=_params(2),
    )(z, z, b_gate, b_gate, ya, yc, dmix)


def _cross_probs(q_ref, kv_ref, hd):
    cols = slice(hd * HEAD_DIM, (hd + 1) * HEAD_DIM)
    qh = q_ref[:, cols]
    kh = kv_ref[:, cols]
    sc = lax.dot_general(qh, kh, (((1,), (1,)), ((), ())), preferred_element_type=F32) * ATTN_SCALE
    e = jnp.exp(sc - jnp.max(sc, axis=1, keepdims=True))
    return qh, kh, e * (1.0 / jnp.sum(e, axis=1, keepdims=True))


def _cross_fwd(qc, kvc):
    s = qc.shape[0]
    n_mem = kvc.shape[0]
    tq = _row_tile(s, 256)

    def body(q_ref, kv_ref, o_ref):
        for hd in range(MEM_HEADS):
            _, _, p = _cross_probs(q_ref, kv_ref, hd)
            vh = kv_ref[:, MEM_WIDTH + hd * HEAD_DIM:MEM_WIDTH + (hd + 1) * HEAD_DIM]
            o_ref[:, hd * HEAD_DIM:(hd + 1) * HEAD_DIM] = jnp.dot(p.astype(BF16), vh, preferred_element_type=F32).astype(BF16)

    return pl.pallas_call(
        body, name="cross_fwd", grid=(s // tq,),
        in_specs=[pl.BlockSpec((tq, MEM_WIDTH), lambda i: (i, 0)), pl.BlockSpec((n_mem, 2 * MEM_WIDTH), lambda i: (0, 0))],
        out_specs=pl.BlockSpec((tq, MEM_WIDTH), lambda i: (i, 0)),
        out_shape=jax.ShapeDtypeStruct((s, MEM_WIDTH), BF16),
        compiler_params=_params(1),
    )(qc, kvc)


def _cross_bwd(qc, kvc, d_out):
    s = qc.shape[0]
    n_mem = kvc.shape[0]
    tq = _row_tile(s, 256)

    def body(q_ref, kv_ref, do_ref, dq_ref, dkv_ref):
        @pl.when(pl.program_id(0) == 0)
        def _():
            dkv_ref[...] = jnp.zeros_like(dkv_ref)

        for hd in range(MEM_HEADS):
            cols = slice(hd * HEAD_DIM, (hd + 1) * HEAD_DIM)
            vcols = slice(MEM_WIDTH + hd * HEAD_DIM, MEM_WIDTH + (hd + 1) * HEAD_DIM)
            qh, kh, p = _cross_probs(q_ref, kv_ref, hd)
            doh = do_ref[:, cols]
            dp = lax.dot_general(doh, kv_ref[:, vcols], (((1,), (1,)), ((), ())), preferred_element_type=F32)
            ds = (p * (dp - jnp.sum(p * dp, axis=1, keepdims=True)) * ATTN_SCALE).astype(BF16)
            dq_ref[:, cols] = jnp.dot(ds, kh, preferred_element_type=F32).astype(BF16)
            dkv_ref[:, cols] += lax.dot_general(ds, qh, (((0,), (0,)), ((), ())), preferred_element_type=F32)
            dkv_ref[:, vcols] += lax.dot_general(p.astype(BF16), doh, (((0,), (0,)), ((), ())), preferred_element_type=F32)

    qspec = pl.BlockSpec((tq, MEM_WIDTH), lambda i: (i, 0))
    kvspec = pl.BlockSpec((n_mem, 2 * MEM_WIDTH), lambda i: (0, 0))
    return pl.pallas_call(
        body, name="cross_bwd", grid=(s // tq,),
        in_specs=[qspec, kvspec, qspec],
        out_specs=[qspec, kvspec],
        out_shape=[jax.ShapeDtypeStruct((s, MEM_WIDTH), BF16), jax.ShapeDtypeStruct((n_mem, 2 * MEM_WIDTH), F32)],
        compiler_params=_params(1),
    )(qc, kvc, d_out)


def _swiglu_fwd(up, gate):
    return up, (gate * jax.nn.sigmoid(gate)) * up


def _swiglu_bwd(d_act, gate, up):
    sg = jax.nn.sigmoid(gate)
    silu = gate * sg
    return d_act * up * (sg * (1.0 + gate * (1.0 - sg))), d_act * silu


GATHER_GROUPS = {"in": ("w_in", "conv_w"), "mix": ("w_attn_out", "w_conv_out", "w_o"), "cross": ("w_cq", "w_ckv", "w_co"),
                 "gate": ("w_gate",), "up": ("w_up",), "down": ("w_down",)}


def _local_step(xs, mems, target, small, fetch, reduce):
    s, d = xs.shape
    w4 = {}
    cos_t, sin_t = _rope_tables(s)

    h = _rmsnorm(xs, small["g_mix"], "norm_mix")
    w4.update(fetch.get("in", h))
    conv4 = w4["conv_w"]
    conv_w = conv4[:, :3, :].transpose(1, 0, 2).reshape(3, N_CHIPS * conv4.shape[2])
    c_in = w4["w_in"].shape[2]
    z = _matmul(h, w4["w_in"], mode="nn", tm=512, tn=c_in, tk=d, out_dtypes=[F32], name="in_proj", b_blocks=N_CHIPS)
    fetch.begin("mix", z)
    q_rot, k_rot, v_b = _rope_fwd(z, cos_t, sin_t)
    attn = _swa_fwd(q_rot, k_rot, v_b, small["sink"])
    co = _conv_fwd(z, conv_w)
    w4.update(fetch.get("mix", co))
    tok = fetch.begin("cross", attn)
    w_o = w4["w_o"].reshape(-1, w4["w_o"].shape[-1])
    c_d = w4["w_attn_out"].shape[2]
    ya = _matmul(attn, w4["w_attn_out"], mode="nn", tm=1024, tn=c_d, tk=ATTN_WIDTH, out_dtypes=[F32], name="attn_out_proj",
                 b_blocks=N_CHIPS, after=tok)
    yc = _matmul(co, w4["w_conv_out"], mode="nn", tm=1024, tn=c_d, tk=CONV_WIDTH, out_dtypes=[F32], name="conv_out_proj",
                 b_blocks=N_CHIPS)
    mix = _gate_fwd(z, small["b_gate"], ya, yc)
    x1 = _matmul(mix, w_o, mode="nn", tm=512, tn=1024, tk=d, out_dtypes=[F32], name="mix_out_proj", extras=[xs],
                 epilogue=_add_residual)
    w4.update(fetch.get("cross", x1))
    tok = fetch.begin("gate", x1)
    w_cq = w4["w_cq"].reshape(-1, w4["w_cq"].shape[-1])
    w_ckv = w4["w_ckv"].reshape(-1, w4["w_ckv"].shape[-1])
    hc = _rmsnorm(x1, small["g_cross"], "norm_cross")
    memn = _rmsnorm(mems, small["g_mem"], "norm_mem")
    qc = _matmul(hc, w_cq, mode="nn", tm=1024, tn=MEM_WIDTH, tk=d, out_dtypes=[BF16], name="cross_q_proj", after=tok)
    kvc = _matmul(memn, w_ckv, mode="nn", tm=256, tn=2 * MEM_WIDTH, tk=d, out_dtypes=[BF16], name="cross_kv_proj")
    oc = _cross_fwd(qc, kvc)
    x2 = _matmul(oc, w4["w_co"], mode="nn", tm=1024, tn=c_d, tk=MEM_WIDTH, out_dtypes=[F32], name="cross_out_proj",
                 extras=[x1], epilogue=_add_residual, b_blocks=N_CHIPS)
    hf = _rmsnorm(x2, small["g_ffn"], "norm_ffn")
    w4.update(fetch.get("gate", hf))
    tok = fetch.begin("up", hf)
    c_ff = w4["w_gate"].shape[2]
    gate = _matmul(hf, w4["w_gate"], mode="nn", tm=512, tn=c_ff, tk=d, out_dtypes=[F32], name="ffn_gate_proj", b_blocks=N_CHIPS,
                   after=tok)
    w4.update(fetch.get("up", gate))
    up, act = _matmul(hf, w4["w_up"], mode="nn", tm=512, tn=c_ff, tk=d, out_dtypes=[F32, BF16], name="ffn_up_proj",
                      extras=[gate], epilogue=_swiglu_fwd, b_blocks=N_CHIPS)
    w4.update(fetch.get("down", act))
    w_down = w4["w_down"].reshape(-1, w4["w_down"].shape[-1])
    x3 = _matmul(act, w_down, mode="nn", tm=512, tn=512, tk=w_down.shape[0], out_dtypes=[F32], name="ffn_down_proj", extras=[x2],
                 epilogue=_add_residual)
    dx3, dx3b, sq, dg_final = _loss_head(x3, small["g_final"], target)

    da, du = _matmul(dx3b, w_down, mode="nt", tm=512, tn=c_ff, tk=d, out_dtypes=[BF16, BF16], name="ffn_down_bwd",
                     extras=[gate, up], epilogue=_swiglu_bwd)
    core = reduce.core
    ffn_shape = dict(row_sharded=False, tm=512, tn=c_ff)
    g_down = _matmul(act, dx3b, mode="tn", tm=c_ff, tn=1024, tk=s, out_dtypes=[BF16], name="ffn_down_wgrad")
    tok = reduce.add("down", {"w_down": g_down}, da)
    t_gate = _wgrad_half(hf, da, core, theirs=True, name="ffn_gate_wgrad_theirs", after=tok, **ffn_shape)
    tok = reduce.step("down", t_gate)
    t_up = _wgrad_half(hf, du, core, theirs=True, name="ffn_up_wgrad_theirs", after=tok, **ffn_shape)
    tok = reduce.send("ffn", {"w_gate": t_gate, "w_up": t_up}, dx3b)
    dhf = _matmul(da, w4["w_gate"], mode="nt", tm=512, tn=1024, tk=N_CHIPS * c_ff, out_dtypes=[F32], name="ffn_gate_bwd", b_blocks=N_CHIPS,
                  after=tok)
    got = reduce.received("ffn", dhf)
    p_gate = _wgrad_half(hf, da, core, theirs=False, name="ffn_gate_wgrad_mine", add=got["w_gate"], **ffn_shape)
    p_up = _wgrad_half(hf, du, core, theirs=False, name="ffn_up_wgrad_mine", add=got["w_up"], **ffn_shape)
    tok = reduce.add_parts("ffn", {"w_gate": p_gate, "w_up": p_up})
    dhf = _matmul(du, w4["w_up"], mode="nt", tm=512, tn=1024, tk=N_CHIPS * c_ff, out_dtypes=[F32], name="ffn_up_bwd", extras=[dhf],
                  epilogue=_add_residual, b_blocks=N_CHIPS, after=tok)
    tok = reduce.step("down", dhf)
    dx2, dx2b, dg_ffn = _rmsnorm_bwd(dhf, x2, small["g_ffn"], dx3, "norm_ffn_bwd")

    d_oc = _matmul(dx2b, w4["w_co"], mode="nt", tm=1024, tn=MEM_WIDTH, tk=d, out_dtypes=[BF16], name="cross_out_bwd",
                   b_blocks=N_CHIPS, after=tok)
    g_co = _matmul(oc, dx2b, mode="tn", tm=MEM_WIDTH, tn=c_d, tk=s, out_dtypes=[BF16], name="cross_out_wgrad", out_blocks=N_CHIPS)
    tok = reduce.step("down", g_co)
    dqc, dkvc = _cross_bwd(qc, kvc, d_oc)
    g_cq = _matmul(hc, dqc, mode="tn", tm=1024, tn=MEM_WIDTH, tk=s, out_dtypes=[BF16], name="cross_q_wgrad", after=tok)
    dhc = _matmul(dqc, w_cq, mode="nt", tm=1024, tn=1024, tk=MEM_WIDTH, out_dtypes=[F32], name="cross_q_bwd")
    g_ckv = _matmul(memn, dkvc, mode="tn", tm=1024, tn=2 * MEM_WIDTH, tk=mems.shape[0], out_dtypes=[BF16], name="cross_kv_wgrad")
    dmemn = _matmul(dkvc, w_ckv, mode="nt", tm=256, tn=1024, tk=2 * MEM_WIDTH, out_dtypes=[F32], name="cross_kv_bwd")
    _, _, dg_mem = _rmsnorm_bwd(dmemn, mems, small["g_mem"], None, "norm_mem_bwd")
    dx1, dx1b, dg_cross = _rmsnorm_bwd(dhc, x1, small["g_cross"], dx2, "norm_cross_bwd")

    dmix = _matmul(dx1b, w_o, mode="nt", tm=512, tn=1024, tk=d, out_dtypes=[F32], name="mix_out_bwd")
    tok = reduce.step("ffn", dmix)
    g_o = _matmul(mix, dx1b, mode="tn", tm=1024, tn=1024, tk=s, out_dtypes=[BF16], name="mix_out_wgrad", after=tok)
    dya, dyc, dgl_a, dgl_c, db_a, db_c = _gate_bwd(z, small["b_gate"], ya, yc, dmix)
    d_attn = _matmul(dya, w4["w_attn_out"], mode="nt", tm=1024, tn=ATTN_WIDTH, tk=d, out_dtypes=[BF16], name="attn_out_bwd",
                     b_blocks=N_CHIPS)
    g_ao = _matmul(attn, dya, mode="tn", tm=ATTN_WIDTH, tn=c_d, tk=s, out_dtypes=[BF16], name="attn_out_wgrad", out_blocks=N_CHIPS)
    d_co = _matmul(dyc, w4["w_conv_out"], mode="nt", tm=1024, tn=CONV_WIDTH, tk=d, out_dtypes=[F32], name="conv_out_bwd",
                   b_blocks=N_CHIPS)
    g_cvo = _matmul(co, dyc, mode="tn", tm=CONV_WIDTH, tn=c_d, tk=s, out_dtypes=[BF16], name="conv_out_wgrad", out_blocks=N_CHIPS)
    tok = reduce.add("mid", {"w_co": g_co, "w_cq": g_cq, "w_ckv": g_ckv, "w_o": g_o, "w_attn_out": g_ao, "w_conv_out": g_cvo}, d_co)
    dcu, dcb, dcc, d_conv_w = _conv_bwd(z, conv_w, d_co)
    dq_rot, dk_rot, dv, dsink = _swa_bwd(q_rot, k_rot, v_b, d_attn, small["sink"])
    tok = reduce.step("mid", dq_rot)
    dq, dk, dvb = _rope_bwd(dq_rot, dk_rot, dv, cos_t, sin_t)
    dz = jnp.concatenate([dq, dk, dvb, dcu, dcb, dcc, dgl_a, dgl_c], axis=1)
    in_shape = dict(row_sharded=False, tm=512, tn=c_in)
    t_in = _wgrad_half(h, dz, core, theirs=True, name="in_proj_wgrad_theirs", after=tok, **in_shape)
    tok = reduce.send("in", {"w_in": t_in}, dk)
    tok = reduce.step("ffn", tok)
    tok = reduce.step("mid", tok)
    got = reduce.received("in", tok)
    p_in = _wgrad_half(h, dz, core, theirs=False, name="in_proj_wgrad_mine", add=got["w_in"], **in_shape)
    tok = reduce.add_parts("in", {"w_in": p_in})
    dh = _matmul(dz, w4["w_in"], mode="nt", tm=512, tn=512, tk=N_CHIPS * c_in, out_dtypes=[F32], name="in_proj_bwd", b_blocks=N_CHIPS,
                 after=tok)
    tok = reduce.step("mid", dh)
    grad_x, _, dg_mix = _rmsnorm_bwd(dh, xs, small["g_mix"], dx1, "norm_mix_bwd")

    small_grads = {
        "g_mix": dg_mix, "sink": dsink[:, 0], "b_gate": jnp.concatenate([db_a, db_c], axis=1), "g_cross": dg_cross,
        "g_mem": dg_mem, "g_ffn": dg_ffn, "g_final": dg_final, "conv_w": d_conv_w,
    }
    return sq, grad_x, small_grads


def _pair_sum(g4, ra, core, name):
    nb, rs, cs = g4.shape
    rh = rs // 2
    tr = _row_tile(rh, 256)
    per = rh // tr

    def body(c_ref, g_ref, r_ref, o_ref):
        o_ref[...] = (g_ref[...].astype(F32) + r_ref[...].astype(F32)).astype(BF16)

    plain = pl.BlockSpec((None, tr, cs), lambda j, i, c: (j, i, 0))
    return pl.pallas_call(
        body, name=name,
        grid_spec=pltpu.PrefetchScalarGridSpec(
            num_scalar_prefetch=1, grid=(nb, per),
            in_specs=[pl.BlockSpec((None, tr, cs), lambda j, i, c: (j, c[0] * per + i, 0)), plain],
            out_specs=plain),
        out_shape=jax.ShapeDtypeStruct((nb, rh, cs), BF16),
        compiler_params=_params(2),
    )(core, g4, ra)


def _quad_sum(parts, rc, place, name):
    _, rh, cs = parts.shape
    tr = _row_tile(rh, 256)
    per = rh // tr

    def body(p_ref, own_ref, r_ref, o_ref):
        acc = own_ref[...].astype(F32)
        for j in range(rc.shape[0]):
            acc = acc + r_ref[j].astype(F32)
        o_ref[...] = acc

    return pl.pallas_call(
        body, name=name,
        grid_spec=pltpu.PrefetchScalarGridSpec(
            num_scalar_prefetch=1, grid=(per,),
            in_specs=[pl.BlockSpec((None, tr, cs), lambda i, p: (p[0], i, 0)),
                      pl.BlockSpec((rc.shape[0], tr, cs), lambda i, p: (0, i, 0))],
            out_specs=pl.BlockSpec((tr, cs), lambda i, p: (p[1] * per + i, 0))),
        out_shape=jax.ShapeDtypeStruct((2 * rh, cs), F32),
        compiler_params=_params(1),
    )(place, parts, rc)


def _cast_to_slot(w, place, dtype, name, after=None):
    rows, cols = w.shape
    tr = _row_tile(rows, 256)

    def body(p_ref, w_ref, *rest):
        o_ref = rest[-1]
        o_ref[...] = w_ref[...].astype(dtype)

    return pl.pallas_call(
        body, name=name,
        grid_spec=pltpu.PrefetchScalarGridSpec(
            num_scalar_prefetch=1, grid=(rows // tr,),
            in_specs=[pl.BlockSpec((tr, cols), lambda i, p: (i, 0))] + ([] if after is None else [ANY]),
            out_specs=pl.BlockSpec((None, tr, cols), lambda i, p: (p[0], i, 0))),
        out_shape=jax.ShapeDtypeStruct((N_CHIPS, rows, cols), dtype),
        compiler_params=_params(1),
    )(place, w, *([] if after is None else [after]))


def _adamw(w, g, m, v, name, after=None):
    rows, cols = w.shape
    tr = _row_tile(rows, 256)

    def body(w_ref, g_ref, m_ref, v_ref, *rest):
        go_ref, d_ref, nm_ref, nv_ref = rest[-4:]
        gv = g_ref[...]
        go_ref[...] = gv
        nm = ADAM_B1 * m_ref[...] + (1.0 - ADAM_B1) * gv
        nv = ADAM_B2 * v_ref[...] + (1.0 - ADAM_B2) * (gv * gv)
        m_hat = nm / ADAM_C1
        v_hat = nv / ADAM_C2
        d_ref[...] = -ADAM_LR * (m_hat / (jnp.sqrt(v_hat) + ADAM_EPS) + ADAM_WD * w_ref[...])
        nm_ref[...] = nm
        nv_ref[...] = nv

    tile = pl.BlockSpec((tr, cols), lambda i: (i, 0))
    shape = jax.ShapeDtypeStruct((rows, cols), F32)
    return pl.pallas_call(
        body, name=name, grid=(rows // tr,),
        in_specs=[tile] * 4 + ([] if after is None else [ANY]), out_specs=[tile] * 4, out_shape=[shape] * 4,
        compiler_params=_params(1),
    )(w, g, m, v, *([] if after is None else [after]))


def _mesh_pos():
    return lax.axis_index("x"), lax.axis_index("y"), lax.axis_index("c")


def _other_chips(x, y):
    return [(1 - x, y), (x, 1 - y), (1 - x, 1 - y)]


def _half_rows(ref, which):
    rh = ref.shape[-2] // 2
    return ref.at[pl.ds(which * rh, rh), :]


def _remote(src, dst, send_sems, recv_sems, sem, to):
    return pltpu.make_async_remote_copy(src_ref=src, dst_ref=dst, send_sem=send_sems.at[sem], recv_sem=recv_sems.at[sem],
                                        device_id=to, device_id_type=MESH)


HBM = pl.BlockSpec(memory_space=pltpu.HBM)
SEM = pl.BlockSpec(memory_space=pltpu.SEMAPHORE)
DATAFLOW_EFFECT = pltpu.SideEffectType.DATAFLOW_SIDE_EFFECTING


def _in_hbm(arrays):
    return [pltpu.with_memory_space_constraint(a, pltpu.HBM) for a in arrays]


def _hbm_like(arrays):
    return [pltpu.HBM(a.shape, a.dtype) for a in arrays]


def _gather_start(bufs, groups, name):
    n, ng = len(bufs), len(groups)

    def body(*refs):
        ins = refs[:n]
        send, recv, token = refs[n:n + ng], refs[n + ng:n + 2 * ng], refs[-1]
        x, y, c = _mesh_pos()
        me = 2 * x + y
        for g, members in enumerate(groups):
            for i, w in enumerate(members):
                mine = _half_rows(ins[w].at[me], c)
                for k, (px, py) in enumerate(_other_chips(x, y)):
                    _remote(mine, mine, send[g], recv[g], 3 * i + k, (px, py, c)).start()
        token[...] = jnp.zeros_like(token)

    sems = [pltpu.SemaphoreType.DMA((3 * len(m),)) for m in groups]
    outs = pl.pallas_call(
        body, name=name,
        in_specs=[HBM] * n, out_specs=[SEM] * (2 * ng) + [HBM] * n + [pl.BlockSpec(memory_space=pltpu.VMEM)],
        out_shape=sems + sems + _hbm_like(bufs) + [jax.ShapeDtypeStruct((8, 128), F32)],
        input_output_aliases={i: 2 * ng + i for i in range(n)},
        compiler_params=pltpu.CompilerParams(has_side_effects=DATAFLOW_EFFECT),
    )(*_in_hbm(bufs))
    return outs[:ng], outs[ng:2 * ng], outs[2 * ng:2 * ng + n], outs[-1]


def _gather_pass(bufs, send, recv, after, name):
    m = len(bufs)

    def body(*refs):
        ins, send_in, recv_in = refs[:m], refs[m], refs[m + 1]
        send_out, recv_out, token = refs[m + 3], refs[m + 4], refs[-1]
        x, y, c = _mesh_pos()
        for i in range(m):
            for k, (px, py) in enumerate(_other_chips(x, y)):
                landed = _half_rows(ins[i].at[2 * px + py], c)
                came = _remote(landed, landed, send_in, recv_in, 3 * i + k, (px, py, c))
                came.wait_recv()
                came.wait_send()
                _remote(landed, landed, send_out, recv_out, 3 * i + k, (x, y, 1 - c)).start()
        token[...] = jnp.zeros_like(token)

    sems = [pltpu.SemaphoreType.DMA((3 * m,))] * 2
    outs = pl.pallas_call(
        body, name=name,
        in_specs=[HBM] * m + [SEM, SEM, ANY], out_specs=[SEM, SEM] + [HBM] * m + [pl.BlockSpec(memory_space=pltpu.VMEM)],
        out_shape=sems + _hbm_like(bufs) + [jax.ShapeDtypeStruct((8, 128), F32)],
        input_output_aliases={i: 2 + i for i in range(m)},
        compiler_params=pltpu.CompilerParams(has_side_effects=DATAFLOW_EFFECT),
    )(*_in_hbm(bufs), send, recv, after)
    return outs[0], outs[1], outs[2:2 + m], outs[-1]


def _gather_done(bufs, send, recv, after, name):
    m = len(bufs)

    def body(*refs):
        ins, send_in, recv_in = refs[:m], refs[m], refs[m + 1]
        x, y, c = _mesh_pos()
        for i in range(m):
            for k, (px, py) in enumerate(_other_chips(x, y)):
                passed = _half_rows(ins[i].at[2 * px + py], 1 - c)
                came = _remote(passed, passed, send_in, recv_in, 3 * i + k, (x, y, 1 - c))
                came.wait_send()
                came.wait_recv()

    return pl.pallas_call(
        body, name=name,
        in_specs=[HBM] * m + [SEM, SEM, ANY], out_specs=[HBM] * m,
        out_shape=_hbm_like(bufs),
        input_output_aliases={i: i for i in range(m)},
        compiler_params=pltpu.CompilerParams(has_side_effects=DATAFLOW_EFFECT),
    )(*_in_hbm(bufs), send, recv, after)


class _Gather:
    def __init__(self, groups):
        self.groups = groups
        self.landing = {}
        self.passing = {}

    def start(self, slotted, group_names, name):
        names = [n for g in group_names for n in self.groups[g]]
        index = {n: i for i, n in enumerate(names)}
        members = [[index[n] for n in self.groups[g]] for g in group_names]
        send, recv, bufs, token = _gather_start([slotted[n] for n in names], members, name)
        for j, g in enumerate(group_names):
            self.landing[g] = (send[j], recv[j], [bufs[index[n]] for n in self.groups[g]])
        return token

    def begin(self, group, after):
        send, recv, bufs = self.landing.pop(group)
        send, recv, bufs, token = _gather_pass(bufs, send, recv, after, "gather_pass_" + group)
        self.passing[group] = (send, recv, bufs)
        return token

    def get(self, group, after):
        if group not in self.passing:
            self.begin(group, after)
        send, recv, bufs = self.passing.pop(group)
        return dict(zip(self.groups[group], _gather_done(bufs, send, recv, after, "gather_done_" + group)))


def _sibling_halves_copies(srcs, dsts, x, y, c):
    out = []
    for s_ref, d_ref in zip(srcs, dsts, strict=True):
        rh = s_ref.shape[1] // 2
        out.append((s_ref.at[:, pl.ds((1 - c) * rh, rh), :], d_ref, (x, y, 1 - c)))
    return out


def _to_sibling_copies(srcs, dsts, x, y, c):
    return [(s_ref, d_ref, (x, y, 1 - c)) for s_ref, d_ref in zip(srcs, dsts, strict=True)]


def _chip_copies(srcs, dsts, x, y, c):
    out = []
    for s_ref, d_ref in zip(srcs, dsts, strict=True):
        for k, (px, py) in enumerate(_other_chips(x, y)):
            out.append((s_ref.at[2 * px + py], d_ref.at[k], (px, py, c)))
    return out


def _join_copies(srcs, dsts, x, y, c):
    out = []
    for s_ref in srcs:
        mine = _half_rows(s_ref, c)
        out.append((mine, mine, (x, y, 1 - c)))
    return out


def _exchange_start(copies_fn, n_copies, srcs, fresh, after, name):
    ns, nb = len(srcs), len(srcs) + len(fresh)

    def body(*refs):
        bufs, send, recv, token = refs[:nb], refs[nb + 1], refs[nb + 2], refs[-1]
        x, y, c = _mesh_pos()
        for i, (s_ref, d_ref, to) in enumerate(copies_fn(bufs[:ns], bufs[ns:] if fresh else bufs[:ns], x, y, c)):
            _remote(s_ref, d_ref, send, recv, i, to).start()
        token[...] = jnp.zeros_like(token)

    sems = [pltpu.SemaphoreType.DMA((n_copies,))] * 2
    outs = pl.pallas_call(
        body, name=name,
        in_specs=[HBM] * nb + [ANY], out_specs=[SEM, SEM] + [HBM] * nb + [pl.BlockSpec(memory_space=pltpu.VMEM)],
        out_shape=sems + _hbm_like(list(srcs) + list(fresh)) + [jax.ShapeDtypeStruct((8, 128), F32)],
        input_output_aliases={i: 2 + i for i in range(nb)},
        compiler_params=pltpu.CompilerParams(has_side_effects=DATAFLOW_EFFECT),
    )(*_in_hbm(list(srcs) + list(fresh)), after)
    return outs[0], outs[1], outs[2:2 + ns], outs[2 + ns:2 + nb], outs[-1]


def _exchange_done(copies_fn, srcs, fresh, send, recv, after, name):
    ns, nb = len(srcs), len(srcs) + len(fresh)

    def body(*refs):
        bufs, send_in, recv_in = refs[:nb], refs[nb], refs[nb + 1]
        x, y, c = _mesh_pos()
        for i, (s_ref, d_ref, to) in enumerate(copies_fn(bufs[:ns], bufs[ns:] if fresh else bufs[:ns], x, y, c)):
            came = _remote(s_ref, d_ref, send_in, recv_in, i, to)
            came.wait_send()
            came.wait_recv()

    outs = pl.pallas_call(
        body, name=name,
        in_specs=[HBM] * nb + [SEM, SEM, ANY], out_specs=[HBM] * nb,
        out_shape=_hbm_like(list(srcs) + list(fresh)),
        input_output_aliases={i: i for i in range(nb)},
        compiler_params=pltpu.CompilerParams(has_side_effects=DATAFLOW_EFFECT),
    )(*_in_hbm(list(srcs) + list(fresh)), send, recv, after)
    return outs[:ns], outs[ns:]


class _Reduce:
    def __init__(self, place, core, shards, mom_m, mom_v):
        self.place, self.core = place, core
        self.shards, self.mom_m, self.mom_v = shards, mom_m, mom_v
        self.state = {}
        self.results = {}

    def add(self, group, grads, after):
        names = list(grads)
        g4s = [g.reshape((N_CHIPS, -1, g.shape[-1])) if g.ndim == 2 else g for g in grads.values()]
        fresh = [lax.empty((N_CHIPS, g.shape[1] // 2, g.shape[2]), BF16) for g in g4s]
        send, recv, g4s, fresh, token = _exchange_start(_sibling_halves_copies, len(names), g4s, fresh, after,
                                                        "pair_start_" + group)
        self.state[group] = (0, names, send, recv, g4s, fresh)
        return token

    def send(self, group, theirs, after):
        names, srcs = list(theirs), list(theirs.values())
        fresh = [lax.empty(s.shape, BF16) for s in srcs]
        send, recv, srcs, fresh, token = _exchange_start(_to_sibling_copies, len(names), srcs, fresh, after, "pair_start_" + group)
        self.state[group] = ("sent", names, send, recv, srcs, fresh)
        return token

    def received(self, group, after):
        stage, names, send, recv, srcs, fresh = self.state.pop(group)
        assert stage == "sent"
        _, got = _exchange_done(_to_sibling_copies, srcs, fresh, send, recv, after, "pair_done_" + group)
        return dict(zip(names, got))

    def add_parts(self, group, parts):
        names, srcs = list(parts), list(parts.values())
        fresh = [lax.empty((N_CHIPS - 1,) + p.shape[1:], BF16) for p in srcs]
        send, recv, srcs, fresh, token = _exchange_start(_chip_copies, 3 * len(names), srcs, fresh, self.core, "chips_start_" + group)
        self.state[group] = (1, names, send, recv, srcs, fresh)
        return token

    def step(self, group, after):
        stage, names, send, recv, srcs, fresh = self.state[group]
        if stage == 0:
            g4s, ras = _exchange_done(_sibling_halves_copies, srcs, fresh, send, recv, after, "pair_done_" + group)
            parts = [_pair_sum(g, r, self.core, "pair_sum_" + n) for g, r, n in zip(g4s, ras, names)]
            fresh = [lax.empty((N_CHIPS - 1,) + p.shape[1:], BF16) for p in parts]
            send, recv, parts, fresh, token = _exchange_start(_chip_copies, 3 * len(names), parts, fresh, self.core,
                                                              "chips_start_" + group)
            self.state[group] = (1, names, send, recv, parts, fresh)
            return token
        if stage == 1:
            parts, rcs = _exchange_done(_chip_copies, srcs, fresh, send, recv, after, "chips_done_" + group)
            wholes = [_quad_sum(p, r, self.place, "quad_sum_" + n) for p, r, n in zip(parts, rcs, names)]
            send, recv, wholes, _, token = _exchange_start(_join_copies, len(names), wholes, [], self.core, "join_start_" + group)
            self.state[group] = (2, names, send, recv, wholes, [])
            return token
        assert stage == 2
        wholes, _ = _exchange_done(_join_copies, srcs, [], send, recv, after, "join_done_" + group)
        token = None
        for n, g in zip(names, wholes):
            self.results[n] = _adamw(self.shards[n], g, self.mom_m[n], self.mom_v[n], "adamw_" + n, after=token)
            token = self.results[n][1]
        del self.state[group]
        return token


N_DEV = 8


def _all_reduce_small(v):
    def body(v_ref, o_ref, slots, send_sems, recv_sems):
        x, y, c = _mesh_pos()
        me = 4 * x + 2 * y + c
        slots[me] = v_ref[...]
        peers = []
        for r in range(1, N_DEV):
            fx, fy, fc = (r >> 2) & 1, (r >> 1) & 1, r & 1
            peers.append((x + fx - 2 * x * fx, y + fy - 2 * y * fy, c + fc - 2 * c * fc))
        sends = []
        for r, peer in enumerate(peers):
            cp = _remote(v_ref, slots.at[me], send_sems, recv_sems, r, peer)
            cp.start()
            sends.append(cp)
        for r, (px, py, pc) in enumerate(peers):
            landed = slots.at[4 * px + 2 * py + pc]
            _remote(landed, landed, send_sems, recv_sems, r, (px, py, pc)).wait_recv()
        for cp in sends:
            cp.wait_send()
        acc = slots[0]
        for i in range(1, N_DEV):
            acc = acc + slots[i]
        o_ref[...] = acc

    vm = pl.BlockSpec(memory_space=pltpu.VMEM)
    return pl.pallas_call(
        body, name="small_grads_all_reduce",
        in_specs=[vm], out_specs=vm,
        out_shape=jax.ShapeDtypeStruct(v.shape, v.dtype),
        scratch_shapes=[pltpu.VMEM((N_DEV,) + v.shape, v.dtype), pltpu.SemaphoreType.DMA((N_DEV - 1,)),
                        pltpu.SemaphoreType.DMA((N_DEV - 1,))],
    )(v)


MATRICES = ("w_in", "w_attn_out", "w_conv_out", "w_o", "w_cq", "w_ckv", "w_co", "w_gate", "w_up", "w_down")
VECTORS = ("g_mix", "b_gate", "g_cross", "g_mem", "g_ffn", "g_final", "conv_w", "sink")
WEIGHT_ORDER = ("g_mix", "w_in", "sink", "conv_w", "b_gate", "w_attn_out", "w_conv_out", "w_o", "g_cross", "g_mem", "w_cq",
                "w_ckv", "w_co", "g_ffn", "w_gate", "w_up", "w_down", "g_final")
CONV_PAD_ROWS = 16
SMALL_ROWS = 8


def _pack(pieces):
    flat = jnp.concatenate([p.reshape(-1) for p in pieces])
    lane_group = SMALL_ROWS * 128
    total = -(-flat.shape[0] // lane_group) * lane_group
    flat = jnp.pad(flat, (0, total - flat.shape[0]))
    return flat.reshape(SMALL_ROWS, total // SMALL_ROWS), [p.size for p in pieces]


def _unpack(packed, pieces):
    flat = packed.reshape(-1)
    out, off = [], 0
    for p in pieces:
        out.append(flat[off:off + p.size].reshape(p.shape))
        off += p.size
    return out


def kernel(x, mem, g_mix, w_in, sink, conv_w, b_gate, w_attn_out, w_conv_out, w_o, g_cross, g_mem, w_cq, w_ckv, w_co, g_ffn, w_gate, w_up, w_down, g_final, loss_target, m_g_mix, m_w_in, m_sink, m_conv_w, m_b_gate, m_w_attn_out, m_w_conv_out, m_w_o, m_g_cross, m_g_mem, m_w_cq, m_w_ckv, m_w_co, m_g_ffn, m_w_gate, m_w_up, m_w_down, m_g_final, v_g_mix, v_w_in, v_sink, v_conv_w, v_b_gate, v_w_attn_out, v_w_conv_out, v_w_o, v_g_cross, v_g_mem, v_w_cq, v_w_ckv, v_w_co, v_g_ffn, v_w_gate, v_w_up, v_w_down, v_g_final):
    given = dict(g_mix=g_mix, w_in=w_in, sink=sink, conv_w=conv_w, b_gate=b_gate, w_attn_out=w_attn_out, w_conv_out=w_conv_out,
                 w_o=w_o, g_cross=g_cross, g_mem=g_mem, w_cq=w_cq, w_ckv=w_ckv, w_co=w_co, g_ffn=g_ffn, w_gate=w_gate, w_up=w_up,
                 w_down=w_down, g_final=g_final)
    mom_m = dict(g_mix=m_g_mix, w_in=m_w_in, sink=m_sink, conv_w=m_conv_w, b_gate=m_b_gate, w_attn_out=m_w_attn_out,
                 w_conv_out=m_w_conv_out, w_o=m_w_o, g_cross=m_g_cross, g_mem=m_g_mem, w_cq=m_w_cq, w_ckv=m_w_ckv, w_co=m_w_co,
                 g_ffn=m_g_ffn, w_gate=m_w_gate, w_up=m_w_up, w_down=m_w_down, g_final=m_g_final)
    mom_v = dict(g_mix=v_g_mix, w_in=v_w_in, sink=v_sink, conv_w=v_conv_w, b_gate=v_b_gate, w_attn_out=v_w_attn_out,
                 w_conv_out=v_w_conv_out, w_o=v_w_o, g_cross=v_g_cross, g_mem=v_g_mem, w_cq=v_w_cq, w_ckv=v_w_ckv, w_co=v_w_co,
                 g_ffn=v_g_ffn, w_gate=v_w_gate, w_up=v_w_up, w_down=v_w_down, g_final=v_g_final)
    xs, mems, target = x[0], mem[0], loss_target[0]
    d_model = xs.shape[1]
    chip = 2 * lax.axis_index("x") + lax.axis_index("y")
    core = jnp.reshape(lax.axis_index("c"), (1,)).astype(jnp.int32)
    place = jnp.stack([chip, lax.axis_index("c")]).astype(jnp.int32)

    shards = {n: given[n][0] for n in MATRICES}
    conv_cols = conv_w.shape[2]
    conv_pad = jnp.pad(conv_w[0], ((0, CONV_PAD_ROWS - conv_w.shape[1]), (0, 0)))
    fetch = _Gather(GATHER_GROUPS)
    first = {"w_in": _cast_to_slot(shards["w_in"], place, BF16, "to_slot_w_in"),
             "conv_w": _cast_to_slot(conv_pad, place, F32, "to_slot_conv_w")}
    tok = fetch.start(first, ["in"], "gather_start_in")
    rest = {n: _cast_to_slot(shards[n], place, BF16, "to_slot_" + n, after=tok) for n in MATRICES if n != "w_in"}
    fetch.start(rest, [g for g in GATHER_GROUPS if g != "in"], "gather_start_rest")
    small = {n: given[n] for n in ("g_mix", "b_gate", "g_cross", "g_mem", "g_ffn")}
    small["g_final"] = g_final[None]
    small["sink"] = sink[0]

    reduce = _Reduce(place, core, shards, {n: mom_m[n][0] for n in MATRICES}, {n: mom_v[n][0] for n in MATRICES})
    sq, grad_x, small_grads = _local_step(xs, mems, target, small, fetch, reduce)

    loss_part = 0.5 * sq[0:1, 0:1] / d_model
    pieces = [small_grads[n] for n in VECTORS] + [loss_part]
    packed, _ = _pack(pieces)
    summed = _unpack(_all_reduce_small(packed), pieces)
    loss = summed[-1][0, 0]
    small_sum = dict(zip(VECTORS, summed[:-1]))
    small_sum["conv_w"] = lax.dynamic_slice_in_dim(small_sum["conv_w"], chip * conv_cols, conv_cols, axis=1)

    grad_out, delta, new_m, new_v = {}, {}, {}, {}
    like = [given[n] for n in VECTORS]
    pw, _ = _pack(like)
    pg, _ = _pack([small_sum[n] for n in VECTORS])
    pm, _ = _pack([mom_m[n] for n in VECTORS])
    pv, _ = _pack([mom_v[n] for n in VECTORS])
    _, pd, pnm, pnv = _adamw(pw, pg, pm, pv, "adamw_small")
    for n, g, d, nm, nv in zip(VECTORS, [small_sum[n] for n in VECTORS], _unpack(pd, like), _unpack(pnm, like), _unpack(pnv, like)):
        grad_out[n] = g.reshape(given[n].shape)
        delta[n], new_m[n], new_v[n] = d, nm, nv
    tok = reduce.step("in", pd)
    reduce.step("in", tok)
    for n in MATRICES:
        g, d, nm, nv = reduce.results[n]
        grad_out[n], delta[n], new_m[n], new_v[n] = g[None], d[None], nm[None], nv[None]

    return (loss, grad_x[None], *[grad_out[n] for n in WEIGHT_ORDER], *[delta[n] for n in WEIGHT_ORDER],
            *[new_m[n] for n in WEIGHT_ORDER], *[new_v[n] for n in WEIGHT_ORDER])
```

```python
import functools

import jax
import jax.numpy as jnp
from jax import lax
from jax.experimental import pallas as pl
from jax.experimental.pallas import tpu as pltpu

F32 = jnp.float32
BF16 = jnp.bfloat16
MESH = pl.DeviceIdType.MESH
ANY = pl.BlockSpec(memory_space=pl.ANY)

VMEM_LIMIT_BYTES = 56 * 1024 * 1024

N_CHIPS = 4
HEAD_DIM = 128
N_Q_HEADS = 8
N_KV_HEADS = 2
Q_GROUP = N_Q_HEADS // N_KV_HEADS
ATTN_WIDTH = N_Q_HEADS * HEAD_DIM
KV_WIDTH = N_KV_HEADS * HEAD_DIM
WINDOW = 128
BLOCK = 128
BAND = 3 * BLOCK
ROPE_THETA = 10000.0
CONV_WIDTH = 1024
MEM_HEADS = 4
MEM_WIDTH = MEM_HEADS * HEAD_DIM
RMS_EPS = 1e-6
NEG_INF = -1e30
ATTN_SCALE = HEAD_DIM ** -0.5

Q_OFF, K_OFF, V_OFF, CU_OFF, CB_OFF, CC_OFF, GL_OFF = 0, 1024, 1280, 1536, 2560, 3584, 4608

ADAM_LR = 0.001
ADAM_B1 = 0.9
ADAM_B2 = 0.999
ADAM_EPS = 1e-08
ADAM_WD = 0.01
ADAM_STEP = 10
ADAM_C1 = 1.0 - ADAM_B1 ** ADAM_STEP
ADAM_C2 = 1.0 - ADAM_B2 ** ADAM_STEP


def _params(n_grid_axes):
    return pltpu.CompilerParams(dimension_semantics=("arbitrary",) * n_grid_axes, vmem_limit_bytes=VMEM_LIMIT_BYTES)


BF16_SUBLANES = 16


def _row_tile(rows, want):
    if rows <= want:
        return rows
    for t in range(want, 0, -BF16_SUBLANES):
        if rows % t == 0:
            return t
    return rows


def _matmul(a, b, *, mode, tm, tn, tk, out_dtypes, name, extras=(), epilogue=None, b_blocks=1, out_blocks=1, after=None):
    if mode == "tn":
        kdim, m = a.shape
    else:
        m, kdim = a.shape
    if b_blocks > 1:
        nb, brows, bcols = b.shape
        assert nb == b_blocks
        if mode == "nn":
            n = bcols * nb
            assert brows == kdim
        else:
            assert mode == "nt" and bcols * nb == kdim
            n = brows
    else:
        n = b.shape[0] if mode == "nt" else b.shape[1]
    tm, tn = min(tm, m), min(tn, n)
    assert m % tm == 0 and n % tn == 0 and tk == kdim, (name, m, n, kdim, tm, tn, tk)
    n_extra, n_out = len(extras), len(out_dtypes)
    n_after = 0 if after is None else 1

    if mode == "tn":
        a_spec = pl.BlockSpec((tk, tm), lambda j, i, k: (k, i))
        dims = (((0,), (0,)), ((), ()))
    else:
        a_spec = pl.BlockSpec((tm, tk), lambda j, i, k: (i, k))
        dims = (((1,), (0,)), ((), ())) if mode == "nn" else (((1,), (1,)), ((), ()))

    if b_blocks > 1 and mode == "nn":
        per = b.shape[2] // tn
        assert b.shape[2] % tn == 0
        b_spec = pl.BlockSpec((None, tk, tn), lambda j, i, k: (j // per, k, j % per))
    elif b_blocks > 1:
        b_spec = pl.BlockSpec((b_blocks, tn, b.shape[2]), lambda j, i, k: (0, j, 0))
    elif mode == "nt":
        b_spec = pl.BlockSpec((tn, tk), lambda j, i, k: (j, k))
    else:
        b_spec = pl.BlockSpec((tk, tn), lambda j, i, k: (k, j))

    tile_spec = pl.BlockSpec((tm, tn), lambda j, i, k: (i, j))
    if out_blocks > 1:
        ncols = n // out_blocks
        assert ncols % tn == 0
        oper = ncols // tn
        out_spec = pl.BlockSpec((None, tm, tn), lambda j, i, k: (j // oper, i, j % oper))
        out_shape = [jax.ShapeDtypeStruct((out_blocks, m, ncols), dt) for dt in out_dtypes]
    else:
        out_spec = tile_spec
        out_shape = [jax.ShapeDtypeStruct((m, n), dt) for dt in out_dtypes]

    def body(a_ref, b_ref, *rest):
        extra_refs = rest[:n_extra]
        out_refs = rest[n_extra + n_after:n_extra + n_after + n_out]
        if mode == "nt" and b_blocks > 1:
            cs = b.shape[2]
            acc = None
            for jb in range(b_blocks):
                prod = lax.dot_general(a_ref[:, jb * cs:(jb + 1) * cs].astype(BF16), b_ref[jb].astype(BF16), dims,
                                       preferred_element_type=F32)
                acc = prod if acc is None else acc + prod
        else:
            acc = lax.dot_general(a_ref[...].astype(BF16), b_ref[...].astype(BF16), dims, preferred_element_type=F32)
        tiles = (acc,) if epilogue is None else epilogue(acc, *[r[...] for r in extra_refs])
        for o_ref, t in zip(out_refs, tiles, strict=True):
            o_ref[...] = t.astype(o_ref.dtype)

    outs = pl.pallas_call(
        body,
        name=name,
        grid=(n // tn, m // tm, 1),
        in_specs=[a_spec, b_spec] + [tile_spec] * n_extra + [ANY] * n_after,
        out_specs=[out_spec] * n_out,
        out_shape=out_shape,
        compiler_params=_params(3),
    )(a, b, *extras, *([] if after is None else [after]))
    return outs[0] if n_out == 1 else outs


def _add_residual(acc, res):
    return (acc + res,)


def _wgrad_half(a, b, core, *, theirs, row_sharded, tm, tn, name, add=None, after=None):
    kdim, m = a.shape
    n = b.shape[1]
    rs, cs = (m // N_CHIPS, n) if row_sharded else (m, n // N_CHIPS)
    rh = rs // 2
    tm, tn = min(tm, rh), min(tn, cs)
    assert rh % tm == 0 and cs % tn == 0, (name, rh, cs, tm, tn)
    mh, per = rh // tm, cs // tn
    has_add = add is not None

    def half(c):
        return 1 - c[0] if theirs else c[0]

    if row_sharded:
        grid = (n // tn, N_CHIPS * mh)
        a_spec = pl.BlockSpec((kdim, tm), lambda j, r, c: (0, ((r // mh) * 2 + half(c)) * mh + r % mh))
        o_spec = pl.BlockSpec((None, tm, tn), lambda j, r, c: (r // mh, r % mh, j))
    else:
        grid = (n // tn, mh)
        a_spec = pl.BlockSpec((kdim, tm), lambda j, r, c: (0, half(c) * mh + r))
        o_spec = pl.BlockSpec((None, tm, tn), lambda j, r, c: (j // per, r, j % per))
    b_spec = pl.BlockSpec((kdim, tn), lambda j, r, c: (0, j))

    def body(c_ref, a_ref, b_ref, *rest):
        o_ref = rest[-1]
        acc = lax.dot_general(a_ref[...].astype(BF16), b_ref[...].astype(BF16), (((0,), (0,)), ((), ())),
                              preferred_element_type=F32)
        if has_add:
            acc = acc + rest[0][...].astype(F32)
        o_ref[...] = acc.astype(BF16)

    operands = [a, b] + ([add] if has_add else []) + ([] if after is None else [after])
    return pl.pallas_call(
        body, name=name,
        grid_spec=pltpu.PrefetchScalarGridSpec(
            num_scalar_prefetch=1, grid=grid,
            in_specs=[a_spec, b_spec] + ([o_spec] if has_add else []) + ([] if after is None else [ANY]),
            out_specs=o_spec),
        out_shape=jax.ShapeDtypeStruct((N_CHIPS, rh, cs), BF16),
        compiler_params=_params(2),
    )(core, *operands)


def _rstd(x):
    return lax.rsqrt(jnp.mean(x * x, axis=-1, keepdims=True) + RMS_EPS)


def _rmsnorm(x, g, name):
    s, d = x.shape
    tr = _row_tile(s, 256)

    def body(x_ref, g_ref, o_ref):
        xv = x_ref[...]
        o_ref[...] = (xv * _rstd(xv) * g_ref[...]).astype(BF16)

    return pl.pallas_call(
        body, name=name, grid=(s // tr,),
        in_specs=[pl.BlockSpec((tr, d), lambda i: (i, 0)), pl.BlockSpec((1, d), lambda i: (0, 0))],
        out_specs=pl.BlockSpec((tr, d), lambda i: (i, 0)),
        out_shape=jax.ShapeDtypeStruct((s, d), BF16),
        compiler_params=_params(1),
    )(x, g)


def _rmsnorm_bwd(dh, x, g, dres, name):
    s, d = x.shape
    tr = _row_tile(s, 256)
    has_res = dres is not None

    def body(*refs):
        if has_res:
            dh_ref, x_ref, g_ref, res_ref, dx_ref, dxb_ref, dg_ref = refs
        else:
            dh_ref, x_ref, g_ref, dx_ref, dxb_ref, dg_ref = refs
        xv = x_ref[...]
        dhv = dh_ref[...].astype(F32)
        r = _rstd(xv)
        xn = xv * r
        dhg = dhv * g_ref[...]
        dx = r * (dhg - xn * jnp.mean(dhg * xn, axis=-1, keepdims=True))
        if has_res:
            dx = dx + res_ref[...]
        dx_ref[...] = dx
        dxb_ref[...] = dx.astype(BF16)
        part = jnp.sum(dhv * xn, axis=0, keepdims=True)

        @pl.when(pl.program_id(0) == 0)
        def _():
            dg_ref[...] = part

        @pl.when(pl.program_id(0) > 0)
        def _():
            dg_ref[...] += part

    row = pl.BlockSpec((tr, d), lambda i: (i, 0))
    vec = pl.BlockSpec((1, d), lambda i: (0, 0))
    return pl.pallas_call(
        body, name=name, grid=(s // tr,),
        in_specs=[row, row, vec] + ([row] if has_res else []),
        out_specs=[row, row, vec],
        out_shape=[jax.ShapeDtypeStruct((s, d), F32), jax.ShapeDtypeStruct((s, d), BF16), jax.ShapeDtypeStruct((1, d), F32)],
        compiler_params=_params(1),
    )(*([dh, x, g] + ([dres] if has_res else [])))


def _loss_head(x3, g, target):
    s, d = x3.shape
    tr = _row_tile(s, 256)

    def body(x_ref, g_ref, t_ref, dx_ref, dxb_ref, sq_ref, dg_ref):
        xv = x_ref[...]
        gv = g_ref[...]
        r = _rstd(xv)
        xn = xv * r
        err = xn * gv - t_ref[...]
        dy = err * (1.0 / d)
        dyg = dy * gv
        dx = r * (dyg - xn * jnp.mean(dyg * xn, axis=-1, keepdims=True))
        dx_ref[...] = dx
        dxb_ref[...] = dx.astype(BF16)
        sq = jnp.sum(jnp.sum(err * err, axis=1, keepdims=True), axis=0, keepdims=True)
        sq = jnp.broadcast_to(sq, (1, 128))
        part = jnp.sum(dy * xn, axis=0, keepdims=True)

        @pl.when(pl.program_id(0) == 0)
        def _():
            sq_ref[...] = sq
            dg_ref[...] = part

        @pl.when(pl.program_id(0) > 0)
        def _():
            sq_ref[...] += sq
            dg_ref[...] += part

    row = pl.BlockSpec((tr, d), lambda i: (i, 0))
    vec = pl.BlockSpec((1, d), lambda i: (0, 0))
    return pl.pallas_call(
        body, name="loss_head", grid=(s // tr,),
        in_specs=[row, vec, row],
        out_specs=[row, row, pl.BlockSpec((1, 128), lambda i: (0, 0)), vec],
        out_shape=[jax.ShapeDtypeStruct((s, d), F32), jax.ShapeDtypeStruct((s, d), BF16),
                   jax.ShapeDtypeStruct((1, 128), F32), jax.ShapeDtypeStruct((1, d), F32)],
        compiler_params=_params(1),
    )(x3, g, target)


def _rope_tables(s):
    inv = 1.0 / (ROPE_THETA ** (jnp.arange(0, HEAD_DIM, 2, dtype=F32) / HEAD_DIM))
    ang = jnp.arange(s, dtype=F32)[:, None] * inv[None, :]
    cos, sin = jnp.cos(ang), jnp.sin(ang)
    return jnp.concatenate([cos, cos], axis=1), jnp.concatenate([-sin, sin], axis=1)


def _swap_halves(t):
    return pltpu.roll(t, HEAD_DIM // 2, 1)


def _rope_fwd(z, cos_t, sin_t):
    s = z.shape[0]
    tr = _row_tile(s, 256)

    def body(zq_ref, zk_ref, zv_ref, c_ref, s_ref, q_ref, k_ref, v_ref):
        c, sn = c_ref[...], s_ref[...]
        for hd in range(N_Q_HEADS):
            cols = slice(hd * HEAD_DIM, (hd + 1) * HEAD_DIM)
            t = zq_ref[:, cols]
            q_ref[:, cols] = (t * c + _swap_halves(t) * sn).astype(BF16)
        for hd in range(N_KV_HEADS):
            cols = slice(hd * HEAD_DIM, (hd + 1) * HEAD_DIM)
            t = zk_ref[:, cols]
            k_ref[:, cols] = (t * c + _swap_halves(t) * sn).astype(BF16)
        v_ref[...] = zv_ref[...].astype(BF16)

    tab = pl.BlockSpec((tr, HEAD_DIM), lambda i: (i, 0))
    return pl.pallas_call(
        body, name="rope_fwd", grid=(s // tr,),
        in_specs=[pl.BlockSpec((tr, ATTN_WIDTH), lambda i: (i, Q_OFF // ATTN_WIDTH)),
                  pl.BlockSpec((tr, KV_WIDTH), lambda i: (i, K_OFF // KV_WIDTH)),
                  pl.BlockSpec((tr, KV_WIDTH), lambda i: (i, V_OFF // KV_WIDTH)), tab, tab],
        out_specs=[pl.BlockSpec((tr, ATTN_WIDTH), lambda i: (i, 0)), pl.BlockSpec((tr, KV_WIDTH), lambda i: (i, 0)),
                   pl.BlockSpec((tr, KV_WIDTH), lambda i: (i, 0))],
        out_shape=[jax.ShapeDtypeStruct((s, ATTN_WIDTH), BF16), jax.ShapeDtypeStruct((s, KV_WIDTH), BF16),
                   jax.ShapeDtypeStruct((s, KV_WIDTH), BF16)],
        compiler_params=_params(1),
    )(z, z, z, cos_t, sin_t)


def _rope_bwd(dq_rot, dk_rot, dv, cos_t, sin_t):
    s = dq_rot.shape[0]
    tr = _row_tile(s, 256)

    def body(dq_ref, dk_ref, dv_ref, c_ref, s_ref, oq_ref, ok_ref, ov_ref):
        c, sn = c_ref[...], s_ref[...]
        for hd in range(N_Q_HEADS):
            cols = slice(hd * HEAD_DIM, (hd + 1) * HEAD_DIM)
            t = dq_ref[:, cols]
            oq_ref[:, cols] = (t * c + _swap_halves(t * sn)).astype(BF16)
        for hd in range(N_KV_HEADS):
            cols = slice(hd * HEAD_DIM, (hd + 1) * HEAD_DIM)
            t = dk_ref[:, cols]
            ok_ref[:, cols] = (t * c + _swap_halves(t * sn)).astype(BF16)
        ov_ref[...] = dv_ref[...].astype(BF16)

    tab = pl.BlockSpec((tr, HEAD_DIM), lambda i: (i, 0))
    wide = pl.BlockSpec((tr, ATTN_WIDTH), lambda i: (i, 0))
    narrow = pl.BlockSpec((tr, KV_WIDTH), lambda i: (i, 0))
    return pl.pallas_call(
        body, name="rope_bwd", grid=(s // tr,),
        in_specs=[wide, narrow, narrow, tab, tab],
        out_specs=[wide, narrow, narrow],
        out_shape=[jax.ShapeDtypeStruct((s, ATTN_WIDTH), BF16), jax.ShapeDtypeStruct((s, KV_WIDTH), BF16),
                   jax.ShapeDtypeStruct((s, KV_WIDTH), BF16)],
        compiler_params=_params(1),
    )(dq_rot, dk_rot, dv, cos_t, sin_t)


def _swa_band(i, s):
    return pl.multiple_of(jnp.clip((i - 1) * BLOCK, 0, s - BAND), BLOCK)


def _swa_probs(q_ref, k_ref, sink_ref, kv, start, valid):
    cols = slice(kv * HEAD_DIM, (kv + 1) * HEAD_DIM)
    kb = k_ref[pl.ds(start, BAND), cols]
    heads = [kv * Q_GROUP + g for g in range(Q_GROUP)]
    qg = jnp.concatenate([q_ref[:, hd * HEAD_DIM:(hd + 1) * HEAD_DIM] for hd in heads], axis=0)
    sc = lax.dot_general(qg, kb, (((1,), (1,)), ((), ())), preferred_element_type=F32) * ATTN_SCALE
    sc = jnp.where(valid, sc, NEG_INF)
    sk = jnp.concatenate([jnp.full((BLOCK, 1), sink_ref[hd], F32) for hd in heads], axis=0)
    mx = jnp.maximum(jnp.max(sc, axis=1, keepdims=True), sk)
    e = jnp.exp(sc - mx)
    es = jnp.exp(sk - mx)
    inv = 1.0 / (jnp.sum(e, axis=1, keepdims=True) + es)
    return qg, kb, e * inv, es * inv


def _swa_valid(i, start):
    q_pos = i * BLOCK + lax.broadcasted_iota(jnp.int32, (BLOCK, 1), 0)
    q_pos = jnp.concatenate([q_pos] * Q_GROUP, axis=0)
    k_pos = start + lax.broadcasted_iota(jnp.int32, (1, BAND), 1)
    return jnp.abs(k_pos - q_pos) <= WINDOW


def _swa_fwd(q, k, v, sink):
    s = q.shape[0]
    assert s % BLOCK == 0 and s >= BAND

    def body(sink_ref, q_ref, k_ref, v_ref, o_ref):
        i = pl.program_id(0)
        start = _swa_band(i, s)
        valid = _swa_valid(i, start)
        for kv in range(N_KV_HEADS):
            _, _, p, _ = _swa_probs(q_ref, k_ref, sink_ref, kv, start, valid)
            vb = v_ref[pl.ds(start, BAND), kv * HEAD_DIM:(kv + 1) * HEAD_DIM]
            o = jnp.dot(p.astype(BF16), vb, preferred_element_type=F32)
            for g in range(Q_GROUP):
                hd = kv * Q_GROUP + g
                o_ref[:, hd * HEAD_DIM:(hd + 1) * HEAD_DIM] = o[g * BLOCK:(g + 1) * BLOCK].astype(BF16)

    whole = pl.BlockSpec((s, KV_WIDTH), lambda i: (0, 0))
    blk = pl.BlockSpec((BLOCK, ATTN_WIDTH), lambda i: (i, 0))
    return pl.pallas_call(
        body, name="swa_fwd", grid=(s // BLOCK,),
        in_specs=[pl.BlockSpec(memory_space=pltpu.SMEM), blk, whole, whole],
        out_specs=blk,
        out_shape=jax.ShapeDtypeStruct((s, ATTN_WIDTH), BF16),
        compiler_params=_params(1),
    )(sink, q, k, v)


def _swa_bwd(q, k, v, d_out, sink):
    s = q.shape[0]

    def body(sink_ref, q_ref, k_ref, v_ref, do_ref, dq_ref, dk_ref, dv_ref, dsink_ref):
        i = pl.program_id(0)

        @pl.when(i == 0)
        def _():
            dk_ref[...] = jnp.zeros_like(dk_ref)
            dv_ref[...] = jnp.zeros_like(dv_ref)
            dsink_ref[...] = jnp.zeros_like(dsink_ref)

        start = _swa_band(i, s)
        valid = _swa_valid(i, start)
        for kv in range(N_KV_HEADS):
            cols = slice(kv * HEAD_DIM, (kv + 1) * HEAD_DIM)
            qg, kb, p, p_sink = _swa_probs(q_ref, k_ref, sink_ref, kv, start, valid)
            vb = v_ref[pl.ds(start, BAND), cols]
            heads = [kv * Q_GROUP + g for g in range(Q_GROUP)]
            dog = jnp.concatenate([do_ref[:, hd * HEAD_DIM:(hd + 1) * HEAD_DIM] for hd in heads], axis=0)
            dp = lax.dot_general(dog, vb, (((1,), (1,)), ((), ())), preferred_element_type=F32)
            delta = jnp.sum(p * dp, axis=1, keepdims=True)
            ds = (p * (dp - delta) * ATTN_SCALE).astype(BF16)
            dqg = jnp.dot(ds, kb, preferred_element_type=F32)
            dk_ref[pl.ds(start, BAND), cols] += lax.dot_general(ds, qg, (((0,), (0,)), ((), ())), preferred_element_type=F32)
            dv_ref[pl.ds(start, BAND), cols] += lax.dot_general(p.astype(BF16), dog, (((0,), (0,)), ((), ())),
                                                                 preferred_element_type=F32)
            dsk = p_sink * delta
            for g, hd in enumerate(heads):
                dq_ref[:, hd * HEAD_DIM:(hd + 1) * HEAD_DIM] = dqg[g * BLOCK:(g + 1) * BLOCK]
                tot = jnp.sum(dsk[g * BLOCK:(g + 1) * BLOCK], axis=0, keepdims=True)
                dsink_ref[hd:hd + 1, :] -= jnp.broadcast_to(tot, (1, 128))

    whole = pl.BlockSpec((s, KV_WIDTH), lambda i: (0, 0))
    blk = pl.BlockSpec((BLOCK, ATTN_WIDTH), lambda i: (i, 0))
    return pl.pallas_call(
        body, name="swa_bwd", grid=(s // BLOCK,),
        in_specs=[pl.BlockSpec(memory_space=pltpu.SMEM), blk, whole, whole, blk],
        out_specs=[blk, whole, whole, pl.BlockSpec((N_Q_HEADS, 128), lambda i: (0, 0))],
        out_shape=[jax.ShapeDtypeStruct((s, ATTN_WIDTH), F32), jax.ShapeDtypeStruct((s, KV_WIDTH), F32),
                   jax.ShapeDtypeStruct((s, KV_WIDTH), F32), jax.ShapeDtypeStruct((N_Q_HEADS, 128), F32)],
        compiler_params=_params(1),
    )(sink, q, k, v, d_out)


CONV_CHUNK = 256


def _shift_rows(t, rows, down):
    n = t.shape[0]
    rolled = pltpu.roll(t, 1 if down else n - 1, 0)
    edge = 0 if down else n - 1
    return jnp.where(rows == edge, 0.0, rolled)


def _conv_specs(s):
    def z_spec(off):
        return pl.BlockSpec((s, CONV_CHUNK), lambda j, off=off: (0, off // CONV_CHUNK + j))
    chunk = pl.BlockSpec((s, CONV_CHUNK), lambda j: (0, j))
    w_spec = pl.BlockSpec((3, CONV_CHUNK), lambda j: (0, j))
    return z_spec(CU_OFF), z_spec(CB_OFF), z_spec(CC_OFF), chunk, w_spec


def _conv_fwd(z, conv_w):
    s = z.shape[0]
    cu_spec, cb_spec, cc_spec, chunk, w_spec = _conv_specs(s)

    def body(cu_ref, cb_ref, cc_ref, w_ref, o_ref):
        rows = lax.broadcasted_iota(jnp.int32, (s, 1), 0)
        t = cc_ref[...] * cu_ref[...]
        c3 = _shift_rows(t, rows, True) * w_ref[0:1, :] + t * w_ref[1:2, :] + _shift_rows(t, rows, False) * w_ref[2:3, :]
        o_ref[...] = (cb_ref[...] * c3).astype(BF16)

    return pl.pallas_call(
        body, name="conv_fwd", grid=(CONV_WIDTH // CONV_CHUNK,),
        in_specs=[cu_spec, cb_spec, cc_spec, w_spec],
        out_specs=chunk,
        out_shape=jax.ShapeDtypeStruct((s, CONV_WIDTH), BF16),
        compiler_params=_params(1),
    )(z, z, z, conv_w)


def _conv_bwd(z, conv_w, d_co):
    s = z.shape[0]
    cu_spec, cb_spec, cc_spec, chunk, w_spec = _conv_specs(s)

    def body(cu_ref, cb_ref, cc_ref, w_ref, d_ref, dcu_ref, dcb_ref, dcc_ref, dw_ref):
        rows = lax.broadcasted_iota(jnp.int32, (s, 1), 0)
        cu, cc = cu_ref[...], cc_ref[...]
        t = cc * cu
        t_dn, t_up = _shift_rows(t, rows, True), _shift_rows(t, rows, False)
        c3 = t_dn * w_ref[0:1, :] + t * w_ref[1:2, :] + t_up * w_ref[2:3, :]
        d = d_ref[...]
        dcb_ref[...] = (d * c3).astype(BF16)
        dc3 = d * cb_ref[...]
        dw_ref[0:1, :] = jnp.sum(dc3 * t_dn, axis=0, keepdims=True)
        dw_ref[1:2, :] = jnp.sum(dc3 * t, axis=0, keepdims=True)
        dw_ref[2:3, :] = jnp.sum(dc3 * t_up, axis=0, keepdims=True)
        dt = _shift_rows(dc3, rows, False) * w_ref[0:1, :] + dc3 * w_ref[1:2, :] + _shift_rows(dc3, rows, True) * w_ref[2:3, :]
        dcc_ref[...] = (dt * cu).astype(BF16)
        dcu_ref[...] = (dt * cc).astype(BF16)

    return pl.pallas_call(
        body, name="conv_bwd", grid=(CONV_WIDTH // CONV_CHUNK,),
        in_specs=[cu_spec, cb_spec, cc_spec, w_spec, chunk],
        out_specs=[chunk, chunk, chunk, w_spec],
        out_shape=[jax.ShapeDtypeStruct((s, CONV_WIDTH), BF16)] * 3 + [jax.ShapeDtypeStruct((3, CONV_WIDTH), F32)],
        compiler_params=_params(1),
    )(z, z, z, conv_w, d_co)


GATE_CHUNK = 512


def _gate_specs(s, d, tr):
    n_chunks = d // GATE_CHUNK
    za = pl.BlockSpec((tr, GATE_CHUNK), lambda j, i: (i, GL_OFF // GATE_CHUNK + j))
    zc = pl.BlockSpec((tr, GATE_CHUNK), lambda j, i: (i, GL_OFF // GATE_CHUNK + n_chunks + j))
    ba = pl.BlockSpec((1, GATE_CHUNK), lambda j, i: (0, j))
    bc = pl.BlockSpec((1, GATE_CHUNK), lambda j, i: (0, n_chunks + j))
    tile = pl.BlockSpec((tr, GATE_CHUNK), lambda j, i: (i, j))
    return za, zc, ba, bc, tile


def _gate_fwd(z, b_gate, ya, yc):
    s, d = ya.shape
    tr = _row_tile(s, 512)
    za, zc, ba, bc, tile = _gate_specs(s, d, tr)

    def body(za_ref, zc_ref, ba_ref, bc_ref, ya_ref, yc_ref, o_ref):
        ga = jax.nn.sigmoid(za_ref[...] + ba_ref[...])
        gc = jax.nn.sigmoid(zc_ref[...] + bc_ref[...])
        o_ref[...] = (ga * ya_ref[...] + gc * yc_ref[...]).astype(BF16)

    return pl.pallas_call(
        body, name="gate_fwd", grid=(d // GATE_CHUNK, s // tr),
        in_specs=[za, zc, ba, bc, tile, tile],
        out_specs=tile,
        out_shape=jax.ShapeDtypeStruct((s, d), BF16),
        compiler_params=_params(2),
    )(z, z, b_gate, b_gate, ya, yc)


def _gate_bwd(z, b_gate, ya, yc, dmix):
    s, d = ya.shape
    tr = _row_tile(s, 512)
    za, zc, ba, bc, tile = _gate_specs(s, d, tr)
    vec = pl.BlockSpec((1, GATE_CHUNK), lambda j, i: (0, j))

    def body(za_ref, zc_ref, ba_ref, bc_ref, ya_ref, yc_ref, dm_ref, dya_ref, dyc_ref, dla_ref, dlc_ref, dba_ref, dbc_ref):
        ga = jax.nn.sigmoid(za_ref[...] + ba_ref[...])
        gc = jax.nn.sigmoid(zc_ref[...] + bc_ref[...])
        dm = dm_ref[...]
        dya_ref[...] = (dm * ga).astype(BF16)
        dyc_ref[...] = (dm * gc).astype(BF16)
        dla = dm * ya_ref[...] * ga * (1.0 - ga)
        dlc = dm * yc_ref[...] * gc * (1.0 - gc)
        dla_ref[...] = dla.astype(BF16)
        dlc_ref[...] = dlc.astype(BF16)
        pa = jnp.sum(dla, axis=0, keepdims=True)
        pc = jnp.sum(dlc, axis=0, keepdims=True)

        @pl.when(pl.program_id(1) == 0)
        def _():
            dba_ref[...] = pa
            dbc_ref[...] = pc

        @pl.when(pl.program_id(1) > 0)
        def _():
            dba_ref[...] += pa
            dbc_ref[...] += pc

    big = jax.ShapeDtypeStruct((s, d), BF16)
    small = jax.ShapeDtypeStruct((1, d), F32)
    return pl.pallas_call(
        body, name="gate_bwd", grid=(d // GATE_CHUNK, s // tr),
        in_specs=[za, zc, ba, bc, tile, tile, tile],
        out_specs=[tile, tile, tile, tile, vec, vec],
        out_shape=[big, big, big, big, small, small],
        compiler_params=_params(2),
    )(z, z, b_gate, b_gate, ya, yc, dmix)


def _cross_probs(q_ref, kv_ref, hd):
    cols = slice(hd * HEAD_DIM, (hd + 1) * HEAD_DIM)
    qh = q_ref[:, cols]
    kh = kv_ref[:, cols]
    sc = lax.dot_general(qh, kh, (((1,), (1,)), ((), ())), preferred_element_type=F32) * ATTN_SCALE
    e = jnp.exp(sc - jnp.max(sc, axis=1, keepdims=True))
    return qh, kh, e * (1.0 / jnp.sum(e, axis=1, keepdims=True))


def _cross_fwd(qc, kvc):
    s = qc.shape[0]
    n_mem = kvc.shape[0]
    tq = _row_tile(s, 256)

    def body(q_ref, kv_ref, o_ref):
        for hd in range(MEM_HEADS):
            _, _, p = _cross_probs(q_ref, kv_ref, hd)
            vh = kv_ref[:, MEM_WIDTH + hd * HEAD_DIM:MEM_WIDTH + (hd + 1) * HEAD_DIM]
            o_ref[:, hd * HEAD_DIM:(hd + 1) * HEAD_DIM] = jnp.dot(p.astype(BF16), vh, preferred_element_type=F32).astype(BF16)

    return pl.pallas_call(
        body, name="cross_fwd", grid=(s // tq,),
        in_specs=[pl.BlockSpec((tq, MEM_WIDTH), lambda i: (i, 0)), pl.BlockSpec((n_mem, 2 * MEM_WIDTH), lambda i: (0, 0))],
        out_specs=pl.BlockSpec((tq, MEM_WIDTH), lambda i: (i, 0)),
        out_shape=jax.ShapeDtypeStruct((s, MEM_WIDTH), BF16),
        compiler_params=_params(1),
    )(qc, kvc)


def _cross_bwd(qc, kvc, d_out):
    s = qc.shape[0]
    n_mem = kvc.shape[0]
    tq = _row_tile(s, 256)

    def body(q_ref, kv_ref, do_ref, dq_ref, dkv_ref):
        @pl.when(pl.program_id(0) == 0)
        def _():
            dkv_ref[...] = jnp.zeros_like(dkv_ref)

        for hd in range(MEM_HEADS):
            cols = slice(hd * HEAD_DIM, (hd + 1) * HEAD_DIM)
            vcols = slice(MEM_WIDTH + hd * HEAD_DIM, MEM_WIDTH + (hd + 1) * HEAD_DIM)
            qh, kh, p = _cross_probs(q_ref, kv_ref, hd)
            doh = do_ref[:, cols]
            dp = lax.dot_general(doh, kv_ref[:, vcols], (((1,), (1,)), ((), ())), preferred_element_type=F32)
            ds = (p * (dp - jnp.sum(p * dp, axis=1, keepdims=True)) * ATTN_SCALE).astype(BF16)
            dq_ref[:, cols] = jnp.dot(ds, kh, preferred_element_type=F32).astype(BF16)
            dkv_ref[:, cols] += lax.dot_general(ds, qh, (((0,), (0,)), ((), ())), preferred_element_type=F32)
            dkv_ref[:, vcols] += lax.dot_general(p.astype(BF16), doh, (((0,), (0,)), ((), ())), preferred_element_type=F32)

    qspec = pl.BlockSpec((tq, MEM_WIDTH), lambda i: (i, 0))
    kvspec = pl.BlockSpec((n_mem, 2 * MEM_WIDTH), lambda i: (0, 0))
    return pl.pallas_call(
        body, name="cross_bwd", grid=(s // tq,),
        in_specs=[qspec, kvspec, qspec],
        out_specs=[qspec, kvspec],
        out_shape=[jax.ShapeDtypeStruct((s, MEM_WIDTH), BF16), jax.ShapeDtypeStruct((n_mem, 2 * MEM_WIDTH), F32)],
        compiler_params=_params(1),
    )(qc, kvc, d_out)


def _swiglu_fwd(up, gate):
    return up, (gate * jax.nn.sigmoid(gate)) * up


def _swiglu_bwd(d_act, gate, up):
    sg = jax.nn.sigmoid(gate)
    silu = gate * sg
    return d_act * up * (sg * (1.0 + gate * (1.0 - sg))), d_act * silu


GATHER_GROUPS = {"in": ("w_in", "conv_w"), "mix": ("w_attn_out", "w_conv_out", "w_o"), "cross": ("w_cq", "w_ckv", "w_co"),
                 "gate": ("w_gate",), "up": ("w_up",), "down": ("w_down",)}


def _local_step(xs, mems, target, small, fetch, reduce):
    s, d = xs.shape
    w4 = {}
    cos_t, sin_t = _rope_tables(s)

    h = _rmsnorm(xs, small["g_mix"], "norm_mix")
    w4.update(fetch.get("in", h))
    conv4 = w4["conv_w"]
    conv_w = conv4[:, :3, :].transpose(1, 0, 2).reshape(3, N_CHIPS * conv4.shape[2])
    c_in = w4["w_in"].shape[2]
    z = _matmul(h, w4["w_in"], mode="nn", tm=512, tn=c_in, tk=d, out_dtypes=[F32], name="in_proj", b_blocks=N_CHIPS)
    fetch.begin("mix", z)
    q_rot, k_rot, v_b = _rope_fwd(z, cos_t, sin_t)
    attn = _swa_fwd(q_rot, k_rot, v_b, small["sink"])
    co = _conv_fwd(z, conv_w)
    w4.update(fetch.get("mix", co))
    tok = fetch.begin("cross", attn)
    w_o = w4["w_o"].reshape(-1, w4["w_o"].shape[-1])
    c_d = w4["w_attn_out"].shape[2]
    ya = _matmul(attn, w4["w_attn_out"], mode="nn", tm=1024, tn=c_d, tk=ATTN_WIDTH, out_dtypes=[F32], name="attn_out_proj",
                 b_blocks=N_CHIPS, after=tok)
    yc = _matmul(co, w4["w_conv_out"], mode="nn", tm=1024, tn=c_d, tk=CONV_WIDTH, out_dtypes=[F32], name="conv_out_proj",
                 b_blocks=N_CHIPS)
    mix = _gate_fwd(z, small["b_gate"], ya, yc)
    x1 = _matmul(mix, w_o, mode="nn", tm=512, tn=1024, tk=d, out_dtypes=[F32], name="mix_out_proj", extras=[xs],
                 epilogue=_add_residual)
    w4.update(fetch.get("cross", x1))
    tok = fetch.begin("gate", x1)
    w_cq = w4["w_cq"].reshape(-1, w4["w_cq"].shape[-1])
    w_ckv = w4["w_ckv"].reshape(-1, w4["w_ckv"].shape[-1])
    hc = _rmsnorm(x1, small["g_cross"], "norm_cross")
    memn = _rmsnorm(mems, small["g_mem"], "norm_mem")
    qc = _matmul(hc, w_cq, mode="nn", tm=1024, tn=MEM_WIDTH, tk=d, out_dtypes=[BF16], name="cross_q_proj", after=tok)
    kvc = _matmul(memn, w_ckv, mode="nn", tm=256, tn=2 * MEM_WIDTH, tk=d, out_dtypes=[BF16], name="cross_kv_proj")
    oc = _cross_fwd(qc, kvc)
    x2 = _matmul(oc, w4["w_co"], mode="nn", tm=1024, tn=c_d, tk=MEM_WIDTH, out_dtypes=[F32], name="cross_out_proj",
                 extras=[x1], epilogue=_add_residual, b_blocks=N_CHIPS)
    hf = _rmsnorm(x2, small["g_ffn"], "norm_ffn")
    w4.update(fetch.get("gate", hf))
    tok = fetch.begin("up", hf)
    c_ff = w4["w_gate"].shape[2]
    gate = _matmul(hf, w4["w_gate"], mode="nn", tm=512, tn=c_ff, tk=d, out_dtypes=[F32], name="ffn_gate_proj", b_blocks=N_CHIPS,
                   after=tok)
    w4.update(fetch.get("up", gate))
    up, act = _matmul(hf, w4["w_up"], mode="nn", tm=512, tn=c_ff, tk=d, out_dtypes=[F32, BF16], name="ffn_up_proj",
                      extras=[gate], epilogue=_swiglu_fwd, b_blocks=N_CHIPS)
    w4.update(fetch.get("down", act))
    w_down = w4["w_down"].reshape(-1, w4["w_down"].shape[-1])
    x3 = _matmul(act, w_down, mode="nn", tm=512, tn=512, tk=w_down.shape[0], out_dtypes=[F32], name="ffn_down_proj", extras=[x2],
                 epilogue=_add_residual)
    dx3, dx3b, sq, dg_final = _loss_head(x3, small["g_final"], target)

    da, du = _matmul(dx3b, w_down, mode="nt", tm=512, tn=c_ff, tk=d, out_dtypes=[BF16, BF16], name="ffn_down_bwd",
                     extras=[gate, up], epilogue=_swiglu_bwd)
    core = reduce.core
    ffn_shape = dict(row_sharded=False, tm=512, tn=c_ff)
    g_down = _matmul(act, dx3b, mode="tn", tm=c_ff, tn=1024, tk=s, out_dtypes=[BF16], name="ffn_down_wgrad")
    tok = reduce.add("down", {"w_down": g_down}, da)
    t_gate = _wgrad_half(hf, da, core, theirs=True, name="ffn_gate_wgrad_theirs", after=tok, **ffn_shape)
    tok = reduce.step("down", t_gate)
    t_up = _wgrad_half(hf, du, core, theirs=True, name="ffn_up_wgrad_theirs", after=tok, **ffn_shape)
    tok = reduce.send("ffn", {"w_gate": t_gate, "w_up": t_up}, dx3b)
    dhf = _matmul(da, w4["w_gate"], mode="nt", tm=512, tn=1024, tk=N_CHIPS * c_ff, out_dtypes=[F32], name="ffn_gate_bwd", b_blocks=N_CHIPS,
                  after=tok)
    got = reduce.received("ffn", dhf)
    p_gate = _wgrad_half(hf, da, core, theirs=False, name="ffn_gate_wgrad_mine", add=got["w_gate"], **ffn_shape)
    p_up = _wgrad_half(hf, du, core, theirs=False, name="ffn_up_wgrad_mine", add=got["w_up"], **ffn_shape)
    tok = reduce.add_parts("ffn", {"w_gate": p_gate, "w_up": p_up})
    dhf = _matmul(du, w4["w_up"], mode="nt", tm=512, tn=1024, tk=N_CHIPS * c_ff, out_dtypes=[F32], name="ffn_up_bwd", extras=[dhf],
                  epilogue=_add_residual, b_blocks=N_CHIPS, after=tok)
    tok = reduce.step("down", dhf)
    dx2, dx2b, dg_ffn = _rmsnorm_bwd(dhf, x2, small["g_ffn"], dx3, "norm_ffn_bwd")

    d_oc = _matmul(dx2b, w4["w_co"], mode="nt", tm=1024, tn=MEM_WIDTH, tk=d, out_dtypes=[BF16], name="cross_out_bwd",
                   b_blocks=N_CHIPS, after=tok)
    g_co = _matmul(oc, dx2b, mode="tn", tm=MEM_WIDTH, tn=c_d, tk=s, out_dtypes=[BF16], name="cross_out_wgrad", out_blocks=N_CHIPS)
    tok = reduce.step("down", g_co)
    dqc, dkvc = _cross_bwd(qc, kvc, d_oc)
    g_cq = _matmul(hc, dqc, mode="tn", tm=1024, tn=MEM_WIDTH, tk=s, out_dtypes=[BF16], name="cross_q_wgrad", after=tok)
    dhc = _matmul(dqc, w_cq, mode="nt", tm=1024, tn=1024, tk=MEM_WIDTH, out_dtypes=[F32], name="cross_q_bwd")
    g_ckv = _matmul(memn, dkvc, mode="tn", tm=1024, tn=2 * MEM_WIDTH, tk=mems.shape[0], out_dtypes=[BF16], name="cross_kv_wgrad")
    dmemn = _matmul(dkvc, w_ckv, mode="nt", tm=256, tn=1024, tk=2 * MEM_WIDTH, out_dtypes=[F32], name="cross_kv_bwd")
    _, _, dg_mem = _rmsnorm_bwd(dmemn, mems, small["g_mem"], None, "norm_mem_bwd")
    dx1, dx1b, dg_cross = _rmsnorm_bwd(dhc, x1, small["g_cross"], dx2, "norm_cross_bwd")

    dmix = _matmul(dx1b, w_o, mode="nt", tm=512, tn=1024, tk=d, out_dtypes=[F32], name="mix_out_bwd")
    g_o = _matmul(mix, dx1b, mode="tn", tm=1024, tn=1024, tk=s, out_dtypes=[BF16], name="mix_out_wgrad")
    dya, dyc, dgl_a, dgl_c, db_a, db_c = _gate_bwd(z, small["b_gate"], ya, yc, dmix)
    d_attn = _matmul(dya, w4["w_attn_out"], mode="nt", tm=1024, tn=ATTN_WIDTH, tk=d, out_dtypes=[BF16], name="attn_out_bwd",
                     b_blocks=N_CHIPS)
    g_ao = _matmul(attn, dya, mode="tn", tm=ATTN_WIDTH, tn=c_d, tk=s, out_dtypes=[BF16], name="attn_out_wgrad", out_blocks=N_CHIPS)
    d_co = _matmul(dyc, w4["w_conv_out"], mode="nt", tm=1024, tn=CONV_WIDTH, tk=d, out_dtypes=[F32], name="conv_out_bwd",
                   b_blocks=N_CHIPS)
    g_cvo = _matmul(co, dyc, mode="tn", tm=CONV_WIDTH, tn=c_d, tk=s, out_dtypes=[BF16], name="conv_out_wgrad", out_blocks=N_CHIPS)
    tok = reduce.step("ffn", g_cvo)
    tok = reduce.add("mid", {"w_co": g_co, "w_cq": g_cq, "w_ckv": g_ckv, "w_o": g_o, "w_attn_out": g_ao, "w_conv_out": g_cvo}, tok)
    dcu, dcb, dcc, d_conv_w = _conv_bwd(z, conv_w, d_co)
    dq_rot, dk_rot, dv, dsink = _swa_bwd(q_rot, k_rot, v_b, d_attn, small["sink"])
    tok = reduce.step("mid", dq_rot)
    dq, dk, dvb = _rope_bwd(dq_rot, dk_rot, dv, cos_t, sin_t)
    dz = jnp.concatenate([dq, dk, dvb, dcu, dcb, dcc, dgl_a, dgl_c], axis=1)
    in_shape = dict(row_sharded=False, tm=512, tn=c_in)
    t_in = _wgrad_half(h, dz, core, theirs=True, name="in_proj_wgrad_theirs", after=tok, **in_shape)
    tok = reduce.send("in", {"w_in": t_in}, dk)
    tok = reduce.step("ffn", tok)
    tok = reduce.step("mid", tok)
    got = reduce.received("in", tok)
    p_in = _wgrad_half(h, dz, core, theirs=False, name="in_proj_wgrad_mine", add=got["w_in"], **in_shape)
    tok = reduce.add_parts("in", {"w_in": p_in})
    dh = _matmul(dz, w4["w_in"], mode="nt", tm=512, tn=512, tk=N_CHIPS * c_in, out_dtypes=[F32], name="in_proj_bwd", b_blocks=N_CHIPS,
                 after=tok)
    tok = reduce.step("mid", dh)
    grad_x, _, dg_mix = _rmsnorm_bwd(dh, xs, small["g_mix"], dx1, "norm_mix_bwd")

    small_grads = {
        "g_mix": dg_mix, "sink": dsink[:, 0], "b_gate": jnp.concatenate([db_a, db_c], axis=1), "g_cross": dg_cross,
        "g_mem": dg_mem, "g_ffn": dg_ffn, "g_final": dg_final, "conv_w": d_conv_w,
    }
    return sq, grad_x, small_grads


def _pair_sum(g4, ra, core, name):
    nb, rs, cs = g4.shape
    rh = rs // 2
    tr = _row_tile(rh, 256)
    per = rh // tr

    def body(c_ref, g_ref, r_ref, o_ref):
        o_ref[...] = (g_ref[...].astype(F32) + r_ref[...].astype(F32)).astype(BF16)

    plain = pl.BlockSpec((None, tr, cs), lambda j, i, c: (j, i, 0))
    return pl.pallas_call(
        body, name=name,
        grid_spec=pltpu.PrefetchScalarGridSpec(
            num_scalar_prefetch=1, grid=(nb, per),
            in_specs=[pl.BlockSpec((None, tr, cs), lambda j, i, c: (j, c[0] * per + i, 0)), plain],
            out_specs=plain),
        out_shape=jax.ShapeDtypeStruct((nb, rh, cs), BF16),
        compiler_params=_params(2),
    )(core, g4, ra)


def _quad_sum(parts, rc, place, name):
    _, rh, cs = parts.shape
    tr = _row_tile(rh, 256)
    per = rh // tr

    def body(p_ref, own_ref, r_ref, o_ref):
        acc = own_ref[...].astype(F32)
        for j in range(rc.shape[0]):
            acc = acc + r_ref[j].astype(F32)
        o_ref[...] = acc

    return pl.pallas_call(
        body, name=name,
        grid_spec=pltpu.PrefetchScalarGridSpec(
            num_scalar_prefetch=1, grid=(per,),
            in_specs=[pl.BlockSpec((None, tr, cs), lambda i, p: (p[0], i, 0)),
                      pl.BlockSpec((rc.shape[0], tr, cs), lambda i, p: (0, i, 0))],
            out_specs=pl.BlockSpec((tr, cs), lambda i, p: (p[1] * per + i, 0))),
        out_shape=jax.ShapeDtypeStruct((2 * rh, cs), F32),
        compiler_params=_params(1),
    )(place, parts, rc)


def _cast_to_slot(w, place, dtype, name, after=None):
    rows, cols = w.shape
    tr = _row_tile(rows, 256)

    def body(p_ref, w_ref, *rest):
        o_ref = rest[-1]
        o_ref[...] = w_ref[...].astype(dtype)

    return pl.pallas_call(
        body, name=name,
        grid_spec=pltpu.PrefetchScalarGridSpec(
            num_scalar_prefetch=1, grid=(rows // tr,),
            in_specs=[pl.BlockSpec((tr, cols), lambda i, p: (i, 0))] + ([] if after is None else [ANY]),
            out_specs=pl.BlockSpec((None, tr, cols), lambda i, p: (p[0], i, 0))),
        out_shape=jax.ShapeDtypeStruct((N_CHIPS, rows, cols), dtype),
        compiler_params=_params(1),
    )(place, w, *([] if after is None else [after]))


def _adamw(w, g, m, v, name, after=None):
    rows, cols = w.shape
    tr = _row_tile(rows, 256)

    def body(w_ref, g_ref, m_ref, v_ref, *rest):
        go_ref, d_ref, nm_ref, nv_ref = rest[-4:]
        gv = g_ref[...]
        go_ref[...] = gv
        nm = ADAM_B1 * m_ref[...] + (1.0 - ADAM_B1) * gv
        nv = ADAM_B2 * v_ref[...] + (1.0 - ADAM_B2) * (gv * gv)
        m_hat = nm / ADAM_C1
        v_hat = nv / ADAM_C2
        d_ref[...] = -ADAM_LR * (m_hat / (jnp.sqrt(v_hat) + ADAM_EPS) + ADAM_WD * w_ref[...])
        nm_ref[...] = nm
        nv_ref[...] = nv

    tile = pl.BlockSpec((tr, cols), lambda i: (i, 0))
    shape = jax.ShapeDtypeStruct((rows, cols), F32)
    return pl.pallas_call(
        body, name=name, grid=(rows // tr,),
        in_specs=[tile] * 4 + ([] if after is None else [ANY]), out_specs=[tile] * 4, out_shape=[shape] * 4,
        compiler_params=_params(1),
    )(w, g, m, v, *([] if after is None else [after]))


def _mesh_pos():
    return lax.axis_index("x"), lax.axis_index("y"), lax.axis_index("c")


def _other_chips(x, y):
    return [(1 - x, y), (x, 1 - y), (1 - x, 1 - y)]


def _half_rows(ref, which):
    rh = ref.shape[-2] // 2
    return ref.at[pl.ds(which * rh, rh), :]


def _remote(src, dst, send_sems, recv_sems, sem, to):
    return pltpu.make_async_remote_copy(src_ref=src, dst_ref=dst, send_sem=send_sems.at[sem], recv_sem=recv_sems.at[sem],
                                        device_id=to, device_id_type=MESH)


HBM = pl.BlockSpec(memory_space=pltpu.HBM)
SEM = pl.BlockSpec(memory_space=pltpu.SEMAPHORE)
DATAFLOW_EFFECT = pltpu.SideEffectType.DATAFLOW_SIDE_EFFECTING


def _in_hbm(arrays):
    return [pltpu.with_memory_space_constraint(a, pltpu.HBM) for a in arrays]


def _hbm_like(arrays):
    return [pltpu.HBM(a.shape, a.dtype) for a in arrays]


def _gather_start(bufs, groups, name):
    n, ng = len(bufs), len(groups)

    def body(*refs):
        ins = refs[:n]
        send, recv, token = refs[n:n + ng], refs[n + ng:n + 2 * ng], refs[-1]
        x, y, c = _mesh_pos()
        me = 2 * x + y
        for g, members in enumerate(groups):
            for i, w in enumerate(members):
                mine = _half_rows(ins[w].at[me], c)
                for k, (px, py) in enumerate(_other_chips(x, y)):
                    _remote(mine, mine, send[g], recv[g], 3 * i + k, (px, py, c)).start()
        token[...] = jnp.zeros_like(token)

    sems = [pltpu.SemaphoreType.DMA((3 * len(m),)) for m in groups]
    outs = pl.pallas_call(
        body, name=name,
        in_specs=[HBM] * n, out_specs=[SEM] * (2 * ng) + [HBM] * n + [pl.BlockSpec(memory_space=pltpu.VMEM)],
        out_shape=sems + sems + _hbm_like(bufs) + [jax.ShapeDtypeStruct((8, 128), F32)],
        input_output_aliases={i: 2 * ng + i for i in range(n)},
        compiler_params=pltpu.CompilerParams(has_side_effects=DATAFLOW_EFFECT),
    )(*_in_hbm(bufs))
    return outs[:ng], outs[ng:2 * ng], outs[2 * ng:2 * ng + n], outs[-1]


def _gather_pass(bufs, send, recv, after, name):
    m = len(bufs)

    def body(*refs):
        ins, send_in, recv_in = refs[:m], refs[m], refs[m + 1]
        send_out, recv_out, token = refs[m + 3], refs[m + 4], refs[-1]
        x, y, c = _mesh_pos()
        for i in range(m):
            for k, (px, py) in enumerate(_other_chips(x, y)):
                landed = _half_rows(ins[i].at[2 * px + py], c)
                came = _remote(landed, landed, send_in, recv_in, 3 * i + k, (px, py, c))
                came.wait_recv()
                came.wait_send()
                _remote(landed, landed, send_out, recv_out, 3 * i + k, (x, y, 1 - c)).start()
        token[...] = jnp.zeros_like(token)

    sems = [pltpu.SemaphoreType.DMA((3 * m,))] * 2
    outs = pl.pallas_call(
        body, name=name,
        in_specs=[HBM] * m + [SEM, SEM, ANY], out_specs=[SEM, SEM] + [HBM] * m + [pl.BlockSpec(memory_space=pltpu.VMEM)],
        out_shape=sems + _hbm_like(bufs) + [jax.ShapeDtypeStruct((8, 128), F32)],
        input_output_aliases={i: 2 + i for i in range(m)},
        compiler_params=pltpu.CompilerParams(has_side_effects=DATAFLOW_EFFECT),
    )(*_in_hbm(bufs), send, recv, after)
    return outs[0], outs[1], outs[2:2 + m], outs[-1]


def _gather_done(bufs, send, recv, after, name):
    m = len(bufs)

    def body(*refs):
        ins, send_in, recv_in = refs[:m], refs[m], refs[m + 1]
        x, y, c = _mesh_pos()
        for i in range(m):
            for k, (px, py) in enumerate(_other_chips(x, y)):
                passed = _half_rows(ins[i].at[2 * px + py], 1 - c)
                came = _remote(passed, passed, send_in, recv_in, 3 * i + k, (x, y, 1 - c))
                came.wait_send()
                came.wait_recv()

    return pl.pallas_call(
        body, name=name,
        in_specs=[HBM] * m + [SEM, SEM, ANY], out_specs=[HBM] * m,
        out_shape=_hbm_like(bufs),
        input_output_aliases={i: i for i in range(m)},
        compiler_params=pltpu.CompilerParams(has_side_effects=DATAFLOW_EFFECT),
    )(*_in_hbm(bufs), send, recv, after)


class _Gather:
    def __init__(self, groups):
        self.groups = groups
        self.landing = {}
        self.passing = {}

    def start(self, slotted, group_names, name):
        names = [n for g in group_names for n in self.groups[g]]
        index = {n: i for i, n in enumerate(names)}
        members = [[index[n] for n in self.groups[g]] for g in group_names]
        send, recv, bufs, token = _gather_start([slotted[n] for n in names], members, name)
        for j, g in enumerate(group_names):
            self.landing[g] = (send[j], recv[j], [bufs[index[n]] for n in self.groups[g]])
        return token

    def begin(self, group, after):
        send, recv, bufs = self.landing.pop(group)
        send, recv, bufs, token = _gather_pass(bufs, send, recv, after, "gather_pass_" + group)
        self.passing[group] = (send, recv, bufs)
        return token

    def get(self, group, after):
        if group not in self.passing:
            self.begin(group, after)
        send, recv, bufs = self.passing.pop(group)
        return dict(zip(self.groups[group], _gather_done(bufs, send, recv, after, "gather_done_" + group)))


def _sibling_halves_copies(srcs, dsts, x, y, c):
    out = []
    for s_ref, d_ref in zip(srcs, dsts, strict=True):
        rh = s_ref.shape[1] // 2
        out.append((s_ref.at[:, pl.ds((1 - c) * rh, rh), :], d_ref, (x, y, 1 - c)))
    return out


def _to_sibling_copies(srcs, dsts, x, y, c):
    return [(s_ref, d_ref, (x, y, 1 - c)) for s_ref, d_ref in zip(srcs, dsts, strict=True)]


def _chip_copies(srcs, dsts, x, y, c):
    out = []
    for s_ref, d_ref in zip(srcs, dsts, strict=True):
        for k, (px, py) in enumerate(_other_chips(x, y)):
            out.append((s_ref.at[2 * px + py], d_ref.at[k], (px, py, c)))
    return out


def _join_copies(srcs, dsts, x, y, c):
    out = []
    for s_ref in srcs:
        mine = _half_rows(s_ref, c)
        out.append((mine, mine, (x, y, 1 - c)))
    return out


def _exchange_start(copies_fn, n_copies, srcs, fresh, after, name):
    ns, nb = len(srcs), len(srcs) + len(fresh)

    def body(*refs):
        bufs, send, recv, token = refs[:nb], refs[nb + 1], refs[nb + 2], refs[-1]
        x, y, c = _mesh_pos()
        for i, (s_ref, d_ref, to) in enumerate(copies_fn(bufs[:ns], bufs[ns:] if fresh else bufs[:ns], x, y, c)):
            _remote(s_ref, d_ref, send, recv, i, to).start()
        token[...] = jnp.zeros_like(token)

    sems = [pltpu.SemaphoreType.DMA((n_copies,))] * 2
    outs = pl.pallas_call(
        body, name=name,
        in_specs=[HBM] * nb + [ANY], out_specs=[SEM, SEM] + [HBM] * nb + [pl.BlockSpec(memory_space=pltpu.VMEM)],
        out_shape=sems + _hbm_like(list(srcs) + list(fresh)) + [jax.ShapeDtypeStruct((8, 128), F32)],
        input_output_aliases={i: 2 + i for i in range(nb)},
        compiler_params=pltpu.CompilerParams(has_side_effects=DATAFLOW_EFFECT),
    )(*_in_hbm(list(srcs) + list(fresh)), after)
    return outs[0], outs[1], outs[2:2 + ns], outs[2 + ns:2 + nb], outs[-1]


def _exchange_done(copies_fn, srcs, fresh, send, recv, after, name):
    ns, nb = len(srcs), len(srcs) + len(fresh)

    def body(*refs):
        bufs, send_in, recv_in = refs[:nb], refs[nb], refs[nb + 1]
        x, y, c = _mesh_pos()
        for i, (s_ref, d_ref, to) in enumerate(copies_fn(bufs[:ns], bufs[ns:] if fresh else bufs[:ns], x, y, c)):
            came = _remote(s_ref, d_ref, send_in, recv_in, i, to)
            came.wait_send()
            came.wait_recv()

    outs = pl.pallas_call(
        body, name=name,
        in_specs=[HBM] * nb + [SEM, SEM, ANY], out_specs=[HBM] * nb,
        out_shape=_hbm_like(list(srcs) + list(fresh)),
        input_output_aliases={i: i for i in range(nb)},
        compiler_params=pltpu.CompilerParams(has_side_effects=DATAFLOW_EFFECT),
    )(*_in_hbm(list(srcs) + list(fresh)), send, recv, after)
    return outs[:ns], outs[ns:]


class _Reduce:
    def __init__(self, place, core, shards, mom_m, mom_v):
        self.place, self.core = place, core
        self.shards, self.mom_m, self.mom_v = shards, mom_m, mom_v
        self.state = {}
        self.results = {}

    def add(self, group, grads, after):
        names = list(grads)
        g4s = [g.reshape((N_CHIPS, -1, g.shape[-1])) if g.ndim == 2 else g for g in grads.values()]
        fresh = [lax.empty((N_CHIPS, g.shape[1] // 2, g.shape[2]), BF16) for g in g4s]
        send, recv, g4s, fresh, token = _exchange_start(_sibling_halves_copies, len(names), g4s, fresh, after,
                                                        "pair_start_" + group)
        self.state[group] = (0, names, send, recv, g4s, fresh)
        return token

    def send(self, group, theirs, after):
        names, srcs = list(theirs), list(theirs.values())
        fresh = [lax.empty(s.shape, BF16) for s in srcs]
        send, recv, srcs, fresh, token = _exchange_start(_to_sibling_copies, len(names), srcs, fresh, after, "pair_start_" + group)
        self.state[group] = ("sent", names, send, recv, srcs, fresh)
        return token

    def received(self, group, after):
        stage, names, send, recv, srcs, fresh = self.state.pop(group)
        assert stage == "sent"
        _, got = _exchange_done(_to_sibling_copies, srcs, fresh, send, recv, after, "pair_done_" + group)
        return dict(zip(names, got))

    def add_parts(self, group, parts):
        names, srcs = list(parts), list(parts.values())
        fresh = [lax.empty((N_CHIPS - 1,) + p.shape[1:], BF16) for p in srcs]
        send, recv, srcs, fresh, token = _exchange_start(_chip_copies, 3 * len(names), srcs, fresh, self.core, "chips_start_" + group)
        self.state[group] = (1, names, send, recv, srcs, fresh)
        return token

    def step(self, group, after):
        stage, names, send, recv, srcs, fresh = self.state[group]
        if stage == 0:
            g4s, ras = _exchange_done(_sibling_halves_copies, srcs, fresh, send, recv, after, "pair_done_" + group)
            parts = [_pair_sum(g, r, self.core, "pair_sum_" + n) for g, r, n in zip(g4s, ras, names)]
            fresh = [lax.empty((N_CHIPS - 1,) + p.shape[1:], BF16) for p in parts]
            send, recv, parts, fresh, token = _exchange_start(_chip_copies, 3 * len(names), parts, fresh, self.core,
                                                              "chips_start_" + group)
            self.state[group] = (1, names, send, recv, parts, fresh)
            return token
        if stage == 1:
            parts, rcs = _exchange_done(_chip_copies, srcs, fresh, send, recv, after, "chips_done_" + group)
            wholes = [_quad_sum(p, r, self.place, "quad_sum_" + n) for p, r, n in zip(parts, rcs, names)]
            send, recv, wholes, _, token = _exchange_start(_join_copies, len(names), wholes, [], self.core, "join_start_" + group)
            self.state[group] = (2, names, send, recv, wholes, [])
            return token
        assert stage == 2
        wholes, _ = _exchange_done(_join_copies, srcs, [], send, recv, after, "join_done_" + group)
        token = None
        for n, g in zip(names, wholes):
            self.results[n] = _adamw(self.shards[n], g, self.mom_m[n], self.mom_v[n], "adamw_" + n, after=token)
            token = self.results[n][1]
        del self.state[group]
        return token


N_DEV = 8


def _all_reduce_small(v):
    def body(v_ref, o_ref, slots, send_sems, recv_sems):
        x, y, c = _mesh_pos()
        me = 4 * x + 2 * y + c
        slots[me] = v_ref[...]
        peers = []
        for r in range(1, N_DEV):
            fx, fy, fc = (r >> 2) & 1, (r >> 1) & 1, r & 1
            peers.append((x + fx - 2 * x * fx, y + fy - 2 * y * fy, c + fc - 2 * c * fc))
        sends = []
        for r, peer in enumerate(peers):
            cp = _remote(v_ref, slots.at[me], send_sems, recv_sems, r, peer)
            cp.start()
            sends.append(cp)
        for r, (px, py, pc) in enumerate(peers):
            landed = slots.at[4 * px + 2 * py + pc]
            _remote(landed, landed, send_sems, recv_sems, r, (px, py, pc)).wait_recv()
        for cp in sends:
            cp.wait_send()
        acc = slots[0]
        for i in range(1, N_DEV):
            acc = acc + slots[i]
        o_ref[...] = acc

    vm = pl.BlockSpec(memory_space=pltpu.VMEM)
    return pl.pallas_call(
        body, name="small_grads_all_reduce",
        in_specs=[vm], out_specs=vm,
        out_shape=jax.ShapeDtypeStruct(v.shape, v.dtype),
        scratch_shapes=[pltpu.VMEM((N_DEV,) + v.shape, v.dtype), pltpu.SemaphoreType.DMA((N_DEV - 1,)),
                        pltpu.SemaphoreType.DMA((N_DEV - 1,))],
    )(v)


MATRICES = ("w_in", "w_attn_out", "w_conv_out", "w_o", "w_cq", "w_ckv", "w_co", "w_gate", "w_up", "w_down")
VECTORS = ("g_mix", "b_gate", "g_cross", "g_mem", "g_ffn", "g_final", "conv_w", "sink")
WEIGHT_ORDER = ("g_mix", "w_in", "sink", "conv_w", "b_gate", "w_attn_out", "w_conv_out", "w_o", "g_cross", "g_mem", "w_cq",
                "w_ckv", "w_co", "g_ffn", "w_gate", "w_up", "w_down", "g_final")
CONV_PAD_ROWS = 16
SMALL_ROWS = 8


def _pack(pieces):
    flat = jnp.concatenate([p.reshape(-1) for p in pieces])
    lane_group = SMALL_ROWS * 128
    total = -(-flat.shape[0] // lane_group) * lane_group
    flat = jnp.pad(flat, (0, total - flat.shape[0]))
    return flat.reshape(SMALL_ROWS, total // SMALL_ROWS), [p.size for p in pieces]


def _unpack(packed, pieces):
    flat = packed.reshape(-1)
    out, off = [], 0
    for p in pieces:
        out.append(flat[off:off + p.size].reshape(p.shape))
        off += p.size
    return out


def kernel(x, mem, g_mix, w_in, sink, conv_w, b_gate, w_attn_out, w_conv_out, w_o, g_cross, g_mem, w_cq, w_ckv, w_co, g_ffn, w_gate, w_up, w_down, g_final, loss_target, m_g_mix, m_w_in, m_sink, m_conv_w, m_b_gate, m_w_attn_out, m_w_conv_out, m_w_o, m_g_cross, m_g_mem, m_w_cq, m_w_ckv, m_w_co, m_g_ffn, m_w_gate, m_w_up, m_w_down, m_g_final, v_g_mix, v_w_in, v_sink, v_conv_w, v_b_gate, v_w_attn_out, v_w_conv_out, v_w_o, v_g_cross, v_g_mem, v_w_cq, v_w_ckv, v_w_co, v_g_ffn, v_w_gate, v_w_up, v_w_down, v_g_final):
    given = dict(g_mix=g_mix, w_in=w_in, sink=sink, conv_w=conv_w, b_gate=b_gate, w_attn_out=w_attn_out, w_conv_out=w_conv_out,
                 w_o=w_o, g_cross=g_cross, g_mem=g_mem, w_cq=w_cq, w_ckv=w_ckv, w_co=w_co, g_ffn=g_ffn, w_gate=w_gate, w_up=w_up,
                 w_down=w_down, g_final=g_final)
    mom_m = dict(g_mix=m_g_mix, w_in=m_w_in, sink=m_sink, conv_w=m_conv_w, b_gate=m_b_gate, w_attn_out=m_w_attn_out,
                 w_conv_out=m_w_conv_out, w_o=m_w_o, g_cross=m_g_cross, g_mem=m_g_mem, w_cq=m_w_cq, w_ckv=m_w_ckv, w_co=m_w_co,
                 g_ffn=m_g_ffn, w_gate=m_w_gate, w_up=m_w_up, w_down=m_w_down, g_final=m_g_final)
    mom_v = dict(g_mix=v_g_mix, w_in=v_w_in, sink=v_sink, conv_w=v_conv_w, b_gate=v_b_gate, w_attn_out=v_w_attn_out,
                 w_conv_out=v_w_conv_out, w_o=v_w_o, g_cross=v_g_cross, g_mem=v_g_mem, w_cq=v_w_cq, w_ckv=v_w_ckv, w_co=v_w_co,
                 g_ffn=v_g_ffn, w_gate=v_w_gate, w_up=v_w_up, w_down=v_w_down, g_final=v_g_final)
    xs, mems, target = x[0], mem[0], loss_target[0]
    d_model = xs.shape[1]
    chip = 2 * lax.axis_index("x") + lax.axis_index("y")
    core = jnp.reshape(lax.axis_index("c"), (1,)).astype(jnp.int32)
    place = jnp.stack([chip, lax.axis_index("c")]).astype(jnp.int32)

    shards = {n: given[n][0] for n in MATRICES}
    conv_cols = conv_w.shape[2]
    conv_pad = jnp.pad(conv_w[0], ((0, CONV_PAD_ROWS - conv_w.shape[1]), (0, 0)))
    fetch = _Gather(GATHER_GROUPS)
    first = {"w_in": _cast_to_slot(shards["w_in"], place, BF16, "to_slot_w_in"),
             "conv_w": _cast_to_slot(conv_pad, place, F32, "to_slot_conv_w")}
    tok = fetch.start(first, ["in"], "gather_start_in")
    rest = {n: _cast_to_slot(shards[n], place, BF16, "to_slot_" + n, after=tok) for n in MATRICES if n != "w_in"}
    fetch.start(rest, [g for g in GATHER_GROUPS if g != "in"], "gather_start_rest")
    small = {n: given[n] for n in ("g_mix", "b_gate", "g_cross", "g_mem", "g_ffn")}
    small["g_final"] = g_final[None]
    small["sink"] = sink[0]

    reduce = _Reduce(place, core, shards, {n: mom_m[n][0] for n in MATRICES}, {n: mom_v[n][0] for n in MATRICES})
    sq, grad_x, small_grads = _local_step(xs, mems, target, small, fetch, reduce)

    loss_part = 0.5 * sq[0:1, 0:1] / d_model
    pieces = [small_grads[n] for n in VECTORS] + [loss_part]
    packed, _ = _pack(pieces)
    summed = _unpack(_all_reduce_small(packed), pieces)
    loss = summed[-1][0, 0]
    small_sum = dict(zip(VECTORS, summed[:-1]))
    small_sum["conv_w"] = lax.dynamic_slice_in_dim(small_sum["conv_w"], chip * conv_cols, conv_cols, axis=1)

    grad_out, delta, new_m, new_v = {}, {}, {}, {}
    like = [given[n] for n in VECTORS]
    pw, _ = _pack(like)
    pg, _ = _pack([small_sum[n] for n in VECTORS])
    pm, _ = _pack([mom_m[n] for n in VECTORS])
    pv, _ = _pack([mom_v[n] for n in VECTORS])
    _, pd, pnm, pnv = _adamw(pw, pg, pm, pv, "adamw_small")
    for n, g, d, nm, nv in zip(VECTORS, [small_sum[n] for n in VECTORS], _unpack(pd, like), _unpack(pnm, like), _unpack(pnv, like)):
        grad_out[n] = g.reshape(given[n].shape)
        delta[n], new_m[n], new_v[n] = d, nm, nv
    tok = reduce.step("in", pd)
    reduce.step("in", tok)
    for n in MATRICES:
        g, d, nm, nv = reduce.results[n]
        grad_out[n], delta[n], new_m[n], new_v[n] = g[None], d[None], nm[None], nv[None]

    return (loss, grad_x[None], *[grad_out[n] for n in WEIGHT_ORDER], *[delta[n] for n in WEIGHT_ORDER],
            *[new_m[n] for n in WEIGHT_ORDER], *[new_v[n] for n in WEIGHT_ORDER])
```

```python
import functools

import jax
import jax.numpy as jnp
from jax import lax
from jax.experimental import pallas as pl
from jax.experimental.pallas import tpu as pltpu

F32 = jnp.float32
BF16 = jnp.bfloat16
MESH = pl.DeviceIdType.MESH
ANY = pl.BlockSpec(memory_space=pl.ANY)

VMEM_LIMIT_BYTES = 56 * 1024 * 1024

N_CHIPS = 4
HEAD_DIM = 128
N_Q_HEADS = 8
N_KV_HEADS = 2
Q_GROUP = N_Q_HEADS // N_KV_HEADS
ATTN_WIDTH = N_Q_HEADS * HEAD_DIM
KV_WIDTH = N_KV_HEADS * HEAD_DIM
WINDOW = 128
BLOCK = 128
BAND = 3 * BLOCK
ROPE_THETA = 10000.0
CONV_WIDTH = 1024
MEM_HEADS = 4
MEM_WIDTH = MEM_HEADS * HEAD_DIM
RMS_EPS = 1e-6
NEG_INF = -1e30
ATTN_SCALE = HEAD_DIM ** -0.5

Q_OFF, K_OFF, V_OFF, CU_OFF, CB_OFF, CC_OFF, GL_OFF = 0, 1024, 1280, 1536, 2560, 3584, 4608

ADAM_LR = 0.001
ADAM_B1 = 0.9
ADAM_B2 = 0.999
ADAM_EPS = 1e-08
ADAM_WD = 0.01
ADAM_STEP = 10
ADAM_C1 = 1.0 - ADAM_B1 ** ADAM_STEP
ADAM_C2 = 1.0 - ADAM_B2 ** ADAM_STEP


def _params(n_grid_axes):
    return pltpu.CompilerParams(dimension_semantics=("arbitrary",) * n_grid_axes, vmem_limit_bytes=VMEM_LIMIT_BYTES)


BF16_SUBLANES = 16


def _row_tile(rows, want):
    if rows <= want:
        return rows
    for t in range(want, 0, -BF16_SUBLANES):
        if rows % t == 0:
            return t
    return rows


def _matmul(a, b, *, mode, tm, tn, tk, out_dtypes, name, extras=(), epilogue=None, b_blocks=1, out_blocks=1, after=None):
    if mode == "tn":
        kdim, m = a.shape
    else:
        m, kdim = a.shape
    if b_blocks > 1:
        nb, brows, bcols = b.shape
        assert nb == b_blocks
        if mode == "nn":
            n = bcols * nb
            assert brows == kdim
        else:
            assert mode == "nt" and bcols * nb == kdim
            n = brows
    else:
        n = b.shape[0] if mode == "nt" else b.shape[1]
    tm, tn = min(tm, m), min(tn, n)
    assert m % tm == 0 and n % tn == 0 and tk == kdim, (name, m, n, kdim, tm, tn, tk)
    n_extra, n_out = len(extras), len(out_dtypes)
    n_after = 0 if after is None else 1

    if mode == "tn":
        a_spec = pl.BlockSpec((tk, tm), lambda j, i, k: (k, i))
        dims = (((0,), (0,)), ((), ()))
    else:
        a_spec = pl.BlockSpec((tm, tk), lambda j, i, k: (i, k))
        dims = (((1,), (0,)), ((), ())) if mode == "nn" else (((1,), (1,)), ((), ()))

    if b_blocks > 1 and mode == "nn":
        per = b.shape[2] // tn
        assert b.shape[2] % tn == 0
        b_spec = pl.BlockSpec((None, tk, tn), lambda j, i, k: (j // per, k, j % per))
    elif b_blocks > 1:
        b_spec = pl.BlockSpec((b_blocks, tn, b.shape[2]), lambda j, i, k: (0, j, 0))
    elif mode == "nt":
        b_spec = pl.BlockSpec((tn, tk), lambda j, i, k: (j, k))
    else:
        b_spec = pl.BlockSpec((tk, tn), lambda j, i, k: (k, j))

    tile_spec = pl.BlockSpec((tm, tn), lambda j, i, k: (i, j))
    if out_blocks > 1:
        ncols = n // out_blocks
        assert ncols % tn == 0
        oper = ncols // tn
        out_spec = pl.BlockSpec((None, tm, tn), lambda j, i, k: (j // oper, i, j % oper))
        out_shape = [jax.ShapeDtypeStruct((out_blocks, m, ncols), dt) for dt in out_dtypes]
    else:
        out_spec = tile_spec
        out_shape = [jax.ShapeDtypeStruct((m, n), dt) for dt in out_dtypes]

    def body(a_ref, b_ref, *rest):
        extra_refs = rest[:n_extra]
        out_refs = rest[n_extra + n_after:n_extra + n_after + n_out]
        if mode == "nt" and b_blocks > 1:
            cs = b.shape[2]
            acc = None
            for jb in range(b_blocks):
                prod = lax.dot_general(a_ref[:, jb * cs:(jb + 1) * cs].astype(BF16), b_ref[jb].astype(BF16), dims,
                                       preferred_element_type=F32)
                acc = prod if acc is None else acc + prod
        else:
            acc = lax.dot_general(a_ref[...].astype(BF16), b_ref[...].astype(BF16), dims, preferred_element_type=F32)
        tiles = (acc,) if epilogue is None else epilogue(acc, *[r[...] for r in extra_refs])
        for o_ref, t in zip(out_refs, tiles, strict=True):
            o_ref[...] = t.astype(o_ref.dtype)

    outs = pl.pallas_call(
        body,
        name=name,
        grid=(n // tn, m // tm, 1),
        in_specs=[a_spec, b_spec] + [tile_spec] * n_extra + [ANY] * n_after,
        out_specs=[out_spec] * n_out,
        out_shape=out_shape,
        compiler_params=_params(3),
    )(a, b, *extras, *([] if after is None else [after]))
    return outs[0] if n_out == 1 else outs


def _add_residual(acc, res):
    return (acc + res,)


def _wgrad_half(a, b, core, *, theirs, row_sharded, tm, tn, name, add=None, after=None):
    kdim, m = a.shape
    n = b.shape[1]
    rs, cs = (m // N_CHIPS, n) if row_sharded else (m, n // N_CHIPS)
    rh = rs // 2
    tm, tn = min(tm, rh), min(tn, cs)
    assert rh % tm == 0 and cs % tn == 0, (name, rh, cs, tm, tn)
    mh, per = rh // tm, cs // tn
    has_add = add is not None

    def half(c):
        return 1 - c[0] if theirs else c[0]

    if row_sharded:
        grid = (n // tn, N_CHIPS * mh)
        a_spec = pl.BlockSpec((kdim, tm), lambda j, r, c: (0, ((r // mh) * 2 + half(c)) * mh + r % mh))
        o_spec = pl.BlockSpec((None, tm, tn), lambda j, r, c: (r // mh, r % mh, j))
    else:
        grid = (n // tn, mh)
        a_spec = pl.BlockSpec((kdim, tm), lambda j, r, c: (0, half(c) * mh + r))
        o_spec = pl.BlockSpec((None, tm, tn), lambda j, r, c: (j // per, r, j % per))
    b_spec = pl.BlockSpec((kdim, tn), lambda j, r, c: (0, j))

    def body(c_ref, a_ref, b_ref, *rest):
        o_ref = rest[-1]
        acc = lax.dot_general(a_ref[...].astype(BF16), b_ref[...].astype(BF16), (((0,), (0,)), ((), ())),
                              preferred_element_type=F32)
        if has_add:
            acc = acc + rest[0][...].astype(F32)
        o_ref[...] = acc.astype(BF16)

    operands = [a, b] + ([add] if has_add else []) + ([] if after is None else [after])
    return pl.pallas_call(
        body, name=name,
        grid_spec=pltpu.PrefetchScalarGridSpec(
            num_scalar_prefetch=1, grid=grid,
            in_specs=[a_spec, b_spec] + ([o_spec] if has_add else []) + ([] if after is None else [ANY]),
            out_specs=o_spec),
        out_shape=jax.ShapeDtypeStruct((N_CHIPS, rh, cs), BF16),
        compiler_params=_params(2),
    )(core, *operands)


def _rstd(x):
    return lax.rsqrt(jnp.mean(x * x, axis=-1, keepdims=True) + RMS_EPS)


def _rmsnorm(x, g, name):
    s, d = x.shape
    tr = _row_tile(s, 256)

    def body(x_ref, g_ref, o_ref):
        xv = x_ref[...]
        o_ref[...] = (xv * _rstd(xv) * g_ref[...]).astype(BF16)

    return pl.pallas_call(
        body, name=name, grid=(s // tr,),
        in_specs=[pl.BlockSpec((tr, d), lambda i: (i, 0)), pl.BlockSpec((1, d), lambda i: (0, 0))],
        out_specs=pl.BlockSpec((tr, d), lambda i: (i, 0)),
        out_shape=jax.ShapeDtypeStruct((s, d), BF16),
        compiler_params=_params(1),
    )(x, g)


def _rmsnorm_bwd(dh, x, g, dres, name):
    s, d = x.shape
    tr = _row_tile(s, 256)
    has_res = dres is not None

    def body(*refs):
        if has_res:
            dh_ref, x_ref, g_ref, res_ref, dx_ref, dxb_ref, dg_ref = refs
        else:
            dh_ref, x_ref, g_ref, dx_ref, dxb_ref, dg_ref = refs
        xv = x_ref[...]
        dhv = dh_ref[...].astype(F32)
        r = _rstd(xv)
        xn = xv * r
        dhg = dhv * g_ref[...]
        dx = r * (dhg - xn * jnp.mean(dhg * xn, axis=-1, keepdims=True))
        if has_res:
            dx = dx + res_ref[...]
        dx_ref[...] = dx
        dxb_ref[...] = dx.astype(BF16)
        part = jnp.sum(dhv * xn, axis=0, keepdims=True)

        @pl.when(pl.program_id(0) == 0)
        def _():
            dg_ref[...] = part

        @pl.when(pl.program_id(0) > 0)
        def _():
            dg_ref[...] += part

    row = pl.BlockSpec((tr, d), lambda i: (i, 0))
    vec = pl.BlockSpec((1, d), lambda i: (0, 0))
    return pl.pallas_call(
        body, name=name, grid=(s // tr,),
        in_specs=[row, row, vec] + ([row] if has_res else []),
        out_specs=[row, row, vec],
        out_shape=[jax.ShapeDtypeStruct((s, d), F32), jax.ShapeDtypeStruct((s, d), BF16), jax.ShapeDtypeStruct((1, d), F32)],
        compiler_params=_params(1),
    )(*([dh, x, g] + ([dres] if has_res else [])))


def _loss_head(x3, g, target):
    s, d = x3.shape
    tr = _row_tile(s, 256)

    def body(x_ref, g_ref, t_ref, dx_ref, dxb_ref, sq_ref, dg_ref):
        xv = x_ref[...]
        gv = g_ref[...]
        r = _rstd(xv)
        xn = xv * r
        err = xn * gv - t_ref[...]
        dy = err * (1.0 / d)
        dyg = dy * gv
        dx = r * (dyg - xn * jnp.mean(dyg * xn, axis=-1, keepdims=True))
        dx_ref[...] = dx
        dxb_ref[...] = dx.astype(BF16)
        sq = jnp.sum(jnp.sum(err * err, axis=1, keepdims=True), axis=0, keepdims=True)
        sq = jnp.broadcast_to(sq, (1, 128))
        part = jnp.sum(dy * xn, axis=0, keepdims=True)

        @pl.when(pl.program_id(0) == 0)
        def _():
            sq_ref[...] = sq
            dg_ref[...] = part

        @pl.when(pl.program_id(0) > 0)
        def _():
            sq_ref[...] += sq
            dg_ref[...] += part

    row = pl.BlockSpec((tr, d), lambda i: (i, 0))
    vec = pl.BlockSpec((1, d), lambda i: (0, 0))
    return pl.pallas_call(
        body, name="loss_head", grid=(s // tr,),
        in_specs=[row, vec, row],
        out_specs=[row, row, pl.BlockSpec((1, 128), lambda i: (0, 0)), vec],
        out_shape=[jax.ShapeDtypeStruct((s, d), F32), jax.ShapeDtypeStruct((s, d), BF16),
                   jax.ShapeDtypeStruct((1, 128), F32), jax.ShapeDtypeStruct((1, d), F32)],
        compiler_params=_params(1),
    )(x3, g, target)


def _rope_tables(s):
    inv = 1.0 / (ROPE_THETA ** (jnp.arange(0, HEAD_DIM, 2, dtype=F32) / HEAD_DIM))
    ang = jnp.arange(s, dtype=F32)[:, None] * inv[None, :]
    cos, sin = jnp.cos(ang), jnp.sin(ang)
    return jnp.concatenate([cos, cos], axis=1), jnp.concatenate([-sin, sin], axis=1)


def _swap_halves(t):
    return pltpu.roll(t, HEAD_DIM // 2, 1)


def _rope_fwd(z, cos_t, sin_t):
    s = z.shape[0]
    tr = _row_tile(s, 256)

    def body(zq_ref, zk_ref, zv_ref, c_ref, s_ref, q_ref, k_ref, v_ref):
        c, sn = c_ref[...], s_ref[...]
        for hd in range(N_Q_HEADS):
            cols = slice(hd * HEAD_DIM, (hd + 1) * HEAD_DIM)
            t = zq_ref[:, cols]
            q_ref[:, cols] = (t * c + _swap_halves(t) * sn).astype(BF16)
        for hd in range(N_KV_HEADS):
            cols = slice(hd * HEAD_DIM, (hd + 1) * HEAD_DIM)
            t = zk_ref[:, cols]
            k_ref[:, cols] = (t * c + _swap_halves(t) * sn).astype(BF16)
        v_ref[...] = zv_ref[...].astype(BF16)

    tab = pl.BlockSpec((tr, HEAD_DIM), lambda i: (i, 0))
    return pl.pallas_call(
        body, name="rope_fwd", grid=(s // tr,),
        in_specs=[pl.BlockSpec((tr, ATTN_WIDTH), lambda i: (i, Q_OFF // ATTN_WIDTH)),
                  pl.BlockSpec((tr, KV_WIDTH), lambda i: (i, K_OFF // KV_WIDTH)),
                  pl.BlockSpec((tr, KV_WIDTH), lambda i: (i, V_OFF // KV_WIDTH)), tab, tab],
        out_specs=[pl.BlockSpec((tr, ATTN_WIDTH), lambda i: (i, 0)), pl.BlockSpec((tr, KV_WIDTH), lambda i: (i, 0)),
                   pl.BlockSpec((tr, KV_WIDTH), lambda i: (i, 0))],
        out_shape=[jax.ShapeDtypeStruct((s, ATTN_WIDTH), BF16), jax.ShapeDtypeStruct((s, KV_WIDTH), BF16),
                   jax.ShapeDtypeStruct((s, KV_WIDTH), BF16)],
        compiler_params=_params(1),
    )(z, z, z, cos_t, sin_t)


def _rope_bwd(dq_rot, dk_rot, dv, cos_t, sin_t):
    s = dq_rot.shape[0]
    tr = _row_tile(s, 256)

    def body(dq_ref, dk_ref, dv_ref, c_ref, s_ref, oq_ref, ok_ref, ov_ref):
        c, sn = c_ref[...], s_ref[...]
        for hd in range(N_Q_HEADS):
            cols = slice(hd * HEAD_DIM, (hd + 1) * HEAD_DIM)
            t = dq_ref[:, cols]
            oq_ref[:, cols] = (t * c + _swap_halves(t * sn)).astype(BF16)
        for hd in range(N_KV_HEADS):
            cols = slice(hd * HEAD_DIM, (hd + 1) * HEAD_DIM)
            t = dk_ref[:, cols]
            ok_ref[:, cols] = (t * c + _swap_halves(t * sn)).astype(BF16)
        ov_ref[...] = dv_ref[...].astype(BF16)

    tab = pl.BlockSpec((tr, HEAD_DIM), lambda i: (i, 0))
    wide = pl.BlockSpec((tr, ATTN_WIDTH), lambda i: (i, 0))
    narrow = pl.BlockSpec((tr, KV_WIDTH), lambda i: (i, 0))
    return pl.pallas_call(
        body, name="rope_bwd", grid=(s // tr,),
        in_specs=[wide, narrow, narrow, tab, tab],
        out_specs=[wide, narrow, narrow],
        out_shape=[jax.ShapeDtypeStruct((s, ATTN_WIDTH), BF16), jax.ShapeDtypeStruct((s, KV_WIDTH), BF16),
                   jax.ShapeDtypeStruct((s, KV_WIDTH), BF16)],
        compiler_params=_params(1),
    )(dq_rot, dk_rot, dv, cos_t, sin_t)


def _swa_band(i, s):
    return pl.multiple_of(jnp.clip((i - 1) * BLOCK, 0, s - BAND), BLOCK)


def _swa_probs(q_ref, k_ref, sink_ref, kv, start, valid):
    cols = slice(kv * HEAD_DIM, (kv + 1) * HEAD_DIM)
    kb = k_ref[pl.ds(start, BAND), cols]
    heads = [kv * Q_GROUP + g for g in range(Q_GROUP)]
    qg = jnp.concatenate([q_ref[:, hd * HEAD_DIM:(hd + 1) * HEAD_DIM] for hd in heads], axis=0)
    sc = lax.dot_general(qg, kb, (((1,), (1,)), ((), ())), preferred_element_type=F32) * ATTN_SCALE
    sc = jnp.where(valid, sc, NEG_INF)
    sk = jnp.concatenate([jnp.full((BLOCK, 1), sink_ref[hd], F32) for hd in heads], axis=0)
    mx = jnp.maximum(jnp.max(sc, axis=1, keepdims=True), sk)
    e = jnp.exp(sc - mx)
    es = jnp.exp(sk - mx)
    inv = 1.0 / (jnp.sum(e, axis=1, keepdims=True) + es)
    return qg, kb, e * inv, es * inv


def _swa_valid(i, start):
    q_pos = i * BLOCK + lax.broadcasted_iota(jnp.int32, (BLOCK, 1), 0)
    q_pos = jnp.concatenate([q_pos] * Q_GROUP, axis=0)
    k_pos = start + lax.broadcasted_iota(jnp.int32, (1, BAND), 1)
    return jnp.abs(k_pos - q_pos) <= WINDOW


def _swa_fwd(q, k, v, sink):
    s = q.shape[0]
    assert s % BLOCK == 0 and s >= BAND

    def body(sink_ref, q_ref, k_ref, v_ref, o_ref):
        i = pl.program_id(0)
        start = _swa_band(i, s)
        valid = _swa_valid(i, start)
        for kv in range(N_KV_HEADS):
            _, _, p, _ = _swa_probs(q_ref, k_ref, sink_ref, kv, start, valid)
            vb = v_ref[pl.ds(start, BAND), kv * HEAD_DIM:(kv + 1) * HEAD_DIM]
            o = jnp.dot(p.astype(BF16), vb, preferred_element_type=F32)
            for g in range(Q_GROUP):
                hd = kv * Q_GROUP + g
                o_ref[:, hd * HEAD_DIM:(hd + 1) * HEAD_DIM] = o[g * BLOCK:(g + 1) * BLOCK].astype(BF16)

    whole = pl.BlockSpec((s, KV_WIDTH), lambda i: (0, 0))
    blk = pl.BlockSpec((BLOCK, ATTN_WIDTH), lambda i: (i, 0))
    return pl.pallas_call(
        body, name="swa_fwd", grid=(s // BLOCK,),
        in_specs=[pl.BlockSpec(memory_space=pltpu.SMEM), blk, whole, whole],
        out_specs=blk,
        out_shape=jax.ShapeDtypeStruct((s, ATTN_WIDTH), BF16),
        compiler_params=_params(1),
    )(sink, q, k, v)


def _swa_bwd(q, k, v, d_out, sink):
    s = q.shape[0]

    def body(sink_ref, q_ref, k_ref, v_ref, do_ref, dq_ref, dk_ref, dv_ref, dsink_ref):
        i = pl.program_id(0)

        @pl.when(i == 0)
        def _():
            dk_ref[...] = jnp.zeros_like(dk_ref)
            dv_ref[...] = jnp.zeros_like(dv_ref)
            dsink_ref[...] = jnp.zeros_like(dsink_ref)

        start = _swa_band(i, s)
        valid = _swa_valid(i, start)
        for kv in range(N_KV_HEADS):
            cols = slice(kv * HEAD_DIM, (kv + 1) * HEAD_DIM)
            qg, kb, p, p_sink = _swa_probs(q_ref, k_ref, sink_ref, kv, start, valid)
            vb = v_ref[pl.ds(start, BAND), cols]
            heads = [kv * Q_GROUP + g for g in range(Q_GROUP)]
            dog = jnp.concatenate([do_ref[:, hd * HEAD_DIM:(hd + 1) * HEAD_DIM] for hd in heads], axis=0)
            dp = lax.dot_general(dog, vb, (((1,), (1,)), ((), ())), preferred_element_type=F32)
            delta = jnp.sum(p * dp, axis=1, keepdims=True)
            ds = (p * (dp - delta) * ATTN_SCALE).astype(BF16)
            dqg = jnp.dot(ds, kb, preferred_element_type=F32)
            dk_ref[pl.ds(start, BAND), cols] += lax.dot_general(ds, qg, (((0,), (0,)), ((), ())), preferred_element_type=F32)
            dv_ref[pl.ds(start, BAND), cols] += lax.dot_general(p.astype(BF16), dog, (((0,), (0,)), ((), ())),
                                                                 preferred_element_type=F32)
            dsk = p_sink * delta
            for g, hd in enumerate(heads):
                dq_ref[:, hd * HEAD_DIM:(hd + 1) * HEAD_DIM] = dqg[g * BLOCK:(g + 1) * BLOCK]
                tot = jnp.sum(dsk[g * BLOCK:(g + 1) * BLOCK], axis=0, keepdims=True)
                dsink_ref[hd:hd + 1, :] -= jnp.broadcast_to(tot, (1, 128))

    whole = pl.BlockSpec((s, KV_WIDTH), lambda i: (0, 0))
    blk = pl.BlockSpec((BLOCK, ATTN_WIDTH), lambda i: (i, 0))
    return pl.pallas_call(
        body, name="swa_bwd", grid=(s // BLOCK,),
        in_specs=[pl.BlockSpec(memory_space=pltpu.SMEM), blk, whole, whole, blk],
        out_specs=[blk, whole, whole, pl.BlockSpec((N_Q_HEADS, 128), lambda i: (0, 0))],
        out_shape=[jax.ShapeDtypeStruct((s, ATTN_WIDTH), F32), jax.ShapeDtypeStruct((s, KV_WIDTH), F32),
                   jax.ShapeDtypeStruct((s, KV_WIDTH), F32), jax.ShapeDtypeStruct((N_Q_HEADS, 128), F32)],
        compiler_params=_params(1),
    )(sink, q, k, v, d_out)


CONV_CHUNK = 256


def _shift_rows(t, rows, down):
    n = t.shape[0]
    rolled = pltpu.roll(t, 1 if down else n - 1, 0)
    edge = 0 if down else n - 1
    return jnp.where(rows == edge, 0.0, rolled)


def _conv_specs(s):
    def z_spec(off):
        return pl.BlockSpec((s, CONV_CHUNK), lambda j, off=off: (0, off // CONV_CHUNK + j))
    chunk = pl.BlockSpec((s, CONV_CHUNK), lambda j: (0, j))
    w_spec = pl.BlockSpec((3, CONV_CHUNK), lambda j: (0, j))
    return z_spec(CU_OFF), z_spec(CB_OFF), z_spec(CC_OFF), chunk, w_spec


def _conv_fwd(z, conv_w):
    s = z.shape[0]
    cu_spec, cb_spec, cc_spec, chunk, w_spec = _conv_specs(s)

    def body(cu_ref, cb_ref, cc_ref, w_ref, o_ref):
        rows = lax.broadcasted_iota(jnp.int32, (s, 1), 0)
        t = cc_ref[...] * cu_ref[...]
        c3 = _shift_rows(t, rows, True) * w_ref[0:1, :] + t * w_ref[1:2, :] + _shift_rows(t, rows, False) * w_ref[2:3, :]
        o_ref[...] = (cb_ref[...] * c3).astype(BF16)

    return pl.pallas_call(
        body, name="conv_fwd", grid=(CONV_WIDTH // CONV_CHUNK,),
        in_specs=[cu_spec, cb_spec, cc_spec, w_spec],
        out_specs=chunk,
        out_shape=jax.ShapeDtypeStruct((s, CONV_WIDTH), BF16),
        compiler_params=_params(1),
    )(z, z, z, conv_w)


def _conv_bwd(z, conv_w, d_co):
    s = z.shape[0]
    cu_spec, cb_spec, cc_spec, chunk, w_spec = _conv_specs(s)

    def body(cu_ref, cb_ref, cc_ref, w_ref, d_ref, dcu_ref, dcb_ref, dcc_ref, dw_ref):
        rows = lax.broadcasted_iota(jnp.int32, (s, 1), 0)
        cu, cc = cu_ref[...], cc_ref[...]
        t = cc * cu
        t_dn, t_up = _shift_rows(t, rows, True), _shift_rows(t, rows, False)
        c3 = t_dn * w_ref[0:1, :] + t * w_ref[1:2, :] + t_up * w_ref[2:3, :]
        d = d_ref[...]
        dcb_ref[...] = (d * c3).astype(BF16)
        dc3 = d * cb_ref[...]
        dw_ref[0:1, :] = jnp.sum(dc3 * t_dn, axis=0, keepdims=True)
        dw_ref[1:2, :] = jnp.sum(dc3 * t, axis=0, keepdims=True)
        dw_ref[2:3, :] = jnp.sum(dc3 * t_up, axis=0, keepdims=True)
        dt = _shift_rows(dc3, rows, False) * w_ref[0:1, :] + dc3 * w_ref[1:2, :] + _shift_rows(dc3, rows, True) * w_ref[2:3, :]
        dcc_ref[...] = (dt * cu).astype(BF16)
        dcu_ref[...] = (dt * cc).astype(BF16)

    return pl.pallas_call(
        body, name="conv_bwd", grid=(CONV_WIDTH // CONV_CHUNK,),
        in_specs=[cu_spec, cb_spec, cc_spec, w_spec, chunk],
        out_specs=[chunk, chunk, chunk, w_spec],
        out_shape=[jax.ShapeDtypeStruct((s, CONV_WIDTH), BF16)] * 3 + [jax.ShapeDtypeStruct((3, CONV_WIDTH), F32)],
        compiler_params=_params(1),
    )(z, z, z, conv_w, d_co)


GATE_CHUNK = 512


def _gate_specs(s, d, tr):
    n_chunks = d // GATE_CHUNK
    za = pl.BlockSpec((tr, GATE_CHUNK), lambda j, i: (i, GL_OFF // GATE_CHUNK + j))
    zc = pl.BlockSpec((tr, GATE_CHUNK), lambda j, i: (i, GL_OFF // GATE_CHUNK + n_chunks + j))
    ba = pl.BlockSpec((1, GATE_CHUNK), lambda j, i: (0, j))
    bc = pl.BlockSpec((1, GATE_CHUNK), lambda j, i: (0, n_chunks + j))
    tile = pl.BlockSpec((tr, GATE_CHUNK), lambda j, i: (i, j))
    return za, zc, ba, bc, tile


def _gate_fwd(z, b_gate, ya, yc):
    s, d = ya.shape
    tr = _row_tile(s, 512)
    za, zc, ba, bc, tile = _gate_specs(s, d, tr)

    def body(za_ref, zc_ref, ba_ref, bc_ref, ya_ref, yc_ref, o_ref):
        ga = jax.nn.sigmoid(za_ref[...] + ba_ref[...])
        gc = jax.nn.sigmoid(zc_ref[...] + bc_ref[...])
        o_ref[...] = (ga * ya_ref[...] + gc * yc_ref[...]).astype(BF16)

    return pl.pallas_call(
        body, name="gate_fwd", grid=(d // GATE_CHUNK, s // tr),
        in_specs=[za, zc, ba, bc, tile, tile],
        out_specs=tile,
        out_shape=jax.ShapeDtypeStruct((s, d), BF16),
        compiler_params=_params(2),
    )(z, z, b_gate, b_gate, ya, yc)


def _gate_bwd(z, b_gate, ya, yc, dmix):
    s, d = ya.shape
    tr = _row_tile(s, 512)
    za, zc, ba, bc, tile = _gate_specs(s, d, tr)
    vec = pl.BlockSpec((1, GATE_CHUNK), lambda j, i: (0, j))

    def body(za_ref, zc_ref, ba_ref, bc_ref, ya_ref, yc_ref, dm_ref, dya_ref, dyc_ref, dla_ref, dlc_ref, dba_ref, dbc_ref):
        ga = jax.nn.sigmoid(za_ref[...] + ba_ref[...])
        gc = jax.nn.sigmoid(zc_ref[...] + bc_ref[...])
        dm = dm_ref[...]
        dya_ref[...] = (dm * ga).astype(BF16)
        dyc_ref[...] = (dm * gc).astype(BF16)
        dla = dm * ya_ref[...] * ga * (1.0 - ga)
        dlc = dm * yc_ref[...] * gc * (1.0 - gc)
        dla_ref[...] = dla.astype(BF16)
        dlc_ref[...] = dlc.astype(BF16)
        pa = jnp.sum(dla, axis=0, keepdims=True)
        pc = jnp.sum(dlc, axis=0, keepdims=True)

        @pl.when(pl.program_id(1) == 0)
        def _():
            dba_ref[...] = pa
            dbc_ref[...] = pc

        @pl.when(pl.program_id(1) > 0)
        def _():
            dba_ref[...] += pa
            dbc_ref[...] += pc

    big = jax.ShapeDtypeStruct((s, d), BF16)
    small = jax.ShapeDtypeStruct((1, d), F32)
    return pl.pallas_call(
        body, name="gate_bwd", grid=(d // GATE_CHUNK, s // tr),
        in_specs=[za, zc, ba, bc, tile, tile, tile],
        out_specs=[tile, tile, tile, tile, vec, vec],
        out_shape=[big, big, big, big, small, small],
        compiler_params=_params(2),
    )(z, z, b_gate, b_gate, ya, yc, dmix)


def _cross_probs(q_ref, kv_ref, hd):
    cols = slice(hd * HEAD_DIM, (hd + 1) * HEAD_DIM)
    qh = q_ref[:, cols]
    kh = kv_ref[:, cols]
    sc = lax.dot_general(qh, kh, (((1,), (1,)), ((), ())), preferred_element_type=F32) * ATTN_SCALE
    e = jnp.exp(sc - jnp.max(sc, axis=1, keepdims=True))
    return qh, kh, e * (1.0 / jnp.sum(e, axis=1, keepdims=True))


def _cross_fwd(qc, kvc):
    s = qc.shape[0]
    n_mem = kvc.shape[0]
    tq = _row_tile(s, 256)

    def body(q_ref, kv_ref, o_ref):
        for hd in range(MEM_HEADS):
            _, _, p = _cross_probs(q_ref, kv_ref, hd)
            vh = kv_ref[:, MEM_WIDTH + hd * HEAD_DIM:MEM_WIDTH + (hd + 1) * HEAD_DIM]
            o_ref[:, hd * HEAD_DIM:(hd + 1) * HEAD_DIM] = jnp.dot(p.astype(BF16), vh, preferred_element_type=F32).astype(BF16)

    return pl.pallas_call(
        body, name="cross_fwd", grid=(s // tq,),
        in_specs=[pl.BlockSpec((tq, MEM_WIDTH), lambda i: (i, 0)), pl.BlockSpec((n_mem, 2 * MEM_WIDTH), lambda i: (0, 0))],
        out_specs=pl.BlockSpec((tq, MEM_WIDTH), lambda i: (i, 0)),
        out_shape=jax.ShapeDtypeStruct((s, MEM_WIDTH), BF16),
        compiler_params=_params(1),
    )(qc, kvc)


def _cross_bwd(qc, kvc, d_out):
    s = qc.shape[0]
    n_mem = kvc.shape[0]
    tq = _row_tile(s, 256)

    def body(q_ref, kv_ref, do_ref, dq_ref, dkv_ref):
        @pl.when(pl.program_id(0) == 0)
        def _():
            dkv_ref[...] = jnp.zeros_like(dkv_ref)

        for hd in range(MEM_HEADS):
            cols = slice(hd * HEAD_DIM, (hd + 1) * HEAD_DIM)
            vcols = slice(MEM_WIDTH + hd * HEAD_DIM, MEM_WIDTH + (hd + 1) * HEAD_DIM)
            qh, kh, p = _cross_probs(q_ref, kv_ref, hd)
            doh = do_ref[:, cols]
            dp = lax.dot_general(doh, kv_ref[:, vcols], (((1,), (1,)), ((), ())), preferred_element_type=F32)
            ds = (p * (dp - jnp.sum(p * dp, axis=1, keepdims=True)) * ATTN_SCALE).astype(BF16)
            dq_ref[:, cols] = jnp.dot(ds, kh, preferred_element_type=F32).astype(BF16)
            dkv_ref[:, cols] += lax.dot_general(ds, qh, (((0,), (0,)), ((), ())), preferred_element_type=F32)
            dkv_ref[:, vcols] += lax.dot_general(p.astype(BF16), doh, (((0,), (0,)), ((), ())), preferred_element_type=F32)

    qspec = pl.BlockSpec((tq, MEM_WIDTH), lambda i: (i, 0))
    kvspec = pl.BlockSpec((n_mem, 2 * MEM_WIDTH), lambda i: (0, 0))
    return pl.pallas_call(
        body, name="cross_bwd", grid=(s // tq,),
        in_specs=[qspec, kvspec, qspec],
        out_specs=[qspec, kvspec],
        out_shape=[jax.ShapeDtypeStruct((s, MEM_WIDTH), BF16), jax.ShapeDtypeStruct((n_mem, 2 * MEM_WIDTH), F32)],
        compiler_params=_params(1),
    )(qc, kvc, d_out)


def _swiglu_fwd(up, gate):
    return up, (gate * jax.nn.sigmoid(gate)) * up


def _swiglu_bwd(d_act, gate, up):
    sg = jax.nn.sigmoid(gate)
    silu = gate * sg
    return d_act * up * (sg * (1.0 + gate * (1.0 - sg))), d_act * silu


GATHER_GROUPS = {"in": ("w_in", "conv_w"), "mix": ("w_attn_out", "w_conv_out", "w_o"), "cross": ("w_cq", "w_ckv", "w_co"),
                 "gate": ("w_gate",), "up": ("w_up",), "down": ("w_down",)}


def _local_step(xs, mems, target, small, fetch, reduce):
    s, d = xs.shape
    w4 = {}
    cos_t, sin_t = _rope_tables(s)

    def near(group, then, after):
        return fetch.step("gather_near_" + group, [("direct", group)],
                          [("forward", group), ("pass_near", group)] + [("direct", g) for g in then], after)

    def far(group, after):
        return fetch.step("gather_far_" + group, [("forward", group)], [("pass_far", group)], after)

    def done(group, after):
        tok = fetch.step("gather_done_" + group, [("pass_near", group), ("pass_far", group)], [], after)
        w4.update(fetch.arrays(group))
        return tok

    h = _rmsnorm(xs, small["g_mix"], "norm_mix")
    tok = near("in", ["mix", "cross"], h)
    tok = far("in", tok)
    tok = done("in", tok)
    conv4 = w4["conv_w"]
    conv_w = conv4[:, :3, :].transpose(1, 0, 2).reshape(3, N_CHIPS * conv4.shape[2])
    c_in = w4["w_in"].shape[2]
    z = _matmul(h, w4["w_in"], mode="nn", tm=512, tn=c_in, tk=d, out_dtypes=[F32], name="in_proj", b_blocks=N_CHIPS, after=tok)
    tok = near("mix", ["gate"], z)
    q_rot, k_rot, v_b = _rope_fwd(z, cos_t, sin_t)
    attn = _swa_fwd(q_rot, k_rot, v_b, small["sink"])
    co = _conv_fwd(z, conv_w)
    tok = far("mix", attn)
    tok = done("mix", tok)
    tok = near("cross", ["up"], tok)
    w_o = w4["w_o"].reshape(-1, w4["w_o"].shape[-1])
    c_d = w4["w_attn_out"].shape[2]
    ya = _matmul(attn, w4["w_attn_out"], mode="nn", tm=1024, tn=c_d, tk=ATTN_WIDTH, out_dtypes=[F32], name="attn_out_proj",
                 b_blocks=N_CHIPS, after=tok)
    yc = _matmul(co, w4["w_conv_out"], mode="nn", tm=1024, tn=c_d, tk=CONV_WIDTH, out_dtypes=[F32], name="conv_out_proj",
                 b_blocks=N_CHIPS)
    mix = _gate_fwd(z, small["b_gate"], ya, yc)
    x1 = _matmul(mix, w_o, mode="nn", tm=512, tn=1024, tk=d, out_dtypes=[F32], name="mix_out_proj", extras=[xs],
                 epilogue=_add_residual)
    tok = far("cross", x1)
    tok = done("cross", tok)
    tok = near("gate", ["down"], tok)
    w_cq = w4["w_cq"].reshape(-1, w4["w_cq"].shape[-1])
    w_ckv = w4["w_ckv"].reshape(-1, w4["w_ckv"].shape[-1])
    hc = _rmsnorm(x1, small["g_cross"], "norm_cross")
    memn = _rmsnorm(mems, small["g_mem"], "norm_mem")
    qc = _matmul(hc, w_cq, mode="nn", tm=1024, tn=MEM_WIDTH, tk=d, out_dtypes=[BF16], name="cross_q_proj", after=tok)
    kvc = _matmul(memn, w_ckv, mode="nn", tm=256, tn=2 * MEM_WIDTH, tk=d, out_dtypes=[BF16], name="cross_kv_proj")
    oc = _cross_fwd(qc, kvc)
    x2 = _matmul(oc, w4["w_co"], mode="nn", tm=1024, tn=c_d, tk=MEM_WIDTH, out_dtypes=[F32], name="cross_out_proj",
                 extras=[x1], epilogue=_add_residual, b_blocks=N_CHIPS)
    hf = _rmsnorm(x2, small["g_ffn"], "norm_ffn")
    tok = far("gate", hf)
    tok = done("gate", tok)
    tok = near("up", [], tok)
    c_ff = w4["w_gate"].shape[2]
    gate = _matmul(hf, w4["w_gate"], mode="nn", tm=512, tn=c_ff, tk=d, out_dtypes=[F32], name="ffn_gate_proj", b_blocks=N_CHIPS,
                   after=tok)
    tok = far("up", gate)
    tok = done("up", tok)
    tok = near("down", [], tok)
    up, act = _matmul(hf, w4["w_up"], mode="nn", tm=512, tn=c_ff, tk=d, out_dtypes=[F32, BF16], name="ffn_up_proj",
                      extras=[gate], epilogue=_swiglu_fwd, b_blocks=N_CHIPS, after=tok)
    tok = far("down", act)
    done("down", tok)
    w_down = w4["w_down"].reshape(-1, w4["w_down"].shape[-1])
    x3 = _matmul(act, w_down, mode="nn", tm=512, tn=512, tk=w_down.shape[0], out_dtypes=[F32], name="ffn_down_proj", extras=[x2],
                 epilogue=_add_residual)
    dx3, dx3b, sq, dg_final = _loss_head(x3, small["g_final"], target)

    da, du = _matmul(dx3b, w_down, mode="nt", tm=512, tn=c_ff, tk=d, out_dtypes=[BF16, BF16], name="ffn_down_bwd",
                     extras=[gate, up], epilogue=_swiglu_bwd)
    core = reduce.core
    ffn_shape = dict(row_sharded=False, tm=512, tn=c_ff)
    g_down = _matmul(act, dx3b, mode="tn", tm=c_ff, tn=1024, tk=s, out_dtypes=[BF16], name="ffn_down_wgrad")
    tok = reduce.add("down", {"w_down": g_down}, da)
    t_gate = _wgrad_half(hf, da, core, theirs=True, name="ffn_gate_wgrad_theirs", after=tok, **ffn_shape)
    tok = reduce.step("down", t_gate)
    t_up = _wgrad_half(hf, du, core, theirs=True, name="ffn_up_wgrad_theirs", after=tok, **ffn_shape)
    tok = reduce.send("ffn", {"w_gate": t_gate, "w_up": t_up}, dx3b)
    dhf = _matmul(da, w4["w_gate"], mode="nt", tm=512, tn=1024, tk=N_CHIPS * c_ff, out_dtypes=[F32], name="ffn_gate_bwd", b_blocks=N_CHIPS,
                  after=tok)
    got = reduce.received("ffn", dhf)
    p_gate = _wgrad_half(hf, da, core, theirs=False, name="ffn_gate_wgrad_mine", add=got["w_gate"], **ffn_shape)
    p_up = _wgrad_half(hf, du, core, theirs=False, name="ffn_up_wgrad_mine", add=got["w_up"], **ffn_shape)
    tok = reduce.add_parts("ffn", {"w_gate": p_gate, "w_up": p_up})
    dhf = _matmul(du, w4["w_up"], mode="nt", tm=512, tn=1024, tk=N_CHIPS * c_ff, out_dtypes=[F32], name="ffn_up_bwd", extras=[dhf],
                  epilogue=_add_residual, b_blocks=N_CHIPS, after=tok)
    tok = reduce.step("down", dhf)
    dx2, dx2b, dg_ffn = _rmsnorm_bwd(dhf, x2, small["g_ffn"], dx3, "norm_ffn_bwd")

    d_oc = _matmul(dx2b, w4["w_co"], mode="nt", tm=1024, tn=MEM_WIDTH, tk=d, out_dtypes=[BF16], name="cross_out_bwd",
                   b_blocks=N_CHIPS, after=tok)
    g_co = _matmul(oc, dx2b, mode="tn", tm=MEM_WIDTH, tn=c_d, tk=s, out_dtypes=[BF16], name="cross_out_wgrad", out_blocks=N_CHIPS)
    tok = reduce.step("down", g_co)
    dqc, dkvc = _cross_bwd(qc, kvc, d_oc)
    g_cq = _matmul(hc, dqc, mode="tn", tm=1024, tn=MEM_WIDTH, tk=s, out_dtypes=[BF16], name="cross_q_wgrad", after=tok)
    dhc = _matmul(dqc, w_cq, mode="nt", tm=1024, tn=1024, tk=MEM_WIDTH, out_dtypes=[F32], name="cross_q_bwd")
    g_ckv = _matmul(memn, dkvc, mode="tn", tm=1024, tn=2 * MEM_WIDTH, tk=mems.shape[0], out_dtypes=[BF16], name="cross_kv_wgrad")
    dmemn = _matmul(dkvc, w_ckv, mode="nt", tm=256, tn=1024, tk=2 * MEM_WIDTH, out_dtypes=[F32], name="cross_kv_bwd")
    _, _, dg_mem = _rmsnorm_bwd(dmemn, mems, small["g_mem"], None, "norm_mem_bwd")
    dx1, dx1b, dg_cross = _rmsnorm_bwd(dhc, x1, small["g_cross"], dx2, "norm_cross_bwd")

    dmix = _matmul(dx1b, w_o, mode="nt", tm=512, tn=1024, tk=d, out_dtypes=[F32], name="mix_out_bwd")
    g_o = _matmul(mix, dx1b, mode="tn", tm=1024, tn=1024, tk=s, out_dtypes=[BF16], name="mix_out_wgrad")
    dya, dyc, dgl_a, dgl_c, db_a, db_c = _gate_bwd(z, small["b_gate"], ya, yc, dmix)
    d_attn = _matmul(dya, w4["w_attn_out"], mode="nt", tm=1024, tn=ATTN_WIDTH, tk=d, out_dtypes=[BF16], name="attn_out_bwd",
                     b_blocks=N_CHIPS)
    g_ao = _matmul(attn, dya, mode="tn", tm=ATTN_WIDTH, tn=c_d, tk=s, out_dtypes=[BF16], name="attn_out_wgrad", out_blocks=N_CHIPS)
    d_co = _matmul(dyc, w4["w_conv_out"], mode="nt", tm=1024, tn=CONV_WIDTH, tk=d, out_dtypes=[F32], name="conv_out_bwd",
                   b_blocks=N_CHIPS)
    g_cvo = _matmul(co, dyc, mode="tn", tm=CONV_WIDTH, tn=c_d, tk=s, out_dtypes=[BF16], name="conv_out_wgrad", out_blocks=N_CHIPS)
    tok = reduce.step("ffn", g_cvo)
    tok = reduce.add("mid", {"w_co": g_co, "w_cq": g_cq, "w_ckv": g_ckv, "w_o": g_o, "w_attn_out": g_ao, "w_conv_out": g_cvo}, tok)
    dcu, dcb, dcc, d_conv_w = _conv_bwd(z, conv_w, d_co)
    dq_rot, dk_rot, dv, dsink = _swa_bwd(q_rot, k_rot, v_b, d_attn, small["sink"])
    tok = reduce.step("mid", dq_rot)
    dq, dk, dvb = _rope_bwd(dq_rot, dk_rot, dv, cos_t, sin_t)
    dz = jnp.concatenate([dq, dk, dvb, dcu, dcb, dcc, dgl_a, dgl_c], axis=1)
    in_shape = dict(row_sharded=False, tm=512, tn=c_in)
    t_in = _wgrad_half(h, dz, core, theirs=True, name="in_proj_wgrad_theirs", after=tok, **in_shape)
    tok = reduce.send("in", {"w_in": t_in}, dk)
    tok = reduce.step("ffn", tok)
    tok = reduce.step("mid", tok)
    got = reduce.received("in", tok)
    p_in = _wgrad_half(h, dz, core, theirs=False, name="in_proj_wgrad_mine", add=got["w_in"], **in_shape)
    tok = reduce.add_parts("in", {"w_in": p_in})
    dh = _matmul(dz, w4["w_in"], mode="nt", tm=512, tn=512, tk=N_CHIPS * c_in, out_dtypes=[F32], name="in_proj_bwd", b_blocks=N_CHIPS,
                 after=tok)
    tok = reduce.step("mid", dh)
    grad_x, _, dg_mix = _rmsnorm_bwd(dh, xs, small["g_mix"], dx1, "norm_mix_bwd")

    small_grads = {
        "g_mix": dg_mix, "sink": dsink[:, 0], "b_gate": jnp.concatenate([db_a, db_c], axis=1), "g_cross": dg_cross,
        "g_mem": dg_mem, "g_ffn": dg_ffn, "g_final": dg_final, "conv_w": d_conv_w,
    }
    return sq, grad_x, small_grads


def _pair_sum(g4, ra, core, name):
    nb, rs, cs = g4.shape
    rh = rs // 2
    tr = _row_tile(rh, 256)
    per = rh // tr

    def body(c_ref, g_ref, r_ref, o_ref):
        o_ref[...] = (g_ref[...].astype(F32) + r_ref[...].astype(F32)).astype(BF16)

    plain = pl.BlockSpec((None, tr, cs), lambda j, i, c: (j, i, 0))
    return pl.pallas_call(
        body, name=name,
        grid_spec=pltpu.PrefetchScalarGridSpec(
            num_scalar_prefetch=1, grid=(nb, per),
            in_specs=[pl.BlockSpec((None, tr, cs), lambda j, i, c: (j, c[0] * per + i, 0)), plain],
            out_specs=plain),
        out_shape=jax.ShapeDtypeStruct((nb, rh, cs), BF16),
        compiler_params=_params(2),
    )(core, g4, ra)


def _quad_sum(parts, rc, place, name):
    _, rh, cs = parts.shape
    tr = _row_tile(rh, 256)
    per = rh // tr

    def body(p_ref, own_ref, r_ref, o_ref):
        acc = own_ref[...].astype(F32)
        for j in range(rc.shape[0]):
            acc = acc + r_ref[j].astype(F32)
        o_ref[...] = acc

    return pl.pallas_call(
        body, name=name,
        grid_spec=pltpu.PrefetchScalarGridSpec(
            num_scalar_prefetch=1, grid=(per,),
            in_specs=[pl.BlockSpec((None, tr, cs), lambda i, p: (p[0], i, 0)),
                      pl.BlockSpec((rc.shape[0], tr, cs), lambda i, p: (0, i, 0))],
            out_specs=pl.BlockSpec((tr, cs), lambda i, p: (p[1] * per + i, 0))),
        out_shape=jax.ShapeDtypeStruct((2 * rh, cs), F32),
        compiler_params=_params(1),
    )(place, parts, rc)


def _cast_to_slot(w, place, dtype, name, after=None):
    rows, cols = w.shape
    tr = _row_tile(rows, 256)

    def body(p_ref, w_ref, *rest):
        o_ref = rest[-1]
        o_ref[...] = w_ref[...].astype(dtype)

    return pl.pallas_call(
        body, name=name,
        grid_spec=pltpu.PrefetchScalarGridSpec(
            num_scalar_prefetch=1, grid=(rows // tr,),
            in_specs=[pl.BlockSpec((tr, cols), lambda i, p: (i, 0))] + ([] if after is None else [ANY]),
            out_specs=pl.BlockSpec((None, tr, cols), lambda i, p: (p[0], i, 0))),
        out_shape=jax.ShapeDtypeStruct((N_CHIPS, rows, cols), dtype),
        compiler_params=_params(1),
    )(place, w, *([] if after is None else [after]))


def _adamw(w, g, m, v, name, after=None):
    rows, cols = w.shape
    tr = _row_tile(rows, 256)

    def body(w_ref, g_ref, m_ref, v_ref, *rest):
        go_ref, d_ref, nm_ref, nv_ref = rest[-4:]
        gv = g_ref[...]
        go_ref[...] = gv
        nm = ADAM_B1 * m_ref[...] + (1.0 - ADAM_B1) * gv
        nv = ADAM_B2 * v_ref[...] + (1.0 - ADAM_B2) * (gv * gv)
        m_hat = nm / ADAM_C1
        v_hat = nv / ADAM_C2
        d_ref[...] = -ADAM_LR * (m_hat / (jnp.sqrt(v_hat) + ADAM_EPS) + ADAM_WD * w_ref[...])
        nm_ref[...] = nm
        nv_ref[...] = nv

    tile = pl.BlockSpec((tr, cols), lambda i: (i, 0))
    shape = jax.ShapeDtypeStruct((rows, cols), F32)
    return pl.pallas_call(
        body, name=name, grid=(rows // tr,),
        in_specs=[tile] * 4 + ([] if after is None else [ANY]), out_specs=[tile] * 4, out_shape=[shape] * 4,
        compiler_params=_params(1),
    )(w, g, m, v, *([] if after is None else [after]))


def _mesh_pos():
    return lax.axis_index("x"), lax.axis_index("y"), lax.axis_index("c")


def _other_chips(x, y):
    return [(1 - x, y), (x, 1 - y), (1 - x, 1 - y)]


def _half_rows(ref, which):
    rh = ref.shape[-2] // 2
    return ref.at[pl.ds(which * rh, rh), :]


def _remote(src, dst, send_sems, recv_sems, sem, to):
    return pltpu.make_async_remote_copy(src_ref=src, dst_ref=dst, send_sem=send_sems.at[sem], recv_sem=recv_sems.at[sem],
                                        device_id=to, device_id_type=MESH)


HBM = pl.BlockSpec(memory_space=pltpu.HBM)
SEM = pl.BlockSpec(memory_space=pltpu.SEMAPHORE)
DATAFLOW_EFFECT = pltpu.SideEffectType.DATAFLOW_SIDE_EFFECTING


def _in_hbm(arrays):
    return [pltpu.with_memory_space_constraint(a, pltpu.HBM) for a in arrays]


def _hbm_like(arrays):
    return [pltpu.HBM(a.shape, a.dtype) for a in arrays]


GATHER_COPIES_PER_ARRAY = {"direct": 2, "forward": 2, "pass_near": 2, "pass_far": 1}


def _gather_copies(kind, refs, x, y, c):
    me, near_x, near_y, far = 2 * x + y, 2 * (1 - x) + y, 2 * x + (1 - y), 2 * (1 - x) + (1 - y)
    to_x, to_y, sibling = (1 - x, y, c), (x, 1 - y, c), (x, y, 1 - c)
    out = []
    for ref in refs:
        rh = ref.shape[1] // 2
        rq = rh // 2

        def half(chip, ref=ref, rh=rh):
            return ref.at[chip, pl.ds(c * rh, rh), :]

        def quarter(chip, q, ref=ref, rh=rh, rq=rq):
            return ref.at[chip, pl.ds(c * rh + q * rq, rq), :]

        if kind == "direct":
            out += [(half(me), half(me), to_x), (half(me), half(me), to_y)]
        elif kind == "forward":
            out += [(quarter(near_x, 0), quarter(near_x, 0), to_y), (quarter(near_y, 1), quarter(near_y, 1), to_x)]
        elif kind == "pass_near":
            out += [(half(near_x), half(near_x), sibling), (half(near_y), half(near_y), sibling)]
        else:
            assert kind == "pass_far"
            out += [(half(far), half(far), sibling)]
    return out


def _gather_step(name, bufs, waits, starts, after):
    nb, nw, ns = len(bufs), len(waits), len(starts)
    n_after = 0 if after is None else 1

    def body(*refs):
        ins = refs[:nb]
        wait_sems = refs[nb:nb + 2 * nw]
        start_sems = refs[nb + 2 * nw + n_after:nb + 2 * nw + n_after + 2 * ns]
        token = refs[-1]
        x, y, c = _mesh_pos()
        for j, (kind, idxs, _, _) in enumerate(waits):
            for i, (s_ref, d_ref, to) in enumerate(_gather_copies(kind, [ins[t] for t in idxs], x, y, c)):
                came = _remote(s_ref, d_ref, wait_sems[2 * j], wait_sems[2 * j + 1], i, to)
                came.wait_recv()
                came.wait_send()
        for j, (kind, idxs) in enumerate(starts):
            for i, (s_ref, d_ref, to) in enumerate(_gather_copies(kind, [ins[t] for t in idxs], x, y, c)):
                _remote(s_ref, d_ref, start_sems[2 * j], start_sems[2 * j + 1], i, to).start()
        token[...] = jnp.zeros_like(token)

    sems = []
    for kind, idxs in starts:
        sems += [pltpu.SemaphoreType.DMA((GATHER_COPIES_PER_ARRAY[kind] * len(idxs),))] * 2
    operands = _in_hbm(bufs) + [sem for w in waits for sem in w[2:]] + ([] if after is None else [after])
    outs = pl.pallas_call(
        body, name=name,
        in_specs=[HBM] * nb + [SEM] * (2 * nw) + [ANY] * n_after,
        out_specs=[SEM] * (2 * ns) + [HBM] * nb + [pl.BlockSpec(memory_space=pltpu.VMEM)],
        out_shape=sems + _hbm_like(bufs) + [jax.ShapeDtypeStruct((8, 128), F32)],
        input_output_aliases={i: 2 * ns + i for i in range(nb)},
        compiler_params=pltpu.CompilerParams(has_side_effects=DATAFLOW_EFFECT),
    )(*operands)
    return outs[2 * ns:2 * ns + nb], [(outs[2 * j], outs[2 * j + 1]) for j in range(ns)], outs[-1]


class _Gather:
    def __init__(self, groups):
        self.groups = groups
        self.bufs = {}
        self.in_flight = {}

    def put(self, slotted):
        self.bufs.update(slotted)

    def step(self, name, waits, starts, after=None):
        names = []
        for _, group in list(waits) + list(starts):
            names += [n for n in self.groups[group] if n not in names]
        index = {n: i for i, n in enumerate(names)}

        def members(group):
            return [index[n] for n in self.groups[group]]

        wait_args = [(kind, members(group)) + self.in_flight.pop((kind, group)) for kind, group in waits]
        start_args = [(kind, members(group)) for kind, group in starts]
        bufs, sems, token = _gather_step(name, [self.bufs[n] for n in names], wait_args, start_args, after)
        self.bufs.update(zip(names, bufs))
        for (kind, group), pair in zip(starts, sems):
            self.in_flight[(kind, group)] = pair
        return token

    def arrays(self, group):
        return {n: self.bufs[n] for n in self.groups[group]}


def _sibling_halves_copies(srcs, dsts, x, y, c):
    out = []
    for s_ref, d_ref in zip(srcs, dsts, strict=True):
        rh = s_ref.shape[1] // 2
        out.append((s_ref.at[:, pl.ds((1 - c) * rh, rh), :], d_ref, (x, y, 1 - c)))
    return out


def _to_sibling_copies(srcs, dsts, x, y, c):
    return [(s_ref, d_ref, (x, y, 1 - c)) for s_ref, d_ref in zip(srcs, dsts, strict=True)]


def _chip_copies(srcs, dsts, x, y, c):
    out = []
    for s_ref, d_ref in zip(srcs, dsts, strict=True):
        for k, (px, py) in enumerate(_other_chips(x, y)):
            out.append((s_ref.at[2 * px + py], d_ref.at[k], (px, py, c)))
    return out


def _join_copies(srcs, dsts, x, y, c):
    out = []
    for s_ref in srcs:
        mine = _half_rows(s_ref, c)
        out.append((mine, mine, (x, y, 1 - c)))
    return out


def _exchange_start(copies_fn, n_copies, srcs, fresh, after, name):
    ns, nb = len(srcs), len(srcs) + len(fresh)

    def body(*refs):
        bufs, send, recv, token = refs[:nb], refs[nb + 1], refs[nb + 2], refs[-1]
        x, y, c = _mesh_pos()
        for i, (s_ref, d_ref, to) in enumerate(copies_fn(bufs[:ns], bufs[ns:] if fresh else bufs[:ns], x, y, c)):
            _remote(s_ref, d_ref, send, recv, i, to).start()
        token[...] = jnp.zeros_like(token)

    sems = [pltpu.SemaphoreType.DMA((n_copies,))] * 2
    outs = pl.pallas_call(
        body, name=name,
        in_specs=[HBM] * nb + [ANY], out_specs=[SEM, SEM] + [HBM] * nb + [pl.BlockSpec(memory_space=pltpu.VMEM)],
        out_shape=sems + _hbm_like(list(srcs) + list(fresh)) + [jax.ShapeDtypeStruct((8, 128), F32)],
        input_output_aliases={i: 2 + i for i in range(nb)},
        compiler_params=pltpu.CompilerParams(has_side_effects=DATAFLOW_EFFECT),
    )(*_in_hbm(list(srcs) + list(fresh)), after)
    return outs[0], outs[1], outs[2:2 + ns], outs[2 + ns:2 + nb], outs[-1]


def _exchange_done(copies_fn, srcs, fresh, send, recv, after, name):
    ns, nb = len(srcs), len(srcs) + len(fresh)

    def body(*refs):
        bufs, send_in, recv_in = refs[:nb], refs[nb], refs[nb + 1]
        x, y, c = _mesh_pos()
        for i, (s_ref, d_ref, to) in enumerate(copies_fn(bufs[:ns], bufs[ns:] if fresh else bufs[:ns], x, y, c)):
            came = _remote(s_ref, d_ref, send_in, recv_in, i, to)
            came.wait_send()
            came.wait_recv()

    outs = pl.pallas_call(
        body, name=name,
        in_specs=[HBM] * nb + [SEM, SEM, ANY], out_specs=[HBM] * nb,
        out_shape=_hbm_like(list(srcs) + list(fresh)),
        input_output_aliases={i: i for i in range(nb)},
        compiler_params=pltpu.CompilerParams(has_side_effects=DATAFLOW_EFFECT),
    )(*_in_hbm(list(srcs) + list(fresh)), send, recv, after)
    return outs[:ns], outs[ns:]


class _Reduce:
    def __init__(self, place, core, shards, mom_m, mom_v):
        self.place, self.core = place, core
        self.shards, self.mom_m, self.mom_v = shards, mom_m, mom_v
        self.state = {}
        self.results = {}

    def add(self, group, grads, after):
        names = list(grads)
        g4s = [g.reshape((N_CHIPS, -1, g.shape[-1])) if g.ndim == 2 else g for g in grads.values()]
        fresh = [lax.empty((N_CHIPS, g.shape[1] // 2, g.shape[2]), BF16) for g in g4s]
        send, recv, g4s, fresh, token = _exchange_start(_sibling_halves_copies, len(names), g4s, fresh, after,
                                                        "pair_start_" + group)
        self.state[group] = (0, names, send, recv, g4s, fresh)
        return token

    def send(self, group, theirs, after):
        names, srcs = list(theirs), list(theirs.values())
        fresh = [lax.empty(s.shape, BF16) for s in srcs]
        send, recv, srcs, fresh, token = _exchange_start(_to_sibling_copies, len(names), srcs, fresh, after, "pair_start_" + group)
        self.state[group] = ("sent", names, send, recv, srcs, fresh)
        return token

    def received(self, group, after):
        stage, names, send, recv, srcs, fresh = self.state.pop(group)
        assert stage == "sent"
        _, got = _exchange_done(_to_sibling_copies, srcs, fresh, send, recv, after, "pair_done_" + group)
        return dict(zip(names, got))

    def add_parts(self, group, parts):
        names, srcs = list(parts), list(parts.values())
        fresh = [lax.empty((N_CHIPS - 1,) + p.shape[1:], BF16) for p in srcs]
        send, recv, srcs, fresh, token = _exchange_start(_chip_copies, 3 * len(names), srcs, fresh, self.core, "chips_start_" + group)
        self.state[group] = (1, names, send, recv, srcs, fresh)
        return token

    def step(self, group, after):
        stage, names, send, recv, srcs, fresh = self.state[group]
        if stage == 0:
            g4s, ras = _exchange_done(_sibling_halves_copies, srcs, fresh, send, recv, after, "pair_done_" + group)
            parts = [_pair_sum(g, r, self.core, "pair_sum_" + n) for g, r, n in zip(g4s, ras, names)]
            fresh = [lax.empty((N_CHIPS - 1,) + p.shape[1:], BF16) for p in parts]
            send, recv, parts, fresh, token = _exchange_start(_chip_copies, 3 * len(names), parts, fresh, self.core,
                                                              "chips_start_" + group)
            self.state[group] = (1, names, send, recv, parts, fresh)
            return token
        if stage == 1:
            parts, rcs = _exchange_done(_chip_copies, srcs, fresh, send, recv, after, "chips_done_" + group)
            wholes = [_quad_sum(p, r, self.place, "quad_sum_" + n) for p, r, n in zip(parts, rcs, names)]
            send, recv, wholes, _, token = _exchange_start(_join_copies, len(names), wholes, [], self.core, "join_start_" + group)
            self.state[group] = (2, names, send, recv, wholes, [])
            return token
        assert stage == 2
        wholes, _ = _exchange_done(_join_copies, srcs, [], send, recv, after, "join_done_" + group)
        token = None
        for n, g in zip(names, wholes):
            self.results[n] = _adamw(self.shards[n], g, self.mom_m[n], self.mom_v[n], "adamw_" + n, after=token)
            token = self.results[n][1]
        del self.state[group]
        return token


N_DEV = 8


def _all_reduce_small(v):
    def body(v_ref, o_ref, slots, send_sems, recv_sems):
        x, y, c = _mesh_pos()
        me = 4 * x + 2 * y + c
        slots[me] = v_ref[...]
        peers = []
        for r in range(1, N_DEV):
            fx, fy, fc = (r >> 2) & 1, (r >> 1) & 1, r & 1
            peers.append((x + fx - 2 * x * fx, y + fy - 2 * y * fy, c + fc - 2 * c * fc))
        sends = []
        for r, peer in enumerate(peers):
            cp = _remote(v_ref, slots.at[me], send_sems, recv_sems, r, peer)
            cp.start()
            sends.append(cp)
        for r, (px, py, pc) in enumerate(peers):
            landed = slots.at[4 * px + 2 * py + pc]
            _remote(landed, landed, send_sems, recv_sems, r, (px, py, pc)).wait_recv()
        for cp in sends:
            cp.wait_send()
        acc = slots[0]
        for i in range(1, N_DEV):
            acc = acc + slots[i]
        o_ref[...] = acc

    vm = pl.BlockSpec(memory_space=pltpu.VMEM)
    return pl.pallas_call(
        body, name="small_grads_all_reduce",
        in_specs=[vm], out_specs=vm,
        out_shape=jax.ShapeDtypeStruct(v.shape, v.dtype),
        scratch_shapes=[pltpu.VMEM((N_DEV,) + v.shape, v.dtype), pltpu.SemaphoreType.DMA((N_DEV - 1,)),
                        pltpu.SemaphoreType.DMA((N_DEV - 1,))],
    )(v)


MATRICES = ("w_in", "w_attn_out", "w_conv_out", "w_o", "w_cq", "w_ckv", "w_co", "w_gate", "w_up", "w_down")
VECTORS = ("g_mix", "b_gate", "g_cross", "g_mem", "g_ffn", "g_final", "conv_w", "sink")
WEIGHT_ORDER = ("g_mix", "w_in", "sink", "conv_w", "b_gate", "w_attn_out", "w_conv_out", "w_o", "g_cross", "g_mem", "w_cq",
                "w_ckv", "w_co", "g_ffn", "w_gate", "w_up", "w_down", "g_final")
CONV_PAD_ROWS = 32
SMALL_ROWS = 8


def _pack(pieces):
    flat = jnp.concatenate([p.reshape(-1) for p in pieces])
    lane_group = SMALL_ROWS * 128
    total = -(-flat.shape[0] // lane_group) * lane_group
    flat = jnp.pad(flat, (0, total - flat.shape[0]))
    return flat.reshape(SMALL_ROWS, total // SMALL_ROWS), [p.size for p in pieces]


def _unpack(packed, pieces):
    flat = packed.reshape(-1)
    out, off = [], 0
    for p in pieces:
        out.append(flat[off:off + p.size].reshape(p.shape))
        off += p.size
    return out


def kernel(x, mem, g_mix, w_in, sink, conv_w, b_gate, w_attn_out, w_conv_out, w_o, g_cross, g_mem, w_cq, w_ckv, w_co, g_ffn, w_gate, w_up, w_down, g_final, loss_target, m_g_mix, m_w_in, m_sink, m_conv_w, m_b_gate, m_w_attn_out, m_w_conv_out, m_w_o, m_g_cross, m_g_mem, m_w_cq, m_w_ckv, m_w_co, m_g_ffn, m_w_gate, m_w_up, m_w_down, m_g_final, v_g_mix, v_w_in, v_sink, v_conv_w, v_b_gate, v_w_attn_out, v_w_conv_out, v_w_o, v_g_cross, v_g_mem, v_w_cq, v_w_ckv, v_w_co, v_g_ffn, v_w_gate, v_w_up, v_w_down, v_g_final):
    given = dict(g_mix=g_mix, w_in=w_in, sink=sink, conv_w=conv_w, b_gate=b_gate, w_attn_out=w_attn_out, w_conv_out=w_conv_out,
                 w_o=w_o, g_cross=g_cross, g_mem=g_mem, w_cq=w_cq, w_ckv=w_ckv, w_co=w_co, g_ffn=g_ffn, w_gate=w_gate, w_up=w_up,
                 w_down=w_down, g_final=g_final)
    mom_m = dict(g_mix=m_g_mix, w_in=m_w_in, sink=m_sink, conv_w=m_conv_w, b_gate=m_b_gate, w_attn_out=m_w_attn_out,
                 w_conv_out=m_w_conv_out, w_o=m_w_o, g_cross=m_g_cross, g_mem=m_g_mem, w_cq=m_w_cq, w_ckv=m_w_ckv, w_co=m_w_co,
                 g_ffn=m_g_ffn, w_gate=m_w_gate, w_up=m_w_up, w_down=m_w_down, g_final=m_g_final)
    mom_v = dict(g_mix=v_g_mix, w_in=v_w_in, sink=v_sink, conv_w=v_conv_w, b_gate=v_b_gate, w_attn_out=v_w_attn_out,
                 w_conv_out=v_w_conv_out, w_o=v_w_o, g_cross=v_g_cross, g_mem=v_g_mem, w_cq=v_w_cq, w_ckv=v_w_ckv, w_co=v_w_co,
                 g_ffn=v_g_ffn, w_gate=v_w_gate, w_up=v_w_up, w_down=v_w_down, g_final=v_g_final)
    xs, mems, target = x[0], mem[0], loss_target[0]
    d_model = xs.shape[1]
    chip = 2 * lax.axis_index("x") + lax.axis_index("y")
    core = jnp.reshape(lax.axis_index("c"), (1,)).astype(jnp.int32)
    place = jnp.stack([chip, lax.axis_index("c")]).astype(jnp.int32)

    shards = {n: given[n][0] for n in MATRICES}
    conv_cols = conv_w.shape[2]
    conv_pad = jnp.pad(conv_w[0], ((0, CONV_PAD_ROWS - conv_w.shape[1]), (0, 0)))
    fetch = _Gather(GATHER_GROUPS)
    first = {"w_in": _cast_to_slot(shards["w_in"], place, BF16, "to_slot_w_in"),
             "conv_w": _cast_to_slot(conv_pad, place, F32, "to_slot_conv_w")}
    fetch.put(first)
    tok = fetch.step("gather_start", [], [("direct", "in")])
    fetch.put({n: _cast_to_slot(shards[n], place, BF16, "to_slot_" + n, after=tok) for n in MATRICES if n != "w_in"})
    small = {n: given[n] for n in ("g_mix", "b_gate", "g_cross", "g_mem", "g_ffn")}
    small["g_final"] = g_final[None]
    small["sink"] = sink[0]

    reduce = _Reduce(place, core, shards, {n: mom_m[n][0] for n in MATRICES}, {n: mom_v[n][0] for n in MATRICES})
    sq, grad_x, small_grads = _local_step(xs, mems, target, small, fetch, reduce)

    loss_part = 0.5 * sq[0:1, 0:1] / d_model
    pieces = [small_grads[n] for n in VECTORS] + [loss_part]
    packed, _ = _pack(pieces)
    summed = _unpack(_all_reduce_small(packed), pieces)
    loss = summed[-1][0, 0]
    small_sum = dict(zip(VECTORS, summed[:-1]))
    small_sum["conv_w"] = lax.dynamic_slice_in_dim(small_sum["conv_w"], chip * conv_cols, conv_cols, axis=1)

    grad_out, delta, new_m, new_v = {}, {}, {}, {}
    like = [given[n] for n in VECTORS]
    pw, _ = _pack(like)
    pg, _ = _pack([small_sum[n] for n in VECTORS])
    pm, _ = _pack([mom_m[n] for n in VECTORS])
    pv, _ = _pack([mom_v[n] for n in VECTORS])
    _, pd, pnm, pnv = _adamw(pw, pg, pm, pv, "adamw_small")
    for n, g, d, nm, nv in zip(VECTORS, [small_sum[n] for n in VECTORS], _unpack(pd, like), _unpack(pnm, like), _unpack(pnv, like)):
        grad_out[n] = g.reshape(given[n].shape)
        delta[n], new_m[n], new_v[n] = d, nm, nv
    tok = reduce.step("in", pd)
    reduce.step("in", tok)
    for n in MATRICES:
        g, d, nm, nv = reduce.results[n]
        grad_out[n], delta[n], new_m[n], new_v[n] = g[None], d[None], nm[None], nv[None]

    return (loss, grad_x[None], *[grad_out[n] for n in WEIGHT_ORDER], *[delta[n] for n in WEIGHT_ORDER],
            *[new_m[n] for n in WEIGHT_ORDER], *[new_v[n] for n in WEIGHT_ORDER])
```

```python
import functools

import jax
import jax.numpy as jnp
from jax import lax
from jax.experimental import pallas as pl
from jax.experimental.pallas import tpu as pltpu

F32 = jnp.float32
BF16 = jnp.bfloat16
MESH = pl.DeviceIdType.MESH
ANY = pl.BlockSpec(memory_space=pl.ANY)

VMEM_LIMIT_BYTES = 56 * 1024 * 1024

N_CHIPS = 4
HEAD_DIM = 128
N_Q_HEADS = 8
N_KV_HEADS = 2
Q_GROUP = N_Q_HEADS // N_KV_HEADS
ATTN_WIDTH = N_Q_HEADS * HEAD_DIM
KV_WIDTH = N_KV_HEADS * HEAD_DIM
WINDOW = 128
BLOCK = 128
BAND = 3 * BLOCK
ROPE_THETA = 10000.0
CONV_WIDTH = 1024
MEM_HEADS = 4
MEM_WIDTH = MEM_HEADS * HEAD_DIM
RMS_EPS = 1e-6
NEG_INF = -1e30
ATTN_SCALE = HEAD_DIM ** -0.5

Q_OFF, K_OFF, V_OFF, CU_OFF, CB_OFF, CC_OFF, GL_OFF = 0, 1024, 1280, 1536, 2560, 3584, 4608

ADAM_LR = 0.001
ADAM_B1 = 0.9
ADAM_B2 = 0.999
ADAM_EPS = 1e-08
ADAM_WD = 0.01
ADAM_STEP = 10
ADAM_C1 = 1.0 - ADAM_B1 ** ADAM_STEP
ADAM_C2 = 1.0 - ADAM_B2 ** ADAM_STEP


def _params(n_grid_axes):
    return pltpu.CompilerParams(dimension_semantics=("arbitrary",) * n_grid_axes, vmem_limit_bytes=VMEM_LIMIT_BYTES)


BF16_SUBLANES = 16


def _row_tile(rows, want):
    if rows <= want:
        return rows
    for t in range(want, 0, -BF16_SUBLANES):
        if rows % t == 0:
            return t
    return rows


def _matmul(a, b, *, mode, tm, tn, tk, out_dtypes, name, extras=(), epilogue=None, b_blocks=1, out_blocks=1, after=None):
    if mode == "tn":
        kdim, m = a.shape
    else:
        m, kdim = a.shape
    if b_blocks > 1:
        nb, brows, bcols = b.shape
        assert nb == b_blocks
        if mode == "nn":
            n = bcols * nb
            assert brows == kdim
        else:
            assert mode == "nt" and bcols * nb == kdim
            n = brows
    else:
        n = b.shape[0] if mode == "nt" else b.shape[1]
    tm, tn = min(tm, m), min(tn, n)
    assert m % tm == 0 and n % tn == 0 and tk == kdim, (name, m, n, kdim, tm, tn, tk)
    n_extra, n_out = len(extras), len(out_dtypes)
    n_after = 0 if after is None else 1

    if mode == "tn":
        a_spec = pl.BlockSpec((tk, tm), lambda j, i, k: (k, i))
        dims = (((0,), (0,)), ((), ()))
    else:
        a_spec = pl.BlockSpec((tm, tk), lambda j, i, k: (i, k))
        dims = (((1,), (0,)), ((), ())) if mode == "nn" else (((1,), (1,)), ((), ()))

    if b_blocks > 1 and mode == "nn":
        per = b.shape[2] // tn
        assert b.shape[2] % tn == 0
        b_spec = pl.BlockSpec((None, tk, tn), lambda j, i, k: (j // per, k, j % per))
    elif b_blocks > 1:
        b_spec = pl.BlockSpec((b_blocks, tn, b.shape[2]), lambda j, i, k: (0, j, 0))
    elif mode == "nt":
        b_spec = pl.BlockSpec((tn, tk), lambda j, i, k: (j, k))
    else:
        b_spec = pl.BlockSpec((tk, tn), lambda j, i, k: (k, j))

    tile_spec = pl.BlockSpec((tm, tn), lambda j, i, k: (i, j))
    if out_blocks > 1:
        ncols = n // out_blocks
        assert ncols % tn == 0
        oper = ncols // tn
        out_spec = pl.BlockSpec((None, tm, tn), lambda j, i, k: (j // oper, i, j % oper))
        out_shape = [jax.ShapeDtypeStruct((out_blocks, m, ncols), dt) for dt in out_dtypes]
    else:
        out_spec = tile_spec
        out_shape = [jax.ShapeDtypeStruct((m, n), dt) for dt in out_dtypes]

    def body(a_ref, b_ref, *rest):
        extra_refs = rest[:n_extra]
        out_refs = rest[n_extra + n_after:n_extra + n_after + n_out]
        if mode == "nt" and b_blocks > 1:
            cs = b.shape[2]
            acc = None
            for jb in range(b_blocks):
                prod = lax.dot_general(a_ref[:, jb * cs:(jb + 1) * cs].astype(BF16), b_ref[jb].astype(BF16), dims,
                                       preferred_element_type=F32)
                acc = prod if acc is None else acc + prod
        else:
            acc = lax.dot_general(a_ref[...].astype(BF16), b_ref[...].astype(BF16), dims, preferred_element_type=F32)
        tiles = (acc,) if epilogue is None else epilogue(acc, *[r[...] for r in extra_refs])
        for o_ref, t in zip(out_refs, tiles, strict=True):
            o_ref[...] = t.astype(o_ref.dtype)

    outs = pl.pallas_call(
        body,
        name=name,
        grid=(n // tn, m // tm, 1),
        in_specs=[a_spec, b_spec] + [tile_spec] * n_extra + [ANY] * n_after,
        out_specs=[out_spec] * n_out,
        out_shape=out_shape,
        compiler_params=_params(3),
    )(a, b, *extras, *([] if after is None else [after]))
    return outs[0] if n_out == 1 else outs


def _add_residual(acc, res):
    return (acc + res,)


def _wgrad_half(a, b, core, *, theirs, row_sharded, tm, tn, name, add=None, after=None):
    kdim, m = a.shape
    n = b.shape[1]
    rs, cs = (m // N_CHIPS, n) if row_sharded else (m, n // N_CHIPS)
    rh = rs // 2
    tm, tn = min(tm, rh), min(tn, cs)
    assert rh % tm == 0 and cs % tn == 0, (name, rh, cs, tm, tn)
    mh, per = rh // tm, cs // tn
    has_add = add is not None

    def half(c):
        return 1 - c[0] if theirs else c[0]

    if row_sharded:
        grid = (n // tn, N_CHIPS * mh)
        a_spec = pl.BlockSpec((kdim, tm), lambda j, r, c: (0, ((r // mh) * 2 + half(c)) * mh + r % mh))
        o_spec = pl.BlockSpec((None, tm, tn), lambda j, r, c: (r // mh, r % mh, j))
    else:
        grid = (n // tn, mh)
        a_spec = pl.BlockSpec((kdim, tm), lambda j, r, c: (0, half(c) * mh + r))
        o_spec = pl.BlockSpec((None, tm, tn), lambda j, r, c: (j // per, r, j % per))
    b_spec = pl.BlockSpec((kdim, tn), lambda j, r, c: (0, j))

    def body(c_ref, a_ref, b_ref, *rest):
        o_ref = rest[-1]
        acc = lax.dot_general(a_ref[...].astype(BF16), b_ref[...].astype(BF16), (((0,), (0,)), ((), ())),
                              preferred_element_type=F32)
        if has_add:
            acc = acc + rest[0][...].astype(F32)
        o_ref[...] = acc.astype(BF16)

    operands = [a, b] + ([add] if has_add else []) + ([] if after is None else [after])
    return pl.pallas_call(
        body, name=name,
        grid_spec=pltpu.PrefetchScalarGridSpec(
            num_scalar_prefetch=1, grid=grid,
            in_specs=[a_spec, b_spec] + ([o_spec] if has_add else []) + ([] if after is None else [ANY]),
            out_specs=o_spec),
        out_shape=jax.ShapeDtypeStruct((N_CHIPS, rh, cs), BF16),
        compiler_params=_params(2),
    )(core, *operands)


def _rstd(x):
    return lax.rsqrt(jnp.mean(x * x, axis=-1, keepdims=True) + RMS_EPS)


def _rmsnorm(x, g, name):
    s, d = x.shape
    tr = _row_tile(s, 256)

    def body(x_ref, g_ref, o_ref):
        xv = x_ref[...]
        o_ref[...] = (xv * _rstd(xv) * g_ref[...]).astype(BF16)

    return pl.pallas_call(
        body, name=name, grid=(s // tr,),
        in_specs=[pl.BlockSpec((tr, d), lambda i: (i, 0)), pl.BlockSpec((1, d), lambda i: (0, 0))],
        out_specs=pl.BlockSpec((tr, d), lambda i: (i, 0)),
        out_shape=jax.ShapeDtypeStruct((s, d), BF16),
        compiler_params=_params(1),
    )(x, g)


def _rmsnorm_bwd(dh, x, g, dres, name):
    s, d = x.shape
    tr = _row_tile(s, 256)
    has_res = dres is not None

    def body(*refs):
        if has_res:
            dh_ref, x_ref, g_ref, res_ref, dx_ref, dxb_ref, dg_ref = refs
        else:
            dh_ref, x_ref, g_ref, dx_ref, dxb_ref, dg_ref = refs
        xv = x_ref[...]
        dhv = dh_ref[...].astype(F32)
        r = _rstd(xv)
        xn = xv * r
        dhg = dhv * g_ref[...]
        dx = r * (dhg - xn * jnp.mean(dhg * xn, axis=-1, keepdims=True))
        if has_res:
            dx = dx + res_ref[...]
        dx_ref[...] = dx
        dxb_ref[...] = dx.astype(BF16)
        part = jnp.sum(dhv * xn, axis=0, keepdims=True)

        @pl.when(pl.program_id(0) == 0)
        def _():
            dg_ref[...] = part

        @pl.when(pl.program_id(0) > 0)
        def _():
            dg_ref[...] += part

    row = pl.BlockSpec((tr, d), lambda i: (i, 0))
    vec = pl.BlockSpec((1, d), lambda i: (0, 0))
    return pl.pallas_call(
        body, name=name, grid=(s // tr,),
        in_specs=[row, row, vec] + ([row] if has_res else []),
        out_specs=[row, row, vec],
        out_shape=[jax.ShapeDtypeStruct((s, d), F32), jax.ShapeDtypeStruct((s, d), BF16), jax.ShapeDtypeStruct((1, d), F32)],
        compiler_params=_params(1),
    )(*([dh, x, g] + ([dres] if has_res else [])))


def _loss_head(x3, g, target):
    s, d = x3.shape
    tr = _row_tile(s, 256)

    def body(x_ref, g_ref, t_ref, dx_ref, dxb_ref, sq_ref, dg_ref):
        xv = x_ref[...]
        gv = g_ref[...]
        r = _rstd(xv)
        xn = xv * r
        err = xn * gv - t_ref[...]
        dy = err * (1.0 / d)
        dyg = dy * gv
        dx = r * (dyg - xn * jnp.mean(dyg * xn, axis=-1, keepdims=True))
        dx_ref[...] = dx
        dxb_ref[...] = dx.astype(BF16)
        sq = jnp.sum(jnp.sum(err * err, axis=1, keepdims=True), axis=0, keepdims=True)
        sq = jnp.broadcast_to(sq, (1, 128))
        part = jnp.sum(dy * xn, axis=0, keepdims=True)

        @pl.when(pl.program_id(0) == 0)
        def _():
            sq_ref[...] = sq
            dg_ref[...] = part

        @pl.when(pl.program_id(0) > 0)
        def _():
            sq_ref[...] += sq
            dg_ref[...] += part

    row = pl.BlockSpec((tr, d), lambda i: (i, 0))
    vec = pl.BlockSpec((1, d), lambda i: (0, 0))
    return pl.pallas_call(
        body, name="loss_head", grid=(s // tr,),
        in_specs=[row, vec, row],
        out_specs=[row, row, pl.BlockSpec((1, 128), lambda i: (0, 0)), vec],
        out_shape=[jax.ShapeDtypeStruct((s, d), F32), jax.ShapeDtypeStruct((s, d), BF16),
                   jax.ShapeDtypeStruct((1, 128), F32), jax.ShapeDtypeStruct((1, d), F32)],
        compiler_params=_params(1),
    )(x3, g, target)


def _rope_tables(s):
    inv = 1.0 / (ROPE_THETA ** (jnp.arange(0, HEAD_DIM, 2, dtype=F32) / HEAD_DIM))
    ang = jnp.arange(s, dtype=F32)[:, None] * inv[None, :]
    cos, sin = jnp.cos(ang), jnp.sin(ang)
    return jnp.concatenate([cos, cos], axis=1), jnp.concatenate([-sin, sin], axis=1)


def _swap_halves(t):
    return pltpu.roll(t, HEAD_DIM // 2, 1)


def _rope_fwd(z, cos_t, sin_t):
    s = z.shape[0]
    tr = _row_tile(s, 256)

    def body(zq_ref, zk_ref, zv_ref, c_ref, s_ref, q_ref, k_ref, v_ref):
        c, sn = c_ref[...], s_ref[...]
        for hd in range(N_Q_HEADS):
            cols = slice(hd * HEAD_DIM, (hd + 1) * HEAD_DIM)
            t = zq_ref[:, cols]
            q_ref[:, cols] = (t * c + _swap_halves(t) * sn).astype(BF16)
        for hd in range(N_KV_HEADS):
            cols = slice(hd * HEAD_DIM, (hd + 1) * HEAD_DIM)
            t = zk_ref[:, cols]
            k_ref[:, cols] = (t * c + _swap_halves(t) * sn).astype(BF16)
        v_ref[...] = zv_ref[...].astype(BF16)

    tab = pl.BlockSpec((tr, HEAD_DIM), lambda i: (i, 0))
    return pl.pallas_call(
        body, name="rope_fwd", grid=(s // tr,),
        in_specs=[pl.BlockSpec((tr, ATTN_WIDTH), lambda i: (i, Q_OFF // ATTN_WIDTH)),
                  pl.BlockSpec((tr, KV_WIDTH), lambda i: (i, K_OFF // KV_WIDTH)),
                  pl.BlockSpec((tr, KV_WIDTH), lambda i: (i, V_OFF // KV_WIDTH)), tab, tab],
        out_specs=[pl.BlockSpec((tr, ATTN_WIDTH), lambda i: (i, 0)), pl.BlockSpec((tr, KV_WIDTH), lambda i: (i, 0)),
                   pl.BlockSpec((tr, KV_WIDTH), lambda i: (i, 0))],
        out_shape=[jax.ShapeDtypeStruct((s, ATTN_WIDTH), BF16), jax.ShapeDtypeStruct((s, KV_WIDTH), BF16),
                   jax.ShapeDtypeStruct((s, KV_WIDTH), BF16)],
        compiler_params=_params(1),
    )(z, z, z, cos_t, sin_t)


def _rope_bwd(dq_rot, dk_rot, dv, cos_t, sin_t):
    s = dq_rot.shape[0]
    tr = _row_tile(s, 256)

    def body(dq_ref, dk_ref, dv_ref, c_ref, s_ref, oq_ref, ok_ref, ov_ref):
        c, sn = c_ref[...], s_ref[...]
        for hd in range(N_Q_HEADS):
            cols = slice(hd * HEAD_DIM, (hd + 1) * HEAD_DIM)
            t = dq_ref[:, cols]
            oq_ref[:, cols] = (t * c + _swap_halves(t * sn)).astype(BF16)
        for hd in range(N_KV_HEADS):
            cols = slice(hd * HEAD_DIM, (hd + 1) * HEAD_DIM)
            t = dk_ref[:, cols]
            ok_ref[:, cols] = (t * c + _swap_halves(t * sn)).astype(BF16)
        ov_ref[...] = dv_ref[...].astype(BF16)

    tab = pl.BlockSpec((tr, HEAD_DIM), lambda i: (i, 0))
    wide = pl.BlockSpec((tr, ATTN_WIDTH), lambda i: (i, 0))
    narrow = pl.BlockSpec((tr, KV_WIDTH), lambda i: (i, 0))
    return pl.pallas_call(
        body, name="rope_bwd", grid=(s // tr,),
        in_specs=[wide, narrow, narrow, tab, tab],
        out_specs=[wide, narrow, narrow],
        out_shape=[jax.ShapeDtypeStruct((s, ATTN_WIDTH), BF16), jax.ShapeDtypeStruct((s, KV_WIDTH), BF16),
                   jax.ShapeDtypeStruct((s, KV_WIDTH), BF16)],
        compiler_params=_params(1),
    )(dq_rot, dk_rot, dv, cos_t, sin_t)


def _swa_band(i, s):
    return pl.multiple_of(jnp.clip((i - 1) * BLOCK, 0, s - BAND), BLOCK)


def _swa_probs(q_ref, k_ref, sink_ref, kv, start, valid):
    cols = slice(kv * HEAD_DIM, (kv + 1) * HEAD_DIM)
    kb = k_ref[pl.ds(start, BAND), cols]
    heads = [kv * Q_GROUP + g for g in range(Q_GROUP)]
    qg = jnp.concatenate([q_ref[:, hd * HEAD_DIM:(hd + 1) * HEAD_DIM] for hd in heads], axis=0)
    sc = lax.dot_general(qg, kb, (((1,), (1,)), ((), ())), preferred_element_type=F32) * ATTN_SCALE
    sc = jnp.where(valid, sc, NEG_INF)
    sk = jnp.concatenate([jnp.full((BLOCK, 1), sink_ref[hd], F32) for hd in heads], axis=0)
    mx = jnp.maximum(jnp.max(sc, axis=1, keepdims=True), sk)
    e = jnp.exp(sc - mx)
    es = jnp.exp(sk - mx)
    inv = 1.0 / (jnp.sum(e, axis=1, keepdims=True) + es)
    return qg, kb, e * inv, es * inv


def _swa_valid(i, start):
    q_pos = i * BLOCK + lax.broadcasted_iota(jnp.int32, (BLOCK, 1), 0)
    q_pos = jnp.concatenate([q_pos] * Q_GROUP, axis=0)
    k_pos = start + lax.broadcasted_iota(jnp.int32, (1, BAND), 1)
    return jnp.abs(k_pos - q_pos) <= WINDOW


def _swa_fwd(q, k, v, sink):
    s = q.shape[0]
    assert s % BLOCK == 0 and s >= BAND

    def body(sink_ref, q_ref, k_ref, v_ref, o_ref):
        i = pl.program_id(0)
        start = _swa_band(i, s)
        valid = _swa_valid(i, start)
        for kv in range(N_KV_HEADS):
            _, _, p, _ = _swa_probs(q_ref, k_ref, sink_ref, kv, start, valid)
            vb = v_ref[pl.ds(start, BAND), kv * HEAD_DIM:(kv + 1) * HEAD_DIM]
            o = jnp.dot(p.astype(BF16), vb, preferred_element_type=F32)
            for g in range(Q_GROUP):
                hd = kv * Q_GROUP + g
                o_ref[:, hd * HEAD_DIM:(hd + 1) * HEAD_DIM] = o[g * BLOCK:(g + 1) * BLOCK].astype(BF16)

    whole = pl.BlockSpec((s, KV_WIDTH), lambda i: (0, 0))
    blk = pl.BlockSpec((BLOCK, ATTN_WIDTH), lambda i: (i, 0))
    return pl.pallas_call(
        body, name="swa_fwd", grid=(s // BLOCK,),
        in_specs=[pl.BlockSpec(memory_space=pltpu.SMEM), blk, whole, whole],
        out_specs=blk,
        out_shape=jax.ShapeDtypeStruct((s, ATTN_WIDTH), BF16),
        compiler_params=_params(1),
    )(sink, q, k, v)


def _swa_bwd(q, k, v, d_out, sink):
    s = q.shape[0]

    def body(sink_ref, q_ref, k_ref, v_ref, do_ref, dq_ref, dk_ref, dv_ref, dsink_ref):
        i = pl.program_id(0)

        @pl.when(i == 0)
        def _():
            dk_ref[...] = jnp.zeros_like(dk_ref)
            dv_ref[...] = jnp.zeros_like(dv_ref)
            dsink_ref[...] = jnp.zeros_like(dsink_ref)

        start = _swa_band(i, s)
        valid = _swa_valid(i, start)
        for kv in range(N_KV_HEADS):
            cols = slice(kv * HEAD_DIM, (kv + 1) * HEAD_DIM)
            qg, kb, p, p_sink = _swa_probs(q_ref, k_ref, sink_ref, kv, start, valid)
            vb = v_ref[pl.ds(start, BAND), cols]
            heads = [kv * Q_GROUP + g for g in range(Q_GROUP)]
            dog = jnp.concatenate([do_ref[:, hd * HEAD_DIM:(hd + 1) * HEAD_DIM] for hd in heads], axis=0)
            dp = lax.dot_general(dog, vb, (((1,), (1,)), ((), ())), preferred_element_type=F32)
            delta = jnp.sum(p * dp, axis=1, keepdims=True)
            ds = (p * (dp - delta) * ATTN_SCALE).astype(BF16)
            dqg = jnp.dot(ds, kb, preferred_element_type=F32)
            dk_ref[pl.ds(start, BAND), cols] += lax.dot_general(ds, qg, (((0,), (0,)), ((), ())), preferred_element_type=F32)
            dv_ref[pl.ds(start, BAND), cols] += lax.dot_general(p.astype(BF16), dog, (((0,), (0,)), ((), ())),
                                                                 preferred_element_type=F32)
            dsk = p_sink * delta
            for g, hd in enumerate(heads):
                dq_ref[:, hd * HEAD_DIM:(hd + 1) * HEAD_DIM] = dqg[g * BLOCK:(g + 1) * BLOCK]
                tot = jnp.sum(dsk[g * BLOCK:(g + 1) * BLOCK], axis=0, keepdims=True)
                dsink_ref[hd:hd + 1, :] -= jnp.broadcast_to(tot, (1, 128))

    whole = pl.BlockSpec((s, KV_WIDTH), lambda i: (0, 0))
    blk = pl.BlockSpec((BLOCK, ATTN_WIDTH), lambda i: (i, 0))
    return pl.pallas_call(
        body, name="swa_bwd", grid=(s // BLOCK,),
        in_specs=[pl.BlockSpec(memory_space=pltpu.SMEM), blk, whole, whole, blk],
        out_specs=[blk, whole, whole, pl.BlockSpec((N_Q_HEADS, 128), lambda i: (0, 0))],
        out_shape=[jax.ShapeDtypeStruct((s, ATTN_WIDTH), F32), jax.ShapeDtypeStruct((s, KV_WIDTH), F32),
                   jax.ShapeDtypeStruct((s, KV_WIDTH), F32), jax.ShapeDtypeStruct((N_Q_HEADS, 128), F32)],
        compiler_params=_params(1),
    )(sink, q, k, v, d_out)


CONV_CHUNK = 256


def _shift_rows(t, rows, down):
    n = t.shape[0]
    rolled = pltpu.roll(t, 1 if down else n - 1, 0)
    edge = 0 if down else n - 1
    return jnp.where(rows == edge, 0.0, rolled)


def _conv_specs(s):
    def z_spec(off):
        return pl.BlockSpec((s, CONV_CHUNK), lambda j, off=off: (0, off // CONV_CHUNK + j))
    chunk = pl.BlockSpec((s, CONV_CHUNK), lambda j: (0, j))
    w_spec = pl.BlockSpec((3, CONV_CHUNK), lambda j: (0, j))
    return z_spec(CU_OFF), z_spec(CB_OFF), z_spec(CC_OFF), chunk, w_spec


def _conv_fwd(z, conv_w):
    s = z.shape[0]
    cu_spec, cb_spec, cc_spec, chunk, w_spec = _conv_specs(s)

    def body(cu_ref, cb_ref, cc_ref, w_ref, o_ref):
        rows = lax.broadcasted_iota(jnp.int32, (s, 1), 0)
        t = cc_ref[...] * cu_ref[...]
        c3 = _shift_rows(t, rows, True) * w_ref[0:1, :] + t * w_ref[1:2, :] + _shift_rows(t, rows, False) * w_ref[2:3, :]
        o_ref[...] = (cb_ref[...] * c3).astype(BF16)

    return pl.pallas_call(
        body, name="conv_fwd", grid=(CONV_WIDTH // CONV_CHUNK,),
        in_specs=[cu_spec, cb_spec, cc_spec, w_spec],
        out_specs=chunk,
        out_shape=jax.ShapeDtypeStruct((s, CONV_WIDTH), BF16),
        compiler_params=_params(1),
    )(z, z, z, conv_w)


def _conv_bwd(z, conv_w, d_co):
    s = z.shape[0]
    cu_spec, cb_spec, cc_spec, chunk, w_spec = _conv_specs(s)

    def body(cu_ref, cb_ref, cc_ref, w_ref, d_ref, dcu_ref, dcb_ref, dcc_ref, dw_ref):
        rows = lax.broadcasted_iota(jnp.int32, (s, 1), 0)
        cu, cc = cu_ref[...], cc_ref[...]
        t = cc * cu
        t_dn, t_up = _shift_rows(t, rows, True), _shift_rows(t, rows, False)
        c3 = t_dn * w_ref[0:1, :] + t * w_ref[1:2, :] + t_up * w_ref[2:3, :]
        d = d_ref[...]
        dcb_ref[...] = (d * c3).astype(BF16)
        dc3 = d * cb_ref[...]
        dw_ref[0:1, :] = jnp.sum(dc3 * t_dn, axis=0, keepdims=True)
        dw_ref[1:2, :] = jnp.sum(dc3 * t, axis=0, keepdims=True)
        dw_ref[2:3, :] = jnp.sum(dc3 * t_up, axis=0, keepdims=True)
        dt = _shift_rows(dc3, rows, False) * w_ref[0:1, :] + dc3 * w_ref[1:2, :] + _shift_rows(dc3, rows, True) * w_ref[2:3, :]
        dcc_ref[...] = (dt * cu).astype(BF16)
        dcu_ref[...] = (dt * cc).astype(BF16)

    return pl.pallas_call(
        body, name="conv_bwd", grid=(CONV_WIDTH // CONV_CHUNK,),
        in_specs=[cu_spec, cb_spec, cc_spec, w_spec, chunk],
        out_specs=[chunk, chunk, chunk, w_spec],
        out_shape=[jax.ShapeDtypeStruct((s, CONV_WIDTH), BF16)] * 3 + [jax.ShapeDtypeStruct((3, CONV_WIDTH), F32)],
        compiler_params=_params(1),
    )(z, z, z, conv_w, d_co)


GATE_CHUNK = 512


def _gate_specs(s, d, tr):
    n_chunks = d // GATE_CHUNK
    za = pl.BlockSpec((tr, GATE_CHUNK), lambda j, i: (i, GL_OFF // GATE_CHUNK + j))
    zc = pl.BlockSpec((tr, GATE_CHUNK), lambda j, i: (i, GL_OFF // GATE_CHUNK + n_chunks + j))
    ba = pl.BlockSpec((1, GATE_CHUNK), lambda j, i: (0, j))
    bc = pl.BlockSpec((1, GATE_CHUNK), lambda j, i: (0, n_chunks + j))
    tile = pl.BlockSpec((tr, GATE_CHUNK), lambda j, i: (i, j))
    return za, zc, ba, bc, tile


def _gate_fwd(z, b_gate, ya, yc):
    s, d = ya.shape
    tr = _row_tile(s, 512)
    za, zc, ba, bc, tile = _gate_specs(s, d, tr)

    def body(za_ref, zc_ref, ba_ref, bc_ref, ya_ref, yc_ref, o_ref):
        ga = jax.nn.sigmoid(za_ref[...] + ba_ref[...])
        gc = jax.nn.sigmoid(zc_ref[...] + bc_ref[...])
        o_ref[...] = (ga * ya_ref[...] + gc * yc_ref[...]).astype(BF16)

    return pl.pallas_call(
        body, name="gate_fwd", grid=(d // GATE_CHUNK, s // tr),
        in_specs=[za, zc, ba, bc, tile, tile],
        out_specs=tile,
        out_shape=jax.ShapeDtypeStruct((s, d), BF16),
        compiler_params=_params(2),
    )(z, z, b_gate, b_gate, ya, yc)


def _gate_bwd(z, b_gate, ya, yc, dmix):
    s, d = ya.shape
    tr = _row_tile(s, 512)
    za, zc, ba, bc, tile = _gate_specs(s, d, tr)
    vec = pl.BlockSpec((1, GATE_CHUNK), lambda j, i: (0, j))

    def body(za_ref, zc_ref, ba_ref, bc_ref, ya_ref, yc_ref, dm_ref, dya_ref, dyc_ref, dla_ref, dlc_ref, dba_ref, dbc_ref):
        ga = jax.nn.sigmoid(za_ref[...] + ba_ref[...])
        gc = jax.nn.sigmoid(zc_ref[...] + bc_ref[...])
        dm = dm_ref[...]
        dya_ref[...] = (dm * ga).astype(BF16)
        dyc_ref[...] = (dm * gc).astype(BF16)
        dla = dm * ya_ref[...] * ga * (1.0 - ga)
        dlc = dm * yc_ref[...] * gc * (1.0 - gc)
        dla_ref[...] = dla.astype(BF16)
        dlc_ref[...] = dlc.astype(BF16)
        pa = jnp.sum(dla, axis=0, keepdims=True)
        pc = jnp.sum(dlc, axis=0, keepdims=True)

        @pl.when(pl.program_id(1) == 0)
        def _():
            dba_ref[...] = pa
            dbc_ref[...] = pc

        @pl.when(pl.program_id(1) > 0)
        def _():
            dba_ref[...] += pa
            dbc_ref[...] += pc

    big = jax.ShapeDtypeStruct((s, d), BF16)
    small = jax.ShapeDtypeStruct((1, d), F32)
    return pl.pallas_call(
        body, name="gate_bwd", grid=(d // GATE_CHUNK, s // tr),
        in_specs=[za, zc, ba, bc, tile, tile, tile],
        out_specs=[tile, tile, tile, tile, vec, vec],
        out_shape=[big, big, big, big, small, small],
        compiler_params=_params(2),
    )(z, z, b_gate, b_gate, ya, yc, dmix)


def _cross_probs(q_ref, kv_ref, hd):
    cols = slice(hd * HEAD_DIM, (hd + 1) * HEAD_DIM)
    qh = q_ref[:, cols]
    kh = kv_ref[:, cols]
    sc = lax.dot_general(qh, kh, (((1,), (1,)), ((), ())), preferred_element_type=F32) * ATTN_SCALE
    e = jnp.exp(sc - jnp.max(sc, axis=1, keepdims=True))
    return qh, kh, e * (1.0 / jnp.sum(e, axis=1, keepdims=True))


def _cross_fwd(qc, kvc):
    s = qc.shape[0]
    n_mem = kvc.shape[0]
    tq = _row_tile(s, 256)

    def body(q_ref, kv_ref, o_ref):
        for hd in range(MEM_HEADS):
            _, _, p = _cross_probs(q_ref, kv_ref, hd)
            vh = kv_ref[:, MEM_WIDTH + hd * HEAD_DIM:MEM_WIDTH + (hd + 1) * HEAD_DIM]
            o_ref[:, hd * HEAD_DIM:(hd + 1) * HEAD_DIM] = jnp.dot(p.astype(BF16), vh, preferred_element_type=F32).astype(BF16)

    return pl.pallas_call(
        body, name="cross_fwd", grid=(s // tq,),
        in_specs=[pl.BlockSpec((tq, MEM_WIDTH), lambda i: (i, 0)), pl.BlockSpec((n_mem, 2 * MEM_WIDTH), lambda i: (0, 0))],
        out_specs=pl.BlockSpec((tq, MEM_WIDTH), lambda i: (i, 0)),
        out_shape=jax.ShapeDtypeStruct((s, MEM_WIDTH), BF16),
        compiler_params=_params(1),
    )(qc, kvc)


def _cross_bwd(qc, kvc, d_out):
    s = qc.shape[0]
    n_mem = kvc.shape[0]
    tq = _row_tile(s, 256)

    def body(q_ref, kv_ref, do_ref, dq_ref, dkv_ref):
        @pl.when(pl.program_id(0) == 0)
        def _():
            dkv_ref[...] = jnp.zeros_like(dkv_ref)

        for hd in range(MEM_HEADS):
            cols = slice(hd * HEAD_DIM, (hd + 1) * HEAD_DIM)
            vcols = slice(MEM_WIDTH + hd * HEAD_DIM, MEM_WIDTH + (hd + 1) * HEAD_DIM)
            qh, kh, p = _cross_probs(q_ref, kv_ref, hd)
            doh = do_ref[:, cols]
            dp = lax.dot_general(doh, kv_ref[:, vcols], (((1,), (1,)), ((), ())), preferred_element_type=F32)
            ds = (p * (dp - jnp.sum(p * dp, axis=1, keepdims=True)) * ATTN_SCALE).astype(BF16)
            dq_ref[:, cols] = jnp.dot(ds, kh, preferred_element_type=F32).astype(BF16)
            dkv_ref[:, cols] += lax.dot_general(ds, qh, (((0,), (0,)), ((), ())), preferred_element_type=F32)
            dkv_ref[:, vcols] += lax.dot_general(p.astype(BF16), doh, (((0,), (0,)), ((), ())), preferred_element_type=F32)

    qspec = pl.BlockSpec((tq, MEM_WIDTH), lambda i: (i, 0))
    kvspec = pl.BlockSpec((n_mem, 2 * MEM_WIDTH), lambda i: (0, 0))
    return pl.pallas_call(
        body, name="cross_bwd", grid=(s // tq,),
        in_specs=[qspec, kvspec, qspec],
        out_specs=[qspec, kvspec],
        out_shape=[jax.ShapeDtypeStruct((s, MEM_WIDTH), BF16), jax.ShapeDtypeStruct((n_mem, 2 * MEM_WIDTH), F32)],
        compiler_params=_params(1),
    )(qc, kvc, d_out)


def _swiglu_fwd(up, gate):
    return up, (gate * jax.nn.sigmoid(gate)) * up


def _swiglu_bwd(d_act, gate, up):
    sg = jax.nn.sigmoid(gate)
    silu = gate * sg
    return d_act * up * (sg * (1.0 + gate * (1.0 - sg))), d_act * silu


GATHER_GROUPS = {"in": ("w_in", "conv_w"), "mid": ("w_attn_out", "w_conv_out", "w_o", "w_cq", "w_ckv", "w_co"),
                 "gate": ("w_gate",), "up": ("w_up",), "down": ("w_down",)}


def _local_step(xs, mems, target, small, fetch, reduce):
    s, d = xs.shape
    w4 = {}
    cos_t, sin_t = _rope_tables(s)

    def near(group, done, then, after):
        waits = [("direct", group)] + ([("pass_near", done), ("pass_far", done)] if done else [])
        starts = [("forward", group), ("pass_near", group)] + [("direct", g) for g in then]
        tok = fetch.step("gather_near_" + group, waits, starts, after)
        if done:
            w4.update(fetch.arrays(done))
        return tok

    def far(group, then, after):
        return fetch.step("gather_far_" + group, [("forward", group)], [("pass_far", group)] + [("direct", g) for g in then], after)

    def last(group, after):
        tok = fetch.step("gather_done_" + group, [("pass_near", group), ("pass_far", group)], [], after)
        w4.update(fetch.arrays(group))
        return tok

    h = _rmsnorm(xs, small["g_mix"], "norm_mix")
    tok = near("in", None, ["mid"], h)
    tok = far("in", ["gate"], tok)
    tok = last("in", tok)
    conv4 = w4["conv_w"]
    conv_w = conv4[:, :3, :].transpose(1, 0, 2).reshape(3, N_CHIPS * conv4.shape[2])
    c_in = w4["w_in"].shape[2]
    z = _matmul(h, w4["w_in"], mode="nn", tm=512, tn=c_in, tk=d, out_dtypes=[F32], name="in_proj", b_blocks=N_CHIPS, after=tok)
    tok = near("mid", None, ["up"], z)
    q_rot, k_rot, v_b = _rope_fwd(z, cos_t, sin_t)
    attn = _swa_fwd(q_rot, k_rot, v_b, small["sink"])
    co = _conv_fwd(z, conv_w)
    tok = far("mid", [], attn)
    tok = near("gate", "mid", ["down"], tok)
    w_o = w4["w_o"].reshape(-1, w4["w_o"].shape[-1])
    c_d = w4["w_attn_out"].shape[2]
    ya = _matmul(attn, w4["w_attn_out"], mode="nn", tm=1024, tn=c_d, tk=ATTN_WIDTH, out_dtypes=[F32], name="attn_out_proj",
                 b_blocks=N_CHIPS, after=tok)
    yc = _matmul(co, w4["w_conv_out"], mode="nn", tm=1024, tn=c_d, tk=CONV_WIDTH, out_dtypes=[F32], name="conv_out_proj",
                 b_blocks=N_CHIPS)
    mix = _gate_fwd(z, small["b_gate"], ya, yc)
    x1 = _matmul(mix, w_o, mode="nn", tm=512, tn=1024, tk=d, out_dtypes=[F32], name="mix_out_proj", extras=[xs],
                 epilogue=_add_residual)
    tok = far("gate", [], x1)
    w_cq = w4["w_cq"].reshape(-1, w4["w_cq"].shape[-1])
    w_ckv = w4["w_ckv"].reshape(-1, w4["w_ckv"].shape[-1])
    hc = _rmsnorm(x1, small["g_cross"], "norm_cross")
    memn = _rmsnorm(mems, small["g_mem"], "norm_mem")
    qc = _matmul(hc, w_cq, mode="nn", tm=1024, tn=MEM_WIDTH, tk=d, out_dtypes=[BF16], name="cross_q_proj", after=tok)
    kvc = _matmul(memn, w_ckv, mode="nn", tm=256, tn=2 * MEM_WIDTH, tk=d, out_dtypes=[BF16], name="cross_kv_proj")
    oc = _cross_fwd(qc, kvc)
    x2 = _matmul(oc, w4["w_co"], mode="nn", tm=1024, tn=c_d, tk=MEM_WIDTH, out_dtypes=[F32], name="cross_out_proj",
                 extras=[x1], epilogue=_add_residual, b_blocks=N_CHIPS)
    hf = _rmsnorm(x2, small["g_ffn"], "norm_ffn")
    tok = near("up", "gate", [], hf)
    c_ff = w4["w_gate"].shape[2]
    gate = _matmul(hf, w4["w_gate"], mode="nn", tm=512, tn=c_ff, tk=d, out_dtypes=[F32], name="ffn_gate_proj", b_blocks=N_CHIPS,
                   after=tok)
    tok = far("up", [], gate)
    tok = near("down", "up", [], tok)
    up, act = _matmul(hf, w4["w_up"], mode="nn", tm=512, tn=c_ff, tk=d, out_dtypes=[F32, BF16], name="ffn_up_proj",
                      extras=[gate], epilogue=_swiglu_fwd, b_blocks=N_CHIPS, after=tok)
    tok = far("down", [], act)
    last("down", tok)
    w_down = w4["w_down"].reshape(-1, w4["w_down"].shape[-1])
    x3 = _matmul(act, w_down, mode="nn", tm=512, tn=512, tk=w_down.shape[0], out_dtypes=[F32], name="ffn_down_proj", extras=[x2],
                 epilogue=_add_residual)
    dx3, dx3b, sq, dg_final = _loss_head(x3, small["g_final"], target)

    da, du = _matmul(dx3b, w_down, mode="nt", tm=512, tn=c_ff, tk=d, out_dtypes=[BF16, BF16], name="ffn_down_bwd",
                     extras=[gate, up], epilogue=_swiglu_bwd)
    core = reduce.core
    ffn_shape = dict(row_sharded=False, tm=512, tn=c_ff)
    g_down = _matmul(act, dx3b, mode="tn", tm=c_ff, tn=1024, tk=s, out_dtypes=[BF16], name="ffn_down_wgrad")
    tok = reduce.add("down", {"w_down": g_down}, da)
    t_gate = _wgrad_half(hf, da, core, theirs=True, name="ffn_gate_wgrad_theirs", after=tok, **ffn_shape)
    tok = reduce.step("down", t_gate)
    t_up = _wgrad_half(hf, du, core, theirs=True, name="ffn_up_wgrad_theirs", after=tok, **ffn_shape)
    tok = reduce.send("ffn", {"w_gate": t_gate, "w_up": t_up}, dx3b)
    dhf = _matmul(da, w4["w_gate"], mode="nt", tm=512, tn=1024, tk=N_CHIPS * c_ff, out_dtypes=[F32], name="ffn_gate_bwd", b_blocks=N_CHIPS,
                  after=tok)
    got = reduce.received("ffn", dhf)
    p_gate = _wgrad_half(hf, da, core, theirs=False, name="ffn_gate_wgrad_mine", add=got["w_gate"], **ffn_shape)
    p_up = _wgrad_half(hf, du, core, theirs=False, name="ffn_up_wgrad_mine", add=got["w_up"], **ffn_shape)
    tok = reduce.add_parts("ffn", {"w_gate": p_gate, "w_up": p_up})
    dhf = _matmul(du, w4["w_up"], mode="nt", tm=512, tn=1024, tk=N_CHIPS * c_ff, out_dtypes=[F32], name="ffn_up_bwd", extras=[dhf],
                  epilogue=_add_residual, b_blocks=N_CHIPS, after=tok)
    tok = reduce.step("down", dhf)
    dx2, dx2b, dg_ffn = _rmsnorm_bwd(dhf, x2, small["g_ffn"], dx3, "norm_ffn_bwd")

    d_oc = _matmul(dx2b, w4["w_co"], mode="nt", tm=1024, tn=MEM_WIDTH, tk=d, out_dtypes=[BF16], name="cross_out_bwd",
                   b_blocks=N_CHIPS, after=tok)
    g_co = _matmul(oc, dx2b, mode="tn", tm=MEM_WIDTH, tn=c_d, tk=s, out_dtypes=[BF16], name="cross_out_wgrad", out_blocks=N_CHIPS)
    tok = reduce.step("down", g_co)
    dqc, dkvc = _cross_bwd(qc, kvc, d_oc)
    g_cq = _matmul(hc, dqc, mode="tn", tm=1024, tn=MEM_WIDTH, tk=s, out_dtypes=[BF16], name="cross_q_wgrad", after=tok)
    dhc = _matmul(dqc, w_cq, mode="nt", tm=1024, tn=1024, tk=MEM_WIDTH, out_dtypes=[F32], name="cross_q_bwd")
    g_ckv = _matmul(memn, dkvc, mode="tn", tm=1024, tn=2 * MEM_WIDTH, tk=mems.shape[0], out_dtypes=[BF16], name="cross_kv_wgrad")
    dmemn = _matmul(dkvc, w_ckv, mode="nt", tm=256, tn=1024, tk=2 * MEM_WIDTH, out_dtypes=[F32], name="cross_kv_bwd")
    _, _, dg_mem = _rmsnorm_bwd(dmemn, mems, small["g_mem"], None, "norm_mem_bwd")
    dx1, dx1b, dg_cross = _rmsnorm_bwd(dhc, x1, small["g_cross"], dx2, "norm_cross_bwd")

    dmix = _matmul(dx1b, w_o, mode="nt", tm=512, tn=1024, tk=d, out_dtypes=[F32], name="mix_out_bwd")
    g_o = _matmul(mix, dx1b, mode="tn", tm=1024, tn=1024, tk=s, out_dtypes=[BF16], name="mix_out_wgrad")
    dya, dyc, dgl_a, dgl_c, db_a, db_c = _gate_bwd(z, small["b_gate"], ya, yc, dmix)
    d_attn = _matmul(dya, w4["w_attn_out"], mode="nt", tm=1024, tn=ATTN_WIDTH, tk=d, out_dtypes=[BF16], name="attn_out_bwd",
                     b_blocks=N_CHIPS)
    g_ao = _matmul(attn, dya, mode="tn", tm=ATTN_WIDTH, tn=c_d, tk=s, out_dtypes=[BF16], name="attn_out_wgrad", out_blocks=N_CHIPS)
    d_co = _matmul(dyc, w4["w_conv_out"], mode="nt", tm=1024, tn=CONV_WIDTH, tk=d, out_dtypes=[F32], name="conv_out_bwd",
                   b_blocks=N_CHIPS)
    g_cvo = _matmul(co, dyc, mode="tn", tm=CONV_WIDTH, tn=c_d, tk=s, out_dtypes=[BF16], name="conv_out_wgrad", out_blocks=N_CHIPS)
    tok = reduce.step("ffn", g_cvo)
    tok = reduce.add("mid", {"w_co": g_co, "w_cq": g_cq, "w_ckv": g_ckv, "w_o": g_o, "w_attn_out": g_ao, "w_conv_out": g_cvo}, tok)
    dcu, dcb, dcc, d_conv_w = _conv_bwd(z, conv_w, d_co)
    dq_rot, dk_rot, dv, dsink = _swa_bwd(q_rot, k_rot, v_b, d_attn, small["sink"])
    tok = reduce.step("mid", dq_rot)
    dq, dk, dvb = _rope_bwd(dq_rot, dk_rot, dv, cos_t, sin_t)
    dz = jnp.concatenate([dq, dk, dvb, dcu, dcb, dcc, dgl_a, dgl_c], axis=1)
    in_shape = dict(row_sharded=False, tm=512, tn=c_in)
    t_in = _wgrad_half(h, dz, core, theirs=True, name="in_proj_wgrad_theirs", after=tok, **in_shape)
    tok = reduce.send("in", {"w_in": t_in}, dk)
    tok = reduce.step("ffn", tok)
    tok = reduce.step("mid", tok)
    got = reduce.received("in", tok)
    p_in = _wgrad_half(h, dz, core, theirs=False, name="in_proj_wgrad_mine", add=got["w_in"], **in_shape)
    tok = reduce.add_parts("in", {"w_in": p_in})
    dh = _matmul(dz, w4["w_in"], mode="nt", tm=512, tn=512, tk=N_CHIPS * c_in, out_dtypes=[F32], name="in_proj_bwd", b_blocks=N_CHIPS,
                 after=tok)
    tok = reduce.step("mid", dh)
    grad_x, _, dg_mix = _rmsnorm_bwd(dh, xs, small["g_mix"], dx1, "norm_mix_bwd")

    small_grads = {
        "g_mix": dg_mix, "sink": dsink[:, 0], "b_gate": jnp.concatenate([db_a, db_c], axis=1), "g_cross": dg_cross,
        "g_mem": dg_mem, "g_ffn": dg_ffn, "g_final": dg_final, "conv_w": d_conv_w,
    }
    return sq, grad_x, small_grads


def _pair_sum(g4, ra, core, name):
    nb, rs, cs = g4.shape
    rh = rs // 2
    tr = _row_tile(rh, 256)
    per = rh // tr

    def body(c_ref, g_ref, r_ref, o_ref):
        o_ref[...] = (g_ref[...].astype(F32) + r_ref[...].astype(F32)).astype(BF16)

    plain = pl.BlockSpec((None, tr, cs), lambda j, i, c: (j, i, 0))
    return pl.pallas_call(
        body, name=name,
        grid_spec=pltpu.PrefetchScalarGridSpec(
            num_scalar_prefetch=1, grid=(nb, per),
            in_specs=[pl.BlockSpec((None, tr, cs), lambda j, i, c: (j, c[0] * per + i, 0)), plain],
            out_specs=plain),
        out_shape=jax.ShapeDtypeStruct((nb, rh, cs), BF16),
        compiler_params=_params(2),
    )(core, g4, ra)


def _quad_sum(parts, rc, place, name):
    _, rh, cs = parts.shape
    tr = _row_tile(rh, 256)
    per = rh // tr

    def body(p_ref, own_ref, r_ref, o_ref):
        acc = own_ref[...].astype(F32)
        for j in range(rc.shape[0]):
            acc = acc + r_ref[j].astype(F32)
        o_ref[...] = acc

    return pl.pallas_call(
        body, name=name,
        grid_spec=pltpu.PrefetchScalarGridSpec(
            num_scalar_prefetch=1, grid=(per,),
            in_specs=[pl.BlockSpec((None, tr, cs), lambda i, p: (p[0], i, 0)),
                      pl.BlockSpec((rc.shape[0], tr, cs), lambda i, p: (0, i, 0))],
            out_specs=pl.BlockSpec((tr, cs), lambda i, p: (p[1] * per + i, 0))),
        out_shape=jax.ShapeDtypeStruct((2 * rh, cs), F32),
        compiler_params=_params(1),
    )(place, parts, rc)


def _cast_to_slot(w, place, dtype, name, after=None):
    rows, cols = w.shape
    tr = _row_tile(rows, 256)

    def body(p_ref, w_ref, *rest):
        o_ref = rest[-1]
        o_ref[...] = w_ref[...].astype(dtype)

    return pl.pallas_call(
        body, name=name,
        grid_spec=pltpu.PrefetchScalarGridSpec(
            num_scalar_prefetch=1, grid=(rows // tr,),
            in_specs=[pl.BlockSpec((tr, cols), lambda i, p: (i, 0))] + ([] if after is None else [ANY]),
            out_specs=pl.BlockSpec((None, tr, cols), lambda i, p: (p[0], i, 0))),
        out_shape=jax.ShapeDtypeStruct((N_CHIPS, rows, cols), dtype),
        compiler_params=_params(1),
    )(place, w, *([] if after is None else [after]))


def _adamw(w, g, m, v, name, after=None):
    rows, cols = w.shape
    tr = _row_tile(rows, 256)

    def body(w_ref, g_ref, m_ref, v_ref, *rest):
        go_ref, d_ref, nm_ref, nv_ref = rest[-4:]
        gv = g_ref[...]
        go_ref[...] = gv
        nm = ADAM_B1 * m_ref[...] + (1.0 - ADAM_B1) * gv
        nv = ADAM_B2 * v_ref[...] + (1.0 - ADAM_B2) * (gv * gv)
        m_hat = nm / ADAM_C1
        v_hat = nv / ADAM_C2
        d_ref[...] = -ADAM_LR * (m_hat / (jnp.sqrt(v_hat) + ADAM_EPS) + ADAM_WD * w_ref[...])
        nm_ref[...] = nm
        nv_ref[...] = nv

    tile = pl.BlockSpec((tr, cols), lambda i: (i, 0))
    shape = jax.ShapeDtypeStruct((rows, cols), F32)
    return pl.pallas_call(
        body, name=name, grid=(rows // tr,),
        in_specs=[tile] * 4 + ([] if after is None else [ANY]), out_specs=[tile] * 4, out_shape=[shape] * 4,
        compiler_params=_params(1),
    )(w, g, m, v, *([] if after is None else [after]))


def _mesh_pos():
    return lax.axis_index("x"), lax.axis_index("y"), lax.axis_index("c")


def _other_chips(x, y):
    return [(1 - x, y), (x, 1 - y), (1 - x, 1 - y)]


def _half_rows(ref, which):
    rh = ref.shape[-2] // 2
    return ref.at[pl.ds(which * rh, rh), :]


def _remote(src, dst, send_sems, recv_sems, sem, to):
    return pltpu.make_async_remote_copy(src_ref=src, dst_ref=dst, send_sem=send_sems.at[sem], recv_sem=recv_sems.at[sem],
                                        device_id=to, device_id_type=MESH)


HBM = pl.BlockSpec(memory_space=pltpu.HBM)
SEM = pl.BlockSpec(memory_space=pltpu.SEMAPHORE)
DATAFLOW_EFFECT = pltpu.SideEffectType.DATAFLOW_SIDE_EFFECTING


def _in_hbm(arrays):
    return [pltpu.with_memory_space_constraint(a, pltpu.HBM) for a in arrays]


def _hbm_like(arrays):
    return [pltpu.HBM(a.shape, a.dtype) for a in arrays]


GATHER_COPIES_PER_ARRAY = {"direct": 2, "forward": 2, "pass_near": 2, "pass_far": 1}


def _gather_copies(kind, refs, x, y, c):
    me, near_x, near_y, far = 2 * x + y, 2 * (1 - x) + y, 2 * x + (1 - y), 2 * (1 - x) + (1 - y)
    to_x, to_y, sibling = (1 - x, y, c), (x, 1 - y, c), (x, y, 1 - c)
    out = []
    for ref in refs:
        rh = ref.shape[1] // 2
        rq = rh // 2

        def half(chip, ref=ref, rh=rh):
            return ref.at[chip, pl.ds(c * rh, rh), :]

        def quarter(chip, q, ref=ref, rh=rh, rq=rq):
            return ref.at[chip, pl.ds(c * rh + q * rq, rq), :]

        if kind == "direct":
            out += [(half(me), half(me), to_x), (half(me), half(me), to_y)]
        elif kind == "forward":
            out += [(quarter(near_x, 0), quarter(near_x, 0), to_y), (quarter(near_y, 1), quarter(near_y, 1), to_x)]
        elif kind == "pass_near":
            out += [(half(near_x), half(near_x), sibling), (half(near_y), half(near_y), sibling)]
        else:
            assert kind == "pass_far"
            out += [(half(far), half(far), sibling)]
    return out


def _gather_step(name, bufs, waits, starts, after):
    nb, nw, ns = len(bufs), len(waits), len(starts)
    n_after = 0 if after is None else 1

    def body(*refs):
        ins = refs[:nb]
        wait_sems = refs[nb:nb + 2 * nw]
        start_sems = refs[nb + 2 * nw + n_after:nb + 2 * nw + n_after + 2 * ns]
        token = refs[-1]
        x, y, c = _mesh_pos()
        for j, (kind, idxs, _, _) in enumerate(waits):
            for i, (s_ref, d_ref, to) in enumerate(_gather_copies(kind, [ins[t] for t in idxs], x, y, c)):
                came = _remote(s_ref, d_ref, wait_sems[2 * j], wait_sems[2 * j + 1], i, to)
                came.wait_recv()
                came.wait_send()
        for j, (kind, idxs) in enumerate(starts):
            for i, (s_ref, d_ref, to) in enumerate(_gather_copies(kind, [ins[t] for t in idxs], x, y, c)):
                _remote(s_ref, d_ref, start_sems[2 * j], start_sems[2 * j + 1], i, to).start()
        token[...] = jnp.zeros_like(token)

    sems = []
    for kind, idxs in starts:
        sems += [pltpu.SemaphoreType.DMA((GATHER_COPIES_PER_ARRAY[kind] * len(idxs),))] * 2
    operands = _in_hbm(bufs) + [sem for w in waits for sem in w[2:]] + ([] if after is None else [after])
    outs = pl.pallas_call(
        body, name=name,
        in_specs=[HBM] * nb + [SEM] * (2 * nw) + [ANY] * n_after,
        out_specs=[SEM] * (2 * ns) + [HBM] * nb + [pl.BlockSpec(memory_space=pltpu.VMEM)],
        out_shape=sems + _hbm_like(bufs) + [jax.ShapeDtypeStruct((8, 128), F32)],
        input_output_aliases={i: 2 * ns + i for i in range(nb)},
        compiler_params=pltpu.CompilerParams(has_side_effects=DATAFLOW_EFFECT),
    )(*operands)
    return outs[2 * ns:2 * ns + nb], [(outs[2 * j], outs[2 * j + 1]) for j in range(ns)], outs[-1]


class _Gather:
    def __init__(self, groups):
        self.groups = groups
        self.bufs = {}
        self.in_flight = {}

    def put(self, slotted):
        self.bufs.update(slotted)

    def step(self, name, waits, starts, after=None):
        names = []
        for _, group in list(waits) + list(starts):
            names += [n for n in self.groups[group] if n not in names]
        index = {n: i for i, n in enumerate(names)}

        def members(group):
            return [index[n] for n in self.groups[group]]

        wait_args = [(kind, members(group)) + self.in_flight.pop((kind, group)) for kind, group in waits]
        start_args = [(kind, members(group)) for kind, group in starts]
        bufs, sems, token = _gather_step(name, [self.bufs[n] for n in names], wait_args, start_args, after)
        self.bufs.update(zip(names, bufs))
        for (kind, group), pair in zip(starts, sems):
            self.in_flight[(kind, group)] = pair
        return token

    def arrays(self, group):
        return {n: self.bufs[n] for n in self.groups[group]}


def _sibling_halves_copies(srcs, dsts, x, y, c):
    out = []
    for s_ref, d_ref in zip(srcs, dsts, strict=True):
        rh = s_ref.shape[1] // 2
        out.append((s_ref.at[:, pl.ds((1 - c) * rh, rh), :], d_ref, (x, y, 1 - c)))
    return out


def _to_sibling_copies(srcs, dsts, x, y, c):
    return [(s_ref, d_ref, (x, y, 1 - c)) for s_ref, d_ref in zip(srcs, dsts, strict=True)]


def _chip_copies(srcs, dsts, x, y, c):
    out = []
    for s_ref, d_ref in zip(srcs, dsts, strict=True):
        for k, (px, py) in enumerate(_other_chips(x, y)):
            out.append((s_ref.at[2 * px + py], d_ref.at[k], (px, py, c)))
    return out


def _join_copies(srcs, dsts, x, y, c):
    out = []
    for s_ref in srcs:
        mine = _half_rows(s_ref, c)
        out.append((mine, mine, (x, y, 1 - c)))
    return out


def _exchange_start(copies_fn, n_copies, srcs, fresh, after, name):
    ns, nb = len(srcs), len(srcs) + len(fresh)

    def body(*refs):
        bufs, send, recv, token = refs[:nb], refs[nb + 1], refs[nb + 2], refs[-1]
        x, y, c = _mesh_pos()
        for i, (s_ref, d_ref, to) in enumerate(copies_fn(bufs[:ns], bufs[ns:] if fresh else bufs[:ns], x, y, c)):
            _remote(s_ref, d_ref, send, recv, i, to).start()
        token[...] = jnp.zeros_like(token)

    sems = [pltpu.SemaphoreType.DMA((n_copies,))] * 2
    outs = pl.pallas_call(
        body, name=name,
        in_specs=[HBM] * nb + [ANY], out_specs=[SEM, SEM] + [HBM] * nb + [pl.BlockSpec(memory_space=pltpu.VMEM)],
        out_shape=sems + _hbm_like(list(srcs) + list(fresh)) + [jax.ShapeDtypeStruct((8, 128), F32)],
        input_output_aliases={i: 2 + i for i in range(nb)},
        compiler_params=pltpu.CompilerParams(has_side_effects=DATAFLOW_EFFECT),
    )(*_in_hbm(list(srcs) + list(fresh)), after)
    return outs[0], outs[1], outs[2:2 + ns], outs[2 + ns:2 + nb], outs[-1]


def _exchange_done(copies_fn, srcs, fresh, send, recv, after, name):
    ns, nb = len(srcs), len(srcs) + len(fresh)

    def body(*refs):
        bufs, send_in, recv_in = refs[:nb], refs[nb], refs[nb + 1]
        x, y, c = _mesh_pos()
        for i, (s_ref, d_ref, to) in enumerate(copies_fn(bufs[:ns], bufs[ns:] if fresh else bufs[:ns], x, y, c)):
            came = _remote(s_ref, d_ref, send_in, recv_in, i, to)
            came.wait_send()
            came.wait_recv()

    outs = pl.pallas_call(
        body, name=name,
        in_specs=[HBM] * nb + [SEM, SEM, ANY], out_specs=[HBM] * nb,
        out_shape=_hbm_like(list(srcs) + list(fresh)),
        input_output_aliases={i: i for i in range(nb)},
        compiler_params=pltpu.CompilerParams(has_side_effects=DATAFLOW_EFFECT),
    )(*_in_hbm(list(srcs) + list(fresh)), send, recv, after)
    return outs[:ns], outs[ns:]


class _Reduce:
    def __init__(self, place, core, shards, mom_m, mom_v):
        self.place, self.core = place, core
        self.shards, self.mom_m, self.mom_v = shards, mom_m, mom_v
        self.state = {}
        self.results = {}

    def add(self, group, grads, after):
        names = list(grads)
        g4s = [g.reshape((N_CHIPS, -1, g.shape[-1])) if g.ndim == 2 else g for g in grads.values()]
        fresh = [lax.empty((N_CHIPS, g.shape[1] // 2, g.shape[2]), BF16) for g in g4s]
        send, recv, g4s, fresh, token = _exchange_start(_sibling_halves_copies, len(names), g4s, fresh, after,
                                                        "pair_start_" + group)
        self.state[group] = (0, names, send, recv, g4s, fresh)
        return token

    def send(self, group, theirs, after):
        names, srcs = list(theirs), list(theirs.values())
        fresh = [lax.empty(s.shape, BF16) for s in srcs]
        send, recv, srcs, fresh, token = _exchange_start(_to_sibling_copies, len(names), srcs, fresh, after, "pair_start_" + group)
        self.state[group] = ("sent", names, send, recv, srcs, fresh)
        return token

    def received(self, group, after):
        stage, names, send, recv, srcs, fresh = self.state.pop(group)
        assert stage == "sent"
        _, got = _exchange_done(_to_sibling_copies, srcs, fresh, send, recv, after, "pair_done_" + group)
        return dict(zip(names, got))

    def add_parts(self, group, parts):
        names, srcs = list(parts), list(parts.values())
        fresh = [lax.empty((N_CHIPS - 1,) + p.shape[1:], BF16) for p in srcs]
        send, recv, srcs, fresh, token = _exchange_start(_chip_copies, 3 * len(names), srcs, fresh, self.core, "chips_start_" + group)
        self.state[group] = (1, names, send, recv, srcs, fresh)
        return token

    def step(self, group, after):
        stage, names, send, recv, srcs, fresh = self.state[group]
        if stage == 0:
            g4s, ras = _exchange_done(_sibling_halves_copies, srcs, fresh, send, recv, after, "pair_done_" + group)
            parts = [_pair_sum(g, r, self.core, "pair_sum_" + n) for g, r, n in zip(g4s, ras, names)]
            fresh = [lax.empty((N_CHIPS - 1,) + p.shape[1:], BF16) for p in parts]
            send, recv, parts, fresh, token = _exchange_start(_chip_copies, 3 * len(names), parts, fresh, self.core,
                                                              "chips_start_" + group)
            self.state[group] = (1, names, send, recv, parts, fresh)
            return token
        if stage == 1:
            parts, rcs = _exchange_done(_chip_copies, srcs, fresh, send, recv, after, "chips_done_" + group)
            wholes = [_quad_sum(p, r, self.place, "quad_sum_" + n) for p, r, n in zip(parts, rcs, names)]
            send, recv, wholes, _, token = _exchange_start(_join_copies, len(names), wholes, [], self.core, "join_start_" + group)
            self.state[group] = (2, names, send, recv, wholes, [])
            return token
        assert stage == 2
        wholes, _ = _exchange_done(_join_copies, srcs, [], send, recv, after, "join_done_" + group)
        token = None
        for n, g in zip(names, wholes):
            self.results[n] = _adamw(self.shards[n], g, self.mom_m[n], self.mom_v[n], "adamw_" + n, after=token)
            token = self.results[n][1]
        del self.state[group]
        return token


N_DEV = 8


def _all_reduce_small(v):
    def body(v_ref, o_ref, slots, send_sems, recv_sems):
        x, y, c = _mesh_pos()
        me = 4 * x + 2 * y + c
        slots[me] = v_ref[...]
        peers = []
        for r in range(1, N_DEV):
            fx, fy, fc = (r >> 2) & 1, (r >> 1) & 1, r & 1
            peers.append((x + fx - 2 * x * fx, y + fy - 2 * y * fy, c + fc - 2 * c * fc))
        sends = []
        for r, peer in enumerate(peers):
            cp = _remote(v_ref, slots.at[me], send_sems, recv_sems, r, peer)
            cp.start()
            sends.append(cp)
        for r, (px, py, pc) in enumerate(peers):
            landed = slots.at[4 * px + 2 * py + pc]
            _remote(landed, landed, send_sems, recv_sems, r, (px, py, pc)).wait_recv()
        for cp in sends:
            cp.wait_send()
        acc = slots[0]
        for i in range(1, N_DEV):
            acc = acc + slots[i]
        o_ref[...] = acc

    vm = pl.BlockSpec(memory_space=pltpu.VMEM)
    return pl.pallas_call(
        body, name="small_grads_all_reduce",
        in_specs=[vm], out_specs=vm,
        out_shape=jax.ShapeDtypeStruct(v.shape, v.dtype),
        scratch_shapes=[pltpu.VMEM((N_DEV,) + v.shape, v.dtype), pltpu.SemaphoreType.DMA((N_DEV - 1,)),
                        pltpu.SemaphoreType.DMA((N_DEV - 1,))],
    )(v)


MATRICES = ("w_in", "w_attn_out", "w_conv_out", "w_o", "w_cq", "w_ckv", "w_co", "w_gate", "w_up", "w_down")
VECTORS = ("g_mix", "b_gate", "g_cross", "g_mem", "g_ffn", "g_final", "conv_w", "sink")
WEIGHT_ORDER = ("g_mix", "w_in", "sink", "conv_w", "b_gate", "w_attn_out", "w_conv_out", "w_o", "g_cross", "g_mem", "w_cq",
                "w_ckv", "w_co", "g_ffn", "w_gate", "w_up", "w_down", "g_final")
CONV_PAD_ROWS = 32
SMALL_ROWS = 8


def _pack(pieces):
    flat = jnp.concatenate([p.reshape(-1) for p in pieces])
    lane_group = SMALL_ROWS * 128
    total = -(-flat.shape[0] // lane_group) * lane_group
    flat = jnp.pad(flat, (0, total - flat.shape[0]))
    return flat.reshape(SMALL_ROWS, total // SMALL_ROWS), [p.size for p in pieces]


def _unpack(packed, pieces):
    flat = packed.reshape(-1)
    out, off = [], 0
    for p in pieces:
        out.append(flat[off:off + p.size].reshape(p.shape))
        off += p.size
    return out


def kernel(x, mem, g_mix, w_in, sink, conv_w, b_gate, w_attn_out, w_conv_out, w_o, g_cross, g_mem, w_cq, w_ckv, w_co, g_ffn, w_gate, w_up, w_down, g_final, loss_target, m_g_mix, m_w_in, m_sink, m_conv_w, m_b_gate, m_w_attn_out, m_w_conv_out, m_w_o, m_g_cross, m_g_mem, m_w_cq, m_w_ckv, m_w_co, m_g_ffn, m_w_gate, m_w_up, m_w_down, m_g_final, v_g_mix, v_w_in, v_sink, v_conv_w, v_b_gate, v_w_attn_out, v_w_conv_out, v_w_o, v_g_cross, v_g_mem, v_w_cq, v_w_ckv, v_w_co, v_g_ffn, v_w_gate, v_w_up, v_w_down, v_g_final):
    given = dict(g_mix=g_mix, w_in=w_in, sink=sink, conv_w=conv_w, b_gate=b_gate, w_attn_out=w_attn_out, w_conv_out=w_conv_out,
                 w_o=w_o, g_cross=g_cross, g_mem=g_mem, w_cq=w_cq, w_ckv=w_ckv, w_co=w_co, g_ffn=g_ffn, w_gate=w_gate, w_up=w_up,
                 w_down=w_down, g_final=g_final)
    mom_m = dict(g_mix=m_g_mix, w_in=m_w_in, sink=m_sink, conv_w=m_conv_w, b_gate=m_b_gate, w_attn_out=m_w_attn_out,
                 w_conv_out=m_w_conv_out, w_o=m_w_o, g_cross=m_g_cross, g_mem=m_g_mem, w_cq=m_w_cq, w_ckv=m_w_ckv, w_co=m_w_co,
                 g_ffn=m_g_ffn, w_gate=m_w_gate, w_up=m_w_up, w_down=m_w_down, g_final=m_g_final)
    mom_v = dict(g_mix=v_g_mix, w_in=v_w_in, sink=v_sink, conv_w=v_conv_w, b_gate=v_b_gate, w_attn_out=v_w_attn_out,
                 w_conv_out=v_w_conv_out, w_o=v_w_o, g_cross=v_g_cross, g_mem=v_g_mem, w_cq=v_w_cq, w_ckv=v_w_ckv, w_co=v_w_co,
                 g_ffn=v_g_ffn, w_gate=v_w_gate, w_up=v_w_up, w_down=v_w_down, g_final=v_g_final)
    xs, mems, target = x[0], mem[0], loss_target[0]
    d_model = xs.shape[1]
    chip = 2 * lax.axis_index("x") + lax.axis_index("y")
    core = jnp.reshape(lax.axis_index("c"), (1,)).astype(jnp.int32)
    place = jnp.stack([chip, lax.axis_index("c")]).astype(jnp.int32)

    shards = {n: given[n][0] for n in MATRICES}
    conv_cols = conv_w.shape[2]
    conv_pad = jnp.pad(conv_w[0], ((0, CONV_PAD_ROWS - conv_w.shape[1]), (0, 0)))
    fetch = _Gather(GATHER_GROUPS)
    first = {"w_in": _cast_to_slot(shards["w_in"], place, BF16, "to_slot_w_in"),
             "conv_w": _cast_to_slot(conv_pad, place, F32, "to_slot_conv_w")}
    fetch.put(first)
    tok = fetch.step("gather_start", [], [("direct", "in")])
    fetch.put({n: _cast_to_slot(shards[n], place, BF16, "to_slot_" + n, after=tok) for n in MATRICES if n != "w_in"})
    small = {n: given[n] for n in ("g_mix", "b_gate", "g_cross", "g_mem", "g_ffn")}
    small["g_final"] = g_final[None]
    small["sink"] = sink[0]

    reduce = _Reduce(place, core, shards, {n: mom_m[n][0] for n in MATRICES}, {n: mom_v[n][0] for n in MATRICES})
    sq, grad_x, small_grads = _local_step(xs, mems, target, small, fetch, reduce)

    loss_part = 0.5 * sq[0:1, 0:1] / d_model
    pieces = [small_grads[n] for n in VECTORS] + [loss_part]
    packed, _ = _pack(pieces)
    summed = _unpack(_all_reduce_small(packed), pieces)
    loss = summed[-1][0, 0]
    small_sum = dict(zip(VECTORS, summed[:-1]))
    small_sum["conv_w"] = lax.dynamic_slice_in_dim(small_sum["conv_w"], chip * conv_cols, conv_cols, axis=1)

    grad_out, delta, new_m, new_v = {}, {}, {}, {}
    like = [given[n] for n in VECTORS]
    pw, _ = _pack(like)
    pg, _ = _pack([small_sum[n] for n in VECTORS])
    pm, _ = _pack([mom_m[n] for n in VECTORS])
    pv, _ = _pack([mom_v[n] for n in VECTORS])
    _, pd, pnm, pnv = _adamw(pw, pg, pm, pv, "adamw_small")
    for n, g, d, nm, nv in zip(VECTORS, [small_sum[n] for n in VECTORS], _unpack(pd, like), _unpack(pnm, like), _unpack(pnv, like)):
        grad_out[n] = g.reshape(given[n].shape)
        delta[n], new_m[n], new_v[n] = d, nm, nv
    tok = reduce.step("in", pd)
    reduce.step("in", tok)
    for n in MATRICES:
        g, d, nm, nv = reduce.results[n]
        grad_out[n], delta[n], new_m[n], new_v[n] = g[None], d[None], nm[None], nv[None]

    return (loss, grad_x[None], *[grad_out[n] for n in WEIGHT_ORDER], *[delta[n] for n in WEIGHT_ORDER],
            *[new_m[n] for n in WEIGHT_ORDER], *[new_v[n] for n in WEIGHT_ORDER])
```

```python
import functools

import jax
import jax.numpy as jnp
from jax import lax
from jax.experimental import pallas as pl
from jax.experimental.pallas import tpu as pltpu

F32 = jnp.float32
BF16 = jnp.bfloat16
MESH = pl.DeviceIdType.MESH
ANY = pl.BlockSpec(memory_space=pl.ANY)

VMEM_LIMIT_BYTES = 56 * 1024 * 1024

N_CHIPS = 4
HEAD_DIM = 128
N_Q_HEADS = 8
N_KV_HEADS = 2
Q_GROUP = N_Q_HEADS // N_KV_HEADS
ATTN_WIDTH = N_Q_HEADS * HEAD_DIM
KV_WIDTH = N_KV_HEADS * HEAD_DIM
WINDOW = 128
BLOCK = 128
BAND = 3 * BLOCK
ROPE_THETA = 10000.0
CONV_WIDTH = 1024
MEM_HEADS = 4
MEM_WIDTH = MEM_HEADS * HEAD_DIM
RMS_EPS = 1e-6
NEG_INF = -1e30
ATTN_SCALE = HEAD_DIM ** -0.5

Q_OFF, K_OFF, V_OFF, CU_OFF, CB_OFF, CC_OFF, GL_OFF = 0, 1024, 1280, 1536, 2560, 3584, 4608

ADAM_LR = 0.001
ADAM_B1 = 0.9
ADAM_B2 = 0.999
ADAM_EPS = 1e-08
ADAM_WD = 0.01
ADAM_STEP = 10
ADAM_C1 = 1.0 - ADAM_B1 ** ADAM_STEP
ADAM_C2 = 1.0 - ADAM_B2 ** ADAM_STEP


def _params(n_grid_axes):
    return pltpu.CompilerParams(dimension_semantics=("arbitrary",) * n_grid_axes, vmem_limit_bytes=VMEM_LIMIT_BYTES)


BF16_SUBLANES = 16


def _row_tile(rows, want):
    if rows <= want:
        return rows
    for t in range(want, 0, -BF16_SUBLANES):
        if rows % t == 0:
            return t
    return rows


def _matmul(a, b, *, mode, tm, tn, tk, out_dtypes, name, extras=(), epilogue=None, b_blocks=1, out_blocks=1, after=None):
    if mode == "tn":
        kdim, m = a.shape
    else:
        m, kdim = a.shape
    if b_blocks > 1:
        nb, brows, bcols = b.shape
        assert nb == b_blocks
        if mode == "nn":
            n = bcols * nb
            assert brows == kdim
        else:
            assert mode == "nt" and bcols * nb == kdim
            n = brows
    else:
        n = b.shape[0] if mode == "nt" else b.shape[1]
    tm, tn = min(tm, m), min(tn, n)
    assert m % tm == 0 and n % tn == 0 and tk == kdim, (name, m, n, kdim, tm, tn, tk)
    n_extra, n_out = len(extras), len(out_dtypes)
    n_after = 0 if after is None else 1

    if mode == "tn":
        a_spec = pl.BlockSpec((tk, tm), lambda j, i, k: (k, i))
        dims = (((0,), (0,)), ((), ()))
    else:
        a_spec = pl.BlockSpec((tm, tk), lambda j, i, k: (i, k))
        dims = (((1,), (0,)), ((), ())) if mode == "nn" else (((1,), (1,)), ((), ()))

    if b_blocks > 1 and mode == "nn":
        per = b.shape[2] // tn
        assert b.shape[2] % tn == 0
        b_spec = pl.BlockSpec((None, tk, tn), lambda j, i, k: (j // per, k, j % per))
    elif b_blocks > 1:
        b_spec = pl.BlockSpec((b_blocks, tn, b.shape[2]), lambda j, i, k: (0, j, 0))
    elif mode == "nt":
        b_spec = pl.BlockSpec((tn, tk), lambda j, i, k: (j, k))
    else:
        b_spec = pl.BlockSpec((tk, tn), lambda j, i, k: (k, j))

    tile_spec = pl.BlockSpec((tm, tn), lambda j, i, k: (i, j))
    if out_blocks > 1:
        ncols = n // out_blocks
        assert ncols % tn == 0
        oper = ncols // tn
        out_spec = pl.BlockSpec((None, tm, tn), lambda j, i, k: (j // oper, i, j % oper))
        out_shape = [jax.ShapeDtypeStruct((out_blocks, m, ncols), dt) for dt in out_dtypes]
    else:
        out_spec = tile_spec
        out_shape = [jax.ShapeDtypeStruct((m, n), dt) for dt in out_dtypes]

    def body(a_ref, b_ref, *rest):
        extra_refs = rest[:n_extra]
        out_refs = rest[n_extra + n_after:n_extra + n_after + n_out]
        if mode == "nt" and b_blocks > 1:
            cs = b.shape[2]
            acc = None
            for jb in range(b_blocks):
                prod = lax.dot_general(a_ref[:, jb * cs:(jb + 1) * cs].astype(BF16), b_ref[jb].astype(BF16), dims,
                                       preferred_element_type=F32)
                acc = prod if acc is None else acc + prod
        else:
            acc = lax.dot_general(a_ref[...].astype(BF16), b_ref[...].astype(BF16), dims, preferred_element_type=F32)
        tiles = (acc,) if epilogue is None else epilogue(acc, *[r[...] for r in extra_refs])
        for o_ref, t in zip(out_refs, tiles, strict=True):
            o_ref[...] = t.astype(o_ref.dtype)

    outs = pl.pallas_call(
        body,
        name=name,
        grid=(n // tn, m // tm, 1),
        in_specs=[a_spec, b_spec] + [tile_spec] * n_extra + [ANY] * n_after,
        out_specs=[out_spec] * n_out,
        out_shape=out_shape,
        compiler_params=_params(3),
    )(a, b, *extras, *([] if after is None else [after]))
    return outs[0] if n_out == 1 else outs


def _add_residual(acc, res):
    return (acc + res,)


def _matmul_column_blocks(a, b4, blocks, out, *, tm, name, after=None):
    m, kdim = a.shape
    nb, _, cols = b4.shape
    tm = min(tm, m)
    assert m % tm == 0

    def body(j_ref, a_ref, b_ref, *rest):
        rest[-1][...] = jnp.dot(a_ref[...], b_ref[...], preferred_element_type=F32)

    extra = ([] if out is None else [out]) + ([] if after is None else [after])
    n_blocks = blocks.shape[0]
    return pl.pallas_call(
        body, name=name,
        grid_spec=pltpu.PrefetchScalarGridSpec(
            num_scalar_prefetch=1, grid=(n_blocks, m // tm),
            in_specs=[pl.BlockSpec((tm, kdim), lambda j, i, blk: (i, 0)),
                      pl.BlockSpec((None, kdim, cols), lambda j, i, blk: (blk[j], 0, 0))] + [ANY] * len(extra),
            out_specs=pl.BlockSpec((tm, cols), lambda j, i, blk: (i, blk[j]))),
        out_shape=jax.ShapeDtypeStruct((m, nb * cols), F32),
        input_output_aliases={} if out is None else {3: 0},
        compiler_params=_params(2),
    )(blocks, a, b4, *extra)


def _wgrad_half(a, b, core, *, theirs, row_sharded, tm, tn, name, add=None, after=None):
    kdim, m = a.shape
    n = b.shape[1]
    rs, cs = (m // N_CHIPS, n) if row_sharded else (m, n // N_CHIPS)
    rh = rs // 2
    tm, tn = min(tm, rh), min(tn, cs)
    assert rh % tm == 0 and cs % tn == 0, (name, rh, cs, tm, tn)
    mh, per = rh // tm, cs // tn
    has_add = add is not None

    def half(c):
        return 1 - c[0] if theirs else c[0]

    if row_sharded:
        grid = (n // tn, N_CHIPS * mh)
        a_spec = pl.BlockSpec((kdim, tm), lambda j, r, c: (0, ((r // mh) * 2 + half(c)) * mh + r % mh))
        o_spec = pl.BlockSpec((None, tm, tn), lambda j, r, c: (r // mh, r % mh, j))
    else:
        grid = (n // tn, mh)
        a_spec = pl.BlockSpec((kdim, tm), lambda j, r, c: (0, half(c) * mh + r))
        o_spec = pl.BlockSpec((None, tm, tn), lambda j, r, c: (j // per, r, j % per))
    b_spec = pl.BlockSpec((kdim, tn), lambda j, r, c: (0, j))

    def body(c_ref, a_ref, b_ref, *rest):
        o_ref = rest[-1]
        acc = lax.dot_general(a_ref[...].astype(BF16), b_ref[...].astype(BF16), (((0,), (0,)), ((), ())),
                              preferred_element_type=F32)
        if has_add:
            acc = acc + rest[0][...].astype(F32)
        o_ref[...] = acc.astype(BF16)

    operands = [a, b] + ([add] if has_add else []) + ([] if after is None else [after])
    return pl.pallas_call(
        body, name=name,
        grid_spec=pltpu.PrefetchScalarGridSpec(
            num_scalar_prefetch=1, grid=grid,
            in_specs=[a_spec, b_spec] + ([o_spec] if has_add else []) + ([] if after is None else [ANY]),
            out_specs=o_spec),
        out_shape=jax.ShapeDtypeStruct((N_CHIPS, rh, cs), BF16),
        compiler_params=_params(2),
    )(core, *operands)


def _rstd(x):
    return lax.rsqrt(jnp.mean(x * x, axis=-1, keepdims=True) + RMS_EPS)


def _rmsnorm(x, g, name):
    s, d = x.shape
    tr = _row_tile(s, 256)

    def body(x_ref, g_ref, o_ref):
        xv = x_ref[...]
        o_ref[...] = (xv * _rstd(xv) * g_ref[...]).astype(BF16)

    return pl.pallas_call(
        body, name=name, grid=(s // tr,),
        in_specs=[pl.BlockSpec((tr, d), lambda i: (i, 0)), pl.BlockSpec((1, d), lambda i: (0, 0))],
        out_specs=pl.BlockSpec((tr, d), lambda i: (i, 0)),
        out_shape=jax.ShapeDtypeStruct((s, d), BF16),
        compiler_params=_params(1),
    )(x, g)


def _rmsnorm_bwd(dh, x, g, dres, name):
    s, d = x.shape
    tr = _row_tile(s, 256)
    has_res = dres is not None

    def body(*refs):
        if has_res:
            dh_ref, x_ref, g_ref, res_ref, dx_ref, dxb_ref, dg_ref = refs
        else:
            dh_ref, x_ref, g_ref, dx_ref, dxb_ref, dg_ref = refs
        xv = x_ref[...]
        dhv = dh_ref[...].astype(F32)
        r = _rstd(xv)
        xn = xv * r
        dhg = dhv * g_ref[...]
        dx = r * (dhg - xn * jnp.mean(dhg * xn, axis=-1, keepdims=True))
        if has_res:
            dx = dx + res_ref[...]
        dx_ref[...] = dx
        dxb_ref[...] = dx.astype(BF16)
        part = jnp.sum(dhv * xn, axis=0, keepdims=True)

        @pl.when(pl.program_id(0) == 0)
        def _():
            dg_ref[...] = part

        @pl.when(pl.program_id(0) > 0)
        def _():
            dg_ref[...] += part

    row = pl.BlockSpec((tr, d), lambda i: (i, 0))
    vec = pl.BlockSpec((1, d), lambda i: (0, 0))
    return pl.pallas_call(
        body, name=name, grid=(s // tr,),
        in_specs=[row, row, vec] + ([row] if has_res else []),
        out_specs=[row, row, vec],
        out_shape=[jax.ShapeDtypeStruct((s, d), F32), jax.ShapeDtypeStruct((s, d), BF16), jax.ShapeDtypeStruct((1, d), F32)],
        compiler_params=_params(1),
    )(*([dh, x, g] + ([dres] if has_res else [])))


def _loss_head(x3, g, target):
    s, d = x3.shape
    tr = _row_tile(s, 256)

    def body(x_ref, g_ref, t_ref, dx_ref, dxb_ref, sq_ref, dg_ref):
        xv = x_ref[...]
        gv = g_ref[...]
        r = _rstd(xv)
        xn = xv * r
        err = xn * gv - t_ref[...]
        dy = err * (1.0 / d)
        dyg = dy * gv
        dx = r * (dyg - xn * jnp.mean(dyg * xn, axis=-1, keepdims=True))
        dx_ref[...] = dx
        dxb_ref[...] = dx.astype(BF16)
        sq = jnp.sum(jnp.sum(err * err, axis=1, keepdims=True), axis=0, keepdims=True)
        sq = jnp.broadcast_to(sq, (1, 128))
        part = jnp.sum(dy * xn, axis=0, keepdims=True)

        @pl.when(pl.program_id(0) == 0)
        def _():
            sq_ref[...] = sq
            dg_ref[...] = part

        @pl.when(pl.program_id(0) > 0)
        def _():
            sq_ref[...] += sq
            dg_ref[...] += part

    row = pl.BlockSpec((tr, d), lambda i: (i, 0))
    vec = pl.BlockSpec((1, d), lambda i: (0, 0))
    return pl.pallas_call(
        body, name="loss_head", grid=(s // tr,),
        in_specs=[row, vec, row],
        out_specs=[row, row, pl.BlockSpec((1, 128), lambda i: (0, 0)), vec],
        out_shape=[jax.ShapeDtypeStruct((s, d), F32), jax.ShapeDtypeStruct((s, d), BF16),
                   jax.ShapeDtypeStruct((1, 128), F32), jax.ShapeDtypeStruct((1, d), F32)],
        compiler_params=_params(1),
    )(x3, g, target)


def _rope_tables(s):
    inv = 1.0 / (ROPE_THETA ** (jnp.arange(0, HEAD_DIM, 2, dtype=F32) / HEAD_DIM))
    ang = jnp.arange(s, dtype=F32)[:, None] * inv[None, :]
    cos, sin = jnp.cos(ang), jnp.sin(ang)
    return jnp.concatenate([cos, cos], axis=1), jnp.concatenate([-sin, sin], axis=1)


def _swap_halves(t):
    return pltpu.roll(t, HEAD_DIM // 2, 1)


def _rope_fwd(z, cos_t, sin_t):
    s = z.shape[0]
    tr = _row_tile(s, 256)

    def body(zq_ref, zk_ref, zv_ref, c_ref, s_ref, q_ref, k_ref, v_ref):
        c, sn = c_ref[...], s_ref[...]
        for hd in range(N_Q_HEADS):
            cols = slice(hd * HEAD_DIM, (hd + 1) * HEAD_DIM)
            t = zq_ref[:, cols]
            q_ref[:, cols] = (t * c + _swap_halves(t) * sn).astype(BF16)
        for hd in range(N_KV_HEADS):
            cols = slice(hd * HEAD_DIM, (hd + 1) * HEAD_DIM)
            t = zk_ref[:, cols]
            k_ref[:, cols] = (t * c + _swap_halves(t) * sn).astype(BF16)
        v_ref[...] = zv_ref[...].astype(BF16)

    tab = pl.BlockSpec((tr, HEAD_DIM), lambda i: (i, 0))
    return pl.pallas_call(
        body, name="rope_fwd", grid=(s // tr,),
        in_specs=[pl.BlockSpec((tr, ATTN_WIDTH), lambda i: (i, Q_OFF // ATTN_WIDTH)),
                  pl.BlockSpec((tr, KV_WIDTH), lambda i: (i, K_OFF // KV_WIDTH)),
                  pl.BlockSpec((tr, KV_WIDTH), lambda i: (i, V_OFF // KV_WIDTH)), tab, tab],
        out_specs=[pl.BlockSpec((tr, ATTN_WIDTH), lambda i: (i, 0)), pl.BlockSpec((tr, KV_WIDTH), lambda i: (i, 0)),
                   pl.BlockSpec((tr, KV_WIDTH), lambda i: (i, 0))],
        out_shape=[jax.ShapeDtypeStruct((s, ATTN_WIDTH), BF16), jax.ShapeDtypeStruct((s, KV_WIDTH), BF16),
                   jax.ShapeDtypeStruct((s, KV_WIDTH), BF16)],
        compiler_params=_params(1),
    )(z, z, z, cos_t, sin_t)


def _rope_bwd(dq_rot, dk_rot, dv, cos_t, sin_t):
    s = dq_rot.shape[0]
    tr = _row_tile(s, 256)

    def body(dq_ref, dk_ref, dv_ref, c_ref, s_ref, oq_ref, ok_ref, ov_ref):
        c, sn = c_ref[...], s_ref[...]
        for hd in range(N_Q_HEADS):
            cols = slice(hd * HEAD_DIM, (hd + 1) * HEAD_DIM)
            t = dq_ref[:, cols]
            oq_ref[:, cols] = (t * c + _swap_halves(t * sn)).astype(BF16)
        for hd in range(N_KV_HEADS):
            cols = slice(hd * HEAD_DIM, (hd + 1) * HEAD_DIM)
            t = dk_ref[:, cols]
            ok_ref[:, cols] = (t * c + _swap_halves(t * sn)).astype(BF16)
        ov_ref[...] = dv_ref[...].astype(BF16)

    tab = pl.BlockSpec((tr, HEAD_DIM), lambda i: (i, 0))
    wide = pl.BlockSpec((tr, ATTN_WIDTH), lambda i: (i, 0))
    narrow = pl.BlockSpec((tr, KV_WIDTH), lambda i: (i, 0))
    return pl.pallas_call(
        body, name="rope_bwd", grid=(s // tr,),
        in_specs=[wide, narrow, narrow, tab, tab],
        out_specs=[wide, narrow, narrow],
        out_shape=[jax.ShapeDtypeStruct((s, ATTN_WIDTH), BF16), jax.ShapeDtypeStruct((s, KV_WIDTH), BF16),
                   jax.ShapeDtypeStruct((s, KV_WIDTH), BF16)],
        compiler_params=_params(1),
    )(dq_rot, dk_rot, dv, cos_t, sin_t)


def _swa_band(i, s):
    return pl.multiple_of(jnp.clip((i - 1) * BLOCK, 0, s - BAND), BLOCK)


def _swa_probs(q_ref, k_ref, sink_ref, kv, start, valid):
    cols = slice(kv * HEAD_DIM, (kv + 1) * HEAD_DIM)
    kb = k_ref[pl.ds(start, BAND), cols]
    heads = [kv * Q_GROUP + g for g in range(Q_GROUP)]
    qg = jnp.concatenate([q_ref[:, hd * HEAD_DIM:(hd + 1) * HEAD_DIM] for hd in heads], axis=0)
    sc = lax.dot_general(qg, kb, (((1,), (1,)), ((), ())), preferred_element_type=F32) * ATTN_SCALE
    sc = jnp.where(valid, sc, NEG_INF)
    sk = jnp.concatenate([jnp.full((BLOCK, 1), sink_ref[hd], F32) for hd in heads], axis=0)
    mx = jnp.maximum(jnp.max(sc, axis=1, keepdims=True), sk)
    e = jnp.exp(sc - mx)
    es = jnp.exp(sk - mx)
    inv = 1.0 / (jnp.sum(e, axis=1, keepdims=True) + es)
    return qg, kb, e * inv, es * inv


def _swa_valid(i, start):
    q_pos = i * BLOCK + lax.broadcasted_iota(jnp.int32, (BLOCK, 1), 0)
    q_pos = jnp.concatenate([q_pos] * Q_GROUP, axis=0)
    k_pos = start + lax.broadcasted_iota(jnp.int32, (1, BAND), 1)
    return jnp.abs(k_pos - q_pos) <= WINDOW


def _swa_fwd(q, k, v, sink):
    s = q.shape[0]
    assert s % BLOCK == 0 and s >= BAND

    def body(sink_ref, q_ref, k_ref, v_ref, o_ref):
        i = pl.program_id(0)
        start = _swa_band(i, s)
        valid = _swa_valid(i, start)
        for kv in range(N_KV_HEADS):
            _, _, p, _ = _swa_probs(q_ref, k_ref, sink_ref, kv, start, valid)
            vb = v_ref[pl.ds(start, BAND), kv * HEAD_DIM:(kv + 1) * HEAD_DIM]
            o = jnp.dot(p.astype(BF16), vb, preferred_element_type=F32)
            for g in range(Q_GROUP):
                hd = kv * Q_GROUP + g
                o_ref[:, hd * HEAD_DIM:(hd + 1) * HEAD_DIM] = o[g * BLOCK:(g + 1) * BLOCK].astype(BF16)

    whole = pl.BlockSpec((s, KV_WIDTH), lambda i: (0, 0))
    blk = pl.BlockSpec((BLOCK, ATTN_WIDTH), lambda i: (i, 0))
    return pl.pallas_call(
        body, name="swa_fwd", grid=(s // BLOCK,),
        in_specs=[pl.BlockSpec(memory_space=pltpu.SMEM), blk, whole, whole],
        out_specs=blk,
        out_shape=jax.ShapeDtypeStruct((s, ATTN_WIDTH), BF16),
        compiler_params=_params(1),
    )(sink, q, k, v)


def _swa_bwd(q, k, v, d_out, sink):
    s = q.shape[0]

    def body(sink_ref, q_ref, k_ref, v_ref, do_ref, dq_ref, dk_ref, dv_ref, dsink_ref):
        i = pl.program_id(0)

        @pl.when(i == 0)
        def _():
            dk_ref[...] = jnp.zeros_like(dk_ref)
            dv_ref[...] = jnp.zeros_like(dv_ref)
            dsink_ref[...] = jnp.zeros_like(dsink_ref)

        start = _swa_band(i, s)
        valid = _swa_valid(i, start)
        for kv in range(N_KV_HEADS):
            cols = slice(kv * HEAD_DIM, (kv + 1) * HEAD_DIM)
            qg, kb, p, p_sink = _swa_probs(q_ref, k_ref, sink_ref, kv, start, valid)
            vb = v_ref[pl.ds(start, BAND), cols]
            heads = [kv * Q_GROUP + g for g in range(Q_GROUP)]
            dog = jnp.concatenate([do_ref[:, hd * HEAD_DIM:(hd + 1) * HEAD_DIM] for hd in heads], axis=0)
            dp = lax.dot_general(dog, vb, (((1,), (1,)), ((), ())), preferred_element_type=F32)
            delta = jnp.sum(p * dp, axis=1, keepdims=True)
            ds = (p * (dp - delta) * ATTN_SCALE).astype(BF16)
            dqg = jnp.dot(ds, kb, preferred_element_type=F32)
            dk_ref[pl.ds(start, BAND), cols] += lax.dot_general(ds, qg, (((0,), (0,)), ((), ())), preferred_element_type=F32)
            dv_ref[pl.ds(start, BAND), cols] += lax.dot_general(p.astype(BF16), dog, (((0,), (0,)), ((), ())),
                                                                 preferred_element_type=F32)
            dsk = p_sink * delta
            for g, hd in enumerate(heads):
                dq_ref[:, hd * HEAD_DIM:(hd + 1) * HEAD_DIM] = dqg[g * BLOCK:(g + 1) * BLOCK]
                tot = jnp.sum(dsk[g * BLOCK:(g + 1) * BLOCK], axis=0, keepdims=True)
                dsink_ref[hd:hd + 1, :] -= jnp.broadcast_to(tot, (1, 128))

    whole = pl.BlockSpec((s, KV_WIDTH), lambda i: (0, 0))
    blk = pl.BlockSpec((BLOCK, ATTN_WIDTH), lambda i: (i, 0))
    return pl.pallas_call(
        body, name="swa_bwd", grid=(s // BLOCK,),
        in_specs=[pl.BlockSpec(memory_space=pltpu.SMEM), blk, whole, whole, blk],
        out_specs=[blk, whole, whole, pl.BlockSpec((N_Q_HEADS, 128), lambda i: (0, 0))],
        out_shape=[jax.ShapeDtypeStruct((s, ATTN_WIDTH), F32), jax.ShapeDtypeStruct((s, KV_WIDTH), F32),
                   jax.ShapeDtypeStruct((s, KV_WIDTH), F32), jax.ShapeDtypeStruct((N_Q_HEADS, 128), F32)],
        compiler_params=_params(1),
    )(sink, q, k, v, d_out)


CONV_CHUNK = 256


def _shift_rows(t, rows, down):
    n = t.shape[0]
    rolled = pltpu.roll(t, 1 if down else n - 1, 0)
    edge = 0 if down else n - 1
    return jnp.where(rows == edge, 0.0, rolled)


def _conv_specs(s):
    def z_spec(off):
        return pl.BlockSpec((s, CONV_CHUNK), lambda j, off=off: (0, off // CONV_CHUNK + j))
    chunk = pl.BlockSpec((s, CONV_CHUNK), lambda j: (0, j))
    w_spec = pl.BlockSpec((3, CONV_CHUNK), lambda j: (0, j))
    return z_spec(CU_OFF), z_spec(CB_OFF), z_spec(CC_OFF), chunk, w_spec


def _conv_fwd(z, conv_w):
    s = z.shape[0]
    cu_spec, cb_spec, cc_spec, chunk, w_spec = _conv_specs(s)

    def body(cu_ref, cb_ref, cc_ref, w_ref, o_ref):
        rows = lax.broadcasted_iota(jnp.int32, (s, 1), 0)
        t = cc_ref[...] * cu_ref[...]
        c3 = _shift_rows(t, rows, True) * w_ref[0:1, :] + t * w_ref[1:2, :] + _shift_rows(t, rows, False) * w_ref[2:3, :]
        o_ref[...] = (cb_ref[...] * c3).astype(BF16)

    return pl.pallas_call(
        body, name="conv_fwd", grid=(CONV_WIDTH // CONV_CHUNK,),
        in_specs=[cu_spec, cb_spec, cc_spec, w_spec],
        out_specs=chunk,
        out_shape=jax.ShapeDtypeStruct((s, CONV_WIDTH), BF16),
        compiler_params=_params(1),
    )(z, z, z, conv_w)


def _conv_bwd(z, conv_w, d_co):
    s = z.shape[0]
    cu_spec, cb_spec, cc_spec, chunk, w_spec = _conv_specs(s)

    def body(cu_ref, cb_ref, cc_ref, w_ref, d_ref, dcu_ref, dcb_ref, dcc_ref, dw_ref):
        rows = lax.broadcasted_iota(jnp.int32, (s, 1), 0)
        cu, cc = cu_ref[...], cc_ref[...]
        t = cc * cu
        t_dn, t_up = _shift_rows(t, rows, True), _shift_rows(t, rows, False)
        c3 = t_dn * w_ref[0:1, :] + t * w_ref[1:2, :] + t_up * w_ref[2:3, :]
        d = d_ref[...]
        dcb_ref[...] = (d * c3).astype(BF16)
        dc3 = d * cb_ref[...]
        dw_ref[0:1, :] = jnp.sum(dc3 * t_dn, axis=0, keepdims=True)
        dw_ref[1:2, :] = jnp.sum(dc3 * t, axis=0, keepdims=True)
        dw_ref[2:3, :] = jnp.sum(dc3 * t_up, axis=0, keepdims=True)
        dt = _shift_rows(dc3, rows, False) * w_ref[0:1, :] + dc3 * w_ref[1:2, :] + _shift_rows(dc3, rows, True) * w_ref[2:3, :]
        dcc_ref[...] = (dt * cu).astype(BF16)
        dcu_ref[...] = (dt * cc).astype(BF16)

    return pl.pallas_call(
        body, name="conv_bwd", grid=(CONV_WIDTH // CONV_CHUNK,),
        in_specs=[cu_spec, cb_spec, cc_spec, w_spec, chunk],
        out_specs=[chunk, chunk, chunk, w_spec],
        out_shape=[jax.ShapeDtypeStruct((s, CONV_WIDTH), BF16)] * 3 + [jax.ShapeDtypeStruct((3, CONV_WIDTH), F32)],
        compiler_params=_params(1),
    )(z, z, z, conv_w, d_co)


GATE_CHUNK = 512


def _gate_specs(s, d, tr):
    n_chunks = d // GATE_CHUNK
    za = pl.BlockSpec((tr, GATE_CHUNK), lambda j, i: (i, GL_OFF // GATE_CHUNK + j))
    zc = pl.BlockSpec((tr, GATE_CHUNK), lambda j, i: (i, GL_OFF // GATE_CHUNK + n_chunks + j))
    ba = pl.BlockSpec((1, GATE_CHUNK), lambda j, i: (0, j))
    bc = pl.BlockSpec((1, GATE_CHUNK), lambda j, i: (0, n_chunks + j))
    tile = pl.BlockSpec((tr, GATE_CHUNK), lambda j, i: (i, j))
    return za, zc, ba, bc, tile


def _gate_fwd(z, b_gate, ya, yc):
    s, d = ya.shape
    tr = _row_tile(s, 512)
    za, zc, ba, bc, tile = _gate_specs(s, d, tr)

    def body(za_ref, zc_ref, ba_ref, bc_ref, ya_ref, yc_ref, o_ref):
        ga = jax.nn.sigmoid(za_ref[...] + ba_ref[...])
        gc = jax.nn.sigmoid(zc_ref[...] + bc_ref[...])
        o_ref[...] = (ga * ya_ref[...] + gc * yc_ref[...]).astype(BF16)

    return pl.pallas_call(
        body, name="gate_fwd", grid=(d // GATE_CHUNK, s // tr),
        in_specs=[za, zc, ba, bc, tile, tile],
        out_specs=tile,
        out_shape=jax.ShapeDtypeStruct((s, d), BF16),
        compiler_params=_params(2),
    )(z, z, b_gate, b_gate, ya, yc)


def _gate_bwd(z, b_gate, ya, yc, dmix):
    s, d = ya.shape
    tr = _row_tile(s, 512)
    za, zc, ba, bc, tile = _gate_specs(s, d, tr)
    vec = pl.BlockSpec((1, GATE_CHUNK), lambda j, i: (0, j))

    def body(za_ref, zc_ref, ba_ref, bc_ref, ya_ref, yc_ref, dm_ref, dya_ref, dyc_ref, dla_ref, dlc_ref, dba_ref, dbc_ref):
        ga = jax.nn.sigmoid(za_ref[...] + ba_ref[...])
        gc = jax.nn.sigmoid(zc_ref[...] + bc_ref[...])
        dm = dm_ref[...]
        dya_ref[...] = (dm * ga).astype(BF16)
        dyc_ref[...] = (dm * gc).astype(BF16)
        dla = dm * ya_ref[...] * ga * (1.0 - ga)
        dlc = dm * yc_ref[...] * gc * (1.0 - gc)
        dla_ref[...] = dla.astype(BF16)
        dlc_ref[...] = dlc.astype(BF16)
        pa = jnp.sum(dla, axis=0, keepdims=True)
        pc = jnp.sum(dlc, axis=0, keepdims=True)

        @pl.when(pl.program_id(1) == 0)
        def _():
            dba_ref[...] = pa
            dbc_ref[...] = pc

        @pl.when(pl.program_id(1) > 0)
        def _():
            dba_ref[...] += pa
            dbc_ref[...] += pc

    big = jax.ShapeDtypeStruct((s, d), BF16)
    small = jax.ShapeDtypeStruct((1, d), F32)
    return pl.pallas_call(
        body, name="gate_bwd", grid=(d // GATE_CHUNK, s // tr),
        in_specs=[za, zc, ba, bc, tile, tile, tile],
        out_specs=[tile, tile, tile, tile, vec, vec],
        out_shape=[big, big, big, big, small, small],
        compiler_params=_params(2),
    )(z, z, b_gate, b_gate, ya, yc, dmix)


def _cross_probs(q_ref, kv_ref, hd):
    cols = slice(hd * HEAD_DIM, (hd + 1) * HEAD_DIM)
    qh = q_ref[:, cols]
    kh = kv_ref[:, cols]
    sc = lax.dot_general(qh, kh, (((1,), (1,)), ((), ())), preferred_element_type=F32) * ATTN_SCALE
    e = jnp.exp(sc - jnp.max(sc, axis=1, keepdims=True))
    return qh, kh, e * (1.0 / jnp.sum(e, axis=1, keepdims=True))


def _cross_fwd(qc, kvc):
    s = qc.shape[0]
    n_mem = kvc.shape[0]
    tq = _row_tile(s, 256)

    def body(q_ref, kv_ref, o_ref):
        for hd in range(MEM_HEADS):
            _, _, p = _cross_probs(q_ref, kv_ref, hd)
            vh = kv_ref[:, MEM_WIDTH + hd * HEAD_DIM:MEM_WIDTH + (hd + 1) * HEAD_DIM]
            o_ref[:, hd * HEAD_DIM:(hd + 1) * HEAD_DIM] = jnp.dot(p.astype(BF16), vh, preferred_element_type=F32).astype(BF16)

    return pl.pallas_call(
        body, name="cross_fwd", grid=(s // tq,),
        in_specs=[pl.BlockSpec((tq, MEM_WIDTH), lambda i: (i, 0)), pl.BlockSpec((n_mem, 2 * MEM_WIDTH), lambda i: (0, 0))],
        out_specs=pl.BlockSpec((tq, MEM_WIDTH), lambda i: (i, 0)),
        out_shape=jax.ShapeDtypeStruct((s, MEM_WIDTH), BF16),
        compiler_params=_params(1),
    )(qc, kvc)


def _cross_bwd(qc, kvc, d_out):
    s = qc.shape[0]
    n_mem = kvc.shape[0]
    tq = _row_tile(s, 256)

    def body(q_ref, kv_ref, do_ref, dq_ref, dkv_ref):
        @pl.when(pl.program_id(0) == 0)
        def _():
            dkv_ref[...] = jnp.zeros_like(dkv_ref)

        for hd in range(MEM_HEADS):
            cols = slice(hd * HEAD_DIM, (hd + 1) * HEAD_DIM)
            vcols = slice(MEM_WIDTH + hd * HEAD_DIM, MEM_WIDTH + (hd + 1) * HEAD_DIM)
            qh, kh, p = _cross_probs(q_ref, kv_ref, hd)
            doh = do_ref[:, cols]
            dp = lax.dot_general(doh, kv_ref[:, vcols], (((1,), (1,)), ((), ())), preferred_element_type=F32)
            ds = (p * (dp - jnp.sum(p * dp, axis=1, keepdims=True)) * ATTN_SCALE).astype(BF16)
            dq_ref[:, cols] = jnp.dot(ds, kh, preferred_element_type=F32).astype(BF16)
            dkv_ref[:, cols] += lax.dot_general(ds, qh, (((0,), (0,)), ((), ())), preferred_element_type=F32)
            dkv_ref[:, vcols] += lax.dot_general(p.astype(BF16), doh, (((0,), (0,)), ((), ())), preferred_element_type=F32)

    qspec = pl.BlockSpec((tq, MEM_WIDTH), lambda i: (i, 0))
    kvspec = pl.BlockSpec((n_mem, 2 * MEM_WIDTH), lambda i: (0, 0))
    return pl.pallas_call(
        body, name="cross_bwd", grid=(s // tq,),
        in_specs=[qspec, kvspec, qspec],
        out_specs=[qspec, kvspec],
        out_shape=[jax.ShapeDtypeStruct((s, MEM_WIDTH), BF16), jax.ShapeDtypeStruct((n_mem, 2 * MEM_WIDTH), F32)],
        compiler_params=_params(1),
    )(qc, kvc, d_out)


def _swiglu_fwd(up, gate):
    return up, (gate * jax.nn.sigmoid(gate)) * up


def _swiglu_bwd(d_act, gate, up):
    sg = jax.nn.sigmoid(gate)
    silu = gate * sg
    return d_act * up * (sg * (1.0 + gate * (1.0 - sg))), d_act * silu


GATHER_GROUPS = {"in": ("w_in", "conv_w"), "mid": ("w_attn_out", "w_conv_out", "w_o", "w_cq", "w_ckv", "w_co"),
                 "gate": ("w_gate",), "up": ("w_up",), "down": ("w_down",)}


def _local_step(xs, mems, target, small, fetch, reduce):
    s, d = xs.shape
    w4 = {}
    cos_t, sin_t = _rope_tables(s)

    def near(group, done, then, after):
        waits = [("direct", group)] + ([("pass_near", done), ("pass_far", done)] if done else [])
        starts = [("forward", group), ("pass_near", group)] + [("direct", g) for g in then]
        tok = fetch.step("gather_near_" + group, waits, starts, after)
        if done:
            w4.update(fetch.arrays(done))
        return tok

    def far(group, then, after):
        return fetch.step("gather_far_" + group, [("forward", group)], [("pass_far", group)] + [("direct", g) for g in then], after)

    def last(group, after):
        tok = fetch.step("gather_done_" + group, [("pass_near", group), ("pass_far", group)], [], after)
        w4.update(fetch.arrays(group))
        return tok

    h = _rmsnorm(xs, small["g_mix"], "norm_mix")
    slots_filled = [a for g in ("gate", "up", "down") for a in fetch.arrays(g).values()]
    chip_x, chip_y = reduce.place[0] // 2, reduce.place[0] % 2
    own_block = jnp.stack([2 * chip_x + chip_y]).astype(jnp.int32)
    near_blocks = jnp.stack([2 * (1 - chip_x) + chip_y, 2 * chip_x + (1 - chip_y)]).astype(jnp.int32)
    far_block = jnp.stack([2 * (1 - chip_x) + (1 - chip_y)]).astype(jnp.int32)
    z = _matmul_column_blocks(h, fetch.arrays("in")["w_in"], own_block, None, tm=512, name="in_proj_own")
    tok = near("in", None, ["mid"], [z] + slots_filled)
    tok = fetch.step("gather_near_done_in", [("pass_near", "in")], [], tok)
    z = _matmul_column_blocks(h, fetch.arrays("in")["w_in"], near_blocks, z, tm=512, name="in_proj_near", after=tok)
    tok = far("in", ["gate"], z)
    tok = fetch.step("gather_done_in", [("pass_far", "in")], [], tok)
    w4.update(fetch.arrays("in"))
    z = _matmul_column_blocks(h, w4["w_in"], far_block, z, tm=512, name="in_proj_far", after=tok)
    conv4 = w4["conv_w"]
    conv_w = conv4[:, :3, :].transpose(1, 0, 2).reshape(3, N_CHIPS * conv4.shape[2])
    c_in = w4["w_in"].shape[2]
    tok = near("mid", None, ["up"], z)
    q_rot, k_rot, v_b = _rope_fwd(z, cos_t, sin_t)
    attn = _swa_fwd(q_rot, k_rot, v_b, small["sink"])
    co = _conv_fwd(z, conv_w)
    tok = far("mid", [], attn)
    tok = near("gate", "mid", ["down"], tok)
    w_o = w4["w_o"].reshape(-1, w4["w_o"].shape[-1])
    c_d = w4["w_attn_out"].shape[2]
    ya = _matmul(attn, w4["w_attn_out"], mode="nn", tm=1024, tn=c_d, tk=ATTN_WIDTH, out_dtypes=[F32], name="attn_out_proj",
                 b_blocks=N_CHIPS, after=tok)
    yc = _matmul(co, w4["w_conv_out"], mode="nn", tm=1024, tn=c_d, tk=CONV_WIDTH, out_dtypes=[F32], name="conv_out_proj",
                 b_blocks=N_CHIPS)
    mix = _gate_fwd(z, small["b_gate"], ya, yc)
    x1 = _matmul(mix, w_o, mode="nn", tm=512, tn=1024, tk=d, out_dtypes=[F32], name="mix_out_proj", extras=[xs],
                 epilogue=_add_residual)
    tok = far("gate", [], x1)
    w_cq = w4["w_cq"].reshape(-1, w4["w_cq"].shape[-1])
    w_ckv = w4["w_ckv"].reshape(-1, w4["w_ckv"].shape[-1])
    hc = _rmsnorm(x1, small["g_cross"], "norm_cross")
    memn = _rmsnorm(mems, small["g_mem"], "norm_mem")
    qc = _matmul(hc, w_cq, mode="nn", tm=1024, tn=MEM_WIDTH, tk=d, out_dtypes=[BF16], name="cross_q_proj", after=tok)
    kvc = _matmul(memn, w_ckv, mode="nn", tm=256, tn=2 * MEM_WIDTH, tk=d, out_dtypes=[BF16], name="cross_kv_proj")
    oc = _cross_fwd(qc, kvc)
    x2 = _matmul(oc, w4["w_co"], mode="nn", tm=1024, tn=c_d, tk=MEM_WIDTH, out_dtypes=[F32], name="cross_out_proj",
                 extras=[x1], epilogue=_add_residual, b_blocks=N_CHIPS)
    hf = _rmsnorm(x2, small["g_ffn"], "norm_ffn")
    tok = near("up", "gate", [], hf)
    c_ff = w4["w_gate"].shape[2]
    gate = _matmul(hf, w4["w_gate"], mode="nn", tm=512, tn=c_ff, tk=d, out_dtypes=[F32], name="ffn_gate_proj", b_blocks=N_CHIPS,
                   after=tok)
    tok = far("up", [], gate)
    tok = near("down", "up", [], tok)
    up, act = _matmul(hf, w4["w_up"], mode="nn", tm=512, tn=c_ff, tk=d, out_dtypes=[F32, BF16], name="ffn_up_proj",
                      extras=[gate], epilogue=_swiglu_fwd, b_blocks=N_CHIPS, after=tok)
    tok = far("down", [], act)
    last("down", tok)
    w_down = w4["w_down"].reshape(-1, w4["w_down"].shape[-1])
    x3 = _matmul(act, w_down, mode="nn", tm=512, tn=512, tk=w_down.shape[0], out_dtypes=[F32], name="ffn_down_proj", extras=[x2],
                 epilogue=_add_residual)
    dx3, dx3b, sq, dg_final = _loss_head(x3, small["g_final"], target)

    da, du = _matmul(dx3b, w_down, mode="nt", tm=512, tn=c_ff, tk=d, out_dtypes=[BF16, BF16], name="ffn_down_bwd",
                     extras=[gate, up], epilogue=_swiglu_bwd)
    core = reduce.core
    ffn_shape = dict(row_sharded=False, tm=512, tn=c_ff)
    g_down = _matmul(act, dx3b, mode="tn", tm=c_ff, tn=1024, tk=s, out_dtypes=[BF16], name="ffn_down_wgrad")
    tok = reduce.add("down", {"w_down": g_down}, da)
    t_gate = _wgrad_half(hf, da, core, theirs=True, name="ffn_gate_wgrad_theirs", after=tok, **ffn_shape)
    tok = reduce.step("down", t_gate)
    t_up = _wgrad_half(hf, du, core, theirs=True, name="ffn_up_wgrad_theirs", after=tok, **ffn_shape)
    tok = reduce.send("ffn", {"w_gate": t_gate, "w_up": t_up}, dx3b)
    dhf = _matmul(da, w4["w_gate"], mode="nt", tm=512, tn=1024, tk=N_CHIPS * c_ff, out_dtypes=[F32], name="ffn_gate_bwd", b_blocks=N_CHIPS,
                  after=tok)
    got = reduce.received("ffn", dhf)
    p_gate = _wgrad_half(hf, da, core, theirs=False, name="ffn_gate_wgrad_mine", add=got["w_gate"], **ffn_shape)
    p_up = _wgrad_half(hf, du, core, theirs=False, name="ffn_up_wgrad_mine", add=got["w_up"], **ffn_shape)
    tok = reduce.add_parts("ffn", {"w_gate": p_gate, "w_up": p_up})
    dhf = _matmul(du, w4["w_up"], mode="nt", tm=512, tn=1024, tk=N_CHIPS * c_ff, out_dtypes=[F32], name="ffn_up_bwd", extras=[dhf],
                  epilogue=_add_residual, b_blocks=N_CHIPS, after=tok)
    tok = reduce.step("down", dhf)
    dx2, dx2b, dg_ffn = _rmsnorm_bwd(dhf, x2, small["g_ffn"], dx3, "norm_ffn_bwd")

    d_oc = _matmul(dx2b, w4["w_co"], mode="nt", tm=1024, tn=MEM_WIDTH, tk=d, out_dtypes=[BF16], name="cross_out_bwd",
                   b_blocks=N_CHIPS, after=tok)
    g_co = _matmul(oc, dx2b, mode="tn", tm=MEM_WIDTH, tn=c_d, tk=s, out_dtypes=[BF16], name="cross_out_wgrad", out_blocks=N_CHIPS)
    tok = reduce.step("down", g_co)
    dqc, dkvc = _cross_bwd(qc, kvc, d_oc)
    g_cq = _matmul(hc, dqc, mode="tn", tm=1024, tn=MEM_WIDTH, tk=s, out_dtypes=[BF16], name="cross_q_wgrad", after=tok)
    dhc = _matmul(dqc, w_cq, mode="nt", tm=1024, tn=1024, tk=MEM_WIDTH, out_dtypes=[F32], name="cross_q_bwd")
    g_ckv = _matmul(memn, dkvc, mode="tn", tm=1024, tn=2 * MEM_WIDTH, tk=mems.shape[0], out_dtypes=[BF16], name="cross_kv_wgrad")
    dmemn = _matmul(dkvc, w_ckv, mode="nt", tm=256, tn=1024, tk=2 * MEM_WIDTH, out_dtypes=[F32], name="cross_kv_bwd")
    _, _, dg_mem = _rmsnorm_bwd(dmemn, mems, small["g_mem"], None, "norm_mem_bwd")
    dx1, dx1b, dg_cross = _rmsnorm_bwd(dhc, x1, small["g_cross"], dx2, "norm_cross_bwd")

    dmix = _matmul(dx1b, w_o, mode="nt", tm=512, tn=1024, tk=d, out_dtypes=[F32], name="mix_out_bwd")
    g_o = _matmul(mix, dx1b, mode="tn", tm=1024, tn=1024, tk=s, out_dtypes=[BF16], name="mix_out_wgrad")
    dya, dyc, dgl_a, dgl_c, db_a, db_c = _gate_bwd(z, small["b_gate"], ya, yc, dmix)
    d_attn = _matmul(dya, w4["w_attn_out"], mode="nt", tm=1024, tn=ATTN_WIDTH, tk=d, out_dtypes=[BF16], name="attn_out_bwd",
                     b_blocks=N_CHIPS)
    g_ao = _matmul(attn, dya, mode="tn", tm=ATTN_WIDTH, tn=c_d, tk=s, out_dtypes=[BF16], name="attn_out_wgrad", out_blocks=N_CHIPS)
    d_co = _matmul(dyc, w4["w_conv_out"], mode="nt", tm=1024, tn=CONV_WIDTH, tk=d, out_dtypes=[F32], name="conv_out_bwd",
                   b_blocks=N_CHIPS)
    g_cvo = _matmul(co, dyc, mode="tn", tm=CONV_WIDTH, tn=c_d, tk=s, out_dtypes=[BF16], name="conv_out_wgrad", out_blocks=N_CHIPS)
    tok = reduce.step("ffn", g_cvo)
    tok = reduce.add("mid", {"w_co": g_co, "w_cq": g_cq, "w_ckv": g_ckv, "w_o": g_o, "w_attn_out": g_ao, "w_conv_out": g_cvo}, tok)
    dcu, dcb, dcc, d_conv_w = _conv_bwd(z, conv_w, d_co)
    dq_rot, dk_rot, dv, dsink = _swa_bwd(q_rot, k_rot, v_b, d_attn, small["sink"])
    tok = reduce.step("mid", dq_rot)
    dq, dk, dvb = _rope_bwd(dq_rot, dk_rot, dv, cos_t, sin_t)
    dz = jnp.concatenate([dq, dk, dvb, dcu, dcb, dcc, dgl_a, dgl_c], axis=1)
    in_shape = dict(row_sharded=False, tm=512, tn=c_in)
    t_in = _wgrad_half(h, dz, core, theirs=True, name="in_proj_wgrad_theirs", after=tok, **in_shape)
    tok = reduce.send("in", {"w_in": t_in}, dk)
    tok = reduce.step("ffn", tok)
    tok = reduce.step("mid", tok)
    got = reduce.received("in", tok)
    p_in = _wgrad_half(h, dz, core, theirs=False, name="in_proj_wgrad_mine", add=got["w_in"], **in_shape)
    tok = reduce.add_parts("in", {"w_in": p_in})
    dh = _matmul(dz, w4["w_in"], mode="nt", tm=512, tn=512, tk=N_CHIPS * c_in, out_dtypes=[F32], name="in_proj_bwd", b_blocks=N_CHIPS,
                 after=tok)
    tok = reduce.step("mid", dh)
    grad_x, _, dg_mix = _rmsnorm_bwd(dh, xs, small["g_mix"], dx1, "norm_mix_bwd")

    small_grads = {
        "g_mix": dg_mix, "sink": dsink[:, 0], "b_gate": jnp.concatenate([db_a, db_c], axis=1), "g_cross": dg_cross,
        "g_mem": dg_mem, "g_ffn": dg_ffn, "g_final": dg_final, "conv_w": d_conv_w,
    }
    return sq, grad_x, small_grads


def _pair_sum(g4, ra, core, name):
    nb, rs, cs = g4.shape
    rh = rs // 2
    tr = _row_tile(rh, 256)
    per = rh // tr

    def body(c_ref, g_ref, r_ref, o_ref):
        o_ref[...] = (g_ref[...].astype(F32) + r_ref[...].astype(F32)).astype(BF16)

    plain = pl.BlockSpec((None, tr, cs), lambda j, i, c: (j, i, 0))
    return pl.pallas_call(
        body, name=name,
        grid_spec=pltpu.PrefetchScalarGridSpec(
            num_scalar_prefetch=1, grid=(nb, per),
            in_specs=[pl.BlockSpec((None, tr, cs), lambda j, i, c: (j, c[0] * per + i, 0)), plain],
            out_specs=plain),
        out_shape=jax.ShapeDtypeStruct((nb, rh, cs), BF16),
        compiler_params=_params(2),
    )(core, g4, ra)


def _quad_sum(parts, rc, place, name):
    _, rh, cs = parts.shape
    tr = _row_tile(rh, 256)
    per = rh // tr

    def body(p_ref, own_ref, r_ref, o_ref):
        acc = own_ref[...].astype(F32)
        for j in range(rc.shape[0]):
            acc = acc + r_ref[j].astype(F32)
        o_ref[...] = acc

    return pl.pallas_call(
        body, name=name,
        grid_spec=pltpu.PrefetchScalarGridSpec(
            num_scalar_prefetch=1, grid=(per,),
            in_specs=[pl.BlockSpec((None, tr, cs), lambda i, p: (p[0], i, 0)),
                      pl.BlockSpec((rc.shape[0], tr, cs), lambda i, p: (0, i, 0))],
            out_specs=pl.BlockSpec((tr, cs), lambda i, p: (p[1] * per + i, 0))),
        out_shape=jax.ShapeDtypeStruct((2 * rh, cs), F32),
        compiler_params=_params(1),
    )(place, parts, rc)


def _cast_to_slot(w, place, dtype, name, after=None):
    rows, cols = w.shape
    tr = _row_tile(rows, 256)

    def body(p_ref, w_ref, *rest):
        o_ref = rest[-1]
        o_ref[...] = w_ref[...].astype(dtype)

    return pl.pallas_call(
        body, name=name,
        grid_spec=pltpu.PrefetchScalarGridSpec(
            num_scalar_prefetch=1, grid=(rows // tr,),
            in_specs=[pl.BlockSpec((tr, cols), lambda i, p: (i, 0))] + ([] if after is None else [ANY]),
            out_specs=pl.BlockSpec((None, tr, cols), lambda i, p: (p[0], i, 0))),
        out_shape=jax.ShapeDtypeStruct((N_CHIPS, rows, cols), dtype),
        compiler_params=_params(1),
    )(place, w, *([] if after is None else [after]))


def _adamw(w, g, m, v, name, after=None):
    rows, cols = w.shape
    tr = _row_tile(rows, 256)

    def body(w_ref, g_ref, m_ref, v_ref, *rest):
        go_ref, d_ref, nm_ref, nv_ref = rest[-4:]
        gv = g_ref[...]
        go_ref[...] = gv
        nm = ADAM_B1 * m_ref[...] + (1.0 - ADAM_B1) * gv
        nv = ADAM_B2 * v_ref[...] + (1.0 - ADAM_B2) * (gv * gv)
        m_hat = nm / ADAM_C1
        v_hat = nv / ADAM_C2
        d_ref[...] = -ADAM_LR * (m_hat / (jnp.sqrt(v_hat) + ADAM_EPS) + ADAM_WD * w_ref[...])
        nm_ref[...] = nm
        nv_ref[...] = nv

    tile = pl.BlockSpec((tr, cols), lambda i: (i, 0))
    shape = jax.ShapeDtypeStruct((rows, cols), F32)
    return pl.pallas_call(
        body, name=name, grid=(rows // tr,),
        in_specs=[tile] * 4 + ([] if after is None else [ANY]), out_specs=[tile] * 4, out_shape=[shape] * 4,
        compiler_params=_params(1),
    )(w, g, m, v, *([] if after is None else [after]))


def _mesh_pos():
    return lax.axis_index("x"), lax.axis_index("y"), lax.axis_index("c")


def _other_chips(x, y):
    return [(1 - x, y), (x, 1 - y), (1 - x, 1 - y)]


def _half_rows(ref, which):
    rh = ref.shape[-2] // 2
    return ref.at[pl.ds(which * rh, rh), :]


def _remote(src, dst, send_sems, recv_sems, sem, to):
    return pltpu.make_async_remote_copy(src_ref=src, dst_ref=dst, send_sem=send_sems.at[sem], recv_sem=recv_sems.at[sem],
                                        device_id=to, device_id_type=MESH)


HBM = pl.BlockSpec(memory_space=pltpu.HBM)
SEM = pl.BlockSpec(memory_space=pltpu.SEMAPHORE)
DATAFLOW_EFFECT = pltpu.SideEffectType.DATAFLOW_SIDE_EFFECTING


def _in_hbm(arrays):
    return [pltpu.with_memory_space_constraint(a, pltpu.HBM) for a in arrays]


def _hbm_like(arrays):
    return [pltpu.HBM(a.shape, a.dtype) for a in arrays]


GATHER_COPIES_PER_ARRAY = {"direct": 2, "forward": 2, "pass_near": 2, "pass_far": 1}


def _gather_copies(kind, refs, x, y, c):
    me, near_x, near_y, far = 2 * x + y, 2 * (1 - x) + y, 2 * x + (1 - y), 2 * (1 - x) + (1 - y)
    to_x, to_y, sibling = (1 - x, y, c), (x, 1 - y, c), (x, y, 1 - c)
    out = []
    for ref in refs:
        rh = ref.shape[1] // 2
        rq = rh // 2

        def half(chip, ref=ref, rh=rh):
            return ref.at[chip, pl.ds(c * rh, rh), :]

        def quarter(chip, q, ref=ref, rh=rh, rq=rq):
            return ref.at[chip, pl.ds(c * rh + q * rq, rq), :]

        if kind == "direct":
            out += [(half(me), half(me), to_x), (half(me), half(me), to_y)]
        elif kind == "forward":
            out += [(quarter(near_x, 0), quarter(near_x, 0), to_y), (quarter(near_y, 1), quarter(near_y, 1), to_x)]
        elif kind == "pass_near":
            out += [(half(near_x), half(near_x), sibling), (half(near_y), half(near_y), sibling)]
        else:
            assert kind == "pass_far"
            out += [(half(far), half(far), sibling)]
    return out


def _gather_step(name, bufs, waits, starts, after):
    nb, nw, ns = len(bufs), len(waits), len(starts)
    after = [] if after is None else list(after) if isinstance(after, (list, tuple)) else [after]
    n_after = len(after)

    def body(*refs):
        ins = refs[:nb]
        wait_sems = refs[nb:nb + 2 * nw]
        start_sems = refs[nb + 2 * nw + n_after:nb + 2 * nw + n_after + 2 * ns]
        token = refs[-1]
        x, y, c = _mesh_pos()
        for j, (kind, idxs, _, _) in enumerate(waits):
            for i, (s_ref, d_ref, to) in enumerate(_gather_copies(kind, [ins[t] for t in idxs], x, y, c)):
                came = _remote(s_ref, d_ref, wait_sems[2 * j], wait_sems[2 * j + 1], i, to)
                came.wait_recv()
                came.wait_send()
        for j, (kind, idxs) in enumerate(starts):
            for i, (s_ref, d_ref, to) in enumerate(_gather_copies(kind, [ins[t] for t in idxs], x, y, c)):
                _remote(s_ref, d_ref, start_sems[2 * j], start_sems[2 * j + 1], i, to).start()
        token[...] = jnp.zeros_like(token)

    sems = []
    for kind, idxs in starts:
        sems += [pltpu.SemaphoreType.DMA((GATHER_COPIES_PER_ARRAY[kind] * len(idxs),))] * 2
    operands = _in_hbm(bufs) + [sem for w in waits for sem in w[2:]] + after
    outs = pl.pallas_call(
        body, name=name,
        in_specs=[HBM] * nb + [SEM] * (2 * nw) + [ANY] * n_after,
        out_specs=[SEM] * (2 * ns) + [HBM] * nb + [pl.BlockSpec(memory_space=pltpu.VMEM)],
        out_shape=sems + _hbm_like(bufs) + [jax.ShapeDtypeStruct((8, 128), F32)],
        input_output_aliases={i: 2 * ns + i for i in range(nb)},
        compiler_params=pltpu.CompilerParams(has_side_effects=DATAFLOW_EFFECT),
    )(*operands)
    return outs[2 * ns:2 * ns + nb], [(outs[2 * j], outs[2 * j + 1]) for j in range(ns)], outs[-1]


class _Gather:
    def __init__(self, groups):
        self.groups = groups
        self.bufs = {}
        self.in_flight = {}

    def put(self, slotted):
        self.bufs.update(slotted)

    def step(self, name, waits, starts, after=None):
        names = []
        for _, group in list(waits) + list(starts):
            names += [n for n in self.groups[group] if n not in names]
        index = {n: i for i, n in enumerate(names)}

        def members(group):
            return [index[n] for n in self.groups[group]]

        wait_args = [(kind, members(group)) + self.in_flight.pop((kind, group)) for kind, group in waits]
        start_args = [(kind, members(group)) for kind, group in starts]
        bufs, sems, token = _gather_step(name, [self.bufs[n] for n in names], wait_args, start_args, after)
        self.bufs.update(zip(names, bufs))
        for (kind, group), pair in zip(starts, sems):
            self.in_flight[(kind, group)] = pair
        return token

    def arrays(self, group):
        return {n: self.bufs[n] for n in self.groups[group]}


def _sibling_halves_copies(srcs, dsts, x, y, c):
    out = []
    for s_ref, d_ref in zip(srcs, dsts, strict=True):
        rh = s_ref.shape[1] // 2
        out.append((s_ref.at[:, pl.ds((1 - c) * rh, rh), :], d_ref, (x, y, 1 - c)))
    return out


def _to_sibling_copies(srcs, dsts, x, y, c):
    return [(s_ref, d_ref, (x, y, 1 - c)) for s_ref, d_ref in zip(srcs, dsts, strict=True)]


def _chip_copies(srcs, dsts, x, y, c):
    out = []
    for s_ref, d_ref in zip(srcs, dsts, strict=True):
        for k, (px, py) in enumerate(_other_chips(x, y)):
            out.append((s_ref.at[2 * px + py], d_ref.at[k], (px, py, c)))
    return out


def _join_copies(srcs, dsts, x, y, c):
    out = []
    for s_ref in srcs:
        mine = _half_rows(s_ref, c)
        out.append((mine, mine, (x, y, 1 - c)))
    return out


def _exchange_start(copies_fn, n_copies, srcs, fresh, after, name):
    ns, nb = len(srcs), len(srcs) + len(fresh)

    def body(*refs):
        bufs, send, recv, token = refs[:nb], refs[nb + 1], refs[nb + 2], refs[-1]
        x, y, c = _mesh_pos()
        for i, (s_ref, d_ref, to) in enumerate(copies_fn(bufs[:ns], bufs[ns:] if fresh else bufs[:ns], x, y, c)):
            _remote(s_ref, d_ref, send, recv, i, to).start()
        token[...] = jnp.zeros_like(token)

    sems = [pltpu.SemaphoreType.DMA((n_copies,))] * 2
    outs = pl.pallas_call(
        body, name=name,
        in_specs=[HBM] * nb + [ANY], out_specs=[SEM, SEM] + [HBM] * nb + [pl.BlockSpec(memory_space=pltpu.VMEM)],
        out_shape=sems + _hbm_like(list(srcs) + list(fresh)) + [jax.ShapeDtypeStruct((8, 128), F32)],
        input_output_aliases={i: 2 + i for i in range(nb)},
        compiler_params=pltpu.CompilerParams(has_side_effects=DATAFLOW_EFFECT),
    )(*_in_hbm(list(srcs) + list(fresh)), after)
    return outs[0], outs[1], outs[2:2 + ns], outs[2 + ns:2 + nb], outs[-1]


def _exchange_done(copies_fn, srcs, fresh, send, recv, after, name):
    ns, nb = len(srcs), len(srcs) + len(fresh)

    def body(*refs):
        bufs, send_in, recv_in = refs[:nb], refs[nb], refs[nb + 1]
        x, y, c = _mesh_pos()
        for i, (s_ref, d_ref, to) in enumerate(copies_fn(bufs[:ns], bufs[ns:] if fresh else bufs[:ns], x, y, c)):
            came = _remote(s_ref, d_ref, send_in, recv_in, i, to)
            came.wait_send()
            came.wait_recv()

    outs = pl.pallas_call(
        body, name=name,
        in_specs=[HBM] * nb + [SEM, SEM, ANY], out_specs=[HBM] * nb,
        out_shape=_hbm_like(list(srcs) + list(fresh)),
        input_output_aliases={i: i for i in range(nb)},
        compiler_params=pltpu.CompilerParams(has_side_effects=DATAFLOW_EFFECT),
    )(*_in_hbm(list(srcs) + list(fresh)), send, recv, after)
    return outs[:ns], outs[ns:]


class _Reduce:
    def __init__(self, place, core, shards, mom_m, mom_v):
        self.place, self.core = place, core
        self.shards, self.mom_m, self.mom_v = shards, mom_m, mom_v
        self.state = {}
        self.results = {}

    def add(self, group, grads, after):
        names = list(grads)
        g4s = [g.reshape((N_CHIPS, -1, g.shape[-1])) if g.ndim == 2 else g for g in grads.values()]
        fresh = [lax.empty((N_CHIPS, g.shape[1] // 2, g.shape[2]), BF16) for g in g4s]
        send, recv, g4s, fresh, token = _exchange_start(_sibling_halves_copies, len(names), g4s, fresh, after,
                                                        "pair_start_" + group)
        self.state[group] = (0, names, send, recv, g4s, fresh)
        return token

    def send(self, group, theirs, after):
        names, srcs = list(theirs), list(theirs.values())
        fresh = [lax.empty(s.shape, BF16) for s in srcs]
        send, recv, srcs, fresh, token = _exchange_start(_to_sibling_copies, len(names), srcs, fresh, after, "pair_start_" + group)
        self.state[group] = ("sent", names, send, recv, srcs, fresh)
        return token

    def received(self, group, after):
        stage, names, send, recv, srcs, fresh = self.state.pop(group)
        assert stage == "sent"
        _, got = _exchange_done(_to_sibling_copies, srcs, fresh, send, recv, after, "pair_done_" + group)
        return dict(zip(names, got))

    def add_parts(self, group, parts):
        names, srcs = list(parts), list(parts.values())
        fresh = [lax.empty((N_CHIPS - 1,) + p.shape[1:], BF16) for p in srcs]
        send, recv, srcs, fresh, token = _exchange_start(_chip_copies, 3 * len(names), srcs, fresh, self.core, "chips_start_" + group)
        self.state[group] = (1, names, send, recv, srcs, fresh)
        return token

    def step(self, group, after):
        stage, names, send, recv, srcs, fresh = self.state[group]
        if stage == 0:
            g4s, ras = _exchange_done(_sibling_halves_copies, srcs, fresh, send, recv, after, "pair_done_" + group)
            parts = [_pair_sum(g, r, self.core, "pair_sum_" + n) for g, r, n in zip(g4s, ras, names)]
            fresh = [lax.empty((N_CHIPS - 1,) + p.shape[1:], BF16) for p in parts]
            send, recv, parts, fresh, token = _exchange_start(_chip_copies, 3 * len(names), parts, fresh, self.core,
                                                              "chips_start_" + group)
            self.state[group] = (1, names, send, recv, parts, fresh)
            return token
        if stage == 1:
            parts, rcs = _exchange_done(_chip_copies, srcs, fresh, send, recv, after, "chips_done_" + group)
            wholes = [_quad_sum(p, r, self.place, "quad_sum_" + n) for p, r, n in zip(parts, rcs, names)]
            send, recv, wholes, _, token = _exchange_start(_join_copies, len(names), wholes, [], self.core, "join_start_" + group)
            self.state[group] = (2, names, send, recv, wholes, [])
            return token
        assert stage == 2
        wholes, _ = _exchange_done(_join_copies, srcs, [], send, recv, after, "join_done_" + group)
        token = None
        for n, g in zip(names, wholes):
            self.results[n] = _adamw(self.shards[n], g, self.mom_m[n], self.mom_v[n], "adamw_" + n, after=token)
            token = self.results[n][1]
        del self.state[group]
        return token


N_DEV = 8


def _all_reduce_small(v):
    def body(v_ref, o_ref, slots, send_sems, recv_sems):
        x, y, c = _mesh_pos()
        me = 4 * x + 2 * y + c
        slots[me] = v_ref[...]
        peers = []
        for r in range(1, N_DEV):
            fx, fy, fc = (r >> 2) & 1, (r >> 1) & 1, r & 1
            peers.append((x + fx - 2 * x * fx, y + fy - 2 * y * fy, c + fc - 2 * c * fc))
        sends = []
        for r, peer in enumerate(peers):
            cp = _remote(v_ref, slots.at[me], send_sems, recv_sems, r, peer)
            cp.start()
            sends.append(cp)
        for r, (px, py, pc) in enumerate(peers):
            landed = slots.at[4 * px + 2 * py + pc]
            _remote(landed, landed, send_sems, recv_sems, r, (px, py, pc)).wait_recv()
        for cp in sends:
            cp.wait_send()
        acc = slots[0]
        for i in range(1, N_DEV):
            acc = acc + slots[i]
        o_ref[...] = acc

    vm = pl.BlockSpec(memory_space=pltpu.VMEM)
    return pl.pallas_call(
        body, name="small_grads_all_reduce",
        in_specs=[vm], out_specs=vm,
        out_shape=jax.ShapeDtypeStruct(v.shape, v.dtype),
        scratch_shapes=[pltpu.VMEM((N_DEV,) + v.shape, v.dtype), pltpu.SemaphoreType.DMA((N_DEV - 1,)),
                        pltpu.SemaphoreType.DMA((N_DEV - 1,))],
    )(v)


MATRICES = ("w_in", "w_attn_out", "w_conv_out", "w_o", "w_cq", "w_ckv", "w_co", "w_gate", "w_up", "w_down")
VECTORS = ("g_mix", "b_gate", "g_cross", "g_mem", "g_ffn", "g_final", "conv_w", "sink")
WEIGHT_ORDER = ("g_mix", "w_in", "sink", "conv_w", "b_gate", "w_attn_out", "w_conv_out", "w_o", "g_cross", "g_mem", "w_cq",
                "w_ckv", "w_co", "g_ffn", "w_gate", "w_up", "w_down", "g_final")
CONV_PAD_ROWS = 32
SMALL_ROWS = 8


def _pack(pieces):
    flat = jnp.concatenate([p.reshape(-1) for p in pieces])
    lane_group = SMALL_ROWS * 128
    total = -(-flat.shape[0] // lane_group) * lane_group
    flat = jnp.pad(flat, (0, total - flat.shape[0]))
    return flat.reshape(SMALL_ROWS, total // SMALL_ROWS), [p.size for p in pieces]


def _unpack(packed, pieces):
    flat = packed.reshape(-1)
    out, off = [], 0
    for p in pieces:
        out.append(flat[off:off + p.size].reshape(p.shape))
        off += p.size
    return out


def kernel(x, mem, g_mix, w_in, sink, conv_w, b_gate, w_attn_out, w_conv_out, w_o, g_cross, g_mem, w_cq, w_ckv, w_co, g_ffn, w_gate, w_up, w_down, g_final, loss_target, m_g_mix, m_w_in, m_sink, m_conv_w, m_b_gate, m_w_attn_out, m_w_conv_out, m_w_o, m_g_cross, m_g_mem, m_w_cq, m_w_ckv, m_w_co, m_g_ffn, m_w_gate, m_w_up, m_w_down, m_g_final, v_g_mix, v_w_in, v_sink, v_conv_w, v_b_gate, v_w_attn_out, v_w_conv_out, v_w_o, v_g_cross, v_g_mem, v_w_cq, v_w_ckv, v_w_co, v_g_ffn, v_w_gate, v_w_up, v_w_down, v_g_final):
    given = dict(g_mix=g_mix, w_in=w_in, sink=sink, conv_w=conv_w, b_gate=b_gate, w_attn_out=w_attn_out, w_conv_out=w_conv_out,
                 w_o=w_o, g_cross=g_cross, g_mem=g_mem, w_cq=w_cq, w_ckv=w_ckv, w_co=w_co, g_ffn=g_ffn, w_gate=w_gate, w_up=w_up,
                 w_down=w_down, g_final=g_final)
    mom_m = dict(g_mix=m_g_mix, w_in=m_w_in, sink=m_sink, conv_w=m_conv_w, b_gate=m_b_gate, w_attn_out=m_w_attn_out,
                 w_conv_out=m_w_conv_out, w_o=m_w_o, g_cross=m_g_cross, g_mem=m_g_mem, w_cq=m_w_cq, w_ckv=m_w_ckv, w_co=m_w_co,
                 g_ffn=m_g_ffn, w_gate=m_w_gate, w_up=m_w_up, w_down=m_w_down, g_final=m_g_final)
    mom_v = dict(g_mix=v_g_mix, w_in=v_w_in, sink=v_sink, conv_w=v_conv_w, b_gate=v_b_gate, w_attn_out=v_w_attn_out,
                 w_conv_out=v_w_conv_out, w_o=v_w_o, g_cross=v_g_cross, g_mem=v_g_mem, w_cq=v_w_cq, w_ckv=v_w_ckv, w_co=v_w_co,
                 g_ffn=v_g_ffn, w_gate=v_w_gate, w_up=v_w_up, w_down=v_w_down, g_final=v_g_final)
    xs, mems, target = x[0], mem[0], loss_target[0]
    d_model = xs.shape[1]
    chip = 2 * lax.axis_index("x") + lax.axis_index("y")
    core = jnp.reshape(lax.axis_index("c"), (1,)).astype(jnp.int32)
    place = jnp.stack([chip, lax.axis_index("c")]).astype(jnp.int32)

    shards = {n: given[n][0] for n in MATRICES}
    conv_cols = conv_w.shape[2]
    conv_pad = jnp.pad(conv_w[0], ((0, CONV_PAD_ROWS - conv_w.shape[1]), (0, 0)))
    fetch = _Gather(GATHER_GROUPS)
    first = {"w_in": _cast_to_slot(shards["w_in"], place, BF16, "to_slot_w_in"),
             "conv_w": _cast_to_slot(conv_pad, place, F32, "to_slot_conv_w")}
    fetch.put(first)
    tok = fetch.step("gather_start", [], [("direct", "in")])
    fetch.put({n: _cast_to_slot(shards[n], place, BF16, "to_slot_" + n, after=tok) for n in MATRICES if n != "w_in"})
    small = {n: given[n] for n in ("g_mix", "b_gate", "g_cross", "g_mem", "g_ffn")}
    small["g_final"] = g_final[None]
    small["sink"] = sink[0]

    reduce = _Reduce(place, core, shards, {n: mom_m[n][0] for n in MATRICES}, {n: mom_v[n][0] for n in MATRICES})
    sq, grad_x, small_grads = _local_step(xs, mems, target, small, fetch, reduce)

    loss_part = 0.5 * sq[0:1, 0:1] / d_model
    pieces = [small_grads[n] for n in VECTORS] + [loss_part]
    packed, _ = _pack(pieces)
    summed = _unpack(_all_reduce_small(packed), pieces)
    loss = summed[-1][0, 0]
    small_sum = dict(zip(VECTORS, summed[:-1]))
    small_sum["conv_w"] = lax.dynamic_slice_in_dim(small_sum["conv_w"], chip * conv_cols, conv_cols, axis=1)

    grad_out, delta, new_m, new_v = {}, {}, {}, {}
    like = [given[n] for n in VECTORS]
    pw, _ = _pack(like)
    pg, _ = _pack([small_sum[n] for n in VECTORS])
    pm, _ = _pack([mom_m[n] for n in VECTORS])
    pv, _ = _pack([mom_v[n] for n in VECTORS])
    _, pd, pnm, pnv = _adamw(pw, pg, pm, pv, "adamw_small")
    for n, g, d, nm, nv in zip(VECTORS, [small_sum[n] for n in VECTORS], _unpack(pd, like), _unpack(pnm, like), _unpack(pnv, like)):
        grad_out[n] = g.reshape(given[n].shape)
        delta[n], new_m[n], new_v[n] = d, nm, nv
    tok = reduce.step("in", pd)
    reduce.step("in", tok)
    for n in MATRICES:
        g, d, nm, nv = reduce.results[n]
        grad_out[n], delta[n], new_m[n], new_v[n] = g[None], d[None], nm[None], nv[None]

    return (loss, grad_x[None], *[grad_out[n] for n in WEIGHT_ORDER], *[delta[n] for n in WEIGHT_ORDER],
            *[new_m[n] for n in WEIGHT_ORDER], *[new_v[n] for n in WEIGHT_ORDER])
```

```python
import functools

import jax
import jax.numpy as jnp
from jax import lax
from jax.experimental import pallas as pl
from jax.experimental.pallas import tpu as pltpu

F32 = jnp.float32
BF16 = jnp.bfloat16
MESH = pl.DeviceIdType.MESH
ANY = pl.BlockSpec(memory_space=pl.ANY)

VMEM_LIMIT_BYTES = 56 * 1024 * 1024

N_CHIPS = 4
HEAD_DIM = 128
N_Q_HEADS = 8
N_KV_HEADS = 2
Q_GROUP = N_Q_HEADS // N_KV_HEADS
ATTN_WIDTH = N_Q_HEADS * HEAD_DIM
KV_WIDTH = N_KV_HEADS * HEAD_DIM
WINDOW = 128
BLOCK = 128
BAND = 3 * BLOCK
ROPE_THETA = 10000.0
CONV_WIDTH = 1024
MEM_HEADS = 4
MEM_WIDTH = MEM_HEADS * HEAD_DIM
RMS_EPS = 1e-6
NEG_INF = -1e30
ATTN_SCALE = HEAD_DIM ** -0.5

Q_OFF, K_OFF, V_OFF, CU_OFF, CB_OFF, CC_OFF, GL_OFF = 0, 1024, 1280, 1536, 2560, 3584, 4608

ADAM_LR = 0.001
ADAM_B1 = 0.9
ADAM_B2 = 0.999
ADAM_EPS = 1e-08
ADAM_WD = 0.01
ADAM_STEP = 10
ADAM_C1 = 1.0 - ADAM_B1 ** ADAM_STEP
ADAM_C2 = 1.0 - ADAM_B2 ** ADAM_STEP


def _params(n_grid_axes):
    return pltpu.CompilerParams(dimension_semantics=("arbitrary",) * n_grid_axes, vmem_limit_bytes=VMEM_LIMIT_BYTES)


BF16_SUBLANES = 16


def _row_tile(rows, want):
    if rows <= want:
        return rows
    for t in range(want, 0, -BF16_SUBLANES):
        if rows % t == 0:
            return t
    return rows


def _matmul(a, b, *, mode, tm, tn, tk, out_dtypes, name, extras=(), epilogue=None, b_blocks=1, out_blocks=1, after=None):
    if mode == "tn":
        kdim, m = a.shape
    else:
        m, kdim = a.shape
    if b_blocks > 1:
        nb, brows, bcols = b.shape
        assert nb == b_blocks
        if mode == "nn":
            n = bcols * nb
            assert brows == kdim
        else:
            assert mode == "nt" and bcols * nb == kdim
            n = brows
    else:
        n = b.shape[0] if mode == "nt" else b.shape[1]
    tm, tn = min(tm, m), min(tn, n)
    assert m % tm == 0 and n % tn == 0 and tk == kdim, (name, m, n, kdim, tm, tn, tk)
    n_extra, n_out = len(extras), len(out_dtypes)
    n_after = 0 if after is None else 1

    if mode == "tn":
        a_spec = pl.BlockSpec((tk, tm), lambda j, i, k: (k, i))
        dims = (((0,), (0,)), ((), ()))
    else:
        a_spec = pl.BlockSpec((tm, tk), lambda j, i, k: (i, k))
        dims = (((1,), (0,)), ((), ())) if mode == "nn" else (((1,), (1,)), ((), ()))

    if b_blocks > 1 and mode == "nn":
        per = b.shape[2] // tn
        assert b.shape[2] % tn == 0
        b_spec = pl.BlockSpec((None, tk, tn), lambda j, i, k: (j // per, k, j % per))
    elif b_blocks > 1:
        b_spec = pl.BlockSpec((b_blocks, tn, b.shape[2]), lambda j, i, k: (0, j, 0))
    elif mode == "nt":
        b_spec = pl.BlockSpec((tn, tk), lambda j, i, k: (j, k))
    else:
        b_spec = pl.BlockSpec((tk, tn), lambda j, i, k: (k, j))

    tile_spec = pl.BlockSpec((tm, tn), lambda j, i, k: (i, j))
    if out_blocks > 1:
        ncols = n // out_blocks
        assert ncols % tn == 0
        oper = ncols // tn
        out_spec = pl.BlockSpec((None, tm, tn), lambda j, i, k: (j // oper, i, j % oper))
        out_shape = [jax.ShapeDtypeStruct((out_blocks, m, ncols), dt) for dt in out_dtypes]
    else:
        out_spec = tile_spec
        out_shape = [jax.ShapeDtypeStruct((m, n), dt) for dt in out_dtypes]

    def body(a_ref, b_ref, *rest):
        extra_refs = rest[:n_extra]
        out_refs = rest[n_extra + n_after:n_extra + n_after + n_out]
        if mode == "nt" and b_blocks > 1:
            cs = b.shape[2]
            acc = None
            for jb in range(b_blocks):
                prod = lax.dot_general(a_ref[:, jb * cs:(jb + 1) * cs].astype(BF16), b_ref[jb].astype(BF16), dims,
                                       preferred_element_type=F32)
                acc = prod if acc is None else acc + prod
        else:
            acc = lax.dot_general(a_ref[...].astype(BF16), b_ref[...].astype(BF16), dims, preferred_element_type=F32)
        tiles = (acc,) if epilogue is None else epilogue(acc, *[r[...] for r in extra_refs])
        for o_ref, t in zip(out_refs, tiles, strict=True):
            o_ref[...] = t.astype(o_ref.dtype)

    outs = pl.pallas_call(
        body,
        name=name,
        grid=(n // tn, m // tm, 1),
        in_specs=[a_spec, b_spec] + [tile_spec] * n_extra + [ANY] * n_after,
        out_specs=[out_spec] * n_out,
        out_shape=out_shape,
        compiler_params=_params(3),
    )(a, b, *extras, *([] if after is None else [after]))
    return outs[0] if n_out == 1 else outs


def _add_residual(acc, res):
    return (acc + res,)


def _matmul_column_blocks(a, b4, blocks, out, *, tm, name, after=None):
    m, kdim = a.shape
    nb, _, cols = b4.shape
    tm = min(tm, m)
    assert m % tm == 0

    def body(j_ref, a_ref, b_ref, *rest):
        rest[-1][...] = jnp.dot(a_ref[...], b_ref[...], preferred_element_type=F32)

    extra = ([] if out is None else [out]) + ([] if after is None else [after])
    n_blocks = blocks.shape[0]
    return pl.pallas_call(
        body, name=name,
        grid_spec=pltpu.PrefetchScalarGridSpec(
            num_scalar_prefetch=1, grid=(n_blocks, m // tm),
            in_specs=[pl.BlockSpec((tm, kdim), lambda j, i, blk: (i, 0)),
                      pl.BlockSpec((None, kdim, cols), lambda j, i, blk: (blk[j], 0, 0))] + [ANY] * len(extra),
            out_specs=pl.BlockSpec((tm, cols), lambda j, i, blk: (i, blk[j]))),
        out_shape=jax.ShapeDtypeStruct((m, nb * cols), F32),
        input_output_aliases={} if out is None else {3: 0},
        compiler_params=_params(2),
    )(blocks, a, b4, *extra)


def _wgrad_half(a, b, core, *, theirs, row_sharded, tm, tn, name, add=None, after=None):
    kdim, m = a.shape
    n = b.shape[1]
    rs, cs = (m // N_CHIPS, n) if row_sharded else (m, n // N_CHIPS)
    rh = rs // 2
    tm, tn = min(tm, rh), min(tn, cs)
    assert rh % tm == 0 and cs % tn == 0, (name, rh, cs, tm, tn)
    mh, per = rh // tm, cs // tn
    has_add = add is not None

    def half(c):
        return 1 - c[0] if theirs else c[0]

    if row_sharded:
        grid = (n // tn, N_CHIPS * mh)
        a_spec = pl.BlockSpec((kdim, tm), lambda j, r, c: (0, ((r // mh) * 2 + half(c)) * mh + r % mh))
        o_spec = pl.BlockSpec((None, tm, tn), lambda j, r, c: (r // mh, r % mh, j))
    else:
        grid = (n // tn, mh)
        a_spec = pl.BlockSpec((kdim, tm), lambda j, r, c: (0, half(c) * mh + r))
        o_spec = pl.BlockSpec((None, tm, tn), lambda j, r, c: (j // per, r, j % per))
    b_spec = pl.BlockSpec((kdim, tn), lambda j, r, c: (0, j))

    def body(c_ref, a_ref, b_ref, *rest):
        o_ref = rest[-1]
        acc = lax.dot_general(a_ref[...].astype(BF16), b_ref[...].astype(BF16), (((0,), (0,)), ((), ())),
                              preferred_element_type=F32)
        if has_add:
            acc = acc + rest[0][...].astype(F32)
        o_ref[...] = acc.astype(BF16)

    operands = [a, b] + ([add] if has_add else []) + ([] if after is None else [after])
    return pl.pallas_call(
        body, name=name,
        grid_spec=pltpu.PrefetchScalarGridSpec(
            num_scalar_prefetch=1, grid=grid,
            in_specs=[a_spec, b_spec] + ([o_spec] if has_add else []) + ([] if after is None else [ANY]),
            out_specs=o_spec),
        out_shape=jax.ShapeDtypeStruct((N_CHIPS, rh, cs), BF16),
        compiler_params=_params(2),
    )(core, *operands)


def _rstd(x):
    return lax.rsqrt(jnp.mean(x * x, axis=-1, keepdims=True) + RMS_EPS)


def _rmsnorm(x, g, name):
    s, d = x.shape
    tr = _row_tile(s, 256)

    def body(x_ref, g_ref, o_ref):
        xv = x_ref[...]
        o_ref[...] = (xv * _rstd(xv) * g_ref[...]).astype(BF16)

    return pl.pallas_call(
        body, name=name, grid=(s // tr,),
        in_specs=[pl.BlockSpec((tr, d), lambda i: (i, 0)), pl.BlockSpec((1, d), lambda i: (0, 0))],
        out_specs=pl.BlockSpec((tr, d), lambda i: (i, 0)),
        out_shape=jax.ShapeDtypeStruct((s, d), BF16),
        compiler_params=_params(1),
    )(x, g)


def _rmsnorm_bwd(dh, x, g, dres, name):
    s, d = x.shape
    tr = _row_tile(s, 256)
    has_res = dres is not None

    def body(*refs):
        if has_res:
            dh_ref, x_ref, g_ref, res_ref, dx_ref, dxb_ref, dg_ref = refs
        else:
            dh_ref, x_ref, g_ref, dx_ref, dxb_ref, dg_ref = refs
        xv = x_ref[...]
        dhv = dh_ref[...].astype(F32)
        r = _rstd(xv)
        xn = xv * r
        dhg = dhv * g_ref[...]
        dx = r * (dhg - xn * jnp.mean(dhg * xn, axis=-1, keepdims=True))
        if has_res:
            dx = dx + res_ref[...]
        dx_ref[...] = dx
        dxb_ref[...] = dx.astype(BF16)
        part = jnp.sum(dhv * xn, axis=0, keepdims=True)

        @pl.when(pl.program_id(0) == 0)
        def _():
            dg_ref[...] = part

        @pl.when(pl.program_id(0) > 0)
        def _():
            dg_ref[...] += part

    row = pl.BlockSpec((tr, d), lambda i: (i, 0))
    vec = pl.BlockSpec((1, d), lambda i: (0, 0))
    return pl.pallas_call(
        body, name=name, grid=(s // tr,),
        in_specs=[row, row, vec] + ([row] if has_res else []),
        out_specs=[row, row, vec],
        out_shape=[jax.ShapeDtypeStruct((s, d), F32), jax.ShapeDtypeStruct((s, d), BF16), jax.ShapeDtypeStruct((1, d), F32)],
        compiler_params=_params(1),
    )(*([dh, x, g] + ([dres] if has_res else [])))


def _loss_head(x3, g, target):
    s, d = x3.shape
    tr = _row_tile(s, 256)

    def body(x_ref, g_ref, t_ref, dx_ref, dxb_ref, sq_ref, dg_ref):
        xv = x_ref[...]
        gv = g_ref[...]
        r = _rstd(xv)
        xn = xv * r
        err = xn * gv - t_ref[...]
        dy = err * (1.0 / d)
        dyg = dy * gv
        dx = r * (dyg - xn * jnp.mean(dyg * xn, axis=-1, keepdims=True))
        dx_ref[...] = dx
        dxb_ref[...] = dx.astype(BF16)
        sq = jnp.sum(jnp.sum(err * err, axis=1, keepdims=True), axis=0, keepdims=True)
        sq = jnp.broadcast_to(sq, (1, 128))
        part = jnp.sum(dy * xn, axis=0, keepdims=True)

        @pl.when(pl.program_id(0) == 0)
        def _():
            sq_ref[...] = sq
            dg_ref[...] = part

        @pl.when(pl.program_id(0) > 0)
        def _():
            sq_ref[...] += sq
            dg_ref[...] += part

    row = pl.BlockSpec((tr, d), lambda i: (i, 0))
    vec = pl.BlockSpec((1, d), lambda i: (0, 0))
    return pl.pallas_call(
        body, name="loss_head", grid=(s // tr,),
        in_specs=[row, vec, row],
        out_specs=[row, row, pl.BlockSpec((1, 128), lambda i: (0, 0)), vec],
        out_shape=[jax.ShapeDtypeStruct((s, d), F32), jax.ShapeDtypeStruct((s, d), BF16),
                   jax.ShapeDtypeStruct((1, 128), F32), jax.ShapeDtypeStruct((1, d), F32)],
        compiler_params=_params(1),
    )(x3, g, target)


def _rope_tables(s):
    inv = 1.0 / (ROPE_THETA ** (jnp.arange(0, HEAD_DIM, 2, dtype=F32) / HEAD_DIM))
    ang = jnp.arange(s, dtype=F32)[:, None] * inv[None, :]
    cos, sin = jnp.cos(ang), jnp.sin(ang)
    return jnp.concatenate([cos, cos], axis=1), jnp.concatenate([-sin, sin], axis=1)


def _swap_halves(t):
    return pltpu.roll(t, HEAD_DIM // 2, 1)


def _rope_fwd(z, cos_t, sin_t):
    s = z.shape[0]
    tr = _row_tile(s, 256)

    def body(zq_ref, zk_ref, zv_ref, c_ref, s_ref, q_ref, k_ref, v_ref):
        c, sn = c_ref[...], s_ref[...]
        for hd in range(N_Q_HEADS):
            cols = slice(hd * HEAD_DIM, (hd + 1) * HEAD_DIM)
            t = zq_ref[:, cols]
            q_ref[:, cols] = (t * c + _swap_halves(t) * sn).astype(BF16)
        for hd in range(N_KV_HEADS):
            cols = slice(hd * HEAD_DIM, (hd + 1) * HEAD_DIM)
            t = zk_ref[:, cols]
            k_ref[:, cols] = (t * c + _swap_halves(t) * sn).astype(BF16)
        v_ref[...] = zv_ref[...].astype(BF16)

    tab = pl.BlockSpec((tr, HEAD_DIM), lambda i: (i, 0))
    return pl.pallas_call(
        body, name="rope_fwd", grid=(s // tr,),
        in_specs=[pl.BlockSpec((tr, ATTN_WIDTH), lambda i: (i, Q_OFF // ATTN_WIDTH)),
                  pl.BlockSpec((tr, KV_WIDTH), lambda i: (i, K_OFF // KV_WIDTH)),
                  pl.BlockSpec((tr, KV_WIDTH), lambda i: (i, V_OFF // KV_WIDTH)), tab, tab],
        out_specs=[pl.BlockSpec((tr, ATTN_WIDTH), lambda i: (i, 0)), pl.BlockSpec((tr, KV_WIDTH), lambda i: (i, 0)),
                   pl.BlockSpec((tr, KV_WIDTH), lambda i: (i, 0))],
        out_shape=[jax.ShapeDtypeStruct((s, ATTN_WIDTH), BF16), jax.ShapeDtypeStruct((s, KV_WIDTH), BF16),
                   jax.ShapeDtypeStruct((s, KV_WIDTH), BF16)],
        compiler_params=_params(1),
    )(z, z, z, cos_t, sin_t)


def _rope_bwd(dq_rot, dk_rot, dv, cos_t, sin_t):
    s = dq_rot.shape[0]
    tr = _row_tile(s, 256)

    def body(dq_ref, dk_ref, dv_ref, c_ref, s_ref, oq_ref, ok_ref, ov_ref):
        c, sn = c_ref[...], s_ref[...]
        for hd in range(N_Q_HEADS):
            cols = slice(hd * HEAD_DIM, (hd + 1) * HEAD_DIM)
            t = dq_ref[:, cols]
            oq_ref[:, cols] = (t * c + _swap_halves(t * sn)).astype(BF16)
        for hd in range(N_KV_HEADS):
            cols = slice(hd * HEAD_DIM, (hd + 1) * HEAD_DIM)
            t = dk_ref[:, cols]
            ok_ref[:, cols] = (t * c + _swap_halves(t * sn)).astype(BF16)
        ov_ref[...] = dv_ref[...].astype(BF16)

    tab = pl.BlockSpec((tr, HEAD_DIM), lambda i: (i, 0))
    wide = pl.BlockSpec((tr, ATTN_WIDTH), lambda i: (i, 0))
    narrow = pl.BlockSpec((tr, KV_WIDTH), lambda i: (i, 0))
    return pl.pallas_call(
        body, name="rope_bwd", grid=(s // tr,),
        in_specs=[wide, narrow, narrow, tab, tab],
        out_specs=[wide, narrow, narrow],
        out_shape=[jax.ShapeDtypeStruct((s, ATTN_WIDTH), BF16), jax.ShapeDtypeStruct((s, KV_WIDTH), BF16),
                   jax.ShapeDtypeStruct((s, KV_WIDTH), BF16)],
        compiler_params=_params(1),
    )(dq_rot, dk_rot, dv, cos_t, sin_t)


def _swa_band(i, s):
    return pl.multiple_of(jnp.clip((i - 1) * BLOCK, 0, s - BAND), BLOCK)


def _swa_probs(q_ref, k_ref, sink_ref, kv, start, valid):
    cols = slice(kv * HEAD_DIM, (kv + 1) * HEAD_DIM)
    kb = k_ref[pl.ds(start, BAND), cols]
    heads = [kv * Q_GROUP + g for g in range(Q_GROUP)]
    qg = jnp.concatenate([q_ref[:, hd * HEAD_DIM:(hd + 1) * HEAD_DIM] for hd in heads], axis=0)
    sc = lax.dot_general(qg, kb, (((1,), (1,)), ((), ())), preferred_element_type=F32) * ATTN_SCALE
    sc = jnp.where(valid, sc, NEG_INF)
    sk = jnp.concatenate([jnp.full((BLOCK, 1), sink_ref[hd], F32) for hd in heads], axis=0)
    mx = jnp.maximum(jnp.max(sc, axis=1, keepdims=True), sk)
    e = jnp.exp(sc - mx)
    es = jnp.exp(sk - mx)
    inv = 1.0 / (jnp.sum(e, axis=1, keepdims=True) + es)
    return qg, kb, e * inv, es * inv


def _swa_valid(i, start):
    q_pos = i * BLOCK + lax.broadcasted_iota(jnp.int32, (BLOCK, 1), 0)
    q_pos = jnp.concatenate([q_pos] * Q_GROUP, axis=0)
    k_pos = start + lax.broadcasted_iota(jnp.int32, (1, BAND), 1)
    return jnp.abs(k_pos - q_pos) <= WINDOW


def _swa_fwd(q, k, v, sink):
    s = q.shape[0]
    assert s % BLOCK == 0 and s >= BAND

    def body(sink_ref, q_ref, k_ref, v_ref, o_ref):
        i = pl.program_id(0)
        start = _swa_band(i, s)
        valid = _swa_valid(i, start)
        for kv in range(N_KV_HEADS):
            _, _, p, _ = _swa_probs(q_ref, k_ref, sink_ref, kv, start, valid)
            vb = v_ref[pl.ds(start, BAND), kv * HEAD_DIM:(kv + 1) * HEAD_DIM]
            o = jnp.dot(p.astype(BF16), vb, preferred_element_type=F32)
            for g in range(Q_GROUP):
                hd = kv * Q_GROUP + g
                o_ref[:, hd * HEAD_DIM:(hd + 1) * HEAD_DIM] = o[g * BLOCK:(g + 1) * BLOCK].astype(BF16)

    whole = pl.BlockSpec((s, KV_WIDTH), lambda i: (0, 0))
    blk = pl.BlockSpec((BLOCK, ATTN_WIDTH), lambda i: (i, 0))
    return pl.pallas_call(
        body, name="swa_fwd", grid=(s // BLOCK,),
        in_specs=[pl.BlockSpec(memory_space=pltpu.SMEM), blk, whole, whole],
        out_specs=blk,
        out_shape=jax.ShapeDtypeStruct((s, ATTN_WIDTH), BF16),
        compiler_params=_params(1),
    )(sink, q, k, v)


def _swa_bwd(q, k, v, d_out, sink):
    s = q.shape[0]

    def body(sink_ref, q_ref, k_ref, v_ref, do_ref, dq_ref, dk_ref, dv_ref, dsink_ref):
        i = pl.program_id(0)

        @pl.when(i == 0)
        def _():
            dk_ref[...] = jnp.zeros_like(dk_ref)
            dv_ref[...] = jnp.zeros_like(dv_ref)
            dsink_ref[...] = jnp.zeros_like(dsink_ref)

        start = _swa_band(i, s)
        valid = _swa_valid(i, start)
        for kv in range(N_KV_HEADS):
            cols = slice(kv * HEAD_DIM, (kv + 1) * HEAD_DIM)
            qg, kb, p, p_sink = _swa_probs(q_ref, k_ref, sink_ref, kv, start, valid)
            vb = v_ref[pl.ds(start, BAND), cols]
            heads = [kv * Q_GROUP + g for g in range(Q_GROUP)]
            dog = jnp.concatenate([do_ref[:, hd * HEAD_DIM:(hd + 1) * HEAD_DIM] for hd in heads], axis=0)
            dp = lax.dot_general(dog, vb, (((1,), (1,)), ((), ())), preferred_element_type=F32)
            delta = jnp.sum(p * dp, axis=1, keepdims=True)
            ds = (p * (dp - delta) * ATTN_SCALE).astype(BF16)
            dqg = jnp.dot(ds, kb, preferred_element_type=F32)
            dk_ref[pl.ds(start, BAND), cols] += lax.dot_general(ds, qg, (((0,), (0,)), ((), ())), preferred_element_type=F32)
            dv_ref[pl.ds(start, BAND), cols] += lax.dot_general(p.astype(BF16), dog, (((0,), (0,)), ((), ())),
                                                                 preferred_element_type=F32)
            dsk = p_sink * delta
            for g, hd in enumerate(heads):
                dq_ref[:, hd * HEAD_DIM:(hd + 1) * HEAD_DIM] = dqg[g * BLOCK:(g + 1) * BLOCK]
                tot = jnp.sum(dsk[g * BLOCK:(g + 1) * BLOCK], axis=0, keepdims=True)
                dsink_ref[hd:hd + 1, :] -= jnp.broadcast_to(tot, (1, 128))

    whole = pl.BlockSpec((s, KV_WIDTH), lambda i: (0, 0))
    blk = pl.BlockSpec((BLOCK, ATTN_WIDTH), lambda i: (i, 0))
    return pl.pallas_call(
        body, name="swa_bwd", grid=(s // BLOCK,),
        in_specs=[pl.BlockSpec(memory_space=pltpu.SMEM), blk, whole, whole, blk],
        out_specs=[blk, whole, whole, pl.BlockSpec((N_Q_HEADS, 128), lambda i: (0, 0))],
        out_shape=[jax.ShapeDtypeStruct((s, ATTN_WIDTH), F32), jax.ShapeDtypeStruct((s, KV_WIDTH), F32),
                   jax.ShapeDtypeStruct((s, KV_WIDTH), F32), jax.ShapeDtypeStruct((N_Q_HEADS, 128), F32)],
        compiler_params=_params(1),
    )(sink, q, k, v, d_out)


CONV_CHUNK = 256


def _shift_rows(t, rows, down):
    n = t.shape[0]
    rolled = pltpu.roll(t, 1 if down else n - 1, 0)
    edge = 0 if down else n - 1
    return jnp.where(rows == edge, 0.0, rolled)


def _conv_specs(s):
    def z_spec(off):
        return pl.BlockSpec((s, CONV_CHUNK), lambda j, off=off: (0, off // CONV_CHUNK + j))
    chunk = pl.BlockSpec((s, CONV_CHUNK), lambda j: (0, j))
    w_spec = pl.BlockSpec((3, CONV_CHUNK), lambda j: (0, j))
    return z_spec(CU_OFF), z_spec(CB_OFF), z_spec(CC_OFF), chunk, w_spec


def _conv_fwd(z, conv_w):
    s = z.shape[0]
    cu_spec, cb_spec, cc_spec, chunk, w_spec = _conv_specs(s)

    def body(cu_ref, cb_ref, cc_ref, w_ref, o_ref):
        rows = lax.broadcasted_iota(jnp.int32, (s, 1), 0)
        t = cc_ref[...] * cu_ref[...]
        c3 = _shift_rows(t, rows, True) * w_ref[0:1, :] + t * w_ref[1:2, :] + _shift_rows(t, rows, False) * w_ref[2:3, :]
        o_ref[...] = (cb_ref[...] * c3).astype(BF16)

    return pl.pallas_call(
        body, name="conv_fwd", grid=(CONV_WIDTH // CONV_CHUNK,),
        in_specs=[cu_spec, cb_spec, cc_spec, w_spec],
        out_specs=chunk,
        out_shape=jax.ShapeDtypeStruct((s, CONV_WIDTH), BF16),
        compiler_params=_params(1),
    )(z, z, z, conv_w)


def _conv_bwd(z, conv_w, d_co):
    s = z.shape[0]
    cu_spec, cb_spec, cc_spec, chunk, w_spec = _conv_specs(s)

    def body(cu_ref, cb_ref, cc_ref, w_ref, d_ref, dcu_ref, dcb_ref, dcc_ref, dw_ref):
        rows = lax.broadcasted_iota(jnp.int32, (s, 1), 0)
        cu, cc = cu_ref[...], cc_ref[...]
        t = cc * cu
        t_dn, t_up = _shift_rows(t, rows, True), _shift_rows(t, rows, False)
        c3 = t_dn * w_ref[0:1, :] + t * w_ref[1:2, :] + t_up * w_ref[2:3, :]
        d = d_ref[...]
        dcb_ref[...] = (d * c3).astype(BF16)
        dc3 = d * cb_ref[...]
        dw_ref[0:1, :] = jnp.sum(dc3 * t_dn, axis=0, keepdims=True)
        dw_ref[1:2, :] = jnp.sum(dc3 * t, axis=0, keepdims=True)
        dw_ref[2:3, :] = jnp.sum(dc3 * t_up, axis=0, keepdims=True)
        dt = _shift_rows(dc3, rows, False) * w_ref[0:1, :] + dc3 * w_ref[1:2, :] + _shift_rows(dc3, rows, True) * w_ref[2:3, :]
        dcc_ref[...] = (dt * cu).astype(BF16)
        dcu_ref[...] = (dt * cc).astype(BF16)

    return pl.pallas_call(
        body, name="conv_bwd", grid=(CONV_WIDTH // CONV_CHUNK,),
        in_specs=[cu_spec, cb_spec, cc_spec, w_spec, chunk],
        out_specs=[chunk, chunk, chunk, w_spec],
        out_shape=[jax.ShapeDtypeStruct((s, CONV_WIDTH), BF16)] * 3 + [jax.ShapeDtypeStruct((3, CONV_WIDTH), F32)],
        compiler_params=_params(1),
    )(z, z, z, conv_w, d_co)


GATE_CHUNK = 512


def _gate_specs(s, d, tr):
    n_chunks = d // GATE_CHUNK
    za = pl.BlockSpec((tr, GATE_CHUNK), lambda j, i: (i, GL_OFF // GATE_CHUNK + j))
    zc = pl.BlockSpec((tr, GATE_CHUNK), lambda j, i: (i, GL_OFF // GATE_CHUNK + n_chunks + j))
    ba = pl.BlockSpec((1, GATE_CHUNK), lambda j, i: (0, j))
    bc = pl.BlockSpec((1, GATE_CHUNK), lambda j, i: (0, n_chunks + j))
    tile = pl.BlockSpec((tr, GATE_CHUNK), lambda j, i: (i, j))
    return za, zc, ba, bc, tile


def _gate_fwd(z, b_gate, ya, yc):
    s, d = ya.shape
    tr = _row_tile(s, 512)
    za, zc, ba, bc, tile = _gate_specs(s, d, tr)

    def body(za_ref, zc_ref, ba_ref, bc_ref, ya_ref, yc_ref, o_ref):
        ga = jax.nn.sigmoid(za_ref[...] + ba_ref[...])
        gc = jax.nn.sigmoid(zc_ref[...] + bc_ref[...])
        o_ref[...] = (ga * ya_ref[...] + gc * yc_ref[...]).astype(BF16)

    return pl.pallas_call(
        body, name="gate_fwd", grid=(d // GATE_CHUNK, s // tr),
        in_specs=[za, zc, ba, bc, tile, tile],
        out_specs=tile,
        out_shape=jax.ShapeDtypeStruct((s, d), BF16),
        compiler_params=_params(2),
    )(z, z, b_gate, b_gate, ya, yc)


def _gate_bwd(z, b_gate, ya, yc, dmix):
    s, d = ya.shape
    tr = _row_tile(s, 512)
    za, zc, ba, bc, tile = _gate_specs(s, d, tr)
    vec = pl.BlockSpec((1, GATE_CHUNK), lambda j, i: (0, j))

    def body(za_ref, zc_ref, ba_ref, bc_ref, ya_ref, yc_ref, dm_ref, dya_ref, dyc_ref, dla_ref, dlc_ref, dba_ref, dbc_ref):
        ga = jax.nn.sigmoid(za_ref[...] + ba_ref[...])
        gc = jax.nn.sigmoid(zc_ref[...] + bc_ref[...])
        dm = dm_ref[...]
        dya_ref[...] = (dm * ga).astype(BF16)
        dyc_ref[...] = (dm * gc).astype(BF16)
        dla = dm * ya_ref[...] * ga * (1.0 - ga)
        dlc = dm * yc_ref[...] * gc * (1.0 - gc)
        dla_ref[...] = dla.astype(BF16)
        dlc_ref[...] = dlc.astype(BF16)
        pa = jnp.sum(dla, axis=0, keepdims=True)
        pc = jnp.sum(dlc, axis=0, keepdims=True)

        @pl.when(pl.program_id(1) == 0)
        def _():
            dba_ref[...] = pa
            dbc_ref[...] = pc

        @pl.when(pl.program_id(1) > 0)
        def _():
            dba_ref[...] += pa
            dbc_ref[...] += pc

    big = jax.ShapeDtypeStruct((s, d), BF16)
    small = jax.ShapeDtypeStruct((1, d), F32)
    return pl.pallas_call(
        body, name="gate_bwd", grid=(d // GATE_CHUNK, s // tr),
        in_specs=[za, zc, ba, bc, tile, tile, tile],
        out_specs=[tile, tile, tile, tile, vec, vec],
        out_shape=[big, big, big, big, small, small],
        compiler_params=_params(2),
    )(z, z, b_gate, b_gate, ya, yc, dmix)


def _cross_probs(q_ref, kv_ref, hd):
    cols = slice(hd * HEAD_DIM, (hd + 1) * HEAD_DIM)
    qh = q_ref[:, cols]
    kh = kv_ref[:, cols]
    sc = lax.dot_general(qh, kh, (((1,), (1,)), ((), ())), preferred_element_type=F32) * ATTN_SCALE
    e = jnp.exp(sc - jnp.max(sc, axis=1, keepdims=True))
    return qh, kh, e * (1.0 / jnp.sum(e, axis=1, keepdims=True))


def _cross_fwd(qc, kvc):
    s = qc.shape[0]
    n_mem = kvc.shape[0]
    tq = _row_tile(s, 256)

    def body(q_ref, kv_ref, o_ref):
        for hd in range(MEM_HEADS):
            _, _, p = _cross_probs(q_ref, kv_ref, hd)
            vh = kv_ref[:, MEM_WIDTH + hd * HEAD_DIM:MEM_WIDTH + (hd + 1) * HEAD_DIM]
            o_ref[:, hd * HEAD_DIM:(hd + 1) * HEAD_DIM] = jnp.dot(p.astype(BF16), vh, preferred_element_type=F32).astype(BF16)

    return pl.pallas_call(
        body, name="cross_fwd", grid=(s // tq,),
        in_specs=[pl.BlockSpec((tq, MEM_WIDTH), lambda i: (i, 0)), pl.BlockSpec((n_mem, 2 * MEM_WIDTH), lambda i: (0, 0))],
        out_specs=pl.BlockSpec((tq, MEM_WIDTH), lambda i: (i, 0)),
        out_shape=jax.ShapeDtypeStruct((s, MEM_WIDTH), BF16),
        compiler_params=_params(1),
    )(qc, kvc)


def _cross_bwd(qc, kvc, d_out):
    s = qc.shape[0]
    n_mem = kvc.shape[0]
    tq = _row_tile(s, 256)

    def body(q_ref, kv_ref, do_ref, dq_ref, dkv_ref):
        @pl.when(pl.program_id(0) == 0)
        def _():
            dkv_ref[...] = jnp.zeros_like(dkv_ref)

        for hd in range(MEM_HEADS):
            cols = slice(hd * HEAD_DIM, (hd + 1) * HEAD_DIM)
            vcols = slice(MEM_WIDTH + hd * HEAD_DIM, MEM_WIDTH + (hd + 1) * HEAD_DIM)
            qh, kh, p = _cross_probs(q_ref, kv_ref, hd)
            doh = do_ref[:, cols]
            dp = lax.dot_general(doh, kv_ref[:, vcols], (((1,), (1,)), ((), ())), preferred_element_type=F32)
            ds = (p * (dp - jnp.sum(p * dp, axis=1, keepdims=True)) * ATTN_SCALE).astype(BF16)
            dq_ref[:, cols] = jnp.dot(ds, kh, preferred_element_type=F32).astype(BF16)
            dkv_ref[:, cols] += lax.dot_general(ds, qh, (((0,), (0,)), ((), ())), preferred_element_type=F32)
            dkv_ref[:, vcols] += lax.dot_general(p.astype(BF16), doh, (((0,), (0,)), ((), ())), preferred_element_type=F32)

    qspec = pl.BlockSpec((tq, MEM_WIDTH), lambda i: (i, 0))
    kvspec = pl.BlockSpec((n_mem, 2 * MEM_WIDTH), lambda i: (0, 0))
    return pl.pallas_call(
        body, name="cross_bwd", grid=(s // tq,),
        in_specs=[qspec, kvspec, qspec],
        out_specs=[qspec, kvspec],
        out_shape=[jax.ShapeDtypeStruct((s, MEM_WIDTH), BF16), jax.ShapeDtypeStruct((n_mem, 2 * MEM_WIDTH), F32)],
        compiler_params=_params(1),
    )(qc, kvc, d_out)


def _swiglu_fwd(up, gate):
    return up, (gate * jax.nn.sigmoid(gate)) * up


def _swiglu_bwd(d_act, gate, up):
    sg = jax.nn.sigmoid(gate)
    silu = gate * sg
    return d_act * up * (sg * (1.0 + gate * (1.0 - sg))), d_act * silu


GATHER_GROUPS = {"in": ("w_in", "conv_w"), "mid": ("w_attn_out", "w_conv_out", "w_o", "w_cq", "w_ckv", "w_co"),
                 "gate": ("w_gate",), "up": ("w_up",), "down": ("w_down",)}


def _local_step(xs, mems, target, small, fetch, reduce):
    s, d = xs.shape
    w4 = {}
    cos_t, sin_t = _rope_tables(s)

    def near(group, done, then, after):
        waits = [("direct", group)] + ([("pass_near", done), ("pass_far", done)] if done else [])
        starts = [("forward", group), ("pass_near", group)] + [("direct", g) for g in then]
        tok = fetch.step("gather_near_" + group, waits, starts, after)
        if done:
            w4.update(fetch.arrays(done))
        return tok

    def far(group, then, after):
        return fetch.step("gather_far_" + group, [("forward", group)], [("pass_far", group)] + [("direct", g) for g in then], after)

    def last(group, after):
        tok = fetch.step("gather_done_" + group, [("pass_near", group), ("pass_far", group)], [], after)
        w4.update(fetch.arrays(group))
        return tok

    h = _rmsnorm(xs, small["g_mix"], "norm_mix")
    slots_filled = [a for g in ("gate", "up", "down") for a in fetch.arrays(g).values()]
    chip_x, chip_y = reduce.place[0] // 2, reduce.place[0] % 2
    own_block = jnp.stack([2 * chip_x + chip_y]).astype(jnp.int32)
    near_blocks = jnp.stack([2 * (1 - chip_x) + chip_y, 2 * chip_x + (1 - chip_y)]).astype(jnp.int32)
    far_block = jnp.stack([2 * (1 - chip_x) + (1 - chip_y)]).astype(jnp.int32)
    z = _matmul_column_blocks(h, fetch.arrays("in")["w_in"], own_block, None, tm=512, name="in_proj_own")
    tok = near("in", None, ["mid"], [z] + slots_filled)
    tok = fetch.step("gather_near_done_in", [("pass_near", "in")], [], tok)
    z = _matmul_column_blocks(h, fetch.arrays("in")["w_in"], near_blocks, z, tm=512, name="in_proj_near", after=tok)
    tok = far("in", [], z)
    tok = fetch.step("gather_done_in", [("pass_far", "in")], [], tok)
    w4.update(fetch.arrays("in"))
    z = _matmul_column_blocks(h, w4["w_in"], far_block, z, tm=512, name="in_proj_far", after=tok)
    conv4 = w4["conv_w"]
    conv_w = conv4[:, :3, :].transpose(1, 0, 2).reshape(3, N_CHIPS * conv4.shape[2])
    c_in = w4["w_in"].shape[2]
    tok = near("mid", None, ["gate"], z)
    q_rot, k_rot, v_b = _rope_fwd(z, cos_t, sin_t)
    attn = _swa_fwd(q_rot, k_rot, v_b, small["sink"])
    co = _conv_fwd(z, conv_w)
    tok = far("mid", ["up"], attn)
    tok = last("mid", tok)
    w_o = w4["w_o"].reshape(-1, w4["w_o"].shape[-1])
    c_d = w4["w_attn_out"].shape[2]
    ya = _matmul(attn, w4["w_attn_out"], mode="nn", tm=1024, tn=c_d, tk=ATTN_WIDTH, out_dtypes=[F32], name="attn_out_proj",
                 b_blocks=N_CHIPS, after=tok)
    yc = _matmul(co, w4["w_conv_out"], mode="nn", tm=1024, tn=c_d, tk=CONV_WIDTH, out_dtypes=[F32], name="conv_out_proj",
                 b_blocks=N_CHIPS)
    mix = _gate_fwd(z, small["b_gate"], ya, yc)
    x1 = _matmul(mix, w_o, mode="nn", tm=512, tn=1024, tk=d, out_dtypes=[F32], name="mix_out_proj", extras=[xs],
                 epilogue=_add_residual)
    tok = near("gate", None, ["down"], x1)
    w_cq = w4["w_cq"].reshape(-1, w4["w_cq"].shape[-1])
    w_ckv = w4["w_ckv"].reshape(-1, w4["w_ckv"].shape[-1])
    hc = _rmsnorm(x1, small["g_cross"], "norm_cross")
    memn = _rmsnorm(mems, small["g_mem"], "norm_mem")
    qc = _matmul(hc, w_cq, mode="nn", tm=1024, tn=MEM_WIDTH, tk=d, out_dtypes=[BF16], name="cross_q_proj", after=tok)
    kvc = _matmul(memn, w_ckv, mode="nn", tm=256, tn=2 * MEM_WIDTH, tk=d, out_dtypes=[BF16], name="cross_kv_proj")
    oc = _cross_fwd(qc, kvc)
    x2 = _matmul(oc, w4["w_co"], mode="nn", tm=1024, tn=c_d, tk=MEM_WIDTH, out_dtypes=[F32], name="cross_out_proj",
                 extras=[x1], epilogue=_add_residual, b_blocks=N_CHIPS)
    hf = _rmsnorm(x2, small["g_ffn"], "norm_ffn")
    tok = far("gate", [], hf)
    tok = near("up", "gate", [], tok)
    c_ff = w4["w_gate"].shape[2]
    gate = _matmul(hf, w4["w_gate"], mode="nn", tm=512, tn=c_ff, tk=d, out_dtypes=[F32], name="ffn_gate_proj", b_blocks=N_CHIPS,
                   after=tok)
    tok = far("up", [], gate)
    tok = near("down", "up", [], tok)
    up, act = _matmul(hf, w4["w_up"], mode="nn", tm=512, tn=c_ff, tk=d, out_dtypes=[F32, BF16], name="ffn_up_proj",
                      extras=[gate], epilogue=_swiglu_fwd, b_blocks=N_CHIPS, after=tok)
    tok = far("down", [], act)
    last("down", tok)
    w_down = w4["w_down"].reshape(-1, w4["w_down"].shape[-1])
    x3 = _matmul(act, w_down, mode="nn", tm=512, tn=512, tk=w_down.shape[0], out_dtypes=[F32], name="ffn_down_proj", extras=[x2],
                 epilogue=_add_residual)
    dx3, dx3b, sq, dg_final = _loss_head(x3, small["g_final"], target)

    da, du = _matmul(dx3b, w_down, mode="nt", tm=512, tn=c_ff, tk=d, out_dtypes=[BF16, BF16], name="ffn_down_bwd",
                     extras=[gate, up], epilogue=_swiglu_bwd)
    core = reduce.core
    ffn_shape = dict(row_sharded=False, tm=512, tn=c_ff)
    g_down = _matmul(act, dx3b, mode="tn", tm=c_ff, tn=1024, tk=s, out_dtypes=[BF16], name="ffn_down_wgrad")
    tok = reduce.add("down", {"w_down": g_down}, da)
    t_gate = _wgrad_half(hf, da, core, theirs=True, name="ffn_gate_wgrad_theirs", after=tok, **ffn_shape)
    tok = reduce.step("down", t_gate)
    t_up = _wgrad_half(hf, du, core, theirs=True, name="ffn_up_wgrad_theirs", after=tok, **ffn_shape)
    tok = reduce.send("ffn", {"w_gate": t_gate, "w_up": t_up}, dx3b)
    dhf = _matmul(da, w4["w_gate"], mode="nt", tm=512, tn=1024, tk=N_CHIPS * c_ff, out_dtypes=[F32], name="ffn_gate_bwd", b_blocks=N_CHIPS,
                  after=tok)
    got = reduce.received("ffn", dhf)
    p_gate = _wgrad_half(hf, da, core, theirs=False, name="ffn_gate_wgrad_mine", add=got["w_gate"], **ffn_shape)
    p_up = _wgrad_half(hf, du, core, theirs=False, name="ffn_up_wgrad_mine", add=got["w_up"], **ffn_shape)
    tok = reduce.add_parts("ffn", {"w_gate": p_gate, "w_up": p_up})
    dhf = _matmul(du, w4["w_up"], mode="nt", tm=512, tn=1024, tk=N_CHIPS * c_ff, out_dtypes=[F32], name="ffn_up_bwd", extras=[dhf],
                  epilogue=_add_residual, b_blocks=N_CHIPS, after=tok)
    tok = reduce.step("down", dhf)
    dx2, dx2b, dg_ffn = _rmsnorm_bwd(dhf, x2, small["g_ffn"], dx3, "norm_ffn_bwd")

    d_oc = _matmul(dx2b, w4["w_co"], mode="nt", tm=1024, tn=MEM_WIDTH, tk=d, out_dtypes=[BF16], name="cross_out_bwd",
                   b_blocks=N_CHIPS, after=tok)
    g_co = _matmul(oc, dx2b, mode="tn", tm=MEM_WIDTH, tn=c_d, tk=s, out_dtypes=[BF16], name="cross_out_wgrad", out_blocks=N_CHIPS)
    tok = reduce.step("down", g_co)
    dqc, dkvc = _cross_bwd(qc, kvc, d_oc)
    g_cq = _matmul(hc, dqc, mode="tn", tm=1024, tn=MEM_WIDTH, tk=s, out_dtypes=[BF16], name="cross_q_wgrad", after=tok)
    dhc = _matmul(dqc, w_cq, mode="nt", tm=1024, tn=1024, tk=MEM_WIDTH, out_dtypes=[F32], name="cross_q_bwd")
    g_ckv = _matmul(memn, dkvc, mode="tn", tm=1024, tn=2 * MEM_WIDTH, tk=mems.shape[0], out_dtypes=[BF16], name="cross_kv_wgrad")
    dmemn = _matmul(dkvc, w_ckv, mode="nt", tm=256, tn=1024, tk=2 * MEM_WIDTH, out_dtypes=[F32], name="cross_kv_bwd")
    _, _, dg_mem = _rmsnorm_bwd(dmemn, mems, small["g_mem"], None, "norm_mem_bwd")
    dx1, dx1b, dg_cross = _rmsnorm_bwd(dhc, x1, small["g_cross"], dx2, "norm_cross_bwd")

    dmix = _matmul(dx1b, w_o, mode="nt", tm=512, tn=1024, tk=d, out_dtypes=[F32], name="mix_out_bwd")
    g_o = _matmul(mix, dx1b, mode="tn", tm=1024, tn=1024, tk=s, out_dtypes=[BF16], name="mix_out_wgrad")
    dya, dyc, dgl_a, dgl_c, db_a, db_c = _gate_bwd(z, small["b_gate"], ya, yc, dmix)
    d_attn = _matmul(dya, w4["w_attn_out"], mode="nt", tm=1024, tn=ATTN_WIDTH, tk=d, out_dtypes=[BF16], name="attn_out_bwd",
                     b_blocks=N_CHIPS)
    g_ao = _matmul(attn, dya, mode="tn", tm=ATTN_WIDTH, tn=c_d, tk=s, out_dtypes=[BF16], name="attn_out_wgrad", out_blocks=N_CHIPS)
    d_co = _matmul(dyc, w4["w_conv_out"], mode="nt", tm=1024, tn=CONV_WIDTH, tk=d, out_dtypes=[F32], name="conv_out_bwd",
                   b_blocks=N_CHIPS)
    g_cvo = _matmul(co, dyc, mode="tn", tm=CONV_WIDTH, tn=c_d, tk=s, out_dtypes=[BF16], name="conv_out_wgrad", out_blocks=N_CHIPS)
    tok = reduce.step("ffn", g_cvo)
    tok = reduce.add("mid", {"w_co": g_co, "w_cq": g_cq, "w_ckv": g_ckv, "w_o": g_o, "w_attn_out": g_ao, "w_conv_out": g_cvo}, tok)
    dcu, dcb, dcc, d_conv_w = _conv_bwd(z, conv_w, d_co)
    dq_rot, dk_rot, dv, dsink = _swa_bwd(q_rot, k_rot, v_b, d_attn, small["sink"])
    tok = reduce.step("mid", dq_rot)
    dq, dk, dvb = _rope_bwd(dq_rot, dk_rot, dv, cos_t, sin_t)
    dz = jnp.concatenate([dq, dk, dvb, dcu, dcb, dcc, dgl_a, dgl_c], axis=1)
    in_shape = dict(row_sharded=False, tm=512, tn=c_in)
    t_in = _wgrad_half(h, dz, core, theirs=True, name="in_proj_wgrad_theirs", after=tok, **in_shape)
    tok = reduce.send("in", {"w_in": t_in}, dk)
    tok = reduce.step("ffn", tok)
    tok = reduce.step("mid", tok)
    got = reduce.received("in", tok)
    p_in = _wgrad_half(h, dz, core, theirs=False, name="in_proj_wgrad_mine", add=got["w_in"], **in_shape)
    tok = reduce.add_parts("in", {"w_in": p_in})
    dh = _matmul(dz, w4["w_in"], mode="nt", tm=512, tn=512, tk=N_CHIPS * c_in, out_dtypes=[F32], name="in_proj_bwd", b_blocks=N_CHIPS,
                 after=tok)
    tok = reduce.step("mid", dh)
    grad_x, _, dg_mix = _rmsnorm_bwd(dh, xs, small["g_mix"], dx1, "norm_mix_bwd")

    small_grads = {
        "g_mix": dg_mix, "sink": dsink[:, 0], "b_gate": jnp.concatenate([db_a, db_c], axis=1), "g_cross": dg_cross,
        "g_mem": dg_mem, "g_ffn": dg_ffn, "g_final": dg_final, "conv_w": d_conv_w,
    }
    return sq, grad_x, small_grads


def _pair_sum(g4, ra, core, name):
    nb, rs, cs = g4.shape
    rh = rs // 2
    tr = _row_tile(rh, 256)
    per = rh // tr

    def body(c_ref, g_ref, r_ref, o_ref):
        o_ref[...] = (g_ref[...].astype(F32) + r_ref[...].astype(F32)).astype(BF16)

    plain = pl.BlockSpec((None, tr, cs), lambda j, i, c: (j, i, 0))
    return pl.pallas_call(
        body, name=name,
        grid_spec=pltpu.PrefetchScalarGridSpec(
            num_scalar_prefetch=1, grid=(nb, per),
            in_specs=[pl.BlockSpec((None, tr, cs), lambda j, i, c: (j, c[0] * per + i, 0)), plain],
            out_specs=plain),
        out_shape=jax.ShapeDtypeStruct((nb, rh, cs), BF16),
        compiler_params=_params(2),
    )(core, g4, ra)


def _quad_sum(parts, rc, place, name):
    _, rh, cs = parts.shape
    tr = _row_tile(rh, 256)
    per = rh // tr

    def body(p_ref, own_ref, r_ref, o_ref):
        acc = own_ref[...].astype(F32)
        for j in range(rc.shape[0]):
            acc = acc + r_ref[j].astype(F32)
        o_ref[...] = acc

    return pl.pallas_call(
        body, name=name,
        grid_spec=pltpu.PrefetchScalarGridSpec(
            num_scalar_prefetch=1, grid=(per,),
            in_specs=[pl.BlockSpec((None, tr, cs), lambda i, p: (p[0], i, 0)),
                      pl.BlockSpec((rc.shape[0], tr, cs), lambda i, p: (0, i, 0))],
            out_specs=pl.BlockSpec((tr, cs), lambda i, p: (p[1] * per + i, 0))),
        out_shape=jax.ShapeDtypeStruct((2 * rh, cs), F32),
        compiler_params=_params(1),
    )(place, parts, rc)


def _cast_to_slot(w, place, dtype, name, after=None):
    rows, cols = w.shape
    tr = _row_tile(rows, 256)

    def body(p_ref, w_ref, *rest):
        o_ref = rest[-1]
        o_ref[...] = w_ref[...].astype(dtype)

    return pl.pallas_call(
        body, name=name,
        grid_spec=pltpu.PrefetchScalarGridSpec(
            num_scalar_prefetch=1, grid=(rows // tr,),
            in_specs=[pl.BlockSpec((tr, cols), lambda i, p: (i, 0))] + ([] if after is None else [ANY]),
            out_specs=pl.BlockSpec((None, tr, cols), lambda i, p: (p[0], i, 0))),
        out_shape=jax.ShapeDtypeStruct((N_CHIPS, rows, cols), dtype),
        compiler_params=_params(1),
    )(place, w, *([] if after is None else [after]))


def _adamw(w, g, m, v, name, after=None):
    rows, cols = w.shape
    tr = _row_tile(rows, 256)

    def body(w_ref, g_ref, m_ref, v_ref, *rest):
        go_ref, d_ref, nm_ref, nv_ref = rest[-4:]
        gv = g_ref[...]
        go_ref[...] = gv
        nm = ADAM_B1 * m_ref[...] + (1.0 - ADAM_B1) * gv
        nv = ADAM_B2 * v_ref[...] + (1.0 - ADAM_B2) * (gv * gv)
        m_hat = nm / ADAM_C1
        v_hat = nv / ADAM_C2
        d_ref[...] = -ADAM_LR * (m_hat / (jnp.sqrt(v_hat) + ADAM_EPS) + ADAM_WD * w_ref[...])
        nm_ref[...] = nm
        nv_ref[...] = nv

    tile = pl.BlockSpec((tr, cols), lambda i: (i, 0))
    shape = jax.ShapeDtypeStruct((rows, cols), F32)
    return pl.pallas_call(
        body, name=name, grid=(rows // tr,),
        in_specs=[tile] * 4 + ([] if after is None else [ANY]), out_specs=[tile] * 4, out_shape=[shape] * 4,
        compiler_params=_params(1),
    )(w, g, m, v, *([] if after is None else [after]))


def _mesh_pos():
    return lax.axis_index("x"), lax.axis_index("y"), lax.axis_index("c")


def _other_chips(x, y):
    return [(1 - x, y), (x, 1 - y), (1 - x, 1 - y)]


def _half_rows(ref, which):
    rh = ref.shape[-2] // 2
    return ref.at[pl.ds(which * rh, rh), :]


def _remote(src, dst, send_sems, recv_sems, sem, to):
    return pltpu.make_async_remote_copy(src_ref=src, dst_ref=dst, send_sem=send_sems.at[sem], recv_sem=recv_sems.at[sem],
                                        device_id=to, device_id_type=MESH)


HBM = pl.BlockSpec(memory_space=pltpu.HBM)
SEM = pl.BlockSpec(memory_space=pltpu.SEMAPHORE)
DATAFLOW_EFFECT = pltpu.SideEffectType.DATAFLOW_SIDE_EFFECTING


def _in_hbm(arrays):
    return [pltpu.with_memory_space_constraint(a, pltpu.HBM) for a in arrays]


def _hbm_like(arrays):
    return [pltpu.HBM(a.shape, a.dtype) for a in arrays]


GATHER_COPIES_PER_ARRAY = {"direct": 2, "forward": 2, "pass_near": 2, "pass_far": 1}


def _gather_copies(kind, refs, x, y, c):
    me, near_x, near_y, far = 2 * x + y, 2 * (1 - x) + y, 2 * x + (1 - y), 2 * (1 - x) + (1 - y)
    to_x, to_y, sibling = (1 - x, y, c), (x, 1 - y, c), (x, y, 1 - c)
    out = []
    for ref in refs:
        rh = ref.shape[1] // 2
        rq = rh // 2

        def half(chip, ref=ref, rh=rh):
            return ref.at[chip, pl.ds(c * rh, rh), :]

        def quarter(chip, q, ref=ref, rh=rh, rq=rq):
            return ref.at[chip, pl.ds(c * rh + q * rq, rq), :]

        if kind == "direct":
            out += [(half(me), half(me), to_x), (half(me), half(me), to_y)]
        elif kind == "forward":
            out += [(quarter(near_x, 0), quarter(near_x, 0), to_y), (quarter(near_y, 1), quarter(near_y, 1), to_x)]
        elif kind == "pass_near":
            out += [(half(near_x), half(near_x), sibling), (half(near_y), half(near_y), sibling)]
        else:
            assert kind == "pass_far"
            out += [(half(far), half(far), sibling)]
    return out


def _gather_step(name, bufs, waits, starts, after):
    nb, nw, ns = len(bufs), len(waits), len(starts)
    after = [] if after is None else list(after) if isinstance(after, (list, tuple)) else [after]
    n_after = len(after)

    def body(*refs):
        ins = refs[:nb]
        wait_sems = refs[nb:nb + 2 * nw]
        start_sems = refs[nb + 2 * nw + n_after:nb + 2 * nw + n_after + 2 * ns]
        token = refs[-1]
        x, y, c = _mesh_pos()
        for j, (kind, idxs, _, _) in enumerate(waits):
            for i, (s_ref, d_ref, to) in enumerate(_gather_copies(kind, [ins[t] for t in idxs], x, y, c)):
                came = _remote(s_ref, d_ref, wait_sems[2 * j], wait_sems[2 * j + 1], i, to)
                came.wait_recv()
                came.wait_send()
        for j, (kind, idxs) in enumerate(starts):
            for i, (s_ref, d_ref, to) in enumerate(_gather_copies(kind, [ins[t] for t in idxs], x, y, c)):
                _remote(s_ref, d_ref, start_sems[2 * j], start_sems[2 * j + 1], i, to).start()
        token[...] = jnp.zeros_like(token)

    sems = []
    for kind, idxs in starts:
        sems += [pltpu.SemaphoreType.DMA((GATHER_COPIES_PER_ARRAY[kind] * len(idxs),))] * 2
    operands = _in_hbm(bufs) + [sem for w in waits for sem in w[2:]] + after
    outs = pl.pallas_call(
        body, name=name,
        in_specs=[HBM] * nb + [SEM] * (2 * nw) + [ANY] * n_after,
        out_specs=[SEM] * (2 * ns) + [HBM] * nb + [pl.BlockSpec(memory_space=pltpu.VMEM)],
        out_shape=sems + _hbm_like(bufs) + [jax.ShapeDtypeStruct((8, 128), F32)],
        input_output_aliases={i: 2 * ns + i for i in range(nb)},
        compiler_params=pltpu.CompilerParams(has_side_effects=DATAFLOW_EFFECT),
    )(*operands)
    return outs[2 * ns:2 * ns + nb], [(outs[2 * j], outs[2 * j + 1]) for j in range(ns)], outs[-1]


class _Gather:
    def __init__(self, groups):
        self.groups = groups
        self.bufs = {}
        self.in_flight = {}

    def put(self, slotted):
        self.bufs.update(slotted)

    def step(self, name, waits, starts, after=None):
        names = []
        for _, group in list(waits) + list(starts):
            names += [n for n in self.groups[group] if n not in names]
        index = {n: i for i, n in enumerate(names)}

        def members(group):
            return [index[n] for n in self.groups[group]]

        wait_args = [(kind, members(group)) + self.in_flight.pop((kind, group)) for kind, group in waits]
        start_args = [(kind, members(group)) for kind, group in starts]
        bufs, sems, token = _gather_step(name, [self.bufs[n] for n in names], wait_args, start_args, after)
        self.bufs.update(zip(names, bufs))
        for (kind, group), pair in zip(starts, sems):
            self.in_flight[(kind, group)] = pair
        return token

    def arrays(self, group):
        return {n: self.bufs[n] for n in self.groups[group]}


def _sibling_halves_copies(srcs, dsts, x, y, c):
    out = []
    for s_ref, d_ref in zip(srcs, dsts, strict=True):
        rh = s_ref.shape[1] // 2
        out.append((s_ref.at[:, pl.ds((1 - c) * rh, rh), :], d_ref, (x, y, 1 - c)))
    return out


def _to_sibling_copies(srcs, dsts, x, y, c):
    return [(s_ref, d_ref, (x, y, 1 - c)) for s_ref, d_ref in zip(srcs, dsts, strict=True)]


def _chip_copies(srcs, dsts, x, y, c):
    out = []
    for s_ref, d_ref in zip(srcs, dsts, strict=True):
        for k, (px, py) in enumerate(_other_chips(x, y)):
            out.append((s_ref.at[2 * px + py], d_ref.at[k], (px, py, c)))
    return out


def _join_copies(srcs, dsts, x, y, c):
    out = []
    for s_ref in srcs:
        mine = _half_rows(s_ref, c)
        out.append((mine, mine, (x, y, 1 - c)))
    return out


def _exchange_start(copies_fn, n_copies, srcs, fresh, after, name):
    ns, nb = len(srcs), len(srcs) + len(fresh)

    def body(*refs):
        bufs, send, recv, token = refs[:nb], refs[nb + 1], refs[nb + 2], refs[-1]
        x, y, c = _mesh_pos()
        for i, (s_ref, d_ref, to) in enumerate(copies_fn(bufs[:ns], bufs[ns:] if fresh else bufs[:ns], x, y, c)):
            _remote(s_ref, d_ref, send, recv, i, to).start()
        token[...] = jnp.zeros_like(token)

    sems = [pltpu.SemaphoreType.DMA((n_copies,))] * 2
    outs = pl.pallas_call(
        body, name=name,
        in_specs=[HBM] * nb + [ANY], out_specs=[SEM, SEM] + [HBM] * nb + [pl.BlockSpec(memory_space=pltpu.VMEM)],
        out_shape=sems + _hbm_like(list(srcs) + list(fresh)) + [jax.ShapeDtypeStruct((8, 128), F32)],
        input_output_aliases={i: 2 + i for i in range(nb)},
        compiler_params=pltpu.CompilerParams(has_side_effects=DATAFLOW_EFFECT),
    )(*_in_hbm(list(srcs) + list(fresh)), after)
    return outs[0], outs[1], outs[2:2 + ns], outs[2 + ns:2 + nb], outs[-1]


def _exchange_done(copies_fn, srcs, fresh, send, recv, after, name):
    ns, nb = len(srcs), len(srcs) + len(fresh)

    def body(*refs):
        bufs, send_in, recv_in = refs[:nb], refs[nb], refs[nb + 1]
        x, y, c = _mesh_pos()
        for i, (s_ref, d_ref, to) in enumerate(copies_fn(bufs[:ns], bufs[ns:] if fresh else bufs[:ns], x, y, c)):
            came = _remote(s_ref, d_ref, send_in, recv_in, i, to)
            came.wait_send()
            came.wait_recv()

    outs = pl.pallas_call(
        body, name=name,
        in_specs=[HBM] * nb + [SEM, SEM, ANY], out_specs=[HBM] * nb,
        out_shape=_hbm_like(list(srcs) + list(fresh)),
        input_output_aliases={i: i for i in range(nb)},
        compiler_params=pltpu.CompilerParams(has_side_effects=DATAFLOW_EFFECT),
    )(*_in_hbm(list(srcs) + list(fresh)), send, recv, after)
    return outs[:ns], outs[ns:]


class _Reduce:
    def __init__(self, place, core, shards, mom_m, mom_v):
        self.place, self.core = place, core
        self.shards, self.mom_m, self.mom_v = shards, mom_m, mom_v
        self.state = {}
        self.results = {}

    def add(self, group, grads, after):
        names = list(grads)
        g4s = [g.reshape((N_CHIPS, -1, g.shape[-1])) if g.ndim == 2 else g for g in grads.values()]
        fresh = [lax.empty((N_CHIPS, g.shape[1] // 2, g.shape[2]), BF16) for g in g4s]
        send, recv, g4s, fresh, token = _exchange_start(_sibling_halves_copies, len(names), g4s, fresh, after,
                                                        "pair_start_" + group)
        self.state[group] = (0, names, send, recv, g4s, fresh)
        return token

    def send(self, group, theirs, after):
        names, srcs = list(theirs), list(theirs.values())
        fresh = [lax.empty(s.shape, BF16) for s in srcs]
        send, recv, srcs, fresh, token = _exchange_start(_to_sibling_copies, len(names), srcs, fresh, after, "pair_start_" + group)
        self.state[group] = ("sent", names, send, recv, srcs, fresh)
        return token

    def received(self, group, after):
        stage, names, send, recv, srcs, fresh = self.state.pop(group)
        assert stage == "sent"
        _, got = _exchange_done(_to_sibling_copies, srcs, fresh, send, recv, after, "pair_done_" + group)
        return dict(zip(names, got))

    def add_parts(self, group, parts):
        names, srcs = list(parts), list(parts.values())
        fresh = [lax.empty((N_CHIPS - 1,) + p.shape[1:], BF16) for p in srcs]
        send, recv, srcs, fresh, token = _exchange_start(_chip_copies, 3 * len(names), srcs, fresh, self.core, "chips_start_" + group)
        self.state[group] = (1, names, send, recv, srcs, fresh)
        return token

    def step(self, group, after):
        stage, names, send, recv, srcs, fresh = self.state[group]
        if stage == 0:
            g4s, ras = _exchange_done(_sibling_halves_copies, srcs, fresh, send, recv, after, "pair_done_" + group)
            parts = [_pair_sum(g, r, self.core, "pair_sum_" + n) for g, r, n in zip(g4s, ras, names)]
            fresh = [lax.empty((N_CHIPS - 1,) + p.shape[1:], BF16) for p in parts]
            send, recv, parts, fresh, token = _exchange_start(_chip_copies, 3 * len(names), parts, fresh, self.core,
                                                              "chips_start_" + group)
            self.state[group] = (1, names, send, recv, parts, fresh)
            return token
        if stage == 1:
            parts, rcs = _exchange_done(_chip_copies, srcs, fresh, send, recv, after, "chips_done_" + group)
            wholes = [_quad_sum(p, r, self.place, "quad_sum_" + n) for p, r, n in zip(parts, rcs, names)]
            send, recv, wholes, _, token = _exchange_start(_join_copies, len(names), wholes, [], self.core, "join_start_" + group)
            self.state[group] = (2, names, send, recv, wholes, [])
            return token
        assert stage == 2
        wholes, _ = _exchange_done(_join_copies, srcs, [], send, recv, after, "join_done_" + group)
        token = None
        for n, g in zip(names, wholes):
            self.results[n] = _adamw(self.shards[n], g, self.mom_m[n], self.mom_v[n], "adamw_" + n, after=token)
            token = self.results[n][1]
        del self.state[group]
        return token


N_DEV = 8


def _all_reduce_small(v):
    def body(v_ref, o_ref, slots, send_sems, recv_sems):
        x, y, c = _mesh_pos()
        me = 4 * x + 2 * y + c
        slots[me] = v_ref[...]
        peers = []
        for r in range(1, N_DEV):
            fx, fy, fc = (r >> 2) & 1, (r >> 1) & 1, r & 1
            peers.append((x + fx - 2 * x * fx, y + fy - 2 * y * fy, c + fc - 2 * c * fc))
        sends = []
        for r, peer in enumerate(peers):
            cp = _remote(v_ref, slots.at[me], send_sems, recv_sems, r, peer)
            cp.start()
            sends.append(cp)
        for r, (px, py, pc) in enumerate(peers):
            landed = slots.at[4 * px + 2 * py + pc]
            _remote(landed, landed, send_sems, recv_sems, r, (px, py, pc)).wait_recv()
        for cp in sends:
            cp.wait_send()
        acc = slots[0]
        for i in range(1, N_DEV):
            acc = acc + slots[i]
        o_ref[...] = acc

    vm = pl.BlockSpec(memory_space=pltpu.VMEM)
    return pl.pallas_call(
        body, name="small_grads_all_reduce",
        in_specs=[vm], out_specs=vm,
        out_shape=jax.ShapeDtypeStruct(v.shape, v.dtype),
        scratch_shapes=[pltpu.VMEM((N_DEV,) + v.shape, v.dtype), pltpu.SemaphoreType.DMA((N_DEV - 1,)),
                        pltpu.SemaphoreType.DMA((N_DEV - 1,))],
    )(v)


MATRICES = ("w_in", "w_attn_out", "w_conv_out", "w_o", "w_cq", "w_ckv", "w_co", "w_gate", "w_up", "w_down")
VECTORS = ("g_mix", "b_gate", "g_cross", "g_mem", "g_ffn", "g_final", "conv_w", "sink")
WEIGHT_ORDER = ("g_mix", "w_in", "sink", "conv_w", "b_gate", "w_attn_out", "w_conv_out", "w_o", "g_cross", "g_mem", "w_cq",
                "w_ckv", "w_co", "g_ffn", "w_gate", "w_up", "w_down", "g_final")
CONV_PAD_ROWS = 32
SMALL_ROWS = 8


def _pack(pieces):
    flat = jnp.concatenate([p.reshape(-1) for p in pieces])
    lane_group = SMALL_ROWS * 128
    total = -(-flat.shape[0] // lane_group) * lane_group
    flat = jnp.pad(flat, (0, total - flat.shape[0]))
    return flat.reshape(SMALL_ROWS, total // SMALL_ROWS), [p.size for p in pieces]


def _unpack(packed, pieces):
    flat = packed.reshape(-1)
    out, off = [], 0
    for p in pieces:
        out.append(flat[off:off + p.size].reshape(p.shape))
        off += p.size
    return out


def kernel(x, mem, g_mix, w_in, sink, conv_w, b_gate, w_attn_out, w_conv_out, w_o, g_cross, g_mem, w_cq, w_ckv, w_co, g_ffn, w_gate, w_up, w_down, g_final, loss_target, m_g_mix, m_w_in, m_sink, m_conv_w, m_b_gate, m_w_attn_out, m_w_conv_out, m_w_o, m_g_cross, m_g_mem, m_w_cq, m_w_ckv, m_w_co, m_g_ffn, m_w_gate, m_w_up, m_w_down, m_g_final, v_g_mix, v_w_in, v_sink, v_conv_w, v_b_gate, v_w_attn_out, v_w_conv_out, v_w_o, v_g_cross, v_g_mem, v_w_cq, v_w_ckv, v_w_co, v_g_ffn, v_w_gate, v_w_up, v_w_down, v_g_final):
    given = dict(g_mix=g_mix, w_in=w_in, sink=sink, conv_w=conv_w, b_gate=b_gate, w_attn_out=w_attn_out, w_conv_out=w_conv_out,
                 w_o=w_o, g_cross=g_cross, g_mem=g_mem, w_cq=w_cq, w_ckv=w_ckv, w_co=w_co, g_ffn=g_ffn, w_gate=w_gate, w_up=w_up,
                 w_down=w_down, g_final=g_final)
    mom_m = dict(g_mix=m_g_mix, w_in=m_w_in, sink=m_sink, conv_w=m_conv_w, b_gate=m_b_gate, w_attn_out=m_w_attn_out,
                 w_conv_out=m_w_conv_out, w_o=m_w_o, g_cross=m_g_cross, g_mem=m_g_mem, w_cq=m_w_cq, w_ckv=m_w_ckv, w_co=m_w_co,
                 g_ffn=m_g_ffn, w_gate=m_w_gate, w_up=m_w_up, w_down=m_w_down, g_final=m_g_final)
    mom_v = dict(g_mix=v_g_mix, w_in=v_w_in, sink=v_sink, conv_w=v_conv_w, b_gate=v_b_gate, w_attn_out=v_w_attn_out,
                 w_conv_out=v_w_conv_out, w_o=v_w_o, g_cross=v_g_cross, g_mem=v_g_mem, w_cq=v_w_cq, w_ckv=v_w_ckv, w_co=v_w_co,
                 g_ffn=v_g_ffn, w_gate=v_w_gate, w_up=v_w_up, w_down=v_w_down, g_final=v_g_final)
    xs, mems, target = x[0], mem[0], loss_target[0]
    d_model = xs.shape[1]
    chip = 2 * lax.axis_index("x") + lax.axis_index("y")
    core = jnp.reshape(lax.axis_index("c"), (1,)).astype(jnp.int32)
    place = jnp.stack([chip, lax.axis_index("c")]).astype(jnp.int32)

    shards = {n: given[n][0] for n in MATRICES}
    conv_cols = conv_w.shape[2]
    conv_pad = jnp.pad(conv_w[0], ((0, CONV_PAD_ROWS - conv_w.shape[1]), (0, 0)))
    fetch = _Gather(GATHER_GROUPS)
    first = {"w_in": _cast_to_slot(shards["w_in"], place, BF16, "to_slot_w_in"),
             "conv_w": _cast_to_slot(conv_pad, place, F32, "to_slot_conv_w")}
    fetch.put(first)
    tok = fetch.step("gather_start", [], [("direct", "in")])
    fetch.put({n: _cast_to_slot(shards[n], place, BF16, "to_slot_" + n, after=tok) for n in MATRICES if n != "w_in"})
    small = {n: given[n] for n in ("g_mix", "b_gate", "g_cross", "g_mem", "g_ffn")}
    small["g_final"] = g_final[None]
    small["sink"] = sink[0]

    reduce = _Reduce(place, core, shards, {n: mom_m[n][0] for n in MATRICES}, {n: mom_v[n][0] for n in MATRICES})
    sq, grad_x, small_grads = _local_step(xs, mems, target, small, fetch, reduce)

    loss_part = 0.5 * sq[0:1, 0:1] / d_model
    pieces = [small_grads[n] for n in VECTORS] + [loss_part]
    packed, _ = _pack(pieces)
    summed = _unpack(_all_reduce_small(packed), pieces)
    loss = summed[-1][0, 0]
    small_sum = dict(zip(VECTORS, summed[:-1]))
    small_sum["conv_w"] = lax.dynamic_slice_in_dim(small_sum["conv_w"], chip * conv_cols, conv_cols, axis=1)

    grad_out, delta, new_m, new_v = {}, {}, {}, {}
    like = [given[n] for n in VECTORS]
    pw, _ = _pack(like)
    pg, _ = _pack([small_sum[n] for n in VECTORS])
    pm, _ = _pack([mom_m[n] for n in VECTORS])
    pv, _ = _pack([mom_v[n] for n in VECTORS])
    _, pd, pnm, pnv = _adamw(pw, pg, pm, pv, "adamw_small")
    for n, g, d, nm, nv in zip(VECTORS, [small_sum[n] for n in VECTORS], _unpack(pd, like), _unpack(pnm, like), _unpack(pnv, like)):
        grad_out[n] = g.reshape(given[n].shape)
        delta[n], new_m[n], new_v[n] = d, nm, nv
    tok = reduce.step("in", pd)
    reduce.step("in", tok)
    for n in MATRICES:
        g, d, nm, nv = reduce.results[n]
        grad_out[n], delta[n], new_m[n], new_v[n] = g[None], d[None], nm[None], nv[None]

    return (loss, grad_x[None], *[grad_out[n] for n in WEIGHT_ORDER], *[delta[n] for n in WEIGHT_ORDER],
            *[new_m[n] for n in WEIGHT_ORDER], *[new_v[n] for n in WEIGHT_ORDER])
```

```python
import functools

import jax
import jax.numpy as jnp
from jax import lax
from jax.experimental import pallas as pl
from jax.experimental.pallas import tpu as pltpu

F32 = jnp.float32
BF16 = jnp.bfloat16
MESH = pl.DeviceIdType.MESH
ANY = pl.BlockSpec(memory_space=pl.ANY)

VMEM_LIMIT_BYTES = 56 * 1024 * 1024

N_CHIPS = 4
HEAD_DIM = 128
N_Q_HEADS = 8
N_KV_HEADS = 2
Q_GROUP = N_Q_HEADS // N_KV_HEADS
ATTN_WIDTH = N_Q_HEADS * HEAD_DIM
KV_WIDTH = N_KV_HEADS * HEAD_DIM
WINDOW = 128
BLOCK = 128
BAND = 3 * BLOCK
ROPE_THETA = 10000.0
CONV_WIDTH = 1024
MEM_HEADS = 4
MEM_WIDTH = MEM_HEADS * HEAD_DIM
RMS_EPS = 1e-6
NEG_INF = -1e30
ATTN_SCALE = HEAD_DIM ** -0.5

Q_OFF, K_OFF, V_OFF, CU_OFF, CB_OFF, CC_OFF, GL_OFF = 0, 1024, 1280, 1536, 2560, 3584, 4608

ADAM_LR = 0.001
ADAM_B1 = 0.9
ADAM_B2 = 0.999
ADAM_EPS = 1e-08
ADAM_WD = 0.01
ADAM_STEP = 10
ADAM_C1 = 1.0 - ADAM_B1 ** ADAM_STEP
ADAM_C2 = 1.0 - ADAM_B2 ** ADAM_STEP


def _params(n_grid_axes):
    return pltpu.CompilerParams(dimension_semantics=("arbitrary",) * n_grid_axes, vmem_limit_bytes=VMEM_LIMIT_BYTES)


BF16_SUBLANES = 16


def _row_tile(rows, want):
    if rows <= want:
        return rows
    for t in range(want, 0, -BF16_SUBLANES):
        if rows % t == 0:
            return t
    return rows


def _matmul(a, b, *, mode, tm, tn, tk, out_dtypes, name, extras=(), epilogue=None, b_blocks=1, out_blocks=1, after=None):
    if mode == "tn":
        kdim, m = a.shape
    else:
        m, kdim = a.shape
    if b_blocks > 1:
        nb, brows, bcols = b.shape
        assert nb == b_blocks
        if mode == "nn":
            n = bcols * nb
            assert brows == kdim
        else:
            assert mode == "nt" and bcols * nb == kdim
            n = brows
    else:
        n = b.shape[0] if mode == "nt" else b.shape[1]
    tm, tn = min(tm, m), min(tn, n)
    assert m % tm == 0 and n % tn == 0 and tk == kdim, (name, m, n, kdim, tm, tn, tk)
    n_extra, n_out = len(extras), len(out_dtypes)
    n_after = 0 if after is None else 1

    if mode == "tn":
        a_spec = pl.BlockSpec((tk, tm), lambda j, i, k: (k, i))
        dims = (((0,), (0,)), ((), ()))
    else:
        a_spec = pl.BlockSpec((tm, tk), lambda j, i, k: (i, k))
        dims = (((1,), (0,)), ((), ())) if mode == "nn" else (((1,), (1,)), ((), ()))

    if b_blocks > 1 and mode == "nn":
        per = b.shape[2] // tn
        assert b.shape[2] % tn == 0
        b_spec = pl.BlockSpec((None, tk, tn), lambda j, i, k: (j // per, k, j % per))
    elif b_blocks > 1:
        b_spec = pl.BlockSpec((b_blocks, tn, b.shape[2]), lambda j, i, k: (0, j, 0))
    elif mode == "nt":
        b_spec = pl.BlockSpec((tn, tk), lambda j, i, k: (j, k))
    else:
        b_spec = pl.BlockSpec((tk, tn), lambda j, i, k: (k, j))

    tile_spec = pl.BlockSpec((tm, tn), lambda j, i, k: (i, j))
    if out_blocks > 1:
        ncols = n // out_blocks
        assert ncols % tn == 0
        oper = ncols // tn
        out_spec = pl.BlockSpec((None, tm, tn), lambda j, i, k: (j // oper, i, j % oper))
        out_shape = [jax.ShapeDtypeStruct((out_blocks, m, ncols), dt) for dt in out_dtypes]
    else:
        out_spec = tile_spec
        out_shape = [jax.ShapeDtypeStruct((m, n), dt) for dt in out_dtypes]

    def body(a_ref, b_ref, *rest):
        extra_refs = rest[:n_extra]
        out_refs = rest[n_extra + n_after:n_extra + n_after + n_out]
        if mode == "nt" and b_blocks > 1:
            cs = b.shape[2]
            acc = None
            for jb in range(b_blocks):
                prod = lax.dot_general(a_ref[:, jb * cs:(jb + 1) * cs].astype(BF16), b_ref[jb].astype(BF16), dims,
                                       preferred_element_type=F32)
                acc = prod if acc is None else acc + prod
        else:
            acc = lax.dot_general(a_ref[...].astype(BF16), b_ref[...].astype(BF16), dims, preferred_element_type=F32)
        tiles = (acc,) if epilogue is None else epilogue(acc, *[r[...] for r in extra_refs])
        for o_ref, t in zip(out_refs, tiles, strict=True):
            o_ref[...] = t.astype(o_ref.dtype)

    outs = pl.pallas_call(
        body,
        name=name,
        grid=(n // tn, m // tm, 1),
        in_specs=[a_spec, b_spec] + [tile_spec] * n_extra + [ANY] * n_after,
        out_specs=[out_spec] * n_out,
        out_shape=out_shape,
        compiler_params=_params(3),
    )(a, b, *extras, *([] if after is None else [after]))
    return outs[0] if n_out == 1 else outs


def _add_residual(acc, res):
    return (acc + res,)


def _matmul_column_blocks(a, b4, blocks, out, *, tm, name, after=None):
    m, kdim = a.shape
    nb, _, cols = b4.shape
    tm = min(tm, m)
    assert m % tm == 0

    def body(j_ref, a_ref, b_ref, *rest):
        rest[-1][...] = jnp.dot(a_ref[...], b_ref[...], preferred_element_type=F32)

    extra = ([] if out is None else [out]) + ([] if after is None else [after])
    n_blocks = blocks.shape[0]
    return pl.pallas_call(
        body, name=name,
        grid_spec=pltpu.PrefetchScalarGridSpec(
            num_scalar_prefetch=1, grid=(n_blocks, m // tm),
            in_specs=[pl.BlockSpec((tm, kdim), lambda j, i, blk: (i, 0)),
                      pl.BlockSpec((None, kdim, cols), lambda j, i, blk: (blk[j], 0, 0))] + [ANY] * len(extra),
            out_specs=pl.BlockSpec((tm, cols), lambda j, i, blk: (i, blk[j]))),
        out_shape=jax.ShapeDtypeStruct((m, nb * cols), F32),
        input_output_aliases={} if out is None else {3: 0},
        compiler_params=_params(2),
    )(blocks, a, b4, *extra)


def _wgrad_half(a, b, core, *, theirs, row_sharded, tm, tn, name, add=None, after=None):
    kdim, m = a.shape
    n = b.shape[1]
    rs, cs = (m // N_CHIPS, n) if row_sharded else (m, n // N_CHIPS)
    rh = rs // 2
    tm, tn = min(tm, rh), min(tn, cs)
    assert rh % tm == 0 and cs % tn == 0, (name, rh, cs, tm, tn)
    mh, per = rh // tm, cs // tn
    has_add = add is not None

    def half(c):
        return 1 - c[0] if theirs else c[0]

    if row_sharded:
        grid = (n // tn, N_CHIPS * mh)
        a_spec = pl.BlockSpec((kdim, tm), lambda j, r, c: (0, ((r // mh) * 2 + half(c)) * mh + r % mh))
        o_spec = pl.BlockSpec((None, tm, tn), lambda j, r, c: (r // mh, r % mh, j))
    else:
        grid = (n // tn, mh)
        a_spec = pl.BlockSpec((kdim, tm), lambda j, r, c: (0, half(c) * mh + r))
        o_spec = pl.BlockSpec((None, tm, tn), lambda j, r, c: (j // per, r, j % per))
    b_spec = pl.BlockSpec((kdim, tn), lambda j, r, c: (0, j))

    def body(c_ref, a_ref, b_ref, *rest):
        o_ref = rest[-1]
        acc = lax.dot_general(a_ref[...].astype(BF16), b_ref[...].astype(BF16), (((0,), (0,)), ((), ())),
                              preferred_element_type=F32)
        if has_add:
            acc = acc + rest[0][...].astype(F32)
        o_ref[...] = acc.astype(BF16)

    operands = [a, b] + ([add] if has_add else []) + ([] if after is None else [after])
    return pl.pallas_call(
        body, name=name,
        grid_spec=pltpu.PrefetchScalarGridSpec(
            num_scalar_prefetch=1, grid=grid,
            in_specs=[a_spec, b_spec] + ([o_spec] if has_add else []) + ([] if after is None else [ANY]),
            out_specs=o_spec),
        out_shape=jax.ShapeDtypeStruct((N_CHIPS, rh, cs), BF16),
        compiler_params=_params(2),
    )(core, *operands)


def _rstd(x):
    return lax.rsqrt(jnp.mean(x * x, axis=-1, keepdims=True) + RMS_EPS)


def _rmsnorm(x, g, name):
    s, d = x.shape
    tr = _row_tile(s, 256)

    def body(x_ref, g_ref, o_ref):
        xv = x_ref[...]
        o_ref[...] = (xv * _rstd(xv) * g_ref[...]).astype(BF16)

    return pl.pallas_call(
        body, name=name, grid=(s // tr,),
        in_specs=[pl.BlockSpec((tr, d), lambda i: (i, 0)), pl.BlockSpec((1, d), lambda i: (0, 0))],
        out_specs=pl.BlockSpec((tr, d), lambda i: (i, 0)),
        out_shape=jax.ShapeDtypeStruct((s, d), BF16),
        compiler_params=_params(1),
    )(x, g)


def _rmsnorm_bwd(dh, x, g, dres, name):
    s, d = x.shape
    tr = _row_tile(s, 256)
    has_res = dres is not None

    def body(*refs):
        if has_res:
            dh_ref, x_ref, g_ref, res_ref, dx_ref, dxb_ref, dg_ref = refs
        else:
            dh_ref, x_ref, g_ref, dx_ref, dxb_ref, dg_ref = refs
        xv = x_ref[...]
        dhv = dh_ref[...].astype(F32)
        r = _rstd(xv)
        xn = xv * r
        dhg = dhv * g_ref[...]
        dx = r * (dhg - xn * jnp.mean(dhg * xn, axis=-1, keepdims=True))
        if has_res:
            dx = dx + res_ref[...]
        dx_ref[...] = dx
        dxb_ref[...] = dx.astype(BF16)
        part = jnp.sum(dhv * xn, axis=0, keepdims=True)

        @pl.when(pl.program_id(0) == 0)
        def _():
            dg_ref[...] = part

        @pl.when(pl.program_id(0) > 0)
        def _():
            dg_ref[...] += part

    row = pl.BlockSpec((tr, d), lambda i: (i, 0))
    vec = pl.BlockSpec((1, d), lambda i: (0, 0))
    return pl.pallas_call(
        body, name=name, grid=(s // tr,),
        in_specs=[row, row, vec] + ([row] if has_res else []),
        out_specs=[row, row, vec],
        out_shape=[jax.ShapeDtypeStruct((s, d), F32), jax.ShapeDtypeStruct((s, d), BF16), jax.ShapeDtypeStruct((1, d), F32)],
        compiler_params=_params(1),
    )(*([dh, x, g] + ([dres] if has_res else [])))


def _loss_head(x3, g, target):
    s, d = x3.shape
    tr = _row_tile(s, 256)

    def body(x_ref, g_ref, t_ref, dx_ref, dxb_ref, sq_ref, dg_ref):
        xv = x_ref[...]
        gv = g_ref[...]
        r = _rstd(xv)
        xn = xv * r
        err = xn * gv - t_ref[...]
        dy = err * (1.0 / d)
        dyg = dy * gv
        dx = r * (dyg - xn * jnp.mean(dyg * xn, axis=-1, keepdims=True))
        dx_ref[...] = dx
        dxb_ref[...] = dx.astype(BF16)
        sq = jnp.sum(jnp.sum(err * err, axis=1, keepdims=True), axis=0, keepdims=True)
        sq = jnp.broadcast_to(sq, (1, 128))
        part = jnp.sum(dy * xn, axis=0, keepdims=True)

        @pl.when(pl.program_id(0) == 0)
        def _():
            sq_ref[...] = sq
            dg_ref[...] = part

        @pl.when(pl.program_id(0) > 0)
        def _():
            sq_ref[...] += sq
            dg_ref[...] += part

    row = pl.BlockSpec((tr, d), lambda i: (i, 0))
    vec = pl.BlockSpec((1, d), lambda i: (0, 0))
    return pl.pallas_call(
        body, name="loss_head", grid=(s // tr,),
        in_specs=[row, vec, row],
        out_specs=[row, row, pl.BlockSpec((1, 128), lambda i: (0, 0)), vec],
        out_shape=[jax.ShapeDtypeStruct((s, d), F32), jax.ShapeDtypeStruct((s, d), BF16),
                   jax.ShapeDtypeStruct((1, 128), F32), jax.ShapeDtypeStruct((1, d), F32)],
        compiler_params=_params(1),
    )(x3, g, target)


def _rope_tables(s):
    inv = 1.0 / (ROPE_THETA ** (jnp.arange(0, HEAD_DIM, 2, dtype=F32) / HEAD_DIM))
    ang = jnp.arange(s, dtype=F32)[:, None] * inv[None, :]
    cos, sin = jnp.cos(ang), jnp.sin(ang)
    return jnp.concatenate([cos, cos], axis=1), jnp.concatenate([-sin, sin], axis=1)


def _swap_halves(t):
    return pltpu.roll(t, HEAD_DIM // 2, 1)


def _rope_fwd(z, cos_t, sin_t):
    s = z.shape[0]
    tr = _row_tile(s, 256)

    def body(zq_ref, zk_ref, zv_ref, c_ref, s_ref, q_ref, k_ref, v_ref):
        c, sn = c_ref[...], s_ref[...]
        for hd in range(N_Q_HEADS):
            cols = slice(hd * HEAD_DIM, (hd + 1) * HEAD_DIM)
            t = zq_ref[:, cols]
            q_ref[:, cols] = (t * c + _swap_halves(t) * sn).astype(BF16)
        for hd in range(N_KV_HEADS):
            cols = slice(hd * HEAD_DIM, (hd + 1) * HEAD_DIM)
            t = zk_ref[:, cols]
            k_ref[:, cols] = (t * c + _swap_halves(t) * sn).astype(BF16)
        v_ref[...] = zv_ref[...].astype(BF16)

    tab = pl.BlockSpec((tr, HEAD_DIM), lambda i: (i, 0))
    return pl.pallas_call(
        body, name="rope_fwd", grid=(s // tr,),
        in_specs=[pl.BlockSpec((tr, ATTN_WIDTH), lambda i: (i, Q_OFF // ATTN_WIDTH)),
                  pl.BlockSpec((tr, KV_WIDTH), lambda i: (i, K_OFF // KV_WIDTH)),
                  pl.BlockSpec((tr, KV_WIDTH), lambda i: (i, V_OFF // KV_WIDTH)), tab, tab],
        out_specs=[pl.BlockSpec((tr, ATTN_WIDTH), lambda i: (i, 0)), pl.BlockSpec((tr, KV_WIDTH), lambda i: (i, 0)),
                   pl.BlockSpec((tr, KV_WIDTH), lambda i: (i, 0))],
        out_shape=[jax.ShapeDtypeStruct((s, ATTN_WIDTH), BF16), jax.ShapeDtypeStruct((s, KV_WIDTH), BF16),
                   jax.ShapeDtypeStruct((s, KV_WIDTH), BF16)],
        compiler_params=_params(1),
    )(z, z, z, cos_t, sin_t)


def _rope_bwd(dq_rot, dk_rot, dv, cos_t, sin_t):
    s = dq_rot.shape[0]
    tr = _row_tile(s, 256)

    def body(dq_ref, dk_ref, dv_ref, c_ref, s_ref, oq_ref, ok_ref, ov_ref):
        c, sn = c_ref[...], s_ref[...]
        for hd in range(N_Q_HEADS):
            cols = slice(hd * HEAD_DIM, (hd + 1) * HEAD_DIM)
            t = dq_ref[:, cols]
            oq_ref[:, cols] = (t * c + _swap_halves(t * sn)).astype(BF16)
        for hd in range(N_KV_HEADS):
            cols = slice(hd * HEAD_DIM, (hd + 1) * HEAD_DIM)
            t = dk_ref[:, cols]
            ok_ref[:, cols] = (t * c + _swap_halves(t * sn)).astype(BF16)
        ov_ref[...] = dv_ref[...].astype(BF16)

    tab = pl.BlockSpec((tr, HEAD_DIM), lambda i: (i, 0))
    wide = pl.BlockSpec((tr, ATTN_WIDTH), lambda i: (i, 0))
    narrow = pl.BlockSpec((tr, KV_WIDTH), lambda i: (i, 0))
    return pl.pallas_call(
        body, name="rope_bwd", grid=(s // tr,),
        in_specs=[wide, narrow, narrow, tab, tab],
        out_specs=[wide, narrow, narrow],
        out_shape=[jax.ShapeDtypeStruct((s, ATTN_WIDTH), BF16), jax.ShapeDtypeStruct((s, KV_WIDTH), BF16),
                   jax.ShapeDtypeStruct((s, KV_WIDTH), BF16)],
        compiler_params=_params(1),
    )(dq_rot, dk_rot, dv, cos_t, sin_t)


def _swa_band(i, s):
    return pl.multiple_of(jnp.clip((i - 1) * BLOCK, 0, s - BAND), BLOCK)


def _swa_probs(q_ref, k_ref, sink_ref, kv, start, valid):
    cols = slice(kv * HEAD_DIM, (kv + 1) * HEAD_DIM)
    kb = k_ref[pl.ds(start, BAND), cols]
    heads = [kv * Q_GROUP + g for g in range(Q_GROUP)]
    qg = jnp.concatenate([q_ref[:, hd * HEAD_DIM:(hd + 1) * HEAD_DIM] for hd in heads], axis=0)
    sc = lax.dot_general(qg, kb, (((1,), (1,)), ((), ())), preferred_element_type=F32) * ATTN_SCALE
    sc = jnp.where(valid, sc, NEG_INF)
    sk = jnp.concatenate([jnp.full((BLOCK, 1), sink_ref[hd], F32) for hd in heads], axis=0)
    mx = jnp.maximum(jnp.max(sc, axis=1, keepdims=True), sk)
    e = jnp.exp(sc - mx)
    es = jnp.exp(sk - mx)
    inv = 1.0 / (jnp.sum(e, axis=1, keepdims=True) + es)
    return qg, kb, e * inv, es * inv


def _swa_valid(i, start):
    q_pos = i * BLOCK + lax.broadcasted_iota(jnp.int32, (BLOCK, 1), 0)
    q_pos = jnp.concatenate([q_pos] * Q_GROUP, axis=0)
    k_pos = start + lax.broadcasted_iota(jnp.int32, (1, BAND), 1)
    return jnp.abs(k_pos - q_pos) <= WINDOW


def _swa_fwd(q, k, v, sink):
    s = q.shape[0]
    assert s % BLOCK == 0 and s >= BAND

    def body(sink_ref, q_ref, k_ref, v_ref, o_ref):
        i = pl.program_id(0)
        start = _swa_band(i, s)
        valid = _swa_valid(i, start)
        for kv in range(N_KV_HEADS):
            _, _, p, _ = _swa_probs(q_ref, k_ref, sink_ref, kv, start, valid)
            vb = v_ref[pl.ds(start, BAND), kv * HEAD_DIM:(kv + 1) * HEAD_DIM]
            o = jnp.dot(p.astype(BF16), vb, preferred_element_type=F32)
            for g in range(Q_GROUP):
                hd = kv * Q_GROUP + g
                o_ref[:, hd * HEAD_DIM:(hd + 1) * HEAD_DIM] = o[g * BLOCK:(g + 1) * BLOCK].astype(BF16)

    whole = pl.BlockSpec((s, KV_WIDTH), lambda i: (0, 0))
    blk = pl.BlockSpec((BLOCK, ATTN_WIDTH), lambda i: (i, 0))
    return pl.pallas_call(
        body, name="swa_fwd", grid=(s // BLOCK,),
        in_specs=[pl.BlockSpec(memory_space=pltpu.SMEM), blk, whole, whole],
        out_specs=blk,
        out_shape=jax.ShapeDtypeStruct((s, ATTN_WIDTH), BF16),
        compiler_params=_params(1),
    )(sink, q, k, v)


def _swa_bwd(q, k, v, d_out, sink):
    s = q.shape[0]

    def body(sink_ref, q_ref, k_ref, v_ref, do_ref, dq_ref, dk_ref, dv_ref, dsink_ref):
        i = pl.program_id(0)

        @pl.when(i == 0)
        def _():
            dk_ref[...] = jnp.zeros_like(dk_ref)
            dv_ref[...] = jnp.zeros_like(dv_ref)
            dsink_ref[...] = jnp.zeros_like(dsink_ref)

        start = _swa_band(i, s)
        valid = _swa_valid(i, start)
        for kv in range(N_KV_HEADS):
            cols = slice(kv * HEAD_DIM, (kv + 1) * HEAD_DIM)
            qg, kb, p, p_sink = _swa_probs(q_ref, k_ref, sink_ref, kv, start, valid)
            vb = v_ref[pl.ds(start, BAND), cols]
            heads = [kv * Q_GROUP + g for g in range(Q_GROUP)]
            dog = jnp.concatenate([do_ref[:, hd * HEAD_DIM:(hd + 1) * HEAD_DIM] for hd in heads], axis=0)
            dp = lax.dot_general(dog, vb, (((1,), (1,)), ((), ())), preferred_element_type=F32)
            delta = jnp.sum(p * dp, axis=1, keepdims=True)
            ds = (p * (dp - delta) * ATTN_SCALE).astype(BF16)
            dqg = jnp.dot(ds, kb, preferred_element_type=F32)
            dk_ref[pl.ds(start, BAND), cols] += lax.dot_general(ds, qg, (((0,), (0,)), ((), ())), preferred_element_type=F32)
            dv_ref[pl.ds(start, BAND), cols] += lax.dot_general(p.astype(BF16), dog, (((0,), (0,)), ((), ())),
                                                                 preferred_element_type=F32)
            dsk = p_sink * delta
            for g, hd in enumerate(heads):
                dq_ref[:, hd * HEAD_DIM:(hd + 1) * HEAD_DIM] = dqg[g * BLOCK:(g + 1) * BLOCK]
                tot = jnp.sum(dsk[g * BLOCK:(g + 1) * BLOCK], axis=0, keepdims=True)
                dsink_ref[hd:hd + 1, :] -= jnp.broadcast_to(tot, (1, 128))

    whole = pl.BlockSpec((s, KV_WIDTH), lambda i: (0, 0))
    blk = pl.BlockSpec((BLOCK, ATTN_WIDTH), lambda i: (i, 0))
    return pl.pallas_call(
        body, name="swa_bwd", grid=(s // BLOCK,),
        in_specs=[pl.BlockSpec(memory_space=pltpu.SMEM), blk, whole, whole, blk],
        out_specs=[blk, whole, whole, pl.BlockSpec((N_Q_HEADS, 128), lambda i: (0, 0))],
        out_shape=[jax.ShapeDtypeStruct((s, ATTN_WIDTH), F32), jax.ShapeDtypeStruct((s, KV_WIDTH), F32),
                   jax.ShapeDtypeStruct((s, KV_WIDTH), F32), jax.ShapeDtypeStruct((N_Q_HEADS, 128), F32)],
        compiler_params=_params(1),
    )(sink, q, k, v, d_out)


CONV_CHUNK = 256


def _shift_rows(t, rows, down):
    n = t.shape[0]
    rolled = pltpu.roll(t, 1 if down else n - 1, 0)
    edge = 0 if down else n - 1
    return jnp.where(rows == edge, 0.0, rolled)


def _conv_specs(s):
    def z_spec(off):
        return pl.BlockSpec((s, CONV_CHUNK), lambda j, off=off: (0, off // CONV_CHUNK + j))
    chunk = pl.BlockSpec((s, CONV_CHUNK), lambda j: (0, j))
    w_spec = pl.BlockSpec((3, CONV_CHUNK), lambda j: (0, j))
    return z_spec(CU_OFF), z_spec(CB_OFF), z_spec(CC_OFF), chunk, w_spec


def _conv_fwd(z, conv_w):
    s = z.shape[0]
    cu_spec, cb_spec, cc_spec, chunk, w_spec = _conv_specs(s)

    def body(cu_ref, cb_ref, cc_ref, w_ref, o_ref):
        rows = lax.broadcasted_iota(jnp.int32, (s, 1), 0)
        t = cc_ref[...] * cu_ref[...]
        c3 = _shift_rows(t, rows, True) * w_ref[0:1, :] + t * w_ref[1:2, :] + _shift_rows(t, rows, False) * w_ref[2:3, :]
        o_ref[...] = (cb_ref[...] * c3).astype(BF16)

    return pl.pallas_call(
        body, name="conv_fwd", grid=(CONV_WIDTH // CONV_CHUNK,),
        in_specs=[cu_spec, cb_spec, cc_spec, w_spec],
        out_specs=chunk,
        out_shape=jax.ShapeDtypeStruct((s, CONV_WIDTH), BF16),
        compiler_params=_params(1),
    )(z, z, z, conv_w)


def _conv_bwd(z, conv_w, d_co):
    s = z.shape[0]
    cu_spec, cb_spec, cc_spec, chunk, w_spec = _conv_specs(s)

    def body(cu_ref, cb_ref, cc_ref, w_ref, d_ref, dcu_ref, dcb_ref, dcc_ref, dw_ref):
        rows = lax.broadcasted_iota(jnp.int32, (s, 1), 0)
        cu, cc = cu_ref[...], cc_ref[...]
        t = cc * cu
        t_dn, t_up = _shift_rows(t, rows, True), _shift_rows(t, rows, False)
        c3 = t_dn * w_ref[0:1, :] + t * w_ref[1:2, :] + t_up * w_ref[2:3, :]
        d = d_ref[...]
        dcb_ref[...] = (d * c3).astype(BF16)
        dc3 = d * cb_ref[...]
        dw_ref[0:1, :] = jnp.sum(dc3 * t_dn, axis=0, keepdims=True)
        dw_ref[1:2, :] = jnp.sum(dc3 * t, axis=0, keepdims=True)
        dw_ref[2:3, :] = jnp.sum(dc3 * t_up, axis=0, keepdims=True)
        dt = _shift_rows(dc3, rows, False) * w_ref[0:1, :] + dc3 * w_ref[1:2, :] + _shift_rows(dc3, rows, True) * w_ref[2:3, :]
        dcc_ref[...] = (dt * cu).astype(BF16)
        dcu_ref[...] = (dt * cc).astype(BF16)

    return pl.pallas_call(
        body, name="conv_bwd", grid=(CONV_WIDTH // CONV_CHUNK,),
        in_specs=[cu_spec, cb_spec, cc_spec, w_spec, chunk],
        out_specs=[chunk, chunk, chunk, w_spec],
        out_shape=[jax.ShapeDtypeStruct((s, CONV_WIDTH), BF16)] * 3 + [jax.ShapeDtypeStruct((3, CONV_WIDTH), F32)],
        compiler_params=_params(1),
    )(z, z, z, conv_w, d_co)


GATE_CHUNK = 512


def _gate_specs(s, d, tr):
    n_chunks = d // GATE_CHUNK
    za = pl.BlockSpec((tr, GATE_CHUNK), lambda j, i: (i, GL_OFF // GATE_CHUNK + j))
    zc = pl.BlockSpec((tr, GATE_CHUNK), lambda j, i: (i, GL_OFF // GATE_CHUNK + n_chunks + j))
    ba = pl.BlockSpec((1, GATE_CHUNK), lambda j, i: (0, j))
    bc = pl.BlockSpec((1, GATE_CHUNK), lambda j, i: (0, n_chunks + j))
    tile = pl.BlockSpec((tr, GATE_CHUNK), lambda j, i: (i, j))
    return za, zc, ba, bc, tile


def _gate_fwd(z, b_gate, ya, yc):
    s, d = ya.shape
    tr = _row_tile(s, 512)
    za, zc, ba, bc, tile = _gate_specs(s, d, tr)

    def body(za_ref, zc_ref, ba_ref, bc_ref, ya_ref, yc_ref, o_ref):
        ga = jax.nn.sigmoid(za_ref[...] + ba_ref[...])
        gc = jax.nn.sigmoid(zc_ref[...] + bc_ref[...])
        o_ref[...] = (ga * ya_ref[...] + gc * yc_ref[...]).astype(BF16)

    return pl.pallas_call(
        body, name="gate_fwd", grid=(d // GATE_CHUNK, s // tr),
        in_specs=[za, zc, ba, bc, tile, tile],
        out_specs=tile,
        out_shape=jax.ShapeDtypeStruct((s, d), BF16),
        compiler_params=_params(2),
    )(z, z, b_gate, b_gate, ya, yc)


def _gate_bwd(z, b_gate, ya, yc, dmix):
    s, d = ya.shape
    tr = _row_tile(s, 512)
    za, zc, ba, bc, tile = _gate_specs(s, d, tr)
    vec = pl.BlockSpec((1, GATE_CHUNK), lambda j, i: (0, j))

    def body(za_ref, zc_ref, ba_ref, bc_ref, ya_ref, yc_ref, dm_ref, dya_ref, dyc_ref, dla_ref, dlc_ref, dba_ref, dbc_ref):
        ga = jax.nn.sigmoid(za_ref[...] + ba_ref[...])
        gc = jax.nn.sigmoid(zc_ref[...] + bc_ref[...])
        dm = dm_ref[...]
        dya_ref[...] = (dm * ga).astype(BF16)
        dyc_ref[...] = (dm * gc).astype(BF16)
        dla = dm * ya_ref[...] * ga * (1.0 - ga)
        dlc = dm * yc_ref[...] * gc * (1.0 - gc)
        dla_ref[...] = dla.astype(BF16)
        dlc_ref[...] = dlc.astype(BF16)
        pa = jnp.sum(dla, axis=0, keepdims=True)
        pc = jnp.sum(dlc, axis=0, keepdims=True)

        @pl.when(pl.program_id(1) == 0)
        def _():
            dba_ref[...] = pa
            dbc_ref[...] = pc

        @pl.when(pl.program_id(1) > 0)
        def _():
            dba_ref[...] += pa
            dbc_ref[...] += pc

    big = jax.ShapeDtypeStruct((s, d), BF16)
    small = jax.ShapeDtypeStruct((1, d), F32)
    return pl.pallas_call(
        body, name="gate_bwd", grid=(d // GATE_CHUNK, s // tr),
        in_specs=[za, zc, ba, bc, tile, tile, tile],
        out_specs=[tile, tile, tile, tile, vec, vec],
        out_shape=[big, big, big, big, small, small],
        compiler_params=_params(2),
    )(z, z, b_gate, b_gate, ya, yc, dmix)


def _cross_probs(q_ref, kv_ref, hd):
    cols = slice(hd * HEAD_DIM, (hd + 1) * HEAD_DIM)
    qh = q_ref[:, cols]
    kh = kv_ref[:, cols]
    sc = lax.dot_general(qh, kh, (((1,), (1,)), ((), ())), preferred_element_type=F32) * ATTN_SCALE
    e = jnp.exp(sc - jnp.max(sc, axis=1, keepdims=True))
    return qh, kh, e * (1.0 / jnp.sum(e, axis=1, keepdims=True))


def _cross_fwd(qc, kvc):
    s = qc.shape[0]
    n_mem = kvc.shape[0]
    tq = _row_tile(s, 256)

    def body(q_ref, kv_ref, o_ref):
        for hd in range(MEM_HEADS):
            _, _, p = _cross_probs(q_ref, kv_ref, hd)
            vh = kv_ref[:, MEM_WIDTH + hd * HEAD_DIM:MEM_WIDTH + (hd + 1) * HEAD_DIM]
            o_ref[:, hd * HEAD_DIM:(hd + 1) * HEAD_DIM] = jnp.dot(p.astype(BF16), vh, preferred_element_type=F32).astype(BF16)

    return pl.pallas_call(
        body, name="cross_fwd", grid=(s // tq,),
        in_specs=[pl.BlockSpec((tq, MEM_WIDTH), lambda i: (i, 0)), pl.BlockSpec((n_mem, 2 * MEM_WIDTH), lambda i: (0, 0))],
        out_specs=pl.BlockSpec((tq, MEM_WIDTH), lambda i: (i, 0)),
        out_shape=jax.ShapeDtypeStruct((s, MEM_WIDTH), BF16),
        compiler_params=_params(1),
    )(qc, kvc)


def _cross_bwd(qc, kvc, d_out):
    s = qc.shape[0]
    n_mem = kvc.shape[0]
    tq = _row_tile(s, 256)

    def body(q_ref, kv_ref, do_ref, dq_ref, dkv_ref):
        @pl.when(pl.program_id(0) == 0)
        def _():
            dkv_ref[...] = jnp.zeros_like(dkv_ref)

        for hd in range(MEM_HEADS):
            cols = slice(hd * HEAD_DIM, (hd + 1) * HEAD_DIM)
            vcols = slice(MEM_WIDTH + hd * HEAD_DIM, MEM_WIDTH + (hd + 1) * HEAD_DIM)
            qh, kh, p = _cross_probs(q_ref, kv_ref, hd)
            doh = do_ref[:, cols]
            dp = lax.dot_general(doh, kv_ref[:, vcols], (((1,), (1,)), ((), ())), preferred_element_type=F32)
            ds = (p * (dp - jnp.sum(p * dp, axis=1, keepdims=True)) * ATTN_SCALE).astype(BF16)
            dq_ref[:, cols] = jnp.dot(ds, kh, preferred_element_type=F32).astype(BF16)
            dkv_ref[:, cols] += lax.dot_general(ds, qh, (((0,), (0,)), ((), ())), preferred_element_type=F32)
            dkv_ref[:, vcols] += lax.dot_general(p.astype(BF16), doh, (((0,), (0,)), ((), ())), preferred_element_type=F32)

    qspec = pl.BlockSpec((tq, MEM_WIDTH), lambda i: (i, 0))
    kvspec = pl.BlockSpec((n_mem, 2 * MEM_WIDTH), lambda i: (0, 0))
    return pl.pallas_call(
        body, name="cross_bwd", grid=(s // tq,),
        in_specs=[qspec, kvspec, qspec],
        out_specs=[qspec, kvspec],
        out_shape=[jax.ShapeDtypeStruct((s, MEM_WIDTH), BF16), jax.ShapeDtypeStruct((n_mem, 2 * MEM_WIDTH), F32)],
        compiler_params=_params(1),
    )(qc, kvc, d_out)


def _swiglu_fwd(up, gate):
    return up, (gate * jax.nn.sigmoid(gate)) * up


def _swiglu_bwd(d_act, gate, up):
    sg = jax.nn.sigmoid(gate)
    silu = gate * sg
    return d_act * up * (sg * (1.0 + gate * (1.0 - sg))), d_act * silu


GATHER_GROUPS = {"in": ("w_in", "conv_w"), "mid": ("w_attn_out", "w_conv_out", "w_o", "w_cq", "w_ckv", "w_co"),
                 "gate": ("w_gate",), "up": ("w_up",), "down": ("w_down",)}


def _local_step(xs, mems, target, small, fetch, reduce):
    s, d = xs.shape
    w4 = {}
    cos_t, sin_t = _rope_tables(s)

    def near(group, done, then, after):
        waits = [("direct", group)] + ([("pass_near", done), ("pass_far", done)] if done else [])
        starts = [("forward", group), ("pass_near", group)] + [("direct", g) for g in then]
        tok = fetch.step("gather_near_" + group, waits, starts, after)
        if done:
            w4.update(fetch.arrays(done))
        return tok

    def far(group, then, after):
        return fetch.step("gather_far_" + group, [("forward", group)], [("pass_far", group)] + [("direct", g) for g in then], after)

    def last(group, after):
        tok = fetch.step("gather_done_" + group, [("pass_near", group), ("pass_far", group)], [], after)
        w4.update(fetch.arrays(group))
        return tok

    h = _rmsnorm(xs, small["g_mix"], "norm_mix")
    slots_filled = [a for g in ("gate", "up", "down") for a in fetch.arrays(g).values()]
    chip_x, chip_y = reduce.place[0] // 2, reduce.place[0] % 2
    own_block = jnp.stack([2 * chip_x + chip_y]).astype(jnp.int32)
    near_blocks = jnp.stack([2 * (1 - chip_x) + chip_y, 2 * chip_x + (1 - chip_y)]).astype(jnp.int32)
    far_block = jnp.stack([2 * (1 - chip_x) + (1 - chip_y)]).astype(jnp.int32)
    z = _matmul_column_blocks(h, fetch.arrays("in")["w_in"], own_block, None, tm=512, name="in_proj_own")
    tok = near("in", None, ["mid"], [z] + slots_filled)
    tok = fetch.step("gather_near_done_in", [("pass_near", "in")], [], tok)
    z = _matmul_column_blocks(h, fetch.arrays("in")["w_in"], near_blocks, z, tm=512, name="in_proj_near", after=tok)
    tok = far("in", [], z)
    tok = fetch.step("gather_done_in", [("pass_far", "in")], [], tok)
    w4.update(fetch.arrays("in"))
    z = _matmul_column_blocks(h, w4["w_in"], far_block, z, tm=512, name="in_proj_far", after=tok)
    conv4 = w4["conv_w"]
    conv_w = conv4[:, :3, :].transpose(1, 0, 2).reshape(3, N_CHIPS * conv4.shape[2])
    c_in = w4["w_in"].shape[2]
    tok = near("mid", None, ["gate"], z)
    q_rot, k_rot, v_b = _rope_fwd(z, cos_t, sin_t)
    attn = _swa_fwd(q_rot, k_rot, v_b, small["sink"])
    co = _conv_fwd(z, conv_w)
    tok = far("mid", ["up"], attn)
    tok = last("mid", tok)
    w_o = w4["w_o"].reshape(-1, w4["w_o"].shape[-1])
    c_d = w4["w_attn_out"].shape[2]
    ya = _matmul(attn, w4["w_attn_out"], mode="nn", tm=1024, tn=c_d, tk=ATTN_WIDTH, out_dtypes=[F32], name="attn_out_proj",
                 b_blocks=N_CHIPS, after=tok)
    yc = _matmul(co, w4["w_conv_out"], mode="nn", tm=1024, tn=c_d, tk=CONV_WIDTH, out_dtypes=[F32], name="conv_out_proj",
                 b_blocks=N_CHIPS)
    mix = _gate_fwd(z, small["b_gate"], ya, yc)
    x1 = _matmul(mix, w_o, mode="nn", tm=512, tn=1024, tk=d, out_dtypes=[F32], name="mix_out_proj", extras=[xs],
                 epilogue=_add_residual)
    tok = near("gate", None, ["down"], x1)
    w_cq = w4["w_cq"].reshape(-1, w4["w_cq"].shape[-1])
    w_ckv = w4["w_ckv"].reshape(-1, w4["w_ckv"].shape[-1])
    hc = _rmsnorm(x1, small["g_cross"], "norm_cross")
    memn = _rmsnorm(mems, small["g_mem"], "norm_mem")
    qc = _matmul(hc, w_cq, mode="nn", tm=1024, tn=MEM_WIDTH, tk=d, out_dtypes=[BF16], name="cross_q_proj", after=tok)
    kvc = _matmul(memn, w_ckv, mode="nn", tm=256, tn=2 * MEM_WIDTH, tk=d, out_dtypes=[BF16], name="cross_kv_proj")
    oc = _cross_fwd(qc, kvc)
    x2 = _matmul(oc, w4["w_co"], mode="nn", tm=1024, tn=c_d, tk=MEM_WIDTH, out_dtypes=[F32], name="cross_out_proj",
                 extras=[x1], epilogue=_add_residual, b_blocks=N_CHIPS)
    hf = _rmsnorm(x2, small["g_ffn"], "norm_ffn")
    tok = far("gate", [], hf)
    tok = near("up", "gate", [], tok)
    c_ff = w4["w_gate"].shape[2]
    gate = _matmul(hf, w4["w_gate"], mode="nn", tm=512, tn=c_ff, tk=d, out_dtypes=[F32], name="ffn_gate_proj", b_blocks=N_CHIPS,
                   after=tok)
    tok = far("up", [], gate)
    tok = near("down", "up", [], tok)
    up, act = _matmul(hf, w4["w_up"], mode="nn", tm=512, tn=c_ff, tk=d, out_dtypes=[F32, BF16], name="ffn_up_proj",
                      extras=[gate], epilogue=_swiglu_fwd, b_blocks=N_CHIPS, after=tok)
    tok = far("down", [], act)
    last("down", tok)
    w_down = w4["w_down"].reshape(-1, w4["w_down"].shape[-1])
    x3 = _matmul(act, w_down, mode="nn", tm=512, tn=512, tk=w_down.shape[0], out_dtypes=[F32], name="ffn_down_proj", extras=[x2],
                 epilogue=_add_residual)
    dx3, dx3b, sq, dg_final = _loss_head(x3, small["g_final"], target)

    da, du = _matmul(dx3b, w_down, mode="nt", tm=512, tn=c_ff, tk=d, out_dtypes=[BF16, BF16], name="ffn_down_bwd",
                     extras=[gate, up], epilogue=_swiglu_bwd)
    core = reduce.core
    ffn_shape = dict(row_sharded=False, tm=1024, tn=c_ff)
    g_down = _matmul(act, dx3b, mode="tn", tm=c_ff, tn=1024, tk=s, out_dtypes=[BF16], name="ffn_down_wgrad")
    tok = reduce.add("down", {"w_down": g_down}, da)
    t_gate = _wgrad_half(hf, da, core, theirs=True, name="ffn_gate_wgrad_theirs", after=tok, **ffn_shape)
    tok = reduce.step("down", t_gate)
    t_up = _wgrad_half(hf, du, core, theirs=True, name="ffn_up_wgrad_theirs", after=tok, **ffn_shape)
    tok = reduce.send("ffn", {"w_gate": t_gate, "w_up": t_up}, dx3b)
    dhf = _matmul(da, w4["w_gate"], mode="nt", tm=512, tn=1024, tk=N_CHIPS * c_ff, out_dtypes=[F32], name="ffn_gate_bwd", b_blocks=N_CHIPS,
                  after=tok)
    got = reduce.received("ffn", dhf)
    p_gate = _wgrad_half(hf, da, core, theirs=False, name="ffn_gate_wgrad_mine", add=got["w_gate"], **ffn_shape)
    p_up = _wgrad_half(hf, du, core, theirs=False, name="ffn_up_wgrad_mine", add=got["w_up"], **ffn_shape)
    tok = reduce.add_parts("ffn", {"w_gate": p_gate, "w_up": p_up})
    dhf = _matmul(du, w4["w_up"], mode="nt", tm=512, tn=1024, tk=N_CHIPS * c_ff, out_dtypes=[F32], name="ffn_up_bwd", extras=[dhf],
                  epilogue=_add_residual, b_blocks=N_CHIPS, after=tok)
    tok = reduce.step("down", dhf)
    dx2, dx2b, dg_ffn = _rmsnorm_bwd(dhf, x2, small["g_ffn"], dx3, "norm_ffn_bwd")

    d_oc = _matmul(dx2b, w4["w_co"], mode="nt", tm=1024, tn=MEM_WIDTH, tk=d, out_dtypes=[BF16], name="cross_out_bwd",
                   b_blocks=N_CHIPS, after=tok)
    g_co = _matmul(oc, dx2b, mode="tn", tm=MEM_WIDTH, tn=c_d, tk=s, out_dtypes=[BF16], name="cross_out_wgrad", out_blocks=N_CHIPS)
    tok = reduce.step("down", g_co)
    dqc, dkvc = _cross_bwd(qc, kvc, d_oc)
    g_cq = _matmul(hc, dqc, mode="tn", tm=1024, tn=MEM_WIDTH, tk=s, out_dtypes=[BF16], name="cross_q_wgrad", after=tok)
    dhc = _matmul(dqc, w_cq, mode="nt", tm=1024, tn=1024, tk=MEM_WIDTH, out_dtypes=[F32], name="cross_q_bwd")
    g_ckv = _matmul(memn, dkvc, mode="tn", tm=1024, tn=2 * MEM_WIDTH, tk=mems.shape[0], out_dtypes=[BF16], name="cross_kv_wgrad")
    dmemn = _matmul(dkvc, w_ckv, mode="nt", tm=256, tn=1024, tk=2 * MEM_WIDTH, out_dtypes=[F32], name="cross_kv_bwd")
    _, _, dg_mem = _rmsnorm_bwd(dmemn, mems, small["g_mem"], None, "norm_mem_bwd")
    dx1, dx1b, dg_cross = _rmsnorm_bwd(dhc, x1, small["g_cross"], dx2, "norm_cross_bwd")

    dmix = _matmul(dx1b, w_o, mode="nt", tm=512, tn=1024, tk=d, out_dtypes=[F32], name="mix_out_bwd")
    g_o = _matmul(mix, dx1b, mode="tn", tm=1024, tn=1024, tk=s, out_dtypes=[BF16], name="mix_out_wgrad")
    dya, dyc, dgl_a, dgl_c, db_a, db_c = _gate_bwd(z, small["b_gate"], ya, yc, dmix)
    d_attn = _matmul(dya, w4["w_attn_out"], mode="nt", tm=1024, tn=ATTN_WIDTH, tk=d, out_dtypes=[BF16], name="attn_out_bwd",
                     b_blocks=N_CHIPS)
    g_ao = _matmul(attn, dya, mode="tn", tm=ATTN_WIDTH, tn=c_d, tk=s, out_dtypes=[BF16], name="attn_out_wgrad", out_blocks=N_CHIPS)
    d_co = _matmul(dyc, w4["w_conv_out"], mode="nt", tm=1024, tn=CONV_WIDTH, tk=d, out_dtypes=[F32], name="conv_out_bwd",
                   b_blocks=N_CHIPS)
    g_cvo = _matmul(co, dyc, mode="tn", tm=CONV_WIDTH, tn=c_d, tk=s, out_dtypes=[BF16], name="conv_out_wgrad", out_blocks=N_CHIPS)
    tok = reduce.step("ffn", g_cvo)
    tok = reduce.add("mid", {"w_co": g_co, "w_cq": g_cq, "w_ckv": g_ckv, "w_o": g_o, "w_attn_out": g_ao, "w_conv_out": g_cvo}, tok)
    dcu, dcb, dcc, d_conv_w = _conv_bwd(z, conv_w, d_co)
    dq_rot, dk_rot, dv, dsink = _swa_bwd(q_rot, k_rot, v_b, d_attn, small["sink"])
    tok = reduce.step("mid", dq_rot)
    dq, dk, dvb = _rope_bwd(dq_rot, dk_rot, dv, cos_t, sin_t)
    dz = jnp.concatenate([dq, dk, dvb, dcu, dcb, dcc, dgl_a, dgl_c], axis=1)
    in_shape = dict(row_sharded=False, tm=1024, tn=c_in)
    t_in = _wgrad_half(h, dz, core, theirs=True, name="in_proj_wgrad_theirs", after=tok, **in_shape)
    tok = reduce.send("in", {"w_in": t_in}, dk)
    tok = reduce.step("ffn", tok)
    tok = reduce.step("mid", tok)
    got = reduce.received("in", tok)
    p_in = _wgrad_half(h, dz, core, theirs=False, name="in_proj_wgrad_mine", add=got["w_in"], **in_shape)
    tok = reduce.add_parts("in", {"w_in": p_in})
    dh = _matmul(dz, w4["w_in"], mode="nt", tm=512, tn=512, tk=N_CHIPS * c_in, out_dtypes=[F32], name="in_proj_bwd", b_blocks=N_CHIPS,
                 after=tok)
    tok = reduce.step("mid", dh)
    grad_x, _, dg_mix = _rmsnorm_bwd(dh, xs, small["g_mix"], dx1, "norm_mix_bwd")

    small_grads = {
        "g_mix": dg_mix, "sink": dsink[:, 0], "b_gate": jnp.concatenate([db_a, db_c], axis=1), "g_cross": dg_cross,
        "g_mem": dg_mem, "g_ffn": dg_ffn, "g_final": dg_final, "conv_w": d_conv_w,
    }
    return sq, grad_x, small_grads


def _pair_sum(g4, ra, core, name):
    nb, rs, cs = g4.shape
    rh = rs // 2
    tr = _row_tile(rh, 256)
    per = rh // tr

    def body(c_ref, g_ref, r_ref, o_ref):
        o_ref[...] = (g_ref[...].astype(F32) + r_ref[...].astype(F32)).astype(BF16)

    plain = pl.BlockSpec((None, tr, cs), lambda j, i, c: (j, i, 0))
    return pl.pallas_call(
        body, name=name,
        grid_spec=pltpu.PrefetchScalarGridSpec(
            num_scalar_prefetch=1, grid=(nb, per),
            in_specs=[pl.BlockSpec((None, tr, cs), lambda j, i, c: (j, c[0] * per + i, 0)), plain],
            out_specs=plain),
        out_shape=jax.ShapeDtypeStruct((nb, rh, cs), BF16),
        compiler_params=_params(2),
    )(core, g4, ra)


def _quad_sum(parts, rc, place, name):
    _, rh, cs = parts.shape
    tr = _row_tile(rh, 256)
    per = rh // tr

    def body(p_ref, own_ref, r_ref, o_ref):
        acc = own_ref[...].astype(F32)
        for j in range(rc.shape[0]):
            acc = acc + r_ref[j].astype(F32)
        o_ref[...] = acc

    return pl.pallas_call(
        body, name=name,
        grid_spec=pltpu.PrefetchScalarGridSpec(
            num_scalar_prefetch=1, grid=(per,),
            in_specs=[pl.BlockSpec((None, tr, cs), lambda i, p: (p[0], i, 0)),
                      pl.BlockSpec((rc.shape[0], tr, cs), lambda i, p: (0, i, 0))],
            out_specs=pl.BlockSpec((tr, cs), lambda i, p: (p[1] * per + i, 0))),
        out_shape=jax.ShapeDtypeStruct((2 * rh, cs), F32),
        compiler_params=_params(1),
    )(place, parts, rc)


def _adamw_update(w, g, m, v):
    nm = ADAM_B1 * m + (1.0 - ADAM_B1) * g
    nv = ADAM_B2 * v + (1.0 - ADAM_B2) * (g * g)
    m_hat = nm / ADAM_C1
    v_hat = nv / ADAM_C2
    return -ADAM_LR * (m_hat / (jnp.sqrt(v_hat) + ADAM_EPS) + ADAM_WD * w), nm, nv


def _adamw_own_half(w, m, v, parts, rc, place, name, after=None):
    rows, cols = w.shape
    rh = rows // 2
    tr = _row_tile(rh, 256)
    per = rh // tr

    def body(p_ref, w_ref, m_ref, v_ref, own_ref, r_ref, *rest):
        g_ref, d_ref, nm_ref, nv_ref = rest[-4:]
        g = own_ref[...].astype(F32)
        for j in range(rc.shape[0]):
            g = g + r_ref[j].astype(F32)
        g_ref[...] = g
        d_ref[...], nm_ref[...], nv_ref[...] = _adamw_update(w_ref[...], g, m_ref[...], v_ref[...])

    mine = pl.BlockSpec((tr, cols), lambda i, p: (p[1] * per + i, 0))
    shape = jax.ShapeDtypeStruct((rows, cols), F32)
    return pl.pallas_call(
        body, name=name,
        grid_spec=pltpu.PrefetchScalarGridSpec(
            num_scalar_prefetch=1, grid=(per,),
            in_specs=[mine, mine, mine, pl.BlockSpec((None, tr, cols), lambda i, p: (p[0], i, 0)),
                      pl.BlockSpec((rc.shape[0], tr, cols), lambda i, p: (0, i, 0))] + ([] if after is None else [ANY]),
            out_specs=[mine] * 4),
        out_shape=[shape] * 4,
        compiler_params=_params(1),
    )(place, w, m, v, parts, rc, *([] if after is None else [after]))


def _adamw_other_half(w, m, v, g, delta, new_m, new_v, place, name, after=None):
    rows, cols = w.shape
    rh = rows // 2
    tr = _row_tile(rh, 256)
    per = rh // tr

    def body(p_ref, w_ref, m_ref, v_ref, g_ref, *rest):
        go_ref, d_ref, nm_ref, nv_ref = rest[-4:]
        gv = g_ref[...]
        go_ref[...] = gv
        d_ref[...], nm_ref[...], nv_ref[...] = _adamw_update(w_ref[...], gv, m_ref[...], v_ref[...])

    other = pl.BlockSpec((tr, cols), lambda i, p: ((1 - p[1]) * per + i, 0))
    shape = jax.ShapeDtypeStruct((rows, cols), F32)
    n_after = 0 if after is None else 1
    return pl.pallas_call(
        body, name=name,
        grid_spec=pltpu.PrefetchScalarGridSpec(
            num_scalar_prefetch=1, grid=(per,),
            in_specs=[other] * 4 + [ANY] * (3 + n_after),
            out_specs=[other] * 4),
        out_shape=[shape] * 4,
        input_output_aliases={4: 0, 5: 1, 6: 2, 7: 3},
        compiler_params=_params(1),
    )(place, w, m, v, g, delta, new_m, new_v, *([] if after is None else [after]))


def _cast_to_slot(w, place, dtype, name, after=None):
    rows, cols = w.shape
    tr = _row_tile(rows, 256)

    def body(p_ref, w_ref, *rest):
        o_ref = rest[-1]
        o_ref[...] = w_ref[...].astype(dtype)

    return pl.pallas_call(
        body, name=name,
        grid_spec=pltpu.PrefetchScalarGridSpec(
            num_scalar_prefetch=1, grid=(rows // tr,),
            in_specs=[pl.BlockSpec((tr, cols), lambda i, p: (i, 0))] + ([] if after is None else [ANY]),
            out_specs=pl.BlockSpec((None, tr, cols), lambda i, p: (p[0], i, 0))),
        out_shape=jax.ShapeDtypeStruct((N_CHIPS, rows, cols), dtype),
        compiler_params=_params(1),
    )(place, w, *([] if after is None else [after]))


def _adamw(w, g, m, v, name, after=None):
    rows, cols = w.shape
    tr = _row_tile(rows, 256)

    def body(w_ref, g_ref, m_ref, v_ref, *rest):
        go_ref, d_ref, nm_ref, nv_ref = rest[-4:]
        gv = g_ref[...]
        go_ref[...] = gv
        d_ref[...], nm_ref[...], nv_ref[...] = _adamw_update(w_ref[...], gv, m_ref[...], v_ref[...])

    tile = pl.BlockSpec((tr, cols), lambda i: (i, 0))
    shape = jax.ShapeDtypeStruct((rows, cols), F32)
    return pl.pallas_call(
        body, name=name, grid=(rows // tr,),
        in_specs=[tile] * 4 + ([] if after is None else [ANY]), out_specs=[tile] * 4, out_shape=[shape] * 4,
        compiler_params=_params(1),
    )(w, g, m, v, *([] if after is None else [after]))


def _mesh_pos():
    return lax.axis_index("x"), lax.axis_index("y"), lax.axis_index("c")


def _other_chips(x, y):
    return [(1 - x, y), (x, 1 - y), (1 - x, 1 - y)]


def _half_rows(ref, which):
    rh = ref.shape[-2] // 2
    return ref.at[pl.ds(which * rh, rh), :]


def _remote(src, dst, send_sems, recv_sems, sem, to):
    return pltpu.make_async_remote_copy(src_ref=src, dst_ref=dst, send_sem=send_sems.at[sem], recv_sem=recv_sems.at[sem],
                                        device_id=to, device_id_type=MESH)


HBM = pl.BlockSpec(memory_space=pltpu.HBM)
SEM = pl.BlockSpec(memory_space=pltpu.SEMAPHORE)
DATAFLOW_EFFECT = pltpu.SideEffectType.DATAFLOW_SIDE_EFFECTING


def _in_hbm(arrays):
    return [pltpu.with_memory_space_constraint(a, pltpu.HBM) for a in arrays]


def _hbm_like(arrays):
    return [pltpu.HBM(a.shape, a.dtype) for a in arrays]


GATHER_COPIES_PER_ARRAY = {"direct": 2, "forward": 2, "pass_near": 2, "pass_far": 1}


def _gather_copies(kind, refs, x, y, c):
    me, near_x, near_y, far = 2 * x + y, 2 * (1 - x) + y, 2 * x + (1 - y), 2 * (1 - x) + (1 - y)
    to_x, to_y, sibling = (1 - x, y, c), (x, 1 - y, c), (x, y, 1 - c)
    out = []
    for ref in refs:
        rh = ref.shape[1] // 2
        rq = rh // 2

        def half(chip, ref=ref, rh=rh):
            return ref.at[chip, pl.ds(c * rh, rh), :]

        def quarter(chip, q, ref=ref, rh=rh, rq=rq):
            return ref.at[chip, pl.ds(c * rh + q * rq, rq), :]

        if kind == "direct":
            out += [(half(me), half(me), to_x), (half(me), half(me), to_y)]
        elif kind == "forward":
            out += [(quarter(near_x, 0), quarter(near_x, 0), to_y), (quarter(near_y, 1), quarter(near_y, 1), to_x)]
        elif kind == "pass_near":
            out += [(half(near_x), half(near_x), sibling), (half(near_y), half(near_y), sibling)]
        else:
            assert kind == "pass_far"
            out += [(half(far), half(far), sibling)]
    return out


def _gather_step(name, bufs, waits, starts, after):
    nb, nw, ns = len(bufs), len(waits), len(starts)
    after = [] if after is None else list(after) if isinstance(after, (list, tuple)) else [after]
    n_after = len(after)

    def body(*refs):
        ins = refs[:nb]
        wait_sems = refs[nb:nb + 2 * nw]
        start_sems = refs[nb + 2 * nw + n_after:nb + 2 * nw + n_after + 2 * ns]
        token = refs[-1]
        x, y, c = _mesh_pos()
        for j, (kind, idxs, _, _) in enumerate(waits):
            for i, (s_ref, d_ref, to) in enumerate(_gather_copies(kind, [ins[t] for t in idxs], x, y, c)):
                came = _remote(s_ref, d_ref, wait_sems[2 * j], wait_sems[2 * j + 1], i, to)
                came.wait_recv()
                came.wait_send()
        for j, (kind, idxs) in enumerate(starts):
            for i, (s_ref, d_ref, to) in enumerate(_gather_copies(kind, [ins[t] for t in idxs], x, y, c)):
                _remote(s_ref, d_ref, start_sems[2 * j], start_sems[2 * j + 1], i, to).start()
        token[...] = jnp.zeros_like(token)

    sems = []
    for kind, idxs in starts:
        sems += [pltpu.SemaphoreType.DMA((GATHER_COPIES_PER_ARRAY[kind] * len(idxs),))] * 2
    operands = _in_hbm(bufs) + [sem for w in waits for sem in w[2:]] + after
    outs = pl.pallas_call(
        body, name=name,
        in_specs=[HBM] * nb + [SEM] * (2 * nw) + [ANY] * n_after,
        out_specs=[SEM] * (2 * ns) + [HBM] * nb + [pl.BlockSpec(memory_space=pltpu.VMEM)],
        out_shape=sems + _hbm_like(bufs) + [jax.ShapeDtypeStruct((8, 128), F32)],
        input_output_aliases={i: 2 * ns + i for i in range(nb)},
        compiler_params=pltpu.CompilerParams(has_side_effects=DATAFLOW_EFFECT),
    )(*operands)
    return outs[2 * ns:2 * ns + nb], [(outs[2 * j], outs[2 * j + 1]) for j in range(ns)], outs[-1]


class _Gather:
    def __init__(self, groups):
        self.groups = groups
        self.bufs = {}
        self.in_flight = {}

    def put(self, slotted):
        self.bufs.update(slotted)

    def step(self, name, waits, starts, after=None):
        names = []
        for _, group in list(waits) + list(starts):
            names += [n for n in self.groups[group] if n not in names]
        index = {n: i for i, n in enumerate(names)}

        def members(group):
            return [index[n] for n in self.groups[group]]

        wait_args = [(kind, members(group)) + self.in_flight.pop((kind, group)) for kind, group in waits]
        start_args = [(kind, members(group)) for kind, group in starts]
        bufs, sems, token = _gather_step(name, [self.bufs[n] for n in names], wait_args, start_args, after)
        self.bufs.update(zip(names, bufs))
        for (kind, group), pair in zip(starts, sems):
            self.in_flight[(kind, group)] = pair
        return token

    def arrays(self, group):
        return {n: self.bufs[n] for n in self.groups[group]}


def _sibling_halves_copies(srcs, dsts, x, y, c):
    out = []
    for s_ref, d_ref in zip(srcs, dsts, strict=True):
        rh = s_ref.shape[1] // 2
        out.append((s_ref.at[:, pl.ds((1 - c) * rh, rh), :], d_ref, (x, y, 1 - c)))
    return out


def _to_sibling_copies(srcs, dsts, x, y, c):
    return [(s_ref, d_ref, (x, y, 1 - c)) for s_ref, d_ref in zip(srcs, dsts, strict=True)]


def _chip_copies(srcs, dsts, x, y, c):
    out = []
    for s_ref, d_ref in zip(srcs, dsts, strict=True):
        for k, (px, py) in enumerate(_other_chips(x, y)):
            out.append((s_ref.at[2 * px + py], d_ref.at[k], (px, py, c)))
    return out


def _join_copies(srcs, dsts, x, y, c):
    out = []
    for s_ref in srcs:
        mine = _half_rows(s_ref, c)
        out.append((mine, mine, (x, y, 1 - c)))
    return out


def _exchange_start(copies_fn, n_copies, srcs, fresh, after, name):
    ns, nb = len(srcs), len(srcs) + len(fresh)

    def body(*refs):
        bufs, send, recv, token = refs[:nb], refs[nb + 1], refs[nb + 2], refs[-1]
        x, y, c = _mesh_pos()
        for i, (s_ref, d_ref, to) in enumerate(copies_fn(bufs[:ns], bufs[ns:] if fresh else bufs[:ns], x, y, c)):
            _remote(s_ref, d_ref, send, recv, i, to).start()
        token[...] = jnp.zeros_like(token)

    sems = [pltpu.SemaphoreType.DMA((n_copies,))] * 2
    outs = pl.pallas_call(
        body, name=name,
        in_specs=[HBM] * nb + [ANY], out_specs=[SEM, SEM] + [HBM] * nb + [pl.BlockSpec(memory_space=pltpu.VMEM)],
        out_shape=sems + _hbm_like(list(srcs) + list(fresh)) + [jax.ShapeDtypeStruct((8, 128), F32)],
        input_output_aliases={i: 2 + i for i in range(nb)},
        compiler_params=pltpu.CompilerParams(has_side_effects=DATAFLOW_EFFECT),
    )(*_in_hbm(list(srcs) + list(fresh)), after)
    return outs[0], outs[1], outs[2:2 + ns], outs[2 + ns:2 + nb], outs[-1]


def _exchange_done(copies_fn, srcs, fresh, send, recv, after, name):
    ns, nb = len(srcs), len(srcs) + len(fresh)

    def body(*refs):
        bufs, send_in, recv_in = refs[:nb], refs[nb], refs[nb + 1]
        x, y, c = _mesh_pos()
        for i, (s_ref, d_ref, to) in enumerate(copies_fn(bufs[:ns], bufs[ns:] if fresh else bufs[:ns], x, y, c)):
            came = _remote(s_ref, d_ref, send_in, recv_in, i, to)
            came.wait_send()
            came.wait_recv()

    outs = pl.pallas_call(
        body, name=name,
        in_specs=[HBM] * nb + [SEM, SEM, ANY], out_specs=[HBM] * nb,
        out_shape=_hbm_like(list(srcs) + list(fresh)),
        input_output_aliases={i: i for i in range(nb)},
        compiler_params=pltpu.CompilerParams(has_side_effects=DATAFLOW_EFFECT),
    )(*_in_hbm(list(srcs) + list(fresh)), send, recv, after)
    return outs[:ns], outs[ns:]


class _Reduce:
    def __init__(self, place, core, shards, mom_m, mom_v):
        self.place, self.core = place, core
        self.shards, self.mom_m, self.mom_v = shards, mom_m, mom_v
        self.state = {}
        self.results = {}

    def add(self, group, grads, after):
        names = list(grads)
        g4s = [g.reshape((N_CHIPS, -1, g.shape[-1])) if g.ndim == 2 else g for g in grads.values()]
        fresh = [lax.empty((N_CHIPS, g.shape[1] // 2, g.shape[2]), BF16) for g in g4s]
        send, recv, g4s, fresh, token = _exchange_start(_sibling_halves_copies, len(names), g4s, fresh, after,
                                                        "pair_start_" + group)
        self.state[group] = (0, names, send, recv, g4s, fresh)
        return token

    def send(self, group, theirs, after):
        names, srcs = list(theirs), list(theirs.values())
        fresh = [lax.empty(s.shape, BF16) for s in srcs]
        send, recv, srcs, fresh, token = _exchange_start(_to_sibling_copies, len(names), srcs, fresh, after, "pair_start_" + group)
        self.state[group] = ("sent", names, send, recv, srcs, fresh)
        return token

    def received(self, group, after):
        stage, names, send, recv, srcs, fresh = self.state.pop(group)
        assert stage == "sent"
        _, got = _exchange_done(_to_sibling_copies, srcs, fresh, send, recv, after, "pair_done_" + group)
        return dict(zip(names, got))

    def add_parts(self, group, parts):
        names, srcs = list(parts), list(parts.values())
        fresh = [lax.empty((N_CHIPS - 1,) + p.shape[1:], BF16) for p in srcs]
        send, recv, srcs, fresh, token = _exchange_start(_chip_copies, 3 * len(names), srcs, fresh, self.core, "chips_start_" + group)
        self.state[group] = (1, names, send, recv, srcs, fresh)
        return token

    def step(self, group, after):
        stage, names, send, recv, srcs, fresh = self.state[group]
        if stage == 0:
            g4s, ras = _exchange_done(_sibling_halves_copies, srcs, fresh, send, recv, after, "pair_done_" + group)
            parts = [_pair_sum(g, r, self.core, "pair_sum_" + n) for g, r, n in zip(g4s, ras, names)]
            fresh = [lax.empty((N_CHIPS - 1,) + p.shape[1:], BF16) for p in parts]
            send, recv, parts, fresh, token = _exchange_start(_chip_copies, 3 * len(names), parts, fresh, self.core,
                                                              "chips_start_" + group)
            self.state[group] = (1, names, send, recv, parts, fresh)
            return token
        if stage == 1:
            parts, rcs = _exchange_done(_chip_copies, srcs, fresh, send, recv, after, "chips_done_" + group)
            token = None
            for n, p, r in zip(names, parts, rcs):
                self.results[n] = _adamw_own_half(self.shards[n], self.mom_m[n], self.mom_v[n], p, r, self.place,
                                                  "adamw_own_" + n, after=token)
                token = self.results[n][1]
            wholes = [self.results[n][0] for n in names]
            send, recv, wholes, _, token = _exchange_start(_join_copies, len(names), wholes, [], token, "join_start_" + group)
            self.state[group] = (2, names, send, recv, wholes, [])
            return token
        assert stage == 2
        wholes, _ = _exchange_done(_join_copies, srcs, [], send, recv, after, "join_done_" + group)
        token = None
        for n, g in zip(names, wholes):
            _, d, nm, nv = self.results[n]
            self.results[n] = _adamw_other_half(self.shards[n], self.mom_m[n], self.mom_v[n], g, d, nm, nv, self.place,
                                                "adamw_other_" + n, after=token)
            token = self.results[n][1]
        del self.state[group]
        return token


N_DEV = 8


def _all_reduce_small(v):
    def body(v_ref, o_ref, slots, send_sems, recv_sems):
        x, y, c = _mesh_pos()
        me = 4 * x + 2 * y + c
        slots[me] = v_ref[...]
        peers = []
        for r in range(1, N_DEV):
            fx, fy, fc = (r >> 2) & 1, (r >> 1) & 1, r & 1
            peers.append((x + fx - 2 * x * fx, y + fy - 2 * y * fy, c + fc - 2 * c * fc))
        sends = []
        for r, peer in enumerate(peers):
            cp = _remote(v_ref, slots.at[me], send_sems, recv_sems, r, peer)
            cp.start()
            sends.append(cp)
        for r, (px, py, pc) in enumerate(peers):
            landed = slots.at[4 * px + 2 * py + pc]
            _remote(landed, landed, send_sems, recv_sems, r, (px, py, pc)).wait_recv()
        for cp in sends:
            cp.wait_send()
        acc = slots[0]
        for i in range(1, N_DEV):
            acc = acc + slots[i]
        o_ref[...] = acc

    vm = pl.BlockSpec(memory_space=pltpu.VMEM)
    return pl.pallas_call(
        body, name="small_grads_all_reduce",
        in_specs=[vm], out_specs=vm,
        out_shape=jax.ShapeDtypeStruct(v.shape, v.dtype),
        scratch_shapes=[pltpu.VMEM((N_DEV,) + v.shape, v.dtype), pltpu.SemaphoreType.DMA((N_DEV - 1,)),
                        pltpu.SemaphoreType.DMA((N_DEV - 1,))],
    )(v)


MATRICES = ("w_in", "w_attn_out", "w_conv_out", "w_o", "w_cq", "w_ckv", "w_co", "w_gate", "w_up", "w_down")
VECTORS = ("g_mix", "b_gate", "g_cross", "g_mem", "g_ffn", "g_final", "conv_w", "sink")
WEIGHT_ORDER = ("g_mix", "w_in", "sink", "conv_w", "b_gate", "w_attn_out", "w_conv_out", "w_o", "g_cross", "g_mem", "w_cq",
                "w_ckv", "w_co", "g_ffn", "w_gate", "w_up", "w_down", "g_final")
CONV_PAD_ROWS = 32
SMALL_ROWS = 8


def _pack(pieces):
    flat = jnp.concatenate([p.reshape(-1) for p in pieces])
    lane_group = SMALL_ROWS * 128
    total = -(-flat.shape[0] // lane_group) * lane_group
    flat = jnp.pad(flat, (0, total - flat.shape[0]))
    return flat.reshape(SMALL_ROWS, total // SMALL_ROWS), [p.size for p in pieces]


def _unpack(packed, pieces):
    flat = packed.reshape(-1)
    out, off = [], 0
    for p in pieces:
        out.append(flat[off:off + p.size].reshape(p.shape))
        off += p.size
    return out


def kernel(x, mem, g_mix, w_in, sink, conv_w, b_gate, w_attn_out, w_conv_out, w_o, g_cross, g_mem, w_cq, w_ckv, w_co, g_ffn, w_gate, w_up, w_down, g_final, loss_target, m_g_mix, m_w_in, m_sink, m_conv_w, m_b_gate, m_w_attn_out, m_w_conv_out, m_w_o, m_g_cross, m_g_mem, m_w_cq, m_w_ckv, m_w_co, m_g_ffn, m_w_gate, m_w_up, m_w_down, m_g_final, v_g_mix, v_w_in, v_sink, v_conv_w, v_b_gate, v_w_attn_out, v_w_conv_out, v_w_o, v_g_cross, v_g_mem, v_w_cq, v_w_ckv, v_w_co, v_g_ffn, v_w_gate, v_w_up, v_w_down, v_g_final):
    given = dict(g_mix=g_mix, w_in=w_in, sink=sink, conv_w=conv_w, b_gate=b_gate, w_attn_out=w_attn_out, w_conv_out=w_conv_out,
                 w_o=w_o, g_cross=g_cross, g_mem=g_mem, w_cq=w_cq, w_ckv=w_ckv, w_co=w_co, g_ffn=g_ffn, w_gate=w_gate, w_up=w_up,
                 w_down=w_down, g_final=g_final)
    mom_m = dict(g_mix=m_g_mix, w_in=m_w_in, sink=m_sink, conv_w=m_conv_w, b_gate=m_b_gate, w_attn_out=m_w_attn_out,
                 w_conv_out=m_w_conv_out, w_o=m_w_o, g_cross=m_g_cross, g_mem=m_g_mem, w_cq=m_w_cq, w_ckv=m_w_ckv, w_co=m_w_co,
                 g_ffn=m_g_ffn, w_gate=m_w_gate, w_up=m_w_up, w_down=m_w_down, g_final=m_g_final)
    mom_v = dict(g_mix=v_g_mix, w_in=v_w_in, sink=v_sink, conv_w=v_conv_w, b_gate=v_b_gate, w_attn_out=v_w_attn_out,
                 w_conv_out=v_w_conv_out, w_o=v_w_o, g_cross=v_g_cross, g_mem=v_g_mem, w_cq=v_w_cq, w_ckv=v_w_ckv, w_co=v_w_co,
                 g_ffn=v_g_ffn, w_gate=v_w_gate, w_up=v_w_up, w_down=v_w_down, g_final=v_g_final)
    xs, mems, target = x[0], mem[0], loss_target[0]
    d_model = xs.shape[1]
    chip = 2 * lax.axis_index("x") + lax.axis_index("y")
    core = jnp.reshape(lax.axis_index("c"), (1,)).astype(jnp.int32)
    place = jnp.stack([chip, lax.axis_index("c")]).astype(jnp.int32)

    shards = {n: given[n][0] for n in MATRICES}
    conv_cols = conv_w.shape[2]
    conv_pad = jnp.pad(conv_w[0], ((0, CONV_PAD_ROWS - conv_w.shape[1]), (0, 0)))
    fetch = _Gather(GATHER_GROUPS)
    first = {"w_in": _cast_to_slot(shards["w_in"], place, BF16, "to_slot_w_in"),
             "conv_w": _cast_to_slot(conv_pad, place, F32, "to_slot_conv_w")}
    fetch.put(first)
    tok = fetch.step("gather_start", [], [("direct", "in")])
    fetch.put({n: _cast_to_slot(shards[n], place, BF16, "to_slot_" + n, after=tok) for n in MATRICES if n != "w_in"})
    small = {n: given[n] for n in ("g_mix", "b_gate", "g_cross", "g_mem", "g_ffn")}
    small["g_final"] = g_final[None]
    small["sink"] = sink[0]

    reduce = _Reduce(place, core, shards, {n: mom_m[n][0] for n in MATRICES}, {n: mom_v[n][0] for n in MATRICES})
    sq, grad_x, small_grads = _local_step(xs, mems, target, small, fetch, reduce)

    loss_part = 0.5 * sq[0:1, 0:1] / d_model
    pieces = [small_grads[n] for n in VECTORS] + [loss_part]
    packed, _ = _pack(pieces)
    summed = _unpack(_all_reduce_small(packed), pieces)
    loss = summed[-1][0, 0]
    small_sum = dict(zip(VECTORS, summed[:-1]))
    small_sum["conv_w"] = lax.dynamic_slice_in_dim(small_sum["conv_w"], chip * conv_cols, conv_cols, axis=1)

    grad_out, delta, new_m, new_v = {}, {}, {}, {}
    like = [given[n] for n in VECTORS]
    pw, _ = _pack(like)
    pg, _ = _pack([small_sum[n] for n in VECTORS])
    pm, _ = _pack([mom_m[n] for n in VECTORS])
    pv, _ = _pack([mom_v[n] for n in VECTORS])
    _, pd, pnm, pnv = _adamw(pw, pg, pm, pv, "adamw_small")
    for n, g, d, nm, nv in zip(VECTORS, [small_sum[n] for n in VECTORS], _unpack(pd, like), _unpack(pnm, like), _unpack(pnv, like)):
        grad_out[n] = g.reshape(given[n].shape)
        delta[n], new_m[n], new_v[n] = d, nm, nv
    tok = reduce.step("in", pd)
    reduce.step("in", tok)
    for n in MATRICES:
        g, d, nm, nv = reduce.results[n]
        grad_out[n], delta[n], new_m[n], new_v[n] = g[None], d[None], nm[None], nv[None]

    return (loss, grad_x[None], *[grad_out[n] for n in WEIGHT_ORDER], *[delta[n] for n in WEIGHT_ORDER],
            *[new_m[n] for n in WEIGHT_ORDER], *[new_v[n] for n in WEIGHT_ORDER])
```

```python
import functools

import jax
import jax.numpy as jnp
from jax import lax
from jax.experimental import pallas as pl
from jax.experimental.pallas import tpu as pltpu

F32 = jnp.float32
BF16 = jnp.bfloat16
MESH = pl.DeviceIdType.MESH
ANY = pl.BlockSpec(memory_space=pl.ANY)

VMEM_LIMIT_BYTES = 56 * 1024 * 1024

N_CHIPS = 4
HEAD_DIM = 128
N_Q_HEADS = 8
N_KV_HEADS = 2
Q_GROUP = N_Q_HEADS // N_KV_HEADS
ATTN_WIDTH = N_Q_HEADS * HEAD_DIM
KV_WIDTH = N_KV_HEADS * HEAD_DIM
WINDOW = 128
BLOCK = 128
BAND = 3 * BLOCK
ROPE_THETA = 10000.0
CONV_WIDTH = 1024
MEM_HEADS = 4
MEM_WIDTH = MEM_HEADS * HEAD_DIM
RMS_EPS = 1e-6
NEG_INF = -1e30
ATTN_SCALE = HEAD_DIM ** -0.5

Q_OFF, K_OFF, V_OFF, CU_OFF, CB_OFF, CC_OFF, GL_OFF = 0, 1024, 1280, 1536, 2560, 3584, 4608

ADAM_LR = 0.001
ADAM_B1 = 0.9
ADAM_B2 = 0.999
ADAM_EPS = 1e-08
ADAM_WD = 0.01
ADAM_STEP = 10
ADAM_C1 = 1.0 - ADAM_B1 ** ADAM_STEP
ADAM_C2 = 1.0 - ADAM_B2 ** ADAM_STEP


def _params(n_grid_axes):
    return pltpu.CompilerParams(dimension_semantics=("arbitrary",) * n_grid_axes, vmem_limit_bytes=VMEM_LIMIT_BYTES)


BF16_SUBLANES = 16


def _row_tile(rows, want):
    if rows <= want:
        return rows
    for t in range(want, 0, -BF16_SUBLANES):
        if rows % t == 0:
            return t
    return rows


def _matmul(a, b, *, mode, tm, tn, tk, out_dtypes, name, extras=(), epilogue=None, b_blocks=1, out_blocks=1, after=None):
    if mode == "tn":
        kdim, m = a.shape
    else:
        m, kdim = a.shape
    if b_blocks > 1:
        nb, brows, bcols = b.shape
        assert nb == b_blocks
        if mode == "nn":
            n = bcols * nb
            assert brows == kdim
        else:
            assert mode == "nt" and bcols * nb == kdim
            n = brows
    else:
        n = b.shape[0] if mode == "nt" else b.shape[1]
    tm, tn = min(tm, m), min(tn, n)
    assert m % tm == 0 and n % tn == 0 and tk == kdim, (name, m, n, kdim, tm, tn, tk)
    n_extra, n_out = len(extras), len(out_dtypes)
    n_after = 0 if after is None else 1

    if mode == "tn":
        a_spec = pl.BlockSpec((tk, tm), lambda j, i, k: (k, i))
        dims = (((0,), (0,)), ((), ()))
    else:
        a_spec = pl.BlockSpec((tm, tk), lambda j, i, k: (i, k))
        dims = (((1,), (0,)), ((), ())) if mode == "nn" else (((1,), (1,)), ((), ()))

    if b_blocks > 1 and mode == "nn":
        per = b.shape[2] // tn
        assert b.shape[2] % tn == 0
        b_spec = pl.BlockSpec((None, tk, tn), lambda j, i, k: (j // per, k, j % per))
    elif b_blocks > 1:
        b_spec = pl.BlockSpec((b_blocks, tn, b.shape[2]), lambda j, i, k: (0, j, 0))
    elif mode == "nt":
        b_spec = pl.BlockSpec((tn, tk), lambda j, i, k: (j, k))
    else:
        b_spec = pl.BlockSpec((tk, tn), lambda j, i, k: (k, j))

    tile_spec = pl.BlockSpec((tm, tn), lambda j, i, k: (i, j))
    if out_blocks > 1:
        ncols = n // out_blocks
        assert ncols % tn == 0
        oper = ncols // tn
        out_spec = pl.BlockSpec((None, tm, tn), lambda j, i, k: (j // oper, i, j % oper))
        out_shape = [jax.ShapeDtypeStruct((out_blocks, m, ncols), dt) for dt in out_dtypes]
    else:
        out_spec = tile_spec
        out_shape = [jax.ShapeDtypeStruct((m, n), dt) for dt in out_dtypes]

    def body(a_ref, b_ref, *rest):
        extra_refs = rest[:n_extra]
        out_refs = rest[n_extra + n_after:n_extra + n_after + n_out]
        if mode == "nt" and b_blocks > 1:
            cs = b.shape[2]
            acc = None
            for jb in range(b_blocks):
                prod = lax.dot_general(a_ref[:, jb * cs:(jb + 1) * cs].astype(BF16), b_ref[jb].astype(BF16), dims,
                                       preferred_element_type=F32)
                acc = prod if acc is None else acc + prod
        else:
            acc = lax.dot_general(a_ref[...].astype(BF16), b_ref[...].astype(BF16), dims, preferred_element_type=F32)
        tiles = (acc,) if epilogue is None else epilogue(acc, *[r[...] for r in extra_refs])
        for o_ref, t in zip(out_refs, tiles, strict=True):
            o_ref[...] = t.astype(o_ref.dtype)

    outs = pl.pallas_call(
        body,
        name=name,
        grid=(n // tn, m // tm, 1),
        in_specs=[a_spec, b_spec] + [tile_spec] * n_extra + [ANY] * n_after,
        out_specs=[out_spec] * n_out,
        out_shape=out_shape,
        compiler_params=_params(3),
    )(a, b, *extras, *([] if after is None else [after]))
    return outs[0] if n_out == 1 else outs


def _add_residual(acc, res):
    return (acc + res,)


def _matmul_column_blocks(a, b4, blocks, out, *, tm, name, after=None):
    m, kdim = a.shape
    nb, _, cols = b4.shape
    tm = min(tm, m)
    assert m % tm == 0

    def body(j_ref, a_ref, b_ref, *rest):
        rest[-1][...] = jnp.dot(a_ref[...], b_ref[...], preferred_element_type=F32)

    extra = ([] if out is None else [out]) + ([] if after is None else [after])
    n_blocks = blocks.shape[0]
    return pl.pallas_call(
        body, name=name,
        grid_spec=pltpu.PrefetchScalarGridSpec(
            num_scalar_prefetch=1, grid=(n_blocks, m // tm),
            in_specs=[pl.BlockSpec((tm, kdim), lambda j, i, blk: (i, 0)),
                      pl.BlockSpec((None, kdim, cols), lambda j, i, blk: (blk[j], 0, 0))] + [ANY] * len(extra),
            out_specs=pl.BlockSpec((tm, cols), lambda j, i, blk: (i, blk[j]))),
        out_shape=jax.ShapeDtypeStruct((m, nb * cols), F32),
        input_output_aliases={} if out is None else {3: 0},
        compiler_params=_params(2),
    )(blocks, a, b4, *extra)


def _wgrad_half(a, b, core, *, theirs, row_sharded, tm, tn, name, add=None, after=None):
    kdim, m = a.shape
    n = b.shape[1]
    rs, cs = (m // N_CHIPS, n) if row_sharded else (m, n // N_CHIPS)
    rh = rs // 2
    tm, tn = min(tm, rh), min(tn, cs)
    assert rh % tm == 0 and cs % tn == 0, (name, rh, cs, tm, tn)
    mh, per = rh // tm, cs // tn
    has_add = add is not None

    def half(c):
        return 1 - c[0] if theirs else c[0]

    if row_sharded:
        grid = (n // tn, N_CHIPS * mh)
        a_spec = pl.BlockSpec((kdim, tm), lambda j, r, c: (0, ((r // mh) * 2 + half(c)) * mh + r % mh))
        o_spec = pl.BlockSpec((None, tm, tn), lambda j, r, c: (r // mh, r % mh, j))
    else:
        grid = (n // tn, mh)
        a_spec = pl.BlockSpec((kdim, tm), lambda j, r, c: (0, half(c) * mh + r))
        o_spec = pl.BlockSpec((None, tm, tn), lambda j, r, c: (j // per, r, j % per))
    b_spec = pl.BlockSpec((kdim, tn), lambda j, r, c: (0, j))

    def body(c_ref, a_ref, b_ref, *rest):
        o_ref = rest[-1]
        acc = lax.dot_general(a_ref[...].astype(BF16), b_ref[...].astype(BF16), (((0,), (0,)), ((), ())),
                              preferred_element_type=F32)
        if has_add:
            acc = acc + rest[0][...].astype(F32)
        o_ref[...] = acc.astype(BF16)

    operands = [a, b] + ([add] if has_add else []) + ([] if after is None else [after])
    return pl.pallas_call(
        body, name=name,
        grid_spec=pltpu.PrefetchScalarGridSpec(
            num_scalar_prefetch=1, grid=grid,
            in_specs=[a_spec, b_spec] + ([o_spec] if has_add else []) + ([] if after is None else [ANY]),
            out_specs=o_spec),
        out_shape=jax.ShapeDtypeStruct((N_CHIPS, rh, cs), BF16),
        compiler_params=_params(2),
    )(core, *operands)


def _rstd(x):
    return lax.rsqrt(jnp.mean(x * x, axis=-1, keepdims=True) + RMS_EPS)


def _rmsnorm(x, g, name):
    s, d = x.shape
    tr = _row_tile(s, 256)

    def body(x_ref, g_ref, o_ref):
        xv = x_ref[...]
        o_ref[...] = (xv * _rstd(xv) * g_ref[...]).astype(BF16)

    return pl.pallas_call(
        body, name=name, grid=(s // tr,),
        in_specs=[pl.BlockSpec((tr, d), lambda i: (i, 0)), pl.BlockSpec((1, d), lambda i: (0, 0))],
        out_specs=pl.BlockSpec((tr, d), lambda i: (i, 0)),
        out_shape=jax.ShapeDtypeStruct((s, d), BF16),
        compiler_params=_params(1),
    )(x, g)


def _rmsnorm_bwd(dh, x, g, dres, name):
    s, d = x.shape
    tr = _row_tile(s, 256)
    has_res = dres is not None

    def body(*refs):
        if has_res:
            dh_ref, x_ref, g_ref, res_ref, dx_ref, dxb_ref, dg_ref = refs
        else:
            dh_ref, x_ref, g_ref, dx_ref, dxb_ref, dg_ref = refs
        xv = x_ref[...]
        dhv = dh_ref[...].astype(F32)
        r = _rstd(xv)
        xn = xv * r
        dhg = dhv * g_ref[...]
        dx = r * (dhg - xn * jnp.mean(dhg * xn, axis=-1, keepdims=True))
        if has_res:
            dx = dx + res_ref[...]
        dx_ref[...] = dx
        dxb_ref[...] = dx.astype(BF16)
        part = jnp.sum(dhv * xn, axis=0, keepdims=True)

        @pl.when(pl.program_id(0) == 0)
        def _():
            dg_ref[...] = part

        @pl.when(pl.program_id(0) > 0)
        def _():
            dg_ref[...] += part

    row = pl.BlockSpec((tr, d), lambda i: (i, 0))
    vec = pl.BlockSpec((1, d), lambda i: (0, 0))
    return pl.pallas_call(
        body, name=name, grid=(s // tr,),
        in_specs=[row, row, vec] + ([row] if has_res else []),
        out_specs=[row, row, vec],
        out_shape=[jax.ShapeDtypeStruct((s, d), F32), jax.ShapeDtypeStruct((s, d), BF16), jax.ShapeDtypeStruct((1, d), F32)],
        compiler_params=_params(1),
    )(*([dh, x, g] + ([dres] if has_res else [])))


def _loss_head(x3, g, target):
    s, d = x3.shape
    tr = _row_tile(s, 256)

    def body(x_ref, g_ref, t_ref, dx_ref, dxb_ref, sq_ref, dg_ref):
        xv = x_ref[...]
        gv = g_ref[...]
        r = _rstd(xv)
        xn = xv * r
        err = xn * gv - t_ref[...]
        dy = err * (1.0 / d)
        dyg = dy * gv
        dx = r * (dyg - xn * jnp.mean(dyg * xn, axis=-1, keepdims=True))
        dx_ref[...] = dx
        dxb_ref[...] = dx.astype(BF16)
        sq = jnp.sum(jnp.sum(err * err, axis=1, keepdims=True), axis=0, keepdims=True)
        sq = jnp.broadcast_to(sq, (1, 128))
        part = jnp.sum(dy * xn, axis=0, keepdims=True)

        @pl.when(pl.program_id(0) == 0)
        def _():
            sq_ref[...] = sq
            dg_ref[...] = part

        @pl.when(pl.program_id(0) > 0)
        def _():
            sq_ref[...] += sq
            dg_ref[...] += part

    row = pl.BlockSpec((tr, d), lambda i: (i, 0))
    vec = pl.BlockSpec((1, d), lambda i: (0, 0))
    return pl.pallas_call(
        body, name="loss_head", grid=(s // tr,),
        in_specs=[row, vec, row],
        out_specs=[row, row, pl.BlockSpec((1, 128), lambda i: (0, 0)), vec],
        out_shape=[jax.ShapeDtypeStruct((s, d), F32), jax.ShapeDtypeStruct((s, d), BF16),
                   jax.ShapeDtypeStruct((1, 128), F32), jax.ShapeDtypeStruct((1, d), F32)],
        compiler_params=_params(1),
    )(x3, g, target)


def _rope_tables(s):
    inv = 1.0 / (ROPE_THETA ** (jnp.arange(0, HEAD_DIM, 2, dtype=F32) / HEAD_DIM))
    ang = jnp.arange(s, dtype=F32)[:, None] * inv[None, :]
    cos, sin = jnp.cos(ang), jnp.sin(ang)
    return jnp.concatenate([cos, cos], axis=1), jnp.concatenate([-sin, sin], axis=1)


def _swap_halves(t):
    return pltpu.roll(t, HEAD_DIM // 2, 1)


def _rope_fwd(z, cos_t, sin_t):
    s = z.shape[0]
    tr = _row_tile(s, 256)

    def body(zq_ref, zk_ref, zv_ref, c_ref, s_ref, q_ref, k_ref, v_ref):
        c, sn = c_ref[...], s_ref[...]
        for hd in range(N_Q_HEADS):
            cols = slice(hd * HEAD_DIM, (hd + 1) * HEAD_DIM)
            t = zq_ref[:, cols]
            q_ref[:, cols] = (t * c + _swap_halves(t) * sn).astype(BF16)
        for hd in range(N_KV_HEADS):
            cols = slice(hd * HEAD_DIM, (hd + 1) * HEAD_DIM)
            t = zk_ref[:, cols]
            k_ref[:, cols] = (t * c + _swap_halves(t) * sn).astype(BF16)
        v_ref[...] = zv_ref[...].astype(BF16)

    tab = pl.BlockSpec((tr, HEAD_DIM), lambda i: (i, 0))
    return pl.pallas_call(
        body, name="rope_fwd", grid=(s // tr,),
        in_specs=[pl.BlockSpec((tr, ATTN_WIDTH), lambda i: (i, Q_OFF // ATTN_WIDTH)),
                  pl.BlockSpec((tr, KV_WIDTH), lambda i: (i, K_OFF // KV_WIDTH)),
                  pl.BlockSpec((tr, KV_WIDTH), lambda i: (i, V_OFF // KV_WIDTH)), tab, tab],
        out_specs=[pl.BlockSpec((tr, ATTN_WIDTH), lambda i: (i, 0)), pl.BlockSpec((tr, KV_WIDTH), lambda i: (i, 0)),
                   pl.BlockSpec((tr, KV_WIDTH), lambda i: (i, 0))],
        out_shape=[jax.ShapeDtypeStruct((s, ATTN_WIDTH), BF16), jax.ShapeDtypeStruct((s, KV_WIDTH), BF16),
                   jax.ShapeDtypeStruct((s, KV_WIDTH), BF16)],
        compiler_params=_params(1),
    )(z, z, z, cos_t, sin_t)


def _rope_bwd(dq_rot, dk_rot, dv, cos_t, sin_t):
    s = dq_rot.shape[0]
    tr = _row_tile(s, 256)

    def body(dq_ref, dk_ref, dv_ref, c_ref, s_ref, oq_ref, ok_ref, ov_ref):
        c, sn = c_ref[...], s_ref[...]
        for hd in range(N_Q_HEADS):
            cols = slice(hd * HEAD_DIM, (hd + 1) * HEAD_DIM)
            t = dq_ref[:, cols]
            oq_ref[:, cols] = (t * c + _swap_halves(t * sn)).astype(BF16)
        for hd in range(N_KV_HEADS):
            cols = slice(hd * HEAD_DIM, (hd + 1) * HEAD_DIM)
            t = dk_ref[:, cols]
            ok_ref[:, cols] = (t * c + _swap_halves(t * sn)).astype(BF16)
        ov_ref[...] = dv_ref[...].astype(BF16)

    tab = pl.BlockSpec((tr, HEAD_DIM), lambda i: (i, 0))
    wide = pl.BlockSpec((tr, ATTN_WIDTH), lambda i: (i, 0))
    narrow = pl.BlockSpec((tr, KV_WIDTH), lambda i: (i, 0))
    return pl.pallas_call(
        body, name="rope_bwd", grid=(s // tr,),
        in_specs=[wide, narrow, narrow, tab, tab],
        out_specs=[wide, narrow, narrow],
        out_shape=[jax.ShapeDtypeStruct((s, ATTN_WIDTH), BF16), jax.ShapeDtypeStruct((s, KV_WIDTH), BF16),
                   jax.ShapeDtypeStruct((s, KV_WIDTH), BF16)],
        compiler_params=_params(1),
    )(dq_rot, dk_rot, dv, cos_t, sin_t)


def _swa_band(i, s):
    return pl.multiple_of(jnp.clip((i - 1) * BLOCK, 0, s - BAND), BLOCK)


def _swa_probs(q_ref, k_ref, sink_ref, kv, start, valid):
    cols = slice(kv * HEAD_DIM, (kv + 1) * HEAD_DIM)
    kb = k_ref[pl.ds(start, BAND), cols]
    heads = [kv * Q_GROUP + g for g in range(Q_GROUP)]
    qg = jnp.concatenate([q_ref[:, hd * HEAD_DIM:(hd + 1) * HEAD_DIM] for hd in heads], axis=0)
    sc = lax.dot_general(qg, kb, (((1,), (1,)), ((), ())), preferred_element_type=F32) * ATTN_SCALE
    sc = jnp.where(valid, sc, NEG_INF)
    sk = jnp.concatenate([jnp.full((BLOCK, 1), sink_ref[hd], F32) for hd in heads], axis=0)
    mx = jnp.maximum(jnp.max(sc, axis=1, keepdims=True), sk)
    e = jnp.exp(sc - mx)
    es = jnp.exp(sk - mx)
    inv = 1.0 / (jnp.sum(e, axis=1, keepdims=True) + es)
    return qg, kb, e * inv, es * inv


def _swa_valid(i, start):
    q_pos = i * BLOCK + lax.broadcasted_iota(jnp.int32, (BLOCK, 1), 0)
    q_pos = jnp.concatenate([q_pos] * Q_GROUP, axis=0)
    k_pos = start + lax.broadcasted_iota(jnp.int32, (1, BAND), 1)
    return jnp.abs(k_pos - q_pos) <= WINDOW


def _swa_fwd(q, k, v, sink):
    s = q.shape[0]
    assert s % BLOCK == 0 and s >= BAND

    def body(sink_ref, q_ref, k_ref, v_ref, o_ref):
        i = pl.program_id(0)
        start = _swa_band(i, s)
        valid = _swa_valid(i, start)
        for kv in range(N_KV_HEADS):
            _, _, p, _ = _swa_probs(q_ref, k_ref, sink_ref, kv, start, valid)
            vb = v_ref[pl.ds(start, BAND), kv * HEAD_DIM:(kv + 1) * HEAD_DIM]
            o = jnp.dot(p.astype(BF16), vb, preferred_element_type=F32)
            for g in range(Q_GROUP):
                hd = kv * Q_GROUP + g
                o_ref[:, hd * HEAD_DIM:(hd + 1) * HEAD_DIM] = o[g * BLOCK:(g + 1) * BLOCK].astype(BF16)

    whole = pl.BlockSpec((s, KV_WIDTH), lambda i: (0, 0))
    blk = pl.BlockSpec((BLOCK, ATTN_WIDTH), lambda i: (i, 0))
    return pl.pallas_call(
        body, name="swa_fwd", grid=(s // BLOCK,),
        in_specs=[pl.BlockSpec(memory_space=pltpu.SMEM), blk, whole, whole],
        out_specs=blk,
        out_shape=jax.ShapeDtypeStruct((s, ATTN_WIDTH), BF16),
        compiler_params=_params(1),
    )(sink, q, k, v)


def _swa_bwd(q, k, v, d_out, sink):
    s = q.shape[0]

    def body(sink_ref, q_ref, k_ref, v_ref, do_ref, dq_ref, dk_ref, dv_ref, dsink_ref):
        i = pl.program_id(0)

        @pl.when(i == 0)
        def _():
            dk_ref[...] = jnp.zeros_like(dk_ref)
            dv_ref[...] = jnp.zeros_like(dv_ref)
            dsink_ref[...] = jnp.zeros_like(dsink_ref)

        start = _swa_band(i, s)
        valid = _swa_valid(i, start)
        for kv in range(N_KV_HEADS):
            cols = slice(kv * HEAD_DIM, (kv + 1) * HEAD_DIM)
            qg, kb, p, p_sink = _swa_probs(q_ref, k_ref, sink_ref, kv, start, valid)
            vb = v_ref[pl.ds(start, BAND), cols]
            heads = [kv * Q_GROUP + g for g in range(Q_GROUP)]
            dog = jnp.concatenate([do_ref[:, hd * HEAD_DIM:(hd + 1) * HEAD_DIM] for hd in heads], axis=0)
            dp = lax.dot_general(dog, vb, (((1,), (1,)), ((), ())), preferred_element_type=F32)
            delta = jnp.sum(p * dp, axis=1, keepdims=True)
            ds = (p * (dp - delta) * ATTN_SCALE).astype(BF16)
            dqg = jnp.dot(ds, kb, preferred_element_type=F32)
            dk_ref[pl.ds(start, BAND), cols] += lax.dot_general(ds, qg, (((0,), (0,)), ((), ())), preferred_element_type=F32)
            dv_ref[pl.ds(start, BAND), cols] += lax.dot_general(p.astype(BF16), dog, (((0,), (0,)), ((), ())),
                                                                 preferred_element_type=F32)
            dsk = p_sink * delta
            for g, hd in enumerate(heads):
                dq_ref[:, hd * HEAD_DIM:(hd + 1) * HEAD_DIM] = dqg[g * BLOCK:(g + 1) * BLOCK]
                tot = jnp.sum(dsk[g * BLOCK:(g + 1) * BLOCK], axis=0, keepdims=True)
                dsink_ref[hd:hd + 1, :] -= jnp.broadcast_to(tot, (1, 128))

    whole = pl.BlockSpec((s, KV_WIDTH), lambda i: (0, 0))
    blk = pl.BlockSpec((BLOCK, ATTN_WIDTH), lambda i: (i, 0))
    return pl.pallas_call(
        body, name="swa_bwd", grid=(s // BLOCK,),
        in_specs=[pl.BlockSpec(memory_space=pltpu.SMEM), blk, whole, whole, blk],
        out_specs=[blk, whole, whole, pl.BlockSpec((N_Q_HEADS, 128), lambda i: (0, 0))],
        out_shape=[jax.ShapeDtypeStruct((s, ATTN_WIDTH), F32), jax.ShapeDtypeStruct((s, KV_WIDTH), F32),
                   jax.ShapeDtypeStruct((s, KV_WIDTH), F32), jax.ShapeDtypeStruct((N_Q_HEADS, 128), F32)],
        compiler_params=_params(1),
    )(sink, q, k, v, d_out)


CONV_CHUNK = 256


def _shift_rows(t, rows, down):
    n = t.shape[0]
    rolled = pltpu.roll(t, 1 if down else n - 1, 0)
    edge = 0 if down else n - 1
    return jnp.where(rows == edge, 0.0, rolled)


def _conv_specs(s):
    def z_spec(off):
        return pl.BlockSpec((s, CONV_CHUNK), lambda j, off=off: (0, off // CONV_CHUNK + j))
    chunk = pl.BlockSpec((s, CONV_CHUNK), lambda j: (0, j))
    w_spec = pl.BlockSpec((3, CONV_CHUNK), lambda j: (0, j))
    return z_spec(CU_OFF), z_spec(CB_OFF), z_spec(CC_OFF), chunk, w_spec


def _conv_fwd(z, conv_w):
    s = z.shape[0]
    cu_spec, cb_spec, cc_spec, chunk, w_spec = _conv_specs(s)

    def body(cu_ref, cb_ref, cc_ref, w_ref, o_ref):
        rows = lax.broadcasted_iota(jnp.int32, (s, 1), 0)
        t = cc_ref[...] * cu_ref[...]
        c3 = _shift_rows(t, rows, True) * w_ref[0:1, :] + t * w_ref[1:2, :] + _shift_rows(t, rows, False) * w_ref[2:3, :]
        o_ref[...] = (cb_ref[...] * c3).astype(BF16)

    return pl.pallas_call(
        body, name="conv_fwd", grid=(CONV_WIDTH // CONV_CHUNK,),
        in_specs=[cu_spec, cb_spec, cc_spec, w_spec],
        out_specs=chunk,
        out_shape=jax.ShapeDtypeStruct((s, CONV_WIDTH), BF16),
        compiler_params=_params(1),
    )(z, z, z, conv_w)


def _conv_bwd(z, conv_w, d_co):
    s = z.shape[0]
    cu_spec, cb_spec, cc_spec, chunk, w_spec = _conv_specs(s)

    def body(cu_ref, cb_ref, cc_ref, w_ref, d_ref, dcu_ref, dcb_ref, dcc_ref, dw_ref):
        rows = lax.broadcasted_iota(jnp.int32, (s, 1), 0)
        cu, cc = cu_ref[...], cc_ref[...]
        t = cc * cu
        t_dn, t_up = _shift_rows(t, rows, True), _shift_rows(t, rows, False)
        c3 = t_dn * w_ref[0:1, :] + t * w_ref[1:2, :] + t_up * w_ref[2:3, :]
        d = d_ref[...]
        dcb_ref[...] = (d * c3).astype(BF16)
        dc3 = d * cb_ref[...]
        dw_ref[0:1, :] = jnp.sum(dc3 * t_dn, axis=0, keepdims=True)
        dw_ref[1:2, :] = jnp.sum(dc3 * t, axis=0, keepdims=True)
        dw_ref[2:3, :] = jnp.sum(dc3 * t_up, axis=0, keepdims=True)
        dt = _shift_rows(dc3, rows, False) * w_ref[0:1, :] + dc3 * w_ref[1:2, :] + _shift_rows(dc3, rows, True) * w_ref[2:3, :]
        dcc_ref[...] = (dt * cu).astype(BF16)
        dcu_ref[...] = (dt * cc).astype(BF16)

    return pl.pallas_call(
        body, name="conv_bwd", grid=(CONV_WIDTH // CONV_CHUNK,),
        in_specs=[cu_spec, cb_spec, cc_spec, w_spec, chunk],
        out_specs=[chunk, chunk, chunk, w_spec],
        out_shape=[jax.ShapeDtypeStruct((s, CONV_WIDTH), BF16)] * 3 + [jax.ShapeDtypeStruct((3, CONV_WIDTH), F32)],
        compiler_params=_params(1),
    )(z, z, z, conv_w, d_co)


GATE_CHUNK = 512


def _gate_specs(s, d, tr):
    n_chunks = d // GATE_CHUNK
    za = pl.BlockSpec((tr, GATE_CHUNK), lambda j, i: (i, GL_OFF // GATE_CHUNK + j))
    zc = pl.BlockSpec((tr, GATE_CHUNK), lambda j, i: (i, GL_OFF // GATE_CHUNK + n_chunks + j))
    ba = pl.BlockSpec((1, GATE_CHUNK), lambda j, i: (0, j))
    bc = pl.BlockSpec((1, GATE_CHUNK), lambda j, i: (0, n_chunks + j))
    tile = pl.BlockSpec((tr, GATE_CHUNK), lambda j, i: (i, j))
    return za, zc, ba, bc, tile


def _gate_fwd(z, b_gate, ya, yc):
    s, d = ya.shape
    tr = _row_tile(s, 512)
    za, zc, ba, bc, tile = _gate_specs(s, d, tr)

    def body(za_ref, zc_ref, ba_ref, bc_ref, ya_ref, yc_ref, o_ref):
        ga = jax.nn.sigmoid(za_ref[...] + ba_ref[...])
        gc = jax.nn.sigmoid(zc_ref[...] + bc_ref[...])
        o_ref[...] = (ga * ya_ref[...] + gc * yc_ref[...]).astype(BF16)

    return pl.pallas_call(
        body, name="gate_fwd", grid=(d // GATE_CHUNK, s // tr),
        in_specs=[za, zc, ba, bc, tile, tile],
        out_specs=tile,
        out_shape=jax.ShapeDtypeStruct((s, d), BF16),
        compiler_params=_params(2),
    )(z, z, b_gate, b_gate, ya, yc)


def _gate_bwd(z, b_gate, ya, yc, dmix):
    s, d = ya.shape
    tr = _row_tile(s, 512)
    za, zc, ba, bc, tile = _gate_specs(s, d, tr)
    vec = pl.BlockSpec((1, GATE_CHUNK), lambda j, i: (0, j))

    def body(za_ref, zc_ref, ba_ref, bc_ref, ya_ref, yc_ref, dm_ref, dya_ref, dyc_ref, dla_ref, dlc_ref, dba_ref, dbc_ref):
        ga = jax.nn.sigmoid(za_ref[...] + ba_ref[...])
        gc = jax.nn.sigmoid(zc_ref[...] + bc_ref[...])
        dm = dm_ref[...]
        dya_ref[...] = (dm * ga).astype(BF16)
        dyc_ref[...] = (dm * gc).astype(BF16)
        dla = dm * ya_ref[...] * ga * (1.0 - ga)
        dlc = dm * yc_ref[...] * gc * (1.0 - gc)
        dla_ref[...] = dla.astype(BF16)
        dlc_ref[...] = dlc.astype(BF16)
        pa = jnp.sum(dla, axis=0, keepdims=True)
        pc = jnp.sum(dlc, axis=0, keepdims=True)

        @pl.when(pl.program_id(1) == 0)
        def _():
            dba_ref[...] = pa
            dbc_ref[...] = pc

        @pl.when(pl.program_id(1) > 0)
        def _():
            dba_ref[...] += pa
            dbc_ref[...] += pc

    big = jax.ShapeDtypeStruct((s, d), BF16)
    small = jax.ShapeDtypeStruct((1, d), F32)
    return pl.pallas_call(
        body, name="gate_bwd", grid=(d // GATE_CHUNK, s // tr),
        in_specs=[za, zc, ba, bc, tile, tile, tile],
        out_specs=[tile, tile, tile, tile, vec, vec],
        out_shape=[big, big, big, big, small, small],
        compiler_params=_params(2),
    )(z, z, b_gate, b_gate, ya, yc, dmix)


def _cross_probs(q_ref, kv_ref, hd):
    cols = slice(hd * HEAD_DIM, (hd + 1) * HEAD_DIM)
    qh = q_ref[:, cols]
    kh = kv_ref[:, cols]
    sc = lax.dot_general(qh, kh, (((1,), (1,)), ((), ())), preferred_element_type=F32) * ATTN_SCALE
    e = jnp.exp(sc - jnp.max(sc, axis=1, keepdims=True))
    return qh, kh, e * (1.0 / jnp.sum(e, axis=1, keepdims=True))


def _cross_fwd(qc, kvc):
    s = qc.shape[0]
    n_mem = kvc.shape[0]
    tq = _row_tile(s, 256)

    def body(q_ref, kv_ref, o_ref):
        for hd in range(MEM_HEADS):
            _, _, p = _cross_probs(q_ref, kv_ref, hd)
            vh = kv_ref[:, MEM_WIDTH + hd * HEAD_DIM:MEM_WIDTH + (hd + 1) * HEAD_DIM]
            o_ref[:, hd * HEAD_DIM:(hd + 1) * HEAD_DIM] = jnp.dot(p.astype(BF16), vh, preferred_element_type=F32).astype(BF16)

    return pl.pallas_call(
        body, name="cross_fwd", grid=(s // tq,),
        in_specs=[pl.BlockSpec((tq, MEM_WIDTH), lambda i: (i, 0)), pl.BlockSpec((n_mem, 2 * MEM_WIDTH), lambda i: (0, 0))],
        out_specs=pl.BlockSpec((tq, MEM_WIDTH), lambda i: (i, 0)),
        out_shape=jax.ShapeDtypeStruct((s, MEM_WIDTH), BF16),
        compiler_params=_params(1),
    )(qc, kvc)


def _cross_bwd(qc, kvc, d_out):
    s = qc.shape[0]
    n_mem = kvc.shape[0]
    tq = _row_tile(s, 256)

    def body(q_ref, kv_ref, do_ref, dq_ref, dkv_ref):
        @pl.when(pl.program_id(0) == 0)
        def _():
            dkv_ref[...] = jnp.zeros_like(dkv_ref)

        for hd in range(MEM_HEADS):
            cols = slice(hd * HEAD_DIM, (hd + 1) * HEAD_DIM)
            vcols = slice(MEM_WIDTH + hd * HEAD_DIM, MEM_WIDTH + (hd + 1) * HEAD_DIM)
            qh, kh, p = _cross_probs(q_ref, kv_ref, hd)
            doh = do_ref[:, cols]
            dp = lax.dot_general(doh, kv_ref[:, vcols], (((1,), (1,)), ((), ())), preferred_element_type=F32)
            ds = (p * (dp - jnp.sum(p * dp, axis=1, keepdims=True)) * ATTN_SCALE).astype(BF16)
            dq_ref[:, cols] = jnp.dot(ds, kh, preferred_element_type=F32).astype(BF16)
            dkv_ref[:, cols] += lax.dot_general(ds, qh, (((0,), (0,)), ((), ())), preferred_element_type=F32)
            dkv_ref[:, vcols] += lax.dot_general(p.astype(BF16), doh, (((0,), (0,)), ((), ())), preferred_element_type=F32)

    qspec = pl.BlockSpec((tq, MEM_WIDTH), lambda i: (i, 0))
    kvspec = pl.BlockSpec((n_mem, 2 * MEM_WIDTH), lambda i: (0, 0))
    return pl.pallas_call(
        body, name="cross_bwd", grid=(s // tq,),
        in_specs=[qspec, kvspec, qspec],
        out_specs=[qspec, kvspec],
        out_shape=[jax.ShapeDtypeStruct((s, MEM_WIDTH), BF16), jax.ShapeDtypeStruct((n_mem, 2 * MEM_WIDTH), F32)],
        compiler_params=_params(1),
    )(qc, kvc, d_out)


def _swiglu_fwd(up, gate):
    return up, (gate * jax.nn.sigmoid(gate)) * up


def _swiglu_bwd(d_act, gate, up):
    sg = jax.nn.sigmoid(gate)
    silu = gate * sg
    return d_act * up * (sg * (1.0 + gate * (1.0 - sg))), d_act * silu


GATHER_GROUPS = {"in": ("w_in", "conv_w"), "mid": ("w_attn_out", "w_conv_out", "w_o", "w_cq", "w_ckv", "w_co"),
                 "gate": ("w_gate",), "up": ("w_up",), "down": ("w_down",)}


def _local_step(xs, mems, target, small, fetch, reduce):
    s, d = xs.shape
    w4 = {}
    cos_t, sin_t = _rope_tables(s)

    def near(group, done, then, after):
        waits = [("direct", group)] + ([("pass_near", done), ("pass_far", done)] if done else [])
        starts = [("forward", group), ("pass_near", group)] + [("direct", g) for g in then]
        tok = fetch.step("gather_near_" + group, waits, starts, after)
        if done:
            w4.update(fetch.arrays(done))
        return tok

    def far(group, then, after):
        return fetch.step("gather_far_" + group, [("forward", group)], [("pass_far", group)] + [("direct", g) for g in then], after)

    def last(group, after):
        tok = fetch.step("gather_done_" + group, [("pass_near", group), ("pass_far", group)], [], after)
        w4.update(fetch.arrays(group))
        return tok

    h = _rmsnorm(xs, small["g_mix"], "norm_mix")
    slots_filled = [a for g in ("gate", "up", "down") for a in fetch.arrays(g).values()]
    chip_x, chip_y = reduce.place[0] // 2, reduce.place[0] % 2
    own_block = jnp.stack([2 * chip_x + chip_y]).astype(jnp.int32)
    near_blocks = jnp.stack([2 * (1 - chip_x) + chip_y, 2 * chip_x + (1 - chip_y)]).astype(jnp.int32)
    far_block = jnp.stack([2 * (1 - chip_x) + (1 - chip_y)]).astype(jnp.int32)
    z = _matmul_column_blocks(h, fetch.arrays("in")["w_in"], own_block, None, tm=512, name="in_proj_own")
    tok = near("in", None, ["mid"], [z] + slots_filled)
    tok = fetch.step("gather_near_done_in", [("pass_near", "in")], [], tok)
    z = _matmul_column_blocks(h, fetch.arrays("in")["w_in"], near_blocks, z, tm=512, name="in_proj_near", after=tok)
    tok = far("in", [], z)
    tok = fetch.step("gather_done_in", [("pass_far", "in")], [], tok)
    w4.update(fetch.arrays("in"))
    z = _matmul_column_blocks(h, w4["w_in"], far_block, z, tm=512, name="in_proj_far", after=tok)
    conv4 = w4["conv_w"]
    conv_w = conv4[:, :3, :].transpose(1, 0, 2).reshape(3, N_CHIPS * conv4.shape[2])
    c_in = w4["w_in"].shape[2]
    tok = near("mid", None, ["gate"], z)
    q_rot, k_rot, v_b = _rope_fwd(z, cos_t, sin_t)
    attn = _swa_fwd(q_rot, k_rot, v_b, small["sink"])
    co = _conv_fwd(z, conv_w)
    tok = far("mid", ["up"], attn)
    tok = last("mid", tok)
    w_o = w4["w_o"].reshape(-1, w4["w_o"].shape[-1])
    c_d = w4["w_attn_out"].shape[2]
    ya = _matmul(attn, w4["w_attn_out"], mode="nn", tm=1024, tn=c_d, tk=ATTN_WIDTH, out_dtypes=[F32], name="attn_out_proj",
                 b_blocks=N_CHIPS, after=tok)
    yc = _matmul(co, w4["w_conv_out"], mode="nn", tm=1024, tn=c_d, tk=CONV_WIDTH, out_dtypes=[F32], name="conv_out_proj",
                 b_blocks=N_CHIPS)
    mix = _gate_fwd(z, small["b_gate"], ya, yc)
    x1 = _matmul(mix, w_o, mode="nn", tm=512, tn=1024, tk=d, out_dtypes=[F32], name="mix_out_proj", extras=[xs],
                 epilogue=_add_residual)
    tok = near("gate", None, ["down"], x1)
    w_cq = w4["w_cq"].reshape(-1, w4["w_cq"].shape[-1])
    w_ckv = w4["w_ckv"].reshape(-1, w4["w_ckv"].shape[-1])
    hc = _rmsnorm(x1, small["g_cross"], "norm_cross")
    memn = _rmsnorm(mems, small["g_mem"], "norm_mem")
    qc = _matmul(hc, w_cq, mode="nn", tm=1024, tn=MEM_WIDTH, tk=d, out_dtypes=[BF16], name="cross_q_proj", after=tok)
    kvc = _matmul(memn, w_ckv, mode="nn", tm=256, tn=2 * MEM_WIDTH, tk=d, out_dtypes=[BF16], name="cross_kv_proj")
    oc = _cross_fwd(qc, kvc)
    x2 = _matmul(oc, w4["w_co"], mode="nn", tm=1024, tn=c_d, tk=MEM_WIDTH, out_dtypes=[F32], name="cross_out_proj",
                 extras=[x1], epilogue=_add_residual, b_blocks=N_CHIPS)
    hf = _rmsnorm(x2, small["g_ffn"], "norm_ffn")
    tok = far("gate", [], hf)
    tok = near("up", "gate", [], tok)
    c_ff = w4["w_gate"].shape[2]
    gate = _matmul(hf, w4["w_gate"], mode="nn", tm=512, tn=c_ff, tk=d, out_dtypes=[F32], name="ffn_gate_proj", b_blocks=N_CHIPS,
                   after=tok)
    tok = far("up", [], gate)
    tok = near("down", "up", [], tok)
    up, act = _matmul(hf, w4["w_up"], mode="nn", tm=512, tn=c_ff, tk=d, out_dtypes=[F32, BF16], name="ffn_up_proj",
                      extras=[gate], epilogue=_swiglu_fwd, b_blocks=N_CHIPS, after=tok)
    tok = far("down", [], act)
    last("down", tok)
    w_down = w4["w_down"].reshape(-1, w4["w_down"].shape[-1])
    x3 = _matmul(act, w_down, mode="nn", tm=512, tn=512, tk=w_down.shape[0], out_dtypes=[F32], name="ffn_down_proj", extras=[x2],
                 epilogue=_add_residual)
    dx3, dx3b, sq, dg_final = _loss_head(x3, small["g_final"], target)

    da, du = _matmul(dx3b, w_down, mode="nt", tm=512, tn=c_ff, tk=d, out_dtypes=[BF16, BF16], name="ffn_down_bwd",
                     extras=[gate, up], epilogue=_swiglu_bwd)
    core = reduce.core
    ffn_shape = dict(row_sharded=False, tm=1024, tn=c_ff)
    g_down = _matmul(act, dx3b, mode="tn", tm=c_ff, tn=1024, tk=s, out_dtypes=[BF16], name="ffn_down_wgrad")
    tok = reduce.add("down", {"w_down": g_down}, da)
    t_gate = _wgrad_half(hf, da, core, theirs=True, name="ffn_gate_wgrad_theirs", after=tok, **ffn_shape)
    tok = reduce.step("down", t_gate)
    t_up = _wgrad_half(hf, du, core, theirs=True, name="ffn_up_wgrad_theirs", after=tok, **ffn_shape)
    tok = reduce.send("ffn", {"w_gate": t_gate, "w_up": t_up}, dx3b)
    dhf = _matmul(da, w4["w_gate"], mode="nt", tm=512, tn=1024, tk=N_CHIPS * c_ff, out_dtypes=[F32], name="ffn_gate_bwd", b_blocks=N_CHIPS,
                  after=tok)
    got = reduce.received("ffn", dhf)
    p_gate = _wgrad_half(hf, da, core, theirs=False, name="ffn_gate_wgrad_mine", add=got["w_gate"], **ffn_shape)
    p_up = _wgrad_half(hf, du, core, theirs=False, name="ffn_up_wgrad_mine", add=got["w_up"], **ffn_shape)
    tok = reduce.add_parts("ffn", {"w_gate": p_gate, "w_up": p_up})
    dhf = _matmul(du, w4["w_up"], mode="nt", tm=512, tn=1024, tk=N_CHIPS * c_ff, out_dtypes=[F32], name="ffn_up_bwd", extras=[dhf],
                  epilogue=_add_residual, b_blocks=N_CHIPS, after=tok)
    tok = reduce.step("down", dhf)
    dx2, dx2b, dg_ffn = _rmsnorm_bwd(dhf, x2, small["g_ffn"], dx3, "norm_ffn_bwd")

    d_oc = _matmul(dx2b, w4["w_co"], mode="nt", tm=1024, tn=MEM_WIDTH, tk=d, out_dtypes=[BF16], name="cross_out_bwd",
                   b_blocks=N_CHIPS, after=tok)
    g_co = _matmul(oc, dx2b, mode="tn", tm=MEM_WIDTH, tn=c_d, tk=s, out_dtypes=[BF16], name="cross_out_wgrad", out_blocks=N_CHIPS)
    tok = reduce.step("down", g_co)
    dqc, dkvc = _cross_bwd(qc, kvc, d_oc)
    g_cq = _matmul(hc, dqc, mode="tn", tm=1024, tn=MEM_WIDTH, tk=s, out_dtypes=[BF16], name="cross_q_wgrad", after=tok)
    dhc = _matmul(dqc, w_cq, mode="nt", tm=1024, tn=1024, tk=MEM_WIDTH, out_dtypes=[F32], name="cross_q_bwd")
    g_ckv = _matmul(memn, dkvc, mode="tn", tm=1024, tn=2 * MEM_WIDTH, tk=mems.shape[0], out_dtypes=[BF16], name="cross_kv_wgrad")
    dmemn = _matmul(dkvc, w_ckv, mode="nt", tm=256, tn=1024, tk=2 * MEM_WIDTH, out_dtypes=[F32], name="cross_kv_bwd")
    _, _, dg_mem = _rmsnorm_bwd(dmemn, mems, small["g_mem"], None, "norm_mem_bwd")
    dx1, dx1b, dg_cross = _rmsnorm_bwd(dhc, x1, small["g_cross"], dx2, "norm_cross_bwd")

    dmix = _matmul(dx1b, w_o, mode="nt", tm=512, tn=1024, tk=d, out_dtypes=[F32], name="mix_out_bwd")
    g_o = _matmul(mix, dx1b, mode="tn", tm=1024, tn=1024, tk=s, out_dtypes=[BF16], name="mix_out_wgrad")
    dya, dyc, dgl_a, dgl_c, db_a, db_c = _gate_bwd(z, small["b_gate"], ya, yc, dmix)
    d_attn = _matmul(dya, w4["w_attn_out"], mode="nt", tm=1024, tn=ATTN_WIDTH, tk=d, out_dtypes=[BF16], name="attn_out_bwd",
                     b_blocks=N_CHIPS)
    g_ao = _matmul(attn, dya, mode="tn", tm=ATTN_WIDTH, tn=c_d, tk=s, out_dtypes=[BF16], name="attn_out_wgrad", out_blocks=N_CHIPS)
    d_co = _matmul(dyc, w4["w_conv_out"], mode="nt", tm=1024, tn=CONV_WIDTH, tk=d, out_dtypes=[F32], name="conv_out_bwd",
                   b_blocks=N_CHIPS)
    g_cvo = _matmul(co, dyc, mode="tn", tm=CONV_WIDTH, tn=c_d, tk=s, out_dtypes=[BF16], name="conv_out_wgrad", out_blocks=N_CHIPS)
    tok = reduce.step("ffn", g_cvo)
    tok = reduce.add("mid", {"w_co": g_co, "w_cq": g_cq, "w_ckv": g_ckv, "w_o": g_o, "w_attn_out": g_ao, "w_conv_out": g_cvo}, tok)
    dcu, dcb, dcc, d_conv_w = _conv_bwd(z, conv_w, d_co)
    dq_rot, dk_rot, dv, dsink = _swa_bwd(q_rot, k_rot, v_b, d_attn, small["sink"])
    tok = reduce.step("mid", dq_rot)
    dq, dk, dvb = _rope_bwd(dq_rot, dk_rot, dv, cos_t, sin_t)
    dz = jnp.concatenate([dq, dk, dvb, dcu, dcb, dcc, dgl_a, dgl_c], axis=1)
    in_shape = dict(row_sharded=False, tm=1024, tn=c_in)
    t_in = _wgrad_half(h, dz, core, theirs=True, name="in_proj_wgrad_theirs", after=tok, **in_shape)
    tok = reduce.send("in", {"w_in": t_in}, dk)
    tok = reduce.step("ffn", tok)
    tok = reduce.step("mid", tok)
    got = reduce.received("in", tok)
    p_in = _wgrad_half(h, dz, core, theirs=False, name="in_proj_wgrad_mine", add=got["w_in"], **in_shape)
    tok = reduce.add_parts("in", {"w_in": p_in})
    dh = _matmul(dz, w4["w_in"], mode="nt", tm=512, tn=512, tk=N_CHIPS * c_in, out_dtypes=[F32], name="in_proj_bwd", b_blocks=N_CHIPS,
                 after=tok)
    tok = reduce.step("mid", dh)
    grad_x, _, dg_mix = _rmsnorm_bwd(dh, xs, small["g_mix"], dx1, "norm_mix_bwd")

    small_grads = {
        "g_mix": dg_mix, "sink": dsink[:, 0], "b_gate": jnp.concatenate([db_a, db_c], axis=1), "g_cross": dg_cross,
        "g_mem": dg_mem, "g_ffn": dg_ffn, "g_final": dg_final, "conv_w": d_conv_w,
    }
    return sq, grad_x, small_grads


def _pair_sum(g4, ra, core, name):
    nb, rs, cs = g4.shape
    rh = rs // 2
    tr = _row_tile(rh, 256)
    per = rh // tr

    def body(c_ref, g_ref, r_ref, o_ref):
        o_ref[...] = (g_ref[...].astype(F32) + r_ref[...].astype(F32)).astype(BF16)

    plain = pl.BlockSpec((None, tr, cs), lambda j, i, c: (j, i, 0))
    return pl.pallas_call(
        body, name=name,
        grid_spec=pltpu.PrefetchScalarGridSpec(
            num_scalar_prefetch=1, grid=(nb, per),
            in_specs=[pl.BlockSpec((None, tr, cs), lambda j, i, c: (j, c[0] * per + i, 0)), plain],
            out_specs=plain),
        out_shape=jax.ShapeDtypeStruct((nb, rh, cs), BF16),
        compiler_params=_params(2),
    )(core, g4, ra)


def _quad_sum(parts, rc, place, name):
    _, rh, cs = parts.shape
    tr = _row_tile(rh, 256)
    per = rh // tr

    def body(p_ref, own_ref, r_ref, o_ref):
        acc = own_ref[...].astype(F32)
        for j in range(rc.shape[0]):
            acc = acc + r_ref[j].astype(F32)
        o_ref[...] = acc

    return pl.pallas_call(
        body, name=name,
        grid_spec=pltpu.PrefetchScalarGridSpec(
            num_scalar_prefetch=1, grid=(per,),
            in_specs=[pl.BlockSpec((None, tr, cs), lambda i, p: (p[0], i, 0)),
                      pl.BlockSpec((rc.shape[0], tr, cs), lambda i, p: (0, i, 0))],
            out_specs=pl.BlockSpec((tr, cs), lambda i, p: (p[1] * per + i, 0))),
        out_shape=jax.ShapeDtypeStruct((2 * rh, cs), F32),
        compiler_params=_params(1),
    )(place, parts, rc)


def _adamw_update(w, g, m, v):
    nm = ADAM_B1 * m + (1.0 - ADAM_B1) * g
    nv = ADAM_B2 * v + (1.0 - ADAM_B2) * (g * g)
    m_hat = nm / ADAM_C1
    v_hat = nv / ADAM_C2
    return -ADAM_LR * (m_hat / (jnp.sqrt(v_hat) + ADAM_EPS) + ADAM_WD * w), nm, nv


def _adamw_own_half(w, m, v, parts, rc, place, name, after=None):
    rows, cols = w.shape
    rh = rows // 2
    tr = _row_tile(rh, 256)
    per = rh // tr

    def body(p_ref, w_ref, m_ref, v_ref, own_ref, r_ref, *rest):
        gx_ref, g_ref, d_ref, nm_ref, nv_ref = rest[-5:]
        g = own_ref[...].astype(F32)
        for j in range(rc.shape[0]):
            g = g + r_ref[j].astype(F32)
        gx_ref[...] = g
        g_ref[...] = g
        d_ref[...], nm_ref[...], nv_ref[...] = _adamw_update(w_ref[...], g, m_ref[...], v_ref[...])

    mine = pl.BlockSpec((tr, cols), lambda i, p: (p[1] * per + i, 0))
    shape = jax.ShapeDtypeStruct((rows, cols), F32)
    return pl.pallas_call(
        body, name=name,
        grid_spec=pltpu.PrefetchScalarGridSpec(
            num_scalar_prefetch=1, grid=(per,),
            in_specs=[mine, mine, mine, pl.BlockSpec((None, tr, cols), lambda i, p: (p[0], i, 0)),
                      pl.BlockSpec((rc.shape[0], tr, cols), lambda i, p: (0, i, 0))] + ([] if after is None else [ANY]),
            out_specs=[mine] * 5),
        out_shape=[shape] * 5,
        compiler_params=_params(1),
    )(place, w, m, v, parts, rc, *([] if after is None else [after]))


def _adamw_other_half(w, m, v, g_exchanged, g, delta, new_m, new_v, place, name, after=None):
    rows, cols = w.shape
    rh = rows // 2
    tr = _row_tile(rh, 256)
    per = rh // tr

    def body(p_ref, w_ref, m_ref, v_ref, gx_ref, *rest):
        g_ref, d_ref, nm_ref, nv_ref = rest[-4:]
        gv = gx_ref[...]
        g_ref[...] = gv
        d_ref[...], nm_ref[...], nv_ref[...] = _adamw_update(w_ref[...], gv, m_ref[...], v_ref[...])

    other = pl.BlockSpec((tr, cols), lambda i, p: ((1 - p[1]) * per + i, 0))
    shape = jax.ShapeDtypeStruct((rows, cols), F32)
    n_after = 0 if after is None else 1
    return pl.pallas_call(
        body, name=name,
        grid_spec=pltpu.PrefetchScalarGridSpec(
            num_scalar_prefetch=1, grid=(per,),
            in_specs=[other] * 4 + [ANY] * (4 + n_after),
            out_specs=[other] * 4),
        out_shape=[shape] * 4,
        input_output_aliases={5: 0, 6: 1, 7: 2, 8: 3},
        compiler_params=_params(1),
    )(place, w, m, v, g_exchanged, g, delta, new_m, new_v, *([] if after is None else [after]))


def _cast_to_slot(w, place, dtype, name, after=None):
    rows, cols = w.shape
    tr = _row_tile(rows, 256)

    def body(p_ref, w_ref, *rest):
        o_ref = rest[-1]
        o_ref[...] = w_ref[...].astype(dtype)

    return pl.pallas_call(
        body, name=name,
        grid_spec=pltpu.PrefetchScalarGridSpec(
            num_scalar_prefetch=1, grid=(rows // tr,),
            in_specs=[pl.BlockSpec((tr, cols), lambda i, p: (i, 0))] + ([] if after is None else [ANY]),
            out_specs=pl.BlockSpec((None, tr, cols), lambda i, p: (p[0], i, 0))),
        out_shape=jax.ShapeDtypeStruct((N_CHIPS, rows, cols), dtype),
        compiler_params=_params(1),
    )(place, w, *([] if after is None else [after]))


def _adamw(w, g, m, v, name, after=None):
    rows, cols = w.shape
    tr = _row_tile(rows, 256)

    def body(w_ref, g_ref, m_ref, v_ref, *rest):
        go_ref, d_ref, nm_ref, nv_ref = rest[-4:]
        gv = g_ref[...]
        go_ref[...] = gv
        d_ref[...], nm_ref[...], nv_ref[...] = _adamw_update(w_ref[...], gv, m_ref[...], v_ref[...])

    tile = pl.BlockSpec((tr, cols), lambda i: (i, 0))
    shape = jax.ShapeDtypeStruct((rows, cols), F32)
    return pl.pallas_call(
        body, name=name, grid=(rows // tr,),
        in_specs=[tile] * 4 + ([] if after is None else [ANY]), out_specs=[tile] * 4, out_shape=[shape] * 4,
        compiler_params=_params(1),
    )(w, g, m, v, *([] if after is None else [after]))


def _mesh_pos():
    return lax.axis_index("x"), lax.axis_index("y"), lax.axis_index("c")


def _other_chips(x, y):
    return [(1 - x, y), (x, 1 - y), (1 - x, 1 - y)]


def _half_rows(ref, which):
    rh = ref.shape[-2] // 2
    return ref.at[pl.ds(which * rh, rh), :]


def _remote(src, dst, send_sems, recv_sems, sem, to):
    return pltpu.make_async_remote_copy(src_ref=src, dst_ref=dst, send_sem=send_sems.at[sem], recv_sem=recv_sems.at[sem],
                                        device_id=to, device_id_type=MESH)


HBM = pl.BlockSpec(memory_space=pltpu.HBM)
SEM = pl.BlockSpec(memory_space=pltpu.SEMAPHORE)
DATAFLOW_EFFECT = pltpu.SideEffectType.DATAFLOW_SIDE_EFFECTING


def _in_hbm(arrays):
    return [pltpu.with_memory_space_constraint(a, pltpu.HBM) for a in arrays]


def _hbm_like(arrays):
    return [pltpu.HBM(a.shape, a.dtype) for a in arrays]


GATHER_COPIES_PER_ARRAY = {"direct": 2, "forward": 2, "pass_near": 2, "pass_far": 1}


def _gather_copies(kind, refs, x, y, c):
    me, near_x, near_y, far = 2 * x + y, 2 * (1 - x) + y, 2 * x + (1 - y), 2 * (1 - x) + (1 - y)
    to_x, to_y, sibling = (1 - x, y, c), (x, 1 - y, c), (x, y, 1 - c)
    out = []
    for ref in refs:
        rh = ref.shape[1] // 2
        rq = rh // 2

        def half(chip, ref=ref, rh=rh):
            return ref.at[chip, pl.ds(c * rh, rh), :]

        def quarter(chip, q, ref=ref, rh=rh, rq=rq):
            return ref.at[chip, pl.ds(c * rh + q * rq, rq), :]

        if kind == "direct":
            out += [(half(me), half(me), to_x), (half(me), half(me), to_y)]
        elif kind == "forward":
            out += [(quarter(near_x, 0), quarter(near_x, 0), to_y), (quarter(near_y, 1), quarter(near_y, 1), to_x)]
        elif kind == "pass_near":
            out += [(half(near_x), half(near_x), sibling), (half(near_y), half(near_y), sibling)]
        else:
            assert kind == "pass_far"
            out += [(half(far), half(far), sibling)]
    return out


def _gather_step(name, bufs, waits, starts, after):
    nb, nw, ns = len(bufs), len(waits), len(starts)
    after = [] if after is None else list(after) if isinstance(after, (list, tuple)) else [after]
    n_after = len(after)

    def body(*refs):
        ins = refs[:nb]
        wait_sems = refs[nb:nb + 2 * nw]
        start_sems = refs[nb + 2 * nw + n_after:nb + 2 * nw + n_after + 2 * ns]
        token = refs[-1]
        x, y, c = _mesh_pos()
        for j, (kind, idxs, _, _) in enumerate(waits):
            for i, (s_ref, d_ref, to) in enumerate(_gather_copies(kind, [ins[t] for t in idxs], x, y, c)):
                came = _remote(s_ref, d_ref, wait_sems[2 * j], wait_sems[2 * j + 1], i, to)
                came.wait_recv()
                came.wait_send()
        for j, (kind, idxs) in enumerate(starts):
            for i, (s_ref, d_ref, to) in enumerate(_gather_copies(kind, [ins[t] for t in idxs], x, y, c)):
                _remote(s_ref, d_ref, start_sems[2 * j], start_sems[2 * j + 1], i, to).start()
        token[...] = jnp.zeros_like(token)

    sems = []
    for kind, idxs in starts:
        sems += [pltpu.SemaphoreType.DMA((GATHER_COPIES_PER_ARRAY[kind] * len(idxs),))] * 2
    operands = _in_hbm(bufs) + [sem for w in waits for sem in w[2:]] + after
    outs = pl.pallas_call(
        body, name=name,
        in_specs=[HBM] * nb + [SEM] * (2 * nw) + [ANY] * n_after,
        out_specs=[SEM] * (2 * ns) + [HBM] * nb + [pl.BlockSpec(memory_space=pltpu.VMEM)],
        out_shape=sems + _hbm_like(bufs) + [jax.ShapeDtypeStruct((8, 128), F32)],
        input_output_aliases={i: 2 * ns + i for i in range(nb)},
        compiler_params=pltpu.CompilerParams(has_side_effects=DATAFLOW_EFFECT),
    )(*operands)
    return outs[2 * ns:2 * ns + nb], [(outs[2 * j], outs[2 * j + 1]) for j in range(ns)], outs[-1]


class _Gather:
    def __init__(self, groups):
        self.groups = groups
        self.bufs = {}
        self.in_flight = {}

    def put(self, slotted):
        self.bufs.update(slotted)

    def step(self, name, waits, starts, after=None):
        names = []
        for _, group in list(waits) + list(starts):
            names += [n for n in self.groups[group] if n not in names]
        index = {n: i for i, n in enumerate(names)}

        def members(group):
            return [index[n] for n in self.groups[group]]

        wait_args = [(kind, members(group)) + self.in_flight.pop((kind, group)) for kind, group in waits]
        start_args = [(kind, members(group)) for kind, group in starts]
        bufs, sems, token = _gather_step(name, [self.bufs[n] for n in names], wait_args, start_args, after)
        self.bufs.update(zip(names, bufs))
        for (kind, group), pair in zip(starts, sems):
            self.in_flight[(kind, group)] = pair
        return token

    def arrays(self, group):
        return {n: self.bufs[n] for n in self.groups[group]}


def _sibling_halves_copies(srcs, dsts, x, y, c):
    out = []
    for s_ref, d_ref in zip(srcs, dsts, strict=True):
        rh = s_ref.shape[1] // 2
        out.append((s_ref.at[:, pl.ds((1 - c) * rh, rh), :], d_ref, (x, y, 1 - c)))
    return out


def _to_sibling_copies(srcs, dsts, x, y, c):
    return [(s_ref, d_ref, (x, y, 1 - c)) for s_ref, d_ref in zip(srcs, dsts, strict=True)]


def _chip_copies(srcs, dsts, x, y, c):
    out = []
    for s_ref, d_ref in zip(srcs, dsts, strict=True):
        for k, (px, py) in enumerate(_other_chips(x, y)):
            out.append((s_ref.at[2 * px + py], d_ref.at[k], (px, py, c)))
    return out


def _join_copies(srcs, dsts, x, y, c):
    out = []
    for s_ref in srcs:
        mine = _half_rows(s_ref, c)
        out.append((mine, mine, (x, y, 1 - c)))
    return out


def _exchange_start(copies_fn, n_copies, srcs, fresh, after, name):
    ns, nb = len(srcs), len(srcs) + len(fresh)

    def body(*refs):
        bufs, send, recv, token = refs[:nb], refs[nb + 1], refs[nb + 2], refs[-1]
        x, y, c = _mesh_pos()
        for i, (s_ref, d_ref, to) in enumerate(copies_fn(bufs[:ns], bufs[ns:] if fresh else bufs[:ns], x, y, c)):
            _remote(s_ref, d_ref, send, recv, i, to).start()
        token[...] = jnp.zeros_like(token)

    sems = [pltpu.SemaphoreType.DMA((n_copies,))] * 2
    outs = pl.pallas_call(
        body, name=name,
        in_specs=[HBM] * nb + [ANY], out_specs=[SEM, SEM] + [HBM] * nb + [pl.BlockSpec(memory_space=pltpu.VMEM)],
        out_shape=sems + _hbm_like(list(srcs) + list(fresh)) + [jax.ShapeDtypeStruct((8, 128), F32)],
        input_output_aliases={i: 2 + i for i in range(nb)},
        compiler_params=pltpu.CompilerParams(has_side_effects=DATAFLOW_EFFECT),
    )(*_in_hbm(list(srcs) + list(fresh)), after)
    return outs[0], outs[1], outs[2:2 + ns], outs[2 + ns:2 + nb], outs[-1]


def _exchange_done(copies_fn, srcs, fresh, send, recv, after, name):
    ns, nb = len(srcs), len(srcs) + len(fresh)

    def body(*refs):
        bufs, send_in, recv_in = refs[:nb], refs[nb], refs[nb + 1]
        x, y, c = _mesh_pos()
        for i, (s_ref, d_ref, to) in enumerate(copies_fn(bufs[:ns], bufs[ns:] if fresh else bufs[:ns], x, y, c)):
            came = _remote(s_ref, d_ref, send_in, recv_in, i, to)
            came.wait_send()
            came.wait_recv()

    outs = pl.pallas_call(
        body, name=name,
        in_specs=[HBM] * nb + [SEM, SEM, ANY], out_specs=[HBM] * nb,
        out_shape=_hbm_like(list(srcs) + list(fresh)),
        input_output_aliases={i: i for i in range(nb)},
        compiler_params=pltpu.CompilerParams(has_side_effects=DATAFLOW_EFFECT),
    )(*_in_hbm(list(srcs) + list(fresh)), send, recv, after)
    return outs[:ns], outs[ns:]


class _Reduce:
    def __init__(self, place, core, shards, mom_m, mom_v):
        self.place, self.core = place, core
        self.shards, self.mom_m, self.mom_v = shards, mom_m, mom_v
        self.state = {}
        self.results = {}

    def add(self, group, grads, after):
        names = list(grads)
        g4s = [g.reshape((N_CHIPS, -1, g.shape[-1])) if g.ndim == 2 else g for g in grads.values()]
        fresh = [lax.empty((N_CHIPS, g.shape[1] // 2, g.shape[2]), BF16) for g in g4s]
        send, recv, g4s, fresh, token = _exchange_start(_sibling_halves_copies, len(names), g4s, fresh, after,
                                                        "pair_start_" + group)
        self.state[group] = (0, names, send, recv, g4s, fresh)
        return token

    def send(self, group, theirs, after):
        names, srcs = list(theirs), list(theirs.values())
        fresh = [lax.empty(s.shape, BF16) for s in srcs]
        send, recv, srcs, fresh, token = _exchange_start(_to_sibling_copies, len(names), srcs, fresh, after, "pair_start_" + group)
        self.state[group] = ("sent", names, send, recv, srcs, fresh)
        return token

    def received(self, group, after):
        stage, names, send, recv, srcs, fresh = self.state.pop(group)
        assert stage == "sent"
        _, got = _exchange_done(_to_sibling_copies, srcs, fresh, send, recv, after, "pair_done_" + group)
        return dict(zip(names, got))

    def add_parts(self, group, parts):
        names, srcs = list(parts), list(parts.values())
        fresh = [lax.empty((N_CHIPS - 1,) + p.shape[1:], BF16) for p in srcs]
        send, recv, srcs, fresh, token = _exchange_start(_chip_copies, 3 * len(names), srcs, fresh, self.core, "chips_start_" + group)
        self.state[group] = (1, names, send, recv, srcs, fresh)
        return token

    def step(self, group, after):
        stage, names, send, recv, srcs, fresh = self.state[group]
        if stage == 0:
            g4s, ras = _exchange_done(_sibling_halves_copies, srcs, fresh, send, recv, after, "pair_done_" + group)
            parts = [_pair_sum(g, r, self.core, "pair_sum_" + n) for g, r, n in zip(g4s, ras, names)]
            fresh = [lax.empty((N_CHIPS - 1,) + p.shape[1:], BF16) for p in parts]
            send, recv, parts, fresh, token = _exchange_start(_chip_copies, 3 * len(names), parts, fresh, self.core,
                                                              "chips_start_" + group)
            self.state[group] = (1, names, send, recv, parts, fresh)
            return token
        if stage == 1:
            parts, rcs = _exchange_done(_chip_copies, srcs, fresh, send, recv, after, "chips_done_" + group)
            token = None
            for n, p, r in zip(names, parts, rcs):
                self.results[n] = _adamw_own_half(self.shards[n], self.mom_m[n], self.mom_v[n], p, r, self.place,
                                                  "adamw_own_" + n, after=token)
                token = self.results[n][2]
            wholes = [self.results[n][0] for n in names]
            send, recv, wholes, _, token = _exchange_start(_join_copies, len(names), wholes, [], token, "join_start_" + group)
            self.state[group] = (2, names, send, recv, wholes, [])
            return token
        assert stage == 2
        wholes, _ = _exchange_done(_join_copies, srcs, [], send, recv, after, "join_done_" + group)
        token = None
        for n, exchanged in zip(names, wholes):
            _, g, d, nm, nv = self.results[n]
            self.results[n] = _adamw_other_half(self.shards[n], self.mom_m[n], self.mom_v[n], exchanged, g, d, nm, nv,
                                                self.place, "adamw_other_" + n, after=token)
            token = self.results[n][1]
        del self.state[group]
        return token


N_DEV = 8


def _all_reduce_small(v):
    def body(v_ref, o_ref, slots, send_sems, recv_sems):
        x, y, c = _mesh_pos()
        me = 4 * x + 2 * y + c
        slots[me] = v_ref[...]
        peers = []
        for r in range(1, N_DEV):
            fx, fy, fc = (r >> 2) & 1, (r >> 1) & 1, r & 1
            peers.append((x + fx - 2 * x * fx, y + fy - 2 * y * fy, c + fc - 2 * c * fc))
        sends = []
        for r, peer in enumerate(peers):
            cp = _remote(v_ref, slots.at[me], send_sems, recv_sems, r, peer)
            cp.start()
            sends.append(cp)
        for r, (px, py, pc) in enumerate(peers):
            landed = slots.at[4 * px + 2 * py + pc]
            _remote(landed, landed, send_sems, recv_sems, r, (px, py, pc)).wait_recv()
        for cp in sends:
            cp.wait_send()
        acc = slots[0]
        for i in range(1, N_DEV):
            acc = acc + slots[i]
        o_ref[...] = acc

    vm = pl.BlockSpec(memory_space=pltpu.VMEM)
    return pl.pallas_call(
        body, name="small_grads_all_reduce",
        in_specs=[vm], out_specs=vm,
        out_shape=jax.ShapeDtypeStruct(v.shape, v.dtype),
        scratch_shapes=[pltpu.VMEM((N_DEV,) + v.shape, v.dtype), pltpu.SemaphoreType.DMA((N_DEV - 1,)),
                        pltpu.SemaphoreType.DMA((N_DEV - 1,))],
    )(v)


MATRICES = ("w_in", "w_attn_out", "w_conv_out", "w_o", "w_cq", "w_ckv", "w_co", "w_gate", "w_up", "w_down")
VECTORS = ("g_mix", "b_gate", "g_cross", "g_mem", "g_ffn", "g_final", "conv_w", "sink")
WEIGHT_ORDER = ("g_mix", "w_in", "sink", "conv_w", "b_gate", "w_attn_out", "w_conv_out", "w_o", "g_cross", "g_mem", "w_cq",
                "w_ckv", "w_co", "g_ffn", "w_gate", "w_up", "w_down", "g_final")
CONV_PAD_ROWS = 32
SMALL_ROWS = 8


def _pack(pieces):
    flat = jnp.concatenate([p.reshape(-1) for p in pieces])
    lane_group = SMALL_ROWS * 128
    total = -(-flat.shape[0] // lane_group) * lane_group
    flat = jnp.pad(flat, (0, total - flat.shape[0]))
    return flat.reshape(SMALL_ROWS, total // SMALL_ROWS), [p.size for p in pieces]


def _unpack(packed, pieces):
    flat = packed.reshape(-1)
    out, off = [], 0
    for p in pieces:
        out.append(flat[off:off + p.size].reshape(p.shape))
        off += p.size
    return out


def kernel(x, mem, g_mix, w_in, sink, conv_w, b_gate, w_attn_out, w_conv_out, w_o, g_cross, g_mem, w_cq, w_ckv, w_co, g_ffn, w_gate, w_up, w_down, g_final, loss_target, m_g_mix, m_w_in, m_sink, m_conv_w, m_b_gate, m_w_attn_out, m_w_conv_out, m_w_o, m_g_cross, m_g_mem, m_w_cq, m_w_ckv, m_w_co, m_g_ffn, m_w_gate, m_w_up, m_w_down, m_g_final, v_g_mix, v_w_in, v_sink, v_conv_w, v_b_gate, v_w_attn_out, v_w_conv_out, v_w_o, v_g_cross, v_g_mem, v_w_cq, v_w_ckv, v_w_co, v_g_ffn, v_w_gate, v_w_up, v_w_down, v_g_final):
    given = dict(g_mix=g_mix, w_in=w_in, sink=sink, conv_w=conv_w, b_gate=b_gate, w_attn_out=w_attn_out, w_conv_out=w_conv_out,
                 w_o=w_o, g_cross=g_cross, g_mem=g_mem, w_cq=w_cq, w_ckv=w_ckv, w_co=w_co, g_ffn=g_ffn, w_gate=w_gate, w_up=w_up,
                 w_down=w_down, g_final=g_final)
    mom_m = dict(g_mix=m_g_mix, w_in=m_w_in, sink=m_sink, conv_w=m_conv_w, b_gate=m_b_gate, w_attn_out=m_w_attn_out,
                 w_conv_out=m_w_conv_out, w_o=m_w_o, g_cross=m_g_cross, g_mem=m_g_mem, w_cq=m_w_cq, w_ckv=m_w_ckv, w_co=m_w_co,
                 g_ffn=m_g_ffn, w_gate=m_w_gate, w_up=m_w_up, w_down=m_w_down, g_final=m_g_final)
    mom_v = dict(g_mix=v_g_mix, w_in=v_w_in, sink=v_sink, conv_w=v_conv_w, b_gate=v_b_gate, w_attn_out=v_w_attn_out,
                 w_conv_out=v_w_conv_out, w_o=v_w_o, g_cross=v_g_cross, g_mem=v_g_mem, w_cq=v_w_cq, w_ckv=v_w_ckv, w_co=v_w_co,
                 g_ffn=v_g_ffn, w_gate=v_w_gate, w_up=v_w_up, w_down=v_w_down, g_final=v_g_final)
    xs, mems, target = x[0], mem[0], loss_target[0]
    d_model = xs.shape[1]
    chip = 2 * lax.axis_index("x") + lax.axis_index("y")
    core = jnp.reshape(lax.axis_index("c"), (1,)).astype(jnp.int32)
    place = jnp.stack([chip, lax.axis_index("c")]).astype(jnp.int32)

    shards = {n: given[n][0] for n in MATRICES}
    conv_cols = conv_w.shape[2]
    conv_pad = jnp.pad(conv_w[0], ((0, CONV_PAD_ROWS - conv_w.shape[1]), (0, 0)))
    fetch = _Gather(GATHER_GROUPS)
    first = {"w_in": _cast_to_slot(shards["w_in"], place, BF16, "to_slot_w_in"),
             "conv_w": _cast_to_slot(conv_pad, place, F32, "to_slot_conv_w")}
    fetch.put(first)
    tok = fetch.step("gather_start", [], [("direct", "in")])
    fetch.put({n: _cast_to_slot(shards[n], place, BF16, "to_slot_" + n, after=tok) for n in MATRICES if n != "w_in"})
    small = {n: given[n] for n in ("g_mix", "b_gate", "g_cross", "g_mem", "g_ffn")}
    small["g_final"] = g_final[None]
    small["sink"] = sink[0]

    reduce = _Reduce(place, core, shards, {n: mom_m[n][0] for n in MATRICES}, {n: mom_v[n][0] for n in MATRICES})
    sq, grad_x, small_grads = _local_step(xs, mems, target, small, fetch, reduce)

    loss_part = 0.5 * sq[0:1, 0:1] / d_model
    pieces = [small_grads[n] for n in VECTORS] + [loss_part]
    packed, _ = _pack(pieces)
    summed = _unpack(_all_reduce_small(packed), pieces)
    loss = summed[-1][0, 0]
    small_sum = dict(zip(VECTORS, summed[:-1]))
    small_sum["conv_w"] = lax.dynamic_slice_in_dim(small_sum["conv_w"], chip * conv_cols, conv_cols, axis=1)

    grad_out, delta, new_m, new_v = {}, {}, {}, {}
    like = [given[n] for n in VECTORS]
    pw, _ = _pack(like)
    pg, _ = _pack([small_sum[n] for n in VECTORS])
    pm, _ = _pack([mom_m[n] for n in VECTORS])
    pv, _ = _pack([mom_v[n] for n in VECTORS])
    _, pd, pnm, pnv = _adamw(pw, pg, pm, pv, "adamw_small")
    for n, g, d, nm, nv in zip(VECTORS, [small_sum[n] for n in VECTORS], _unpack(pd, like), _unpack(pnm, like), _unpack(pnv, like)):
        grad_out[n] = g.reshape(given[n].shape)
        delta[n], new_m[n], new_v[n] = d, nm, nv
    tok = reduce.step("in", pd)
    reduce.step("in", tok)
    for n in MATRICES:
        g, d, nm, nv = reduce.results[n]
        grad_out[n], delta[n], new_m[n], new_v[n] = g[None], d[None], nm[None], nv[None]

    return (loss, grad_x[None], *[grad_out[n] for n in WEIGHT_ORDER], *[delta[n] for n in WEIGHT_ORDER],
            *[new_m[n] for n in WEIGHT_ORDER], *[new_v[n] for n in WEIGHT_ORDER])
```

```python
import functools

import jax
import jax.numpy as jnp
from jax import lax
from jax.experimental import pallas as pl
from jax.experimental.pallas import tpu as pltpu

F32 = jnp.float32
BF16 = jnp.bfloat16
MESH = pl.DeviceIdType.MESH
ANY = pl.BlockSpec(memory_space=pl.ANY)

VMEM_LIMIT_BYTES = 56 * 1024 * 1024

N_CHIPS = 4
HEAD_DIM = 128
N_Q_HEADS = 8
N_KV_HEADS = 2
Q_GROUP = N_Q_HEADS // N_KV_HEADS
ATTN_WIDTH = N_Q_HEADS * HEAD_DIM
KV_WIDTH = N_KV_HEADS * HEAD_DIM
WINDOW = 128
BLOCK = 128
BAND = 3 * BLOCK
ROPE_THETA = 10000.0
CONV_WIDTH = 1024
MEM_HEADS = 4
MEM_WIDTH = MEM_HEADS * HEAD_DIM
RMS_EPS = 1e-6
NEG_INF = -1e30
ATTN_SCALE = HEAD_DIM ** -0.5

Q_OFF, K_OFF, V_OFF, CU_OFF, CB_OFF, CC_OFF, GL_OFF = 0, 1024, 1280, 1536, 2560, 3584, 4608

ADAM_LR = 0.001
ADAM_B1 = 0.9
ADAM_B2 = 0.999
ADAM_EPS = 1e-08
ADAM_WD = 0.01
ADAM_STEP = 10
ADAM_C1 = 1.0 - ADAM_B1 ** ADAM_STEP
ADAM_C2 = 1.0 - ADAM_B2 ** ADAM_STEP


def _params(n_grid_axes):
    return pltpu.CompilerParams(dimension_semantics=("arbitrary",) * n_grid_axes, vmem_limit_bytes=VMEM_LIMIT_BYTES)


BF16_SUBLANES = 16


def _row_tile(rows, want):
    if rows <= want:
        return rows
    for t in range(want, 0, -BF16_SUBLANES):
        if rows % t == 0:
            return t
    return rows


def _matmul(a, b, *, mode, tm, tn, tk, out_dtypes, name, extras=(), epilogue=None, b_blocks=1, out_blocks=1, after=None):
    if mode == "tn":
        kdim, m = a.shape
    else:
        m, kdim = a.shape
    if b_blocks > 1:
        nb, brows, bcols = b.shape
        assert nb == b_blocks
        if mode == "nn":
            n = bcols * nb
            assert brows == kdim
        else:
            assert mode == "nt" and bcols * nb == kdim
            n = brows
    else:
        n = b.shape[0] if mode == "nt" else b.shape[1]
    tm, tn = min(tm, m), min(tn, n)
    assert m % tm == 0 and n % tn == 0 and tk == kdim, (name, m, n, kdim, tm, tn, tk)
    n_extra, n_out = len(extras), len(out_dtypes)
    n_after = 0 if after is None else 1

    if mode == "tn":
        a_spec = pl.BlockSpec((tk, tm), lambda j, i, k: (k, i))
        dims = (((0,), (0,)), ((), ()))
    else:
        a_spec = pl.BlockSpec((tm, tk), lambda j, i, k: (i, k))
        dims = (((1,), (0,)), ((), ())) if mode == "nn" else (((1,), (1,)), ((), ()))

    if b_blocks > 1 and mode == "nn":
        per = b.shape[2] // tn
        assert b.shape[2] % tn == 0
        b_spec = pl.BlockSpec((None, tk, tn), lambda j, i, k: (j // per, k, j % per))
    elif b_blocks > 1:
        b_spec = pl.BlockSpec((b_blocks, tn, b.shape[2]), lambda j, i, k: (0, j, 0))
    elif mode == "nt":
        b_spec = pl.BlockSpec((tn, tk), lambda j, i, k: (j, k))
    else:
        b_spec = pl.BlockSpec((tk, tn), lambda j, i, k: (k, j))

    tile_spec = pl.BlockSpec((tm, tn), lambda j, i, k: (i, j))
    if out_blocks > 1:
        ncols = n // out_blocks
        assert ncols % tn == 0
        oper = ncols // tn
        out_spec = pl.BlockSpec((None, tm, tn), lambda j, i, k: (j // oper, i, j % oper))
        out_shape = [jax.ShapeDtypeStruct((out_blocks, m, ncols), dt) for dt in out_dtypes]
    else:
        out_spec = tile_spec
        out_shape = [jax.ShapeDtypeStruct((m, n), dt) for dt in out_dtypes]

    def body(a_ref, b_ref, *rest):
        extra_refs = rest[:n_extra]
        out_refs = rest[n_extra + n_after:n_extra + n_after + n_out]
        if mode == "nt" and b_blocks > 1:
            cs = b.shape[2]
            acc = None
            for jb in range(b_blocks):
                prod = lax.dot_general(a_ref[:, jb * cs:(jb + 1) * cs].astype(BF16), b_ref[jb].astype(BF16), dims,
                                       preferred_element_type=F32)
                acc = prod if acc is None else acc + prod
        else:
            acc = lax.dot_general(a_ref[...].astype(BF16), b_ref[...].astype(BF16), dims, preferred_element_type=F32)
        tiles = (acc,) if epilogue is None else epilogue(acc, *[r[...] for r in extra_refs])
        for o_ref, t in zip(out_refs, tiles, strict=True):
            o_ref[...] = t.astype(o_ref.dtype)

    outs = pl.pallas_call(
        body,
        name=name,
        grid=(n // tn, m // tm, 1),
        in_specs=[a_spec, b_spec] + [tile_spec] * n_extra + [ANY] * n_after,
        out_specs=[out_spec] * n_out,
        out_shape=out_shape,
        compiler_params=_params(3),
    )(a, b, *extras, *([] if after is None else [after]))
    return outs[0] if n_out == 1 else outs


def _add_residual(acc, res):
    return (acc + res,)


def _matmul_column_blocks(a, b4, blocks, out, *, tm, name, after=None):
    m, kdim = a.shape
    nb, _, cols = b4.shape
    tm = min(tm, m)
    assert m % tm == 0

    def body(j_ref, a_ref, b_ref, *rest):
        rest[-1][...] = jnp.dot(a_ref[...], b_ref[...], preferred_element_type=F32)

    extra = ([] if out is None else [out]) + ([] if after is None else [after])
    n_blocks = blocks.shape[0]
    return pl.pallas_call(
        body, name=name,
        grid_spec=pltpu.PrefetchScalarGridSpec(
            num_scalar_prefetch=1, grid=(n_blocks, m // tm),
            in_specs=[pl.BlockSpec((tm, kdim), lambda j, i, blk: (i, 0)),
                      pl.BlockSpec((None, kdim, cols), lambda j, i, blk: (blk[j], 0, 0))] + [ANY] * len(extra),
            out_specs=pl.BlockSpec((tm, cols), lambda j, i, blk: (i, blk[j]))),
        out_shape=jax.ShapeDtypeStruct((m, nb * cols), F32),
        input_output_aliases={} if out is None else {3: 0},
        compiler_params=_params(2),
    )(blocks, a, b4, *extra)


def _wgrad_half(a, b, core, *, theirs, row_sharded, tm, tn, name, add=None, after=None):
    kdim, m = a.shape
    n = b.shape[1]
    rs, cs = (m // N_CHIPS, n) if row_sharded else (m, n // N_CHIPS)
    rh = rs // 2
    tm, tn = min(tm, rh), min(tn, cs)
    assert rh % tm == 0 and cs % tn == 0, (name, rh, cs, tm, tn)
    mh, per = rh // tm, cs // tn
    has_add = add is not None

    def half(c):
        return 1 - c[0] if theirs else c[0]

    if row_sharded:
        grid = (n // tn, N_CHIPS * mh)
        a_spec = pl.BlockSpec((kdim, tm), lambda j, r, c: (0, ((r // mh) * 2 + half(c)) * mh + r % mh))
        o_spec = pl.BlockSpec((None, tm, tn), lambda j, r, c: (r // mh, r % mh, j))
    else:
        grid = (n // tn, mh)
        a_spec = pl.BlockSpec((kdim, tm), lambda j, r, c: (0, half(c) * mh + r))
        o_spec = pl.BlockSpec((None, tm, tn), lambda j, r, c: (j // per, r, j % per))
    b_spec = pl.BlockSpec((kdim, tn), lambda j, r, c: (0, j))

    def body(c_ref, a_ref, b_ref, *rest):
        o_ref = rest[-1]
        acc = lax.dot_general(a_ref[...].astype(BF16), b_ref[...].astype(BF16), (((0,), (0,)), ((), ())),
                              preferred_element_type=F32)
        if has_add:
            acc = acc + rest[0][...].astype(F32)
        o_ref[...] = acc.astype(BF16)

    operands = [a, b] + ([add] if has_add else []) + ([] if after is None else [after])
    return pl.pallas_call(
        body, name=name,
        grid_spec=pltpu.PrefetchScalarGridSpec(
            num_scalar_prefetch=1, grid=grid,
            in_specs=[a_spec, b_spec] + ([o_spec] if has_add else []) + ([] if after is None else [ANY]),
            out_specs=o_spec),
        out_shape=jax.ShapeDtypeStruct((N_CHIPS, rh, cs), BF16),
        compiler_params=_params(2),
    )(core, *operands)


def _rstd(x):
    return lax.rsqrt(jnp.mean(x * x, axis=-1, keepdims=True) + RMS_EPS)


def _rmsnorm(x, g, name):
    s, d = x.shape
    tr = _row_tile(s, 256)

    def body(x_ref, g_ref, o_ref):
        xv = x_ref[...]
        o_ref[...] = (xv * _rstd(xv) * g_ref[...]).astype(BF16)

    return pl.pallas_call(
        body, name=name, grid=(s // tr,),
        in_specs=[pl.BlockSpec((tr, d), lambda i: (i, 0)), pl.BlockSpec((1, d), lambda i: (0, 0))],
        out_specs=pl.BlockSpec((tr, d), lambda i: (i, 0)),
        out_shape=jax.ShapeDtypeStruct((s, d), BF16),
        compiler_params=_params(1),
    )(x, g)


def _rmsnorm_bwd(dh, x, g, dres, name):
    s, d = x.shape
    tr = _row_tile(s, 256)
    has_res = dres is not None

    def body(*refs):
        if has_res:
            dh_ref, x_ref, g_ref, res_ref, dx_ref, dxb_ref, dg_ref = refs
        else:
            dh_ref, x_ref, g_ref, dx_ref, dxb_ref, dg_ref = refs
        xv = x_ref[...]
        dhv = dh_ref[...].astype(F32)
        r = _rstd(xv)
        xn = xv * r
        dhg = dhv * g_ref[...]
        dx = r * (dhg - xn * jnp.mean(dhg * xn, axis=-1, keepdims=True))
        if has_res:
            dx = dx + res_ref[...]
        dx_ref[...] = dx
        dxb_ref[...] = dx.astype(BF16)
        part = jnp.sum(dhv * xn, axis=0, keepdims=True)

        @pl.when(pl.program_id(0) == 0)
        def _():
            dg_ref[...] = part

        @pl.when(pl.program_id(0) > 0)
        def _():
            dg_ref[...] += part

    row = pl.BlockSpec((tr, d), lambda i: (i, 0))
    vec = pl.BlockSpec((1, d), lambda i: (0, 0))
    return pl.pallas_call(
        body, name=name, grid=(s // tr,),
        in_specs=[row, row, vec] + ([row] if has_res else []),
        out_specs=[row, row, vec],
        out_shape=[jax.ShapeDtypeStruct((s, d), F32), jax.ShapeDtypeStruct((s, d), BF16), jax.ShapeDtypeStruct((1, d), F32)],
        compiler_params=_params(1),
    )(*([dh, x, g] + ([dres] if has_res else [])))


def _loss_head(x3, g, target):
    s, d = x3.shape
    tr = _row_tile(s, 256)

    def body(x_ref, g_ref, t_ref, dx_ref, dxb_ref, sq_ref, dg_ref):
        xv = x_ref[...]
        gv = g_ref[...]
        r = _rstd(xv)
        xn = xv * r
        err = xn * gv - t_ref[...]
        dy = err * (1.0 / d)
        dyg = dy * gv
        dx = r * (dyg - xn * jnp.mean(dyg * xn, axis=-1, keepdims=True))
        dx_ref[...] = dx
        dxb_ref[...] = dx.astype(BF16)
        sq = jnp.sum(jnp.sum(err * err, axis=1, keepdims=True), axis=0, keepdims=True)
        sq = jnp.broadcast_to(sq, (1, 128))
        part = jnp.sum(dy * xn, axis=0, keepdims=True)

        @pl.when(pl.program_id(0) == 0)
        def _():
            sq_ref[...] = sq
            dg_ref[...] = part

        @pl.when(pl.program_id(0) > 0)
        def _():
            sq_ref[...] += sq
            dg_ref[...] += part

    row = pl.BlockSpec((tr, d), lambda i: (i, 0))
    vec = pl.BlockSpec((1, d), lambda i: (0, 0))
    return pl.pallas_call(
        body, name="loss_head", grid=(s // tr,),
        in_specs=[row, vec, row],
        out_specs=[row, row, pl.BlockSpec((1, 128), lambda i: (0, 0)), vec],
        out_shape=[jax.ShapeDtypeStruct((s, d), F32), jax.ShapeDtypeStruct((s, d), BF16),
                   jax.ShapeDtypeStruct((1, 128), F32), jax.ShapeDtypeStruct((1, d), F32)],
        compiler_params=_params(1),
    )(x3, g, target)


def _rope_tables(s):
    inv = 1.0 / (ROPE_THETA ** (jnp.arange(0, HEAD_DIM, 2, dtype=F32) / HEAD_DIM))
    ang = jnp.arange(s, dtype=F32)[:, None] * inv[None, :]
    cos, sin = jnp.cos(ang), jnp.sin(ang)
    return jnp.concatenate([cos, cos], axis=1), jnp.concatenate([-sin, sin], axis=1)


def _swap_halves(t):
    return pltpu.roll(t, HEAD_DIM // 2, 1)


def _rope_fwd(z, cos_t, sin_t):
    s = z.shape[0]
    tr = _row_tile(s, 256)

    def body(zq_ref, zk_ref, zv_ref, c_ref, s_ref, q_ref, k_ref, v_ref):
        c, sn = c_ref[...], s_ref[...]
        for hd in range(N_Q_HEADS):
            cols = slice(hd * HEAD_DIM, (hd + 1) * HEAD_DIM)
            t = zq_ref[:, cols]
            q_ref[:, cols] = (t * c + _swap_halves(t) * sn).astype(BF16)
        for hd in range(N_KV_HEADS):
            cols = slice(hd * HEAD_DIM, (hd + 1) * HEAD_DIM)
            t = zk_ref[:, cols]
            k_ref[:, cols] = (t * c + _swap_halves(t) * sn).astype(BF16)
        v_ref[...] = zv_ref[...].astype(BF16)

    tab = pl.BlockSpec((tr, HEAD_DIM), lambda i: (i, 0))
    return pl.pallas_call(
        body, name="rope_fwd", grid=(s // tr,),
        in_specs=[pl.BlockSpec((tr, ATTN_WIDTH), lambda i: (i, Q_OFF // ATTN_WIDTH)),
                  pl.BlockSpec((tr, KV_WIDTH), lambda i: (i, K_OFF // KV_WIDTH)),
                  pl.BlockSpec((tr, KV_WIDTH), lambda i: (i, V_OFF // KV_WIDTH)), tab, tab],
        out_specs=[pl.BlockSpec((tr, ATTN_WIDTH), lambda i: (i, 0)), pl.BlockSpec((tr, KV_WIDTH), lambda i: (i, 0)),
                   pl.BlockSpec((tr, KV_WIDTH), lambda i: (i, 0))],
        out_shape=[jax.ShapeDtypeStruct((s, ATTN_WIDTH), BF16), jax.ShapeDtypeStruct((s, KV_WIDTH), BF16),
                   jax.ShapeDtypeStruct((s, KV_WIDTH), BF16)],
        compiler_params=_params(1),
    )(z, z, z, cos_t, sin_t)


def _rope_bwd(dq_rot, dk_rot, dv, cos_t, sin_t):
    s = dq_rot.shape[0]
    tr = _row_tile(s, 256)

    def body(dq_ref, dk_ref, dv_ref, c_ref, s_ref, oq_ref, ok_ref, ov_ref):
        c, sn = c_ref[...], s_ref[...]
        for hd in range(N_Q_HEADS):
            cols = slice(hd * HEAD_DIM, (hd + 1) * HEAD_DIM)
            t = dq_ref[:, cols]
            oq_ref[:, cols] = (t * c + _swap_halves(t * sn)).astype(BF16)
        for hd in range(N_KV_HEADS):
            cols = slice(hd * HEAD_DIM, (hd + 1) * HEAD_DIM)
            t = dk_ref[:, cols]
            ok_ref[:, cols] = (t * c + _swap_halves(t * sn)).astype(BF16)
        ov_ref[...] = dv_ref[...].astype(BF16)

    tab = pl.BlockSpec((tr, HEAD_DIM), lambda i: (i, 0))
    wide = pl.BlockSpec((tr, ATTN_WIDTH), lambda i: (i, 0))
    narrow = pl.BlockSpec((tr, KV_WIDTH), lambda i: (i, 0))
    return pl.pallas_call(
        body, name="rope_bwd", grid=(s // tr,),
        in_specs=[wide, narrow, narrow, tab, tab],
        out_specs=[wide, narrow, narrow],
        out_shape=[jax.ShapeDtypeStruct((s, ATTN_WIDTH), BF16), jax.ShapeDtypeStruct((s, KV_WIDTH), BF16),
                   jax.ShapeDtypeStruct((s, KV_WIDTH), BF16)],
        compiler_params=_params(1),
    )(dq_rot, dk_rot, dv, cos_t, sin_t)


def _swa_band(i, s):
    return pl.multiple_of(jnp.clip((i - 1) * BLOCK, 0, s - BAND), BLOCK)


SWA_HEADS_PER_PASS = Q_GROUP


def _swa_probs(q_ref, k_ref, sink_ref, heads, start, valid):
    kv = heads[0] // Q_GROUP
    cols = slice(kv * HEAD_DIM, (kv + 1) * HEAD_DIM)
    kb = k_ref[pl.ds(start, BAND), cols]
    qg = jnp.concatenate([q_ref[:, hd * HEAD_DIM:(hd + 1) * HEAD_DIM] for hd in heads], axis=0)
    sc = lax.dot_general(qg, kb, (((1,), (1,)), ((), ())), preferred_element_type=F32) * ATTN_SCALE
    sc = jnp.where(valid, sc, NEG_INF)
    sk = jnp.concatenate([jnp.full((BLOCK, 1), sink_ref[hd], F32) for hd in heads], axis=0)
    mx = jnp.maximum(jnp.max(sc, axis=1, keepdims=True), sk)
    e = jnp.exp(sc - mx)
    es = jnp.exp(sk - mx)
    inv = 1.0 / (jnp.sum(e, axis=1, keepdims=True) + es)
    return qg, kb, e * inv, es * inv


def _swa_head_passes():
    return [list(range(h0, h0 + SWA_HEADS_PER_PASS)) for h0 in range(0, N_Q_HEADS, SWA_HEADS_PER_PASS)]


def _swa_valid(i, start):
    q_pos = i * BLOCK + lax.broadcasted_iota(jnp.int32, (BLOCK, 1), 0)
    q_pos = jnp.concatenate([q_pos] * SWA_HEADS_PER_PASS, axis=0)
    k_pos = start + lax.broadcasted_iota(jnp.int32, (1, BAND), 1)
    return jnp.abs(k_pos - q_pos) <= WINDOW


def _swa_fwd(q, k, v, sink):
    s = q.shape[0]
    assert s % BLOCK == 0 and s >= BAND

    def body(sink_ref, q_ref, k_ref, v_ref, o_ref):
        i = pl.program_id(0)
        start = _swa_band(i, s)
        valid = _swa_valid(i, start)
        for heads in _swa_head_passes():
            kv = heads[0] // Q_GROUP
            _, _, p, _ = _swa_probs(q_ref, k_ref, sink_ref, heads, start, valid)
            vb = v_ref[pl.ds(start, BAND), kv * HEAD_DIM:(kv + 1) * HEAD_DIM]
            o = jnp.dot(p.astype(BF16), vb, preferred_element_type=F32)
            for g, hd in enumerate(heads):
                o_ref[:, hd * HEAD_DIM:(hd + 1) * HEAD_DIM] = o[g * BLOCK:(g + 1) * BLOCK].astype(BF16)

    whole = pl.BlockSpec((s, KV_WIDTH), lambda i: (0, 0))
    blk = pl.BlockSpec((BLOCK, ATTN_WIDTH), lambda i: (i, 0))
    return pl.pallas_call(
        body, name="swa_fwd", grid=(s // BLOCK,),
        in_specs=[pl.BlockSpec(memory_space=pltpu.SMEM), blk, whole, whole],
        out_specs=blk,
        out_shape=jax.ShapeDtypeStruct((s, ATTN_WIDTH), BF16),
        compiler_params=_params(1),
    )(sink, q, k, v)


def _swa_bwd(q, k, v, d_out, sink):
    s = q.shape[0]

    def body(sink_ref, q_ref, k_ref, v_ref, do_ref, dq_ref, dk_ref, dv_ref, dsink_ref):
        i = pl.program_id(0)

        @pl.when(i == 0)
        def _():
            dk_ref[...] = jnp.zeros_like(dk_ref)
            dv_ref[...] = jnp.zeros_like(dv_ref)
            dsink_ref[...] = jnp.zeros_like(dsink_ref)

        start = _swa_band(i, s)
        valid = _swa_valid(i, start)
        for heads in _swa_head_passes():
            kv = heads[0] // Q_GROUP
            cols = slice(kv * HEAD_DIM, (kv + 1) * HEAD_DIM)
            qg, kb, p, p_sink = _swa_probs(q_ref, k_ref, sink_ref, heads, start, valid)
            vb = v_ref[pl.ds(start, BAND), cols]
            dog = jnp.concatenate([do_ref[:, hd * HEAD_DIM:(hd + 1) * HEAD_DIM] for hd in heads], axis=0)
            dp = lax.dot_general(dog, vb, (((1,), (1,)), ((), ())), preferred_element_type=F32)
            delta = jnp.sum(p * dp, axis=1, keepdims=True)
            ds = (p * (dp - delta) * ATTN_SCALE).astype(BF16)
            dqg = jnp.dot(ds, kb, preferred_element_type=F32)
            dk_ref[pl.ds(start, BAND), cols] += lax.dot_general(ds, qg, (((0,), (0,)), ((), ())), preferred_element_type=F32)
            dv_ref[pl.ds(start, BAND), cols] += lax.dot_general(p.astype(BF16), dog, (((0,), (0,)), ((), ())),
                                                                 preferred_element_type=F32)
            dsk = p_sink * delta
            for g, hd in enumerate(heads):
                dq_ref[:, hd * HEAD_DIM:(hd + 1) * HEAD_DIM] = dqg[g * BLOCK:(g + 1) * BLOCK]
                tot = jnp.sum(dsk[g * BLOCK:(g + 1) * BLOCK], axis=0, keepdims=True)
                dsink_ref[hd:hd + 1, :] -= jnp.broadcast_to(tot, (1, 128))

    whole = pl.BlockSpec((s, KV_WIDTH), lambda i: (0, 0))
    blk = pl.BlockSpec((BLOCK, ATTN_WIDTH), lambda i: (i, 0))
    return pl.pallas_call(
        body, name="swa_bwd", grid=(s // BLOCK,),
        in_specs=[pl.BlockSpec(memory_space=pltpu.SMEM), blk, whole, whole, blk],
        out_specs=[blk, whole, whole, pl.BlockSpec((N_Q_HEADS, 128), lambda i: (0, 0))],
        out_shape=[jax.ShapeDtypeStruct((s, ATTN_WIDTH), F32), jax.ShapeDtypeStruct((s, KV_WIDTH), F32),
                   jax.ShapeDtypeStruct((s, KV_WIDTH), F32), jax.ShapeDtypeStruct((N_Q_HEADS, 128), F32)],
        compiler_params=_params(1),
    )(sink, q, k, v, d_out)


CONV_CHUNK = 256


def _shift_rows(t, rows, down):
    n = t.shape[0]
    rolled = pltpu.roll(t, 1 if down else n - 1, 0)
    edge = 0 if down else n - 1
    return jnp.where(rows == edge, 0.0, rolled)


def _conv_specs(s):
    def z_spec(off):
        return pl.BlockSpec((s, CONV_CHUNK), lambda j, off=off: (0, off // CONV_CHUNK + j))
    chunk = pl.BlockSpec((s, CONV_CHUNK), lambda j: (0, j))
    w_spec = pl.BlockSpec((3, CONV_CHUNK), lambda j: (0, j))
    return z_spec(CU_OFF), z_spec(CB_OFF), z_spec(CC_OFF), chunk, w_spec


def _conv_fwd(z, conv_w):
    s = z.shape[0]
    cu_spec, cb_spec, cc_spec, chunk, w_spec = _conv_specs(s)

    def body(cu_ref, cb_ref, cc_ref, w_ref, o_ref):
        rows = lax.broadcasted_iota(jnp.int32, (s, 1), 0)
        t = cc_ref[...] * cu_ref[...]
        c3 = _shift_rows(t, rows, True) * w_ref[0:1, :] + t * w_ref[1:2, :] + _shift_rows(t, rows, False) * w_ref[2:3, :]
        o_ref[...] = (cb_ref[...] * c3).astype(BF16)

    return pl.pallas_call(
        body, name="conv_fwd", grid=(CONV_WIDTH // CONV_CHUNK,),
        in_specs=[cu_spec, cb_spec, cc_spec, w_spec],
        out_specs=chunk,
        out_shape=jax.ShapeDtypeStruct((s, CONV_WIDTH), BF16),
        compiler_params=_params(1),
    )(z, z, z, conv_w)


def _conv_bwd(z, conv_w, d_co):
    s = z.shape[0]
    cu_spec, cb_spec, cc_spec, chunk, w_spec = _conv_specs(s)

    def body(cu_ref, cb_ref, cc_ref, w_ref, d_ref, dcu_ref, dcb_ref, dcc_ref, dw_ref):
        rows = lax.broadcasted_iota(jnp.int32, (s, 1), 0)
        cu, cc = cu_ref[...], cc_ref[...]
        t = cc * cu
        t_dn, t_up = _shift_rows(t, rows, True), _shift_rows(t, rows, False)
        c3 = t_dn * w_ref[0:1, :] + t * w_ref[1:2, :] + t_up * w_ref[2:3, :]
        d = d_ref[...]
        dcb_ref[...] = (d * c3).astype(BF16)
        dc3 = d * cb_ref[...]
        dw_ref[0:1, :] = jnp.sum(dc3 * t_dn, axis=0, keepdims=True)
        dw_ref[1:2, :] = jnp.sum(dc3 * t, axis=0, keepdims=True)
        dw_ref[2:3, :] = jnp.sum(dc3 * t_up, axis=0, keepdims=True)
        dt = _shift_rows(dc3, rows, False) * w_ref[0:1, :] + dc3 * w_ref[1:2, :] + _shift_rows(dc3, rows, True) * w_ref[2:3, :]
        dcc_ref[...] = (dt * cu).astype(BF16)
        dcu_ref[...] = (dt * cc).astype(BF16)

    return pl.pallas_call(
        body, name="conv_bwd", grid=(CONV_WIDTH // CONV_CHUNK,),
        in_specs=[cu_spec, cb_spec, cc_spec, w_spec, chunk],
        out_specs=[chunk, chunk, chunk, w_spec],
        out_shape=[jax.ShapeDtypeStruct((s, CONV_WIDTH), BF16)] * 3 + [jax.ShapeDtypeStruct((3, CONV_WIDTH), F32)],
        compiler_params=_params(1),
    )(z, z, z, conv_w, d_co)


GATE_CHUNK = 512


def _gate_specs(s, d, tr):
    n_chunks = d // GATE_CHUNK
    za = pl.BlockSpec((tr, GATE_CHUNK), lambda j, i: (i, GL_OFF // GATE_CHUNK + j))
    zc = pl.BlockSpec((tr, GATE_CHUNK), lambda j, i: (i, GL_OFF // GATE_CHUNK + n_chunks + j))
    ba = pl.BlockSpec((1, GATE_CHUNK), lambda j, i: (0, j))
    bc = pl.BlockSpec((1, GATE_CHUNK), lambda j, i: (0, n_chunks + j))
    tile = pl.BlockSpec((tr, GATE_CHUNK), lambda j, i: (i, j))
    return za, zc, ba, bc, tile


def _gate_fwd(z, b_gate, ya, yc):
    s, d = ya.shape
    tr = _row_tile(s, 512)
    za, zc, ba, bc, tile = _gate_specs(s, d, tr)

    def body(za_ref, zc_ref, ba_ref, bc_ref, ya_ref, yc_ref, o_ref):
        ga = jax.nn.sigmoid(za_ref[...] + ba_ref[...])
        gc = jax.nn.sigmoid(zc_ref[...] + bc_ref[...])
        o_ref[...] = (ga * ya_ref[...] + gc * yc_ref[...]).astype(BF16)

    return pl.pallas_call(
        body, name="gate_fwd", grid=(d // GATE_CHUNK, s // tr),
        in_specs=[za, zc, ba, bc, tile, tile],
        out_specs=tile,
        out_shape=jax.ShapeDtypeStruct((s, d), BF16),
        compiler_params=_params(2),
    )(z, z, b_gate, b_gate, ya, yc)


def _gate_bwd(z, b_gate, ya, yc, dmix):
    s, d = ya.shape
    tr = _row_tile(s, 512)
    za, zc, ba, bc, tile = _gate_specs(s, d, tr)
    vec = pl.BlockSpec((1, GATE_CHUNK), lambda j, i: (0, j))

    def body(za_ref, zc_ref, ba_ref, bc_ref, ya_ref, yc_ref, dm_ref, dya_ref, dyc_ref, dla_ref, dlc_ref, dba_ref, dbc_ref):
        ga = jax.nn.sigmoid(za_ref[...] + ba_ref[...])
        gc = jax.nn.sigmoid(zc_ref[...] + bc_ref[...])
        dm = dm_ref[...]
        dya_ref[...] = (dm * ga).astype(BF16)
        dyc_ref[...] = (dm * gc).astype(BF16)
        dla = dm * ya_ref[...] * ga * (1.0 - ga)
        dlc = dm * yc_ref[...] * gc * (1.0 - gc)
        dla_ref[...] = dla.astype(BF16)
        dlc_ref[...] = dlc.astype(BF16)
        pa = jnp.sum(dla, axis=0, keepdims=True)
        pc = jnp.sum(dlc, axis=0, keepdims=True)

        @pl.when(pl.program_id(1) == 0)
        def _():
            dba_ref[...] = pa
            dbc_ref[...] = pc

        @pl.when(pl.program_id(1) > 0)
        def _():
            dba_ref[...] += pa
            dbc_ref[...] += pc

    big = jax.ShapeDtypeStruct((s, d), BF16)
    small = jax.ShapeDtypeStruct((1, d), F32)
    return pl.pallas_call(
        body, name="gate_bwd", grid=(d // GATE_CHUNK, s // tr),
        in_specs=[za, zc, ba, bc, tile, tile, tile],
        out_specs=[tile, tile, tile, tile, vec, vec],
        out_shape=[big, big, big, big, small, small],
        compiler_params=_params(2),
    )(z, z, b_gate, b_gate, ya, yc, dmix)


def _cross_probs(q_ref, kv_ref, hd):
    cols = slice(hd * HEAD_DIM, (hd + 1) * HEAD_DIM)
    qh = q_ref[:, cols]
    kh = kv_ref[:, cols]
    sc = lax.dot_general(qh, kh, (((1,), (1,)), ((), ())), preferred_element_type=F32) * ATTN_SCALE
    e = jnp.exp(sc - jnp.max(sc, axis=1, keepdims=True))
    return qh, kh, e * (1.0 / jnp.sum(e, axis=1, keepdims=True))


def _cross_fwd(qc, kvc):
    s = qc.shape[0]
    n_mem = kvc.shape[0]
    tq = _row_tile(s, 256)

    def body(q_ref, kv_ref, o_ref):
        for hd in range(MEM_HEADS):
            _, _, p = _cross_probs(q_ref, kv_ref, hd)
            vh = kv_ref[:, MEM_WIDTH + hd * HEAD_DIM:MEM_WIDTH + (hd + 1) * HEAD_DIM]
            o_ref[:, hd * HEAD_DIM:(hd + 1) * HEAD_DIM] = jnp.dot(p.astype(BF16), vh, preferred_element_type=F32).astype(BF16)

    return pl.pallas_call(
        body, name="cross_fwd", grid=(s // tq,),
        in_specs=[pl.BlockSpec((tq, MEM_WIDTH), lambda i: (i, 0)), pl.BlockSpec((n_mem, 2 * MEM_WIDTH), lambda i: (0, 0))],
        out_specs=pl.BlockSpec((tq, MEM_WIDTH), lambda i: (i, 0)),
        out_shape=jax.ShapeDtypeStruct((s, MEM_WIDTH), BF16),
        compiler_params=_params(1),
    )(qc, kvc)


def _cross_bwd(qc, kvc, d_out):
    s = qc.shape[0]
    n_mem = kvc.shape[0]
    tq = _row_tile(s, 256)

    def body(q_ref, kv_ref, do_ref, dq_ref, dkv_ref):
        @pl.when(pl.program_id(0) == 0)
        def _():
            dkv_ref[...] = jnp.zeros_like(dkv_ref)

        for hd in range(MEM_HEADS):
            cols = slice(hd * HEAD_DIM, (hd + 1) * HEAD_DIM)
            vcols = slice(MEM_WIDTH + hd * HEAD_DIM, MEM_WIDTH + (hd + 1) * HEAD_DIM)
            qh, kh, p = _cross_probs(q_ref, kv_ref, hd)
            doh = do_ref[:, cols]
            dp = lax.dot_general(doh, kv_ref[:, vcols], (((1,), (1,)), ((), ())), preferred_element_type=F32)
            ds = (p * (dp - jnp.sum(p * dp, axis=1, keepdims=True)) * ATTN_SCALE).astype(BF16)
            dq_ref[:, cols] = jnp.dot(ds, kh, preferred_element_type=F32).astype(BF16)
            dkv_ref[:, cols] += lax.dot_general(ds, qh, (((0,), (0,)), ((), ())), preferred_element_type=F32)
            dkv_ref[:, vcols] += lax.dot_general(p.astype(BF16), doh, (((0,), (0,)), ((), ())), preferred_element_type=F32)

    qspec = pl.BlockSpec((tq, MEM_WIDTH), lambda i: (i, 0))
    kvspec = pl.BlockSpec((n_mem, 2 * MEM_WIDTH), lambda i: (0, 0))
    return pl.pallas_call(
        body, name="cross_bwd", grid=(s // tq,),
        in_specs=[qspec, kvspec, qspec],
        out_specs=[qspec, kvspec],
        out_shape=[jax.ShapeDtypeStruct((s, MEM_WIDTH), BF16), jax.ShapeDtypeStruct((n_mem, 2 * MEM_WIDTH), F32)],
        compiler_params=_params(1),
    )(qc, kvc, d_out)


def _swiglu_fwd(up, gate):
    return up, (gate * jax.nn.sigmoid(gate)) * up


def _swiglu_bwd(d_act, gate, up):
    sg = jax.nn.sigmoid(gate)
    silu = gate * sg
    return d_act * up * (sg * (1.0 + gate * (1.0 - sg))), d_act * silu


GATHER_GROUPS = {"in": ("w_in", "conv_w"), "mid": ("w_attn_out", "w_conv_out", "w_o", "w_cq", "w_ckv", "w_co"),
                 "gate": ("w_gate",), "up": ("w_up",), "down": ("w_down",)}


def _local_step(xs, mems, target, small, fetch, reduce):
    s, d = xs.shape
    w4 = {}
    cos_t, sin_t = _rope_tables(s)

    def near(group, done, then, after):
        waits = [("direct", group)] + ([("pass_near", done), ("pass_far", done)] if done else [])
        starts = [("forward", group), ("pass_near", group)] + [("direct", g) for g in then]
        tok = fetch.step("gather_near_" + group, waits, starts, after)
        if done:
            w4.update(fetch.arrays(done))
        return tok

    def far(group, then, after):
        return fetch.step("gather_far_" + group, [("forward", group)], [("pass_far", group)] + [("direct", g) for g in then], after)

    def last(group, after):
        tok = fetch.step("gather_done_" + group, [("pass_near", group), ("pass_far", group)], [], after)
        w4.update(fetch.arrays(group))
        return tok

    h = _rmsnorm(xs, small["g_mix"], "norm_mix")
    slots_filled = [a for g in ("gate", "up", "down") for a in fetch.arrays(g).values()]
    chip_x, chip_y = reduce.place[0] // 2, reduce.place[0] % 2
    own_block = jnp.stack([2 * chip_x + chip_y]).astype(jnp.int32)
    near_blocks = jnp.stack([2 * (1 - chip_x) + chip_y, 2 * chip_x + (1 - chip_y)]).astype(jnp.int32)
    far_block = jnp.stack([2 * (1 - chip_x) + (1 - chip_y)]).astype(jnp.int32)
    z = _matmul_column_blocks(h, fetch.arrays("in")["w_in"], own_block, None, tm=512, name="in_proj_own")
    tok = near("in", None, ["mid"], [z] + slots_filled)
    tok = fetch.step("gather_near_done_in", [("pass_near", "in")], [], tok)
    z = _matmul_column_blocks(h, fetch.arrays("in")["w_in"], near_blocks, z, tm=512, name="in_proj_near", after=tok)
    tok = far("in", [], z)
    tok = fetch.step("gather_done_in", [("pass_far", "in")], [], tok)
    w4.update(fetch.arrays("in"))
    z = _matmul_column_blocks(h, w4["w_in"], far_block, z, tm=512, name="in_proj_far", after=tok)
    conv4 = w4["conv_w"]
    conv_w = conv4[:, :3, :].transpose(1, 0, 2).reshape(3, N_CHIPS * conv4.shape[2])
    c_in = w4["w_in"].shape[2]
    tok = near("mid", None, ["gate"], z)
    q_rot, k_rot, v_b = _rope_fwd(z, cos_t, sin_t)
    attn = _swa_fwd(q_rot, k_rot, v_b, small["sink"])
    co = _conv_fwd(z, conv_w)
    tok = far("mid", ["up"], attn)
    tok = last("mid", tok)
    w_o = w4["w_o"].reshape(-1, w4["w_o"].shape[-1])
    c_d = w4["w_attn_out"].shape[2]
    ya = _matmul(attn, w4["w_attn_out"], mode="nn", tm=1024, tn=c_d, tk=ATTN_WIDTH, out_dtypes=[F32], name="attn_out_proj",
                 b_blocks=N_CHIPS, after=tok)
    yc = _matmul(co, w4["w_conv_out"], mode="nn", tm=1024, tn=c_d, tk=CONV_WIDTH, out_dtypes=[F32], name="conv_out_proj",
                 b_blocks=N_CHIPS)
    mix = _gate_fwd(z, small["b_gate"], ya, yc)
    x1 = _matmul(mix, w_o, mode="nn", tm=512, tn=1024, tk=d, out_dtypes=[F32], name="mix_out_proj", extras=[xs],
                 epilogue=_add_residual)
    tok = near("gate", None, ["down"], x1)
    w_cq = w4["w_cq"].reshape(-1, w4["w_cq"].shape[-1])
    w_ckv = w4["w_ckv"].reshape(-1, w4["w_ckv"].shape[-1])
    hc = _rmsnorm(x1, small["g_cross"], "norm_cross")
    memn = _rmsnorm(mems, small["g_mem"], "norm_mem")
    qc = _matmul(hc, w_cq, mode="nn", tm=1024, tn=MEM_WIDTH, tk=d, out_dtypes=[BF16], name="cross_q_proj", after=tok)
    kvc = _matmul(memn, w_ckv, mode="nn", tm=256, tn=2 * MEM_WIDTH, tk=d, out_dtypes=[BF16], name="cross_kv_proj")
    oc = _cross_fwd(qc, kvc)
    tok = far("gate", [], oc)
    x2 = _matmul(oc, w4["w_co"], mode="nn", tm=1024, tn=c_d, tk=MEM_WIDTH, out_dtypes=[F32], name="cross_out_proj",
                 extras=[x1], epilogue=_add_residual, b_blocks=N_CHIPS, after=tok)
    hf = _rmsnorm(x2, small["g_ffn"], "norm_ffn")
    tok = near("up", "gate", [], hf)
    c_ff = w4["w_gate"].shape[2]
    gate = _matmul(hf, w4["w_gate"], mode="nn", tm=512, tn=c_ff, tk=d, out_dtypes=[F32], name="ffn_gate_proj", b_blocks=N_CHIPS,
                   after=tok)
    tok = far("up", [], gate)
    tok = near("down", "up", [], tok)
    up, act = _matmul(hf, w4["w_up"], mode="nn", tm=512, tn=c_ff, tk=d, out_dtypes=[F32, BF16], name="ffn_up_proj",
                      extras=[gate], epilogue=_swiglu_fwd, b_blocks=N_CHIPS, after=tok)
    tok = far("down", [], act)
    last("down", tok)
    w_down = w4["w_down"].reshape(-1, w4["w_down"].shape[-1])
    x3 = _matmul(act, w_down, mode="nn", tm=512, tn=512, tk=w_down.shape[0], out_dtypes=[F32], name="ffn_down_proj", extras=[x2],
                 epilogue=_add_residual)
    dx3, dx3b, sq, dg_final = _loss_head(x3, small["g_final"], target)

    da, du = _matmul(dx3b, w_down, mode="nt", tm=512, tn=c_ff, tk=d, out_dtypes=[BF16, BF16], name="ffn_down_bwd",
                     extras=[gate, up], epilogue=_swiglu_bwd)
    core = reduce.core
    ffn_shape = dict(row_sharded=False, tm=1024, tn=c_ff)
    g_down = _matmul(act, dx3b, mode="tn", tm=c_ff, tn=1024, tk=s, out_dtypes=[BF16], name="ffn_down_wgrad")
    tok = reduce.add("down", {"w_down": g_down}, da)
    t_gate = _wgrad_half(hf, da, core, theirs=True, name="ffn_gate_wgrad_theirs", after=tok, **ffn_shape)
    tok = reduce.step("down", t_gate)
    t_up = _wgrad_half(hf, du, core, theirs=True, name="ffn_up_wgrad_theirs", after=tok, **ffn_shape)
    tok = reduce.send("ffn", {"w_gate": t_gate, "w_up": t_up}, dx3b)
    dhf = _matmul(da, w4["w_gate"], mode="nt", tm=512, tn=1024, tk=N_CHIPS * c_ff, out_dtypes=[F32], name="ffn_gate_bwd", b_blocks=N_CHIPS,
                  after=tok)
    got = reduce.received("ffn", dhf)
    p_gate = _wgrad_half(hf, da, core, theirs=False, name="ffn_gate_wgrad_mine", add=got["w_gate"], **ffn_shape)
    p_up = _wgrad_half(hf, du, core, theirs=False, name="ffn_up_wgrad_mine", add=got["w_up"], **ffn_shape)
    tok = reduce.add_parts("ffn", {"w_gate": p_gate, "w_up": p_up})
    dhf = _matmul(du, w4["w_up"], mode="nt", tm=512, tn=1024, tk=N_CHIPS * c_ff, out_dtypes=[F32], name="ffn_up_bwd", extras=[dhf],
                  epilogue=_add_residual, b_blocks=N_CHIPS, after=tok)
    tok = reduce.step("down", dhf)
    dx2, dx2b, dg_ffn = _rmsnorm_bwd(dhf, x2, small["g_ffn"], dx3, "norm_ffn_bwd")

    d_oc = _matmul(dx2b, w4["w_co"], mode="nt", tm=1024, tn=MEM_WIDTH, tk=d, out_dtypes=[BF16], name="cross_out_bwd",
                   b_blocks=N_CHIPS, after=tok)
    g_co = _matmul(oc, dx2b, mode="tn", tm=MEM_WIDTH, tn=c_d, tk=s, out_dtypes=[BF16], name="cross_out_wgrad", out_blocks=N_CHIPS)
    tok = reduce.step("down", g_co)
    dqc, dkvc = _cross_bwd(qc, kvc, d_oc)
    g_cq = _matmul(hc, dqc, mode="tn", tm=1024, tn=MEM_WIDTH, tk=s, out_dtypes=[BF16], name="cross_q_wgrad", after=tok)
    dhc = _matmul(dqc, w_cq, mode="nt", tm=1024, tn=1024, tk=MEM_WIDTH, out_dtypes=[F32], name="cross_q_bwd")
    g_ckv = _matmul(memn, dkvc, mode="tn", tm=1024, tn=2 * MEM_WIDTH, tk=mems.shape[0], out_dtypes=[BF16], name="cross_kv_wgrad")
    dmemn = _matmul(dkvc, w_ckv, mode="nt", tm=256, tn=1024, tk=2 * MEM_WIDTH, out_dtypes=[F32], name="cross_kv_bwd")
    _, _, dg_mem = _rmsnorm_bwd(dmemn, mems, small["g_mem"], None, "norm_mem_bwd")
    dx1, dx1b, dg_cross = _rmsnorm_bwd(dhc, x1, small["g_cross"], dx2, "norm_cross_bwd")

    dmix = _matmul(dx1b, w_o, mode="nt", tm=512, tn=1024, tk=d, out_dtypes=[F32], name="mix_out_bwd")
    g_o = _matmul(mix, dx1b, mode="tn", tm=1024, tn=1024, tk=s, out_dtypes=[BF16], name="mix_out_wgrad")
    dya, dyc, dgl_a, dgl_c, db_a, db_c = _gate_bwd(z, small["b_gate"], ya, yc, dmix)
    d_attn = _matmul(dya, w4["w_attn_out"], mode="nt", tm=1024, tn=ATTN_WIDTH, tk=d, out_dtypes=[BF16], name="attn_out_bwd",
                     b_blocks=N_CHIPS)
    g_ao = _matmul(attn, dya, mode="tn", tm=ATTN_WIDTH, tn=c_d, tk=s, out_dtypes=[BF16], name="attn_out_wgrad", out_blocks=N_CHIPS)
    d_co = _matmul(dyc, w4["w_conv_out"], mode="nt", tm=1024, tn=CONV_WIDTH, tk=d, out_dtypes=[F32], name="conv_out_bwd",
                   b_blocks=N_CHIPS)
    g_cvo = _matmul(co, dyc, mode="tn", tm=CONV_WIDTH, tn=c_d, tk=s, out_dtypes=[BF16], name="conv_out_wgrad", out_blocks=N_CHIPS)
    tok = reduce.step("ffn", g_cvo)
    tok = reduce.add("mid", {"w_co": g_co, "w_cq": g_cq, "w_ckv": g_ckv, "w_o": g_o, "w_attn_out": g_ao, "w_conv_out": g_cvo}, tok)
    dcu, dcb, dcc, d_conv_w = _conv_bwd(z, conv_w, d_co)
    dq_rot, dk_rot, dv, dsink = _swa_bwd(q_rot, k_rot, v_b, d_attn, small["sink"])
    tok = reduce.step("mid", dq_rot)
    dq, dk, dvb = _rope_bwd(dq_rot, dk_rot, dv, cos_t, sin_t)
    dz = jnp.concatenate([dq, dk, dvb, dcu, dcb, dcc, dgl_a, dgl_c], axis=1)
    in_shape = dict(row_sharded=False, tm=1024, tn=c_in)
    t_in = _wgrad_half(h, dz, core, theirs=True, name="in_proj_wgrad_theirs", after=tok, **in_shape)
    tok = reduce.send("in", {"w_in": t_in}, dk)
    tok = reduce.step("ffn", tok)
    tok = reduce.step("mid", tok)
    got = reduce.received("in", tok)
    p_in = _wgrad_half(h, dz, core, theirs=False, name="in_proj_wgrad_mine", add=got["w_in"], **in_shape)
    tok = reduce.add_parts("in", {"w_in": p_in})
    dh = _matmul(dz, w4["w_in"], mode="nt", tm=512, tn=512, tk=N_CHIPS * c_in, out_dtypes=[F32], name="in_proj_bwd", b_blocks=N_CHIPS,
                 after=tok)
    tok = reduce.step("mid", dh)
    grad_x, _, dg_mix = _rmsnorm_bwd(dh, xs, small["g_mix"], dx1, "norm_mix_bwd")

    small_grads = {
        "g_mix": dg_mix, "sink": dsink[:, 0], "b_gate": jnp.concatenate([db_a, db_c], axis=1), "g_cross": dg_cross,
        "g_mem": dg_mem, "g_ffn": dg_ffn, "g_final": dg_final, "conv_w": d_conv_w,
    }
    return sq, grad_x, small_grads


def _pair_sum(g4, ra, core, name):
    nb, rs, cs = g4.shape
    rh = rs // 2
    tr = _row_tile(rh, 256)
    per = rh // tr

    def body(c_ref, g_ref, r_ref, o_ref):
        o_ref[...] = (g_ref[...].astype(F32) + r_ref[...].astype(F32)).astype(BF16)

    plain = pl.BlockSpec((None, tr, cs), lambda j, i, c: (j, i, 0))
    return pl.pallas_call(
        body, name=name,
        grid_spec=pltpu.PrefetchScalarGridSpec(
            num_scalar_prefetch=1, grid=(nb, per),
            in_specs=[pl.BlockSpec((None, tr, cs), lambda j, i, c: (j, c[0] * per + i, 0)), plain],
            out_specs=plain),
        out_shape=jax.ShapeDtypeStruct((nb, rh, cs), BF16),
        compiler_params=_params(2),
    )(core, g4, ra)


def _quad_sum(parts, rc, place, name):
    _, rh, cs = parts.shape
    tr = _row_tile(rh, 256)
    per = rh // tr

    def body(p_ref, own_ref, r_ref, o_ref):
        acc = own_ref[...].astype(F32)
        for j in range(rc.shape[0]):
            acc = acc + r_ref[j].astype(F32)
        o_ref[...] = acc

    return pl.pallas_call(
        body, name=name,
        grid_spec=pltpu.PrefetchScalarGridSpec(
            num_scalar_prefetch=1, grid=(per,),
            in_specs=[pl.BlockSpec((None, tr, cs), lambda i, p: (p[0], i, 0)),
                      pl.BlockSpec((rc.shape[0], tr, cs), lambda i, p: (0, i, 0))],
            out_specs=pl.BlockSpec((tr, cs), lambda i, p: (p[1] * per + i, 0))),
        out_shape=jax.ShapeDtypeStruct((2 * rh, cs), F32),
        compiler_params=_params(1),
    )(place, parts, rc)


def _adamw_update(w, g, m, v):
    nm = ADAM_B1 * m + (1.0 - ADAM_B1) * g
    nv = ADAM_B2 * v + (1.0 - ADAM_B2) * (g * g)
    m_hat = nm / ADAM_C1
    v_hat = nv / ADAM_C2
    return -ADAM_LR * (m_hat / (jnp.sqrt(v_hat) + ADAM_EPS) + ADAM_WD * w), nm, nv


def _adamw_own_half(w, m, v, parts, rc, place, name, after=None):
    rows, cols = w.shape
    rh = rows // 2
    tr = _row_tile(rh, 256)
    per = rh // tr

    def body(p_ref, w_ref, m_ref, v_ref, own_ref, r_ref, *rest):
        gx_ref, g_ref, d_ref, nm_ref, nv_ref = rest[-5:]
        g = own_ref[...].astype(F32)
        for j in range(rc.shape[0]):
            g = g + r_ref[j].astype(F32)
        gx_ref[...] = g
        g_ref[...] = g
        d_ref[...], nm_ref[...], nv_ref[...] = _adamw_update(w_ref[...], g, m_ref[...], v_ref[...])

    mine = pl.BlockSpec((tr, cols), lambda i, p: (p[1] * per + i, 0))
    shape = jax.ShapeDtypeStruct((rows, cols), F32)
    return pl.pallas_call(
        body, name=name,
        grid_spec=pltpu.PrefetchScalarGridSpec(
            num_scalar_prefetch=1, grid=(per,),
            in_specs=[mine, mine, mine, pl.BlockSpec((None, tr, cols), lambda i, p: (p[0], i, 0)),
                      pl.BlockSpec((rc.shape[0], tr, cols), lambda i, p: (0, i, 0))] + ([] if after is None else [ANY]),
            out_specs=[mine] * 5),
        out_shape=[shape] * 5,
        compiler_params=_params(1),
    )(place, w, m, v, parts, rc, *([] if after is None else [after]))


def _adamw_other_half(w, m, v, g_exchanged, g, delta, new_m, new_v, place, name, after=None):
    rows, cols = w.shape
    rh = rows // 2
    tr = _row_tile(rh, 256)
    per = rh // tr

    def body(p_ref, w_ref, m_ref, v_ref, gx_ref, *rest):
        g_ref, d_ref, nm_ref, nv_ref = rest[-4:]
        gv = gx_ref[...]
        g_ref[...] = gv
        d_ref[...], nm_ref[...], nv_ref[...] = _adamw_update(w_ref[...], gv, m_ref[...], v_ref[...])

    other = pl.BlockSpec((tr, cols), lambda i, p: ((1 - p[1]) * per + i, 0))
    shape = jax.ShapeDtypeStruct((rows, cols), F32)
    n_after = 0 if after is None else 1
    return pl.pallas_call(
        body, name=name,
        grid_spec=pltpu.PrefetchScalarGridSpec(
            num_scalar_prefetch=1, grid=(per,),
            in_specs=[other] * 4 + [ANY] * (4 + n_after),
            out_specs=[other] * 4),
        out_shape=[shape] * 4,
        input_output_aliases={5: 0, 6: 1, 7: 2, 8: 3},
        compiler_params=_params(1),
    )(place, w, m, v, g_exchanged, g, delta, new_m, new_v, *([] if after is None else [after]))


def _cast_to_slot(w, place, dtype, name, after=None):
    rows, cols = w.shape
    tr = _row_tile(rows, 256)

    def body(p_ref, w_ref, *rest):
        o_ref = rest[-1]
        o_ref[...] = w_ref[...].astype(dtype)

    return pl.pallas_call(
        body, name=name,
        grid_spec=pltpu.PrefetchScalarGridSpec(
            num_scalar_prefetch=1, grid=(rows // tr,),
            in_specs=[pl.BlockSpec((tr, cols), lambda i, p: (i, 0))] + ([] if after is None else [ANY]),
            out_specs=pl.BlockSpec((None, tr, cols), lambda i, p: (p[0], i, 0))),
        out_shape=jax.ShapeDtypeStruct((N_CHIPS, rows, cols), dtype),
        compiler_params=_params(1),
    )(place, w, *([] if after is None else [after]))


def _adamw(w, g, m, v, name, after=None):
    rows, cols = w.shape
    tr = _row_tile(rows, 256)

    def body(w_ref, g_ref, m_ref, v_ref, *rest):
        go_ref, d_ref, nm_ref, nv_ref = rest[-4:]
        gv = g_ref[...]
        go_ref[...] = gv
        d_ref[...], nm_ref[...], nv_ref[...] = _adamw_update(w_ref[...], gv, m_ref[...], v_ref[...])

    tile = pl.BlockSpec((tr, cols), lambda i: (i, 0))
    shape = jax.ShapeDtypeStruct((rows, cols), F32)
    return pl.pallas_call(
        body, name=name, grid=(rows // tr,),
        in_specs=[tile] * 4 + ([] if after is None else [ANY]), out_specs=[tile] * 4, out_shape=[shape] * 4,
        compiler_params=_params(1),
    )(w, g, m, v, *([] if after is None else [after]))


def _mesh_pos():
    return lax.axis_index("x"), lax.axis_index("y"), lax.axis_index("c")


def _other_chips(x, y):
    return [(1 - x, y), (x, 1 - y), (1 - x, 1 - y)]


def _half_rows(ref, which):
    rh = ref.shape[-2] // 2
    return ref.at[pl.ds(which * rh, rh), :]


def _remote(src, dst, send_sems, recv_sems, sem, to):
    return pltpu.make_async_remote_copy(src_ref=src, dst_ref=dst, send_sem=send_sems.at[sem], recv_sem=recv_sems.at[sem],
                                        device_id=to, device_id_type=MESH)


HBM = pl.BlockSpec(memory_space=pltpu.HBM)
SEM = pl.BlockSpec(memory_space=pltpu.SEMAPHORE)
DATAFLOW_EFFECT = pltpu.SideEffectType.DATAFLOW_SIDE_EFFECTING


def _in_hbm(arrays):
    return [pltpu.with_memory_space_constraint(a, pltpu.HBM) for a in arrays]


def _hbm_like(arrays):
    return [pltpu.HBM(a.shape, a.dtype) for a in arrays]


GATHER_COPIES_PER_ARRAY = {"direct": 2, "forward": 2, "pass_near": 2, "pass_far": 1}


def _gather_copies(kind, refs, x, y, c):
    me, near_x, near_y, far = 2 * x + y, 2 * (1 - x) + y, 2 * x + (1 - y), 2 * (1 - x) + (1 - y)
    to_x, to_y, sibling = (1 - x, y, c), (x, 1 - y, c), (x, y, 1 - c)
    out = []
    for ref in refs:
        rh = ref.shape[1] // 2
        rq = rh // 2

        def half(chip, ref=ref, rh=rh):
            return ref.at[chip, pl.ds(c * rh, rh), :]

        def quarter(chip, q, ref=ref, rh=rh, rq=rq):
            return ref.at[chip, pl.ds(c * rh + q * rq, rq), :]

        if kind == "direct":
            out += [(half(me), half(me), to_x), (half(me), half(me), to_y)]
        elif kind == "forward":
            out += [(quarter(near_x, 0), quarter(near_x, 0), to_y), (quarter(near_y, 1), quarter(near_y, 1), to_x)]
        elif kind == "pass_near":
            out += [(half(near_x), half(near_x), sibling), (half(near_y), half(near_y), sibling)]
        else:
            assert kind == "pass_far"
            out += [(half(far), half(far), sibling)]
    return out


def _gather_step(name, bufs, waits, starts, after):
    nb, nw, ns = len(bufs), len(waits), len(starts)
    after = [] if after is None else list(after) if isinstance(after, (list, tuple)) else [after]
    n_after = len(after)

    def body(*refs):
        ins = refs[:nb]
        wait_sems = refs[nb:nb + 2 * nw]
        start_sems = refs[nb + 2 * nw + n_after:nb + 2 * nw + n_after + 2 * ns]
        token = refs[-1]
        x, y, c = _mesh_pos()
        for j, (kind, idxs, _, _) in enumerate(waits):
            for i, (s_ref, d_ref, to) in enumerate(_gather_copies(kind, [ins[t] for t in idxs], x, y, c)):
                came = _remote(s_ref, d_ref, wait_sems[2 * j], wait_sems[2 * j + 1], i, to)
                came.wait_recv()
                came.wait_send()
        for j, (kind, idxs) in enumerate(starts):
            for i, (s_ref, d_ref, to) in enumerate(_gather_copies(kind, [ins[t] for t in idxs], x, y, c)):
                _remote(s_ref, d_ref, start_sems[2 * j], start_sems[2 * j + 1], i, to).start()
        token[...] = jnp.zeros_like(token)

    sems = []
    for kind, idxs in starts:
        sems += [pltpu.SemaphoreType.DMA((GATHER_COPIES_PER_ARRAY[kind] * len(idxs),))] * 2
    operands = _in_hbm(bufs) + [sem for w in waits for sem in w[2:]] + after
    outs = pl.pallas_call(
        body, name=name,
        in_specs=[HBM] * nb + [SEM] * (2 * nw) + [ANY] * n_after,
        out_specs=[SEM] * (2 * ns) + [HBM] * nb + [pl.BlockSpec(memory_space=pltpu.VMEM)],
        out_shape=sems + _hbm_like(bufs) + [jax.ShapeDtypeStruct((8, 128), F32)],
        input_output_aliases={i: 2 * ns + i for i in range(nb)},
        compiler_params=pltpu.CompilerParams(has_side_effects=DATAFLOW_EFFECT),
    )(*operands)
    return outs[2 * ns:2 * ns + nb], [(outs[2 * j], outs[2 * j + 1]) for j in range(ns)], outs[-1]


class _Gather:
    def __init__(self, groups):
        self.groups = groups
        self.bufs = {}
        self.in_flight = {}

    def put(self, slotted):
        self.bufs.update(slotted)

    def step(self, name, waits, starts, after=None):
        names = []
        for _, group in list(waits) + list(starts):
            names += [n for n in self.groups[group] if n not in names]
        index = {n: i for i, n in enumerate(names)}

        def members(group):
            return [index[n] for n in self.groups[group]]

        wait_args = [(kind, members(group)) + self.in_flight.pop((kind, group)) for kind, group in waits]
        start_args = [(kind, members(group)) for kind, group in starts]
        bufs, sems, token = _gather_step(name, [self.bufs[n] for n in names], wait_args, start_args, after)
        self.bufs.update(zip(names, bufs))
        for (kind, group), pair in zip(starts, sems):
            self.in_flight[(kind, group)] = pair
        return token

    def arrays(self, group):
        return {n: self.bufs[n] for n in self.groups[group]}


def _sibling_halves_copies(srcs, dsts, x, y, c):
    out = []
    for s_ref, d_ref in zip(srcs, dsts, strict=True):
        rh = s_ref.shape[1] // 2
        out.append((s_ref.at[:, pl.ds((1 - c) * rh, rh), :], d_ref, (x, y, 1 - c)))
    return out


def _to_sibling_copies(srcs, dsts, x, y, c):
    return [(s_ref, d_ref, (x, y, 1 - c)) for s_ref, d_ref in zip(srcs, dsts, strict=True)]


def _chip_copies(srcs, dsts, x, y, c):
    out = []
    for s_ref, d_ref in zip(srcs, dsts, strict=True):
        for k, (px, py) in enumerate(_other_chips(x, y)):
            out.append((s_ref.at[2 * px + py], d_ref.at[k], (px, py, c)))
    return out


def _join_copies(srcs, dsts, x, y, c):
    out = []
    for s_ref in srcs:
        mine = _half_rows(s_ref, c)
        out.append((mine, mine, (x, y, 1 - c)))
    return out


def _exchange_start(copies_fn, n_copies, srcs, fresh, after, name):
    ns, nb = len(srcs), len(srcs) + len(fresh)

    def body(*refs):
        bufs, send, recv, token = refs[:nb], refs[nb + 1], refs[nb + 2], refs[-1]
        x, y, c = _mesh_pos()
        for i, (s_ref, d_ref, to) in enumerate(copies_fn(bufs[:ns], bufs[ns:] if fresh else bufs[:ns], x, y, c)):
            _remote(s_ref, d_ref, send, recv, i, to).start()
        token[...] = jnp.zeros_like(token)

    sems = [pltpu.SemaphoreType.DMA((n_copies,))] * 2
    outs = pl.pallas_call(
        body, name=name,
        in_specs=[HBM] * nb + [ANY], out_specs=[SEM, SEM] + [HBM] * nb + [pl.BlockSpec(memory_space=pltpu.VMEM)],
        out_shape=sems + _hbm_like(list(srcs) + list(fresh)) + [jax.ShapeDtypeStruct((8, 128), F32)],
        input_output_aliases={i: 2 + i for i in range(nb)},
        compiler_params=pltpu.CompilerParams(has_side_effects=DATAFLOW_EFFECT),
    )(*_in_hbm(list(srcs) + list(fresh)), after)
    return outs[0], outs[1], outs[2:2 + ns], outs[2 + ns:2 + nb], outs[-1]


def _exchange_done(copies_fn, srcs, fresh, send, recv, after, name):
    ns, nb = len(srcs), len(srcs) + len(fresh)

    def body(*refs):
        bufs, send_in, recv_in = refs[:nb], refs[nb], refs[nb + 1]
        x, y, c = _mesh_pos()
        for i, (s_ref, d_ref, to) in enumerate(copies_fn(bufs[:ns], bufs[ns:] if fresh else bufs[:ns], x, y, c)):
            came = _remote(s_ref, d_ref, send_in, recv_in, i, to)
            came.wait_send()
            came.wait_recv()

    outs = pl.pallas_call(
        body, name=name,
        in_specs=[HBM] * nb + [SEM, SEM, ANY], out_specs=[HBM] * nb,
        out_shape=_hbm_like(list(srcs) + list(fresh)),
        input_output_aliases={i: i for i in range(nb)},
        compiler_params=pltpu.CompilerParams(has_side_effects=DATAFLOW_EFFECT),
    )(*_in_hbm(list(srcs) + list(fresh)), send, recv, after)
    return outs[:ns], outs[ns:]


class _Reduce:
    def __init__(self, place, core, shards, mom_m, mom_v):
        self.place, self.core = place, core
        self.shards, self.mom_m, self.mom_v = shards, mom_m, mom_v
        self.state = {}
        self.results = {}

    def add(self, group, grads, after):
        names = list(grads)
        g4s = [g.reshape((N_CHIPS, -1, g.shape[-1])) if g.ndim == 2 else g for g in grads.values()]
        fresh = [lax.empty((N_CHIPS, g.shape[1] // 2, g.shape[2]), BF16) for g in g4s]
        send, recv, g4s, fresh, token = _exchange_start(_sibling_halves_copies, len(names), g4s, fresh, after,
                                                        "pair_start_" + group)
        self.state[group] = (0, names, send, recv, g4s, fresh)
        return token

    def send(self, group, theirs, after):
        names, srcs = list(theirs), list(theirs.values())
        fresh = [lax.empty(s.shape, BF16) for s in srcs]
        send, recv, srcs, fresh, token = _exchange_start(_to_sibling_copies, len(names), srcs, fresh, after, "pair_start_" + group)
        self.state[group] = ("sent", names, send, recv, srcs, fresh)
        return token

    def received(self, group, after):
        stage, names, send, recv, srcs, fresh = self.state.pop(group)
        assert stage == "sent"
        _, got = _exchange_done(_to_sibling_copies, srcs, fresh, send, recv, after, "pair_done_" + group)
        return dict(zip(names, got))

    def add_parts(self, group, parts):
        names, srcs = list(parts), list(parts.values())
        fresh = [lax.empty((N_CHIPS - 1,) + p.shape[1:], BF16) for p in srcs]
        send, recv, srcs, fresh, token = _exchange_start(_chip_copies, 3 * len(names), srcs, fresh, self.core, "chips_start_" + group)
        self.state[group] = (1, names, send, recv, srcs, fresh)
        return token

    def step(self, group, after):
        stage, names, send, recv, srcs, fresh = self.state[group]
        if stage == 0:
            g4s, ras = _exchange_done(_sibling_halves_copies, srcs, fresh, send, recv, after, "pair_done_" + group)
            parts = [_pair_sum(g, r, self.core, "pair_sum_" + n) for g, r, n in zip(g4s, ras, names)]
            fresh = [lax.empty((N_CHIPS - 1,) + p.shape[1:], BF16) for p in parts]
            send, recv, parts, fresh, token = _exchange_start(_chip_copies, 3 * len(names), parts, fresh, self.core,
                                                              "chips_start_" + group)
            self.state[group] = (1, names, send, recv, parts, fresh)
            return token
        if stage == 1:
            parts, rcs = _exchange_done(_chip_copies, srcs, fresh, send, recv, after, "chips_done_" + group)
            token = None
            for n, p, r in zip(names, parts, rcs):
                self.results[n] = _adamw_own_half(self.shards[n], self.mom_m[n], self.mom_v[n], p, r, self.place,
                                                  "adamw_own_" + n, after=token)
                token = self.results[n][2]
            wholes = [self.results[n][0] for n in names]
            send, recv, wholes, _, token = _exchange_start(_join_copies, len(names), wholes, [], token, "join_start_" + group)
            self.state[group] = (2, names, send, recv, wholes, [])
            return token
        assert stage == 2
        wholes, _ = _exchange_done(_join_copies, srcs, [], send, recv, after, "join_done_" + group)
        token = None
        for n, exchanged in zip(names, wholes):
            _, g, d, nm, nv = self.results[n]
            self.results[n] = _adamw_other_half(self.shards[n], self.mom_m[n], self.mom_v[n], exchanged, g, d, nm, nv,
                                                self.place, "adamw_other_" + n, after=token)
            token = self.results[n][1]
        del self.state[group]
        return token


N_DEV = 8


def _all_reduce_small(v):
    def body(v_ref, o_ref, slots, send_sems, recv_sems):
        x, y, c = _mesh_pos()
        me = 4 * x + 2 * y + c
        slots[me] = v_ref[...]
        peers = []
        for r in range(1, N_DEV):
            fx, fy, fc = (r >> 2) & 1, (r >> 1) & 1, r & 1
            peers.append((x + fx - 2 * x * fx, y + fy - 2 * y * fy, c + fc - 2 * c * fc))
        sends = []
        for r, peer in enumerate(peers):
            cp = _remote(v_ref, slots.at[me], send_sems, recv_sems, r, peer)
            cp.start()
            sends.append(cp)
        for r, (px, py, pc) in enumerate(peers):
            landed = slots.at[4 * px + 2 * py + pc]
            _remote(landed, landed, send_sems, recv_sems, r, (px, py, pc)).wait_recv()
        for cp in sends:
            cp.wait_send()
        acc = slots[0]
        for i in range(1, N_DEV):
            acc = acc + slots[i]
        o_ref[...] = acc

    vm = pl.BlockSpec(memory_space=pltpu.VMEM)
    return pl.pallas_call(
        body, name="small_grads_all_reduce",
        in_specs=[vm], out_specs=vm,
        out_shape=jax.ShapeDtypeStruct(v.shape, v.dtype),
        scratch_shapes=[pltpu.VMEM((N_DEV,) + v.shape, v.dtype), pltpu.SemaphoreType.DMA((N_DEV - 1,)),
                        pltpu.SemaphoreType.DMA((N_DEV - 1,))],
    )(v)


MATRICES = ("w_in", "w_attn_out", "w_conv_out", "w_o", "w_cq", "w_ckv", "w_co", "w_gate", "w_up", "w_down")
VECTORS = ("g_mix", "b_gate", "g_cross", "g_mem", "g_ffn", "g_final", "conv_w", "sink")
WEIGHT_ORDER = ("g_mix", "w_in", "sink", "conv_w", "b_gate", "w_attn_out", "w_conv_out", "w_o", "g_cross", "g_mem", "w_cq",
                "w_ckv", "w_co", "g_ffn", "w_gate", "w_up", "w_down", "g_final")
CONV_PAD_ROWS = 32
SMALL_ROWS = 8


def _pack(pieces):
    flat = jnp.concatenate([p.reshape(-1) for p in pieces])
    lane_group = SMALL_ROWS * 128
    total = -(-flat.shape[0] // lane_group) * lane_group
    flat = jnp.pad(flat, (0, total - flat.shape[0]))
    return flat.reshape(SMALL_ROWS, total // SMALL_ROWS), [p.size for p in pieces]


def _unpack(packed, pieces):
    flat = packed.reshape(-1)
    out, off = [], 0
    for p in pieces:
        out.append(flat[off:off + p.size].reshape(p.shape))
        off += p.size
    return out


def kernel(x, mem, g_mix, w_in, sink, conv_w, b_gate, w_attn_out, w_conv_out, w_o, g_cross, g_mem, w_cq, w_ckv, w_co, g_ffn, w_gate, w_up, w_down, g_final, loss_target, m_g_mix, m_w_in, m_sink, m_conv_w, m_b_gate, m_w_attn_out, m_w_conv_out, m_w_o, m_g_cross, m_g_mem, m_w_cq, m_w_ckv, m_w_co, m_g_ffn, m_w_gate, m_w_up, m_w_down, m_g_final, v_g_mix, v_w_in, v_sink, v_conv_w, v_b_gate, v_w_attn_out, v_w_conv_out, v_w_o, v_g_cross, v_g_mem, v_w_cq, v_w_ckv, v_w_co, v_g_ffn, v_w_gate, v_w_up, v_w_down, v_g_final):
    given = dict(g_mix=g_mix, w_in=w_in, sink=sink, conv_w=conv_w, b_gate=b_gate, w_attn_out=w_attn_out, w_conv_out=w_conv_out,
                 w_o=w_o, g_cross=g_cross, g_mem=g_mem, w_cq=w_cq, w_ckv=w_ckv, w_co=w_co, g_ffn=g_ffn, w_gate=w_gate, w_up=w_up,
                 w_down=w_down, g_final=g_final)
    mom_m = dict(g_mix=m_g_mix, w_in=m_w_in, sink=m_sink, conv_w=m_conv_w, b_gate=m_b_gate, w_attn_out=m_w_attn_out,
                 w_conv_out=m_w_conv_out, w_o=m_w_o, g_cross=m_g_cross, g_mem=m_g_mem, w_cq=m_w_cq, w_ckv=m_w_ckv, w_co=m_w_co,
                 g_ffn=m_g_ffn, w_gate=m_w_gate, w_up=m_w_up, w_down=m_w_down, g_final=m_g_final)
    mom_v = dict(g_mix=v_g_mix, w_in=v_w_in, sink=v_sink, conv_w=v_conv_w, b_gate=v_b_gate, w_attn_out=v_w_attn_out,
                 w_conv_out=v_w_conv_out, w_o=v_w_o, g_cross=v_g_cross, g_mem=v_g_mem, w_cq=v_w_cq, w_ckv=v_w_ckv, w_co=v_w_co,
                 g_ffn=v_g_ffn, w_gate=v_w_gate, w_up=v_w_up, w_down=v_w_down, g_final=v_g_final)
    xs, mems, target = x[0], mem[0], loss_target[0]
    d_model = xs.shape[1]
    chip = 2 * lax.axis_index("x") + lax.axis_index("y")
    core = jnp.reshape(lax.axis_index("c"), (1,)).astype(jnp.int32)
    place = jnp.stack([chip, lax.axis_index("c")]).astype(jnp.int32)

    shards = {n: given[n][0] for n in MATRICES}
    conv_cols = conv_w.shape[2]
    conv_pad = jnp.pad(conv_w[0], ((0, CONV_PAD_ROWS - conv_w.shape[1]), (0, 0)))
    fetch = _Gather(GATHER_GROUPS)
    first = {"w_in": _cast_to_slot(shards["w_in"], place, BF16, "to_slot_w_in"),
             "conv_w": _cast_to_slot(conv_pad, place, F32, "to_slot_conv_w")}
    fetch.put(first)
    tok = fetch.step("gather_start", [], [("direct", "in")])
    fetch.put({n: _cast_to_slot(shards[n], place, BF16, "to_slot_" + n, after=tok) for n in MATRICES if n != "w_in"})
    small = {n: given[n] for n in ("g_mix", "b_gate", "g_cross", "g_mem", "g_ffn")}
    small["g_final"] = g_final[None]
    small["sink"] = sink[0]

    reduce = _Reduce(place, core, shards, {n: mom_m[n][0] for n in MATRICES}, {n: mom_v[n][0] for n in MATRICES})
    sq, grad_x, small_grads = _local_step(xs, mems, target, small, fetch, reduce)

    loss_part = 0.5 * sq[0:1, 0:1] / d_model
    pieces = [small_grads[n] for n in VECTORS] + [loss_part]
    packed, _ = _pack(pieces)
    summed = _unpack(_all_reduce_small(packed), pieces)
    loss = summed[-1][0, 0]
    small_sum = dict(zip(VECTORS, summed[:-1]))
    small_sum["conv_w"] = lax.dynamic_slice_in_dim(small_sum["conv_w"], chip * conv_cols, conv_cols, axis=1)

    grad_out, delta, new_m, new_v = {}, {}, {}, {}
    like = [given[n] for n in VECTORS]
    pw, _ = _pack(like)
    pg, _ = _pack([small_sum[n] for n in VECTORS])
    pm, _ = _pack([mom_m[n] for n in VECTORS])
    pv, _ = _pack([mom_v[n] for n in VECTORS])
    tok = reduce.step("in", pg)
    _, pd, pnm, pnv = _adamw(pw, pg, pm, pv, "adamw_small", after=tok)
    for n, g, d, nm, nv in zip(VECTORS, [small_sum[n] for n in VECTORS], _unpack(pd, like), _unpack(pnm, like), _unpack(pnv, like)):
        grad_out[n] = g.reshape(given[n].shape)
        delta[n], new_m[n], new_v[n] = d, nm, nv
    reduce.step("in", pd)
    for n in MATRICES:
        g, d, nm, nv = reduce.results[n]
        grad_out[n], delta[n], new_m[n], new_v[n] = g[None], d[None], nm[None], nv[None]

    return (loss, grad_x[None], *[grad_out[n] for n in WEIGHT_ORDER], *[delta[n] for n in WEIGHT_ORDER],
            *[new_m[n] for n in WEIGHT_ORDER], *[new_v[n] for n in WEIGHT_ORDER])
```

```python
import functools

import jax
import jax.numpy as jnp
from jax import lax
from jax.experimental import pallas as pl
from jax.experimental.pallas import tpu as pltpu

F32 = jnp.float32
BF16 = jnp.bfloat16
MESH = pl.DeviceIdType.MESH
ANY = pl.BlockSpec(memory_space=pl.ANY)

VMEM_LIMIT_BYTES = 56 * 1024 * 1024

N_CHIPS = 4
HEAD_DIM = 128
N_Q_HEADS = 8
N_KV_HEADS = 2
Q_GROUP = N_Q_HEADS // N_KV_HEADS
ATTN_WIDTH = N_Q_HEADS * HEAD_DIM
KV_WIDTH = N_KV_HEADS * HEAD_DIM
WINDOW = 128
BLOCK = 128
BAND = 3 * BLOCK
ROPE_THETA = 10000.0
CONV_WIDTH = 1024
MEM_HEADS = 4
MEM_WIDTH = MEM_HEADS * HEAD_DIM
RMS_EPS = 1e-6
NEG_INF = -1e30
ATTN_SCALE = HEAD_DIM ** -0.5

Q_OFF, K_OFF, V_OFF, CU_OFF, CB_OFF, CC_OFF, GL_OFF = 0, 1024, 1280, 1536, 2560, 3584, 4608

ADAM_LR = 0.001
ADAM_B1 = 0.9
ADAM_B2 = 0.999
ADAM_EPS = 1e-08
ADAM_WD = 0.01
ADAM_STEP = 10
ADAM_C1 = 1.0 - ADAM_B1 ** ADAM_STEP
ADAM_C2 = 1.0 - ADAM_B2 ** ADAM_STEP


def _params(n_grid_axes):
    return pltpu.CompilerParams(dimension_semantics=("arbitrary",) * n_grid_axes, vmem_limit_bytes=VMEM_LIMIT_BYTES)


BF16_SUBLANES = 16


def _row_tile(rows, want):
    if rows <= want:
        return rows
    for t in range(want, 0, -BF16_SUBLANES):
        if rows % t == 0:
            return t
    return rows


def _matmul(a, b, *, mode, tm, tn, tk, out_dtypes, name, extras=(), epilogue=None, b_blocks=1, out_blocks=1, after=None):
    if mode == "tn":
        kdim, m = a.shape
    else:
        m, kdim = a.shape
    if b_blocks > 1:
        nb, brows, bcols = b.shape
        assert nb == b_blocks
        if mode == "nn":
            n = bcols * nb
            assert brows == kdim
        else:
            assert mode == "nt" and bcols * nb == kdim
            n = brows
    else:
        n = b.shape[0] if mode == "nt" else b.shape[1]
    tm, tn = min(tm, m), min(tn, n)
    assert m % tm == 0 and n % tn == 0 and tk == kdim, (name, m, n, kdim, tm, tn, tk)
    n_extra, n_out = len(extras), len(out_dtypes)
    n_after = 0 if after is None else 1

    if mode == "tn":
        a_spec = pl.BlockSpec((tk, tm), lambda j, i, k: (k, i))
        dims = (((0,), (0,)), ((), ()))
    else:
        a_spec = pl.BlockSpec((tm, tk), lambda j, i, k: (i, k))
        dims = (((1,), (0,)), ((), ())) if mode == "nn" else (((1,), (1,)), ((), ()))

    if b_blocks > 1 and mode == "nn":
        per = b.shape[2] // tn
        assert b.shape[2] % tn == 0
        b_spec = pl.BlockSpec((None, tk, tn), lambda j, i, k: (j // per, k, j % per))
    elif b_blocks > 1:
        b_spec = pl.BlockSpec((b_blocks, tn, b.shape[2]), lambda j, i, k: (0, j, 0))
    elif mode == "nt":
        b_spec = pl.BlockSpec((tn, tk), lambda j, i, k: (j, k))
    else:
        b_spec = pl.BlockSpec((tk, tn), lambda j, i, k: (k, j))

    tile_spec = pl.BlockSpec((tm, tn), lambda j, i, k: (i, j))
    if out_blocks > 1:
        ncols = n // out_blocks
        assert ncols % tn == 0
        oper = ncols // tn
        out_spec = pl.BlockSpec((None, tm, tn), lambda j, i, k: (j // oper, i, j % oper))
        out_shape = [jax.ShapeDtypeStruct((out_blocks, m, ncols), dt) for dt in out_dtypes]
    else:
        out_spec = tile_spec
        out_shape = [jax.ShapeDtypeStruct((m, n), dt) for dt in out_dtypes]

    def body(a_ref, b_ref, *rest):
        extra_refs = rest[:n_extra]
        out_refs = rest[n_extra + n_after:n_extra + n_after + n_out]
        if mode == "nt" and b_blocks > 1:
            cs = b.shape[2]
            acc = None
            for jb in range(b_blocks):
                prod = lax.dot_general(a_ref[:, jb * cs:(jb + 1) * cs].astype(BF16), b_ref[jb].astype(BF16), dims,
                                       preferred_element_type=F32)
                acc = prod if acc is None else acc + prod
        else:
            acc = lax.dot_general(a_ref[...].astype(BF16), b_ref[...].astype(BF16), dims, preferred_element_type=F32)
        tiles = (acc,) if epilogue is None else epilogue(acc, *[r[...] for r in extra_refs])
        for o_ref, t in zip(out_refs, tiles, strict=True):
            o_ref[...] = t.astype(o_ref.dtype)

    outs = pl.pallas_call(
        body,
        name=name,
        grid=(n // tn, m // tm, 1),
        in_specs=[a_spec, b_spec] + [tile_spec] * n_extra + [ANY] * n_after,
        out_specs=[out_spec] * n_out,
        out_shape=out_shape,
        compiler_params=_params(3),
    )(a, b, *extras, *([] if after is None else [after]))
    return outs[0] if n_out == 1 else outs


def _add_residual(acc, res):
    return (acc + res,)


def _matmul_column_blocks(a, b4, blocks, out, *, tm, name, after=None):
    m, kdim = a.shape
    nb, _, cols = b4.shape
    tm = min(tm, m)
    assert m % tm == 0

    def body(j_ref, a_ref, b_ref, *rest):
        rest[-1][...] = jnp.dot(a_ref[...], b_ref[...], preferred_element_type=F32)

    extra = ([] if out is None else [out]) + ([] if after is None else [after])
    n_blocks = blocks.shape[0]
    return pl.pallas_call(
        body, name=name,
        grid_spec=pltpu.PrefetchScalarGridSpec(
            num_scalar_prefetch=1, grid=(n_blocks, m // tm),
            in_specs=[pl.BlockSpec((tm, kdim), lambda j, i, blk: (i, 0)),
                      pl.BlockSpec((None, kdim, cols), lambda j, i, blk: (blk[j], 0, 0))] + [ANY] * len(extra),
            out_specs=pl.BlockSpec((tm, cols), lambda j, i, blk: (i, blk[j]))),
        out_shape=jax.ShapeDtypeStruct((m, nb * cols), F32),
        input_output_aliases={} if out is None else {3: 0},
        compiler_params=_params(2),
    )(blocks, a, b4, *extra)


def _wgrad_half(a, b, core, *, theirs, row_sharded, tm, tn, name, add=None, after=None):
    kdim, m = a.shape
    n = b.shape[1]
    rs, cs = (m // N_CHIPS, n) if row_sharded else (m, n // N_CHIPS)
    rh = rs // 2
    tm, tn = min(tm, rh), min(tn, cs)
    assert rh % tm == 0 and cs % tn == 0, (name, rh, cs, tm, tn)
    mh, per = rh // tm, cs // tn
    has_add = add is not None

    def half(c):
        return 1 - c[0] if theirs else c[0]

    if row_sharded:
        grid = (n // tn, N_CHIPS * mh)
        a_spec = pl.BlockSpec((kdim, tm), lambda j, r, c: (0, ((r // mh) * 2 + half(c)) * mh + r % mh))
        o_spec = pl.BlockSpec((None, tm, tn), lambda j, r, c: (r // mh, r % mh, j))
    else:
        grid = (n // tn, mh)
        a_spec = pl.BlockSpec((kdim, tm), lambda j, r, c: (0, half(c) * mh + r))
        o_spec = pl.BlockSpec((None, tm, tn), lambda j, r, c: (j // per, r, j % per))
    b_spec = pl.BlockSpec((kdim, tn), lambda j, r, c: (0, j))

    def body(c_ref, a_ref, b_ref, *rest):
        o_ref = rest[-1]
        acc = lax.dot_general(a_ref[...].astype(BF16), b_ref[...].astype(BF16), (((0,), (0,)), ((), ())),
                              preferred_element_type=F32)
        if has_add:
            acc = acc + rest[0][...].astype(F32)
        o_ref[...] = acc.astype(BF16)

    operands = [a, b] + ([add] if has_add else []) + ([] if after is None else [after])
    return pl.pallas_call(
        body, name=name,
        grid_spec=pltpu.PrefetchScalarGridSpec(
            num_scalar_prefetch=1, grid=grid,
            in_specs=[a_spec, b_spec] + ([o_spec] if has_add else []) + ([] if after is None else [ANY]),
            out_specs=o_spec),
        out_shape=jax.ShapeDtypeStruct((N_CHIPS, rh, cs), BF16),
        compiler_params=_params(2),
    )(core, *operands)


def _rstd(x):
    return lax.rsqrt(jnp.mean(x * x, axis=-1, keepdims=True) + RMS_EPS)


def _rmsnorm(x, g, name):
    s, d = x.shape
    tr = _row_tile(s, 256)

    def body(x_ref, g_ref, o_ref):
        xv = x_ref[...]
        o_ref[...] = (xv * _rstd(xv) * g_ref[...]).astype(BF16)

    return pl.pallas_call(
        body, name=name, grid=(s // tr,),
        in_specs=[pl.BlockSpec((tr, d), lambda i: (i, 0)), pl.BlockSpec((1, d), lambda i: (0, 0))],
        out_specs=pl.BlockSpec((tr, d), lambda i: (i, 0)),
        out_shape=jax.ShapeDtypeStruct((s, d), BF16),
        compiler_params=_params(1),
    )(x, g)


def _rmsnorm_bwd(dh, x, g, dres, name):
    s, d = x.shape
    tr = _row_tile(s, 256)
    has_res = dres is not None

    def body(*refs):
        if has_res:
            dh_ref, x_ref, g_ref, res_ref, dx_ref, dxb_ref, dg_ref = refs
        else:
            dh_ref, x_ref, g_ref, dx_ref, dxb_ref, dg_ref = refs
        xv = x_ref[...]
        dhv = dh_ref[...].astype(F32)
        r = _rstd(xv)
        xn = xv * r
        dhg = dhv * g_ref[...]
        dx = r * (dhg - xn * jnp.mean(dhg * xn, axis=-1, keepdims=True))
        if has_res:
            dx = dx + res_ref[...]
        dx_ref[...] = dx
        dxb_ref[...] = dx.astype(BF16)
        part = jnp.sum(dhv * xn, axis=0, keepdims=True)

        @pl.when(pl.program_id(0) == 0)
        def _():
            dg_ref[...] = part

        @pl.when(pl.program_id(0) > 0)
        def _():
            dg_ref[...] += part

    row = pl.BlockSpec((tr, d), lambda i: (i, 0))
    vec = pl.BlockSpec((1, d), lambda i: (0, 0))
    return pl.pallas_call(
        body, name=name, grid=(s // tr,),
        in_specs=[row, row, vec] + ([row] if has_res else []),
        out_specs=[row, row, vec],
        out_shape=[jax.ShapeDtypeStruct((s, d), F32), jax.ShapeDtypeStruct((s, d), BF16), jax.ShapeDtypeStruct((1, d), F32)],
        compiler_params=_params(1),
    )(*([dh, x, g] + ([dres] if has_res else [])))


def _loss_head(x3, g, target):
    s, d = x3.shape
    tr = _row_tile(s, 256)

    def body(x_ref, g_ref, t_ref, dx_ref, dxb_ref, sq_ref, dg_ref):
        xv = x_ref[...]
        gv = g_ref[...]
        r = _rstd(xv)
        xn = xv * r
        err = xn * gv - t_ref[...]
        dy = err * (1.0 / d)
        dyg = dy * gv
        dx = r * (dyg - xn * jnp.mean(dyg * xn, axis=-1, keepdims=True))
        dx_ref[...] = dx
        dxb_ref[...] = dx.astype(BF16)
        sq = jnp.sum(jnp.sum(err * err, axis=1, keepdims=True), axis=0, keepdims=True)
        sq = jnp.broadcast_to(sq, (1, 128))
        part = jnp.sum(dy * xn, axis=0, keepdims=True)

        @pl.when(pl.program_id(0) == 0)
        def _():
            sq_ref[...] = sq
            dg_ref[...] = part

        @pl.when(pl.program_id(0) > 0)
        def _():
            sq_ref[...] += sq
            dg_ref[...] += part

    row = pl.BlockSpec((tr, d), lambda i: (i, 0))
    vec = pl.BlockSpec((1, d), lambda i: (0, 0))
    return pl.pallas_call(
        body, name="loss_head", grid=(s // tr,),
        in_specs=[row, vec, row],
        out_specs=[row, row, pl.BlockSpec((1, 128), lambda i: (0, 0)), vec],
        out_shape=[jax.ShapeDtypeStruct((s, d), F32), jax.ShapeDtypeStruct((s, d), BF16),
                   jax.ShapeDtypeStruct((1, 128), F32), jax.ShapeDtypeStruct((1, d), F32)],
        compiler_params=_params(1),
    )(x3, g, target)


def _rope_tables(s):
    inv = 1.0 / (ROPE_THETA ** (jnp.arange(0, HEAD_DIM, 2, dtype=F32) / HEAD_DIM))
    ang = jnp.arange(s, dtype=F32)[:, None] * inv[None, :]
    cos, sin = jnp.cos(ang), jnp.sin(ang)
    return jnp.concatenate([cos, cos], axis=1), jnp.concatenate([-sin, sin], axis=1)


def _swap_halves(t):
    return pltpu.roll(t, HEAD_DIM // 2, 1)


def _rope_fwd(z, cos_t, sin_t):
    s = z.shape[0]
    tr = _row_tile(s, 256)

    def body(zq_ref, zk_ref, zv_ref, c_ref, s_ref, q_ref, k_ref, v_ref):
        c, sn = c_ref[...], s_ref[...]
        for hd in range(N_Q_HEADS):
            cols = slice(hd * HEAD_DIM, (hd + 1) * HEAD_DIM)
            t = zq_ref[:, cols]
            q_ref[:, cols] = (t * c + _swap_halves(t) * sn).astype(BF16)
        for hd in range(N_KV_HEADS):
            cols = slice(hd * HEAD_DIM, (hd + 1) * HEAD_DIM)
            t = zk_ref[:, cols]
            k_ref[:, cols] = (t * c + _swap_halves(t) * sn).astype(BF16)
        v_ref[...] = zv_ref[...].astype(BF16)

    tab = pl.BlockSpec((tr, HEAD_DIM), lambda i: (i, 0))
    return pl.pallas_call(
        body, name="rope_fwd", grid=(s // tr,),
        in_specs=[pl.BlockSpec((tr, ATTN_WIDTH), lambda i: (i, Q_OFF // ATTN_WIDTH)),
                  pl.BlockSpec((tr, KV_WIDTH), lambda i: (i, K_OFF // KV_WIDTH)),
                  pl.BlockSpec((tr, KV_WIDTH), lambda i: (i, V_OFF // KV_WIDTH)), tab, tab],
        out_specs=[pl.BlockSpec((tr, ATTN_WIDTH), lambda i: (i, 0)), pl.BlockSpec((tr, KV_WIDTH), lambda i: (i, 0)),
                   pl.BlockSpec((tr, KV_WIDTH), lambda i: (i, 0))],
        out_shape=[jax.ShapeDtypeStruct((s, ATTN_WIDTH), BF16), jax.ShapeDtypeStruct((s, KV_WIDTH), BF16),
                   jax.ShapeDtypeStruct((s, KV_WIDTH), BF16)],
        compiler_params=_params(1),
    )(z, z, z, cos_t, sin_t)


def _rope_bwd(dq_rot, dk_rot, dv, cos_t, sin_t, dz):
    s = dq_rot.shape[0]
    tr = _row_tile(s, 256)
    qkv_width = V_OFF + KV_WIDTH

    def body(dq_ref, dk_ref, dv_ref, c_ref, s_ref, dz_in_ref, o_ref):
        c, sn = c_ref[...], s_ref[...]
        for hd in range(N_Q_HEADS):
            t = dq_ref[:, hd * HEAD_DIM:(hd + 1) * HEAD_DIM]
            o_ref[:, Q_OFF + hd * HEAD_DIM:Q_OFF + (hd + 1) * HEAD_DIM] = (t * c + _swap_halves(t * sn)).astype(BF16)
        for hd in range(N_KV_HEADS):
            t = dk_ref[:, hd * HEAD_DIM:(hd + 1) * HEAD_DIM]
            o_ref[:, K_OFF + hd * HEAD_DIM:K_OFF + (hd + 1) * HEAD_DIM] = (t * c + _swap_halves(t * sn)).astype(BF16)
        o_ref[:, V_OFF:V_OFF + KV_WIDTH] = dv_ref[...].astype(BF16)

    tab = pl.BlockSpec((tr, HEAD_DIM), lambda i: (i, 0))
    wide = pl.BlockSpec((tr, ATTN_WIDTH), lambda i: (i, 0))
    narrow = pl.BlockSpec((tr, KV_WIDTH), lambda i: (i, 0))
    return pl.pallas_call(
        body, name="rope_bwd", grid=(s // tr,),
        in_specs=[wide, narrow, narrow, tab, tab, ANY],
        out_specs=pl.BlockSpec((tr, qkv_width), lambda i: (i, 0)),
        out_shape=jax.ShapeDtypeStruct(dz.shape, dz.dtype),
        input_output_aliases={5: 0},
        compiler_params=_params(1),
    )(dq_rot, dk_rot, dv, cos_t, sin_t, dz)


def _swa_band(i, s):
    return pl.multiple_of(jnp.clip((i - 1) * BLOCK, 0, s - BAND), BLOCK)


SWA_HEADS_PER_PASS = Q_GROUP


def _swa_probs(q_ref, k_ref, sink_ref, heads, start, valid):
    kv = heads[0] // Q_GROUP
    cols = slice(kv * HEAD_DIM, (kv + 1) * HEAD_DIM)
    kb = k_ref[pl.ds(start, BAND), cols]
    qg = jnp.concatenate([q_ref[:, hd * HEAD_DIM:(hd + 1) * HEAD_DIM] for hd in heads], axis=0)
    sc = lax.dot_general(qg, kb, (((1,), (1,)), ((), ())), preferred_element_type=F32) * ATTN_SCALE
    sc = jnp.where(valid, sc, NEG_INF)
    sk = jnp.concatenate([jnp.full((BLOCK, 1), sink_ref[hd], F32) for hd in heads], axis=0)
    mx = jnp.maximum(jnp.max(sc, axis=1, keepdims=True), sk)
    e = jnp.exp(sc - mx)
    es = jnp.exp(sk - mx)
    inv = 1.0 / (jnp.sum(e, axis=1, keepdims=True) + es)
    return qg, kb, e * inv, es * inv


def _swa_head_passes():
    return [list(range(h0, h0 + SWA_HEADS_PER_PASS)) for h0 in range(0, N_Q_HEADS, SWA_HEADS_PER_PASS)]


def _swa_valid(i, start):
    q_pos = i * BLOCK + lax.broadcasted_iota(jnp.int32, (BLOCK, 1), 0)
    q_pos = jnp.concatenate([q_pos] * SWA_HEADS_PER_PASS, axis=0)
    k_pos = start + lax.broadcasted_iota(jnp.int32, (1, BAND), 1)
    return jnp.abs(k_pos - q_pos) <= WINDOW


def _swa_fwd(q, k, v, sink):
    s = q.shape[0]
    assert s % BLOCK == 0 and s >= BAND

    def body(sink_ref, q_ref, k_ref, v_ref, o_ref):
        i = pl.program_id(0)
        start = _swa_band(i, s)
        valid = _swa_valid(i, start)
        for heads in _swa_head_passes():
            kv = heads[0] // Q_GROUP
            _, _, p, _ = _swa_probs(q_ref, k_ref, sink_ref, heads, start, valid)
            vb = v_ref[pl.ds(start, BAND), kv * HEAD_DIM:(kv + 1) * HEAD_DIM]
            o = jnp.dot(p.astype(BF16), vb, preferred_element_type=F32)
            for g, hd in enumerate(heads):
                o_ref[:, hd * HEAD_DIM:(hd + 1) * HEAD_DIM] = o[g * BLOCK:(g + 1) * BLOCK].astype(BF16)

    whole = pl.BlockSpec((s, KV_WIDTH), lambda i: (0, 0))
    blk = pl.BlockSpec((BLOCK, ATTN_WIDTH), lambda i: (i, 0))
    return pl.pallas_call(
        body, name="swa_fwd", grid=(s // BLOCK,),
        in_specs=[pl.BlockSpec(memory_space=pltpu.SMEM), blk, whole, whole],
        out_specs=blk,
        out_shape=jax.ShapeDtypeStruct((s, ATTN_WIDTH), BF16),
        compiler_params=_params(1),
    )(sink, q, k, v)


def _swa_bwd(q, k, v, d_out, sink):
    s = q.shape[0]

    def body(sink_ref, q_ref, k_ref, v_ref, do_ref, dq_ref, dk_ref, dv_ref, dsink_ref):
        i = pl.program_id(0)

        @pl.when(i == 0)
        def _():
            dk_ref[...] = jnp.zeros_like(dk_ref)
            dv_ref[...] = jnp.zeros_like(dv_ref)
            dsink_ref[...] = jnp.zeros_like(dsink_ref)

        start = _swa_band(i, s)
        valid = _swa_valid(i, start)
        for heads in _swa_head_passes():
            kv = heads[0] // Q_GROUP
            cols = slice(kv * HEAD_DIM, (kv + 1) * HEAD_DIM)
            qg, kb, p, p_sink = _swa_probs(q_ref, k_ref, sink_ref, heads, start, valid)
            vb = v_ref[pl.ds(start, BAND), cols]
            dog = jnp.concatenate([do_ref[:, hd * HEAD_DIM:(hd + 1) * HEAD_DIM] for hd in heads], axis=0)
            dp = lax.dot_general(dog, vb, (((1,), (1,)), ((), ())), preferred_element_type=F32)
            delta = jnp.sum(p * dp, axis=1, keepdims=True)
            ds = (p * (dp - delta) * ATTN_SCALE).astype(BF16)
            dqg = jnp.dot(ds, kb, preferred_element_type=F32)
            dk_ref[pl.ds(start, BAND), cols] += lax.dot_general(ds, qg, (((0,), (0,)), ((), ())), preferred_element_type=F32)
            dv_ref[pl.ds(start, BAND), cols] += lax.dot_general(p.astype(BF16), dog, (((0,), (0,)), ((), ())),
                                                                 preferred_element_type=F32)
            dsk = p_sink * delta
            for g, hd in enumerate(heads):
                dq_ref[:, hd * HEAD_DIM:(hd + 1) * HEAD_DIM] = dqg[g * BLOCK:(g + 1) * BLOCK]
                tot = jnp.sum(dsk[g * BLOCK:(g + 1) * BLOCK], axis=0, keepdims=True)
                dsink_ref[hd:hd + 1, :] -= jnp.broadcast_to(tot, (1, 128))

    whole = pl.BlockSpec((s, KV_WIDTH), lambda i: (0, 0))
    blk = pl.BlockSpec((BLOCK, ATTN_WIDTH), lambda i: (i, 0))
    return pl.pallas_call(
        body, name="swa_bwd", grid=(s // BLOCK,),
        in_specs=[pl.BlockSpec(memory_space=pltpu.SMEM), blk, whole, whole, blk],
        out_specs=[blk, whole, whole, pl.BlockSpec((N_Q_HEADS, 128), lambda i: (0, 0))],
        out_shape=[jax.ShapeDtypeStruct((s, ATTN_WIDTH), F32), jax.ShapeDtypeStruct((s, KV_WIDTH), F32),
                   jax.ShapeDtypeStruct((s, KV_WIDTH), F32), jax.ShapeDtypeStruct((N_Q_HEADS, 128), F32)],
        compiler_params=_params(1),
    )(sink, q, k, v, d_out)


CONV_CHUNK = 256


def _shift_rows(t, rows, down):
    n = t.shape[0]
    rolled = pltpu.roll(t, 1 if down else n - 1, 0)
    edge = 0 if down else n - 1
    return jnp.where(rows == edge, 0.0, rolled)


def _conv_specs(s):
    def z_spec(off):
        return pl.BlockSpec((s, CONV_CHUNK), lambda j, off=off: (0, off // CONV_CHUNK + j))
    chunk = pl.BlockSpec((s, CONV_CHUNK), lambda j: (0, j))
    w_spec = pl.BlockSpec((3, CONV_CHUNK), lambda j: (0, j))
    return z_spec(CU_OFF), z_spec(CB_OFF), z_spec(CC_OFF), chunk, w_spec


def _conv_fwd(z, conv_w):
    s = z.shape[0]
    cu_spec, cb_spec, cc_spec, chunk, w_spec = _conv_specs(s)

    def body(cu_ref, cb_ref, cc_ref, w_ref, o_ref):
        rows = lax.broadcasted_iota(jnp.int32, (s, 1), 0)
        t = cc_ref[...] * cu_ref[...]
        c3 = _shift_rows(t, rows, True) * w_ref[0:1, :] + t * w_ref[1:2, :] + _shift_rows(t, rows, False) * w_ref[2:3, :]
        o_ref[...] = (cb_ref[...] * c3).astype(BF16)

    return pl.pallas_call(
        body, name="conv_fwd", grid=(CONV_WIDTH // CONV_CHUNK,),
        in_specs=[cu_spec, cb_spec, cc_spec, w_spec],
        out_specs=chunk,
        out_shape=jax.ShapeDtypeStruct((s, CONV_WIDTH), BF16),
        compiler_params=_params(1),
    )(z, z, z, conv_w)


def _conv_bwd(z, conv_w, d_co, dz):
    s = z.shape[0]
    cu_spec, cb_spec, cc_spec, chunk, w_spec = _conv_specs(s)
    n_chunks = CONV_WIDTH // CONV_CHUNK
    offsets = (CU_OFF, CB_OFF, CC_OFF)

    def body(cu_ref, cb_ref, cc_ref, w_ref, d_ref, dz_in_ref, dz_ref, dw_ref, buf, sems):
        j = pl.program_id(0)

        def copies(j_at):
            return [pltpu.make_async_copy(buf.at[h], dz_ref.at[:, pl.ds(off + j_at * CONV_CHUNK, CONV_CHUNK)], sems.at[h])
                    for h, off in enumerate(offsets)]

        rows = lax.broadcasted_iota(jnp.int32, (s, 1), 0)
        cu, cc = cu_ref[...], cc_ref[...]
        t = cc * cu
        t_dn, t_up = _shift_rows(t, rows, True), _shift_rows(t, rows, False)
        c3 = t_dn * w_ref[0:1, :] + t * w_ref[1:2, :] + t_up * w_ref[2:3, :]
        d = d_ref[...]
        dc3 = d * cb_ref[...]
        dw_ref[0:1, :] = jnp.sum(dc3 * t_dn, axis=0, keepdims=True)
        dw_ref[1:2, :] = jnp.sum(dc3 * t, axis=0, keepdims=True)
        dw_ref[2:3, :] = jnp.sum(dc3 * t_up, axis=0, keepdims=True)
        dt = _shift_rows(dc3, rows, False) * w_ref[0:1, :] + dc3 * w_ref[1:2, :] + _shift_rows(dc3, rows, True) * w_ref[2:3, :]

        @pl.when(j > 0)
        def _():
            for cp in copies(j):
                cp.wait()

        buf[0] = (dt * cc).astype(BF16)
        buf[1] = (d * c3).astype(BF16)
        buf[2] = (dt * cu).astype(BF16)
        for cp in copies(j):
            cp.start()

        @pl.when(j == n_chunks - 1)
        def _():
            for cp in copies(j):
                cp.wait()

    return pl.pallas_call(
        body, name="conv_bwd", grid=(n_chunks,),
        in_specs=[cu_spec, cb_spec, cc_spec, w_spec, chunk, ANY],
        out_specs=[ANY, w_spec],
        out_shape=[jax.ShapeDtypeStruct(dz.shape, dz.dtype), jax.ShapeDtypeStruct((3, CONV_WIDTH), F32)],
        input_output_aliases={5: 0},
        scratch_shapes=[pltpu.VMEM((3, s, CONV_CHUNK), BF16), pltpu.SemaphoreType.DMA((3,))],
        compiler_params=_params(1),
    )(z, z, z, conv_w, d_co, dz)


GATE_CHUNK = 512


def _gate_specs(s, d, tr):
    n_chunks = d // GATE_CHUNK
    za = pl.BlockSpec((tr, GATE_CHUNK), lambda j, i: (i, GL_OFF // GATE_CHUNK + j))
    zc = pl.BlockSpec((tr, GATE_CHUNK), lambda j, i: (i, GL_OFF // GATE_CHUNK + n_chunks + j))
    ba = pl.BlockSpec((1, GATE_CHUNK), lambda j, i: (0, j))
    bc = pl.BlockSpec((1, GATE_CHUNK), lambda j, i: (0, n_chunks + j))
    tile = pl.BlockSpec((tr, GATE_CHUNK), lambda j, i: (i, j))
    return za, zc, ba, bc, tile


def _gate_fwd(z, b_gate, ya, yc):
    s, d = ya.shape
    tr = _row_tile(s, 512)
    za, zc, ba, bc, tile = _gate_specs(s, d, tr)

    def body(za_ref, zc_ref, ba_ref, bc_ref, ya_ref, yc_ref, o_ref):
        ga = jax.nn.sigmoid(za_ref[...] + ba_ref[...])
        gc = jax.nn.sigmoid(zc_ref[...] + bc_ref[...])
        o_ref[...] = (ga * ya_ref[...] + gc * yc_ref[...]).astype(BF16)

    return pl.pallas_call(
        body, name="gate_fwd", grid=(d // GATE_CHUNK, s // tr),
        in_specs=[za, zc, ba, bc, tile, tile],
        out_specs=tile,
        out_shape=jax.ShapeDtypeStruct((s, d), BF16),
        compiler_params=_params(2),
    )(z, z, b_gate, b_gate, ya, yc)


def _gate_bwd(z, b_gate, ya, yc, dmix):
    s, d = ya.shape
    tr = _row_tile(s, 512)
    za, zc, ba, bc, tile = _gate_specs(s, d, tr)
    vec = pl.BlockSpec((1, GATE_CHUNK), lambda j, i: (0, j))
    n_rows = s // tr
    in_width = z.shape[1]

    def body(za_ref, zc_ref, ba_ref, bc_ref, ya_ref, yc_ref, dm_ref, dya_ref, dyc_ref, dz_ref, dba_ref, dbc_ref, buf, sems):
        j, i = pl.program_id(0), pl.program_id(1)

        def copies(j_at, i_at):
            rows = pl.ds(i_at * tr, tr)
            return [pltpu.make_async_copy(buf.at[h], dz_ref.at[rows, pl.ds(GL_OFF + h * d + j_at * GATE_CHUNK, GATE_CHUNK)],
                                          sems.at[h]) for h in range(2)]

        ga = jax.nn.sigmoid(za_ref[...] + ba_ref[...])
        gc = jax.nn.sigmoid(zc_ref[...] + bc_ref[...])
        dm = dm_ref[...]
        dya_ref[...] = (dm * ga).astype(BF16)
        dyc_ref[...] = (dm * gc).astype(BF16)
        dla = dm * ya_ref[...] * ga * (1.0 - ga)
        dlc = dm * yc_ref[...] * gc * (1.0 - gc)

        @pl.when(j * n_rows + i > 0)
        def _():
            for cp in copies(j, i):
                cp.wait()

        buf[0] = dla.astype(BF16)
        buf[1] = dlc.astype(BF16)
        for cp in copies(j, i):
            cp.start()

        @pl.when((j == d // GATE_CHUNK - 1) & (i == n_rows - 1))
        def _():
            for cp in copies(j, i):
                cp.wait()

        pa = jnp.sum(dla, axis=0, keepdims=True)
        pc = jnp.sum(dlc, axis=0, keepdims=True)

        @pl.when(i == 0)
        def _():
            dba_ref[...] = pa
            dbc_ref[...] = pc

        @pl.when(i > 0)
        def _():
            dba_ref[...] += pa
            dbc_ref[...] += pc

    big = jax.ShapeDtypeStruct((s, d), BF16)
    small = jax.ShapeDtypeStruct((1, d), F32)
    return pl.pallas_call(
        body, name="gate_bwd", grid=(d // GATE_CHUNK, n_rows),
        in_specs=[za, zc, ba, bc, tile, tile, tile],
        out_specs=[tile, tile, ANY, vec, vec],
        out_shape=[big, big, jax.ShapeDtypeStruct((s, in_width), BF16), small, small],
        scratch_shapes=[pltpu.VMEM((2, tr, GATE_CHUNK), BF16), pltpu.SemaphoreType.DMA((2,))],
        compiler_params=_params(2),
    )(z, z, b_gate, b_gate, ya, yc, dmix)


def _cross_probs(q_ref, kv_ref, hd):
    cols = slice(hd * HEAD_DIM, (hd + 1) * HEAD_DIM)
    qh = q_ref[:, cols]
    kh = kv_ref[:, cols]
    sc = lax.dot_general(qh, kh, (((1,), (1,)), ((), ())), preferred_element_type=F32) * ATTN_SCALE
    e = jnp.exp(sc - jnp.max(sc, axis=1, keepdims=True))
    return qh, kh, e * (1.0 / jnp.sum(e, axis=1, keepdims=True))


def _cross_fwd(qc, kvc):
    s = qc.shape[0]
    n_mem = kvc.shape[0]
    tq = _row_tile(s, 256)

    def body(q_ref, kv_ref, o_ref):
        for hd in range(MEM_HEADS):
            _, _, p = _cross_probs(q_ref, kv_ref, hd)
            vh = kv_ref[:, MEM_WIDTH + hd * HEAD_DIM:MEM_WIDTH + (hd + 1) * HEAD_DIM]
            o_ref[:, hd * HEAD_DIM:(hd + 1) * HEAD_DIM] = jnp.dot(p.astype(BF16), vh, preferred_element_type=F32).astype(BF16)

    return pl.pallas_call(
        body, name="cross_fwd", grid=(s // tq,),
        in_specs=[pl.BlockSpec((tq, MEM_WIDTH), lambda i: (i, 0)), pl.BlockSpec((n_mem, 2 * MEM_WIDTH), lambda i: (0, 0))],
        out_specs=pl.BlockSpec((tq, MEM_WIDTH), lambda i: (i, 0)),
        out_shape=jax.ShapeDtypeStruct((s, MEM_WIDTH), BF16),
        compiler_params=_params(1),
    )(qc, kvc)


def _cross_bwd(qc, kvc, d_out):
    s = qc.shape[0]
    n_mem = kvc.shape[0]
    tq = _row_tile(s, 256)

    def body(q_ref, kv_ref, do_ref, dq_ref, dkv_ref):
        @pl.when(pl.program_id(0) == 0)
        def _():
            dkv_ref[...] = jnp.zeros_like(dkv_ref)

        for hd in range(MEM_HEADS):
            cols = slice(hd * HEAD_DIM, (hd + 1) * HEAD_DIM)
            vcols = slice(MEM_WIDTH + hd * HEAD_DIM, MEM_WIDTH + (hd + 1) * HEAD_DIM)
            qh, kh, p = _cross_probs(q_ref, kv_ref, hd)
            doh = do_ref[:, cols]
            dp = lax.dot_general(doh, kv_ref[:, vcols], (((1,), (1,)), ((), ())), preferred_element_type=F32)
            ds = (p * (dp - jnp.sum(p * dp, axis=1, keepdims=True)) * ATTN_SCALE).astype(BF16)
            dq_ref[:, cols] = jnp.dot(ds, kh, preferred_element_type=F32).astype(BF16)
            dkv_ref[:, cols] += lax.dot_general(ds, qh, (((0,), (0,)), ((), ())), preferred_element_type=F32)
            dkv_ref[:, vcols] += lax.dot_general(p.astype(BF16), doh, (((0,), (0,)), ((), ())), preferred_element_type=F32)

    qspec = pl.BlockSpec((tq, MEM_WIDTH), lambda i: (i, 0))
    kvspec = pl.BlockSpec((n_mem, 2 * MEM_WIDTH), lambda i: (0, 0))
    return pl.pallas_call(
        body, name="cross_bwd", grid=(s // tq,),
        in_specs=[qspec, kvspec, qspec],
        out_specs=[qspec, kvspec],
        out_shape=[jax.ShapeDtypeStruct((s, MEM_WIDTH), BF16), jax.ShapeDtypeStruct((n_mem, 2 * MEM_WIDTH), F32)],
        compiler_params=_params(1),
    )(qc, kvc, d_out)


def _swiglu_fwd(up, gate):
    return up, (gate * jax.nn.sigmoid(gate)) * up


def _swiglu_bwd(d_act, gate, up):
    sg = jax.nn.sigmoid(gate)
    silu = gate * sg
    return d_act * up * (sg * (1.0 + gate * (1.0 - sg))), d_act * silu


GATHER_GROUPS = {"in": ("w_in", "conv_w"), "mid": ("w_attn_out", "w_conv_out", "w_o", "w_cq", "w_ckv", "w_co"),
                 "gate": ("w_gate",), "up": ("w_up",), "down": ("w_down",)}


def _local_step(xs, mems, target, small, fetch, reduce):
    s, d = xs.shape
    w4 = {}
    cos_t, sin_t = _rope_tables(s)

    def near(group, done, then, after):
        waits = [("direct", group)] + ([("pass_near", done), ("pass_far", done)] if done else [])
        starts = [("forward", group), ("pass_near", group)] + [("direct", g) for g in then]
        tok = fetch.step("gather_near_" + group, waits, starts, after)
        if done:
            w4.update(fetch.arrays(done))
        return tok

    def far(group, then, after):
        return fetch.step("gather_far_" + group, [("forward", group)], [("pass_far", group)] + [("direct", g) for g in then], after)

    def last(group, after):
        tok = fetch.step("gather_done_" + group, [("pass_near", group), ("pass_far", group)], [], after)
        w4.update(fetch.arrays(group))
        return tok

    h = _rmsnorm(xs, small["g_mix"], "norm_mix")
    slots_filled = [a for g in ("gate", "up", "down") for a in fetch.arrays(g).values()]
    chip_x, chip_y = reduce.place[0] // 2, reduce.place[0] % 2
    own_block = jnp.stack([2 * chip_x + chip_y]).astype(jnp.int32)
    near_blocks = jnp.stack([2 * (1 - chip_x) + chip_y, 2 * chip_x + (1 - chip_y)]).astype(jnp.int32)
    far_block = jnp.stack([2 * (1 - chip_x) + (1 - chip_y)]).astype(jnp.int32)
    z = _matmul_column_blocks(h, fetch.arrays("in")["w_in"], own_block, None, tm=512, name="in_proj_own")
    tok = near("in", None, ["mid"], [z] + slots_filled)
    tok = fetch.step("gather_near_done_in", [("pass_near", "in")], [], tok)
    z = _matmul_column_blocks(h, fetch.arrays("in")["w_in"], near_blocks, z, tm=512, name="in_proj_near", after=tok)
    tok = far("in", [], z)
    tok = fetch.step("gather_done_in", [("pass_far", "in")], [], tok)
    w4.update(fetch.arrays("in"))
    z = _matmul_column_blocks(h, w4["w_in"], far_block, z, tm=512, name="in_proj_far", after=tok)
    conv4 = w4["conv_w"]
    conv_w = conv4[:, :3, :].transpose(1, 0, 2).reshape(3, N_CHIPS * conv4.shape[2])
    c_in = w4["w_in"].shape[2]
    tok = near("mid", None, ["gate"], z)
    q_rot, k_rot, v_b = _rope_fwd(z, cos_t, sin_t)
    attn = _swa_fwd(q_rot, k_rot, v_b, small["sink"])
    co = _conv_fwd(z, conv_w)
    tok = far("mid", ["up"], attn)
    tok = last("mid", tok)
    w_o = w4["w_o"].reshape(-1, w4["w_o"].shape[-1])
    c_d = w4["w_attn_out"].shape[2]
    ya = _matmul(attn, w4["w_attn_out"], mode="nn", tm=1024, tn=c_d, tk=ATTN_WIDTH, out_dtypes=[F32], name="attn_out_proj",
                 b_blocks=N_CHIPS, after=tok)
    yc = _matmul(co, w4["w_conv_out"], mode="nn", tm=1024, tn=c_d, tk=CONV_WIDTH, out_dtypes=[F32], name="conv_out_proj",
                 b_blocks=N_CHIPS)
    mix = _gate_fwd(z, small["b_gate"], ya, yc)
    x1 = _matmul(mix, w_o, mode="nn", tm=512, tn=1024, tk=d, out_dtypes=[F32], name="mix_out_proj", extras=[xs],
                 epilogue=_add_residual)
    tok = near("gate", None, ["down"], x1)
    w_cq = w4["w_cq"].reshape(-1, w4["w_cq"].shape[-1])
    w_ckv = w4["w_ckv"].reshape(-1, w4["w_ckv"].shape[-1])
    hc = _rmsnorm(x1, small["g_cross"], "norm_cross")
    memn = _rmsnorm(mems, small["g_mem"], "norm_mem")
    qc = _matmul(hc, w_cq, mode="nn", tm=1024, tn=MEM_WIDTH, tk=d, out_dtypes=[BF16], name="cross_q_proj", after=tok)
    kvc = _matmul(memn, w_ckv, mode="nn", tm=256, tn=2 * MEM_WIDTH, tk=d, out_dtypes=[BF16], name="cross_kv_proj")
    oc = _cross_fwd(qc, kvc)
    tok = far("gate", [], oc)
    x2 = _matmul(oc, w4["w_co"], mode="nn", tm=1024, tn=c_d, tk=MEM_WIDTH, out_dtypes=[F32], name="cross_out_proj",
                 extras=[x1], epilogue=_add_residual, b_blocks=N_CHIPS, after=tok)
    hf = _rmsnorm(x2, small["g_ffn"], "norm_ffn")
    tok = near("up", "gate", [], hf)
    c_ff = w4["w_gate"].shape[2]
    gate = _matmul(hf, w4["w_gate"], mode="nn", tm=512, tn=c_ff, tk=d, out_dtypes=[F32], name="ffn_gate_proj", b_blocks=N_CHIPS,
                   after=tok)
    tok = far("up", [], gate)
    tok = near("down", "up", [], tok)
    up, act = _matmul(hf, w4["w_up"], mode="nn", tm=512, tn=c_ff, tk=d, out_dtypes=[F32, BF16], name="ffn_up_proj",
                      extras=[gate], epilogue=_swiglu_fwd, b_blocks=N_CHIPS, after=tok)
    tok = far("down", [], act)
    last("down", tok)
    w_down = w4["w_down"].reshape(-1, w4["w_down"].shape[-1])
    x3 = _matmul(act, w_down, mode="nn", tm=512, tn=512, tk=w_down.shape[0], out_dtypes=[F32], name="ffn_down_proj", extras=[x2],
                 epilogue=_add_residual)
    dx3, dx3b, sq, dg_final = _loss_head(x3, small["g_final"], target)

    da, du = _matmul(dx3b, w_down, mode="nt", tm=512, tn=c_ff, tk=d, out_dtypes=[BF16, BF16], name="ffn_down_bwd",
                     extras=[gate, up], epilogue=_swiglu_bwd)
    core = reduce.core
    ffn_shape = dict(row_sharded=False, tm=1024, tn=c_ff)
    g_down = _matmul(act, dx3b, mode="tn", tm=c_ff, tn=1024, tk=s, out_dtypes=[BF16], name="ffn_down_wgrad")
    tok = reduce.add("down", {"w_down": g_down}, da)
    t_gate = _wgrad_half(hf, da, core, theirs=True, name="ffn_gate_wgrad_theirs", after=tok, **ffn_shape)
    tok = reduce.step("down", t_gate)
    t_up = _wgrad_half(hf, du, core, theirs=True, name="ffn_up_wgrad_theirs", after=tok, **ffn_shape)
    tok = reduce.send("ffn", {"w_gate": t_gate, "w_up": t_up}, dx3b)
    dhf = _matmul(da, w4["w_gate"], mode="nt", tm=512, tn=1024, tk=N_CHIPS * c_ff, out_dtypes=[F32], name="ffn_gate_bwd", b_blocks=N_CHIPS,
                  after=tok)
    got = reduce.received("ffn", dhf)
    p_gate = _wgrad_half(hf, da, core, theirs=False, name="ffn_gate_wgrad_mine", add=got["w_gate"], **ffn_shape)
    p_up = _wgrad_half(hf, du, core, theirs=False, name="ffn_up_wgrad_mine", add=got["w_up"], **ffn_shape)
    tok = reduce.add_parts("ffn", {"w_gate": p_gate, "w_up": p_up})
    dhf = _matmul(du, w4["w_up"], mode="nt", tm=512, tn=1024, tk=N_CHIPS * c_ff, out_dtypes=[F32], name="ffn_up_bwd", extras=[dhf],
                  epilogue=_add_residual, b_blocks=N_CHIPS, after=tok)
    tok = reduce.step("down", dhf)
    dx2, dx2b, dg_ffn = _rmsnorm_bwd(dhf, x2, small["g_ffn"], dx3, "norm_ffn_bwd")

    d_oc = _matmul(dx2b, w4["w_co"], mode="nt", tm=1024, tn=MEM_WIDTH, tk=d, out_dtypes=[BF16], name="cross_out_bwd",
                   b_blocks=N_CHIPS, after=tok)
    g_co = _matmul(oc, dx2b, mode="tn", tm=MEM_WIDTH, tn=c_d, tk=s, out_dtypes=[BF16], name="cross_out_wgrad", out_blocks=N_CHIPS)
    tok = reduce.step("down", g_co)
    dqc, dkvc = _cross_bwd(qc, kvc, d_oc)
    g_cq = _matmul(hc, dqc, mode="tn", tm=1024, tn=MEM_WIDTH, tk=s, out_dtypes=[BF16], name="cross_q_wgrad", after=tok)
    dhc = _matmul(dqc, w_cq, mode="nt", tm=1024, tn=1024, tk=MEM_WIDTH, out_dtypes=[F32], name="cross_q_bwd")
    g_ckv = _matmul(memn, dkvc, mode="tn", tm=1024, tn=2 * MEM_WIDTH, tk=mems.shape[0], out_dtypes=[BF16], name="cross_kv_wgrad")
    dmemn = _matmul(dkvc, w_ckv, mode="nt", tm=256, tn=1024, tk=2 * MEM_WIDTH, out_dtypes=[F32], name="cross_kv_bwd")
    _, _, dg_mem = _rmsnorm_bwd(dmemn, mems, small["g_mem"], None, "norm_mem_bwd")
    dx1, dx1b, dg_cross = _rmsnorm_bwd(dhc, x1, small["g_cross"], dx2, "norm_cross_bwd")

    dmix = _matmul(dx1b, w_o, mode="nt", tm=512, tn=1024, tk=d, out_dtypes=[F32], name="mix_out_bwd")
    g_o = _matmul(mix, dx1b, mode="tn", tm=1024, tn=1024, tk=s, out_dtypes=[BF16], name="mix_out_wgrad")
    dya, dyc, dz, db_a, db_c = _gate_bwd(z, small["b_gate"], ya, yc, dmix)
    d_attn = _matmul(dya, w4["w_attn_out"], mode="nt", tm=1024, tn=ATTN_WIDTH, tk=d, out_dtypes=[BF16], name="attn_out_bwd",
                     b_blocks=N_CHIPS)
    g_ao = _matmul(attn, dya, mode="tn", tm=ATTN_WIDTH, tn=c_d, tk=s, out_dtypes=[BF16], name="attn_out_wgrad", out_blocks=N_CHIPS)
    d_co = _matmul(dyc, w4["w_conv_out"], mode="nt", tm=1024, tn=CONV_WIDTH, tk=d, out_dtypes=[F32], name="conv_out_bwd",
                   b_blocks=N_CHIPS)
    g_cvo = _matmul(co, dyc, mode="tn", tm=CONV_WIDTH, tn=c_d, tk=s, out_dtypes=[BF16], name="conv_out_wgrad", out_blocks=N_CHIPS)
    tok = reduce.step("ffn", g_cvo)
    tok = reduce.add("mid", {"w_co": g_co, "w_cq": g_cq, "w_ckv": g_ckv, "w_o": g_o, "w_attn_out": g_ao, "w_conv_out": g_cvo}, tok)
    dz, d_conv_w = _conv_bwd(z, conv_w, d_co, dz)
    dq_rot, dk_rot, dv, dsink = _swa_bwd(q_rot, k_rot, v_b, d_attn, small["sink"])
    tok = reduce.step("mid", dq_rot)
    dz = _rope_bwd(dq_rot, dk_rot, dv, cos_t, sin_t, dz)
    in_shape = dict(row_sharded=False, tm=1024, tn=c_in)
    t_in = _wgrad_half(h, dz, core, theirs=True, name="in_proj_wgrad_theirs", after=tok, **in_shape)
    tok = reduce.send("in", {"w_in": t_in}, dk_rot)
    tok = reduce.step("ffn", tok)
    tok = reduce.step("mid", tok)
    got = reduce.received("in", tok)
    p_in = _wgrad_half(h, dz, core, theirs=False, name="in_proj_wgrad_mine", add=got["w_in"], **in_shape)
    tok = reduce.add_parts("in", {"w_in": p_in})
    dh = _matmul(dz, w4["w_in"], mode="nt", tm=512, tn=512, tk=N_CHIPS * c_in, out_dtypes=[F32], name="in_proj_bwd", b_blocks=N_CHIPS,
                 after=tok)
    tok = reduce.step("mid", dh)
    grad_x, _, dg_mix = _rmsnorm_bwd(dh, xs, small["g_mix"], dx1, "norm_mix_bwd")

    small_grads = {
        "g_mix": dg_mix, "sink": dsink[:, 0], "b_gate": jnp.concatenate([db_a, db_c], axis=1), "g_cross": dg_cross,
        "g_mem": dg_mem, "g_ffn": dg_ffn, "g_final": dg_final, "conv_w": d_conv_w,
    }
    return sq, grad_x, small_grads


def _pair_sum(g4, ra, core, name):
    nb, rs, cs = g4.shape
    rh = rs // 2
    tr = _row_tile(rh, 256)
    per = rh // tr

    def body(c_ref, g_ref, r_ref, o_ref):
        o_ref[...] = (g_ref[...].astype(F32) + r_ref[...].astype(F32)).astype(BF16)

    plain = pl.BlockSpec((None, tr, cs), lambda j, i, c: (j, i, 0))
    return pl.pallas_call(
        body, name=name,
        grid_spec=pltpu.PrefetchScalarGridSpec(
            num_scalar_prefetch=1, grid=(nb, per),
            in_specs=[pl.BlockSpec((None, tr, cs), lambda j, i, c: (j, c[0] * per + i, 0)), plain],
            out_specs=plain),
        out_shape=jax.ShapeDtypeStruct((nb, rh, cs), BF16),
        compiler_params=_params(2),
    )(core, g4, ra)


def _quad_sum(parts, rc, place, name):
    _, rh, cs = parts.shape
    tr = _row_tile(rh, 256)
    per = rh // tr

    def body(p_ref, own_ref, r_ref, o_ref):
        acc = own_ref[...].astype(F32)
        for j in range(rc.shape[0]):
            acc = acc + r_ref[j].astype(F32)
        o_ref[...] = acc

    return pl.pallas_call(
        body, name=name,
        grid_spec=pltpu.PrefetchScalarGridSpec(
            num_scalar_prefetch=1, grid=(per,),
            in_specs=[pl.BlockSpec((None, tr, cs), lambda i, p: (p[0], i, 0)),
                      pl.BlockSpec((rc.shape[0], tr, cs), lambda i, p: (0, i, 0))],
            out_specs=pl.BlockSpec((tr, cs), lambda i, p: (p[1] * per + i, 0))),
        out_shape=jax.ShapeDtypeStruct((2 * rh, cs), F32),
        compiler_params=_params(1),
    )(place, parts, rc)


def _adamw_update(w, g, m, v):
    nm = ADAM_B1 * m + (1.0 - ADAM_B1) * g
    nv = ADAM_B2 * v + (1.0 - ADAM_B2) * (g * g)
    m_hat = nm / ADAM_C1
    v_hat = nv / ADAM_C2
    return -ADAM_LR * (m_hat / (jnp.sqrt(v_hat) + ADAM_EPS) + ADAM_WD * w), nm, nv


def _adamw_own_half(w, m, v, parts, rc, place, name, after=None):
    rows, cols = w.shape
    rh = rows // 2
    tr = _row_tile(rh, 256)
    per = rh // tr

    def body(p_ref, w_ref, m_ref, v_ref, own_ref, r_ref, *rest):
        gx_ref, g_ref, d_ref, nm_ref, nv_ref = rest[-5:]
        g = own_ref[...].astype(F32)
        for j in range(rc.shape[0]):
            g = g + r_ref[j].astype(F32)
        gx_ref[...] = g
        g_ref[...] = g
        d_ref[...], nm_ref[...], nv_ref[...] = _adamw_update(w_ref[...], g, m_ref[...], v_ref[...])

    mine = pl.BlockSpec((tr, cols), lambda i, p: (p[1] * per + i, 0))
    shape = jax.ShapeDtypeStruct((rows, cols), F32)
    return pl.pallas_call(
        body, name=name,
        grid_spec=pltpu.PrefetchScalarGridSpec(
            num_scalar_prefetch=1, grid=(per,),
            in_specs=[mine, mine, mine, pl.BlockSpec((None, tr, cols), lambda i, p: (p[0], i, 0)),
                      pl.BlockSpec((rc.shape[0], tr, cols), lambda i, p: (0, i, 0))] + ([] if after is None else [ANY]),
            out_specs=[mine] * 5),
        out_shape=[shape] * 5,
        compiler_params=_params(1),
    )(place, w, m, v, parts, rc, *([] if after is None else [after]))


def _adamw_other_half(w, m, v, g_exchanged, g, delta, new_m, new_v, place, name, after=None):
    rows, cols = w.shape
    rh = rows // 2
    tr = _row_tile(rh, 256)
    per = rh // tr

    def body(p_ref, w_ref, m_ref, v_ref, gx_ref, *rest):
        g_ref, d_ref, nm_ref, nv_ref = rest[-4:]
        gv = gx_ref[...]
        g_ref[...] = gv
        d_ref[...], nm_ref[...], nv_ref[...] = _adamw_update(w_ref[...], gv, m_ref[...], v_ref[...])

    other = pl.BlockSpec((tr, cols), lambda i, p: ((1 - p[1]) * per + i, 0))
    shape = jax.ShapeDtypeStruct((rows, cols), F32)
    n_after = 0 if after is None else 1
    return pl.pallas_call(
        body, name=name,
        grid_spec=pltpu.PrefetchScalarGridSpec(
            num_scalar_prefetch=1, grid=(per,),
            in_specs=[other] * 4 + [ANY] * (4 + n_after),
            out_specs=[other] * 4),
        out_shape=[shape] * 4,
        input_output_aliases={5: 0, 6: 1, 7: 2, 8: 3},
        compiler_params=_params(1),
    )(place, w, m, v, g_exchanged, g, delta, new_m, new_v, *([] if after is None else [after]))


def _cast_to_slot(w, place, dtype, name, after=None):
    rows, cols = w.shape
    tr = _row_tile(rows, 256)

    def body(p_ref, w_ref, *rest):
        o_ref = rest[-1]
        o_ref[...] = w_ref[...].astype(dtype)

    return pl.pallas_call(
        body, name=name,
        grid_spec=pltpu.PrefetchScalarGridSpec(
            num_scalar_prefetch=1, grid=(rows // tr,),
            in_specs=[pl.BlockSpec((tr, cols), lambda i, p: (i, 0))] + ([] if after is None else [ANY]),
            out_specs=pl.BlockSpec((None, tr, cols), lambda i, p: (p[0], i, 0))),
        out_shape=jax.ShapeDtypeStruct((N_CHIPS, rows, cols), dtype),
        compiler_params=_params(1),
    )(place, w, *([] if after is None else [after]))


def _adamw(w, g, m, v, name, after=None):
    rows, cols = w.shape
    tr = _row_tile(rows, 256)

    def body(w_ref, g_ref, m_ref, v_ref, *rest):
        go_ref, d_ref, nm_ref, nv_ref = rest[-4:]
        gv = g_ref[...]
        go_ref[...] = gv
        d_ref[...], nm_ref[...], nv_ref[...] = _adamw_update(w_ref[...], gv, m_ref[...], v_ref[...])

    tile = pl.BlockSpec((tr, cols), lambda i: (i, 0))
    shape = jax.ShapeDtypeStruct((rows, cols), F32)
    return pl.pallas_call(
        body, name=name, grid=(rows // tr,),
        in_specs=[tile] * 4 + ([] if after is None else [ANY]), out_specs=[tile] * 4, out_shape=[shape] * 4,
        compiler_params=_params(1),
    )(w, g, m, v, *([] if after is None else [after]))


def _mesh_pos():
    return lax.axis_index("x"), lax.axis_index("y"), lax.axis_index("c")


def _other_chips(x, y):
    return [(1 - x, y), (x, 1 - y), (1 - x, 1 - y)]


def _half_rows(ref, which):
    rh = ref.shape[-2] // 2
    return ref.at[pl.ds(which * rh, rh), :]


def _remote(src, dst, send_sems, recv_sems, sem, to):
    return pltpu.make_async_remote_copy(src_ref=src, dst_ref=dst, send_sem=send_sems.at[sem], recv_sem=recv_sems.at[sem],
                                        device_id=to, device_id_type=MESH)


HBM = pl.BlockSpec(memory_space=pltpu.HBM)
SEM = pl.BlockSpec(memory_space=pltpu.SEMAPHORE)
DATAFLOW_EFFECT = pltpu.SideEffectType.DATAFLOW_SIDE_EFFECTING


def _in_hbm(arrays):
    return [pltpu.with_memory_space_constraint(a, pltpu.HBM) for a in arrays]


def _hbm_like(arrays):
    return [pltpu.HBM(a.shape, a.dtype) for a in arrays]


GATHER_COPIES_PER_ARRAY = {"direct": 2, "forward": 2, "pass_near": 2, "pass_far": 1}


def _gather_copies(kind, refs, x, y, c):
    me, near_x, near_y, far = 2 * x + y, 2 * (1 - x) + y, 2 * x + (1 - y), 2 * (1 - x) + (1 - y)
    to_x, to_y, sibling = (1 - x, y, c), (x, 1 - y, c), (x, y, 1 - c)
    out = []
    for ref in refs:
        rh = ref.shape[1] // 2
        rq = rh // 2

        def half(chip, ref=ref, rh=rh):
            return ref.at[chip, pl.ds(c * rh, rh), :]

        def quarter(chip, q, ref=ref, rh=rh, rq=rq):
            return ref.at[chip, pl.ds(c * rh + q * rq, rq), :]

        if kind == "direct":
            out += [(half(me), half(me), to_x), (half(me), half(me), to_y)]
        elif kind == "forward":
            out += [(quarter(near_x, 0), quarter(near_x, 0), to_y), (quarter(near_y, 1), quarter(near_y, 1), to_x)]
        elif kind == "pass_near":
            out += [(half(near_x), half(near_x), sibling), (half(near_y), half(near_y), sibling)]
        else:
            assert kind == "pass_far"
            out += [(half(far), half(far), sibling)]
    return out


def _gather_step(name, bufs, waits, starts, after):
    nb, nw, ns = len(bufs), len(waits), len(starts)
    after = [] if after is None else list(after) if isinstance(after, (list, tuple)) else [after]
    n_after = len(after)

    def body(*refs):
        ins = refs[:nb]
        wait_sems = refs[nb:nb + 2 * nw]
        start_sems = refs[nb + 2 * nw + n_after:nb + 2 * nw + n_after + 2 * ns]
        token = refs[-1]
        x, y, c = _mesh_pos()
        for j, (kind, idxs, _, _) in enumerate(waits):
            for i, (s_ref, d_ref, to) in enumerate(_gather_copies(kind, [ins[t] for t in idxs], x, y, c)):
                came = _remote(s_ref, d_ref, wait_sems[2 * j], wait_sems[2 * j + 1], i, to)
                came.wait_recv()
                came.wait_send()
        for j, (kind, idxs) in enumerate(starts):
            for i, (s_ref, d_ref, to) in enumerate(_gather_copies(kind, [ins[t] for t in idxs], x, y, c)):
                _remote(s_ref, d_ref, start_sems[2 * j], start_sems[2 * j + 1], i, to).start()
        token[...] = jnp.zeros_like(token)

    sems = []
    for kind, idxs in starts:
        sems += [pltpu.SemaphoreType.DMA((GATHER_COPIES_PER_ARRAY[kind] * len(idxs),))] * 2
    operands = _in_hbm(bufs) + [sem for w in waits for sem in w[2:]] + after
    outs = pl.pallas_call(
        body, name=name,
        in_specs=[HBM] * nb + [SEM] * (2 * nw) + [ANY] * n_after,
        out_specs=[SEM] * (2 * ns) + [HBM] * nb + [pl.BlockSpec(memory_space=pltpu.VMEM)],
        out_shape=sems + _hbm_like(bufs) + [jax.ShapeDtypeStruct((8, 128), F32)],
        input_output_aliases={i: 2 * ns + i for i in range(nb)},
        compiler_params=pltpu.CompilerParams(has_side_effects=DATAFLOW_EFFECT),
    )(*operands)
    return outs[2 * ns:2 * ns + nb], [(outs[2 * j], outs[2 * j + 1]) for j in range(ns)], outs[-1]


class _Gather:
    def __init__(self, groups):
        self.groups = groups
        self.bufs = {}
        self.in_flight = {}

    def put(self, slotted):
        self.bufs.update(slotted)

    def step(self, name, waits, starts, after=None):
        names = []
        for _, group in list(waits) + list(starts):
            names += [n for n in self.groups[group] if n not in names]
        index = {n: i for i, n in enumerate(names)}

        def members(group):
            return [index[n] for n in self.groups[group]]

        wait_args = [(kind, members(group)) + self.in_flight.pop((kind, group)) for kind, group in waits]
        start_args = [(kind, members(group)) for kind, group in starts]
        bufs, sems, token = _gather_step(name, [self.bufs[n] for n in names], wait_args, start_args, after)
        self.bufs.update(zip(names, bufs))
        for (kind, group), pair in zip(starts, sems):
            self.in_flight[(kind, group)] = pair
        return token

    def arrays(self, group):
        return {n: self.bufs[n] for n in self.groups[group]}


def _sibling_halves_copies(srcs, dsts, x, y, c):
    out = []
    for s_ref, d_ref in zip(srcs, dsts, strict=True):
        rh = s_ref.shape[1] // 2
        out.append((s_ref.at[:, pl.ds((1 - c) * rh, rh), :], d_ref, (x, y, 1 - c)))
    return out


def _to_sibling_copies(srcs, dsts, x, y, c):
    return [(s_ref, d_ref, (x, y, 1 - c)) for s_ref, d_ref in zip(srcs, dsts, strict=True)]


def _chip_copies(srcs, dsts, x, y, c):
    out = []
    for s_ref, d_ref in zip(srcs, dsts, strict=True):
        for k, (px, py) in enumerate(_other_chips(x, y)):
            out.append((s_ref.at[2 * px + py], d_ref.at[k], (px, py, c)))
    return out


def _join_copies(srcs, dsts, x, y, c):
    out = []
    for s_ref in srcs:
        mine = _half_rows(s_ref, c)
        out.append((mine, mine, (x, y, 1 - c)))
    return out


def _exchange_start(copies_fn, n_copies, srcs, fresh, after, name):
    ns, nb = len(srcs), len(srcs) + len(fresh)

    def body(*refs):
        bufs, send, recv, token = refs[:nb], refs[nb + 1], refs[nb + 2], refs[-1]
        x, y, c = _mesh_pos()
        for i, (s_ref, d_ref, to) in enumerate(copies_fn(bufs[:ns], bufs[ns:] if fresh else bufs[:ns], x, y, c)):
            _remote(s_ref, d_ref, send, recv, i, to).start()
        token[...] = jnp.zeros_like(token)

    sems = [pltpu.SemaphoreType.DMA((n_copies,))] * 2
    outs = pl.pallas_call(
        body, name=name,
        in_specs=[HBM] * nb + [ANY], out_specs=[SEM, SEM] + [HBM] * nb + [pl.BlockSpec(memory_space=pltpu.VMEM)],
        out_shape=sems + _hbm_like(list(srcs) + list(fresh)) + [jax.ShapeDtypeStruct((8, 128), F32)],
        input_output_aliases={i: 2 + i for i in range(nb)},
        compiler_params=pltpu.CompilerParams(has_side_effects=DATAFLOW_EFFECT),
    )(*_in_hbm(list(srcs) + list(fresh)), after)
    return outs[0], outs[1], outs[2:2 + ns], outs[2 + ns:2 + nb], outs[-1]


def _exchange_done(copies_fn, srcs, fresh, send, recv, after, name):
    ns, nb = len(srcs), len(srcs) + len(fresh)

    def body(*refs):
        bufs, send_in, recv_in = refs[:nb], refs[nb], refs[nb + 1]
        x, y, c = _mesh_pos()
        for i, (s_ref, d_ref, to) in enumerate(copies_fn(bufs[:ns], bufs[ns:] if fresh else bufs[:ns], x, y, c)):
            came = _remote(s_ref, d_ref, send_in, recv_in, i, to)
            came.wait_send()
            came.wait_recv()

    outs = pl.pallas_call(
        body, name=name,
        in_specs=[HBM] * nb + [SEM, SEM, ANY], out_specs=[HBM] * nb,
        out_shape=_hbm_like(list(srcs) + list(fresh)),
        input_output_aliases={i: i for i in range(nb)},
        compiler_params=pltpu.CompilerParams(has_side_effects=DATAFLOW_EFFECT),
    )(*_in_hbm(list(srcs) + list(fresh)), send, recv, after)
    return outs[:ns], outs[ns:]


class _Reduce:
    def __init__(self, place, core, shards, mom_m, mom_v):
        self.place, self.core = place, core
        self.shards, self.mom_m, self.mom_v = shards, mom_m, mom_v
        self.state = {}
        self.results = {}

    def add(self, group, grads, after):
        names = list(grads)
        g4s = [g.reshape((N_CHIPS, -1, g.shape[-1])) if g.ndim == 2 else g for g in grads.values()]
        fresh = [lax.empty((N_CHIPS, g.shape[1] // 2, g.shape[2]), BF16) for g in g4s]
        send, recv, g4s, fresh, token = _exchange_start(_sibling_halves_copies, len(names), g4s, fresh, after,
                                                        "pair_start_" + group)
        self.state[group] = (0, names, send, recv, g4s, fresh)
        return token

    def send(self, group, theirs, after):
        names, srcs = list(theirs), list(theirs.values())
        fresh = [lax.empty(s.shape, BF16) for s in srcs]
        send, recv, srcs, fresh, token = _exchange_start(_to_sibling_copies, len(names), srcs, fresh, after, "pair_start_" + group)
        self.state[group] = ("sent", names, send, recv, srcs, fresh)
        return token

    def received(self, group, after):
        stage, names, send, recv, srcs, fresh = self.state.pop(group)
        assert stage == "sent"
        _, got = _exchange_done(_to_sibling_copies, srcs, fresh, send, recv, after, "pair_done_" + group)
        return dict(zip(names, got))

    def add_parts(self, group, parts):
        names, srcs = list(parts), list(parts.values())
        fresh = [lax.empty((N_CHIPS - 1,) + p.shape[1:], BF16) for p in srcs]
        send, recv, srcs, fresh, token = _exchange_start(_chip_copies, 3 * len(names), srcs, fresh, self.core, "chips_start_" + group)
        self.state[group] = (1, names, send, recv, srcs, fresh)
        return token

    def step(self, group, after):
        stage, names, send, recv, srcs, fresh = self.state[group]
        if stage == 0:
            g4s, ras = _exchange_done(_sibling_halves_copies, srcs, fresh, send, recv, after, "pair_done_" + group)
            parts = [_pair_sum(g, r, self.core, "pair_sum_" + n) for g, r, n in zip(g4s, ras, names)]
            fresh = [lax.empty((N_CHIPS - 1,) + p.shape[1:], BF16) for p in parts]
            send, recv, parts, fresh, token = _exchange_start(_chip_copies, 3 * len(names), parts, fresh, self.core,
                                                              "chips_start_" + group)
            self.state[group] = (1, names, send, recv, parts, fresh)
            return token
        if stage == 1:
            parts, rcs = _exchange_done(_chip_copies, srcs, fresh, send, recv, after, "chips_done_" + group)
            token = None
            for n, p, r in zip(names, parts, rcs):
                self.results[n] = _adamw_own_half(self.shards[n], self.mom_m[n], self.mom_v[n], p, r, self.place,
                                                  "adamw_own_" + n, after=token)
                token = self.results[n][2]
            wholes = [self.results[n][0] for n in names]
            send, recv, wholes, _, token = _exchange_start(_join_copies, len(names), wholes, [], token, "join_start_" + group)
            self.state[group] = (2, names, send, recv, wholes, [])
            return token
        assert stage == 2
        wholes, _ = _exchange_done(_join_copies, srcs, [], send, recv, after, "join_done_" + group)
        token = None
        for n, exchanged in zip(names, wholes):
            _, g, d, nm, nv = self.results[n]
            self.results[n] = _adamw_other_half(self.shards[n], self.mom_m[n], self.mom_v[n], exchanged, g, d, nm, nv,
                                                self.place, "adamw_other_" + n, after=token)
            token = self.results[n][1]
        del self.state[group]
        return token


N_DEV = 8


def _all_reduce_small(v):
    def body(v_ref, o_ref, slots, send_sems, recv_sems):
        x, y, c = _mesh_pos()
        me = 4 * x + 2 * y + c
        slots[me] = v_ref[...]
        peers = []
        for r in range(1, N_DEV):
            fx, fy, fc = (r >> 2) & 1, (r >> 1) & 1, r & 1
            peers.append((x + fx - 2 * x * fx, y + fy - 2 * y * fy, c + fc - 2 * c * fc))
        sends = []
        for r, peer in enumerate(peers):
            cp = _remote(v_ref, slots.at[me], send_sems, recv_sems, r, peer)
            cp.start()
            sends.append(cp)
        for r, (px, py, pc) in enumerate(peers):
            landed = slots.at[4 * px + 2 * py + pc]
            _remote(landed, landed, send_sems, recv_sems, r, (px, py, pc)).wait_recv()
        for cp in sends:
            cp.wait_send()
        acc = slots[0]
        for i in range(1, N_DEV):
            acc = acc + slots[i]
        o_ref[...] = acc

    vm = pl.BlockSpec(memory_space=pltpu.VMEM)
    return pl.pallas_call(
        body, name="small_grads_all_reduce",
        in_specs=[vm], out_specs=vm,
        out_shape=jax.ShapeDtypeStruct(v.shape, v.dtype),
        scratch_shapes=[pltpu.VMEM((N_DEV,) + v.shape, v.dtype), pltpu.SemaphoreType.DMA((N_DEV - 1,)),
                        pltpu.SemaphoreType.DMA((N_DEV - 1,))],
    )(v)


MATRICES = ("w_in", "w_attn_out", "w_conv_out", "w_o", "w_cq", "w_ckv", "w_co", "w_gate", "w_up", "w_down")
VECTORS = ("g_mix", "b_gate", "g_cross", "g_mem", "g_ffn", "g_final", "conv_w", "sink")
WEIGHT_ORDER = ("g_mix", "w_in", "sink", "conv_w", "b_gate", "w_attn_out", "w_conv_out", "w_o", "g_cross", "g_mem", "w_cq",
                "w_ckv", "w_co", "g_ffn", "w_gate", "w_up", "w_down", "g_final")
CONV_PAD_ROWS = 32
SMALL_ROWS = 8


def _pack(pieces):
    flat = jnp.concatenate([p.reshape(-1) for p in pieces])
    lane_group = SMALL_ROWS * 128
    total = -(-flat.shape[0] // lane_group) * lane_group
    flat = jnp.pad(flat, (0, total - flat.shape[0]))
    return flat.reshape(SMALL_ROWS, total // SMALL_ROWS), [p.size for p in pieces]


def _unpack(packed, pieces):
    flat = packed.reshape(-1)
    out, off = [], 0
    for p in pieces:
        out.append(flat[off:off + p.size].reshape(p.shape))
        off += p.size
    return out


def kernel(x, mem, g_mix, w_in, sink, conv_w, b_gate, w_attn_out, w_conv_out, w_o, g_cross, g_mem, w_cq, w_ckv, w_co, g_ffn, w_gate, w_up, w_down, g_final, loss_target, m_g_mix, m_w_in, m_sink, m_conv_w, m_b_gate, m_w_attn_out, m_w_conv_out, m_w_o, m_g_cross, m_g_mem, m_w_cq, m_w_ckv, m_w_co, m_g_ffn, m_w_gate, m_w_up, m_w_down, m_g_final, v_g_mix, v_w_in, v_sink, v_conv_w, v_b_gate, v_w_attn_out, v_w_conv_out, v_w_o, v_g_cross, v_g_mem, v_w_cq, v_w_ckv, v_w_co, v_g_ffn, v_w_gate, v_w_up, v_w_down, v_g_final):
    given = dict(g_mix=g_mix, w_in=w_in, sink=sink, conv_w=conv_w, b_gate=b_gate, w_attn_out=w_attn_out, w_conv_out=w_conv_out,
                 w_o=w_o, g_cross=g_cross, g_mem=g_mem, w_cq=w_cq, w_ckv=w_ckv, w_co=w_co, g_ffn=g_ffn, w_gate=w_gate, w_up=w_up,
                 w_down=w_down, g_final=g_final)
    mom_m = dict(g_mix=m_g_mix, w_in=m_w_in, sink=m_sink, conv_w=m_conv_w, b_gate=m_b_gate, w_attn_out=m_w_attn_out,
                 w_conv_out=m_w_conv_out, w_o=m_w_o, g_cross=m_g_cross, g_mem=m_g_mem, w_cq=m_w_cq, w_ckv=m_w_ckv, w_co=m_w_co,
                 g_ffn=m_g_ffn, w_gate=m_w_gate, w_up=m_w_up, w_down=m_w_down, g_final=m_g_final)
    mom_v = dict(g_mix=v_g_mix, w_in=v_w_in, sink=v_sink, conv_w=v_conv_w, b_gate=v_b_gate, w_attn_out=v_w_attn_out,
                 w_conv_out=v_w_conv_out, w_o=v_w_o, g_cross=v_g_cross, g_mem=v_g_mem, w_cq=v_w_cq, w_ckv=v_w_ckv, w_co=v_w_co,
                 g_ffn=v_g_ffn, w_gate=v_w_gate, w_up=v_w_up, w_down=v_w_down, g_final=v_g_final)
    xs, mems, target = x[0], mem[0], loss_target[0]
    d_model = xs.shape[1]
    chip = 2 * lax.axis_index("x") + lax.axis_index("y")
    core = jnp.reshape(lax.axis_index("c"), (1,)).astype(jnp.int32)
    place = jnp.stack([chip, lax.axis_index("c")]).astype(jnp.int32)

    shards = {n: given[n][0] for n in MATRICES}
    conv_cols = conv_w.shape[2]
    conv_pad = jnp.pad(conv_w[0], ((0, CONV_PAD_ROWS - conv_w.shape[1]), (0, 0)))
    fetch = _Gather(GATHER_GROUPS)
    first = {"w_in": _cast_to_slot(shards["w_in"], place, BF16, "to_slot_w_in"),
             "conv_w": _cast_to_slot(conv_pad, place, F32, "to_slot_conv_w")}
    fetch.put(first)
    tok = fetch.step("gather_start", [], [("direct", "in")])
    fetch.put({n: _cast_to_slot(shards[n], place, BF16, "to_slot_" + n, after=tok) for n in MATRICES if n != "w_in"})
    small = {n: given[n] for n in ("g_mix", "b_gate", "g_cross", "g_mem", "g_ffn")}
    small["g_final"] = g_final[None]
    small["sink"] = sink[0]

    reduce = _Reduce(place, core, shards, {n: mom_m[n][0] for n in MATRICES}, {n: mom_v[n][0] for n in MATRICES})
    sq, grad_x, small_grads = _local_step(xs, mems, target, small, fetch, reduce)

    loss_part = 0.5 * sq[0:1, 0:1] / d_model
    pieces = [small_grads[n] for n in VECTORS] + [loss_part]
    packed, _ = _pack(pieces)
    summed = _unpack(_all_reduce_small(packed), pieces)
    loss = summed[-1][0, 0]
    small_sum = dict(zip(VECTORS, summed[:-1]))
    small_sum["conv_w"] = lax.dynamic_slice_in_dim(small_sum["conv_w"], chip * conv_cols, conv_cols, axis=1)

    grad_out, delta, new_m, new_v = {}, {}, {}, {}
    like = [given[n] for n in VECTORS]
    pw, _ = _pack(like)
    pg, _ = _pack([small_sum[n] for n in VECTORS])
    pm, _ = _pack([mom_m[n] for n in VECTORS])
    pv, _ = _pack([mom_v[n] for n in VECTORS])
    tok = reduce.step("in", pg)
    _, pd, pnm, pnv = _adamw(pw, pg, pm, pv, "adamw_small", after=tok)
    for n, g, d, nm, nv in zip(VECTORS, [small_sum[n] for n in VECTORS], _unpack(pd, like), _unpack(pnm, like), _unpack(pnv, like)):
        grad_out[n] = g.reshape(given[n].shape)
        delta[n], new_m[n], new_v[n] = d, nm, nv
    reduce.step("in", pd)
    for n in MATRICES:
        g, d, nm, nv = reduce.results[n]
        grad_out[n], delta[n], new_m[n], new_v[n] = g[None], d[None], nm[None], nv[None]

    return (loss, grad_x[None], *[grad_out[n] for n in WEIGHT_ORDER], *[delta[n] for n in WEIGHT_ORDER],
            *[new_m[n] for n in WEIGHT_ORDER], *[new_v[n] for n in WEIGHT_ORDER])
```

```python
import functools

import jax
import jax.numpy as jnp
from jax import lax
from jax.experimental import pallas as pl
from jax.experimental.pallas import tpu as pltpu

F32 = jnp.float32
BF16 = jnp.bfloat16
MESH = pl.DeviceIdType.MESH
ANY = pl.BlockSpec(memory_space=pl.ANY)

VMEM_LIMIT_BYTES = 56 * 1024 * 1024

N_CHIPS = 4
HEAD_DIM = 128
N_Q_HEADS = 8
N_KV_HEADS = 2
Q_GROUP = N_Q_HEADS // N_KV_HEADS
ATTN_WIDTH = N_Q_HEADS * HEAD_DIM
KV_WIDTH = N_KV_HEADS * HEAD_DIM
WINDOW = 128
BLOCK = 128
BAND = 3 * BLOCK
ROPE_THETA = 10000.0
CONV_WIDTH = 1024
MEM_HEADS = 4
MEM_WIDTH = MEM_HEADS * HEAD_DIM
RMS_EPS = 1e-6
NEG_INF = -1e30
ATTN_SCALE = HEAD_DIM ** -0.5

Q_OFF, K_OFF, V_OFF, CU_OFF, CB_OFF, CC_OFF, GL_OFF = 0, 1024, 1280, 1536, 2560, 3584, 4608

ADAM_LR = 0.001
ADAM_B1 = 0.9
ADAM_B2 = 0.999
ADAM_EPS = 1e-08
ADAM_WD = 0.01
ADAM_STEP = 10
ADAM_C1 = 1.0 - ADAM_B1 ** ADAM_STEP
ADAM_C2 = 1.0 - ADAM_B2 ** ADAM_STEP


def _params(n_grid_axes):
    return pltpu.CompilerParams(dimension_semantics=("arbitrary",) * n_grid_axes, vmem_limit_bytes=VMEM_LIMIT_BYTES)


BF16_SUBLANES = 16


def _row_tile(rows, want):
    if rows <= want:
        return rows
    for t in range(want, 0, -BF16_SUBLANES):
        if rows % t == 0:
            return t
    return rows


def _matmul(a, b, *, mode, tm, tn, tk, out_dtypes, name, extras=(), epilogue=None, b_blocks=1, out_blocks=1, after=None):
    if mode == "tn":
        kdim, m = a.shape
    else:
        m, kdim = a.shape
    if b_blocks > 1:
        nb, brows, bcols = b.shape
        assert nb == b_blocks
        if mode == "nn":
            n = bcols * nb
            assert brows == kdim
        else:
            assert mode == "nt" and bcols * nb == kdim
            n = brows
    else:
        n = b.shape[0] if mode == "nt" else b.shape[1]
    tm, tn = min(tm, m), min(tn, n)
    assert m % tm == 0 and n % tn == 0 and tk == kdim, (name, m, n, kdim, tm, tn, tk)
    n_extra, n_out = len(extras), len(out_dtypes)
    n_after = 0 if after is None else 1

    if mode == "tn":
        a_spec = pl.BlockSpec((tk, tm), lambda j, i, k: (k, i))
        dims = (((0,), (0,)), ((), ()))
    else:
        a_spec = pl.BlockSpec((tm, tk), lambda j, i, k: (i, k))
        dims = (((1,), (0,)), ((), ())) if mode == "nn" else (((1,), (1,)), ((), ()))

    if b_blocks > 1 and mode == "nn":
        per = b.shape[2] // tn
        assert b.shape[2] % tn == 0
        b_spec = pl.BlockSpec((None, tk, tn), lambda j, i, k: (j // per, k, j % per))
    elif b_blocks > 1:
        b_spec = pl.BlockSpec((b_blocks, tn, b.shape[2]), lambda j, i, k: (0, j, 0))
    elif mode == "nt":
        b_spec = pl.BlockSpec((tn, tk), lambda j, i, k: (j, k))
    else:
        b_spec = pl.BlockSpec((tk, tn), lambda j, i, k: (k, j))

    tile_spec = pl.BlockSpec((tm, tn), lambda j, i, k: (i, j))
    if out_blocks > 1:
        ncols = n // out_blocks
        assert ncols % tn == 0
        oper = ncols // tn
        out_spec = pl.BlockSpec((None, tm, tn), lambda j, i, k: (j // oper, i, j % oper))
        out_shape = [jax.ShapeDtypeStruct((out_blocks, m, ncols), dt) for dt in out_dtypes]
    else:
        out_spec = tile_spec
        out_shape = [jax.ShapeDtypeStruct((m, n), dt) for dt in out_dtypes]

    def body(a_ref, b_ref, *rest):
        extra_refs = rest[:n_extra]
        out_refs = rest[n_extra + n_after:n_extra + n_after + n_out]
        if mode == "nt" and b_blocks > 1:
            cs = b.shape[2]
            acc = None
            for jb in range(b_blocks):
                prod = lax.dot_general(a_ref[:, jb * cs:(jb + 1) * cs].astype(BF16), b_ref[jb].astype(BF16), dims,
                                       preferred_element_type=F32)
                acc = prod if acc is None else acc + prod
        else:
            acc = lax.dot_general(a_ref[...].astype(BF16), b_ref[...].astype(BF16), dims, preferred_element_type=F32)
        tiles = (acc,) if epilogue is None else epilogue(acc, *[r[...] for r in extra_refs])
        for o_ref, t in zip(out_refs, tiles, strict=True):
            o_ref[...] = t.astype(o_ref.dtype)

    outs = pl.pallas_call(
        body,
        name=name,
        grid=(n // tn, m // tm, 1),
        in_specs=[a_spec, b_spec] + [tile_spec] * n_extra + [ANY] * n_after,
        out_specs=[out_spec] * n_out,
        out_shape=out_shape,
        compiler_params=_params(3),
    )(a, b, *extras, *([] if after is None else [after]))
    return outs[0] if n_out == 1 else outs


def _add_residual(acc, res):
    return (acc + res,)


def _matmul_column_blocks(a, b4, blocks, out, *, tm, name, after=None):
    m, kdim = a.shape
    nb, _, cols = b4.shape
    tm = min(tm, m)
    assert m % tm == 0

    def body(j_ref, a_ref, b_ref, *rest):
        rest[-1][...] = jnp.dot(a_ref[...], b_ref[...], preferred_element_type=F32)

    extra = ([] if out is None else [out]) + ([] if after is None else [after])
    n_blocks = blocks.shape[0]
    return pl.pallas_call(
        body, name=name,
        grid_spec=pltpu.PrefetchScalarGridSpec(
            num_scalar_prefetch=1, grid=(n_blocks, m // tm),
            in_specs=[pl.BlockSpec((tm, kdim), lambda j, i, blk: (i, 0)),
                      pl.BlockSpec((None, kdim, cols), lambda j, i, blk: (blk[j], 0, 0))] + [ANY] * len(extra),
            out_specs=pl.BlockSpec((tm, cols), lambda j, i, blk: (i, blk[j]))),
        out_shape=jax.ShapeDtypeStruct((m, nb * cols), F32),
        input_output_aliases={} if out is None else {3: 0},
        compiler_params=_params(2),
    )(blocks, a, b4, *extra)


def _wgrad_half(a, b, core, *, theirs, row_sharded, tm, tn, name, add=None, after=None):
    kdim, m = a.shape
    n = b.shape[1]
    rs, cs = (m // N_CHIPS, n) if row_sharded else (m, n // N_CHIPS)
    rh = rs // 2
    tm, tn = min(tm, rh), min(tn, cs)
    assert rh % tm == 0 and cs % tn == 0, (name, rh, cs, tm, tn)
    mh, per = rh // tm, cs // tn
    has_add = add is not None

    def half(c):
        return 1 - c[0] if theirs else c[0]

    if row_sharded:
        grid = (n // tn, N_CHIPS * mh)
        a_spec = pl.BlockSpec((kdim, tm), lambda j, r, c: (0, ((r // mh) * 2 + half(c)) * mh + r % mh))
        o_spec = pl.BlockSpec((None, tm, tn), lambda j, r, c: (r // mh, r % mh, j))
    else:
        grid = (n // tn, mh)
        a_spec = pl.BlockSpec((kdim, tm), lambda j, r, c: (0, half(c) * mh + r))
        o_spec = pl.BlockSpec((None, tm, tn), lambda j, r, c: (j // per, r, j % per))
    b_spec = pl.BlockSpec((kdim, tn), lambda j, r, c: (0, j))

    def body(c_ref, a_ref, b_ref, *rest):
        o_ref = rest[-1]
        acc = lax.dot_general(a_ref[...].astype(BF16), b_ref[...].astype(BF16), (((0,), (0,)), ((), ())),
                              preferred_element_type=F32)
        if has_add:
            acc = acc + rest[0][...].astype(F32)
        o_ref[...] = acc.astype(BF16)

    operands = [a, b] + ([add] if has_add else []) + ([] if after is None else [after])
    return pl.pallas_call(
        body, name=name,
        grid_spec=pltpu.PrefetchScalarGridSpec(
            num_scalar_prefetch=1, grid=grid,
            in_specs=[a_spec, b_spec] + ([o_spec] if has_add else []) + ([] if after is None else [ANY]),
            out_specs=o_spec),
        out_shape=jax.ShapeDtypeStruct((N_CHIPS, rh, cs), BF16),
        compiler_params=_params(2),
    )(core, *operands)


def _rstd(x):
    return lax.rsqrt(jnp.mean(x * x, axis=-1, keepdims=True) + RMS_EPS)


def _rmsnorm(x, g, name):
    s, d = x.shape
    tr = _row_tile(s, 256)

    def body(x_ref, g_ref, o_ref):
        xv = x_ref[...]
        o_ref[...] = (xv * _rstd(xv) * g_ref[...]).astype(BF16)

    return pl.pallas_call(
        body, name=name, grid=(s // tr,),
        in_specs=[pl.BlockSpec((tr, d), lambda i: (i, 0)), pl.BlockSpec((1, d), lambda i: (0, 0))],
        out_specs=pl.BlockSpec((tr, d), lambda i: (i, 0)),
        out_shape=jax.ShapeDtypeStruct((s, d), BF16),
        compiler_params=_params(1),
    )(x, g)


def _rmsnorm_bwd(dh, x, g, dres, name):
    s, d = x.shape
    tr = _row_tile(s, 256)
    has_res = dres is not None

    def body(*refs):
        if has_res:
            dh_ref, x_ref, g_ref, res_ref, dx_ref, dxb_ref, dg_ref = refs
        else:
            dh_ref, x_ref, g_ref, dx_ref, dxb_ref, dg_ref = refs
        xv = x_ref[...]
        dhv = dh_ref[...].astype(F32)
        r = _rstd(xv)
        xn = xv * r
        dhg = dhv * g_ref[...]
        dx = r * (dhg - xn * jnp.mean(dhg * xn, axis=-1, keepdims=True))
        if has_res:
            dx = dx + res_ref[...]
        dx_ref[...] = dx
        dxb_ref[...] = dx.astype(BF16)
        part = jnp.sum(dhv * xn, axis=0, keepdims=True)

        @pl.when(pl.program_id(0) == 0)
        def _():
            dg_ref[...] = part

        @pl.when(pl.program_id(0) > 0)
        def _():
            dg_ref[...] += part

    row = pl.BlockSpec((tr, d), lambda i: (i, 0))
    vec = pl.BlockSpec((1, d), lambda i: (0, 0))
    return pl.pallas_call(
        body, name=name, grid=(s // tr,),
        in_specs=[row, row, vec] + ([row] if has_res else []),
        out_specs=[row, row, vec],
        out_shape=[jax.ShapeDtypeStruct((s, d), F32), jax.ShapeDtypeStruct((s, d), BF16), jax.ShapeDtypeStruct((1, d), F32)],
        compiler_params=_params(1),
    )(*([dh, x, g] + ([dres] if has_res else [])))


def _loss_head(x3, g, target):
    s, d = x3.shape
    tr = _row_tile(s, 256)

    def body(x_ref, g_ref, t_ref, dx_ref, dxb_ref, sq_ref, dg_ref):
        xv = x_ref[...]
        gv = g_ref[...]
        r = _rstd(xv)
        xn = xv * r
        err = xn * gv - t_ref[...]
        dy = err * (1.0 / d)
        dyg = dy * gv
        dx = r * (dyg - xn * jnp.mean(dyg * xn, axis=-1, keepdims=True))
        dx_ref[...] = dx
        dxb_ref[...] = dx.astype(BF16)
        sq = jnp.sum(jnp.sum(err * err, axis=1, keepdims=True), axis=0, keepdims=True)
        sq = jnp.broadcast_to(sq, (1, 128))
        part = jnp.sum(dy * xn, axis=0, keepdims=True)

        @pl.when(pl.program_id(0) == 0)
        def _():
            sq_ref[...] = sq
            dg_ref[...] = part

        @pl.when(pl.program_id(0) > 0)
        def _():
            sq_ref[...] += sq
            dg_ref[...] += part

    row = pl.BlockSpec((tr, d), lambda i: (i, 0))
    vec = pl.BlockSpec((1, d), lambda i: (0, 0))
    return pl.pallas_call(
        body, name="loss_head", grid=(s // tr,),
        in_specs=[row, vec, row],
        out_specs=[row, row, pl.BlockSpec((1, 128), lambda i: (0, 0)), vec],
        out_shape=[jax.ShapeDtypeStruct((s, d), F32), jax.ShapeDtypeStruct((s, d), BF16),
                   jax.ShapeDtypeStruct((1, 128), F32), jax.ShapeDtypeStruct((1, d), F32)],
        compiler_params=_params(1),
    )(x3, g, target)


def _rope_tables(s):
    inv = 1.0 / (ROPE_THETA ** (jnp.arange(0, HEAD_DIM, 2, dtype=F32) / HEAD_DIM))
    ang = jnp.arange(s, dtype=F32)[:, None] * inv[None, :]
    cos, sin = jnp.cos(ang), jnp.sin(ang)
    return jnp.concatenate([cos, cos], axis=1), jnp.concatenate([-sin, sin], axis=1)


def _swap_halves(t):
    return pltpu.roll(t, HEAD_DIM // 2, 1)


def _rope_fwd(z, cos_t, sin_t):
    s = z.shape[0]
    tr = _row_tile(s, 256)

    def body(zq_ref, zk_ref, zv_ref, c_ref, s_ref, q_ref, k_ref, v_ref):
        c, sn = c_ref[...], s_ref[...]
        for hd in range(N_Q_HEADS):
            cols = slice(hd * HEAD_DIM, (hd + 1) * HEAD_DIM)
            t = zq_ref[:, cols]
            q_ref[:, cols] = (t * c + _swap_halves(t) * sn).astype(BF16)
        for hd in range(N_KV_HEADS):
            cols = slice(hd * HEAD_DIM, (hd + 1) * HEAD_DIM)
            t = zk_ref[:, cols]
            k_ref[:, cols] = (t * c + _swap_halves(t) * sn).astype(BF16)
        v_ref[...] = zv_ref[...].astype(BF16)

    tab = pl.BlockSpec((tr, HEAD_DIM), lambda i: (i, 0))
    return pl.pallas_call(
        body, name="rope_fwd", grid=(s // tr,),
        in_specs=[pl.BlockSpec((tr, ATTN_WIDTH), lambda i: (i, Q_OFF // ATTN_WIDTH)),
                  pl.BlockSpec((tr, KV_WIDTH), lambda i: (i, K_OFF // KV_WIDTH)),
                  pl.BlockSpec((tr, KV_WIDTH), lambda i: (i, V_OFF // KV_WIDTH)), tab, tab],
        out_specs=[pl.BlockSpec((tr, ATTN_WIDTH), lambda i: (i, 0)), pl.BlockSpec((tr, KV_WIDTH), lambda i: (i, 0)),
                   pl.BlockSpec((tr, KV_WIDTH), lambda i: (i, 0))],
        out_shape=[jax.ShapeDtypeStruct((s, ATTN_WIDTH), BF16), jax.ShapeDtypeStruct((s, KV_WIDTH), BF16),
                   jax.ShapeDtypeStruct((s, KV_WIDTH), BF16)],
        compiler_params=_params(1),
    )(z, z, z, cos_t, sin_t)


def _rope_bwd(dq_rot, dk_rot, dv, cos_t, sin_t, dz):
    s = dq_rot.shape[0]
    tr = _row_tile(s, 256)
    qkv_width = V_OFF + KV_WIDTH

    def body(dq_ref, dk_ref, dv_ref, c_ref, s_ref, dz_in_ref, o_ref):
        c, sn = c_ref[...], s_ref[...]
        for hd in range(N_Q_HEADS):
            t = dq_ref[:, hd * HEAD_DIM:(hd + 1) * HEAD_DIM]
            o_ref[:, Q_OFF + hd * HEAD_DIM:Q_OFF + (hd + 1) * HEAD_DIM] = (t * c + _swap_halves(t * sn)).astype(BF16)
        for hd in range(N_KV_HEADS):
            t = dk_ref[:, hd * HEAD_DIM:(hd + 1) * HEAD_DIM]
            o_ref[:, K_OFF + hd * HEAD_DIM:K_OFF + (hd + 1) * HEAD_DIM] = (t * c + _swap_halves(t * sn)).astype(BF16)
        o_ref[:, V_OFF:V_OFF + KV_WIDTH] = dv_ref[...].astype(BF16)

    tab = pl.BlockSpec((tr, HEAD_DIM), lambda i: (i, 0))
    wide = pl.BlockSpec((tr, ATTN_WIDTH), lambda i: (i, 0))
    narrow = pl.BlockSpec((tr, KV_WIDTH), lambda i: (i, 0))
    return pl.pallas_call(
        body, name="rope_bwd", grid=(s // tr,),
        in_specs=[wide, narrow, narrow, tab, tab, ANY],
        out_specs=pl.BlockSpec((tr, qkv_width), lambda i: (i, 0)),
        out_shape=jax.ShapeDtypeStruct(dz.shape, dz.dtype),
        input_output_aliases={5: 0},
        compiler_params=_params(1),
    )(dq_rot, dk_rot, dv, cos_t, sin_t, dz)


def _swa_band(i, s):
    return pl.multiple_of(jnp.clip((i - 1) * BLOCK, 0, s - BAND), BLOCK)


SWA_HEADS_PER_PASS = Q_GROUP


def _swa_probs(q_ref, k_ref, sink_ref, heads, start, valid):
    kv = heads[0] // Q_GROUP
    cols = slice(kv * HEAD_DIM, (kv + 1) * HEAD_DIM)
    kb = k_ref[pl.ds(start, BAND), cols]
    qg = jnp.concatenate([q_ref[:, hd * HEAD_DIM:(hd + 1) * HEAD_DIM] for hd in heads], axis=0)
    sc = lax.dot_general(qg, kb, (((1,), (1,)), ((), ())), preferred_element_type=F32) * ATTN_SCALE
    sc = jnp.where(valid, sc, NEG_INF)
    sk = jnp.concatenate([jnp.full((BLOCK, 1), sink_ref[hd], F32) for hd in heads], axis=0)
    mx = jnp.maximum(jnp.max(sc, axis=1, keepdims=True), sk)
    e = jnp.exp(sc - mx)
    es = jnp.exp(sk - mx)
    inv = 1.0 / (jnp.sum(e, axis=1, keepdims=True) + es)
    return qg, kb, e * inv, es * inv


def _swa_head_passes():
    return [list(range(h0, h0 + SWA_HEADS_PER_PASS)) for h0 in range(0, N_Q_HEADS, SWA_HEADS_PER_PASS)]


def _swa_valid(i, start):
    q_pos = i * BLOCK + lax.broadcasted_iota(jnp.int32, (BLOCK, 1), 0)
    q_pos = jnp.concatenate([q_pos] * SWA_HEADS_PER_PASS, axis=0)
    k_pos = start + lax.broadcasted_iota(jnp.int32, (1, BAND), 1)
    return jnp.abs(k_pos - q_pos) <= WINDOW


def _swa_fwd(q, k, v, sink):
    s = q.shape[0]
    assert s % BLOCK == 0 and s >= BAND

    def body(sink_ref, q_ref, k_ref, v_ref, o_ref):
        i = pl.program_id(0)
        start = _swa_band(i, s)
        valid = _swa_valid(i, start)
        for heads in _swa_head_passes():
            kv = heads[0] // Q_GROUP
            _, _, p, _ = _swa_probs(q_ref, k_ref, sink_ref, heads, start, valid)
            vb = v_ref[pl.ds(start, BAND), kv * HEAD_DIM:(kv + 1) * HEAD_DIM]
            o = jnp.dot(p.astype(BF16), vb, preferred_element_type=F32)
            for g, hd in enumerate(heads):
                o_ref[:, hd * HEAD_DIM:(hd + 1) * HEAD_DIM] = o[g * BLOCK:(g + 1) * BLOCK].astype(BF16)

    whole = pl.BlockSpec((s, KV_WIDTH), lambda i: (0, 0))
    blk = pl.BlockSpec((BLOCK, ATTN_WIDTH), lambda i: (i, 0))
    return pl.pallas_call(
        body, name="swa_fwd", grid=(s // BLOCK,),
        in_specs=[pl.BlockSpec(memory_space=pltpu.SMEM), blk, whole, whole],
        out_specs=blk,
        out_shape=jax.ShapeDtypeStruct((s, ATTN_WIDTH), BF16),
        compiler_params=_params(1),
    )(sink, q, k, v)


def _swa_bwd(q, k, v, d_out, sink):
    s = q.shape[0]

    def body(sink_ref, q_ref, k_ref, v_ref, do_ref, dq_ref, dk_ref, dv_ref, dsink_ref):
        i = pl.program_id(0)

        @pl.when(i == 0)
        def _():
            dk_ref[...] = jnp.zeros_like(dk_ref)
            dv_ref[...] = jnp.zeros_like(dv_ref)
            dsink_ref[...] = jnp.zeros_like(dsink_ref)

        start = _swa_band(i, s)
        valid = _swa_valid(i, start)
        for heads in _swa_head_passes():
            kv = heads[0] // Q_GROUP
            cols = slice(kv * HEAD_DIM, (kv + 1) * HEAD_DIM)
            qg, kb, p, p_sink = _swa_probs(q_ref, k_ref, sink_ref, heads, start, valid)
            vb = v_ref[pl.ds(start, BAND), cols]
            dog = jnp.concatenate([do_ref[:, hd * HEAD_DIM:(hd + 1) * HEAD_DIM] for hd in heads], axis=0)
            dp = lax.dot_general(dog, vb, (((1,), (1,)), ((), ())), preferred_element_type=F32)
            delta = jnp.sum(p * dp, axis=1, keepdims=True)
            ds = (p * (dp - delta) * ATTN_SCALE).astype(BF16)
            dqg = jnp.dot(ds, kb, preferred_element_type=F32)
            dk_ref[pl.ds(start, BAND), cols] += lax.dot_general(ds, qg, (((0,), (0,)), ((), ())), preferred_element_type=F32)
            dv_ref[pl.ds(start, BAND), cols] += lax.dot_general(p.astype(BF16), dog, (((0,), (0,)), ((), ())),
                                                                 preferred_element_type=F32)
            dsk = p_sink * delta
            for g, hd in enumerate(heads):
                dq_ref[:, hd * HEAD_DIM:(hd + 1) * HEAD_DIM] = dqg[g * BLOCK:(g + 1) * BLOCK]
                tot = jnp.sum(dsk[g * BLOCK:(g + 1) * BLOCK], axis=0, keepdims=True)
                dsink_ref[hd:hd + 1, :] -= jnp.broadcast_to(tot, (1, 128))

    whole = pl.BlockSpec((s, KV_WIDTH), lambda i: (0, 0))
    blk = pl.BlockSpec((BLOCK, ATTN_WIDTH), lambda i: (i, 0))
    return pl.pallas_call(
        body, name="swa_bwd", grid=(s // BLOCK,),
        in_specs=[pl.BlockSpec(memory_space=pltpu.SMEM), blk, whole, whole, blk],
        out_specs=[blk, whole, whole, pl.BlockSpec((N_Q_HEADS, 128), lambda i: (0, 0))],
        out_shape=[jax.ShapeDtypeStruct((s, ATTN_WIDTH), F32), jax.ShapeDtypeStruct((s, KV_WIDTH), F32),
                   jax.ShapeDtypeStruct((s, KV_WIDTH), F32), jax.ShapeDtypeStruct((N_Q_HEADS, 128), F32)],
        compiler_params=_params(1),
    )(sink, q, k, v, d_out)


CONV_CHUNK = 256


def _shift_rows(t, rows, down):
    n = t.shape[0]
    rolled = pltpu.roll(t, 1 if down else n - 1, 0)
    edge = 0 if down else n - 1
    return jnp.where(rows == edge, 0.0, rolled)


def _conv_specs(s):
    def z_spec(off):
        return pl.BlockSpec((s, CONV_CHUNK), lambda j, off=off: (0, off // CONV_CHUNK + j))
    chunk = pl.BlockSpec((s, CONV_CHUNK), lambda j: (0, j))
    w_spec = pl.BlockSpec((3, CONV_CHUNK), lambda j: (0, j))
    return z_spec(CU_OFF), z_spec(CB_OFF), z_spec(CC_OFF), chunk, w_spec


def _conv_fwd(z, conv_w):
    s = z.shape[0]
    cu_spec, cb_spec, cc_spec, chunk, w_spec = _conv_specs(s)

    def body(cu_ref, cb_ref, cc_ref, w_ref, o_ref):
        rows = lax.broadcasted_iota(jnp.int32, (s, 1), 0)
        t = cc_ref[...] * cu_ref[...]
        c3 = _shift_rows(t, rows, True) * w_ref[0:1, :] + t * w_ref[1:2, :] + _shift_rows(t, rows, False) * w_ref[2:3, :]
        o_ref[...] = (cb_ref[...] * c3).astype(BF16)

    return pl.pallas_call(
        body, name="conv_fwd", grid=(CONV_WIDTH // CONV_CHUNK,),
        in_specs=[cu_spec, cb_spec, cc_spec, w_spec],
        out_specs=chunk,
        out_shape=jax.ShapeDtypeStruct((s, CONV_WIDTH), BF16),
        compiler_params=_params(1),
    )(z, z, z, conv_w)


def _conv_bwd(z, conv_w, d_co, dz):
    s = z.shape[0]
    cu_spec, cb_spec, cc_spec, chunk, w_spec = _conv_specs(s)
    n_chunks = CONV_WIDTH // CONV_CHUNK
    offsets = (CU_OFF, CB_OFF, CC_OFF)

    def body(cu_ref, cb_ref, cc_ref, w_ref, d_ref, dz_in_ref, dz_ref, dw_ref, buf, sems):
        j = pl.program_id(0)

        def copies(j_at):
            return [pltpu.make_async_copy(buf.at[h], dz_ref.at[:, pl.ds(off + j_at * CONV_CHUNK, CONV_CHUNK)], sems.at[h])
                    for h, off in enumerate(offsets)]

        rows = lax.broadcasted_iota(jnp.int32, (s, 1), 0)
        cu, cc = cu_ref[...], cc_ref[...]
        t = cc * cu
        t_dn, t_up = _shift_rows(t, rows, True), _shift_rows(t, rows, False)
        c3 = t_dn * w_ref[0:1, :] + t * w_ref[1:2, :] + t_up * w_ref[2:3, :]
        d = d_ref[...]
        dc3 = d * cb_ref[...]
        dw_ref[0:1, :] = jnp.sum(dc3 * t_dn, axis=0, keepdims=True)
        dw_ref[1:2, :] = jnp.sum(dc3 * t, axis=0, keepdims=True)
        dw_ref[2:3, :] = jnp.sum(dc3 * t_up, axis=0, keepdims=True)
        dt = _shift_rows(dc3, rows, False) * w_ref[0:1, :] + dc3 * w_ref[1:2, :] + _shift_rows(dc3, rows, True) * w_ref[2:3, :]

        @pl.when(j > 0)
        def _():
            for cp in copies(j):
                cp.wait()

        buf[0] = (dt * cc).astype(BF16)
        buf[1] = (d * c3).astype(BF16)
        buf[2] = (dt * cu).astype(BF16)
        for cp in copies(j):
            cp.start()

        @pl.when(j == n_chunks - 1)
        def _():
            for cp in copies(j):
                cp.wait()

    return pl.pallas_call(
        body, name="conv_bwd", grid=(n_chunks,),
        in_specs=[cu_spec, cb_spec, cc_spec, w_spec, chunk, ANY],
        out_specs=[ANY, w_spec],
        out_shape=[jax.ShapeDtypeStruct(dz.shape, dz.dtype), jax.ShapeDtypeStruct((3, CONV_WIDTH), F32)],
        input_output_aliases={5: 0},
        scratch_shapes=[pltpu.VMEM((3, s, CONV_CHUNK), BF16), pltpu.SemaphoreType.DMA((3,))],
        compiler_params=_params(1),
    )(z, z, z, conv_w, d_co, dz)


GATE_CHUNK = 512


def _gate_specs(s, d, tr):
    n_chunks = d // GATE_CHUNK
    za = pl.BlockSpec((tr, GATE_CHUNK), lambda j, i: (i, GL_OFF // GATE_CHUNK + j))
    zc = pl.BlockSpec((tr, GATE_CHUNK), lambda j, i: (i, GL_OFF // GATE_CHUNK + n_chunks + j))
    ba = pl.BlockSpec((1, GATE_CHUNK), lambda j, i: (0, j))
    bc = pl.BlockSpec((1, GATE_CHUNK), lambda j, i: (0, n_chunks + j))
    tile = pl.BlockSpec((tr, GATE_CHUNK), lambda j, i: (i, j))
    return za, zc, ba, bc, tile


def _gate_fwd(z, b_gate, ya, yc):
    s, d = ya.shape
    tr = _row_tile(s, 512)
    za, zc, ba, bc, tile = _gate_specs(s, d, tr)

    def body(za_ref, zc_ref, ba_ref, bc_ref, ya_ref, yc_ref, o_ref):
        ga = jax.nn.sigmoid(za_ref[...] + ba_ref[...])
        gc = jax.nn.sigmoid(zc_ref[...] + bc_ref[...])
        o_ref[...] = (ga * ya_ref[...] + gc * yc_ref[...]).astype(BF16)

    return pl.pallas_call(
        body, name="gate_fwd", grid=(d // GATE_CHUNK, s // tr),
        in_specs=[za, zc, ba, bc, tile, tile],
        out_specs=tile,
        out_shape=jax.ShapeDtypeStruct((s, d), BF16),
        compiler_params=_params(2),
    )(z, z, b_gate, b_gate, ya, yc)


def _gate_bwd(z, b_gate, ya, yc, dmix):
    s, d = ya.shape
    tr = _row_tile(s, 512)
    za, zc, ba, bc, tile = _gate_specs(s, d, tr)
    vec = pl.BlockSpec((1, GATE_CHUNK), lambda j, i: (0, j))
    n_rows = s // tr
    in_width = z.shape[1]

    def body(za_ref, zc_ref, ba_ref, bc_ref, ya_ref, yc_ref, dm_ref, dya_ref, dyc_ref, dz_ref, dba_ref, dbc_ref, buf, sems):
        j, i = pl.program_id(0), pl.program_id(1)

        def copies(j_at, i_at):
            rows = pl.ds(i_at * tr, tr)
            return [pltpu.make_async_copy(buf.at[h], dz_ref.at[rows, pl.ds(GL_OFF + h * d + j_at * GATE_CHUNK, GATE_CHUNK)],
                                          sems.at[h]) for h in range(2)]

        ga = jax.nn.sigmoid(za_ref[...] + ba_ref[...])
        gc = jax.nn.sigmoid(zc_ref[...] + bc_ref[...])
        dm = dm_ref[...]
        dya_ref[...] = (dm * ga).astype(BF16)
        dyc_ref[...] = (dm * gc).astype(BF16)
        dla = dm * ya_ref[...] * ga * (1.0 - ga)
        dlc = dm * yc_ref[...] * gc * (1.0 - gc)

        @pl.when(j * n_rows + i > 0)
        def _():
            for cp in copies(j, i):
                cp.wait()

        buf[0] = dla.astype(BF16)
        buf[1] = dlc.astype(BF16)
        for cp in copies(j, i):
            cp.start()

        @pl.when((j == d // GATE_CHUNK - 1) & (i == n_rows - 1))
        def _():
            for cp in copies(j, i):
                cp.wait()

        pa = jnp.sum(dla, axis=0, keepdims=True)
        pc = jnp.sum(dlc, axis=0, keepdims=True)

        @pl.when(i == 0)
        def _():
            dba_ref[...] = pa
            dbc_ref[...] = pc

        @pl.when(i > 0)
        def _():
            dba_ref[...] += pa
            dbc_ref[...] += pc

    big = jax.ShapeDtypeStruct((s, d), BF16)
    small = jax.ShapeDtypeStruct((1, d), F32)
    return pl.pallas_call(
        body, name="gate_bwd", grid=(d // GATE_CHUNK, n_rows),
        in_specs=[za, zc, ba, bc, tile, tile, tile],
        out_specs=[tile, tile, ANY, vec, vec],
        out_shape=[big, big, jax.ShapeDtypeStruct((s, in_width), BF16), small, small],
        scratch_shapes=[pltpu.VMEM((2, tr, GATE_CHUNK), BF16), pltpu.SemaphoreType.DMA((2,))],
        compiler_params=_params(2),
    )(z, z, b_gate, b_gate, ya, yc, dmix)


def _cross_probs(q_ref, kv_ref, hd):
    cols = slice(hd * HEAD_DIM, (hd + 1) * HEAD_DIM)
    qh = q_ref[:, cols]
    kh = kv_ref[:, cols]
    sc = lax.dot_general(qh, kh, (((1,), (1,)), ((), ())), preferred_element_type=F32) * ATTN_SCALE
    e = jnp.exp(sc - jnp.max(sc, axis=1, keepdims=True))
    return qh, kh, e * (1.0 / jnp.sum(e, axis=1, keepdims=True))


def _cross_fwd(qc, kvc):
    s = qc.shape[0]
    n_mem = kvc.shape[0]
    tq = _row_tile(s, 256)

    def body(q_ref, kv_ref, o_ref):
        for hd in range(MEM_HEADS):
            _, _, p = _cross_probs(q_ref, kv_ref, hd)
            vh = kv_ref[:, MEM_WIDTH + hd * HEAD_DIM:MEM_WIDTH + (hd + 1) * HEAD_DIM]
            o_ref[:, hd * HEAD_DIM:(hd + 1) * HEAD_DIM] = jnp.dot(p.astype(BF16), vh, preferred_element_type=F32).astype(BF16)

    return pl.pallas_call(
        body, name="cross_fwd", grid=(s // tq,),
        in_specs=[pl.BlockSpec((tq, MEM_WIDTH), lambda i: (i, 0)), pl.BlockSpec((n_mem, 2 * MEM_WIDTH), lambda i: (0, 0))],
        out_specs=pl.BlockSpec((tq, MEM_WIDTH), lambda i: (i, 0)),
        out_shape=jax.ShapeDtypeStruct((s, MEM_WIDTH), BF16),
        compiler_params=_params(1),
    )(qc, kvc)


def _cross_bwd(qc, kvc, d_out):
    s = qc.shape[0]
    n_mem = kvc.shape[0]
    tq = _row_tile(s, 256)

    def body(q_ref, kv_ref, do_ref, dq_ref, dkv_ref):
        @pl.when(pl.program_id(0) == 0)
        def _():
            dkv_ref[...] = jnp.zeros_like(dkv_ref)

        for hd in range(MEM_HEADS):
            cols = slice(hd * HEAD_DIM, (hd + 1) * HEAD_DIM)
            vcols = slice(MEM_WIDTH + hd * HEAD_DIM, MEM_WIDTH + (hd + 1) * HEAD_DIM)
            qh, kh, p = _cross_probs(q_ref, kv_ref, hd)
            doh = do_ref[:, cols]
            dp = lax.dot_general(doh, kv_ref[:, vcols], (((1,), (1,)), ((), ())), preferred_element_type=F32)
            ds = (p * (dp - jnp.sum(p * dp, axis=1, keepdims=True)) * ATTN_SCALE).astype(BF16)
            dq_ref[:, cols] = jnp.dot(ds, kh, preferred_element_type=F32).astype(BF16)
            dkv_ref[:, cols] += lax.dot_general(ds, qh, (((0,), (0,)), ((), ())), preferred_element_type=F32)
            dkv_ref[:, vcols] += lax.dot_general(p.astype(BF16), doh, (((0,), (0,)), ((), ())), preferred_element_type=F32)

    qspec = pl.BlockSpec((tq, MEM_WIDTH), lambda i: (i, 0))
    kvspec = pl.BlockSpec((n_mem, 2 * MEM_WIDTH), lambda i: (0, 0))
    return pl.pallas_call(
        body, name="cross_bwd", grid=(s // tq,),
        in_specs=[qspec, kvspec, qspec],
        out_specs=[qspec, kvspec],
        out_shape=[jax.ShapeDtypeStruct((s, MEM_WIDTH), BF16), jax.ShapeDtypeStruct((n_mem, 2 * MEM_WIDTH), F32)],
        compiler_params=_params(1),
    )(qc, kvc, d_out)


def _swiglu_fwd(up, gate):
    return up, (gate * jax.nn.sigmoid(gate)) * up


def _swiglu_bwd(d_act, gate, up):
    sg = jax.nn.sigmoid(gate)
    silu = gate * sg
    return d_act * up * (sg * (1.0 + gate * (1.0 - sg))), d_act * silu


GATHER_GROUPS = {"in": ("w_in", "conv_w"), "mid": ("w_attn_out", "w_conv_out", "w_o", "w_cq", "w_ckv", "w_co"),
                 "gate": ("w_gate",), "up": ("w_up",), "down": ("w_down",)}


def _local_step(xs, mems, target, small, fetch, reduce):
    s, d = xs.shape
    w4 = {}
    cos_t, sin_t = _rope_tables(s)

    def near(group, done, then, after):
        waits = [("direct", group)] + ([("pass_near", done), ("pass_far", done)] if done else [])
        starts = [("forward", group), ("pass_near", group)] + [("direct", g) for g in then]
        tok = fetch.step("gather_near_" + group, waits, starts, after)
        if done:
            w4.update(fetch.arrays(done))
        return tok

    def far(group, then, after):
        return fetch.step("gather_far_" + group, [("forward", group)], [("pass_far", group)] + [("direct", g) for g in then], after)

    def last(group, after):
        tok = fetch.step("gather_done_" + group, [("pass_near", group), ("pass_far", group)], [], after)
        w4.update(fetch.arrays(group))
        return tok

    h = _rmsnorm(xs, small["g_mix"], "norm_mix")
    slots_filled = [a for g in ("gate", "up", "down") for a in fetch.arrays(g).values()]
    chip_x, chip_y = reduce.place[0] // 2, reduce.place[0] % 2
    own_block = jnp.stack([2 * chip_x + chip_y]).astype(jnp.int32)
    near_blocks = jnp.stack([2 * (1 - chip_x) + chip_y, 2 * chip_x + (1 - chip_y)]).astype(jnp.int32)
    far_block = jnp.stack([2 * (1 - chip_x) + (1 - chip_y)]).astype(jnp.int32)
    z = _matmul_column_blocks(h, fetch.arrays("in")["w_in"], own_block, None, tm=512, name="in_proj_own")
    tok = near("in", None, ["mid"], [z] + slots_filled)
    tok = fetch.step("gather_near_done_in", [("pass_near", "in")], [], tok)
    z = _matmul_column_blocks(h, fetch.arrays("in")["w_in"], near_blocks, z, tm=512, name="in_proj_near", after=tok)
    tok = far("in", [], z)
    tok = fetch.step("gather_done_in", [("pass_far", "in")], [], tok)
    w4.update(fetch.arrays("in"))
    z = _matmul_column_blocks(h, w4["w_in"], far_block, z, tm=512, name="in_proj_far", after=tok)
    conv4 = w4["conv_w"]
    conv_w = conv4[:, :3, :].transpose(1, 0, 2).reshape(3, N_CHIPS * conv4.shape[2])
    c_in = w4["w_in"].shape[2]
    tok = near("mid", None, ["gate"], z)
    q_rot, k_rot, v_b = _rope_fwd(z, cos_t, sin_t)
    attn = _swa_fwd(q_rot, k_rot, v_b, small["sink"])
    co = _conv_fwd(z, conv_w)
    tok = far("mid", ["up"], attn)
    tok = last("mid", tok)
    w_o = w4["w_o"].reshape(-1, w4["w_o"].shape[-1])
    c_d = w4["w_attn_out"].shape[2]
    ya = _matmul(attn, w4["w_attn_out"], mode="nn", tm=1024, tn=c_d, tk=ATTN_WIDTH, out_dtypes=[F32], name="attn_out_proj",
                 b_blocks=N_CHIPS, after=tok)
    yc = _matmul(co, w4["w_conv_out"], mode="nn", tm=1024, tn=c_d, tk=CONV_WIDTH, out_dtypes=[F32], name="conv_out_proj",
                 b_blocks=N_CHIPS)
    mix = _gate_fwd(z, small["b_gate"], ya, yc)
    x1 = _matmul(mix, w_o, mode="nn", tm=512, tn=1024, tk=d, out_dtypes=[F32], name="mix_out_proj", extras=[xs],
                 epilogue=_add_residual)
    tok = near("gate", None, ["down"], x1)
    w_cq = w4["w_cq"].reshape(-1, w4["w_cq"].shape[-1])
    w_ckv = w4["w_ckv"].reshape(-1, w4["w_ckv"].shape[-1])
    hc = _rmsnorm(x1, small["g_cross"], "norm_cross")
    memn = _rmsnorm(mems, small["g_mem"], "norm_mem")
    qc = _matmul(hc, w_cq, mode="nn", tm=1024, tn=MEM_WIDTH, tk=d, out_dtypes=[BF16], name="cross_q_proj", after=tok)
    kvc = _matmul(memn, w_ckv, mode="nn", tm=256, tn=2 * MEM_WIDTH, tk=d, out_dtypes=[BF16], name="cross_kv_proj")
    oc = _cross_fwd(qc, kvc)
    tok = far("gate", [], oc)
    x2 = _matmul(oc, w4["w_co"], mode="nn", tm=1024, tn=c_d, tk=MEM_WIDTH, out_dtypes=[F32], name="cross_out_proj",
                 extras=[x1], epilogue=_add_residual, b_blocks=N_CHIPS, after=tok)
    hf = _rmsnorm(x2, small["g_ffn"], "norm_ffn")
    tok = near("up", "gate", [], hf)
    c_ff = w4["w_gate"].shape[2]
    gate = _matmul(hf, w4["w_gate"], mode="nn", tm=512, tn=c_ff, tk=d, out_dtypes=[F32], name="ffn_gate_proj", b_blocks=N_CHIPS,
                   after=tok)
    tok = far("up", [], gate)
    tok = near("down", "up", [], tok)
    up, act = _matmul(hf, w4["w_up"], mode="nn", tm=512, tn=c_ff, tk=d, out_dtypes=[F32, BF16], name="ffn_up_proj",
                      extras=[gate], epilogue=_swiglu_fwd, b_blocks=N_CHIPS, after=tok)
    tok = far("down", [], act)
    last("down", tok)
    w_down = w4["w_down"].reshape(-1, w4["w_down"].shape[-1])
    x3 = _matmul(act, w_down, mode="nn", tm=512, tn=512, tk=w_down.shape[0], out_dtypes=[F32], name="ffn_down_proj", extras=[x2],
                 epilogue=_add_residual)
    dx3, dx3b, sq, dg_final = _loss_head(x3, small["g_final"], target)

    da, du = _matmul(dx3b, w_down, mode="nt", tm=512, tn=c_ff, tk=d, out_dtypes=[BF16, BF16], name="ffn_down_bwd",
                     extras=[gate, up], epilogue=_swiglu_bwd)
    core = reduce.core
    ffn_shape = dict(row_sharded=False, tm=1024, tn=c_ff)
    g_down = _matmul(act, dx3b, mode="tn", tm=c_ff, tn=1024, tk=s, out_dtypes=[BF16], name="ffn_down_wgrad")
    tok = reduce.add("down", {"w_down": g_down}, da)
    t_gate = _wgrad_half(hf, da, core, theirs=True, name="ffn_gate_wgrad_theirs", after=tok, **ffn_shape)
    tok = reduce.step("down", t_gate)
    t_up = _wgrad_half(hf, du, core, theirs=True, name="ffn_up_wgrad_theirs", after=tok, **ffn_shape)
    tok = reduce.send("ffn", {"w_gate": t_gate, "w_up": t_up}, dx3b)
    dhf = _matmul(da, w4["w_gate"], mode="nt", tm=512, tn=1024, tk=N_CHIPS * c_ff, out_dtypes=[F32], name="ffn_gate_bwd", b_blocks=N_CHIPS,
                  after=tok)
    got = reduce.received("ffn", dhf)
    p_gate = _wgrad_half(hf, da, core, theirs=False, name="ffn_gate_wgrad_mine", add=got["w_gate"], **ffn_shape)
    p_up = _wgrad_half(hf, du, core, theirs=False, name="ffn_up_wgrad_mine", add=got["w_up"], **ffn_shape)
    tok = reduce.add_parts("ffn", {"w_gate": p_gate, "w_up": p_up})
    dhf = _matmul(du, w4["w_up"], mode="nt", tm=512, tn=1024, tk=N_CHIPS * c_ff, out_dtypes=[F32], name="ffn_up_bwd", extras=[dhf],
                  epilogue=_add_residual, b_blocks=N_CHIPS, after=tok)
    tok = reduce.step("down", dhf)
    dx2, dx2b, dg_ffn = _rmsnorm_bwd(dhf, x2, small["g_ffn"], dx3, "norm_ffn_bwd")

    d_oc = _matmul(dx2b, w4["w_co"], mode="nt", tm=1024, tn=MEM_WIDTH, tk=d, out_dtypes=[BF16], name="cross_out_bwd",
                   b_blocks=N_CHIPS, after=tok)
    g_co = _matmul(oc, dx2b, mode="tn", tm=MEM_WIDTH, tn=c_d, tk=s, out_dtypes=[BF16], name="cross_out_wgrad", out_blocks=N_CHIPS)
    tok = reduce.step("down", g_co)
    dqc, dkvc = _cross_bwd(qc, kvc, d_oc)
    g_cq = _matmul(hc, dqc, mode="tn", tm=1024, tn=MEM_WIDTH, tk=s, out_dtypes=[BF16], name="cross_q_wgrad", after=tok)
    dhc = _matmul(dqc, w_cq, mode="nt", tm=1024, tn=1024, tk=MEM_WIDTH, out_dtypes=[F32], name="cross_q_bwd")
    g_ckv = _matmul(memn, dkvc, mode="tn", tm=1024, tn=2 * MEM_WIDTH, tk=mems.shape[0], out_dtypes=[BF16], name="cross_kv_wgrad")
    dmemn = _matmul(dkvc, w_ckv, mode="nt", tm=256, tn=1024, tk=2 * MEM_WIDTH, out_dtypes=[F32], name="cross_kv_bwd")
    _, _, dg_mem = _rmsnorm_bwd(dmemn, mems, small["g_mem"], None, "norm_mem_bwd")
    dx1, dx1b, dg_cross = _rmsnorm_bwd(dhc, x1, small["g_cross"], dx2, "norm_cross_bwd")

    dmix = _matmul(dx1b, w_o, mode="nt", tm=512, tn=1024, tk=d, out_dtypes=[F32], name="mix_out_bwd")
    g_o = _matmul(mix, dx1b, mode="tn", tm=1024, tn=1024, tk=s, out_dtypes=[BF16], name="mix_out_wgrad")
    dya, dyc, dz, db_a, db_c = _gate_bwd(z, small["b_gate"], ya, yc, dmix)
    d_attn = _matmul(dya, w4["w_attn_out"], mode="nt", tm=1024, tn=ATTN_WIDTH, tk=d, out_dtypes=[BF16], name="attn_out_bwd",
                     b_blocks=N_CHIPS)
    g_ao = _matmul(attn, dya, mode="tn", tm=ATTN_WIDTH, tn=c_d, tk=s, out_dtypes=[BF16], name="attn_out_wgrad", out_blocks=N_CHIPS)
    d_co = _matmul(dyc, w4["w_conv_out"], mode="nt", tm=1024, tn=CONV_WIDTH, tk=d, out_dtypes=[F32], name="conv_out_bwd",
                   b_blocks=N_CHIPS)
    g_cvo = _matmul(co, dyc, mode="tn", tm=CONV_WIDTH, tn=c_d, tk=s, out_dtypes=[BF16], name="conv_out_wgrad", out_blocks=N_CHIPS)
    tok = reduce.step("ffn", g_cvo)
    tok = reduce.add("mid", {"w_co": g_co, "w_cq": g_cq, "w_ckv": g_ckv, "w_o": g_o, "w_attn_out": g_ao, "w_conv_out": g_cvo}, tok)
    dz, d_conv_w = _conv_bwd(z, conv_w, d_co, dz)
    dq_rot, dk_rot, dv, dsink = _swa_bwd(q_rot, k_rot, v_b, d_attn, small["sink"])
    tok = reduce.step("mid", dq_rot)
    dz = _rope_bwd(dq_rot, dk_rot, dv, cos_t, sin_t, dz)
    in_shape = dict(row_sharded=False, tm=1024, tn=c_in)
    t_in = _wgrad_half(h, dz, core, theirs=True, name="in_proj_wgrad_theirs", after=tok, **in_shape)
    tok = reduce.send("in", {"w_in": t_in}, dk_rot)
    tok = reduce.step("ffn", tok)
    tok = reduce.step("mid", tok)
    got = reduce.received("in", tok)
    p_in = _wgrad_half(h, dz, core, theirs=False, name="in_proj_wgrad_mine", add=got["w_in"], **in_shape)
    tok = reduce.add_parts("in", {"w_in": p_in})
    dh = _matmul(dz, w4["w_in"], mode="nt", tm=512, tn=512, tk=N_CHIPS * c_in, out_dtypes=[F32], name="in_proj_bwd", b_blocks=N_CHIPS,
                 after=tok)
    tok = reduce.step("mid", dh)
    grad_x, _, dg_mix = _rmsnorm_bwd(dh, xs, small["g_mix"], dx1, "norm_mix_bwd")

    small_grads = {
        "g_mix": dg_mix, "sink": dsink[:, 0], "b_gate": jnp.concatenate([db_a, db_c], axis=1), "g_cross": dg_cross,
        "g_mem": dg_mem, "g_ffn": dg_ffn, "g_final": dg_final, "conv_w": d_conv_w,
    }
    return sq, grad_x, small_grads


def _pair_sum(g4, ra, core, name):
    nb, rs, cs = g4.shape
    rh = rs // 2
    tr = _row_tile(rh, 256)
    per = rh // tr

    def body(c_ref, g_ref, r_ref, o_ref):
        o_ref[...] = (g_ref[...].astype(F32) + r_ref[...].astype(F32)).astype(BF16)

    plain = pl.BlockSpec((None, tr, cs), lambda j, i, c: (j, i, 0))
    return pl.pallas_call(
        body, name=name,
        grid_spec=pltpu.PrefetchScalarGridSpec(
            num_scalar_prefetch=1, grid=(nb, per),
            in_specs=[pl.BlockSpec((None, tr, cs), lambda j, i, c: (j, c[0] * per + i, 0)), plain],
            out_specs=plain),
        out_shape=jax.ShapeDtypeStruct((nb, rh, cs), BF16),
        compiler_params=_params(2),
    )(core, g4, ra)


def _quad_sum(parts, rc, place, name):
    _, rh, cs = parts.shape
    tr = _row_tile(rh, 256)
    per = rh // tr

    def body(p_ref, own_ref, r_ref, o_ref):
        acc = own_ref[...].astype(F32)
        for j in range(rc.shape[0]):
            acc = acc + r_ref[j].astype(F32)
        o_ref[...] = acc

    return pl.pallas_call(
        body, name=name,
        grid_spec=pltpu.PrefetchScalarGridSpec(
            num_scalar_prefetch=1, grid=(per,),
            in_specs=[pl.BlockSpec((None, tr, cs), lambda i, p: (p[0], i, 0)),
                      pl.BlockSpec((rc.shape[0], tr, cs), lambda i, p: (0, i, 0))],
            out_specs=pl.BlockSpec((tr, cs), lambda i, p: (p[1] * per + i, 0))),
        out_shape=jax.ShapeDtypeStruct((2 * rh, cs), F32),
        compiler_params=_params(1),
    )(place, parts, rc)


def _adamw_update(w, g, m, v):
    nm = ADAM_B1 * m + (1.0 - ADAM_B1) * g
    nv = ADAM_B2 * v + (1.0 - ADAM_B2) * (g * g)
    m_hat = nm / ADAM_C1
    v_hat = nv / ADAM_C2
    return -ADAM_LR * (m_hat / (jnp.sqrt(v_hat) + ADAM_EPS) + ADAM_WD * w), nm, nv


def _adamw_own_half(w, m, v, parts, rc, place, name, after=None):
    rows, cols = w.shape
    rh = rows // 2
    tr = _row_tile(rh, 256)
    per = rh // tr

    def body(p_ref, w_ref, m_ref, v_ref, own_ref, r_ref, *rest):
        gx_ref, g_ref, d_ref, nm_ref, nv_ref = rest[-5:]
        g = own_ref[...].astype(F32)
        for j in range(rc.shape[0]):
            g = g + r_ref[j].astype(F32)
        gx_ref[...] = g
        g_ref[...] = g
        d_ref[...], nm_ref[...], nv_ref[...] = _adamw_update(w_ref[...], g, m_ref[...], v_ref[...])

    mine = pl.BlockSpec((tr, cols), lambda i, p: (p[1] * per + i, 0))
    shape = jax.ShapeDtypeStruct((rows, cols), F32)
    return pl.pallas_call(
        body, name=name,
        grid_spec=pltpu.PrefetchScalarGridSpec(
            num_scalar_prefetch=1, grid=(per,),
            in_specs=[mine, mine, mine, pl.BlockSpec((None, tr, cols), lambda i, p: (p[0], i, 0)),
                      pl.BlockSpec((rc.shape[0], tr, cols), lambda i, p: (0, i, 0))] + ([] if after is None else [ANY]),
            out_specs=[mine] * 5),
        out_shape=[shape] * 5,
        compiler_params=_params(1),
    )(place, w, m, v, parts, rc, *([] if after is None else [after]))


def _adamw_other_half(w, m, v, g_exchanged, g, delta, new_m, new_v, place, name, after=None):
    rows, cols = w.shape
    rh = rows // 2
    tr = _row_tile(rh, 256)
    per = rh // tr

    def body(p_ref, w_ref, m_ref, v_ref, gx_ref, *rest):
        g_ref, d_ref, nm_ref, nv_ref = rest[-4:]
        gv = gx_ref[...]
        g_ref[...] = gv
        d_ref[...], nm_ref[...], nv_ref[...] = _adamw_update(w_ref[...], gv, m_ref[...], v_ref[...])

    other = pl.BlockSpec((tr, cols), lambda i, p: ((1 - p[1]) * per + i, 0))
    shape = jax.ShapeDtypeStruct((rows, cols), F32)
    n_after = 0 if after is None else 1
    return pl.pallas_call(
        body, name=name,
        grid_spec=pltpu.PrefetchScalarGridSpec(
            num_scalar_prefetch=1, grid=(per,),
            in_specs=[other] * 4 + [ANY] * (4 + n_after),
            out_specs=[other] * 4),
        out_shape=[shape] * 4,
        input_output_aliases={5: 0, 6: 1, 7: 2, 8: 3},
        compiler_params=_params(1),
    )(place, w, m, v, g_exchanged, g, delta, new_m, new_v, *([] if after is None else [after]))


def _cast_to_slot(w, place, dtype, name, after=None):
    rows, cols = w.shape
    tr = _row_tile(rows, 1024)

    def body(p_ref, w_ref, *rest):
        o_ref = rest[-1]
        o_ref[...] = w_ref[...].astype(dtype)

    return pl.pallas_call(
        body, name=name,
        grid_spec=pltpu.PrefetchScalarGridSpec(
            num_scalar_prefetch=1, grid=(rows // tr,),
            in_specs=[pl.BlockSpec((tr, cols), lambda i, p: (i, 0))] + ([] if after is None else [ANY]),
            out_specs=pl.BlockSpec((None, tr, cols), lambda i, p: (p[0], i, 0))),
        out_shape=jax.ShapeDtypeStruct((N_CHIPS, rows, cols), dtype),
        compiler_params=_params(1),
    )(place, w, *([] if after is None else [after]))


def _adamw(w, g, m, v, name, after=None):
    rows, cols = w.shape
    tr = _row_tile(rows, 256)

    def body(w_ref, g_ref, m_ref, v_ref, *rest):
        go_ref, d_ref, nm_ref, nv_ref = rest[-4:]
        gv = g_ref[...]
        go_ref[...] = gv
        d_ref[...], nm_ref[...], nv_ref[...] = _adamw_update(w_ref[...], gv, m_ref[...], v_ref[...])

    tile = pl.BlockSpec((tr, cols), lambda i: (i, 0))
    shape = jax.ShapeDtypeStruct((rows, cols), F32)
    return pl.pallas_call(
        body, name=name, grid=(rows // tr,),
        in_specs=[tile] * 4 + ([] if after is None else [ANY]), out_specs=[tile] * 4, out_shape=[shape] * 4,
        compiler_params=_params(1),
    )(w, g, m, v, *([] if after is None else [after]))


def _mesh_pos():
    return lax.axis_index("x"), lax.axis_index("y"), lax.axis_index("c")


def _other_chips(x, y):
    return [(1 - x, y), (x, 1 - y), (1 - x, 1 - y)]


def _half_rows(ref, which):
    rh = ref.shape[-2] // 2
    return ref.at[pl.ds(which * rh, rh), :]


def _remote(src, dst, send_sems, recv_sems, sem, to):
    return pltpu.make_async_remote_copy(src_ref=src, dst_ref=dst, send_sem=send_sems.at[sem], recv_sem=recv_sems.at[sem],
                                        device_id=to, device_id_type=MESH)


HBM = pl.BlockSpec(memory_space=pltpu.HBM)
SEM = pl.BlockSpec(memory_space=pltpu.SEMAPHORE)
DATAFLOW_EFFECT = pltpu.SideEffectType.DATAFLOW_SIDE_EFFECTING


def _in_hbm(arrays):
    return [pltpu.with_memory_space_constraint(a, pltpu.HBM) for a in arrays]


def _hbm_like(arrays):
    return [pltpu.HBM(a.shape, a.dtype) for a in arrays]


GATHER_COPIES_PER_ARRAY = {"direct": 2, "forward": 2, "pass_near": 2, "pass_far": 1}


def _gather_copies(kind, refs, x, y, c):
    me, near_x, near_y, far = 2 * x + y, 2 * (1 - x) + y, 2 * x + (1 - y), 2 * (1 - x) + (1 - y)
    to_x, to_y, sibling = (1 - x, y, c), (x, 1 - y, c), (x, y, 1 - c)
    out = []
    for ref in refs:
        rh = ref.shape[1] // 2
        rq = rh // 2

        def half(chip, ref=ref, rh=rh):
            return ref.at[chip, pl.ds(c * rh, rh), :]

        def quarter(chip, q, ref=ref, rh=rh, rq=rq):
            return ref.at[chip, pl.ds(c * rh + q * rq, rq), :]

        if kind == "direct":
            out += [(half(me), half(me), to_x), (half(me), half(me), to_y)]
        elif kind == "forward":
            out += [(quarter(near_x, 0), quarter(near_x, 0), to_y), (quarter(near_y, 1), quarter(near_y, 1), to_x)]
        elif kind == "pass_near":
            out += [(half(near_x), half(near_x), sibling), (half(near_y), half(near_y), sibling)]
        else:
            assert kind == "pass_far"
            out += [(half(far), half(far), sibling)]
    return out


def _gather_step(name, bufs, waits, starts, after):
    nb, nw, ns = len(bufs), len(waits), len(starts)
    after = [] if after is None else list(after) if isinstance(after, (list, tuple)) else [after]
    n_after = len(after)

    def body(*refs):
        ins = refs[:nb]
        wait_sems = refs[nb:nb + 2 * nw]
        start_sems = refs[nb + 2 * nw + n_after:nb + 2 * nw + n_after + 2 * ns]
        token = refs[-1]
        x, y, c = _mesh_pos()
        for j, (kind, idxs, _, _) in enumerate(waits):
            for i, (s_ref, d_ref, to) in enumerate(_gather_copies(kind, [ins[t] for t in idxs], x, y, c)):
                came = _remote(s_ref, d_ref, wait_sems[2 * j], wait_sems[2 * j + 1], i, to)
                came.wait_recv()
                came.wait_send()
        for j, (kind, idxs) in enumerate(starts):
            for i, (s_ref, d_ref, to) in enumerate(_gather_copies(kind, [ins[t] for t in idxs], x, y, c)):
                _remote(s_ref, d_ref, start_sems[2 * j], start_sems[2 * j + 1], i, to).start()
        token[...] = jnp.zeros_like(token)

    sems = []
    for kind, idxs in starts:
        sems += [pltpu.SemaphoreType.DMA((GATHER_COPIES_PER_ARRAY[kind] * len(idxs),))] * 2
    operands = _in_hbm(bufs) + [sem for w in waits for sem in w[2:]] + after
    outs = pl.pallas_call(
        body, name=name,
        in_specs=[HBM] * nb + [SEM] * (2 * nw) + [ANY] * n_after,
        out_specs=[SEM] * (2 * ns) + [HBM] * nb + [pl.BlockSpec(memory_space=pltpu.VMEM)],
        out_shape=sems + _hbm_like(bufs) + [jax.ShapeDtypeStruct((8, 128), F32)],
        input_output_aliases={i: 2 * ns + i for i in range(nb)},
        compiler_params=pltpu.CompilerParams(has_side_effects=DATAFLOW_EFFECT),
    )(*operands)
    return outs[2 * ns:2 * ns + nb], [(outs[2 * j], outs[2 * j + 1]) for j in range(ns)], outs[-1]


class _Gather:
    def __init__(self, groups):
        self.groups = groups
        self.bufs = {}
        self.in_flight = {}

    def put(self, slotted):
        self.bufs.update(slotted)

    def step(self, name, waits, starts, after=None):
        names = []
        for _, group in list(waits) + list(starts):
            names += [n for n in self.groups[group] if n not in names]
        index = {n: i for i, n in enumerate(names)}

        def members(group):
            return [index[n] for n in self.groups[group]]

        wait_args = [(kind, members(group)) + self.in_flight.pop((kind, group)) for kind, group in waits]
        start_args = [(kind, members(group)) for kind, group in starts]
        bufs, sems, token = _gather_step(name, [self.bufs[n] for n in names], wait_args, start_args, after)
        self.bufs.update(zip(names, bufs))
        for (kind, group), pair in zip(starts, sems):
            self.in_flight[(kind, group)] = pair
        return token

    def arrays(self, group):
        return {n: self.bufs[n] for n in self.groups[group]}


def _sibling_halves_copies(srcs, dsts, x, y, c):
    out = []
    for s_ref, d_ref in zip(srcs, dsts, strict=True):
        rh = s_ref.shape[1] // 2
        out.append((s_ref.at[:, pl.ds((1 - c) * rh, rh), :], d_ref, (x, y, 1 - c)))
    return out


def _to_sibling_copies(srcs, dsts, x, y, c):
    return [(s_ref, d_ref, (x, y, 1 - c)) for s_ref, d_ref in zip(srcs, dsts, strict=True)]


def _chip_copies(srcs, dsts, x, y, c):
    out = []
    for s_ref, d_ref in zip(srcs, dsts, strict=True):
        for k, (px, py) in enumerate(_other_chips(x, y)):
            out.append((s_ref.at[2 * px + py], d_ref.at[k], (px, py, c)))
    return out


def _join_copies(srcs, dsts, x, y, c):
    out = []
    for s_ref in srcs:
        mine = _half_rows(s_ref, c)
        out.append((mine, mine, (x, y, 1 - c)))
    return out


def _exchange_start(copies_fn, n_copies, srcs, fresh, after, name):
    ns, nb = len(srcs), len(srcs) + len(fresh)

    def body(*refs):
        bufs, send, recv, token = refs[:nb], refs[nb + 1], refs[nb + 2], refs[-1]
        x, y, c = _mesh_pos()
        for i, (s_ref, d_ref, to) in enumerate(copies_fn(bufs[:ns], bufs[ns:] if fresh else bufs[:ns], x, y, c)):
            _remote(s_ref, d_ref, send, recv, i, to).start()
        token[...] = jnp.zeros_like(token)

    sems = [pltpu.SemaphoreType.DMA((n_copies,))] * 2
    outs = pl.pallas_call(
        body, name=name,
        in_specs=[HBM] * nb + [ANY], out_specs=[SEM, SEM] + [HBM] * nb + [pl.BlockSpec(memory_space=pltpu.VMEM)],
        out_shape=sems + _hbm_like(list(srcs) + list(fresh)) + [jax.ShapeDtypeStruct((8, 128), F32)],
        input_output_aliases={i: 2 + i for i in range(nb)},
        compiler_params=pltpu.CompilerParams(has_side_effects=DATAFLOW_EFFECT),
    )(*_in_hbm(list(srcs) + list(fresh)), after)
    return outs[0], outs[1], outs[2:2 + ns], outs[2 + ns:2 + nb], outs[-1]


def _exchange_done(copies_fn, srcs, fresh, send, recv, after, name):
    ns, nb = len(srcs), len(srcs) + len(fresh)

    def body(*refs):
        bufs, send_in, recv_in = refs[:nb], refs[nb], refs[nb + 1]
        x, y, c = _mesh_pos()
        for i, (s_ref, d_ref, to) in enumerate(copies_fn(bufs[:ns], bufs[ns:] if fresh else bufs[:ns], x, y, c)):
            came = _remote(s_ref, d_ref, send_in, recv_in, i, to)
            came.wait_send()
            came.wait_recv()

    outs = pl.pallas_call(
        body, name=name,
        in_specs=[HBM] * nb + [SEM, SEM, ANY], out_specs=[HBM] * nb,
        out_shape=_hbm_like(list(srcs) + list(fresh)),
        input_output_aliases={i: i for i in range(nb)},
        compiler_params=pltpu.CompilerParams(has_side_effects=DATAFLOW_EFFECT),
    )(*_in_hbm(list(srcs) + list(fresh)), send, recv, after)
    return outs[:ns], outs[ns:]


class _Reduce:
    def __init__(self, place, core, shards, mom_m, mom_v):
        self.place, self.core = place, core
        self.shards, self.mom_m, self.mom_v = shards, mom_m, mom_v
        self.state = {}
        self.results = {}

    def add(self, group, grads, after):
        names = list(grads)
        g4s = [g.reshape((N_CHIPS, -1, g.shape[-1])) if g.ndim == 2 else g for g in grads.values()]
        fresh = [lax.empty((N_CHIPS, g.shape[1] // 2, g.shape[2]), BF16) for g in g4s]
        send, recv, g4s, fresh, token = _exchange_start(_sibling_halves_copies, len(names), g4s, fresh, after,
                                                        "pair_start_" + group)
        self.state[group] = (0, names, send, recv, g4s, fresh)
        return token

    def send(self, group, theirs, after):
        names, srcs = list(theirs), list(theirs.values())
        fresh = [lax.empty(s.shape, BF16) for s in srcs]
        send, recv, srcs, fresh, token = _exchange_start(_to_sibling_copies, len(names), srcs, fresh, after, "pair_start_" + group)
        self.state[group] = ("sent", names, send, recv, srcs, fresh)
        return token

    def received(self, group, after):
        stage, names, send, recv, srcs, fresh = self.state.pop(group)
        assert stage == "sent"
        _, got = _exchange_done(_to_sibling_copies, srcs, fresh, send, recv, after, "pair_done_" + group)
        return dict(zip(names, got))

    def add_parts(self, group, parts):
        names, srcs = list(parts), list(parts.values())
        fresh = [lax.empty((N_CHIPS - 1,) + p.shape[1:], BF16) for p in srcs]
        send, recv, srcs, fresh, token = _exchange_start(_chip_copies, 3 * len(names), srcs, fresh, self.core, "chips_start_" + group)
        self.state[group] = (1, names, send, recv, srcs, fresh)
        return token

    def step(self, group, after):
        stage, names, send, recv, srcs, fresh = self.state[group]
        if stage == 0:
            g4s, ras = _exchange_done(_sibling_halves_copies, srcs, fresh, send, recv, after, "pair_done_" + group)
            parts = [_pair_sum(g, r, self.core, "pair_sum_" + n) for g, r, n in zip(g4s, ras, names)]
            fresh = [lax.empty((N_CHIPS - 1,) + p.shape[1:], BF16) for p in parts]
            send, recv, parts, fresh, token = _exchange_start(_chip_copies, 3 * len(names), parts, fresh, self.core,
                                                              "chips_start_" + group)
            self.state[group] = (1, names, send, recv, parts, fresh)
            return token
        if stage == 1:
            parts, rcs = _exchange_done(_chip_copies, srcs, fresh, send, recv, after, "chips_done_" + group)
            token = None
            for n, p, r in zip(names, parts, rcs):
                self.results[n] = _adamw_own_half(self.shards[n], self.mom_m[n], self.mom_v[n], p, r, self.place,
                                                  "adamw_own_" + n, after=token)
                token = self.results[n][2]
            wholes = [self.results[n][0] for n in names]
            send, recv, wholes, _, token = _exchange_start(_join_copies, len(names), wholes, [], token, "join_start_" + group)
            self.state[group] = (2, names, send, recv, wholes, [])
            return token
        assert stage == 2
        wholes, _ = _exchange_done(_join_copies, srcs, [], send, recv, after, "join_done_" + group)
        token = None
        for n, exchanged in zip(names, wholes):
            _, g, d, nm, nv = self.results[n]
            self.results[n] = _adamw_other_half(self.shards[n], self.mom_m[n], self.mom_v[n], exchanged, g, d, nm, nv,
                                                self.place, "adamw_other_" + n, after=token)
            token = self.results[n][1]
        del self.state[group]
        return token


N_DEV = 8


def _all_reduce_small(v):
    def body(v_ref, o_ref, slots, send_sems, recv_sems):
        x, y, c = _mesh_pos()
        me = 4 * x + 2 * y + c
        slots[me] = v_ref[...]
        peers = []
        for r in range(1, N_DEV):
            fx, fy, fc = (r >> 2) & 1, (r >> 1) & 1, r & 1
            peers.append((x + fx - 2 * x * fx, y + fy - 2 * y * fy, c + fc - 2 * c * fc))
        sends = []
        for r, peer in enumerate(peers):
            cp = _remote(v_ref, slots.at[me], send_sems, recv_sems, r, peer)
            cp.start()
            sends.append(cp)
        for r, (px, py, pc) in enumerate(peers):
            landed = slots.at[4 * px + 2 * py + pc]
            _remote(landed, landed, send_sems, recv_sems, r, (px, py, pc)).wait_recv()
        for cp in sends:
            cp.wait_send()
        acc = slots[0]
        for i in range(1, N_DEV):
            acc = acc + slots[i]
        o_ref[...] = acc

    vm = pl.BlockSpec(memory_space=pltpu.VMEM)
    return pl.pallas_call(
        body, name="small_grads_all_reduce",
        in_specs=[vm], out_specs=vm,
        out_shape=jax.ShapeDtypeStruct(v.shape, v.dtype),
        scratch_shapes=[pltpu.VMEM((N_DEV,) + v.shape, v.dtype), pltpu.SemaphoreType.DMA((N_DEV - 1,)),
                        pltpu.SemaphoreType.DMA((N_DEV - 1,))],
    )(v)


MATRICES = ("w_in", "w_attn_out", "w_conv_out", "w_o", "w_cq", "w_ckv", "w_co", "w_gate", "w_up", "w_down")
VECTORS = ("g_mix", "b_gate", "g_cross", "g_mem", "g_ffn", "g_final", "conv_w", "sink")
WEIGHT_ORDER = ("g_mix", "w_in", "sink", "conv_w", "b_gate", "w_attn_out", "w_conv_out", "w_o", "g_cross", "g_mem", "w_cq",
                "w_ckv", "w_co", "g_ffn", "w_gate", "w_up", "w_down", "g_final")
CONV_PAD_ROWS = 32
SMALL_ROWS = 8


def _pack(pieces):
    flat = jnp.concatenate([p.reshape(-1) for p in pieces])
    lane_group = SMALL_ROWS * 128
    total = -(-flat.shape[0] // lane_group) * lane_group
    flat = jnp.pad(flat, (0, total - flat.shape[0]))
    return flat.reshape(SMALL_ROWS, total // SMALL_ROWS), [p.size for p in pieces]


def _unpack(packed, pieces):
    flat = packed.reshape(-1)
    out, off = [], 0
    for p in pieces:
        out.append(flat[off:off + p.size].reshape(p.shape))
        off += p.size
    return out


def kernel(x, mem, g_mix, w_in, sink, conv_w, b_gate, w_attn_out, w_conv_out, w_o, g_cross, g_mem, w_cq, w_ckv, w_co, g_ffn, w_gate, w_up, w_down, g_final, loss_target, m_g_mix, m_w_in, m_sink, m_conv_w, m_b_gate, m_w_attn_out, m_w_conv_out, m_w_o, m_g_cross, m_g_mem, m_w_cq, m_w_ckv, m_w_co, m_g_ffn, m_w_gate, m_w_up, m_w_down, m_g_final, v_g_mix, v_w_in, v_sink, v_conv_w, v_b_gate, v_w_attn_out, v_w_conv_out, v_w_o, v_g_cross, v_g_mem, v_w_cq, v_w_ckv, v_w_co, v_g_ffn, v_w_gate, v_w_up, v_w_down, v_g_final):
    given = dict(g_mix=g_mix, w_in=w_in, sink=sink, conv_w=conv_w, b_gate=b_gate, w_attn_out=w_attn_out, w_conv_out=w_conv_out,
                 w_o=w_o, g_cross=g_cross, g_mem=g_mem, w_cq=w_cq, w_ckv=w_ckv, w_co=w_co, g_ffn=g_ffn, w_gate=w_gate, w_up=w_up,
                 w_down=w_down, g_final=g_final)
    mom_m = dict(g_mix=m_g_mix, w_in=m_w_in, sink=m_sink, conv_w=m_conv_w, b_gate=m_b_gate, w_attn_out=m_w_attn_out,
                 w_conv_out=m_w_conv_out, w_o=m_w_o, g_cross=m_g_cross, g_mem=m_g_mem, w_cq=m_w_cq, w_ckv=m_w_ckv, w_co=m_w_co,
                 g_ffn=m_g_ffn, w_gate=m_w_gate, w_up=m_w_up, w_down=m_w_down, g_final=m_g_final)
    mom_v = dict(g_mix=v_g_mix, w_in=v_w_in, sink=v_sink, conv_w=v_conv_w, b_gate=v_b_gate, w_attn_out=v_w_attn_out,
                 w_conv_out=v_w_conv_out, w_o=v_w_o, g_cross=v_g_cross, g_mem=v_g_mem, w_cq=v_w_cq, w_ckv=v_w_ckv, w_co=v_w_co,
                 g_ffn=v_g_ffn, w_gate=v_w_gate, w_up=v_w_up, w_down=v_w_down, g_final=v_g_final)
    xs, mems, target = x[0], mem[0], loss_target[0]
    d_model = xs.shape[1]
    chip = 2 * lax.axis_index("x") + lax.axis_index("y")
    core = jnp.reshape(lax.axis_index("c"), (1,)).astype(jnp.int32)
    place = jnp.stack([chip, lax.axis_index("c")]).astype(jnp.int32)

    shards = {n: given[n][0] for n in MATRICES}
    conv_cols = conv_w.shape[2]
    conv_pad = jnp.pad(conv_w[0], ((0, CONV_PAD_ROWS - conv_w.shape[1]), (0, 0)))
    fetch = _Gather(GATHER_GROUPS)
    first = {"w_in": _cast_to_slot(shards["w_in"], place, BF16, "to_slot_w_in"),
             "conv_w": _cast_to_slot(conv_pad, place, F32, "to_slot_conv_w")}
    fetch.put(first)
    tok = fetch.step("gather_start", [], [("direct", "in")])
    fetch.put({n: _cast_to_slot(shards[n], place, BF16, "to_slot_" + n, after=tok) for n in MATRICES if n != "w_in"})
    small = {n: given[n] for n in ("g_mix", "b_gate", "g_cross", "g_mem", "g_ffn")}
    small["g_final"] = g_final[None]
    small["sink"] = sink[0]

    reduce = _Reduce(place, core, shards, {n: mom_m[n][0] for n in MATRICES}, {n: mom_v[n][0] for n in MATRICES})
    sq, grad_x, small_grads = _local_step(xs, mems, target, small, fetch, reduce)

    loss_part = 0.5 * sq[0:1, 0:1] / d_model
    pieces = [small_grads[n] for n in VECTORS] + [loss_part]
    packed, _ = _pack(pieces)
    summed = _unpack(_all_reduce_small(packed), pieces)
    loss = summed[-1][0, 0]
    small_sum = dict(zip(VECTORS, summed[:-1]))
    small_sum["conv_w"] = lax.dynamic_slice_in_dim(small_sum["conv_w"], chip * conv_cols, conv_cols, axis=1)

    grad_out, delta, new_m, new_v = {}, {}, {}, {}
    like = [given[n] for n in VECTORS]
    pw, _ = _pack(like)
    pg, _ = _pack([small_sum[n] for n in VECTORS])
    pm, _ = _pack([mom_m[n] for n in VECTORS])
    pv, _ = _pack([mom_v[n] for n in VECTORS])
    tok = reduce.step("in", pg)
    _, pd, pnm, pnv = _adamw(pw, pg, pm, pv, "adamw_small", after=tok)
    for n, g, d, nm, nv in zip(VECTORS, [small_sum[n] for n in VECTORS], _unpack(pd, like), _unpack(pnm, like), _unpack(pnv, like)):
        grad_out[n] = g.reshape(given[n].shape)
        delta[n], new_m[n], new_v[n] = d, nm, nv
    reduce.step("in", pd)
    for n in MATRICES:
        g, d, nm, nv = reduce.results[n]
        grad_out[n], delta[n], new_m[n], new_v[n] = g[None], d[None], nm[None], nv[None]

    return (loss, grad_x[None], *[grad_out[n] for n in WEIGHT_ORDER], *[delta[n] for n in WEIGHT_ORDER],
            *[new_m[n] for n in WEIGHT_ORDER], *[new_v[n] for n in WEIGHT_ORDER])
```

```python
import functools

import jax
import jax.numpy as jnp
from jax import lax
from jax.experimental import pallas as pl
from jax.experimental.pallas import tpu as pltpu
from jax.experimental.pallas import tpu_sc as plsc

F32 = jnp.float32
BF16 = jnp.bfloat16
MESH = pl.DeviceIdType.MESH
ANY = pl.BlockSpec(memory_space=pl.ANY)

VMEM_LIMIT_BYTES = 56 * 1024 * 1024

N_CHIPS = 4
HEAD_DIM = 128
N_Q_HEADS = 8
N_KV_HEADS = 2
Q_GROUP = N_Q_HEADS // N_KV_HEADS
ATTN_WIDTH = N_Q_HEADS * HEAD_DIM
KV_WIDTH = N_KV_HEADS * HEAD_DIM
WINDOW = 128
BLOCK = 128
BAND = 3 * BLOCK
ROPE_THETA = 10000.0
CONV_WIDTH = 1024
MEM_HEADS = 4
MEM_WIDTH = MEM_HEADS * HEAD_DIM
RMS_EPS = 1e-6
NEG_INF = -1e30
ATTN_SCALE = HEAD_DIM ** -0.5

Q_OFF, K_OFF, V_OFF, CU_OFF, CB_OFF, CC_OFF, GL_OFF = 0, 1024, 1280, 1536, 2560, 3584, 4608

ADAM_LR = 0.001
ADAM_B1 = 0.9
ADAM_B2 = 0.999
ADAM_EPS = 1e-08
ADAM_WD = 0.01
ADAM_STEP = 10
ADAM_C1 = 1.0 - ADAM_B1 ** ADAM_STEP
ADAM_C2 = 1.0 - ADAM_B2 ** ADAM_STEP


def _params(n_grid_axes):
    return pltpu.CompilerParams(dimension_semantics=("arbitrary",) * n_grid_axes, vmem_limit_bytes=VMEM_LIMIT_BYTES)


BF16_SUBLANES = 16


def _row_tile(rows, want):
    if rows <= want:
        return rows
    for t in range(want, 0, -BF16_SUBLANES):
        if rows % t == 0:
            return t
    return rows


def _matmul(a, b, *, mode, tm, tn, tk, out_dtypes, name, extras=(), epilogue=None, b_blocks=1, out_blocks=1, after=None):
    if mode == "tn":
        kdim, m = a.shape
    else:
        m, kdim = a.shape
    if b_blocks > 1:
        nb, brows, bcols = b.shape
        assert nb == b_blocks
        if mode == "nn":
            n = bcols * nb
            assert brows == kdim
        else:
            assert mode == "nt" and bcols * nb == kdim
            n = brows
    else:
        n = b.shape[0] if mode == "nt" else b.shape[1]
    tm, tn = min(tm, m), min(tn, n)
    assert m % tm == 0 and n % tn == 0 and tk == kdim, (name, m, n, kdim, tm, tn, tk)
    n_extra, n_out = len(extras), len(out_dtypes)
    n_after = 0 if after is None else 1

    if mode == "tn":
        a_spec = pl.BlockSpec((tk, tm), lambda j, i, k: (k, i))
        dims = (((0,), (0,)), ((), ()))
    else:
        a_spec = pl.BlockSpec((tm, tk), lambda j, i, k: (i, k))
        dims = (((1,), (0,)), ((), ())) if mode == "nn" else (((1,), (1,)), ((), ()))

    if b_blocks > 1 and mode == "nn":
        per = b.shape[2] // tn
        assert b.shape[2] % tn == 0
        b_spec = pl.BlockSpec((None, tk, tn), lambda j, i, k: (j // per, k, j % per))
    elif b_blocks > 1:
        b_spec = pl.BlockSpec((b_blocks, tn, b.shape[2]), lambda j, i, k: (0, j, 0))
    elif mode == "nt":
        b_spec = pl.BlockSpec((tn, tk), lambda j, i, k: (j, k))
    else:
        b_spec = pl.BlockSpec((tk, tn), lambda j, i, k: (k, j))

    tile_spec = pl.BlockSpec((tm, tn), lambda j, i, k: (i, j))
    if out_blocks > 1:
        ncols = n // out_blocks
        assert ncols % tn == 0
        oper = ncols // tn
        out_spec = pl.BlockSpec((None, tm, tn), lambda j, i, k: (j // oper, i, j % oper))
        out_shape = [jax.ShapeDtypeStruct((out_blocks, m, ncols), dt) for dt in out_dtypes]
    else:
        out_spec = tile_spec
        out_shape = [jax.ShapeDtypeStruct((m, n), dt) for dt in out_dtypes]

    def body(a_ref, b_ref, *rest):
        extra_refs = rest[:n_extra]
        out_refs = rest[n_extra + n_after:n_extra + n_after + n_out]
        if mode == "nt" and b_blocks > 1:
            cs = b.shape[2]
            acc = None
            for jb in range(b_blocks):
                prod = lax.dot_general(a_ref[:, jb * cs:(jb + 1) * cs].astype(BF16), b_ref[jb].astype(BF16), dims,
                                       preferred_element_type=F32)
                acc = prod if acc is None else acc + prod
        else:
            acc = lax.dot_general(a_ref[...].astype(BF16), b_ref[...].astype(BF16), dims, preferred_element_type=F32)
        tiles = (acc,) if epilogue is None else epilogue(acc, *[r[...] for r in extra_refs])
        for o_ref, t in zip(out_refs, tiles, strict=True):
            o_ref[...] = t.astype(o_ref.dtype)

    outs = pl.pallas_call(
        body,
        name=name,
        grid=(n // tn, m // tm, 1),
        in_specs=[a_spec, b_spec] + [tile_spec] * n_extra + [ANY] * n_after,
        out_specs=[out_spec] * n_out,
        out_shape=out_shape,
        compiler_params=_params(3),
    )(a, b, *extras, *([] if after is None else [after]))
    return outs[0] if n_out == 1 else outs


def _add_residual(acc, res):
    return (acc + res,)


def _matmul_column_blocks(a, b4, blocks, out, *, tm, name, after=None):
    m, kdim = a.shape
    nb, _, cols = b4.shape
    tm = min(tm, m)
    assert m % tm == 0

    def body(j_ref, a_ref, b_ref, *rest):
        rest[-1][...] = jnp.dot(a_ref[...], b_ref[...], preferred_element_type=F32)

    extra = ([] if out is None else [out]) + ([] if after is None else [after])
    n_blocks = blocks.shape[0]
    return pl.pallas_call(
        body, name=name,
        grid_spec=pltpu.PrefetchScalarGridSpec(
            num_scalar_prefetch=1, grid=(n_blocks, m // tm),
            in_specs=[pl.BlockSpec((tm, kdim), lambda j, i, blk: (i, 0)),
                      pl.BlockSpec((None, kdim, cols), lambda j, i, blk: (blk[j], 0, 0))] + [ANY] * len(extra),
            out_specs=pl.BlockSpec((tm, cols), lambda j, i, blk: (i, blk[j]))),
        out_shape=jax.ShapeDtypeStruct((m, nb * cols), F32),
        input_output_aliases={} if out is None else {3: 0},
        compiler_params=_params(2),
    )(blocks, a, b4, *extra)


def _wgrad_half(a, b, core, *, theirs, row_sharded, tm, tn, name, add=None, after=None):
    kdim, m = a.shape
    n = b.shape[1]
    rs, cs = (m // N_CHIPS, n) if row_sharded else (m, n // N_CHIPS)
    rh = rs // 2
    tm, tn = min(tm, rh), min(tn, cs)
    assert rh % tm == 0 and cs % tn == 0, (name, rh, cs, tm, tn)
    mh, per = rh // tm, cs // tn
    has_add = add is not None

    def half(c):
        return 1 - c[0] if theirs else c[0]

    if row_sharded:
        grid = (n // tn, N_CHIPS * mh)
        a_spec = pl.BlockSpec((kdim, tm), lambda j, r, c: (0, ((r // mh) * 2 + half(c)) * mh + r % mh))
        o_spec = pl.BlockSpec((None, tm, tn), lambda j, r, c: (r // mh, r % mh, j))
    else:
        grid = (n // tn, mh)
        a_spec = pl.BlockSpec((kdim, tm), lambda j, r, c: (0, half(c) * mh + r))
        o_spec = pl.BlockSpec((None, tm, tn), lambda j, r, c: (j // per, r, j % per))
    b_spec = pl.BlockSpec((kdim, tn), lambda j, r, c: (0, j))

    def body(c_ref, a_ref, b_ref, *rest):
        o_ref = rest[-1]
        acc = lax.dot_general(a_ref[...].astype(BF16), b_ref[...].astype(BF16), (((0,), (0,)), ((), ())),
                              preferred_element_type=F32)
        if has_add:
            acc = acc + rest[0][...].astype(F32)
        o_ref[...] = acc.astype(BF16)

    operands = [a, b] + ([add] if has_add else []) + ([] if after is None else [after])
    return pl.pallas_call(
        body, name=name,
        grid_spec=pltpu.PrefetchScalarGridSpec(
            num_scalar_prefetch=1, grid=grid,
            in_specs=[a_spec, b_spec] + ([o_spec] if has_add else []) + ([] if after is None else [ANY]),
            out_specs=o_spec),
        out_shape=jax.ShapeDtypeStruct((N_CHIPS, rh, cs), BF16),
        compiler_params=_params(2),
    )(core, *operands)


def _rstd(x):
    return lax.rsqrt(jnp.mean(x * x, axis=-1, keepdims=True) + RMS_EPS)


def _rmsnorm(x, g, name):
    s, d = x.shape
    tr = _row_tile(s, 256)

    def body(x_ref, g_ref, o_ref):
        xv = x_ref[...]
        o_ref[...] = (xv * _rstd(xv) * g_ref[...]).astype(BF16)

    return pl.pallas_call(
        body, name=name, grid=(s // tr,),
        in_specs=[pl.BlockSpec((tr, d), lambda i: (i, 0)), pl.BlockSpec((1, d), lambda i: (0, 0))],
        out_specs=pl.BlockSpec((tr, d), lambda i: (i, 0)),
        out_shape=jax.ShapeDtypeStruct((s, d), BF16),
        compiler_params=_params(1),
    )(x, g)


def _rmsnorm_bwd(dh, x, g, dres, name):
    s, d = x.shape
    tr = _row_tile(s, 256)
    has_res = dres is not None

    def body(*refs):
        if has_res:
            dh_ref, x_ref, g_ref, res_ref, dx_ref, dxb_ref, dg_ref = refs
        else:
            dh_ref, x_ref, g_ref, dx_ref, dxb_ref, dg_ref = refs
        xv = x_ref[...]
        dhv = dh_ref[...].astype(F32)
        r = _rstd(xv)
        xn = xv * r
        dhg = dhv * g_ref[...]
        dx = r * (dhg - xn * jnp.mean(dhg * xn, axis=-1, keepdims=True))
        if has_res:
            dx = dx + res_ref[...]
        dx_ref[...] = dx
        dxb_ref[...] = dx.astype(BF16)
        part = jnp.sum(dhv * xn, axis=0, keepdims=True)

        @pl.when(pl.program_id(0) == 0)
        def _():
            dg_ref[...] = part

        @pl.when(pl.program_id(0) > 0)
        def _():
            dg_ref[...] += part

    row = pl.BlockSpec((tr, d), lambda i: (i, 0))
    vec = pl.BlockSpec((1, d), lambda i: (0, 0))
    return pl.pallas_call(
        body, name=name, grid=(s // tr,),
        in_specs=[row, row, vec] + ([row] if has_res else []),
        out_specs=[row, row, vec],
        out_shape=[jax.ShapeDtypeStruct((s, d), F32), jax.ShapeDtypeStruct((s, d), BF16), jax.ShapeDtypeStruct((1, d), F32)],
        compiler_params=_params(1),
    )(*([dh, x, g] + ([dres] if has_res else [])))


def _loss_head(x3, g, target):
    s, d = x3.shape
    tr = _row_tile(s, 256)

    def body(x_ref, g_ref, t_ref, dx_ref, dxb_ref, sq_ref, dg_ref):
        xv = x_ref[...]
        gv = g_ref[...]
        r = _rstd(xv)
        xn = xv * r
        err = xn * gv - t_ref[...]
        dy = err * (1.0 / d)
        dyg = dy * gv
        dx = r * (dyg - xn * jnp.mean(dyg * xn, axis=-1, keepdims=True))
        dx_ref[...] = dx
        dxb_ref[...] = dx.astype(BF16)
        sq = jnp.sum(jnp.sum(err * err, axis=1, keepdims=True), axis=0, keepdims=True)
        sq = jnp.broadcast_to(sq, (1, 128))
        part = jnp.sum(dy * xn, axis=0, keepdims=True)

        @pl.when(pl.program_id(0) == 0)
        def _():
            sq_ref[...] = sq
            dg_ref[...] = part

        @pl.when(pl.program_id(0) > 0)
        def _():
            sq_ref[...] += sq
            dg_ref[...] += part

    row = pl.BlockSpec((tr, d), lambda i: (i, 0))
    vec = pl.BlockSpec((1, d), lambda i: (0, 0))
    return pl.pallas_call(
        body, name="loss_head", grid=(s // tr,),
        in_specs=[row, vec, row],
        out_specs=[row, row, pl.BlockSpec((1, 128), lambda i: (0, 0)), vec],
        out_shape=[jax.ShapeDtypeStruct((s, d), F32), jax.ShapeDtypeStruct((s, d), BF16),
                   jax.ShapeDtypeStruct((1, 128), F32), jax.ShapeDtypeStruct((1, d), F32)],
        compiler_params=_params(1),
    )(x3, g, target)


def _rope_tables(s):
    inv = 1.0 / (ROPE_THETA ** (jnp.arange(0, HEAD_DIM, 2, dtype=F32) / HEAD_DIM))
    ang = jnp.arange(s, dtype=F32)[:, None] * inv[None, :]
    cos, sin = jnp.cos(ang), jnp.sin(ang)
    return jnp.concatenate([cos, cos], axis=1), jnp.concatenate([-sin, sin], axis=1)


def _swap_halves(t):
    return pltpu.roll(t, HEAD_DIM // 2, 1)


def _rope_fwd(z, cos_t, sin_t):
    s = z.shape[0]
    tr = _row_tile(s, 256)

    def body(zq_ref, zk_ref, zv_ref, c_ref, s_ref, q_ref, k_ref, v_ref):
        c, sn = c_ref[...], s_ref[...]
        for hd in range(N_Q_HEADS):
            cols = slice(hd * HEAD_DIM, (hd + 1) * HEAD_DIM)
            t = zq_ref[:, cols]
            q_ref[:, cols] = (t * c + _swap_halves(t) * sn).astype(BF16)
        for hd in range(N_KV_HEADS):
            cols = slice(hd * HEAD_DIM, (hd + 1) * HEAD_DIM)
            t = zk_ref[:, cols]
            k_ref[:, cols] = (t * c + _swap_halves(t) * sn).astype(BF16)
        v_ref[...] = zv_ref[...].astype(BF16)

    tab = pl.BlockSpec((tr, HEAD_DIM), lambda i: (i, 0))
    return pl.pallas_call(
        body, name="rope_fwd", grid=(s // tr,),
        in_specs=[pl.BlockSpec((tr, ATTN_WIDTH), lambda i: (i, Q_OFF // ATTN_WIDTH)),
                  pl.BlockSpec((tr, KV_WIDTH), lambda i: (i, K_OFF // KV_WIDTH)),
                  pl.BlockSpec((tr, KV_WIDTH), lambda i: (i, V_OFF // KV_WIDTH)), tab, tab],
        out_specs=[pl.BlockSpec((tr, ATTN_WIDTH), lambda i: (i, 0)), pl.BlockSpec((tr, KV_WIDTH), lambda i: (i, 0)),
                   pl.BlockSpec((tr, KV_WIDTH), lambda i: (i, 0))],
        out_shape=[jax.ShapeDtypeStruct((s, ATTN_WIDTH), BF16), jax.ShapeDtypeStruct((s, KV_WIDTH), BF16),
                   jax.ShapeDtypeStruct((s, KV_WIDTH), BF16)],
        compiler_params=_params(1),
    )(z, z, z, cos_t, sin_t)


def _rope_bwd(dq_rot, dk_rot, dv, cos_t, sin_t, dz):
    s = dq_rot.shape[0]
    tr = _row_tile(s, 256)
    qkv_width = V_OFF + KV_WIDTH

    def body(dq_ref, dk_ref, dv_ref, c_ref, s_ref, dz_in_ref, o_ref):
        c, sn = c_ref[...], s_ref[...]
        for hd in range(N_Q_HEADS):
            t = dq_ref[:, hd * HEAD_DIM:(hd + 1) * HEAD_DIM]
            o_ref[:, Q_OFF + hd * HEAD_DIM:Q_OFF + (hd + 1) * HEAD_DIM] = (t * c + _swap_halves(t * sn)).astype(BF16)
        for hd in range(N_KV_HEADS):
            t = dk_ref[:, hd * HEAD_DIM:(hd + 1) * HEAD_DIM]
            o_ref[:, K_OFF + hd * HEAD_DIM:K_OFF + (hd + 1) * HEAD_DIM] = (t * c + _swap_halves(t * sn)).astype(BF16)
        o_ref[:, V_OFF:V_OFF + KV_WIDTH] = dv_ref[...].astype(BF16)

    tab = pl.BlockSpec((tr, HEAD_DIM), lambda i: (i, 0))
    wide = pl.BlockSpec((tr, ATTN_WIDTH), lambda i: (i, 0))
    narrow = pl.BlockSpec((tr, KV_WIDTH), lambda i: (i, 0))
    return pl.pallas_call(
        body, name="rope_bwd", grid=(s // tr,),
        in_specs=[wide, narrow, narrow, tab, tab, ANY],
        out_specs=pl.BlockSpec((tr, qkv_width), lambda i: (i, 0)),
        out_shape=jax.ShapeDtypeStruct(dz.shape, dz.dtype),
        input_output_aliases={5: 0},
        compiler_params=_params(1),
    )(dq_rot, dk_rot, dv, cos_t, sin_t, dz)


def _swa_band(i, s):
    return pl.multiple_of(jnp.clip((i - 1) * BLOCK, 0, s - BAND), BLOCK)


SWA_HEADS_PER_PASS = Q_GROUP


def _swa_probs(q_ref, k_ref, sink_ref, heads, start, valid):
    kv = heads[0] // Q_GROUP
    cols = slice(kv * HEAD_DIM, (kv + 1) * HEAD_DIM)
    kb = k_ref[pl.ds(start, BAND), cols]
    qg = jnp.concatenate([q_ref[:, hd * HEAD_DIM:(hd + 1) * HEAD_DIM] for hd in heads], axis=0)
    sc = lax.dot_general(qg, kb, (((1,), (1,)), ((), ())), preferred_element_type=F32) * ATTN_SCALE
    sc = jnp.where(valid, sc, NEG_INF)
    sk = jnp.concatenate([jnp.full((BLOCK, 1), sink_ref[hd], F32) for hd in heads], axis=0)
    mx = jnp.maximum(jnp.max(sc, axis=1, keepdims=True), sk)
    e = jnp.exp(sc - mx)
    es = jnp.exp(sk - mx)
    inv = 1.0 / (jnp.sum(e, axis=1, keepdims=True) + es)
    return qg, kb, e * inv, es * inv


def _swa_head_passes():
    return [list(range(h0, h0 + SWA_HEADS_PER_PASS)) for h0 in range(0, N_Q_HEADS, SWA_HEADS_PER_PASS)]


def _swa_valid(i, start):
    q_pos = i * BLOCK + lax.broadcasted_iota(jnp.int32, (BLOCK, 1), 0)
    q_pos = jnp.concatenate([q_pos] * SWA_HEADS_PER_PASS, axis=0)
    k_pos = start + lax.broadcasted_iota(jnp.int32, (1, BAND), 1)
    return jnp.abs(k_pos - q_pos) <= WINDOW


def _swa_fwd(q, k, v, sink):
    s = q.shape[0]
    assert s % BLOCK == 0 and s >= BAND

    def body(sink_ref, q_ref, k_ref, v_ref, o_ref):
        i = pl.program_id(0)
        start = _swa_band(i, s)
        valid = _swa_valid(i, start)
        for heads in _swa_head_passes():
            kv = heads[0] // Q_GROUP
            _, _, p, _ = _swa_probs(q_ref, k_ref, sink_ref, heads, start, valid)
            vb = v_ref[pl.ds(start, BAND), kv * HEAD_DIM:(kv + 1) * HEAD_DIM]
            o = jnp.dot(p.astype(BF16), vb, preferred_element_type=F32)
            for g, hd in enumerate(heads):
                o_ref[:, hd * HEAD_DIM:(hd + 1) * HEAD_DIM] = o[g * BLOCK:(g + 1) * BLOCK].astype(BF16)

    whole = pl.BlockSpec((s, KV_WIDTH), lambda i: (0, 0))
    blk = pl.BlockSpec((BLOCK, ATTN_WIDTH), lambda i: (i, 0))
    return pl.pallas_call(
        body, name="swa_fwd", grid=(s // BLOCK,),
        in_specs=[pl.BlockSpec(memory_space=pltpu.SMEM), blk, whole, whole],
        out_specs=blk,
        out_shape=jax.ShapeDtypeStruct((s, ATTN_WIDTH), BF16),
        compiler_params=_params(1),
    )(sink, q, k, v)


def _swa_bwd(q, k, v, d_out, sink):
    s = q.shape[0]

    def body(sink_ref, q_ref, k_ref, v_ref, do_ref, dq_ref, dk_ref, dv_ref, dsink_ref):
        i = pl.program_id(0)

        @pl.when(i == 0)
        def _():
            dk_ref[...] = jnp.zeros_like(dk_ref)
            dv_ref[...] = jnp.zeros_like(dv_ref)
            dsink_ref[...] = jnp.zeros_like(dsink_ref)

        start = _swa_band(i, s)
        valid = _swa_valid(i, start)
        for heads in _swa_head_passes():
            kv = heads[0] // Q_GROUP
            cols = slice(kv * HEAD_DIM, (kv + 1) * HEAD_DIM)
            qg, kb, p, p_sink = _swa_probs(q_ref, k_ref, sink_ref, heads, start, valid)
            vb = v_ref[pl.ds(start, BAND), cols]
            dog = jnp.concatenate([do_ref[:, hd * HEAD_DIM:(hd + 1) * HEAD_DIM] for hd in heads], axis=0)
            dp = lax.dot_general(dog, vb, (((1,), (1,)), ((), ())), preferred_element_type=F32)
            delta = jnp.sum(p * dp, axis=1, keepdims=True)
            ds = (p * (dp - delta) * ATTN_SCALE).astype(BF16)
            dqg = jnp.dot(ds, kb, preferred_element_type=F32)
            dk_ref[pl.ds(start, BAND), cols] += lax.dot_general(ds, qg, (((0,), (0,)), ((), ())), preferred_element_type=F32)
            dv_ref[pl.ds(start, BAND), cols] += lax.dot_general(p.astype(BF16), dog, (((0,), (0,)), ((), ())),
                                                                 preferred_element_type=F32)
            dsk = p_sink * delta
            for g, hd in enumerate(heads):
                dq_ref[:, hd * HEAD_DIM:(hd + 1) * HEAD_DIM] = dqg[g * BLOCK:(g + 1) * BLOCK]
                tot = jnp.sum(dsk[g * BLOCK:(g + 1) * BLOCK], axis=0, keepdims=True)
                dsink_ref[hd:hd + 1, :] -= jnp.broadcast_to(tot, (1, 128))

    whole = pl.BlockSpec((s, KV_WIDTH), lambda i: (0, 0))
    blk = pl.BlockSpec((BLOCK, ATTN_WIDTH), lambda i: (i, 0))
    return pl.pallas_call(
        body, name="swa_bwd", grid=(s // BLOCK,),
        in_specs=[pl.BlockSpec(memory_space=pltpu.SMEM), blk, whole, whole, blk],
        out_specs=[blk, whole, whole, pl.BlockSpec((N_Q_HEADS, 128), lambda i: (0, 0))],
        out_shape=[jax.ShapeDtypeStruct((s, ATTN_WIDTH), F32), jax.ShapeDtypeStruct((s, KV_WIDTH), F32),
                   jax.ShapeDtypeStruct((s, KV_WIDTH), F32), jax.ShapeDtypeStruct((N_Q_HEADS, 128), F32)],
        compiler_params=_params(1),
    )(sink, q, k, v, d_out)


CONV_CHUNK = 256


def _shift_rows(t, rows, down):
    n = t.shape[0]
    rolled = pltpu.roll(t, 1 if down else n - 1, 0)
    edge = 0 if down else n - 1
    return jnp.where(rows == edge, 0.0, rolled)


def _conv_specs(s):
    def z_spec(off):
        return pl.BlockSpec((s, CONV_CHUNK), lambda j, off=off: (0, off // CONV_CHUNK + j))
    chunk = pl.BlockSpec((s, CONV_CHUNK), lambda j: (0, j))
    w_spec = pl.BlockSpec((3, CONV_CHUNK), lambda j: (0, j))
    return z_spec(CU_OFF), z_spec(CB_OFF), z_spec(CC_OFF), chunk, w_spec


def _conv_fwd(z, conv_w):
    s = z.shape[0]
    cu_spec, cb_spec, cc_spec, chunk, w_spec = _conv_specs(s)

    def body(cu_ref, cb_ref, cc_ref, w_ref, o_ref):
        rows = lax.broadcasted_iota(jnp.int32, (s, 1), 0)
        t = cc_ref[...] * cu_ref[...]
        c3 = _shift_rows(t, rows, True) * w_ref[0:1, :] + t * w_ref[1:2, :] + _shift_rows(t, rows, False) * w_ref[2:3, :]
        o_ref[...] = (cb_ref[...] * c3).astype(BF16)

    return pl.pallas_call(
        body, name="conv_fwd", grid=(CONV_WIDTH // CONV_CHUNK,),
        in_specs=[cu_spec, cb_spec, cc_spec, w_spec],
        out_specs=chunk,
        out_shape=jax.ShapeDtypeStruct((s, CONV_WIDTH), BF16),
        compiler_params=_params(1),
    )(z, z, z, conv_w)


def _conv_bwd(z, conv_w, d_co, dz):
    s = z.shape[0]
    cu_spec, cb_spec, cc_spec, chunk, w_spec = _conv_specs(s)
    n_chunks = CONV_WIDTH // CONV_CHUNK
    offsets = (CU_OFF, CB_OFF, CC_OFF)

    def body(cu_ref, cb_ref, cc_ref, w_ref, d_ref, dz_in_ref, dz_ref, dw_ref, buf, sems):
        j = pl.program_id(0)

        def copies(j_at):
            return [pltpu.make_async_copy(buf.at[h], dz_ref.at[:, pl.ds(off + j_at * CONV_CHUNK, CONV_CHUNK)], sems.at[h])
                    for h, off in enumerate(offsets)]

        rows = lax.broadcasted_iota(jnp.int32, (s, 1), 0)
        cu, cc = cu_ref[...], cc_ref[...]
        t = cc * cu
        t_dn, t_up = _shift_rows(t, rows, True), _shift_rows(t, rows, False)
        c3 = t_dn * w_ref[0:1, :] + t * w_ref[1:2, :] + t_up * w_ref[2:3, :]
        d = d_ref[...]
        dc3 = d * cb_ref[...]
        dw_ref[0:1, :] = jnp.sum(dc3 * t_dn, axis=0, keepdims=True)
        dw_ref[1:2, :] = jnp.sum(dc3 * t, axis=0, keepdims=True)
        dw_ref[2:3, :] = jnp.sum(dc3 * t_up, axis=0, keepdims=True)
        dt = _shift_rows(dc3, rows, False) * w_ref[0:1, :] + dc3 * w_ref[1:2, :] + _shift_rows(dc3, rows, True) * w_ref[2:3, :]

        @pl.when(j > 0)
        def _():
            for cp in copies(j):
                cp.wait()

        buf[0] = (dt * cc).astype(BF16)
        buf[1] = (d * c3).astype(BF16)
        buf[2] = (dt * cu).astype(BF16)
        for cp in copies(j):
            cp.start()

        @pl.when(j == n_chunks - 1)
        def _():
            for cp in copies(j):
                cp.wait()

    return pl.pallas_call(
        body, name="conv_bwd", grid=(n_chunks,),
        in_specs=[cu_spec, cb_spec, cc_spec, w_spec, chunk, ANY],
        out_specs=[ANY, w_spec],
        out_shape=[jax.ShapeDtypeStruct(dz.shape, dz.dtype), jax.ShapeDtypeStruct((3, CONV_WIDTH), F32)],
        input_output_aliases={5: 0},
        scratch_shapes=[pltpu.VMEM((3, s, CONV_CHUNK), BF16), pltpu.SemaphoreType.DMA((3,))],
        compiler_params=_params(1),
    )(z, z, z, conv_w, d_co, dz)


GATE_CHUNK = 512


def _gate_specs(s, d, tr):
    n_chunks = d // GATE_CHUNK
    za = pl.BlockSpec((tr, GATE_CHUNK), lambda j, i: (i, GL_OFF // GATE_CHUNK + j))
    zc = pl.BlockSpec((tr, GATE_CHUNK), lambda j, i: (i, GL_OFF // GATE_CHUNK + n_chunks + j))
    ba = pl.BlockSpec((1, GATE_CHUNK), lambda j, i: (0, j))
    bc = pl.BlockSpec((1, GATE_CHUNK), lambda j, i: (0, n_chunks + j))
    tile = pl.BlockSpec((tr, GATE_CHUNK), lambda j, i: (i, j))
    return za, zc, ba, bc, tile


def _gate_fwd(z, b_gate, ya, yc):
    s, d = ya.shape
    tr = _row_tile(s, 512)
    za, zc, ba, bc, tile = _gate_specs(s, d, tr)

    def body(za_ref, zc_ref, ba_ref, bc_ref, ya_ref, yc_ref, o_ref):
        ga = jax.nn.sigmoid(za_ref[...] + ba_ref[...])
        gc = jax.nn.sigmoid(zc_ref[...] + bc_ref[...])
        o_ref[...] = (ga * ya_ref[...] + gc * yc_ref[...]).astype(BF16)

    return pl.pallas_call(
        body, name="gate_fwd", grid=(d // GATE_CHUNK, s // tr),
        in_specs=[za, zc, ba, bc, tile, tile],
        out_specs=tile,
        out_shape=jax.ShapeDtypeStruct((s, d), BF16),
        compiler_params=_params(2),
    )(z, z, b_gate, b_gate, ya, yc)


def _gate_bwd(z, b_gate, ya, yc, dmix):
    s, d = ya.shape
    tr = _row_tile(s, 512)
    za, zc, ba, bc, tile = _gate_specs(s, d, tr)
    vec = pl.BlockSpec((1, GATE_CHUNK), lambda j, i: (0, j))
    n_rows = s // tr
    in_width = z.shape[1]

    def body(za_ref, zc_ref, ba_ref, bc_ref, ya_ref, yc_ref, dm_ref, dya_ref, dyc_ref, dz_ref, dba_ref, dbc_ref, buf, sems):
        j, i = pl.program_id(0), pl.program_id(1)

        def copies(j_at, i_at):
            rows = pl.ds(i_at * tr, tr)
            return [pltpu.make_async_copy(buf.at[h], dz_ref.at[rows, pl.ds(GL_OFF + h * d + j_at * GATE_CHUNK, GATE_CHUNK)],
                                          sems.at[h]) for h in range(2)]

        ga = jax.nn.sigmoid(za_ref[...] + ba_ref[...])
        gc = jax.nn.sigmoid(zc_ref[...] + bc_ref[...])
        dm = dm_ref[...]
        dya_ref[...] = (dm * ga).astype(BF16)
        dyc_ref[...] = (dm * gc).astype(BF16)
        dla = dm * ya_ref[...] * ga * (1.0 - ga)
        dlc = dm * yc_ref[...] * gc * (1.0 - gc)

        @pl.when(j * n_rows + i > 0)
        def _():
            for cp in copies(j, i):
                cp.wait()

        buf[0] = dla.astype(BF16)
        buf[1] = dlc.astype(BF16)
        for cp in copies(j, i):
            cp.start()

        @pl.when((j == d // GATE_CHUNK - 1) & (i == n_rows - 1))
        def _():
            for cp in copies(j, i):
                cp.wait()

        pa = jnp.sum(dla, axis=0, keepdims=True)
        pc = jnp.sum(dlc, axis=0, keepdims=True)

        @pl.when(i == 0)
        def _():
            dba_ref[...] = pa
            dbc_ref[...] = pc

        @pl.when(i > 0)
        def _():
            dba_ref[...] += pa
            dbc_ref[...] += pc

    big = jax.ShapeDtypeStruct((s, d), BF16)
    small = jax.ShapeDtypeStruct((1, d), F32)
    return pl.pallas_call(
        body, name="gate_bwd", grid=(d // GATE_CHUNK, n_rows),
        in_specs=[za, zc, ba, bc, tile, tile, tile],
        out_specs=[tile, tile, ANY, vec, vec],
        out_shape=[big, big, jax.ShapeDtypeStruct((s, in_width), BF16), small, small],
        scratch_shapes=[pltpu.VMEM((2, tr, GATE_CHUNK), BF16), pltpu.SemaphoreType.DMA((2,))],
        compiler_params=_params(2),
    )(z, z, b_gate, b_gate, ya, yc, dmix)


def _cross_probs(q_ref, kv_ref, hd):
    cols = slice(hd * HEAD_DIM, (hd + 1) * HEAD_DIM)
    qh = q_ref[:, cols]
    kh = kv_ref[:, cols]
    sc = lax.dot_general(qh, kh, (((1,), (1,)), ((), ())), preferred_element_type=F32) * ATTN_SCALE
    e = jnp.exp(sc - jnp.max(sc, axis=1, keepdims=True))
    return qh, kh, e * (1.0 / jnp.sum(e, axis=1, keepdims=True))


def _cross_fwd(qc, kvc):
    s = qc.shape[0]
    n_mem = kvc.shape[0]
    tq = _row_tile(s, 256)

    def body(q_ref, kv_ref, o_ref):
        for hd in range(MEM_HEADS):
            _, _, p = _cross_probs(q_ref, kv_ref, hd)
            vh = kv_ref[:, MEM_WIDTH + hd * HEAD_DIM:MEM_WIDTH + (hd + 1) * HEAD_DIM]
            o_ref[:, hd * HEAD_DIM:(hd + 1) * HEAD_DIM] = jnp.dot(p.astype(BF16), vh, preferred_element_type=F32).astype(BF16)

    return pl.pallas_call(
        body, name="cross_fwd", grid=(s // tq,),
        in_specs=[pl.BlockSpec((tq, MEM_WIDTH), lambda i: (i, 0)), pl.BlockSpec((n_mem, 2 * MEM_WIDTH), lambda i: (0, 0))],
        out_specs=pl.BlockSpec((tq, MEM_WIDTH), lambda i: (i, 0)),
        out_shape=jax.ShapeDtypeStruct((s, MEM_WIDTH), BF16),
        compiler_params=_params(1),
    )(qc, kvc)


def _cross_bwd(qc, kvc, d_out):
    s = qc.shape[0]
    n_mem = kvc.shape[0]
    tq = _row_tile(s, 256)

    def body(q_ref, kv_ref, do_ref, dq_ref, dkv_ref):
        @pl.when(pl.program_id(0) == 0)
        def _():
            dkv_ref[...] = jnp.zeros_like(dkv_ref)

        for hd in range(MEM_HEADS):
            cols = slice(hd * HEAD_DIM, (hd + 1) * HEAD_DIM)
            vcols = slice(MEM_WIDTH + hd * HEAD_DIM, MEM_WIDTH + (hd + 1) * HEAD_DIM)
            qh, kh, p = _cross_probs(q_ref, kv_ref, hd)
            doh = do_ref[:, cols]
            dp = lax.dot_general(doh, kv_ref[:, vcols], (((1,), (1,)), ((), ())), preferred_element_type=F32)
            ds = (p * (dp - jnp.sum(p * dp, axis=1, keepdims=True)) * ATTN_SCALE).astype(BF16)
            dq_ref[:, cols] = jnp.dot(ds, kh, preferred_element_type=F32).astype(BF16)
            dkv_ref[:, cols] += lax.dot_general(ds, qh, (((0,), (0,)), ((), ())), preferred_element_type=F32)
            dkv_ref[:, vcols] += lax.dot_general(p.astype(BF16), doh, (((0,), (0,)), ((), ())), preferred_element_type=F32)

    qspec = pl.BlockSpec((tq, MEM_WIDTH), lambda i: (i, 0))
    kvspec = pl.BlockSpec((n_mem, 2 * MEM_WIDTH), lambda i: (0, 0))
    return pl.pallas_call(
        body, name="cross_bwd", grid=(s // tq,),
        in_specs=[qspec, kvspec, qspec],
        out_specs=[qspec, kvspec],
        out_shape=[jax.ShapeDtypeStruct((s, MEM_WIDTH), BF16), jax.ShapeDtypeStruct((n_mem, 2 * MEM_WIDTH), F32)],
        compiler_params=_params(1),
    )(qc, kvc, d_out)


def _swiglu_fwd(up, gate):
    return up, (gate * jax.nn.sigmoid(gate)) * up


def _swiglu_bwd(d_act, gate, up):
    sg = jax.nn.sigmoid(gate)
    silu = gate * sg
    return d_act * up * (sg * (1.0 + gate * (1.0 - sg))), d_act * silu


GATHER_GROUPS = {"in": ("w_in", "conv_w"), "mid": ("w_attn_out", "w_conv_out", "w_o", "w_cq", "w_ckv", "w_co"),
                 "gate": ("w_gate",), "up": ("w_up",), "down": ("w_down",)}


def _local_step(xs, mems, target, small, fetch, reduce):
    s, d = xs.shape
    w4 = {}
    cos_t, sin_t = _rope_tables(s)

    def near(group, done, then, after):
        waits = [("direct", group)] + ([("pass_near", done), ("pass_far", done)] if done else [])
        starts = [("forward", group), ("pass_near", group)] + [("direct", g) for g in then]
        tok = fetch.step("gather_near_" + group, waits, starts, after)
        if done:
            w4.update(fetch.arrays(done))
        return tok

    def far(group, then, after):
        return fetch.step("gather_far_" + group, [("forward", group)], [("pass_far", group)] + [("direct", g) for g in then], after)

    def last(group, after):
        tok = fetch.step("gather_done_" + group, [("pass_near", group), ("pass_far", group)], [], after)
        w4.update(fetch.arrays(group))
        return tok

    h = _rmsnorm(xs, small["g_mix"], "norm_mix")
    slots_filled = [a for g in ("gate", "up", "down") for a in fetch.arrays(g).values()]
    chip_x, chip_y = reduce.place[0] // 2, reduce.place[0] % 2
    own_block = jnp.stack([2 * chip_x + chip_y]).astype(jnp.int32)
    near_blocks = jnp.stack([2 * (1 - chip_x) + chip_y, 2 * chip_x + (1 - chip_y)]).astype(jnp.int32)
    far_block = jnp.stack([2 * (1 - chip_x) + (1 - chip_y)]).astype(jnp.int32)
    z = _matmul_column_blocks(h, fetch.arrays("in")["w_in"], own_block, None, tm=512, name="in_proj_own")
    tok = near("in", None, ["mid"], [z] + slots_filled)
    tok = fetch.step("gather_near_done_in", [("pass_near", "in")], [], tok)
    z = _matmul_column_blocks(h, fetch.arrays("in")["w_in"], near_blocks, z, tm=512, name="in_proj_near", after=tok)
    tok = far("in", [], z)
    tok = fetch.step("gather_done_in", [("pass_far", "in")], [], tok)
    w4.update(fetch.arrays("in"))
    z = _matmul_column_blocks(h, w4["w_in"], far_block, z, tm=512, name="in_proj_far", after=tok)
    conv4 = w4["conv_w"]
    conv_w = conv4[:, :3, :].transpose(1, 0, 2).reshape(3, N_CHIPS * conv4.shape[2])
    c_in = w4["w_in"].shape[2]
    tok = near("mid", None, ["gate"], z)
    q_rot, k_rot, v_b = _rope_fwd(z, cos_t, sin_t)
    attn = _swa_fwd(q_rot, k_rot, v_b, small["sink"])
    co = _conv_fwd(z, conv_w)
    tok = far("mid", ["up"], attn)
    tok = last("mid", tok)
    w_o = w4["w_o"].reshape(-1, w4["w_o"].shape[-1])
    c_d = w4["w_attn_out"].shape[2]
    ya = _matmul(attn, w4["w_attn_out"], mode="nn", tm=1024, tn=c_d, tk=ATTN_WIDTH, out_dtypes=[F32], name="attn_out_proj",
                 b_blocks=N_CHIPS, after=tok)
    yc = _matmul(co, w4["w_conv_out"], mode="nn", tm=1024, tn=c_d, tk=CONV_WIDTH, out_dtypes=[F32], name="conv_out_proj",
                 b_blocks=N_CHIPS)
    mix = _gate_fwd(z, small["b_gate"], ya, yc)
    x1 = _matmul(mix, w_o, mode="nn", tm=512, tn=1024, tk=d, out_dtypes=[F32], name="mix_out_proj", extras=[xs],
                 epilogue=_add_residual)
    tok = near("gate", None, ["down"], x1)
    w_cq = w4["w_cq"].reshape(-1, w4["w_cq"].shape[-1])
    w_ckv = w4["w_ckv"].reshape(-1, w4["w_ckv"].shape[-1])
    hc = _rmsnorm(x1, small["g_cross"], "norm_cross")
    memn = _rmsnorm(mems, small["g_mem"], "norm_mem")
    qc = _matmul(hc, w_cq, mode="nn", tm=1024, tn=MEM_WIDTH, tk=d, out_dtypes=[BF16], name="cross_q_proj", after=tok)
    kvc = _matmul(memn, w_ckv, mode="nn", tm=256, tn=2 * MEM_WIDTH, tk=d, out_dtypes=[BF16], name="cross_kv_proj")
    oc = _cross_fwd(qc, kvc)
    tok = far("gate", [], oc)
    x2 = _matmul(oc, w4["w_co"], mode="nn", tm=1024, tn=c_d, tk=MEM_WIDTH, out_dtypes=[F32], name="cross_out_proj",
                 extras=[x1], epilogue=_add_residual, b_blocks=N_CHIPS, after=tok)
    hf = _rmsnorm(x2, small["g_ffn"], "norm_ffn")
    tok = near("up", "gate", [], hf)
    c_ff = w4["w_gate"].shape[2]
    gate = _matmul(hf, w4["w_gate"], mode="nn", tm=512, tn=c_ff, tk=d, out_dtypes=[F32], name="ffn_gate_proj", b_blocks=N_CHIPS,
                   after=tok)
    tok = far("up", [], gate)
    tok = near("down", "up", [], tok)
    up, act = _matmul(hf, w4["w_up"], mode="nn", tm=512, tn=c_ff, tk=d, out_dtypes=[F32, BF16], name="ffn_up_proj",
                      extras=[gate], epilogue=_swiglu_fwd, b_blocks=N_CHIPS, after=tok)
    tok = far("down", [], act)
    last("down", tok)
    w_down = w4["w_down"].reshape(-1, w4["w_down"].shape[-1])
    x3 = _matmul(act, w_down, mode="nn", tm=512, tn=512, tk=w_down.shape[0], out_dtypes=[F32], name="ffn_down_proj", extras=[x2],
                 epilogue=_add_residual)
    dx3, dx3b, sq, dg_final = _loss_head(x3, small["g_final"], target)

    da, du = _matmul(dx3b, w_down, mode="nt", tm=512, tn=c_ff, tk=d, out_dtypes=[BF16, BF16], name="ffn_down_bwd",
                     extras=[gate, up], epilogue=_swiglu_bwd)
    core = reduce.core
    ffn_shape = dict(row_sharded=False, tm=1024, tn=c_ff)
    g_down = _matmul(act, dx3b, mode="tn", tm=c_ff, tn=1024, tk=s, out_dtypes=[BF16], name="ffn_down_wgrad")
    tok = reduce.add("down", {"w_down": g_down}, da)
    t_gate = _wgrad_half(hf, da, core, theirs=True, name="ffn_gate_wgrad_theirs", after=tok, **ffn_shape)
    tok = reduce.step("down", t_gate)
    t_up = _wgrad_half(hf, du, core, theirs=True, name="ffn_up_wgrad_theirs", after=tok, **ffn_shape)
    tok = reduce.send("ffn", {"w_gate": t_gate, "w_up": t_up}, dx3b)
    dhf = _matmul(da, w4["w_gate"], mode="nt", tm=512, tn=1024, tk=N_CHIPS * c_ff, out_dtypes=[F32], name="ffn_gate_bwd", b_blocks=N_CHIPS,
                  after=tok)
    got = reduce.received("ffn", dhf)
    p_gate = _wgrad_half(hf, da, core, theirs=False, name="ffn_gate_wgrad_mine", add=got["w_gate"], **ffn_shape)
    p_up = _wgrad_half(hf, du, core, theirs=False, name="ffn_up_wgrad_mine", add=got["w_up"], **ffn_shape)
    tok = reduce.add_parts("ffn", {"w_gate": p_gate, "w_up": p_up})
    dhf = _matmul(du, w4["w_up"], mode="nt", tm=512, tn=1024, tk=N_CHIPS * c_ff, out_dtypes=[F32], name="ffn_up_bwd", extras=[dhf],
                  epilogue=_add_residual, b_blocks=N_CHIPS, after=tok)
    tok = reduce.step("down", dhf)
    dx2, dx2b, dg_ffn = _rmsnorm_bwd(dhf, x2, small["g_ffn"], dx3, "norm_ffn_bwd")

    d_oc = _matmul(dx2b, w4["w_co"], mode="nt", tm=1024, tn=MEM_WIDTH, tk=d, out_dtypes=[BF16], name="cross_out_bwd",
                   b_blocks=N_CHIPS, after=tok)
    g_co = _matmul(oc, dx2b, mode="tn", tm=MEM_WIDTH, tn=c_d, tk=s, out_dtypes=[BF16], name="cross_out_wgrad", out_blocks=N_CHIPS)
    tok = reduce.step("down", g_co)
    dqc, dkvc = _cross_bwd(qc, kvc, d_oc)
    g_cq = _matmul(hc, dqc, mode="tn", tm=1024, tn=MEM_WIDTH, tk=s, out_dtypes=[BF16], name="cross_q_wgrad", after=tok)
    dhc = _matmul(dqc, w_cq, mode="nt", tm=1024, tn=1024, tk=MEM_WIDTH, out_dtypes=[F32], name="cross_q_bwd")
    g_ckv = _matmul(memn, dkvc, mode="tn", tm=1024, tn=2 * MEM_WIDTH, tk=mems.shape[0], out_dtypes=[BF16], name="cross_kv_wgrad")
    dmemn = _matmul(dkvc, w_ckv, mode="nt", tm=256, tn=1024, tk=2 * MEM_WIDTH, out_dtypes=[F32], name="cross_kv_bwd")
    _, _, dg_mem = _rmsnorm_bwd(dmemn, mems, small["g_mem"], None, "norm_mem_bwd")
    dx1, dx1b, dg_cross = _rmsnorm_bwd(dhc, x1, small["g_cross"], dx2, "norm_cross_bwd")

    dmix = _matmul(dx1b, w_o, mode="nt", tm=512, tn=1024, tk=d, out_dtypes=[F32], name="mix_out_bwd")
    g_o = _matmul(mix, dx1b, mode="tn", tm=1024, tn=1024, tk=s, out_dtypes=[BF16], name="mix_out_wgrad")
    dya, dyc, dz, db_a, db_c = _gate_bwd(z, small["b_gate"], ya, yc, dmix)
    d_attn = _matmul(dya, w4["w_attn_out"], mode="nt", tm=1024, tn=ATTN_WIDTH, tk=d, out_dtypes=[BF16], name="attn_out_bwd",
                     b_blocks=N_CHIPS)
    g_ao = _matmul(attn, dya, mode="tn", tm=ATTN_WIDTH, tn=c_d, tk=s, out_dtypes=[BF16], name="attn_out_wgrad", out_blocks=N_CHIPS)
    d_co = _matmul(dyc, w4["w_conv_out"], mode="nt", tm=1024, tn=CONV_WIDTH, tk=d, out_dtypes=[F32], name="conv_out_bwd",
                   b_blocks=N_CHIPS)
    g_cvo = _matmul(co, dyc, mode="tn", tm=CONV_WIDTH, tn=c_d, tk=s, out_dtypes=[BF16], name="conv_out_wgrad", out_blocks=N_CHIPS)
    tok = reduce.step("ffn", g_cvo)
    tok = reduce.add("mid", {"w_co": g_co, "w_cq": g_cq, "w_ckv": g_ckv, "w_o": g_o, "w_attn_out": g_ao, "w_conv_out": g_cvo}, tok)
    dz, d_conv_w = _conv_bwd(z, conv_w, d_co, dz)
    dq_rot, dk_rot, dv, dsink = _swa_bwd(q_rot, k_rot, v_b, d_attn, small["sink"])
    tok = reduce.step("mid", dq_rot)
    dz = _rope_bwd(dq_rot, dk_rot, dv, cos_t, sin_t, dz)
    in_shape = dict(row_sharded=False, tm=1024, tn=c_in)
    t_in = _wgrad_half(h, dz, core, theirs=True, name="in_proj_wgrad_theirs", after=tok, **in_shape)
    tok = reduce.send("in", {"w_in": t_in}, dk_rot)
    tok = reduce.step("ffn", tok)
    tok = reduce.step("mid", tok)
    got = reduce.received("in", tok)
    p_in = _wgrad_half(h, dz, core, theirs=False, name="in_proj_wgrad_mine", add=got["w_in"], **in_shape)
    tok = reduce.add_parts("in", {"w_in": p_in})
    dh = _matmul(dz, w4["w_in"], mode="nt", tm=512, tn=512, tk=N_CHIPS * c_in, out_dtypes=[F32], name="in_proj_bwd", b_blocks=N_CHIPS,
                 after=tok)
    tok = reduce.step("mid", dh)
    grad_x, _, dg_mix = _rmsnorm_bwd(dh, xs, small["g_mix"], dx1, "norm_mix_bwd")

    small_grads = {
        "g_mix": dg_mix, "sink": dsink[:, 0], "b_gate": jnp.concatenate([db_a, db_c], axis=1), "g_cross": dg_cross,
        "g_mem": dg_mem, "g_ffn": dg_ffn, "g_final": dg_final, "conv_w": d_conv_w,
    }
    return sq, grad_x, small_grads


def _pair_sum(g4, ra, core, name):
    nb, rs, cs = g4.shape
    rh = rs // 2
    tr = _row_tile(rh, 256)
    per = rh // tr

    def body(c_ref, g_ref, r_ref, o_ref):
        o_ref[...] = (g_ref[...].astype(F32) + r_ref[...].astype(F32)).astype(BF16)

    plain = pl.BlockSpec((None, tr, cs), lambda j, i, c: (j, i, 0))
    return pl.pallas_call(
        body, name=name,
        grid_spec=pltpu.PrefetchScalarGridSpec(
            num_scalar_prefetch=1, grid=(nb, per),
            in_specs=[pl.BlockSpec((None, tr, cs), lambda j, i, c: (j, c[0] * per + i, 0)), plain],
            out_specs=plain),
        out_shape=jax.ShapeDtypeStruct((nb, rh, cs), BF16),
        compiler_params=_params(2),
    )(core, g4, ra)


def _quad_sum(parts, rc, place, name):
    _, rh, cs = parts.shape
    tr = _row_tile(rh, 256)
    per = rh // tr

    def body(p_ref, own_ref, r_ref, o_ref):
        acc = own_ref[...].astype(F32)
        for j in range(rc.shape[0]):
            acc = acc + r_ref[j].astype(F32)
        o_ref[...] = acc

    return pl.pallas_call(
        body, name=name,
        grid_spec=pltpu.PrefetchScalarGridSpec(
            num_scalar_prefetch=1, grid=(per,),
            in_specs=[pl.BlockSpec((None, tr, cs), lambda i, p: (p[0], i, 0)),
                      pl.BlockSpec((rc.shape[0], tr, cs), lambda i, p: (0, i, 0))],
            out_specs=pl.BlockSpec((tr, cs), lambda i, p: (p[1] * per + i, 0))),
        out_shape=jax.ShapeDtypeStruct((2 * rh, cs), F32),
        compiler_params=_params(1),
    )(place, parts, rc)


def _adamw_update(w, g, m, v):
    nm = ADAM_B1 * m + (1.0 - ADAM_B1) * g
    nv = ADAM_B2 * v + (1.0 - ADAM_B2) * (g * g)
    m_hat = nm / ADAM_C1
    v_hat = nv / ADAM_C2
    return -ADAM_LR * (m_hat / (jnp.sqrt(v_hat) + ADAM_EPS) + ADAM_WD * w), nm, nv


def _adamw_own_half(w, m, v, parts, rc, place, name, after=None):
    rows, cols = w.shape
    rh = rows // 2
    tr = _row_tile(rh, 256)
    per = rh // tr

    def body(p_ref, w_ref, m_ref, v_ref, own_ref, r_ref, *rest):
        gx_ref, g_ref, d_ref, nm_ref, nv_ref = rest[-5:]
        g = own_ref[...].astype(F32)
        for j in range(rc.shape[0]):
            g = g + r_ref[j].astype(F32)
        gx_ref[...] = g
        g_ref[...] = g
        d_ref[...], nm_ref[...], nv_ref[...] = _adamw_update(w_ref[...], g, m_ref[...], v_ref[...])

    mine = pl.BlockSpec((tr, cols), lambda i, p: (p[1] * per + i, 0))
    shape = jax.ShapeDtypeStruct((rows, cols), F32)
    return pl.pallas_call(
        body, name=name,
        grid_spec=pltpu.PrefetchScalarGridSpec(
            num_scalar_prefetch=1, grid=(per,),
            in_specs=[mine, mine, mine, pl.BlockSpec((None, tr, cols), lambda i, p: (p[0], i, 0)),
                      pl.BlockSpec((rc.shape[0], tr, cols), lambda i, p: (0, i, 0))] + ([] if after is None else [ANY]),
            out_specs=[mine] * 5),
        out_shape=[shape] * 5,
        compiler_params=_params(1),
    )(place, w, m, v, parts, rc, *([] if after is None else [after]))


def _adamw_other_half(w, m, v, g_exchanged, g, delta, new_m, new_v, place, name, after=None):
    rows, cols = w.shape
    rh = rows // 2
    tr = _row_tile(rh, 256)
    per = rh // tr

    def body(p_ref, w_ref, m_ref, v_ref, gx_ref, *rest):
        g_ref, d_ref, nm_ref, nv_ref = rest[-4:]
        gv = gx_ref[...]
        g_ref[...] = gv
        d_ref[...], nm_ref[...], nv_ref[...] = _adamw_update(w_ref[...], gv, m_ref[...], v_ref[...])

    other = pl.BlockSpec((tr, cols), lambda i, p: ((1 - p[1]) * per + i, 0))
    shape = jax.ShapeDtypeStruct((rows, cols), F32)
    n_after = 0 if after is None else 1
    return pl.pallas_call(
        body, name=name,
        grid_spec=pltpu.PrefetchScalarGridSpec(
            num_scalar_prefetch=1, grid=(per,),
            in_specs=[other] * 4 + [ANY] * (4 + n_after),
            out_specs=[other] * 4),
        out_shape=[shape] * 4,
        input_output_aliases={5: 0, 6: 1, 7: 2, 8: 3},
        compiler_params=_params(1),
    )(place, w, m, v, g_exchanged, g, delta, new_m, new_v, *([] if after is None else [after]))


def _cast_to_slot(w, place, dtype, name, after=None):
    rows, cols = w.shape
    tr = _row_tile(rows, 1024)

    def body(p_ref, w_ref, *rest):
        o_ref = rest[-1]
        o_ref[...] = w_ref[...].astype(dtype)

    return pl.pallas_call(
        body, name=name,
        grid_spec=pltpu.PrefetchScalarGridSpec(
            num_scalar_prefetch=1, grid=(rows // tr,),
            in_specs=[pl.BlockSpec((tr, cols), lambda i, p: (i, 0))] + ([] if after is None else [ANY]),
            out_specs=pl.BlockSpec((None, tr, cols), lambda i, p: (p[0], i, 0))),
        out_shape=jax.ShapeDtypeStruct((N_CHIPS, rows, cols), dtype),
        compiler_params=_params(1),
    )(place, w, *([] if after is None else [after]))


SC_TILES = 32
SC_SLAB_ROWS = 8
SC_LANES = 16


def _adamw_on_sparsecores(w, g, m, v, name):
    rows, cols = w.shape
    slabs = rows // SC_SLAB_ROWS
    n_chunks = 1
    while (slabs * n_chunks) % SC_TILES or cols % (n_chunks * 128):
        n_chunks += 1
    width = cols // n_chunks
    per_tile = slabs * n_chunks // SC_TILES

    def body(w_hbm, g_hbm, m_hbm, v_hbm, go_hbm, d_hbm, nm_hbm, nv_hbm, wb, gb, mb, vb):
        tile = lax.axis_index("sc_tile") * 2 + lax.axis_index("sc_core")

        @pl.loop(0, per_tile)
        def _(k):
            piece = tile * per_tile + k
            rs = pl.ds(pl.multiple_of((piece // n_chunks) * SC_SLAB_ROWS, SC_SLAB_ROWS), SC_SLAB_ROWS)
            cs = pl.ds(pl.multiple_of((piece % n_chunks) * width, 128), width)
            pltpu.sync_copy(w_hbm.at[rs, cs], wb)
            pltpu.sync_copy(g_hbm.at[rs, cs], gb)
            pltpu.sync_copy(m_hbm.at[rs, cs], mb)
            pltpu.sync_copy(v_hbm.at[rs, cs], vb)
            for r in range(SC_SLAB_ROWS):
                @pl.loop(0, width, step=SC_LANES)
                def _(c):
                    lanes = pl.ds(c, SC_LANES)
                    gv = gb[r, lanes]
                    nm = ADAM_B1 * mb[r, lanes] + (1.0 - ADAM_B1) * gv
                    nv = ADAM_B2 * vb[r, lanes] + (1.0 - ADAM_B2) * (gv * gv)
                    mb[r, lanes] = nm
                    vb[r, lanes] = nv
                    wb[r, lanes] = -ADAM_LR * ((nm / ADAM_C1) / (jnp.sqrt(nv / ADAM_C2) + ADAM_EPS) + ADAM_WD * wb[r, lanes])
            pltpu.sync_copy(gb, go_hbm.at[rs, cs])
            pltpu.sync_copy(wb, d_hbm.at[rs, cs])
            pltpu.sync_copy(mb, nm_hbm.at[rs, cs])
            pltpu.sync_copy(vb, nv_hbm.at[rs, cs])

    shape = jax.ShapeDtypeStruct((rows, cols), F32)
    buf = pltpu.VMEM((SC_SLAB_ROWS, width), F32)
    return pl.kernel(body, name=name, out_type=[shape] * 4,
                     mesh=plsc.VectorSubcoreMesh(core_axis_name="sc_core", subcore_axis_name="sc_tile"),
                     scratch_types=[buf] * 4)(w, g, m, v)


def _adamw(w, g, m, v, name, after=None):
    rows, cols = w.shape
    tr = _row_tile(rows, 256)

    def body(w_ref, g_ref, m_ref, v_ref, *rest):
        go_ref, d_ref, nm_ref, nv_ref = rest[-4:]
        gv = g_ref[...]
        go_ref[...] = gv
        d_ref[...], nm_ref[...], nv_ref[...] = _adamw_update(w_ref[...], gv, m_ref[...], v_ref[...])

    tile = pl.BlockSpec((tr, cols), lambda i: (i, 0))
    shape = jax.ShapeDtypeStruct((rows, cols), F32)
    return pl.pallas_call(
        body, name=name, grid=(rows // tr,),
        in_specs=[tile] * 4 + ([] if after is None else [ANY]), out_specs=[tile] * 4, out_shape=[shape] * 4,
        compiler_params=_params(1),
    )(w, g, m, v, *([] if after is None else [after]))


def _mesh_pos():
    return lax.axis_index("x"), lax.axis_index("y"), lax.axis_index("c")


def _other_chips(x, y):
    return [(1 - x, y), (x, 1 - y), (1 - x, 1 - y)]


def _half_rows(ref, which):
    rh = ref.shape[-2] // 2
    return ref.at[pl.ds(which * rh, rh), :]


def _remote(src, dst, send_sems, recv_sems, sem, to):
    return pltpu.make_async_remote_copy(src_ref=src, dst_ref=dst, send_sem=send_sems.at[sem], recv_sem=recv_sems.at[sem],
                                        device_id=to, device_id_type=MESH)


HBM = pl.BlockSpec(memory_space=pltpu.HBM)
SEM = pl.BlockSpec(memory_space=pltpu.SEMAPHORE)
DATAFLOW_EFFECT = pltpu.SideEffectType.DATAFLOW_SIDE_EFFECTING


def _in_hbm(arrays):
    return [pltpu.with_memory_space_constraint(a, pltpu.HBM) for a in arrays]


def _hbm_like(arrays):
    return [pltpu.HBM(a.shape, a.dtype) for a in arrays]


GATHER_COPIES_PER_ARRAY = {"direct": 2, "forward": 2, "pass_near": 2, "pass_far": 1}


def _gather_copies(kind, refs, x, y, c):
    me, near_x, near_y, far = 2 * x + y, 2 * (1 - x) + y, 2 * x + (1 - y), 2 * (1 - x) + (1 - y)
    to_x, to_y, sibling = (1 - x, y, c), (x, 1 - y, c), (x, y, 1 - c)
    out = []
    for ref in refs:
        rh = ref.shape[1] // 2
        rq = rh // 2

        def half(chip, ref=ref, rh=rh):
            return ref.at[chip, pl.ds(c * rh, rh), :]

        def quarter(chip, q, ref=ref, rh=rh, rq=rq):
            return ref.at[chip, pl.ds(c * rh + q * rq, rq), :]

        if kind == "direct":
            out += [(half(me), half(me), to_x), (half(me), half(me), to_y)]
        elif kind == "forward":
            out += [(quarter(near_x, 0), quarter(near_x, 0), to_y), (quarter(near_y, 1), quarter(near_y, 1), to_x)]
        elif kind == "pass_near":
            out += [(half(near_x), half(near_x), sibling), (half(near_y), half(near_y), sibling)]
        else:
            assert kind == "pass_far"
            out += [(half(far), half(far), sibling)]
    return out


def _gather_step(name, bufs, waits, starts, after):
    nb, nw, ns = len(bufs), len(waits), len(starts)
    after = [] if after is None else list(after) if isinstance(after, (list, tuple)) else [after]
    n_after = len(after)

    def body(*refs):
        ins = refs[:nb]
        wait_sems = refs[nb:nb + 2 * nw]
        start_sems = refs[nb + 2 * nw + n_after:nb + 2 * nw + n_after + 2 * ns]
        token = refs[-1]
        x, y, c = _mesh_pos()
        for j, (kind, idxs, _, _) in enumerate(waits):
            for i, (s_ref, d_ref, to) in enumerate(_gather_copies(kind, [ins[t] for t in idxs], x, y, c)):
                came = _remote(s_ref, d_ref, wait_sems[2 * j], wait_sems[2 * j + 1], i, to)
                came.wait_recv()
                came.wait_send()
        for j, (kind, idxs) in enumerate(starts):
            for i, (s_ref, d_ref, to) in enumerate(_gather_copies(kind, [ins[t] for t in idxs], x, y, c)):
                _remote(s_ref, d_ref, start_sems[2 * j], start_sems[2 * j + 1], i, to).start()
        token[...] = jnp.zeros_like(token)

    sems = []
    for kind, idxs in starts:
        sems += [pltpu.SemaphoreType.DMA((GATHER_COPIES_PER_ARRAY[kind] * len(idxs),))] * 2
    operands = _in_hbm(bufs) + [sem for w in waits for sem in w[2:]] + after
    outs = pl.pallas_call(
        body, name=name,
        in_specs=[HBM] * nb + [SEM] * (2 * nw) + [ANY] * n_after,
        out_specs=[SEM] * (2 * ns) + [HBM] * nb + [pl.BlockSpec(memory_space=pltpu.VMEM)],
        out_shape=sems + _hbm_like(bufs) + [jax.ShapeDtypeStruct((8, 128), F32)],
        input_output_aliases={i: 2 * ns + i for i in range(nb)},
        compiler_params=pltpu.CompilerParams(has_side_effects=DATAFLOW_EFFECT),
    )(*operands)
    return outs[2 * ns:2 * ns + nb], [(outs[2 * j], outs[2 * j + 1]) for j in range(ns)], outs[-1]


class _Gather:
    def __init__(self, groups):
        self.groups = groups
        self.bufs = {}
        self.in_flight = {}

    def put(self, slotted):
        self.bufs.update(slotted)

    def step(self, name, waits, starts, after=None):
        names = []
        for _, group in list(waits) + list(starts):
            names += [n for n in self.groups[group] if n not in names]
        index = {n: i for i, n in enumerate(names)}

        def members(group):
            return [index[n] for n in self.groups[group]]

        wait_args = [(kind, members(group)) + self.in_flight.pop((kind, group)) for kind, group in waits]
        start_args = [(kind, members(group)) for kind, group in starts]
        bufs, sems, token = _gather_step(name, [self.bufs[n] for n in names], wait_args, start_args, after)
        self.bufs.update(zip(names, bufs))
        for (kind, group), pair in zip(starts, sems):
            self.in_flight[(kind, group)] = pair
        return token

    def arrays(self, group):
        return {n: self.bufs[n] for n in self.groups[group]}


def _sibling_halves_copies(srcs, dsts, x, y, c):
    out = []
    for s_ref, d_ref in zip(srcs, dsts, strict=True):
        rh = s_ref.shape[1] // 2
        out.append((s_ref.at[:, pl.ds((1 - c) * rh, rh), :], d_ref, (x, y, 1 - c)))
    return out


def _to_sibling_copies(srcs, dsts, x, y, c):
    return [(s_ref, d_ref, (x, y, 1 - c)) for s_ref, d_ref in zip(srcs, dsts, strict=True)]


def _chip_copies(srcs, dsts, x, y, c):
    out = []
    for s_ref, d_ref in zip(srcs, dsts, strict=True):
        for k, (px, py) in enumerate(_other_chips(x, y)):
            out.append((s_ref.at[2 * px + py], d_ref.at[k], (px, py, c)))
    return out


def _join_copies(srcs, dsts, x, y, c):
    out = []
    for s_ref in srcs:
        mine = _half_rows(s_ref, c)
        out.append((mine, mine, (x, y, 1 - c)))
    return out


def _exchange_start(copies_fn, n_copies, srcs, fresh, after, name):
    ns, nb = len(srcs), len(srcs) + len(fresh)

    def body(*refs):
        bufs, send, recv, token = refs[:nb], refs[nb + 1], refs[nb + 2], refs[-1]
        x, y, c = _mesh_pos()
        for i, (s_ref, d_ref, to) in enumerate(copies_fn(bufs[:ns], bufs[ns:] if fresh else bufs[:ns], x, y, c)):
            _remote(s_ref, d_ref, send, recv, i, to).start()
        token[...] = jnp.zeros_like(token)

    sems = [pltpu.SemaphoreType.DMA((n_copies,))] * 2
    outs = pl.pallas_call(
        body, name=name,
        in_specs=[HBM] * nb + [ANY], out_specs=[SEM, SEM] + [HBM] * nb + [pl.BlockSpec(memory_space=pltpu.VMEM)],
        out_shape=sems + _hbm_like(list(srcs) + list(fresh)) + [jax.ShapeDtypeStruct((8, 128), F32)],
        input_output_aliases={i: 2 + i for i in range(nb)},
        compiler_params=pltpu.CompilerParams(has_side_effects=DATAFLOW_EFFECT),
    )(*_in_hbm(list(srcs) + list(fresh)), after)
    return outs[0], outs[1], outs[2:2 + ns], outs[2 + ns:2 + nb], outs[-1]


def _exchange_done(copies_fn, srcs, fresh, send, recv, after, name):
    ns, nb = len(srcs), len(srcs) + len(fresh)

    def body(*refs):
        bufs, send_in, recv_in = refs[:nb], refs[nb], refs[nb + 1]
        x, y, c = _mesh_pos()
        for i, (s_ref, d_ref, to) in enumerate(copies_fn(bufs[:ns], bufs[ns:] if fresh else bufs[:ns], x, y, c)):
            came = _remote(s_ref, d_ref, send_in, recv_in, i, to)
            came.wait_send()
            came.wait_recv()

    outs = pl.pallas_call(
        body, name=name,
        in_specs=[HBM] * nb + [SEM, SEM, ANY], out_specs=[HBM] * nb,
        out_shape=_hbm_like(list(srcs) + list(fresh)),
        input_output_aliases={i: i for i in range(nb)},
        compiler_params=pltpu.CompilerParams(has_side_effects=DATAFLOW_EFFECT),
    )(*_in_hbm(list(srcs) + list(fresh)), send, recv, after)
    return outs[:ns], outs[ns:]


SPARSECORE_ADAMW_GROUPS = ("down", "ffn")


class _Reduce:
    def __init__(self, place, core, shards, mom_m, mom_v):
        self.place, self.core = place, core
        self.shards, self.mom_m, self.mom_v = shards, mom_m, mom_v
        self.state = {}
        self.results = {}

    def add(self, group, grads, after):
        names = list(grads)
        g4s = [g.reshape((N_CHIPS, -1, g.shape[-1])) if g.ndim == 2 else g for g in grads.values()]
        fresh = [lax.empty((N_CHIPS, g.shape[1] // 2, g.shape[2]), BF16) for g in g4s]
        send, recv, g4s, fresh, token = _exchange_start(_sibling_halves_copies, len(names), g4s, fresh, after,
                                                        "pair_start_" + group)
        self.state[group] = (0, names, send, recv, g4s, fresh)
        return token

    def send(self, group, theirs, after):
        names, srcs = list(theirs), list(theirs.values())
        fresh = [lax.empty(s.shape, BF16) for s in srcs]
        send, recv, srcs, fresh, token = _exchange_start(_to_sibling_copies, len(names), srcs, fresh, after, "pair_start_" + group)
        self.state[group] = ("sent", names, send, recv, srcs, fresh)
        return token

    def received(self, group, after):
        stage, names, send, recv, srcs, fresh = self.state.pop(group)
        assert stage == "sent"
        _, got = _exchange_done(_to_sibling_copies, srcs, fresh, send, recv, after, "pair_done_" + group)
        return dict(zip(names, got))

    def add_parts(self, group, parts):
        names, srcs = list(parts), list(parts.values())
        fresh = [lax.empty((N_CHIPS - 1,) + p.shape[1:], BF16) for p in srcs]
        send, recv, srcs, fresh, token = _exchange_start(_chip_copies, 3 * len(names), srcs, fresh, self.core, "chips_start_" + group)
        self.state[group] = (1, names, send, recv, srcs, fresh)
        return token

    def step(self, group, after):
        stage, names, send, recv, srcs, fresh = self.state[group]
        if stage == 0:
            g4s, ras = _exchange_done(_sibling_halves_copies, srcs, fresh, send, recv, after, "pair_done_" + group)
            parts = [_pair_sum(g, r, self.core, "pair_sum_" + n) for g, r, n in zip(g4s, ras, names)]
            fresh = [lax.empty((N_CHIPS - 1,) + p.shape[1:], BF16) for p in parts]
            send, recv, parts, fresh, token = _exchange_start(_chip_copies, 3 * len(names), parts, fresh, self.core,
                                                              "chips_start_" + group)
            self.state[group] = (1, names, send, recv, parts, fresh)
            return token
        if stage == 1:
            parts, rcs = _exchange_done(_chip_copies, srcs, fresh, send, recv, after, "chips_done_" + group)
            if group in SPARSECORE_ADAMW_GROUPS:
                wholes = [_quad_sum(p, r, self.place, "quad_sum_" + n) for p, r, n in zip(parts, rcs, names)]
                send, recv, wholes, _, token = _exchange_start(_join_copies, len(names), wholes, [], self.core, "join_start_" + group)
                self.state[group] = (2, names, send, recv, wholes, [])
                return token
            token = None
            for n, p, r in zip(names, parts, rcs):
                self.results[n] = _adamw_own_half(self.shards[n], self.mom_m[n], self.mom_v[n], p, r, self.place,
                                                  "adamw_own_" + n, after=token)
                token = self.results[n][2]
            wholes = [self.results[n][0] for n in names]
            send, recv, wholes, _, token = _exchange_start(_join_copies, len(names), wholes, [], token, "join_start_" + group)
            self.state[group] = (2, names, send, recv, wholes, [])
            return token
        assert stage == 2
        wholes, _ = _exchange_done(_join_copies, srcs, [], send, recv, after, "join_done_" + group)
        if group in SPARSECORE_ADAMW_GROUPS:
            for n, g in zip(names, wholes):
                self.results[n] = _adamw_on_sparsecores(self.shards[n], g, self.mom_m[n], self.mom_v[n], "adamw_sc_" + n)
            del self.state[group]
            return wholes[0]
        token = None
        for n, exchanged in zip(names, wholes):
            _, g, d, nm, nv = self.results[n]
            self.results[n] = _adamw_other_half(self.shards[n], self.mom_m[n], self.mom_v[n], exchanged, g, d, nm, nv,
                                                self.place, "adamw_other_" + n, after=token)
            token = self.results[n][1]
        del self.state[group]
        return token


N_DEV = 8


def _all_reduce_small(v):
    def body(v_ref, o_ref, slots, send_sems, recv_sems):
        x, y, c = _mesh_pos()
        me = 4 * x + 2 * y + c
        slots[me] = v_ref[...]
        peers = []
        for r in range(1, N_DEV):
            fx, fy, fc = (r >> 2) & 1, (r >> 1) & 1, r & 1
            peers.append((x + fx - 2 * x * fx, y + fy - 2 * y * fy, c + fc - 2 * c * fc))
        sends = []
        for r, peer in enumerate(peers):
            cp = _remote(v_ref, slots.at[me], send_sems, recv_sems, r, peer)
            cp.start()
            sends.append(cp)
        for r, (px, py, pc) in enumerate(peers):
            landed = slots.at[4 * px + 2 * py + pc]
            _remote(landed, landed, send_sems, recv_sems, r, (px, py, pc)).wait_recv()
        for cp in sends:
            cp.wait_send()
        acc = slots[0]
        for i in range(1, N_DEV):
            acc = acc + slots[i]
        o_ref[...] = acc

    vm = pl.BlockSpec(memory_space=pltpu.VMEM)
    return pl.pallas_call(
        body, name="small_grads_all_reduce",
        in_specs=[vm], out_specs=vm,
        out_shape=jax.ShapeDtypeStruct(v.shape, v.dtype),
        scratch_shapes=[pltpu.VMEM((N_DEV,) + v.shape, v.dtype), pltpu.SemaphoreType.DMA((N_DEV - 1,)),
                        pltpu.SemaphoreType.DMA((N_DEV - 1,))],
    )(v)


MATRICES = ("w_in", "w_attn_out", "w_conv_out", "w_o", "w_cq", "w_ckv", "w_co", "w_gate", "w_up", "w_down")
VECTORS = ("g_mix", "b_gate", "g_cross", "g_mem", "g_ffn", "g_final", "conv_w", "sink")
WEIGHT_ORDER = ("g_mix", "w_in", "sink", "conv_w", "b_gate", "w_attn_out", "w_conv_out", "w_o", "g_cross", "g_mem", "w_cq",
                "w_ckv", "w_co", "g_ffn", "w_gate", "w_up", "w_down", "g_final")
CONV_PAD_ROWS = 32
SMALL_ROWS = 8


def _pack(pieces):
    flat = jnp.concatenate([p.reshape(-1) for p in pieces])
    lane_group = SMALL_ROWS * 128
    total = -(-flat.shape[0] // lane_group) * lane_group
    flat = jnp.pad(flat, (0, total - flat.shape[0]))
    return flat.reshape(SMALL_ROWS, total // SMALL_ROWS), [p.size for p in pieces]


def _unpack(packed, pieces):
    flat = packed.reshape(-1)
    out, off = [], 0
    for p in pieces:
        out.append(flat[off:off + p.size].reshape(p.shape))
        off += p.size
    return out


def kernel(x, mem, g_mix, w_in, sink, conv_w, b_gate, w_attn_out, w_conv_out, w_o, g_cross, g_mem, w_cq, w_ckv, w_co, g_ffn, w_gate, w_up, w_down, g_final, loss_target, m_g_mix, m_w_in, m_sink, m_conv_w, m_b_gate, m_w_attn_out, m_w_conv_out, m_w_o, m_g_cross, m_g_mem, m_w_cq, m_w_ckv, m_w_co, m_g_ffn, m_w_gate, m_w_up, m_w_down, m_g_final, v_g_mix, v_w_in, v_sink, v_conv_w, v_b_gate, v_w_attn_out, v_w_conv_out, v_w_o, v_g_cross, v_g_mem, v_w_cq, v_w_ckv, v_w_co, v_g_ffn, v_w_gate, v_w_up, v_w_down, v_g_final):
    given = dict(g_mix=g_mix, w_in=w_in, sink=sink, conv_w=conv_w, b_gate=b_gate, w_attn_out=w_attn_out, w_conv_out=w_conv_out,
                 w_o=w_o, g_cross=g_cross, g_mem=g_mem, w_cq=w_cq, w_ckv=w_ckv, w_co=w_co, g_ffn=g_ffn, w_gate=w_gate, w_up=w_up,
                 w_down=w_down, g_final=g_final)
    mom_m = dict(g_mix=m_g_mix, w_in=m_w_in, sink=m_sink, conv_w=m_conv_w, b_gate=m_b_gate, w_attn_out=m_w_attn_out,
                 w_conv_out=m_w_conv_out, w_o=m_w_o, g_cross=m_g_cross, g_mem=m_g_mem, w_cq=m_w_cq, w_ckv=m_w_ckv, w_co=m_w_co,
                 g_ffn=m_g_ffn, w_gate=m_w_gate, w_up=m_w_up, w_down=m_w_down, g_final=m_g_final)
    mom_v = dict(g_mix=v_g_mix, w_in=v_w_in, sink=v_sink, conv_w=v_conv_w, b_gate=v_b_gate, w_attn_out=v_w_attn_out,
                 w_conv_out=v_w_conv_out, w_o=v_w_o, g_cross=v_g_cross, g_mem=v_g_mem, w_cq=v_w_cq, w_ckv=v_w_ckv, w_co=v_w_co,
                 g_ffn=v_g_ffn, w_gate=v_w_gate, w_up=v_w_up, w_down=v_w_down, g_final=v_g_final)
    xs, mems, target = x[0], mem[0], loss_target[0]
    d_model = xs.shape[1]
    chip = 2 * lax.axis_index("x") + lax.axis_index("y")
    core = jnp.reshape(lax.axis_index("c"), (1,)).astype(jnp.int32)
    place = jnp.stack([chip, lax.axis_index("c")]).astype(jnp.int32)

    shards = {n: given[n][0] for n in MATRICES}
    conv_cols = conv_w.shape[2]
    conv_pad = jnp.pad(conv_w[0], ((0, CONV_PAD_ROWS - conv_w.shape[1]), (0, 0)))
    fetch = _Gather(GATHER_GROUPS)
    first = {"w_in": _cast_to_slot(shards["w_in"], place, BF16, "to_slot_w_in"),
             "conv_w": _cast_to_slot(conv_pad, place, F32, "to_slot_conv_w")}
    fetch.put(first)
    tok = fetch.step("gather_start", [], [("direct", "in")])
    fetch.put({n: _cast_to_slot(shards[n], place, BF16, "to_slot_" + n, after=tok) for n in MATRICES if n != "w_in"})
    small = {n: given[n] for n in ("g_mix", "b_gate", "g_cross", "g_mem", "g_ffn")}
    small["g_final"] = g_final[None]
    small["sink"] = sink[0]

    reduce = _Reduce(place, core, shards, {n: mom_m[n][0] for n in MATRICES}, {n: mom_v[n][0] for n in MATRICES})
    sq, grad_x, small_grads = _local_step(xs, mems, target, small, fetch, reduce)

    loss_part = 0.5 * sq[0:1, 0:1] / d_model
    pieces = [small_grads[n] for n in VECTORS] + [loss_part]
    packed, _ = _pack(pieces)
    summed = _unpack(_all_reduce_small(packed), pieces)
    loss = summed[-1][0, 0]
    small_sum = dict(zip(VECTORS, summed[:-1]))
    small_sum["conv_w"] = lax.dynamic_slice_in_dim(small_sum["conv_w"], chip * conv_cols, conv_cols, axis=1)

    grad_out, delta, new_m, new_v = {}, {}, {}, {}
    like = [given[n] for n in VECTORS]
    pw, _ = _pack(like)
    pg, _ = _pack([small_sum[n] for n in VECTORS])
    pm, _ = _pack([mom_m[n] for n in VECTORS])
    pv, _ = _pack([mom_v[n] for n in VECTORS])
    tok = reduce.step("in", pg)
    _, pd, pnm, pnv = _adamw(pw, pg, pm, pv, "adamw_small", after=tok)
    for n, g, d, nm, nv in zip(VECTORS, [small_sum[n] for n in VECTORS], _unpack(pd, like), _unpack(pnm, like), _unpack(pnv, like)):
        grad_out[n] = g.reshape(given[n].shape)
        delta[n], new_m[n], new_v[n] = d, nm, nv
    reduce.step("in", pd)
    for n in MATRICES:
        g, d, nm, nv = reduce.results[n]
        grad_out[n], delta[n], new_m[n], new_v[n] = g[None], d[None], nm[None], nv[None]

    return (loss, grad_x[None], *[grad_out[n] for n in WEIGHT_ORDER], *[delta[n] for n in WEIGHT_ORDER],
            *[new_m[n] for n in WEIGHT_ORDER], *[new_v[n] for n in WEIGHT_ORDER])
```

```python
import functools

import jax
import jax.numpy as jnp
from jax import lax
from jax.experimental import pallas as pl
from jax.experimental.pallas import tpu as pltpu
from jax.experimental.pallas import tpu_sc as plsc

F32 = jnp.float32
BF16 = jnp.bfloat16
MESH = pl.DeviceIdType.MESH
ANY = pl.BlockSpec(memory_space=pl.ANY)

VMEM_LIMIT_BYTES = 56 * 1024 * 1024

N_CHIPS = 4
HEAD_DIM = 128
N_Q_HEADS = 8
N_KV_HEADS = 2
Q_GROUP = N_Q_HEADS // N_KV_HEADS
ATTN_WIDTH = N_Q_HEADS * HEAD_DIM
KV_WIDTH = N_KV_HEADS * HEAD_DIM
WINDOW = 128
BLOCK = 128
BAND = 3 * BLOCK
ROPE_THETA = 10000.0
CONV_WIDTH = 1024
MEM_HEADS = 4
MEM_WIDTH = MEM_HEADS * HEAD_DIM
RMS_EPS = 1e-6
NEG_INF = -1e30
ATTN_SCALE = HEAD_DIM ** -0.5

Q_OFF, K_OFF, V_OFF, CU_OFF, CB_OFF, CC_OFF, GL_OFF = 0, 1024, 1280, 1536, 2560, 3584, 4608

ADAM_LR = 0.001
ADAM_B1 = 0.9
ADAM_B2 = 0.999
ADAM_EPS = 1e-08
ADAM_WD = 0.01
ADAM_STEP = 10
ADAM_C1 = 1.0 - ADAM_B1 ** ADAM_STEP
ADAM_C2 = 1.0 - ADAM_B2 ** ADAM_STEP


def _params(n_grid_axes):
    return pltpu.CompilerParams(dimension_semantics=("arbitrary",) * n_grid_axes, vmem_limit_bytes=VMEM_LIMIT_BYTES)


BF16_SUBLANES = 16


def _row_tile(rows, want):
    if rows <= want:
        return rows
    for t in range(want, 0, -BF16_SUBLANES):
        if rows % t == 0:
            return t
    return rows


def _matmul(a, b, *, mode, tm, tn, tk, out_dtypes, name, extras=(), epilogue=None, b_blocks=1, out_blocks=1, after=None):
    if mode == "tn":
        kdim, m = a.shape
    else:
        m, kdim = a.shape
    if b_blocks > 1:
        nb, brows, bcols = b.shape
        assert nb == b_blocks
        if mode == "nn":
            n = bcols * nb
            assert brows == kdim
        else:
            assert mode == "nt" and bcols * nb == kdim
            n = brows
    else:
        n = b.shape[0] if mode == "nt" else b.shape[1]
    tm, tn = min(tm, m), min(tn, n)
    assert m % tm == 0 and n % tn == 0 and tk == kdim, (name, m, n, kdim, tm, tn, tk)
    n_extra, n_out = len(extras), len(out_dtypes)
    n_after = 0 if after is None else 1

    if mode == "tn":
        a_spec = pl.BlockSpec((tk, tm), lambda j, i, k: (k, i))
        dims = (((0,), (0,)), ((), ()))
    else:
        a_spec = pl.BlockSpec((tm, tk), lambda j, i, k: (i, k))
        dims = (((1,), (0,)), ((), ())) if mode == "nn" else (((1,), (1,)), ((), ()))

    if b_blocks > 1 and mode == "nn":
        per = b.shape[2] // tn
        assert b.shape[2] % tn == 0
        b_spec = pl.BlockSpec((None, tk, tn), lambda j, i, k: (j // per, k, j % per))
    elif b_blocks > 1:
        b_spec = pl.BlockSpec((b_blocks, tn, b.shape[2]), lambda j, i, k: (0, j, 0))
    elif mode == "nt":
        b_spec = pl.BlockSpec((tn, tk), lambda j, i, k: (j, k))
    else:
        b_spec = pl.BlockSpec((tk, tn), lambda j, i, k: (k, j))

    tile_spec = pl.BlockSpec((tm, tn), lambda j, i, k: (i, j))
    if out_blocks > 1:
        ncols = n // out_blocks
        assert ncols % tn == 0
        oper = ncols // tn
        out_spec = pl.BlockSpec((None, tm, tn), lambda j, i, k: (j // oper, i, j % oper))
        out_shape = [jax.ShapeDtypeStruct((out_blocks, m, ncols), dt) for dt in out_dtypes]
    else:
        out_spec = tile_spec
        out_shape = [jax.ShapeDtypeStruct((m, n), dt) for dt in out_dtypes]

    def body(a_ref, b_ref, *rest):
        extra_refs = rest[:n_extra]
        out_refs = rest[n_extra + n_after:n_extra + n_after + n_out]
        if mode == "nt" and b_blocks > 1:
            cs = b.shape[2]
            acc = None
            for jb in range(b_blocks):
                prod = lax.dot_general(a_ref[:, jb * cs:(jb + 1) * cs].astype(BF16), b_ref[jb].astype(BF16), dims,
                                       preferred_element_type=F32)
                acc = prod if acc is None else acc + prod
        else:
            acc = lax.dot_general(a_ref[...].astype(BF16), b_ref[...].astype(BF16), dims, preferred_element_type=F32)
        tiles = (acc,) if epilogue is None else epilogue(acc, *[r[...] for r in extra_refs])
        for o_ref, t in zip(out_refs, tiles, strict=True):
            o_ref[...] = t.astype(o_ref.dtype)

    outs = pl.pallas_call(
        body,
        name=name,
        grid=(n // tn, m // tm, 1),
        in_specs=[a_spec, b_spec] + [tile_spec] * n_extra + [ANY] * n_after,
        out_specs=[out_spec] * n_out,
        out_shape=out_shape,
        compiler_params=_params(3),
    )(a, b, *extras, *([] if after is None else [after]))
    return outs[0] if n_out == 1 else outs


def _add_residual(acc, res):
    return (acc + res,)


def _matmul_column_blocks(a, b4, blocks, out, *, tm, name, after=None):
    m, kdim = a.shape
    nb, _, cols = b4.shape
    tm = min(tm, m)
    assert m % tm == 0

    def body(j_ref, a_ref, b_ref, *rest):
        rest[-1][...] = jnp.dot(a_ref[...], b_ref[...], preferred_element_type=F32)

    extra = ([] if out is None else [out]) + ([] if after is None else [after])
    n_blocks = blocks.shape[0]
    return pl.pallas_call(
        body, name=name,
        grid_spec=pltpu.PrefetchScalarGridSpec(
            num_scalar_prefetch=1, grid=(n_blocks, m // tm),
            in_specs=[pl.BlockSpec((tm, kdim), lambda j, i, blk: (i, 0)),
                      pl.BlockSpec((None, kdim, cols), lambda j, i, blk: (blk[j], 0, 0))] + [ANY] * len(extra),
            out_specs=pl.BlockSpec((tm, cols), lambda j, i, blk: (i, blk[j]))),
        out_shape=jax.ShapeDtypeStruct((m, nb * cols), F32),
        input_output_aliases={} if out is None else {3: 0},
        compiler_params=_params(2),
    )(blocks, a, b4, *extra)


def _wgrad_half(a, b, core, *, theirs, row_sharded, tm, tn, name, add=None, after=None):
    kdim, m = a.shape
    n = b.shape[1]
    rs, cs = (m // N_CHIPS, n) if row_sharded else (m, n // N_CHIPS)
    rh = rs // 2
    tm, tn = min(tm, rh), min(tn, cs)
    assert rh % tm == 0 and cs % tn == 0, (name, rh, cs, tm, tn)
    mh, per = rh // tm, cs // tn
    has_add = add is not None

    def half(c):
        return 1 - c[0] if theirs else c[0]

    if row_sharded:
        grid = (n // tn, N_CHIPS * mh)
        a_spec = pl.BlockSpec((kdim, tm), lambda j, r, c: (0, ((r // mh) * 2 + half(c)) * mh + r % mh))
        o_spec = pl.BlockSpec((None, tm, tn), lambda j, r, c: (r // mh, r % mh, j))
    else:
        grid = (n // tn, mh)
        a_spec = pl.BlockSpec((kdim, tm), lambda j, r, c: (0, half(c) * mh + r))
        o_spec = pl.BlockSpec((None, tm, tn), lambda j, r, c: (j // per, r, j % per))
    b_spec = pl.BlockSpec((kdim, tn), lambda j, r, c: (0, j))

    def body(c_ref, a_ref, b_ref, *rest):
        o_ref = rest[-1]
        acc = lax.dot_general(a_ref[...].astype(BF16), b_ref[...].astype(BF16), (((0,), (0,)), ((), ())),
                              preferred_element_type=F32)
        if has_add:
            acc = acc + rest[0][...].astype(F32)
        o_ref[...] = acc.astype(BF16)

    operands = [a, b] + ([add] if has_add else []) + ([] if after is None else [after])
    return pl.pallas_call(
        body, name=name,
        grid_spec=pltpu.PrefetchScalarGridSpec(
            num_scalar_prefetch=1, grid=grid,
            in_specs=[a_spec, b_spec] + ([o_spec] if has_add else []) + ([] if after is None else [ANY]),
            out_specs=o_spec),
        out_shape=jax.ShapeDtypeStruct((N_CHIPS, rh, cs), BF16),
        compiler_params=_params(2),
    )(core, *operands)


def _rstd(x):
    return lax.rsqrt(jnp.mean(x * x, axis=-1, keepdims=True) + RMS_EPS)


def _rmsnorm(x, g, name):
    s, d = x.shape
    tr = _row_tile(s, 256)

    def body(x_ref, g_ref, o_ref):
        xv = x_ref[...]
        o_ref[...] = (xv * _rstd(xv) * g_ref[...]).astype(BF16)

    return pl.pallas_call(
        body, name=name, grid=(s // tr,),
        in_specs=[pl.BlockSpec((tr, d), lambda i: (i, 0)), pl.BlockSpec((1, d), lambda i: (0, 0))],
        out_specs=pl.BlockSpec((tr, d), lambda i: (i, 0)),
        out_shape=jax.ShapeDtypeStruct((s, d), BF16),
        compiler_params=_params(1),
    )(x, g)


def _rmsnorm_bwd(dh, x, g, dres, name):
    s, d = x.shape
    tr = _row_tile(s, 256)
    has_res = dres is not None

    def body(*refs):
        if has_res:
            dh_ref, x_ref, g_ref, res_ref, dx_ref, dxb_ref, dg_ref = refs
        else:
            dh_ref, x_ref, g_ref, dx_ref, dxb_ref, dg_ref = refs
        xv = x_ref[...]
        dhv = dh_ref[...].astype(F32)
        r = _rstd(xv)
        xn = xv * r
        dhg = dhv * g_ref[...]
        dx = r * (dhg - xn * jnp.mean(dhg * xn, axis=-1, keepdims=True))
        if has_res:
            dx = dx + res_ref[...]
        dx_ref[...] = dx
        dxb_ref[...] = dx.astype(BF16)
        part = jnp.sum(dhv * xn, axis=0, keepdims=True)

        @pl.when(pl.program_id(0) == 0)
        def _():
            dg_ref[...] = part

        @pl.when(pl.program_id(0) > 0)
        def _():
            dg_ref[...] += part

    row = pl.BlockSpec((tr, d), lambda i: (i, 0))
    vec = pl.BlockSpec((1, d), lambda i: (0, 0))
    return pl.pallas_call(
        body, name=name, grid=(s // tr,),
        in_specs=[row, row, vec] + ([row] if has_res else []),
        out_specs=[row, row, vec],
        out_shape=[jax.ShapeDtypeStruct((s, d), F32), jax.ShapeDtypeStruct((s, d), BF16), jax.ShapeDtypeStruct((1, d), F32)],
        compiler_params=_params(1),
    )(*([dh, x, g] + ([dres] if has_res else [])))


def _loss_head(x3, g, target):
    s, d = x3.shape
    tr = _row_tile(s, 256)

    def body(x_ref, g_ref, t_ref, dx_ref, dxb_ref, sq_ref, dg_ref):
        xv = x_ref[...]
        gv = g_ref[...]
        r = _rstd(xv)
        xn = xv * r
        err = xn * gv - t_ref[...]
        dy = err * (1.0 / d)
        dyg = dy * gv
        dx = r * (dyg - xn * jnp.mean(dyg * xn, axis=-1, keepdims=True))
        dx_ref[...] = dx
        dxb_ref[...] = dx.astype(BF16)
        sq = jnp.sum(jnp.sum(err * err, axis=1, keepdims=True), axis=0, keepdims=True)
        sq = jnp.broadcast_to(sq, (1, 128))
        part = jnp.sum(dy * xn, axis=0, keepdims=True)

        @pl.when(pl.program_id(0) == 0)
        def _():
            sq_ref[...] = sq
            dg_ref[...] = part

        @pl.when(pl.program_id(0) > 0)
        def _():
            sq_ref[...] += sq
            dg_ref[...] += part

    row = pl.BlockSpec((tr, d), lambda i: (i, 0))
    vec = pl.BlockSpec((1, d), lambda i: (0, 0))
    return pl.pallas_call(
        body, name="loss_head", grid=(s // tr,),
        in_specs=[row, vec, row],
        out_specs=[row, row, pl.BlockSpec((1, 128), lambda i: (0, 0)), vec],
        out_shape=[jax.ShapeDtypeStruct((s, d), F32), jax.ShapeDtypeStruct((s, d), BF16),
                   jax.ShapeDtypeStruct((1, 128), F32), jax.ShapeDtypeStruct((1, d), F32)],
        compiler_params=_params(1),
    )(x3, g, target)


def _rope_tables(s):
    inv = 1.0 / (ROPE_THETA ** (jnp.arange(0, HEAD_DIM, 2, dtype=F32) / HEAD_DIM))
    ang = jnp.arange(s, dtype=F32)[:, None] * inv[None, :]
    cos, sin = jnp.cos(ang), jnp.sin(ang)
    return jnp.concatenate([cos, cos], axis=1), jnp.concatenate([-sin, sin], axis=1)


def _swap_halves(t):
    return pltpu.roll(t, HEAD_DIM // 2, 1)


def _rope_fwd(z, cos_t, sin_t, after=None):
    s = z.shape[0]
    tr = _row_tile(s, 256)

    def body(zq_ref, zk_ref, zv_ref, c_ref, s_ref, *rest):
        q_ref, k_ref, v_ref = rest[-3:]
        c, sn = c_ref[...], s_ref[...]
        for hd in range(N_Q_HEADS):
            cols = slice(hd * HEAD_DIM, (hd + 1) * HEAD_DIM)
            t = zq_ref[:, cols]
            q_ref[:, cols] = (t * c + _swap_halves(t) * sn).astype(BF16)
        for hd in range(N_KV_HEADS):
            cols = slice(hd * HEAD_DIM, (hd + 1) * HEAD_DIM)
            t = zk_ref[:, cols]
            k_ref[:, cols] = (t * c + _swap_halves(t) * sn).astype(BF16)
        v_ref[...] = zv_ref[...].astype(BF16)

    tab = pl.BlockSpec((tr, HEAD_DIM), lambda i: (i, 0))
    return pl.pallas_call(
        body, name="rope_fwd", grid=(s // tr,),
        in_specs=[pl.BlockSpec((tr, ATTN_WIDTH), lambda i: (i, Q_OFF // ATTN_WIDTH)),
                  pl.BlockSpec((tr, KV_WIDTH), lambda i: (i, K_OFF // KV_WIDTH)),
                  pl.BlockSpec((tr, KV_WIDTH), lambda i: (i, V_OFF // KV_WIDTH)), tab, tab] + ([] if after is None else [ANY]),
        out_specs=[pl.BlockSpec((tr, ATTN_WIDTH), lambda i: (i, 0)), pl.BlockSpec((tr, KV_WIDTH), lambda i: (i, 0)),
                   pl.BlockSpec((tr, KV_WIDTH), lambda i: (i, 0))],
        out_shape=[jax.ShapeDtypeStruct((s, ATTN_WIDTH), BF16), jax.ShapeDtypeStruct((s, KV_WIDTH), BF16),
                   jax.ShapeDtypeStruct((s, KV_WIDTH), BF16)],
        compiler_params=_params(1),
    )(z, z, z, cos_t, sin_t, *([] if after is None else [after]))


def _rope_bwd(dq_rot, dk_rot, dv, cos_t, sin_t, dz):
    s = dq_rot.shape[0]
    tr = _row_tile(s, 256)
    qkv_width = V_OFF + KV_WIDTH

    def body(dq_ref, dk_ref, dv_ref, c_ref, s_ref, dz_in_ref, o_ref):
        c, sn = c_ref[...], s_ref[...]
        for hd in range(N_Q_HEADS):
            t = dq_ref[:, hd * HEAD_DIM:(hd + 1) * HEAD_DIM]
            o_ref[:, Q_OFF + hd * HEAD_DIM:Q_OFF + (hd + 1) * HEAD_DIM] = (t * c + _swap_halves(t * sn)).astype(BF16)
        for hd in range(N_KV_HEADS):
            t = dk_ref[:, hd * HEAD_DIM:(hd + 1) * HEAD_DIM]
            o_ref[:, K_OFF + hd * HEAD_DIM:K_OFF + (hd + 1) * HEAD_DIM] = (t * c + _swap_halves(t * sn)).astype(BF16)
        o_ref[:, V_OFF:V_OFF + KV_WIDTH] = dv_ref[...].astype(BF16)

    tab = pl.BlockSpec((tr, HEAD_DIM), lambda i: (i, 0))
    wide = pl.BlockSpec((tr, ATTN_WIDTH), lambda i: (i, 0))
    narrow = pl.BlockSpec((tr, KV_WIDTH), lambda i: (i, 0))
    return pl.pallas_call(
        body, name="rope_bwd", grid=(s // tr,),
        in_specs=[wide, narrow, narrow, tab, tab, ANY],
        out_specs=pl.BlockSpec((tr, qkv_width), lambda i: (i, 0)),
        out_shape=jax.ShapeDtypeStruct(dz.shape, dz.dtype),
        input_output_aliases={5: 0},
        compiler_params=_params(1),
    )(dq_rot, dk_rot, dv, cos_t, sin_t, dz)


def _swa_band(i, s):
    return pl.multiple_of(jnp.clip((i - 1) * BLOCK, 0, s - BAND), BLOCK)


SWA_HEADS_PER_PASS = Q_GROUP


def _swa_probs(q_ref, k_ref, sink_ref, heads, start, valid):
    kv = heads[0] // Q_GROUP
    cols = slice(kv * HEAD_DIM, (kv + 1) * HEAD_DIM)
    kb = k_ref[pl.ds(start, BAND), cols]
    qg = jnp.concatenate([q_ref[:, hd * HEAD_DIM:(hd + 1) * HEAD_DIM] for hd in heads], axis=0)
    sc = lax.dot_general(qg, kb, (((1,), (1,)), ((), ())), preferred_element_type=F32) * ATTN_SCALE
    sc = jnp.where(valid, sc, NEG_INF)
    sk = jnp.concatenate([jnp.full((BLOCK, 1), sink_ref[hd], F32) for hd in heads], axis=0)
    mx = jnp.maximum(jnp.max(sc, axis=1, keepdims=True), sk)
    e = jnp.exp(sc - mx)
    es = jnp.exp(sk - mx)
    inv = 1.0 / (jnp.sum(e, axis=1, keepdims=True) + es)
    return qg, kb, e * inv, es * inv


def _swa_head_passes():
    return [list(range(h0, h0 + SWA_HEADS_PER_PASS)) for h0 in range(0, N_Q_HEADS, SWA_HEADS_PER_PASS)]


def _swa_valid(i, start):
    q_pos = i * BLOCK + lax.broadcasted_iota(jnp.int32, (BLOCK, 1), 0)
    q_pos = jnp.concatenate([q_pos] * SWA_HEADS_PER_PASS, axis=0)
    k_pos = start + lax.broadcasted_iota(jnp.int32, (1, BAND), 1)
    return jnp.abs(k_pos - q_pos) <= WINDOW


def _swa_fwd(q, k, v, sink):
    s = q.shape[0]
    assert s % BLOCK == 0 and s >= BAND

    def body(sink_ref, q_ref, k_ref, v_ref, o_ref):
        i = pl.program_id(0)
        start = _swa_band(i, s)
        valid = _swa_valid(i, start)
        for heads in _swa_head_passes():
            kv = heads[0] // Q_GROUP
            _, _, p, _ = _swa_probs(q_ref, k_ref, sink_ref, heads, start, valid)
            vb = v_ref[pl.ds(start, BAND), kv * HEAD_DIM:(kv + 1) * HEAD_DIM]
            o = jnp.dot(p.astype(BF16), vb, preferred_element_type=F32)
            for g, hd in enumerate(heads):
                o_ref[:, hd * HEAD_DIM:(hd + 1) * HEAD_DIM] = o[g * BLOCK:(g + 1) * BLOCK].astype(BF16)

    whole = pl.BlockSpec((s, KV_WIDTH), lambda i: (0, 0))
    blk = pl.BlockSpec((BLOCK, ATTN_WIDTH), lambda i: (i, 0))
    return pl.pallas_call(
        body, name="swa_fwd", grid=(s // BLOCK,),
        in_specs=[pl.BlockSpec(memory_space=pltpu.SMEM), blk, whole, whole],
        out_specs=blk,
        out_shape=jax.ShapeDtypeStruct((s, ATTN_WIDTH), BF16),
        compiler_params=_params(1),
    )(sink, q, k, v)


def _swa_bwd(q, k, v, d_out, sink):
    s = q.shape[0]

    def body(sink_ref, q_ref, k_ref, v_ref, do_ref, dq_ref, dk_ref, dv_ref, dsink_ref):
        i = pl.program_id(0)

        @pl.when(i == 0)
        def _():
            dk_ref[...] = jnp.zeros_like(dk_ref)
            dv_ref[...] = jnp.zeros_like(dv_ref)
            dsink_ref[...] = jnp.zeros_like(dsink_ref)

        start = _swa_band(i, s)
        valid = _swa_valid(i, start)
        for heads in _swa_head_passes():
            kv = heads[0] // Q_GROUP
            cols = slice(kv * HEAD_DIM, (kv + 1) * HEAD_DIM)
            qg, kb, p, p_sink = _swa_probs(q_ref, k_ref, sink_ref, heads, start, valid)
            vb = v_ref[pl.ds(start, BAND), cols]
            dog = jnp.concatenate([do_ref[:, hd * HEAD_DIM:(hd + 1) * HEAD_DIM] for hd in heads], axis=0)
            dp = lax.dot_general(dog, vb, (((1,), (1,)), ((), ())), preferred_element_type=F32)
            delta = jnp.sum(p * dp, axis=1, keepdims=True)
            ds = (p * (dp - delta) * ATTN_SCALE).astype(BF16)
            dqg = jnp.dot(ds, kb, preferred_element_type=F32)
            dk_ref[pl.ds(start, BAND), cols] += lax.dot_general(ds, qg, (((0,), (0,)), ((), ())), preferred_element_type=F32)
            dv_ref[pl.ds(start, BAND), cols] += lax.dot_general(p.astype(BF16), dog, (((0,), (0,)), ((), ())),
                                                                 preferred_element_type=F32)
            dsk = p_sink * delta
            for g, hd in enumerate(heads):
                dq_ref[:, hd * HEAD_DIM:(hd + 1) * HEAD_DIM] = dqg[g * BLOCK:(g + 1) * BLOCK]
                tot = jnp.sum(dsk[g * BLOCK:(g + 1) * BLOCK], axis=0, keepdims=True)
                dsink_ref[hd:hd + 1, :] -= jnp.broadcast_to(tot, (1, 128))

    whole = pl.BlockSpec((s, KV_WIDTH), lambda i: (0, 0))
    blk = pl.BlockSpec((BLOCK, ATTN_WIDTH), lambda i: (i, 0))
    return pl.pallas_call(
        body, name="swa_bwd", grid=(s // BLOCK,),
        in_specs=[pl.BlockSpec(memory_space=pltpu.SMEM), blk, whole, whole, blk],
        out_specs=[blk, whole, whole, pl.BlockSpec((N_Q_HEADS, 128), lambda i: (0, 0))],
        out_shape=[jax.ShapeDtypeStruct((s, ATTN_WIDTH), F32), jax.ShapeDtypeStruct((s, KV_WIDTH), F32),
                   jax.ShapeDtypeStruct((s, KV_WIDTH), F32), jax.ShapeDtypeStruct((N_Q_HEADS, 128), F32)],
        compiler_params=_params(1),
    )(sink, q, k, v, d_out)


CONV_CHUNK = 256


def _shift_rows(t, rows, down):
    n = t.shape[0]
    rolled = pltpu.roll(t, 1 if down else n - 1, 0)
    edge = 0 if down else n - 1
    return jnp.where(rows == edge, 0.0, rolled)


def _conv_specs(s):
    def z_spec(off):
        return pl.BlockSpec((s, CONV_CHUNK), lambda j, off=off: (0, off // CONV_CHUNK + j))
    chunk = pl.BlockSpec((s, CONV_CHUNK), lambda j: (0, j))
    w_spec = pl.BlockSpec((3, CONV_CHUNK), lambda j: (0, j))
    return z_spec(CU_OFF), z_spec(CB_OFF), z_spec(CC_OFF), chunk, w_spec


def _conv_fwd(z, conv_w, after=None):
    s = z.shape[0]
    cu_spec, cb_spec, cc_spec, chunk, w_spec = _conv_specs(s)

    def body(cu_ref, cb_ref, cc_ref, w_ref, *rest):
        o_ref = rest[-1]
        rows = lax.broadcasted_iota(jnp.int32, (s, 1), 0)
        t = cc_ref[...] * cu_ref[...]
        c3 = _shift_rows(t, rows, True) * w_ref[0:1, :] + t * w_ref[1:2, :] + _shift_rows(t, rows, False) * w_ref[2:3, :]
        o_ref[...] = (cb_ref[...] * c3).astype(BF16)

    return pl.pallas_call(
        body, name="conv_fwd", grid=(CONV_WIDTH // CONV_CHUNK,),
        in_specs=[cu_spec, cb_spec, cc_spec, w_spec] + ([] if after is None else [ANY]),
        out_specs=chunk,
        out_shape=jax.ShapeDtypeStruct((s, CONV_WIDTH), BF16),
        compiler_params=_params(1),
    )(z, z, z, conv_w, *([] if after is None else [after]))


def _conv_bwd(z, conv_w, d_co, dz):
    s = z.shape[0]
    cu_spec, cb_spec, cc_spec, chunk, w_spec = _conv_specs(s)
    n_chunks = CONV_WIDTH // CONV_CHUNK
    offsets = (CU_OFF, CB_OFF, CC_OFF)

    def body(cu_ref, cb_ref, cc_ref, w_ref, d_ref, dz_in_ref, dz_ref, dw_ref, buf, sems):
        j = pl.program_id(0)

        def copies(j_at):
            return [pltpu.make_async_copy(buf.at[h], dz_ref.at[:, pl.ds(off + j_at * CONV_CHUNK, CONV_CHUNK)], sems.at[h])
                    for h, off in enumerate(offsets)]

        rows = lax.broadcasted_iota(jnp.int32, (s, 1), 0)
        cu, cc = cu_ref[...], cc_ref[...]
        t = cc * cu
        t_dn, t_up = _shift_rows(t, rows, True), _shift_rows(t, rows, False)
        c3 = t_dn * w_ref[0:1, :] + t * w_ref[1:2, :] + t_up * w_ref[2:3, :]
        d = d_ref[...]
        dc3 = d * cb_ref[...]
        dw_ref[0:1, :] = jnp.sum(dc3 * t_dn, axis=0, keepdims=True)
        dw_ref[1:2, :] = jnp.sum(dc3 * t, axis=0, keepdims=True)
        dw_ref[2:3, :] = jnp.sum(dc3 * t_up, axis=0, keepdims=True)
        dt = _shift_rows(dc3, rows, False) * w_ref[0:1, :] + dc3 * w_ref[1:2, :] + _shift_rows(dc3, rows, True) * w_ref[2:3, :]

        @pl.when(j > 0)
        def _():
            for cp in copies(j):
                cp.wait()

        buf[0] = (dt * cc).astype(BF16)
        buf[1] = (d * c3).astype(BF16)
        buf[2] = (dt * cu).astype(BF16)
        for cp in copies(j):
            cp.start()

        @pl.when(j == n_chunks - 1)
        def _():
            for cp in copies(j):
                cp.wait()

    return pl.pallas_call(
        body, name="conv_bwd", grid=(n_chunks,),
        in_specs=[cu_spec, cb_spec, cc_spec, w_spec, chunk, ANY],
        out_specs=[ANY, w_spec],
        out_shape=[jax.ShapeDtypeStruct(dz.shape, dz.dtype), jax.ShapeDtypeStruct((3, CONV_WIDTH), F32)],
        input_output_aliases={5: 0},
        scratch_shapes=[pltpu.VMEM((3, s, CONV_CHUNK), BF16), pltpu.SemaphoreType.DMA((3,))],
        compiler_params=_params(1),
    )(z, z, z, conv_w, d_co, dz)


GATE_CHUNK = 512


def _gate_specs(s, d, tr):
    n_chunks = d // GATE_CHUNK
    za = pl.BlockSpec((tr, GATE_CHUNK), lambda j, i: (i, GL_OFF // GATE_CHUNK + j))
    zc = pl.BlockSpec((tr, GATE_CHUNK), lambda j, i: (i, GL_OFF // GATE_CHUNK + n_chunks + j))
    ba = pl.BlockSpec((1, GATE_CHUNK), lambda j, i: (0, j))
    bc = pl.BlockSpec((1, GATE_CHUNK), lambda j, i: (0, n_chunks + j))
    tile = pl.BlockSpec((tr, GATE_CHUNK), lambda j, i: (i, j))
    return za, zc, ba, bc, tile


def _gate_fwd(z, b_gate, ya, yc):
    s, d = ya.shape
    tr = _row_tile(s, 512)
    za, zc, ba, bc, tile = _gate_specs(s, d, tr)

    def body(za_ref, zc_ref, ba_ref, bc_ref, ya_ref, yc_ref, o_ref):
        ga = jax.nn.sigmoid(za_ref[...] + ba_ref[...])
        gc = jax.nn.sigmoid(zc_ref[...] + bc_ref[...])
        o_ref[...] = (ga * ya_ref[...] + gc * yc_ref[...]).astype(BF16)

    return pl.pallas_call(
        body, name="gate_fwd", grid=(d // GATE_CHUNK, s // tr),
        in_specs=[za, zc, ba, bc, tile, tile],
        out_specs=tile,
        out_shape=jax.ShapeDtypeStruct((s, d), BF16),
        compiler_params=_params(2),
    )(z, z, b_gate, b_gate, ya, yc)


def _gate_bwd(z, b_gate, ya, yc, dmix):
    s, d = ya.shape
    tr = _row_tile(s, 512)
    za, zc, ba, bc, tile = _gate_specs(s, d, tr)
    vec = pl.BlockSpec((1, GATE_CHUNK), lambda j, i: (0, j))
    n_rows = s // tr
    in_width = z.shape[1]

    def body(za_ref, zc_ref, ba_ref, bc_ref, ya_ref, yc_ref, dm_ref, dya_ref, dyc_ref, dz_ref, dba_ref, dbc_ref, buf, sems):
        j, i = pl.program_id(0), pl.program_id(1)

        def copies(j_at, i_at):
            rows = pl.ds(i_at * tr, tr)
            return [pltpu.make_async_copy(buf.at[h], dz_ref.at[rows, pl.ds(GL_OFF + h * d + j_at * GATE_CHUNK, GATE_CHUNK)],
                                          sems.at[h]) for h in range(2)]

        ga = jax.nn.sigmoid(za_ref[...] + ba_ref[...])
        gc = jax.nn.sigmoid(zc_ref[...] + bc_ref[...])
        dm = dm_ref[...]
        dya_ref[...] = (dm * ga).astype(BF16)
        dyc_ref[...] = (dm * gc).astype(BF16)
        dla = dm * ya_ref[...] * ga * (1.0 - ga)
        dlc = dm * yc_ref[...] * gc * (1.0 - gc)

        @pl.when(j * n_rows + i > 0)
        def _():
            for cp in copies(j, i):
                cp.wait()

        buf[0] = dla.astype(BF16)
        buf[1] = dlc.astype(BF16)
        for cp in copies(j, i):
            cp.start()

        @pl.when((j == d // GATE_CHUNK - 1) & (i == n_rows - 1))
        def _():
            for cp in copies(j, i):
                cp.wait()

        pa = jnp.sum(dla, axis=0, keepdims=True)
        pc = jnp.sum(dlc, axis=0, keepdims=True)

        @pl.when(i == 0)
        def _():
            dba_ref[...] = pa
            dbc_ref[...] = pc

        @pl.when(i > 0)
        def _():
            dba_ref[...] += pa
            dbc_ref[...] += pc

    big = jax.ShapeDtypeStruct((s, d), BF16)
    small = jax.ShapeDtypeStruct((1, d), F32)
    return pl.pallas_call(
        body, name="gate_bwd", grid=(d // GATE_CHUNK, n_rows),
        in_specs=[za, zc, ba, bc, tile, tile, tile],
        out_specs=[tile, tile, ANY, vec, vec],
        out_shape=[big, big, jax.ShapeDtypeStruct((s, in_width), BF16), small, small],
        scratch_shapes=[pltpu.VMEM((2, tr, GATE_CHUNK), BF16), pltpu.SemaphoreType.DMA((2,))],
        compiler_params=_params(2),
    )(z, z, b_gate, b_gate, ya, yc, dmix)


def _cross_probs(q_ref, kv_ref, hd):
    cols = slice(hd * HEAD_DIM, (hd + 1) * HEAD_DIM)
    qh = q_ref[:, cols]
    kh = kv_ref[:, cols]
    sc = lax.dot_general(qh, kh, (((1,), (1,)), ((), ())), preferred_element_type=F32) * ATTN_SCALE
    e = jnp.exp(sc - jnp.max(sc, axis=1, keepdims=True))
    return qh, kh, e * (1.0 / jnp.sum(e, axis=1, keepdims=True))


def _cross_fwd(qc, kvc):
    s = qc.shape[0]
    n_mem = kvc.shape[0]
    tq = _row_tile(s, 256)

    def body(q_ref, kv_ref, o_ref):
        for hd in range(MEM_HEADS):
            _, _, p = _cross_probs(q_ref, kv_ref, hd)
            vh = kv_ref[:, MEM_WIDTH + hd * HEAD_DIM:MEM_WIDTH + (hd + 1) * HEAD_DIM]
            o_ref[:, hd * HEAD_DIM:(hd + 1) * HEAD_DIM] = jnp.dot(p.astype(BF16), vh, preferred_element_type=F32).astype(BF16)

    return pl.pallas_call(
        body, name="cross_fwd", grid=(s // tq,),
        in_specs=[pl.BlockSpec((tq, MEM_WIDTH), lambda i: (i, 0)), pl.BlockSpec((n_mem, 2 * MEM_WIDTH), lambda i: (0, 0))],
        out_specs=pl.BlockSpec((tq, MEM_WIDTH), lambda i: (i, 0)),
        out_shape=jax.ShapeDtypeStruct((s, MEM_WIDTH), BF16),
        compiler_params=_params(1),
    )(qc, kvc)


def _cross_bwd(qc, kvc, d_out):
    s = qc.shape[0]
    n_mem = kvc.shape[0]
    tq = _row_tile(s, 256)

    def body(q_ref, kv_ref, do_ref, dq_ref, dkv_ref):
        @pl.when(pl.program_id(0) == 0)
        def _():
            dkv_ref[...] = jnp.zeros_like(dkv_ref)

        for hd in range(MEM_HEADS):
            cols = slice(hd * HEAD_DIM, (hd + 1) * HEAD_DIM)
            vcols = slice(MEM_WIDTH + hd * HEAD_DIM, MEM_WIDTH + (hd + 1) * HEAD_DIM)
            qh, kh, p = _cross_probs(q_ref, kv_ref, hd)
            doh = do_ref[:, cols]
            dp = lax.dot_general(doh, kv_ref[:, vcols], (((1,), (1,)), ((), ())), preferred_element_type=F32)
            ds = (p * (dp - jnp.sum(p * dp, axis=1, keepdims=True)) * ATTN_SCALE).astype(BF16)
            dq_ref[:, cols] = jnp.dot(ds, kh, preferred_element_type=F32).astype(BF16)
            dkv_ref[:, cols] += lax.dot_general(ds, qh, (((0,), (0,)), ((), ())), preferred_element_type=F32)
            dkv_ref[:, vcols] += lax.dot_general(p.astype(BF16), doh, (((0,), (0,)), ((), ())), preferred_element_type=F32)

    qspec = pl.BlockSpec((tq, MEM_WIDTH), lambda i: (i, 0))
    kvspec = pl.BlockSpec((n_mem, 2 * MEM_WIDTH), lambda i: (0, 0))
    return pl.pallas_call(
        body, name="cross_bwd", grid=(s // tq,),
        in_specs=[qspec, kvspec, qspec],
        out_specs=[qspec, kvspec],
        out_shape=[jax.ShapeDtypeStruct((s, MEM_WIDTH), BF16), jax.ShapeDtypeStruct((n_mem, 2 * MEM_WIDTH), F32)],
        compiler_params=_params(1),
    )(qc, kvc, d_out)


def _swiglu_fwd(up, gate):
    return up, (gate * jax.nn.sigmoid(gate)) * up


def _swiglu_bwd(d_act, gate, up):
    sg = jax.nn.sigmoid(gate)
    silu = gate * sg
    return d_act * up * (sg * (1.0 + gate * (1.0 - sg))), d_act * silu


GATHER_GROUPS = {"in": ("w_in", "conv_w"), "mid": ("w_attn_out", "w_conv_out", "w_o", "w_cq", "w_ckv", "w_co"),
                 "gate": ("w_gate",), "up": ("w_up",), "down": ("w_down",)}


def _local_step(xs, mems, target, small, fetch, reduce):
    s, d = xs.shape
    w4 = {}
    cos_t, sin_t = _rope_tables(s)

    def near(group, done, then, after):
        waits = [("direct", group)] + ([("pass_near", done), ("pass_far", done)] if done else [])
        starts = [("forward", group), ("pass_near", group)] + [("direct", g) for g in then]
        tok = fetch.step("gather_near_" + group, waits, starts, after)
        if done:
            w4.update(fetch.arrays(done))
        return tok

    def far(group, then, after):
        return fetch.step("gather_far_" + group, [("forward", group)], [("pass_far", group)] + [("direct", g) for g in then], after)

    def last(group, after):
        tok = fetch.step("gather_done_" + group, [("pass_near", group), ("pass_far", group)], [], after)
        w4.update(fetch.arrays(group))
        return tok

    h = _rmsnorm(xs, small["g_mix"], "norm_mix")
    slots_filled = [a for g in ("gate", "up", "down") for a in fetch.arrays(g).values()]
    chip_x, chip_y = reduce.place[0] // 2, reduce.place[0] % 2
    own_block = jnp.stack([2 * chip_x + chip_y]).astype(jnp.int32)
    near_blocks = jnp.stack([2 * (1 - chip_x) + chip_y, 2 * chip_x + (1 - chip_y)]).astype(jnp.int32)
    far_block = jnp.stack([2 * (1 - chip_x) + (1 - chip_y)]).astype(jnp.int32)
    z = _matmul_column_blocks(h, fetch.arrays("in")["w_in"], own_block, None, tm=512, name="in_proj_own")
    tok = near("in", None, ["mid"], [z] + slots_filled)
    tok = fetch.step("gather_near_done_in", [("pass_near", "in")], [], tok)
    z = _matmul_column_blocks(h, fetch.arrays("in")["w_in"], near_blocks, z, tm=512, name="in_proj_near", after=tok)
    tok = far("in", [], z)
    tok = fetch.step("gather_done_in", [("pass_far", "in")], [], tok)
    w4.update(fetch.arrays("in"))
    z = _matmul_column_blocks(h, w4["w_in"], far_block, z, tm=512, name="in_proj_far", after=tok)
    conv4 = w4["conv_w"]
    conv_w = conv4[:, :3, :].transpose(1, 0, 2).reshape(3, N_CHIPS * conv4.shape[2])
    c_in = w4["w_in"].shape[2]
    tok = near("mid", None, ["gate"], z)
    q_rot, k_rot, v_b = _rope_fwd(z, cos_t, sin_t, after=tok)
    attn = _swa_fwd(q_rot, k_rot, v_b, small["sink"])
    co = _conv_fwd(z, conv_w, after=tok)
    tok = far("mid", ["up"], attn)
    tok = last("mid", tok)
    w_o = w4["w_o"].reshape(-1, w4["w_o"].shape[-1])
    c_d = w4["w_attn_out"].shape[2]
    ya = _matmul(attn, w4["w_attn_out"], mode="nn", tm=1024, tn=c_d, tk=ATTN_WIDTH, out_dtypes=[F32], name="attn_out_proj",
                 b_blocks=N_CHIPS, after=tok)
    yc = _matmul(co, w4["w_conv_out"], mode="nn", tm=1024, tn=c_d, tk=CONV_WIDTH, out_dtypes=[F32], name="conv_out_proj",
                 b_blocks=N_CHIPS)
    mix = _gate_fwd(z, small["b_gate"], ya, yc)
    x1 = _matmul(mix, w_o, mode="nn", tm=512, tn=1024, tk=d, out_dtypes=[F32], name="mix_out_proj", extras=[xs],
                 epilogue=_add_residual)
    tok = near("gate", None, ["down"], x1)
    w_cq = w4["w_cq"].reshape(-1, w4["w_cq"].shape[-1])
    w_ckv = w4["w_ckv"].reshape(-1, w4["w_ckv"].shape[-1])
    hc = _rmsnorm(x1, small["g_cross"], "norm_cross")
    memn = _rmsnorm(mems, small["g_mem"], "norm_mem")
    qc = _matmul(hc, w_cq, mode="nn", tm=1024, tn=MEM_WIDTH, tk=d, out_dtypes=[BF16], name="cross_q_proj", after=tok)
    kvc = _matmul(memn, w_ckv, mode="nn", tm=256, tn=2 * MEM_WIDTH, tk=d, out_dtypes=[BF16], name="cross_kv_proj")
    oc = _cross_fwd(qc, kvc)
    tok = far("gate", [], oc)
    x2 = _matmul(oc, w4["w_co"], mode="nn", tm=1024, tn=c_d, tk=MEM_WIDTH, out_dtypes=[F32], name="cross_out_proj",
                 extras=[x1], epilogue=_add_residual, b_blocks=N_CHIPS, after=tok)
    hf = _rmsnorm(x2, small["g_ffn"], "norm_ffn")
    tok = near("up", "gate", [], hf)
    c_ff = w4["w_gate"].shape[2]
    gate = _matmul(hf, w4["w_gate"], mode="nn", tm=512, tn=c_ff, tk=d, out_dtypes=[F32], name="ffn_gate_proj", b_blocks=N_CHIPS,
                   after=tok)
    tok = far("up", [], gate)
    tok = near("down", "up", [], tok)
    up, act = _matmul(hf, w4["w_up"], mode="nn", tm=512, tn=c_ff, tk=d, out_dtypes=[F32, BF16], name="ffn_up_proj",
                      extras=[gate], epilogue=_swiglu_fwd, b_blocks=N_CHIPS, after=tok)
    tok = far("down", [], act)
    last("down", tok)
    w_down = w4["w_down"].reshape(-1, w4["w_down"].shape[-1])
    x3 = _matmul(act, w_down, mode="nn", tm=512, tn=512, tk=w_down.shape[0], out_dtypes=[F32], name="ffn_down_proj", extras=[x2],
                 epilogue=_add_residual)
    dx3, dx3b, sq, dg_final = _loss_head(x3, small["g_final"], target)

    da, du = _matmul(dx3b, w_down, mode="nt", tm=512, tn=c_ff, tk=d, out_dtypes=[BF16, BF16], name="ffn_down_bwd",
                     extras=[gate, up], epilogue=_swiglu_bwd)
    core = reduce.core
    ffn_shape = dict(row_sharded=False, tm=1024, tn=c_ff)
    g_down = _matmul(act, dx3b, mode="tn", tm=c_ff, tn=1024, tk=s, out_dtypes=[BF16], name="ffn_down_wgrad")
    tok = reduce.add("down", {"w_down": g_down}, da)
    t_gate = _wgrad_half(hf, da, core, theirs=True, name="ffn_gate_wgrad_theirs", after=tok, **ffn_shape)
    tok = reduce.step("down", t_gate)
    t_up = _wgrad_half(hf, du, core, theirs=True, name="ffn_up_wgrad_theirs", after=tok, **ffn_shape)
    tok = reduce.send("ffn", {"w_gate": t_gate, "w_up": t_up}, dx3b)
    dhf = _matmul(da, w4["w_gate"], mode="nt", tm=512, tn=1024, tk=N_CHIPS * c_ff, out_dtypes=[F32], name="ffn_gate_bwd", b_blocks=N_CHIPS,
                  after=tok)
    got = reduce.received("ffn", dhf)
    p_gate = _wgrad_half(hf, da, core, theirs=False, name="ffn_gate_wgrad_mine", add=got["w_gate"], **ffn_shape)
    p_up = _wgrad_half(hf, du, core, theirs=False, name="ffn_up_wgrad_mine", add=got["w_up"], **ffn_shape)
    tok = reduce.add_parts("ffn", {"w_gate": p_gate, "w_up": p_up})
    dhf = _matmul(du, w4["w_up"], mode="nt", tm=512, tn=1024, tk=N_CHIPS * c_ff, out_dtypes=[F32], name="ffn_up_bwd", extras=[dhf],
                  epilogue=_add_residual, b_blocks=N_CHIPS, after=tok)
    tok = reduce.step("down", dhf)
    dx2, dx2b, dg_ffn = _rmsnorm_bwd(dhf, x2, small["g_ffn"], dx3, "norm_ffn_bwd")

    d_oc = _matmul(dx2b, w4["w_co"], mode="nt", tm=1024, tn=MEM_WIDTH, tk=d, out_dtypes=[BF16], name="cross_out_bwd",
                   b_blocks=N_CHIPS, after=tok)
    g_co = _matmul(oc, dx2b, mode="tn", tm=MEM_WIDTH, tn=c_d, tk=s, out_dtypes=[BF16], name="cross_out_wgrad", out_blocks=N_CHIPS)
    tok = reduce.step("down", g_co)
    dqc, dkvc = _cross_bwd(qc, kvc, d_oc)
    g_cq = _matmul(hc, dqc, mode="tn", tm=1024, tn=MEM_WIDTH, tk=s, out_dtypes=[BF16], name="cross_q_wgrad", after=tok)
    dhc = _matmul(dqc, w_cq, mode="nt", tm=1024, tn=1024, tk=MEM_WIDTH, out_dtypes=[F32], name="cross_q_bwd")
    g_ckv = _matmul(memn, dkvc, mode="tn", tm=1024, tn=2 * MEM_WIDTH, tk=mems.shape[0], out_dtypes=[BF16], name="cross_kv_wgrad")
    dmemn = _matmul(dkvc, w_ckv, mode="nt", tm=256, tn=1024, tk=2 * MEM_WIDTH, out_dtypes=[F32], name="cross_kv_bwd")
    _, _, dg_mem = _rmsnorm_bwd(dmemn, mems, small["g_mem"], None, "norm_mem_bwd")
    dx1, dx1b, dg_cross = _rmsnorm_bwd(dhc, x1, small["g_cross"], dx2, "norm_cross_bwd")

    dmix = _matmul(dx1b, w_o, mode="nt", tm=512, tn=1024, tk=d, out_dtypes=[F32], name="mix_out_bwd")
    g_o = _matmul(mix, dx1b, mode="tn", tm=1024, tn=1024, tk=s, out_dtypes=[BF16], name="mix_out_wgrad")
    dya, dyc, dz, db_a, db_c = _gate_bwd(z, small["b_gate"], ya, yc, dmix)
    d_attn = _matmul(dya, w4["w_attn_out"], mode="nt", tm=1024, tn=ATTN_WIDTH, tk=d, out_dtypes=[BF16], name="attn_out_bwd",
                     b_blocks=N_CHIPS)
    g_ao = _matmul(attn, dya, mode="tn", tm=ATTN_WIDTH, tn=c_d, tk=s, out_dtypes=[BF16], name="attn_out_wgrad", out_blocks=N_CHIPS)
    d_co = _matmul(dyc, w4["w_conv_out"], mode="nt", tm=1024, tn=CONV_WIDTH, tk=d, out_dtypes=[F32], name="conv_out_bwd",
                   b_blocks=N_CHIPS)
    g_cvo = _matmul(co, dyc, mode="tn", tm=CONV_WIDTH, tn=c_d, tk=s, out_dtypes=[BF16], name="conv_out_wgrad", out_blocks=N_CHIPS)
    tok = reduce.step("ffn", g_cvo)
    tok = reduce.add("mid", {"w_co": g_co, "w_cq": g_cq, "w_ckv": g_ckv, "w_o": g_o, "w_attn_out": g_ao, "w_conv_out": g_cvo}, tok)
    dz, d_conv_w = _conv_bwd(z, conv_w, d_co, dz)
    dq_rot, dk_rot, dv, dsink = _swa_bwd(q_rot, k_rot, v_b, d_attn, small["sink"])
    tok = reduce.step("ffn", dq_rot)
    dz = _rope_bwd(dq_rot, dk_rot, dv, cos_t, sin_t, dz)
    in_shape = dict(row_sharded=False, tm=1024, tn=c_in)
    t_in = _wgrad_half(h, dz, core, theirs=True, name="in_proj_wgrad_theirs", after=tok, **in_shape)
    tok = reduce.send("in", {"w_in": t_in}, dk_rot)
    tok = reduce.step("mid", tok)
    got = reduce.received("in", tok)
    p_in = _wgrad_half(h, dz, core, theirs=False, name="in_proj_wgrad_mine", add=got["w_in"], **in_shape)
    tok = reduce.add_parts("in", {"w_in": p_in})
    dh = _matmul(dz, w4["w_in"], mode="nt", tm=512, tn=512, tk=N_CHIPS * c_in, out_dtypes=[F32], name="in_proj_bwd", b_blocks=N_CHIPS,
                 after=tok)
    tok = reduce.step("mid", dh)
    grad_x, _, dg_mix = _rmsnorm_bwd(dh, xs, small["g_mix"], dx1, "norm_mix_bwd")

    small_grads = {
        "g_mix": dg_mix, "sink": dsink[:, 0], "b_gate": jnp.concatenate([db_a, db_c], axis=1), "g_cross": dg_cross,
        "g_mem": dg_mem, "g_ffn": dg_ffn, "g_final": dg_final, "conv_w": d_conv_w,
    }
    return sq, grad_x, small_grads


def _pair_sum(g4, ra, core, name):
    nb, rs, cs = g4.shape
    rh = rs // 2
    tr = _row_tile(rh, 256)
    per = rh // tr

    def body(c_ref, g_ref, r_ref, o_ref):
        o_ref[...] = (g_ref[...].astype(F32) + r_ref[...].astype(F32)).astype(BF16)

    plain = pl.BlockSpec((None, tr, cs), lambda j, i, c: (j, i, 0))
    return pl.pallas_call(
        body, name=name,
        grid_spec=pltpu.PrefetchScalarGridSpec(
            num_scalar_prefetch=1, grid=(nb, per),
            in_specs=[pl.BlockSpec((None, tr, cs), lambda j, i, c: (j, c[0] * per + i, 0)), plain],
            out_specs=plain),
        out_shape=jax.ShapeDtypeStruct((nb, rh, cs), BF16),
        compiler_params=_params(2),
    )(core, g4, ra)


def _quad_sum(parts, rc, place, name):
    _, rh, cs = parts.shape
    tr = _row_tile(rh, 256)
    per = rh // tr

    def body(p_ref, own_ref, r_ref, o_ref):
        acc = own_ref[...].astype(F32)
        for j in range(rc.shape[0]):
            acc = acc + r_ref[j].astype(F32)
        o_ref[...] = acc

    return pl.pallas_call(
        body, name=name,
        grid_spec=pltpu.PrefetchScalarGridSpec(
            num_scalar_prefetch=1, grid=(per,),
            in_specs=[pl.BlockSpec((None, tr, cs), lambda i, p: (p[0], i, 0)),
                      pl.BlockSpec((rc.shape[0], tr, cs), lambda i, p: (0, i, 0))],
            out_specs=pl.BlockSpec((tr, cs), lambda i, p: (p[1] * per + i, 0))),
        out_shape=jax.ShapeDtypeStruct((2 * rh, cs), F32),
        compiler_params=_params(1),
    )(place, parts, rc)


def _adamw_update(w, g, m, v):
    nm = ADAM_B1 * m + (1.0 - ADAM_B1) * g
    nv = ADAM_B2 * v + (1.0 - ADAM_B2) * (g * g)
    m_hat = nm / ADAM_C1
    v_hat = nv / ADAM_C2
    return -ADAM_LR * (m_hat / (jnp.sqrt(v_hat) + ADAM_EPS) + ADAM_WD * w), nm, nv


def _adamw_own_half(w, m, v, parts, rc, place, name, after=None):
    rows, cols = w.shape
    rh = rows // 2
    tr = _row_tile(rh, 256)
    per = rh // tr

    def body(p_ref, w_ref, m_ref, v_ref, own_ref, r_ref, *rest):
        gx_ref, g_ref, d_ref, nm_ref, nv_ref = rest[-5:]
        g = own_ref[...].astype(F32)
        for j in range(rc.shape[0]):
            g = g + r_ref[j].astype(F32)
        gx_ref[...] = g
        g_ref[...] = g
        d_ref[...], nm_ref[...], nv_ref[...] = _adamw_update(w_ref[...], g, m_ref[...], v_ref[...])

    mine = pl.BlockSpec((tr, cols), lambda i, p: (p[1] * per + i, 0))
    shape = jax.ShapeDtypeStruct((rows, cols), F32)
    return pl.pallas_call(
        body, name=name,
        grid_spec=pltpu.PrefetchScalarGridSpec(
            num_scalar_prefetch=1, grid=(per,),
            in_specs=[mine, mine, mine, pl.BlockSpec((None, tr, cols), lambda i, p: (p[0], i, 0)),
                      pl.BlockSpec((rc.shape[0], tr, cols), lambda i, p: (0, i, 0))] + ([] if after is None else [ANY]),
            out_specs=[mine] * 5),
        out_shape=[shape] * 5,
        compiler_params=_params(1),
    )(place, w, m, v, parts, rc, *([] if after is None else [after]))


def _adamw_other_half(w, m, v, g_exchanged, g, delta, new_m, new_v, place, name, after=None):
    rows, cols = w.shape
    rh = rows // 2
    tr = _row_tile(rh, 256)
    per = rh // tr

    def body(p_ref, w_ref, m_ref, v_ref, gx_ref, *rest):
        g_ref, d_ref, nm_ref, nv_ref = rest[-4:]
        gv = gx_ref[...]
        g_ref[...] = gv
        d_ref[...], nm_ref[...], nv_ref[...] = _adamw_update(w_ref[...], gv, m_ref[...], v_ref[...])

    other = pl.BlockSpec((tr, cols), lambda i, p: ((1 - p[1]) * per + i, 0))
    shape = jax.ShapeDtypeStruct((rows, cols), F32)
    n_after = 0 if after is None else 1
    return pl.pallas_call(
        body, name=name,
        grid_spec=pltpu.PrefetchScalarGridSpec(
            num_scalar_prefetch=1, grid=(per,),
            in_specs=[other] * 4 + [ANY] * (4 + n_after),
            out_specs=[other] * 4),
        out_shape=[shape] * 4,
        input_output_aliases={5: 0, 6: 1, 7: 2, 8: 3},
        compiler_params=_params(1),
    )(place, w, m, v, g_exchanged, g, delta, new_m, new_v, *([] if after is None else [after]))


def _cast_to_slot(w, place, dtype, name, after=None):
    rows, cols = w.shape
    tr = _row_tile(rows, 1024)

    def body(p_ref, w_ref, *rest):
        o_ref = rest[-1]
        o_ref[...] = w_ref[...].astype(dtype)

    return pl.pallas_call(
        body, name=name,
        grid_spec=pltpu.PrefetchScalarGridSpec(
            num_scalar_prefetch=1, grid=(rows // tr,),
            in_specs=[pl.BlockSpec((tr, cols), lambda i, p: (i, 0))] + ([] if after is None else [ANY]),
            out_specs=pl.BlockSpec((None, tr, cols), lambda i, p: (p[0], i, 0))),
        out_shape=jax.ShapeDtypeStruct((N_CHIPS, rows, cols), dtype),
        compiler_params=_params(1),
    )(place, w, *([] if after is None else [after]))


SC_TILES = 32
SC_SLAB_ROWS = 8
SC_LANES = 16


def _adamw_on_sparsecores(w, g, m, v, name):
    rows, cols = w.shape
    slabs = rows // SC_SLAB_ROWS
    n_chunks = 1
    while (slabs * n_chunks) % (2 * SC_TILES) or cols % (n_chunks * 128):
        n_chunks += 1
    width = cols // n_chunks
    per_tile = slabs * n_chunks // SC_TILES

    def body(w_hbm, g_hbm, m_hbm, v_hbm, go_hbm, d_hbm, nm_hbm, nv_hbm, wb, gb, mb, vb, load_sems, store_sems):
        tile = lax.axis_index("sc_tile") * 2 + lax.axis_index("sc_core")
        sources = (w_hbm, g_hbm, m_hbm, v_hbm)
        sinks = (d_hbm, go_hbm, nm_hbm, nv_hbm)
        bufs = (wb, gb, mb, vb)

        def window(k):
            piece = tile * per_tile + k
            rs = pl.ds(pl.multiple_of((piece // n_chunks) * SC_SLAB_ROWS, SC_SLAB_ROWS), SC_SLAB_ROWS)
            cs = pl.ds(pl.multiple_of((piece % n_chunks) * width, 128), width)
            return rs, cs

        def loads(k, b):
            rs, cs = window(k)
            return [pltpu.make_async_copy(src.at[rs, cs], buf.at[b], load_sems.at[b]) for src, buf in zip(sources, bufs)]

        def stores(k, b):
            rs, cs = window(k)
            return [pltpu.make_async_copy(buf.at[b], dst.at[rs, cs], store_sems.at[b]) for dst, buf in zip(sinks, bufs)]

        for cp in loads(0, 0):
            cp.start()

        @pl.loop(0, per_tile, step=2)
        def _(k0):
            for b in range(2):
                k = k0 + b

                @pl.when(k + 1 < per_tile)
                def _():
                    @pl.when(k >= 1)
                    def _():
                        for cp in stores(k - 1, 1 - b):
                            cp.wait()
                    for cp in loads(k + 1, 1 - b):
                        cp.start()

                for cp in loads(k, b):
                    cp.wait()
                for r in range(SC_SLAB_ROWS):
                    @pl.loop(0, width, step=SC_LANES)
                    def _(c):
                        lanes = pl.ds(c, SC_LANES)
                        gv = gb[b, r, lanes]
                        nm = ADAM_B1 * mb[b, r, lanes] + (1.0 - ADAM_B1) * gv
                        nv = ADAM_B2 * vb[b, r, lanes] + (1.0 - ADAM_B2) * (gv * gv)
                        mb[b, r, lanes] = nm
                        vb[b, r, lanes] = nv
                        wb[b, r, lanes] = -ADAM_LR * ((nm / ADAM_C1) / (jnp.sqrt(nv / ADAM_C2) + ADAM_EPS)
                                                      + ADAM_WD * wb[b, r, lanes])
                for cp in stores(k, b):
                    cp.start()

        for b in range(2):
            for cp in stores(per_tile - 2 + b, b):
                cp.wait()

    shape = jax.ShapeDtypeStruct((rows, cols), F32)
    buf = pltpu.VMEM((2, SC_SLAB_ROWS, width), F32)
    return pl.kernel(body, name=name, out_type=[shape] * 4,
                     mesh=plsc.VectorSubcoreMesh(core_axis_name="sc_core", subcore_axis_name="sc_tile"),
                     scratch_types=[buf] * 4 + [pltpu.SemaphoreType.DMA((2,)), pltpu.SemaphoreType.DMA((2,))])(w, g, m, v)


def _adamw(w, g, m, v, name, after=None):
    rows, cols = w.shape
    tr = _row_tile(rows, 256)

    def body(w_ref, g_ref, m_ref, v_ref, *rest):
        go_ref, d_ref, nm_ref, nv_ref = rest[-4:]
        gv = g_ref[...]
        go_ref[...] = gv
        d_ref[...], nm_ref[...], nv_ref[...] = _adamw_update(w_ref[...], gv, m_ref[...], v_ref[...])

    tile = pl.BlockSpec((tr, cols), lambda i: (i, 0))
    shape = jax.ShapeDtypeStruct((rows, cols), F32)
    return pl.pallas_call(
        body, name=name, grid=(rows // tr,),
        in_specs=[tile] * 4 + ([] if after is None else [ANY]), out_specs=[tile] * 4, out_shape=[shape] * 4,
        compiler_params=_params(1),
    )(w, g, m, v, *([] if after is None else [after]))


def _mesh_pos():
    return lax.axis_index("x"), lax.axis_index("y"), lax.axis_index("c")


def _other_chips(x, y):
    return [(1 - x, y), (x, 1 - y), (1 - x, 1 - y)]


def _half_rows(ref, which):
    rh = ref.shape[-2] // 2
    return ref.at[pl.ds(which * rh, rh), :]


def _remote(src, dst, send_sems, recv_sems, sem, to):
    return pltpu.make_async_remote_copy(src_ref=src, dst_ref=dst, send_sem=send_sems.at[sem], recv_sem=recv_sems.at[sem],
                                        device_id=to, device_id_type=MESH)


HBM = pl.BlockSpec(memory_space=pltpu.HBM)
SEM = pl.BlockSpec(memory_space=pltpu.SEMAPHORE)
DATAFLOW_EFFECT = pltpu.SideEffectType.DATAFLOW_SIDE_EFFECTING


def _in_hbm(arrays):
    return [pltpu.with_memory_space_constraint(a, pltpu.HBM) for a in arrays]


def _hbm_like(arrays):
    return [pltpu.HBM(a.shape, a.dtype) for a in arrays]


GATHER_COPIES_PER_ARRAY = {"direct": 2, "forward": 2, "pass_near": 2, "pass_far": 1}


def _gather_copies(kind, refs, x, y, c):
    me, near_x, near_y, far = 2 * x + y, 2 * (1 - x) + y, 2 * x + (1 - y), 2 * (1 - x) + (1 - y)
    to_x, to_y, sibling = (1 - x, y, c), (x, 1 - y, c), (x, y, 1 - c)
    out = []
    for ref in refs:
        rh = ref.shape[1] // 2
        rq = rh // 2

        def half(chip, ref=ref, rh=rh):
            return ref.at[chip, pl.ds(c * rh, rh), :]

        def quarter(chip, q, ref=ref, rh=rh, rq=rq):
            return ref.at[chip, pl.ds(c * rh + q * rq, rq), :]

        if kind == "direct":
            out += [(half(me), half(me), to_x), (half(me), half(me), to_y)]
        elif kind == "forward":
            out += [(quarter(near_x, 0), quarter(near_x, 0), to_y), (quarter(near_y, 1), quarter(near_y, 1), to_x)]
        elif kind == "pass_near":
            out += [(half(near_x), half(near_x), sibling), (half(near_y), half(near_y), sibling)]
        else:
            assert kind == "pass_far"
            out += [(half(far), half(far), sibling)]
    return out


def _gather_step(name, bufs, waits, starts, after):
    nb, nw, ns = len(bufs), len(waits), len(starts)
    after = [] if after is None else list(after) if isinstance(after, (list, tuple)) else [after]
    n_after = len(after)

    def body(*refs):
        ins = refs[:nb]
        wait_sems = refs[nb:nb + 2 * nw]
        start_sems = refs[nb + 2 * nw + n_after:nb + 2 * nw + n_after + 2 * ns]
        token = refs[-1]
        x, y, c = _mesh_pos()
        for j, (kind, idxs, _, _) in enumerate(waits):
            for i, (s_ref, d_ref, to) in enumerate(_gather_copies(kind, [ins[t] for t in idxs], x, y, c)):
                came = _remote(s_ref, d_ref, wait_sems[2 * j], wait_sems[2 * j + 1], i, to)
                came.wait_recv()
                came.wait_send()
        for j, (kind, idxs) in enumerate(starts):
            for i, (s_ref, d_ref, to) in enumerate(_gather_copies(kind, [ins[t] for t in idxs], x, y, c)):
                _remote(s_ref, d_ref, start_sems[2 * j], start_sems[2 * j + 1], i, to).start()
        token[...] = jnp.zeros_like(token)

    sems = []
    for kind, idxs in starts:
        sems += [pltpu.SemaphoreType.DMA((GATHER_COPIES_PER_ARRAY[kind] * len(idxs),))] * 2
    operands = _in_hbm(bufs) + [sem for w in waits for sem in w[2:]] + after
    outs = pl.pallas_call(
        body, name=name,
        in_specs=[HBM] * nb + [SEM] * (2 * nw) + [ANY] * n_after,
        out_specs=[SEM] * (2 * ns) + [HBM] * nb + [pl.BlockSpec(memory_space=pltpu.VMEM)],
        out_shape=sems + _hbm_like(bufs) + [jax.ShapeDtypeStruct((8, 128), F32)],
        input_output_aliases={i: 2 * ns + i for i in range(nb)},
        compiler_params=pltpu.CompilerParams(has_side_effects=DATAFLOW_EFFECT),
    )(*operands)
    return outs[2 * ns:2 * ns + nb], [(outs[2 * j], outs[2 * j + 1]) for j in range(ns)], outs[-1]


class _Gather:
    def __init__(self, groups):
        self.groups = groups
        self.bufs = {}
        self.in_flight = {}

    def put(self, slotted):
        self.bufs.update(slotted)

    def step(self, name, waits, starts, after=None):
        names = []
        for _, group in list(waits) + list(starts):
            names += [n for n in self.groups[group] if n not in names]
        index = {n: i for i, n in enumerate(names)}

        def members(group):
            return [index[n] for n in self.groups[group]]

        wait_args = [(kind, members(group)) + self.in_flight.pop((kind, group)) for kind, group in waits]
        start_args = [(kind, members(group)) for kind, group in starts]
        bufs, sems, token = _gather_step(name, [self.bufs[n] for n in names], wait_args, start_args, after)
        self.bufs.update(zip(names, bufs))
        for (kind, group), pair in zip(starts, sems):
            self.in_flight[(kind, group)] = pair
        return token

    def arrays(self, group):
        return {n: self.bufs[n] for n in self.groups[group]}


def _sibling_halves_copies(srcs, dsts, x, y, c):
    out = []
    for s_ref, d_ref in zip(srcs, dsts, strict=True):
        rh = s_ref.shape[1] // 2
        out.append((s_ref.at[:, pl.ds((1 - c) * rh, rh), :], d_ref, (x, y, 1 - c)))
    return out


def _to_sibling_copies(srcs, dsts, x, y, c):
    return [(s_ref, d_ref, (x, y, 1 - c)) for s_ref, d_ref in zip(srcs, dsts, strict=True)]


def _chip_copies(srcs, dsts, x, y, c):
    out = []
    for s_ref, d_ref in zip(srcs, dsts, strict=True):
        for k, (px, py) in enumerate(_other_chips(x, y)):
            out.append((s_ref.at[2 * px + py], d_ref.at[k], (px, py, c)))
    return out


def _join_copies(srcs, dsts, x, y, c):
    out = []
    for s_ref in srcs:
        mine = _half_rows(s_ref, c)
        out.append((mine, mine, (x, y, 1 - c)))
    return out


def _exchange_start(copies_fn, n_copies, srcs, fresh, after, name):
    ns, nb = len(srcs), len(srcs) + len(fresh)

    def body(*refs):
        bufs, send, recv, token = refs[:nb], refs[nb + 1], refs[nb + 2], refs[-1]
        x, y, c = _mesh_pos()
        for i, (s_ref, d_ref, to) in enumerate(copies_fn(bufs[:ns], bufs[ns:] if fresh else bufs[:ns], x, y, c)):
            _remote(s_ref, d_ref, send, recv, i, to).start()
        token[...] = jnp.zeros_like(token)

    sems = [pltpu.SemaphoreType.DMA((n_copies,))] * 2
    outs = pl.pallas_call(
        body, name=name,
        in_specs=[HBM] * nb + [ANY], out_specs=[SEM, SEM] + [HBM] * nb + [pl.BlockSpec(memory_space=pltpu.VMEM)],
        out_shape=sems + _hbm_like(list(srcs) + list(fresh)) + [jax.ShapeDtypeStruct((8, 128), F32)],
        input_output_aliases={i: 2 + i for i in range(nb)},
        compiler_params=pltpu.CompilerParams(has_side_effects=DATAFLOW_EFFECT),
    )(*_in_hbm(list(srcs) + list(fresh)), after)
    return outs[0], outs[1], outs[2:2 + ns], outs[2 + ns:2 + nb], outs[-1]


def _exchange_done(copies_fn, srcs, fresh, send, recv, after, name):
    ns, nb = len(srcs), len(srcs) + len(fresh)

    def body(*refs):
        bufs, send_in, recv_in = refs[:nb], refs[nb], refs[nb + 1]
        x, y, c = _mesh_pos()
        for i, (s_ref, d_ref, to) in enumerate(copies_fn(bufs[:ns], bufs[ns:] if fresh else bufs[:ns], x, y, c)):
            came = _remote(s_ref, d_ref, send_in, recv_in, i, to)
            came.wait_send()
            came.wait_recv()

    outs = pl.pallas_call(
        body, name=name,
        in_specs=[HBM] * nb + [SEM, SEM, ANY], out_specs=[HBM] * nb,
        out_shape=_hbm_like(list(srcs) + list(fresh)),
        input_output_aliases={i: i for i in range(nb)},
        compiler_params=pltpu.CompilerParams(has_side_effects=DATAFLOW_EFFECT),
    )(*_in_hbm(list(srcs) + list(fresh)), send, recv, after)
    return outs[:ns], outs[ns:]


SPARSECORE_ADAMW_GROUPS = ("down", "ffn")


class _Reduce:
    def __init__(self, place, core, shards, mom_m, mom_v):
        self.place, self.core = place, core
        self.shards, self.mom_m, self.mom_v = shards, mom_m, mom_v
        self.state = {}
        self.results = {}

    def add(self, group, grads, after):
        names = list(grads)
        g4s = [g.reshape((N_CHIPS, -1, g.shape[-1])) if g.ndim == 2 else g for g in grads.values()]
        fresh = [lax.empty((N_CHIPS, g.shape[1] // 2, g.shape[2]), BF16) for g in g4s]
        send, recv, g4s, fresh, token = _exchange_start(_sibling_halves_copies, len(names), g4s, fresh, after,
                                                        "pair_start_" + group)
        self.state[group] = (0, names, send, recv, g4s, fresh)
        return token

    def send(self, group, theirs, after):
        names, srcs = list(theirs), list(theirs.values())
        fresh = [lax.empty(s.shape, BF16) for s in srcs]
        send, recv, srcs, fresh, token = _exchange_start(_to_sibling_copies, len(names), srcs, fresh, after, "pair_start_" + group)
        self.state[group] = ("sent", names, send, recv, srcs, fresh)
        return token

    def received(self, group, after):
        stage, names, send, recv, srcs, fresh = self.state.pop(group)
        assert stage == "sent"
        _, got = _exchange_done(_to_sibling_copies, srcs, fresh, send, recv, after, "pair_done_" + group)
        return dict(zip(names, got))

    def add_parts(self, group, parts):
        names, srcs = list(parts), list(parts.values())
        fresh = [lax.empty((N_CHIPS - 1,) + p.shape[1:], BF16) for p in srcs]
        send, recv, srcs, fresh, token = _exchange_start(_chip_copies, 3 * len(names), srcs, fresh, self.core, "chips_start_" + group)
        self.state[group] = (1, names, send, recv, srcs, fresh)
        return token

    def step(self, group, after):
        stage, names, send, recv, srcs, fresh = self.state[group]
        if stage == 0:
            g4s, ras = _exchange_done(_sibling_halves_copies, srcs, fresh, send, recv, after, "pair_done_" + group)
            parts = [_pair_sum(g, r, self.core, "pair_sum_" + n) for g, r, n in zip(g4s, ras, names)]
            fresh = [lax.empty((N_CHIPS - 1,) + p.shape[1:], BF16) for p in parts]
            send, recv, parts, fresh, token = _exchange_start(_chip_copies, 3 * len(names), parts, fresh, self.core,
                                                              "chips_start_" + group)
            self.state[group] = (1, names, send, recv, parts, fresh)
            return token
        if stage == 1:
            parts, rcs = _exchange_done(_chip_copies, srcs, fresh, send, recv, after, "chips_done_" + group)
            if group in SPARSECORE_ADAMW_GROUPS:
                wholes = [_quad_sum(p, r, self.place, "quad_sum_" + n) for p, r, n in zip(parts, rcs, names)]
                send, recv, wholes, _, token = _exchange_start(_join_copies, len(names), wholes, [], self.core, "join_start_" + group)
                self.state[group] = (2, names, send, recv, wholes, [])
                return token
            token = None
            for n, p, r in zip(names, parts, rcs):
                self.results[n] = _adamw_own_half(self.shards[n], self.mom_m[n], self.mom_v[n], p, r, self.place,
                                                  "adamw_own_" + n, after=token)
                token = self.results[n][2]
            wholes = [self.results[n][0] for n in names]
            send, recv, wholes, _, token = _exchange_start(_join_copies, len(names), wholes, [], token, "join_start_" + group)
            self.state[group] = (2, names, send, recv, wholes, [])
            return token
        assert stage == 2
        wholes, _ = _exchange_done(_join_copies, srcs, [], send, recv, after, "join_done_" + group)
        if group in SPARSECORE_ADAMW_GROUPS:
            for n, g in zip(names, wholes):
                self.results[n] = _adamw_on_sparsecores(self.shards[n], g, self.mom_m[n], self.mom_v[n], "adamw_sc_" + n)
            del self.state[group]
            return wholes[0]
        token = None
        for n, exchanged in zip(names, wholes):
            _, g, d, nm, nv = self.results[n]
            self.results[n] = _adamw_other_half(self.shards[n], self.mom_m[n], self.mom_v[n], exchanged, g, d, nm, nv,
                                                self.place, "adamw_other_" + n, after=token)
            token = self.results[n][1]
        del self.state[group]
        return token


N_DEV = 8


def _all_reduce_small(v):
    def body(v_ref, o_ref, slots, send_sems, recv_sems):
        x, y, c = _mesh_pos()
        me = 4 * x + 2 * y + c
        slots[me] = v_ref[...]
        peers = []
        for r in range(1, N_DEV):
            fx, fy, fc = (r >> 2) & 1, (r >> 1) & 1, r & 1
            peers.append((x + fx - 2 * x * fx, y + fy - 2 * y * fy, c + fc - 2 * c * fc))
        sends = []
        for r, peer in enumerate(peers):
            cp = _remote(v_ref, slots.at[me], send_sems, recv_sems, r, peer)
            cp.start()
            sends.append(cp)
        for r, (px, py, pc) in enumerate(peers):
            landed = slots.at[4 * px + 2 * py + pc]
            _remote(landed, landed, send_sems, recv_sems, r, (px, py, pc)).wait_recv()
        for cp in sends:
            cp.wait_send()
        acc = slots[0]
        for i in range(1, N_DEV):
            acc = acc + slots[i]
        o_ref[...] = acc

    vm = pl.BlockSpec(memory_space=pltpu.VMEM)
    return pl.pallas_call(
        body, name="small_grads_all_reduce",
        in_specs=[vm], out_specs=vm,
        out_shape=jax.ShapeDtypeStruct(v.shape, v.dtype),
        scratch_shapes=[pltpu.VMEM((N_DEV,) + v.shape, v.dtype), pltpu.SemaphoreType.DMA((N_DEV - 1,)),
                        pltpu.SemaphoreType.DMA((N_DEV - 1,))],
    )(v)


MATRICES = ("w_in", "w_attn_out", "w_conv_out", "w_o", "w_cq", "w_ckv", "w_co", "w_gate", "w_up", "w_down")
VECTORS = ("g_mix", "b_gate", "g_cross", "g_mem", "g_ffn", "g_final", "conv_w", "sink")
WEIGHT_ORDER = ("g_mix", "w_in", "sink", "conv_w", "b_gate", "w_attn_out", "w_conv_out", "w_o", "g_cross", "g_mem", "w_cq",
                "w_ckv", "w_co", "g_ffn", "w_gate", "w_up", "w_down", "g_final")
CONV_PAD_ROWS = 32
SMALL_ROWS = 8


def _pack(pieces):
    flat = jnp.concatenate([p.reshape(-1) for p in pieces])
    lane_group = SMALL_ROWS * 128
    total = -(-flat.shape[0] // lane_group) * lane_group
    flat = jnp.pad(flat, (0, total - flat.shape[0]))
    return flat.reshape(SMALL_ROWS, total // SMALL_ROWS), [p.size for p in pieces]


def _unpack(packed, pieces):
    flat = packed.reshape(-1)
    out, off = [], 0
    for p in pieces:
        out.append(flat[off:off + p.size].reshape(p.shape))
        off += p.size
    return out


def kernel(x, mem, g_mix, w_in, sink, conv_w, b_gate, w_attn_out, w_conv_out, w_o, g_cross, g_mem, w_cq, w_ckv, w_co, g_ffn, w_gate, w_up, w_down, g_final, loss_target, m_g_mix, m_w_in, m_sink, m_conv_w, m_b_gate, m_w_attn_out, m_w_conv_out, m_w_o, m_g_cross, m_g_mem, m_w_cq, m_w_ckv, m_w_co, m_g_ffn, m_w_gate, m_w_up, m_w_down, m_g_final, v_g_mix, v_w_in, v_sink, v_conv_w, v_b_gate, v_w_attn_out, v_w_conv_out, v_w_o, v_g_cross, v_g_mem, v_w_cq, v_w_ckv, v_w_co, v_g_ffn, v_w_gate, v_w_up, v_w_down, v_g_final):
    given = dict(g_mix=g_mix, w_in=w_in, sink=sink, conv_w=conv_w, b_gate=b_gate, w_attn_out=w_attn_out, w_conv_out=w_conv_out,
                 w_o=w_o, g_cross=g_cross, g_mem=g_mem, w_cq=w_cq, w_ckv=w_ckv, w_co=w_co, g_ffn=g_ffn, w_gate=w_gate, w_up=w_up,
                 w_down=w_down, g_final=g_final)
    mom_m = dict(g_mix=m_g_mix, w_in=m_w_in, sink=m_sink, conv_w=m_conv_w, b_gate=m_b_gate, w_attn_out=m_w_attn_out,
                 w_conv_out=m_w_conv_out, w_o=m_w_o, g_cross=m_g_cross, g_mem=m_g_mem, w_cq=m_w_cq, w_ckv=m_w_ckv, w_co=m_w_co,
                 g_ffn=m_g_ffn, w_gate=m_w_gate, w_up=m_w_up, w_down=m_w_down, g_final=m_g_final)
    mom_v = dict(g_mix=v_g_mix, w_in=v_w_in, sink=v_sink, conv_w=v_conv_w, b_gate=v_b_gate, w_attn_out=v_w_attn_out,
                 w_conv_out=v_w_conv_out, w_o=v_w_o, g_cross=v_g_cross, g_mem=v_g_mem, w_cq=v_w_cq, w_ckv=v_w_ckv, w_co=v_w_co,
                 g_ffn=v_g_ffn, w_gate=v_w_gate, w_up=v_w_up, w_down=v_w_down, g_final=v_g_final)
    xs, mems, target = x[0], mem[0], loss_target[0]
    d_model = xs.shape[1]
    chip = 2 * lax.axis_index("x") + lax.axis_index("y")
    core = jnp.reshape(lax.axis_index("c"), (1,)).astype(jnp.int32)
    place = jnp.stack([chip, lax.axis_index("c")]).astype(jnp.int32)

    shards = {n: given[n][0] for n in MATRICES}
    conv_cols = conv_w.shape[2]
    conv_pad = jnp.pad(conv_w[0], ((0, CONV_PAD_ROWS - conv_w.shape[1]), (0, 0)))
    fetch = _Gather(GATHER_GROUPS)
    first = {"w_in": _cast_to_slot(shards["w_in"], place, BF16, "to_slot_w_in"),
             "conv_w": _cast_to_slot(conv_pad, place, F32, "to_slot_conv_w")}
    fetch.put(first)
    tok = fetch.step("gather_start", [], [("direct", "in")])
    fetch.put({n: _cast_to_slot(shards[n], place, BF16, "to_slot_" + n, after=tok) for n in MATRICES if n != "w_in"})
    small = {n: given[n] for n in ("g_mix", "b_gate", "g_cross", "g_mem", "g_ffn")}
    small["g_final"] = g_final[None]
    small["sink"] = sink[0]

    reduce = _Reduce(place, core, shards, {n: mom_m[n][0] for n in MATRICES}, {n: mom_v[n][0] for n in MATRICES})
    sq, grad_x, small_grads = _local_step(xs, mems, target, small, fetch, reduce)

    loss_part = 0.5 * sq[0:1, 0:1] / d_model
    pieces = [small_grads[n] for n in VECTORS] + [loss_part]
    packed, _ = _pack(pieces)
    summed = _unpack(_all_reduce_small(packed), pieces)
    loss = summed[-1][0, 0]
    small_sum = dict(zip(VECTORS, summed[:-1]))
    small_sum["conv_w"] = lax.dynamic_slice_in_dim(small_sum["conv_w"], chip * conv_cols, conv_cols, axis=1)

    grad_out, delta, new_m, new_v = {}, {}, {}, {}
    like = [given[n] for n in VECTORS]
    pw, _ = _pack(like)
    pg, _ = _pack([small_sum[n] for n in VECTORS])
    pm, _ = _pack([mom_m[n] for n in VECTORS])
    pv, _ = _pack([mom_v[n] for n in VECTORS])
    tok = reduce.step("mid", pg)
    tok = reduce.step("in", tok)
    _, pd, pnm, pnv = _adamw(pw, pg, pm, pv, "adamw_small", after=tok)
    for n, g, d, nm, nv in zip(VECTORS, [small_sum[n] for n in VECTORS], _unpack(pd, like), _unpack(pnm, like), _unpack(pnv, like)):
        grad_out[n] = g.reshape(given[n].shape)
        delta[n], new_m[n], new_v[n] = d, nm, nv
    reduce.step("in", pd)
    for n in MATRICES:
        g, d, nm, nv = reduce.results[n]
        grad_out[n], delta[n], new_m[n], new_v[n] = g[None], d[None], nm[None], nv[None]

    return (loss, grad_x[None], *[grad_out[n] for n in WEIGHT_ORDER], *[delta[n] for n in WEIGHT_ORDER],
            *[new_m[n] for n in WEIGHT_ORDER], *[new_v[n] for n in WEIGHT_ORDER])
```

```python
import functools

import jax
import jax.numpy as jnp
from jax import lax
from jax.experimental import pallas as pl
from jax.experimental.pallas import tpu as pltpu

F32 = jnp.float32
BF16 = jnp.bfloat16
MESH = pl.DeviceIdType.MESH
ANY = pl.BlockSpec(memory_space=pl.ANY)

VMEM_LIMIT_BYTES = 56 * 1024 * 1024

N_CHIPS = 4
HEAD_DIM = 128
N_Q_HEADS = 8
N_KV_HEADS = 2
Q_GROUP = N_Q_HEADS // N_KV_HEADS
ATTN_WIDTH = N_Q_HEADS * HEAD_DIM
KV_WIDTH = N_KV_HEADS * HEAD_DIM
WINDOW = 128
BLOCK = 128
BAND = 3 * BLOCK
ROPE_THETA = 10000.0
CONV_WIDTH = 1024
MEM_HEADS = 4
MEM_WIDTH = MEM_HEADS * HEAD_DIM
RMS_EPS = 1e-6
NEG_INF = -1e30
ATTN_SCALE = HEAD_DIM ** -0.5

Q_OFF, K_OFF, V_OFF, CU_OFF, CB_OFF, CC_OFF, GL_OFF = 0, 1024, 1280, 1536, 2560, 3584, 4608

ADAM_LR = 0.001
ADAM_B1 = 0.9
ADAM_B2 = 0.999
ADAM_EPS = 1e-08
ADAM_WD = 0.01
ADAM_STEP = 10
ADAM_C1 = 1.0 - ADAM_B1 ** ADAM_STEP
ADAM_C2 = 1.0 - ADAM_B2 ** ADAM_STEP


def _params(n_grid_axes):
    return pltpu.CompilerParams(dimension_semantics=("arbitrary",) * n_grid_axes, vmem_limit_bytes=VMEM_LIMIT_BYTES)


BF16_SUBLANES = 16


def _row_tile(rows, want):
    if rows <= want:
        return rows
    for t in range(want, 0, -BF16_SUBLANES):
        if rows % t == 0:
            return t
    return rows


def _matmul(a, b, *, mode, tm, tn, tk, out_dtypes, name, extras=(), epilogue=None, b_blocks=1, out_blocks=1, after=None):
    if mode == "tn":
        kdim, m = a.shape
    else:
        m, kdim = a.shape
    if b_blocks > 1:
        nb, brows, bcols = b.shape
        assert nb == b_blocks
        if mode == "nn":
            n = bcols * nb
            assert brows == kdim
        else:
            assert mode == "nt" and bcols * nb == kdim
            n = brows
    else:
        n = b.shape[0] if mode == "nt" else b.shape[1]
    tm, tn = min(tm, m), min(tn, n)
    assert m % tm == 0 and n % tn == 0 and tk == kdim, (name, m, n, kdim, tm, tn, tk)
    n_extra, n_out = len(extras), len(out_dtypes)
    n_after = 0 if after is None else 1

    if mode == "tn":
        a_spec = pl.BlockSpec((tk, tm), lambda j, i, k: (k, i))
        dims = (((0,), (0,)), ((), ()))
    else:
        a_spec = pl.BlockSpec((tm, tk), lambda j, i, k: (i, k))
        dims = (((1,), (0,)), ((), ())) if mode == "nn" else (((1,), (1,)), ((), ()))

    if b_blocks > 1 and mode == "nn":
        per = b.shape[2] // tn
        assert b.shape[2] % tn == 0
        b_spec = pl.BlockSpec((None, tk, tn), lambda j, i, k: (j // per, k, j % per))
    elif b_blocks > 1:
        b_spec = pl.BlockSpec((b_blocks, tn, b.shape[2]), lambda j, i, k: (0, j, 0))
    elif mode == "nt":
        b_spec = pl.BlockSpec((tn, tk), lambda j, i, k: (j, k))
    else:
        b_spec = pl.BlockSpec((tk, tn), lambda j, i, k: (k, j))

    tile_spec = pl.BlockSpec((tm, tn), lambda j, i, k: (i, j))
    if out_blocks > 1:
        ncols = n // out_blocks
        assert ncols % tn == 0
        oper = ncols // tn
        out_spec = pl.BlockSpec((None, tm, tn), lambda j, i, k: (j // oper, i, j % oper))
        out_shape = [jax.ShapeDtypeStruct((out_blocks, m, ncols), dt) for dt in out_dtypes]
    else:
        out_spec = tile_spec
        out_shape = [jax.ShapeDtypeStruct((m, n), dt) for dt in out_dtypes]

    def body(a_ref, b_ref, *rest):
        extra_refs = rest[:n_extra]
        out_refs = rest[n_extra + n_after:n_extra + n_after + n_out]
        if mode == "nt" and b_blocks > 1:
            cs = b.shape[2]
            acc = None
            for jb in range(b_blocks):
                prod = lax.dot_general(a_ref[:, jb * cs:(jb + 1) * cs].astype(BF16), b_ref[jb].astype(BF16), dims,
                                       preferred_element_type=F32)
                acc = prod if acc is None else acc + prod
        else:
            acc = lax.dot_general(a_ref[...].astype(BF16), b_ref[...].astype(BF16), dims, preferred_element_type=F32)
        tiles = (acc,) if epilogue is None else epilogue(acc, *[r[...] for r in extra_refs])
        for o_ref, t in zip(out_refs, tiles, strict=True):
            o_ref[...] = t.astype(o_ref.dtype)

    outs = pl.pallas_call(
        body,
        name=name,
        grid=(n // tn, m // tm, 1),
        in_specs=[a_spec, b_spec] + [tile_spec] * n_extra + [ANY] * n_after,
        out_specs=[out_spec] * n_out,
        out_shape=out_shape,
        compiler_params=_params(3),
    )(a, b, *extras, *([] if after is None else [after]))
    return outs[0] if n_out == 1 else outs


def _add_residual(acc, res):
    return (acc + res,)


def _matmul_column_blocks(a, b4, blocks, out, *, tm, name, after=None):
    m, kdim = a.shape
    nb, _, cols = b4.shape
    tm = min(tm, m)
    assert m % tm == 0

    def body(j_ref, a_ref, b_ref, *rest):
        rest[-1][...] = jnp.dot(a_ref[...], b_ref[...], preferred_element_type=F32)

    extra = ([] if out is None else [out]) + ([] if after is None else [after])
    n_blocks = blocks.shape[0]
    return pl.pallas_call(
        body, name=name,
        grid_spec=pltpu.PrefetchScalarGridSpec(
            num_scalar_prefetch=1, grid=(n_blocks, m // tm),
            in_specs=[pl.BlockSpec((tm, kdim), lambda j, i, blk: (i, 0)),
                      pl.BlockSpec((None, kdim, cols), lambda j, i, blk: (blk[j], 0, 0))] + [ANY] * len(extra),
            out_specs=pl.BlockSpec((tm, cols), lambda j, i, blk: (i, blk[j]))),
        out_shape=jax.ShapeDtypeStruct((m, nb * cols), F32),
        input_output_aliases={} if out is None else {3: 0},
        compiler_params=_params(2),
    )(blocks, a, b4, *extra)


def _wgrad_half(a, b, core, *, theirs, row_sharded, tm, tn, name, add=None, after=None):
    kdim, m = a.shape
    n = b.shape[1]
    rs, cs = (m // N_CHIPS, n) if row_sharded else (m, n // N_CHIPS)
    rh = rs // 2
    tm, tn = min(tm, rh), min(tn, cs)
    assert rh % tm == 0 and cs % tn == 0, (name, rh, cs, tm, tn)
    mh, per = rh // tm, cs // tn
    has_add = add is not None

    def half(c):
        return 1 - c[0] if theirs else c[0]

    if row_sharded:
        grid = (n // tn, N_CHIPS * mh)
        a_spec = pl.BlockSpec((kdim, tm), lambda j, r, c: (0, ((r // mh) * 2 + half(c)) * mh + r % mh))
        o_spec = pl.BlockSpec((None, tm, tn), lambda j, r, c: (r // mh, r % mh, j))
    else:
        grid = (n // tn, mh)
        a_spec = pl.BlockSpec((kdim, tm), lambda j, r, c: (0, half(c) * mh + r))
        o_spec = pl.BlockSpec((None, tm, tn), lambda j, r, c: (j // per, r, j % per))
    b_spec = pl.BlockSpec((kdim, tn), lambda j, r, c: (0, j))

    def body(c_ref, a_ref, b_ref, *rest):
        o_ref = rest[-1]
        acc = lax.dot_general(a_ref[...].astype(BF16), b_ref[...].astype(BF16), (((0,), (0,)), ((), ())),
                              preferred_element_type=F32)
        if has_add:
            acc = acc + rest[0][...].astype(F32)
        o_ref[...] = acc.astype(BF16)

    operands = [a, b] + ([add] if has_add else []) + ([] if after is None else [after])
    return pl.pallas_call(
        body, name=name,
        grid_spec=pltpu.PrefetchScalarGridSpec(
            num_scalar_prefetch=1, grid=grid,
            in_specs=[a_spec, b_spec] + ([o_spec] if has_add else []) + ([] if after is None else [ANY]),
            out_specs=o_spec),
        out_shape=jax.ShapeDtypeStruct((N_CHIPS, rh, cs), BF16),
        compiler_params=_params(2),
    )(core, *operands)


def _rstd(x):
    return lax.rsqrt(jnp.mean(x * x, axis=-1, keepdims=True) + RMS_EPS)


def _rmsnorm(x, g, name):
    s, d = x.shape
    tr = _row_tile(s, 256)

    def body(x_ref, g_ref, o_ref):
        xv = x_ref[...]
        o_ref[...] = (xv * _rstd(xv) * g_ref[...]).astype(BF16)

    return pl.pallas_call(
        body, name=name, grid=(s // tr,),
        in_specs=[pl.BlockSpec((tr, d), lambda i: (i, 0)), pl.BlockSpec((1, d), lambda i: (0, 0))],
        out_specs=pl.BlockSpec((tr, d), lambda i: (i, 0)),
        out_shape=jax.ShapeDtypeStruct((s, d), BF16),
        compiler_params=_params(1),
    )(x, g)


def _rmsnorm_bwd(dh, x, g, dres, name):
    s, d = x.shape
    tr = _row_tile(s, 256)
    has_res = dres is not None

    def body(*refs):
        if has_res:
            dh_ref, x_ref, g_ref, res_ref, dx_ref, dxb_ref, dg_ref = refs
        else:
            dh_ref, x_ref, g_ref, dx_ref, dxb_ref, dg_ref = refs
        xv = x_ref[...]
        dhv = dh_ref[...].astype(F32)
        r = _rstd(xv)
        xn = xv * r
        dhg = dhv * g_ref[...]
        dx = r * (dhg - xn * jnp.mean(dhg * xn, axis=-1, keepdims=True))
        if has_res:
            dx = dx + res_ref[...]
        dx_ref[...] = dx
        dxb_ref[...] = dx.astype(BF16)
        part = jnp.sum(dhv * xn, axis=0, keepdims=True)

        @pl.when(pl.program_id(0) == 0)
        def _():
            dg_ref[...] = part

        @pl.when(pl.program_id(0) > 0)
        def _():
            dg_ref[...] += part

    row = pl.BlockSpec((tr, d), lambda i: (i, 0))
    vec = pl.BlockSpec((1, d), lambda i: (0, 0))
    return pl.pallas_call(
        body, name=name, grid=(s // tr,),
        in_specs=[row, row, vec] + ([row] if has_res else []),
        out_specs=[row, row, vec],
        out_shape=[jax.ShapeDtypeStruct((s, d), F32), jax.ShapeDtypeStruct((s, d), BF16), jax.ShapeDtypeStruct((1, d), F32)],
        compiler_params=_params(1),
    )(*([dh, x, g] + ([dres] if has_res else [])))


def _loss_head(x3, g, target):
    s, d = x3.shape
    tr = _row_tile(s, 256)

    def body(x_ref, g_ref, t_ref, dx_ref, dxb_ref, sq_ref, dg_ref):
        xv = x_ref[...]
        gv = g_ref[...]
        r = _rstd(xv)
        xn = xv * r
        err = xn * gv - t_ref[...]
        dy = err * (1.0 / d)
        dyg = dy * gv
        dx = r * (dyg - xn * jnp.mean(dyg * xn, axis=-1, keepdims=True))
        dx_ref[...] = dx
        dxb_ref[...] = dx.astype(BF16)
        sq = jnp.sum(jnp.sum(err * err, axis=1, keepdims=True), axis=0, keepdims=True)
        sq = jnp.broadcast_to(sq, (1, 128))
        part = jnp.sum(dy * xn, axis=0, keepdims=True)

        @pl.when(pl.program_id(0) == 0)
        def _():
            sq_ref[...] = sq
            dg_ref[...] = part

        @pl.when(pl.program_id(0) > 0)
        def _():
            sq_ref[...] += sq
            dg_ref[...] += part

    row = pl.BlockSpec((tr, d), lambda i: (i, 0))
    vec = pl.BlockSpec((1, d), lambda i: (0, 0))
    return pl.pallas_call(
        body, name="loss_head", grid=(s // tr,),
        in_specs=[row, vec, row],
        out_specs=[row, row, pl.BlockSpec((1, 128), lambda i: (0, 0)), vec],
        out_shape=[jax.ShapeDtypeStruct((s, d), F32), jax.ShapeDtypeStruct((s, d), BF16),
                   jax.ShapeDtypeStruct((1, 128), F32), jax.ShapeDtypeStruct((1, d), F32)],
        compiler_params=_params(1),
    )(x3, g, target)


def _rope_tables(s):
    inv = 1.0 / (ROPE_THETA ** (jnp.arange(0, HEAD_DIM, 2, dtype=F32) / HEAD_DIM))
    ang = jnp.arange(s, dtype=F32)[:, None] * inv[None, :]
    cos, sin = jnp.cos(ang), jnp.sin(ang)
    return jnp.concatenate([cos, cos], axis=1), jnp.concatenate([-sin, sin], axis=1)


def _swap_halves(t):
    return pltpu.roll(t, HEAD_DIM // 2, 1)


def _rope_fwd(z, cos_t, sin_t, after=None):
    s = z.shape[0]
    tr = _row_tile(s, 256)

    def body(zq_ref, zk_ref, zv_ref, c_ref, s_ref, *rest):
        q_ref, k_ref, v_ref = rest[-3:]
        c, sn = c_ref[...], s_ref[...]
        for hd in range(N_Q_HEADS):
            cols = slice(hd * HEAD_DIM, (hd + 1) * HEAD_DIM)
            t = zq_ref[:, cols]
            q_ref[:, cols] = (t * c + _swap_halves(t) * sn).astype(BF16)
        for hd in range(N_KV_HEADS):
            cols = slice(hd * HEAD_DIM, (hd + 1) * HEAD_DIM)
            t = zk_ref[:, cols]
            k_ref[:, cols] = (t * c + _swap_halves(t) * sn).astype(BF16)
        v_ref[...] = zv_ref[...].astype(BF16)

    tab = pl.BlockSpec((tr, HEAD_DIM), lambda i: (i, 0))
    return pl.pallas_call(
        body, name="rope_fwd", grid=(s // tr,),
        in_specs=[pl.BlockSpec((tr, ATTN_WIDTH), lambda i: (i, Q_OFF // ATTN_WIDTH)),
                  pl.BlockSpec((tr, KV_WIDTH), lambda i: (i, K_OFF // KV_WIDTH)),
                  pl.BlockSpec((tr, KV_WIDTH), lambda i: (i, V_OFF // KV_WIDTH)), tab, tab] + ([] if after is None else [ANY]),
        out_specs=[pl.BlockSpec((tr, ATTN_WIDTH), lambda i: (i, 0)), pl.BlockSpec((tr, KV_WIDTH), lambda i: (i, 0)),
                   pl.BlockSpec((tr, KV_WIDTH), lambda i: (i, 0))],
        out_shape=[jax.ShapeDtypeStruct((s, ATTN_WIDTH), BF16), jax.ShapeDtypeStruct((s, KV_WIDTH), BF16),
                   jax.ShapeDtypeStruct((s, KV_WIDTH), BF16)],
        compiler_params=_params(1),
    )(z, z, z, cos_t, sin_t, *([] if after is None else [after]))


def _rope_bwd(dq_rot, dk_rot, dv, cos_t, sin_t, dz):
    s = dq_rot.shape[0]
    tr = _row_tile(s, 256)
    qkv_width = V_OFF + KV_WIDTH

    def body(dq_ref, dk_ref, dv_ref, c_ref, s_ref, dz_in_ref, o_ref):
        c, sn = c_ref[...], s_ref[...]
        for hd in range(N_Q_HEADS):
            t = dq_ref[:, hd * HEAD_DIM:(hd + 1) * HEAD_DIM]
            o_ref[:, Q_OFF + hd * HEAD_DIM:Q_OFF + (hd + 1) * HEAD_DIM] = (t * c + _swap_halves(t * sn)).astype(BF16)
        for hd in range(N_KV_HEADS):
            t = dk_ref[:, hd * HEAD_DIM:(hd + 1) * HEAD_DIM]
            o_ref[:, K_OFF + hd * HEAD_DIM:K_OFF + (hd + 1) * HEAD_DIM] = (t * c + _swap_halves(t * sn)).astype(BF16)
        o_ref[:, V_OFF:V_OFF + KV_WIDTH] = dv_ref[...].astype(BF16)

    tab = pl.BlockSpec((tr, HEAD_DIM), lambda i: (i, 0))
    wide = pl.BlockSpec((tr, ATTN_WIDTH), lambda i: (i, 0))
    narrow = pl.BlockSpec((tr, KV_WIDTH), lambda i: (i, 0))
    return pl.pallas_call(
        body, name="rope_bwd", grid=(s // tr,),
        in_specs=[wide, narrow, narrow, tab, tab, ANY],
        out_specs=pl.BlockSpec((tr, qkv_width), lambda i: (i, 0)),
        out_shape=jax.ShapeDtypeStruct(dz.shape, dz.dtype),
        input_output_aliases={5: 0},
        compiler_params=_params(1),
    )(dq_rot, dk_rot, dv, cos_t, sin_t, dz)


def _swa_band(i, s):
    return pl.multiple_of(jnp.clip((i - 1) * BLOCK, 0, s - BAND), BLOCK)


SWA_HEADS_PER_PASS = Q_GROUP


def _swa_probs(q_ref, k_ref, sink_ref, heads, start, valid):
    kv = heads[0] // Q_GROUP
    cols = slice(kv * HEAD_DIM, (kv + 1) * HEAD_DIM)
    kb = k_ref[pl.ds(start, BAND), cols]
    qg = jnp.concatenate([q_ref[:, hd * HEAD_DIM:(hd + 1) * HEAD_DIM] for hd in heads], axis=0)
    sc = lax.dot_general(qg, kb, (((1,), (1,)), ((), ())), preferred_element_type=F32) * ATTN_SCALE
    sc = jnp.where(valid, sc, NEG_INF)
    sk = jnp.concatenate([jnp.full((BLOCK, 1), sink_ref[hd], F32) for hd in heads], axis=0)
    mx = jnp.maximum(jnp.max(sc, axis=1, keepdims=True), sk)
    e = jnp.exp(sc - mx)
    es = jnp.exp(sk - mx)
    inv = 1.0 / (jnp.sum(e, axis=1, keepdims=True) + es)
    return qg, kb, e * inv, es * inv


def _swa_head_passes():
    return [list(range(h0, h0 + SWA_HEADS_PER_PASS)) for h0 in range(0, N_Q_HEADS, SWA_HEADS_PER_PASS)]


def _swa_valid(i, start):
    q_pos = i * BLOCK + lax.broadcasted_iota(jnp.int32, (BLOCK, 1), 0)
    q_pos = jnp.concatenate([q_pos] * SWA_HEADS_PER_PASS, axis=0)
    k_pos = start + lax.broadcasted_iota(jnp.int32, (1, BAND), 1)
    return jnp.abs(k_pos - q_pos) <= WINDOW


def _swa_fwd(q, k, v, sink):
    s = q.shape[0]
    assert s % BLOCK == 0 and s >= BAND

    def body(sink_ref, q_ref, k_ref, v_ref, o_ref):
        i = pl.program_id(0)
        start = _swa_band(i, s)
        valid = _swa_valid(i, start)
        for heads in _swa_head_passes():
            kv = heads[0] // Q_GROUP
            _, _, p, _ = _swa_probs(q_ref, k_ref, sink_ref, heads, start, valid)
            vb = v_ref[pl.ds(start, BAND), kv * HEAD_DIM:(kv + 1) * HEAD_DIM]
            o = jnp.dot(p.astype(BF16), vb, preferred_element_type=F32)
            for g, hd in enumerate(heads):
                o_ref[:, hd * HEAD_DIM:(hd + 1) * HEAD_DIM] = o[g * BLOCK:(g + 1) * BLOCK].astype(BF16)

    whole = pl.BlockSpec((s, KV_WIDTH), lambda i: (0, 0))
    blk = pl.BlockSpec((BLOCK, ATTN_WIDTH), lambda i: (i, 0))
    return pl.pallas_call(
        body, name="swa_fwd", grid=(s // BLOCK,),
        in_specs=[pl.BlockSpec(memory_space=pltpu.SMEM), blk, whole, whole],
        out_specs=blk,
        out_shape=jax.ShapeDtypeStruct((s, ATTN_WIDTH), BF16),
        compiler_params=_params(1),
    )(sink, q, k, v)


def _swa_bwd(q, k, v, d_out, sink):
    s = q.shape[0]

    def body(sink_ref, q_ref, k_ref, v_ref, do_ref, dq_ref, dk_ref, dv_ref, dsink_ref):
        i = pl.program_id(0)

        @pl.when(i == 0)
        def _():
            dk_ref[...] = jnp.zeros_like(dk_ref)
            dv_ref[...] = jnp.zeros_like(dv_ref)
            dsink_ref[...] = jnp.zeros_like(dsink_ref)

        start = _swa_band(i, s)
        valid = _swa_valid(i, start)
        for heads in _swa_head_passes():
            kv = heads[0] // Q_GROUP
            cols = slice(kv * HEAD_DIM, (kv + 1) * HEAD_DIM)
            qg, kb, p, p_sink = _swa_probs(q_ref, k_ref, sink_ref, heads, start, valid)
            vb = v_ref[pl.ds(start, BAND), cols]
            dog = jnp.concatenate([do_ref[:, hd * HEAD_DIM:(hd + 1) * HEAD_DIM] for hd in heads], axis=0)
            dp = lax.dot_general(dog, vb, (((1,), (1,)), ((), ())), preferred_element_type=F32)
            delta = jnp.sum(p * dp, axis=1, keepdims=True)
            ds = (p * (dp - delta) * ATTN_SCALE).astype(BF16)
            dqg = jnp.dot(ds, kb, preferred_element_type=F32)
            dk_ref[pl.ds(start, BAND), cols] += lax.dot_general(ds, qg, (((0,), (0,)), ((), ())), preferred_element_type=F32)
            dv_ref[pl.ds(start, BAND), cols] += lax.dot_general(p.astype(BF16), dog, (((0,), (0,)), ((), ())),
                                                                 preferred_element_type=F32)
            dsk = p_sink * delta
            for g, hd in enumerate(heads):
                dq_ref[:, hd * HEAD_DIM:(hd + 1) * HEAD_DIM] = dqg[g * BLOCK:(g + 1) * BLOCK]
                tot = jnp.sum(dsk[g * BLOCK:(g + 1) * BLOCK], axis=0, keepdims=True)
                dsink_ref[hd:hd + 1, :] -= jnp.broadcast_to(tot, (1, 128))

    whole = pl.BlockSpec((s, KV_WIDTH), lambda i: (0, 0))
    blk = pl.BlockSpec((BLOCK, ATTN_WIDTH), lambda i: (i, 0))
    return pl.pallas_call(
        body, name="swa_bwd", grid=(s // BLOCK,),
        in_specs=[pl.BlockSpec(memory_space=pltpu.SMEM), blk, whole, whole, blk],
        out_specs=[blk, whole, whole, pl.BlockSpec((N_Q_HEADS, 128), lambda i: (0, 0))],
        out_shape=[jax.ShapeDtypeStruct((s, ATTN_WIDTH), F32), jax.ShapeDtypeStruct((s, KV_WIDTH), F32),
                   jax.ShapeDtypeStruct((s, KV_WIDTH), F32), jax.ShapeDtypeStruct((N_Q_HEADS, 128), F32)],
        compiler_params=_params(1),
    )(sink, q, k, v, d_out)


CONV_CHUNK = 256


def _shift_rows(t, rows, down):
    n = t.shape[0]
    rolled = pltpu.roll(t, 1 if down else n - 1, 0)
    edge = 0 if down else n - 1
    return jnp.where(rows == edge, 0.0, rolled)


def _conv_specs(s):
    def z_spec(off):
        return pl.BlockSpec((s, CONV_CHUNK), lambda j, off=off: (0, off // CONV_CHUNK + j))
    chunk = pl.BlockSpec((s, CONV_CHUNK), lambda j: (0, j))
    w_spec = pl.BlockSpec((3, CONV_CHUNK), lambda j: (0, j))
    return z_spec(CU_OFF), z_spec(CB_OFF), z_spec(CC_OFF), chunk, w_spec


def _conv_fwd(z, conv_w, after=None):
    s = z.shape[0]
    cu_spec, cb_spec, cc_spec, chunk, w_spec = _conv_specs(s)

    def body(cu_ref, cb_ref, cc_ref, w_ref, *rest):
        o_ref = rest[-1]
        rows = lax.broadcasted_iota(jnp.int32, (s, 1), 0)
        t = cc_ref[...] * cu_ref[...]
        c3 = _shift_rows(t, rows, True) * w_ref[0:1, :] + t * w_ref[1:2, :] + _shift_rows(t, rows, False) * w_ref[2:3, :]
        o_ref[...] = (cb_ref[...] * c3).astype(BF16)

    return pl.pallas_call(
        body, name="conv_fwd", grid=(CONV_WIDTH // CONV_CHUNK,),
        in_specs=[cu_spec, cb_spec, cc_spec, w_spec] + ([] if after is None else [ANY]),
        out_specs=chunk,
        out_shape=jax.ShapeDtypeStruct((s, CONV_WIDTH), BF16),
        compiler_params=_params(1),
    )(z, z, z, conv_w, *([] if after is None else [after]))


def _conv_bwd(z, conv_w, d_co, dz):
    s = z.shape[0]
    cu_spec, cb_spec, cc_spec, chunk, w_spec = _conv_specs(s)
    n_chunks = CONV_WIDTH // CONV_CHUNK
    offsets = (CU_OFF, CB_OFF, CC_OFF)

    def body(cu_ref, cb_ref, cc_ref, w_ref, d_ref, dz_in_ref, dz_ref, dw_ref, buf, sems):
        j = pl.program_id(0)

        def copies(j_at):
            return [pltpu.make_async_copy(buf.at[h], dz_ref.at[:, pl.ds(off + j_at * CONV_CHUNK, CONV_CHUNK)], sems.at[h])
                    for h, off in enumerate(offsets)]

        rows = lax.broadcasted_iota(jnp.int32, (s, 1), 0)
        cu, cc = cu_ref[...], cc_ref[...]
        t = cc * cu
        t_dn, t_up = _shift_rows(t, rows, True), _shift_rows(t, rows, False)
        c3 = t_dn * w_ref[0:1, :] + t * w_ref[1:2, :] + t_up * w_ref[2:3, :]
        d = d_ref[...]
        dc3 = d * cb_ref[...]
        dw_ref[0:1, :] = jnp.sum(dc3 * t_dn, axis=0, keepdims=True)
        dw_ref[1:2, :] = jnp.sum(dc3 * t, axis=0, keepdims=True)
        dw_ref[2:3, :] = jnp.sum(dc3 * t_up, axis=0, keepdims=True)
        dt = _shift_rows(dc3, rows, False) * w_ref[0:1, :] + dc3 * w_ref[1:2, :] + _shift_rows(dc3, rows, True) * w_ref[2:3, :]

        @pl.when(j > 0)
        def _():
            for cp in copies(j):
                cp.wait()

        buf[0] = (dt * cc).astype(BF16)
        buf[1] = (d * c3).astype(BF16)
        buf[2] = (dt * cu).astype(BF16)
        for cp in copies(j):
            cp.start()

        @pl.when(j == n_chunks - 1)
        def _():
            for cp in copies(j):
                cp.wait()

    return pl.pallas_call(
        body, name="conv_bwd", grid=(n_chunks,),
        in_specs=[cu_spec, cb_spec, cc_spec, w_spec, chunk, ANY],
        out_specs=[ANY, w_spec],
        out_shape=[jax.ShapeDtypeStruct(dz.shape, dz.dtype), jax.ShapeDtypeStruct((3, CONV_WIDTH), F32)],
        input_output_aliases={5: 0},
        scratch_shapes=[pltpu.VMEM((3, s, CONV_CHUNK), BF16), pltpu.SemaphoreType.DMA((3,))],
        compiler_params=_params(1),
    )(z, z, z, conv_w, d_co, dz)


GATE_CHUNK = 512


def _gate_specs(s, d, tr):
    n_chunks = d // GATE_CHUNK
    za = pl.BlockSpec((tr, GATE_CHUNK), lambda j, i: (i, GL_OFF // GATE_CHUNK + j))
    zc = pl.BlockSpec((tr, GATE_CHUNK), lambda j, i: (i, GL_OFF // GATE_CHUNK + n_chunks + j))
    ba = pl.BlockSpec((1, GATE_CHUNK), lambda j, i: (0, j))
    bc = pl.BlockSpec((1, GATE_CHUNK), lambda j, i: (0, n_chunks + j))
    tile = pl.BlockSpec((tr, GATE_CHUNK), lambda j, i: (i, j))
    return za, zc, ba, bc, tile


def _gate_fwd(z, b_gate, ya, yc):
    s, d = ya.shape
    tr = _row_tile(s, 512)
    za, zc, ba, bc, tile = _gate_specs(s, d, tr)

    def body(za_ref, zc_ref, ba_ref, bc_ref, ya_ref, yc_ref, o_ref):
        ga = jax.nn.sigmoid(za_ref[...] + ba_ref[...])
        gc = jax.nn.sigmoid(zc_ref[...] + bc_ref[...])
        o_ref[...] = (ga * ya_ref[...] + gc * yc_ref[...]).astype(BF16)

    return pl.pallas_call(
        body, name="gate_fwd", grid=(d // GATE_CHUNK, s // tr),
        in_specs=[za, zc, ba, bc, tile, tile],
        out_specs=tile,
        out_shape=jax.ShapeDtypeStruct((s, d), BF16),
        compiler_params=_params(2),
    )(z, z, b_gate, b_gate, ya, yc)


def _gate_bwd(z, b_gate, ya, yc, dmix):
    s, d = ya.shape
    tr = _row_tile(s, 512)
    za, zc, ba, bc, tile = _gate_specs(s, d, tr)
    vec = pl.BlockSpec((1, GATE_CHUNK), lambda j, i: (0, j))
    n_rows = s // tr
    in_width = z.shape[1]

    def body(za_ref, zc_ref, ba_ref, bc_ref, ya_ref, yc_ref, dm_ref, dya_ref, dyc_ref, dz_ref, dba_ref, dbc_ref, buf, sems):
        j, i = pl.program_id(0), pl.program_id(1)

        def copies(j_at, i_at):
            rows = pl.ds(i_at * tr, tr)
            return [pltpu.make_async_copy(buf.at[h], dz_ref.at[rows, pl.ds(GL_OFF + h * d + j_at * GATE_CHUNK, GATE_CHUNK)],
                                          sems.at[h]) for h in range(2)]

        ga = jax.nn.sigmoid(za_ref[...] + ba_ref[...])
        gc = jax.nn.sigmoid(zc_ref[...] + bc_ref[...])
        dm = dm_ref[...]
        dya_ref[...] = (dm * ga).astype(BF16)
        dyc_ref[...] = (dm * gc).astype(BF16)
        dla = dm * ya_ref[...] * ga * (1.0 - ga)
        dlc = dm * yc_ref[...] * gc * (1.0 - gc)

        @pl.when(j * n_rows + i > 0)
        def _():
            for cp in copies(j, i):
                cp.wait()

        buf[0] = dla.astype(BF16)
        buf[1] = dlc.astype(BF16)
        for cp in copies(j, i):
            cp.start()

        @pl.when((j == d // GATE_CHUNK - 1) & (i == n_rows - 1))
        def _():
            for cp in copies(j, i):
                cp.wait()

        pa = jnp.sum(dla, axis=0, keepdims=True)
        pc = jnp.sum(dlc, axis=0, keepdims=True)

        @pl.when(i == 0)
        def _():
            dba_ref[...] = pa
            dbc_ref[...] = pc

        @pl.when(i > 0)
        def _():
            dba_ref[...] += pa
            dbc_ref[...] += pc

    big = jax.ShapeDtypeStruct((s, d), BF16)
    small = jax.ShapeDtypeStruct((1, d), F32)
    return pl.pallas_call(
        body, name="gate_bwd", grid=(d // GATE_CHUNK, n_rows),
        in_specs=[za, zc, ba, bc, tile, tile, tile],
        out_specs=[tile, tile, ANY, vec, vec],
        out_shape=[big, big, jax.ShapeDtypeStruct((s, in_width), BF16), small, small],
        scratch_shapes=[pltpu.VMEM((2, tr, GATE_CHUNK), BF16), pltpu.SemaphoreType.DMA((2,))],
        compiler_params=_params(2),
    )(z, z, b_gate, b_gate, ya, yc, dmix)


def _cross_probs(q_ref, kv_ref, hd):
    cols = slice(hd * HEAD_DIM, (hd + 1) * HEAD_DIM)
    qh = q_ref[:, cols]
    kh = kv_ref[:, cols]
    sc = lax.dot_general(qh, kh, (((1,), (1,)), ((), ())), preferred_element_type=F32) * ATTN_SCALE
    e = jnp.exp(sc - jnp.max(sc, axis=1, keepdims=True))
    return qh, kh, e * (1.0 / jnp.sum(e, axis=1, keepdims=True))


def _cross_fwd(qc, kvc):
    s = qc.shape[0]
    n_mem = kvc.shape[0]
    tq = _row_tile(s, 256)

    def body(q_ref, kv_ref, o_ref):
        for hd in range(MEM_HEADS):
            _, _, p = _cross_probs(q_ref, kv_ref, hd)
            vh = kv_ref[:, MEM_WIDTH + hd * HEAD_DIM:MEM_WIDTH + (hd + 1) * HEAD_DIM]
            o_ref[:, hd * HEAD_DIM:(hd + 1) * HEAD_DIM] = jnp.dot(p.astype(BF16), vh, preferred_element_type=F32).astype(BF16)

    return pl.pallas_call(
        body, name="cross_fwd", grid=(s // tq,),
        in_specs=[pl.BlockSpec((tq, MEM_WIDTH), lambda i: (i, 0)), pl.BlockSpec((n_mem, 2 * MEM_WIDTH), lambda i: (0, 0))],
        out_specs=pl.BlockSpec((tq, MEM_WIDTH), lambda i: (i, 0)),
        out_shape=jax.ShapeDtypeStruct((s, MEM_WIDTH), BF16),
        compiler_params=_params(1),
    )(qc, kvc)


def _cross_bwd(qc, kvc, d_out):
    s = qc.shape[0]
    n_mem = kvc.shape[0]
    tq = _row_tile(s, 256)

    def body(q_ref, kv_ref, do_ref, dq_ref, dkv_ref):
        @pl.when(pl.program_id(0) == 0)
        def _():
            dkv_ref[...] = jnp.zeros_like(dkv_ref)

        for hd in range(MEM_HEADS):
            cols = slice(hd * HEAD_DIM, (hd + 1) * HEAD_DIM)
            vcols = slice(MEM_WIDTH + hd * HEAD_DIM, MEM_WIDTH + (hd + 1) * HEAD_DIM)
            qh, kh, p = _cross_probs(q_ref, kv_ref, hd)
            doh = do_ref[:, cols]
            dp = lax.dot_general(doh, kv_ref[:, vcols], (((1,), (1,)), ((), ())), preferred_element_type=F32)
            ds = (p * (dp - jnp.sum(p * dp, axis=1, keepdims=True)) * ATTN_SCALE).astype(BF16)
            dq_ref[:, cols] = jnp.dot(ds, kh, preferred_element_type=F32).astype(BF16)
            dkv_ref[:, cols] += lax.dot_general(ds, qh, (((0,), (0,)), ((), ())), preferred_element_type=F32)
            dkv_ref[:, vcols] += lax.dot_general(p.astype(BF16), doh, (((0,), (0,)), ((), ())), preferred_element_type=F32)

    qspec = pl.BlockSpec((tq, MEM_WIDTH), lambda i: (i, 0))
    kvspec = pl.BlockSpec((n_mem, 2 * MEM_WIDTH), lambda i: (0, 0))
    return pl.pallas_call(
        body, name="cross_bwd", grid=(s // tq,),
        in_specs=[qspec, kvspec, qspec],
        out_specs=[qspec, kvspec],
        out_shape=[jax.ShapeDtypeStruct((s, MEM_WIDTH), BF16), jax.ShapeDtypeStruct((n_mem, 2 * MEM_WIDTH), F32)],
        compiler_params=_params(1),
    )(qc, kvc, d_out)


def _swiglu_fwd(up, gate):
    return up, (gate * jax.nn.sigmoid(gate)) * up


def _swiglu_bwd(d_act, gate, up):
    sg = jax.nn.sigmoid(gate)
    silu = gate * sg
    return d_act * up * (sg * (1.0 + gate * (1.0 - sg))), d_act * silu


GATHER_GROUPS = {"in": ("w_in", "conv_w"), "mid": ("w_attn_out", "w_conv_out", "w_o", "w_cq", "w_ckv", "w_co"),
                 "gate": ("w_gate",), "up": ("w_up",), "down": ("w_down",)}


def _local_step(xs, mems, target, small, fetch, reduce):
    s, d = xs.shape
    w4 = {}
    cos_t, sin_t = _rope_tables(s)

    def near(group, done, then, after):
        waits = [("direct", group)] + ([("pass_near", done), ("pass_far", done)] if done else [])
        starts = [("forward", group), ("pass_near", group)] + [("direct", g) for g in then]
        tok = fetch.step("gather_near_" + group, waits, starts, after)
        if done:
            w4.update(fetch.arrays(done))
        return tok

    def far(group, then, after):
        return fetch.step("gather_far_" + group, [("forward", group)], [("pass_far", group)] + [("direct", g) for g in then], after)

    def last(group, after):
        tok = fetch.step("gather_done_" + group, [("pass_near", group), ("pass_far", group)], [], after)
        w4.update(fetch.arrays(group))
        return tok

    h = _rmsnorm(xs, small["g_mix"], "norm_mix")
    slots_filled = [a for g in ("gate", "up", "down") for a in fetch.arrays(g).values()]
    chip_x, chip_y = reduce.place[0] // 2, reduce.place[0] % 2
    own_block = jnp.stack([2 * chip_x + chip_y]).astype(jnp.int32)
    near_blocks = jnp.stack([2 * (1 - chip_x) + chip_y, 2 * chip_x + (1 - chip_y)]).astype(jnp.int32)
    far_block = jnp.stack([2 * (1 - chip_x) + (1 - chip_y)]).astype(jnp.int32)
    z = _matmul_column_blocks(h, fetch.arrays("in")["w_in"], own_block, None, tm=512, name="in_proj_own")
    tok = near("in", None, ["mid"], [z] + slots_filled)
    tok = fetch.step("gather_near_done_in", [("pass_near", "in")], [], tok)
    z = _matmul_column_blocks(h, fetch.arrays("in")["w_in"], near_blocks, z, tm=512, name="in_proj_near", after=tok)
    tok = far("in", [], z)
    tok = fetch.step("gather_done_in", [("pass_far", "in")], [], tok)
    w4.update(fetch.arrays("in"))
    z = _matmul_column_blocks(h, w4["w_in"], far_block, z, tm=512, name="in_proj_far", after=tok)
    conv4 = w4["conv_w"]
    conv_w = conv4[:, :3, :].transpose(1, 0, 2).reshape(3, N_CHIPS * conv4.shape[2])
    c_in = w4["w_in"].shape[2]
    tok = near("mid", None, ["gate"], z)
    q_rot, k_rot, v_b = _rope_fwd(z, cos_t, sin_t, after=tok)
    attn = _swa_fwd(q_rot, k_rot, v_b, small["sink"])
    co = _conv_fwd(z, conv_w, after=tok)
    tok = far("mid", ["up"], attn)
    tok = last("mid", tok)
    w_o = w4["w_o"].reshape(-1, w4["w_o"].shape[-1])
    c_d = w4["w_attn_out"].shape[2]
    ya = _matmul(attn, w4["w_attn_out"], mode="nn", tm=1024, tn=c_d, tk=ATTN_WIDTH, out_dtypes=[F32], name="attn_out_proj",
                 b_blocks=N_CHIPS, after=tok)
    yc = _matmul(co, w4["w_conv_out"], mode="nn", tm=1024, tn=c_d, tk=CONV_WIDTH, out_dtypes=[F32], name="conv_out_proj",
                 b_blocks=N_CHIPS)
    mix = _gate_fwd(z, small["b_gate"], ya, yc)
    x1 = _matmul(mix, w_o, mode="nn", tm=512, tn=1024, tk=d, out_dtypes=[F32], name="mix_out_proj", extras=[xs],
                 epilogue=_add_residual)
    tok = near("gate", None, ["down"], x1)
    w_cq = w4["w_cq"].reshape(-1, w4["w_cq"].shape[-1])
    w_ckv = w4["w_ckv"].reshape(-1, w4["w_ckv"].shape[-1])
    hc = _rmsnorm(x1, small["g_cross"], "norm_cross")
    memn = _rmsnorm(mems, small["g_mem"], "norm_mem")
    qc = _matmul(hc, w_cq, mode="nn", tm=1024, tn=MEM_WIDTH, tk=d, out_dtypes=[BF16], name="cross_q_proj", after=tok)
    kvc = _matmul(memn, w_ckv, mode="nn", tm=256, tn=2 * MEM_WIDTH, tk=d, out_dtypes=[BF16], name="cross_kv_proj")
    oc = _cross_fwd(qc, kvc)
    tok = far("gate", [], oc)
    x2 = _matmul(oc, w4["w_co"], mode="nn", tm=1024, tn=c_d, tk=MEM_WIDTH, out_dtypes=[F32], name="cross_out_proj",
                 extras=[x1], epilogue=_add_residual, b_blocks=N_CHIPS, after=tok)
    hf = _rmsnorm(x2, small["g_ffn"], "norm_ffn")
    tok = near("up", "gate", [], hf)
    c_ff = w4["w_gate"].shape[2]
    gate = _matmul(hf, w4["w_gate"], mode="nn", tm=512, tn=c_ff, tk=d, out_dtypes=[F32], name="ffn_gate_proj", b_blocks=N_CHIPS,
                   after=tok)
    tok = far("up", [], gate)
    tok = near("down", "up", [], tok)
    up, act = _matmul(hf, w4["w_up"], mode="nn", tm=512, tn=c_ff, tk=d, out_dtypes=[F32, BF16], name="ffn_up_proj",
                      extras=[gate], epilogue=_swiglu_fwd, b_blocks=N_CHIPS, after=tok)
    tok = far("down", [], act)
    last("down", tok)
    w_down = w4["w_down"].reshape(-1, w4["w_down"].shape[-1])
    x3 = _matmul(act, w_down, mode="nn", tm=512, tn=512, tk=w_down.shape[0], out_dtypes=[F32], name="ffn_down_proj", extras=[x2],
                 epilogue=_add_residual)
    dx3, dx3b, sq, dg_final = _loss_head(x3, small["g_final"], target)

    da, du = _matmul(dx3b, w_down, mode="nt", tm=512, tn=c_ff, tk=d, out_dtypes=[BF16, BF16], name="ffn_down_bwd",
                     extras=[gate, up], epilogue=_swiglu_bwd)
    core = reduce.core
    ffn_shape = dict(row_sharded=False, tm=1024, tn=c_ff)
    g_down = _matmul(act, dx3b, mode="tn", tm=c_ff, tn=1024, tk=s, out_dtypes=[BF16], name="ffn_down_wgrad")
    tok = reduce.add("down", {"w_down": g_down}, da)
    t_gate = _wgrad_half(hf, da, core, theirs=True, name="ffn_gate_wgrad_theirs", after=tok, **ffn_shape)
    tok = reduce.step("down", t_gate)
    t_up = _wgrad_half(hf, du, core, theirs=True, name="ffn_up_wgrad_theirs", after=tok, **ffn_shape)
    tok = reduce.send("ffn", {"w_gate": t_gate, "w_up": t_up}, dx3b)
    dhf = _matmul(da, w4["w_gate"], mode="nt", tm=512, tn=1024, tk=N_CHIPS * c_ff, out_dtypes=[F32], name="ffn_gate_bwd", b_blocks=N_CHIPS,
                  after=tok)
    got = reduce.received("ffn", dhf)
    p_gate = _wgrad_half(hf, da, core, theirs=False, name="ffn_gate_wgrad_mine", add=got["w_gate"], **ffn_shape)
    p_up = _wgrad_half(hf, du, core, theirs=False, name="ffn_up_wgrad_mine", add=got["w_up"], **ffn_shape)
    tok = reduce.add_parts("ffn", {"w_gate": p_gate, "w_up": p_up})
    dhf = _matmul(du, w4["w_up"], mode="nt", tm=512, tn=1024, tk=N_CHIPS * c_ff, out_dtypes=[F32], name="ffn_up_bwd", extras=[dhf],
                  epilogue=_add_residual, b_blocks=N_CHIPS, after=tok)
    tok = reduce.step("down", dhf)
    dx2, dx2b, dg_ffn = _rmsnorm_bwd(dhf, x2, small["g_ffn"], dx3, "norm_ffn_bwd")

    d_oc = _matmul(dx2b, w4["w_co"], mode="nt", tm=1024, tn=MEM_WIDTH, tk=d, out_dtypes=[BF16], name="cross_out_bwd",
                   b_blocks=N_CHIPS, after=tok)
    g_co = _matmul(oc, dx2b, mode="tn", tm=MEM_WIDTH, tn=c_d, tk=s, out_dtypes=[BF16], name="cross_out_wgrad", out_blocks=N_CHIPS)
    tok = reduce.step("down", g_co)
    dqc, dkvc = _cross_bwd(qc, kvc, d_oc)
    g_cq = _matmul(hc, dqc, mode="tn", tm=1024, tn=MEM_WIDTH, tk=s, out_dtypes=[BF16], name="cross_q_wgrad", after=tok)
    dhc = _matmul(dqc, w_cq, mode="nt", tm=1024, tn=1024, tk=MEM_WIDTH, out_dtypes=[F32], name="cross_q_bwd")
    g_ckv = _matmul(memn, dkvc, mode="tn", tm=1024, tn=2 * MEM_WIDTH, tk=mems.shape[0], out_dtypes=[BF16], name="cross_kv_wgrad")
    dmemn = _matmul(dkvc, w_ckv, mode="nt", tm=256, tn=1024, tk=2 * MEM_WIDTH, out_dtypes=[F32], name="cross_kv_bwd")
    _, _, dg_mem = _rmsnorm_bwd(dmemn, mems, small["g_mem"], None, "norm_mem_bwd")
    dx1, dx1b, dg_cross = _rmsnorm_bwd(dhc, x1, small["g_cross"], dx2, "norm_cross_bwd")

    dmix = _matmul(dx1b, w_o, mode="nt", tm=512, tn=1024, tk=d, out_dtypes=[F32], name="mix_out_bwd")
    g_o = _matmul(mix, dx1b, mode="tn", tm=1024, tn=1024, tk=s, out_dtypes=[BF16], name="mix_out_wgrad")
    dya, dyc, dz, db_a, db_c = _gate_bwd(z, small["b_gate"], ya, yc, dmix)
    d_attn = _matmul(dya, w4["w_attn_out"], mode="nt", tm=1024, tn=ATTN_WIDTH, tk=d, out_dtypes=[BF16], name="attn_out_bwd",
                     b_blocks=N_CHIPS)
    g_ao = _matmul(attn, dya, mode="tn", tm=ATTN_WIDTH, tn=c_d, tk=s, out_dtypes=[BF16], name="attn_out_wgrad", out_blocks=N_CHIPS)
    d_co = _matmul(dyc, w4["w_conv_out"], mode="nt", tm=1024, tn=CONV_WIDTH, tk=d, out_dtypes=[F32], name="conv_out_bwd",
                   b_blocks=N_CHIPS)
    g_cvo = _matmul(co, dyc, mode="tn", tm=CONV_WIDTH, tn=c_d, tk=s, out_dtypes=[BF16], name="conv_out_wgrad", out_blocks=N_CHIPS)
    tok = reduce.step("ffn", g_cvo)
    tok = reduce.add("mid", {"w_co": g_co, "w_cq": g_cq, "w_ckv": g_ckv, "w_o": g_o, "w_attn_out": g_ao, "w_conv_out": g_cvo}, tok)
    dz, d_conv_w = _conv_bwd(z, conv_w, d_co, dz)
    dq_rot, dk_rot, dv, dsink = _swa_bwd(q_rot, k_rot, v_b, d_attn, small["sink"])
    tok = reduce.step("mid", dq_rot)
    dz = _rope_bwd(dq_rot, dk_rot, dv, cos_t, sin_t, dz)
    in_shape = dict(row_sharded=False, tm=1024, tn=c_in)
    t_in = _wgrad_half(h, dz, core, theirs=True, name="in_proj_wgrad_theirs", after=tok, **in_shape)
    tok = reduce.send("in", {"w_in": t_in}, dk_rot)
    tok = reduce.step("ffn", tok)
    tok = reduce.step("mid", tok)
    got = reduce.received("in", tok)
    p_in = _wgrad_half(h, dz, core, theirs=False, name="in_proj_wgrad_mine", add=got["w_in"], **in_shape)
    tok = reduce.add_parts("in", {"w_in": p_in})
    dh = _matmul(dz, w4["w_in"], mode="nt", tm=512, tn=512, tk=N_CHIPS * c_in, out_dtypes=[F32], name="in_proj_bwd", b_blocks=N_CHIPS,
                 after=tok)
    tok = reduce.step("mid", dh)
    grad_x, _, dg_mix = _rmsnorm_bwd(dh, xs, small["g_mix"], dx1, "norm_mix_bwd")

    small_grads = {
        "g_mix": dg_mix, "sink": dsink[:, 0], "b_gate": jnp.concatenate([db_a, db_c], axis=1), "g_cross": dg_cross,
        "g_mem": dg_mem, "g_ffn": dg_ffn, "g_final": dg_final, "conv_w": d_conv_w,
    }
    return sq, grad_x, small_grads


def _pair_sum(g4, ra, core, name):
    nb, rs, cs = g4.shape
    rh = rs // 2
    tr = _row_tile(rh, 256)
    per = rh // tr

    def body(c_ref, g_ref, r_ref, o_ref):
        o_ref[...] = (g_ref[...].astype(F32) + r_ref[...].astype(F32)).astype(BF16)

    plain = pl.BlockSpec((None, tr, cs), lambda j, i, c: (j, i, 0))
    return pl.pallas_call(
        body, name=name,
        grid_spec=pltpu.PrefetchScalarGridSpec(
            num_scalar_prefetch=1, grid=(nb, per),
            in_specs=[pl.BlockSpec((None, tr, cs), lambda j, i, c: (j, c[0] * per + i, 0)), plain],
            out_specs=plain),
        out_shape=jax.ShapeDtypeStruct((nb, rh, cs), BF16),
        compiler_params=_params(2),
    )(core, g4, ra)


def _quad_sum(parts, rc, place, name):
    _, rh, cs = parts.shape
    tr = _row_tile(rh, 256)
    per = rh // tr

    def body(p_ref, own_ref, r_ref, o_ref):
        acc = own_ref[...].astype(F32)
        for j in range(rc.shape[0]):
            acc = acc + r_ref[j].astype(F32)
        o_ref[...] = acc

    return pl.pallas_call(
        body, name=name,
        grid_spec=pltpu.PrefetchScalarGridSpec(
            num_scalar_prefetch=1, grid=(per,),
            in_specs=[pl.BlockSpec((None, tr, cs), lambda i, p: (p[0], i, 0)),
                      pl.BlockSpec((rc.shape[0], tr, cs), lambda i, p: (0, i, 0))],
            out_specs=pl.BlockSpec((tr, cs), lambda i, p: (p[1] * per + i, 0))),
        out_shape=jax.ShapeDtypeStruct((2 * rh, cs), F32),
        compiler_params=_params(1),
    )(place, parts, rc)


def _adamw_update(w, g, m, v):
    nm = ADAM_B1 * m + (1.0 - ADAM_B1) * g
    nv = ADAM_B2 * v + (1.0 - ADAM_B2) * (g * g)
    m_hat = nm / ADAM_C1
    v_hat = nv / ADAM_C2
    return -ADAM_LR * (m_hat / (jnp.sqrt(v_hat) + ADAM_EPS) + ADAM_WD * w), nm, nv


def _adamw_own_half(w, m, v, parts, rc, place, name, after=None):
    rows, cols = w.shape
    rh = rows // 2
    tr = _row_tile(rh, 256)
    per = rh // tr

    def body(p_ref, w_ref, m_ref, v_ref, own_ref, r_ref, *rest):
        gx_ref, g_ref, d_ref, nm_ref, nv_ref = rest[-5:]
        g = own_ref[...].astype(F32)
        for j in range(rc.shape[0]):
            g = g + r_ref[j].astype(F32)
        gx_ref[...] = g
        g_ref[...] = g
        d_ref[...], nm_ref[...], nv_ref[...] = _adamw_update(w_ref[...], g, m_ref[...], v_ref[...])

    mine = pl.BlockSpec((tr, cols), lambda i, p: (p[1] * per + i, 0))
    shape = jax.ShapeDtypeStruct((rows, cols), F32)
    return pl.pallas_call(
        body, name=name,
        grid_spec=pltpu.PrefetchScalarGridSpec(
            num_scalar_prefetch=1, grid=(per,),
            in_specs=[mine, mine, mine, pl.BlockSpec((None, tr, cols), lambda i, p: (p[0], i, 0)),
                      pl.BlockSpec((rc.shape[0], tr, cols), lambda i, p: (0, i, 0))] + ([] if after is None else [ANY]),
            out_specs=[mine] * 5),
        out_shape=[shape] * 5,
        compiler_params=_params(1),
    )(place, w, m, v, parts, rc, *([] if after is None else [after]))


def _adamw_other_half(w, m, v, g_exchanged, g, delta, new_m, new_v, place, name, after=None):
    rows, cols = w.shape
    rh = rows // 2
    tr = _row_tile(rh, 256)
    per = rh // tr

    def body(p_ref, w_ref, m_ref, v_ref, gx_ref, *rest):
        g_ref, d_ref, nm_ref, nv_ref = rest[-4:]
        gv = gx_ref[...]
        g_ref[...] = gv
        d_ref[...], nm_ref[...], nv_ref[...] = _adamw_update(w_ref[...], gv, m_ref[...], v_ref[...])

    other = pl.BlockSpec((tr, cols), lambda i, p: ((1 - p[1]) * per + i, 0))
    shape = jax.ShapeDtypeStruct((rows, cols), F32)
    n_after = 0 if after is None else 1
    return pl.pallas_call(
        body, name=name,
        grid_spec=pltpu.PrefetchScalarGridSpec(
            num_scalar_prefetch=1, grid=(per,),
            in_specs=[other] * 4 + [ANY] * (4 + n_after),
            out_specs=[other] * 4),
        out_shape=[shape] * 4,
        input_output_aliases={5: 0, 6: 1, 7: 2, 8: 3},
        compiler_params=_params(1),
    )(place, w, m, v, g_exchanged, g, delta, new_m, new_v, *([] if after is None else [after]))


def _cast_to_slot(w, place, dtype, name, after=None):
    rows, cols = w.shape
    tr = _row_tile(rows, 1024)

    def body(p_ref, w_ref, *rest):
        o_ref = rest[-1]
        o_ref[...] = w_ref[...].astype(dtype)

    return pl.pallas_call(
        body, name=name,
        grid_spec=pltpu.PrefetchScalarGridSpec(
            num_scalar_prefetch=1, grid=(rows // tr,),
            in_specs=[pl.BlockSpec((tr, cols), lambda i, p: (i, 0))] + ([] if after is None else [ANY]),
            out_specs=pl.BlockSpec((None, tr, cols), lambda i, p: (p[0], i, 0))),
        out_shape=jax.ShapeDtypeStruct((N_CHIPS, rows, cols), dtype),
        compiler_params=_params(1),
    )(place, w, *([] if after is None else [after]))


def _adamw(w, g, m, v, name, after=None):
    rows, cols = w.shape
    tr = _row_tile(rows, 256)

    def body(w_ref, g_ref, m_ref, v_ref, *rest):
        go_ref, d_ref, nm_ref, nv_ref = rest[-4:]
        gv = g_ref[...]
        go_ref[...] = gv
        d_ref[...], nm_ref[...], nv_ref[...] = _adamw_update(w_ref[...], gv, m_ref[...], v_ref[...])

    tile = pl.BlockSpec((tr, cols), lambda i: (i, 0))
    shape = jax.ShapeDtypeStruct((rows, cols), F32)
    return pl.pallas_call(
        body, name=name, grid=(rows // tr,),
        in_specs=[tile] * 4 + ([] if after is None else [ANY]), out_specs=[tile] * 4, out_shape=[shape] * 4,
        compiler_params=_params(1),
    )(w, g, m, v, *([] if after is None else [after]))


def _mesh_pos():
    return lax.axis_index("x"), lax.axis_index("y"), lax.axis_index("c")


def _other_chips(x, y):
    return [(1 - x, y), (x, 1 - y), (1 - x, 1 - y)]


def _half_rows(ref, which):
    rh = ref.shape[-2] // 2
    return ref.at[pl.ds(which * rh, rh), :]


def _remote(src, dst, send_sems, recv_sems, sem, to):
    return pltpu.make_async_remote_copy(src_ref=src, dst_ref=dst, send_sem=send_sems.at[sem], recv_sem=recv_sems.at[sem],
                                        device_id=to, device_id_type=MESH)


HBM = pl.BlockSpec(memory_space=pltpu.HBM)
SEM = pl.BlockSpec(memory_space=pltpu.SEMAPHORE)
DATAFLOW_EFFECT = pltpu.SideEffectType.DATAFLOW_SIDE_EFFECTING


def _in_hbm(arrays):
    return [pltpu.with_memory_space_constraint(a, pltpu.HBM) for a in arrays]


def _hbm_like(arrays):
    return [pltpu.HBM(a.shape, a.dtype) for a in arrays]


GATHER_COPIES_PER_ARRAY = {"direct": 2, "forward": 2, "pass_near": 2, "pass_far": 1}


def _gather_copies(kind, refs, x, y, c):
    me, near_x, near_y, far = 2 * x + y, 2 * (1 - x) + y, 2 * x + (1 - y), 2 * (1 - x) + (1 - y)
    to_x, to_y, sibling = (1 - x, y, c), (x, 1 - y, c), (x, y, 1 - c)
    out = []
    for ref in refs:
        rh = ref.shape[1] // 2
        rq = rh // 2

        def half(chip, ref=ref, rh=rh):
            return ref.at[chip, pl.ds(c * rh, rh), :]

        def quarter(chip, q, ref=ref, rh=rh, rq=rq):
            return ref.at[chip, pl.ds(c * rh + q * rq, rq), :]

        if kind == "direct":
            out += [(half(me), half(me), to_x), (half(me), half(me), to_y)]
        elif kind == "forward":
            out += [(quarter(near_x, 0), quarter(near_x, 0), to_y), (quarter(near_y, 1), quarter(near_y, 1), to_x)]
        elif kind == "pass_near":
            out += [(half(near_x), half(near_x), sibling), (half(near_y), half(near_y), sibling)]
        else:
            assert kind == "pass_far"
            out += [(half(far), half(far), sibling)]
    return out


def _gather_step(name, bufs, waits, starts, after):
    nb, nw, ns = len(bufs), len(waits), len(starts)
    after = [] if after is None else list(after) if isinstance(after, (list, tuple)) else [after]
    n_after = len(after)

    def body(*refs):
        ins = refs[:nb]
        wait_sems = refs[nb:nb + 2 * nw]
        start_sems = refs[nb + 2 * nw + n_after:nb + 2 * nw + n_after + 2 * ns]
        token = refs[-1]
        x, y, c = _mesh_pos()
        for j, (kind, idxs, _, _) in enumerate(waits):
            for i, (s_ref, d_ref, to) in enumerate(_gather_copies(kind, [ins[t] for t in idxs], x, y, c)):
                came = _remote(s_ref, d_ref, wait_sems[2 * j], wait_sems[2 * j + 1], i, to)
                came.wait_recv()
                came.wait_send()
        for j, (kind, idxs) in enumerate(starts):
            for i, (s_ref, d_ref, to) in enumerate(_gather_copies(kind, [ins[t] for t in idxs], x, y, c)):
                _remote(s_ref, d_ref, start_sems[2 * j], start_sems[2 * j + 1], i, to).start()
        token[...] = jnp.zeros_like(token)

    sems = []
    for kind, idxs in starts:
        sems += [pltpu.SemaphoreType.DMA((GATHER_COPIES_PER_ARRAY[kind] * len(idxs),))] * 2
    operands = _in_hbm(bufs) + [sem for w in waits for sem in w[2:]] + after
    outs = pl.pallas_call(
        body, name=name,
        in_specs=[HBM] * nb + [SEM] * (2 * nw) + [ANY] * n_after,
        out_specs=[SEM] * (2 * ns) + [HBM] * nb + [pl.BlockSpec(memory_space=pltpu.VMEM)],
        out_shape=sems + _hbm_like(bufs) + [jax.ShapeDtypeStruct((8, 128), F32)],
        input_output_aliases={i: 2 * ns + i for i in range(nb)},
        compiler_params=pltpu.CompilerParams(has_side_effects=DATAFLOW_EFFECT),
    )(*operands)
    return outs[2 * ns:2 * ns + nb], [(outs[2 * j], outs[2 * j + 1]) for j in range(ns)], outs[-1]


class _Gather:
    def __init__(self, groups):
        self.groups = groups
        self.bufs = {}
        self.in_flight = {}

    def put(self, slotted):
        self.bufs.update(slotted)

    def step(self, name, waits, starts, after=None):
        names = []
        for _, group in list(waits) + list(starts):
            names += [n for n in self.groups[group] if n not in names]
        index = {n: i for i, n in enumerate(names)}

        def members(group):
            return [index[n] for n in self.groups[group]]

        wait_args = [(kind, members(group)) + self.in_flight.pop((kind, group)) for kind, group in waits]
        start_args = [(kind, members(group)) for kind, group in starts]
        bufs, sems, token = _gather_step(name, [self.bufs[n] for n in names], wait_args, start_args, after)
        self.bufs.update(zip(names, bufs))
        for (kind, group), pair in zip(starts, sems):
            self.in_flight[(kind, group)] = pair
        return token

    def arrays(self, group):
        return {n: self.bufs[n] for n in self.groups[group]}


def _sibling_halves_copies(srcs, dsts, x, y, c):
    out = []
    for s_ref, d_ref in zip(srcs, dsts, strict=True):
        rh = s_ref.shape[1] // 2
        out.append((s_ref.at[:, pl.ds((1 - c) * rh, rh), :], d_ref, (x, y, 1 - c)))
    return out


def _to_sibling_copies(srcs, dsts, x, y, c):
    return [(s_ref, d_ref, (x, y, 1 - c)) for s_ref, d_ref in zip(srcs, dsts, strict=True)]


def _chip_copies(srcs, dsts, x, y, c):
    out = []
    for s_ref, d_ref in zip(srcs, dsts, strict=True):
        for k, (px, py) in enumerate(_other_chips(x, y)):
            out.append((s_ref.at[2 * px + py], d_ref.at[k], (px, py, c)))
    return out


def _join_copies(srcs, dsts, x, y, c):
    out = []
    for s_ref in srcs:
        mine = _half_rows(s_ref, c)
        out.append((mine, mine, (x, y, 1 - c)))
    return out


def _exchange_start(copies_fn, n_copies, srcs, fresh, after, name):
    ns, nb = len(srcs), len(srcs) + len(fresh)

    def body(*refs):
        bufs, send, recv, token = refs[:nb], refs[nb + 1], refs[nb + 2], refs[-1]
        x, y, c = _mesh_pos()
        for i, (s_ref, d_ref, to) in enumerate(copies_fn(bufs[:ns], bufs[ns:] if fresh else bufs[:ns], x, y, c)):
            _remote(s_ref, d_ref, send, recv, i, to).start()
        token[...] = jnp.zeros_like(token)

    sems = [pltpu.SemaphoreType.DMA((n_copies,))] * 2
    outs = pl.pallas_call(
        body, name=name,
        in_specs=[HBM] * nb + [ANY], out_specs=[SEM, SEM] + [HBM] * nb + [pl.BlockSpec(memory_space=pltpu.VMEM)],
        out_shape=sems + _hbm_like(list(srcs) + list(fresh)) + [jax.ShapeDtypeStruct((8, 128), F32)],
        input_output_aliases={i: 2 + i for i in range(nb)},
        compiler_params=pltpu.CompilerParams(has_side_effects=DATAFLOW_EFFECT),
    )(*_in_hbm(list(srcs) + list(fresh)), after)
    return outs[0], outs[1], outs[2:2 + ns], outs[2 + ns:2 + nb], outs[-1]


def _exchange_done(copies_fn, srcs, fresh, send, recv, after, name):
    ns, nb = len(srcs), len(srcs) + len(fresh)

    def body(*refs):
        bufs, send_in, recv_in = refs[:nb], refs[nb], refs[nb + 1]
        x, y, c = _mesh_pos()
        for i, (s_ref, d_ref, to) in enumerate(copies_fn(bufs[:ns], bufs[ns:] if fresh else bufs[:ns], x, y, c)):
            came = _remote(s_ref, d_ref, send_in, recv_in, i, to)
            came.wait_send()
            came.wait_recv()

    outs = pl.pallas_call(
        body, name=name,
        in_specs=[HBM] * nb + [SEM, SEM, ANY], out_specs=[HBM] * nb,
        out_shape=_hbm_like(list(srcs) + list(fresh)),
        input_output_aliases={i: i for i in range(nb)},
        compiler_params=pltpu.CompilerParams(has_side_effects=DATAFLOW_EFFECT),
    )(*_in_hbm(list(srcs) + list(fresh)), send, recv, after)
    return outs[:ns], outs[ns:]


class _Reduce:
    def __init__(self, place, core, shards, mom_m, mom_v):
        self.place, self.core = place, core
        self.shards, self.mom_m, self.mom_v = shards, mom_m, mom_v
        self.state = {}
        self.results = {}

    def add(self, group, grads, after):
        names = list(grads)
        g4s = [g.reshape((N_CHIPS, -1, g.shape[-1])) if g.ndim == 2 else g for g in grads.values()]
        fresh = [lax.empty((N_CHIPS, g.shape[1] // 2, g.shape[2]), BF16) for g in g4s]
        send, recv, g4s, fresh, token = _exchange_start(_sibling_halves_copies, len(names), g4s, fresh, after,
                                                        "pair_start_" + group)
        self.state[group] = (0, names, send, recv, g4s, fresh)
        return token

    def send(self, group, theirs, after):
        names, srcs = list(theirs), list(theirs.values())
        fresh = [lax.empty(s.shape, BF16) for s in srcs]
        send, recv, srcs, fresh, token = _exchange_start(_to_sibling_copies, len(names), srcs, fresh, after, "pair_start_" + group)
        self.state[group] = ("sent", names, send, recv, srcs, fresh)
        return token

    def received(self, group, after):
        stage, names, send, recv, srcs, fresh = self.state.pop(group)
        assert stage == "sent"
        _, got = _exchange_done(_to_sibling_copies, srcs, fresh, send, recv, after, "pair_done_" + group)
        return dict(zip(names, got))

    def add_parts(self, group, parts):
        names, srcs = list(parts), list(parts.values())
        fresh = [lax.empty((N_CHIPS - 1,) + p.shape[1:], BF16) for p in srcs]
        send, recv, srcs, fresh, token = _exchange_start(_chip_copies, 3 * len(names), srcs, fresh, self.core, "chips_start_" + group)
        self.state[group] = (1, names, send, recv, srcs, fresh)
        return token

    def step(self, group, after):
        stage, names, send, recv, srcs, fresh = self.state[group]
        if stage == 0:
            g4s, ras = _exchange_done(_sibling_halves_copies, srcs, fresh, send, recv, after, "pair_done_" + group)
            parts = [_pair_sum(g, r, self.core, "pair_sum_" + n) for g, r, n in zip(g4s, ras, names)]
            fresh = [lax.empty((N_CHIPS - 1,) + p.shape[1:], BF16) for p in parts]
            send, recv, parts, fresh, token = _exchange_start(_chip_copies, 3 * len(names), parts, fresh, self.core,
                                                              "chips_start_" + group)
            self.state[group] = (1, names, send, recv, parts, fresh)
            return token
        if stage == 1:
            parts, rcs = _exchange_done(_chip_copies, srcs, fresh, send, recv, after, "chips_done_" + group)
            token = None
            for n, p, r in zip(names, parts, rcs):
                self.results[n] = _adamw_own_half(self.shards[n], self.mom_m[n], self.mom_v[n], p, r, self.place,
                                                  "adamw_own_" + n, after=token)
                token = self.results[n][2]
            wholes = [self.results[n][0] for n in names]
            send, recv, wholes, _, token = _exchange_start(_join_copies, len(names), wholes, [], token, "join_start_" + group)
            self.state[group] = (2, names, send, recv, wholes, [])
            return token
        assert stage == 2
        wholes, _ = _exchange_done(_join_copies, srcs, [], send, recv, after, "join_done_" + group)
        token = None
        for n, exchanged in zip(names, wholes):
            _, g, d, nm, nv = self.results[n]
            self.results[n] = _adamw_other_half(self.shards[n], self.mom_m[n], self.mom_v[n], exchanged, g, d, nm, nv,
                                                self.place, "adamw_other_" + n, after=token)
            token = self.results[n][1]
        del self.state[group]
        return token


N_DEV = 8


def _all_reduce_small(v):
    def body(v_ref, o_ref, slots, send_sems, recv_sems):
        x, y, c = _mesh_pos()
        me = 4 * x + 2 * y + c
        slots[me] = v_ref[...]
        peers = []
        for r in range(1, N_DEV):
            fx, fy, fc = (r >> 2) & 1, (r >> 1) & 1, r & 1
            peers.append((x + fx - 2 * x * fx, y + fy - 2 * y * fy, c + fc - 2 * c * fc))
        sends = []
        for r, peer in enumerate(peers):
            cp = _remote(v_ref, slots.at[me], send_sems, recv_sems, r, peer)
            cp.start()
            sends.append(cp)
        for r, (px, py, pc) in enumerate(peers):
            landed = slots.at[4 * px + 2 * py + pc]
            _remote(landed, landed, send_sems, recv_sems, r, (px, py, pc)).wait_recv()
        for cp in sends:
            cp.wait_send()
        acc = slots[0]
        for i in range(1, N_DEV):
            acc = acc + slots[i]
        o_ref[...] = acc

    vm = pl.BlockSpec(memory_space=pltpu.VMEM)
    return pl.pallas_call(
        body, name="small_grads_all_reduce",
        in_specs=[vm], out_specs=vm,
        out_shape=jax.ShapeDtypeStruct(v.shape, v.dtype),
        scratch_shapes=[pltpu.VMEM((N_DEV,) + v.shape, v.dtype), pltpu.SemaphoreType.DMA((N_DEV - 1,)),
                        pltpu.SemaphoreType.DMA((N_DEV - 1,))],
    )(v)


MATRICES = ("w_in", "w_attn_out", "w_conv_out", "w_o", "w_cq", "w_ckv", "w_co", "w_gate", "w_up", "w_down")
VECTORS = ("g_mix", "b_gate", "g_cross", "g_mem", "g_ffn", "g_final", "conv_w", "sink")
WEIGHT_ORDER = ("g_mix", "w_in", "sink", "conv_w", "b_gate", "w_attn_out", "w_conv_out", "w_o", "g_cross", "g_mem", "w_cq",
                "w_ckv", "w_co", "g_ffn", "w_gate", "w_up", "w_down", "g_final")
CONV_PAD_ROWS = 32
SMALL_ROWS = 8


def _pack(pieces):
    flat = jnp.concatenate([p.reshape(-1) for p in pieces])
    lane_group = SMALL_ROWS * 128
    total = -(-flat.shape[0] // lane_group) * lane_group
    flat = jnp.pad(flat, (0, total - flat.shape[0]))
    return flat.reshape(SMALL_ROWS, total // SMALL_ROWS), [p.size for p in pieces]


def _unpack(packed, pieces):
    flat = packed.reshape(-1)
    out, off = [], 0
    for p in pieces:
        out.append(flat[off:off + p.size].reshape(p.shape))
        off += p.size
    return out


def kernel(x, mem, g_mix, w_in, sink, conv_w, b_gate, w_attn_out, w_conv_out, w_o, g_cross, g_mem, w_cq, w_ckv, w_co, g_ffn, w_gate, w_up, w_down, g_final, loss_target, m_g_mix, m_w_in, m_sink, m_conv_w, m_b_gate, m_w_attn_out, m_w_conv_out, m_w_o, m_g_cross, m_g_mem, m_w_cq, m_w_ckv, m_w_co, m_g_ffn, m_w_gate, m_w_up, m_w_down, m_g_final, v_g_mix, v_w_in, v_sink, v_conv_w, v_b_gate, v_w_attn_out, v_w_conv_out, v_w_o, v_g_cross, v_g_mem, v_w_cq, v_w_ckv, v_w_co, v_g_ffn, v_w_gate, v_w_up, v_w_down, v_g_final):
    given = dict(g_mix=g_mix, w_in=w_in, sink=sink, conv_w=conv_w, b_gate=b_gate, w_attn_out=w_attn_out, w_conv_out=w_conv_out,
                 w_o=w_o, g_cross=g_cross, g_mem=g_mem, w_cq=w_cq, w_ckv=w_ckv, w_co=w_co, g_ffn=g_ffn, w_gate=w_gate, w_up=w_up,
                 w_down=w_down, g_final=g_final)
    mom_m = dict(g_mix=m_g_mix, w_in=m_w_in, sink=m_sink, conv_w=m_conv_w, b_gate=m_b_gate, w_attn_out=m_w_attn_out,
                 w_conv_out=m_w_conv_out, w_o=m_w_o, g_cross=m_g_cross, g_mem=m_g_mem, w_cq=m_w_cq, w_ckv=m_w_ckv, w_co=m_w_co,
                 g_ffn=m_g_ffn, w_gate=m_w_gate, w_up=m_w_up, w_down=m_w_down, g_final=m_g_final)
    mom_v = dict(g_mix=v_g_mix, w_in=v_w_in, sink=v_sink, conv_w=v_conv_w, b_gate=v_b_gate, w_attn_out=v_w_attn_out,
                 w_conv_out=v_w_conv_out, w_o=v_w_o, g_cross=v_g_cross, g_mem=v_g_mem, w_cq=v_w_cq, w_ckv=v_w_ckv, w_co=v_w_co,
                 g_ffn=v_g_ffn, w_gate=v_w_gate, w_up=v_w_up, w_down=v_w_down, g_final=v_g_final)
    xs, mems, target = x[0], mem[0], loss_target[0]
    d_model = xs.shape[1]
    chip = 2 * lax.axis_index("x") + lax.axis_index("y")
    core = jnp.reshape(lax.axis_index("c"), (1,)).astype(jnp.int32)
    place = jnp.stack([chip, lax.axis_index("c")]).astype(jnp.int32)

    shards = {n: given[n][0] for n in MATRICES}
    conv_cols = conv_w.shape[2]
    conv_pad = jnp.pad(conv_w[0], ((0, CONV_PAD_ROWS - conv_w.shape[1]), (0, 0)))
    fetch = _Gather(GATHER_GROUPS)
    first = {"w_in": _cast_to_slot(shards["w_in"], place, BF16, "to_slot_w_in"),
             "conv_w": _cast_to_slot(conv_pad, place, F32, "to_slot_conv_w")}
    fetch.put(first)
    tok = fetch.step("gather_start", [], [("direct", "in")])
    fetch.put({n: _cast_to_slot(shards[n], place, BF16, "to_slot_" + n, after=tok) for n in MATRICES if n != "w_in"})
    small = {n: given[n] for n in ("g_mix", "b_gate", "g_cross", "g_mem", "g_ffn")}
    small["g_final"] = g_final[None]
    small["sink"] = sink[0]

    reduce = _Reduce(place, core, shards, {n: mom_m[n][0] for n in MATRICES}, {n: mom_v[n][0] for n in MATRICES})
    sq, grad_x, small_grads = _local_step(xs, mems, target, small, fetch, reduce)

    loss_part = 0.5 * sq[0:1, 0:1] / d_model
    pieces = [small_grads[n] for n in VECTORS] + [loss_part]
    packed, _ = _pack(pieces)
    summed = _unpack(_all_reduce_small(packed), pieces)
    loss = summed[-1][0, 0]
    small_sum = dict(zip(VECTORS, summed[:-1]))
    small_sum["conv_w"] = lax.dynamic_slice_in_dim(small_sum["conv_w"], chip * conv_cols, conv_cols, axis=1)

    grad_out, delta, new_m, new_v = {}, {}, {}, {}
    like = [given[n] for n in VECTORS]
    pw, _ = _pack(like)
    pg, _ = _pack([small_sum[n] for n in VECTORS])
    pm, _ = _pack([mom_m[n] for n in VECTORS])
    pv, _ = _pack([mom_v[n] for n in VECTORS])
    tok = reduce.step("in", pg)
    _, pd, pnm, pnv = _adamw(pw, pg, pm, pv, "adamw_small", after=tok)
    for n, g, d, nm, nv in zip(VECTORS, [small_sum[n] for n in VECTORS], _unpack(pd, like), _unpack(pnm, like), _unpack(pnv, like)):
        grad_out[n] = g.reshape(given[n].shape)
        delta[n], new_m[n], new_v[n] = d, nm, nv
    reduce.step("in", pd)
    for n in MATRICES:
        g, d, nm, nv = reduce.results[n]
        grad_out[n], delta[n], new_m[n], new_v[n] = g[None], d[None], nm[None], nv[None]

    return (loss, grad_x[None], *[grad_out[n] for n in WEIGHT_ORDER], *[delta[n] for n in WEIGHT_ORDER],
            *[new_m[n] for n in WEIGHT_ORDER], *[new_v[n] for n in WEIGHT_ORDER])
```

```python
import functools

import jax
import jax.numpy as jnp
from jax import lax
from jax.experimental import pallas as pl
from jax.experimental.pallas import tpu as pltpu

F32 = jnp.float32
BF16 = jnp.bfloat16
MESH = pl.DeviceIdType.MESH
ANY = pl.BlockSpec(memory_space=pl.ANY)

VMEM_LIMIT_BYTES = 56 * 1024 * 1024

N_CHIPS = 4
HEAD_DIM = 128
N_Q_HEADS = 8
N_KV_HEADS = 2
Q_GROUP = N_Q_HEADS // N_KV_HEADS
ATTN_WIDTH = N_Q_HEADS * HEAD_DIM
KV_WIDTH = N_KV_HEADS * HEAD_DIM
WINDOW = 128
BLOCK = 128
BAND = 3 * BLOCK
ROPE_THETA = 10000.0
CONV_WIDTH = 1024
MEM_HEADS = 4
MEM_WIDTH = MEM_HEADS * HEAD_DIM
RMS_EPS = 1e-6
NEG_INF = -1e30
ATTN_SCALE = HEAD_DIM ** -0.5

Q_OFF, K_OFF, V_OFF, CU_OFF, CB_OFF, CC_OFF, GL_OFF = 0, 1024, 1280, 1536, 2560, 3584, 4608

ADAM_LR = 0.001
ADAM_B1 = 0.9
ADAM_B2 = 0.999
ADAM_EPS = 1e-08
ADAM_WD = 0.01
ADAM_STEP = 10
ADAM_C1 = 1.0 - ADAM_B1 ** ADAM_STEP
ADAM_C2 = 1.0 - ADAM_B2 ** ADAM_STEP


def _params(n_grid_axes):
    return pltpu.CompilerParams(dimension_semantics=("arbitrary",) * n_grid_axes, vmem_limit_bytes=VMEM_LIMIT_BYTES)


BF16_SUBLANES = 16


def _row_tile(rows, want):
    if rows <= want:
        return rows
    for t in range(want, 0, -BF16_SUBLANES):
        if rows % t == 0:
            return t
    return rows


def _matmul(a, b, *, mode, tm, tn, tk, out_dtypes, name, extras=(), epilogue=None, b_blocks=1, out_blocks=1, after=None,
            a_norm_gain=None):
    if mode == "tn":
        kdim, m = a.shape
    else:
        m, kdim = a.shape
    if b_blocks > 1:
        nb, brows, bcols = b.shape
        assert nb == b_blocks
        if mode == "nn":
            n = bcols * nb
            assert brows == kdim
        else:
            assert mode == "nt" and bcols * nb == kdim
            n = brows
    else:
        n = b.shape[0] if mode == "nt" else b.shape[1]
    tm, tn = min(tm, m), min(tn, n)
    assert m % tm == 0 and n % tn == 0 and tk == kdim, (name, m, n, kdim, tm, tn, tk)
    n_extra, n_out = len(extras), len(out_dtypes)
    n_after = 0 if after is None else 1
    normed = a_norm_gain is not None
    assert not normed or mode != "tn"

    if mode == "tn":
        a_spec = pl.BlockSpec((tk, tm), lambda j, i, k: (k, i))
        dims = (((0,), (0,)), ((), ()))
    else:
        a_spec = pl.BlockSpec((tm, tk), lambda j, i, k: (i, k))
        dims = (((1,), (0,)), ((), ())) if mode == "nn" else (((1,), (1,)), ((), ()))

    if b_blocks > 1 and mode == "nn":
        per = b.shape[2] // tn
        assert b.shape[2] % tn == 0
        b_spec = pl.BlockSpec((None, tk, tn), lambda j, i, k: (j // per, k, j % per))
    elif b_blocks > 1:
        b_spec = pl.BlockSpec((b_blocks, tn, b.shape[2]), lambda j, i, k: (0, j, 0))
    elif mode == "nt":
        b_spec = pl.BlockSpec((tn, tk), lambda j, i, k: (j, k))
    else:
        b_spec = pl.BlockSpec((tk, tn), lambda j, i, k: (k, j))

    tile_spec = pl.BlockSpec((tm, tn), lambda j, i, k: (i, j))
    if out_blocks > 1:
        ncols = n // out_blocks
        assert ncols % tn == 0
        oper = ncols // tn
        out_spec = pl.BlockSpec((None, tm, tn), lambda j, i, k: (j // oper, i, j % oper))
        out_shape = [jax.ShapeDtypeStruct((out_blocks, m, ncols), dt) for dt in out_dtypes]
    else:
        out_spec = tile_spec
        out_shape = [jax.ShapeDtypeStruct((m, n), dt) for dt in out_dtypes]

    def body(a_ref, b_ref, *rest):
        extra_refs = rest[:n_extra]
        out_refs = rest[n_extra + n_after + normed:n_extra + n_after + normed + n_out]
        if normed:
            xv = a_ref[...]
            a_ref = rest[-1]
            a_ref[...] = (xv * _rstd(xv) * rest[n_extra + n_after][...]).astype(BF16)
        if mode == "nt" and b_blocks > 1:
            cs = b.shape[2]
            acc = None
            for jb in range(b_blocks):
                prod = lax.dot_general(a_ref[:, jb * cs:(jb + 1) * cs].astype(BF16), b_ref[jb].astype(BF16), dims,
                                       preferred_element_type=F32)
                acc = prod if acc is None else acc + prod
        else:
            acc = lax.dot_general(a_ref[...].astype(BF16), b_ref[...].astype(BF16), dims, preferred_element_type=F32)
        tiles = (acc,) if epilogue is None else epilogue(acc, *[r[...] for r in extra_refs])
        for o_ref, t in zip(out_refs, tiles, strict=True):
            o_ref[...] = t.astype(o_ref.dtype)

    outs = pl.pallas_call(
        body,
        name=name,
        grid=(n // tn, m // tm, 1),
        in_specs=[a_spec, b_spec] + [tile_spec] * n_extra + [ANY] * n_after
        + ([pl.BlockSpec((1, kdim), lambda j, i, k: (0, 0))] if normed else []),
        out_specs=[out_spec] * n_out + ([a_spec] if normed else []),
        out_shape=out_shape + ([jax.ShapeDtypeStruct((m, kdim), BF16)] if normed else []),
        compiler_params=_params(3),
    )(a, b, *extras, *([] if after is None else [after]), *([a_norm_gain] if normed else []))
    return outs[0] if len(outs) == 1 else outs


def _add_residual(acc, res):
    return (acc + res,)


def _matmul_column_blocks(a, b4, blocks, out, *, tm, name, after=None, norm_gain=None):
    m, kdim = a.shape
    nb, _, cols = b4.shape
    tm = min(tm, m)
    assert m % tm == 0
    n_blocks = blocks.shape[0]
    normed = norm_gain is not None
    assert not normed or (n_blocks == 1 and out is None)

    def body(j_ref, a_ref, b_ref, *rest):
        if normed:
            xv = a_ref[...]
            hv = (xv * _rstd(xv) * rest[0][...]).astype(BF16)
            rest[-1][...] = hv
            rest[-2][...] = jnp.dot(hv, b_ref[...], preferred_element_type=F32)
        else:
            rest[-1][...] = jnp.dot(a_ref[...], b_ref[...], preferred_element_type=F32)

    extra = ([] if out is None else [out]) + ([] if after is None else [after])
    z_spec = pl.BlockSpec((tm, cols), lambda j, i, blk: (i, blk[j]))
    z_shape = jax.ShapeDtypeStruct((m, nb * cols), F32)
    return pl.pallas_call(
        body, name=name,
        grid_spec=pltpu.PrefetchScalarGridSpec(
            num_scalar_prefetch=1, grid=(n_blocks, m // tm),
            in_specs=[pl.BlockSpec((tm, kdim), lambda j, i, blk: (i, 0)),
                      pl.BlockSpec((None, kdim, cols), lambda j, i, blk: (blk[j], 0, 0))]
            + ([pl.BlockSpec((1, kdim), lambda j, i, blk: (0, 0))] if normed else []) + [ANY] * len(extra),
            out_specs=[z_spec, pl.BlockSpec((tm, kdim), lambda j, i, blk: (i, 0))] if normed else z_spec),
        out_shape=[z_shape, jax.ShapeDtypeStruct((m, kdim), BF16)] if normed else z_shape,
        input_output_aliases={} if out is None else {3: 0},
        compiler_params=_params(2),
    )(blocks, a, b4, *([norm_gain] if normed else []), *extra)


def _wgrad_half(a, b, core, *, theirs, row_sharded, tm, tn, name, add=None, after=None):
    kdim, m = a.shape
    n = b.shape[1]
    rs, cs = (m // N_CHIPS, n) if row_sharded else (m, n // N_CHIPS)
    rh = rs // 2
    tm, tn = min(tm, rh), min(tn, cs)
    assert rh % tm == 0 and cs % tn == 0, (name, rh, cs, tm, tn)
    mh, per = rh // tm, cs // tn
    has_add = add is not None

    def half(c):
        return 1 - c[0] if theirs else c[0]

    if row_sharded:
        grid = (n // tn, N_CHIPS * mh)
        a_spec = pl.BlockSpec((kdim, tm), lambda j, r, c: (0, ((r // mh) * 2 + half(c)) * mh + r % mh))
        o_spec = pl.BlockSpec((None, tm, tn), lambda j, r, c: (r // mh, r % mh, j))
    else:
        grid = (n // tn, mh)
        a_spec = pl.BlockSpec((kdim, tm), lambda j, r, c: (0, half(c) * mh + r))
        o_spec = pl.BlockSpec((None, tm, tn), lambda j, r, c: (j // per, r, j % per))
    b_spec = pl.BlockSpec((kdim, tn), lambda j, r, c: (0, j))

    def body(c_ref, a_ref, b_ref, *rest):
        o_ref = rest[-1]
        acc = lax.dot_general(a_ref[...].astype(BF16), b_ref[...].astype(BF16), (((0,), (0,)), ((), ())),
                              preferred_element_type=F32)
        if has_add:
            acc = acc + rest[0][...].astype(F32)
        o_ref[...] = acc.astype(BF16)

    operands = [a, b] + ([add] if has_add else []) + ([] if after is None else [after])
    return pl.pallas_call(
        body, name=name,
        grid_spec=pltpu.PrefetchScalarGridSpec(
            num_scalar_prefetch=1, grid=grid,
            in_specs=[a_spec, b_spec] + ([o_spec] if has_add else []) + ([] if after is None else [ANY]),
            out_specs=o_spec),
        out_shape=jax.ShapeDtypeStruct((N_CHIPS, rh, cs), BF16),
        compiler_params=_params(2),
    )(core, *operands)


def _rstd(x):
    return lax.rsqrt(jnp.mean(x * x, axis=-1, keepdims=True) + RMS_EPS)


def _rmsnorm(x, g, name):
    s, d = x.shape
    tr = _row_tile(s, 256)

    def body(x_ref, g_ref, o_ref):
        xv = x_ref[...]
        o_ref[...] = (xv * _rstd(xv) * g_ref[...]).astype(BF16)

    return pl.pallas_call(
        body, name=name, grid=(s // tr,),
        in_specs=[pl.BlockSpec((tr, d), lambda i: (i, 0)), pl.BlockSpec((1, d), lambda i: (0, 0))],
        out_specs=pl.BlockSpec((tr, d), lambda i: (i, 0)),
        out_shape=jax.ShapeDtypeStruct((s, d), BF16),
        compiler_params=_params(1),
    )(x, g)


def _rmsnorm_bwd(dh, x, g, dres, name):
    s, d = x.shape
    tr = _row_tile(s, 256)
    has_res = dres is not None

    def body(*refs):
        if has_res:
            dh_ref, x_ref, g_ref, res_ref, dx_ref, dxb_ref, dg_ref = refs
        else:
            dh_ref, x_ref, g_ref, dx_ref, dxb_ref, dg_ref = refs
        xv = x_ref[...]
        dhv = dh_ref[...].astype(F32)
        r = _rstd(xv)
        xn = xv * r
        dhg = dhv * g_ref[...]
        dx = r * (dhg - xn * jnp.mean(dhg * xn, axis=-1, keepdims=True))
        if has_res:
            dx = dx + res_ref[...]
        dx_ref[...] = dx
        dxb_ref[...] = dx.astype(BF16)
        part = jnp.sum(dhv * xn, axis=0, keepdims=True)

        @pl.when(pl.program_id(0) == 0)
        def _():
            dg_ref[...] = part

        @pl.when(pl.program_id(0) > 0)
        def _():
            dg_ref[...] += part

    row = pl.BlockSpec((tr, d), lambda i: (i, 0))
    vec = pl.BlockSpec((1, d), lambda i: (0, 0))
    return pl.pallas_call(
        body, name=name, grid=(s // tr,),
        in_specs=[row, row, vec] + ([row] if has_res else []),
        out_specs=[row, row, vec],
        out_shape=[jax.ShapeDtypeStruct((s, d), F32), jax.ShapeDtypeStruct((s, d), BF16), jax.ShapeDtypeStruct((1, d), F32)],
        compiler_params=_params(1),
    )(*([dh, x, g] + ([dres] if has_res else [])))


def _loss_head(x3, g, target):
    s, d = x3.shape
    tr = _row_tile(s, 256)

    def body(x_ref, g_ref, t_ref, dx_ref, dxb_ref, sq_ref, dg_ref):
        xv = x_ref[...]
        gv = g_ref[...]
        r = _rstd(xv)
        xn = xv * r
        err = xn * gv - t_ref[...]
        dy = err * (1.0 / d)
        dyg = dy * gv
        dx = r * (dyg - xn * jnp.mean(dyg * xn, axis=-1, keepdims=True))
        dx_ref[...] = dx
        dxb_ref[...] = dx.astype(BF16)
        sq = jnp.sum(jnp.sum(err * err, axis=1, keepdims=True), axis=0, keepdims=True)
        sq = jnp.broadcast_to(sq, (1, 128))
        part = jnp.sum(dy * xn, axis=0, keepdims=True)

        @pl.when(pl.program_id(0) == 0)
        def _():
            sq_ref[...] = sq
            dg_ref[...] = part

        @pl.when(pl.program_id(0) > 0)
        def _():
            sq_ref[...] += sq
            dg_ref[...] += part

    row = pl.BlockSpec((tr, d), lambda i: (i, 0))
    vec = pl.BlockSpec((1, d), lambda i: (0, 0))
    return pl.pallas_call(
        body, name="loss_head", grid=(s // tr,),
        in_specs=[row, vec, row],
        out_specs=[row, row, pl.BlockSpec((1, 128), lambda i: (0, 0)), vec],
        out_shape=[jax.ShapeDtypeStruct((s, d), F32), jax.ShapeDtypeStruct((s, d), BF16),
                   jax.ShapeDtypeStruct((1, 128), F32), jax.ShapeDtypeStruct((1, d), F32)],
        compiler_params=_params(1),
    )(x3, g, target)


def _rope_tables(s):
    inv = 1.0 / (ROPE_THETA ** (jnp.arange(0, HEAD_DIM, 2, dtype=F32) / HEAD_DIM))
    ang = jnp.arange(s, dtype=F32)[:, None] * inv[None, :]
    cos, sin = jnp.cos(ang), jnp.sin(ang)
    return jnp.concatenate([cos, cos], axis=1), jnp.concatenate([-sin, sin], axis=1)


def _swap_halves(t):
    return pltpu.roll(t, HEAD_DIM // 2, 1)


def _rope_fwd(z, cos_t, sin_t, after=None):
    s = z.shape[0]
    tr = _row_tile(s, 256)

    def body(zq_ref, zk_ref, zv_ref, c_ref, s_ref, *rest):
        q_ref, k_ref, v_ref = rest[-3:]
        c, sn = c_ref[...], s_ref[...]
        for hd in range(N_Q_HEADS):
            cols = slice(hd * HEAD_DIM, (hd + 1) * HEAD_DIM)
            t = zq_ref[:, cols]
            q_ref[:, cols] = (t * c + _swap_halves(t) * sn).astype(BF16)
        for hd in range(N_KV_HEADS):
            cols = slice(hd * HEAD_DIM, (hd + 1) * HEAD_DIM)
            t = zk_ref[:, cols]
            k_ref[:, cols] = (t * c + _swap_halves(t) * sn).astype(BF16)
        v_ref[...] = zv_ref[...].astype(BF16)

    tab = pl.BlockSpec((tr, HEAD_DIM), lambda i: (i, 0))
    return pl.pallas_call(
        body, name="rope_fwd", grid=(s // tr,),
        in_specs=[pl.BlockSpec((tr, ATTN_WIDTH), lambda i: (i, Q_OFF // ATTN_WIDTH)),
                  pl.BlockSpec((tr, KV_WIDTH), lambda i: (i, K_OFF // KV_WIDTH)),
                  pl.BlockSpec((tr, KV_WIDTH), lambda i: (i, V_OFF // KV_WIDTH)), tab, tab] + ([] if after is None else [ANY]),
        out_specs=[pl.BlockSpec((tr, ATTN_WIDTH), lambda i: (i, 0)), pl.BlockSpec((tr, KV_WIDTH), lambda i: (i, 0)),
                   pl.BlockSpec((tr, KV_WIDTH), lambda i: (i, 0))],
        out_shape=[jax.ShapeDtypeStruct((s, ATTN_WIDTH), BF16), jax.ShapeDtypeStruct((s, KV_WIDTH), BF16),
                   jax.ShapeDtypeStruct((s, KV_WIDTH), BF16)],
        compiler_params=_params(1),
    )(z, z, z, cos_t, sin_t, *([] if after is None else [after]))


def _rope_bwd(dq_rot, dk_rot, dv, cos_t, sin_t, dz):
    s = dq_rot.shape[0]
    tr = _row_tile(s, 256)
    qkv_width = V_OFF + KV_WIDTH

    def body(dq_ref, dk_ref, dv_ref, c_ref, s_ref, dz_in_ref, o_ref):
        c, sn = c_ref[...], s_ref[...]
        for hd in range(N_Q_HEADS):
            t = dq_ref[:, hd * HEAD_DIM:(hd + 1) * HEAD_DIM]
            o_ref[:, Q_OFF + hd * HEAD_DIM:Q_OFF + (hd + 1) * HEAD_DIM] = (t * c + _swap_halves(t * sn)).astype(BF16)
        for hd in range(N_KV_HEADS):
            t = dk_ref[:, hd * HEAD_DIM:(hd + 1) * HEAD_DIM]
            o_ref[:, K_OFF + hd * HEAD_DIM:K_OFF + (hd + 1) * HEAD_DIM] = (t * c + _swap_halves(t * sn)).astype(BF16)
        o_ref[:, V_OFF:V_OFF + KV_WIDTH] = dv_ref[...].astype(BF16)

    tab = pl.BlockSpec((tr, HEAD_DIM), lambda i: (i, 0))
    wide = pl.BlockSpec((tr, ATTN_WIDTH), lambda i: (i, 0))
    narrow = pl.BlockSpec((tr, KV_WIDTH), lambda i: (i, 0))
    return pl.pallas_call(
        body, name="rope_bwd", grid=(s // tr,),
        in_specs=[wide, narrow, narrow, tab, tab, ANY],
        out_specs=pl.BlockSpec((tr, qkv_width), lambda i: (i, 0)),
        out_shape=jax.ShapeDtypeStruct(dz.shape, dz.dtype),
        input_output_aliases={5: 0},
        compiler_params=_params(1),
    )(dq_rot, dk_rot, dv, cos_t, sin_t, dz)


def _swa_band(i, s):
    return pl.multiple_of(jnp.clip((i - 1) * BLOCK, 0, s - BAND), BLOCK)


SWA_HEADS_PER_PASS = Q_GROUP


def _swa_probs(q_ref, k_ref, sink_ref, heads, start, valid):
    kv = heads[0] // Q_GROUP
    cols = slice(kv * HEAD_DIM, (kv + 1) * HEAD_DIM)
    kb = k_ref[pl.ds(start, BAND), cols]
    qg = jnp.concatenate([q_ref[:, hd * HEAD_DIM:(hd + 1) * HEAD_DIM] for hd in heads], axis=0)
    sc = lax.dot_general(qg, kb, (((1,), (1,)), ((), ())), preferred_element_type=F32) * ATTN_SCALE
    sc = jnp.where(valid, sc, NEG_INF)
    sk = jnp.concatenate([jnp.full((BLOCK, 1), sink_ref[hd], F32) for hd in heads], axis=0)
    mx = jnp.maximum(jnp.max(sc, axis=1, keepdims=True), sk)
    e = jnp.exp(sc - mx)
    es = jnp.exp(sk - mx)
    inv = 1.0 / (jnp.sum(e, axis=1, keepdims=True) + es)
    return qg, kb, e * inv, es * inv


def _swa_head_passes():
    return [list(range(h0, h0 + SWA_HEADS_PER_PASS)) for h0 in range(0, N_Q_HEADS, SWA_HEADS_PER_PASS)]


def _swa_valid(i, start):
    q_pos = i * BLOCK + lax.broadcasted_iota(jnp.int32, (BLOCK, 1), 0)
    q_pos = jnp.concatenate([q_pos] * SWA_HEADS_PER_PASS, axis=0)
    k_pos = start + lax.broadcasted_iota(jnp.int32, (1, BAND), 1)
    return jnp.abs(k_pos - q_pos) <= WINDOW


def _swa_fwd(q, k, v, sink):
    s = q.shape[0]
    assert s % BLOCK == 0 and s >= BAND

    def body(sink_ref, q_ref, k_ref, v_ref, o_ref):
        i = pl.program_id(0)
        start = _swa_band(i, s)
        valid = _swa_valid(i, start)
        for heads in _swa_head_passes():
            kv = heads[0] // Q_GROUP
            _, _, p, _ = _swa_probs(q_ref, k_ref, sink_ref, heads, start, valid)
            vb = v_ref[pl.ds(start, BAND), kv * HEAD_DIM:(kv + 1) * HEAD_DIM]
            o = jnp.dot(p.astype(BF16), vb, preferred_element_type=F32)
            for g, hd in enumerate(heads):
                o_ref[:, hd * HEAD_DIM:(hd + 1) * HEAD_DIM] = o[g * BLOCK:(g + 1) * BLOCK].astype(BF16)

    whole = pl.BlockSpec((s, KV_WIDTH), lambda i: (0, 0))
    blk = pl.BlockSpec((BLOCK, ATTN_WIDTH), lambda i: (i, 0))
    return pl.pallas_call(
        body, name="swa_fwd", grid=(s // BLOCK,),
        in_specs=[pl.BlockSpec(memory_space=pltpu.SMEM), blk, whole, whole],
        out_specs=blk,
        out_shape=jax.ShapeDtypeStruct((s, ATTN_WIDTH), BF16),
        compiler_params=_params(1),
    )(sink, q, k, v)


def _swa_bwd(q, k, v, d_out, sink):
    s = q.shape[0]

    def body(sink_ref, q_ref, k_ref, v_ref, do_ref, dq_ref, dk_ref, dv_ref, dsink_ref):
        i = pl.program_id(0)

        @pl.when(i == 0)
        def _():
            dk_ref[...] = jnp.zeros_like(dk_ref)
            dv_ref[...] = jnp.zeros_like(dv_ref)
            dsink_ref[...] = jnp.zeros_like(dsink_ref)

        start = _swa_band(i, s)
        valid = _swa_valid(i, start)
        for heads in _swa_head_passes():
            kv = heads[0] // Q_GROUP
            cols = slice(kv * HEAD_DIM, (kv + 1) * HEAD_DIM)
            qg, kb, p, p_sink = _swa_probs(q_ref, k_ref, sink_ref, heads, start, valid)
            vb = v_ref[pl.ds(start, BAND), cols]
            dog = jnp.concatenate([do_ref[:, hd * HEAD_DIM:(hd + 1) * HEAD_DIM] for hd in heads], axis=0)
            dp = lax.dot_general(dog, vb, (((1,), (1,)), ((), ())), preferred_element_type=F32)
            delta = jnp.sum(p * dp, axis=1, keepdims=True)
            ds = (p * (dp - delta) * ATTN_SCALE).astype(BF16)
            dqg = jnp.dot(ds, kb, preferred_element_type=F32)
            dk_ref[pl.ds(start, BAND), cols] += lax.dot_general(ds, qg, (((0,), (0,)), ((), ())), preferred_element_type=F32)
            dv_ref[pl.ds(start, BAND), cols] += lax.dot_general(p.astype(BF16), dog, (((0,), (0,)), ((), ())),
                                                                 preferred_element_type=F32)
            dsk = p_sink * delta
            for g, hd in enumerate(heads):
                dq_ref[:, hd * HEAD_DIM:(hd + 1) * HEAD_DIM] = dqg[g * BLOCK:(g + 1) * BLOCK]
                tot = jnp.sum(dsk[g * BLOCK:(g + 1) * BLOCK], axis=0, keepdims=True)
                dsink_ref[hd:hd + 1, :] -= jnp.broadcast_to(tot, (1, 128))

    whole = pl.BlockSpec((s, KV_WIDTH), lambda i: (0, 0))
    blk = pl.BlockSpec((BLOCK, ATTN_WIDTH), lambda i: (i, 0))
    return pl.pallas_call(
        body, name="swa_bwd", grid=(s // BLOCK,),
        in_specs=[pl.BlockSpec(memory_space=pltpu.SMEM), blk, whole, whole, blk],
        out_specs=[blk, whole, whole, pl.BlockSpec((N_Q_HEADS, 128), lambda i: (0, 0))],
        out_shape=[jax.ShapeDtypeStruct((s, ATTN_WIDTH), F32), jax.ShapeDtypeStruct((s, KV_WIDTH), F32),
                   jax.ShapeDtypeStruct((s, KV_WIDTH), F32), jax.ShapeDtypeStruct((N_Q_HEADS, 128), F32)],
        compiler_params=_params(1),
    )(sink, q, k, v, d_out)


CONV_CHUNK = 256


def _shift_rows(t, rows, down):
    n = t.shape[0]
    rolled = pltpu.roll(t, 1 if down else n - 1, 0)
    edge = 0 if down else n - 1
    return jnp.where(rows == edge, 0.0, rolled)


def _conv_specs(s):
    def z_spec(off):
        return pl.BlockSpec((s, CONV_CHUNK), lambda j, off=off: (0, off // CONV_CHUNK + j))
    chunk = pl.BlockSpec((s, CONV_CHUNK), lambda j: (0, j))
    w_spec = pl.BlockSpec((3, CONV_CHUNK), lambda j: (0, j))
    return z_spec(CU_OFF), z_spec(CB_OFF), z_spec(CC_OFF), chunk, w_spec


def _conv_fwd(z, conv_w, after=None):
    s = z.shape[0]
    cu_spec, cb_spec, cc_spec, chunk, w_spec = _conv_specs(s)

    def body(cu_ref, cb_ref, cc_ref, w_ref, *rest):
        o_ref = rest[-1]
        rows = lax.broadcasted_iota(jnp.int32, (s, 1), 0)
        t = cc_ref[...] * cu_ref[...]
        c3 = _shift_rows(t, rows, True) * w_ref[0:1, :] + t * w_ref[1:2, :] + _shift_rows(t, rows, False) * w_ref[2:3, :]
        o_ref[...] = (cb_ref[...] * c3).astype(BF16)

    return pl.pallas_call(
        body, name="conv_fwd", grid=(CONV_WIDTH // CONV_CHUNK,),
        in_specs=[cu_spec, cb_spec, cc_spec, w_spec] + ([] if after is None else [ANY]),
        out_specs=chunk,
        out_shape=jax.ShapeDtypeStruct((s, CONV_WIDTH), BF16),
        compiler_params=_params(1),
    )(z, z, z, conv_w, *([] if after is None else [after]))


def _conv_bwd(z, conv_w, d_co, dz):
    s = z.shape[0]
    cu_spec, cb_spec, cc_spec, chunk, w_spec = _conv_specs(s)
    n_chunks = CONV_WIDTH // CONV_CHUNK
    offsets = (CU_OFF, CB_OFF, CC_OFF)

    def body(cu_ref, cb_ref, cc_ref, w_ref, d_ref, dz_in_ref, dz_ref, dw_ref, buf, sems):
        j = pl.program_id(0)

        def copies(j_at):
            return [pltpu.make_async_copy(buf.at[h], dz_ref.at[:, pl.ds(off + j_at * CONV_CHUNK, CONV_CHUNK)], sems.at[h])
                    for h, off in enumerate(offsets)]

        rows = lax.broadcasted_iota(jnp.int32, (s, 1), 0)
        cu, cc = cu_ref[...], cc_ref[...]
        t = cc * cu
        t_dn, t_up = _shift_rows(t, rows, True), _shift_rows(t, rows, False)
        c3 = t_dn * w_ref[0:1, :] + t * w_ref[1:2, :] + t_up * w_ref[2:3, :]
        d = d_ref[...]
        dc3 = d * cb_ref[...]
        dw_ref[0:1, :] = jnp.sum(dc3 * t_dn, axis=0, keepdims=True)
        dw_ref[1:2, :] = jnp.sum(dc3 * t, axis=0, keepdims=True)
        dw_ref[2:3, :] = jnp.sum(dc3 * t_up, axis=0, keepdims=True)
        dt = _shift_rows(dc3, rows, False) * w_ref[0:1, :] + dc3 * w_ref[1:2, :] + _shift_rows(dc3, rows, True) * w_ref[2:3, :]

        @pl.when(j > 0)
        def _():
            for cp in copies(j):
                cp.wait()

        buf[0] = (dt * cc).astype(BF16)
        buf[1] = (d * c3).astype(BF16)
        buf[2] = (dt * cu).astype(BF16)
        for cp in copies(j):
            cp.start()

        @pl.when(j == n_chunks - 1)
        def _():
            for cp in copies(j):
                cp.wait()

    return pl.pallas_call(
        body, name="conv_bwd", grid=(n_chunks,),
        in_specs=[cu_spec, cb_spec, cc_spec, w_spec, chunk, ANY],
        out_specs=[ANY, w_spec],
        out_shape=[jax.ShapeDtypeStruct(dz.shape, dz.dtype), jax.ShapeDtypeStruct((3, CONV_WIDTH), F32)],
        input_output_aliases={5: 0},
        scratch_shapes=[pltpu.VMEM((3, s, CONV_CHUNK), BF16), pltpu.SemaphoreType.DMA((3,))],
        compiler_params=_params(1),
    )(z, z, z, conv_w, d_co, dz)


GATE_CHUNK = 512


def _gate_specs(s, d, tr):
    n_chunks = d // GATE_CHUNK
    za = pl.BlockSpec((tr, GATE_CHUNK), lambda j, i: (i, GL_OFF // GATE_CHUNK + j))
    zc = pl.BlockSpec((tr, GATE_CHUNK), lambda j, i: (i, GL_OFF // GATE_CHUNK + n_chunks + j))
    ba = pl.BlockSpec((1, GATE_CHUNK), lambda j, i: (0, j))
    bc = pl.BlockSpec((1, GATE_CHUNK), lambda j, i: (0, n_chunks + j))
    tile = pl.BlockSpec((tr, GATE_CHUNK), lambda j, i: (i, j))
    return za, zc, ba, bc, tile


def _gate_fwd(z, b_gate, ya, yc):
    s, d = ya.shape
    tr = _row_tile(s, 512)
    za, zc, ba, bc, tile = _gate_specs(s, d, tr)

    def body(za_ref, zc_ref, ba_ref, bc_ref, ya_ref, yc_ref, o_ref):
        ga = jax.nn.sigmoid(za_ref[...] + ba_ref[...])
        gc = jax.nn.sigmoid(zc_ref[...] + bc_ref[...])
        o_ref[...] = (ga * ya_ref[...] + gc * yc_ref[...]).astype(BF16)

    return pl.pallas_call(
        body, name="gate_fwd", grid=(d // GATE_CHUNK, s // tr),
        in_specs=[za, zc, ba, bc, tile, tile],
        out_specs=tile,
        out_shape=jax.ShapeDtypeStruct((s, d), BF16),
        compiler_params=_params(2),
    )(z, z, b_gate, b_gate, ya, yc)


def _gate_bwd(z, b_gate, ya, yc, dmix):
    s, d = ya.shape
    tr = _row_tile(s, 512)
    za, zc, ba, bc, tile = _gate_specs(s, d, tr)
    vec = pl.BlockSpec((1, GATE_CHUNK), lambda j, i: (0, j))
    n_rows = s // tr
    in_width = z.shape[1]

    def body(za_ref, zc_ref, ba_ref, bc_ref, ya_ref, yc_ref, dm_ref, dya_ref, dyc_ref, dz_ref, dba_ref, dbc_ref, buf, sems):
        j, i = pl.program_id(0), pl.program_id(1)

        def copies(j_at, i_at):
            rows = pl.ds(i_at * tr, tr)
            return [pltpu.make_async_copy(buf.at[h], dz_ref.at[rows, pl.ds(GL_OFF + h * d + j_at * GATE_CHUNK, GATE_CHUNK)],
                                          sems.at[h]) for h in range(2)]

        ga = jax.nn.sigmoid(za_ref[...] + ba_ref[...])
        gc = jax.nn.sigmoid(zc_ref[...] + bc_ref[...])
        dm = dm_ref[...]
        dya_ref[...] = (dm * ga).astype(BF16)
        dyc_ref[...] = (dm * gc).astype(BF16)
        dla = dm * ya_ref[...] * ga * (1.0 - ga)
        dlc = dm * yc_ref[...] * gc * (1.0 - gc)

        @pl.when(j * n_rows + i > 0)
        def _():
            for cp in copies(j, i):
                cp.wait()

        buf[0] = dla.astype(BF16)
        buf[1] = dlc.astype(BF16)
        for cp in copies(j, i):
            cp.start()

        @pl.when((j == d // GATE_CHUNK - 1) & (i == n_rows - 1))
        def _():
            for cp in copies(j, i):
                cp.wait()

        pa = jnp.sum(dla, axis=0, keepdims=True)
        pc = jnp.sum(dlc, axis=0, keepdims=True)

        @pl.when(i == 0)
        def _():
            dba_ref[...] = pa
            dbc_ref[...] = pc

        @pl.when(i > 0)
        def _():
            dba_ref[...] += pa
            dbc_ref[...] += pc

    big = jax.ShapeDtypeStruct((s, d), BF16)
    small = jax.ShapeDtypeStruct((1, d), F32)
    return pl.pallas_call(
        body, name="gate_bwd", grid=(d // GATE_CHUNK, n_rows),
        in_specs=[za, zc, ba, bc, tile, tile, tile],
        out_specs=[tile, tile, ANY, vec, vec],
        out_shape=[big, big, jax.ShapeDtypeStruct((s, in_width), BF16), small, small],
        scratch_shapes=[pltpu.VMEM((2, tr, GATE_CHUNK), BF16), pltpu.SemaphoreType.DMA((2,))],
        compiler_params=_params(2),
    )(z, z, b_gate, b_gate, ya, yc, dmix)


def _cross_probs(q_ref, kv_ref, hd):
    cols = slice(hd * HEAD_DIM, (hd + 1) * HEAD_DIM)
    qh = q_ref[:, cols]
    kh = kv_ref[:, cols]
    sc = lax.dot_general(qh, kh, (((1,), (1,)), ((), ())), preferred_element_type=F32) * ATTN_SCALE
    e = jnp.exp(sc - jnp.max(sc, axis=1, keepdims=True))
    return qh, kh, e * (1.0 / jnp.sum(e, axis=1, keepdims=True))


def _cross_fwd(qc, kvc):
    s = qc.shape[0]
    n_mem = kvc.shape[0]
    tq = _row_tile(s, 256)

    def body(q_ref, kv_ref, o_ref):
        for hd in range(MEM_HEADS):
            _, _, p = _cross_probs(q_ref, kv_ref, hd)
            vh = kv_ref[:, MEM_WIDTH + hd * HEAD_DIM:MEM_WIDTH + (hd + 1) * HEAD_DIM]
            o_ref[:, hd * HEAD_DIM:(hd + 1) * HEAD_DIM] = jnp.dot(p.astype(BF16), vh, preferred_element_type=F32).astype(BF16)

    return pl.pallas_call(
        body, name="cross_fwd", grid=(s // tq,),
        in_specs=[pl.BlockSpec((tq, MEM_WIDTH), lambda i: (i, 0)), pl.BlockSpec((n_mem, 2 * MEM_WIDTH), lambda i: (0, 0))],
        out_specs=pl.BlockSpec((tq, MEM_WIDTH), lambda i: (i, 0)),
        out_shape=jax.ShapeDtypeStruct((s, MEM_WIDTH), BF16),
        compiler_params=_params(1),
    )(qc, kvc)


def _cross_bwd(qc, kvc, d_out):
    s = qc.shape[0]
    n_mem = kvc.shape[0]
    tq = _row_tile(s, 256)

    def body(q_ref, kv_ref, do_ref, dq_ref, dkv_ref):
        @pl.when(pl.program_id(0) == 0)
        def _():
            dkv_ref[...] = jnp.zeros_like(dkv_ref)

        for hd in range(MEM_HEADS):
            cols = slice(hd * HEAD_DIM, (hd + 1) * HEAD_DIM)
            vcols = slice(MEM_WIDTH + hd * HEAD_DIM, MEM_WIDTH + (hd + 1) * HEAD_DIM)
            qh, kh, p = _cross_probs(q_ref, kv_ref, hd)
            doh = do_ref[:, cols]
            dp = lax.dot_general(doh, kv_ref[:, vcols], (((1,), (1,)), ((), ())), preferred_element_type=F32)
            ds = (p * (dp - jnp.sum(p * dp, axis=1, keepdims=True)) * ATTN_SCALE).astype(BF16)
            dq_ref[:, cols] = jnp.dot(ds, kh, preferred_element_type=F32).astype(BF16)
            dkv_ref[:, cols] += lax.dot_general(ds, qh, (((0,), (0,)), ((), ())), preferred_element_type=F32)
            dkv_ref[:, vcols] += lax.dot_general(p.astype(BF16), doh, (((0,), (0,)), ((), ())), preferred_element_type=F32)

    qspec = pl.BlockSpec((tq, MEM_WIDTH), lambda i: (i, 0))
    kvspec = pl.BlockSpec((n_mem, 2 * MEM_WIDTH), lambda i: (0, 0))
    return pl.pallas_call(
        body, name="cross_bwd", grid=(s // tq,),
        in_specs=[qspec, kvspec, qspec],
        out_specs=[qspec, kvspec],
        out_shape=[jax.ShapeDtypeStruct((s, MEM_WIDTH), BF16), jax.ShapeDtypeStruct((n_mem, 2 * MEM_WIDTH), F32)],
        compiler_params=_params(1),
    )(qc, kvc, d_out)


def _swiglu_fwd(up, gate):
    sg = jax.nn.sigmoid(gate)
    silu = gate * sg
    return silu * up, up * (sg * (1.0 + gate * (1.0 - sg))), silu


def _swiglu_bwd(d_act, dact_dgate, dact_dup):
    return d_act * dact_dgate.astype(F32), d_act * dact_dup.astype(F32)


GATHER_GROUPS = {"in": ("w_in", "conv_w"), "mid": ("w_attn_out", "w_conv_out", "w_o", "w_cq", "w_ckv", "w_co"),
                 "gate": ("w_gate",), "up": ("w_up",), "down": ("w_down",)}


def _local_step(xs, mems, target, small, fetch, reduce):
    s, d = xs.shape
    w4 = {}
    cos_t, sin_t = _rope_tables(s)

    def near(group, done, then, after):
        waits = [("direct", group)] + ([("pass_near", done), ("pass_far", done)] if done else [])
        starts = [("forward", group), ("pass_near", group)] + [("direct", g) for g in then]
        tok = fetch.step("gather_near_" + group, waits, starts, after)
        if done:
            w4.update(fetch.arrays(done))
        return tok

    def far(group, then, after):
        return fetch.step("gather_far_" + group, [("forward", group)], [("pass_far", group)] + [("direct", g) for g in then], after)

    def last(group, after):
        tok = fetch.step("gather_done_" + group, [("pass_near", group), ("pass_far", group)], [], after)
        w4.update(fetch.arrays(group))
        return tok

    slots_filled = [a for g in ("gate", "up", "down") for a in fetch.arrays(g).values()]
    chip_x, chip_y = reduce.place[0] // 2, reduce.place[0] % 2
    own_block = jnp.stack([2 * chip_x + chip_y]).astype(jnp.int32)
    near_blocks = jnp.stack([2 * (1 - chip_x) + chip_y, 2 * chip_x + (1 - chip_y)]).astype(jnp.int32)
    far_block = jnp.stack([2 * (1 - chip_x) + (1 - chip_y)]).astype(jnp.int32)
    z, h = _matmul_column_blocks(xs, fetch.arrays("in")["w_in"], own_block, None, tm=512, name="in_proj_own",
                                 norm_gain=small["g_mix"])
    tok = near("in", None, ["mid"], [z] + slots_filled)
    tok = fetch.step("gather_near_done_in", [("pass_near", "in")], [], tok)
    z = _matmul_column_blocks(h, fetch.arrays("in")["w_in"], near_blocks, z, tm=512, name="in_proj_near", after=tok)
    tok = far("in", [], z)
    tok = fetch.step("gather_done_in", [("pass_far", "in")], [], tok)
    w4.update(fetch.arrays("in"))
    z = _matmul_column_blocks(h, w4["w_in"], far_block, z, tm=512, name="in_proj_far", after=tok)
    conv4 = w4["conv_w"]
    conv_w = conv4[:, :3, :].transpose(1, 0, 2).reshape(3, N_CHIPS * conv4.shape[2])
    c_in = w4["w_in"].shape[2]
    tok = near("mid", None, ["gate"], z)
    q_rot, k_rot, v_b = _rope_fwd(z, cos_t, sin_t)
    attn = _swa_fwd(q_rot, k_rot, v_b, small["sink"])
    co = _conv_fwd(z, conv_w)
    tok = far("mid", ["up"], attn)
    tok = last("mid", tok)
    w_o = w4["w_o"].reshape(-1, w4["w_o"].shape[-1])
    c_d = w4["w_attn_out"].shape[2]
    ya = _matmul(attn, w4["w_attn_out"], mode="nn", tm=1024, tn=c_d, tk=ATTN_WIDTH, out_dtypes=[F32], name="attn_out_proj",
                 b_blocks=N_CHIPS, after=tok)
    yc = _matmul(co, w4["w_conv_out"], mode="nn", tm=1024, tn=c_d, tk=CONV_WIDTH, out_dtypes=[F32], name="conv_out_proj",
                 b_blocks=N_CHIPS)
    mix = _gate_fwd(z, small["b_gate"], ya, yc)
    x1 = _matmul(mix, w_o, mode="nn", tm=512, tn=1024, tk=d, out_dtypes=[F32], name="mix_out_proj", extras=[xs],
                 epilogue=_add_residual)
    tok = near("gate", None, ["down"], x1)
    w_cq = w4["w_cq"].reshape(-1, w4["w_cq"].shape[-1])
    w_ckv = w4["w_ckv"].reshape(-1, w4["w_ckv"].shape[-1])
    memn = _rmsnorm(mems, small["g_mem"], "norm_mem")
    qc, hc = _matmul(x1, w_cq, mode="nn", tm=1024, tn=MEM_WIDTH, tk=d, out_dtypes=[BF16], name="cross_q_proj", after=tok,
                     a_norm_gain=small["g_cross"])
    kvc = _matmul(memn, w_ckv, mode="nn", tm=256, tn=2 * MEM_WIDTH, tk=d, out_dtypes=[BF16], name="cross_kv_proj")
    oc = _cross_fwd(qc, kvc)
    tok = far("gate", [], oc)
    x2 = _matmul(oc, w4["w_co"], mode="nn", tm=1024, tn=c_d, tk=MEM_WIDTH, out_dtypes=[F32], name="cross_out_proj",
                 extras=[x1], epilogue=_add_residual, b_blocks=N_CHIPS, after=tok)
    hf = _rmsnorm(x2, small["g_ffn"], "norm_ffn")
    tok = near("up", "gate", [], hf)
    c_ff = w4["w_gate"].shape[2]
    gate = _matmul(hf, w4["w_gate"], mode="nn", tm=512, tn=c_ff, tk=d, out_dtypes=[F32], name="ffn_gate_proj", b_blocks=N_CHIPS,
                   after=tok)
    tok = far("up", [], gate)
    tok = near("down", "up", [], tok)
    act, dact_dgate, dact_dup = _matmul(hf, w4["w_up"], mode="nn", tm=512, tn=c_ff, tk=d, out_dtypes=[BF16, BF16, BF16],
                                        name="ffn_up_proj", extras=[gate], epilogue=_swiglu_fwd, b_blocks=N_CHIPS, after=tok)
    tok = far("down", [], act)
    last("down", tok)
    w_down = w4["w_down"].reshape(-1, w4["w_down"].shape[-1])
    x3 = _matmul(act, w_down, mode="nn", tm=512, tn=512, tk=w_down.shape[0], out_dtypes=[F32], name="ffn_down_proj", extras=[x2],
                 epilogue=_add_residual)
    dx3, dx3b, sq, dg_final = _loss_head(x3, small["g_final"], target)

    da, du = _matmul(dx3b, w_down, mode="nt", tm=512, tn=c_ff, tk=d, out_dtypes=[BF16, BF16], name="ffn_down_bwd",
                     extras=[dact_dgate, dact_dup], epilogue=_swiglu_bwd)
    core = reduce.core
    ffn_shape = dict(row_sharded=False, tm=1024, tn=c_ff)
    g_down = _matmul(act, dx3b, mode="tn", tm=c_ff, tn=1024, tk=s, out_dtypes=[BF16], name="ffn_down_wgrad")
    tok = reduce.add("down", {"w_down": g_down}, da)
    t_gate = _wgrad_half(hf, da, core, theirs=True, name="ffn_gate_wgrad_theirs", after=tok, **ffn_shape)
    tok = reduce.step("down", t_gate)
    t_up = _wgrad_half(hf, du, core, theirs=True, name="ffn_up_wgrad_theirs", after=tok, **ffn_shape)
    tok = reduce.send("ffn", {"w_gate": t_gate, "w_up": t_up}, dx3b)
    dhf = _matmul(da, w4["w_gate"], mode="nt", tm=512, tn=1024, tk=N_CHIPS * c_ff, out_dtypes=[F32], name="ffn_gate_bwd", b_blocks=N_CHIPS,
                  after=tok)
    got = reduce.received("ffn", dhf)
    p_gate = _wgrad_half(hf, da, core, theirs=False, name="ffn_gate_wgrad_mine", add=got["w_gate"], **ffn_shape)
    p_up = _wgrad_half(hf, du, core, theirs=False, name="ffn_up_wgrad_mine", add=got["w_up"], **ffn_shape)
    tok = reduce.add_parts("ffn", {"w_gate": p_gate, "w_up": p_up})
    dhf = _matmul(du, w4["w_up"], mode="nt", tm=512, tn=1024, tk=N_CHIPS * c_ff, out_dtypes=[F32], name="ffn_up_bwd", extras=[dhf],
                  epilogue=_add_residual, b_blocks=N_CHIPS, after=tok)
    tok = reduce.step("down", dhf)
    dx2, dx2b, dg_ffn = _rmsnorm_bwd(dhf, x2, small["g_ffn"], dx3, "norm_ffn_bwd")

    d_oc = _matmul(dx2b, w4["w_co"], mode="nt", tm=1024, tn=MEM_WIDTH, tk=d, out_dtypes=[BF16], name="cross_out_bwd",
                   b_blocks=N_CHIPS, after=tok)
    g_co = _matmul(oc, dx2b, mode="tn", tm=MEM_WIDTH, tn=c_d, tk=s, out_dtypes=[BF16], name="cross_out_wgrad", out_blocks=N_CHIPS)
    tok = reduce.step("down", g_co)
    dqc, dkvc = _cross_bwd(qc, kvc, d_oc)
    g_cq = _matmul(hc, dqc, mode="tn", tm=1024, tn=MEM_WIDTH, tk=s, out_dtypes=[BF16], name="cross_q_wgrad", after=tok)
    dhc = _matmul(dqc, w_cq, mode="nt", tm=1024, tn=1024, tk=MEM_WIDTH, out_dtypes=[F32], name="cross_q_bwd")
    g_ckv = _matmul(memn, dkvc, mode="tn", tm=1024, tn=2 * MEM_WIDTH, tk=mems.shape[0], out_dtypes=[BF16], name="cross_kv_wgrad")
    dmemn = _matmul(dkvc, w_ckv, mode="nt", tm=256, tn=1024, tk=2 * MEM_WIDTH, out_dtypes=[F32], name="cross_kv_bwd")
    _, _, dg_mem = _rmsnorm_bwd(dmemn, mems, small["g_mem"], None, "norm_mem_bwd")
    dx1, dx1b, dg_cross = _rmsnorm_bwd(dhc, x1, small["g_cross"], dx2, "norm_cross_bwd")

    dmix = _matmul(dx1b, w_o, mode="nt", tm=512, tn=1024, tk=d, out_dtypes=[F32], name="mix_out_bwd")
    g_o = _matmul(mix, dx1b, mode="tn", tm=1024, tn=1024, tk=s, out_dtypes=[BF16], name="mix_out_wgrad")
    dya, dyc, dz, db_a, db_c = _gate_bwd(z, small["b_gate"], ya, yc, dmix)
    d_attn = _matmul(dya, w4["w_attn_out"], mode="nt", tm=1024, tn=ATTN_WIDTH, tk=d, out_dtypes=[BF16], name="attn_out_bwd",
                     b_blocks=N_CHIPS)
    g_ao = _matmul(attn, dya, mode="tn", tm=ATTN_WIDTH, tn=c_d, tk=s, out_dtypes=[BF16], name="attn_out_wgrad", out_blocks=N_CHIPS)
    d_co = _matmul(dyc, w4["w_conv_out"], mode="nt", tm=1024, tn=CONV_WIDTH, tk=d, out_dtypes=[F32], name="conv_out_bwd",
                   b_blocks=N_CHIPS)
    g_cvo = _matmul(co, dyc, mode="tn", tm=CONV_WIDTH, tn=c_d, tk=s, out_dtypes=[BF16], name="conv_out_wgrad", out_blocks=N_CHIPS)
    tok = reduce.step("ffn", g_cvo)
    tok = reduce.add("mid", {"w_co": g_co, "w_cq": g_cq, "w_ckv": g_ckv, "w_o": g_o, "w_attn_out": g_ao, "w_conv_out": g_cvo}, tok)
    dz, d_conv_w = _conv_bwd(z, conv_w, d_co, dz)
    dq_rot, dk_rot, dv, dsink = _swa_bwd(q_rot, k_rot, v_b, d_attn, small["sink"])
    tok = reduce.step("mid", dq_rot)
    dz = _rope_bwd(dq_rot, dk_rot, dv, cos_t, sin_t, dz)
    in_shape = dict(row_sharded=False, tm=1024, tn=c_in)
    t_in = _wgrad_half(h, dz, core, theirs=True, name="in_proj_wgrad_theirs", after=tok, **in_shape)
    tok = reduce.send("in", {"w_in": t_in}, dk_rot)
    tok = reduce.step("ffn", tok)
    tok = reduce.step("mid", tok)
    got = reduce.received("in", tok)
    p_in = _wgrad_half(h, dz, core, theirs=False, name="in_proj_wgrad_mine", add=got["w_in"], **in_shape)
    tok = reduce.add_parts("in", {"w_in": p_in})
    dh = _matmul(dz, w4["w_in"], mode="nt", tm=512, tn=512, tk=N_CHIPS * c_in, out_dtypes=[F32], name="in_proj_bwd", b_blocks=N_CHIPS,
                 after=tok)
    tok = reduce.step("mid", dh)
    grad_x, _, dg_mix = _rmsnorm_bwd(dh, xs, small["g_mix"], dx1, "norm_mix_bwd")

    small_grads = {
        "g_mix": dg_mix, "sink": dsink[:, 0], "b_gate": jnp.concatenate([db_a, db_c], axis=1), "g_cross": dg_cross,
        "g_mem": dg_mem, "g_ffn": dg_ffn, "g_final": dg_final, "conv_w": d_conv_w,
    }
    return sq, grad_x, small_grads


def _pair_sum(g4, ra, core, name):
    nb, rs, cs = g4.shape
    rh = rs // 2
    tr = _row_tile(rh, 256)
    per = rh // tr

    def body(c_ref, g_ref, r_ref, o_ref):
        o_ref[...] = (g_ref[...].astype(F32) + r_ref[...].astype(F32)).astype(BF16)

    plain = pl.BlockSpec((None, tr, cs), lambda j, i, c: (j, i, 0))
    return pl.pallas_call(
        body, name=name,
        grid_spec=pltpu.PrefetchScalarGridSpec(
            num_scalar_prefetch=1, grid=(nb, per),
            in_specs=[pl.BlockSpec((None, tr, cs), lambda j, i, c: (j, c[0] * per + i, 0)), plain],
            out_specs=plain),
        out_shape=jax.ShapeDtypeStruct((nb, rh, cs), BF16),
        compiler_params=_params(2),
    )(core, g4, ra)


def _quad_sum(parts, rc, place, name):
    _, rh, cs = parts.shape
    tr = _row_tile(rh, 256)
    per = rh // tr

    def body(p_ref, own_ref, r_ref, o_ref):
        acc = own_ref[...].astype(F32)
        for j in range(rc.shape[0]):
            acc = acc + r_ref[j].astype(F32)
        o_ref[...] = acc

    return pl.pallas_call(
        body, name=name,
        grid_spec=pltpu.PrefetchScalarGridSpec(
            num_scalar_prefetch=1, grid=(per,),
            in_specs=[pl.BlockSpec((None, tr, cs), lambda i, p: (p[0], i, 0)),
                      pl.BlockSpec((rc.shape[0], tr, cs), lambda i, p: (0, i, 0))],
            out_specs=pl.BlockSpec((tr, cs), lambda i, p: (p[1] * per + i, 0))),
        out_shape=jax.ShapeDtypeStruct((2 * rh, cs), F32),
        compiler_params=_params(1),
    )(place, parts, rc)


def _adamw_update(w, g, m, v):
    nm = ADAM_B1 * m + (1.0 - ADAM_B1) * g
    nv = ADAM_B2 * v + (1.0 - ADAM_B2) * (g * g)
    m_hat = nm / ADAM_C1
    v_hat = nv / ADAM_C2
    return -ADAM_LR * (m_hat / (jnp.sqrt(v_hat) + ADAM_EPS) + ADAM_WD * w), nm, nv


def _adamw_own_half(w, m, v, parts, rc, place, name, after=None):
    rows, cols = w.shape
    rh = rows // 2
    tr = _row_tile(rh, 256)
    per = rh // tr

    def body(p_ref, w_ref, m_ref, v_ref, own_ref, r_ref, *rest):
        gx_ref, g_ref, d_ref, nm_ref, nv_ref = rest[-5:]
        g = own_ref[...].astype(F32)
        for j in range(rc.shape[0]):
            g = g + r_ref[j].astype(F32)
        gx_ref[...] = g
        g_ref[...] = g
        d_ref[...], nm_ref[...], nv_ref[...] = _adamw_update(w_ref[...], g, m_ref[...], v_ref[...])

    mine = pl.BlockSpec((tr, cols), lambda i, p: (p[1] * per + i, 0))
    shape = jax.ShapeDtypeStruct((rows, cols), F32)
    return pl.pallas_call(
        body, name=name,
        grid_spec=pltpu.PrefetchScalarGridSpec(
            num_scalar_prefetch=1, grid=(per,),
            in_specs=[mine, mine, mine, pl.BlockSpec((None, tr, cols), lambda i, p: (p[0], i, 0)),
                      pl.BlockSpec((rc.shape[0], tr, cols), lambda i, p: (0, i, 0))] + ([] if after is None else [ANY]),
            out_specs=[mine] * 5),
        out_shape=[shape] * 5,
        compiler_params=_params(1),
    )(place, w, m, v, parts, rc, *([] if after is None else [after]))


def _adamw_other_half(w, m, v, g_exchanged, g, delta, new_m, new_v, place, name, after=None):
    rows, cols = w.shape
    rh = rows // 2
    tr = _row_tile(rh, 256)
    per = rh // tr

    def body(p_ref, w_ref, m_ref, v_ref, gx_ref, *rest):
        g_ref, d_ref, nm_ref, nv_ref = rest[-4:]
        gv = gx_ref[...]
        g_ref[...] = gv
        d_ref[...], nm_ref[...], nv_ref[...] = _adamw_update(w_ref[...], gv, m_ref[...], v_ref[...])

    other = pl.BlockSpec((tr, cols), lambda i, p: ((1 - p[1]) * per + i, 0))
    shape = jax.ShapeDtypeStruct((rows, cols), F32)
    n_after = 0 if after is None else 1
    return pl.pallas_call(
        body, name=name,
        grid_spec=pltpu.PrefetchScalarGridSpec(
            num_scalar_prefetch=1, grid=(per,),
            in_specs=[other] * 4 + [ANY] * (4 + n_after),
            out_specs=[other] * 4),
        out_shape=[shape] * 4,
        input_output_aliases={5: 0, 6: 1, 7: 2, 8: 3},
        compiler_params=_params(1),
    )(place, w, m, v, g_exchanged, g, delta, new_m, new_v, *([] if after is None else [after]))


def _cast_to_slot(w, place, dtype, name, after=None):
    rows, cols = w.shape
    tr = _row_tile(rows, 1024)

    def body(p_ref, w_ref, *rest):
        o_ref = rest[-1]
        o_ref[...] = w_ref[...].astype(dtype)

    return pl.pallas_call(
        body, name=name,
        grid_spec=pltpu.PrefetchScalarGridSpec(
            num_scalar_prefetch=1, grid=(rows // tr,),
            in_specs=[pl.BlockSpec((tr, cols), lambda i, p: (i, 0))] + ([] if after is None else [ANY]),
            out_specs=pl.BlockSpec((None, tr, cols), lambda i, p: (p[0], i, 0))),
        out_shape=jax.ShapeDtypeStruct((N_CHIPS, rows, cols), dtype),
        compiler_params=_params(1),
    )(place, w, *([] if after is None else [after]))


def _adamw(w, g, m, v, name, after=None):
    rows, cols = w.shape
    tr = _row_tile(rows, 256)

    def body(w_ref, g_ref, m_ref, v_ref, *rest):
        go_ref, d_ref, nm_ref, nv_ref = rest[-4:]
        gv = g_ref[...]
        go_ref[...] = gv
        d_ref[...], nm_ref[...], nv_ref[...] = _adamw_update(w_ref[...], gv, m_ref[...], v_ref[...])

    tile = pl.BlockSpec((tr, cols), lambda i: (i, 0))
    shape = jax.ShapeDtypeStruct((rows, cols), F32)
    return pl.pallas_call(
        body, name=name, grid=(rows // tr,),
        in_specs=[tile] * 4 + ([] if after is None else [ANY]), out_specs=[tile] * 4, out_shape=[shape] * 4,
        compiler_params=_params(1),
    )(w, g, m, v, *([] if after is None else [after]))


def _mesh_pos():
    return lax.axis_index("x"), lax.axis_index("y"), lax.axis_index("c")


def _other_chips(x, y):
    return [(1 - x, y), (x, 1 - y), (1 - x, 1 - y)]


def _half_rows(ref, which):
    rh = ref.shape[-2] // 2
    return ref.at[pl.ds(which * rh, rh), :]


def _remote(src, dst, send_sems, recv_sems, sem, to):
    return pltpu.make_async_remote_copy(src_ref=src, dst_ref=dst, send_sem=send_sems.at[sem], recv_sem=recv_sems.at[sem],
                                        device_id=to, device_id_type=MESH)


HBM = pl.BlockSpec(memory_space=pltpu.HBM)
SEM = pl.BlockSpec(memory_space=pltpu.SEMAPHORE)
DATAFLOW_EFFECT = pltpu.SideEffectType.DATAFLOW_SIDE_EFFECTING


def _in_hbm(arrays):
    return [pltpu.with_memory_space_constraint(a, pltpu.HBM) for a in arrays]


def _hbm_like(arrays):
    return [pltpu.HBM(a.shape, a.dtype) for a in arrays]


GATHER_COPIES_PER_ARRAY = {"direct": 2, "forward": 2, "pass_near": 2, "pass_far": 1}


def _gather_copies(kind, refs, x, y, c):
    me, near_x, near_y, far = 2 * x + y, 2 * (1 - x) + y, 2 * x + (1 - y), 2 * (1 - x) + (1 - y)
    to_x, to_y, sibling = (1 - x, y, c), (x, 1 - y, c), (x, y, 1 - c)
    out = []
    for ref in refs:
        rh = ref.shape[1] // 2
        rq = rh // 2

        def half(chip, ref=ref, rh=rh):
            return ref.at[chip, pl.ds(c * rh, rh), :]

        def quarter(chip, q, ref=ref, rh=rh, rq=rq):
            return ref.at[chip, pl.ds(c * rh + q * rq, rq), :]

        if kind == "direct":
            out += [(half(me), half(me), to_x), (half(me), half(me), to_y)]
        elif kind == "forward":
            out += [(quarter(near_x, 0), quarter(near_x, 0), to_y), (quarter(near_y, 1), quarter(near_y, 1), to_x)]
        elif kind == "pass_near":
            out += [(half(near_x), half(near_x), sibling), (half(near_y), half(near_y), sibling)]
        else:
            assert kind == "pass_far"
            out += [(half(far), half(far), sibling)]
    return out


def _gather_step(name, bufs, waits, starts, after):
    nb, nw, ns = len(bufs), len(waits), len(starts)
    after = [] if after is None else list(after) if isinstance(after, (list, tuple)) else [after]
    n_after = len(after)

    def body(*refs):
        ins = refs[:nb]
        wait_sems = refs[nb:nb + 2 * nw]
        start_sems = refs[nb + 2 * nw + n_after:nb + 2 * nw + n_after + 2 * ns]
        token = refs[-1]
        x, y, c = _mesh_pos()
        for j, (kind, idxs, _, _) in enumerate(waits):
            for i, (s_ref, d_ref, to) in enumerate(_gather_copies(kind, [ins[t] for t in idxs], x, y, c)):
                came = _remote(s_ref, d_ref, wait_sems[2 * j], wait_sems[2 * j + 1], i, to)
                came.wait_recv()
                came.wait_send()
        for j, (kind, idxs) in enumerate(starts):
            for i, (s_ref, d_ref, to) in enumerate(_gather_copies(kind, [ins[t] for t in idxs], x, y, c)):
                _remote(s_ref, d_ref, start_sems[2 * j], start_sems[2 * j + 1], i, to).start()
        token[...] = jnp.zeros_like(token)

    sems = []
    for kind, idxs in starts:
        sems += [pltpu.SemaphoreType.DMA((GATHER_COPIES_PER_ARRAY[kind] * len(idxs),))] * 2
    operands = _in_hbm(bufs) + [sem for w in waits for sem in w[2:]] + after
    outs = pl.pallas_call(
        body, name=name,
        in_specs=[HBM] * nb + [SEM] * (2 * nw) + [ANY] * n_after,
        out_specs=[SEM] * (2 * ns) + [HBM] * nb + [pl.BlockSpec(memory_space=pltpu.VMEM)],
        out_shape=sems + _hbm_like(bufs) + [jax.ShapeDtypeStruct((8, 128), F32)],
        input_output_aliases={i: 2 * ns + i for i in range(nb)},
        compiler_params=pltpu.CompilerParams(has_side_effects=DATAFLOW_EFFECT),
    )(*operands)
    return outs[2 * ns:2 * ns + nb], [(outs[2 * j], outs[2 * j + 1]) for j in range(ns)], outs[-1]


class _Gather:
    def __init__(self, groups):
        self.groups = groups
        self.bufs = {}
        self.in_flight = {}

    def put(self, slotted):
        self.bufs.update(slotted)

    def step(self, name, waits, starts, after=None):
        names = []
        for _, group in list(waits) + list(starts):
            names += [n for n in self.groups[group] if n not in names]
        index = {n: i for i, n in enumerate(names)}

        def members(group):
            return [index[n] for n in self.groups[group]]

        wait_args = [(kind, members(group)) + self.in_flight.pop((kind, group)) for kind, group in waits]
        start_args = [(kind, members(group)) for kind, group in starts]
        bufs, sems, token = _gather_step(name, [self.bufs[n] for n in names], wait_args, start_args, after)
        self.bufs.update(zip(names, bufs))
        for (kind, group), pair in zip(starts, sems):
            self.in_flight[(kind, group)] = pair
        return token

    def arrays(self, group):
        return {n: self.bufs[n] for n in self.groups[group]}


def _sibling_halves_copies(srcs, dsts, x, y, c):
    out = []
    for s_ref, d_ref in zip(srcs, dsts, strict=True):
        rh = s_ref.shape[1] // 2
        out.append((s_ref.at[:, pl.ds((1 - c) * rh, rh), :], d_ref, (x, y, 1 - c)))
    return out


def _to_sibling_copies(srcs, dsts, x, y, c):
    return [(s_ref, d_ref, (x, y, 1 - c)) for s_ref, d_ref in zip(srcs, dsts, strict=True)]


def _chip_copies(srcs, dsts, x, y, c):
    out = []
    for s_ref, d_ref in zip(srcs, dsts, strict=True):
        for k, (px, py) in enumerate(_other_chips(x, y)):
            out.append((s_ref.at[2 * px + py], d_ref.at[k], (px, py, c)))
    return out


def _join_copies(srcs, dsts, x, y, c):
    out = []
    for s_ref in srcs:
        mine = _half_rows(s_ref, c)
        out.append((mine, mine, (x, y, 1 - c)))
    return out


def _exchange_start(copies_fn, n_copies, srcs, fresh, after, name):
    ns, nb = len(srcs), len(srcs) + len(fresh)

    def body(*refs):
        bufs, send, recv, token = refs[:nb], refs[nb + 1], refs[nb + 2], refs[-1]
        x, y, c = _mesh_pos()
        for i, (s_ref, d_ref, to) in enumerate(copies_fn(bufs[:ns], bufs[ns:] if fresh else bufs[:ns], x, y, c)):
            _remote(s_ref, d_ref, send, recv, i, to).start()
        token[...] = jnp.zeros_like(token)

    sems = [pltpu.SemaphoreType.DMA((n_copies,))] * 2
    outs = pl.pallas_call(
        body, name=name,
        in_specs=[HBM] * nb + [ANY], out_specs=[SEM, SEM] + [HBM] * nb + [pl.BlockSpec(memory_space=pltpu.VMEM)],
        out_shape=sems + _hbm_like(list(srcs) + list(fresh)) + [jax.ShapeDtypeStruct((8, 128), F32)],
        input_output_aliases={i: 2 + i for i in range(nb)},
        compiler_params=pltpu.CompilerParams(has_side_effects=DATAFLOW_EFFECT),
    )(*_in_hbm(list(srcs) + list(fresh)), after)
    return outs[0], outs[1], outs[2:2 + ns], outs[2 + ns:2 + nb], outs[-1]


def _exchange_done(copies_fn, srcs, fresh, send, recv, after, name):
    ns, nb = len(srcs), len(srcs) + len(fresh)

    def body(*refs):
        bufs, send_in, recv_in = refs[:nb], refs[nb], refs[nb + 1]
        x, y, c = _mesh_pos()
        for i, (s_ref, d_ref, to) in enumerate(copies_fn(bufs[:ns], bufs[ns:] if fresh else bufs[:ns], x, y, c)):
            came = _remote(s_ref, d_ref, send_in, recv_in, i, to)
            came.wait_send()
            came.wait_recv()

    outs = pl.pallas_call(
        body, name=name,
        in_specs=[HBM] * nb + [SEM, SEM, ANY], out_specs=[HBM] * nb,
        out_shape=_hbm_like(list(srcs) + list(fresh)),
        input_output_aliases={i: i for i in range(nb)},
        compiler_params=pltpu.CompilerParams(has_side_effects=DATAFLOW_EFFECT),
    )(*_in_hbm(list(srcs) + list(fresh)), send, recv, after)
    return outs[:ns], outs[ns:]


class _Reduce:
    def __init__(self, place, core, shards, mom_m, mom_v):
        self.place, self.core = place, core
        self.shards, self.mom_m, self.mom_v = shards, mom_m, mom_v
        self.state = {}
        self.results = {}

    def add(self, group, grads, after):
        names = list(grads)
        g4s = [g.reshape((N_CHIPS, -1, g.shape[-1])) if g.ndim == 2 else g for g in grads.values()]
        fresh = [lax.empty((N_CHIPS, g.shape[1] // 2, g.shape[2]), BF16) for g in g4s]
        send, recv, g4s, fresh, token = _exchange_start(_sibling_halves_copies, len(names), g4s, fresh, after,
                                                        "pair_start_" + group)
        self.state[group] = (0, names, send, recv, g4s, fresh)
        return token

    def send(self, group, theirs, after):
        names, srcs = list(theirs), list(theirs.values())
        fresh = [lax.empty(s.shape, BF16) for s in srcs]
        send, recv, srcs, fresh, token = _exchange_start(_to_sibling_copies, len(names), srcs, fresh, after, "pair_start_" + group)
        self.state[group] = ("sent", names, send, recv, srcs, fresh)
        return token

    def received(self, group, after):
        stage, names, send, recv, srcs, fresh = self.state.pop(group)
        assert stage == "sent"
        _, got = _exchange_done(_to_sibling_copies, srcs, fresh, send, recv, after, "pair_done_" + group)
        return dict(zip(names, got))

    def add_parts(self, group, parts):
        names, srcs = list(parts), list(parts.values())
        fresh = [lax.empty((N_CHIPS - 1,) + p.shape[1:], BF16) for p in srcs]
        send, recv, srcs, fresh, token = _exchange_start(_chip_copies, 3 * len(names), srcs, fresh, self.core, "chips_start_" + group)
        self.state[group] = (1, names, send, recv, srcs, fresh)
        return token

    def step(self, group, after):
        stage, names, send, recv, srcs, fresh = self.state[group]
        if stage == 0:
            g4s, ras = _exchange_done(_sibling_halves_copies, srcs, fresh, send, recv, after, "pair_done_" + group)
            parts = [_pair_sum(g, r, self.core, "pair_sum_" + n) for g, r, n in zip(g4s, ras, names)]
            fresh = [lax.empty((N_CHIPS - 1,) + p.shape[1:], BF16) for p in parts]
            send, recv, parts, fresh, token = _exchange_start(_chip_copies, 3 * len(names), parts, fresh, self.core,
                                                              "chips_start_" + group)
            self.state[group] = (1, names, send, recv, parts, fresh)
            return token
        if stage == 1:
            parts, rcs = _exchange_done(_chip_copies, srcs, fresh, send, recv, after, "chips_done_" + group)
            token = None
            for n, p, r in zip(names, parts, rcs):
                self.results[n] = _adamw_own_half(self.shards[n], self.mom_m[n], self.mom_v[n], p, r, self.place,
                                                  "adamw_own_" + n, after=token)
                token = self.results[n][2]
            wholes = [self.results[n][0] for n in names]
            send, recv, wholes, _, token = _exchange_start(_join_copies, len(names), wholes, [], token, "join_start_" + group)
            self.state[group] = (2, names, send, recv, wholes, [])
            return token
        assert stage == 2
        wholes, _ = _exchange_done(_join_copies, srcs, [], send, recv, after, "join_done_" + group)
        token = None
        for n, exchanged in zip(names, wholes):
            _, g, d, nm, nv = self.results[n]
            self.results[n] = _adamw_other_half(self.shards[n], self.mom_m[n], self.mom_v[n], exchanged, g, d, nm, nv,
                                                self.place, "adamw_other_" + n, after=token)
            token = self.results[n][1]
        del self.state[group]
        return token


N_DEV = 8


def _all_reduce_small(v):
    def body(v_ref, o_ref, slots, send_sems, recv_sems):
        x, y, c = _mesh_pos()
        me = 4 * x + 2 * y + c
        slots[me] = v_ref[...]
        peers = []
        for r in range(1, N_DEV):
            fx, fy, fc = (r >> 2) & 1, (r >> 1) & 1, r & 1
            peers.append((x + fx - 2 * x * fx, y + fy - 2 * y * fy, c + fc - 2 * c * fc))
        sends = []
        for r, peer in enumerate(peers):
            cp = _remote(v_ref, slots.at[me], send_sems, recv_sems, r, peer)
            cp.start()
            sends.append(cp)
        for r, (px, py, pc) in enumerate(peers):
            landed = slots.at[4 * px + 2 * py + pc]
            _remote(landed, landed, send_sems, recv_sems, r, (px, py, pc)).wait_recv()
        for cp in sends:
            cp.wait_send()
        acc = slots[0]
        for i in range(1, N_DEV):
            acc = acc + slots[i]
        o_ref[...] = acc

    vm = pl.BlockSpec(memory_space=pltpu.VMEM)
    return pl.pallas_call(
        body, name="small_grads_all_reduce",
        in_specs=[vm], out_specs=vm,
        out_shape=jax.ShapeDtypeStruct(v.shape, v.dtype),
        scratch_shapes=[pltpu.VMEM((N_DEV,) + v.shape, v.dtype), pltpu.SemaphoreType.DMA((N_DEV - 1,)),
                        pltpu.SemaphoreType.DMA((N_DEV - 1,))],
    )(v)


MATRICES = ("w_in", "w_attn_out", "w_conv_out", "w_o", "w_cq", "w_ckv", "w_co", "w_gate", "w_up", "w_down")
VECTORS = ("g_mix", "b_gate", "g_cross", "g_mem", "g_ffn", "g_final", "conv_w", "sink")
WEIGHT_ORDER = ("g_mix", "w_in", "sink", "conv_w", "b_gate", "w_attn_out", "w_conv_out", "w_o", "g_cross", "g_mem", "w_cq",
                "w_ckv", "w_co", "g_ffn", "w_gate", "w_up", "w_down", "g_final")
CONV_PAD_ROWS = 32
SMALL_ROWS = 8


def _pack(pieces):
    flat = jnp.concatenate([p.reshape(-1) for p in pieces])
    lane_group = SMALL_ROWS * 128
    total = -(-flat.shape[0] // lane_group) * lane_group
    flat = jnp.pad(flat, (0, total - flat.shape[0]))
    return flat.reshape(SMALL_ROWS, total // SMALL_ROWS), [p.size for p in pieces]


def _unpack(packed, pieces):
    flat = packed.reshape(-1)
    out, off = [], 0
    for p in pieces:
        out.append(flat[off:off + p.size].reshape(p.shape))
        off += p.size
    return out


def kernel(x, mem, g_mix, w_in, sink, conv_w, b_gate, w_attn_out, w_conv_out, w_o, g_cross, g_mem, w_cq, w_ckv, w_co, g_ffn, w_gate, w_up, w_down, g_final, loss_target, m_g_mix, m_w_in, m_sink, m_conv_w, m_b_gate, m_w_attn_out, m_w_conv_out, m_w_o, m_g_cross, m_g_mem, m_w_cq, m_w_ckv, m_w_co, m_g_ffn, m_w_gate, m_w_up, m_w_down, m_g_final, v_g_mix, v_w_in, v_sink, v_conv_w, v_b_gate, v_w_attn_out, v_w_conv_out, v_w_o, v_g_cross, v_g_mem, v_w_cq, v_w_ckv, v_w_co, v_g_ffn, v_w_gate, v_w_up, v_w_down, v_g_final):
    given = dict(g_mix=g_mix, w_in=w_in, sink=sink, conv_w=conv_w, b_gate=b_gate, w_attn_out=w_attn_out, w_conv_out=w_conv_out,
                 w_o=w_o, g_cross=g_cross, g_mem=g_mem, w_cq=w_cq, w_ckv=w_ckv, w_co=w_co, g_ffn=g_ffn, w_gate=w_gate, w_up=w_up,
                 w_down=w_down, g_final=g_final)
    mom_m = dict(g_mix=m_g_mix, w_in=m_w_in, sink=m_sink, conv_w=m_conv_w, b_gate=m_b_gate, w_attn_out=m_w_attn_out,
                 w_conv_out=m_w_conv_out, w_o=m_w_o, g_cross=m_g_cross, g_mem=m_g_mem, w_cq=m_w_cq, w_ckv=m_w_ckv, w_co=m_w_co,
                 g_ffn=m_g_ffn, w_gate=m_w_gate, w_up=m_w_up, w_down=m_w_down, g_final=m_g_final)
    mom_v = dict(g_mix=v_g_mix, w_in=v_w_in, sink=v_sink, conv_w=v_conv_w, b_gate=v_b_gate, w_attn_out=v_w_attn_out,
                 w_conv_out=v_w_conv_out, w_o=v_w_o, g_cross=v_g_cross, g_mem=v_g_mem, w_cq=v_w_cq, w_ckv=v_w_ckv, w_co=v_w_co,
                 g_ffn=v_g_ffn, w_gate=v_w_gate, w_up=v_w_up, w_down=v_w_down, g_final=v_g_final)
    xs, mems, target = x[0], mem[0], loss_target[0]
    d_model = xs.shape[1]
    chip = 2 * lax.axis_index("x") + lax.axis_index("y")
    core = jnp.reshape(lax.axis_index("c"), (1,)).astype(jnp.int32)
    place = jnp.stack([chip, lax.axis_index("c")]).astype(jnp.int32)

    shards = {n: given[n][0] for n in MATRICES}
    conv_cols = conv_w.shape[2]
    conv_pad = jnp.pad(conv_w[0], ((0, CONV_PAD_ROWS - conv_w.shape[1]), (0, 0)))
    fetch = _Gather(GATHER_GROUPS)
    first = {"w_in": _cast_to_slot(shards["w_in"], place, BF16, "to_slot_w_in"),
             "conv_w": _cast_to_slot(conv_pad, place, F32, "to_slot_conv_w")}
    fetch.put(first)
    tok = fetch.step("gather_start", [], [("direct", "in")])
    fetch.put({n: _cast_to_slot(shards[n], place, BF16, "to_slot_" + n, after=tok) for n in MATRICES if n != "w_in"})
    small = {n: given[n] for n in ("g_mix", "b_gate", "g_cross", "g_mem", "g_ffn")}
    small["g_final"] = g_final[None]
    small["sink"] = sink[0]

    reduce = _Reduce(place, core, shards, {n: mom_m[n][0] for n in MATRICES}, {n: mom_v[n][0] for n in MATRICES})
    sq, grad_x, small_grads = _local_step(xs, mems, target, small, fetch, reduce)

    loss_part = 0.5 * sq[0:1, 0:1] / d_model
    pieces = [small_grads[n] for n in VECTORS] + [loss_part]
    packed, _ = _pack(pieces)
    summed = _unpack(_all_reduce_small(packed), pieces)
    loss = summed[-1][0, 0]
    small_sum = dict(zip(VECTORS, summed[:-1]))
    small_sum["conv_w"] = lax.dynamic_slice_in_dim(small_sum["conv_w"], chip * conv_cols, conv_cols, axis=1)

    grad_out, delta, new_m, new_v = {}, {}, {}, {}
    like = [given[n] for n in VECTORS]
    pw, _ = _pack(like)
    pg, _ = _pack([small_sum[n] for n in VECTORS])
    pm, _ = _pack([mom_m[n] for n in VECTORS])
    pv, _ = _pack([mom_v[n] for n in VECTORS])
    tok = reduce.step("in", pg)
    _, pd, pnm, pnv = _adamw(pw, pg, pm, pv, "adamw_small", after=tok)
    for n, g, d, nm, nv in zip(VECTORS, [small_sum[n] for n in VECTORS], _unpack(pd, like), _unpack(pnm, like), _unpack(pnv, like)):
        grad_out[n] = g.reshape(given[n].shape)
        delta[n], new_m[n], new_v[n] = d, nm, nv
    reduce.step("in", pd)
    for n in MATRICES:
        g, d, nm, nv = reduce.results[n]
        grad_out[n], delta[n], new_m[n], new_v[n] = g[None], d[None], nm[None], nv[None]

    return (loss, grad_x[None], *[grad_out[n] for n in WEIGHT_ORDER], *[delta[n] for n in WEIGHT_ORDER],
            *[new_m[n] for n in WEIGHT_ORDER], *[new_v[n] for n in WEIGHT_ORDER])
```

```python
import functools

import jax
import jax.numpy as jnp
from jax import lax
from jax.experimental import pallas as pl
from jax.experimental.pallas import tpu as pltpu

F32 = jnp.float32
BF16 = jnp.bfloat16
MESH = pl.DeviceIdType.MESH
ANY = pl.BlockSpec(memory_space=pl.ANY)

VMEM_LIMIT_BYTES = 56 * 1024 * 1024

N_CHIPS = 4
HEAD_DIM = 128
N_Q_HEADS = 8
N_KV_HEADS = 2
Q_GROUP = N_Q_HEADS // N_KV_HEADS
ATTN_WIDTH = N_Q_HEADS * HEAD_DIM
KV_WIDTH = N_KV_HEADS * HEAD_DIM
WINDOW = 128
BLOCK = 128
BAND = 3 * BLOCK
ROPE_THETA = 10000.0
CONV_WIDTH = 1024
MEM_HEADS = 4
MEM_WIDTH = MEM_HEADS * HEAD_DIM
RMS_EPS = 1e-6
NEG_INF = -1e30
ATTN_SCALE = HEAD_DIM ** -0.5

Q_OFF, K_OFF, V_OFF, CU_OFF, CB_OFF, CC_OFF, GL_OFF = 0, 1024, 1280, 1536, 2560, 3584, 4608

ADAM_LR = 0.001
ADAM_B1 = 0.9
ADAM_B2 = 0.999
ADAM_EPS = 1e-08
ADAM_WD = 0.01
ADAM_STEP = 10
ADAM_C1 = 1.0 - ADAM_B1 ** ADAM_STEP
ADAM_C2 = 1.0 - ADAM_B2 ** ADAM_STEP


def _params(n_grid_axes):
    return pltpu.CompilerParams(dimension_semantics=("arbitrary",) * n_grid_axes, vmem_limit_bytes=VMEM_LIMIT_BYTES)


BF16_SUBLANES = 16


def _row_tile(rows, want):
    if rows <= want:
        return rows
    for t in range(want, 0, -BF16_SUBLANES):
        if rows % t == 0:
            return t
    return rows


def _matmul(a, b, *, mode, tm, tn, tk, out_dtypes, name, extras=(), epilogue=None, b_blocks=1, out_blocks=1, after=None,
            a_norm_gain=None):
    if mode == "tn":
        kdim, m = a.shape
    else:
        m, kdim = a.shape
    if b_blocks > 1:
        nb, brows, bcols = b.shape
        assert nb == b_blocks
        if mode == "nn":
            n = bcols * nb
            assert brows == kdim
        else:
            assert mode == "nt" and bcols * nb == kdim
            n = brows
    else:
        n = b.shape[0] if mode == "nt" else b.shape[1]
    tm, tn = min(tm, m), min(tn, n)
    assert m % tm == 0 and n % tn == 0 and tk == kdim, (name, m, n, kdim, tm, tn, tk)
    n_extra, n_out = len(extras), len(out_dtypes)
    n_after = 0 if after is None else 1
    normed = a_norm_gain is not None
    assert not normed or mode != "tn"

    if mode == "tn":
        a_spec = pl.BlockSpec((tk, tm), lambda j, i, k: (k, i))
        dims = (((0,), (0,)), ((), ()))
    else:
        a_spec = pl.BlockSpec((tm, tk), lambda j, i, k: (i, k))
        dims = (((1,), (0,)), ((), ())) if mode == "nn" else (((1,), (1,)), ((), ()))

    if b_blocks > 1 and mode == "nn":
        per = b.shape[2] // tn
        assert b.shape[2] % tn == 0
        b_spec = pl.BlockSpec((None, tk, tn), lambda j, i, k: (j // per, k, j % per))
    elif b_blocks > 1:
        b_spec = pl.BlockSpec((b_blocks, tn, b.shape[2]), lambda j, i, k: (0, j, 0))
    elif mode == "nt":
        b_spec = pl.BlockSpec((tn, tk), lambda j, i, k: (j, k))
    else:
        b_spec = pl.BlockSpec((tk, tn), lambda j, i, k: (k, j))

    tile_spec = pl.BlockSpec((tm, tn), lambda j, i, k: (i, j))
    if out_blocks > 1:
        ncols = n // out_blocks
        assert ncols % tn == 0
        oper = ncols // tn
        out_spec = pl.BlockSpec((None, tm, tn), lambda j, i, k: (j // oper, i, j % oper))
        out_shape = [jax.ShapeDtypeStruct((out_blocks, m, ncols), dt) for dt in out_dtypes]
    else:
        out_spec = tile_spec
        out_shape = [jax.ShapeDtypeStruct((m, n), dt) for dt in out_dtypes]

    def body(a_ref, b_ref, *rest):
        extra_refs = rest[:n_extra]
        out_refs = rest[n_extra + n_after + normed:n_extra + n_after + normed + n_out]
        if normed:
            xv = a_ref[...]
            a_ref = rest[-1]
            a_ref[...] = (xv * _rstd(xv) * rest[n_extra + n_after][...]).astype(BF16)
        if mode == "nt" and b_blocks > 1:
            cs = b.shape[2]
            acc = None
            for jb in range(b_blocks):
                prod = lax.dot_general(a_ref[:, jb * cs:(jb + 1) * cs].astype(BF16), b_ref[jb].astype(BF16), dims,
                                       preferred_element_type=F32)
                acc = prod if acc is None else acc + prod
        else:
            acc = lax.dot_general(a_ref[...].astype(BF16), b_ref[...].astype(BF16), dims, preferred_element_type=F32)
        tiles = (acc,) if epilogue is None else epilogue(acc, *[r[...] for r in extra_refs])
        for o_ref, t in zip(out_refs, tiles, strict=True):
            o_ref[...] = t.astype(o_ref.dtype)

    outs = pl.pallas_call(
        body,
        name=name,
        grid=(n // tn, m // tm, 1),
        in_specs=[a_spec, b_spec] + [tile_spec] * n_extra + [ANY] * n_after
        + ([pl.BlockSpec((1, kdim), lambda j, i, k: (0, 0))] if normed else []),
        out_specs=[out_spec] * n_out + ([a_spec] if normed else []),
        out_shape=out_shape + ([jax.ShapeDtypeStruct((m, kdim), BF16)] if normed else []),
        compiler_params=_params(3),
    )(a, b, *extras, *([] if after is None else [after]), *([a_norm_gain] if normed else []))
    return outs[0] if len(outs) == 1 else outs


def _add_residual(acc, res):
    return (acc + res,)


def _matmul_column_blocks(a, b4, blocks, out, *, tm, name, after=None, norm_gain=None):
    m, kdim = a.shape
    nb, _, cols = b4.shape
    tm = min(tm, m)
    assert m % tm == 0
    n_blocks = blocks.shape[0]
    normed = norm_gain is not None
    assert not normed or (n_blocks == 1 and out is None)

    def body(j_ref, a_ref, b_ref, *rest):
        if normed:
            xv = a_ref[...]
            hv = (xv * _rstd(xv) * rest[0][...]).astype(BF16)
            rest[-1][...] = hv
            rest[-2][...] = jnp.dot(hv, b_ref[...], preferred_element_type=F32)
        else:
            rest[-1][...] = jnp.dot(a_ref[...], b_ref[...], preferred_element_type=F32)

    extra = ([] if out is None else [out]) + ([] if after is None else [after])
    z_spec = pl.BlockSpec((tm, cols), lambda j, i, blk: (i, blk[j]))
    z_shape = jax.ShapeDtypeStruct((m, nb * cols), F32)
    return pl.pallas_call(
        body, name=name,
        grid_spec=pltpu.PrefetchScalarGridSpec(
            num_scalar_prefetch=1, grid=(n_blocks, m // tm),
            in_specs=[pl.BlockSpec((tm, kdim), lambda j, i, blk: (i, 0)),
                      pl.BlockSpec((None, kdim, cols), lambda j, i, blk: (blk[j], 0, 0))]
            + ([pl.BlockSpec((1, kdim), lambda j, i, blk: (0, 0))] if normed else []) + [ANY] * len(extra),
            out_specs=[z_spec, pl.BlockSpec((tm, kdim), lambda j, i, blk: (i, 0))] if normed else z_spec),
        out_shape=[z_shape, jax.ShapeDtypeStruct((m, kdim), BF16)] if normed else z_shape,
        input_output_aliases={} if out is None else {3: 0},
        compiler_params=_params(2),
    )(blocks, a, b4, *([norm_gain] if normed else []), *extra)


def _wgrad_half(a, b, core, *, theirs, row_sharded, tm, tn, name, add=None, after=None):
    kdim, m = a.shape
    n = b.shape[1]
    rs, cs = (m // N_CHIPS, n) if row_sharded else (m, n // N_CHIPS)
    rh = rs // 2
    tm, tn = min(tm, rh), min(tn, cs)
    assert rh % tm == 0 and cs % tn == 0, (name, rh, cs, tm, tn)
    mh, per = rh // tm, cs // tn
    has_add = add is not None

    def half(c):
        return 1 - c[0] if theirs else c[0]

    if row_sharded:
        grid = (n // tn, N_CHIPS * mh)
        a_spec = pl.BlockSpec((kdim, tm), lambda j, r, c: (0, ((r // mh) * 2 + half(c)) * mh + r % mh))
        o_spec = pl.BlockSpec((None, tm, tn), lambda j, r, c: (r // mh, r % mh, j))
    else:
        grid = (n // tn, mh)
        a_spec = pl.BlockSpec((kdim, tm), lambda j, r, c: (0, half(c) * mh + r))
        o_spec = pl.BlockSpec((None, tm, tn), lambda j, r, c: (j // per, r, j % per))
    b_spec = pl.BlockSpec((kdim, tn), lambda j, r, c: (0, j))

    def body(c_ref, a_ref, b_ref, *rest):
        o_ref = rest[-1]
        acc = lax.dot_general(a_ref[...].astype(BF16), b_ref[...].astype(BF16), (((0,), (0,)), ((), ())),
                              preferred_element_type=F32)
        if has_add:
            acc = acc + rest[0][...].astype(F32)
        o_ref[...] = acc.astype(BF16)

    operands = [a, b] + ([add] if has_add else []) + ([] if after is None else [after])
    return pl.pallas_call(
        body, name=name,
        grid_spec=pltpu.PrefetchScalarGridSpec(
            num_scalar_prefetch=1, grid=grid,
            in_specs=[a_spec, b_spec] + ([o_spec] if has_add else []) + ([] if after is None else [ANY]),
            out_specs=o_spec),
        out_shape=jax.ShapeDtypeStruct((N_CHIPS, rh, cs), BF16),
        compiler_params=_params(2),
    )(core, *operands)


def _rstd(x):
    return lax.rsqrt(jnp.mean(x * x, axis=-1, keepdims=True) + RMS_EPS)


def _rmsnorm(x, g, name):
    s, d = x.shape
    tr = _row_tile(s, 256)

    def body(x_ref, g_ref, o_ref):
        xv = x_ref[...]
        o_ref[...] = (xv * _rstd(xv) * g_ref[...]).astype(BF16)

    return pl.pallas_call(
        body, name=name, grid=(s // tr,),
        in_specs=[pl.BlockSpec((tr, d), lambda i: (i, 0)), pl.BlockSpec((1, d), lambda i: (0, 0))],
        out_specs=pl.BlockSpec((tr, d), lambda i: (i, 0)),
        out_shape=jax.ShapeDtypeStruct((s, d), BF16),
        compiler_params=_params(1),
    )(x, g)


def _rmsnorm_bwd(dh, x, g, dres, name):
    s, d = x.shape
    tr = _row_tile(s, 256)
    has_res = dres is not None

    def body(*refs):
        if has_res:
            dh_ref, x_ref, g_ref, res_ref, dx_ref, dxb_ref, dg_ref = refs
        else:
            dh_ref, x_ref, g_ref, dx_ref, dxb_ref, dg_ref = refs
        xv = x_ref[...]
        dhv = dh_ref[...].astype(F32)
        r = _rstd(xv)
        xn = xv * r
        dhg = dhv * g_ref[...]
        dx = r * (dhg - xn * jnp.mean(dhg * xn, axis=-1, keepdims=True))
        if has_res:
            dx = dx + res_ref[...]
        dx_ref[...] = dx
        dxb_ref[...] = dx.astype(BF16)
        part = jnp.sum(dhv * xn, axis=0, keepdims=True)

        @pl.when(pl.program_id(0) == 0)
        def _():
            dg_ref[...] = part

        @pl.when(pl.program_id(0) > 0)
        def _():
            dg_ref[...] += part

    row = pl.BlockSpec((tr, d), lambda i: (i, 0))
    vec = pl.BlockSpec((1, d), lambda i: (0, 0))
    return pl.pallas_call(
        body, name=name, grid=(s // tr,),
        in_specs=[row, row, vec] + ([row] if has_res else []),
        out_specs=[row, row, vec],
        out_shape=[jax.ShapeDtypeStruct((s, d), F32), jax.ShapeDtypeStruct((s, d), BF16), jax.ShapeDtypeStruct((1, d), F32)],
        compiler_params=_params(1),
    )(*([dh, x, g] + ([dres] if has_res else [])))


def _loss_head(x3, g, target):
    s, d = x3.shape
    tr = _row_tile(s, 256)

    def body(x_ref, g_ref, t_ref, dx_ref, dxb_ref, sq_ref, dg_ref):
        xv = x_ref[...]
        gv = g_ref[...]
        r = _rstd(xv)
        xn = xv * r
        err = xn * gv - t_ref[...]
        dy = err * (1.0 / d)
        dyg = dy * gv
        dx = r * (dyg - xn * jnp.mean(dyg * xn, axis=-1, keepdims=True))
        dx_ref[...] = dx
        dxb_ref[...] = dx.astype(BF16)
        sq = jnp.sum(jnp.sum(err * err, axis=1, keepdims=True), axis=0, keepdims=True)
        sq = jnp.broadcast_to(sq, (1, 128))
        part = jnp.sum(dy * xn, axis=0, keepdims=True)

        @pl.when(pl.program_id(0) == 0)
        def _():
            sq_ref[...] = sq
            dg_ref[...] = part

        @pl.when(pl.program_id(0) > 0)
        def _():
            sq_ref[...] += sq
            dg_ref[...] += part

    row = pl.BlockSpec((tr, d), lambda i: (i, 0))
    vec = pl.BlockSpec((1, d), lambda i: (0, 0))
    return pl.pallas_call(
        body, name="loss_head", grid=(s // tr,),
        in_specs=[row, vec, row],
        out_specs=[row, row, pl.BlockSpec((1, 128), lambda i: (0, 0)), vec],
        out_shape=[jax.ShapeDtypeStruct((s, d), F32), jax.ShapeDtypeStruct((s, d), BF16),
                   jax.ShapeDtypeStruct((1, 128), F32), jax.ShapeDtypeStruct((1, d), F32)],
        compiler_params=_params(1),
    )(x3, g, target)


def _rope_tables(s):
    inv = 1.0 / (ROPE_THETA ** (jnp.arange(0, HEAD_DIM, 2, dtype=F32) / HEAD_DIM))
    ang = jnp.arange(s, dtype=F32)[:, None] * inv[None, :]
    cos, sin = jnp.cos(ang), jnp.sin(ang)
    return jnp.concatenate([cos, cos], axis=1), jnp.concatenate([-sin, sin], axis=1)


def _swap_halves(t):
    return pltpu.roll(t, HEAD_DIM // 2, 1)


def _rope_fwd(z, cos_t, sin_t, after=None):
    s = z.shape[0]
    tr = _row_tile(s, 256)

    def body(zq_ref, zk_ref, zv_ref, c_ref, s_ref, *rest):
        q_ref, k_ref, v_ref = rest[-3:]
        c, sn = c_ref[...], s_ref[...]
        for hd in range(N_Q_HEADS):
            cols = slice(hd * HEAD_DIM, (hd + 1) * HEAD_DIM)
            t = zq_ref[:, cols]
            q_ref[:, cols] = (t * c + _swap_halves(t) * sn).astype(BF16)
        for hd in range(N_KV_HEADS):
            cols = slice(hd * HEAD_DIM, (hd + 1) * HEAD_DIM)
            t = zk_ref[:, cols]
            k_ref[:, cols] = (t * c + _swap_halves(t) * sn).astype(BF16)
        v_ref[...] = zv_ref[...].astype(BF16)

    tab = pl.BlockSpec((tr, HEAD_DIM), lambda i: (i, 0))
    return pl.pallas_call(
        body, name="rope_fwd", grid=(s // tr,),
        in_specs=[pl.BlockSpec((tr, ATTN_WIDTH), lambda i: (i, Q_OFF // ATTN_WIDTH)),
                  pl.BlockSpec((tr, KV_WIDTH), lambda i: (i, K_OFF // KV_WIDTH)),
                  pl.BlockSpec((tr, KV_WIDTH), lambda i: (i, V_OFF // KV_WIDTH)), tab, tab] + ([] if after is None else [ANY]),
        out_specs=[pl.BlockSpec((tr, ATTN_WIDTH), lambda i: (i, 0)), pl.BlockSpec((tr, KV_WIDTH), lambda i: (i, 0)),
                   pl.BlockSpec((tr, KV_WIDTH), lambda i: (i, 0))],
        out_shape=[jax.ShapeDtypeStruct((s, ATTN_WIDTH), BF16), jax.ShapeDtypeStruct((s, KV_WIDTH), BF16),
                   jax.ShapeDtypeStruct((s, KV_WIDTH), BF16)],
        compiler_params=_params(1),
    )(z, z, z, cos_t, sin_t, *([] if after is None else [after]))


def _rope_bwd(dq_rot, dk_rot, dv, cos_t, sin_t, dz):
    s = dq_rot.shape[0]
    tr = _row_tile(s, 256)
    qkv_width = V_OFF + KV_WIDTH

    def body(dq_ref, dk_ref, dv_ref, c_ref, s_ref, dz_in_ref, o_ref):
        c, sn = c_ref[...], s_ref[...]
        for hd in range(N_Q_HEADS):
            t = dq_ref[:, hd * HEAD_DIM:(hd + 1) * HEAD_DIM]
            o_ref[:, Q_OFF + hd * HEAD_DIM:Q_OFF + (hd + 1) * HEAD_DIM] = (t * c + _swap_halves(t * sn)).astype(BF16)
        for hd in range(N_KV_HEADS):
            t = dk_ref[:, hd * HEAD_DIM:(hd + 1) * HEAD_DIM]
            o_ref[:, K_OFF + hd * HEAD_DIM:K_OFF + (hd + 1) * HEAD_DIM] = (t * c + _swap_halves(t * sn)).astype(BF16)
        o_ref[:, V_OFF:V_OFF + KV_WIDTH] = dv_ref[...].astype(BF16)

    tab = pl.BlockSpec((tr, HEAD_DIM), lambda i: (i, 0))
    wide = pl.BlockSpec((tr, ATTN_WIDTH), lambda i: (i, 0))
    narrow = pl.BlockSpec((tr, KV_WIDTH), lambda i: (i, 0))
    return pl.pallas_call(
        body, name="rope_bwd", grid=(s // tr,),
        in_specs=[wide, narrow, narrow, tab, tab, ANY],
        out_specs=pl.BlockSpec((tr, qkv_width), lambda i: (i, 0)),
        out_shape=jax.ShapeDtypeStruct(dz.shape, dz.dtype),
        input_output_aliases={5: 0},
        compiler_params=_params(1),
    )(dq_rot, dk_rot, dv, cos_t, sin_t, dz)


def _swa_band(i, s):
    return pl.multiple_of(jnp.clip((i - 1) * BLOCK, 0, s - BAND), BLOCK)


SWA_HEADS_PER_PASS = Q_GROUP


def _swa_probs(q_ref, k_ref, sink_ref, heads, start, valid):
    kv = heads[0] // Q_GROUP
    cols = slice(kv * HEAD_DIM, (kv + 1) * HEAD_DIM)
    kb = k_ref[pl.ds(start, BAND), cols]
    qg = jnp.concatenate([q_ref[:, hd * HEAD_DIM:(hd + 1) * HEAD_DIM] for hd in heads], axis=0)
    sc = lax.dot_general(qg, kb, (((1,), (1,)), ((), ())), preferred_element_type=F32) * ATTN_SCALE
    sc = jnp.where(valid, sc, NEG_INF)
    sk = jnp.concatenate([jnp.full((BLOCK, 1), sink_ref[hd], F32) for hd in heads], axis=0)
    mx = jnp.maximum(jnp.max(sc, axis=1, keepdims=True), sk)
    e = jnp.exp(sc - mx)
    es = jnp.exp(sk - mx)
    inv = 1.0 / (jnp.sum(e, axis=1, keepdims=True) + es)
    return qg, kb, e * inv, es * inv


def _swa_head_passes():
    return [list(range(h0, h0 + SWA_HEADS_PER_PASS)) for h0 in range(0, N_Q_HEADS, SWA_HEADS_PER_PASS)]


def _swa_valid(i, start):
    q_pos = i * BLOCK + lax.broadcasted_iota(jnp.int32, (BLOCK, 1), 0)
    q_pos = jnp.concatenate([q_pos] * SWA_HEADS_PER_PASS, axis=0)
    k_pos = start + lax.broadcasted_iota(jnp.int32, (1, BAND), 1)
    return jnp.abs(k_pos - q_pos) <= WINDOW


def _swa_fwd(q, k, v, sink):
    s = q.shape[0]
    assert s % BLOCK == 0 and s >= BAND

    def body(sink_ref, q_ref, k_ref, v_ref, o_ref):
        i = pl.program_id(0)
        start = _swa_band(i, s)
        valid = _swa_valid(i, start)
        for heads in _swa_head_passes():
            kv = heads[0] // Q_GROUP
            _, _, p, _ = _swa_probs(q_ref, k_ref, sink_ref, heads, start, valid)
            vb = v_ref[pl.ds(start, BAND), kv * HEAD_DIM:(kv + 1) * HEAD_DIM]
            o = jnp.dot(p.astype(BF16), vb, preferred_element_type=F32)
            for g, hd in enumerate(heads):
                o_ref[:, hd * HEAD_DIM:(hd + 1) * HEAD_DIM] = o[g * BLOCK:(g + 1) * BLOCK].astype(BF16)

    whole = pl.BlockSpec((s, KV_WIDTH), lambda i: (0, 0))
    blk = pl.BlockSpec((BLOCK, ATTN_WIDTH), lambda i: (i, 0))
    return pl.pallas_call(
        body, name="swa_fwd", grid=(s // BLOCK,),
        in_specs=[pl.BlockSpec(memory_space=pltpu.SMEM), blk, whole, whole],
        out_specs=blk,
        out_shape=jax.ShapeDtypeStruct((s, ATTN_WIDTH), BF16),
        compiler_params=_params(1),
    )(sink, q, k, v)


def _swa_bwd(q, k, v, d_out, sink):
    s = q.shape[0]

    def body(sink_ref, q_ref, k_ref, v_ref, do_ref, dq_ref, dk_ref, dv_ref, dsink_ref):
        i = pl.program_id(0)

        @pl.when(i == 0)
        def _():
            dk_ref[...] = jnp.zeros_like(dk_ref)
            dv_ref[...] = jnp.zeros_like(dv_ref)
            dsink_ref[...] = jnp.zeros_like(dsink_ref)

        start = _swa_band(i, s)
        valid = _swa_valid(i, start)
        for heads in _swa_head_passes():
            kv = heads[0] // Q_GROUP
            cols = slice(kv * HEAD_DIM, (kv + 1) * HEAD_DIM)
            qg, kb, p, p_sink = _swa_probs(q_ref, k_ref, sink_ref, heads, start, valid)
            vb = v_ref[pl.ds(start, BAND), cols]
            dog = jnp.concatenate([do_ref[:, hd * HEAD_DIM:(hd + 1) * HEAD_DIM] for hd in heads], axis=0)
            dp = lax.dot_general(dog, vb, (((1,), (1,)), ((), ())), preferred_element_type=F32)
            delta = jnp.sum(p * dp, axis=1, keepdims=True)
            ds = (p * (dp - delta) * ATTN_SCALE).astype(BF16)
            dqg = jnp.dot(ds, kb, preferred_element_type=F32)
            dk_ref[pl.ds(start, BAND), cols] += lax.dot_general(ds, qg, (((0,), (0,)), ((), ())), preferred_element_type=F32)
            dv_ref[pl.ds(start, BAND), cols] += lax.dot_general(p.astype(BF16), dog, (((0,), (0,)), ((), ())),
                                                                 preferred_element_type=F32)
            dsk = p_sink * delta
            for g, hd in enumerate(heads):
                dq_ref[:, hd * HEAD_DIM:(hd + 1) * HEAD_DIM] = dqg[g * BLOCK:(g + 1) * BLOCK]
                tot = jnp.sum(dsk[g * BLOCK:(g + 1) * BLOCK], axis=0, keepdims=True)
                dsink_ref[hd:hd + 1, :] -= jnp.broadcast_to(tot, (1, 128))

    whole = pl.BlockSpec((s, KV_WIDTH), lambda i: (0, 0))
    blk = pl.BlockSpec((BLOCK, ATTN_WIDTH), lambda i: (i, 0))
    return pl.pallas_call(
        body, name="swa_bwd", grid=(s // BLOCK,),
        in_specs=[pl.BlockSpec(memory_space=pltpu.SMEM), blk, whole, whole, blk],
        out_specs=[blk, whole, whole, pl.BlockSpec((N_Q_HEADS, 128), lambda i: (0, 0))],
        out_shape=[jax.ShapeDtypeStruct((s, ATTN_WIDTH), F32), jax.ShapeDtypeStruct((s, KV_WIDTH), F32),
                   jax.ShapeDtypeStruct((s, KV_WIDTH), F32), jax.ShapeDtypeStruct((N_Q_HEADS, 128), F32)],
        compiler_params=_params(1),
    )(sink, q, k, v, d_out)


CONV_CHUNK = 256


def _shift_rows(t, rows, down):
    n = t.shape[0]
    rolled = pltpu.roll(t, 1 if down else n - 1, 0)
    edge = 0 if down else n - 1
    return jnp.where(rows == edge, 0.0, rolled)


def _conv_specs(s):
    def z_spec(off):
        return pl.BlockSpec((s, CONV_CHUNK), lambda j, off=off: (0, off // CONV_CHUNK + j))
    chunk = pl.BlockSpec((s, CONV_CHUNK), lambda j: (0, j))
    w_spec = pl.BlockSpec((3, CONV_CHUNK), lambda j: (0, j))
    return z_spec(CU_OFF), z_spec(CB_OFF), z_spec(CC_OFF), chunk, w_spec


def _conv_fwd(z, conv_w, after=None):
    s = z.shape[0]
    cu_spec, cb_spec, cc_spec, chunk, w_spec = _conv_specs(s)

    def body(cu_ref, cb_ref, cc_ref, w_ref, *rest):
        o_ref = rest[-1]
        rows = lax.broadcasted_iota(jnp.int32, (s, 1), 0)
        t = cc_ref[...] * cu_ref[...]
        c3 = _shift_rows(t, rows, True) * w_ref[0:1, :] + t * w_ref[1:2, :] + _shift_rows(t, rows, False) * w_ref[2:3, :]
        o_ref[...] = (cb_ref[...] * c3).astype(BF16)

    return pl.pallas_call(
        body, name="conv_fwd", grid=(CONV_WIDTH // CONV_CHUNK,),
        in_specs=[cu_spec, cb_spec, cc_spec, w_spec] + ([] if after is None else [ANY]),
        out_specs=chunk,
        out_shape=jax.ShapeDtypeStruct((s, CONV_WIDTH), BF16),
        compiler_params=_params(1),
    )(z, z, z, conv_w, *([] if after is None else [after]))


def _conv_bwd(z, conv_w, d_co, dz):
    s = z.shape[0]
    cu_spec, cb_spec, cc_spec, chunk, w_spec = _conv_specs(s)
    n_chunks = CONV_WIDTH // CONV_CHUNK
    offsets = (CU_OFF, CB_OFF, CC_OFF)

    def body(cu_ref, cb_ref, cc_ref, w_ref, d_ref, dz_in_ref, dz_ref, dw_ref, buf, sems):
        j = pl.program_id(0)

        def copies(j_at):
            return [pltpu.make_async_copy(buf.at[h], dz_ref.at[:, pl.ds(off + j_at * CONV_CHUNK, CONV_CHUNK)], sems.at[h])
                    for h, off in enumerate(offsets)]

        rows = lax.broadcasted_iota(jnp.int32, (s, 1), 0)
        cu, cc = cu_ref[...], cc_ref[...]
        t = cc * cu
        t_dn, t_up = _shift_rows(t, rows, True), _shift_rows(t, rows, False)
        c3 = t_dn * w_ref[0:1, :] + t * w_ref[1:2, :] + t_up * w_ref[2:3, :]
        d = d_ref[...]
        dc3 = d * cb_ref[...]
        dw_ref[0:1, :] = jnp.sum(dc3 * t_dn, axis=0, keepdims=True)
        dw_ref[1:2, :] = jnp.sum(dc3 * t, axis=0, keepdims=True)
        dw_ref[2:3, :] = jnp.sum(dc3 * t_up, axis=0, keepdims=True)
        dt = _shift_rows(dc3, rows, False) * w_ref[0:1, :] + dc3 * w_ref[1:2, :] + _shift_rows(dc3, rows, True) * w_ref[2:3, :]

        @pl.when(j > 0)
        def _():
            for cp in copies(j):
                cp.wait()

        buf[0] = (dt * cc).astype(BF16)
        buf[1] = (d * c3).astype(BF16)
        buf[2] = (dt * cu).astype(BF16)
        for cp in copies(j):
            cp.start()

        @pl.when(j == n_chunks - 1)
        def _():
            for cp in copies(j):
                cp.wait()

    return pl.pallas_call(
        body, name="conv_bwd", grid=(n_chunks,),
        in_specs=[cu_spec, cb_spec, cc_spec, w_spec, chunk, ANY],
        out_specs=[ANY, w_spec],
        out_shape=[jax.ShapeDtypeStruct(dz.shape, dz.dtype), jax.ShapeDtypeStruct((3, CONV_WIDTH), F32)],
        input_output_aliases={5: 0},
        scratch_shapes=[pltpu.VMEM((3, s, CONV_CHUNK), BF16), pltpu.SemaphoreType.DMA((3,))],
        compiler_params=_params(1),
    )(z, z, z, conv_w, d_co, dz)


GATE_CHUNK = 512


def _gate_specs(s, d, tr):
    n_chunks = d // GATE_CHUNK
    za = pl.BlockSpec((tr, GATE_CHUNK), lambda j, i: (i, GL_OFF // GATE_CHUNK + j))
    zc = pl.BlockSpec((tr, GATE_CHUNK), lambda j, i: (i, GL_OFF // GATE_CHUNK + n_chunks + j))
    ba = pl.BlockSpec((1, GATE_CHUNK), lambda j, i: (0, j))
    bc = pl.BlockSpec((1, GATE_CHUNK), lambda j, i: (0, n_chunks + j))
    tile = pl.BlockSpec((tr, GATE_CHUNK), lambda j, i: (i, j))
    return za, zc, ba, bc, tile


def _gate_fwd(z, b_gate, ya, yc):
    s, d = ya.shape
    tr = _row_tile(s, 512)
    za, zc, ba, bc, tile = _gate_specs(s, d, tr)

    def body(za_ref, zc_ref, ba_ref, bc_ref, ya_ref, yc_ref, o_ref):
        ga = jax.nn.sigmoid(za_ref[...] + ba_ref[...])
        gc = jax.nn.sigmoid(zc_ref[...] + bc_ref[...])
        o_ref[...] = (ga * ya_ref[...] + gc * yc_ref[...]).astype(BF16)

    return pl.pallas_call(
        body, name="gate_fwd", grid=(d // GATE_CHUNK, s // tr),
        in_specs=[za, zc, ba, bc, tile, tile],
        out_specs=tile,
        out_shape=jax.ShapeDtypeStruct((s, d), BF16),
        compiler_params=_params(2),
    )(z, z, b_gate, b_gate, ya, yc)


def _gate_bwd(z, b_gate, ya, yc, dmix):
    s, d = ya.shape
    tr = _row_tile(s, 512)
    za, zc, ba, bc, tile = _gate_specs(s, d, tr)
    vec = pl.BlockSpec((1, GATE_CHUNK), lambda j, i: (0, j))
    n_rows = s // tr
    in_width = z.shape[1]

    def body(za_ref, zc_ref, ba_ref, bc_ref, ya_ref, yc_ref, dm_ref, dya_ref, dyc_ref, dz_ref, dba_ref, dbc_ref, buf, sems):
        j, i = pl.program_id(0), pl.program_id(1)

        def copies(j_at, i_at):
            rows = pl.ds(i_at * tr, tr)
            return [pltpu.make_async_copy(buf.at[h], dz_ref.at[rows, pl.ds(GL_OFF + h * d + j_at * GATE_CHUNK, GATE_CHUNK)],
                                          sems.at[h]) for h in range(2)]

        ga = jax.nn.sigmoid(za_ref[...] + ba_ref[...])
        gc = jax.nn.sigmoid(zc_ref[...] + bc_ref[...])
        dm = dm_ref[...]
        dya_ref[...] = (dm * ga).astype(BF16)
        dyc_ref[...] = (dm * gc).astype(BF16)
        dla = dm * ya_ref[...] * ga * (1.0 - ga)
        dlc = dm * yc_ref[...] * gc * (1.0 - gc)

        @pl.when(j * n_rows + i > 0)
        def _():
            for cp in copies(j, i):
                cp.wait()

        buf[0] = dla.astype(BF16)
        buf[1] = dlc.astype(BF16)
        for cp in copies(j, i):
            cp.start()

        @pl.when((j == d // GATE_CHUNK - 1) & (i == n_rows - 1))
        def _():
            for cp in copies(j, i):
                cp.wait()

        pa = jnp.sum(dla, axis=0, keepdims=True)
        pc = jnp.sum(dlc, axis=0, keepdims=True)

        @pl.when(i == 0)
        def _():
            dba_ref[...] = pa
            dbc_ref[...] = pc

        @pl.when(i > 0)
        def _():
            dba_ref[...] += pa
            dbc_ref[...] += pc

    big = jax.ShapeDtypeStruct((s, d), BF16)
    small = jax.ShapeDtypeStruct((1, d), F32)
    return pl.pallas_call(
        body, name="gate_bwd", grid=(d // GATE_CHUNK, n_rows),
        in_specs=[za, zc, ba, bc, tile, tile, tile],
        out_specs=[tile, tile, ANY, vec, vec],
        out_shape=[big, big, jax.ShapeDtypeStruct((s, in_width), BF16), small, small],
        scratch_shapes=[pltpu.VMEM((2, tr, GATE_CHUNK), BF16), pltpu.SemaphoreType.DMA((2,))],
        compiler_params=_params(2),
    )(z, z, b_gate, b_gate, ya, yc, dmix)


def _cross_probs(q_ref, kv_ref, hd):
    cols = slice(hd * HEAD_DIM, (hd + 1) * HEAD_DIM)
    qh = q_ref[:, cols]
    kh = kv_ref[:, cols]
    sc = lax.dot_general(qh, kh, (((1,), (1,)), ((), ())), preferred_element_type=F32) * ATTN_SCALE
    e = jnp.exp(sc - jnp.max(sc, axis=1, keepdims=True))
    return qh, kh, e * (1.0 / jnp.sum(e, axis=1, keepdims=True))


def _cross_fwd(qc, kvc):
    s = qc.shape[0]
    n_mem = kvc.shape[0]
    tq = _row_tile(s, 256)

    def body(q_ref, kv_ref, o_ref):
        for hd in range(MEM_HEADS):
            _, _, p = _cross_probs(q_ref, kv_ref, hd)
            vh = kv_ref[:, MEM_WIDTH + hd * HEAD_DIM:MEM_WIDTH + (hd + 1) * HEAD_DIM]
            o_ref[:, hd * HEAD_DIM:(hd + 1) * HEAD_DIM] = jnp.dot(p.astype(BF16), vh, preferred_element_type=F32).astype(BF16)

    return pl.pallas_call(
        body, name="cross_fwd", grid=(s // tq,),
        in_specs=[pl.BlockSpec((tq, MEM_WIDTH), lambda i: (i, 0)), pl.BlockSpec((n_mem, 2 * MEM_WIDTH), lambda i: (0, 0))],
        out_specs=pl.BlockSpec((tq, MEM_WIDTH), lambda i: (i, 0)),
        out_shape=jax.ShapeDtypeStruct((s, MEM_WIDTH), BF16),
        compiler_params=_params(1),
    )(qc, kvc)


def _cross_bwd(qc, kvc, d_out):
    s = qc.shape[0]
    n_mem = kvc.shape[0]
    tq = _row_tile(s, 256)

    def body(q_ref, kv_ref, do_ref, dq_ref, dkv_ref):
        @pl.when(pl.program_id(0) == 0)
        def _():
            dkv_ref[...] = jnp.zeros_like(dkv_ref)

        for hd in range(MEM_HEADS):
            cols = slice(hd * HEAD_DIM, (hd + 1) * HEAD_DIM)
            vcols = slice(MEM_WIDTH + hd * HEAD_DIM, MEM_WIDTH + (hd + 1) * HEAD_DIM)
            qh, kh, p = _cross_probs(q_ref, kv_ref, hd)
            doh = do_ref[:, cols]
            dp = lax.dot_general(doh, kv_ref[:, vcols], (((1,), (1,)), ((), ())), preferred_element_type=F32)
            ds = (p * (dp - jnp.sum(p * dp, axis=1, keepdims=True)) * ATTN_SCALE).astype(BF16)
            dq_ref[:, cols] = jnp.dot(ds, kh, preferred_element_type=F32).astype(BF16)
            dkv_ref[:, cols] += lax.dot_general(ds, qh, (((0,), (0,)), ((), ())), preferred_element_type=F32)
            dkv_ref[:, vcols] += lax.dot_general(p.astype(BF16), doh, (((0,), (0,)), ((), ())), preferred_element_type=F32)

    qspec = pl.BlockSpec((tq, MEM_WIDTH), lambda i: (i, 0))
    kvspec = pl.BlockSpec((n_mem, 2 * MEM_WIDTH), lambda i: (0, 0))
    return pl.pallas_call(
        body, name="cross_bwd", grid=(s // tq,),
        in_specs=[qspec, kvspec, qspec],
        out_specs=[qspec, kvspec],
        out_shape=[jax.ShapeDtypeStruct((s, MEM_WIDTH), BF16), jax.ShapeDtypeStruct((n_mem, 2 * MEM_WIDTH), F32)],
        compiler_params=_params(1),
    )(qc, kvc, d_out)


def _swiglu_fwd(up, gate):
    sg = jax.nn.sigmoid(gate)
    silu = gate * sg
    return silu * up, up * (sg * (1.0 + gate * (1.0 - sg))), silu


def _swiglu_bwd(d_act, dact_dgate, dact_dup):
    return d_act * dact_dgate.astype(F32), d_act * dact_dup.astype(F32)


GATHER_GROUPS = {"in": ("w_in", "conv_w"), "mid": ("w_attn_out", "w_conv_out", "w_o", "w_cq", "w_ckv", "w_co"),
                 "gate": ("w_gate",), "up": ("w_up",), "down": ("w_down",)}


def _local_step(xs, mems, target, small, fetch, reduce):
    s, d = xs.shape
    w4 = {}
    cos_t, sin_t = _rope_tables(s)

    def near(group, done, then, after):
        waits = [("direct", group)] + ([("pass_near", done), ("pass_far", done)] if done else [])
        starts = [("forward", group), ("pass_near", group)] + [("direct", g) for g in then]
        tok = fetch.step("gather_near_" + group, waits, starts, after)
        if done:
            w4.update(fetch.arrays(done))
        return tok

    def far(group, then, after):
        return fetch.step("gather_far_" + group, [("forward", group)], [("pass_far", group)] + [("direct", g) for g in then], after)

    def last(group, after):
        tok = fetch.step("gather_done_" + group, [("pass_near", group), ("pass_far", group)], [], after)
        w4.update(fetch.arrays(group))
        return tok

    h = _rmsnorm(xs, small["g_mix"], "norm_mix")
    slots_filled = [a for g in ("gate", "up", "down") for a in fetch.arrays(g).values()]
    chip_x, chip_y = reduce.place[0] // 2, reduce.place[0] % 2
    own_block = jnp.stack([2 * chip_x + chip_y]).astype(jnp.int32)
    near_blocks = jnp.stack([2 * (1 - chip_x) + chip_y, 2 * chip_x + (1 - chip_y)]).astype(jnp.int32)
    far_block = jnp.stack([2 * (1 - chip_x) + (1 - chip_y)]).astype(jnp.int32)
    z = _matmul_column_blocks(h, fetch.arrays("in")["w_in"], own_block, None, tm=512, name="in_proj_own")
    tok = near("in", None, ["mid"], [z] + slots_filled)
    tok = fetch.step("gather_near_done_in", [("pass_near", "in")], [], tok)
    z = _matmul_column_blocks(h, fetch.arrays("in")["w_in"], near_blocks, z, tm=512, name="in_proj_near", after=tok)
    tok = far("in", [], z)
    tok = fetch.step("gather_done_in", [("pass_far", "in")], [], tok)
    w4.update(fetch.arrays("in"))
    z = _matmul_column_blocks(h, w4["w_in"], far_block, z, tm=512, name="in_proj_far", after=tok)
    conv4 = w4["conv_w"]
    conv_w = conv4[:, :3, :].transpose(1, 0, 2).reshape(3, N_CHIPS * conv4.shape[2])
    c_in = w4["w_in"].shape[2]
    tok = near("mid", None, ["gate"], z)
    q_rot, k_rot, v_b = _rope_fwd(z, cos_t, sin_t)
    attn = _swa_fwd(q_rot, k_rot, v_b, small["sink"])
    co = _conv_fwd(z, conv_w)
    tok = far("mid", ["up"], attn)
    tok = last("mid", tok)
    w_o = w4["w_o"].reshape(-1, w4["w_o"].shape[-1])
    c_d = w4["w_attn_out"].shape[2]
    ya = _matmul(attn, w4["w_attn_out"], mode="nn", tm=2048, tn=c_d, tk=ATTN_WIDTH, out_dtypes=[F32], name="attn_out_proj",
                 b_blocks=N_CHIPS, after=tok)
    yc = _matmul(co, w4["w_conv_out"], mode="nn", tm=2048, tn=c_d, tk=CONV_WIDTH, out_dtypes=[F32], name="conv_out_proj",
                 b_blocks=N_CHIPS)
    mix = _gate_fwd(z, small["b_gate"], ya, yc)
    x1 = _matmul(mix, w_o, mode="nn", tm=1024, tn=1024, tk=d, out_dtypes=[F32], name="mix_out_proj", extras=[xs],
                 epilogue=_add_residual)
    tok = near("gate", None, ["down"], x1)
    w_cq = w4["w_cq"].reshape(-1, w4["w_cq"].shape[-1])
    w_ckv = w4["w_ckv"].reshape(-1, w4["w_ckv"].shape[-1])
    hc = _rmsnorm(x1, small["g_cross"], "norm_cross")
    memn = _rmsnorm(mems, small["g_mem"], "norm_mem")
    qc = _matmul(hc, w_cq, mode="nn", tm=1024, tn=MEM_WIDTH, tk=d, out_dtypes=[BF16], name="cross_q_proj", after=tok)
    kvc = _matmul(memn, w_ckv, mode="nn", tm=256, tn=2 * MEM_WIDTH, tk=d, out_dtypes=[BF16], name="cross_kv_proj")
    oc = _cross_fwd(qc, kvc)
    tok = far("gate", [], oc)
    x2 = _matmul(oc, w4["w_co"], mode="nn", tm=2048, tn=c_d, tk=MEM_WIDTH, out_dtypes=[F32], name="cross_out_proj",
                 extras=[x1], epilogue=_add_residual, b_blocks=N_CHIPS, after=tok)
    hf = _rmsnorm(x2, small["g_ffn"], "norm_ffn")
    tok = near("up", "gate", [], hf)
    c_ff = w4["w_gate"].shape[2]
    gate = _matmul(hf, w4["w_gate"], mode="nn", tm=512, tn=c_ff, tk=d, out_dtypes=[F32], name="ffn_gate_proj", b_blocks=N_CHIPS,
                   after=tok)
    tok = far("up", [], gate)
    tok = near("down", "up", [], tok)
    act, dact_dgate, dact_dup = _matmul(hf, w4["w_up"], mode="nn", tm=512, tn=c_ff, tk=d, out_dtypes=[BF16, BF16, BF16],
                                        name="ffn_up_proj", extras=[gate], epilogue=_swiglu_fwd, b_blocks=N_CHIPS, after=tok)
    tok = far("down", [], act)
    last("down", tok)
    w_down = w4["w_down"].reshape(-1, w4["w_down"].shape[-1])
    x3 = _matmul(act, w_down, mode="nn", tm=512, tn=512, tk=w_down.shape[0], out_dtypes=[F32], name="ffn_down_proj", extras=[x2],
                 epilogue=_add_residual)
    dx3, dx3b, sq, dg_final = _loss_head(x3, small["g_final"], target)

    da, du = _matmul(dx3b, w_down, mode="nt", tm=512, tn=c_ff, tk=d, out_dtypes=[BF16, BF16], name="ffn_down_bwd",
                     extras=[dact_dgate, dact_dup], epilogue=_swiglu_bwd)
    core = reduce.core
    ffn_shape = dict(row_sharded=False, tm=1024, tn=c_ff)
    g_down = _matmul(act, dx3b, mode="tn", tm=c_ff, tn=1024, tk=s, out_dtypes=[BF16], name="ffn_down_wgrad")
    tok = reduce.add("down", {"w_down": g_down}, da)
    t_gate = _wgrad_half(hf, da, core, theirs=True, name="ffn_gate_wgrad_theirs", after=tok, **ffn_shape)
    tok = reduce.step("down", t_gate)
    t_up = _wgrad_half(hf, du, core, theirs=True, name="ffn_up_wgrad_theirs", after=tok, **ffn_shape)
    tok = reduce.send("ffn", {"w_gate": t_gate, "w_up": t_up}, dx3b)
    dhf = _matmul(da, w4["w_gate"], mode="nt", tm=512, tn=1024, tk=N_CHIPS * c_ff, out_dtypes=[F32], name="ffn_gate_bwd", b_blocks=N_CHIPS,
                  after=tok)
    got = reduce.received("ffn", dhf)
    p_gate = _wgrad_half(hf, da, core, theirs=False, name="ffn_gate_wgrad_mine", add=got["w_gate"], **ffn_shape)
    p_up = _wgrad_half(hf, du, core, theirs=False, name="ffn_up_wgrad_mine", add=got["w_up"], **ffn_shape)
    tok = reduce.add_parts("ffn", {"w_gate": p_gate, "w_up": p_up})
    dhf = _matmul(du, w4["w_up"], mode="nt", tm=512, tn=1024, tk=N_CHIPS * c_ff, out_dtypes=[F32], name="ffn_up_bwd", extras=[dhf],
                  epilogue=_add_residual, b_blocks=N_CHIPS, after=tok)
    tok = reduce.step("down", dhf)
    dx2, dx2b, dg_ffn = _rmsnorm_bwd(dhf, x2, small["g_ffn"], dx3, "norm_ffn_bwd")

    d_oc = _matmul(dx2b, w4["w_co"], mode="nt", tm=1024, tn=MEM_WIDTH, tk=d, out_dtypes=[BF16], name="cross_out_bwd",
                   b_blocks=N_CHIPS, after=tok)
    g_co = _matmul(oc, dx2b, mode="tn", tm=MEM_WIDTH, tn=c_d, tk=s, out_dtypes=[BF16], name="cross_out_wgrad", out_blocks=N_CHIPS)
    tok = reduce.step("down", g_co)
    dqc, dkvc = _cross_bwd(qc, kvc, d_oc)
    g_cq = _matmul(hc, dqc, mode="tn", tm=1024, tn=MEM_WIDTH, tk=s, out_dtypes=[BF16], name="cross_q_wgrad", after=tok)
    dhc = _matmul(dqc, w_cq, mode="nt", tm=1024, tn=1024, tk=MEM_WIDTH, out_dtypes=[F32], name="cross_q_bwd")
    g_ckv = _matmul(memn, dkvc, mode="tn", tm=1024, tn=2 * MEM_WIDTH, tk=mems.shape[0], out_dtypes=[BF16], name="cross_kv_wgrad")
    dmemn = _matmul(dkvc, w_ckv, mode="nt", tm=256, tn=1024, tk=2 * MEM_WIDTH, out_dtypes=[F32], name="cross_kv_bwd")
    _, _, dg_mem = _rmsnorm_bwd(dmemn, mems, small["g_mem"], None, "norm_mem_bwd")
    dx1, dx1b, dg_cross = _rmsnorm_bwd(dhc, x1, small["g_cross"], dx2, "norm_cross_bwd")

    dmix = _matmul(dx1b, w_o, mode="nt", tm=1024, tn=1024, tk=d, out_dtypes=[F32], name="mix_out_bwd")
    g_o = _matmul(mix, dx1b, mode="tn", tm=1024, tn=1024, tk=s, out_dtypes=[BF16], name="mix_out_wgrad")
    dya, dyc, dz, db_a, db_c = _gate_bwd(z, small["b_gate"], ya, yc, dmix)
    d_attn = _matmul(dya, w4["w_attn_out"], mode="nt", tm=1024, tn=ATTN_WIDTH, tk=d, out_dtypes=[BF16], name="attn_out_bwd",
                     b_blocks=N_CHIPS)
    g_ao = _matmul(attn, dya, mode="tn", tm=ATTN_WIDTH, tn=c_d, tk=s, out_dtypes=[BF16], name="attn_out_wgrad", out_blocks=N_CHIPS)
    d_co = _matmul(dyc, w4["w_conv_out"], mode="nt", tm=1024, tn=CONV_WIDTH, tk=d, out_dtypes=[F32], name="conv_out_bwd",
                   b_blocks=N_CHIPS)
    g_cvo = _matmul(co, dyc, mode="tn", tm=CONV_WIDTH, tn=c_d, tk=s, out_dtypes=[BF16], name="conv_out_wgrad", out_blocks=N_CHIPS)
    tok = reduce.step("ffn", g_cvo)
    tok = reduce.add("mid", {"w_co": g_co, "w_cq": g_cq, "w_ckv": g_ckv, "w_o": g_o, "w_attn_out": g_ao, "w_conv_out": g_cvo}, tok)
    dz, d_conv_w = _conv_bwd(z, conv_w, d_co, dz)
    dq_rot, dk_rot, dv, dsink = _swa_bwd(q_rot, k_rot, v_b, d_attn, small["sink"])
    tok = reduce.step("mid", dq_rot)
    dz = _rope_bwd(dq_rot, dk_rot, dv, cos_t, sin_t, dz)
    in_shape = dict(row_sharded=False, tm=1024, tn=c_in)
    t_in = _wgrad_half(h, dz, core, theirs=True, name="in_proj_wgrad_theirs", after=tok, **in_shape)
    tok = reduce.send("in", {"w_in": t_in}, dk_rot)
    tok = reduce.step("ffn", tok)
    tok = reduce.step("mid", tok)
    got = reduce.received("in", tok)
    p_in = _wgrad_half(h, dz, core, theirs=False, name="in_proj_wgrad_mine", add=got["w_in"], **in_shape)
    tok = reduce.add_parts("in", {"w_in": p_in})
    dh = _matmul(dz, w4["w_in"], mode="nt", tm=512, tn=512, tk=N_CHIPS * c_in, out_dtypes=[F32], name="in_proj_bwd", b_blocks=N_CHIPS,
                 after=tok)
    tok = reduce.step("mid", dh)
    grad_x, _, dg_mix = _rmsnorm_bwd(dh, xs, small["g_mix"], dx1, "norm_mix_bwd")

    small_grads = {
        "g_mix": dg_mix, "sink": dsink[:, 0], "b_gate": jnp.concatenate([db_a, db_c], axis=1), "g_cross": dg_cross,
        "g_mem": dg_mem, "g_ffn": dg_ffn, "g_final": dg_final, "conv_w": d_conv_w,
    }
    return sq, grad_x, small_grads


def _pair_sum(g4, ra, core, name):
    nb, rs, cs = g4.shape
    rh = rs // 2
    tr = _row_tile(rh, 256)
    per = rh // tr

    def body(c_ref, g_ref, r_ref, o_ref):
        o_ref[...] = (g_ref[...].astype(F32) + r_ref[...].astype(F32)).astype(BF16)

    plain = pl.BlockSpec((None, tr, cs), lambda j, i, c: (j, i, 0))
    return pl.pallas_call(
        body, name=name,
        grid_spec=pltpu.PrefetchScalarGridSpec(
            num_scalar_prefetch=1, grid=(nb, per),
            in_specs=[pl.BlockSpec((None, tr, cs), lambda j, i, c: (j, c[0] * per + i, 0)), plain],
            out_specs=plain),
        out_shape=jax.ShapeDtypeStruct((nb, rh, cs), BF16),
        compiler_params=_params(2),
    )(core, g4, ra)


def _quad_sum(parts, rc, place, name):
    _, rh, cs = parts.shape
    tr = _row_tile(rh, 256)
    per = rh // tr

    def body(p_ref, own_ref, r_ref, o_ref):
        acc = own_ref[...].astype(F32)
        for j in range(rc.shape[0]):
            acc = acc + r_ref[j].astype(F32)
        o_ref[...] = acc

    return pl.pallas_call(
        body, name=name,
        grid_spec=pltpu.PrefetchScalarGridSpec(
            num_scalar_prefetch=1, grid=(per,),
            in_specs=[pl.BlockSpec((None, tr, cs), lambda i, p: (p[0], i, 0)),
                      pl.BlockSpec((rc.shape[0], tr, cs), lambda i, p: (0, i, 0))],
            out_specs=pl.BlockSpec((tr, cs), lambda i, p: (p[1] * per + i, 0))),
        out_shape=jax.ShapeDtypeStruct((2 * rh, cs), F32),
        compiler_params=_params(1),
    )(place, parts, rc)


def _adamw_update(w, g, m, v):
    nm = ADAM_B1 * m + (1.0 - ADAM_B1) * g
    nv = ADAM_B2 * v + (1.0 - ADAM_B2) * (g * g)
    m_hat = nm / ADAM_C1
    v_hat = nv / ADAM_C2
    return -ADAM_LR * (m_hat / (jnp.sqrt(v_hat) + ADAM_EPS) + ADAM_WD * w), nm, nv


def _adamw_own_half(w, m, v, parts, rc, place, name, after=None):
    rows, cols = w.shape
    rh = rows // 2
    tr = _row_tile(rh, 256)
    per = rh // tr

    def body(p_ref, w_ref, m_ref, v_ref, own_ref, r_ref, *rest):
        gx_ref, g_ref, d_ref, nm_ref, nv_ref = rest[-5:]
        g = own_ref[...].astype(F32)
        for j in range(rc.shape[0]):
            g = g + r_ref[j].astype(F32)
        gx_ref[...] = g
        g_ref[...] = g
        d_ref[...], nm_ref[...], nv_ref[...] = _adamw_update(w_ref[...], g, m_ref[...], v_ref[...])

    mine = pl.BlockSpec((tr, cols), lambda i, p: (p[1] * per + i, 0))
    shape = jax.ShapeDtypeStruct((rows, cols), F32)
    return pl.pallas_call(
        body, name=name,
        grid_spec=pltpu.PrefetchScalarGridSpec(
            num_scalar_prefetch=1, grid=(per,),
            in_specs=[mine, mine, mine, pl.BlockSpec((None, tr, cols), lambda i, p: (p[0], i, 0)),
                      pl.BlockSpec((rc.shape[0], tr, cols), lambda i, p: (0, i, 0))] + ([] if after is None else [ANY]),
            out_specs=[mine] * 5),
        out_shape=[shape] * 5,
        compiler_params=_params(1),
    )(place, w, m, v, parts, rc, *([] if after is None else [after]))


def _adamw_other_half(w, m, v, g_exchanged, g, delta, new_m, new_v, place, name, after=None):
    rows, cols = w.shape
    rh = rows // 2
    tr = _row_tile(rh, 256)
    per = rh // tr

    def body(p_ref, w_ref, m_ref, v_ref, gx_ref, *rest):
        g_ref, d_ref, nm_ref, nv_ref = rest[-4:]
        gv = gx_ref[...]
        g_ref[...] = gv
        d_ref[...], nm_ref[...], nv_ref[...] = _adamw_update(w_ref[...], gv, m_ref[...], v_ref[...])

    other = pl.BlockSpec((tr, cols), lambda i, p: ((1 - p[1]) * per + i, 0))
    shape = jax.ShapeDtypeStruct((rows, cols), F32)
    n_after = 0 if after is None else 1
    return pl.pallas_call(
        body, name=name,
        grid_spec=pltpu.PrefetchScalarGridSpec(
            num_scalar_prefetch=1, grid=(per,),
            in_specs=[other] * 4 + [ANY] * (4 + n_after),
            out_specs=[other] * 4),
        out_shape=[shape] * 4,
        input_output_aliases={5: 0, 6: 1, 7: 2, 8: 3},
        compiler_params=_params(1),
    )(place, w, m, v, g_exchanged, g, delta, new_m, new_v, *([] if after is None else [after]))


def _cast_to_slot(w, place, dtype, name, after=None):
    rows, cols = w.shape
    tr = _row_tile(rows, 1024)

    def body(p_ref, w_ref, *rest):
        o_ref = rest[-1]
        o_ref[...] = w_ref[...].astype(dtype)

    return pl.pallas_call(
        body, name=name,
        grid_spec=pltpu.PrefetchScalarGridSpec(
            num_scalar_prefetch=1, grid=(rows // tr,),
            in_specs=[pl.BlockSpec((tr, cols), lambda i, p: (i, 0))] + ([] if after is None else [ANY]),
            out_specs=pl.BlockSpec((None, tr, cols), lambda i, p: (p[0], i, 0))),
        out_shape=jax.ShapeDtypeStruct((N_CHIPS, rows, cols), dtype),
        compiler_params=_params(1),
    )(place, w, *([] if after is None else [after]))


def _adamw(w, g, m, v, name, after=None):
    rows, cols = w.shape
    tr = _row_tile(rows, 256)

    def body(w_ref, g_ref, m_ref, v_ref, *rest):
        go_ref, d_ref, nm_ref, nv_ref = rest[-4:]
        gv = g_ref[...]
        go_ref[...] = gv
        d_ref[...], nm_ref[...], nv_ref[...] = _adamw_update(w_ref[...], gv, m_ref[...], v_ref[...])

    tile = pl.BlockSpec((tr, cols), lambda i: (i, 0))
    shape = jax.ShapeDtypeStruct((rows, cols), F32)
    return pl.pallas_call(
        body, name=name, grid=(rows // tr,),
        in_specs=[tile] * 4 + ([] if after is None else [ANY]), out_specs=[tile] * 4, out_shape=[shape] * 4,
        compiler_params=_params(1),
    )(w, g, m, v, *([] if after is None else [after]))


def _mesh_pos():
    return lax.axis_index("x"), lax.axis_index("y"), lax.axis_index("c")


def _other_chips(x, y):
    return [(1 - x, y), (x, 1 - y), (1 - x, 1 - y)]


def _half_rows(ref, which):
    rh = ref.shape[-2] // 2
    return ref.at[pl.ds(which * rh, rh), :]


def _remote(src, dst, send_sems, recv_sems, sem, to):
    return pltpu.make_async_remote_copy(src_ref=src, dst_ref=dst, send_sem=send_sems.at[sem], recv_sem=recv_sems.at[sem],
                                        device_id=to, device_id_type=MESH)


HBM = pl.BlockSpec(memory_space=pltpu.HBM)
SEM = pl.BlockSpec(memory_space=pltpu.SEMAPHORE)
DATAFLOW_EFFECT = pltpu.SideEffectType.DATAFLOW_SIDE_EFFECTING


def _in_hbm(arrays):
    return [pltpu.with_memory_space_constraint(a, pltpu.HBM) for a in arrays]


def _hbm_like(arrays):
    return [pltpu.HBM(a.shape, a.dtype) for a in arrays]


GATHER_COPIES_PER_ARRAY = {"direct": 2, "forward": 2, "pass_near": 2, "pass_far": 1}


def _gather_copies(kind, refs, x, y, c):
    me, near_x, near_y, far = 2 * x + y, 2 * (1 - x) + y, 2 * x + (1 - y), 2 * (1 - x) + (1 - y)
    to_x, to_y, sibling = (1 - x, y, c), (x, 1 - y, c), (x, y, 1 - c)
    out = []
    for ref in refs:
        rh = ref.shape[1] // 2
        rq = rh // 2

        def half(chip, ref=ref, rh=rh):
            return ref.at[chip, pl.ds(c * rh, rh), :]

        def quarter(chip, q, ref=ref, rh=rh, rq=rq):
            return ref.at[chip, pl.ds(c * rh + q * rq, rq), :]

        if kind == "direct":
            out += [(half(me), half(me), to_x), (half(me), half(me), to_y)]
        elif kind == "forward":
            out += [(quarter(near_x, 0), quarter(near_x, 0), to_y), (quarter(near_y, 1), quarter(near_y, 1), to_x)]
        elif kind == "pass_near":
            out += [(half(near_x), half(near_x), sibling), (half(near_y), half(near_y), sibling)]
        else:
            assert kind == "pass_far"
            out += [(half(far), half(far), sibling)]
    return out


def _gather_step(name, bufs, waits, starts, after):
    nb, nw, ns = len(bufs), len(waits), len(starts)
    after = [] if after is None else list(after) if isinstance(after, (list, tuple)) else [after]
    n_after = len(after)

    def body(*refs):
        ins = refs[:nb]
        wait_sems = refs[nb:nb + 2 * nw]
        start_sems = refs[nb + 2 * nw + n_after:nb + 2 * nw + n_after + 2 * ns]
        token = refs[-1]
        x, y, c = _mesh_pos()
        for j, (kind, idxs, _, _) in enumerate(waits):
            for i, (s_ref, d_ref, to) in enumerate(_gather_copies(kind, [ins[t] for t in idxs], x, y, c)):
                came = _remote(s_ref, d_ref, wait_sems[2 * j], wait_sems[2 * j + 1], i, to)
                came.wait_recv()
                came.wait_send()
        for j, (kind, idxs) in enumerate(starts):
            for i, (s_ref, d_ref, to) in enumerate(_gather_copies(kind, [ins[t] for t in idxs], x, y, c)):
                _remote(s_ref, d_ref, start_sems[2 * j], start_sems[2 * j + 1], i, to).start()
        token[...] = jnp.zeros_like(token)

    sems = []
    for kind, idxs in starts:
        sems += [pltpu.SemaphoreType.DMA((GATHER_COPIES_PER_ARRAY[kind] * len(idxs),))] * 2
    operands = _in_hbm(bufs) + [sem for w in waits for sem in w[2:]] + after
    outs = pl.pallas_call(
        body, name=name,
        in_specs=[HBM] * nb + [SEM] * (2 * nw) + [ANY] * n_after,
        out_specs=[SEM] * (2 * ns) + [HBM] * nb + [pl.BlockSpec(memory_space=pltpu.VMEM)],
        out_shape=sems + _hbm_like(bufs) + [jax.ShapeDtypeStruct((8, 128), F32)],
        input_output_aliases={i: 2 * ns + i for i in range(nb)},
        compiler_params=pltpu.CompilerParams(has_side_effects=DATAFLOW_EFFECT),
    )(*operands)
    return outs[2 * ns:2 * ns + nb], [(outs[2 * j], outs[2 * j + 1]) for j in range(ns)], outs[-1]


class _Gather:
    def __init__(self, groups):
        self.groups = groups
        self.bufs = {}
        self.in_flight = {}

    def put(self, slotted):
        self.bufs.update(slotted)

    def step(self, name, waits, starts, after=None):
        names = []
        for _, group in list(waits) + list(starts):
            names += [n for n in self.groups[group] if n not in names]
        index = {n: i for i, n in enumerate(names)}

        def members(group):
            return [index[n] for n in self.groups[group]]

        wait_args = [(kind, members(group)) + self.in_flight.pop((kind, group)) for kind, group in waits]
        start_args = [(kind, members(group)) for kind, group in starts]
        bufs, sems, token = _gather_step(name, [self.bufs[n] for n in names], wait_args, start_args, after)
        self.bufs.update(zip(names, bufs))
        for (kind, group), pair in zip(starts, sems):
            self.in_flight[(kind, group)] = pair
        return token

    def arrays(self, group):
        return {n: self.bufs[n] for n in self.groups[group]}


def _sibling_halves_copies(srcs, dsts, x, y, c):
    out = []
    for s_ref, d_ref in zip(srcs, dsts, strict=True):
        rh = s_ref.shape[1] // 2
        out.append((s_ref.at[:, pl.ds((1 - c) * rh, rh), :], d_ref, (x, y, 1 - c)))
    return out


def _to_sibling_copies(srcs, dsts, x, y, c):
    return [(s_ref, d_ref, (x, y, 1 - c)) for s_ref, d_ref in zip(srcs, dsts, strict=True)]


def _chip_copies(srcs, dsts, x, y, c):
    out = []
    for s_ref, d_ref in zip(srcs, dsts, strict=True):
        for k, (px, py) in enumerate(_other_chips(x, y)):
            out.append((s_ref.at[2 * px + py], d_ref.at[k], (px, py, c)))
    return out


def _join_copies(srcs, dsts, x, y, c):
    out = []
    for s_ref in srcs:
        mine = _half_rows(s_ref, c)
        out.append((mine, mine, (x, y, 1 - c)))
    return out


def _exchange_start(copies_fn, n_copies, srcs, fresh, after, name):
    ns, nb = len(srcs), len(srcs) + len(fresh)

    def body(*refs):
        bufs, send, recv, token = refs[:nb], refs[nb + 1], refs[nb + 2], refs[-1]
        x, y, c = _mesh_pos()
        for i, (s_ref, d_ref, to) in enumerate(copies_fn(bufs[:ns], bufs[ns:] if fresh else bufs[:ns], x, y, c)):
            _remote(s_ref, d_ref, send, recv, i, to).start()
        token[...] = jnp.zeros_like(token)

    sems = [pltpu.SemaphoreType.DMA((n_copies,))] * 2
    outs = pl.pallas_call(
        body, name=name,
        in_specs=[HBM] * nb + [ANY], out_specs=[SEM, SEM] + [HBM] * nb + [pl.BlockSpec(memory_space=pltpu.VMEM)],
        out_shape=sems + _hbm_like(list(srcs) + list(fresh)) + [jax.ShapeDtypeStruct((8, 128), F32)],
        input_output_aliases={i: 2 + i for i in range(nb)},
        compiler_params=pltpu.CompilerParams(has_side_effects=DATAFLOW_EFFECT),
    )(*_in_hbm(list(srcs) + list(fresh)), after)
    return outs[0], outs[1], outs[2:2 + ns], outs[2 + ns:2 + nb], outs[-1]


def _exchange_done(copies_fn, srcs, fresh, send, recv, after, name):
    ns, nb = len(srcs), len(srcs) + len(fresh)

    def body(*refs):
        bufs, send_in, recv_in = refs[:nb], refs[nb], refs[nb + 1]
        x, y, c = _mesh_pos()
        for i, (s_ref, d_ref, to) in enumerate(copies_fn(bufs[:ns], bufs[ns:] if fresh else bufs[:ns], x, y, c)):
            came = _remote(s_ref, d_ref, send_in, recv_in, i, to)
            came.wait_send()
            came.wait_recv()

    outs = pl.pallas_call(
        body, name=name,
        in_specs=[HBM] * nb + [SEM, SEM, ANY], out_specs=[HBM] * nb,
        out_shape=_hbm_like(list(srcs) + list(fresh)),
        input_output_aliases={i: i for i in range(nb)},
        compiler_params=pltpu.CompilerParams(has_side_effects=DATAFLOW_EFFECT),
    )(*_in_hbm(list(srcs) + list(fresh)), send, recv, after)
    return outs[:ns], outs[ns:]


class _Reduce:
    def __init__(self, place, core, shards, mom_m, mom_v):
        self.place, self.core = place, core
        self.shards, self.mom_m, self.mom_v = shards, mom_m, mom_v
        self.state = {}
        self.results = {}

    def add(self, group, grads, after):
        names = list(grads)
        g4s = [g.reshape((N_CHIPS, -1, g.shape[-1])) if g.ndim == 2 else g for g in grads.values()]
        fresh = [lax.empty((N_CHIPS, g.shape[1] // 2, g.shape[2]), BF16) for g in g4s]
        send, recv, g4s, fresh, token = _exchange_start(_sibling_halves_copies, len(names), g4s, fresh, after,
                                                        "pair_start_" + group)
        self.state[group] = (0, names, send, recv, g4s, fresh)
        return token

    def send(self, group, theirs, after):
        names, srcs = list(theirs), list(theirs.values())
        fresh = [lax.empty(s.shape, BF16) for s in srcs]
        send, recv, srcs, fresh, token = _exchange_start(_to_sibling_copies, len(names), srcs, fresh, after, "pair_start_" + group)
        self.state[group] = ("sent", names, send, recv, srcs, fresh)
        return token

    def received(self, group, after):
        stage, names, send, recv, srcs, fresh = self.state.pop(group)
        assert stage == "sent"
        _, got = _exchange_done(_to_sibling_copies, srcs, fresh, send, recv, after, "pair_done_" + group)
        return dict(zip(names, got))

    def add_parts(self, group, parts):
        names, srcs = list(parts), list(parts.values())
        fresh = [lax.empty((N_CHIPS - 1,) + p.shape[1:], BF16) for p in srcs]
        send, recv, srcs, fresh, token = _exchange_start(_chip_copies, 3 * len(names), srcs, fresh, self.core, "chips_start_" + group)
        self.state[group] = (1, names, send, recv, srcs, fresh)
        return token

    def step(self, group, after):
        stage, names, send, recv, srcs, fresh = self.state[group]
        if stage == 0:
            g4s, ras = _exchange_done(_sibling_halves_copies, srcs, fresh, send, recv, after, "pair_done_" + group)
            parts = [_pair_sum(g, r, self.core, "pair_sum_" + n) for g, r, n in zip(g4s, ras, names)]
            fresh = [lax.empty((N_CHIPS - 1,) + p.shape[1:], BF16) for p in parts]
            send, recv, parts, fresh, token = _exchange_start(_chip_copies, 3 * len(names), parts, fresh, self.core,
                                                              "chips_start_" + group)
            self.state[group] = (1, names, send, recv, parts, fresh)
            return token
        if stage == 1:
            parts, rcs = _exchange_done(_chip_copies, srcs, fresh, send, recv, after, "chips_done_" + group)
            token = None
            for n, p, r in zip(names, parts, rcs):
                self.results[n] = _adamw_own_half(self.shards[n], self.mom_m[n], self.mom_v[n], p, r, self.place,
                                                  "adamw_own_" + n, after=token)
                token = self.results[n][2]
            wholes = [self.results[n][0] for n in names]
            send, recv, wholes, _, token = _exchange_start(_join_copies, len(names), wholes, [], token, "join_start_" + group)
            self.state[group] = (2, names, send, recv, wholes, [])
            return token
        assert stage == 2
        wholes, _ = _exchange_done(_join_copies, srcs, [], send, recv, after, "join_done_" + group)
        token = None
        for n, exchanged in zip(names, wholes):
            _, g, d, nm, nv = self.results[n]
            self.results[n] = _adamw_other_half(self.shards[n], self.mom_m[n], self.mom_v[n], exchanged, g, d, nm, nv,
                                                self.place, "adamw_other_" + n, after=token)
            token = self.results[n][1]
        del self.state[group]
        return token


N_DEV = 8


def _all_reduce_small(v):
    def body(v_ref, o_ref, slots, send_sems, recv_sems):
        x, y, c = _mesh_pos()
        me = 4 * x + 2 * y + c
        slots[me] = v_ref[...]
        peers = []
        for r in range(1, N_DEV):
            fx, fy, fc = (r >> 2) & 1, (r >> 1) & 1, r & 1
            peers.append((x + fx - 2 * x * fx, y + fy - 2 * y * fy, c + fc - 2 * c * fc))
        sends = []
        for r, peer in enumerate(peers):
            cp = _remote(v_ref, slots.at[me], send_sems, recv_sems, r, peer)
            cp.start()
            sends.append(cp)
        for r, (px, py, pc) in enumerate(peers):
            landed = slots.at[4 * px + 2 * py + pc]
            _remote(landed, landed, send_sems, recv_sems, r, (px, py, pc)).wait_recv()
        for cp in sends:
            cp.wait_send()
        acc = slots[0]
        for i in range(1, N_DEV):
            acc = acc + slots[i]
        o_ref[...] = acc

    vm = pl.BlockSpec(memory_space=pltpu.VMEM)
    return pl.pallas_call(
        body, name="small_grads_all_reduce",
        in_specs=[vm], out_specs=vm,
        out_shape=jax.ShapeDtypeStruct(v.shape, v.dtype),
        scratch_shapes=[pltpu.VMEM((N_DEV,) + v.shape, v.dtype), pltpu.SemaphoreType.DMA((N_DEV - 1,)),
                        pltpu.SemaphoreType.DMA((N_DEV - 1,))],
    )(v)


MATRICES = ("w_in", "w_attn_out", "w_conv_out", "w_o", "w_cq", "w_ckv", "w_co", "w_gate", "w_up", "w_down")
VECTORS = ("g_mix", "b_gate", "g_cross", "g_mem", "g_ffn", "g_final", "conv_w", "sink")
WEIGHT_ORDER = ("g_mix", "w_in", "sink", "conv_w", "b_gate", "w_attn_out", "w_conv_out", "w_o", "g_cross", "g_mem", "w_cq",
                "w_ckv", "w_co", "g_ffn", "w_gate", "w_up", "w_down", "g_final")
CONV_PAD_ROWS = 32
SMALL_ROWS = 8


def _pack(pieces):
    flat = jnp.concatenate([p.reshape(-1) for p in pieces])
    lane_group = SMALL_ROWS * 128
    total = -(-flat.shape[0] // lane_group) * lane_group
    flat = jnp.pad(flat, (0, total - flat.shape[0]))
    return flat.reshape(SMALL_ROWS, total // SMALL_ROWS), [p.size for p in pieces]


def _unpack(packed, pieces):
    flat = packed.reshape(-1)
    out, off = [], 0
    for p in pieces:
        out.append(flat[off:off + p.size].reshape(p.shape))
        off += p.size
    return out


def kernel(x, mem, g_mix, w_in, sink, conv_w, b_gate, w_attn_out, w_conv_out, w_o, g_cross, g_mem, w_cq, w_ckv, w_co, g_ffn, w_gate, w_up, w_down, g_final, loss_target, m_g_mix, m_w_in, m_sink, m_conv_w, m_b_gate, m_w_attn_out, m_w_conv_out, m_w_o, m_g_cross, m_g_mem, m_w_cq, m_w_ckv, m_w_co, m_g_ffn, m_w_gate, m_w_up, m_w_down, m_g_final, v_g_mix, v_w_in, v_sink, v_conv_w, v_b_gate, v_w_attn_out, v_w_conv_out, v_w_o, v_g_cross, v_g_mem, v_w_cq, v_w_ckv, v_w_co, v_g_ffn, v_w_gate, v_w_up, v_w_down, v_g_final):
    given = dict(g_mix=g_mix, w_in=w_in, sink=sink, conv_w=conv_w, b_gate=b_gate, w_attn_out=w_attn_out, w_conv_out=w_conv_out,
                 w_o=w_o, g_cross=g_cross, g_mem=g_mem, w_cq=w_cq, w_ckv=w_ckv, w_co=w_co, g_ffn=g_ffn, w_gate=w_gate, w_up=w_up,
                 w_down=w_down, g_final=g_final)
    mom_m = dict(g_mix=m_g_mix, w_in=m_w_in, sink=m_sink, conv_w=m_conv_w, b_gate=m_b_gate, w_attn_out=m_w_attn_out,
                 w_conv_out=m_w_conv_out, w_o=m_w_o, g_cross=m_g_cross, g_mem=m_g_mem, w_cq=m_w_cq, w_ckv=m_w_ckv, w_co=m_w_co,
                 g_ffn=m_g_ffn, w_gate=m_w_gate, w_up=m_w_up, w_down=m_w_down, g_final=m_g_final)
    mom_v = dict(g_mix=v_g_mix, w_in=v_w_in, sink=v_sink, conv_w=v_conv_w, b_gate=v_b_gate, w_attn_out=v_w_attn_out,
                 w_conv_out=v_w_conv_out, w_o=v_w_o, g_cross=v_g_cross, g_mem=v_g_mem, w_cq=v_w_cq, w_ckv=v_w_ckv, w_co=v_w_co,
                 g_ffn=v_g_ffn, w_gate=v_w_gate, w_up=v_w_up, w_down=v_w_down, g_final=v_g_final)
    xs, mems, target = x[0], mem[0], loss_target[0]
    d_model = xs.shape[1]
    chip = 2 * lax.axis_index("x") + lax.axis_index("y")
    core = jnp.reshape(lax.axis_index("c"), (1,)).astype(jnp.int32)
    place = jnp.stack([chip, lax.axis_index("c")]).astype(jnp.int32)

    shards = {n: given[n][0] for n in MATRICES}
    conv_cols = conv_w.shape[2]
    conv_pad = jnp.pad(conv_w[0], ((0, CONV_PAD_ROWS - conv_w.shape[1]), (0, 0)))
    fetch = _Gather(GATHER_GROUPS)
    first = {"w_in": _cast_to_slot(shards["w_in"], place, BF16, "to_slot_w_in"),
             "conv_w": _cast_to_slot(conv_pad, place, F32, "to_slot_conv_w")}
    fetch.put(first)
    tok = fetch.step("gather_start", [], [("direct", "in")])
    fetch.put({n: _cast_to_slot(shards[n], place, BF16, "to_slot_" + n, after=tok) for n in MATRICES if n != "w_in"})
    small = {n: given[n] for n in ("g_mix", "b_gate", "g_cross", "g_mem", "g_ffn")}
    small["g_final"] = g_final[None]
    small["sink"] = sink[0]

    reduce = _Reduce(place, core, shards, {n: mom_m[n][0] for n in MATRICES}, {n: mom_v[n][0] for n in MATRICES})
    sq, grad_x, small_grads = _local_step(xs, mems, target, small, fetch, reduce)

    loss_part = 0.5 * sq[0:1, 0:1] / d_model
    pieces = [small_grads[n] for n in VECTORS] + [loss_part]
    packed, _ = _pack(pieces)
    summed = _unpack(_all_reduce_small(packed), pieces)
    loss = summed[-1][0, 0]
    small_sum = dict(zip(VECTORS, summed[:-1]))
    small_sum["conv_w"] = lax.dynamic_slice_in_dim(small_sum["conv_w"], chip * conv_cols, conv_cols, axis=1)

    grad_out, delta, new_m, new_v = {}, {}, {}, {}
    like = [given[n] for n in VECTORS]
    pw, _ = _pack(like)
    pg, _ = _pack([small_sum[n] for n in VECTORS])
    pm, _ = _pack([mom_m[n] for n in VECTORS])
    pv, _ = _pack([mom_v[n] for n in VECTORS])
    tok = reduce.step("in", pg)
    _, pd, pnm, pnv = _adamw(pw, pg, pm, pv, "adamw_small", after=tok)
    for n, g, d, nm, nv in zip(VECTORS, [small_sum[n] for n in VECTORS], _unpack(pd, like), _unpack(pnm, like), _unpack(pnv, like)):
        grad_out[n] = g.reshape(given[n].shape)
        delta[n], new_m[n], new_v[n] = d, nm, nv
    reduce.step("in", pd)
    for n in MATRICES:
        g, d, nm, nv = reduce.results[n]
        grad_out[n], delta[n], new_m[n], new_v[n] = g[None], d[None], nm[None], nv[None]

    return (loss, grad_x[None], *[grad_out[n] for n in WEIGHT_ORDER], *[delta[n] for n in WEIGHT_ORDER],
            *[new_m[n] for n in WEIGHT_ORDER], *[new_v[n] for n in WEIGHT_ORDER])
```

```python
import functools

import jax
import jax.numpy as jnp
from jax import lax
from jax.experimental import pallas as pl
from jax.experimental.pallas import tpu as pltpu

F32 = jnp.float32
BF16 = jnp.bfloat16
MESH = pl.DeviceIdType.MESH
ANY = pl.BlockSpec(memory_space=pl.ANY)

VMEM_LIMIT_BYTES = 56 * 1024 * 1024

N_CHIPS = 4
HEAD_DIM = 128
N_Q_HEADS = 8
N_KV_HEADS = 2
Q_GROUP = N_Q_HEADS // N_KV_HEADS
ATTN_WIDTH = N_Q_HEADS * HEAD_DIM
KV_WIDTH = N_KV_HEADS * HEAD_DIM
WINDOW = 128
BLOCK = 128
BAND = 3 * BLOCK
ROPE_THETA = 10000.0
CONV_WIDTH = 1024
MEM_HEADS = 4
MEM_WIDTH = MEM_HEADS * HEAD_DIM
RMS_EPS = 1e-6
NEG_INF = -1e30
ATTN_SCALE = HEAD_DIM ** -0.5

Q_OFF, K_OFF, V_OFF, CU_OFF, CB_OFF, CC_OFF, GL_OFF = 0, 1024, 1280, 1536, 2560, 3584, 4608

ADAM_LR = 0.001
ADAM_B1 = 0.9
ADAM_B2 = 0.999
ADAM_EPS = 1e-08
ADAM_WD = 0.01
ADAM_STEP = 10
ADAM_C1 = 1.0 - ADAM_B1 ** ADAM_STEP
ADAM_C2 = 1.0 - ADAM_B2 ** ADAM_STEP


def _params(n_grid_axes):
    return pltpu.CompilerParams(dimension_semantics=("arbitrary",) * n_grid_axes, vmem_limit_bytes=VMEM_LIMIT_BYTES)


BF16_SUBLANES = 16


def _row_tile(rows, want):
    if rows <= want:
        return rows
    for t in range(want, 0, -BF16_SUBLANES):
        if rows % t == 0:
            return t
    return rows


def _matmul(a, b, *, mode, tm, tn, tk, out_dtypes, name, extras=(), epilogue=None, b_blocks=1, out_blocks=1, after=None,
            a_norm_gain=None):
    if mode == "tn":
        kdim, m = a.shape
    else:
        m, kdim = a.shape
    if b_blocks > 1:
        nb, brows, bcols = b.shape
        assert nb == b_blocks
        if mode == "nn":
            n = bcols * nb
            assert brows == kdim
        else:
            assert mode == "nt" and bcols * nb == kdim
            n = brows
    else:
        n = b.shape[0] if mode == "nt" else b.shape[1]
    tm, tn = min(tm, m), min(tn, n)
    assert m % tm == 0 and n % tn == 0 and tk == kdim, (name, m, n, kdim, tm, tn, tk)
    n_extra, n_out = len(extras), len(out_dtypes)
    n_after = 0 if after is None else 1
    normed = a_norm_gain is not None
    assert not normed or mode != "tn"

    if mode == "tn":
        a_spec = pl.BlockSpec((tk, tm), lambda j, i, k: (k, i))
        dims = (((0,), (0,)), ((), ()))
    else:
        a_spec = pl.BlockSpec((tm, tk), lambda j, i, k: (i, k))
        dims = (((1,), (0,)), ((), ())) if mode == "nn" else (((1,), (1,)), ((), ()))

    if b_blocks > 1 and mode == "nn":
        per = b.shape[2] // tn
        assert b.shape[2] % tn == 0
        b_spec = pl.BlockSpec((None, tk, tn), lambda j, i, k: (j // per, k, j % per))
    elif b_blocks > 1:
        b_spec = pl.BlockSpec((b_blocks, tn, b.shape[2]), lambda j, i, k: (0, j, 0))
    elif mode == "nt":
        b_spec = pl.BlockSpec((tn, tk), lambda j, i, k: (j, k))
    else:
        b_spec = pl.BlockSpec((tk, tn), lambda j, i, k: (k, j))

    tile_spec = pl.BlockSpec((tm, tn), lambda j, i, k: (i, j))
    if out_blocks > 1:
        ncols = n // out_blocks
        assert ncols % tn == 0
        oper = ncols // tn
        out_spec = pl.BlockSpec((None, tm, tn), lambda j, i, k: (j // oper, i, j % oper))
        out_shape = [jax.ShapeDtypeStruct((out_blocks, m, ncols), dt) for dt in out_dtypes]
    else:
        out_spec = tile_spec
        out_shape = [jax.ShapeDtypeStruct((m, n), dt) for dt in out_dtypes]

    def body(a_ref, b_ref, *rest):
        extra_refs = rest[:n_extra]
        out_refs = rest[n_extra + n_after + normed:n_extra + n_after + normed + n_out]
        if normed:
            xv = a_ref[...]
            a_ref = rest[-1]
            a_ref[...] = (xv * _rstd(xv) * rest[n_extra + n_after][...]).astype(BF16)
        if mode == "nt" and b_blocks > 1:
            cs = b.shape[2]
            acc = None
            for jb in range(b_blocks):
                prod = lax.dot_general(a_ref[:, jb * cs:(jb + 1) * cs].astype(BF16), b_ref[jb].astype(BF16), dims,
                                       preferred_element_type=F32)
                acc = prod if acc is None else acc + prod
        else:
            acc = lax.dot_general(a_ref[...].astype(BF16), b_ref[...].astype(BF16), dims, preferred_element_type=F32)
        tiles = (acc,) if epilogue is None else epilogue(acc, *[r[...] for r in extra_refs])
        for o_ref, t in zip(out_refs, tiles, strict=True):
            o_ref[...] = t.astype(o_ref.dtype)

    outs = pl.pallas_call(
        body,
        name=name,
        grid=(n // tn, m // tm, 1),
        in_specs=[a_spec, b_spec] + [tile_spec] * n_extra + [ANY] * n_after
        + ([pl.BlockSpec((1, kdim), lambda j, i, k: (0, 0))] if normed else []),
        out_specs=[out_spec] * n_out + ([a_spec] if normed else []),
        out_shape=out_shape + ([jax.ShapeDtypeStruct((m, kdim), BF16)] if normed else []),
        compiler_params=_params(3),
    )(a, b, *extras, *([] if after is None else [after]), *([a_norm_gain] if normed else []))
    return outs[0] if len(outs) == 1 else outs


def _add_residual(acc, res):
    return (acc + res,)


def _matmul_column_blocks(a, b4, blocks, out, *, tm, name, after=None, norm_gain=None):
    m, kdim = a.shape
    nb, _, cols = b4.shape
    tm = min(tm, m)
    assert m % tm == 0
    n_blocks = blocks.shape[0]
    normed = norm_gain is not None
    assert not normed or (n_blocks == 1 and out is None)

    def body(j_ref, a_ref, b_ref, *rest):
        if normed:
            xv = a_ref[...]
            hv = (xv * _rstd(xv) * rest[0][...]).astype(BF16)
            rest[-1][...] = hv
            rest[-2][...] = jnp.dot(hv, b_ref[...], preferred_element_type=F32)
        else:
            rest[-1][...] = jnp.dot(a_ref[...], b_ref[...], preferred_element_type=F32)

    extra = ([] if out is None else [out]) + ([] if after is None else [after])
    z_spec = pl.BlockSpec((tm, cols), lambda j, i, blk: (i, blk[j]))
    z_shape = jax.ShapeDtypeStruct((m, nb * cols), F32)
    return pl.pallas_call(
        body, name=name,
        grid_spec=pltpu.PrefetchScalarGridSpec(
            num_scalar_prefetch=1, grid=(n_blocks, m // tm),
            in_specs=[pl.BlockSpec((tm, kdim), lambda j, i, blk: (i, 0)),
                      pl.BlockSpec((None, kdim, cols), lambda j, i, blk: (blk[j], 0, 0))]
            + ([pl.BlockSpec((1, kdim), lambda j, i, blk: (0, 0))] if normed else []) + [ANY] * len(extra),
            out_specs=[z_spec, pl.BlockSpec((tm, kdim), lambda j, i, blk: (i, 0))] if normed else z_spec),
        out_shape=[z_shape, jax.ShapeDtypeStruct((m, kdim), BF16)] if normed else z_shape,
        input_output_aliases={} if out is None else {3: 0},
        compiler_params=_params(2),
    )(blocks, a, b4, *([norm_gain] if normed else []), *extra)


def _wgrad_half(a, b, core, *, theirs, row_sharded, tm, tn, name, add=None, after=None):
    kdim, m = a.shape
    n = b.shape[1]
    rs, cs = (m // N_CHIPS, n) if row_sharded else (m, n // N_CHIPS)
    rh = rs // 2
    tm, tn = min(tm, rh), min(tn, cs)
    assert rh % tm == 0 and cs % tn == 0, (name, rh, cs, tm, tn)
    mh, per = rh // tm, cs // tn
    has_add = add is not None

    def half(c):
        return 1 - c[0] if theirs else c[0]

    if row_sharded:
        grid = (n // tn, N_CHIPS * mh)
        a_spec = pl.BlockSpec((kdim, tm), lambda j, r, c: (0, ((r // mh) * 2 + half(c)) * mh + r % mh))
        o_spec = pl.BlockSpec((None, tm, tn), lambda j, r, c: (r // mh, r % mh, j))
    else:
        grid = (n // tn, mh)
        a_spec = pl.BlockSpec((kdim, tm), lambda j, r, c: (0, half(c) * mh + r))
        o_spec = pl.BlockSpec((None, tm, tn), lambda j, r, c: (j // per, r, j % per))
    b_spec = pl.BlockSpec((kdim, tn), lambda j, r, c: (0, j))

    def body(c_ref, a_ref, b_ref, *rest):
        o_ref = rest[-1]
        acc = lax.dot_general(a_ref[...].astype(BF16), b_ref[...].astype(BF16), (((0,), (0,)), ((), ())),
                              preferred_element_type=F32)
        if has_add:
            acc = acc + rest[0][...].astype(F32)
        o_ref[...] = acc.astype(BF16)

    operands = [a, b] + ([add] if has_add else []) + ([] if after is None else [after])
    return pl.pallas_call(
        body, name=name,
        grid_spec=pltpu.PrefetchScalarGridSpec(
            num_scalar_prefetch=1, grid=grid,
            in_specs=[a_spec, b_spec] + ([o_spec] if has_add else []) + ([] if after is None else [ANY]),
            out_specs=o_spec),
        out_shape=jax.ShapeDtypeStruct((N_CHIPS, rh, cs), BF16),
        compiler_params=_params(2),
    )(core, *operands)


def _rstd(x):
    return lax.rsqrt(jnp.mean(x * x, axis=-1, keepdims=True) + RMS_EPS)


def _rmsnorm(x, g, name):
    s, d = x.shape
    tr = _row_tile(s, 256)

    def body(x_ref, g_ref, o_ref):
        xv = x_ref[...]
        o_ref[...] = (xv * _rstd(xv) * g_ref[...]).astype(BF16)

    return pl.pallas_call(
        body, name=name, grid=(s // tr,),
        in_specs=[pl.BlockSpec((tr, d), lambda i: (i, 0)), pl.BlockSpec((1, d), lambda i: (0, 0))],
        out_specs=pl.BlockSpec((tr, d), lambda i: (i, 0)),
        out_shape=jax.ShapeDtypeStruct((s, d), BF16),
        compiler_params=_params(1),
    )(x, g)


def _rmsnorm_bwd(dh, x, g, dres, name):
    s, d = x.shape
    tr = _row_tile(s, 256)
    has_res = dres is not None

    def body(*refs):
        if has_res:
            dh_ref, x_ref, g_ref, res_ref, dx_ref, dxb_ref, dg_ref = refs
        else:
            dh_ref, x_ref, g_ref, dx_ref, dxb_ref, dg_ref = refs
        xv = x_ref[...]
        dhv = dh_ref[...].astype(F32)
        r = _rstd(xv)
        xn = xv * r
        dhg = dhv * g_ref[...]
        dx = r * (dhg - xn * jnp.mean(dhg * xn, axis=-1, keepdims=True))
        if has_res:
            dx = dx + res_ref[...]
        dx_ref[...] = dx
        dxb_ref[...] = dx.astype(BF16)
        part = jnp.sum(dhv * xn, axis=0, keepdims=True)

        @pl.when(pl.program_id(0) == 0)
        def _():
            dg_ref[...] = part

        @pl.when(pl.program_id(0) > 0)
        def _():
            dg_ref[...] += part

    row = pl.BlockSpec((tr, d), lambda i: (i, 0))
    vec = pl.BlockSpec((1, d), lambda i: (0, 0))
    return pl.pallas_call(
        body, name=name, grid=(s // tr,),
        in_specs=[row, row, vec] + ([row] if has_res else []),
        out_specs=[row, row, vec],
        out_shape=[jax.ShapeDtypeStruct((s, d), F32), jax.ShapeDtypeStruct((s, d), BF16), jax.ShapeDtypeStruct((1, d), F32)],
        compiler_params=_params(1),
    )(*([dh, x, g] + ([dres] if has_res else [])))


def _loss_head(x3, g, target):
    s, d = x3.shape
    tr = _row_tile(s, 256)

    def body(x_ref, g_ref, t_ref, dx_ref, dxb_ref, sq_ref, dg_ref):
        xv = x_ref[...]
        gv = g_ref[...]
        r = _rstd(xv)
        xn = xv * r
        err = xn * gv - t_ref[...]
        dy = err * (1.0 / d)
        dyg = dy * gv
        dx = r * (dyg - xn * jnp.mean(dyg * xn, axis=-1, keepdims=True))
        dx_ref[...] = dx
        dxb_ref[...] = dx.astype(BF16)
        sq = jnp.sum(jnp.sum(err * err, axis=1, keepdims=True), axis=0, keepdims=True)
        sq = jnp.broadcast_to(sq, (1, 128))
        part = jnp.sum(dy * xn, axis=0, keepdims=True)

        @pl.when(pl.program_id(0) == 0)
        def _():
            sq_ref[...] = sq
            dg_ref[...] = part

        @pl.when(pl.program_id(0) > 0)
        def _():
            sq_ref[...] += sq
            dg_ref[...] += part

    row = pl.BlockSpec((tr, d), lambda i: (i, 0))
    vec = pl.BlockSpec((1, d), lambda i: (0, 0))
    return pl.pallas_call(
        body, name="loss_head", grid=(s // tr,),
        in_specs=[row, vec, row],
        out_specs=[row, row, pl.BlockSpec((1, 128), lambda i: (0, 0)), vec],
        out_shape=[jax.ShapeDtypeStruct((s, d), F32), jax.ShapeDtypeStruct((s, d), BF16),
                   jax.ShapeDtypeStruct((1, 128), F32), jax.ShapeDtypeStruct((1, d), F32)],
        compiler_params=_params(1),
    )(x3, g, target)


def _rope_tables(s):
    inv = 1.0 / (ROPE_THETA ** (jnp.arange(0, HEAD_DIM, 2, dtype=F32) / HEAD_DIM))
    ang = jnp.arange(s, dtype=F32)[:, None] * inv[None, :]
    cos, sin = jnp.cos(ang), jnp.sin(ang)
    return jnp.concatenate([cos, cos], axis=1), jnp.concatenate([-sin, sin], axis=1)


def _swap_halves(t):
    return pltpu.roll(t, HEAD_DIM // 2, 1)


def _rope_fwd(z, cos_t, sin_t, after=None):
    s = z.shape[0]
    tr = _row_tile(s, 256)

    def body(zq_ref, zk_ref, zv_ref, c_ref, s_ref, *rest):
        q_ref, k_ref, v_ref = rest[-3:]
        c, sn = c_ref[...], s_ref[...]
        for hd in range(N_Q_HEADS):
            cols = slice(hd * HEAD_DIM, (hd + 1) * HEAD_DIM)
            t = zq_ref[:, cols]
            q_ref[:, cols] = (t * c + _swap_halves(t) * sn).astype(BF16)
        for hd in range(N_KV_HEADS):
            cols = slice(hd * HEAD_DIM, (hd + 1) * HEAD_DIM)
            t = zk_ref[:, cols]
            k_ref[:, cols] = (t * c + _swap_halves(t) * sn).astype(BF16)
        v_ref[...] = zv_ref[...].astype(BF16)

    tab = pl.BlockSpec((tr, HEAD_DIM), lambda i: (i, 0))
    return pl.pallas_call(
        body, name="rope_fwd", grid=(s // tr,),
        in_specs=[pl.BlockSpec((tr, ATTN_WIDTH), lambda i: (i, Q_OFF // ATTN_WIDTH)),
                  pl.BlockSpec((tr, KV_WIDTH), lambda i: (i, K_OFF // KV_WIDTH)),
                  pl.BlockSpec((tr, KV_WIDTH), lambda i: (i, V_OFF // KV_WIDTH)), tab, tab] + ([] if after is None else [ANY]),
        out_specs=[pl.BlockSpec((tr, ATTN_WIDTH), lambda i: (i, 0)), pl.BlockSpec((tr, KV_WIDTH), lambda i: (i, 0)),
                   pl.BlockSpec((tr, KV_WIDTH), lambda i: (i, 0))],
        out_shape=[jax.ShapeDtypeStruct((s, ATTN_WIDTH), BF16), jax.ShapeDtypeStruct((s, KV_WIDTH), BF16),
                   jax.ShapeDtypeStruct((s, KV_WIDTH), BF16)],
        compiler_params=_params(1),
    )(z, z, z, cos_t, sin_t, *([] if after is None else [after]))


def _rope_bwd(dq_rot, dk_rot, dv, cos_t, sin_t, dz):
    s = dq_rot.shape[0]
    tr = _row_tile(s, 256)
    qkv_width = V_OFF + KV_WIDTH

    def body(dq_ref, dk_ref, dv_ref, c_ref, s_ref, dz_in_ref, o_ref):
        c, sn = c_ref[...], s_ref[...]
        for hd in range(N_Q_HEADS):
            t = dq_ref[:, hd * HEAD_DIM:(hd + 1) * HEAD_DIM]
            o_ref[:, Q_OFF + hd * HEAD_DIM:Q_OFF + (hd + 1) * HEAD_DIM] = (t * c + _swap_halves(t * sn)).astype(BF16)
        for hd in range(N_KV_HEADS):
            t = dk_ref[:, hd * HEAD_DIM:(hd + 1) * HEAD_DIM]
            o_ref[:, K_OFF + hd * HEAD_DIM:K_OFF + (hd + 1) * HEAD_DIM] = (t * c + _swap_halves(t * sn)).astype(BF16)
        o_ref[:, V_OFF:V_OFF + KV_WIDTH] = dv_ref[...].astype(BF16)

    tab = pl.BlockSpec((tr, HEAD_DIM), lambda i: (i, 0))
    wide = pl.BlockSpec((tr, ATTN_WIDTH), lambda i: (i, 0))
    narrow = pl.BlockSpec((tr, KV_WIDTH), lambda i: (i, 0))
    return pl.pallas_call(
        body, name="rope_bwd", grid=(s // tr,),
        in_specs=[wide, narrow, narrow, tab, tab, ANY],
        out_specs=pl.BlockSpec((tr, qkv_width), lambda i: (i, 0)),
        out_shape=jax.ShapeDtypeStruct(dz.shape, dz.dtype),
        input_output_aliases={5: 0},
        compiler_params=_params(1),
    )(dq_rot, dk_rot, dv, cos_t, sin_t, dz)


def _swa_band(i, s):
    return pl.multiple_of(jnp.clip((i - 1) * BLOCK, 0, s - BAND), BLOCK)


SWA_HEADS_PER_PASS = Q_GROUP


def _swa_probs(q_ref, k_ref, sink_ref, heads, start, valid):
    kv = heads[0] // Q_GROUP
    cols = slice(kv * HEAD_DIM, (kv + 1) * HEAD_DIM)
    kb = k_ref[pl.ds(start, BAND), cols]
    qg = jnp.concatenate([q_ref[:, hd * HEAD_DIM:(hd + 1) * HEAD_DIM] for hd in heads], axis=0)
    sc = lax.dot_general(qg, kb, (((1,), (1,)), ((), ())), preferred_element_type=F32) * ATTN_SCALE
    sc = jnp.where(valid, sc, NEG_INF)
    sk = jnp.concatenate([jnp.full((BLOCK, 1), sink_ref[hd], F32) for hd in heads], axis=0)
    mx = jnp.maximum(jnp.max(sc, axis=1, keepdims=True), sk)
    e = jnp.exp(sc - mx)
    es = jnp.exp(sk - mx)
    inv = 1.0 / (jnp.sum(e, axis=1, keepdims=True) + es)
    return qg, kb, e * inv, es * inv


def _swa_head_passes():
    return [list(range(h0, h0 + SWA_HEADS_PER_PASS)) for h0 in range(0, N_Q_HEADS, SWA_HEADS_PER_PASS)]


def _swa_valid(i, start):
    q_pos = i * BLOCK + lax.broadcasted_iota(jnp.int32, (BLOCK, 1), 0)
    q_pos = jnp.concatenate([q_pos] * SWA_HEADS_PER_PASS, axis=0)
    k_pos = start + lax.broadcasted_iota(jnp.int32, (1, BAND), 1)
    return jnp.abs(k_pos - q_pos) <= WINDOW


def _swa_fwd(q, k, v, sink):
    s = q.shape[0]
    assert s % BLOCK == 0 and s >= BAND

    def body(sink_ref, q_ref, k_ref, v_ref, o_ref):
        i = pl.program_id(0)
        start = _swa_band(i, s)
        valid = _swa_valid(i, start)
        for heads in _swa_head_passes():
            kv = heads[0] // Q_GROUP
            _, _, p, _ = _swa_probs(q_ref, k_ref, sink_ref, heads, start, valid)
            vb = v_ref[pl.ds(start, BAND), kv * HEAD_DIM:(kv + 1) * HEAD_DIM]
            o = jnp.dot(p.astype(BF16), vb, preferred_element_type=F32)
            for g, hd in enumerate(heads):
                o_ref[:, hd * HEAD_DIM:(hd + 1) * HEAD_DIM] = o[g * BLOCK:(g + 1) * BLOCK].astype(BF16)

    whole = pl.BlockSpec((s, KV_WIDTH), lambda i: (0, 0))
    blk = pl.BlockSpec((BLOCK, ATTN_WIDTH), lambda i: (i, 0))
    return pl.pallas_call(
        body, name="swa_fwd", grid=(s // BLOCK,),
        in_specs=[pl.BlockSpec(memory_space=pltpu.SMEM), blk, whole, whole],
        out_specs=blk,
        out_shape=jax.ShapeDtypeStruct((s, ATTN_WIDTH), BF16),
        compiler_params=_params(1),
    )(sink, q, k, v)


def _swa_bwd(q, k, v, d_out, sink):
    s = q.shape[0]

    def body(sink_ref, q_ref, k_ref, v_ref, do_ref, dq_ref, dk_ref, dv_ref, dsink_ref):
        i = pl.program_id(0)

        @pl.when(i == 0)
        def _():
            dk_ref[...] = jnp.zeros_like(dk_ref)
            dv_ref[...] = jnp.zeros_like(dv_ref)
            dsink_ref[...] = jnp.zeros_like(dsink_ref)

        start = _swa_band(i, s)
        valid = _swa_valid(i, start)
        for heads in _swa_head_passes():
            kv = heads[0] // Q_GROUP
            cols = slice(kv * HEAD_DIM, (kv + 1) * HEAD_DIM)
            qg, kb, p, p_sink = _swa_probs(q_ref, k_ref, sink_ref, heads, start, valid)
            vb = v_ref[pl.ds(start, BAND), cols]
            dog = jnp.concatenate([do_ref[:, hd * HEAD_DIM:(hd + 1) * HEAD_DIM] for hd in heads], axis=0)
            dp = lax.dot_general(dog, vb, (((1,), (1,)), ((), ())), preferred_element_type=F32)
            delta = jnp.sum(p * dp, axis=1, keepdims=True)
            ds = (p * (dp - delta) * ATTN_SCALE).astype(BF16)
            dqg = jnp.dot(ds, kb, preferred_element_type=F32)
            dk_ref[pl.ds(start, BAND), cols] += lax.dot_general(ds, qg, (((0,), (0,)), ((), ())), preferred_element_type=F32)
            dv_ref[pl.ds(start, BAND), cols] += lax.dot_general(p.astype(BF16), dog, (((0,), (0,)), ((), ())),
                                                                 preferred_element_type=F32)
            dsk = p_sink * delta
            for g, hd in enumerate(heads):
                dq_ref[:, hd * HEAD_DIM:(hd + 1) * HEAD_DIM] = dqg[g * BLOCK:(g + 1) * BLOCK]
                tot = jnp.sum(dsk[g * BLOCK:(g + 1) * BLOCK], axis=0, keepdims=True)
                dsink_ref[hd:hd + 1, :] -= jnp.broadcast_to(tot, (1, 128))

    whole = pl.BlockSpec((s, KV_WIDTH), lambda i: (0, 0))
    blk = pl.BlockSpec((BLOCK, ATTN_WIDTH), lambda i: (i, 0))
    return pl.pallas_call(
        body, name="swa_bwd", grid=(s // BLOCK,),
        in_specs=[pl.BlockSpec(memory_space=pltpu.SMEM), blk, whole, whole, blk],
        out_specs=[blk, whole, whole, pl.BlockSpec((N_Q_HEADS, 128), lambda i: (0, 0))],
        out_shape=[jax.ShapeDtypeStruct((s, ATTN_WIDTH), F32), jax.ShapeDtypeStruct((s, KV_WIDTH), F32),
                   jax.ShapeDtypeStruct((s, KV_WIDTH), F32), jax.ShapeDtypeStruct((N_Q_HEADS, 128), F32)],
        compiler_params=_params(1),
    )(sink, q, k, v, d_out)


CONV_CHUNK = 256


def _shift_rows(t, rows, down):
    n = t.shape[0]
    rolled = pltpu.roll(t, 1 if down else n - 1, 0)
    edge = 0 if down else n - 1
    return jnp.where(rows == edge, 0.0, rolled)


def _conv_specs(s):
    def z_spec(off):
        return pl.BlockSpec((s, CONV_CHUNK), lambda j, off=off: (0, off // CONV_CHUNK + j))
    chunk = pl.BlockSpec((s, CONV_CHUNK), lambda j: (0, j))
    w_spec = pl.BlockSpec((3, CONV_CHUNK), lambda j: (0, j))
    return z_spec(CU_OFF), z_spec(CB_OFF), z_spec(CC_OFF), chunk, w_spec


def _conv_fwd(z, conv_w, after=None):
    s = z.shape[0]
    cu_spec, cb_spec, cc_spec, chunk, w_spec = _conv_specs(s)

    def body(cu_ref, cb_ref, cc_ref, w_ref, *rest):
        o_ref = rest[-1]
        rows = lax.broadcasted_iota(jnp.int32, (s, 1), 0)
        t = cc_ref[...] * cu_ref[...]
        c3 = _shift_rows(t, rows, True) * w_ref[0:1, :] + t * w_ref[1:2, :] + _shift_rows(t, rows, False) * w_ref[2:3, :]
        o_ref[...] = (cb_ref[...] * c3).astype(BF16)

    return pl.pallas_call(
        body, name="conv_fwd", grid=(CONV_WIDTH // CONV_CHUNK,),
        in_specs=[cu_spec, cb_spec, cc_spec, w_spec] + ([] if after is None else [ANY]),
        out_specs=chunk,
        out_shape=jax.ShapeDtypeStruct((s, CONV_WIDTH), BF16),
        compiler_params=_params(1),
    )(z, z, z, conv_w, *([] if after is None else [after]))


def _conv_bwd(z, conv_w, d_co, dz):
    s = z.shape[0]
    cu_spec, cb_spec, cc_spec, chunk, w_spec = _conv_specs(s)
    n_chunks = CONV_WIDTH // CONV_CHUNK
    offsets = (CU_OFF, CB_OFF, CC_OFF)

    def body(cu_ref, cb_ref, cc_ref, w_ref, d_ref, dz_in_ref, dz_ref, dw_ref, buf, sems):
        j = pl.program_id(0)

        def copies(j_at):
            return [pltpu.make_async_copy(buf.at[h], dz_ref.at[:, pl.ds(off + j_at * CONV_CHUNK, CONV_CHUNK)], sems.at[h])
                    for h, off in enumerate(offsets)]

        rows = lax.broadcasted_iota(jnp.int32, (s, 1), 0)
        cu, cc = cu_ref[...], cc_ref[...]
        t = cc * cu
        t_dn, t_up = _shift_rows(t, rows, True), _shift_rows(t, rows, False)
        c3 = t_dn * w_ref[0:1, :] + t * w_ref[1:2, :] + t_up * w_ref[2:3, :]
        d = d_ref[...]
        dc3 = d * cb_ref[...]
        dw_ref[0:1, :] = jnp.sum(dc3 * t_dn, axis=0, keepdims=True)
        dw_ref[1:2, :] = jnp.sum(dc3 * t, axis=0, keepdims=True)
        dw_ref[2:3, :] = jnp.sum(dc3 * t_up, axis=0, keepdims=True)
        dt = _shift_rows(dc3, rows, False) * w_ref[0:1, :] + dc3 * w_ref[1:2, :] + _shift_rows(dc3, rows, True) * w_ref[2:3, :]

        @pl.when(j > 0)
        def _():
            for cp in copies(j):
                cp.wait()

        buf[0] = (dt * cc).astype(BF16)
        buf[1] = (d * c3).astype(BF16)
        buf[2] = (dt * cu).astype(BF16)
        for cp in copies(j):
            cp.start()

        @pl.when(j == n_chunks - 1)
        def _():
            for cp in copies(j):
                cp.wait()

    return pl.pallas_call(
        body, name="conv_bwd", grid=(n_chunks,),
        in_specs=[cu_spec, cb_spec, cc_spec, w_spec, chunk, ANY],
        out_specs=[ANY, w_spec],
        out_shape=[jax.ShapeDtypeStruct(dz.shape, dz.dtype), jax.ShapeDtypeStruct((3, CONV_WIDTH), F32)],
        input_output_aliases={5: 0},
        scratch_shapes=[pltpu.VMEM((3, s, CONV_CHUNK), BF16), pltpu.SemaphoreType.DMA((3,))],
        compiler_params=_params(1),
    )(z, z, z, conv_w, d_co, dz)


GATE_CHUNK = 512


def _gate_specs(s, d, tr):
    n_chunks = d // GATE_CHUNK
    za = pl.BlockSpec((tr, GATE_CHUNK), lambda j, i: (i, GL_OFF // GATE_CHUNK + j))
    zc = pl.BlockSpec((tr, GATE_CHUNK), lambda j, i: (i, GL_OFF // GATE_CHUNK + n_chunks + j))
    ba = pl.BlockSpec((1, GATE_CHUNK), lambda j, i: (0, j))
    bc = pl.BlockSpec((1, GATE_CHUNK), lambda j, i: (0, n_chunks + j))
    tile = pl.BlockSpec((tr, GATE_CHUNK), lambda j, i: (i, j))
    return za, zc, ba, bc, tile


def _gate_fwd(z, b_gate, ya, yc):
    s, d = ya.shape
    tr = _row_tile(s, 512)
    za, zc, ba, bc, tile = _gate_specs(s, d, tr)

    def body(za_ref, zc_ref, ba_ref, bc_ref, ya_ref, yc_ref, o_ref):
        ga = jax.nn.sigmoid(za_ref[...] + ba_ref[...])
        gc = jax.nn.sigmoid(zc_ref[...] + bc_ref[...])
        o_ref[...] = (ga * ya_ref[...] + gc * yc_ref[...]).astype(BF16)

    return pl.pallas_call(
        body, name="gate_fwd", grid=(d // GATE_CHUNK, s // tr),
        in_specs=[za, zc, ba, bc, tile, tile],
        out_specs=tile,
        out_shape=jax.ShapeDtypeStruct((s, d), BF16),
        compiler_params=_params(2),
    )(z, z, b_gate, b_gate, ya, yc)


def _gate_bwd(z, b_gate, ya, yc, dmix):
    s, d = ya.shape
    tr = _row_tile(s, 512)
    za, zc, ba, bc, tile = _gate_specs(s, d, tr)
    vec = pl.BlockSpec((1, GATE_CHUNK), lambda j, i: (0, j))
    n_rows = s // tr
    in_width = z.shape[1]

    def body(za_ref, zc_ref, ba_ref, bc_ref, ya_ref, yc_ref, dm_ref, dya_ref, dyc_ref, dz_ref, dba_ref, dbc_ref, buf, sems):
        j, i = pl.program_id(0), pl.program_id(1)

        def copies(j_at, i_at):
            rows = pl.ds(i_at * tr, tr)
            return [pltpu.make_async_copy(buf.at[h], dz_ref.at[rows, pl.ds(GL_OFF + h * d + j_at * GATE_CHUNK, GATE_CHUNK)],
                                          sems.at[h]) for h in range(2)]

        ga = jax.nn.sigmoid(za_ref[...] + ba_ref[...])
        gc = jax.nn.sigmoid(zc_ref[...] + bc_ref[...])
        dm = dm_ref[...]
        dya_ref[...] = (dm * ga).astype(BF16)
        dyc_ref[...] = (dm * gc).astype(BF16)
        dla = dm * ya_ref[...] * ga * (1.0 - ga)
        dlc = dm * yc_ref[...] * gc * (1.0 - gc)

        @pl.when(j * n_rows + i > 0)
        def _():
            for cp in copies(j, i):
                cp.wait()

        buf[0] = dla.astype(BF16)
        buf[1] = dlc.astype(BF16)
        for cp in copies(j, i):
            cp.start()

        @pl.when((j == d // GATE_CHUNK - 1) & (i == n_rows - 1))
        def _():
            for cp in copies(j, i):
                cp.wait()

        pa = jnp.sum(dla, axis=0, keepdims=True)
        pc = jnp.sum(dlc, axis=0, keepdims=True)

        @pl.when(i == 0)
        def _():
            dba_ref[...] = pa
            dbc_ref[...] = pc

        @pl.when(i > 0)
        def _():
            dba_ref[...] += pa
            dbc_ref[...] += pc

    big = jax.ShapeDtypeStruct((s, d), BF16)
    small = jax.ShapeDtypeStruct((1, d), F32)
    return pl.pallas_call(
        body, name="gate_bwd", grid=(d // GATE_CHUNK, n_rows),
        in_specs=[za, zc, ba, bc, tile, tile, tile],
        out_specs=[tile, tile, ANY, vec, vec],
        out_shape=[big, big, jax.ShapeDtypeStruct((s, in_width), BF16), small, small],
        scratch_shapes=[pltpu.VMEM((2, tr, GATE_CHUNK), BF16), pltpu.SemaphoreType.DMA((2,))],
        compiler_params=_params(2),
    )(z, z, b_gate, b_gate, ya, yc, dmix)


def _cross_probs(q_ref, kv_ref, hd):
    cols = slice(hd * HEAD_DIM, (hd + 1) * HEAD_DIM)
    qh = q_ref[:, cols]
    kh = kv_ref[:, cols]
    sc = lax.dot_general(qh, kh, (((1,), (1,)), ((), ())), preferred_element_type=F32) * ATTN_SCALE
    e = jnp.exp(sc - jnp.max(sc, axis=1, keepdims=True))
    return qh, kh, e * (1.0 / jnp.sum(e, axis=1, keepdims=True))


def _cross_fwd(qc, kvc):
    s = qc.shape[0]
    n_mem = kvc.shape[0]
    tq = _row_tile(s, 256)

    def body(q_ref, kv_ref, o_ref):
        for hd in range(MEM_HEADS):
            _, _, p = _cross_probs(q_ref, kv_ref, hd)
            vh = kv_ref[:, MEM_WIDTH + hd * HEAD_DIM:MEM_WIDTH + (hd + 1) * HEAD_DIM]
            o_ref[:, hd * HEAD_DIM:(hd + 1) * HEAD_DIM] = jnp.dot(p.astype(BF16), vh, preferred_element_type=F32).astype(BF16)

    return pl.pallas_call(
        body, name="cross_fwd", grid=(s // tq,),
        in_specs=[pl.BlockSpec((tq, MEM_WIDTH), lambda i: (i, 0)), pl.BlockSpec((n_mem, 2 * MEM_WIDTH), lambda i: (0, 0))],
        out_specs=pl.BlockSpec((tq, MEM_WIDTH), lambda i: (i, 0)),
        out_shape=jax.ShapeDtypeStruct((s, MEM_WIDTH), BF16),
        compiler_params=_params(1),
    )(qc, kvc)


def _cross_bwd(qc, kvc, d_out):
    s = qc.shape[0]
    n_mem = kvc.shape[0]
    tq = _row_tile(s, 256)

    def body(q_ref, kv_ref, do_ref, dq_ref, dkv_ref):
        @pl.when(pl.program_id(0) == 0)
        def _():
            dkv_ref[...] = jnp.zeros_like(dkv_ref)

        for hd in range(MEM_HEADS):
            cols = slice(hd * HEAD_DIM, (hd + 1) * HEAD_DIM)
            vcols = slice(MEM_WIDTH + hd * HEAD_DIM, MEM_WIDTH + (hd + 1) * HEAD_DIM)
            qh, kh, p = _cross_probs(q_ref, kv_ref, hd)
            doh = do_ref[:, cols]
            dp = lax.dot_general(doh, kv_ref[:, vcols], (((1,), (1,)), ((), ())), preferred_element_type=F32)
            ds = (p * (dp - jnp.sum(p * dp, axis=1, keepdims=True)) * ATTN_SCALE).astype(BF16)
            dq_ref[:, cols] = jnp.dot(ds, kh, preferred_element_type=F32).astype(BF16)
            dkv_ref[:, cols] += lax.dot_general(ds, qh, (((0,), (0,)), ((), ())), preferred_element_type=F32)
            dkv_ref[:, vcols] += lax.dot_general(p.astype(BF16), doh, (((0,), (0,)), ((), ())), preferred_element_type=F32)

    qspec = pl.BlockSpec((tq, MEM_WIDTH), lambda i: (i, 0))
    kvspec = pl.BlockSpec((n_mem, 2 * MEM_WIDTH), lambda i: (0, 0))
    return pl.pallas_call(
        body, name="cross_bwd", grid=(s // tq,),
        in_specs=[qspec, kvspec, qspec],
        out_specs=[qspec, kvspec],
        out_shape=[jax.ShapeDtypeStruct((s, MEM_WIDTH), BF16), jax.ShapeDtypeStruct((n_mem, 2 * MEM_WIDTH), F32)],
        compiler_params=_params(1),
    )(qc, kvc, d_out)


def _swiglu_fwd(up, gate):
    sg = jax.nn.sigmoid(gate)
    silu = gate * sg
    return silu * up, up * (sg * (1.0 + gate * (1.0 - sg))), silu


def _swiglu_bwd(d_act, dact_dgate, dact_dup):
    return d_act * dact_dgate.astype(F32), d_act * dact_dup.astype(F32)


GATHER_GROUPS = {"in": ("w_in", "conv_w"), "mid": ("w_attn_out", "w_conv_out", "w_o", "w_cq", "w_ckv", "w_co"),
                 "gate": ("w_gate",), "up": ("w_up",), "down": ("w_down",)}


def _local_step(xs, mems, target, small, fetch, reduce):
    s, d = xs.shape
    w4 = {}
    cos_t, sin_t = _rope_tables(s)

    def near(group, done, then, after):
        waits = [("direct", group)] + ([("pass_near", done), ("pass_far", done)] if done else [])
        starts = [("forward", group), ("pass_near", group)] + [("direct", g) for g in then]
        tok = fetch.step("gather_near_" + group, waits, starts, after)
        if done:
            w4.update(fetch.arrays(done))
        return tok

    def far(group, then, after):
        return fetch.step("gather_far_" + group, [("forward", group)], [("pass_far", group)] + [("direct", g) for g in then], after)

    def last(group, after):
        tok = fetch.step("gather_done_" + group, [("pass_near", group), ("pass_far", group)], [], after)
        w4.update(fetch.arrays(group))
        return tok

    h = _rmsnorm(xs, small["g_mix"], "norm_mix")
    slots_filled = [a for g in ("gate", "up", "down") for a in fetch.arrays(g).values()]
    chip_x, chip_y = reduce.place[0] // 2, reduce.place[0] % 2
    own_block = jnp.stack([2 * chip_x + chip_y]).astype(jnp.int32)
    near_blocks = jnp.stack([2 * (1 - chip_x) + chip_y, 2 * chip_x + (1 - chip_y)]).astype(jnp.int32)
    far_block = jnp.stack([2 * (1 - chip_x) + (1 - chip_y)]).astype(jnp.int32)
    z = _matmul_column_blocks(h, fetch.arrays("in")["w_in"], own_block, None, tm=512, name="in_proj_own")
    tok = near("in", None, ["mid"], [z] + slots_filled)
    tok = fetch.step("gather_near_done_in", [("pass_near", "in")], [], tok)
    z = _matmul_column_blocks(h, fetch.arrays("in")["w_in"], near_blocks, z, tm=1024, name="in_proj_near", after=tok)
    tok = far("in", [], z)
    tok = fetch.step("gather_done_in", [("pass_far", "in")], [], tok)
    w4.update(fetch.arrays("in"))
    z = _matmul_column_blocks(h, w4["w_in"], far_block, z, tm=1024, name="in_proj_far", after=tok)
    conv4 = w4["conv_w"]
    conv_w = conv4[:, :3, :].transpose(1, 0, 2).reshape(3, N_CHIPS * conv4.shape[2])
    c_in = w4["w_in"].shape[2]
    tok = near("mid", None, ["gate"], z)
    q_rot, k_rot, v_b = _rope_fwd(z, cos_t, sin_t)
    attn = _swa_fwd(q_rot, k_rot, v_b, small["sink"])
    co = _conv_fwd(z, conv_w)
    tok = far("mid", ["up"], attn)
    tok = last("mid", tok)
    w_o = w4["w_o"].reshape(-1, w4["w_o"].shape[-1])
    c_d = w4["w_attn_out"].shape[2]
    ya = _matmul(attn, w4["w_attn_out"], mode="nn", tm=2048, tn=c_d, tk=ATTN_WIDTH, out_dtypes=[F32], name="attn_out_proj",
                 b_blocks=N_CHIPS, after=tok)
    yc = _matmul(co, w4["w_conv_out"], mode="nn", tm=2048, tn=c_d, tk=CONV_WIDTH, out_dtypes=[F32], name="conv_out_proj",
                 b_blocks=N_CHIPS)
    mix = _gate_fwd(z, small["b_gate"], ya, yc)
    x1 = _matmul(mix, w_o, mode="nn", tm=1024, tn=1024, tk=d, out_dtypes=[F32], name="mix_out_proj", extras=[xs],
                 epilogue=_add_residual)
    tok = near("gate", None, ["down"], x1)
    w_cq = w4["w_cq"].reshape(-1, w4["w_cq"].shape[-1])
    w_ckv = w4["w_ckv"].reshape(-1, w4["w_ckv"].shape[-1])
    hc = _rmsnorm(x1, small["g_cross"], "norm_cross")
    memn = _rmsnorm(mems, small["g_mem"], "norm_mem")
    qc = _matmul(hc, w_cq, mode="nn", tm=2048, tn=MEM_WIDTH, tk=d, out_dtypes=[BF16], name="cross_q_proj", after=tok)
    kvc = _matmul(memn, w_ckv, mode="nn", tm=256, tn=2 * MEM_WIDTH, tk=d, out_dtypes=[BF16], name="cross_kv_proj")
    oc = _cross_fwd(qc, kvc)
    tok = far("gate", [], oc)
    x2 = _matmul(oc, w4["w_co"], mode="nn", tm=2048, tn=c_d, tk=MEM_WIDTH, out_dtypes=[F32], name="cross_out_proj",
                 extras=[x1], epilogue=_add_residual, b_blocks=N_CHIPS, after=tok)
    hf = _rmsnorm(x2, small["g_ffn"], "norm_ffn")
    tok = near("up", "gate", [], hf)
    c_ff = w4["w_gate"].shape[2]
    gate = _matmul(hf, w4["w_gate"], mode="nn", tm=1024, tn=c_ff, tk=d, out_dtypes=[F32], name="ffn_gate_proj", b_blocks=N_CHIPS,
                   after=tok)
    tok = far("up", [], gate)
    tok = near("down", "up", [], tok)
    act, dact_dgate, dact_dup = _matmul(hf, w4["w_up"], mode="nn", tm=1024, tn=c_ff, tk=d, out_dtypes=[BF16, BF16, BF16],
                                        name="ffn_up_proj", extras=[gate], epilogue=_swiglu_fwd, b_blocks=N_CHIPS, after=tok)
    tok = far("down", [], act)
    last("down", tok)
    w_down = w4["w_down"].reshape(-1, w4["w_down"].shape[-1])
    x3 = _matmul(act, w_down, mode="nn", tm=512, tn=512, tk=w_down.shape[0], out_dtypes=[F32], name="ffn_down_proj", extras=[x2],
                 epilogue=_add_residual)
    dx3, dx3b, sq, dg_final = _loss_head(x3, small["g_final"], target)

    da, du = _matmul(dx3b, w_down, mode="nt", tm=512, tn=c_ff, tk=d, out_dtypes=[BF16, BF16], name="ffn_down_bwd",
                     extras=[dact_dgate, dact_dup], epilogue=_swiglu_bwd)
    core = reduce.core
    ffn_shape = dict(row_sharded=False, tm=1024, tn=c_ff)
    g_down = _matmul(act, dx3b, mode="tn", tm=c_ff, tn=1024, tk=s, out_dtypes=[BF16], name="ffn_down_wgrad")
    tok = reduce.add("down", {"w_down": g_down}, da)
    t_gate = _wgrad_half(hf, da, core, theirs=True, name="ffn_gate_wgrad_theirs", after=tok, **ffn_shape)
    tok = reduce.step("down", t_gate)
    t_up = _wgrad_half(hf, du, core, theirs=True, name="ffn_up_wgrad_theirs", after=tok, **ffn_shape)
    tok = reduce.send("ffn", {"w_gate": t_gate, "w_up": t_up}, dx3b)
    dhf = _matmul(da, w4["w_gate"], mode="nt", tm=512, tn=1024, tk=N_CHIPS * c_ff, out_dtypes=[F32], name="ffn_gate_bwd", b_blocks=N_CHIPS,
                  after=tok)
    got = reduce.received("ffn", dhf)
    p_gate = _wgrad_half(hf, da, core, theirs=False, name="ffn_gate_wgrad_mine", add=got["w_gate"], **ffn_shape)
    p_up = _wgrad_half(hf, du, core, theirs=False, name="ffn_up_wgrad_mine", add=got["w_up"], **ffn_shape)
    tok = reduce.add_parts("ffn", {"w_gate": p_gate, "w_up": p_up})
    dhf = _matmul(du, w4["w_up"], mode="nt", tm=512, tn=1024, tk=N_CHIPS * c_ff, out_dtypes=[F32], name="ffn_up_bwd", extras=[dhf],
                  epilogue=_add_residual, b_blocks=N_CHIPS, after=tok)
    tok = reduce.step("down", dhf)
    dx2, dx2b, dg_ffn = _rmsnorm_bwd(dhf, x2, small["g_ffn"], dx3, "norm_ffn_bwd")

    d_oc = _matmul(dx2b, w4["w_co"], mode="nt", tm=1024, tn=MEM_WIDTH, tk=d, out_dtypes=[BF16], name="cross_out_bwd",
                   b_blocks=N_CHIPS, after=tok)
    g_co = _matmul(oc, dx2b, mode="tn", tm=MEM_WIDTH, tn=c_d, tk=s, out_dtypes=[BF16], name="cross_out_wgrad", out_blocks=N_CHIPS)
    tok = reduce.step("down", g_co)
    dqc, dkvc = _cross_bwd(qc, kvc, d_oc)
    g_cq = _matmul(hc, dqc, mode="tn", tm=1024, tn=MEM_WIDTH, tk=s, out_dtypes=[BF16], name="cross_q_wgrad", after=tok)
    dhc = _matmul(dqc, w_cq, mode="nt", tm=1024, tn=1024, tk=MEM_WIDTH, out_dtypes=[F32], name="cross_q_bwd")
    g_ckv = _matmul(memn, dkvc, mode="tn", tm=1024, tn=2 * MEM_WIDTH, tk=mems.shape[0], out_dtypes=[BF16], name="cross_kv_wgrad")
    dmemn = _matmul(dkvc, w_ckv, mode="nt", tm=256, tn=1024, tk=2 * MEM_WIDTH, out_dtypes=[F32], name="cross_kv_bwd")
    _, _, dg_mem = _rmsnorm_bwd(dmemn, mems, small["g_mem"], None, "norm_mem_bwd")
    dx1, dx1b, dg_cross = _rmsnorm_bwd(dhc, x1, small["g_cross"], dx2, "norm_cross_bwd")

    dmix = _matmul(dx1b, w_o, mode="nt", tm=1024, tn=1024, tk=d, out_dtypes=[F32], name="mix_out_bwd")
    g_o = _matmul(mix, dx1b, mode="tn", tm=1024, tn=1024, tk=s, out_dtypes=[BF16], name="mix_out_wgrad")
    dya, dyc, dz, db_a, db_c = _gate_bwd(z, small["b_gate"], ya, yc, dmix)
    d_attn = _matmul(dya, w4["w_attn_out"], mode="nt", tm=1024, tn=ATTN_WIDTH, tk=d, out_dtypes=[BF16], name="attn_out_bwd",
                     b_blocks=N_CHIPS)
    g_ao = _matmul(attn, dya, mode="tn", tm=ATTN_WIDTH, tn=c_d, tk=s, out_dtypes=[BF16], name="attn_out_wgrad", out_blocks=N_CHIPS)
    d_co = _matmul(dyc, w4["w_conv_out"], mode="nt", tm=1024, tn=CONV_WIDTH, tk=d, out_dtypes=[F32], name="conv_out_bwd",
                   b_blocks=N_CHIPS)
    g_cvo = _matmul(co, dyc, mode="tn", tm=CONV_WIDTH, tn=c_d, tk=s, out_dtypes=[BF16], name="conv_out_wgrad", out_blocks=N_CHIPS)
    tok = reduce.step("ffn", g_cvo)
    tok = reduce.add("mid", {"w_co": g_co, "w_cq": g_cq, "w_ckv": g_ckv, "w_o": g_o, "w_attn_out": g_ao, "w_conv_out": g_cvo}, tok)
    dz, d_conv_w = _conv_bwd(z, conv_w, d_co, dz)
    dq_rot, dk_rot, dv, dsink = _swa_bwd(q_rot, k_rot, v_b, d_attn, small["sink"])
    tok = reduce.step("mid", dq_rot)
    dz = _rope_bwd(dq_rot, dk_rot, dv, cos_t, sin_t, dz)
    in_shape = dict(row_sharded=False, tm=1024, tn=c_in)
    t_in = _wgrad_half(h, dz, core, theirs=True, name="in_proj_wgrad_theirs", after=tok, **in_shape)
    tok = reduce.send("in", {"w_in": t_in}, dk_rot)
    tok = reduce.step("ffn", tok)
    tok = reduce.step("mid", tok)
    got = reduce.received("in", tok)
    p_in = _wgrad_half(h, dz, core, theirs=False, name="in_proj_wgrad_mine", add=got["w_in"], **in_shape)
    tok = reduce.add_parts("in", {"w_in": p_in})
    dh = _matmul(dz, w4["w_in"], mode="nt", tm=512, tn=512, tk=N_CHIPS * c_in, out_dtypes=[F32], name="in_proj_bwd", b_blocks=N_CHIPS,
                 after=tok)
    tok = reduce.step("mid", dh)
    grad_x, _, dg_mix = _rmsnorm_bwd(dh, xs, small["g_mix"], dx1, "norm_mix_bwd")

    small_grads = {
        "g_mix": dg_mix, "sink": dsink[:, 0], "b_gate": jnp.concatenate([db_a, db_c], axis=1), "g_cross": dg_cross,
        "g_mem": dg_mem, "g_ffn": dg_ffn, "g_final": dg_final, "conv_w": d_conv_w,
    }
    return sq, grad_x, small_grads


def _pair_sum(g4, ra, core, name):
    nb, rs, cs = g4.shape
    rh = rs // 2
    tr = _row_tile(rh, 256)
    per = rh // tr

    def body(c_ref, g_ref, r_ref, o_ref):
        o_ref[...] = (g_ref[...].astype(F32) + r_ref[...].astype(F32)).astype(BF16)

    plain = pl.BlockSpec((None, tr, cs), lambda j, i, c: (j, i, 0))
    return pl.pallas_call(
        body, name=name,
        grid_spec=pltpu.PrefetchScalarGridSpec(
            num_scalar_prefetch=1, grid=(nb, per),
            in_specs=[pl.BlockSpec((None, tr, cs), lambda j, i, c: (j, c[0] * per + i, 0)), plain],
            out_specs=plain),
        out_shape=jax.ShapeDtypeStruct((nb, rh, cs), BF16),
        compiler_params=_params(2),
    )(core, g4, ra)


def _quad_sum(parts, rc, place, name):
    _, rh, cs = parts.shape
    tr = _row_tile(rh, 256)
    per = rh // tr

    def body(p_ref, own_ref, r_ref, o_ref):
        acc = own_ref[...].astype(F32)
        for j in range(rc.shape[0]):
            acc = acc + r_ref[j].astype(F32)
        o_ref[...] = acc

    return pl.pallas_call(
        body, name=name,
        grid_spec=pltpu.PrefetchScalarGridSpec(
            num_scalar_prefetch=1, grid=(per,),
            in_specs=[pl.BlockSpec((None, tr, cs), lambda i, p: (p[0], i, 0)),
                      pl.BlockSpec((rc.shape[0], tr, cs), lambda i, p: (0, i, 0))],
            out_specs=pl.BlockSpec((tr, cs), lambda i, p: (p[1] * per + i, 0))),
        out_shape=jax.ShapeDtypeStruct((2 * rh, cs), F32),
        compiler_params=_params(1),
    )(place, parts, rc)


def _adamw_update(w, g, m, v):
    nm = ADAM_B1 * m + (1.0 - ADAM_B1) * g
    nv = ADAM_B2 * v + (1.0 - ADAM_B2) * (g * g)
    m_hat = nm / ADAM_C1
    v_hat = nv / ADAM_C2
    return -ADAM_LR * (m_hat / (jnp.sqrt(v_hat) + ADAM_EPS) + ADAM_WD * w), nm, nv


def _adamw_own_half(w, m, v, parts, rc, place, name, after=None):
    rows, cols = w.shape
    rh = rows // 2
    tr = _row_tile(rh, 256)
    per = rh // tr

    def body(p_ref, w_ref, m_ref, v_ref, own_ref, r_ref, *rest):
        gx_ref, g_ref, d_ref, nm_ref, nv_ref = rest[-5:]
        g = own_ref[...].astype(F32)
        for j in range(rc.shape[0]):
            g = g + r_ref[j].astype(F32)
        gx_ref[...] = g
        g_ref[...] = g
        d_ref[...], nm_ref[...], nv_ref[...] = _adamw_update(w_ref[...], g, m_ref[...], v_ref[...])

    mine = pl.BlockSpec((tr, cols), lambda i, p: (p[1] * per + i, 0))
    shape = jax.ShapeDtypeStruct((rows, cols), F32)
    return pl.pallas_call(
        body, name=name,
        grid_spec=pltpu.PrefetchScalarGridSpec(
            num_scalar_prefetch=1, grid=(per,),
            in_specs=[mine, mine, mine, pl.BlockSpec((None, tr, cols), lambda i, p: (p[0], i, 0)),
                      pl.BlockSpec((rc.shape[0], tr, cols), lambda i, p: (0, i, 0))] + ([] if after is None else [ANY]),
            out_specs=[mine] * 5),
        out_shape=[shape] * 5,
        compiler_params=_params(1),
    )(place, w, m, v, parts, rc, *([] if after is None else [after]))


def _adamw_other_half(w, m, v, g_exchanged, g, delta, new_m, new_v, place, name, after=None):
    rows, cols = w.shape
    rh = rows // 2
    tr = _row_tile(rh, 256)
    per = rh // tr

    def body(p_ref, w_ref, m_ref, v_ref, gx_ref, *rest):
        g_ref, d_ref, nm_ref, nv_ref = rest[-4:]
        gv = gx_ref[...]
        g_ref[...] = gv
        d_ref[...], nm_ref[...], nv_ref[...] = _adamw_update(w_ref[...], gv, m_ref[...], v_ref[...])

    other = pl.BlockSpec((tr, cols), lambda i, p: ((1 - p[1]) * per + i, 0))
    shape = jax.ShapeDtypeStruct((rows, cols), F32)
    n_after = 0 if after is None else 1
    return pl.pallas_call(
        body, name=name,
        grid_spec=pltpu.PrefetchScalarGridSpec(
            num_scalar_prefetch=1, grid=(per,),
            in_specs=[other] * 4 + [ANY] * (4 + n_after),
            out_specs=[other] * 4),
        out_shape=[shape] * 4,
        input_output_aliases={5: 0, 6: 1, 7: 2, 8: 3},
        compiler_params=_params(1),
    )(place, w, m, v, g_exchanged, g, delta, new_m, new_v, *([] if after is None else [after]))


def _cast_to_slot(w, place, dtype, name, after=None):
    rows, cols = w.shape
    tr = _row_tile(rows, 1024)

    def body(p_ref, w_ref, *rest):
        o_ref = rest[-1]
        o_ref[...] = w_ref[...].astype(dtype)

    return pl.pallas_call(
        body, name=name,
        grid_spec=pltpu.PrefetchScalarGridSpec(
            num_scalar_prefetch=1, grid=(rows // tr,),
            in_specs=[pl.BlockSpec((tr, cols), lambda i, p: (i, 0))] + ([] if after is None else [ANY]),
            out_specs=pl.BlockSpec((None, tr, cols), lambda i, p: (p[0], i, 0))),
        out_shape=jax.ShapeDtypeStruct((N_CHIPS, rows, cols), dtype),
        compiler_params=_params(1),
    )(place, w, *([] if after is None else [after]))


def _adamw(w, g, m, v, name, after=None):
    rows, cols = w.shape
    tr = _row_tile(rows, 256)

    def body(w_ref, g_ref, m_ref, v_ref, *rest):
        go_ref, d_ref, nm_ref, nv_ref = rest[-4:]
        gv = g_ref[...]
        go_ref[...] = gv
        d_ref[...], nm_ref[...], nv_ref[...] = _adamw_update(w_ref[...], gv, m_ref[...], v_ref[...])

    tile = pl.BlockSpec((tr, cols), lambda i: (i, 0))
    shape = jax.ShapeDtypeStruct((rows, cols), F32)
    return pl.pallas_call(
        body, name=name, grid=(rows // tr,),
        in_specs=[tile] * 4 + ([] if after is None else [ANY]), out_specs=[tile] * 4, out_shape=[shape] * 4,
        compiler_params=_params(1),
    )(w, g, m, v, *([] if after is None else [after]))


def _mesh_pos():
    return lax.axis_index("x"), lax.axis_index("y"), lax.axis_index("c")


def _other_chips(x, y):
    return [(1 - x, y), (x, 1 - y), (1 - x, 1 - y)]


def _half_rows(ref, which):
    rh = ref.shape[-2] // 2
    return ref.at[pl.ds(which * rh, rh), :]


def _remote(src, dst, send_sems, recv_sems, sem, to):
    return pltpu.make_async_remote_copy(src_ref=src, dst_ref=dst, send_sem=send_sems.at[sem], recv_sem=recv_sems.at[sem],
                                        device_id=to, device_id_type=MESH)


HBM = pl.BlockSpec(memory_space=pltpu.HBM)
SEM = pl.BlockSpec(memory_space=pltpu.SEMAPHORE)
DATAFLOW_EFFECT = pltpu.SideEffectType.DATAFLOW_SIDE_EFFECTING


def _in_hbm(arrays):
    return [pltpu.with_memory_space_constraint(a, pltpu.HBM) for a in arrays]


def _hbm_like(arrays):
    return [pltpu.HBM(a.shape, a.dtype) for a in arrays]


GATHER_COPIES_PER_ARRAY = {"direct": 2, "forward": 2, "pass_near": 2, "pass_far": 1}


def _gather_copies(kind, refs, x, y, c):
    me, near_x, near_y, far = 2 * x + y, 2 * (1 - x) + y, 2 * x + (1 - y), 2 * (1 - x) + (1 - y)
    to_x, to_y, sibling = (1 - x, y, c), (x, 1 - y, c), (x, y, 1 - c)
    out = []
    for ref in refs:
        rh = ref.shape[1] // 2
        rq = rh // 2

        def half(chip, ref=ref, rh=rh):
            return ref.at[chip, pl.ds(c * rh, rh), :]

        def quarter(chip, q, ref=ref, rh=rh, rq=rq):
            return ref.at[chip, pl.ds(c * rh + q * rq, rq), :]

        if kind == "direct":
            out += [(half(me), half(me), to_x), (half(me), half(me), to_y)]
        elif kind == "forward":
            out += [(quarter(near_x, 0), quarter(near_x, 0), to_y), (quarter(near_y, 1), quarter(near_y, 1), to_x)]
        elif kind == "pass_near":
            out += [(half(near_x), half(near_x), sibling), (half(near_y), half(near_y), sibling)]
        else:
            assert kind == "pass_far"
            out += [(half(far), half(far), sibling)]
    return out


def _gather_step(name, bufs, waits, starts, after):
    nb, nw, ns = len(bufs), len(waits), len(starts)
    after = [] if after is None else list(after) if isinstance(after, (list, tuple)) else [after]
    n_after = len(after)

    def body(*refs):
        ins = refs[:nb]
        wait_sems = refs[nb:nb + 2 * nw]
        start_sems = refs[nb + 2 * nw + n_after:nb + 2 * nw + n_after + 2 * ns]
        token = refs[-1]
        x, y, c = _mesh_pos()
        for j, (kind, idxs, _, _) in enumerate(waits):
            for i, (s_ref, d_ref, to) in enumerate(_gather_copies(kind, [ins[t] for t in idxs], x, y, c)):
                came = _remote(s_ref, d_ref, wait_sems[2 * j], wait_sems[2 * j + 1], i, to)
                came.wait_recv()
                came.wait_send()
        for j, (kind, idxs) in enumerate(starts):
            for i, (s_ref, d_ref, to) in enumerate(_gather_copies(kind, [ins[t] for t in idxs], x, y, c)):
                _remote(s_ref, d_ref, start_sems[2 * j], start_sems[2 * j + 1], i, to).start()
        token[...] = jnp.zeros_like(token)

    sems = []
    for kind, idxs in starts:
        sems += [pltpu.SemaphoreType.DMA((GATHER_COPIES_PER_ARRAY[kind] * len(idxs),))] * 2
    operands = _in_hbm(bufs) + [sem for w in waits for sem in w[2:]] + after
    outs = pl.pallas_call(
        body, name=name,
        in_specs=[HBM] * nb + [SEM] * (2 * nw) + [ANY] * n_after,
        out_specs=[SEM] * (2 * ns) + [HBM] * nb + [pl.BlockSpec(memory_space=pltpu.VMEM)],
        out_shape=sems + _hbm_like(bufs) + [jax.ShapeDtypeStruct((8, 128), F32)],
        input_output_aliases={i: 2 * ns + i for i in range(nb)},
        compiler_params=pltpu.CompilerParams(has_side_effects=DATAFLOW_EFFECT),
    )(*operands)
    return outs[2 * ns:2 * ns + nb], [(outs[2 * j], outs[2 * j + 1]) for j in range(ns)], outs[-1]


class _Gather:
    def __init__(self, groups):
        self.groups = groups
        self.bufs = {}
        self.in_flight = {}

    def put(self, slotted):
        self.bufs.update(slotted)

    def step(self, name, waits, starts, after=None):
        names = []
        for _, group in list(waits) + list(starts):
            names += [n for n in self.groups[group] if n not in names]
        index = {n: i for i, n in enumerate(names)}

        def members(group):
            return [index[n] for n in self.groups[group]]

        wait_args = [(kind, members(group)) + self.in_flight.pop((kind, group)) for kind, group in waits]
        start_args = [(kind, members(group)) for kind, group in starts]
        bufs, sems, token = _gather_step(name, [self.bufs[n] for n in names], wait_args, start_args, after)
        self.bufs.update(zip(names, bufs))
        for (kind, group), pair in zip(starts, sems):
            self.in_flight[(kind, group)] = pair
        return token

    def arrays(self, group):
        return {n: self.bufs[n] for n in self.groups[group]}


def _sibling_halves_copies(srcs, dsts, x, y, c):
    out = []
    for s_ref, d_ref in zip(srcs, dsts, strict=True):
        rh = s_ref.shape[1] // 2
        out.append((s_ref.at[:, pl.ds((1 - c) * rh, rh), :], d_ref, (x, y, 1 - c)))
    return out


def _to_sibling_copies(srcs, dsts, x, y, c):
    return [(s_ref, d_ref, (x, y, 1 - c)) for s_ref, d_ref in zip(srcs, dsts, strict=True)]


def _chip_copies(srcs, dsts, x, y, c):
    out = []
    for s_ref, d_ref in zip(srcs, dsts, strict=True):
        for k, (px, py) in enumerate(_other_chips(x, y)):
            out.append((s_ref.at[2 * px + py], d_ref.at[k], (px, py, c)))
    return out


def _join_copies(srcs, dsts, x, y, c):
    out = []
    for s_ref in srcs:
        mine = _half_rows(s_ref, c)
        out.append((mine, mine, (x, y, 1 - c)))
    return out


def _exchange_start(copies_fn, n_copies, srcs, fresh, after, name):
    ns, nb = len(srcs), len(srcs) + len(fresh)

    def body(*refs):
        bufs, send, recv, token = refs[:nb], refs[nb + 1], refs[nb + 2], refs[-1]
        x, y, c = _mesh_pos()
        for i, (s_ref, d_ref, to) in enumerate(copies_fn(bufs[:ns], bufs[ns:] if fresh else bufs[:ns], x, y, c)):
            _remote(s_ref, d_ref, send, recv, i, to).start()
        token[...] = jnp.zeros_like(token)

    sems = [pltpu.SemaphoreType.DMA((n_copies,))] * 2
    outs = pl.pallas_call(
        body, name=name,
        in_specs=[HBM] * nb + [ANY], out_specs=[SEM, SEM] + [HBM] * nb + [pl.BlockSpec(memory_space=pltpu.VMEM)],
        out_shape=sems + _hbm_like(list(srcs) + list(fresh)) + [jax.ShapeDtypeStruct((8, 128), F32)],
        input_output_aliases={i: 2 + i for i in range(nb)},
        compiler_params=pltpu.CompilerParams(has_side_effects=DATAFLOW_EFFECT),
    )(*_in_hbm(list(srcs) + list(fresh)), after)
    return outs[0], outs[1], outs[2:2 + ns], outs[2 + ns:2 + nb], outs[-1]


def _exchange_done(copies_fn, srcs, fresh, send, recv, after, name):
    ns, nb = len(srcs), len(srcs) + len(fresh)

    def body(*refs):
        bufs, send_in, recv_in = refs[:nb], refs[nb], refs[nb + 1]
        x, y, c = _mesh_pos()
        for i, (s_ref, d_ref, to) in enumerate(copies_fn(bufs[:ns], bufs[ns:] if fresh else bufs[:ns], x, y, c)):
            came = _remote(s_ref, d_ref, send_in, recv_in, i, to)
            came.wait_send()
            came.wait_recv()

    outs = pl.pallas_call(
        body, name=name,
        in_specs=[HBM] * nb + [SEM, SEM, ANY], out_specs=[HBM] * nb,
        out_shape=_hbm_like(list(srcs) + list(fresh)),
        input_output_aliases={i: i for i in range(nb)},
        compiler_params=pltpu.CompilerParams(has_side_effects=DATAFLOW_EFFECT),
    )(*_in_hbm(list(srcs) + list(fresh)), send, recv, after)
    return outs[:ns], outs[ns:]


class _Reduce:
    def __init__(self, place, core, shards, mom_m, mom_v):
        self.place, self.core = place, core
        self.shards, self.mom_m, self.mom_v = shards, mom_m, mom_v
        self.state = {}
        self.results = {}

    def add(self, group, grads, after):
        names = list(grads)
        g4s = [g.reshape((N_CHIPS, -1, g.shape[-1])) if g.ndim == 2 else g for g in grads.values()]
        fresh = [lax.empty((N_CHIPS, g.shape[1] // 2, g.shape[2]), BF16) for g in g4s]
        send, recv, g4s, fresh, token = _exchange_start(_sibling_halves_copies, len(names), g4s, fresh, after,
                                                        "pair_start_" + group)
        self.state[group] = (0, names, send, recv, g4s, fresh)
        return token

    def send(self, group, theirs, after):
        names, srcs = list(theirs), list(theirs.values())
        fresh = [lax.empty(s.shape, BF16) for s in srcs]
        send, recv, srcs, fresh, token = _exchange_start(_to_sibling_copies, len(names), srcs, fresh, after, "pair_start_" + group)
        self.state[group] = ("sent", names, send, recv, srcs, fresh)
        return token

    def received(self, group, after):
        stage, names, send, recv, srcs, fresh = self.state.pop(group)
        assert stage == "sent"
        _, got = _exchange_done(_to_sibling_copies, srcs, fresh, send, recv, after, "pair_done_" + group)
        return dict(zip(names, got))

    def add_parts(self, group, parts):
        names, srcs = list(parts), list(parts.values())
        fresh = [lax.empty((N_CHIPS - 1,) + p.shape[1:], BF16) for p in srcs]
        send, recv, srcs, fresh, token = _exchange_start(_chip_copies, 3 * len(names), srcs, fresh, self.core, "chips_start_" + group)
        self.state[group] = (1, names, send, recv, srcs, fresh)
        return token

    def step(self, group, after):
        stage, names, send, recv, srcs, fresh = self.state[group]
        if stage == 0:
            g4s, ras = _exchange_done(_sibling_halves_copies, srcs, fresh, send, recv, after, "pair_done_" + group)
            parts = [_pair_sum(g, r, self.core, "pair_sum_" + n) for g, r, n in zip(g4s, ras, names)]
            fresh = [lax.empty((N_CHIPS - 1,) + p.shape[1:], BF16) for p in parts]
            send, recv, parts, fresh, token = _exchange_start(_chip_copies, 3 * len(names), parts, fresh, self.core,
                                                              "chips_start_" + group)
            self.state[group] = (1, names, send, recv, parts, fresh)
            return token
        if stage == 1:
            parts, rcs = _exchange_done(_chip_copies, srcs, fresh, send, recv, after, "chips_done_" + group)
            token = None
            for n, p, r in zip(names, parts, rcs):
                self.results[n] = _adamw_own_half(self.shards[n], self.mom_m[n], self.mom_v[n], p, r, self.place,
                                                  "adamw_own_" + n, after=token)
                token = self.results[n][2]
            wholes = [self.results[n][0] for n in names]
            send, recv, wholes, _, token = _exchange_start(_join_copies, len(names), wholes, [], token, "join_start_" + group)
            self.state[group] = (2, names, send, recv, wholes, [])
            return token
        assert stage == 2
        wholes, _ = _exchange_done(_join_copies, srcs, [], send, recv, after, "join_done_" + group)
        token = None
        for n, exchanged in zip(names, wholes):
            _, g, d, nm, nv = self.results[n]
            self.results[n] = _adamw_other_half(self.shards[n], self.mom_m[n], self.mom_v[n], exchanged, g, d, nm, nv,
                                                self.place, "adamw_other_" + n, after=token)
            token = self.results[n][1]
        del self.state[group]
        return token


N_DEV = 8


def _all_reduce_small(v):
    def body(v_ref, o_ref, slots, send_sems, recv_sems):
        x, y, c = _mesh_pos()
        me = 4 * x + 2 * y + c
        slots[me] = v_ref[...]
        peers = []
        for r in range(1, N_DEV):
            fx, fy, fc = (r >> 2) & 1, (r >> 1) & 1, r & 1
            peers.append((x + fx - 2 * x * fx, y + fy - 2 * y * fy, c + fc - 2 * c * fc))
        sends = []
        for r, peer in enumerate(peers):
            cp = _remote(v_ref, slots.at[me], send_sems, recv_sems, r, peer)
            cp.start()
            sends.append(cp)
        for r, (px, py, pc) in enumerate(peers):
            landed = slots.at[4 * px + 2 * py + pc]
            _remote(landed, landed, send_sems, recv_sems, r, (px, py, pc)).wait_recv()
        for cp in sends:
            cp.wait_send()
        acc = slots[0]
        for i in range(1, N_DEV):
            acc = acc + slots[i]
        o_ref[...] = acc

    vm = pl.BlockSpec(memory_space=pltpu.VMEM)
    return pl.pallas_call(
        body, name="small_grads_all_reduce",
        in_specs=[vm], out_specs=vm,
        out_shape=jax.ShapeDtypeStruct(v.shape, v.dtype),
        scratch_shapes=[pltpu.VMEM((N_DEV,) + v.shape, v.dtype), pltpu.SemaphoreType.DMA((N_DEV - 1,)),
                        pltpu.SemaphoreType.DMA((N_DEV - 1,))],
    )(v)


MATRICES = ("w_in", "w_attn_out", "w_conv_out", "w_o", "w_cq", "w_ckv", "w_co", "w_gate", "w_up", "w_down")
VECTORS = ("g_mix", "b_gate", "g_cross", "g_mem", "g_ffn", "g_final", "conv_w", "sink")
WEIGHT_ORDER = ("g_mix", "w_in", "sink", "conv_w", "b_gate", "w_attn_out", "w_conv_out", "w_o", "g_cross", "g_mem", "w_cq",
                "w_ckv", "w_co", "g_ffn", "w_gate", "w_up", "w_down", "g_final")
CONV_PAD_ROWS = 32
SMALL_ROWS = 8


def _pack(pieces):
    flat = jnp.concatenate([p.reshape(-1) for p in pieces])
    lane_group = SMALL_ROWS * 128
    total = -(-flat.shape[0] // lane_group) * lane_group
    flat = jnp.pad(flat, (0, total - flat.shape[0]))
    return flat.reshape(SMALL_ROWS, total // SMALL_ROWS), [p.size for p in pieces]


def _unpack(packed, pieces):
    flat = packed.reshape(-1)
    out, off = [], 0
    for p in pieces:
        out.append(flat[off:off + p.size].reshape(p.shape))
        off += p.size
    return out


def kernel(x, mem, g_mix, w_in, sink, conv_w, b_gate, w_attn_out, w_conv_out, w_o, g_cross, g_mem, w_cq, w_ckv, w_co, g_ffn, w_gate, w_up, w_down, g_final, loss_target, m_g_mix, m_w_in, m_sink, m_conv_w, m_b_gate, m_w_attn_out, m_w_conv_out, m_w_o, m_g_cross, m_g_mem, m_w_cq, m_w_ckv, m_w_co, m_g_ffn, m_w_gate, m_w_up, m_w_down, m_g_final, v_g_mix, v_w_in, v_sink, v_conv_w, v_b_gate, v_w_attn_out, v_w_conv_out, v_w_o, v_g_cross, v_g_mem, v_w_cq, v_w_ckv, v_w_co, v_g_ffn, v_w_gate, v_w_up, v_w_down, v_g_final):
    given = dict(g_mix=g_mix, w_in=w_in, sink=sink, conv_w=conv_w, b_gate=b_gate, w_attn_out=w_attn_out, w_conv_out=w_conv_out,
                 w_o=w_o, g_cross=g_cross, g_mem=g_mem, w_cq=w_cq, w_ckv=w_ckv, w_co=w_co, g_ffn=g_ffn, w_gate=w_gate, w_up=w_up,
                 w_down=w_down, g_final=g_final)
    mom_m = dict(g_mix=m_g_mix, w_in=m_w_in, sink=m_sink, conv_w=m_conv_w, b_gate=m_b_gate, w_attn_out=m_w_attn_out,
                 w_conv_out=m_w_conv_out, w_o=m_w_o, g_cross=m_g_cross, g_mem=m_g_mem, w_cq=m_w_cq, w_ckv=m_w_ckv, w_co=m_w_co,
                 g_ffn=m_g_ffn, w_gate=m_w_gate, w_up=m_w_up, w_down=m_w_down, g_final=m_g_final)
    mom_v = dict(g_mix=v_g_mix, w_in=v_w_in, sink=v_sink, conv_w=v_conv_w, b_gate=v_b_gate, w_attn_out=v_w_attn_out,
                 w_conv_out=v_w_conv_out, w_o=v_w_o, g_cross=v_g_cross, g_mem=v_g_mem, w_cq=v_w_cq, w_ckv=v_w_ckv, w_co=v_w_co,
                 g_ffn=v_g_ffn, w_gate=v_w_gate, w_up=v_w_up, w_down=v_w_down, g_final=v_g_final)
    xs, mems, target = x[0], mem[0], loss_target[0]
    d_model = xs.shape[1]
    chip = 2 * lax.axis_index("x") + lax.axis_index("y")
    core = jnp.reshape(lax.axis_index("c"), (1,)).astype(jnp.int32)
    place = jnp.stack([chip, lax.axis_index("c")]).astype(jnp.int32)

    shards = {n: given[n][0] for n in MATRICES}
    conv_cols = conv_w.shape[2]
    conv_pad = jnp.pad(conv_w[0], ((0, CONV_PAD_ROWS - conv_w.shape[1]), (0, 0)))
    fetch = _Gather(GATHER_GROUPS)
    first = {"w_in": _cast_to_slot(shards["w_in"], place, BF16, "to_slot_w_in"),
             "conv_w": _cast_to_slot(conv_pad, place, F32, "to_slot_conv_w")}
    fetch.put(first)
    tok = fetch.step("gather_start", [], [("direct", "in")])
    fetch.put({n: _cast_to_slot(shards[n], place, BF16, "to_slot_" + n, after=tok) for n in MATRICES if n != "w_in"})
    small = {n: given[n] for n in ("g_mix", "b_gate", "g_cross", "g_mem", "g_ffn")}
    small["g_final"] = g_final[None]
    small["sink"] = sink[0]

    reduce = _Reduce(place, core, shards, {n: mom_m[n][0] for n in MATRICES}, {n: mom_v[n][0] for n in MATRICES})
    sq, grad_x, small_grads = _local_step(xs, mems, target, small, fetch, reduce)

    loss_part = 0.5 * sq[0:1, 0:1] / d_model
    pieces = [small_grads[n] for n in VECTORS] + [loss_part]
    packed, _ = _pack(pieces)
    summed = _unpack(_all_reduce_small(packed), pieces)
    loss = summed[-1][0, 0]
    small_sum = dict(zip(VECTORS, summed[:-1]))
    small_sum["conv_w"] = lax.dynamic_slice_in_dim(small_sum["conv_w"], chip * conv_cols, conv_cols, axis=1)

    grad_out, delta, new_m, new_v = {}, {}, {}, {}
    like = [given[n] for n in VECTORS]
    pw, _ = _pack(like)
    pg, _ = _pack([small_sum[n] for n in VECTORS])
    pm, _ = _pack([mom_m[n] for n in VECTORS])
    pv, _ = _pack([mom_v[n] for n in VECTORS])
    tok = reduce.step("in", pg)
    _, pd, pnm, pnv = _adamw(pw, pg, pm, pv, "adamw_small", after=tok)
    for n, g, d, nm, nv in zip(VECTORS, [small_sum[n] for n in VECTORS], _unpack(pd, like), _unpack(pnm, like), _unpack(pnv, like)):
        grad_out[n] = g.reshape(given[n].shape)
        delta[n], new_m[n], new_v[n] = d, nm, nv
    reduce.step("in", pd)
    for n in MATRICES:
        g, d, nm, nv = reduce.results[n]
        grad_out[n], delta[n], new_m[n], new_v[n] = g[None], d[None], nm[None], nv[None]

    return (loss, grad_x[None], *[grad_out[n] for n in WEIGHT_ORDER], *[delta[n] for n in WEIGHT_ORDER],
            *[new_m[n] for n in WEIGHT_ORDER], *[new_v[n] for n in WEIGHT_ORDER])
```

```python
import functools

import jax
import jax.numpy as jnp
from jax import lax
from jax.experimental import pallas as pl
from jax.experimental.pallas import tpu as pltpu

F32 = jnp.float32
BF16 = jnp.bfloat16
MESH = pl.DeviceIdType.MESH
ANY = pl.BlockSpec(memory_space=pl.ANY)

VMEM_LIMIT_BYTES = 56 * 1024 * 1024

N_CHIPS = 4
HEAD_DIM = 128
N_Q_HEADS = 8
N_KV_HEADS = 2
Q_GROUP = N_Q_HEADS // N_KV_HEADS
ATTN_WIDTH = N_Q_HEADS * HEAD_DIM
KV_WIDTH = N_KV_HEADS * HEAD_DIM
WINDOW = 128
BLOCK = 128
BAND = 3 * BLOCK
ROPE_THETA = 10000.0
CONV_WIDTH = 1024
MEM_HEADS = 4
MEM_WIDTH = MEM_HEADS * HEAD_DIM
RMS_EPS = 1e-6
NEG_INF = -1e30
ATTN_SCALE = HEAD_DIM ** -0.5

Q_OFF, K_OFF, V_OFF, CU_OFF, CB_OFF, CC_OFF, GL_OFF = 0, 1024, 1280, 1536, 2560, 3584, 4608

ADAM_LR = 0.001
ADAM_B1 = 0.9
ADAM_B2 = 0.999
ADAM_EPS = 1e-08
ADAM_WD = 0.01
ADAM_STEP = 10
ADAM_C1 = 1.0 - ADAM_B1 ** ADAM_STEP
ADAM_C2 = 1.0 - ADAM_B2 ** ADAM_STEP


def _params(n_grid_axes):
    return pltpu.CompilerParams(dimension_semantics=("arbitrary",) * n_grid_axes, vmem_limit_bytes=VMEM_LIMIT_BYTES)


BF16_SUBLANES = 16


def _row_tile(rows, want):
    if rows <= want:
        return rows
    for t in range(want, 0, -BF16_SUBLANES):
        if rows % t == 0:
            return t
    return rows


def _matmul(a, b, *, mode, tm, tn, tk, out_dtypes, name, extras=(), epilogue=None, b_blocks=1, out_blocks=1, after=None,
            a_norm_gain=None):
    if mode == "tn":
        kdim, m = a.shape
    else:
        m, kdim = a.shape
    if b_blocks > 1:
        nb, brows, bcols = b.shape
        assert nb == b_blocks
        if mode == "nn":
            n = bcols * nb
            assert brows == kdim
        else:
            assert mode == "nt" and bcols * nb == kdim
            n = brows
    else:
        n = b.shape[0] if mode == "nt" else b.shape[1]
    tm, tn = min(tm, m), min(tn, n)
    assert m % tm == 0 and n % tn == 0 and tk == kdim, (name, m, n, kdim, tm, tn, tk)
    n_extra, n_out = len(extras), len(out_dtypes)
    n_after = 0 if after is None else 1
    normed = a_norm_gain is not None
    assert not normed or mode != "tn"

    if mode == "tn":
        a_spec = pl.BlockSpec((tk, tm), lambda j, i, k: (k, i))
        dims = (((0,), (0,)), ((), ()))
    else:
        a_spec = pl.BlockSpec((tm, tk), lambda j, i, k: (i, k))
        dims = (((1,), (0,)), ((), ())) if mode == "nn" else (((1,), (1,)), ((), ()))

    if b_blocks > 1 and mode == "nn":
        per = b.shape[2] // tn
        assert b.shape[2] % tn == 0
        b_spec = pl.BlockSpec((None, tk, tn), lambda j, i, k: (j // per, k, j % per))
    elif b_blocks > 1:
        b_spec = pl.BlockSpec((b_blocks, tn, b.shape[2]), lambda j, i, k: (0, j, 0))
    elif mode == "nt":
        b_spec = pl.BlockSpec((tn, tk), lambda j, i, k: (j, k))
    else:
        b_spec = pl.BlockSpec((tk, tn), lambda j, i, k: (k, j))

    tile_spec = pl.BlockSpec((tm, tn), lambda j, i, k: (i, j))
    if out_blocks > 1:
        ncols = n // out_blocks
        assert ncols % tn == 0
        oper = ncols // tn
        out_spec = pl.BlockSpec((None, tm, tn), lambda j, i, k: (j // oper, i, j % oper))
        out_shape = [jax.ShapeDtypeStruct((out_blocks, m, ncols), dt) for dt in out_dtypes]
    else:
        out_spec = tile_spec
        out_shape = [jax.ShapeDtypeStruct((m, n), dt) for dt in out_dtypes]

    def body(a_ref, b_ref, *rest):
        extra_refs = rest[:n_extra]
        out_refs = rest[n_extra + n_after + normed:n_extra + n_after + normed + n_out]
        if normed:
            xv = a_ref[...]
            a_ref = rest[-1]
            a_ref[...] = (xv * _rstd(xv) * rest[n_extra + n_after][...]).astype(BF16)
        if mode == "nt" and b_blocks > 1:
            cs = b.shape[2]
            acc = None
            for jb in range(b_blocks):
                prod = lax.dot_general(a_ref[:, jb * cs:(jb + 1) * cs].astype(BF16), b_ref[jb].astype(BF16), dims,
                                       preferred_element_type=F32)
                acc = prod if acc is None else acc + prod
        else:
            acc = lax.dot_general(a_ref[...].astype(BF16), b_ref[...].astype(BF16), dims, preferred_element_type=F32)
        tiles = (acc,) if epilogue is None else epilogue(acc, *[r[...] for r in extra_refs])
        for o_ref, t in zip(out_refs, tiles, strict=True):
            o_ref[...] = t.astype(o_ref.dtype)

    outs = pl.pallas_call(
        body,
        name=name,
        grid=(n // tn, m // tm, 1),
        in_specs=[a_spec, b_spec] + [tile_spec] * n_extra + [ANY] * n_after
        + ([pl.BlockSpec((1, kdim), lambda j, i, k: (0, 0))] if normed else []),
        out_specs=[out_spec] * n_out + ([a_spec] if normed else []),
        out_shape=out_shape + ([jax.ShapeDtypeStruct((m, kdim), BF16)] if normed else []),
        compiler_params=_params(3),
    )(a, b, *extras, *([] if after is None else [after]), *([a_norm_gain] if normed else []))
    return outs[0] if len(outs) == 1 else outs


def _add_residual(acc, res):
    return (acc + res,)


def _matmul_column_blocks(a, b4, blocks, out, *, tm, name, after=None, norm_gain=None):
    m, kdim = a.shape
    nb, _, cols = b4.shape
    tm = min(tm, m)
    assert m % tm == 0
    n_blocks = blocks.shape[0]
    normed = norm_gain is not None
    assert not normed or (n_blocks == 1 and out is None)

    def body(j_ref, a_ref, b_ref, *rest):
        if normed:
            xv = a_ref[...]
            hv = (xv * _rstd(xv) * rest[0][...]).astype(BF16)
            rest[-1][...] = hv
            rest[-2][...] = jnp.dot(hv, b_ref[...], preferred_element_type=F32)
        else:
            rest[-1][...] = jnp.dot(a_ref[...], b_ref[...], preferred_element_type=F32)

    extra = ([] if out is None else [out]) + ([] if after is None else [after])
    z_spec = pl.BlockSpec((tm, cols), lambda j, i, blk: (i, blk[j]))
    z_shape = jax.ShapeDtypeStruct((m, nb * cols), F32)
    return pl.pallas_call(
        body, name=name,
        grid_spec=pltpu.PrefetchScalarGridSpec(
            num_scalar_prefetch=1, grid=(n_blocks, m // tm),
            in_specs=[pl.BlockSpec((tm, kdim), lambda j, i, blk: (i, 0)),
                      pl.BlockSpec((None, kdim, cols), lambda j, i, blk: (blk[j], 0, 0))]
            + ([pl.BlockSpec((1, kdim), lambda j, i, blk: (0, 0))] if normed else []) + [ANY] * len(extra),
            out_specs=[z_spec, pl.BlockSpec((tm, kdim), lambda j, i, blk: (i, 0))] if normed else z_spec),
        out_shape=[z_shape, jax.ShapeDtypeStruct((m, kdim), BF16)] if normed else z_shape,
        input_output_aliases={} if out is None else {3: 0},
        compiler_params=_params(2),
    )(blocks, a, b4, *([norm_gain] if normed else []), *extra)


def _wgrad_half(a, b, core, *, theirs, row_sharded, tm, tn, name, add=None, after=None):
    kdim, m = a.shape
    n = b.shape[1]
    rs, cs = (m // N_CHIPS, n) if row_sharded else (m, n // N_CHIPS)
    rh = rs // 2
    tm, tn = min(tm, rh), min(tn, cs)
    assert rh % tm == 0 and cs % tn == 0, (name, rh, cs, tm, tn)
    mh, per = rh // tm, cs // tn
    has_add = add is not None

    def half(c):
        return 1 - c[0] if theirs else c[0]

    if row_sharded:
        grid = (n // tn, N_CHIPS * mh)
        a_spec = pl.BlockSpec((kdim, tm), lambda j, r, c: (0, ((r // mh) * 2 + half(c)) * mh + r % mh))
        o_spec = pl.BlockSpec((None, tm, tn), lambda j, r, c: (r // mh, r % mh, j))
    else:
        grid = (n // tn, mh)
        a_spec = pl.BlockSpec((kdim, tm), lambda j, r, c: (0, half(c) * mh + r))
        o_spec = pl.BlockSpec((None, tm, tn), lambda j, r, c: (j // per, r, j % per))
    b_spec = pl.BlockSpec((kdim, tn), lambda j, r, c: (0, j))

    def body(c_ref, a_ref, b_ref, *rest):
        o_ref = rest[-1]
        acc = lax.dot_general(a_ref[...].astype(BF16), b_ref[...].astype(BF16), (((0,), (0,)), ((), ())),
                              preferred_element_type=F32)
        if has_add:
            acc = acc + rest[0][...].astype(F32)
        o_ref[...] = acc.astype(BF16)

    operands = [a, b] + ([add] if has_add else []) + ([] if after is None else [after])
    return pl.pallas_call(
        body, name=name,
        grid_spec=pltpu.PrefetchScalarGridSpec(
            num_scalar_prefetch=1, grid=grid,
            in_specs=[a_spec, b_spec] + ([o_spec] if has_add else []) + ([] if after is None else [ANY]),
            out_specs=o_spec),
        out_shape=jax.ShapeDtypeStruct((N_CHIPS, rh, cs), BF16),
        compiler_params=_params(2),
    )(core, *operands)


def _rstd(x):
    return lax.rsqrt(jnp.mean(x * x, axis=-1, keepdims=True) + RMS_EPS)


def _rmsnorm(x, g, name):
    s, d = x.shape
    tr = _row_tile(s, 512)

    def body(x_ref, g_ref, o_ref):
        xv = x_ref[...]
        o_ref[...] = (xv * _rstd(xv) * g_ref[...]).astype(BF16)

    return pl.pallas_call(
        body, name=name, grid=(s // tr,),
        in_specs=[pl.BlockSpec((tr, d), lambda i: (i, 0)), pl.BlockSpec((1, d), lambda i: (0, 0))],
        out_specs=pl.BlockSpec((tr, d), lambda i: (i, 0)),
        out_shape=jax.ShapeDtypeStruct((s, d), BF16),
        compiler_params=_params(1),
    )(x, g)


def _rmsnorm_bwd(dh, x, g, dres, name):
    s, d = x.shape
    tr = _row_tile(s, 512)
    has_res = dres is not None

    def body(*refs):
        if has_res:
            dh_ref, x_ref, g_ref, res_ref, dx_ref, dxb_ref, dg_ref = refs
        else:
            dh_ref, x_ref, g_ref, dx_ref, dxb_ref, dg_ref = refs
        xv = x_ref[...]
        dhv = dh_ref[...].astype(F32)
        r = _rstd(xv)
        xn = xv * r
        dhg = dhv * g_ref[...]
        dx = r * (dhg - xn * jnp.mean(dhg * xn, axis=-1, keepdims=True))
        if has_res:
            dx = dx + res_ref[...]
        dx_ref[...] = dx
        dxb_ref[...] = dx.astype(BF16)
        part = jnp.sum(dhv * xn, axis=0, keepdims=True)

        @pl.when(pl.program_id(0) == 0)
        def _():
            dg_ref[...] = part

        @pl.when(pl.program_id(0) > 0)
        def _():
            dg_ref[...] += part

    row = pl.BlockSpec((tr, d), lambda i: (i, 0))
    vec = pl.BlockSpec((1, d), lambda i: (0, 0))
    return pl.pallas_call(
        body, name=name, grid=(s // tr,),
        in_specs=[row, row, vec] + ([row] if has_res else []),
        out_specs=[row, row, vec],
        out_shape=[jax.ShapeDtypeStruct((s, d), F32), jax.ShapeDtypeStruct((s, d), BF16), jax.ShapeDtypeStruct((1, d), F32)],
        compiler_params=_params(1),
    )(*([dh, x, g] + ([dres] if has_res else [])))


def _loss_head(x3, g, target):
    s, d = x3.shape
    tr = _row_tile(s, 512)

    def body(x_ref, g_ref, t_ref, dx_ref, dxb_ref, sq_ref, dg_ref):
        xv = x_ref[...]
        gv = g_ref[...]
        r = _rstd(xv)
        xn = xv * r
        err = xn * gv - t_ref[...]
        dy = err * (1.0 / d)
        dyg = dy * gv
        dx = r * (dyg - xn * jnp.mean(dyg * xn, axis=-1, keepdims=True))
        dx_ref[...] = dx
        dxb_ref[...] = dx.astype(BF16)
        sq = jnp.sum(jnp.sum(err * err, axis=1, keepdims=True), axis=0, keepdims=True)
        sq = jnp.broadcast_to(sq, (1, 128))
        part = jnp.sum(dy * xn, axis=0, keepdims=True)

        @pl.when(pl.program_id(0) == 0)
        def _():
            sq_ref[...] = sq
            dg_ref[...] = part

        @pl.when(pl.program_id(0) > 0)
        def _():
            sq_ref[...] += sq
            dg_ref[...] += part

    row = pl.BlockSpec((tr, d), lambda i: (i, 0))
    vec = pl.BlockSpec((1, d), lambda i: (0, 0))
    return pl.pallas_call(
        body, name="loss_head", grid=(s // tr,),
        in_specs=[row, vec, row],
        out_specs=[row, row, pl.BlockSpec((1, 128), lambda i: (0, 0)), vec],
        out_shape=[jax.ShapeDtypeStruct((s, d), F32), jax.ShapeDtypeStruct((s, d), BF16),
                   jax.ShapeDtypeStruct((1, 128), F32), jax.ShapeDtypeStruct((1, d), F32)],
        compiler_params=_params(1),
    )(x3, g, target)


def _rope_tables(s):
    inv = 1.0 / (ROPE_THETA ** (jnp.arange(0, HEAD_DIM, 2, dtype=F32) / HEAD_DIM))
    ang = jnp.arange(s, dtype=F32)[:, None] * inv[None, :]
    cos, sin = jnp.cos(ang), jnp.sin(ang)
    return jnp.concatenate([cos, cos], axis=1), jnp.concatenate([-sin, sin], axis=1)


def _swap_halves(t):
    return pltpu.roll(t, HEAD_DIM // 2, 1)


def _rope_fwd(z, cos_t, sin_t, after=None):
    s = z.shape[0]
    tr = _row_tile(s, 256)

    def body(zq_ref, zk_ref, zv_ref, c_ref, s_ref, *rest):
        q_ref, k_ref, v_ref = rest[-3:]
        c, sn = c_ref[...], s_ref[...]
        for hd in range(N_Q_HEADS):
            cols = slice(hd * HEAD_DIM, (hd + 1) * HEAD_DIM)
            t = zq_ref[:, cols]
            q_ref[:, cols] = (t * c + _swap_halves(t) * sn).astype(BF16)
        for hd in range(N_KV_HEADS):
            cols = slice(hd * HEAD_DIM, (hd + 1) * HEAD_DIM)
            t = zk_ref[:, cols]
            k_ref[:, cols] = (t * c + _swap_halves(t) * sn).astype(BF16)
        v_ref[...] = zv_ref[...].astype(BF16)

    tab = pl.BlockSpec((tr, HEAD_DIM), lambda i: (i, 0))
    return pl.pallas_call(
        body, name="rope_fwd", grid=(s // tr,),
        in_specs=[pl.BlockSpec((tr, ATTN_WIDTH), lambda i: (i, Q_OFF // ATTN_WIDTH)),
                  pl.BlockSpec((tr, KV_WIDTH), lambda i: (i, K_OFF // KV_WIDTH)),
                  pl.BlockSpec((tr, KV_WIDTH), lambda i: (i, V_OFF // KV_WIDTH)), tab, tab] + ([] if after is None else [ANY]),
        out_specs=[pl.BlockSpec((tr, ATTN_WIDTH), lambda i: (i, 0)), pl.BlockSpec((tr, KV_WIDTH), lambda i: (i, 0)),
                   pl.BlockSpec((tr, KV_WIDTH), lambda i: (i, 0))],
        out_shape=[jax.ShapeDtypeStruct((s, ATTN_WIDTH), BF16), jax.ShapeDtypeStruct((s, KV_WIDTH), BF16),
                   jax.ShapeDtypeStruct((s, KV_WIDTH), BF16)],
        compiler_params=_params(1),
    )(z, z, z, cos_t, sin_t, *([] if after is None else [after]))


def _rope_bwd(dq_rot, dk_rot, dv, cos_t, sin_t, dz):
    s = dq_rot.shape[0]
    tr = _row_tile(s, 256)
    qkv_width = V_OFF + KV_WIDTH

    def body(dq_ref, dk_ref, dv_ref, c_ref, s_ref, dz_in_ref, o_ref):
        c, sn = c_ref[...], s_ref[...]
        for hd in range(N_Q_HEADS):
            t = dq_ref[:, hd * HEAD_DIM:(hd + 1) * HEAD_DIM]
            o_ref[:, Q_OFF + hd * HEAD_DIM:Q_OFF + (hd + 1) * HEAD_DIM] = (t * c + _swap_halves(t * sn)).astype(BF16)
        for hd in range(N_KV_HEADS):
            t = dk_ref[:, hd * HEAD_DIM:(hd + 1) * HEAD_DIM]
            o_ref[:, K_OFF + hd * HEAD_DIM:K_OFF + (hd + 1) * HEAD_DIM] = (t * c + _swap_halves(t * sn)).astype(BF16)
        o_ref[:, V_OFF:V_OFF + KV_WIDTH] = dv_ref[...].astype(BF16)

    tab = pl.BlockSpec((tr, HEAD_DIM), lambda i: (i, 0))
    wide = pl.BlockSpec((tr, ATTN_WIDTH), lambda i: (i, 0))
    narrow = pl.BlockSpec((tr, KV_WIDTH), lambda i: (i, 0))
    return pl.pallas_call(
        body, name="rope_bwd", grid=(s // tr,),
        in_specs=[wide, narrow, narrow, tab, tab, ANY],
        out_specs=pl.BlockSpec((tr, qkv_width), lambda i: (i, 0)),
        out_shape=jax.ShapeDtypeStruct(dz.shape, dz.dtype),
        input_output_aliases={5: 0},
        compiler_params=_params(1),
    )(dq_rot, dk_rot, dv, cos_t, sin_t, dz)


def _swa_band(i, s):
    return pl.multiple_of(jnp.clip((i - 1) * BLOCK, 0, s - BAND), BLOCK)


SWA_HEADS_PER_PASS = Q_GROUP


def _swa_probs(q_ref, k_ref, sink_ref, heads, start, valid):
    kv = heads[0] // Q_GROUP
    cols = slice(kv * HEAD_DIM, (kv + 1) * HEAD_DIM)
    kb = k_ref[pl.ds(start, BAND), cols]
    qg = jnp.concatenate([q_ref[:, hd * HEAD_DIM:(hd + 1) * HEAD_DIM] for hd in heads], axis=0)
    sc = lax.dot_general(qg, kb, (((1,), (1,)), ((), ())), preferred_element_type=F32) * ATTN_SCALE
    sc = jnp.where(valid, sc, NEG_INF)
    sk = jnp.concatenate([jnp.full((BLOCK, 1), sink_ref[hd], F32) for hd in heads], axis=0)
    mx = jnp.maximum(jnp.max(sc, axis=1, keepdims=True), sk)
    e = jnp.exp(sc - mx)
    es = jnp.exp(sk - mx)
    inv = 1.0 / (jnp.sum(e, axis=1, keepdims=True) + es)
    return qg, kb, e * inv, es * inv


def _swa_head_passes():
    return [list(range(h0, h0 + SWA_HEADS_PER_PASS)) for h0 in range(0, N_Q_HEADS, SWA_HEADS_PER_PASS)]


def _swa_valid(i, start):
    q_pos = i * BLOCK + lax.broadcasted_iota(jnp.int32, (BLOCK, 1), 0)
    q_pos = jnp.concatenate([q_pos] * SWA_HEADS_PER_PASS, axis=0)
    k_pos = start + lax.broadcasted_iota(jnp.int32, (1, BAND), 1)
    return jnp.abs(k_pos - q_pos) <= WINDOW


def _swa_fwd(q, k, v, sink):
    s = q.shape[0]
    assert s % BLOCK == 0 and s >= BAND

    def body(sink_ref, q_ref, k_ref, v_ref, o_ref):
        i = pl.program_id(0)
        start = _swa_band(i, s)
        valid = _swa_valid(i, start)
        for heads in _swa_head_passes():
            kv = heads[0] // Q_GROUP
            _, _, p, _ = _swa_probs(q_ref, k_ref, sink_ref, heads, start, valid)
            vb = v_ref[pl.ds(start, BAND), kv * HEAD_DIM:(kv + 1) * HEAD_DIM]
            o = jnp.dot(p.astype(BF16), vb, preferred_element_type=F32)
            for g, hd in enumerate(heads):
                o_ref[:, hd * HEAD_DIM:(hd + 1) * HEAD_DIM] = o[g * BLOCK:(g + 1) * BLOCK].astype(BF16)

    whole = pl.BlockSpec((s, KV_WIDTH), lambda i: (0, 0))
    blk = pl.BlockSpec((BLOCK, ATTN_WIDTH), lambda i: (i, 0))
    return pl.pallas_call(
        body, name="swa_fwd", grid=(s // BLOCK,),
        in_specs=[pl.BlockSpec(memory_space=pltpu.SMEM), blk, whole, whole],
        out_specs=blk,
        out_shape=jax.ShapeDtypeStruct((s, ATTN_WIDTH), BF16),
        compiler_params=_params(1),
    )(sink, q, k, v)


def _swa_bwd(q, k, v, d_out, sink):
    s = q.shape[0]

    def body(sink_ref, q_ref, k_ref, v_ref, do_ref, dq_ref, dk_ref, dv_ref, dsink_ref):
        i = pl.program_id(0)

        @pl.when(i == 0)
        def _():
            dk_ref[...] = jnp.zeros_like(dk_ref)
            dv_ref[...] = jnp.zeros_like(dv_ref)
            dsink_ref[...] = jnp.zeros_like(dsink_ref)

        start = _swa_band(i, s)
        valid = _swa_valid(i, start)
        for heads in _swa_head_passes():
            kv = heads[0] // Q_GROUP
            cols = slice(kv * HEAD_DIM, (kv + 1) * HEAD_DIM)
            qg, kb, p, p_sink = _swa_probs(q_ref, k_ref, sink_ref, heads, start, valid)
            vb = v_ref[pl.ds(start, BAND), cols]
            dog = jnp.concatenate([do_ref[:, hd * HEAD_DIM:(hd + 1) * HEAD_DIM] for hd in heads], axis=0)
            dp = lax.dot_general(dog, vb, (((1,), (1,)), ((), ())), preferred_element_type=F32)
            delta = jnp.sum(p * dp, axis=1, keepdims=True)
            ds = (p * (dp - delta) * ATTN_SCALE).astype(BF16)
            dqg = jnp.dot(ds, kb, preferred_element_type=F32)
            dk_ref[pl.ds(start, BAND), cols] += lax.dot_general(ds, qg, (((0,), (0,)), ((), ())), preferred_element_type=F32)
            dv_ref[pl.ds(start, BAND), cols] += lax.dot_general(p.astype(BF16), dog, (((0,), (0,)), ((), ())),
                                                                 preferred_element_type=F32)
            dsk = p_sink * delta
            for g, hd in enumerate(heads):
                dq_ref[:, hd * HEAD_DIM:(hd + 1) * HEAD_DIM] = dqg[g * BLOCK:(g + 1) * BLOCK]
                tot = jnp.sum(dsk[g * BLOCK:(g + 1) * BLOCK], axis=0, keepdims=True)
                dsink_ref[hd:hd + 1, :] -= jnp.broadcast_to(tot, (1, 128))

    whole = pl.BlockSpec((s, KV_WIDTH), lambda i: (0, 0))
    blk = pl.BlockSpec((BLOCK, ATTN_WIDTH), lambda i: (i, 0))
    return pl.pallas_call(
        body, name="swa_bwd", grid=(s // BLOCK,),
        in_specs=[pl.BlockSpec(memory_space=pltpu.SMEM), blk, whole, whole, blk],
        out_specs=[blk, whole, whole, pl.BlockSpec((N_Q_HEADS, 128), lambda i: (0, 0))],
        out_shape=[jax.ShapeDtypeStruct((s, ATTN_WIDTH), F32), jax.ShapeDtypeStruct((s, KV_WIDTH), F32),
                   jax.ShapeDtypeStruct((s, KV_WIDTH), F32), jax.ShapeDtypeStruct((N_Q_HEADS, 128), F32)],
        compiler_params=_params(1),
    )(sink, q, k, v, d_out)


CONV_CHUNK = 256


def _shift_rows(t, rows, down):
    n = t.shape[0]
    rolled = pltpu.roll(t, 1 if down else n - 1, 0)
    edge = 0 if down else n - 1
    return jnp.where(rows == edge, 0.0, rolled)


def _conv_specs(s):
    def z_spec(off):
        return pl.BlockSpec((s, CONV_CHUNK), lambda j, off=off: (0, off // CONV_CHUNK + j))
    chunk = pl.BlockSpec((s, CONV_CHUNK), lambda j: (0, j))
    w_spec = pl.BlockSpec((3, CONV_CHUNK), lambda j: (0, j))
    return z_spec(CU_OFF), z_spec(CB_OFF), z_spec(CC_OFF), chunk, w_spec


def _conv_fwd(z, conv_w, after=None):
    s = z.shape[0]
    cu_spec, cb_spec, cc_spec, chunk, w_spec = _conv_specs(s)

    def body(cu_ref, cb_ref, cc_ref, w_ref, *rest):
        o_ref = rest[-1]
        rows = lax.broadcasted_iota(jnp.int32, (s, 1), 0)
        t = cc_ref[...] * cu_ref[...]
        c3 = _shift_rows(t, rows, True) * w_ref[0:1, :] + t * w_ref[1:2, :] + _shift_rows(t, rows, False) * w_ref[2:3, :]
        o_ref[...] = (cb_ref[...] * c3).astype(BF16)

    return pl.pallas_call(
        body, name="conv_fwd", grid=(CONV_WIDTH // CONV_CHUNK,),
        in_specs=[cu_spec, cb_spec, cc_spec, w_spec] + ([] if after is None else [ANY]),
        out_specs=chunk,
        out_shape=jax.ShapeDtypeStruct((s, CONV_WIDTH), BF16),
        compiler_params=_params(1),
    )(z, z, z, conv_w, *([] if after is None else [after]))


def _conv_bwd(z, conv_w, d_co, dz):
    s = z.shape[0]
    cu_spec, cb_spec, cc_spec, chunk, w_spec = _conv_specs(s)
    n_chunks = CONV_WIDTH // CONV_CHUNK
    offsets = (CU_OFF, CB_OFF, CC_OFF)

    def body(cu_ref, cb_ref, cc_ref, w_ref, d_ref, dz_in_ref, dz_ref, dw_ref, buf, sems):
        j = pl.program_id(0)

        def copies(j_at):
            return [pltpu.make_async_copy(buf.at[h], dz_ref.at[:, pl.ds(off + j_at * CONV_CHUNK, CONV_CHUNK)], sems.at[h])
                    for h, off in enumerate(offsets)]

        rows = lax.broadcasted_iota(jnp.int32, (s, 1), 0)
        cu, cc = cu_ref[...], cc_ref[...]
        t = cc * cu
        t_dn, t_up = _shift_rows(t, rows, True), _shift_rows(t, rows, False)
        c3 = t_dn * w_ref[0:1, :] + t * w_ref[1:2, :] + t_up * w_ref[2:3, :]
        d = d_ref[...]
        dc3 = d * cb_ref[...]
        dw_ref[0:1, :] = jnp.sum(dc3 * t_dn, axis=0, keepdims=True)
        dw_ref[1:2, :] = jnp.sum(dc3 * t, axis=0, keepdims=True)
        dw_ref[2:3, :] = jnp.sum(dc3 * t_up, axis=0, keepdims=True)
        dt = _shift_rows(dc3, rows, False) * w_ref[0:1, :] + dc3 * w_ref[1:2, :] + _shift_rows(dc3, rows, True) * w_ref[2:3, :]

        @pl.when(j > 0)
        def _():
            for cp in copies(j):
                cp.wait()

        buf[0] = (dt * cc).astype(BF16)
        buf[1] = (d * c3).astype(BF16)
        buf[2] = (dt * cu).astype(BF16)
        for cp in copies(j):
            cp.start()

        @pl.when(j == n_chunks - 1)
        def _():
            for cp in copies(j):
                cp.wait()

    return pl.pallas_call(
        body, name="conv_bwd", grid=(n_chunks,),
        in_specs=[cu_spec, cb_spec, cc_spec, w_spec, chunk, ANY],
        out_specs=[ANY, w_spec],
        out_shape=[jax.ShapeDtypeStruct(dz.shape, dz.dtype), jax.ShapeDtypeStruct((3, CONV_WIDTH), F32)],
        input_output_aliases={5: 0},
        scratch_shapes=[pltpu.VMEM((3, s, CONV_CHUNK), BF16), pltpu.SemaphoreType.DMA((3,))],
        compiler_params=_params(1),
    )(z, z, z, conv_w, d_co, dz)


GATE_CHUNK = 512


def _gate_specs(s, d, tr):
    n_chunks = d // GATE_CHUNK
    za = pl.BlockSpec((tr, GATE_CHUNK), lambda j, i: (i, GL_OFF // GATE_CHUNK + j))
    zc = pl.BlockSpec((tr, GATE_CHUNK), lambda j, i: (i, GL_OFF // GATE_CHUNK + n_chunks + j))
    ba = pl.BlockSpec((1, GATE_CHUNK), lambda j, i: (0, j))
    bc = pl.BlockSpec((1, GATE_CHUNK), lambda j, i: (0, n_chunks + j))
    tile = pl.BlockSpec((tr, GATE_CHUNK), lambda j, i: (i, j))
    return za, zc, ba, bc, tile


def _gate_fwd(z, b_gate, ya, yc):
    s, d = ya.shape
    tr = _row_tile(s, 512)
    za, zc, ba, bc, tile = _gate_specs(s, d, tr)

    def body(za_ref, zc_ref, ba_ref, bc_ref, ya_ref, yc_ref, o_ref):
        ga = jax.nn.sigmoid(za_ref[...] + ba_ref[...])
        gc = jax.nn.sigmoid(zc_ref[...] + bc_ref[...])
        o_ref[...] = (ga * ya_ref[...] + gc * yc_ref[...]).astype(BF16)

    return pl.pallas_call(
        body, name="gate_fwd", grid=(d // GATE_CHUNK, s // tr),
        in_specs=[za, zc, ba, bc, tile, tile],
        out_specs=tile,
        out_shape=jax.ShapeDtypeStruct((s, d), BF16),
        compiler_params=_params(2),
    )(z, z, b_gate, b_gate, ya, yc)


def _gate_bwd(z, b_gate, ya, yc, dmix):
    s, d = ya.shape
    tr = _row_tile(s, 512)
    za, zc, ba, bc, tile = _gate_specs(s, d, tr)
    vec = pl.BlockSpec((1, GATE_CHUNK), lambda j, i: (0, j))
    n_rows = s // tr
    in_width = z.shape[1]

    def body(za_ref, zc_ref, ba_ref, bc_ref, ya_ref, yc_ref, dm_ref, dya_ref, dyc_ref, dz_ref, dba_ref, dbc_ref, buf, sems):
        j, i = pl.program_id(0), pl.program_id(1)

        def copies(j_at, i_at):
            rows = pl.ds(i_at * tr, tr)
            return [pltpu.make_async_copy(buf.at[h], dz_ref.at[rows, pl.ds(GL_OFF + h * d + j_at * GATE_CHUNK, GATE_CHUNK)],
                                          sems.at[h]) for h in range(2)]

        ga = jax.nn.sigmoid(za_ref[...] + ba_ref[...])
        gc = jax.nn.sigmoid(zc_ref[...] + bc_ref[...])
        dm = dm_ref[...]
        dya_ref[...] = (dm * ga).astype(BF16)
        dyc_ref[...] = (dm * gc).astype(BF16)
        dla = dm * ya_ref[...] * ga * (1.0 - ga)
        dlc = dm * yc_ref[...] * gc * (1.0 - gc)

        @pl.when(j * n_rows + i > 0)
        def _():
            for cp in copies(j, i):
                cp.wait()

        buf[0] = dla.astype(BF16)
        buf[1] = dlc.astype(BF16)
        for cp in copies(j, i):
            cp.start()

        @pl.when((j == d // GATE_CHUNK - 1) & (i == n_rows - 1))
        def _():
            for cp in copies(j, i):
                cp.wait()

        pa = jnp.sum(dla, axis=0, keepdims=True)
        pc = jnp.sum(dlc, axis=0, keepdims=True)

        @pl.when(i == 0)
        def _():
            dba_ref[...] = pa
            dbc_ref[...] = pc

        @pl.when(i > 0)
        def _():
            dba_ref[...] += pa
            dbc_ref[...] += pc

    big = jax.ShapeDtypeStruct((s, d), BF16)
    small = jax.ShapeDtypeStruct((1, d), F32)
    return pl.pallas_call(
        body, name="gate_bwd", grid=(d // GATE_CHUNK, n_rows),
        in_specs=[za, zc, ba, bc, tile, tile, tile],
        out_specs=[tile, tile, ANY, vec, vec],
        out_shape=[big, big, jax.ShapeDtypeStruct((s, in_width), BF16), small, small],
        scratch_shapes=[pltpu.VMEM((2, tr, GATE_CHUNK), BF16), pltpu.SemaphoreType.DMA((2,))],
        compiler_params=_params(2),
    )(z, z, b_gate, b_gate, ya, yc, dmix)


def _cross_probs(q_ref, kv_ref, hd):
    cols = slice(hd * HEAD_DIM, (hd + 1) * HEAD_DIM)
    qh = q_ref[:, cols]
    kh = kv_ref[:, cols]
    sc = lax.dot_general(qh, kh, (((1,), (1,)), ((), ())), preferred_element_type=F32) * ATTN_SCALE
    e = jnp.exp(sc - jnp.max(sc, axis=1, keepdims=True))
    return qh, kh, e * (1.0 / jnp.sum(e, axis=1, keepdims=True))


def _cross_fwd(qc, kvc):
    s = qc.shape[0]
    n_mem = kvc.shape[0]
    tq = _row_tile(s, 256)

    def body(q_ref, kv_ref, o_ref):
        for hd in range(MEM_HEADS):
            _, _, p = _cross_probs(q_ref, kv_ref, hd)
            vh = kv_ref[:, MEM_WIDTH + hd * HEAD_DIM:MEM_WIDTH + (hd + 1) * HEAD_DIM]
            o_ref[:, hd * HEAD_DIM:(hd + 1) * HEAD_DIM] = jnp.dot(p.astype(BF16), vh, preferred_element_type=F32).astype(BF16)

    return pl.pallas_call(
        body, name="cross_fwd", grid=(s // tq,),
        in_specs=[pl.BlockSpec((tq, MEM_WIDTH), lambda i: (i, 0)), pl.BlockSpec((n_mem, 2 * MEM_WIDTH), lambda i: (0, 0))],
        out_specs=pl.BlockSpec((tq, MEM_WIDTH), lambda i: (i, 0)),
        out_shape=jax.ShapeDtypeStruct((s, MEM_WIDTH), BF16),
        compiler_params=_params(1),
    )(qc, kvc)


def _cross_bwd(qc, kvc, d_out):
    s = qc.shape[0]
    n_mem = kvc.shape[0]
    tq = _row_tile(s, 256)

    def body(q_ref, kv_ref, do_ref, dq_ref, dkv_ref):
        @pl.when(pl.program_id(0) == 0)
        def _():
            dkv_ref[...] = jnp.zeros_like(dkv_ref)

        for hd in range(MEM_HEADS):
            cols = slice(hd * HEAD_DIM, (hd + 1) * HEAD_DIM)
            vcols = slice(MEM_WIDTH + hd * HEAD_DIM, MEM_WIDTH + (hd + 1) * HEAD_DIM)
            qh, kh, p = _cross_probs(q_ref, kv_ref, hd)
            doh = do_ref[:, cols]
            dp = lax.dot_general(doh, kv_ref[:, vcols], (((1,), (1,)), ((), ())), preferred_element_type=F32)
            ds = (p * (dp - jnp.sum(p * dp, axis=1, keepdims=True)) * ATTN_SCALE).astype(BF16)
            dq_ref[:, cols] = jnp.dot(ds, kh, preferred_element_type=F32).astype(BF16)
            dkv_ref[:, cols] += lax.dot_general(ds, qh, (((0,), (0,)), ((), ())), preferred_element_type=F32)
            dkv_ref[:, vcols] += lax.dot_general(p.astype(BF16), doh, (((0,), (0,)), ((), ())), preferred_element_type=F32)

    qspec = pl.BlockSpec((tq, MEM_WIDTH), lambda i: (i, 0))
    kvspec = pl.BlockSpec((n_mem, 2 * MEM_WIDTH), lambda i: (0, 0))
    return pl.pallas_call(
        body, name="cross_bwd", grid=(s // tq,),
        in_specs=[qspec, kvspec, qspec],
        out_specs=[qspec, kvspec],
        out_shape=[jax.ShapeDtypeStruct((s, MEM_WIDTH), BF16), jax.ShapeDtypeStruct((n_mem, 2 * MEM_WIDTH), F32)],
        compiler_params=_params(1),
    )(qc, kvc, d_out)


def _swiglu_fwd(up, gate):
    sg = jax.nn.sigmoid(gate)
    silu = gate * sg
    return silu * up, up * (sg * (1.0 + gate * (1.0 - sg))), silu


def _swiglu_bwd(d_act, dact_dgate, dact_dup):
    return d_act * dact_dgate.astype(F32), d_act * dact_dup.astype(F32)


GATHER_GROUPS = {"in": ("w_in", "conv_w"), "mid": ("w_attn_out", "w_conv_out", "w_o", "w_cq", "w_ckv", "w_co"),
                 "gate": ("w_gate",), "up": ("w_up",), "down": ("w_down",)}


def _local_step(xs, mems, target, small, fetch, reduce):
    s, d = xs.shape
    w4 = {}
    cos_t, sin_t = _rope_tables(s)

    def near(group, done, then, after):
        waits = [("direct", group)] + ([("pass_near", done), ("pass_far", done)] if done else [])
        starts = [("forward", group), ("pass_near", group)] + [("direct", g) for g in then]
        tok = fetch.step("gather_near_" + group, waits, starts, after)
        if done:
            w4.update(fetch.arrays(done))
        return tok

    def far(group, then, after):
        return fetch.step("gather_far_" + group, [("forward", group)], [("pass_far", group)] + [("direct", g) for g in then], after)

    def last(group, after):
        tok = fetch.step("gather_done_" + group, [("pass_near", group), ("pass_far", group)], [], after)
        w4.update(fetch.arrays(group))
        return tok

    h = _rmsnorm(xs, small["g_mix"], "norm_mix")
    slots_filled = [a for g in ("gate", "up", "down") for a in fetch.arrays(g).values()]
    chip_x, chip_y = reduce.place[0] // 2, reduce.place[0] % 2
    own_block = jnp.stack([2 * chip_x + chip_y]).astype(jnp.int32)
    near_blocks = jnp.stack([2 * (1 - chip_x) + chip_y, 2 * chip_x + (1 - chip_y)]).astype(jnp.int32)
    far_block = jnp.stack([2 * (1 - chip_x) + (1 - chip_y)]).astype(jnp.int32)
    z = _matmul_column_blocks(h, fetch.arrays("in")["w_in"], own_block, None, tm=512, name="in_proj_own")
    tok = near("in", None, ["mid"], [z] + slots_filled)
    tok = fetch.step("gather_near_done_in", [("pass_near", "in")], [], tok)
    z = _matmul_column_blocks(h, fetch.arrays("in")["w_in"], near_blocks, z, tm=1024, name="in_proj_near", after=tok)
    tok = far("in", [], z)
    tok = fetch.step("gather_done_in", [("pass_far", "in")], [], tok)
    w4.update(fetch.arrays("in"))
    z = _matmul_column_blocks(h, w4["w_in"], far_block, z, tm=1024, name="in_proj_far", after=tok)
    conv4 = w4["conv_w"]
    conv_w = conv4[:, :3, :].transpose(1, 0, 2).reshape(3, N_CHIPS * conv4.shape[2])
    c_in = w4["w_in"].shape[2]
    tok = near("mid", None, ["gate"], z)
    q_rot, k_rot, v_b = _rope_fwd(z, cos_t, sin_t)
    attn = _swa_fwd(q_rot, k_rot, v_b, small["sink"])
    co = _conv_fwd(z, conv_w)
    tok = far("mid", ["up"], attn)
    tok = last("mid", tok)
    w_o = w4["w_o"].reshape(-1, w4["w_o"].shape[-1])
    c_d = w4["w_attn_out"].shape[2]
    ya = _matmul(attn, w4["w_attn_out"], mode="nn", tm=2048, tn=c_d, tk=ATTN_WIDTH, out_dtypes=[F32], name="attn_out_proj",
                 b_blocks=N_CHIPS, after=tok)
    yc = _matmul(co, w4["w_conv_out"], mode="nn", tm=2048, tn=c_d, tk=CONV_WIDTH, out_dtypes=[F32], name="conv_out_proj",
                 b_blocks=N_CHIPS)
    mix = _gate_fwd(z, small["b_gate"], ya, yc)
    x1 = _matmul(mix, w_o, mode="nn", tm=1024, tn=1024, tk=d, out_dtypes=[F32], name="mix_out_proj", extras=[xs],
                 epilogue=_add_residual)
    tok = near("gate", None, ["down"], x1)
    w_cq = w4["w_cq"].reshape(-1, w4["w_cq"].shape[-1])
    w_ckv = w4["w_ckv"].reshape(-1, w4["w_ckv"].shape[-1])
    hc = _rmsnorm(x1, small["g_cross"], "norm_cross")
    memn = _rmsnorm(mems, small["g_mem"], "norm_mem")
    qc = _matmul(hc, w_cq, mode="nn", tm=2048, tn=MEM_WIDTH, tk=d, out_dtypes=[BF16], name="cross_q_proj", after=tok)
    kvc = _matmul(memn, w_ckv, mode="nn", tm=256, tn=2 * MEM_WIDTH, tk=d, out_dtypes=[BF16], name="cross_kv_proj")
    oc = _cross_fwd(qc, kvc)
    tok = far("gate", [], oc)
    x2 = _matmul(oc, w4["w_co"], mode="nn", tm=2048, tn=c_d, tk=MEM_WIDTH, out_dtypes=[F32], name="cross_out_proj",
                 extras=[x1], epilogue=_add_residual, b_blocks=N_CHIPS, after=tok)
    hf = _rmsnorm(x2, small["g_ffn"], "norm_ffn")
    tok = near("up", "gate", [], hf)
    c_ff = w4["w_gate"].shape[2]
    gate = _matmul(hf, w4["w_gate"], mode="nn", tm=1024, tn=c_ff, tk=d, out_dtypes=[F32], name="ffn_gate_proj", b_blocks=N_CHIPS,
                   after=tok)
    tok = far("up", [], gate)
    tok = near("down", "up", [], tok)
    act, dact_dgate, dact_dup = _matmul(hf, w4["w_up"], mode="nn", tm=1024, tn=c_ff, tk=d, out_dtypes=[BF16, BF16, BF16],
                                        name="ffn_up_proj", extras=[gate], epilogue=_swiglu_fwd, b_blocks=N_CHIPS, after=tok)
    tok = far("down", [], act)
    last("down", tok)
    w_down = w4["w_down"].reshape(-1, w4["w_down"].shape[-1])
    x3 = _matmul(act, w_down, mode="nn", tm=512, tn=512, tk=w_down.shape[0], out_dtypes=[F32], name="ffn_down_proj", extras=[x2],
                 epilogue=_add_residual)
    dx3, dx3b, sq, dg_final = _loss_head(x3, small["g_final"], target)

    da, du = _matmul(dx3b, w_down, mode="nt", tm=1024, tn=c_ff, tk=d, out_dtypes=[BF16, BF16], name="ffn_down_bwd",
                     extras=[dact_dgate, dact_dup], epilogue=_swiglu_bwd)
    core = reduce.core
    ffn_shape = dict(row_sharded=False, tm=1024, tn=c_ff)
    g_down = _matmul(act, dx3b, mode="tn", tm=c_ff, tn=1024, tk=s, out_dtypes=[BF16], name="ffn_down_wgrad")
    tok = reduce.add("down", {"w_down": g_down}, da)
    t_gate = _wgrad_half(hf, da, core, theirs=True, name="ffn_gate_wgrad_theirs", after=tok, **ffn_shape)
    tok = reduce.step("down", t_gate)
    t_up = _wgrad_half(hf, du, core, theirs=True, name="ffn_up_wgrad_theirs", after=tok, **ffn_shape)
    tok = reduce.send("ffn", {"w_gate": t_gate, "w_up": t_up}, dx3b)
    dhf = _matmul(da, w4["w_gate"], mode="nt", tm=512, tn=1024, tk=N_CHIPS * c_ff, out_dtypes=[F32], name="ffn_gate_bwd", b_blocks=N_CHIPS,
                  after=tok)
    got = reduce.received("ffn", dhf)
    p_gate = _wgrad_half(hf, da, core, theirs=False, name="ffn_gate_wgrad_mine", add=got["w_gate"], **ffn_shape)
    p_up = _wgrad_half(hf, du, core, theirs=False, name="ffn_up_wgrad_mine", add=got["w_up"], **ffn_shape)
    tok = reduce.add_parts("ffn", {"w_gate": p_gate, "w_up": p_up})
    dhf = _matmul(du, w4["w_up"], mode="nt", tm=512, tn=1024, tk=N_CHIPS * c_ff, out_dtypes=[F32], name="ffn_up_bwd", extras=[dhf],
                  epilogue=_add_residual, b_blocks=N_CHIPS, after=tok)
    tok = reduce.step("down", dhf)
    dx2, dx2b, dg_ffn = _rmsnorm_bwd(dhf, x2, small["g_ffn"], dx3, "norm_ffn_bwd")

    d_oc = _matmul(dx2b, w4["w_co"], mode="nt", tm=1024, tn=MEM_WIDTH, tk=d, out_dtypes=[BF16], name="cross_out_bwd",
                   b_blocks=N_CHIPS, after=tok)
    g_co = _matmul(oc, dx2b, mode="tn", tm=MEM_WIDTH, tn=c_d, tk=s, out_dtypes=[BF16], name="cross_out_wgrad", out_blocks=N_CHIPS)
    tok = reduce.step("down", g_co)
    dqc, dkvc = _cross_bwd(qc, kvc, d_oc)
    g_cq = _matmul(hc, dqc, mode="tn", tm=1024, tn=MEM_WIDTH, tk=s, out_dtypes=[BF16], name="cross_q_wgrad", after=tok)
    dhc = _matmul(dqc, w_cq, mode="nt", tm=1024, tn=1024, tk=MEM_WIDTH, out_dtypes=[F32], name="cross_q_bwd")
    g_ckv = _matmul(memn, dkvc, mode="tn", tm=1024, tn=2 * MEM_WIDTH, tk=mems.shape[0], out_dtypes=[BF16], name="cross_kv_wgrad")
    dmemn = _matmul(dkvc, w_ckv, mode="nt", tm=256, tn=1024, tk=2 * MEM_WIDTH, out_dtypes=[F32], name="cross_kv_bwd")
    _, _, dg_mem = _rmsnorm_bwd(dmemn, mems, small["g_mem"], None, "norm_mem_bwd")
    dx1, dx1b, dg_cross = _rmsnorm_bwd(dhc, x1, small["g_cross"], dx2, "norm_cross_bwd")

    dmix = _matmul(dx1b, w_o, mode="nt", tm=1024, tn=1024, tk=d, out_dtypes=[F32], name="mix_out_bwd")
    g_o = _matmul(mix, dx1b, mode="tn", tm=1024, tn=1024, tk=s, out_dtypes=[BF16], name="mix_out_wgrad")
    dya, dyc, dz, db_a, db_c = _gate_bwd(z, small["b_gate"], ya, yc, dmix)
    d_attn = _matmul(dya, w4["w_attn_out"], mode="nt", tm=1024, tn=ATTN_WIDTH, tk=d, out_dtypes=[BF16], name="attn_out_bwd",
                     b_blocks=N_CHIPS)
    g_ao = _matmul(attn, dya, mode="tn", tm=ATTN_WIDTH, tn=c_d, tk=s, out_dtypes=[BF16], name="attn_out_wgrad", out_blocks=N_CHIPS)
    d_co = _matmul(dyc, w4["w_conv_out"], mode="nt", tm=1024, tn=CONV_WIDTH, tk=d, out_dtypes=[F32], name="conv_out_bwd",
                   b_blocks=N_CHIPS)
    g_cvo = _matmul(co, dyc, mode="tn", tm=CONV_WIDTH, tn=c_d, tk=s, out_dtypes=[BF16], name="conv_out_wgrad", out_blocks=N_CHIPS)
    tok = reduce.step("ffn", g_cvo)
    tok = reduce.add("mid", {"w_co": g_co, "w_cq": g_cq, "w_ckv": g_ckv, "w_o": g_o, "w_attn_out": g_ao, "w_conv_out": g_cvo}, tok)
    dz, d_conv_w = _conv_bwd(z, conv_w, d_co, dz)
    dq_rot, dk_rot, dv, dsink = _swa_bwd(q_rot, k_rot, v_b, d_attn, small["sink"])
    tok = reduce.step("mid", dq_rot)
    dz = _rope_bwd(dq_rot, dk_rot, dv, cos_t, sin_t, dz)
    in_shape = dict(row_sharded=False, tm=1024, tn=c_in)
    t_in = _wgrad_half(h, dz, core, theirs=True, name="in_proj_wgrad_theirs", after=tok, **in_shape)
    tok = reduce.send("in", {"w_in": t_in}, dk_rot)
    tok = reduce.step("ffn", tok)
    tok = reduce.step("mid", tok)
    got = reduce.received("in", tok)
    p_in = _wgrad_half(h, dz, core, theirs=False, name="in_proj_wgrad_mine", add=got["w_in"], **in_shape)
    tok = reduce.add_parts("in", {"w_in": p_in})
    dh = _matmul(dz, w4["w_in"], mode="nt", tm=512, tn=512, tk=N_CHIPS * c_in, out_dtypes=[F32], name="in_proj_bwd", b_blocks=N_CHIPS,
                 after=tok)
    tok = reduce.step("mid", dh)
    grad_x, _, dg_mix = _rmsnorm_bwd(dh, xs, small["g_mix"], dx1, "norm_mix_bwd")

    small_grads = {
        "g_mix": dg_mix, "sink": dsink[:, 0], "b_gate": jnp.concatenate([db_a, db_c], axis=1), "g_cross": dg_cross,
        "g_mem": dg_mem, "g_ffn": dg_ffn, "g_final": dg_final, "conv_w": d_conv_w,
    }
    return sq, grad_x, small_grads


def _pair_sum(g4, ra, core, name):
    nb, rs, cs = g4.shape
    rh = rs // 2
    tr = _row_tile(rh, 256)
    per = rh // tr

    def body(c_ref, g_ref, r_ref, o_ref):
        o_ref[...] = (g_ref[...].astype(F32) + r_ref[...].astype(F32)).astype(BF16)

    plain = pl.BlockSpec((None, tr, cs), lambda j, i, c: (j, i, 0))
    return pl.pallas_call(
        body, name=name,
        grid_spec=pltpu.PrefetchScalarGridSpec(
            num_scalar_prefetch=1, grid=(nb, per),
            in_specs=[pl.BlockSpec((None, tr, cs), lambda j, i, c: (j, c[0] * per + i, 0)), plain],
            out_specs=plain),
        out_shape=jax.ShapeDtypeStruct((nb, rh, cs), BF16),
        compiler_params=_params(2),
    )(core, g4, ra)


def _quad_sum(parts, rc, place, name):
    _, rh, cs = parts.shape
    tr = _row_tile(rh, 256)
    per = rh // tr

    def body(p_ref, own_ref, r_ref, o_ref):
        acc = own_ref[...].astype(F32)
        for j in range(rc.shape[0]):
            acc = acc + r_ref[j].astype(F32)
        o_ref[...] = acc

    return pl.pallas_call(
        body, name=name,
        grid_spec=pltpu.PrefetchScalarGridSpec(
            num_scalar_prefetch=1, grid=(per,),
            in_specs=[pl.BlockSpec((None, tr, cs), lambda i, p: (p[0], i, 0)),
                      pl.BlockSpec((rc.shape[0], tr, cs), lambda i, p: (0, i, 0))],
            out_specs=pl.BlockSpec((tr, cs), lambda i, p: (p[1] * per + i, 0))),
        out_shape=jax.ShapeDtypeStruct((2 * rh, cs), F32),
        compiler_params=_params(1),
    )(place, parts, rc)


def _adamw_update(w, g, m, v):
    nm = ADAM_B1 * m + (1.0 - ADAM_B1) * g
    nv = ADAM_B2 * v + (1.0 - ADAM_B2) * (g * g)
    m_hat = nm / ADAM_C1
    v_hat = nv / ADAM_C2
    return -ADAM_LR * (m_hat / (jnp.sqrt(v_hat) + ADAM_EPS) + ADAM_WD * w), nm, nv


def _adamw_own_half(w, m, v, parts, rc, place, name, after=None):
    rows, cols = w.shape
    rh = rows // 2
    tr = _row_tile(rh, 256)
    per = rh // tr

    def body(p_ref, w_ref, m_ref, v_ref, own_ref, r_ref, *rest):
        gx_ref, g_ref, d_ref, nm_ref, nv_ref = rest[-5:]
        g = own_ref[...].astype(F32)
        for j in range(rc.shape[0]):
            g = g + r_ref[j].astype(F32)
        gx_ref[...] = g
        g_ref[...] = g
        d_ref[...], nm_ref[...], nv_ref[...] = _adamw_update(w_ref[...], g, m_ref[...], v_ref[...])

    mine = pl.BlockSpec((tr, cols), lambda i, p: (p[1] * per + i, 0))
    shape = jax.ShapeDtypeStruct((rows, cols), F32)
    return pl.pallas_call(
        body, name=name,
        grid_spec=pltpu.PrefetchScalarGridSpec(
            num_scalar_prefetch=1, grid=(per,),
            in_specs=[mine, mine, mine, pl.BlockSpec((None, tr, cols), lambda i, p: (p[0], i, 0)),
                      pl.BlockSpec((rc.shape[0], tr, cols), lambda i, p: (0, i, 0))] + ([] if after is None else [ANY]),
            out_specs=[mine] * 5),
        out_shape=[shape] * 5,
        compiler_params=_params(1),
    )(place, w, m, v, parts, rc, *([] if after is None else [after]))


def _adamw_other_half(w, m, v, g_exchanged, g, delta, new_m, new_v, place, name, after=None):
    rows, cols = w.shape
    rh = rows // 2
    tr = _row_tile(rh, 256)
    per = rh // tr

    def body(p_ref, w_ref, m_ref, v_ref, gx_ref, *rest):
        g_ref, d_ref, nm_ref, nv_ref = rest[-4:]
        gv = gx_ref[...]
        g_ref[...] = gv
        d_ref[...], nm_ref[...], nv_ref[...] = _adamw_update(w_ref[...], gv, m_ref[...], v_ref[...])

    other = pl.BlockSpec((tr, cols), lambda i, p: ((1 - p[1]) * per + i, 0))
    shape = jax.ShapeDtypeStruct((rows, cols), F32)
    n_after = 0 if after is None else 1
    return pl.pallas_call(
        body, name=name,
        grid_spec=pltpu.PrefetchScalarGridSpec(
            num_scalar_prefetch=1, grid=(per,),
            in_specs=[other] * 4 + [ANY] * (4 + n_after),
            out_specs=[other] * 4),
        out_shape=[shape] * 4,
        input_output_aliases={5: 0, 6: 1, 7: 2, 8: 3},
        compiler_params=_params(1),
    )(place, w, m, v, g_exchanged, g, delta, new_m, new_v, *([] if after is None else [after]))


def _cast_to_slot(w, place, dtype, name, after=None):
    rows, cols = w.shape
    tr = _row_tile(rows, 1024)

    def body(p_ref, w_ref, *rest):
        o_ref = rest[-1]
        o_ref[...] = w_ref[...].astype(dtype)

    return pl.pallas_call(
        body, name=name,
        grid_spec=pltpu.PrefetchScalarGridSpec(
            num_scalar_prefetch=1, grid=(rows // tr,),
            in_specs=[pl.BlockSpec((tr, cols), lambda i, p: (i, 0))] + ([] if after is None else [ANY]),
            out_specs=pl.BlockSpec((None, tr, cols), lambda i, p: (p[0], i, 0))),
        out_shape=jax.ShapeDtypeStruct((N_CHIPS, rows, cols), dtype),
        compiler_params=_params(1),
    )(place, w, *([] if after is None else [after]))


def _adamw(w, g, m, v, name, after=None):
    rows, cols = w.shape
    tr = _row_tile(rows, 256)

    def body(w_ref, g_ref, m_ref, v_ref, *rest):
        go_ref, d_ref, nm_ref, nv_ref = rest[-4:]
        gv = g_ref[...]
        go_ref[...] = gv
        d_ref[...], nm_ref[...], nv_ref[...] = _adamw_update(w_ref[...], gv, m_ref[...], v_ref[...])

    tile = pl.BlockSpec((tr, cols), lambda i: (i, 0))
    shape = jax.ShapeDtypeStruct((rows, cols), F32)
    return pl.pallas_call(
        body, name=name, grid=(rows // tr,),
        in_specs=[tile] * 4 + ([] if after is None else [ANY]), out_specs=[tile] * 4, out_shape=[shape] * 4,
        compiler_params=_params(1),
    )(w, g, m, v, *([] if after is None else [after]))


def _mesh_pos():
    return lax.axis_index("x"), lax.axis_index("y"), lax.axis_index("c")


def _other_chips(x, y):
    return [(1 - x, y), (x, 1 - y), (1 - x, 1 - y)]


def _half_rows(ref, which):
    rh = ref.shape[-2] // 2
    return ref.at[pl.ds(which * rh, rh), :]


def _remote(src, dst, send_sems, recv_sems, sem, to):
    return pltpu.make_async_remote_copy(src_ref=src, dst_ref=dst, send_sem=send_sems.at[sem], recv_sem=recv_sems.at[sem],
                                        device_id=to, device_id_type=MESH)


HBM = pl.BlockSpec(memory_space=pltpu.HBM)
SEM = pl.BlockSpec(memory_space=pltpu.SEMAPHORE)
DATAFLOW_EFFECT = pltpu.SideEffectType.DATAFLOW_SIDE_EFFECTING


def _in_hbm(arrays):
    return [pltpu.with_memory_space_constraint(a, pltpu.HBM) for a in arrays]


def _hbm_like(arrays):
    return [pltpu.HBM(a.shape, a.dtype) for a in arrays]


GATHER_COPIES_PER_ARRAY = {"direct": 2, "forward": 2, "pass_near": 2, "pass_far": 1}


def _gather_copies(kind, refs, x, y, c):
    me, near_x, near_y, far = 2 * x + y, 2 * (1 - x) + y, 2 * x + (1 - y), 2 * (1 - x) + (1 - y)
    to_x, to_y, sibling = (1 - x, y, c), (x, 1 - y, c), (x, y, 1 - c)
    out = []
    for ref in refs:
        rh = ref.shape[1] // 2
        rq = rh // 2

        def half(chip, ref=ref, rh=rh):
            return ref.at[chip, pl.ds(c * rh, rh), :]

        def quarter(chip, q, ref=ref, rh=rh, rq=rq):
            return ref.at[chip, pl.ds(c * rh + q * rq, rq), :]

        if kind == "direct":
            out += [(half(me), half(me), to_x), (half(me), half(me), to_y)]
        elif kind == "forward":
            out += [(quarter(near_x, 0), quarter(near_x, 0), to_y), (quarter(near_y, 1), quarter(near_y, 1), to_x)]
        elif kind == "pass_near":
            out += [(half(near_x), half(near_x), sibling), (half(near_y), half(near_y), sibling)]
        else:
            assert kind == "pass_far"
            out += [(half(far), half(far), sibling)]
    return out


def _gather_step(name, bufs, waits, starts, after):
    nb, nw, ns = len(bufs), len(waits), len(starts)
    after = [] if after is None else list(after) if isinstance(after, (list, tuple)) else [after]
    n_after = len(after)

    def body(*refs):
        ins = refs[:nb]
        wait_sems = refs[nb:nb + 2 * nw]
        start_sems = refs[nb + 2 * nw + n_after:nb + 2 * nw + n_after + 2 * ns]
        token = refs[-1]
        x, y, c = _mesh_pos()
        for j, (kind, idxs, _, _) in enumerate(waits):
            for i, (s_ref, d_ref, to) in enumerate(_gather_copies(kind, [ins[t] for t in idxs], x, y, c)):
                came = _remote(s_ref, d_ref, wait_sems[2 * j], wait_sems[2 * j + 1], i, to)
                came.wait_recv()
                came.wait_send()
        for j, (kind, idxs) in enumerate(starts):
            for i, (s_ref, d_ref, to) in enumerate(_gather_copies(kind, [ins[t] for t in idxs], x, y, c)):
                _remote(s_ref, d_ref, start_sems[2 * j], start_sems[2 * j + 1], i, to).start()
        token[...] = jnp.zeros_like(token)

    sems = []
    for kind, idxs in starts:
        sems += [pltpu.SemaphoreType.DMA((GATHER_COPIES_PER_ARRAY[kind] * len(idxs),))] * 2
    operands = _in_hbm(bufs) + [sem for w in waits for sem in w[2:]] + after
    outs = pl.pallas_call(
        body, name=name,
        in_specs=[HBM] * nb + [SEM] * (2 * nw) + [ANY] * n_after,
        out_specs=[SEM] * (2 * ns) + [HBM] * nb + [pl.BlockSpec(memory_space=pltpu.VMEM)],
        out_shape=sems + _hbm_like(bufs) + [jax.ShapeDtypeStruct((8, 128), F32)],
        input_output_aliases={i: 2 * ns + i for i in range(nb)},
        compiler_params=pltpu.CompilerParams(has_side_effects=DATAFLOW_EFFECT),
    )(*operands)
    return outs[2 * ns:2 * ns + nb], [(outs[2 * j], outs[2 * j + 1]) for j in range(ns)], outs[-1]


class _Gather:
    def __init__(self, groups):
        self.groups = groups
        self.bufs = {}
        self.in_flight = {}

    def put(self, slotted):
        self.bufs.update(slotted)

    def step(self, name, waits, starts, after=None):
        names = []
        for _, group in list(waits) + list(starts):
            names += [n for n in self.groups[group] if n not in names]
        index = {n: i for i, n in enumerate(names)}

        def members(group):
            return [index[n] for n in self.groups[group]]

        wait_args = [(kind, members(group)) + self.in_flight.pop((kind, group)) for kind, group in waits]
        start_args = [(kind, members(group)) for kind, group in starts]
        bufs, sems, token = _gather_step(name, [self.bufs[n] for n in names], wait_args, start_args, after)
        self.bufs.update(zip(names, bufs))
        for (kind, group), pair in zip(starts, sems):
            self.in_flight[(kind, group)] = pair
        return token

    def arrays(self, group):
        return {n: self.bufs[n] for n in self.groups[group]}


def _sibling_halves_copies(srcs, dsts, x, y, c):
    out = []
    for s_ref, d_ref in zip(srcs, dsts, strict=True):
        rh = s_ref.shape[1] // 2
        out.append((s_ref.at[:, pl.ds((1 - c) * rh, rh), :], d_ref, (x, y, 1 - c)))
    return out


def _to_sibling_copies(srcs, dsts, x, y, c):
    return [(s_ref, d_ref, (x, y, 1 - c)) for s_ref, d_ref in zip(srcs, dsts, strict=True)]


def _chip_copies(srcs, dsts, x, y, c):
    out = []
    for s_ref, d_ref in zip(srcs, dsts, strict=True):
        for k, (px, py) in enumerate(_other_chips(x, y)):
            out.append((s_ref.at[2 * px + py], d_ref.at[k], (px, py, c)))
    return out


def _join_copies(srcs, dsts, x, y, c):
    out = []
    for s_ref in srcs:
        mine = _half_rows(s_ref, c)
        out.append((mine, mine, (x, y, 1 - c)))
    return out


def _exchange_start(copies_fn, n_copies, srcs, fresh, after, name):
    ns, nb = len(srcs), len(srcs) + len(fresh)

    def body(*refs):
        bufs, send, recv, token = refs[:nb], refs[nb + 1], refs[nb + 2], refs[-1]
        x, y, c = _mesh_pos()
        for i, (s_ref, d_ref, to) in enumerate(copies_fn(bufs[:ns], bufs[ns:] if fresh else bufs[:ns], x, y, c)):
            _remote(s_ref, d_ref, send, recv, i, to).start()
        token[...] = jnp.zeros_like(token)

    sems = [pltpu.SemaphoreType.DMA((n_copies,))] * 2
    outs = pl.pallas_call(
        body, name=name,
        in_specs=[HBM] * nb + [ANY], out_specs=[SEM, SEM] + [HBM] * nb + [pl.BlockSpec(memory_space=pltpu.VMEM)],
        out_shape=sems + _hbm_like(list(srcs) + list(fresh)) + [jax.ShapeDtypeStruct((8, 128), F32)],
        input_output_aliases={i: 2 + i for i in range(nb)},
        compiler_params=pltpu.CompilerParams(has_side_effects=DATAFLOW_EFFECT),
    )(*_in_hbm(list(srcs) + list(fresh)), after)
    return outs[0], outs[1], outs[2:2 + ns], outs[2 + ns:2 + nb], outs[-1]


def _exchange_done(copies_fn, srcs, fresh, send, recv, after, name):
    ns, nb = len(srcs), len(srcs) + len(fresh)

    def body(*refs):
        bufs, send_in, recv_in = refs[:nb], refs[nb], refs[nb + 1]
        x, y, c = _mesh_pos()
        for i, (s_ref, d_ref, to) in enumerate(copies_fn(bufs[:ns], bufs[ns:] if fresh else bufs[:ns], x, y, c)):
            came = _remote(s_ref, d_ref, send_in, recv_in, i, to)
            came.wait_send()
            came.wait_recv()

    outs = pl.pallas_call(
        body, name=name,
        in_specs=[HBM] * nb + [SEM, SEM, ANY], out_specs=[HBM] * nb,
        out_shape=_hbm_like(list(srcs) + list(fresh)),
        input_output_aliases={i: i for i in range(nb)},
        compiler_params=pltpu.CompilerParams(has_side_effects=DATAFLOW_EFFECT),
    )(*_in_hbm(list(srcs) + list(fresh)), send, recv, after)
    return outs[:ns], outs[ns:]


class _Reduce:
    def __init__(self, place, core, shards, mom_m, mom_v):
        self.place, self.core = place, core
        self.shards, self.mom_m, self.mom_v = shards, mom_m, mom_v
        self.state = {}
        self.results = {}

    def add(self, group, grads, after):
        names = list(grads)
        g4s = [g.reshape((N_CHIPS, -1, g.shape[-1])) if g.ndim == 2 else g for g in grads.values()]
        fresh = [lax.empty((N_CHIPS, g.shape[1] // 2, g.shape[2]), BF16) for g in g4s]
        send, recv, g4s, fresh, token = _exchange_start(_sibling_halves_copies, len(names), g4s, fresh, after,
                                                        "pair_start_" + group)
        self.state[group] = (0, names, send, recv, g4s, fresh)
        return token

    def send(self, group, theirs, after):
        names, srcs = list(theirs), list(theirs.values())
        fresh = [lax.empty(s.shape, BF16) for s in srcs]
        send, recv, srcs, fresh, token = _exchange_start(_to_sibling_copies, len(names), srcs, fresh, after, "pair_start_" + group)
        self.state[group] = ("sent", names, send, recv, srcs, fresh)
        return token

    def received(self, group, after):
        stage, names, send, recv, srcs, fresh = self.state.pop(group)
        assert stage == "sent"
        _, got = _exchange_done(_to_sibling_copies, srcs, fresh, send, recv, after, "pair_done_" + group)
        return dict(zip(names, got))

    def add_parts(self, group, parts):
        names, srcs = list(parts), list(parts.values())
        fresh = [lax.empty((N_CHIPS - 1,) + p.shape[1:], BF16) for p in srcs]
        send, recv, srcs, fresh, token = _exchange_start(_chip_copies, 3 * len(names), srcs, fresh, self.core, "chips_start_" + group)
        self.state[group] = (1, names, send, recv, srcs, fresh)
        return token

    def step(self, group, after):
        stage, names, send, recv, srcs, fresh = self.state[group]
        if stage == 0:
            g4s, ras = _exchange_done(_sibling_halves_copies, srcs, fresh, send, recv, after, "pair_done_" + group)
            parts = [_pair_sum(g, r, self.core, "pair_sum_" + n) for g, r, n in zip(g4s, ras, names)]
            fresh = [lax.empty((N_CHIPS - 1,) + p.shape[1:], BF16) for p in parts]
            send, recv, parts, fresh, token = _exchange_start(_chip_copies, 3 * len(names), parts, fresh, self.core,
                                                              "chips_start_" + group)
            self.state[group] = (1, names, send, recv, parts, fresh)
            return token
        if stage == 1:
            parts, rcs = _exchange_done(_chip_copies, srcs, fresh, send, recv, after, "chips_done_" + group)
            token = None
            for n, p, r in zip(names, parts, rcs):
                self.results[n] = _adamw_own_half(self.shards[n], self.mom_m[n], self.mom_v[n], p, r, self.place,
                                                  "adamw_own_" + n, after=token)
                token = self.results[n][2]
            wholes = [self.results[n][0] for n in names]
            send, recv, wholes, _, token = _exchange_start(_join_copies, len(names), wholes, [], token, "join_start_" + group)
            self.state[group] = (2, names, send, recv, wholes, [])
            return token
        assert stage == 2
        wholes, _ = _exchange_done(_join_copies, srcs, [], send, recv, after, "join_done_" + group)
        token = None
        for n, exchanged in zip(names, wholes):
            _, g, d, nm, nv = self.results[n]
            self.results[n] = _adamw_other_half(self.shards[n], self.mom_m[n], self.mom_v[n], exchanged, g, d, nm, nv,
                                                self.place, "adamw_other_" + n, after=token)
            token = self.results[n][1]
        del self.state[group]
        return token


N_DEV = 8


def _all_reduce_small(v):
    def body(v_ref, o_ref, slots, send_sems, recv_sems):
        x, y, c = _mesh_pos()
        me = 4 * x + 2 * y + c
        slots[me] = v_ref[...]
        peers = []
        for r in range(1, N_DEV):
            fx, fy, fc = (r >> 2) & 1, (r >> 1) & 1, r & 1
            peers.append((x + fx - 2 * x * fx, y + fy - 2 * y * fy, c + fc - 2 * c * fc))
        sends = []
        for r, peer in enumerate(peers):
            cp = _remote(v_ref, slots.at[me], send_sems, recv_sems, r, peer)
            cp.start()
            sends.append(cp)
        for r, (px, py, pc) in enumerate(peers):
            landed = slots.at[4 * px + 2 * py + pc]
            _remote(landed, landed, send_sems, recv_sems, r, (px, py, pc)).wait_recv()
        for cp in sends:
            cp.wait_send()
        acc = slots[0]
        for i in range(1, N_DEV):
            acc = acc + slots[i]
        o_ref[...] = acc

    vm = pl.BlockSpec(memory_space=pltpu.VMEM)
    return pl.pallas_call(
        body, name="small_grads_all_reduce",
        in_specs=[vm], out_specs=vm,
        out_shape=jax.ShapeDtypeStruct(v.shape, v.dtype),
        scratch_shapes=[pltpu.VMEM((N_DEV,) + v.shape, v.dtype), pltpu.SemaphoreType.DMA((N_DEV - 1,)),
                        pltpu.SemaphoreType.DMA((N_DEV - 1,))],
    )(v)


MATRICES = ("w_in", "w_attn_out", "w_conv_out", "w_o", "w_cq", "w_ckv", "w_co", "w_gate", "w_up", "w_down")
VECTORS = ("g_mix", "b_gate", "g_cross", "g_mem", "g_ffn", "g_final", "conv_w", "sink")
WEIGHT_ORDER = ("g_mix", "w_in", "sink", "conv_w", "b_gate", "w_attn_out", "w_conv_out", "w_o", "g_cross", "g_mem", "w_cq",
                "w_ckv", "w_co", "g_ffn", "w_gate", "w_up", "w_down", "g_final")
CONV_PAD_ROWS = 32
SMALL_ROWS = 8


def _pack(pieces):
    flat = jnp.concatenate([p.reshape(-1) for p in pieces])
    lane_group = SMALL_ROWS * 128
    total = -(-flat.shape[0] // lane_group) * lane_group
    flat = jnp.pad(flat, (0, total - flat.shape[0]))
    return flat.reshape(SMALL_ROWS, total // SMALL_ROWS), [p.size for p in pieces]


def _unpack(packed, pieces):
    flat = packed.reshape(-1)
    out, off = [], 0
    for p in pieces:
        out.append(flat[off:off + p.size].reshape(p.shape))
        off += p.size
    return out


def kernel(x, mem, g_mix, w_in, sink, conv_w, b_gate, w_attn_out, w_conv_out, w_o, g_cross, g_mem, w_cq, w_ckv, w_co, g_ffn, w_gate, w_up, w_down, g_final, loss_target, m_g_mix, m_w_in, m_sink, m_conv_w, m_b_gate, m_w_attn_out, m_w_conv_out, m_w_o, m_g_cross, m_g_mem, m_w_cq, m_w_ckv, m_w_co, m_g_ffn, m_w_gate, m_w_up, m_w_down, m_g_final, v_g_mix, v_w_in, v_sink, v_conv_w, v_b_gate, v_w_attn_out, v_w_conv_out, v_w_o, v_g_cross, v_g_mem, v_w_cq, v_w_ckv, v_w_co, v_g_ffn, v_w_gate, v_w_up, v_w_down, v_g_final):
    given = dict(g_mix=g_mix, w_in=w_in, sink=sink, conv_w=conv_w, b_gate=b_gate, w_attn_out=w_attn_out, w_conv_out=w_conv_out,
                 w_o=w_o, g_cross=g_cross, g_mem=g_mem, w_cq=w_cq, w_ckv=w_ckv, w_co=w_co, g_ffn=g_ffn, w_gate=w_gate, w_up=w_up,
                 w_down=w_down, g_final=g_final)
    mom_m = dict(g_mix=m_g_mix, w_in=m_w_in, sink=m_sink, conv_w=m_conv_w, b_gate=m_b_gate, w_attn_out=m_w_attn_out,
                 w_conv_out=m_w_conv_out, w_o=m_w_o, g_cross=m_g_cross, g_mem=m_g_mem, w_cq=m_w_cq, w_ckv=m_w_ckv, w_co=m_w_co,
                 g_ffn=m_g_ffn, w_gate=m_w_gate, w_up=m_w_up, w_down=m_w_down, g_final=m_g_final)
    mom_v = dict(g_mix=v_g_mix, w_in=v_w_in, sink=v_sink, conv_w=v_conv_w, b_gate=v_b_gate, w_attn_out=v_w_attn_out,
                 w_conv_out=v_w_conv_out, w_o=v_w_o, g_cross=v_g_cross, g_mem=v_g_mem, w_cq=v_w_cq, w_ckv=v_w_ckv, w_co=v_w_co,
                 g_ffn=v_g_ffn, w_gate=v_w_gate, w_up=v_w_up, w_down=v_w_down, g_final=v_g_final)
    xs, mems, target = x[0], mem[0], loss_target[0]
    d_model = xs.shape[1]
    chip = 2 * lax.axis_index("x") + lax.axis_index("y")
    core = jnp.reshape(lax.axis_index("c"), (1,)).astype(jnp.int32)
    place = jnp.stack([chip, lax.axis_index("c")]).astype(jnp.int32)

    shards = {n: given[n][0] for n in MATRICES}
    conv_cols = conv_w.shape[2]
    conv_pad = jnp.pad(conv_w[0], ((0, CONV_PAD_ROWS - conv_w.shape[1]), (0, 0)))
    fetch = _Gather(GATHER_GROUPS)
    first = {"w_in": _cast_to_slot(shards["w_in"], place, BF16, "to_slot_w_in"),
             "conv_w": _cast_to_slot(conv_pad, place, F32, "to_slot_conv_w")}
    fetch.put(first)
    tok = fetch.step("gather_start", [], [("direct", "in")])
    fetch.put({n: _cast_to_slot(shards[n], place, BF16, "to_slot_" + n, after=tok) for n in MATRICES if n != "w_in"})
    small = {n: given[n] for n in ("g_mix", "b_gate", "g_cross", "g_mem", "g_ffn")}
    small["g_final"] = g_final[None]
    small["sink"] = sink[0]

    reduce = _Reduce(place, core, shards, {n: mom_m[n][0] for n in MATRICES}, {n: mom_v[n][0] for n in MATRICES})
    sq, grad_x, small_grads = _local_step(xs, mems, target, small, fetch, reduce)

    loss_part = 0.5 * sq[0:1, 0:1] / d_model
    pieces = [small_grads[n] for n in VECTORS] + [loss_part]
    packed, _ = _pack(pieces)
    summed = _unpack(_all_reduce_small(packed), pieces)
    loss = summed[-1][0, 0]
    small_sum = dict(zip(VECTORS, summed[:-1]))
    small_sum["conv_w"] = lax.dynamic_slice_in_dim(small_sum["conv_w"], chip * conv_cols, conv_cols, axis=1)

    grad_out, delta, new_m, new_v = {}, {}, {}, {}
    like = [given[n] for n in VECTORS]
    pw, _ = _pack(like)
    pg, _ = _pack([small_sum[n] for n in VECTORS])
    pm, _ = _pack([mom_m[n] for n in VECTORS])
    pv, _ = _pack([mom_v[n] for n in VECTORS])
    tok = reduce.step("in", pg)
    _, pd, pnm, pnv = _adamw(pw, pg, pm, pv, "adamw_small", after=tok)
    for n, g, d, nm, nv in zip(VECTORS, [small_sum[n] for n in VECTORS], _unpack(pd, like), _unpack(pnm, like), _unpack(pnv, like)):
        grad_out[n] = g.reshape(given[n].shape)
        delta[n], new_m[n], new_v[n] = d, nm, nv
    reduce.step("in", pd)
    for n in MATRICES:
        g, d, nm, nv = reduce.results[n]
        grad_out[n], delta[n], new_m[n], new_v[n] = g[None], d[None], nm[None], nv[None]

    return (loss, grad_x[None], *[grad_out[n] for n in WEIGHT_ORDER], *[delta[n] for n in WEIGHT_ORDER],
            *[new_m[n] for n in WEIGHT_ORDER], *[new_v[n] for n in WEIGHT_ORDER])
```

```python
import jax
import jax.numpy as jnp
from jax import lax
from jax.experimental import pallas as pl
from jax.experimental.pallas import tpu as pltpu

F32 = jnp.float32
BF16 = jnp.bfloat16
MESH = pl.DeviceIdType.MESH
ANY = pl.BlockSpec(memory_space=pl.ANY)

VMEM_LIMIT_BYTES = 56 * 1024 * 1024

N_CHIPS = 4
HEAD_DIM = 128
N_Q_HEADS = 8
N_KV_HEADS = 2
Q_GROUP = N_Q_HEADS // N_KV_HEADS
ATTN_WIDTH = N_Q_HEADS * HEAD_DIM
KV_WIDTH = N_KV_HEADS * HEAD_DIM
WINDOW = 128
BLOCK = 128
BAND = 3 * BLOCK
ROPE_THETA = 10000.0
CONV_WIDTH = 1024
MEM_HEADS = 4
MEM_WIDTH = MEM_HEADS * HEAD_DIM
RMS_EPS = 1e-6
NEG_INF = -1e30
ATTN_SCALE = HEAD_DIM ** -0.5

Q_OFF, K_OFF, V_OFF, CU_OFF, CB_OFF, CC_OFF, GL_OFF = 0, 1024, 1280, 1536, 2560, 3584, 4608

ADAM_LR = 0.001
ADAM_B1 = 0.9
ADAM_B2 = 0.999
ADAM_EPS = 1e-08
ADAM_WD = 0.01
ADAM_STEP = 10
ADAM_C1 = 1.0 - ADAM_B1 ** ADAM_STEP
ADAM_C2 = 1.0 - ADAM_B2 ** ADAM_STEP


def _params(n_grid_axes):
    return pltpu.CompilerParams(dimension_semantics=("arbitrary",) * n_grid_axes, vmem_limit_bytes=VMEM_LIMIT_BYTES)


BF16_SUBLANES = 16


def _row_tile(rows, want):
    if rows <= want:
        return rows
    for t in range(want, 0, -BF16_SUBLANES):
        if rows % t == 0:
            return t
    return rows


def _matmul(a, b, *, mode, tm, tn, out_dtypes, name, extras=(), epilogue=None, b_blocks=1, out_blocks=1, after=None):
    if mode == "tn":
        kdim, m = a.shape
    else:
        m, kdim = a.shape
    if b_blocks > 1:
        nb, brows, bcols = b.shape
        assert nb == b_blocks
        if mode == "nn":
            n = bcols * nb
            assert brows == kdim
        else:
            assert mode == "nt" and bcols * nb == kdim
            n = brows
    else:
        n = b.shape[0] if mode == "nt" else b.shape[1]
    tm, tn = min(tm, m), min(tn, n)
    tk = kdim
    assert m % tm == 0 and n % tn == 0, (name, m, n, tm, tn)
    n_extra, n_out = len(extras), len(out_dtypes)
    n_after = 0 if after is None else 1

    if mode == "tn":
        a_spec = pl.BlockSpec((tk, tm), lambda j, i, k: (k, i))
        dims = (((0,), (0,)), ((), ()))
    else:
        a_spec = pl.BlockSpec((tm, tk), lambda j, i, k: (i, k))
        dims = (((1,), (0,)), ((), ())) if mode == "nn" else (((1,), (1,)), ((), ()))

    if b_blocks > 1 and mode == "nn":
        per = b.shape[2] // tn
        assert b.shape[2] % tn == 0
        b_spec = pl.BlockSpec((None, tk, tn), lambda j, i, k: (j // per, k, j % per))
    elif b_blocks > 1:
        b_spec = pl.BlockSpec((b_blocks, tn, b.shape[2]), lambda j, i, k: (0, j, 0))
    elif mode == "nt":
        b_spec = pl.BlockSpec((tn, tk), lambda j, i, k: (j, k))
    else:
        b_spec = pl.BlockSpec((tk, tn), lambda j, i, k: (k, j))

    tile_spec = pl.BlockSpec((tm, tn), lambda j, i, k: (i, j))
    if out_blocks > 1:
        ncols = n // out_blocks
        assert ncols % tn == 0
        oper = ncols // tn
        out_spec = pl.BlockSpec((None, tm, tn), lambda j, i, k: (j // oper, i, j % oper))
        out_shape = [jax.ShapeDtypeStruct((out_blocks, m, ncols), dt) for dt in out_dtypes]
    else:
        out_spec = tile_spec
        out_shape = [jax.ShapeDtypeStruct((m, n), dt) for dt in out_dtypes]

    def body(a_ref, b_ref, *rest):
        extra_refs = rest[:n_extra]
        out_refs = rest[n_extra + n_after:n_extra + n_after + n_out]
        if mode == "nt" and b_blocks > 1:
            cs = b.shape[2]
            acc = None
            for jb in range(b_blocks):
                prod = lax.dot_general(a_ref[:, jb * cs:(jb + 1) * cs].astype(BF16), b_ref[jb].astype(BF16), dims,
                                       preferred_element_type=F32)
                acc = prod if acc is None else acc + prod
        else:
            acc = lax.dot_general(a_ref[...].astype(BF16), b_ref[...].astype(BF16), dims, preferred_element_type=F32)
        tiles = (acc,) if epilogue is None else epilogue(acc, *[r[...] for r in extra_refs])
        for o_ref, t in zip(out_refs, tiles, strict=True):
            o_ref[...] = t.astype(o_ref.dtype)

    outs = pl.pallas_call(
        body,
        name=name,
        grid=(n // tn, m // tm, 1),
        in_specs=[a_spec, b_spec] + [tile_spec] * n_extra + [ANY] * n_after,
        out_specs=[out_spec] * n_out,
        out_shape=out_shape,
        compiler_params=_params(3),
    )(a, b, *extras, *([] if after is None else [after]))
    return outs[0] if n_out == 1 else outs


def _add_residual(acc, res):
    return (acc + res,)


def _matmul_column_blocks(a, b4, blocks, out, *, tm, name, after=None):
    m, kdim = a.shape
    nb, _, cols = b4.shape
    tm = min(tm, m)
    assert m % tm == 0

    def body(j_ref, a_ref, b_ref, *rest):
        rest[-1][...] = jnp.dot(a_ref[...], b_ref[...], preferred_element_type=F32)

    extra = ([] if out is None else [out]) + ([] if after is None else [after])
    n_blocks = blocks.shape[0]
    return pl.pallas_call(
        body, name=name,
        grid_spec=pltpu.PrefetchScalarGridSpec(
            num_scalar_prefetch=1, grid=(n_blocks, m // tm),
            in_specs=[pl.BlockSpec((tm, kdim), lambda j, i, blk: (i, 0)),
                      pl.BlockSpec((None, kdim, cols), lambda j, i, blk: (blk[j], 0, 0))] + [ANY] * len(extra),
            out_specs=pl.BlockSpec((tm, cols), lambda j, i, blk: (i, blk[j]))),
        out_shape=jax.ShapeDtypeStruct((m, nb * cols), F32),
        input_output_aliases={} if out is None else {3: 0},
        compiler_params=_params(2),
    )(blocks, a, b4, *extra)


def _wgrad_half(a, b, core, *, theirs, row_sharded, tm, tn, name, add=None, after=None):
    kdim, m = a.shape
    n = b.shape[1]
    rs, cs = (m // N_CHIPS, n) if row_sharded else (m, n // N_CHIPS)
    rh = rs // 2
    tm, tn = min(tm, rh), min(tn, cs)
    assert rh % tm == 0 and cs % tn == 0, (name, rh, cs, tm, tn)
    mh, per = rh // tm, cs // tn
    has_add = add is not None

    def half(c):
        return 1 - c[0] if theirs else c[0]

    if row_sharded:
        grid = (n // tn, N_CHIPS * mh)
        a_spec = pl.BlockSpec((kdim, tm), lambda j, r, c: (0, ((r // mh) * 2 + half(c)) * mh + r % mh))
        o_spec = pl.BlockSpec((None, tm, tn), lambda j, r, c: (r // mh, r % mh, j))
    else:
        grid = (n // tn, mh)
        a_spec = pl.BlockSpec((kdim, tm), lambda j, r, c: (0, half(c) * mh + r))
        o_spec = pl.BlockSpec((None, tm, tn), lambda j, r, c: (j // per, r, j % per))
    b_spec = pl.BlockSpec((kdim, tn), lambda j, r, c: (0, j))

    def body(c_ref, a_ref, b_ref, *rest):
        o_ref = rest[-1]
        acc = lax.dot_general(a_ref[...].astype(BF16), b_ref[...].astype(BF16), (((0,), (0,)), ((), ())),
                              preferred_element_type=F32)
        if has_add:
            acc = acc + rest[0][...].astype(F32)
        o_ref[...] = acc.astype(BF16)

    operands = [a, b] + ([add] if has_add else []) + ([] if after is None else [after])
    return pl.pallas_call(
        body, name=name,
        grid_spec=pltpu.PrefetchScalarGridSpec(
            num_scalar_prefetch=1, grid=grid,
            in_specs=[a_spec, b_spec] + ([o_spec] if has_add else []) + ([] if after is None else [ANY]),
            out_specs=o_spec),
        out_shape=jax.ShapeDtypeStruct((N_CHIPS, rh, cs), BF16),
        compiler_params=_params(2),
    )(core, *operands)


def _rstd(x):
    return lax.rsqrt(jnp.mean(x * x, axis=-1, keepdims=True) + RMS_EPS)


def _rmsnorm(x, g, name):
    s, d = x.shape
    tr = _row_tile(s, 512)

    def body(x_ref, g_ref, o_ref):
        xv = x_ref[...]
        o_ref[...] = (xv * _rstd(xv) * g_ref[...]).astype(BF16)

    return pl.pallas_call(
        body, name=name, grid=(s // tr,),
        in_specs=[pl.BlockSpec((tr, d), lambda i: (i, 0)), pl.BlockSpec((1, d), lambda i: (0, 0))],
        out_specs=pl.BlockSpec((tr, d), lambda i: (i, 0)),
        out_shape=jax.ShapeDtypeStruct((s, d), BF16),
        compiler_params=_params(1),
    )(x, g)


def _rmsnorm_bwd(dh, x, g, dres, name):
    s, d = x.shape
    tr = _row_tile(s, 512)
    has_res = dres is not None

    def body(*refs):
        if has_res:
            dh_ref, x_ref, g_ref, res_ref, dx_ref, dxb_ref, dg_ref = refs
        else:
            dh_ref, x_ref, g_ref, dx_ref, dxb_ref, dg_ref = refs
        xv = x_ref[...]
        dhv = dh_ref[...].astype(F32)
        r = _rstd(xv)
        xn = xv * r
        dhg = dhv * g_ref[...]
        dx = r * (dhg - xn * jnp.mean(dhg * xn, axis=-1, keepdims=True))
        if has_res:
            dx = dx + res_ref[...]
        dx_ref[...] = dx
        dxb_ref[...] = dx.astype(BF16)
        part = jnp.sum(dhv * xn, axis=0, keepdims=True)

        @pl.when(pl.program_id(0) == 0)
        def _():
            dg_ref[...] = part

        @pl.when(pl.program_id(0) > 0)
        def _():
            dg_ref[...] += part

    row = pl.BlockSpec((tr, d), lambda i: (i, 0))
    vec = pl.BlockSpec((1, d), lambda i: (0, 0))
    return pl.pallas_call(
        body, name=name, grid=(s // tr,),
        in_specs=[row, row, vec] + ([row] if has_res else []),
        out_specs=[row, row, vec],
        out_shape=[jax.ShapeDtypeStruct((s, d), F32), jax.ShapeDtypeStruct((s, d), BF16), jax.ShapeDtypeStruct((1, d), F32)],
        compiler_params=_params(1),
    )(*([dh, x, g] + ([dres] if has_res else [])))


def _loss_head(x3, g, target):
    s, d = x3.shape
    tr = _row_tile(s, 512)

    def body(x_ref, g_ref, t_ref, dx_ref, dxb_ref, sq_ref, dg_ref):
        xv = x_ref[...]
        gv = g_ref[...]
        r = _rstd(xv)
        xn = xv * r
        err = xn * gv - t_ref[...]
        dy = err * (1.0 / d)
        dyg = dy * gv
        dx = r * (dyg - xn * jnp.mean(dyg * xn, axis=-1, keepdims=True))
        dx_ref[...] = dx
        dxb_ref[...] = dx.astype(BF16)
        sq = jnp.sum(jnp.sum(err * err, axis=1, keepdims=True), axis=0, keepdims=True)
        sq = jnp.broadcast_to(sq, (1, 128))
        part = jnp.sum(dy * xn, axis=0, keepdims=True)

        @pl.when(pl.program_id(0) == 0)
        def _():
            sq_ref[...] = sq
            dg_ref[...] = part

        @pl.when(pl.program_id(0) > 0)
        def _():
            sq_ref[...] += sq
            dg_ref[...] += part

    row = pl.BlockSpec((tr, d), lambda i: (i, 0))
    vec = pl.BlockSpec((1, d), lambda i: (0, 0))
    return pl.pallas_call(
        body, name="loss_head", grid=(s // tr,),
        in_specs=[row, vec, row],
        out_specs=[row, row, pl.BlockSpec((1, 128), lambda i: (0, 0)), vec],
        out_shape=[jax.ShapeDtypeStruct((s, d), F32), jax.ShapeDtypeStruct((s, d), BF16),
                   jax.ShapeDtypeStruct((1, 128), F32), jax.ShapeDtypeStruct((1, d), F32)],
        compiler_params=_params(1),
    )(x3, g, target)


def _rope_tables(s):
    inv = 1.0 / (ROPE_THETA ** (jnp.arange(0, HEAD_DIM, 2, dtype=F32) / HEAD_DIM))
    ang = jnp.arange(s, dtype=F32)[:, None] * inv[None, :]
    cos, sin = jnp.cos(ang), jnp.sin(ang)
    return jnp.concatenate([cos, cos], axis=1), jnp.concatenate([-sin, sin], axis=1)


def _swap_halves(t):
    return pltpu.roll(t, HEAD_DIM // 2, 1)


def _rope_fwd(z, cos_t, sin_t):
    s = z.shape[0]
    tr = _row_tile(s, 256)

    def body(zq_ref, zk_ref, zv_ref, c_ref, s_ref, q_ref, k_ref, v_ref):
        c, sn = c_ref[...], s_ref[...]
        for hd in range(N_Q_HEADS):
            cols = slice(hd * HEAD_DIM, (hd + 1) * HEAD_DIM)
            t = zq_ref[:, cols]
            q_ref[:, cols] = (t * c + _swap_halves(t) * sn).astype(BF16)
        for hd in range(N_KV_HEADS):
            cols = slice(hd * HEAD_DIM, (hd + 1) * HEAD_DIM)
            t = zk_ref[:, cols]
            k_ref[:, cols] = (t * c + _swap_halves(t) * sn).astype(BF16)
        v_ref[...] = zv_ref[...].astype(BF16)

    tab = pl.BlockSpec((tr, HEAD_DIM), lambda i: (i, 0))
    return pl.pallas_call(
        body, name="rope_fwd", grid=(s // tr,),
        in_specs=[pl.BlockSpec((tr, ATTN_WIDTH), lambda i: (i, Q_OFF // ATTN_WIDTH)),
                  pl.BlockSpec((tr, KV_WIDTH), lambda i: (i, K_OFF // KV_WIDTH)),
                  pl.BlockSpec((tr, KV_WIDTH), lambda i: (i, V_OFF // KV_WIDTH)), tab, tab],
        out_specs=[pl.BlockSpec((tr, ATTN_WIDTH), lambda i: (i, 0)), pl.BlockSpec((tr, KV_WIDTH), lambda i: (i, 0)),
                   pl.BlockSpec((tr, KV_WIDTH), lambda i: (i, 0))],
        out_shape=[jax.ShapeDtypeStruct((s, ATTN_WIDTH), BF16), jax.ShapeDtypeStruct((s, KV_WIDTH), BF16),
                   jax.ShapeDtypeStruct((s, KV_WIDTH), BF16)],
        compiler_params=_params(1),
    )(z, z, z, cos_t, sin_t)


def _rope_bwd(dq_rot, dk_rot, dv, cos_t, sin_t, dz):
    s = dq_rot.shape[0]
    tr = _row_tile(s, 256)
    qkv_width = V_OFF + KV_WIDTH

    def body(dq_ref, dk_ref, dv_ref, c_ref, s_ref, dz_in_ref, o_ref):
        c, sn = c_ref[...], s_ref[...]
        for hd in range(N_Q_HEADS):
            t = dq_ref[:, hd * HEAD_DIM:(hd + 1) * HEAD_DIM]
            o_ref[:, Q_OFF + hd * HEAD_DIM:Q_OFF + (hd + 1) * HEAD_DIM] = (t * c + _swap_halves(t * sn)).astype(BF16)
        for hd in range(N_KV_HEADS):
            t = dk_ref[:, hd * HEAD_DIM:(hd + 1) * HEAD_DIM]
            o_ref[:, K_OFF + hd * HEAD_DIM:K_OFF + (hd + 1) * HEAD_DIM] = (t * c + _swap_halves(t * sn)).astype(BF16)
        o_ref[:, V_OFF:V_OFF + KV_WIDTH] = dv_ref[...].astype(BF16)

    tab = pl.BlockSpec((tr, HEAD_DIM), lambda i: (i, 0))
    wide = pl.BlockSpec((tr, ATTN_WIDTH), lambda i: (i, 0))
    narrow = pl.BlockSpec((tr, KV_WIDTH), lambda i: (i, 0))
    return pl.pallas_call(
        body, name="rope_bwd", grid=(s // tr,),
        in_specs=[wide, narrow, narrow, tab, tab, ANY],
        out_specs=pl.BlockSpec((tr, qkv_width), lambda i: (i, 0)),
        out_shape=jax.ShapeDtypeStruct(dz.shape, dz.dtype),
        input_output_aliases={5: 0},
        compiler_params=_params(1),
    )(dq_rot, dk_rot, dv, cos_t, sin_t, dz)


def _swa_band(i, s):
    return pl.multiple_of(jnp.clip((i - 1) * BLOCK, 0, s - BAND), BLOCK)


SWA_HEADS_PER_PASS = Q_GROUP


def _swa_probs(q_ref, k_ref, sink_ref, heads, start, valid):
    kv = heads[0] // Q_GROUP
    cols = slice(kv * HEAD_DIM, (kv + 1) * HEAD_DIM)
    kb = k_ref[pl.ds(start, BAND), cols]
    qg = jnp.concatenate([q_ref[:, hd * HEAD_DIM:(hd + 1) * HEAD_DIM] for hd in heads], axis=0)
    sc = lax.dot_general(qg, kb, (((1,), (1,)), ((), ())), preferred_element_type=F32) * ATTN_SCALE
    sc = jnp.where(valid, sc, NEG_INF)
    sk = jnp.concatenate([jnp.full((BLOCK, 1), sink_ref[hd], F32) for hd in heads], axis=0)
    mx = jnp.maximum(jnp.max(sc, axis=1, keepdims=True), sk)
    e = jnp.exp(sc - mx)
    es = jnp.exp(sk - mx)
    inv = 1.0 / (jnp.sum(e, axis=1, keepdims=True) + es)
    return qg, kb, e * inv, es * inv


def _swa_head_passes():
    return [list(range(h0, h0 + SWA_HEADS_PER_PASS)) for h0 in range(0, N_Q_HEADS, SWA_HEADS_PER_PASS)]


def _swa_valid(i, start):
    q_pos = i * BLOCK + lax.broadcasted_iota(jnp.int32, (BLOCK, 1), 0)
    q_pos = jnp.concatenate([q_pos] * SWA_HEADS_PER_PASS, axis=0)
    k_pos = start + lax.broadcasted_iota(jnp.int32, (1, BAND), 1)
    return jnp.abs(k_pos - q_pos) <= WINDOW


def _swa_fwd(q, k, v, sink):
    s = q.shape[0]
    assert s % BLOCK == 0 and s >= BAND

    def body(sink_ref, q_ref, k_ref, v_ref, o_ref):
        i = pl.program_id(0)
        start = _swa_band(i, s)
        valid = _swa_valid(i, start)
        for heads in _swa_head_passes():
            kv = heads[0] // Q_GROUP
            _, _, p, _ = _swa_probs(q_ref, k_ref, sink_ref, heads, start, valid)
            vb = v_ref[pl.ds(start, BAND), kv * HEAD_DIM:(kv + 1) * HEAD_DIM]
            o = jnp.dot(p.astype(BF16), vb, preferred_element_type=F32)
            for g, hd in enumerate(heads):
                o_ref[:, hd * HEAD_DIM:(hd + 1) * HEAD_DIM] = o[g * BLOCK:(g + 1) * BLOCK].astype(BF16)

    whole = pl.BlockSpec((s, KV_WIDTH), lambda i: (0, 0))
    blk = pl.BlockSpec((BLOCK, ATTN_WIDTH), lambda i: (i, 0))
    return pl.pallas_call(
        body, name="swa_fwd", grid=(s // BLOCK,),
        in_specs=[pl.BlockSpec(memory_space=pltpu.SMEM), blk, whole, whole],
        out_specs=blk,
        out_shape=jax.ShapeDtypeStruct((s, ATTN_WIDTH), BF16),
        compiler_params=_params(1),
    )(sink, q, k, v)


def _swa_bwd(q, k, v, d_out, sink):
    s = q.shape[0]

    def body(sink_ref, q_ref, k_ref, v_ref, do_ref, dq_ref, dk_ref, dv_ref, dsink_ref):
        i = pl.program_id(0)

        @pl.when(i == 0)
        def _():
            dk_ref[...] = jnp.zeros_like(dk_ref)
            dv_ref[...] = jnp.zeros_like(dv_ref)
            dsink_ref[...] = jnp.zeros_like(dsink_ref)

        start = _swa_band(i, s)
        valid = _swa_valid(i, start)
        for heads in _swa_head_passes():
            kv = heads[0] // Q_GROUP
            cols = slice(kv * HEAD_DIM, (kv + 1) * HEAD_DIM)
            qg, kb, p, p_sink = _swa_probs(q_ref, k_ref, sink_ref, heads, start, valid)
            vb = v_ref[pl.ds(start, BAND), cols]
            dog = jnp.concatenate([do_ref[:, hd * HEAD_DIM:(hd + 1) * HEAD_DIM] for hd in heads], axis=0)
            dp = lax.dot_general(dog, vb, (((1,), (1,)), ((), ())), preferred_element_type=F32)
            delta = jnp.sum(p * dp, axis=1, keepdims=True)
            ds = (p * (dp - delta) * ATTN_SCALE).astype(BF16)
            dqg = jnp.dot(ds, kb, preferred_element_type=F32)
            dk_ref[pl.ds(start, BAND), cols] += lax.dot_general(ds, qg, (((0,), (0,)), ((), ())), preferred_element_type=F32)
            dv_ref[pl.ds(start, BAND), cols] += lax.dot_general(p.astype(BF16), dog, (((0,), (0,)), ((), ())),
                                                                 preferred_element_type=F32)
            dsk = p_sink * delta
            for g, hd in enumerate(heads):
                dq_ref[:, hd * HEAD_DIM:(hd + 1) * HEAD_DIM] = dqg[g * BLOCK:(g + 1) * BLOCK]
                tot = jnp.sum(dsk[g * BLOCK:(g + 1) * BLOCK], axis=0, keepdims=True)
                dsink_ref[hd:hd + 1, :] -= jnp.broadcast_to(tot, (1, 128))

    whole = pl.BlockSpec((s, KV_WIDTH), lambda i: (0, 0))
    blk = pl.BlockSpec((BLOCK, ATTN_WIDTH), lambda i: (i, 0))
    return pl.pallas_call(
        body, name="swa_bwd", grid=(s // BLOCK,),
        in_specs=[pl.BlockSpec(memory_space=pltpu.SMEM), blk, whole, whole, blk],
        out_specs=[blk, whole, whole, pl.BlockSpec((N_Q_HEADS, 128), lambda i: (0, 0))],
        out_shape=[jax.ShapeDtypeStruct((s, ATTN_WIDTH), F32), jax.ShapeDtypeStruct((s, KV_WIDTH), F32),
                   jax.ShapeDtypeStruct((s, KV_WIDTH), F32), jax.ShapeDtypeStruct((N_Q_HEADS, 128), F32)],
        compiler_params=_params(1),
    )(sink, q, k, v, d_out)


CONV_CHUNK = 256


def _shift_rows(t, rows, down):
    n = t.shape[0]
    rolled = pltpu.roll(t, 1 if down else n - 1, 0)
    edge = 0 if down else n - 1
    return jnp.where(rows == edge, 0.0, rolled)


def _conv_specs(s):
    def z_spec(off):
        return pl.BlockSpec((s, CONV_CHUNK), lambda j, off=off: (0, off // CONV_CHUNK + j))
    chunk = pl.BlockSpec((s, CONV_CHUNK), lambda j: (0, j))
    w_spec = pl.BlockSpec((3, CONV_CHUNK), lambda j: (0, j))
    return z_spec(CU_OFF), z_spec(CB_OFF), z_spec(CC_OFF), chunk, w_spec


def _conv_fwd(z, conv_w):
    s = z.shape[0]
    cu_spec, cb_spec, cc_spec, chunk, w_spec = _conv_specs(s)

    def body(cu_ref, cb_ref, cc_ref, w_ref, o_ref):
        rows = lax.broadcasted_iota(jnp.int32, (s, 1), 0)
        t = cc_ref[...] * cu_ref[...]
        c3 = _shift_rows(t, rows, True) * w_ref[0:1, :] + t * w_ref[1:2, :] + _shift_rows(t, rows, False) * w_ref[2:3, :]
        o_ref[...] = (cb_ref[...] * c3).astype(BF16)

    return pl.pallas_call(
        body, name="conv_fwd", grid=(CONV_WIDTH // CONV_CHUNK,),
        in_specs=[cu_spec, cb_spec, cc_spec, w_spec],
        out_specs=chunk,
        out_shape=jax.ShapeDtypeStruct((s, CONV_WIDTH), BF16),
        compiler_params=_params(1),
    )(z, z, z, conv_w)


def _conv_bwd(z, conv_w, d_co, dz):
    s = z.shape[0]
    cu_spec, cb_spec, cc_spec, chunk, w_spec = _conv_specs(s)
    n_chunks = CONV_WIDTH // CONV_CHUNK
    offsets = (CU_OFF, CB_OFF, CC_OFF)

    def body(cu_ref, cb_ref, cc_ref, w_ref, d_ref, dz_in_ref, dz_ref, dw_ref, buf, sems):
        j = pl.program_id(0)

        def copies(j_at):
            return [pltpu.make_async_copy(buf.at[h], dz_ref.at[:, pl.ds(off + j_at * CONV_CHUNK, CONV_CHUNK)], sems.at[h])
                    for h, off in enumerate(offsets)]

        rows = lax.broadcasted_iota(jnp.int32, (s, 1), 0)
        cu, cc = cu_ref[...], cc_ref[...]
        t = cc * cu
        t_dn, t_up = _shift_rows(t, rows, True), _shift_rows(t, rows, False)
        c3 = t_dn * w_ref[0:1, :] + t * w_ref[1:2, :] + t_up * w_ref[2:3, :]
        d = d_ref[...]
        dc3 = d * cb_ref[...]
        dw_ref[0:1, :] = jnp.sum(dc3 * t_dn, axis=0, keepdims=True)
        dw_ref[1:2, :] = jnp.sum(dc3 * t, axis=0, keepdims=True)
        dw_ref[2:3, :] = jnp.sum(dc3 * t_up, axis=0, keepdims=True)
        dt = _shift_rows(dc3, rows, False) * w_ref[0:1, :] + dc3 * w_ref[1:2, :] + _shift_rows(dc3, rows, True) * w_ref[2:3, :]

        @pl.when(j > 0)
        def _():
            for cp in copies(j):
                cp.wait()

        buf[0] = (dt * cc).astype(BF16)
        buf[1] = (d * c3).astype(BF16)
        buf[2] = (dt * cu).astype(BF16)
        for cp in copies(j):
            cp.start()

        @pl.when(j == n_chunks - 1)
        def _():
            for cp in copies(j):
                cp.wait()

    return pl.pallas_call(
        body, name="conv_bwd", grid=(n_chunks,),
        in_specs=[cu_spec, cb_spec, cc_spec, w_spec, chunk, ANY],
        out_specs=[ANY, w_spec],
        out_shape=[jax.ShapeDtypeStruct(dz.shape, dz.dtype), jax.ShapeDtypeStruct((3, CONV_WIDTH), F32)],
        input_output_aliases={5: 0},
        scratch_shapes=[pltpu.VMEM((3, s, CONV_CHUNK), BF16), pltpu.SemaphoreType.DMA((3,))],
        compiler_params=_params(1),
    )(z, z, z, conv_w, d_co, dz)


GATE_CHUNK = 512


def _gate_specs(s, d, tr):
    n_chunks = d // GATE_CHUNK
    za = pl.BlockSpec((tr, GATE_CHUNK), lambda j, i: (i, GL_OFF // GATE_CHUNK + j))
    zc = pl.BlockSpec((tr, GATE_CHUNK), lambda j, i: (i, GL_OFF // GATE_CHUNK + n_chunks + j))
    ba = pl.BlockSpec((1, GATE_CHUNK), lambda j, i: (0, j))
    bc = pl.BlockSpec((1, GATE_CHUNK), lambda j, i: (0, n_chunks + j))
    tile = pl.BlockSpec((tr, GATE_CHUNK), lambda j, i: (i, j))
    return za, zc, ba, bc, tile


def _gate_fwd(z, b_gate, ya, yc):
    s, d = ya.shape
    tr = _row_tile(s, 512)
    za, zc, ba, bc, tile = _gate_specs(s, d, tr)

    def body(za_ref, zc_ref, ba_ref, bc_ref, ya_ref, yc_ref, o_ref):
        ga = jax.nn.sigmoid(za_ref[...] + ba_ref[...])
        gc = jax.nn.sigmoid(zc_ref[...] + bc_ref[...])
        o_ref[...] = (ga * ya_ref[...] + gc * yc_ref[...]).astype(BF16)

    return pl.pallas_call(
        body, name="gate_fwd", grid=(d // GATE_CHUNK, s // tr),
        in_specs=[za, zc, ba, bc, tile, tile],
        out_specs=tile,
        out_shape=jax.ShapeDtypeStruct((s, d), BF16),
        compiler_params=_params(2),
    )(z, z, b_gate, b_gate, ya, yc)


def _gate_bwd(z, b_gate, ya, yc, dmix):
    s, d = ya.shape
    tr = _row_tile(s, 512)
    za, zc, ba, bc, tile = _gate_specs(s, d, tr)
    vec = pl.BlockSpec((1, GATE_CHUNK), lambda j, i: (0, j))
    n_rows = s // tr
    in_width = z.shape[1]

    def body(za_ref, zc_ref, ba_ref, bc_ref, ya_ref, yc_ref, dm_ref, dya_ref, dyc_ref, dz_ref, dba_ref, dbc_ref, buf, sems):
        j, i = pl.program_id(0), pl.program_id(1)

        def copies(j_at, i_at):
            rows = pl.ds(i_at * tr, tr)
            return [pltpu.make_async_copy(buf.at[h], dz_ref.at[rows, pl.ds(GL_OFF + h * d + j_at * GATE_CHUNK, GATE_CHUNK)],
                                          sems.at[h]) for h in range(2)]

        ga = jax.nn.sigmoid(za_ref[...] + ba_ref[...])
        gc = jax.nn.sigmoid(zc_ref[...] + bc_ref[...])
        dm = dm_ref[...]
        dya_ref[...] = (dm * ga).astype(BF16)
        dyc_ref[...] = (dm * gc).astype(BF16)
        dla = dm * ya_ref[...] * ga * (1.0 - ga)
        dlc = dm * yc_ref[...] * gc * (1.0 - gc)

        @pl.when(j * n_rows + i > 0)
        def _():
            for cp in copies(j, i):
                cp.wait()

        buf[0] = dla.astype(BF16)
        buf[1] = dlc.astype(BF16)
        for cp in copies(j, i):
            cp.start()

        @pl.when((j == d // GATE_CHUNK - 1) & (i == n_rows - 1))
        def _():
            for cp in copies(j, i):
                cp.wait()

        pa = jnp.sum(dla, axis=0, keepdims=True)
        pc = jnp.sum(dlc, axis=0, keepdims=True)

        @pl.when(i == 0)
        def _():
            dba_ref[...] = pa
            dbc_ref[...] = pc

        @pl.when(i > 0)
        def _():
            dba_ref[...] += pa
            dbc_ref[...] += pc

    big = jax.ShapeDtypeStruct((s, d), BF16)
    small = jax.ShapeDtypeStruct((1, d), F32)
    return pl.pallas_call(
        body, name="gate_bwd", grid=(d // GATE_CHUNK, n_rows),
        in_specs=[za, zc, ba, bc, tile, tile, tile],
        out_specs=[tile, tile, ANY, vec, vec],
        out_shape=[big, big, jax.ShapeDtypeStruct((s, in_width), BF16), small, small],
        scratch_shapes=[pltpu.VMEM((2, tr, GATE_CHUNK), BF16), pltpu.SemaphoreType.DMA((2,))],
        compiler_params=_params(2),
    )(z, z, b_gate, b_gate, ya, yc, dmix)


def _cross_probs(q_ref, kv_ref, hd):
    cols = slice(hd * HEAD_DIM, (hd + 1) * HEAD_DIM)
    qh = q_ref[:, cols]
    kh = kv_ref[:, cols]
    sc = lax.dot_general(qh, kh, (((1,), (1,)), ((), ())), preferred_element_type=F32) * ATTN_SCALE
    e = jnp.exp(sc - jnp.max(sc, axis=1, keepdims=True))
    return qh, kh, e * (1.0 / jnp.sum(e, axis=1, keepdims=True))


def _cross_fwd(qc, kvc):
    s = qc.shape[0]
    n_mem = kvc.shape[0]
    tq = _row_tile(s, 256)

    def body(q_ref, kv_ref, o_ref):
        for hd in range(MEM_HEADS):
            _, _, p = _cross_probs(q_ref, kv_ref, hd)
            vh = kv_ref[:, MEM_WIDTH + hd * HEAD_DIM:MEM_WIDTH + (hd + 1) * HEAD_DIM]
            o_ref[:, hd * HEAD_DIM:(hd + 1) * HEAD_DIM] = jnp.dot(p.astype(BF16), vh, preferred_element_type=F32).astype(BF16)

    return pl.pallas_call(
        body, name="cross_fwd", grid=(s // tq,),
        in_specs=[pl.BlockSpec((tq, MEM_WIDTH), lambda i: (i, 0)), pl.BlockSpec((n_mem, 2 * MEM_WIDTH), lambda i: (0, 0))],
        out_specs=pl.BlockSpec((tq, MEM_WIDTH), lambda i: (i, 0)),
        out_shape=jax.ShapeDtypeStruct((s, MEM_WIDTH), BF16),
        compiler_params=_params(1),
    )(qc, kvc)


def _cross_bwd(qc, kvc, d_out):
    s = qc.shape[0]
    n_mem = kvc.shape[0]
    tq = _row_tile(s, 256)

    def body(q_ref, kv_ref, do_ref, dq_ref, dkv_ref):
        @pl.when(pl.program_id(0) == 0)
        def _():
            dkv_ref[...] = jnp.zeros_like(dkv_ref)

        for hd in range(MEM_HEADS):
            cols = slice(hd * HEAD_DIM, (hd + 1) * HEAD_DIM)
            vcols = slice(MEM_WIDTH + hd * HEAD_DIM, MEM_WIDTH + (hd + 1) * HEAD_DIM)
            qh, kh, p = _cross_probs(q_ref, kv_ref, hd)
            doh = do_ref[:, cols]
            dp = lax.dot_general(doh, kv_ref[:, vcols], (((1,), (1,)), ((), ())), preferred_element_type=F32)
            ds = (p * (dp - jnp.sum(p * dp, axis=1, keepdims=True)) * ATTN_SCALE).astype(BF16)
            dq_ref[:, cols] = jnp.dot(ds, kh, preferred_element_type=F32).astype(BF16)
            dkv_ref[:, cols] += lax.dot_general(ds, qh, (((0,), (0,)), ((), ())), preferred_element_type=F32)
            dkv_ref[:, vcols] += lax.dot_general(p.astype(BF16), doh, (((0,), (0,)), ((), ())), preferred_element_type=F32)

    qspec = pl.BlockSpec((tq, MEM_WIDTH), lambda i: (i, 0))
    kvspec = pl.BlockSpec((n_mem, 2 * MEM_WIDTH), lambda i: (0, 0))
    return pl.pallas_call(
        body, name="cross_bwd", grid=(s // tq,),
        in_specs=[qspec, kvspec, qspec],
        out_specs=[qspec, kvspec],
        out_shape=[jax.ShapeDtypeStruct((s, MEM_WIDTH), BF16), jax.ShapeDtypeStruct((n_mem, 2 * MEM_WIDTH), F32)],
        compiler_params=_params(1),
    )(qc, kvc, d_out)


def _swiglu_fwd(up, gate):
    sg = jax.nn.sigmoid(gate)
    silu = gate * sg
    return silu * up, up * (sg * (1.0 + gate * (1.0 - sg))), silu


def _swiglu_bwd(d_act, dact_dgate, dact_dup):
    return d_act * dact_dgate.astype(F32), d_act * dact_dup.astype(F32)


GATHER_GROUPS = {"in": ("w_in", "conv_w"), "mid": ("w_attn_out", "w_conv_out", "w_o", "w_cq", "w_ckv", "w_co"),
                 "gate": ("w_gate",), "up": ("w_up",), "down": ("w_down",)}


def _local_step(xs, mems, target, small, fetch, reduce):
    s, d = xs.shape
    w4 = {}
    cos_t, sin_t = _rope_tables(s)

    def near(group, done, then, after):
        waits = [("direct", group)] + ([("pass_near", done), ("pass_far", done)] if done else [])
        starts = [("forward", group), ("pass_near", group)] + [("direct", g) for g in then]
        tok = fetch.step("gather_near_" + group, waits, starts, after)
        if done:
            w4.update(fetch.arrays(done))
        return tok

    def far(group, then, after):
        return fetch.step("gather_far_" + group, [("forward", group)], [("pass_far", group)] + [("direct", g) for g in then], after)

    def last(group, after):
        tok = fetch.step("gather_done_" + group, [("pass_near", group), ("pass_far", group)], [], after)
        w4.update(fetch.arrays(group))
        return tok

    h = _rmsnorm(xs, small["g_mix"], "norm_mix")
    slots_filled = [a for g in ("gate", "up", "down") for a in fetch.arrays(g).values()]
    chip_x, chip_y = reduce.place[0] // 2, reduce.place[0] % 2
    own_block = jnp.stack([2 * chip_x + chip_y]).astype(jnp.int32)
    near_blocks = jnp.stack([2 * (1 - chip_x) + chip_y, 2 * chip_x + (1 - chip_y)]).astype(jnp.int32)
    far_block = jnp.stack([2 * (1 - chip_x) + (1 - chip_y)]).astype(jnp.int32)
    z = _matmul_column_blocks(h, fetch.arrays("in")["w_in"], own_block, None, tm=512, name="in_proj_own")
    tok = near("in", None, ["mid"], [z] + slots_filled)
    tok = fetch.step("gather_near_done_in", [("pass_near", "in")], [], tok)
    z = _matmul_column_blocks(h, fetch.arrays("in")["w_in"], near_blocks, z, tm=1024, name="in_proj_near", after=tok)
    tok = far("in", [], z)
    tok = fetch.step("gather_done_in", [("pass_far", "in")], [], tok)
    w4.update(fetch.arrays("in"))
    z = _matmul_column_blocks(h, w4["w_in"], far_block, z, tm=1024, name="in_proj_far", after=tok)
    conv4 = w4["conv_w"]
    conv_w = conv4[:, :3, :].transpose(1, 0, 2).reshape(3, N_CHIPS * conv4.shape[2])
    c_in = w4["w_in"].shape[2]
    tok = near("mid", None, ["gate"], z)
    q_rot, k_rot, v_b = _rope_fwd(z, cos_t, sin_t)
    attn = _swa_fwd(q_rot, k_rot, v_b, small["sink"])
    co = _conv_fwd(z, conv_w)
    tok = far("mid", ["up"], attn)
    tok = last("mid", tok)
    w_o = w4["w_o"].reshape(-1, w4["w_o"].shape[-1])
    c_d = w4["w_attn_out"].shape[2]
    ya = _matmul(attn, w4["w_attn_out"], mode="nn", tm=2048, tn=c_d, out_dtypes=[F32], name="attn_out_proj",
                 b_blocks=N_CHIPS, after=tok)
    yc = _matmul(co, w4["w_conv_out"], mode="nn", tm=2048, tn=c_d, out_dtypes=[F32], name="conv_out_proj",
                 b_blocks=N_CHIPS)
    mix = _gate_fwd(z, small["b_gate"], ya, yc)
    x1 = _matmul(mix, w_o, mode="nn", tm=1024, tn=1024, out_dtypes=[F32], name="mix_out_proj", extras=[xs],
                 epilogue=_add_residual)
    tok = near("gate", None, ["down"], x1)
    w_cq = w4["w_cq"].reshape(-1, w4["w_cq"].shape[-1])
    w_ckv = w4["w_ckv"].reshape(-1, w4["w_ckv"].shape[-1])
    hc = _rmsnorm(x1, small["g_cross"], "norm_cross")
    memn = _rmsnorm(mems, small["g_mem"], "norm_mem")
    qc = _matmul(hc, w_cq, mode="nn", tm=2048, tn=MEM_WIDTH, out_dtypes=[BF16], name="cross_q_proj", after=tok)
    kvc = _matmul(memn, w_ckv, mode="nn", tm=256, tn=2 * MEM_WIDTH, out_dtypes=[BF16], name="cross_kv_proj")
    oc = _cross_fwd(qc, kvc)
    tok = far("gate", [], oc)
    x2 = _matmul(oc, w4["w_co"], mode="nn", tm=2048, tn=c_d, out_dtypes=[F32], name="cross_out_proj",
                 extras=[x1], epilogue=_add_residual, b_blocks=N_CHIPS, after=tok)
    hf = _rmsnorm(x2, small["g_ffn"], "norm_ffn")
    tok = near("up", "gate", [], hf)
    c_ff = w4["w_gate"].shape[2]
    gate = _matmul(hf, w4["w_gate"], mode="nn", tm=1024, tn=c_ff, out_dtypes=[F32], name="ffn_gate_proj", b_blocks=N_CHIPS,
                   after=tok)
    tok = far("up", [], gate)
    tok = near("down", "up", [], tok)
    act, dact_dgate, dact_dup = _matmul(hf, w4["w_up"], mode="nn", tm=1024, tn=c_ff, out_dtypes=[BF16, BF16, BF16],
                                        name="ffn_up_proj", extras=[gate], epilogue=_swiglu_fwd, b_blocks=N_CHIPS, after=tok)
    tok = far("down", [], act)
    last("down", tok)
    w_down = w4["w_down"].reshape(-1, w4["w_down"].shape[-1])
    x3 = _matmul(act, w_down, mode="nn", tm=512, tn=512, out_dtypes=[F32], name="ffn_down_proj", extras=[x2],
                 epilogue=_add_residual)
    dx3, dx3b, sq, dg_final = _loss_head(x3, small["g_final"], target)

    da, du = _matmul(dx3b, w_down, mode="nt", tm=1024, tn=c_ff, out_dtypes=[BF16, BF16], name="ffn_down_bwd",
                     extras=[dact_dgate, dact_dup], epilogue=_swiglu_bwd)
    core = reduce.core
    ffn_shape = dict(row_sharded=False, tm=1024, tn=c_ff)
    g_down = _matmul(act, dx3b, mode="tn", tm=c_ff, tn=1024, out_dtypes=[BF16], name="ffn_down_wgrad")
    tok = reduce.add("down", {"w_down": g_down}, da)
    t_gate = _wgrad_half(hf, da, core, theirs=True, name="ffn_gate_wgrad_theirs", after=tok, **ffn_shape)
    tok = reduce.step("down", t_gate)
    t_up = _wgrad_half(hf, du, core, theirs=True, name="ffn_up_wgrad_theirs", after=tok, **ffn_shape)
    tok = reduce.send("ffn", {"w_gate": t_gate, "w_up": t_up}, dx3b)
    dhf = _matmul(da, w4["w_gate"], mode="nt", tm=512, tn=1024, out_dtypes=[F32], name="ffn_gate_bwd", b_blocks=N_CHIPS,
                  after=tok)
    got = reduce.received("ffn", dhf)
    p_gate = _wgrad_half(hf, da, core, theirs=False, name="ffn_gate_wgrad_mine", add=got["w_gate"], **ffn_shape)
    p_up = _wgrad_half(hf, du, core, theirs=False, name="ffn_up_wgrad_mine", add=got["w_up"], **ffn_shape)
    tok = reduce.add_parts("ffn", {"w_gate": p_gate, "w_up": p_up})
    dhf = _matmul(du, w4["w_up"], mode="nt", tm=512, tn=1024, out_dtypes=[F32], name="ffn_up_bwd", extras=[dhf],
                  epilogue=_add_residual, b_blocks=N_CHIPS, after=tok)
    tok = reduce.step("down", dhf)
    dx2, dx2b, dg_ffn = _rmsnorm_bwd(dhf, x2, small["g_ffn"], dx3, "norm_ffn_bwd")

    d_oc = _matmul(dx2b, w4["w_co"], mode="nt", tm=1024, tn=MEM_WIDTH, out_dtypes=[BF16], name="cross_out_bwd",
                   b_blocks=N_CHIPS, after=tok)
    g_co = _matmul(oc, dx2b, mode="tn", tm=MEM_WIDTH, tn=c_d, out_dtypes=[BF16], name="cross_out_wgrad", out_blocks=N_CHIPS)
    tok = reduce.step("down", g_co)
    dqc, dkvc = _cross_bwd(qc, kvc, d_oc)
    g_cq = _matmul(hc, dqc, mode="tn", tm=1024, tn=MEM_WIDTH, out_dtypes=[BF16], name="cross_q_wgrad", after=tok)
    dhc = _matmul(dqc, w_cq, mode="nt", tm=1024, tn=1024, out_dtypes=[F32], name="cross_q_bwd")
    g_ckv = _matmul(memn, dkvc, mode="tn", tm=1024, tn=2 * MEM_WIDTH, out_dtypes=[BF16], name="cross_kv_wgrad")
    dmemn = _matmul(dkvc, w_ckv, mode="nt", tm=256, tn=1024, out_dtypes=[F32], name="cross_kv_bwd")
    _, _, dg_mem = _rmsnorm_bwd(dmemn, mems, small["g_mem"], None, "norm_mem_bwd")
    dx1, dx1b, dg_cross = _rmsnorm_bwd(dhc, x1, small["g_cross"], dx2, "norm_cross_bwd")

    dmix = _matmul(dx1b, w_o, mode="nt", tm=1024, tn=1024, out_dtypes=[F32], name="mix_out_bwd")
    g_o = _matmul(mix, dx1b, mode="tn", tm=1024, tn=1024, out_dtypes=[BF16], name="mix_out_wgrad")
    dya, dyc, dz, db_a, db_c = _gate_bwd(z, small["b_gate"], ya, yc, dmix)
    d_attn = _matmul(dya, w4["w_attn_out"], mode="nt", tm=1024, tn=ATTN_WIDTH, out_dtypes=[BF16], name="attn_out_bwd",
                     b_blocks=N_CHIPS)
    g_ao = _matmul(attn, dya, mode="tn", tm=ATTN_WIDTH, tn=c_d, out_dtypes=[BF16], name="attn_out_wgrad", out_blocks=N_CHIPS)
    d_co = _matmul(dyc, w4["w_conv_out"], mode="nt", tm=1024, tn=CONV_WIDTH, out_dtypes=[F32], name="conv_out_bwd",
                   b_blocks=N_CHIPS)
    g_cvo = _matmul(co, dyc, mode="tn", tm=CONV_WIDTH, tn=c_d, out_dtypes=[BF16], name="conv_out_wgrad", out_blocks=N_CHIPS)
    tok = reduce.step("ffn", g_cvo)
    tok = reduce.add("mid", {"w_co": g_co, "w_cq": g_cq, "w_ckv": g_ckv, "w_o": g_o, "w_attn_out": g_ao, "w_conv_out": g_cvo}, tok)
    dz, d_conv_w = _conv_bwd(z, conv_w, d_co, dz)
    dq_rot, dk_rot, dv, dsink = _swa_bwd(q_rot, k_rot, v_b, d_attn, small["sink"])
    tok = reduce.step("mid", dq_rot)
    dz = _rope_bwd(dq_rot, dk_rot, dv, cos_t, sin_t, dz)
    in_shape = dict(row_sharded=False, tm=1024, tn=c_in)
    t_in = _wgrad_half(h, dz, core, theirs=True, name="in_proj_wgrad_theirs", after=tok, **in_shape)
    tok = reduce.send("in", {"w_in": t_in}, dk_rot)
    tok = reduce.step("ffn", tok)
    got = reduce.received("in", tok)
    p_in = _wgrad_half(h, dz, core, theirs=False, name="in_proj_wgrad_mine", add=got["w_in"], **in_shape)
    tok = reduce.add_parts("in", {"w_in": p_in})
    tok = reduce.step("mid", tok)
    dh = _matmul(dz, w4["w_in"], mode="nt", tm=512, tn=512, out_dtypes=[F32], name="in_proj_bwd", b_blocks=N_CHIPS,
                 after=tok)
    grad_x, _, dg_mix = _rmsnorm_bwd(dh, xs, small["g_mix"], dx1, "norm_mix_bwd")

    small_grads = {
        "g_mix": dg_mix, "sink": dsink[:, 0], "b_gate": jnp.concatenate([db_a, db_c], axis=1), "g_cross": dg_cross,
        "g_mem": dg_mem, "g_ffn": dg_ffn, "g_final": dg_final, "conv_w": d_conv_w,
    }
    return sq, grad_x, small_grads


def _pair_sum(g4, ra, core, name):
    nb, rs, cs = g4.shape
    rh = rs // 2
    tr = _row_tile(rh, 256)
    per = rh // tr

    def body(c_ref, g_ref, r_ref, o_ref):
        o_ref[...] = (g_ref[...].astype(F32) + r_ref[...].astype(F32)).astype(BF16)

    plain = pl.BlockSpec((None, tr, cs), lambda j, i, c: (j, i, 0))
    return pl.pallas_call(
        body, name=name,
        grid_spec=pltpu.PrefetchScalarGridSpec(
            num_scalar_prefetch=1, grid=(nb, per),
            in_specs=[pl.BlockSpec((None, tr, cs), lambda j, i, c: (j, c[0] * per + i, 0)), plain],
            out_specs=plain),
        out_shape=jax.ShapeDtypeStruct((nb, rh, cs), BF16),
        compiler_params=_params(2),
    )(core, g4, ra)


def _adamw_update(w, g, m, v):
    nm = ADAM_B1 * m + (1.0 - ADAM_B1) * g
    nv = ADAM_B2 * v + (1.0 - ADAM_B2) * (g * g)
    m_hat = nm / ADAM_C1
    v_hat = nv / ADAM_C2
    return -ADAM_LR * (m_hat / (jnp.sqrt(v_hat) + ADAM_EPS) + ADAM_WD * w), nm, nv


def _adamw_own_half(w, m, v, parts, rc, place, name, after=None):
    rows, cols = w.shape
    rh = rows // 2
    tr = _row_tile(rh, 256)
    per = rh // tr

    def body(p_ref, w_ref, m_ref, v_ref, own_ref, r_ref, *rest):
        gx_ref, g_ref, d_ref, nm_ref, nv_ref = rest[-5:]
        g = own_ref[...].astype(F32)
        for j in range(rc.shape[0]):
            g = g + r_ref[j].astype(F32)
        gx_ref[...] = g
        g_ref[...] = g
        d_ref[...], nm_ref[...], nv_ref[...] = _adamw_update(w_ref[...], g, m_ref[...], v_ref[...])

    mine = pl.BlockSpec((tr, cols), lambda i, p: (p[1] * per + i, 0))
    shape = jax.ShapeDtypeStruct((rows, cols), F32)
    return pl.pallas_call(
        body, name=name,
        grid_spec=pltpu.PrefetchScalarGridSpec(
            num_scalar_prefetch=1, grid=(per,),
            in_specs=[mine, mine, mine, pl.BlockSpec((None, tr, cols), lambda i, p: (p[0], i, 0)),
                      pl.BlockSpec((rc.shape[0], tr, cols), lambda i, p: (0, i, 0))] + ([] if after is None else [ANY]),
            out_specs=[mine] * 5),
        out_shape=[shape] * 5,
        compiler_params=_params(1),
    )(place, w, m, v, parts, rc, *([] if after is None else [after]))


def _adamw_other_half(w, m, v, g_exchanged, g, delta, new_m, new_v, place, name, after=None):
    rows, cols = w.shape
    rh = rows // 2
    tr = _row_tile(rh, 256)
    per = rh // tr

    def body(p_ref, w_ref, m_ref, v_ref, gx_ref, *rest):
        g_ref, d_ref, nm_ref, nv_ref = rest[-4:]
        gv = gx_ref[...]
        g_ref[...] = gv
        d_ref[...], nm_ref[...], nv_ref[...] = _adamw_update(w_ref[...], gv, m_ref[...], v_ref[...])

    other = pl.BlockSpec((tr, cols), lambda i, p: ((1 - p[1]) * per + i, 0))
    shape = jax.ShapeDtypeStruct((rows, cols), F32)
    n_after = 0 if after is None else 1
    return pl.pallas_call(
        body, name=name,
        grid_spec=pltpu.PrefetchScalarGridSpec(
            num_scalar_prefetch=1, grid=(per,),
            in_specs=[other] * 4 + [ANY] * (4 + n_after),
            out_specs=[other] * 4),
        out_shape=[shape] * 4,
        input_output_aliases={5: 0, 6: 1, 7: 2, 8: 3},
        compiler_params=_params(1),
    )(place, w, m, v, g_exchanged, g, delta, new_m, new_v, *([] if after is None else [after]))


def _cast_to_slot(w, place, dtype, name, after=None):
    rows, cols = w.shape
    tr = _row_tile(rows, 1024)

    def body(p_ref, w_ref, *rest):
        o_ref = rest[-1]
        o_ref[...] = w_ref[...].astype(dtype)

    return pl.pallas_call(
        body, name=name,
        grid_spec=pltpu.PrefetchScalarGridSpec(
            num_scalar_prefetch=1, grid=(rows // tr,),
            in_specs=[pl.BlockSpec((tr, cols), lambda i, p: (i, 0))] + ([] if after is None else [ANY]),
            out_specs=pl.BlockSpec((None, tr, cols), lambda i, p: (p[0], i, 0))),
        out_shape=jax.ShapeDtypeStruct((N_CHIPS, rows, cols), dtype),
        compiler_params=_params(1),
    )(place, w, *([] if after is None else [after]))


def _adamw(w, g, m, v, name, after=None):
    rows, cols = w.shape
    tr = _row_tile(rows, 256)

    def body(w_ref, g_ref, m_ref, v_ref, *rest):
        go_ref, d_ref, nm_ref, nv_ref = rest[-4:]
        gv = g_ref[...]
        go_ref[...] = gv
        d_ref[...], nm_ref[...], nv_ref[...] = _adamw_update(w_ref[...], gv, m_ref[...], v_ref[...])

    tile = pl.BlockSpec((tr, cols), lambda i: (i, 0))
    shape = jax.ShapeDtypeStruct((rows, cols), F32)
    return pl.pallas_call(
        body, name=name, grid=(rows // tr,),
        in_specs=[tile] * 4 + ([] if after is None else [ANY]), out_specs=[tile] * 4, out_shape=[shape] * 4,
        compiler_params=_params(1),
    )(w, g, m, v, *([] if after is None else [after]))


def _mesh_pos():
    return lax.axis_index("x"), lax.axis_index("y"), lax.axis_index("c")


def _other_chips(x, y):
    return [(1 - x, y), (x, 1 - y), (1 - x, 1 - y)]


def _half_rows(ref, which):
    rh = ref.shape[-2] // 2
    return ref.at[pl.ds(which * rh, rh), :]


def _remote(src, dst, send_sems, recv_sems, sem, to):
    return pltpu.make_async_remote_copy(src_ref=src, dst_ref=dst, send_sem=send_sems.at[sem], recv_sem=recv_sems.at[sem],
                                        device_id=to, device_id_type=MESH)


HBM = pl.BlockSpec(memory_space=pltpu.HBM)
SEM = pl.BlockSpec(memory_space=pltpu.SEMAPHORE)
DATAFLOW_EFFECT = pltpu.SideEffectType.DATAFLOW_SIDE_EFFECTING


def _in_hbm(arrays):
    return [pltpu.with_memory_space_constraint(a, pltpu.HBM) for a in arrays]


def _hbm_like(arrays):
    return [pltpu.HBM(a.shape, a.dtype) for a in arrays]


GATHER_COPIES_PER_ARRAY = {"direct": 2, "forward": 2, "pass_near": 2, "pass_far": 1}


def _gather_copies(kind, refs, x, y, c):
    me, near_x, near_y, far = 2 * x + y, 2 * (1 - x) + y, 2 * x + (1 - y), 2 * (1 - x) + (1 - y)
    to_x, to_y, sibling = (1 - x, y, c), (x, 1 - y, c), (x, y, 1 - c)
    out = []
    for ref in refs:
        rh = ref.shape[1] // 2
        rq = rh // 2

        def half(chip, ref=ref, rh=rh):
            return ref.at[chip, pl.ds(c * rh, rh), :]

        def quarter(chip, q, ref=ref, rh=rh, rq=rq):
            return ref.at[chip, pl.ds(c * rh + q * rq, rq), :]

        if kind == "direct":
            out += [(half(me), half(me), to_x), (half(me), half(me), to_y)]
        elif kind == "forward":
            out += [(quarter(near_x, 0), quarter(near_x, 0), to_y), (quarter(near_y, 1), quarter(near_y, 1), to_x)]
        elif kind == "pass_near":
            out += [(half(near_x), half(near_x), sibling), (half(near_y), half(near_y), sibling)]
        else:
            assert kind == "pass_far"
            out += [(half(far), half(far), sibling)]
    return out


def _gather_step(name, bufs, waits, starts, after):
    nb, nw, ns = len(bufs), len(waits), len(starts)
    after = [] if after is None else list(after) if isinstance(after, (list, tuple)) else [after]
    n_after = len(after)

    def body(*refs):
        ins = refs[:nb]
        wait_sems = refs[nb:nb + 2 * nw]
        start_sems = refs[nb + 2 * nw + n_after:nb + 2 * nw + n_after + 2 * ns]
        token = refs[-1]
        x, y, c = _mesh_pos()
        for j, (kind, idxs, _, _) in enumerate(waits):
            for i, (s_ref, d_ref, to) in enumerate(_gather_copies(kind, [ins[t] for t in idxs], x, y, c)):
                came = _remote(s_ref, d_ref, wait_sems[2 * j], wait_sems[2 * j + 1], i, to)
                came.wait_recv()
                came.wait_send()
        for j, (kind, idxs) in enumerate(starts):
            for i, (s_ref, d_ref, to) in enumerate(_gather_copies(kind, [ins[t] for t in idxs], x, y, c)):
                _remote(s_ref, d_ref, start_sems[2 * j], start_sems[2 * j + 1], i, to).start()
        token[...] = jnp.zeros_like(token)

    sems = []
    for kind, idxs in starts:
        sems += [pltpu.SemaphoreType.DMA((GATHER_COPIES_PER_ARRAY[kind] * len(idxs),))] * 2
    operands = _in_hbm(bufs) + [sem for w in waits for sem in w[2:]] + after
    outs = pl.pallas_call(
        body, name=name,
        in_specs=[HBM] * nb + [SEM] * (2 * nw) + [ANY] * n_after,
        out_specs=[SEM] * (2 * ns) + [HBM] * nb + [pl.BlockSpec(memory_space=pltpu.VMEM)],
        out_shape=sems + _hbm_like(bufs) + [jax.ShapeDtypeStruct((8, 128), F32)],
        input_output_aliases={i: 2 * ns + i for i in range(nb)},
        compiler_params=pltpu.CompilerParams(has_side_effects=DATAFLOW_EFFECT),
    )(*operands)
    return outs[2 * ns:2 * ns + nb], [(outs[2 * j], outs[2 * j + 1]) for j in range(ns)], outs[-1]


class _Gather:
    def __init__(self, groups):
        self.groups = groups
        self.bufs = {}
        self.in_flight = {}

    def put(self, slotted):
        self.bufs.update(slotted)

    def step(self, name, waits, starts, after=None):
        names = []
        for _, group in list(waits) + list(starts):
            names += [n for n in self.groups[group] if n not in names]
        index = {n: i for i, n in enumerate(names)}

        def members(group):
            return [index[n] for n in self.groups[group]]

        wait_args = [(kind, members(group)) + self.in_flight.pop((kind, group)) for kind, group in waits]
        start_args = [(kind, members(group)) for kind, group in starts]
        bufs, sems, token = _gather_step(name, [self.bufs[n] for n in names], wait_args, start_args, after)
        self.bufs.update(zip(names, bufs))
        for (kind, group), pair in zip(starts, sems):
            self.in_flight[(kind, group)] = pair
        return token

    def arrays(self, group):
        return {n: self.bufs[n] for n in self.groups[group]}


def _sibling_halves_copies(srcs, dsts, x, y, c):
    out = []
    for s_ref, d_ref in zip(srcs, dsts, strict=True):
        rh = s_ref.shape[1] // 2
        out.append((s_ref.at[:, pl.ds((1 - c) * rh, rh), :], d_ref, (x, y, 1 - c)))
    return out


def _to_sibling_copies(srcs, dsts, x, y, c):
    return [(s_ref, d_ref, (x, y, 1 - c)) for s_ref, d_ref in zip(srcs, dsts, strict=True)]


def _chip_copies(srcs, dsts, x, y, c):
    out = []
    for s_ref, d_ref in zip(srcs, dsts, strict=True):
        for k, (px, py) in enumerate(_other_chips(x, y)):
            out.append((s_ref.at[2 * px + py], d_ref.at[k], (px, py, c)))
    return out


def _join_copies(srcs, dsts, x, y, c):
    out = []
    for s_ref in srcs:
        mine = _half_rows(s_ref, c)
        out.append((mine, mine, (x, y, 1 - c)))
    return out


def _exchange_start(copies_fn, n_copies, srcs, fresh, after, name):
    ns, nb = len(srcs), len(srcs) + len(fresh)

    def body(*refs):
        bufs, send, recv, token = refs[:nb], refs[nb + 1], refs[nb + 2], refs[-1]
        x, y, c = _mesh_pos()
        for i, (s_ref, d_ref, to) in enumerate(copies_fn(bufs[:ns], bufs[ns:] if fresh else bufs[:ns], x, y, c)):
            _remote(s_ref, d_ref, send, recv, i, to).start()
        token[...] = jnp.zeros_like(token)

    sems = [pltpu.SemaphoreType.DMA((n_copies,))] * 2
    outs = pl.pallas_call(
        body, name=name,
        in_specs=[HBM] * nb + [ANY], out_specs=[SEM, SEM] + [HBM] * nb + [pl.BlockSpec(memory_space=pltpu.VMEM)],
        out_shape=sems + _hbm_like(list(srcs) + list(fresh)) + [jax.ShapeDtypeStruct((8, 128), F32)],
        input_output_aliases={i: 2 + i for i in range(nb)},
        compiler_params=pltpu.CompilerParams(has_side_effects=DATAFLOW_EFFECT),
    )(*_in_hbm(list(srcs) + list(fresh)), after)
    return outs[0], outs[1], outs[2:2 + ns], outs[2 + ns:2 + nb], outs[-1]


def _exchange_done(copies_fn, srcs, fresh, send, recv, after, name):
    ns, nb = len(srcs), len(srcs) + len(fresh)

    def body(*refs):
        bufs, send_in, recv_in = refs[:nb], refs[nb], refs[nb + 1]
        x, y, c = _mesh_pos()
        for i, (s_ref, d_ref, to) in enumerate(copies_fn(bufs[:ns], bufs[ns:] if fresh else bufs[:ns], x, y, c)):
            came = _remote(s_ref, d_ref, send_in, recv_in, i, to)
            came.wait_send()
            came.wait_recv()

    outs = pl.pallas_call(
        body, name=name,
        in_specs=[HBM] * nb + [SEM, SEM, ANY], out_specs=[HBM] * nb,
        out_shape=_hbm_like(list(srcs) + list(fresh)),
        input_output_aliases={i: i for i in range(nb)},
        compiler_params=pltpu.CompilerParams(has_side_effects=DATAFLOW_EFFECT),
    )(*_in_hbm(list(srcs) + list(fresh)), send, recv, after)
    return outs[:ns], outs[ns:]


class _Reduce:
    def __init__(self, place, core, shards, mom_m, mom_v):
        self.place, self.core = place, core
        self.shards, self.mom_m, self.mom_v = shards, mom_m, mom_v
        self.state = {}
        self.results = {}

    def add(self, group, grads, after):
        names = list(grads)
        g4s = [g.reshape((N_CHIPS, -1, g.shape[-1])) if g.ndim == 2 else g for g in grads.values()]
        fresh = [lax.empty((N_CHIPS, g.shape[1] // 2, g.shape[2]), BF16) for g in g4s]
        send, recv, g4s, fresh, token = _exchange_start(_sibling_halves_copies, len(names), g4s, fresh, after,
                                                        "pair_start_" + group)
        self.state[group] = (0, names, send, recv, g4s, fresh)
        return token

    def send(self, group, theirs, after):
        names, srcs = list(theirs), list(theirs.values())
        fresh = [lax.empty(s.shape, BF16) for s in srcs]
        send, recv, srcs, fresh, token = _exchange_start(_to_sibling_copies, len(names), srcs, fresh, after, "pair_start_" + group)
        self.state[group] = ("sent", names, send, recv, srcs, fresh)
        return token

    def received(self, group, after):
        stage, names, send, recv, srcs, fresh = self.state.pop(group)
        assert stage == "sent"
        _, got = _exchange_done(_to_sibling_copies, srcs, fresh, send, recv, after, "pair_done_" + group)
        return dict(zip(names, got))

    def add_parts(self, group, parts):
        names, srcs = list(parts), list(parts.values())
        fresh = [lax.empty((N_CHIPS - 1,) + p.shape[1:], BF16) for p in srcs]
        send, recv, srcs, fresh, token = _exchange_start(_chip_copies, 3 * len(names), srcs, fresh, self.core, "chips_start_" + group)
        self.state[group] = (1, names, send, recv, srcs, fresh)
        return token

    def step(self, group, after):
        stage, names, send, recv, srcs, fresh = self.state[group]
        if stage == 0:
            g4s, ras = _exchange_done(_sibling_halves_copies, srcs, fresh, send, recv, after, "pair_done_" + group)
            parts = [_pair_sum(g, r, self.core, "pair_sum_" + n) for g, r, n in zip(g4s, ras, names)]
            fresh = [lax.empty((N_CHIPS - 1,) + p.shape[1:], BF16) for p in parts]
            send, recv, parts, fresh, token = _exchange_start(_chip_copies, 3 * len(names), parts, fresh, self.core,
                                                              "chips_start_" + group)
            self.state[group] = (1, names, send, recv, parts, fresh)
            return token
        if stage == 1:
            parts, rcs = _exchange_done(_chip_copies, srcs, fresh, send, recv, after, "chips_done_" + group)
            token = None
            for n, p, r in zip(names, parts, rcs):
                self.results[n] = _adamw_own_half(self.shards[n], self.mom_m[n], self.mom_v[n], p, r, self.place,
                                                  "adamw_own_" + n, after=token)
                token = self.results[n][2]
            wholes = [self.results[n][0] for n in names]
            send, recv, wholes, _, token = _exchange_start(_join_copies, len(names), wholes, [], token, "join_start_" + group)
            self.state[group] = (2, names, send, recv, wholes, [])
            return token
        assert stage == 2
        wholes, _ = _exchange_done(_join_copies, srcs, [], send, recv, after, "join_done_" + group)
        token = None
        for n, exchanged in zip(names, wholes):
            _, g, d, nm, nv = self.results[n]
            self.results[n] = _adamw_other_half(self.shards[n], self.mom_m[n], self.mom_v[n], exchanged, g, d, nm, nv,
                                                self.place, "adamw_other_" + n, after=token)
            token = self.results[n][1]
        del self.state[group]
        return token


N_DEV = 8


def _to_all_copies(srcs, dsts, x, y, c):
    out = []
    for r in range(1, N_DEV):
        fx, fy, fc = (r >> 2) & 1, (r >> 1) & 1, r & 1
        out.append((srcs[0], dsts[0].at[r - 1], (x + fx - 2 * x * fx, y + fy - 2 * y * fy, c + fc - 2 * c * fc)))
    return out


def _all_reduce_small_start(v, after):
    slots = lax.empty((N_DEV - 1,) + v.shape, v.dtype)
    send, recv, (v,), (slots,), token = _exchange_start(_to_all_copies, N_DEV - 1, [v], [slots], after, "small_grads_start")
    return (send, recv, v, slots), token


def _all_reduce_small_done(started, after):
    send, recv, v, slots = started
    (v,), (slots,) = _exchange_done(_to_all_copies, [v], [slots], send, recv, after, "small_grads_done")

    def body(v_ref, slots_ref, o_ref):
        x, y, c = _mesh_pos()
        me = 4 * x + 2 * y + c
        acc = None
        for i in range(N_DEV):
            r = jnp.bitwise_xor(me, i)
            part = jnp.where(r == 0, v_ref[...], slots_ref[jnp.maximum(r - 1, 0)])
            acc = part if acc is None else acc + part
        o_ref[...] = acc

    vm = pl.BlockSpec(memory_space=pltpu.VMEM)
    return pl.pallas_call(body, name="small_grads_sum", in_specs=[vm, vm], out_specs=vm,
                          out_shape=jax.ShapeDtypeStruct(v.shape, v.dtype))(v, slots)


MATRICES = ("w_in", "w_attn_out", "w_conv_out", "w_o", "w_cq", "w_ckv", "w_co", "w_gate", "w_up", "w_down")
VECTORS = ("g_mix", "b_gate", "g_cross", "g_mem", "g_ffn", "g_final", "conv_w", "sink")
WEIGHT_ORDER = ("g_mix", "w_in", "sink", "conv_w", "b_gate", "w_attn_out", "w_conv_out", "w_o", "g_cross", "g_mem", "w_cq",
                "w_ckv", "w_co", "g_ffn", "w_gate", "w_up", "w_down", "g_final")
CONV_PAD_ROWS = 32
SMALL_ROWS = 8


def _pack(pieces):
    flat = jnp.concatenate([p.reshape(-1) for p in pieces])
    lane_group = SMALL_ROWS * 128
    total = -(-flat.shape[0] // lane_group) * lane_group
    flat = jnp.pad(flat, (0, total - flat.shape[0]))
    return flat.reshape(SMALL_ROWS, total // SMALL_ROWS), [p.size for p in pieces]


def _unpack(packed, pieces):
    flat = packed.reshape(-1)
    out, off = [], 0
    for p in pieces:
        out.append(flat[off:off + p.size].reshape(p.shape))
        off += p.size
    return out


def kernel(x, mem, g_mix, w_in, sink, conv_w, b_gate, w_attn_out, w_conv_out, w_o, g_cross, g_mem, w_cq, w_ckv, w_co, g_ffn, w_gate, w_up, w_down, g_final, loss_target, m_g_mix, m_w_in, m_sink, m_conv_w, m_b_gate, m_w_attn_out, m_w_conv_out, m_w_o, m_g_cross, m_g_mem, m_w_cq, m_w_ckv, m_w_co, m_g_ffn, m_w_gate, m_w_up, m_w_down, m_g_final, v_g_mix, v_w_in, v_sink, v_conv_w, v_b_gate, v_w_attn_out, v_w_conv_out, v_w_o, v_g_cross, v_g_mem, v_w_cq, v_w_ckv, v_w_co, v_g_ffn, v_w_gate, v_w_up, v_w_down, v_g_final):
    given = dict(g_mix=g_mix, w_in=w_in, sink=sink, conv_w=conv_w, b_gate=b_gate, w_attn_out=w_attn_out, w_conv_out=w_conv_out,
                 w_o=w_o, g_cross=g_cross, g_mem=g_mem, w_cq=w_cq, w_ckv=w_ckv, w_co=w_co, g_ffn=g_ffn, w_gate=w_gate, w_up=w_up,
                 w_down=w_down, g_final=g_final)
    mom_m = dict(g_mix=m_g_mix, w_in=m_w_in, sink=m_sink, conv_w=m_conv_w, b_gate=m_b_gate, w_attn_out=m_w_attn_out,
                 w_conv_out=m_w_conv_out, w_o=m_w_o, g_cross=m_g_cross, g_mem=m_g_mem, w_cq=m_w_cq, w_ckv=m_w_ckv, w_co=m_w_co,
                 g_ffn=m_g_ffn, w_gate=m_w_gate, w_up=m_w_up, w_down=m_w_down, g_final=m_g_final)
    mom_v = dict(g_mix=v_g_mix, w_in=v_w_in, sink=v_sink, conv_w=v_conv_w, b_gate=v_b_gate, w_attn_out=v_w_attn_out,
                 w_conv_out=v_w_conv_out, w_o=v_w_o, g_cross=v_g_cross, g_mem=v_g_mem, w_cq=v_w_cq, w_ckv=v_w_ckv, w_co=v_w_co,
                 g_ffn=v_g_ffn, w_gate=v_w_gate, w_up=v_w_up, w_down=v_w_down, g_final=v_g_final)
    xs, mems, target = x[0], mem[0], loss_target[0]
    d_model = xs.shape[1]
    chip = 2 * lax.axis_index("x") + lax.axis_index("y")
    core = jnp.reshape(lax.axis_index("c"), (1,)).astype(jnp.int32)
    place = jnp.stack([chip, lax.axis_index("c")]).astype(jnp.int32)

    shards = {n: given[n][0] for n in MATRICES}
    conv_cols = conv_w.shape[2]
    conv_pad = jnp.pad(conv_w[0], ((0, CONV_PAD_ROWS - conv_w.shape[1]), (0, 0)))
    fetch = _Gather(GATHER_GROUPS)
    first = {"w_in": _cast_to_slot(shards["w_in"], place, BF16, "to_slot_w_in"),
             "conv_w": _cast_to_slot(conv_pad, place, F32, "to_slot_conv_w")}
    fetch.put(first)
    tok = fetch.step("gather_start", [], [("direct", "in")])
    fetch.put({n: _cast_to_slot(shards[n], place, BF16, "to_slot_" + n, after=tok) for n in MATRICES if n != "w_in"})
    small = {n: given[n] for n in ("g_mix", "b_gate", "g_cross", "g_mem", "g_ffn")}
    small["g_final"] = g_final[None]
    small["sink"] = sink[0]

    reduce = _Reduce(place, core, shards, {n: mom_m[n][0] for n in MATRICES}, {n: mom_v[n][0] for n in MATRICES})
    sq, grad_x, small_grads = _local_step(xs, mems, target, small, fetch, reduce)

    loss_part = 0.5 * sq[0:1, 0:1] / d_model
    pieces = [small_grads[n] for n in VECTORS] + [loss_part]
    packed, _ = _pack(pieces)
    started, tok = _all_reduce_small_start(packed, core)
    tok = reduce.step("mid", tok)
    summed = _unpack(_all_reduce_small_done(started, tok), pieces)
    loss = summed[-1][0, 0]
    small_sum = dict(zip(VECTORS, summed[:-1]))
    small_sum["conv_w"] = lax.dynamic_slice_in_dim(small_sum["conv_w"], chip * conv_cols, conv_cols, axis=1)

    grad_out, delta, new_m, new_v = {}, {}, {}, {}
    like = [given[n] for n in VECTORS]
    pw, _ = _pack(like)
    pg, _ = _pack([small_sum[n] for n in VECTORS])
    pm, _ = _pack([mom_m[n] for n in VECTORS])
    pv, _ = _pack([mom_v[n] for n in VECTORS])
    tok = reduce.step("in", pg)
    _, pd, pnm, pnv = _adamw(pw, pg, pm, pv, "adamw_small", after=tok)
    for n, g, d, nm, nv in zip(VECTORS, [small_sum[n] for n in VECTORS], _unpack(pd, like), _unpack(pnm, like), _unpack(pnv, like)):
        grad_out[n] = g.reshape(given[n].shape)
        delta[n], new_m[n], new_v[n] = d, nm, nv
    reduce.step("in", pd)
    for n in MATRICES:
        g, d, nm, nv = reduce.results[n]
        grad_out[n], delta[n], new_m[n], new_v[n] = g[None], d[None], nm[None], nv[None]

    return (loss, grad_x[None], *[grad_out[n] for n in WEIGHT_ORDER], *[delta[n] for n in WEIGHT_ORDER],
            *[new_m[n] for n in WEIGHT_ORDER], *[new_v[n] for n in WEIGHT_ORDER])
```

```python
import jax
import jax.numpy as jnp
from jax import lax
from jax.experimental import pallas as pl
from jax.experimental.pallas import tpu as pltpu

F32 = jnp.float32
BF16 = jnp.bfloat16
MESH = pl.DeviceIdType.MESH
ANY = pl.BlockSpec(memory_space=pl.ANY)

VMEM_LIMIT_BYTES = 56 * 1024 * 1024

N_CHIPS = 4
HEAD_DIM = 128
N_Q_HEADS = 8
N_KV_HEADS = 2
Q_GROUP = N_Q_HEADS // N_KV_HEADS
ATTN_WIDTH = N_Q_HEADS * HEAD_DIM
KV_WIDTH = N_KV_HEADS * HEAD_DIM
WINDOW = 128
BLOCK = 128
BAND = 3 * BLOCK
ROPE_THETA = 10000.0
CONV_WIDTH = 1024
MEM_HEADS = 4
MEM_WIDTH = MEM_HEADS * HEAD_DIM
RMS_EPS = 1e-6
NEG_INF = -1e30
ATTN_SCALE = HEAD_DIM ** -0.5

Q_OFF, K_OFF, V_OFF, CU_OFF, CB_OFF, CC_OFF, GL_OFF = 0, 1024, 1280, 1536, 2560, 3584, 4608

ADAM_LR = 0.001
ADAM_B1 = 0.9
ADAM_B2 = 0.999
ADAM_EPS = 1e-08
ADAM_WD = 0.01
ADAM_STEP = 10
ADAM_C1 = 1.0 - ADAM_B1 ** ADAM_STEP
ADAM_C2 = 1.0 - ADAM_B2 ** ADAM_STEP


def _params(n_grid_axes):
    return pltpu.CompilerParams(dimension_semantics=("arbitrary",) * n_grid_axes, vmem_limit_bytes=VMEM_LIMIT_BYTES)


BF16_SUBLANES = 16


def _row_tile(rows, want):
    if rows <= want:
        return rows
    for t in range(want, 0, -BF16_SUBLANES):
        if rows % t == 0:
            return t
    return rows


def _matmul(a, b, *, mode, tm, tn, out_dtypes, name, extras=(), epilogue=None, b_blocks=1, out_blocks=1, after=None):
    if mode == "tn":
        kdim, m = a.shape
    else:
        m, kdim = a.shape
    if b_blocks > 1:
        nb, brows, bcols = b.shape
        assert nb == b_blocks
        if mode == "nn":
            n = bcols * nb
            assert brows == kdim
        else:
            assert mode == "nt" and bcols * nb == kdim
            n = brows
    else:
        n = b.shape[0] if mode == "nt" else b.shape[1]
    tm, tn = min(tm, m), min(tn, n)
    tk = kdim
    assert m % tm == 0 and n % tn == 0, (name, m, n, tm, tn)
    n_extra, n_out = len(extras), len(out_dtypes)
    n_after = 0 if after is None else 1

    if mode == "tn":
        a_spec = pl.BlockSpec((tk, tm), lambda j, i, k: (k, i))
        dims = (((0,), (0,)), ((), ()))
    else:
        a_spec = pl.BlockSpec((tm, tk), lambda j, i, k: (i, k))
        dims = (((1,), (0,)), ((), ())) if mode == "nn" else (((1,), (1,)), ((), ()))

    if b_blocks > 1 and mode == "nn":
        per = b.shape[2] // tn
        assert b.shape[2] % tn == 0
        b_spec = pl.BlockSpec((None, tk, tn), lambda j, i, k: (j // per, k, j % per))
    elif b_blocks > 1:
        b_spec = pl.BlockSpec((b_blocks, tn, b.shape[2]), lambda j, i, k: (0, j, 0))
    elif mode == "nt":
        b_spec = pl.BlockSpec((tn, tk), lambda j, i, k: (j, k))
    else:
        b_spec = pl.BlockSpec((tk, tn), lambda j, i, k: (k, j))

    tile_spec = pl.BlockSpec((tm, tn), lambda j, i, k: (i, j))
    if out_blocks > 1:
        ncols = n // out_blocks
        assert ncols % tn == 0
        oper = ncols // tn
        out_spec = pl.BlockSpec((None, tm, tn), lambda j, i, k: (j // oper, i, j % oper))
        out_shape = [jax.ShapeDtypeStruct((out_blocks, m, ncols), dt) for dt in out_dtypes]
    else:
        out_spec = tile_spec
        out_shape = [jax.ShapeDtypeStruct((m, n), dt) for dt in out_dtypes]

    def body(a_ref, b_ref, *rest):
        extra_refs = rest[:n_extra]
        out_refs = rest[n_extra + n_after:n_extra + n_after + n_out]
        if mode == "nt" and b_blocks > 1:
            cs = b.shape[2]
            acc = None
            for jb in range(b_blocks):
                prod = lax.dot_general(a_ref[:, jb * cs:(jb + 1) * cs].astype(BF16), b_ref[jb].astype(BF16), dims,
                                       preferred_element_type=F32)
                acc = prod if acc is None else acc + prod
        else:
            acc = lax.dot_general(a_ref[...].astype(BF16), b_ref[...].astype(BF16), dims, preferred_element_type=F32)
        tiles = (acc,) if epilogue is None else epilogue(acc, *[r[...] for r in extra_refs])
        for o_ref, t in zip(out_refs, tiles, strict=True):
            o_ref[...] = t.astype(o_ref.dtype)

    outs = pl.pallas_call(
        body,
        name=name,
        grid=(n // tn, m // tm, 1),
        in_specs=[a_spec, b_spec] + [tile_spec] * n_extra + [ANY] * n_after,
        out_specs=[out_spec] * n_out,
        out_shape=out_shape,
        compiler_params=_params(3),
    )(a, b, *extras, *([] if after is None else [after]))
    return outs[0] if n_out == 1 else outs


def _add_residual(acc, res):
    return (acc + res,)


def _matmul_column_blocks(a, b4, blocks, out, *, tm, name, after=None):
    m, kdim = a.shape
    nb, _, cols = b4.shape
    tm = min(tm, m)
    assert m % tm == 0

    def body(j_ref, a_ref, b_ref, *rest):
        rest[-1][...] = jnp.dot(a_ref[...], b_ref[...], preferred_element_type=F32)

    extra = ([] if out is None else [out]) + ([] if after is None else [after])
    n_blocks = blocks.shape[0]
    return pl.pallas_call(
        body, name=name,
        grid_spec=pltpu.PrefetchScalarGridSpec(
            num_scalar_prefetch=1, grid=(n_blocks, m // tm),
            in_specs=[pl.BlockSpec((tm, kdim), lambda j, i, blk: (i, 0)),
                      pl.BlockSpec((None, kdim, cols), lambda j, i, blk: (blk[j], 0, 0))] + [ANY] * len(extra),
            out_specs=pl.BlockSpec((tm, cols), lambda j, i, blk: (i, blk[j]))),
        out_shape=jax.ShapeDtypeStruct((m, nb * cols), F32),
        input_output_aliases={} if out is None else {3: 0},
        compiler_params=_params(2),
    )(blocks, a, b4, *extra)


def _wgrad_half(a, b, core, *, theirs, row_sharded, tm, tn, name, add=None, after=None):
    kdim, m = a.shape
    n = b.shape[1]
    rs, cs = (m // N_CHIPS, n) if row_sharded else (m, n // N_CHIPS)
    rh = rs // 2
    tm, tn = min(tm, rh), min(tn, cs)
    assert rh % tm == 0 and cs % tn == 0, (name, rh, cs, tm, tn)
    mh, per = rh // tm, cs // tn
    has_add = add is not None

    def half(c):
        return 1 - c[0] if theirs else c[0]

    if row_sharded:
        grid = (n // tn, N_CHIPS * mh)
        a_spec = pl.BlockSpec((kdim, tm), lambda j, r, c: (0, ((r // mh) * 2 + half(c)) * mh + r % mh))
        o_spec = pl.BlockSpec((None, tm, tn), lambda j, r, c: (r // mh, r % mh, j))
    else:
        grid = (n // tn, mh)
        a_spec = pl.BlockSpec((kdim, tm), lambda j, r, c: (0, half(c) * mh + r))
        o_spec = pl.BlockSpec((None, tm, tn), lambda j, r, c: (j // per, r, j % per))
    b_spec = pl.BlockSpec((kdim, tn), lambda j, r, c: (0, j))

    def body(c_ref, a_ref, b_ref, *rest):
        o_ref = rest[-1]
        acc = lax.dot_general(a_ref[...].astype(BF16), b_ref[...].astype(BF16), (((0,), (0,)), ((), ())),
                              preferred_element_type=F32)
        if has_add:
            acc = acc + rest[0][...].astype(F32)
        o_ref[...] = acc.astype(BF16)

    operands = [a, b] + ([add] if has_add else []) + ([] if after is None else [after])
    return pl.pallas_call(
        body, name=name,
        grid_spec=pltpu.PrefetchScalarGridSpec(
            num_scalar_prefetch=1, grid=grid,
            in_specs=[a_spec, b_spec] + ([o_spec] if has_add else []) + ([] if after is None else [ANY]),
            out_specs=o_spec),
        out_shape=jax.ShapeDtypeStruct((N_CHIPS, rh, cs), BF16),
        compiler_params=_params(2),
    )(core, *operands)


def _rstd(x):
    return lax.rsqrt(jnp.mean(x * x, axis=-1, keepdims=True) + RMS_EPS)


def _rmsnorm(x, g, name):
    s, d = x.shape
    tr = _row_tile(s, 512)

    def body(x_ref, g_ref, o_ref):
        xv = x_ref[...]
        o_ref[...] = (xv * _rstd(xv) * g_ref[...]).astype(BF16)

    return pl.pallas_call(
        body, name=name, grid=(s // tr,),
        in_specs=[pl.BlockSpec((tr, d), lambda i: (i, 0)), pl.BlockSpec((1, d), lambda i: (0, 0))],
        out_specs=pl.BlockSpec((tr, d), lambda i: (i, 0)),
        out_shape=jax.ShapeDtypeStruct((s, d), BF16),
        compiler_params=_params(1),
    )(x, g)


def _rmsnorm_bwd(dh, x, g, dres, name):
    s, d = x.shape
    tr = _row_tile(s, 512)
    has_res = dres is not None

    def body(*refs):
        if has_res:
            dh_ref, x_ref, g_ref, res_ref, dx_ref, dxb_ref, dg_ref = refs
        else:
            dh_ref, x_ref, g_ref, dx_ref, dxb_ref, dg_ref = refs
        xv = x_ref[...]
        dhv = dh_ref[...].astype(F32)
        r = _rstd(xv)
        xn = xv * r
        dhg = dhv * g_ref[...]
        dx = r * (dhg - xn * jnp.mean(dhg * xn, axis=-1, keepdims=True))
        if has_res:
            dx = dx + res_ref[...]
        dx_ref[...] = dx
        dxb_ref[...] = dx.astype(BF16)
        part = jnp.sum(dhv * xn, axis=0, keepdims=True)

        @pl.when(pl.program_id(0) == 0)
        def _():
            dg_ref[...] = part

        @pl.when(pl.program_id(0) > 0)
        def _():
            dg_ref[...] += part

    row = pl.BlockSpec((tr, d), lambda i: (i, 0))
    vec = pl.BlockSpec((1, d), lambda i: (0, 0))
    return pl.pallas_call(
        body, name=name, grid=(s // tr,),
        in_specs=[row, row, vec] + ([row] if has_res else []),
        out_specs=[row, row, vec],
        out_shape=[jax.ShapeDtypeStruct((s, d), F32), jax.ShapeDtypeStruct((s, d), BF16), jax.ShapeDtypeStruct((1, d), F32)],
        compiler_params=_params(1),
    )(*([dh, x, g] + ([dres] if has_res else [])))


def _loss_head(x3, g, target):
    s, d = x3.shape
    tr = _row_tile(s, 512)

    def body(x_ref, g_ref, t_ref, dx_ref, dxb_ref, sq_ref, dg_ref):
        xv = x_ref[...]
        gv = g_ref[...]
        r = _rstd(xv)
        xn = xv * r
        err = xn * gv - t_ref[...]
        dy = err * (1.0 / d)
        dyg = dy * gv
        dx = r * (dyg - xn * jnp.mean(dyg * xn, axis=-1, keepdims=True))
        dx_ref[...] = dx
        dxb_ref[...] = dx.astype(BF16)
        sq = jnp.sum(jnp.sum(err * err, axis=1, keepdims=True), axis=0, keepdims=True)
        sq = jnp.broadcast_to(sq, (1, 128))
        part = jnp.sum(dy * xn, axis=0, keepdims=True)

        @pl.when(pl.program_id(0) == 0)
        def _():
            sq_ref[...] = sq
            dg_ref[...] = part

        @pl.when(pl.program_id(0) > 0)
        def _():
            sq_ref[...] += sq
            dg_ref[...] += part

    row = pl.BlockSpec((tr, d), lambda i: (i, 0))
    vec = pl.BlockSpec((1, d), lambda i: (0, 0))
    return pl.pallas_call(
        body, name="loss_head", grid=(s // tr,),
        in_specs=[row, vec, row],
        out_specs=[row, row, pl.BlockSpec((1, 128), lambda i: (0, 0)), vec],
        out_shape=[jax.ShapeDtypeStruct((s, d), F32), jax.ShapeDtypeStruct((s, d), BF16),
                   jax.ShapeDtypeStruct((1, 128), F32), jax.ShapeDtypeStruct((1, d), F32)],
        compiler_params=_params(1),
    )(x3, g, target)


def _rope_tables(s):
    inv = 1.0 / (ROPE_THETA ** (jnp.arange(0, HEAD_DIM, 2, dtype=F32) / HEAD_DIM))
    ang = jnp.arange(s, dtype=F32)[:, None] * inv[None, :]
    cos, sin = jnp.cos(ang), jnp.sin(ang)
    return jnp.concatenate([cos, cos], axis=1), jnp.concatenate([-sin, sin], axis=1)


def _swap_halves(t):
    return pltpu.roll(t, HEAD_DIM // 2, 1)


def _rope_fwd(z, cos_t, sin_t):
    s = z.shape[0]
    tr = _row_tile(s, 256)

    def body(zq_ref, zk_ref, zv_ref, c_ref, s_ref, q_ref, k_ref, v_ref):
        c, sn = c_ref[...], s_ref[...]
        for hd in range(N_Q_HEADS):
            cols = slice(hd * HEAD_DIM, (hd + 1) * HEAD_DIM)
            t = zq_ref[:, cols]
            q_ref[:, cols] = (t * c + _swap_halves(t) * sn).astype(BF16)
        for hd in range(N_KV_HEADS):
            cols = slice(hd * HEAD_DIM, (hd + 1) * HEAD_DIM)
            t = zk_ref[:, cols]
            k_ref[:, cols] = (t * c + _swap_halves(t) * sn).astype(BF16)
        v_ref[...] = zv_ref[...].astype(BF16)

    tab = pl.BlockSpec((tr, HEAD_DIM), lambda i: (i, 0))
    return pl.pallas_call(
        body, name="rope_fwd", grid=(s // tr,),
        in_specs=[pl.BlockSpec((tr, ATTN_WIDTH), lambda i: (i, Q_OFF // ATTN_WIDTH)),
                  pl.BlockSpec((tr, KV_WIDTH), lambda i: (i, K_OFF // KV_WIDTH)),
                  pl.BlockSpec((tr, KV_WIDTH), lambda i: (i, V_OFF // KV_WIDTH)), tab, tab],
        out_specs=[pl.BlockSpec((tr, ATTN_WIDTH), lambda i: (i, 0)), pl.BlockSpec((tr, KV_WIDTH), lambda i: (i, 0)),
                   pl.BlockSpec((tr, KV_WIDTH), lambda i: (i, 0))],
        out_shape=[jax.ShapeDtypeStruct((s, ATTN_WIDTH), BF16), jax.ShapeDtypeStruct((s, KV_WIDTH), BF16),
                   jax.ShapeDtypeStruct((s, KV_WIDTH), BF16)],
        compiler_params=_params(1),
    )(z, z, z, cos_t, sin_t)


def _rope_bwd(dq_rot, dk_rot, dv, cos_t, sin_t, dz):
    s = dq_rot.shape[0]
    tr = _row_tile(s, 256)
    qkv_width = V_OFF + KV_WIDTH

    def body(dq_ref, dk_ref, dv_ref, c_ref, s_ref, dz_in_ref, o_ref):
        c, sn = c_ref[...], s_ref[...]
        for hd in range(N_Q_HEADS):
            t = dq_ref[:, hd * HEAD_DIM:(hd + 1) * HEAD_DIM]
            o_ref[:, Q_OFF + hd * HEAD_DIM:Q_OFF + (hd + 1) * HEAD_DIM] = (t * c + _swap_halves(t * sn)).astype(BF16)
        for hd in range(N_KV_HEADS):
            t = dk_ref[:, hd * HEAD_DIM:(hd + 1) * HEAD_DIM]
            o_ref[:, K_OFF + hd * HEAD_DIM:K_OFF + (hd + 1) * HEAD_DIM] = (t * c + _swap_halves(t * sn)).astype(BF16)
        o_ref[:, V_OFF:V_OFF + KV_WIDTH] = dv_ref[...].astype(BF16)

    tab = pl.BlockSpec((tr, HEAD_DIM), lambda i: (i, 0))
    wide = pl.BlockSpec((tr, ATTN_WIDTH), lambda i: (i, 0))
    narrow = pl.BlockSpec((tr, KV_WIDTH), lambda i: (i, 0))
    return pl.pallas_call(
        body, name="rope_bwd", grid=(s // tr,),
        in_specs=[wide, narrow, narrow, tab, tab, ANY],
        out_specs=pl.BlockSpec((tr, qkv_width), lambda i: (i, 0)),
        out_shape=jax.ShapeDtypeStruct(dz.shape, dz.dtype),
        input_output_aliases={5: 0},
        compiler_params=_params(1),
    )(dq_rot, dk_rot, dv, cos_t, sin_t, dz)


def _swa_band(i, s):
    return pl.multiple_of(jnp.clip((i - 1) * BLOCK, 0, s - BAND), BLOCK)


SWA_HEADS_PER_PASS = Q_GROUP


def _swa_probs(q_ref, k_ref, sink_ref, heads, start, valid):
    kv = heads[0] // Q_GROUP
    cols = slice(kv * HEAD_DIM, (kv + 1) * HEAD_DIM)
    kb = k_ref[pl.ds(start, BAND), cols]
    qg = jnp.concatenate([q_ref[:, hd * HEAD_DIM:(hd + 1) * HEAD_DIM] for hd in heads], axis=0)
    sc = lax.dot_general(qg, kb, (((1,), (1,)), ((), ())), preferred_element_type=F32) * ATTN_SCALE
    sc = jnp.where(valid, sc, NEG_INF)
    sk = jnp.concatenate([jnp.full((BLOCK, 1), sink_ref[hd], F32) for hd in heads], axis=0)
    mx = jnp.maximum(jnp.max(sc, axis=1, keepdims=True), sk)
    e = jnp.exp(sc - mx)
    es = jnp.exp(sk - mx)
    inv = 1.0 / (jnp.sum(e, axis=1, keepdims=True) + es)
    return qg, kb, e * inv, es * inv


def _swa_head_passes():
    return [list(range(h0, h0 + SWA_HEADS_PER_PASS)) for h0 in range(0, N_Q_HEADS, SWA_HEADS_PER_PASS)]


def _swa_valid(i, start):
    q_pos = i * BLOCK + lax.broadcasted_iota(jnp.int32, (BLOCK, 1), 0)
    q_pos = jnp.concatenate([q_pos] * SWA_HEADS_PER_PASS, axis=0)
    k_pos = start + lax.broadcasted_iota(jnp.int32, (1, BAND), 1)
    return jnp.abs(k_pos - q_pos) <= WINDOW


def _swa_fwd(q, k, v, sink):
    s = q.shape[0]
    assert s % BLOCK == 0 and s >= BAND

    def body(sink_ref, q_ref, k_ref, v_ref, o_ref):
        i = pl.program_id(0)
        start = _swa_band(i, s)
        valid = _swa_valid(i, start)
        for heads in _swa_head_passes():
            kv = heads[0] // Q_GROUP
            _, _, p, _ = _swa_probs(q_ref, k_ref, sink_ref, heads, start, valid)
            vb = v_ref[pl.ds(start, BAND), kv * HEAD_DIM:(kv + 1) * HEAD_DIM]
            o = jnp.dot(p.astype(BF16), vb, preferred_element_type=F32)
            for g, hd in enumerate(heads):
                o_ref[:, hd * HEAD_DIM:(hd + 1) * HEAD_DIM] = o[g * BLOCK:(g + 1) * BLOCK].astype(BF16)

    whole = pl.BlockSpec((s, KV_WIDTH), lambda i: (0, 0))
    blk = pl.BlockSpec((BLOCK, ATTN_WIDTH), lambda i: (i, 0))
    return pl.pallas_call(
        body, name="swa_fwd", grid=(s // BLOCK,),
        in_specs=[pl.BlockSpec(memory_space=pltpu.SMEM), blk, whole, whole],
        out_specs=blk,
        out_shape=jax.ShapeDtypeStruct((s, ATTN_WIDTH), BF16),
        compiler_params=_params(1),
    )(sink, q, k, v)


def _swa_bwd(q, k, v, d_out, sink):
    s = q.shape[0]

    def body(sink_ref, q_ref, k_ref, v_ref, do_ref, dq_ref, dk_ref, dv_ref, dsink_ref):
        i = pl.program_id(0)

        @pl.when(i == 0)
        def _():
            dk_ref[...] = jnp.zeros_like(dk_ref)
            dv_ref[...] = jnp.zeros_like(dv_ref)
            dsink_ref[...] = jnp.zeros_like(dsink_ref)

        start = _swa_band(i, s)
        valid = _swa_valid(i, start)
        for heads in _swa_head_passes():
            kv = heads[0] // Q_GROUP
            cols = slice(kv * HEAD_DIM, (kv + 1) * HEAD_DIM)
            qg, kb, p, p_sink = _swa_probs(q_ref, k_ref, sink_ref, heads, start, valid)
            vb = v_ref[pl.ds(start, BAND), cols]
            dog = jnp.concatenate([do_ref[:, hd * HEAD_DIM:(hd + 1) * HEAD_DIM] for hd in heads], axis=0)
            dp = lax.dot_general(dog, vb, (((1,), (1,)), ((), ())), preferred_element_type=F32)
            delta = jnp.sum(p * dp, axis=1, keepdims=True)
            ds = (p * (dp - delta) * ATTN_SCALE).astype(BF16)
            dqg = jnp.dot(ds, kb, preferred_element_type=F32)
            dk_ref[pl.ds(start, BAND), cols] += lax.dot_general(ds, qg, (((0,), (0,)), ((), ())), preferred_element_type=F32)
            dv_ref[pl.ds(start, BAND), cols] += lax.dot_general(p.astype(BF16), dog, (((0,), (0,)), ((), ())),
                                                                 preferred_element_type=F32)
            dsk = p_sink * delta
            for g, hd in enumerate(heads):
                dq_ref[:, hd * HEAD_DIM:(hd + 1) * HEAD_DIM] = dqg[g * BLOCK:(g + 1) * BLOCK]
                tot = jnp.sum(dsk[g * BLOCK:(g + 1) * BLOCK], axis=0, keepdims=True)
                dsink_ref[hd:hd + 1, :] -= jnp.broadcast_to(tot, (1, 128))

    whole = pl.BlockSpec((s, KV_WIDTH), lambda i: (0, 0))
    blk = pl.BlockSpec((BLOCK, ATTN_WIDTH), lambda i: (i, 0))
    return pl.pallas_call(
        body, name="swa_bwd", grid=(s // BLOCK,),
        in_specs=[pl.BlockSpec(memory_space=pltpu.SMEM), blk, whole, whole, blk],
        out_specs=[blk, whole, whole, pl.BlockSpec((N_Q_HEADS, 128), lambda i: (0, 0))],
        out_shape=[jax.ShapeDtypeStruct((s, ATTN_WIDTH), F32), jax.ShapeDtypeStruct((s, KV_WIDTH), F32),
                   jax.ShapeDtypeStruct((s, KV_WIDTH), F32), jax.ShapeDtypeStruct((N_Q_HEADS, 128), F32)],
        compiler_params=_params(1),
    )(sink, q, k, v, d_out)


CONV_CHUNK = 256


def _shift_rows(t, rows, down):
    n = t.shape[0]
    rolled = pltpu.roll(t, 1 if down else n - 1, 0)
    edge = 0 if down else n - 1
    return jnp.where(rows == edge, 0.0, rolled)


def _conv_specs(s):
    def z_spec(off):
        return pl.BlockSpec((s, CONV_CHUNK), lambda j, off=off: (0, off // CONV_CHUNK + j))
    chunk = pl.BlockSpec((s, CONV_CHUNK), lambda j: (0, j))
    w_spec = pl.BlockSpec((3, CONV_CHUNK), lambda j: (0, j))
    return z_spec(CU_OFF), z_spec(CB_OFF), z_spec(CC_OFF), chunk, w_spec


def _conv_fwd(z, conv_w):
    s = z.shape[0]
    cu_spec, cb_spec, cc_spec, chunk, w_spec = _conv_specs(s)

    def body(cu_ref, cb_ref, cc_ref, w_ref, o_ref):
        rows = lax.broadcasted_iota(jnp.int32, (s, 1), 0)
        t = cc_ref[...] * cu_ref[...]
        c3 = _shift_rows(t, rows, True) * w_ref[0:1, :] + t * w_ref[1:2, :] + _shift_rows(t, rows, False) * w_ref[2:3, :]
        o_ref[...] = (cb_ref[...] * c3).astype(BF16)

    return pl.pallas_call(
        body, name="conv_fwd", grid=(CONV_WIDTH // CONV_CHUNK,),
        in_specs=[cu_spec, cb_spec, cc_spec, w_spec],
        out_specs=chunk,
        out_shape=jax.ShapeDtypeStruct((s, CONV_WIDTH), BF16),
        compiler_params=_params(1),
    )(z, z, z, conv_w)


def _conv_bwd(z, conv_w, d_co, dz):
    s = z.shape[0]
    cu_spec, cb_spec, cc_spec, chunk, w_spec = _conv_specs(s)
    n_chunks = CONV_WIDTH // CONV_CHUNK
    offsets = (CU_OFF, CB_OFF, CC_OFF)

    def body(cu_ref, cb_ref, cc_ref, w_ref, d_ref, dz_in_ref, dz_ref, dw_ref, buf, sems):
        j = pl.program_id(0)

        def copies(j_at):
            return [pltpu.make_async_copy(buf.at[h], dz_ref.at[:, pl.ds(off + j_at * CONV_CHUNK, CONV_CHUNK)], sems.at[h])
                    for h, off in enumerate(offsets)]

        rows = lax.broadcasted_iota(jnp.int32, (s, 1), 0)
        cu, cc = cu_ref[...], cc_ref[...]
        t = cc * cu
        t_dn, t_up = _shift_rows(t, rows, True), _shift_rows(t, rows, False)
        c3 = t_dn * w_ref[0:1, :] + t * w_ref[1:2, :] + t_up * w_ref[2:3, :]
        d = d_ref[...]
        dc3 = d * cb_ref[...]
        dw_ref[0:1, :] = jnp.sum(dc3 * t_dn, axis=0, keepdims=True)
        dw_ref[1:2, :] = jnp.sum(dc3 * t, axis=0, keepdims=True)
        dw_ref[2:3, :] = jnp.sum(dc3 * t_up, axis=0, keepdims=True)
        dt = _shift_rows(dc3, rows, False) * w_ref[0:1, :] + dc3 * w_ref[1:2, :] + _shift_rows(dc3, rows, True) * w_ref[2:3, :]

        @pl.when(j > 0)
        def _():
            for cp in copies(j):
                cp.wait()

        buf[0] = (dt * cc).astype(BF16)
        buf[1] = (d * c3).astype(BF16)
        buf[2] = (dt * cu).astype(BF16)
        for cp in copies(j):
            cp.start()

        @pl.when(j == n_chunks - 1)
        def _():
            for cp in copies(j):
                cp.wait()

    return pl.pallas_call(
        body, name="conv_bwd", grid=(n_chunks,),
        in_specs=[cu_spec, cb_spec, cc_spec, w_spec, chunk, ANY],
        out_specs=[ANY, w_spec],
        out_shape=[jax.ShapeDtypeStruct(dz.shape, dz.dtype), jax.ShapeDtypeStruct((3, CONV_WIDTH), F32)],
        input_output_aliases={5: 0},
        scratch_shapes=[pltpu.VMEM((3, s, CONV_CHUNK), BF16), pltpu.SemaphoreType.DMA((3,))],
        compiler_params=_params(1),
    )(z, z, z, conv_w, d_co, dz)


GATE_CHUNK = 512


def _gate_specs(s, d, tr):
    n_chunks = d // GATE_CHUNK
    za = pl.BlockSpec((tr, GATE_CHUNK), lambda j, i: (i, GL_OFF // GATE_CHUNK + j))
    zc = pl.BlockSpec((tr, GATE_CHUNK), lambda j, i: (i, GL_OFF // GATE_CHUNK + n_chunks + j))
    ba = pl.BlockSpec((1, GATE_CHUNK), lambda j, i: (0, j))
    bc = pl.BlockSpec((1, GATE_CHUNK), lambda j, i: (0, n_chunks + j))
    tile = pl.BlockSpec((tr, GATE_CHUNK), lambda j, i: (i, j))
    return za, zc, ba, bc, tile


def _gate_fwd(z, b_gate, ya, yc):
    s, d = ya.shape
    tr = _row_tile(s, 512)
    za, zc, ba, bc, tile = _gate_specs(s, d, tr)

    def body(za_ref, zc_ref, ba_ref, bc_ref, ya_ref, yc_ref, o_ref):
        ga = jax.nn.sigmoid(za_ref[...] + ba_ref[...])
        gc = jax.nn.sigmoid(zc_ref[...] + bc_ref[...])
        o_ref[...] = (ga * ya_ref[...] + gc * yc_ref[...]).astype(BF16)

    return pl.pallas_call(
        body, name="gate_fwd", grid=(d // GATE_CHUNK, s // tr),
        in_specs=[za, zc, ba, bc, tile, tile],
        out_specs=tile,
        out_shape=jax.ShapeDtypeStruct((s, d), BF16),
        compiler_params=_params(2),
    )(z, z, b_gate, b_gate, ya, yc)


def _gate_bwd(z, b_gate, ya, yc, dmix):
    s, d = ya.shape
    tr = _row_tile(s, 512)
    za, zc, ba, bc, tile = _gate_specs(s, d, tr)
    vec = pl.BlockSpec((1, GATE_CHUNK), lambda j, i: (0, j))
    n_rows = s // tr
    in_width = z.shape[1]

    def body(za_ref, zc_ref, ba_ref, bc_ref, ya_ref, yc_ref, dm_ref, dya_ref, dyc_ref, dz_ref, dba_ref, dbc_ref, buf, sems):
        j, i = pl.program_id(0), pl.program_id(1)

        def copies(j_at, i_at):
            rows = pl.ds(i_at * tr, tr)
            return [pltpu.make_async_copy(buf.at[h], dz_ref.at[rows, pl.ds(GL_OFF + h * d + j_at * GATE_CHUNK, GATE_CHUNK)],
                                          sems.at[h]) for h in range(2)]

        ga = jax.nn.sigmoid(za_ref[...] + ba_ref[...])
        gc = jax.nn.sigmoid(zc_ref[...] + bc_ref[...])
        dm = dm_ref[...]
        dya_ref[...] = (dm * ga).astype(BF16)
        dyc_ref[...] = (dm * gc).astype(BF16)
        dla = dm * ya_ref[...] * ga * (1.0 - ga)
        dlc = dm * yc_ref[...] * gc * (1.0 - gc)

        @pl.when(j * n_rows + i > 0)
        def _():
            for cp in copies(j, i):
                cp.wait()

        buf[0] = dla.astype(BF16)
        buf[1] = dlc.astype(BF16)
        for cp in copies(j, i):
            cp.start()

        @pl.when((j == d // GATE_CHUNK - 1) & (i == n_rows - 1))
        def _():
            for cp in copies(j, i):
                cp.wait()

        pa = jnp.sum(dla, axis=0, keepdims=True)
        pc = jnp.sum(dlc, axis=0, keepdims=True)

        @pl.when(i == 0)
        def _():
            dba_ref[...] = pa
            dbc_ref[...] = pc

        @pl.when(i > 0)
        def _():
            dba_ref[...] += pa
            dbc_ref[...] += pc

    big = jax.ShapeDtypeStruct((s, d), BF16)
    small = jax.ShapeDtypeStruct((1, d), F32)
    return pl.pallas_call(
        body, name="gate_bwd", grid=(d // GATE_CHUNK, n_rows),
        in_specs=[za, zc, ba, bc, tile, tile, tile],
        out_specs=[tile, tile, ANY, vec, vec],
        out_shape=[big, big, jax.ShapeDtypeStruct((s, in_width), BF16), small, small],
        scratch_shapes=[pltpu.VMEM((2, tr, GATE_CHUNK), BF16), pltpu.SemaphoreType.DMA((2,))],
        compiler_params=_params(2),
    )(z, z, b_gate, b_gate, ya, yc, dmix)


def _cross_probs(q_ref, kv_ref, hd):
    cols = slice(hd * HEAD_DIM, (hd + 1) * HEAD_DIM)
    qh = q_ref[:, cols]
    kh = kv_ref[:, cols]
    sc = lax.dot_general(qh, kh, (((1,), (1,)), ((), ())), preferred_element_type=F32) * ATTN_SCALE
    e = jnp.exp(sc - jnp.max(sc, axis=1, keepdims=True))
    return qh, kh, e * (1.0 / jnp.sum(e, axis=1, keepdims=True))


def _cross_fwd(qc, kvc):
    s = qc.shape[0]
    n_mem = kvc.shape[0]
    tq = _row_tile(s, 256)

    def body(q_ref, kv_ref, o_ref):
        for hd in range(MEM_HEADS):
            _, _, p = _cross_probs(q_ref, kv_ref, hd)
            vh = kv_ref[:, MEM_WIDTH + hd * HEAD_DIM:MEM_WIDTH + (hd + 1) * HEAD_DIM]
            o_ref[:, hd * HEAD_DIM:(hd + 1) * HEAD_DIM] = jnp.dot(p.astype(BF16), vh, preferred_element_type=F32).astype(BF16)

    return pl.pallas_call(
        body, name="cross_fwd", grid=(s // tq,),
        in_specs=[pl.BlockSpec((tq, MEM_WIDTH), lambda i: (i, 0)), pl.BlockSpec((n_mem, 2 * MEM_WIDTH), lambda i: (0, 0))],
        out_specs=pl.BlockSpec((tq, MEM_WIDTH), lambda i: (i, 0)),
        out_shape=jax.ShapeDtypeStruct((s, MEM_WIDTH), BF16),
        compiler_params=_params(1),
    )(qc, kvc)


def _cross_bwd(qc, kvc, d_out):
    s = qc.shape[0]
    n_mem = kvc.shape[0]
    tq = _row_tile(s, 256)

    def body(q_ref, kv_ref, do_ref, dq_ref, dkv_ref):
        @pl.when(pl.program_id(0) == 0)
        def _():
            dkv_ref[...] = jnp.zeros_like(dkv_ref)

        for hd in range(MEM_HEADS):
            cols = slice(hd * HEAD_DIM, (hd + 1) * HEAD_DIM)
            vcols = slice(MEM_WIDTH + hd * HEAD_DIM, MEM_WIDTH + (hd + 1) * HEAD_DIM)
            qh, kh, p = _cross_probs(q_ref, kv_ref, hd)
            doh = do_ref[:, cols]
            dp = lax.dot_general(doh, kv_ref[:, vcols], (((1,), (1,)), ((), ())), preferred_element_type=F32)
            ds = (p * (dp - jnp.sum(p * dp, axis=1, keepdims=True)) * ATTN_SCALE).astype(BF16)
            dq_ref[:, cols] = jnp.dot(ds, kh, preferred_element_type=F32).astype(BF16)
            dkv_ref[:, cols] += lax.dot_general(ds, qh, (((0,), (0,)), ((), ())), preferred_element_type=F32)
            dkv_ref[:, vcols] += lax.dot_general(p.astype(BF16), doh, (((0,), (0,)), ((), ())), preferred_element_type=F32)

    qspec = pl.BlockSpec((tq, MEM_WIDTH), lambda i: (i, 0))
    kvspec = pl.BlockSpec((n_mem, 2 * MEM_WIDTH), lambda i: (0, 0))
    return pl.pallas_call(
        body, name="cross_bwd", grid=(s // tq,),
        in_specs=[qspec, kvspec, qspec],
        out_specs=[qspec, kvspec],
        out_shape=[jax.ShapeDtypeStruct((s, MEM_WIDTH), BF16), jax.ShapeDtypeStruct((n_mem, 2 * MEM_WIDTH), F32)],
        compiler_params=_params(1),
    )(qc, kvc, d_out)


def _swiglu_fwd(up, gate):
    sg = jax.nn.sigmoid(gate)
    silu = gate * sg
    return silu * up, up * (sg * (1.0 + gate * (1.0 - sg))), silu


def _swiglu_bwd(d_act, dact_dgate, dact_dup):
    return d_act * dact_dgate.astype(F32), d_act * dact_dup.astype(F32)


GATHER_GROUPS = {"in": ("w_in", "conv_w"), "mid": ("w_attn_out", "w_conv_out", "w_o", "w_cq", "w_ckv", "w_co"),
                 "gate": ("w_gate",), "up": ("w_up",), "down": ("w_down",)}


def _local_step(xs, mems, target, small, fetch, reduce):
    s, d = xs.shape
    w4 = {}
    cos_t, sin_t = _rope_tables(s)

    def near(group, done, then, after):
        waits = [("direct", group)] + ([("pass_near", done), ("pass_far", done)] if done else [])
        starts = [("forward", group), ("pass_near", group)] + [("direct", g) for g in then]
        tok = fetch.step("gather_near_" + group, waits, starts, after)
        if done:
            w4.update(fetch.arrays(done))
        return tok

    def far(group, then, after):
        return fetch.step("gather_far_" + group, [("forward", group)], [("pass_far", group)] + [("direct", g) for g in then], after)

    def last(group, after):
        tok = fetch.step("gather_done_" + group, [("pass_near", group), ("pass_far", group)], [], after)
        w4.update(fetch.arrays(group))
        return tok

    h = _rmsnorm(xs, small["g_mix"], "norm_mix")
    slots_filled = [a for g in ("gate", "up", "down") for a in fetch.arrays(g).values()]
    chip_x, chip_y = reduce.place[0] // 2, reduce.place[0] % 2
    own_block = jnp.stack([2 * chip_x + chip_y]).astype(jnp.int32)
    near_blocks = jnp.stack([2 * (1 - chip_x) + chip_y, 2 * chip_x + (1 - chip_y)]).astype(jnp.int32)
    far_block = jnp.stack([2 * (1 - chip_x) + (1 - chip_y)]).astype(jnp.int32)
    z = _matmul_column_blocks(h, fetch.arrays("in")["w_in"], own_block, None, tm=512, name="in_proj_own")
    tok = near("in", None, ["mid"], [z] + slots_filled)
    tok = fetch.step("gather_near_done_in", [("pass_near", "in")], [], tok)
    z = _matmul_column_blocks(h, fetch.arrays("in")["w_in"], near_blocks, z, tm=1024, name="in_proj_near", after=tok)
    tok = far("in", [], z)
    tok = fetch.step("gather_done_in", [("pass_far", "in")], [], tok)
    w4.update(fetch.arrays("in"))
    z = _matmul_column_blocks(h, w4["w_in"], far_block, z, tm=1024, name="in_proj_far", after=tok)
    conv4 = w4["conv_w"]
    conv_w = conv4[:, :3, :].transpose(1, 0, 2).reshape(3, N_CHIPS * conv4.shape[2])
    c_in = w4["w_in"].shape[2]
    tok = near("mid", None, ["gate"], z)
    q_rot, k_rot, v_b = _rope_fwd(z, cos_t, sin_t)
    attn = _swa_fwd(q_rot, k_rot, v_b, small["sink"])
    co = _conv_fwd(z, conv_w)
    tok = far("mid", ["up"], attn)
    tok = last("mid", tok)
    w_o = w4["w_o"].reshape(-1, w4["w_o"].shape[-1])
    c_d = w4["w_attn_out"].shape[2]
    ya = _matmul(attn, w4["w_attn_out"], mode="nn", tm=2048, tn=c_d, out_dtypes=[F32], name="attn_out_proj",
                 b_blocks=N_CHIPS, after=tok)
    yc = _matmul(co, w4["w_conv_out"], mode="nn", tm=2048, tn=c_d, out_dtypes=[F32], name="conv_out_proj",
                 b_blocks=N_CHIPS)
    mix = _gate_fwd(z, small["b_gate"], ya, yc)
    x1 = _matmul(mix, w_o, mode="nn", tm=1024, tn=1024, out_dtypes=[F32], name="mix_out_proj", extras=[xs],
                 epilogue=_add_residual)
    tok = near("gate", None, ["down"], x1)
    w_cq = w4["w_cq"].reshape(-1, w4["w_cq"].shape[-1])
    w_ckv = w4["w_ckv"].reshape(-1, w4["w_ckv"].shape[-1])
    hc = _rmsnorm(x1, small["g_cross"], "norm_cross")
    memn = _rmsnorm(mems, small["g_mem"], "norm_mem")
    qc = _matmul(hc, w_cq, mode="nn", tm=2048, tn=MEM_WIDTH, out_dtypes=[BF16], name="cross_q_proj", after=tok)
    kvc = _matmul(memn, w_ckv, mode="nn", tm=256, tn=2 * MEM_WIDTH, out_dtypes=[BF16], name="cross_kv_proj")
    oc = _cross_fwd(qc, kvc)
    tok = far("gate", [], oc)
    x2 = _matmul(oc, w4["w_co"], mode="nn", tm=2048, tn=c_d, out_dtypes=[F32], name="cross_out_proj",
                 extras=[x1], epilogue=_add_residual, b_blocks=N_CHIPS, after=tok)
    hf = _rmsnorm(x2, small["g_ffn"], "norm_ffn")
    tok = near("up", "gate", [], hf)
    c_ff = w4["w_gate"].shape[2]
    gate = _matmul(hf, w4["w_gate"], mode="nn", tm=1024, tn=c_ff, out_dtypes=[F32], name="ffn_gate_proj", b_blocks=N_CHIPS,
                   after=tok)
    tok = far("up", [], gate)
    tok = near("down", "up", [], tok)
    act, dact_dgate, dact_dup = _matmul(hf, w4["w_up"], mode="nn", tm=1024, tn=c_ff, out_dtypes=[BF16, BF16, BF16],
                                        name="ffn_up_proj", extras=[gate], epilogue=_swiglu_fwd, b_blocks=N_CHIPS, after=tok)
    tok = far("down", [], act)
    last("down", tok)
    w_down = w4["w_down"].reshape(-1, w4["w_down"].shape[-1])
    x3 = _matmul(act, w_down, mode="nn", tm=512, tn=512, out_dtypes=[F32], name="ffn_down_proj", extras=[x2],
                 epilogue=_add_residual)
    dx3, dx3b, sq, dg_final = _loss_head(x3, small["g_final"], target)

    da, du = _matmul(dx3b, w_down, mode="nt", tm=1024, tn=c_ff, out_dtypes=[BF16, BF16], name="ffn_down_bwd",
                     extras=[dact_dgate, dact_dup], epilogue=_swiglu_bwd)
    core = reduce.core
    ffn_shape = dict(row_sharded=False, tm=1024, tn=c_ff)
    g_down = _matmul(act, dx3b, mode="tn", tm=c_ff, tn=1024, out_dtypes=[BF16], name="ffn_down_wgrad")
    tok = reduce.add("down", {"w_down": g_down}, da)
    t_gate = _wgrad_half(hf, da, core, theirs=True, name="ffn_gate_wgrad_theirs", after=tok, **ffn_shape)
    tok = reduce.step("down", t_gate)
    t_up = _wgrad_half(hf, du, core, theirs=True, name="ffn_up_wgrad_theirs", after=tok, **ffn_shape)
    tok = reduce.send("ffn", {"w_gate": t_gate, "w_up": t_up}, dx3b)
    dhf = _matmul(da, w4["w_gate"], mode="nt", tm=512, tn=1024, out_dtypes=[F32], name="ffn_gate_bwd", b_blocks=N_CHIPS,
                  after=tok)
    got = reduce.received("ffn", dhf)
    p_gate = _wgrad_half(hf, da, core, theirs=False, name="ffn_gate_wgrad_mine", add=got["w_gate"], **ffn_shape)
    p_up = _wgrad_half(hf, du, core, theirs=False, name="ffn_up_wgrad_mine", add=got["w_up"], **ffn_shape)
    tok = reduce.add_parts("ffn", {"w_gate": p_gate, "w_up": p_up})
    dhf = _matmul(du, w4["w_up"], mode="nt", tm=512, tn=1024, out_dtypes=[F32], name="ffn_up_bwd", extras=[dhf],
                  epilogue=_add_residual, b_blocks=N_CHIPS, after=tok)
    tok = reduce.step("down", dhf)
    dx2, dx2b, dg_ffn = _rmsnorm_bwd(dhf, x2, small["g_ffn"], dx3, "norm_ffn_bwd")

    d_oc = _matmul(dx2b, w4["w_co"], mode="nt", tm=1024, tn=MEM_WIDTH, out_dtypes=[BF16], name="cross_out_bwd",
                   b_blocks=N_CHIPS, after=tok)
    g_co = _matmul(oc, dx2b, mode="tn", tm=MEM_WIDTH, tn=c_d, out_dtypes=[BF16], name="cross_out_wgrad", out_blocks=N_CHIPS)
    tok = reduce.step("down", g_co)
    dqc, dkvc = _cross_bwd(qc, kvc, d_oc)
    g_cq = _matmul(hc, dqc, mode="tn", tm=1024, tn=MEM_WIDTH, out_dtypes=[BF16], name="cross_q_wgrad", after=tok)
    dhc = _matmul(dqc, w_cq, mode="nt", tm=1024, tn=1024, out_dtypes=[F32], name="cross_q_bwd")
    g_ckv = _matmul(memn, dkvc, mode="tn", tm=1024, tn=2 * MEM_WIDTH, out_dtypes=[BF16], name="cross_kv_wgrad")
    dmemn = _matmul(dkvc, w_ckv, mode="nt", tm=256, tn=1024, out_dtypes=[F32], name="cross_kv_bwd")
    _, _, dg_mem = _rmsnorm_bwd(dmemn, mems, small["g_mem"], None, "norm_mem_bwd")
    dx1, dx1b, dg_cross = _rmsnorm_bwd(dhc, x1, small["g_cross"], dx2, "norm_cross_bwd")

    dmix = _matmul(dx1b, w_o, mode="nt", tm=1024, tn=1024, out_dtypes=[F32], name="mix_out_bwd")
    g_o = _matmul(mix, dx1b, mode="tn", tm=1024, tn=1024, out_dtypes=[BF16], name="mix_out_wgrad")
    dya, dyc, dz, db_a, db_c = _gate_bwd(z, small["b_gate"], ya, yc, dmix)
    d_attn = _matmul(dya, w4["w_attn_out"], mode="nt", tm=1024, tn=ATTN_WIDTH, out_dtypes=[BF16], name="attn_out_bwd",
                     b_blocks=N_CHIPS)
    g_ao = _matmul(attn, dya, mode="tn", tm=ATTN_WIDTH, tn=c_d, out_dtypes=[BF16], name="attn_out_wgrad", out_blocks=N_CHIPS)
    d_co = _matmul(dyc, w4["w_conv_out"], mode="nt", tm=1024, tn=CONV_WIDTH, out_dtypes=[F32], name="conv_out_bwd",
                   b_blocks=N_CHIPS)
    g_cvo = _matmul(co, dyc, mode="tn", tm=CONV_WIDTH, tn=c_d, out_dtypes=[BF16], name="conv_out_wgrad", out_blocks=N_CHIPS)
    tok = reduce.step("ffn", g_cvo)
    tok = reduce.add("mid", {"w_co": g_co, "w_cq": g_cq, "w_ckv": g_ckv, "w_o": g_o, "w_attn_out": g_ao, "w_conv_out": g_cvo}, tok)
    dz, d_conv_w = _conv_bwd(z, conv_w, d_co, dz)
    dq_rot, dk_rot, dv, dsink = _swa_bwd(q_rot, k_rot, v_b, d_attn, small["sink"])
    tok = reduce.step("mid", dq_rot)
    dz = _rope_bwd(dq_rot, dk_rot, dv, cos_t, sin_t, dz)
    in_shape = dict(row_sharded=False, tm=1024, tn=c_in)
    t_in = _wgrad_half(h, dz, core, theirs=True, name="in_proj_wgrad_theirs", after=tok, **in_shape)
    tok = reduce.send("in", {"w_in": t_in}, dk_rot)
    tok = reduce.step("ffn", tok)
    got = reduce.received("in", tok)
    p_in = _wgrad_half(h, dz, core, theirs=False, name="in_proj_wgrad_mine", add=got["w_in"], **in_shape)
    tok = reduce.add_parts("in", {"w_in": p_in})
    tok = reduce.step("mid", tok)
    dh = _matmul(dz, w4["w_in"], mode="nt", tm=512, tn=512, out_dtypes=[F32], name="in_proj_bwd", b_blocks=N_CHIPS,
                 after=tok)
    grad_x, _, dg_mix = _rmsnorm_bwd(dh, xs, small["g_mix"], dx1, "norm_mix_bwd")

    small_grads = {
        "g_mix": dg_mix, "sink": dsink[:, 0], "b_gate": jnp.concatenate([db_a, db_c], axis=1), "g_cross": dg_cross,
        "g_mem": dg_mem, "g_ffn": dg_ffn, "g_final": dg_final, "conv_w": d_conv_w,
    }
    return sq, grad_x, small_grads


def _pair_sum(g4, ra, core, name):
    nb, rs, cs = g4.shape
    rh = rs // 2
    tr = _row_tile(rh, 256)
    per = rh // tr

    def body(c_ref, g_ref, r_ref, o_ref):
        o_ref[...] = (g_ref[...].astype(F32) + r_ref[...].astype(F32)).astype(BF16)

    plain = pl.BlockSpec((None, tr, cs), lambda j, i, c: (j, i, 0))
    return pl.pallas_call(
        body, name=name,
        grid_spec=pltpu.PrefetchScalarGridSpec(
            num_scalar_prefetch=1, grid=(nb, per),
            in_specs=[pl.BlockSpec((None, tr, cs), lambda j, i, c: (j, c[0] * per + i, 0)), plain],
            out_specs=plain),
        out_shape=jax.ShapeDtypeStruct((nb, rh, cs), BF16),
        compiler_params=_params(2),
    )(core, g4, ra)


def _adamw_update(w, g, m, v):
    nm = ADAM_B1 * m + (1.0 - ADAM_B1) * g
    nv = ADAM_B2 * v + (1.0 - ADAM_B2) * (g * g)
    m_hat = nm / ADAM_C1
    v_hat = nv / ADAM_C2
    return -ADAM_LR * (m_hat / (jnp.sqrt(v_hat) + ADAM_EPS) + ADAM_WD * w), nm, nv


def _adamw_own_half(w, m, v, parts, rc, place, name, after=None):
    rows, cols = w.shape
    rh = rows // 2
    tr = _row_tile(rh, 256)
    per = rh // tr

    def body(p_ref, w_ref, m_ref, v_ref, own_ref, r_ref, *rest):
        gx_ref, g_ref, d_ref, nm_ref, nv_ref = rest[-5:]
        g = own_ref[...].astype(F32)
        for j in range(rc.shape[0]):
            g = g + r_ref[j].astype(F32)
        gx_ref[...] = g
        g_ref[...] = g
        d_ref[...], nm_ref[...], nv_ref[...] = _adamw_update(w_ref[...], g, m_ref[...], v_ref[...])

    mine = pl.BlockSpec((tr, cols), lambda i, p: (p[1] * per + i, 0))
    shape = jax.ShapeDtypeStruct((rows, cols), F32)
    return pl.pallas_call(
        body, name=name,
        grid_spec=pltpu.PrefetchScalarGridSpec(
            num_scalar_prefetch=1, grid=(per,),
            in_specs=[mine, mine, mine, pl.BlockSpec((None, tr, cols), lambda i, p: (p[0], i, 0)),
                      pl.BlockSpec((rc.shape[0], tr, cols), lambda i, p: (0, i, 0))] + ([] if after is None else [ANY]),
            out_specs=[mine] * 5),
        out_shape=[shape] * 5,
        compiler_params=_params(1),
    )(place, w, m, v, parts, rc, *([] if after is None else [after]))


def _adamw_other_half(w, m, v, g_exchanged, g, delta, new_m, new_v, place, name, after=None):
    rows, cols = w.shape
    rh = rows // 2
    tr = _row_tile(rh, 256)
    per = rh // tr

    def body(p_ref, w_ref, m_ref, v_ref, gx_ref, *rest):
        g_ref, d_ref, nm_ref, nv_ref = rest[-4:]
        gv = gx_ref[...]
        g_ref[...] = gv
        d_ref[...], nm_ref[...], nv_ref[...] = _adamw_update(w_ref[...], gv, m_ref[...], v_ref[...])

    other = pl.BlockSpec((tr, cols), lambda i, p: ((1 - p[1]) * per + i, 0))
    shape = jax.ShapeDtypeStruct((rows, cols), F32)
    n_after = 0 if after is None else 1
    return pl.pallas_call(
        body, name=name,
        grid_spec=pltpu.PrefetchScalarGridSpec(
            num_scalar_prefetch=1, grid=(per,),
            in_specs=[other] * 4 + [ANY] * (4 + n_after),
            out_specs=[other] * 4),
        out_shape=[shape] * 4,
        input_output_aliases={5: 0, 6: 1, 7: 2, 8: 3},
        compiler_params=_params(1),
    )(place, w, m, v, g_exchanged, g, delta, new_m, new_v, *([] if after is None else [after]))


def _cast_to_slot(w, place, dtype, name, after=None):
    rows, cols = w.shape
    tr = _row_tile(rows, 1024)

    def body(p_ref, w_ref, *rest):
        o_ref = rest[-1]
        o_ref[...] = w_ref[...].astype(dtype)

    return pl.pallas_call(
        body, name=name,
        grid_spec=pltpu.PrefetchScalarGridSpec(
            num_scalar_prefetch=1, grid=(rows // tr,),
            in_specs=[pl.BlockSpec((tr, cols), lambda i, p: (i, 0))] + ([] if after is None else [ANY]),
            out_specs=pl.BlockSpec((None, tr, cols), lambda i, p: (p[0], i, 0))),
        out_shape=jax.ShapeDtypeStruct((N_CHIPS, rows, cols), dtype),
        compiler_params=_params(1),
    )(place, w, *([] if after is None else [after]))


def _adamw(w, g, m, v, name, after=None):
    rows, cols = w.shape
    tr = _row_tile(rows, 256)

    def body(w_ref, g_ref, m_ref, v_ref, *rest):
        go_ref, d_ref, nm_ref, nv_ref = rest[-4:]
        gv = g_ref[...]
        go_ref[...] = gv
        d_ref[...], nm_ref[...], nv_ref[...] = _adamw_update(w_ref[...], gv, m_ref[...], v_ref[...])

    tile = pl.BlockSpec((tr, cols), lambda i: (i, 0))
    shape = jax.ShapeDtypeStruct((rows, cols), F32)
    return pl.pallas_call(
        body, name=name, grid=(rows // tr,),
        in_specs=[tile] * 4 + ([] if after is None else [ANY]), out_specs=[tile] * 4, out_shape=[shape] * 4,
        compiler_params=_params(1),
    )(w, g, m, v, *([] if after is None else [after]))


def _mesh_pos():
    return lax.axis_index("x"), lax.axis_index("y"), lax.axis_index("c")


def _other_chips(x, y):
    return [(1 - x, y), (x, 1 - y), (1 - x, 1 - y)]


def _half_rows(ref, which):
    rh = ref.shape[-2] // 2
    return ref.at[pl.ds(which * rh, rh), :]


def _remote(src, dst, send_sems, recv_sems, sem, to):
    return pltpu.make_async_remote_copy(src_ref=src, dst_ref=dst, send_sem=send_sems.at[sem], recv_sem=recv_sems.at[sem],
                                        device_id=to, device_id_type=MESH)


HBM = pl.BlockSpec(memory_space=pltpu.HBM)
SEM = pl.BlockSpec(memory_space=pltpu.SEMAPHORE)
DATAFLOW_EFFECT = pltpu.SideEffectType.DATAFLOW_SIDE_EFFECTING


def _in_hbm(arrays):
    return [pltpu.with_memory_space_constraint(a, pltpu.HBM) for a in arrays]


def _hbm_like(arrays):
    return [pltpu.HBM(a.shape, a.dtype) for a in arrays]


GATHER_COPIES_PER_ARRAY = {"direct": 2, "forward": 2, "pass_near": 2, "pass_far": 1}


def _gather_copies(kind, refs, x, y, c):
    me, near_x, near_y, far = 2 * x + y, 2 * (1 - x) + y, 2 * x + (1 - y), 2 * (1 - x) + (1 - y)
    to_x, to_y, sibling = (1 - x, y, c), (x, 1 - y, c), (x, y, 1 - c)
    out = []
    for ref in refs:
        rh = ref.shape[1] // 2
        rq = rh // 2

        def half(chip, ref=ref, rh=rh):
            return ref.at[chip, pl.ds(c * rh, rh), :]

        def quarter(chip, q, ref=ref, rh=rh, rq=rq):
            return ref.at[chip, pl.ds(c * rh + q * rq, rq), :]

        if kind == "direct":
            out += [(half(me), half(me), to_x), (half(me), half(me), to_y)]
        elif kind == "forward":
            out += [(quarter(near_x, 0), quarter(near_x, 0), to_y), (quarter(near_y, 1), quarter(near_y, 1), to_x)]
        elif kind == "pass_near":
            out += [(half(near_x), half(near_x), sibling), (half(near_y), half(near_y), sibling)]
        else:
            assert kind == "pass_far"
            out += [(half(far), half(far), sibling)]
    return out


def _gather_step(name, bufs, waits, starts, after):
    nb, nw, ns = len(bufs), len(waits), len(starts)
    after = [] if after is None else list(after) if isinstance(after, (list, tuple)) else [after]
    n_after = len(after)

    def body(*refs):
        ins = refs[:nb]
        wait_sems = refs[nb:nb + 2 * nw]
        start_sems = refs[nb + 2 * nw + n_after:nb + 2 * nw + n_after + 2 * ns]
        token = refs[-1]
        x, y, c = _mesh_pos()
        for j, (kind, idxs, _, _) in enumerate(waits):
            for i, (s_ref, d_ref, to) in enumerate(_gather_copies(kind, [ins[t] for t in idxs], x, y, c)):
                came = _remote(s_ref, d_ref, wait_sems[2 * j], wait_sems[2 * j + 1], i, to)
                came.wait_recv()
                came.wait_send()
        for j, (kind, idxs) in enumerate(starts):
            for i, (s_ref, d_ref, to) in enumerate(_gather_copies(kind, [ins[t] for t in idxs], x, y, c)):
                _remote(s_ref, d_ref, start_sems[2 * j], start_sems[2 * j + 1], i, to).start()
        token[...] = jnp.zeros_like(token)

    sems = []
    for kind, idxs in starts:
        sems += [pltpu.SemaphoreType.DMA((GATHER_COPIES_PER_ARRAY[kind] * len(idxs),))] * 2
    operands = _in_hbm(bufs) + [sem for w in waits for sem in w[2:]] + after
    outs = pl.pallas_call(
        body, name=name,
        in_specs=[HBM] * nb + [SEM] * (2 * nw) + [ANY] * n_after,
        out_specs=[SEM] * (2 * ns) + [HBM] * nb + [pl.BlockSpec(memory_space=pltpu.VMEM)],
        out_shape=sems + _hbm_like(bufs) + [jax.ShapeDtypeStruct((8, 128), F32)],
        input_output_aliases={i: 2 * ns + i for i in range(nb)},
        compiler_params=pltpu.CompilerParams(has_side_effects=DATAFLOW_EFFECT),
    )(*operands)
    return outs[2 * ns:2 * ns + nb], [(outs[2 * j], outs[2 * j + 1]) for j in range(ns)], outs[-1]


class _Gather:
    def __init__(self, groups):
        self.groups = groups
        self.bufs = {}
        self.in_flight = {}

    def put(self, slotted):
        self.bufs.update(slotted)

    def step(self, name, waits, starts, after=None):
        names = []
        for _, group in list(waits) + list(starts):
            names += [n for n in self.groups[group] if n not in names]
        index = {n: i for i, n in enumerate(names)}

        def members(group):
            return [index[n] for n in self.groups[group]]

        wait_args = [(kind, members(group)) + self.in_flight.pop((kind, group)) for kind, group in waits]
        start_args = [(kind, members(group)) for kind, group in starts]
        bufs, sems, token = _gather_step(name, [self.bufs[n] for n in names], wait_args, start_args, after)
        self.bufs.update(zip(names, bufs))
        for (kind, group), pair in zip(starts, sems):
            self.in_flight[(kind, group)] = pair
        return token

    def arrays(self, group):
        return {n: self.bufs[n] for n in self.groups[group]}


def _sibling_halves_copies(srcs, dsts, x, y, c):
    out = []
    for s_ref, d_ref in zip(srcs, dsts, strict=True):
        rh = s_ref.shape[1] // 2
        out.append((s_ref.at[:, pl.ds((1 - c) * rh, rh), :], d_ref, (x, y, 1 - c)))
    return out


def _to_sibling_copies(srcs, dsts, x, y, c):
    return [(s_ref, d_ref, (x, y, 1 - c)) for s_ref, d_ref in zip(srcs, dsts, strict=True)]


def _chip_copies(srcs, dsts, x, y, c):
    out = []
    for s_ref, d_ref in zip(srcs, dsts, strict=True):
        for k, (px, py) in enumerate(_other_chips(x, y)):
            out.append((s_ref.at[2 * px + py], d_ref.at[k], (px, py, c)))
    return out


def _join_copies(srcs, dsts, x, y, c):
    out = []
    for s_ref in srcs:
        mine = _half_rows(s_ref, c)
        out.append((mine, mine, (x, y, 1 - c)))
    return out


def _exchange_start(copies_fn, n_copies, srcs, fresh, after, name):
    ns, nb = len(srcs), len(srcs) + len(fresh)

    def body(*refs):
        bufs, send, recv, token = refs[:nb], refs[nb + 1], refs[nb + 2], refs[-1]
        x, y, c = _mesh_pos()
        for i, (s_ref, d_ref, to) in enumerate(copies_fn(bufs[:ns], bufs[ns:] if fresh else bufs[:ns], x, y, c)):
            _remote(s_ref, d_ref, send, recv, i, to).start()
        token[...] = jnp.zeros_like(token)

    sems = [pltpu.SemaphoreType.DMA((n_copies,))] * 2
    outs = pl.pallas_call(
        body, name=name,
        in_specs=[HBM] * nb + [ANY], out_specs=[SEM, SEM] + [HBM] * nb + [pl.BlockSpec(memory_space=pltpu.VMEM)],
        out_shape=sems + _hbm_like(list(srcs) + list(fresh)) + [jax.ShapeDtypeStruct((8, 128), F32)],
        input_output_aliases={i: 2 + i for i in range(nb)},
        compiler_params=pltpu.CompilerParams(has_side_effects=DATAFLOW_EFFECT),
    )(*_in_hbm(list(srcs) + list(fresh)), after)
    return outs[0], outs[1], outs[2:2 + ns], outs[2 + ns:2 + nb], outs[-1]


def _exchange_done(copies_fn, srcs, fresh, send, recv, after, name):
    ns, nb = len(srcs), len(srcs) + len(fresh)

    def body(*refs):
        bufs, send_in, recv_in = refs[:nb], refs[nb], refs[nb + 1]
        x, y, c = _mesh_pos()
        for i, (s_ref, d_ref, to) in enumerate(copies_fn(bufs[:ns], bufs[ns:] if fresh else bufs[:ns], x, y, c)):
            came = _remote(s_ref, d_ref, send_in, recv_in, i, to)
            came.wait_send()
            came.wait_recv()

    outs = pl.pallas_call(
        body, name=name,
        in_specs=[HBM] * nb + [SEM, SEM, ANY], out_specs=[HBM] * nb,
        out_shape=_hbm_like(list(srcs) + list(fresh)),
        input_output_aliases={i: i for i in range(nb)},
        compiler_params=pltpu.CompilerParams(has_side_effects=DATAFLOW_EFFECT),
    )(*_in_hbm(list(srcs) + list(fresh)), send, recv, after)
    return outs[:ns], outs[ns:]


class _Reduce:
    def __init__(self, place, core, shards, mom_m, mom_v):
        self.place, self.core = place, core
        self.shards, self.mom_m, self.mom_v = shards, mom_m, mom_v
        self.state = {}
        self.results = {}

    def add(self, group, grads, after):
        names = list(grads)
        g4s = [g.reshape((N_CHIPS, -1, g.shape[-1])) if g.ndim == 2 else g for g in grads.values()]
        fresh = [lax.empty((N_CHIPS, g.shape[1] // 2, g.shape[2]), BF16) for g in g4s]
        send, recv, g4s, fresh, token = _exchange_start(_sibling_halves_copies, len(names), g4s, fresh, after,
                                                        "pair_start_" + group)
        self.state[group] = (0, names, send, recv, g4s, fresh)
        return token

    def send(self, group, theirs, after):
        names, srcs = list(theirs), list(theirs.values())
        fresh = [lax.empty(s.shape, BF16) for s in srcs]
        send, recv, srcs, fresh, token = _exchange_start(_to_sibling_copies, len(names), srcs, fresh, after, "pair_start_" + group)
        self.state[group] = ("sent", names, send, recv, srcs, fresh)
        return token

    def received(self, group, after):
        stage, names, send, recv, srcs, fresh = self.state.pop(group)
        assert stage == "sent"
        _, got = _exchange_done(_to_sibling_copies, srcs, fresh, send, recv, after, "pair_done_" + group)
        return dict(zip(names, got))

    def add_parts(self, group, parts):
        names, srcs = list(parts), list(parts.values())
        fresh = [lax.empty((N_CHIPS - 1,) + p.shape[1:], BF16) for p in srcs]
        send, recv, srcs, fresh, token = _exchange_start(_chip_copies, 3 * len(names), srcs, fresh, self.core, "chips_start_" + group)
        self.state[group] = (1, names, send, recv, srcs, fresh)
        return token

    def step(self, group, after):
        stage, names, send, recv, srcs, fresh = self.state[group]
        if stage == 0:
            g4s, ras = _exchange_done(_sibling_halves_copies, srcs, fresh, send, recv, after, "pair_done_" + group)
            parts = [_pair_sum(g, r, self.core, "pair_sum_" + n) for g, r, n in zip(g4s, ras, names)]
            fresh = [lax.empty((N_CHIPS - 1,) + p.shape[1:], BF16) for p in parts]
            send, recv, parts, fresh, token = _exchange_start(_chip_copies, 3 * len(names), parts, fresh, self.core,
                                                              "chips_start_" + group)
            self.state[group] = (1, names, send, recv, parts, fresh)
            return token
        if stage == 1:
            parts, rcs = _exchange_done(_chip_copies, srcs, fresh, send, recv, after, "chips_done_" + group)
            token = None
            for n, p, r in zip(names, parts, rcs):
                self.results[n] = _adamw_own_half(self.shards[n], self.mom_m[n], self.mom_v[n], p, r, self.place,
                                                  "adamw_own_" + n, after=token)
                token = self.results[n][2]
            wholes = [self.results[n][0] for n in names]
            send, recv, wholes, _, token = _exchange_start(_join_copies, len(names), wholes, [], token, "join_start_" + group)
            self.state[group] = (2, names, send, recv, wholes, [])
            return token
        assert stage == 2
        wholes, _ = _exchange_done(_join_copies, srcs, [], send, recv, after, "join_done_" + group)
        token = None
        for n, exchanged in zip(names, wholes):
            _, g, d, nm, nv = self.results[n]
            self.results[n] = _adamw_other_half(self.shards[n], self.mom_m[n], self.mom_v[n], exchanged, g, d, nm, nv,
                                                self.place, "adamw_other_" + n, after=token)
            token = self.results[n][1]
        del self.state[group]
        return token


N_DEV = 8


def _to_all_copies(srcs, dsts, x, y, c):
    out = []
    for r in range(1, N_DEV):
        fx, fy, fc = (r >> 2) & 1, (r >> 1) & 1, r & 1
        out.append((srcs[0], dsts[0].at[r - 1], (x + fx - 2 * x * fx, y + fy - 2 * y * fy, c + fc - 2 * c * fc)))
    return out


def _all_reduce_small_start(v, after):
    slots = lax.empty((N_DEV - 1,) + v.shape, v.dtype)
    send, recv, (v,), (slots,), token = _exchange_start(_to_all_copies, N_DEV - 1, [v], [slots], after, "small_grads_start")
    return (send, recv, v, slots), token


def _all_reduce_small_done(started, after):
    send, recv, v, slots = started
    (v,), (slots,) = _exchange_done(_to_all_copies, [v], [slots], send, recv, after, "small_grads_done")

    def body(v_ref, slots_ref, o_ref):
        x, y, c = _mesh_pos()
        me = 4 * x + 2 * y + c
        acc = None
        for i in range(N_DEV):
            r = jnp.bitwise_xor(me, i)
            part = jnp.where(r == 0, v_ref[...], slots_ref[jnp.maximum(r - 1, 0)])
            acc = part if acc is None else acc + part
        o_ref[...] = acc

    vm = pl.BlockSpec(memory_space=pltpu.VMEM)
    return pl.pallas_call(body, name="small_grads_sum", in_specs=[vm, vm], out_specs=vm,
                          out_shape=jax.ShapeDtypeStruct(v.shape, v.dtype))(v, slots)


MATRICES = ("w_in", "w_attn_out", "w_conv_out", "w_o", "w_cq", "w_ckv", "w_co", "w_gate", "w_up", "w_down")
VECTORS = ("g_mix", "b_gate", "g_cross", "g_mem", "g_ffn", "g_final", "conv_w", "sink")
WEIGHT_ORDER = ("g_mix", "w_in", "sink", "conv_w", "b_gate", "w_attn_out", "w_conv_out", "w_o", "g_cross", "g_mem", "w_cq",
                "w_ckv", "w_co", "g_ffn", "w_gate", "w_up", "w_down", "g_final")
CONV_PAD_ROWS = 32
SMALL_ROWS = 8


def _pack(pieces):
    flat = jnp.concatenate([p.reshape(-1) for p in pieces])
    lane_group = SMALL_ROWS * 128
    total = -(-flat.shape[0] // lane_group) * lane_group
    flat = jnp.pad(flat, (0, total - flat.shape[0]))
    return flat.reshape(SMALL_ROWS, total // SMALL_ROWS), [p.size for p in pieces]


def _unpack(packed, pieces):
    flat = packed.reshape(-1)
    out, off = [], 0
    for p in pieces:
        out.append(flat[off:off + p.size].reshape(p.shape))
        off += p.size
    return out


def kernel(x, mem, g_mix, w_in, sink, conv_w, b_gate, w_attn_out, w_conv_out, w_o, g_cross, g_mem, w_cq, w_ckv, w_co, g_ffn, w_gate, w_up, w_down, g_final, loss_target, m_g_mix, m_w_in, m_sink, m_conv_w, m_b_gate, m_w_attn_out, m_w_conv_out, m_w_o, m_g_cross, m_g_mem, m_w_cq, m_w_ckv, m_w_co, m_g_ffn, m_w_gate, m_w_up, m_w_down, m_g_final, v_g_mix, v_w_in, v_sink, v_conv_w, v_b_gate, v_w_attn_out, v_w_conv_out, v_w_o, v_g_cross, v_g_mem, v_w_cq, v_w_ckv, v_w_co, v_g_ffn, v_w_gate, v_w_up, v_w_down, v_g_final):
    given = dict(g_mix=g_mix, w_in=w_in, sink=sink, conv_w=conv_w, b_gate=b_gate, w_attn_out=w_attn_out, w_conv_out=w_conv_out,
                 w_o=w_o, g_cross=g_cross, g_mem=g_mem, w_cq=w_cq, w_ckv=w_ckv, w_co=w_co, g_ffn=g_ffn, w_gate=w_gate, w_up=w_up,
                 w_down=w_down, g_final=g_final)
    mom_m = dict(g_mix=m_g_mix, w_in=m_w_in, sink=m_sink, conv_w=m_conv_w, b_gate=m_b_gate, w_attn_out=m_w_attn_out,
                 w_conv_out=m_w_conv_out, w_o=m_w_o, g_cross=m_g_cross, g_mem=m_g_mem, w_cq=m_w_cq, w_ckv=m_w_ckv, w_co=m_w_co,
                 g_ffn=m_g_ffn, w_gate=m_w_gate, w_up=m_w_up, w_down=m_w_down, g_final=m_g_final)
    mom_v = dict(g_mix=v_g_mix, w_in=v_w_in, sink=v_sink, conv_w=v_conv_w, b_gate=v_b_gate, w_attn_out=v_w_attn_out,
                 w_conv_out=v_w_conv_out, w_o=v_w_o, g_cross=v_g_cross, g_mem=v_g_mem, w_cq=v_w_cq, w_ckv=v_w_ckv, w_co=v_w_co,
                 g_ffn=v_g_ffn, w_gate=v_w_gate, w_up=v_w_up, w_down=v_w_down, g_final=v_g_final)
    xs, mems, target = x[0], mem[0], loss_target[0]
    d_model = xs.shape[1]
    chip = 2 * lax.axis_index("x") + lax.axis_index("y")
    core = jnp.reshape(lax.axis_index("c"), (1,)).astype(jnp.int32)
    place = jnp.stack([chip, lax.axis_index("c")]).astype(jnp.int32)

    shards = {n: given[n][0] for n in MATRICES}
    conv_cols = conv_w.shape[2]
    conv_pad = jnp.pad(conv_w[0], ((0, CONV_PAD_ROWS - conv_w.shape[1]), (0, 0)))
    fetch = _Gather(GATHER_GROUPS)
    first = {"w_in": _cast_to_slot(shards["w_in"], place, BF16, "to_slot_w_in"),
             "conv_w": _cast_to_slot(conv_pad, place, F32, "to_slot_conv_w")}
    fetch.put(first)
    tok = fetch.step("gather_start", [], [("direct", "in")])
    fetch.put({n: _cast_to_slot(shards[n], place, BF16, "to_slot_" + n, after=tok) for n in MATRICES if n != "w_in"})
    small = {n: given[n] for n in ("g_mix", "b_gate", "g_cross", "g_mem", "g_ffn")}
    small["g_final"] = g_final[None]
    small["sink"] = sink[0]

    reduce = _Reduce(place, core, shards, {n: mom_m[n][0] for n in MATRICES}, {n: mom_v[n][0] for n in MATRICES})
    sq, grad_x, small_grads = _local_step(xs, mems, target, small, fetch, reduce)

    loss_part = 0.5 * sq[0:1, 0:1] / d_model
    pieces = [small_grads[n] for n in VECTORS] + [loss_part]
    packed, _ = _pack(pieces)
    started, tok = _all_reduce_small_start(packed, core)
    tok = reduce.step("in", tok)
    tok = reduce.step("mid", tok)
    summed = _unpack(_all_reduce_small_done(started, tok), pieces)
    loss = summed[-1][0, 0]
    small_sum = dict(zip(VECTORS, summed[:-1]))
    small_sum["conv_w"] = lax.dynamic_slice_in_dim(small_sum["conv_w"], chip * conv_cols, conv_cols, axis=1)

    grad_out, delta, new_m, new_v = {}, {}, {}, {}
    like = [given[n] for n in VECTORS]
    pw, _ = _pack(like)
    pg, _ = _pack([small_sum[n] for n in VECTORS])
    pm, _ = _pack([mom_m[n] for n in VECTORS])
    pv, _ = _pack([mom_v[n] for n in VECTORS])
    _, pd, pnm, pnv = _adamw(pw, pg, pm, pv, "adamw_small")
    for n, g, d, nm, nv in zip(VECTORS, [small_sum[n] for n in VECTORS], _unpack(pd, like), _unpack(pnm, like), _unpack(pnv, like)):
        grad_out[n] = g.reshape(given[n].shape)
        delta[n], new_m[n], new_v[n] = d, nm, nv
    reduce.step("in", pd)
    for n in MATRICES:
        g, d, nm, nv = reduce.results[n]
        grad_out[n], delta[n], new_m[n], new_v[n] = g[None], d[None], nm[None], nv[None]

    return (loss, grad_x[None], *[grad_out[n] for n in WEIGHT_ORDER], *[delta[n] for n in WEIGHT_ORDER],
            *[new_m[n] for n in WEIGHT_ORDER], *[new_v[n] for n in WEIGHT_ORDER])
```

```python
import jax
import jax.numpy as jnp
from jax import lax
from jax.experimental import pallas as pl
from jax.experimental.pallas import tpu as pltpu

F32 = jnp.float32
BF16 = jnp.bfloat16
MESH = pl.DeviceIdType.MESH
ANY = pl.BlockSpec(memory_space=pl.ANY)

VMEM_LIMIT_BYTES = 56 * 1024 * 1024

N_CHIPS = 4
HEAD_DIM = 128
N_Q_HEADS = 8
N_KV_HEADS = 2
Q_GROUP = N_Q_HEADS // N_KV_HEADS
ATTN_WIDTH = N_Q_HEADS * HEAD_DIM
KV_WIDTH = N_KV_HEADS * HEAD_DIM
WINDOW = 128
BLOCK = 128
BAND = 3 * BLOCK
ROPE_THETA = 10000.0
CONV_WIDTH = 1024
MEM_HEADS = 4
MEM_WIDTH = MEM_HEADS * HEAD_DIM
RMS_EPS = 1e-6
NEG_INF = -1e30
ATTN_SCALE = HEAD_DIM ** -0.5

Q_OFF, K_OFF, V_OFF, CU_OFF, CB_OFF, CC_OFF, GL_OFF = 0, 1024, 1280, 1536, 2560, 3584, 4608

ADAM_LR = 0.001
ADAM_B1 = 0.9
ADAM_B2 = 0.999
ADAM_EPS = 1e-08
ADAM_WD = 0.01
ADAM_STEP = 10
ADAM_C1 = 1.0 - ADAM_B1 ** ADAM_STEP
ADAM_C2 = 1.0 - ADAM_B2 ** ADAM_STEP


def _params(n_grid_axes):
    return pltpu.CompilerParams(dimension_semantics=("arbitrary",) * n_grid_axes, vmem_limit_bytes=VMEM_LIMIT_BYTES)


BF16_SUBLANES = 16


def _row_tile(rows, want):
    if rows <= want:
        return rows
    for t in range(want, 0, -BF16_SUBLANES):
        if rows % t == 0:
            return t
    return rows


def _matmul(a, b, *, mode, tm, tn, out_dtypes, name, extras=(), epilogue=None, b_blocks=1, out_blocks=1, after=None):
    if mode == "tn":
        kdim, m = a.shape
    else:
        m, kdim = a.shape
    if b_blocks > 1:
        nb, brows, bcols = b.shape
        assert nb == b_blocks
        if mode == "nn":
            n = bcols * nb
            assert brows == kdim
        else:
            assert mode == "nt" and bcols * nb == kdim
            n = brows
    else:
        n = b.shape[0] if mode == "nt" else b.shape[1]
    tm, tn = min(tm, m), min(tn, n)
    tk = kdim
    assert m % tm == 0 and n % tn == 0, (name, m, n, tm, tn)
    n_extra, n_out = len(extras), len(out_dtypes)
    n_after = 0 if after is None else 1

    if mode == "tn":
        a_spec = pl.BlockSpec((tk, tm), lambda j, i, k: (k, i))
        dims = (((0,), (0,)), ((), ()))
    else:
        a_spec = pl.BlockSpec((tm, tk), lambda j, i, k: (i, k))
        dims = (((1,), (0,)), ((), ())) if mode == "nn" else (((1,), (1,)), ((), ()))

    if b_blocks > 1 and mode == "nn":
        per = b.shape[2] // tn
        assert b.shape[2] % tn == 0
        b_spec = pl.BlockSpec((None, tk, tn), lambda j, i, k: (j // per, k, j % per))
    elif b_blocks > 1:
        b_spec = pl.BlockSpec((b_blocks, tn, b.shape[2]), lambda j, i, k: (0, j, 0))
    elif mode == "nt":
        b_spec = pl.BlockSpec((tn, tk), lambda j, i, k: (j, k))
    else:
        b_spec = pl.BlockSpec((tk, tn), lambda j, i, k: (k, j))

    tile_spec = pl.BlockSpec((tm, tn), lambda j, i, k: (i, j))
    if out_blocks > 1:
        ncols = n // out_blocks
        assert ncols % tn == 0
        oper = ncols // tn
        out_spec = pl.BlockSpec((None, tm, tn), lambda j, i, k: (j // oper, i, j % oper))
        out_shape = [jax.ShapeDtypeStruct((out_blocks, m, ncols), dt) for dt in out_dtypes]
    else:
        out_spec = tile_spec
        out_shape = [jax.ShapeDtypeStruct((m, n), dt) for dt in out_dtypes]

    def body(a_ref, b_ref, *rest):
        extra_refs = rest[:n_extra]
        out_refs = rest[n_extra + n_after:n_extra + n_after + n_out]
        if mode == "nt" and b_blocks > 1:
            cs = b.shape[2]
            acc = None
            for jb in range(b_blocks):
                prod = lax.dot_general(a_ref[:, jb * cs:(jb + 1) * cs].astype(BF16), b_ref[jb].astype(BF16), dims,
                                       preferred_element_type=F32)
                acc = prod if acc is None else acc + prod
        else:
            acc = lax.dot_general(a_ref[...].astype(BF16), b_ref[...].astype(BF16), dims, preferred_element_type=F32)
        tiles = (acc,) if epilogue is None else epilogue(acc, *[r[...] for r in extra_refs])
        for o_ref, t in zip(out_refs, tiles, strict=True):
            o_ref[...] = t.astype(o_ref.dtype)

    outs = pl.pallas_call(
        body,
        name=name,
        grid=(n // tn, m // tm, 1),
        in_specs=[a_spec, b_spec] + [tile_spec] * n_extra + [ANY] * n_after,
        out_specs=[out_spec] * n_out,
        out_shape=out_shape,
        compiler_params=_params(3),
    )(a, b, *extras, *([] if after is None else [after]))
    return outs[0] if n_out == 1 else outs


def _add_residual(acc, res):
    return (acc + res,)


def _matmul_column_blocks(a, b4, blocks, out, *, tm, name, after=None):
    m, kdim = a.shape
    nb, _, cols = b4.shape
    tm = min(tm, m)
    assert m % tm == 0

    def body(j_ref, a_ref, b_ref, *rest):
        rest[-1][...] = jnp.dot(a_ref[...], b_ref[...], preferred_element_type=F32)

    extra = ([] if out is None else [out]) + ([] if after is None else [after])
    n_blocks = blocks.shape[0]
    return pl.pallas_call(
        body, name=name,
        grid_spec=pltpu.PrefetchScalarGridSpec(
            num_scalar_prefetch=1, grid=(n_blocks, m // tm),
            in_specs=[pl.BlockSpec((tm, kdim), lambda j, i, blk: (i, 0)),
                      pl.BlockSpec((None, kdim, cols), lambda j, i, blk: (blk[j], 0, 0))] + [ANY] * len(extra),
            out_specs=pl.BlockSpec((tm, cols), lambda j, i, blk: (i, blk[j]))),
        out_shape=jax.ShapeDtypeStruct((m, nb * cols), F32),
        input_output_aliases={} if out is None else {3: 0},
        compiler_params=_params(2),
    )(blocks, a, b4, *extra)


def _wgrad_half(a, b, core, *, theirs, row_sharded, tm, tn, name, add=None, after=None):
    kdim, m = a.shape
    n = b.shape[1]
    rs, cs = (m // N_CHIPS, n) if row_sharded else (m, n // N_CHIPS)
    rh = rs // 2
    tm, tn = min(tm, rh), min(tn, cs)
    assert rh % tm == 0 and cs % tn == 0, (name, rh, cs, tm, tn)
    mh, per = rh // tm, cs // tn
    has_add = add is not None

    def half(c):
        return 1 - c[0] if theirs else c[0]

    if row_sharded:
        grid = (n // tn, N_CHIPS * mh)
        a_spec = pl.BlockSpec((kdim, tm), lambda j, r, c: (0, ((r // mh) * 2 + half(c)) * mh + r % mh))
        o_spec = pl.BlockSpec((None, tm, tn), lambda j, r, c: (r // mh, r % mh, j))
    else:
        grid = (n // tn, mh)
        a_spec = pl.BlockSpec((kdim, tm), lambda j, r, c: (0, half(c) * mh + r))
        o_spec = pl.BlockSpec((None, tm, tn), lambda j, r, c: (j // per, r, j % per))
    b_spec = pl.BlockSpec((kdim, tn), lambda j, r, c: (0, j))

    def body(c_ref, a_ref, b_ref, *rest):
        o_ref = rest[-1]
        acc = lax.dot_general(a_ref[...].astype(BF16), b_ref[...].astype(BF16), (((0,), (0,)), ((), ())),
                              preferred_element_type=F32)
        if has_add:
            acc = acc + rest[0][...].astype(F32)
        o_ref[...] = acc.astype(BF16)

    operands = [a, b] + ([add] if has_add else []) + ([] if after is None else [after])
    return pl.pallas_call(
        body, name=name,
        grid_spec=pltpu.PrefetchScalarGridSpec(
            num_scalar_prefetch=1, grid=grid,
            in_specs=[a_spec, b_spec] + ([o_spec] if has_add else []) + ([] if after is None else [ANY]),
            out_specs=o_spec),
        out_shape=jax.ShapeDtypeStruct((N_CHIPS, rh, cs), BF16),
        compiler_params=_params(2),
    )(core, *operands)


def _rstd(x):
    return lax.rsqrt(jnp.mean(x * x, axis=-1, keepdims=True) + RMS_EPS)


def _rmsnorm(x, g, name):
    s, d = x.shape
    tr = _row_tile(s, 512)

    def body(x_ref, g_ref, o_ref):
        xv = x_ref[...]
        o_ref[...] = (xv * _rstd(xv) * g_ref[...]).astype(BF16)

    return pl.pallas_call(
        body, name=name, grid=(s // tr,),
        in_specs=[pl.BlockSpec((tr, d), lambda i: (i, 0)), pl.BlockSpec((1, d), lambda i: (0, 0))],
        out_specs=pl.BlockSpec((tr, d), lambda i: (i, 0)),
        out_shape=jax.ShapeDtypeStruct((s, d), BF16),
        compiler_params=_params(1),
    )(x, g)


def _rmsnorm_bwd(dh, x, g, dres, name):
    s, d = x.shape
    tr = _row_tile(s, 512)
    has_res = dres is not None

    def body(*refs):
        if has_res:
            dh_ref, x_ref, g_ref, res_ref, dx_ref, dxb_ref, dg_ref = refs
        else:
            dh_ref, x_ref, g_ref, dx_ref, dxb_ref, dg_ref = refs
        xv = x_ref[...]
        dhv = dh_ref[...].astype(F32)
        r = _rstd(xv)
        xn = xv * r
        dhg = dhv * g_ref[...]
        dx = r * (dhg - xn * jnp.mean(dhg * xn, axis=-1, keepdims=True))
        if has_res:
            dx = dx + res_ref[...]
        dx_ref[...] = dx
        dxb_ref[...] = dx.astype(BF16)
        part = jnp.sum(dhv * xn, axis=0, keepdims=True)

        @pl.when(pl.program_id(0) == 0)
        def _():
            dg_ref[...] = part

        @pl.when(pl.program_id(0) > 0)
        def _():
            dg_ref[...] += part

    row = pl.BlockSpec((tr, d), lambda i: (i, 0))
    vec = pl.BlockSpec((1, d), lambda i: (0, 0))
    return pl.pallas_call(
        body, name=name, grid=(s // tr,),
        in_specs=[row, row, vec] + ([row] if has_res else []),
        out_specs=[row, row, vec],
        out_shape=[jax.ShapeDtypeStruct((s, d), F32), jax.ShapeDtypeStruct((s, d), BF16), jax.ShapeDtypeStruct((1, d), F32)],
        compiler_params=_params(1),
    )(*([dh, x, g] + ([dres] if has_res else [])))


def _loss_head(x3, g, target):
    s, d = x3.shape
    tr = _row_tile(s, 512)

    def body(x_ref, g_ref, t_ref, dx_ref, dxb_ref, sq_ref, dg_ref):
        xv = x_ref[...]
        gv = g_ref[...]
        r = _rstd(xv)
        xn = xv * r
        err = xn * gv - t_ref[...]
        dy = err * (1.0 / d)
        dyg = dy * gv
        dx = r * (dyg - xn * jnp.mean(dyg * xn, axis=-1, keepdims=True))
        dx_ref[...] = dx
        dxb_ref[...] = dx.astype(BF16)
        sq = jnp.sum(jnp.sum(err * err, axis=1, keepdims=True), axis=0, keepdims=True)
        sq = jnp.broadcast_to(sq, (1, 128))
        part = jnp.sum(dy * xn, axis=0, keepdims=True)

        @pl.when(pl.program_id(0) == 0)
        def _():
            sq_ref[...] = sq
            dg_ref[...] = part

        @pl.when(pl.program_id(0) > 0)
        def _():
            sq_ref[...] += sq
            dg_ref[...] += part

    row = pl.BlockSpec((tr, d), lambda i: (i, 0))
    vec = pl.BlockSpec((1, d), lambda i: (0, 0))
    return pl.pallas_call(
        body, name="loss_head", grid=(s // tr,),
        in_specs=[row, vec, row],
        out_specs=[row, row, pl.BlockSpec((1, 128), lambda i: (0, 0)), vec],
        out_shape=[jax.ShapeDtypeStruct((s, d), F32), jax.ShapeDtypeStruct((s, d), BF16),
                   jax.ShapeDtypeStruct((1, 128), F32), jax.ShapeDtypeStruct((1, d), F32)],
        compiler_params=_params(1),
    )(x3, g, target)


def _rope_tables(s):
    inv = 1.0 / (ROPE_THETA ** (jnp.arange(0, HEAD_DIM, 2, dtype=F32) / HEAD_DIM))
    ang = jnp.arange(s, dtype=F32)[:, None] * inv[None, :]
    cos, sin = jnp.cos(ang), jnp.sin(ang)
    return jnp.concatenate([cos, cos], axis=1), jnp.concatenate([-sin, sin], axis=1)


def _swap_halves(t):
    return pltpu.roll(t, HEAD_DIM // 2, 1)


def _rope_fwd(z, cos_t, sin_t):
    s = z.shape[0]
    tr = _row_tile(s, 256)

    def body(zq_ref, zk_ref, zv_ref, c_ref, s_ref, q_ref, k_ref, v_ref):
        c, sn = c_ref[...], s_ref[...]
        for hd in range(N_Q_HEADS):
            cols = slice(hd * HEAD_DIM, (hd + 1) * HEAD_DIM)
            t = zq_ref[:, cols]
            q_ref[:, cols] = (t * c + _swap_halves(t) * sn).astype(BF16)
        for hd in range(N_KV_HEADS):
            cols = slice(hd * HEAD_DIM, (hd + 1) * HEAD_DIM)
            t = zk_ref[:, cols]
            k_ref[:, cols] = (t * c + _swap_halves(t) * sn).astype(BF16)
        v_ref[...] = zv_ref[...].astype(BF16)

    tab = pl.BlockSpec((tr, HEAD_DIM), lambda i: (i, 0))
    return pl.pallas_call(
        body, name="rope_fwd", grid=(s // tr,),
        in_specs=[pl.BlockSpec((tr, ATTN_WIDTH), lambda i: (i, Q_OFF // ATTN_WIDTH)),
                  pl.BlockSpec((tr, KV_WIDTH), lambda i: (i, K_OFF // KV_WIDTH)),
                  pl.BlockSpec((tr, KV_WIDTH), lambda i: (i, V_OFF // KV_WIDTH)), tab, tab],
        out_specs=[pl.BlockSpec((tr, ATTN_WIDTH), lambda i: (i, 0)), pl.BlockSpec((tr, KV_WIDTH), lambda i: (i, 0)),
                   pl.BlockSpec((tr, KV_WIDTH), lambda i: (i, 0))],
        out_shape=[jax.ShapeDtypeStruct((s, ATTN_WIDTH), BF16), jax.ShapeDtypeStruct((s, KV_WIDTH), BF16),
                   jax.ShapeDtypeStruct((s, KV_WIDTH), BF16)],
        compiler_params=_params(1),
    )(z, z, z, cos_t, sin_t)


def _rope_bwd(dq_rot, dk_rot, dv, cos_t, sin_t, dz):
    s = dq_rot.shape[0]
    tr = _row_tile(s, 256)
    qkv_width = V_OFF + KV_WIDTH

    def body(dq_ref, dk_ref, dv_ref, c_ref, s_ref, dz_in_ref, o_ref):
        c, sn = c_ref[...], s_ref[...]
        for hd in range(N_Q_HEADS):
            t = dq_ref[:, hd * HEAD_DIM:(hd + 1) * HEAD_DIM]
            o_ref[:, Q_OFF + hd * HEAD_DIM:Q_OFF + (hd + 1) * HEAD_DIM] = (t * c + _swap_halves(t * sn)).astype(BF16)
        for hd in range(N_KV_HEADS):
            t = dk_ref[:, hd * HEAD_DIM:(hd + 1) * HEAD_DIM]
            o_ref[:, K_OFF + hd * HEAD_DIM:K_OFF + (hd + 1) * HEAD_DIM] = (t * c + _swap_halves(t * sn)).astype(BF16)
        o_ref[:, V_OFF:V_OFF + KV_WIDTH] = dv_ref[...].astype(BF16)

    tab = pl.BlockSpec((tr, HEAD_DIM), lambda i: (i, 0))
    wide = pl.BlockSpec((tr, ATTN_WIDTH), lambda i: (i, 0))
    narrow = pl.BlockSpec((tr, KV_WIDTH), lambda i: (i, 0))
    return pl.pallas_call(
        body, name="rope_bwd", grid=(s // tr,),
        in_specs=[wide, narrow, narrow, tab, tab, ANY],
        out_specs=pl.BlockSpec((tr, qkv_width), lambda i: (i, 0)),
        out_shape=jax.ShapeDtypeStruct(dz.shape, dz.dtype),
        input_output_aliases={5: 0},
        compiler_params=_params(1),
    )(dq_rot, dk_rot, dv, cos_t, sin_t, dz)


def _swa_band(i, s):
    return pl.multiple_of(jnp.clip((i - 1) * BLOCK, 0, s - BAND), BLOCK)


SWA_HEADS_PER_PASS = Q_GROUP


def _swa_probs(q_ref, k_ref, sink_ref, heads, start, valid):
    kv = heads[0] // Q_GROUP
    cols = slice(kv * HEAD_DIM, (kv + 1) * HEAD_DIM)
    kb = k_ref[pl.ds(start, BAND), cols]
    qg = jnp.concatenate([q_ref[:, hd * HEAD_DIM:(hd + 1) * HEAD_DIM] for hd in heads], axis=0)
    sc = lax.dot_general(qg, kb, (((1,), (1,)), ((), ())), preferred_element_type=F32) * ATTN_SCALE
    sc = jnp.where(valid, sc, NEG_INF)
    sk = jnp.concatenate([jnp.full((BLOCK, 1), sink_ref[hd], F32) for hd in heads], axis=0)
    mx = jnp.maximum(jnp.max(sc, axis=1, keepdims=True), sk)
    e = jnp.exp(sc - mx)
    es = jnp.exp(sk - mx)
    inv = 1.0 / (jnp.sum(e, axis=1, keepdims=True) + es)
    return qg, kb, e * inv, es * inv


def _swa_head_passes():
    return [list(range(h0, h0 + SWA_HEADS_PER_PASS)) for h0 in range(0, N_Q_HEADS, SWA_HEADS_PER_PASS)]


def _swa_valid(i, start):
    q_pos = i * BLOCK + lax.broadcasted_iota(jnp.int32, (BLOCK, 1), 0)
    q_pos = jnp.concatenate([q_pos] * SWA_HEADS_PER_PASS, axis=0)
    k_pos = start + lax.broadcasted_iota(jnp.int32, (1, BAND), 1)
    return jnp.abs(k_pos - q_pos) <= WINDOW


def _swa_fwd(q, k, v, sink):
    s = q.shape[0]
    assert s % BLOCK == 0 and s >= BAND

    def body(sink_ref, q_ref, k_ref, v_ref, o_ref):
        i = pl.program_id(0)
        start = _swa_band(i, s)
        valid = _swa_valid(i, start)
        for heads in _swa_head_passes():
            kv = heads[0] // Q_GROUP
            _, _, p, _ = _swa_probs(q_ref, k_ref, sink_ref, heads, start, valid)
            vb = v_ref[pl.ds(start, BAND), kv * HEAD_DIM:(kv + 1) * HEAD_DIM]
            o = jnp.dot(p.astype(BF16), vb, preferred_element_type=F32)
            for g, hd in enumerate(heads):
                o_ref[:, hd * HEAD_DIM:(hd + 1) * HEAD_DIM] = o[g * BLOCK:(g + 1) * BLOCK].astype(BF16)

    whole = pl.BlockSpec((s, KV_WIDTH), lambda i: (0, 0))
    blk = pl.BlockSpec((BLOCK, ATTN_WIDTH), lambda i: (i, 0))
    return pl.pallas_call(
        body, name="swa_fwd", grid=(s // BLOCK,),
        in_specs=[pl.BlockSpec(memory_space=pltpu.SMEM), blk, whole, whole],
        out_specs=blk,
        out_shape=jax.ShapeDtypeStruct((s, ATTN_WIDTH), BF16),
        compiler_params=_params(1),
    )(sink, q, k, v)


def _swa_bwd(q, k, v, d_out, sink):
    s = q.shape[0]

    def body(sink_ref, q_ref, k_ref, v_ref, do_ref, dq_ref, dk_ref, dv_ref, dsink_ref):
        i = pl.program_id(0)

        @pl.when(i == 0)
        def _():
            dk_ref[...] = jnp.zeros_like(dk_ref)
            dv_ref[...] = jnp.zeros_like(dv_ref)
            dsink_ref[...] = jnp.zeros_like(dsink_ref)

        start = _swa_band(i, s)
        valid = _swa_valid(i, start)
        for heads in _swa_head_passes():
            kv = heads[0] // Q_GROUP
            cols = slice(kv * HEAD_DIM, (kv + 1) * HEAD_DIM)
            qg, kb, p, p_sink = _swa_probs(q_ref, k_ref, sink_ref, heads, start, valid)
            vb = v_ref[pl.ds(start, BAND), cols]
            dog = jnp.concatenate([do_ref[:, hd * HEAD_DIM:(hd + 1) * HEAD_DIM] for hd in heads], axis=0)
            dp = lax.dot_general(dog, vb, (((1,), (1,)), ((), ())), preferred_element_type=F32)
            delta = jnp.sum(p * dp, axis=1, keepdims=True)
            ds = (p * (dp - delta) * ATTN_SCALE).astype(BF16)
            dqg = jnp.dot(ds, kb, preferred_element_type=F32)
            dk_ref[pl.ds(start, BAND), cols] += lax.dot_general(ds, qg, (((0,), (0,)), ((), ())), preferred_element_type=F32)
            dv_ref[pl.ds(start, BAND), cols] += lax.dot_general(p.astype(BF16), dog, (((0,), (0,)), ((), ())),
                                                                 preferred_element_type=F32)
            dsk = p_sink * delta
            for g, hd in enumerate(heads):
                dq_ref[:, hd * HEAD_DIM:(hd + 1) * HEAD_DIM] = dqg[g * BLOCK:(g + 1) * BLOCK]
                tot = jnp.sum(dsk[g * BLOCK:(g + 1) * BLOCK], axis=0, keepdims=True)
                dsink_ref[hd:hd + 1, :] -= jnp.broadcast_to(tot, (1, 128))

    whole = pl.BlockSpec((s, KV_WIDTH), lambda i: (0, 0))
    blk = pl.BlockSpec((BLOCK, ATTN_WIDTH), lambda i: (i, 0))
    return pl.pallas_call(
        body, name="swa_bwd", grid=(s // BLOCK,),
        in_specs=[pl.BlockSpec(memory_space=pltpu.SMEM), blk, whole, whole, blk],
        out_specs=[blk, whole, whole, pl.BlockSpec((N_Q_HEADS, 128), lambda i: (0, 0))],
        out_shape=[jax.ShapeDtypeStruct((s, ATTN_WIDTH), F32), jax.ShapeDtypeStruct((s, KV_WIDTH), F32),
                   jax.ShapeDtypeStruct((s, KV_WIDTH), F32), jax.ShapeDtypeStruct((N_Q_HEADS, 128), F32)],
        compiler_params=_params(1),
    )(sink, q, k, v, d_out)


CONV_CHUNK = 256


def _shift_rows(t, rows, down):
    n = t.shape[0]
    rolled = pltpu.roll(t, 1 if down else n - 1, 0)
    edge = 0 if down else n - 1
    return jnp.where(rows == edge, 0.0, rolled)


def _conv_specs(s):
    def z_spec(off):
        return pl.BlockSpec((s, CONV_CHUNK), lambda j, off=off: (0, off // CONV_CHUNK + j))
    chunk = pl.BlockSpec((s, CONV_CHUNK), lambda j: (0, j))
    w_spec = pl.BlockSpec((3, CONV_CHUNK), lambda j: (0, j))
    return z_spec(CU_OFF), z_spec(CB_OFF), z_spec(CC_OFF), chunk, w_spec


def _conv_fwd(z, conv_w):
    s = z.shape[0]
    cu_spec, cb_spec, cc_spec, chunk, w_spec = _conv_specs(s)

    def body(cu_ref, cb_ref, cc_ref, w_ref, o_ref):
        rows = lax.broadcasted_iota(jnp.int32, (s, 1), 0)
        t = cc_ref[...] * cu_ref[...]
        c3 = _shift_rows(t, rows, True) * w_ref[0:1, :] + t * w_ref[1:2, :] + _shift_rows(t, rows, False) * w_ref[2:3, :]
        o_ref[...] = (cb_ref[...] * c3).astype(BF16)

    return pl.pallas_call(
        body, name="conv_fwd", grid=(CONV_WIDTH // CONV_CHUNK,),
        in_specs=[cu_spec, cb_spec, cc_spec, w_spec],
        out_specs=chunk,
        out_shape=jax.ShapeDtypeStruct((s, CONV_WIDTH), BF16),
        compiler_params=_params(1),
    )(z, z, z, conv_w)


def _conv_bwd(z, conv_w, d_co, dz):
    s = z.shape[0]
    cu_spec, cb_spec, cc_spec, chunk, w_spec = _conv_specs(s)
    n_chunks = CONV_WIDTH // CONV_CHUNK
    offsets = (CU_OFF, CB_OFF, CC_OFF)

    def body(cu_ref, cb_ref, cc_ref, w_ref, d_ref, dz_in_ref, dz_ref, dw_ref, buf, sems):
        j = pl.program_id(0)

        def copies(j_at):
            return [pltpu.make_async_copy(buf.at[h], dz_ref.at[:, pl.ds(off + j_at * CONV_CHUNK, CONV_CHUNK)], sems.at[h])
                    for h, off in enumerate(offsets)]

        rows = lax.broadcasted_iota(jnp.int32, (s, 1), 0)
        cu, cc = cu_ref[...], cc_ref[...]
        t = cc * cu
        t_dn, t_up = _shift_rows(t, rows, True), _shift_rows(t, rows, False)
        c3 = t_dn * w_ref[0:1, :] + t * w_ref[1:2, :] + t_up * w_ref[2:3, :]
        d = d_ref[...]
        dc3 = d * cb_ref[...]
        dw_ref[0:1, :] = jnp.sum(dc3 * t_dn, axis=0, keepdims=True)
        dw_ref[1:2, :] = jnp.sum(dc3 * t, axis=0, keepdims=True)
        dw_ref[2:3, :] = jnp.sum(dc3 * t_up, axis=0, keepdims=True)
        dt = _shift_rows(dc3, rows, False) * w_ref[0:1, :] + dc3 * w_ref[1:2, :] + _shift_rows(dc3, rows, True) * w_ref[2:3, :]

        @pl.when(j > 0)
        def _():
            for cp in copies(j):
                cp.wait()

        buf[0] = (dt * cc).astype(BF16)
        buf[1] = (d * c3).astype(BF16)
        buf[2] = (dt * cu).astype(BF16)
        for cp in copies(j):
            cp.start()

        @pl.when(j == n_chunks - 1)
        def _():
            for cp in copies(j):
                cp.wait()

    return pl.pallas_call(
        body, name="conv_bwd", grid=(n_chunks,),
        in_specs=[cu_spec, cb_spec, cc_spec, w_spec, chunk, ANY],
        out_specs=[ANY, w_spec],
        out_shape=[jax.ShapeDtypeStruct(dz.shape, dz.dtype), jax.ShapeDtypeStruct((3, CONV_WIDTH), F32)],
        input_output_aliases={5: 0},
        scratch_shapes=[pltpu.VMEM((3, s, CONV_CHUNK), BF16), pltpu.SemaphoreType.DMA((3,))],
        compiler_params=_params(1),
    )(z, z, z, conv_w, d_co, dz)


GATE_CHUNK = 512


def _gate_specs(s, d, tr):
    n_chunks = d // GATE_CHUNK
    za = pl.BlockSpec((tr, GATE_CHUNK), lambda j, i: (i, GL_OFF // GATE_CHUNK + j))
    zc = pl.BlockSpec((tr, GATE_CHUNK), lambda j, i: (i, GL_OFF // GATE_CHUNK + n_chunks + j))
    ba = pl.BlockSpec((1, GATE_CHUNK), lambda j, i: (0, j))
    bc = pl.BlockSpec((1, GATE_CHUNK), lambda j, i: (0, n_chunks + j))
    tile = pl.BlockSpec((tr, GATE_CHUNK), lambda j, i: (i, j))
    return za, zc, ba, bc, tile


def _gate_fwd(z, b_gate, ya, yc):
    s, d = ya.shape
    tr = _row_tile(s, 512)
    za, zc, ba, bc, tile = _gate_specs(s, d, tr)

    def body(za_ref, zc_ref, ba_ref, bc_ref, ya_ref, yc_ref, o_ref):
        ga = jax.nn.sigmoid(za_ref[...] + ba_ref[...])
        gc = jax.nn.sigmoid(zc_ref[...] + bc_ref[...])
        o_ref[...] = (ga * ya_ref[...] + gc * yc_ref[...]).astype(BF16)

    return pl.pallas_call(
        body, name="gate_fwd", grid=(d // GATE_CHUNK, s // tr),
        in_specs=[za, zc, ba, bc, tile, tile],
        out_specs=tile,
        out_shape=jax.ShapeDtypeStruct((s, d), BF16),
        compiler_params=_params(2),
    )(z, z, b_gate, b_gate, ya, yc)


def _gate_bwd(z, b_gate, ya, yc, dmix):
    s, d = ya.shape
    tr = _row_tile(s, 512)
    za, zc, ba, bc, tile = _gate_specs(s, d, tr)
    vec = pl.BlockSpec((1, GATE_CHUNK), lambda j, i: (0, j))
    n_rows = s // tr
    in_width = z.shape[1]

    def body(za_ref, zc_ref, ba_ref, bc_ref, ya_ref, yc_ref, dm_ref, dya_ref, dyc_ref, dz_ref, dba_ref, dbc_ref, buf, sems):
        j, i = pl.program_id(0), pl.program_id(1)

        def copies(j_at, i_at):
            rows = pl.ds(i_at * tr, tr)
            return [pltpu.make_async_copy(buf.at[h], dz_ref.at[rows, pl.ds(GL_OFF + h * d + j_at * GATE_CHUNK, GATE_CHUNK)],
                                          sems.at[h]) for h in range(2)]

        ga = jax.nn.sigmoid(za_ref[...] + ba_ref[...])
        gc = jax.nn.sigmoid(zc_ref[...] + bc_ref[...])
        dm = dm_ref[...]
        dya_ref[...] = (dm * ga).astype(BF16)
        dyc_ref[...] = (dm * gc).astype(BF16)
        dla = dm * ya_ref[...] * ga * (1.0 - ga)
        dlc = dm * yc_ref[...] * gc * (1.0 - gc)

        @pl.when(j * n_rows + i > 0)
        def _():
            for cp in copies(j, i):
                cp.wait()

        buf[0] = dla.astype(BF16)
        buf[1] = dlc.astype(BF16)
        for cp in copies(j, i):
            cp.start()

        @pl.when((j == d // GATE_CHUNK - 1) & (i == n_rows - 1))
        def _():
            for cp in copies(j, i):
                cp.wait()

        pa = jnp.sum(dla, axis=0, keepdims=True)
        pc = jnp.sum(dlc, axis=0, keepdims=True)

        @pl.when(i == 0)
        def _():
            dba_ref[...] = pa
            dbc_ref[...] = pc

        @pl.when(i > 0)
        def _():
            dba_ref[...] += pa
            dbc_ref[...] += pc

    big = jax.ShapeDtypeStruct((s, d), BF16)
    small = jax.ShapeDtypeStruct((1, d), F32)
    return pl.pallas_call(
        body, name="gate_bwd", grid=(d // GATE_CHUNK, n_rows),
        in_specs=[za, zc, ba, bc, tile, tile, tile],
        out_specs=[tile, tile, ANY, vec, vec],
        out_shape=[big, big, jax.ShapeDtypeStruct((s, in_width), BF16), small, small],
        scratch_shapes=[pltpu.VMEM((2, tr, GATE_CHUNK), BF16), pltpu.SemaphoreType.DMA((2,))],
        compiler_params=_params(2),
    )(z, z, b_gate, b_gate, ya, yc, dmix)


def _cross_probs(q_ref, kv_ref, hd):
    cols = slice(hd * HEAD_DIM, (hd + 1) * HEAD_DIM)
    qh = q_ref[:, cols]
    kh = kv_ref[:, cols]
    sc = lax.dot_general(qh, kh, (((1,), (1,)), ((), ())), preferred_element_type=F32) * ATTN_SCALE
    e = jnp.exp(sc - jnp.max(sc, axis=1, keepdims=True))
    return qh, kh, e * (1.0 / jnp.sum(e, axis=1, keepdims=True))


def _cross_fwd(qc, kvc):
    s = qc.shape[0]
    n_mem = kvc.shape[0]
    tq = _row_tile(s, 256)

    def body(q_ref, kv_ref, o_ref):
        for hd in range(MEM_HEADS):
            _, _, p = _cross_probs(q_ref, kv_ref, hd)
            vh = kv_ref[:, MEM_WIDTH + hd * HEAD_DIM:MEM_WIDTH + (hd + 1) * HEAD_DIM]
            o_ref[:, hd * HEAD_DIM:(hd + 1) * HEAD_DIM] = jnp.dot(p.astype(BF16), vh, preferred_element_type=F32).astype(BF16)

    return pl.pallas_call(
        body, name="cross_fwd", grid=(s // tq,),
        in_specs=[pl.BlockSpec((tq, MEM_WIDTH), lambda i: (i, 0)), pl.BlockSpec((n_mem, 2 * MEM_WIDTH), lambda i: (0, 0))],
        out_specs=pl.BlockSpec((tq, MEM_WIDTH), lambda i: (i, 0)),
        out_shape=jax.ShapeDtypeStruct((s, MEM_WIDTH), BF16),
        compiler_params=_params(1),
    )(qc, kvc)


def _cross_bwd(qc, kvc, d_out):
    s = qc.shape[0]
    n_mem = kvc.shape[0]
    tq = _row_tile(s, 256)

    def body(q_ref, kv_ref, do_ref, dq_ref, dkv_ref):
        @pl.when(pl.program_id(0) == 0)
        def _():
            dkv_ref[...] = jnp.zeros_like(dkv_ref)

        for hd in range(MEM_HEADS):
            cols = slice(hd * HEAD_DIM, (hd + 1) * HEAD_DIM)
            vcols = slice(MEM_WIDTH + hd * HEAD_DIM, MEM_WIDTH + (hd + 1) * HEAD_DIM)
            qh, kh, p = _cross_probs(q_ref, kv_ref, hd)
            doh = do_ref[:, cols]
            dp = lax.dot_general(doh, kv_ref[:, vcols], (((1,), (1,)), ((), ())), preferred_element_type=F32)
            ds = (p * (dp - jnp.sum(p * dp, axis=1, keepdims=True)) * ATTN_SCALE).astype(BF16)
            dq_ref[:, cols] = jnp.dot(ds, kh, preferred_element_type=F32).astype(BF16)
            dkv_ref[:, cols] += lax.dot_general(ds, qh, (((0,), (0,)), ((), ())), preferred_element_type=F32)
            dkv_ref[:, vcols] += lax.dot_general(p.astype(BF16), doh, (((0,), (0,)), ((), ())), preferred_element_type=F32)

    qspec = pl.BlockSpec((tq, MEM_WIDTH), lambda i: (i, 0))
    kvspec = pl.BlockSpec((n_mem, 2 * MEM_WIDTH), lambda i: (0, 0))
    return pl.pallas_call(
        body, name="cross_bwd", grid=(s // tq,),
        in_specs=[qspec, kvspec, qspec],
        out_specs=[qspec, kvspec],
        out_shape=[jax.ShapeDtypeStruct((s, MEM_WIDTH), BF16), jax.ShapeDtypeStruct((n_mem, 2 * MEM_WIDTH), F32)],
        compiler_params=_params(1),
    )(qc, kvc, d_out)


def _swiglu_fwd(up, gate):
    sg = jax.nn.sigmoid(gate)
    silu = gate * sg
    return silu * up, up * (sg * (1.0 + gate * (1.0 - sg))), silu


def _swiglu_bwd(d_act, dact_dgate, dact_dup):
    return d_act * dact_dgate.astype(F32), d_act * dact_dup.astype(F32)


GATHER_GROUPS = {"in": ("w_in", "conv_w"), "mid": ("w_attn_out", "w_conv_out", "w_o", "w_cq", "w_ckv", "w_co"),
                 "gate": ("w_gate",), "up": ("w_up",), "down": ("w_down",)}


def _local_step(xs, mems, target, small, fetch, reduce):
    s, d = xs.shape
    w4 = {}
    cos_t, sin_t = _rope_tables(s)

    def near(group, done, then, after):
        waits = [("direct", group)] + ([("pass_near", done), ("pass_far", done)] if done else [])
        starts = [("forward", group), ("pass_near", group)] + [("direct", g) for g in then]
        tok = fetch.step("gather_near_" + group, waits, starts, after)
        if done:
            w4.update(fetch.arrays(done))
        return tok

    def far(group, then, after):
        return fetch.step("gather_far_" + group, [("forward", group)], [("pass_far", group)] + [("direct", g) for g in then], after)

    def last(group, after):
        tok = fetch.step("gather_done_" + group, [("pass_near", group), ("pass_far", group)], [], after)
        w4.update(fetch.arrays(group))
        return tok

    h = _rmsnorm(xs, small["g_mix"], "norm_mix")
    slots_filled = [a for g in ("gate", "up", "down") for a in fetch.arrays(g).values()]
    chip_x, chip_y = reduce.place[0] // 2, reduce.place[0] % 2
    own_block = jnp.stack([2 * chip_x + chip_y]).astype(jnp.int32)
    near_blocks = jnp.stack([2 * (1 - chip_x) + chip_y, 2 * chip_x + (1 - chip_y)]).astype(jnp.int32)
    far_block = jnp.stack([2 * (1 - chip_x) + (1 - chip_y)]).astype(jnp.int32)
    z = _matmul_column_blocks(h, fetch.arrays("in")["w_in"], own_block, None, tm=512, name="in_proj_own")
    tok = near("in", None, ["mid"], [z] + slots_filled)
    tok = fetch.step("gather_near_done_in", [("pass_near", "in")], [], tok)
    z = _matmul_column_blocks(h, fetch.arrays("in")["w_in"], near_blocks, z, tm=1024, name="in_proj_near", after=tok)
    tok = far("in", [], z)
    tok = fetch.step("gather_done_in", [("pass_far", "in")], [], tok)
    w4.update(fetch.arrays("in"))
    z = _matmul_column_blocks(h, w4["w_in"], far_block, z, tm=1024, name="in_proj_far", after=tok)
    conv4 = w4["conv_w"]
    conv_w = conv4[:, :3, :].transpose(1, 0, 2).reshape(3, N_CHIPS * conv4.shape[2])
    c_in = w4["w_in"].shape[2]
    tok = near("mid", None, ["gate"], z)
    q_rot, k_rot, v_b = _rope_fwd(z, cos_t, sin_t)
    attn = _swa_fwd(q_rot, k_rot, v_b, small["sink"])
    co = _conv_fwd(z, conv_w)
    tok = far("mid", ["up"], attn)
    tok = last("mid", tok)
    w_o = w4["w_o"].reshape(-1, w4["w_o"].shape[-1])
    c_d = w4["w_attn_out"].shape[2]
    ya = _matmul(attn, w4["w_attn_out"], mode="nn", tm=2048, tn=c_d, out_dtypes=[F32], name="attn_out_proj",
                 b_blocks=N_CHIPS, after=tok)
    yc = _matmul(co, w4["w_conv_out"], mode="nn", tm=2048, tn=c_d, out_dtypes=[F32], name="conv_out_proj",
                 b_blocks=N_CHIPS)
    mix = _gate_fwd(z, small["b_gate"], ya, yc)
    x1 = _matmul(mix, w_o, mode="nn", tm=1024, tn=1024, out_dtypes=[F32], name="mix_out_proj", extras=[xs],
                 epilogue=_add_residual)
    tok = near("gate", None, ["down"], x1)
    w_cq = w4["w_cq"].reshape(-1, w4["w_cq"].shape[-1])
    w_ckv = w4["w_ckv"].reshape(-1, w4["w_ckv"].shape[-1])
    hc = _rmsnorm(x1, small["g_cross"], "norm_cross")
    memn = _rmsnorm(mems, small["g_mem"], "norm_mem")
    qc = _matmul(hc, w_cq, mode="nn", tm=2048, tn=MEM_WIDTH, out_dtypes=[BF16], name="cross_q_proj", after=tok)
    kvc = _matmul(memn, w_ckv, mode="nn", tm=256, tn=2 * MEM_WIDTH, out_dtypes=[BF16], name="cross_kv_proj")
    oc = _cross_fwd(qc, kvc)
    tok = far("gate", [], oc)
    x2 = _matmul(oc, w4["w_co"], mode="nn", tm=2048, tn=c_d, out_dtypes=[F32], name="cross_out_proj",
                 extras=[x1], epilogue=_add_residual, b_blocks=N_CHIPS, after=tok)
    hf = _rmsnorm(x2, small["g_ffn"], "norm_ffn")
    tok = near("up", "gate", [], hf)
    c_ff = w4["w_gate"].shape[2]
    gate = _matmul(hf, w4["w_gate"], mode="nn", tm=1024, tn=c_ff, out_dtypes=[F32], name="ffn_gate_proj", b_blocks=N_CHIPS,
                   after=tok)
    tok = far("up", [], gate)
    tok = near("down", "up", [], tok)
    act, dact_dgate, dact_dup = _matmul(hf, w4["w_up"], mode="nn", tm=1024, tn=c_ff, out_dtypes=[BF16, BF16, BF16],
                                        name="ffn_up_proj", extras=[gate], epilogue=_swiglu_fwd, b_blocks=N_CHIPS, after=tok)
    tok = far("down", [], act)
    last("down", tok)
    w_down = w4["w_down"].reshape(-1, w4["w_down"].shape[-1])
    x3 = _matmul(act, w_down, mode="nn", tm=512, tn=512, out_dtypes=[F32], name="ffn_down_proj", extras=[x2],
                 epilogue=_add_residual)
    dx3, dx3b, sq, dg_final = _loss_head(x3, small["g_final"], target)

    da, du = _matmul(dx3b, w_down, mode="nt", tm=1024, tn=c_ff, out_dtypes=[BF16, BF16], name="ffn_down_bwd",
                     extras=[dact_dgate, dact_dup], epilogue=_swiglu_bwd)
    core = reduce.core
    ffn_shape = dict(row_sharded=False, tm=1024, tn=c_ff)
    g_down = _matmul(act, dx3b, mode="tn", tm=c_ff, tn=1024, out_dtypes=[BF16], name="ffn_down_wgrad")
    tok = reduce.add("down", {"w_down": g_down}, da)
    t_gate = _wgrad_half(hf, da, core, theirs=True, name="ffn_gate_wgrad_theirs", after=tok, **ffn_shape)
    tok = reduce.step("down", t_gate)
    t_up = _wgrad_half(hf, du, core, theirs=True, name="ffn_up_wgrad_theirs", after=tok, **ffn_shape)
    tok = reduce.send("ffn", {"w_gate": t_gate, "w_up": t_up}, dx3b)
    dhf = _matmul(da, w4["w_gate"], mode="nt", tm=512, tn=1024, out_dtypes=[F32], name="ffn_gate_bwd", b_blocks=N_CHIPS,
                  after=tok)
    got = reduce.received("ffn", dhf)
    p_gate = _wgrad_half(hf, da, core, theirs=False, name="ffn_gate_wgrad_mine", add=got["w_gate"], **ffn_shape)
    p_up = _wgrad_half(hf, du, core, theirs=False, name="ffn_up_wgrad_mine", add=got["w_up"], **ffn_shape)
    tok = reduce.add_parts("ffn", {"w_gate": p_gate, "w_up": p_up})
    dhf = _matmul(du, w4["w_up"], mode="nt", tm=512, tn=1024, out_dtypes=[F32], name="ffn_up_bwd", extras=[dhf],
                  epilogue=_add_residual, b_blocks=N_CHIPS, after=tok)
    tok = reduce.step("down", dhf)
    dx2, dx2b, dg_ffn = _rmsnorm_bwd(dhf, x2, small["g_ffn"], dx3, "norm_ffn_bwd")

    d_oc = _matmul(dx2b, w4["w_co"], mode="nt", tm=1024, tn=MEM_WIDTH, out_dtypes=[BF16], name="cross_out_bwd",
                   b_blocks=N_CHIPS, after=tok)
    g_co = _matmul(oc, dx2b, mode="tn", tm=MEM_WIDTH, tn=c_d, out_dtypes=[BF16], name="cross_out_wgrad", out_blocks=N_CHIPS)
    tok = reduce.step("down", g_co)
    dqc, dkvc = _cross_bwd(qc, kvc, d_oc)
    g_cq = _matmul(hc, dqc, mode="tn", tm=1024, tn=MEM_WIDTH, out_dtypes=[BF16], name="cross_q_wgrad", after=tok)
    dhc = _matmul(dqc, w_cq, mode="nt", tm=1024, tn=1024, out_dtypes=[F32], name="cross_q_bwd")
    g_ckv = _matmul(memn, dkvc, mode="tn", tm=1024, tn=2 * MEM_WIDTH, out_dtypes=[BF16], name="cross_kv_wgrad")
    dmemn = _matmul(dkvc, w_ckv, mode="nt", tm=256, tn=1024, out_dtypes=[F32], name="cross_kv_bwd")
    _, _, dg_mem = _rmsnorm_bwd(dmemn, mems, small["g_mem"], None, "norm_mem_bwd")
    dx1, dx1b, dg_cross = _rmsnorm_bwd(dhc, x1, small["g_cross"], dx2, "norm_cross_bwd")

    dmix = _matmul(dx1b, w_o, mode="nt", tm=1024, tn=1024, out_dtypes=[F32], name="mix_out_bwd")
    g_o = _matmul(mix, dx1b, mode="tn", tm=1024, tn=1024, out_dtypes=[BF16], name="mix_out_wgrad")
    dya, dyc, dz, db_a, db_c = _gate_bwd(z, small["b_gate"], ya, yc, dmix)
    d_attn = _matmul(dya, w4["w_attn_out"], mode="nt", tm=1024, tn=ATTN_WIDTH, out_dtypes=[BF16], name="attn_out_bwd",
                     b_blocks=N_CHIPS)
    g_ao = _matmul(attn, dya, mode="tn", tm=ATTN_WIDTH, tn=c_d, out_dtypes=[BF16], name="attn_out_wgrad", out_blocks=N_CHIPS)
    d_co = _matmul(dyc, w4["w_conv_out"], mode="nt", tm=1024, tn=CONV_WIDTH, out_dtypes=[F32], name="conv_out_bwd",
                   b_blocks=N_CHIPS)
    g_cvo = _matmul(co, dyc, mode="tn", tm=CONV_WIDTH, tn=c_d, out_dtypes=[BF16], name="conv_out_wgrad", out_blocks=N_CHIPS)
    tok = reduce.step("ffn", g_cvo)
    tok = reduce.add("mid", {"w_co": g_co, "w_cq": g_cq, "w_ckv": g_ckv, "w_o": g_o, "w_attn_out": g_ao, "w_conv_out": g_cvo}, tok)
    dz, d_conv_w = _conv_bwd(z, conv_w, d_co, dz)
    dq_rot, dk_rot, dv, dsink = _swa_bwd(q_rot, k_rot, v_b, d_attn, small["sink"])
    tok = reduce.step("mid", dq_rot)
    dz = _rope_bwd(dq_rot, dk_rot, dv, cos_t, sin_t, dz)
    in_shape = dict(row_sharded=False, tm=1024, tn=c_in)
    t_in = _wgrad_half(h, dz, core, theirs=True, name="in_proj_wgrad_theirs", after=tok, **in_shape)
    tok = reduce.send("in", {"w_in": t_in}, dk_rot)
    tok = reduce.step("ffn", tok, count=1)
    got = reduce.received("in", tok)
    p_in = _wgrad_half(h, dz, core, theirs=False, name="in_proj_wgrad_mine", add=got["w_in"], **in_shape)
    tok = reduce.add_parts("in", {"w_in": p_in})
    tok = reduce.step("ffn", tok)
    tok = reduce.step("mid", tok)
    dh = _matmul(dz, w4["w_in"], mode="nt", tm=512, tn=512, out_dtypes=[F32], name="in_proj_bwd", b_blocks=N_CHIPS,
                 after=tok)
    grad_x, _, dg_mix = _rmsnorm_bwd(dh, xs, small["g_mix"], dx1, "norm_mix_bwd")

    small_grads = {
        "g_mix": dg_mix, "sink": dsink[:, 0], "b_gate": jnp.concatenate([db_a, db_c], axis=1), "g_cross": dg_cross,
        "g_mem": dg_mem, "g_ffn": dg_ffn, "g_final": dg_final, "conv_w": d_conv_w,
    }
    return sq, grad_x, small_grads


def _pair_sum(g4, ra, core, name):
    nb, rs, cs = g4.shape
    rh = rs // 2
    tr = _row_tile(rh, 256)
    per = rh // tr

    def body(c_ref, g_ref, r_ref, o_ref):
        o_ref[...] = (g_ref[...].astype(F32) + r_ref[...].astype(F32)).astype(BF16)

    plain = pl.BlockSpec((None, tr, cs), lambda j, i, c: (j, i, 0))
    return pl.pallas_call(
        body, name=name,
        grid_spec=pltpu.PrefetchScalarGridSpec(
            num_scalar_prefetch=1, grid=(nb, per),
            in_specs=[pl.BlockSpec((None, tr, cs), lambda j, i, c: (j, c[0] * per + i, 0)), plain],
            out_specs=plain),
        out_shape=jax.ShapeDtypeStruct((nb, rh, cs), BF16),
        compiler_params=_params(2),
    )(core, g4, ra)


def _adamw_update(w, g, m, v):
    nm = ADAM_B1 * m + (1.0 - ADAM_B1) * g
    nv = ADAM_B2 * v + (1.0 - ADAM_B2) * (g * g)
    m_hat = nm / ADAM_C1
    v_hat = nv / ADAM_C2
    return -ADAM_LR * (m_hat / (jnp.sqrt(v_hat) + ADAM_EPS) + ADAM_WD * w), nm, nv


def _adamw_own_half(w, m, v, parts, rc, place, name, after=None):
    rows, cols = w.shape
    rh = rows // 2
    tr = _row_tile(rh, 256)
    per = rh // tr

    def body(p_ref, w_ref, m_ref, v_ref, own_ref, r_ref, *rest):
        gx_ref, g_ref, d_ref, nm_ref, nv_ref = rest[-5:]
        g = own_ref[...].astype(F32)
        for j in range(rc.shape[0]):
            g = g + r_ref[j].astype(F32)
        gx_ref[...] = g
        g_ref[...] = g
        d_ref[...], nm_ref[...], nv_ref[...] = _adamw_update(w_ref[...], g, m_ref[...], v_ref[...])

    mine = pl.BlockSpec((tr, cols), lambda i, p: (p[1] * per + i, 0))
    shape = jax.ShapeDtypeStruct((rows, cols), F32)
    return pl.pallas_call(
        body, name=name,
        grid_spec=pltpu.PrefetchScalarGridSpec(
            num_scalar_prefetch=1, grid=(per,),
            in_specs=[mine, mine, mine, pl.BlockSpec((None, tr, cols), lambda i, p: (p[0], i, 0)),
                      pl.BlockSpec((rc.shape[0], tr, cols), lambda i, p: (0, i, 0))] + ([] if after is None else [ANY]),
            out_specs=[mine] * 5),
        out_shape=[shape] * 5,
        compiler_params=_params(1),
    )(place, w, m, v, parts, rc, *([] if after is None else [after]))


def _adamw_other_half(w, m, v, g_exchanged, g, delta, new_m, new_v, place, name, after=None):
    rows, cols = w.shape
    rh = rows // 2
    tr = _row_tile(rh, 256)
    per = rh // tr

    def body(p_ref, w_ref, m_ref, v_ref, gx_ref, *rest):
        g_ref, d_ref, nm_ref, nv_ref = rest[-4:]
        gv = gx_ref[...]
        g_ref[...] = gv
        d_ref[...], nm_ref[...], nv_ref[...] = _adamw_update(w_ref[...], gv, m_ref[...], v_ref[...])

    other = pl.BlockSpec((tr, cols), lambda i, p: ((1 - p[1]) * per + i, 0))
    shape = jax.ShapeDtypeStruct((rows, cols), F32)
    n_after = 0 if after is None else 1
    return pl.pallas_call(
        body, name=name,
        grid_spec=pltpu.PrefetchScalarGridSpec(
            num_scalar_prefetch=1, grid=(per,),
            in_specs=[other] * 4 + [ANY] * (4 + n_after),
            out_specs=[other] * 4),
        out_shape=[shape] * 4,
        input_output_aliases={5: 0, 6: 1, 7: 2, 8: 3},
        compiler_params=_params(1),
    )(place, w, m, v, g_exchanged, g, delta, new_m, new_v, *([] if after is None else [after]))


def _cast_to_slot(w, place, dtype, name, after=None):
    rows, cols = w.shape
    tr = _row_tile(rows, 1024)

    def body(p_ref, w_ref, *rest):
        o_ref = rest[-1]
        o_ref[...] = w_ref[...].astype(dtype)

    return pl.pallas_call(
        body, name=name,
        grid_spec=pltpu.PrefetchScalarGridSpec(
            num_scalar_prefetch=1, grid=(rows // tr,),
            in_specs=[pl.BlockSpec((tr, cols), lambda i, p: (i, 0))] + ([] if after is None else [ANY]),
            out_specs=pl.BlockSpec((None, tr, cols), lambda i, p: (p[0], i, 0))),
        out_shape=jax.ShapeDtypeStruct((N_CHIPS, rows, cols), dtype),
        compiler_params=_params(1),
    )(place, w, *([] if after is None else [after]))


def _adamw(w, g, m, v, name, after=None):
    rows, cols = w.shape
    tr = _row_tile(rows, 256)

    def body(w_ref, g_ref, m_ref, v_ref, *rest):
        go_ref, d_ref, nm_ref, nv_ref = rest[-4:]
        gv = g_ref[...]
        go_ref[...] = gv
        d_ref[...], nm_ref[...], nv_ref[...] = _adamw_update(w_ref[...], gv, m_ref[...], v_ref[...])

    tile = pl.BlockSpec((tr, cols), lambda i: (i, 0))
    shape = jax.ShapeDtypeStruct((rows, cols), F32)
    return pl.pallas_call(
        body, name=name, grid=(rows // tr,),
        in_specs=[tile] * 4 + ([] if after is None else [ANY]), out_specs=[tile] * 4, out_shape=[shape] * 4,
        compiler_params=_params(1),
    )(w, g, m, v, *([] if after is None else [after]))


def _mesh_pos():
    return lax.axis_index("x"), lax.axis_index("y"), lax.axis_index("c")


def _other_chips(x, y):
    return [(1 - x, y), (x, 1 - y), (1 - x, 1 - y)]


def _half_rows(ref, which):
    rh = ref.shape[-2] // 2
    return ref.at[pl.ds(which * rh, rh), :]


def _remote(src, dst, send_sems, recv_sems, sem, to):
    return pltpu.make_async_remote_copy(src_ref=src, dst_ref=dst, send_sem=send_sems.at[sem], recv_sem=recv_sems.at[sem],
                                        device_id=to, device_id_type=MESH)


HBM = pl.BlockSpec(memory_space=pltpu.HBM)
SEM = pl.BlockSpec(memory_space=pltpu.SEMAPHORE)
DATAFLOW_EFFECT = pltpu.SideEffectType.DATAFLOW_SIDE_EFFECTING


def _in_hbm(arrays):
    return [pltpu.with_memory_space_constraint(a, pltpu.HBM) for a in arrays]


def _hbm_like(arrays):
    return [pltpu.HBM(a.shape, a.dtype) for a in arrays]


GATHER_COPIES_PER_ARRAY = {"direct": 2, "forward": 2, "pass_near": 2, "pass_far": 1}


def _gather_copies(kind, refs, x, y, c):
    me, near_x, near_y, far = 2 * x + y, 2 * (1 - x) + y, 2 * x + (1 - y), 2 * (1 - x) + (1 - y)
    to_x, to_y, sibling = (1 - x, y, c), (x, 1 - y, c), (x, y, 1 - c)
    out = []
    for ref in refs:
        rh = ref.shape[1] // 2
        rq = rh // 2

        def half(chip, ref=ref, rh=rh):
            return ref.at[chip, pl.ds(c * rh, rh), :]

        def quarter(chip, q, ref=ref, rh=rh, rq=rq):
            return ref.at[chip, pl.ds(c * rh + q * rq, rq), :]

        if kind == "direct":
            out += [(half(me), half(me), to_x), (half(me), half(me), to_y)]
        elif kind == "forward":
            out += [(quarter(near_x, 0), quarter(near_x, 0), to_y), (quarter(near_y, 1), quarter(near_y, 1), to_x)]
        elif kind == "pass_near":
            out += [(half(near_x), half(near_x), sibling), (half(near_y), half(near_y), sibling)]
        else:
            assert kind == "pass_far"
            out += [(half(far), half(far), sibling)]
    return out


def _gather_step(name, bufs, waits, starts, after):
    nb, nw, ns = len(bufs), len(waits), len(starts)
    after = [] if after is None else list(after) if isinstance(after, (list, tuple)) else [after]
    n_after = len(after)

    def body(*refs):
        ins = refs[:nb]
        wait_sems = refs[nb:nb + 2 * nw]
        start_sems = refs[nb + 2 * nw + n_after:nb + 2 * nw + n_after + 2 * ns]
        token = refs[-1]
        x, y, c = _mesh_pos()
        for j, (kind, idxs, _, _) in enumerate(waits):
            for i, (s_ref, d_ref, to) in enumerate(_gather_copies(kind, [ins[t] for t in idxs], x, y, c)):
                came = _remote(s_ref, d_ref, wait_sems[2 * j], wait_sems[2 * j + 1], i, to)
                came.wait_recv()
                came.wait_send()
        for j, (kind, idxs) in enumerate(starts):
            for i, (s_ref, d_ref, to) in enumerate(_gather_copies(kind, [ins[t] for t in idxs], x, y, c)):
                _remote(s_ref, d_ref, start_sems[2 * j], start_sems[2 * j + 1], i, to).start()
        token[...] = jnp.zeros_like(token)

    sems = []
    for kind, idxs in starts:
        sems += [pltpu.SemaphoreType.DMA((GATHER_COPIES_PER_ARRAY[kind] * len(idxs),))] * 2
    operands = _in_hbm(bufs) + [sem for w in waits for sem in w[2:]] + after
    outs = pl.pallas_call(
        body, name=name,
        in_specs=[HBM] * nb + [SEM] * (2 * nw) + [ANY] * n_after,
        out_specs=[SEM] * (2 * ns) + [HBM] * nb + [pl.BlockSpec(memory_space=pltpu.VMEM)],
        out_shape=sems + _hbm_like(bufs) + [jax.ShapeDtypeStruct((8, 128), F32)],
        input_output_aliases={i: 2 * ns + i for i in range(nb)},
        compiler_params=pltpu.CompilerParams(has_side_effects=DATAFLOW_EFFECT),
    )(*operands)
    return outs[2 * ns:2 * ns + nb], [(outs[2 * j], outs[2 * j + 1]) for j in range(ns)], outs[-1]


class _Gather:
    def __init__(self, groups):
        self.groups = groups
        self.bufs = {}
        self.in_flight = {}

    def put(self, slotted):
        self.bufs.update(slotted)

    def step(self, name, waits, starts, after=None):
        names = []
        for _, group in list(waits) + list(starts):
            names += [n for n in self.groups[group] if n not in names]
        index = {n: i for i, n in enumerate(names)}

        def members(group):
            return [index[n] for n in self.groups[group]]

        wait_args = [(kind, members(group)) + self.in_flight.pop((kind, group)) for kind, group in waits]
        start_args = [(kind, members(group)) for kind, group in starts]
        bufs, sems, token = _gather_step(name, [self.bufs[n] for n in names], wait_args, start_args, after)
        self.bufs.update(zip(names, bufs))
        for (kind, group), pair in zip(starts, sems):
            self.in_flight[(kind, group)] = pair
        return token

    def arrays(self, group):
        return {n: self.bufs[n] for n in self.groups[group]}


def _sibling_halves_copies(srcs, dsts, x, y, c):
    out = []
    for s_ref, d_ref in zip(srcs, dsts, strict=True):
        rh = s_ref.shape[1] // 2
        out.append((s_ref.at[:, pl.ds((1 - c) * rh, rh), :], d_ref, (x, y, 1 - c)))
    return out


def _to_sibling_copies(srcs, dsts, x, y, c):
    return [(s_ref, d_ref, (x, y, 1 - c)) for s_ref, d_ref in zip(srcs, dsts, strict=True)]


def _chip_copies(srcs, dsts, x, y, c):
    out = []
    for s_ref, d_ref in zip(srcs, dsts, strict=True):
        for k, (px, py) in enumerate(_other_chips(x, y)):
            out.append((s_ref.at[2 * px + py], d_ref.at[k], (px, py, c)))
    return out


def _join_copies(srcs, dsts, x, y, c):
    out = []
    for s_ref in srcs:
        mine = _half_rows(s_ref, c)
        out.append((mine, mine, (x, y, 1 - c)))
    return out


def _exchange_start(copies_fn, n_copies, srcs, fresh, after, name):
    ns, nb = len(srcs), len(srcs) + len(fresh)

    def body(*refs):
        bufs, send, recv, token = refs[:nb], refs[nb + 1], refs[nb + 2], refs[-1]
        x, y, c = _mesh_pos()
        for i, (s_ref, d_ref, to) in enumerate(copies_fn(bufs[:ns], bufs[ns:] if fresh else bufs[:ns], x, y, c)):
            _remote(s_ref, d_ref, send, recv, i, to).start()
        token[...] = jnp.zeros_like(token)

    sems = [pltpu.SemaphoreType.DMA((n_copies,))] * 2
    outs = pl.pallas_call(
        body, name=name,
        in_specs=[HBM] * nb + [ANY], out_specs=[SEM, SEM] + [HBM] * nb + [pl.BlockSpec(memory_space=pltpu.VMEM)],
        out_shape=sems + _hbm_like(list(srcs) + list(fresh)) + [jax.ShapeDtypeStruct((8, 128), F32)],
        input_output_aliases={i: 2 + i for i in range(nb)},
        compiler_params=pltpu.CompilerParams(has_side_effects=DATAFLOW_EFFECT),
    )(*_in_hbm(list(srcs) + list(fresh)), after)
    return outs[0], outs[1], outs[2:2 + ns], outs[2 + ns:2 + nb], outs[-1]


def _exchange_done(copies_fn, srcs, fresh, send, recv, after, name):
    ns, nb = len(srcs), len(srcs) + len(fresh)

    def body(*refs):
        bufs, send_in, recv_in = refs[:nb], refs[nb], refs[nb + 1]
        x, y, c = _mesh_pos()
        for i, (s_ref, d_ref, to) in enumerate(copies_fn(bufs[:ns], bufs[ns:] if fresh else bufs[:ns], x, y, c)):
            came = _remote(s_ref, d_ref, send_in, recv_in, i, to)
            came.wait_send()
            came.wait_recv()

    outs = pl.pallas_call(
        body, name=name,
        in_specs=[HBM] * nb + [SEM, SEM, ANY], out_specs=[HBM] * nb,
        out_shape=_hbm_like(list(srcs) + list(fresh)),
        input_output_aliases={i: i for i in range(nb)},
        compiler_params=pltpu.CompilerParams(has_side_effects=DATAFLOW_EFFECT),
    )(*_in_hbm(list(srcs) + list(fresh)), send, recv, after)
    return outs[:ns], outs[ns:]


class _Reduce:
    def __init__(self, place, core, shards, mom_m, mom_v):
        self.place, self.core = place, core
        self.shards, self.mom_m, self.mom_v = shards, mom_m, mom_v
        self.state = {}
        self.results = {}

    def add(self, group, grads, after):
        names = list(grads)
        g4s = [g.reshape((N_CHIPS, -1, g.shape[-1])) if g.ndim == 2 else g for g in grads.values()]
        fresh = [lax.empty((N_CHIPS, g.shape[1] // 2, g.shape[2]), BF16) for g in g4s]
        send, recv, g4s, fresh, token = _exchange_start(_sibling_halves_copies, len(names), g4s, fresh, after,
                                                        "pair_start_" + group)
        self.state[group] = (0, names, send, recv, g4s, fresh)
        return token

    def send(self, group, theirs, after):
        names, srcs = list(theirs), list(theirs.values())
        fresh = [lax.empty(s.shape, BF16) for s in srcs]
        send, recv, srcs, fresh, token = _exchange_start(_to_sibling_copies, len(names), srcs, fresh, after, "pair_start_" + group)
        self.state[group] = ("sent", names, send, recv, srcs, fresh)
        return token

    def received(self, group, after):
        stage, names, send, recv, srcs, fresh = self.state.pop(group)
        assert stage == "sent"
        _, got = _exchange_done(_to_sibling_copies, srcs, fresh, send, recv, after, "pair_done_" + group)
        return dict(zip(names, got))

    def add_parts(self, group, parts):
        names, srcs = list(parts), list(parts.values())
        fresh = [lax.empty((N_CHIPS - 1,) + p.shape[1:], BF16) for p in srcs]
        send, recv, srcs, fresh, token = _exchange_start(_chip_copies, 3 * len(names), srcs, fresh, self.core, "chips_start_" + group)
        self.state[group] = (1, names, send, recv, srcs, fresh)
        return token

    def step(self, group, after, count=None):
        stage, names, send, recv, srcs, fresh = self.state[group]
        if stage == 0:
            g4s, ras = _exchange_done(_sibling_halves_copies, srcs, fresh, send, recv, after, "pair_done_" + group)
            parts = [_pair_sum(g, r, self.core, "pair_sum_" + n) for g, r, n in zip(g4s, ras, names)]
            fresh = [lax.empty((N_CHIPS - 1,) + p.shape[1:], BF16) for p in parts]
            send, recv, parts, fresh, token = _exchange_start(_chip_copies, 3 * len(names), parts, fresh, self.core,
                                                              "chips_start_" + group)
            self.state[group] = (1, names, send, recv, parts, fresh)
            return token
        if stage == 1:
            parts, rcs = _exchange_done(_chip_copies, srcs, fresh, send, recv, after, "chips_done_" + group)
            token = None
            for n, p, r in zip(names, parts, rcs):
                self.results[n] = _adamw_own_half(self.shards[n], self.mom_m[n], self.mom_v[n], p, r, self.place,
                                                  "adamw_own_" + n, after=token)
                token = self.results[n][2]
            wholes = [self.results[n][0] for n in names]
            send, recv, wholes, _, token = _exchange_start(_join_copies, len(names), wholes, [], token, "join_start_" + group)
            self.state[group] = (2, names, send, recv, wholes, [])
            return token
        assert stage in (2, 3)
        if stage == 2:
            srcs, _ = _exchange_done(_join_copies, srcs, [], send, recv, after, "join_done_" + group)
            after = None
        token = after
        count = len(names) if count is None else count
        for n, exchanged in zip(names[:count], srcs):
            _, g, d, nm, nv = self.results[n]
            self.results[n] = _adamw_other_half(self.shards[n], self.mom_m[n], self.mom_v[n], exchanged, g, d, nm, nv,
                                                self.place, "adamw_other_" + n, after=token)
            token = self.results[n][1]
        if count < len(names):
            self.state[group] = (3, names[count:], None, None, srcs[count:], [])
        else:
            del self.state[group]
        return token


N_DEV = 8


def _to_all_copies(srcs, dsts, x, y, c):
    out = []
    for r in range(1, N_DEV):
        fx, fy, fc = (r >> 2) & 1, (r >> 1) & 1, r & 1
        out.append((srcs[0], dsts[0].at[r - 1], (x + fx - 2 * x * fx, y + fy - 2 * y * fy, c + fc - 2 * c * fc)))
    return out


def _all_reduce_small_start(v, after):
    slots = lax.empty((N_DEV - 1,) + v.shape, v.dtype)
    send, recv, (v,), (slots,), token = _exchange_start(_to_all_copies, N_DEV - 1, [v], [slots], after, "small_grads_start")
    return (send, recv, v, slots), token


def _all_reduce_small_done(started, after):
    send, recv, v, slots = started
    (v,), (slots,) = _exchange_done(_to_all_copies, [v], [slots], send, recv, after, "small_grads_done")

    def body(v_ref, slots_ref, o_ref):
        x, y, c = _mesh_pos()
        me = 4 * x + 2 * y + c
        acc = None
        for i in range(N_DEV):
            r = jnp.bitwise_xor(me, i)
            part = jnp.where(r == 0, v_ref[...], slots_ref[jnp.maximum(r - 1, 0)])
            acc = part if acc is None else acc + part
        o_ref[...] = acc

    vm = pl.BlockSpec(memory_space=pltpu.VMEM)
    return pl.pallas_call(body, name="small_grads_sum", in_specs=[vm, vm], out_specs=vm,
                          out_shape=jax.ShapeDtypeStruct(v.shape, v.dtype))(v, slots)


MATRICES = ("w_in", "w_attn_out", "w_conv_out", "w_o", "w_cq", "w_ckv", "w_co", "w_gate", "w_up", "w_down")
VECTORS = ("g_mix", "b_gate", "g_cross", "g_mem", "g_ffn", "g_final", "conv_w", "sink")
WEIGHT_ORDER = ("g_mix", "w_in", "sink", "conv_w", "b_gate", "w_attn_out", "w_conv_out", "w_o", "g_cross", "g_mem", "w_cq",
                "w_ckv", "w_co", "g_ffn", "w_gate", "w_up", "w_down", "g_final")
CONV_PAD_ROWS = 32
SMALL_ROWS = 8


def _pack(pieces):
    flat = jnp.concatenate([p.reshape(-1) for p in pieces])
    lane_group = SMALL_ROWS * 128
    total = -(-flat.shape[0] // lane_group) * lane_group
    flat = jnp.pad(flat, (0, total - flat.shape[0]))
    return flat.reshape(SMALL_ROWS, total // SMALL_ROWS), [p.size for p in pieces]


def _unpack(packed, pieces):
    flat = packed.reshape(-1)
    out, off = [], 0
    for p in pieces:
        out.append(flat[off:off + p.size].reshape(p.shape))
        off += p.size
    return out


def kernel(x, mem, g_mix, w_in, sink, conv_w, b_gate, w_attn_out, w_conv_out, w_o, g_cross, g_mem, w_cq, w_ckv, w_co, g_ffn, w_gate, w_up, w_down, g_final, loss_target, m_g_mix, m_w_in, m_sink, m_conv_w, m_b_gate, m_w_attn_out, m_w_conv_out, m_w_o, m_g_cross, m_g_mem, m_w_cq, m_w_ckv, m_w_co, m_g_ffn, m_w_gate, m_w_up, m_w_down, m_g_final, v_g_mix, v_w_in, v_sink, v_conv_w, v_b_gate, v_w_attn_out, v_w_conv_out, v_w_o, v_g_cross, v_g_mem, v_w_cq, v_w_ckv, v_w_co, v_g_ffn, v_w_gate, v_w_up, v_w_down, v_g_final):
    given = dict(g_mix=g_mix, w_in=w_in, sink=sink, conv_w=conv_w, b_gate=b_gate, w_attn_out=w_attn_out, w_conv_out=w_conv_out,
                 w_o=w_o, g_cross=g_cross, g_mem=g_mem, w_cq=w_cq, w_ckv=w_ckv, w_co=w_co, g_ffn=g_ffn, w_gate=w_gate, w_up=w_up,
                 w_down=w_down, g_final=g_final)
    mom_m = dict(g_mix=m_g_mix, w_in=m_w_in, sink=m_sink, conv_w=m_conv_w, b_gate=m_b_gate, w_attn_out=m_w_attn_out,
                 w_conv_out=m_w_conv_out, w_o=m_w_o, g_cross=m_g_cross, g_mem=m_g_mem, w_cq=m_w_cq, w_ckv=m_w_ckv, w_co=m_w_co,
                 g_ffn=m_g_ffn, w_gate=m_w_gate, w_up=m_w_up, w_down=m_w_down, g_final=m_g_final)
    mom_v = dict(g_mix=v_g_mix, w_in=v_w_in, sink=v_sink, conv_w=v_conv_w, b_gate=v_b_gate, w_attn_out=v_w_attn_out,
                 w_conv_out=v_w_conv_out, w_o=v_w_o, g_cross=v_g_cross, g_mem=v_g_mem, w_cq=v_w_cq, w_ckv=v_w_ckv, w_co=v_w_co,
                 g_ffn=v_g_ffn, w_gate=v_w_gate, w_up=v_w_up, w_down=v_w_down, g_final=v_g_final)
    xs, mems, target = x[0], mem[0], loss_target[0]
    d_model = xs.shape[1]
    chip = 2 * lax.axis_index("x") + lax.axis_index("y")
    core = jnp.reshape(lax.axis_index("c"), (1,)).astype(jnp.int32)
    place = jnp.stack([chip, lax.axis_index("c")]).astype(jnp.int32)

    shards = {n: given[n][0] for n in MATRICES}
    conv_cols = conv_w.shape[2]
    conv_pad = jnp.pad(conv_w[0], ((0, CONV_PAD_ROWS - conv_w.shape[1]), (0, 0)))
    fetch = _Gather(GATHER_GROUPS)
    first = {"w_in": _cast_to_slot(shards["w_in"], place, BF16, "to_slot_w_in"),
             "conv_w": _cast_to_slot(conv_pad, place, F32, "to_slot_conv_w")}
    fetch.put(first)
    tok = fetch.step("gather_start", [], [("direct", "in")])
    fetch.put({n: _cast_to_slot(shards[n], place, BF16, "to_slot_" + n, after=tok) for n in MATRICES if n != "w_in"})
    small = {n: given[n] for n in ("g_mix", "b_gate", "g_cross", "g_mem", "g_ffn")}
    small["g_final"] = g_final[None]
    small["sink"] = sink[0]

    reduce = _Reduce(place, core, shards, {n: mom_m[n][0] for n in MATRICES}, {n: mom_v[n][0] for n in MATRICES})
    sq, grad_x, small_grads = _local_step(xs, mems, target, small, fetch, reduce)

    loss_part = 0.5 * sq[0:1, 0:1] / d_model
    pieces = [small_grads[n] for n in VECTORS] + [loss_part]
    packed, _ = _pack(pieces)
    started, tok = _all_reduce_small_start(packed, core)
    tok = reduce.step("in", tok)
    tok = reduce.step("mid", tok)
    summed = _unpack(_all_reduce_small_done(started, tok), pieces)
    loss = summed[-1][0, 0]
    small_sum = dict(zip(VECTORS, summed[:-1]))
    small_sum["conv_w"] = lax.dynamic_slice_in_dim(small_sum["conv_w"], chip * conv_cols, conv_cols, axis=1)

    grad_out, delta, new_m, new_v = {}, {}, {}, {}
    like = [given[n] for n in VECTORS]
    pw, _ = _pack(like)
    pg, _ = _pack([small_sum[n] for n in VECTORS])
    pm, _ = _pack([mom_m[n] for n in VECTORS])
    pv, _ = _pack([mom_v[n] for n in VECTORS])
    _, pd, pnm, pnv = _adamw(pw, pg, pm, pv, "adamw_small")
    for n, g, d, nm, nv in zip(VECTORS, [small_sum[n] for n in VECTORS], _unpack(pd, like), _unpack(pnm, like), _unpack(pnv, like)):
        grad_out[n] = g.reshape(given[n].shape)
        delta[n], new_m[n], new_v[n] = d, nm, nv
    reduce.step("in", pd)
    for n in MATRICES:
        g, d, nm, nv = reduce.results[n]
        grad_out[n], delta[n], new_m[n], new_v[n] = g[None], d[None], nm[None], nv[None]

    return (loss, grad_x[None], *[grad_out[n] for n in WEIGHT_ORDER], *[delta[n] for n in WEIGHT_ORDER],
            *[new_m[n] for n in WEIGHT_ORDER], *[new_v[n] for n in WEIGHT_ORDER])
```

```python
import jax
import jax.numpy as jnp
from jax import lax
from jax.experimental import pallas as pl
from jax.experimental.pallas import tpu as pltpu

F32 = jnp.float32
BF16 = jnp.bfloat16
MESH = pl.DeviceIdType.MESH
ANY = pl.BlockSpec(memory_space=pl.ANY)

VMEM_LIMIT_BYTES = 56 * 1024 * 1024

N_CHIPS = 4
HEAD_DIM = 128
N_Q_HEADS = 8
N_KV_HEADS = 2
Q_GROUP = N_Q_HEADS // N_KV_HEADS
ATTN_WIDTH = N_Q_HEADS * HEAD_DIM
KV_WIDTH = N_KV_HEADS * HEAD_DIM
WINDOW = 128
BLOCK = 128
BAND = 3 * BLOCK
ROPE_THETA = 10000.0
CONV_WIDTH = 1024
MEM_HEADS = 4
MEM_WIDTH = MEM_HEADS * HEAD_DIM
RMS_EPS = 1e-6
NEG_INF = -1e30
ATTN_SCALE = HEAD_DIM ** -0.5

Q_OFF, K_OFF, V_OFF, CU_OFF, CB_OFF, CC_OFF, GL_OFF = 0, 1024, 1280, 1536, 2560, 3584, 4608

ADAM_LR = 0.001
ADAM_B1 = 0.9
ADAM_B2 = 0.999
ADAM_EPS = 1e-08
ADAM_WD = 0.01
ADAM_STEP = 10
ADAM_C1 = 1.0 - ADAM_B1 ** ADAM_STEP
ADAM_C2 = 1.0 - ADAM_B2 ** ADAM_STEP


def _params(n_grid_axes):
    return pltpu.CompilerParams(dimension_semantics=("arbitrary",) * n_grid_axes, vmem_limit_bytes=VMEM_LIMIT_BYTES)


BF16_SUBLANES = 16


def _row_tile(rows, want):
    if rows <= want:
        return rows
    for t in range(want, 0, -BF16_SUBLANES):
        if rows % t == 0:
            return t
    return rows


def _matmul(a, b, *, mode, tm, tn, out_dtypes, name, extras=(), epilogue=None, b_blocks=1, out_blocks=1, after=None):
    if mode == "tn":
        kdim, m = a.shape
    else:
        m, kdim = a.shape
    if b_blocks > 1:
        nb, brows, bcols = b.shape
        assert nb == b_blocks
        if mode == "nn":
            n = bcols * nb
            assert brows == kdim
        else:
            assert mode == "nt" and bcols * nb == kdim
            n = brows
    else:
        n = b.shape[0] if mode == "nt" else b.shape[1]
    tm, tn = min(tm, m), min(tn, n)
    tk = kdim
    assert m % tm == 0 and n % tn == 0, (name, m, n, tm, tn)
    n_extra, n_out = len(extras), len(out_dtypes)
    n_after = 0 if after is None else 1

    if mode == "tn":
        a_spec = pl.BlockSpec((tk, tm), lambda j, i, k: (k, i))
        dims = (((0,), (0,)), ((), ()))
    else:
        a_spec = pl.BlockSpec((tm, tk), lambda j, i, k: (i, k))
        dims = (((1,), (0,)), ((), ())) if mode == "nn" else (((1,), (1,)), ((), ()))

    if b_blocks > 1 and mode == "nn":
        per = b.shape[2] // tn
        assert b.shape[2] % tn == 0
        b_spec = pl.BlockSpec((None, tk, tn), lambda j, i, k: (j // per, k, j % per))
    elif b_blocks > 1:
        b_spec = pl.BlockSpec((b_blocks, tn, b.shape[2]), lambda j, i, k: (0, j, 0))
    elif mode == "nt":
        b_spec = pl.BlockSpec((tn, tk), lambda j, i, k: (j, k))
    else:
        b_spec = pl.BlockSpec((tk, tn), lambda j, i, k: (k, j))

    tile_spec = pl.BlockSpec((tm, tn), lambda j, i, k: (i, j))
    if out_blocks > 1:
        ncols = n // out_blocks
        assert ncols % tn == 0
        oper = ncols // tn
        out_spec = pl.BlockSpec((None, tm, tn), lambda j, i, k: (j // oper, i, j % oper))
        out_shape = [jax.ShapeDtypeStruct((out_blocks, m, ncols), dt) for dt in out_dtypes]
    else:
        out_spec = tile_spec
        out_shape = [jax.ShapeDtypeStruct((m, n), dt) for dt in out_dtypes]

    def body(a_ref, b_ref, *rest):
        extra_refs = rest[:n_extra]
        out_refs = rest[n_extra + n_after:n_extra + n_after + n_out]
        if mode == "nt" and b_blocks > 1:
            cs = b.shape[2]
            acc = None
            for jb in range(b_blocks):
                prod = lax.dot_general(a_ref[:, jb * cs:(jb + 1) * cs].astype(BF16), b_ref[jb].astype(BF16), dims,
                                       preferred_element_type=F32)
                acc = prod if acc is None else acc + prod
        else:
            acc = lax.dot_general(a_ref[...].astype(BF16), b_ref[...].astype(BF16), dims, preferred_element_type=F32)
        tiles = (acc,) if epilogue is None else epilogue(acc, *[r[...] for r in extra_refs])
        for o_ref, t in zip(out_refs, tiles, strict=True):
            o_ref[...] = t.astype(o_ref.dtype)

    outs = pl.pallas_call(
        body,
        name=name,
        grid=(n // tn, m // tm, 1),
        in_specs=[a_spec, b_spec] + [tile_spec] * n_extra + [ANY] * n_after,
        out_specs=[out_spec] * n_out,
        out_shape=out_shape,
        compiler_params=_params(3),
    )(a, b, *extras, *([] if after is None else [after]))
    return outs[0] if n_out == 1 else outs


def _add_residual(acc, res):
    return (acc + res,)


def _matmul_column_blocks(a, b4, blocks, out, *, tm, name, after=None):
    m, kdim = a.shape
    nb, _, cols = b4.shape
    tm = min(tm, m)
    assert m % tm == 0

    def body(j_ref, a_ref, b_ref, *rest):
        rest[-1][...] = jnp.dot(a_ref[...], b_ref[...], preferred_element_type=F32)

    extra = ([] if out is None else [out]) + ([] if after is None else [after])
    n_blocks = blocks.shape[0]
    return pl.pallas_call(
        body, name=name,
        grid_spec=pltpu.PrefetchScalarGridSpec(
            num_scalar_prefetch=1, grid=(n_blocks, m // tm),
            in_specs=[pl.BlockSpec((tm, kdim), lambda j, i, blk: (i, 0)),
                      pl.BlockSpec((None, kdim, cols), lambda j, i, blk: (blk[j], 0, 0))] + [ANY] * len(extra),
            out_specs=pl.BlockSpec((tm, cols), lambda j, i, blk: (i, blk[j]))),
        out_shape=jax.ShapeDtypeStruct((m, nb * cols), F32),
        input_output_aliases={} if out is None else {3: 0},
        compiler_params=_params(2),
    )(blocks, a, b4, *extra)


def _wgrad_half(a, b, core, *, theirs, row_sharded, tm, tn, name, add=None, after=None):
    kdim, m = a.shape
    n = b.shape[1]
    rs, cs = (m // N_CHIPS, n) if row_sharded else (m, n // N_CHIPS)
    rh = rs // 2
    tm, tn = min(tm, rh), min(tn, cs)
    assert rh % tm == 0 and cs % tn == 0, (name, rh, cs, tm, tn)
    mh, per = rh // tm, cs // tn
    has_add = add is not None

    def half(c):
        return 1 - c[0] if theirs else c[0]

    if row_sharded:
        grid = (n // tn, N_CHIPS * mh)
        a_spec = pl.BlockSpec((kdim, tm), lambda j, r, c: (0, ((r // mh) * 2 + half(c)) * mh + r % mh))
        o_spec = pl.BlockSpec((None, tm, tn), lambda j, r, c: (r // mh, r % mh, j))
    else:
        grid = (n // tn, mh)
        a_spec = pl.BlockSpec((kdim, tm), lambda j, r, c: (0, half(c) * mh + r))
        o_spec = pl.BlockSpec((None, tm, tn), lambda j, r, c: (j // per, r, j % per))
    b_spec = pl.BlockSpec((kdim, tn), lambda j, r, c: (0, j))

    def body(c_ref, a_ref, b_ref, *rest):
        o_ref = rest[-1]
        acc = lax.dot_general(a_ref[...].astype(BF16), b_ref[...].astype(BF16), (((0,), (0,)), ((), ())),
                              preferred_element_type=F32)
        if has_add:
            acc = acc + rest[0][...].astype(F32)
        o_ref[...] = acc.astype(BF16)

    operands = [a, b] + ([add] if has_add else []) + ([] if after is None else [after])
    return pl.pallas_call(
        body, name=name,
        grid_spec=pltpu.PrefetchScalarGridSpec(
            num_scalar_prefetch=1, grid=grid,
            in_specs=[a_spec, b_spec] + ([o_spec] if has_add else []) + ([] if after is None else [ANY]),
            out_specs=o_spec),
        out_shape=jax.ShapeDtypeStruct((N_CHIPS, rh, cs), BF16),
        compiler_params=_params(2),
    )(core, *operands)


def _rstd(x):
    return lax.rsqrt(jnp.mean(x * x, axis=-1, keepdims=True) + RMS_EPS)


def _rmsnorm(x, g, name):
    s, d = x.shape
    tr = _row_tile(s, 512)

    def body(x_ref, g_ref, o_ref):
        xv = x_ref[...]
        o_ref[...] = (xv * _rstd(xv) * g_ref[...]).astype(BF16)

    return pl.pallas_call(
        body, name=name, grid=(s // tr,),
        in_specs=[pl.BlockSpec((tr, d), lambda i: (i, 0)), pl.BlockSpec((1, d), lambda i: (0, 0))],
        out_specs=pl.BlockSpec((tr, d), lambda i: (i, 0)),
        out_shape=jax.ShapeDtypeStruct((s, d), BF16),
        compiler_params=_params(1),
    )(x, g)


def _rmsnorm_bwd(dh, x, g, dres, name):
    s, d = x.shape
    tr = _row_tile(s, 512)
    has_res = dres is not None

    def body(*refs):
        if has_res:
            dh_ref, x_ref, g_ref, res_ref, dx_ref, dxb_ref, dg_ref = refs
        else:
            dh_ref, x_ref, g_ref, dx_ref, dxb_ref, dg_ref = refs
        xv = x_ref[...]
        dhv = dh_ref[...].astype(F32)
        r = _rstd(xv)
        xn = xv * r
        dhg = dhv * g_ref[...]
        dx = r * (dhg - xn * jnp.mean(dhg * xn, axis=-1, keepdims=True))
        if has_res:
            dx = dx + res_ref[...]
        dx_ref[...] = dx
        dxb_ref[...] = dx.astype(BF16)
        part = jnp.sum(dhv * xn, axis=0, keepdims=True)

        @pl.when(pl.program_id(0) == 0)
        def _():
            dg_ref[...] = part

        @pl.when(pl.program_id(0) > 0)
        def _():
            dg_ref[...] += part

    row = pl.BlockSpec((tr, d), lambda i: (i, 0))
    vec = pl.BlockSpec((1, d), lambda i: (0, 0))
    return pl.pallas_call(
        body, name=name, grid=(s // tr,),
        in_specs=[row, row, vec] + ([row] if has_res else []),
        out_specs=[row, row, vec],
        out_shape=[jax.ShapeDtypeStruct((s, d), F32), jax.ShapeDtypeStruct((s, d), BF16), jax.ShapeDtypeStruct((1, d), F32)],
        compiler_params=_params(1),
    )(*([dh, x, g] + ([dres] if has_res else [])))


def _loss_head(x3, g, target):
    s, d = x3.shape
    tr = _row_tile(s, 512)

    def body(x_ref, g_ref, t_ref, dx_ref, dxb_ref, sq_ref, dg_ref):
        xv = x_ref[...]
        gv = g_ref[...]
        r = _rstd(xv)
        xn = xv * r
        err = xn * gv - t_ref[...]
        dy = err * (1.0 / d)
        dyg = dy * gv
        dx = r * (dyg - xn * jnp.mean(dyg * xn, axis=-1, keepdims=True))
        dx_ref[...] = dx
        dxb_ref[...] = dx.astype(BF16)
        sq = jnp.sum(jnp.sum(err * err, axis=1, keepdims=True), axis=0, keepdims=True)
        sq = jnp.broadcast_to(sq, (1, 128))
        part = jnp.sum(dy * xn, axis=0, keepdims=True)

        @pl.when(pl.program_id(0) == 0)
        def _():
            sq_ref[...] = sq
            dg_ref[...] = part

        @pl.when(pl.program_id(0) > 0)
        def _():
            sq_ref[...] += sq
            dg_ref[...] += part

    row = pl.BlockSpec((tr, d), lambda i: (i, 0))
    vec = pl.BlockSpec((1, d), lambda i: (0, 0))
    return pl.pallas_call(
        body, name="loss_head", grid=(s // tr,),
        in_specs=[row, vec, row],
        out_specs=[row, row, pl.BlockSpec((1, 128), lambda i: (0, 0)), vec],
        out_shape=[jax.ShapeDtypeStruct((s, d), F32), jax.ShapeDtypeStruct((s, d), BF16),
                   jax.ShapeDtypeStruct((1, 128), F32), jax.ShapeDtypeStruct((1, d), F32)],
        compiler_params=_params(1),
    )(x3, g, target)


def _rope_tables(s):
    inv = 1.0 / (ROPE_THETA ** (jnp.arange(0, HEAD_DIM, 2, dtype=F32) / HEAD_DIM))
    ang = jnp.arange(s, dtype=F32)[:, None] * inv[None, :]
    cos, sin = jnp.cos(ang), jnp.sin(ang)
    return jnp.concatenate([cos, cos], axis=1), jnp.concatenate([-sin, sin], axis=1)


def _swap_halves(t):
    return pltpu.roll(t, HEAD_DIM // 2, 1)


def _rope_fwd(z, cos_t, sin_t):
    s = z.shape[0]
    tr = _row_tile(s, 256)

    def body(zq_ref, zk_ref, zv_ref, c_ref, s_ref, q_ref, k_ref, v_ref):
        c, sn = c_ref[...], s_ref[...]
        for hd in range(N_Q_HEADS):
            cols = slice(hd * HEAD_DIM, (hd + 1) * HEAD_DIM)
            t = zq_ref[:, cols]
            q_ref[:, cols] = (t * c + _swap_halves(t) * sn).astype(BF16)
        for hd in range(N_KV_HEADS):
            cols = slice(hd * HEAD_DIM, (hd + 1) * HEAD_DIM)
            t = zk_ref[:, cols]
            k_ref[:, cols] = (t * c + _swap_halves(t) * sn).astype(BF16)
        v_ref[...] = zv_ref[...].astype(BF16)

    tab = pl.BlockSpec((tr, HEAD_DIM), lambda i: (i, 0))
    return pl.pallas_call(
        body, name="rope_fwd", grid=(s // tr,),
        in_specs=[pl.BlockSpec((tr, ATTN_WIDTH), lambda i: (i, Q_OFF // ATTN_WIDTH)),
                  pl.BlockSpec((tr, KV_WIDTH), lambda i: (i, K_OFF // KV_WIDTH)),
                  pl.BlockSpec((tr, KV_WIDTH), lambda i: (i, V_OFF // KV_WIDTH)), tab, tab],
        out_specs=[pl.BlockSpec((tr, ATTN_WIDTH), lambda i: (i, 0)), pl.BlockSpec((tr, KV_WIDTH), lambda i: (i, 0)),
                   pl.BlockSpec((tr, KV_WIDTH), lambda i: (i, 0))],
        out_shape=[jax.ShapeDtypeStruct((s, ATTN_WIDTH), BF16), jax.ShapeDtypeStruct((s, KV_WIDTH), BF16),
                   jax.ShapeDtypeStruct((s, KV_WIDTH), BF16)],
        compiler_params=_params(1),
    )(z, z, z, cos_t, sin_t)


def _rope_bwd(dq_rot, dk_rot, dv, cos_t, sin_t, dz):
    s = dq_rot.shape[0]
    tr = _row_tile(s, 256)
    qkv_width = V_OFF + KV_WIDTH

    def body(dq_ref, dk_ref, dv_ref, c_ref, s_ref, dz_in_ref, o_ref):
        c, sn = c_ref[...], s_ref[...]
        for hd in range(N_Q_HEADS):
            t = dq_ref[:, hd * HEAD_DIM:(hd + 1) * HEAD_DIM]
            o_ref[:, Q_OFF + hd * HEAD_DIM:Q_OFF + (hd + 1) * HEAD_DIM] = (t * c + _swap_halves(t * sn)).astype(BF16)
        for hd in range(N_KV_HEADS):
            t = dk_ref[:, hd * HEAD_DIM:(hd + 1) * HEAD_DIM]
            o_ref[:, K_OFF + hd * HEAD_DIM:K_OFF + (hd + 1) * HEAD_DIM] = (t * c + _swap_halves(t * sn)).astype(BF16)
        o_ref[:, V_OFF:V_OFF + KV_WIDTH] = dv_ref[...].astype(BF16)

    tab = pl.BlockSpec((tr, HEAD_DIM), lambda i: (i, 0))
    wide = pl.BlockSpec((tr, ATTN_WIDTH), lambda i: (i, 0))
    narrow = pl.BlockSpec((tr, KV_WIDTH), lambda i: (i, 0))
    return pl.pallas_call(
        body, name="rope_bwd", grid=(s // tr,),
        in_specs=[wide, narrow, narrow, tab, tab, ANY],
        out_specs=pl.BlockSpec((tr, qkv_width), lambda i: (i, 0)),
        out_shape=jax.ShapeDtypeStruct(dz.shape, dz.dtype),
        input_output_aliases={5: 0},
        compiler_params=_params(1),
    )(dq_rot, dk_rot, dv, cos_t, sin_t, dz)


def _swa_band(i, s):
    return pl.multiple_of(jnp.clip((i - 1) * BLOCK, 0, s - BAND), BLOCK)


SWA_HEADS_PER_PASS = Q_GROUP


def _swa_probs(q_ref, k_ref, sink_ref, heads, start, valid):
    kv = heads[0] // Q_GROUP
    cols = slice(kv * HEAD_DIM, (kv + 1) * HEAD_DIM)
    kb = k_ref[pl.ds(start, BAND), cols]
    qg = jnp.concatenate([q_ref[:, hd * HEAD_DIM:(hd + 1) * HEAD_DIM] for hd in heads], axis=0)
    sc = lax.dot_general(qg, kb, (((1,), (1,)), ((), ())), preferred_element_type=F32) * ATTN_SCALE
    sc = jnp.where(valid, sc, NEG_INF)
    sk = jnp.concatenate([jnp.full((BLOCK, 1), sink_ref[hd], F32) for hd in heads], axis=0)
    mx = jnp.maximum(jnp.max(sc, axis=1, keepdims=True), sk)
    e = jnp.exp(sc - mx)
    es = jnp.exp(sk - mx)
    inv = 1.0 / (jnp.sum(e, axis=1, keepdims=True) + es)
    return qg, kb, e * inv, es * inv


def _swa_head_passes():
    return [list(range(h0, h0 + SWA_HEADS_PER_PASS)) for h0 in range(0, N_Q_HEADS, SWA_HEADS_PER_PASS)]


def _swa_valid(i, start):
    q_pos = i * BLOCK + lax.broadcasted_iota(jnp.int32, (BLOCK, 1), 0)
    q_pos = jnp.concatenate([q_pos] * SWA_HEADS_PER_PASS, axis=0)
    k_pos = start + lax.broadcasted_iota(jnp.int32, (1, BAND), 1)
    return jnp.abs(k_pos - q_pos) <= WINDOW


def _swa_fwd(q, k, v, sink):
    s = q.shape[0]
    assert s % BLOCK == 0 and s >= BAND

    def body(sink_ref, q_ref, k_ref, v_ref, o_ref):
        i = pl.program_id(0)
        start = _swa_band(i, s)
        valid = _swa_valid(i, start)
        for heads in _swa_head_passes():
            kv = heads[0] // Q_GROUP
            _, _, p, _ = _swa_probs(q_ref, k_ref, sink_ref, heads, start, valid)
            vb = v_ref[pl.ds(start, BAND), kv * HEAD_DIM:(kv + 1) * HEAD_DIM]
            o = jnp.dot(p.astype(BF16), vb, preferred_element_type=F32)
            for g, hd in enumerate(heads):
                o_ref[:, hd * HEAD_DIM:(hd + 1) * HEAD_DIM] = o[g * BLOCK:(g + 1) * BLOCK].astype(BF16)

    whole = pl.BlockSpec((s, KV_WIDTH), lambda i: (0, 0))
    blk = pl.BlockSpec((BLOCK, ATTN_WIDTH), lambda i: (i, 0))
    return pl.pallas_call(
        body, name="swa_fwd", grid=(s // BLOCK,),
        in_specs=[pl.BlockSpec(memory_space=pltpu.SMEM), blk, whole, whole],
        out_specs=blk,
        out_shape=jax.ShapeDtypeStruct((s, ATTN_WIDTH), BF16),
        compiler_params=_params(1),
    )(sink, q, k, v)


def _swa_bwd(q, k, v, d_out, sink):
    s = q.shape[0]

    def body(sink_ref, q_ref, k_ref, v_ref, do_ref, dq_ref, dk_ref, dv_ref, dsink_ref):
        i = pl.program_id(0)

        @pl.when(i == 0)
        def _():
            dk_ref[...] = jnp.zeros_like(dk_ref)
            dv_ref[...] = jnp.zeros_like(dv_ref)
            dsink_ref[...] = jnp.zeros_like(dsink_ref)

        start = _swa_band(i, s)
        valid = _swa_valid(i, start)
        for heads in _swa_head_passes():
            kv = heads[0] // Q_GROUP
            cols = slice(kv * HEAD_DIM, (kv + 1) * HEAD_DIM)
            qg, kb, p, p_sink = _swa_probs(q_ref, k_ref, sink_ref, heads, start, valid)
            vb = v_ref[pl.ds(start, BAND), cols]
            dog = jnp.concatenate([do_ref[:, hd * HEAD_DIM:(hd + 1) * HEAD_DIM] for hd in heads], axis=0)
            dp = lax.dot_general(dog, vb, (((1,), (1,)), ((), ())), preferred_element_type=F32)
            delta = jnp.sum(p * dp, axis=1, keepdims=True)
            ds = (p * (dp - delta) * ATTN_SCALE).astype(BF16)
            dqg = jnp.dot(ds, kb, preferred_element_type=F32)
            dk_ref[pl.ds(start, BAND), cols] += lax.dot_general(ds, qg, (((0,), (0,)), ((), ())), preferred_element_type=F32)
            dv_ref[pl.ds(start, BAND), cols] += lax.dot_general(p.astype(BF16), dog, (((0,), (0,)), ((), ())),
                                                                 preferred_element_type=F32)
            dsk = p_sink * delta
            for g, hd in enumerate(heads):
                dq_ref[:, hd * HEAD_DIM:(hd + 1) * HEAD_DIM] = dqg[g * BLOCK:(g + 1) * BLOCK]
                tot = jnp.sum(dsk[g * BLOCK:(g + 1) * BLOCK], axis=0, keepdims=True)
                dsink_ref[hd:hd + 1, :] -= jnp.broadcast_to(tot, (1, 128))

    whole = pl.BlockSpec((s, KV_WIDTH), lambda i: (0, 0))
    blk = pl.BlockSpec((BLOCK, ATTN_WIDTH), lambda i: (i, 0))
    return pl.pallas_call(
        body, name="swa_bwd", grid=(s // BLOCK,),
        in_specs=[pl.BlockSpec(memory_space=pltpu.SMEM), blk, whole, whole, blk],
        out_specs=[blk, whole, whole, pl.BlockSpec((N_Q_HEADS, 128), lambda i: (0, 0))],
        out_shape=[jax.ShapeDtypeStruct((s, ATTN_WIDTH), F32), jax.ShapeDtypeStruct((s, KV_WIDTH), F32),
                   jax.ShapeDtypeStruct((s, KV_WIDTH), F32), jax.ShapeDtypeStruct((N_Q_HEADS, 128), F32)],
        compiler_params=_params(1),
    )(sink, q, k, v, d_out)


CONV_CHUNK = 256


def _shift_rows(t, rows, down):
    n = t.shape[0]
    rolled = pltpu.roll(t, 1 if down else n - 1, 0)
    edge = 0 if down else n - 1
    return jnp.where(rows == edge, 0.0, rolled)


def _conv_specs(s):
    def z_spec(off):
        return pl.BlockSpec((s, CONV_CHUNK), lambda j, off=off: (0, off // CONV_CHUNK + j))
    chunk = pl.BlockSpec((s, CONV_CHUNK), lambda j: (0, j))
    w_spec = pl.BlockSpec((3, CONV_CHUNK), lambda j: (0, j))
    return z_spec(CU_OFF), z_spec(CB_OFF), z_spec(CC_OFF), chunk, w_spec


def _conv_fwd(z, conv_w):
    s = z.shape[0]
    cu_spec, cb_spec, cc_spec, chunk, w_spec = _conv_specs(s)

    def body(cu_ref, cb_ref, cc_ref, w_ref, o_ref):
        rows = lax.broadcasted_iota(jnp.int32, (s, 1), 0)
        t = cc_ref[...] * cu_ref[...]
        c3 = _shift_rows(t, rows, True) * w_ref[0:1, :] + t * w_ref[1:2, :] + _shift_rows(t, rows, False) * w_ref[2:3, :]
        o_ref[...] = (cb_ref[...] * c3).astype(BF16)

    return pl.pallas_call(
        body, name="conv_fwd", grid=(CONV_WIDTH // CONV_CHUNK,),
        in_specs=[cu_spec, cb_spec, cc_spec, w_spec],
        out_specs=chunk,
        out_shape=jax.ShapeDtypeStruct((s, CONV_WIDTH), BF16),
        compiler_params=_params(1),
    )(z, z, z, conv_w)


def _conv_bwd(z, conv_w, d_co, dz):
    s = z.shape[0]
    cu_spec, cb_spec, cc_spec, chunk, w_spec = _conv_specs(s)
    n_chunks = CONV_WIDTH // CONV_CHUNK
    offsets = (CU_OFF, CB_OFF, CC_OFF)

    def body(cu_ref, cb_ref, cc_ref, w_ref, d_ref, dz_in_ref, dz_ref, dw_ref, buf, sems):
        j = pl.program_id(0)

        def copies(j_at):
            return [pltpu.make_async_copy(buf.at[h], dz_ref.at[:, pl.ds(off + j_at * CONV_CHUNK, CONV_CHUNK)], sems.at[h])
                    for h, off in enumerate(offsets)]

        rows = lax.broadcasted_iota(jnp.int32, (s, 1), 0)
        cu, cc = cu_ref[...], cc_ref[...]
        t = cc * cu
        t_dn, t_up = _shift_rows(t, rows, True), _shift_rows(t, rows, False)
        c3 = t_dn * w_ref[0:1, :] + t * w_ref[1:2, :] + t_up * w_ref[2:3, :]
        d = d_ref[...]
        dc3 = d * cb_ref[...]
        dw_ref[0:1, :] = jnp.sum(dc3 * t_dn, axis=0, keepdims=True)
        dw_ref[1:2, :] = jnp.sum(dc3 * t, axis=0, keepdims=True)
        dw_ref[2:3, :] = jnp.sum(dc3 * t_up, axis=0, keepdims=True)
        dt = _shift_rows(dc3, rows, False) * w_ref[0:1, :] + dc3 * w_ref[1:2, :] + _shift_rows(dc3, rows, True) * w_ref[2:3, :]

        @pl.when(j > 0)
        def _():
            for cp in copies(j):
                cp.wait()

        buf[0] = (dt * cc).astype(BF16)
        buf[1] = (d * c3).astype(BF16)
        buf[2] = (dt * cu).astype(BF16)
        for cp in copies(j):
            cp.start()

        @pl.when(j == n_chunks - 1)
        def _():
            for cp in copies(j):
                cp.wait()

    return pl.pallas_call(
        body, name="conv_bwd", grid=(n_chunks,),
        in_specs=[cu_spec, cb_spec, cc_spec, w_spec, chunk, ANY],
        out_specs=[ANY, w_spec],
        out_shape=[jax.ShapeDtypeStruct(dz.shape, dz.dtype), jax.ShapeDtypeStruct((3, CONV_WIDTH), F32)],
        input_output_aliases={5: 0},
        scratch_shapes=[pltpu.VMEM((3, s, CONV_CHUNK), BF16), pltpu.SemaphoreType.DMA((3,))],
        compiler_params=_params(1),
    )(z, z, z, conv_w, d_co, dz)


GATE_CHUNK = 512


def _gate_specs(s, d, tr):
    n_chunks = d // GATE_CHUNK
    za = pl.BlockSpec((tr, GATE_CHUNK), lambda j, i: (i, GL_OFF // GATE_CHUNK + j))
    zc = pl.BlockSpec((tr, GATE_CHUNK), lambda j, i: (i, GL_OFF // GATE_CHUNK + n_chunks + j))
    ba = pl.BlockSpec((1, GATE_CHUNK), lambda j, i: (0, j))
    bc = pl.BlockSpec((1, GATE_CHUNK), lambda j, i: (0, n_chunks + j))
    tile = pl.BlockSpec((tr, GATE_CHUNK), lambda j, i: (i, j))
    return za, zc, ba, bc, tile


def _gate_fwd(z, b_gate, ya, yc):
    s, d = ya.shape
    tr = _row_tile(s, 512)
    za, zc, ba, bc, tile = _gate_specs(s, d, tr)

    def body(za_ref, zc_ref, ba_ref, bc_ref, ya_ref, yc_ref, o_ref):
        ga = jax.nn.sigmoid(za_ref[...] + ba_ref[...])
        gc = jax.nn.sigmoid(zc_ref[...] + bc_ref[...])
        o_ref[...] = (ga * ya_ref[...] + gc * yc_ref[...]).astype(BF16)

    return pl.pallas_call(
        body, name="gate_fwd", grid=(d // GATE_CHUNK, s // tr),
        in_specs=[za, zc, ba, bc, tile, tile],
        out_specs=tile,
        out_shape=jax.ShapeDtypeStruct((s, d), BF16),
        compiler_params=_params(2),
    )(z, z, b_gate, b_gate, ya, yc)


def _gate_bwd(z, b_gate, ya, yc, dmix):
    s, d = ya.shape
    tr = _row_tile(s, 512)
    za, zc, ba, bc, tile = _gate_specs(s, d, tr)
    vec = pl.BlockSpec((1, GATE_CHUNK), lambda j, i: (0, j))
    n_rows = s // tr
    in_width = z.shape[1]

    def body(za_ref, zc_ref, ba_ref, bc_ref, ya_ref, yc_ref, dm_ref, dya_ref, dyc_ref, dz_ref, dba_ref, dbc_ref, buf, sems):
        j, i = pl.program_id(0), pl.program_id(1)

        def copies(j_at, i_at):
            rows = pl.ds(i_at * tr, tr)
            return [pltpu.make_async_copy(buf.at[h], dz_ref.at[rows, pl.ds(GL_OFF + h * d + j_at * GATE_CHUNK, GATE_CHUNK)],
                                          sems.at[h]) for h in range(2)]

        ga = jax.nn.sigmoid(za_ref[...] + ba_ref[...])
        gc = jax.nn.sigmoid(zc_ref[...] + bc_ref[...])
        dm = dm_ref[...]
        dya_ref[...] = (dm * ga).astype(BF16)
        dyc_ref[...] = (dm * gc).astype(BF16)
        dla = dm * ya_ref[...] * ga * (1.0 - ga)
        dlc = dm * yc_ref[...] * gc * (1.0 - gc)

        @pl.when(j * n_rows + i > 0)
        def _():
            for cp in copies(j, i):
                cp.wait()

        buf[0] = dla.astype(BF16)
        buf[1] = dlc.astype(BF16)
        for cp in copies(j, i):
            cp.start()

        @pl.when((j == d // GATE_CHUNK - 1) & (i == n_rows - 1))
        def _():
            for cp in copies(j, i):
                cp.wait()

        pa = jnp.sum(dla, axis=0, keepdims=True)
        pc = jnp.sum(dlc, axis=0, keepdims=True)

        @pl.when(i == 0)
        def _():
            dba_ref[...] = pa
            dbc_ref[...] = pc

        @pl.when(i > 0)
        def _():
            dba_ref[...] += pa
            dbc_ref[...] += pc

    big = jax.ShapeDtypeStruct((s, d), BF16)
    small = jax.ShapeDtypeStruct((1, d), F32)
    return pl.pallas_call(
        body, name="gate_bwd", grid=(d // GATE_CHUNK, n_rows),
        in_specs=[za, zc, ba, bc, tile, tile, tile],
        out_specs=[tile, tile, ANY, vec, vec],
        out_shape=[big, big, jax.ShapeDtypeStruct((s, in_width), BF16), small, small],
        scratch_shapes=[pltpu.VMEM((2, tr, GATE_CHUNK), BF16), pltpu.SemaphoreType.DMA((2,))],
        compiler_params=_params(2),
    )(z, z, b_gate, b_gate, ya, yc, dmix)


def _cross_probs(q_ref, kv_ref, hd):
    cols = slice(hd * HEAD_DIM, (hd + 1) * HEAD_DIM)
    qh = q_ref[:, cols]
    kh = kv_ref[:, cols]
    sc = lax.dot_general(qh, kh, (((1,), (1,)), ((), ())), preferred_element_type=F32) * ATTN_SCALE
    e = jnp.exp(sc - jnp.max(sc, axis=1, keepdims=True))
    return qh, kh, e * (1.0 / jnp.sum(e, axis=1, keepdims=True))


def _cross_fwd(qc, kvc):
    s = qc.shape[0]
    n_mem = kvc.shape[0]
    tq = _row_tile(s, 256)

    def body(q_ref, kv_ref, o_ref):
        for hd in range(MEM_HEADS):
            _, _, p = _cross_probs(q_ref, kv_ref, hd)
            vh = kv_ref[:, MEM_WIDTH + hd * HEAD_DIM:MEM_WIDTH + (hd + 1) * HEAD_DIM]
            o_ref[:, hd * HEAD_DIM:(hd + 1) * HEAD_DIM] = jnp.dot(p.astype(BF16), vh, preferred_element_type=F32).astype(BF16)

    return pl.pallas_call(
        body, name="cross_fwd", grid=(s // tq,),
        in_specs=[pl.BlockSpec((tq, MEM_WIDTH), lambda i: (i, 0)), pl.BlockSpec((n_mem, 2 * MEM_WIDTH), lambda i: (0, 0))],
        out_specs=pl.BlockSpec((tq, MEM_WIDTH), lambda i: (i, 0)),
        out_shape=jax.ShapeDtypeStruct((s, MEM_WIDTH), BF16),
        compiler_params=_params(1),
    )(qc, kvc)


def _cross_bwd(qc, kvc, d_out):
    s = qc.shape[0]
    n_mem = kvc.shape[0]
    tq = _row_tile(s, 256)

    def body(q_ref, kv_ref, do_ref, dq_ref, dkv_ref):
        @pl.when(pl.program_id(0) == 0)
        def _():
            dkv_ref[...] = jnp.zeros_like(dkv_ref)

        for hd in range(MEM_HEADS):
            cols = slice(hd * HEAD_DIM, (hd + 1) * HEAD_DIM)
            vcols = slice(MEM_WIDTH + hd * HEAD_DIM, MEM_WIDTH + (hd + 1) * HEAD_DIM)
            qh, kh, p = _cross_probs(q_ref, kv_ref, hd)
            doh = do_ref[:, cols]
            dp = lax.dot_general(doh, kv_ref[:, vcols], (((1,), (1,)), ((), ())), preferred_element_type=F32)
            ds = (p * (dp - jnp.sum(p * dp, axis=1, keepdims=True)) * ATTN_SCALE).astype(BF16)
            dq_ref[:, cols] = jnp.dot(ds, kh, preferred_element_type=F32).astype(BF16)
            dkv_ref[:, cols] += lax.dot_general(ds, qh, (((0,), (0,)), ((), ())), preferred_element_type=F32)
            dkv_ref[:, vcols] += lax.dot_general(p.astype(BF16), doh, (((0,), (0,)), ((), ())), preferred_element_type=F32)

    qspec = pl.BlockSpec((tq, MEM_WIDTH), lambda i: (i, 0))
    kvspec = pl.BlockSpec((n_mem, 2 * MEM_WIDTH), lambda i: (0, 0))
    return pl.pallas_call(
        body, name="cross_bwd", grid=(s // tq,),
        in_specs=[qspec, kvspec, qspec],
        out_specs=[qspec, kvspec],
        out_shape=[jax.ShapeDtypeStruct((s, MEM_WIDTH), BF16), jax.ShapeDtypeStruct((n_mem, 2 * MEM_WIDTH), F32)],
        compiler_params=_params(1),
    )(qc, kvc, d_out)


def _swiglu_fwd(up, gate):
    sg = jax.nn.sigmoid(gate)
    silu = gate * sg
    return silu * up, up * (sg * (1.0 + gate * (1.0 - sg))), silu


def _swiglu_bwd(d_act, dact_dgate, dact_dup):
    return d_act * dact_dgate.astype(F32), d_act * dact_dup.astype(F32)


GATHER_GROUPS = {"in": ("w_in", "conv_w"), "mid": ("w_attn_out", "w_conv_out", "w_o", "w_cq", "w_ckv", "w_co"),
                 "gate": ("w_gate",), "up": ("w_up",), "down": ("w_down",)}


def _local_step(xs, mems, target, small, fetch, reduce):
    s, d = xs.shape
    w4 = {}
    cos_t, sin_t = _rope_tables(s)

    def near(group, done, then, after):
        waits = [("direct", group)] + ([("pass_near", done), ("pass_far", done)] if done else [])
        starts = [("forward", group), ("pass_near", group)] + [("direct", g) for g in then]
        tok = fetch.step("gather_near_" + group, waits, starts, after)
        if done:
            w4.update(fetch.arrays(done))
        return tok

    def far(group, then, after):
        return fetch.step("gather_far_" + group, [("forward", group)], [("pass_far", group)] + [("direct", g) for g in then], after)

    def last(group, after):
        tok = fetch.step("gather_done_" + group, [("pass_near", group), ("pass_far", group)], [], after)
        w4.update(fetch.arrays(group))
        return tok

    h = _rmsnorm(xs, small["g_mix"], "norm_mix")
    slots_filled = [a for g in ("gate", "up", "down") for a in fetch.arrays(g).values()]
    chip_x, chip_y = reduce.place[0] // 2, reduce.place[0] % 2
    own_block = jnp.stack([2 * chip_x + chip_y]).astype(jnp.int32)
    near_blocks = jnp.stack([2 * (1 - chip_x) + chip_y, 2 * chip_x + (1 - chip_y)]).astype(jnp.int32)
    far_block = jnp.stack([2 * (1 - chip_x) + (1 - chip_y)]).astype(jnp.int32)
    z = _matmul_column_blocks(h, fetch.arrays("in")["w_in"], own_block, None, tm=512, name="in_proj_own")
    tok = near("in", None, ["mid"], [z] + slots_filled)
    tok = fetch.step("gather_near_done_in", [("pass_near", "in")], [], tok)
    z = _matmul_column_blocks(h, fetch.arrays("in")["w_in"], near_blocks, z, tm=1024, name="in_proj_near", after=tok)
    tok = far("in", [], z)
    tok = fetch.step("gather_done_in", [("pass_far", "in")], [], tok)
    w4.update(fetch.arrays("in"))
    z = _matmul_column_blocks(h, w4["w_in"], far_block, z, tm=1024, name="in_proj_far", after=tok)
    conv4 = w4["conv_w"]
    conv_w = conv4[:, :3, :].transpose(1, 0, 2).reshape(3, N_CHIPS * conv4.shape[2])
    c_in = w4["w_in"].shape[2]
    tok = near("mid", None, ["gate"], z)
    q_rot, k_rot, v_b = _rope_fwd(z, cos_t, sin_t)
    attn = _swa_fwd(q_rot, k_rot, v_b, small["sink"])
    co = _conv_fwd(z, conv_w)
    tok = far("mid", ["up"], attn)
    tok = last("mid", tok)
    w_o = w4["w_o"].reshape(-1, w4["w_o"].shape[-1])
    c_d = w4["w_attn_out"].shape[2]
    ya = _matmul(attn, w4["w_attn_out"], mode="nn", tm=2048, tn=c_d, out_dtypes=[F32], name="attn_out_proj",
                 b_blocks=N_CHIPS, after=tok)
    yc = _matmul(co, w4["w_conv_out"], mode="nn", tm=2048, tn=c_d, out_dtypes=[F32], name="conv_out_proj",
                 b_blocks=N_CHIPS)
    mix = _gate_fwd(z, small["b_gate"], ya, yc)
    x1 = _matmul(mix, w_o, mode="nn", tm=1024, tn=1024, out_dtypes=[F32], name="mix_out_proj", extras=[xs],
                 epilogue=_add_residual)
    tok = near("gate", None, ["down"], x1)
    w_cq = w4["w_cq"].reshape(-1, w4["w_cq"].shape[-1])
    w_ckv = w4["w_ckv"].reshape(-1, w4["w_ckv"].shape[-1])
    hc = _rmsnorm(x1, small["g_cross"], "norm_cross")
    memn = _rmsnorm(mems, small["g_mem"], "norm_mem")
    qc = _matmul(hc, w_cq, mode="nn", tm=2048, tn=MEM_WIDTH, out_dtypes=[BF16], name="cross_q_proj", after=tok)
    kvc = _matmul(memn, w_ckv, mode="nn", tm=256, tn=2 * MEM_WIDTH, out_dtypes=[BF16], name="cross_kv_proj")
    oc = _cross_fwd(qc, kvc)
    tok = far("gate", [], oc)
    x2 = _matmul(oc, w4["w_co"], mode="nn", tm=2048, tn=c_d, out_dtypes=[F32], name="cross_out_proj",
                 extras=[x1], epilogue=_add_residual, b_blocks=N_CHIPS, after=tok)
    hf = _rmsnorm(x2, small["g_ffn"], "norm_ffn")
    tok = near("up", "gate", [], hf)
    c_ff = w4["w_gate"].shape[2]
    gate = _matmul(hf, w4["w_gate"], mode="nn", tm=1024, tn=c_ff, out_dtypes=[F32], name="ffn_gate_proj", b_blocks=N_CHIPS,
                   after=tok)
    tok = far("up", [], gate)
    tok = near("down", "up", [], tok)
    act, dact_dgate, dact_dup = _matmul(hf, w4["w_up"], mode="nn", tm=1024, tn=c_ff, out_dtypes=[BF16, BF16, BF16],
                                        name="ffn_up_proj", extras=[gate], epilogue=_swiglu_fwd, b_blocks=N_CHIPS, after=tok)
    tok = far("down", [], act)
    last("down", tok)
    w_down = w4["w_down"].reshape(-1, w4["w_down"].shape[-1])
    x3 = _matmul(act, w_down, mode="nn", tm=512, tn=512, out_dtypes=[F32], name="ffn_down_proj", extras=[x2],
                 epilogue=_add_residual)
    dx3, dx3b, sq, dg_final = _loss_head(x3, small["g_final"], target)

    da, du = _matmul(dx3b, w_down, mode="nt", tm=1024, tn=c_ff, out_dtypes=[BF16, BF16], name="ffn_down_bwd",
                     extras=[dact_dgate, dact_dup], epilogue=_swiglu_bwd)
    core = reduce.core
    ffn_shape = dict(row_sharded=False, tm=1024, tn=c_ff)
    g_down = _matmul(act, dx3b, mode="tn", tm=c_ff, tn=1024, out_dtypes=[BF16], name="ffn_down_wgrad")
    tok = reduce.add("down", {"w_down": g_down}, da)
    t_gate = _wgrad_half(hf, da, core, theirs=True, name="ffn_gate_wgrad_theirs", after=tok, **ffn_shape)
    tok = reduce.step("down", t_gate)
    t_up = _wgrad_half(hf, du, core, theirs=True, name="ffn_up_wgrad_theirs", after=tok, **ffn_shape)
    tok = reduce.send("ffn", {"w_gate": t_gate, "w_up": t_up}, dx3b)
    dhf = _matmul(da, w4["w_gate"], mode="nt", tm=512, tn=1024, out_dtypes=[F32], name="ffn_gate_bwd", b_blocks=N_CHIPS,
                  after=tok)
    got = reduce.received("ffn", dhf)
    p_gate = _wgrad_half(hf, da, core, theirs=False, name="ffn_gate_wgrad_mine", add=got["w_gate"], **ffn_shape)
    p_up = _wgrad_half(hf, du, core, theirs=False, name="ffn_up_wgrad_mine", add=got["w_up"], **ffn_shape)
    tok = reduce.add_parts("ffn", {"w_gate": p_gate, "w_up": p_up})
    dhf = _matmul(du, w4["w_up"], mode="nt", tm=512, tn=1024, out_dtypes=[F32], name="ffn_up_bwd", extras=[dhf],
                  epilogue=_add_residual, b_blocks=N_CHIPS, after=tok)
    tok = reduce.step("down", dhf)
    dx2, dx2b, dg_ffn = _rmsnorm_bwd(dhf, x2, small["g_ffn"], dx3, "norm_ffn_bwd")

    d_oc = _matmul(dx2b, w4["w_co"], mode="nt", tm=1024, tn=MEM_WIDTH, out_dtypes=[BF16], name="cross_out_bwd",
                   b_blocks=N_CHIPS, after=tok)
    g_co = _matmul(oc, dx2b, mode="tn", tm=MEM_WIDTH, tn=c_d, out_dtypes=[BF16], name="cross_out_wgrad", out_blocks=N_CHIPS)
    tok = reduce.step("down", g_co)
    dqc, dkvc = _cross_bwd(qc, kvc, d_oc)
    g_cq = _matmul(hc, dqc, mode="tn", tm=1024, tn=MEM_WIDTH, out_dtypes=[BF16], name="cross_q_wgrad", after=tok)
    dhc = _matmul(dqc, w_cq, mode="nt", tm=1024, tn=1024, out_dtypes=[F32], name="cross_q_bwd")
    g_ckv = _matmul(memn, dkvc, mode="tn", tm=1024, tn=2 * MEM_WIDTH, out_dtypes=[BF16], name="cross_kv_wgrad")
    dx1, dx1b, dg_cross = _rmsnorm_bwd(dhc, x1, small["g_cross"], dx2, "norm_cross_bwd")

    dmix = _matmul(dx1b, w_o, mode="nt", tm=1024, tn=1024, out_dtypes=[F32], name="mix_out_bwd")
    g_o = _matmul(mix, dx1b, mode="tn", tm=1024, tn=1024, out_dtypes=[BF16], name="mix_out_wgrad")
    dya, dyc, dz, db_a, db_c = _gate_bwd(z, small["b_gate"], ya, yc, dmix)
    d_attn = _matmul(dya, w4["w_attn_out"], mode="nt", tm=1024, tn=ATTN_WIDTH, out_dtypes=[BF16], name="attn_out_bwd",
                     b_blocks=N_CHIPS)
    g_ao = _matmul(attn, dya, mode="tn", tm=ATTN_WIDTH, tn=c_d, out_dtypes=[BF16], name="attn_out_wgrad", out_blocks=N_CHIPS)
    d_co = _matmul(dyc, w4["w_conv_out"], mode="nt", tm=1024, tn=CONV_WIDTH, out_dtypes=[F32], name="conv_out_bwd",
                   b_blocks=N_CHIPS)
    g_cvo = _matmul(co, dyc, mode="tn", tm=CONV_WIDTH, tn=c_d, out_dtypes=[BF16], name="conv_out_wgrad", out_blocks=N_CHIPS)
    tok = reduce.step("ffn", g_cvo)
    tok = reduce.add("mid", {"w_co": g_co, "w_cq": g_cq, "w_ckv": g_ckv, "w_o": g_o, "w_attn_out": g_ao, "w_conv_out": g_cvo}, tok)
    dz, d_conv_w = _conv_bwd(z, conv_w, d_co, dz)
    dq_rot, dk_rot, dv, dsink = _swa_bwd(q_rot, k_rot, v_b, d_attn, small["sink"])
    tok = reduce.step("mid", dq_rot)
    dz = _rope_bwd(dq_rot, dk_rot, dv, cos_t, sin_t, dz)
    in_shape = dict(row_sharded=False, tm=1024, tn=c_in)
    t_in = _wgrad_half(h, dz, core, theirs=True, name="in_proj_wgrad_theirs", after=tok, **in_shape)
    tok = reduce.send("in", {"w_in": t_in}, dk_rot)
    tok = reduce.step("ffn", tok, count=1)
    dmemn = _matmul(dkvc, w_ckv, mode="nt", tm=256, tn=1024, out_dtypes=[F32], name="cross_kv_bwd", after=tok)
    _, _, dg_mem = _rmsnorm_bwd(dmemn, mems, small["g_mem"], None, "norm_mem_bwd")
    got = reduce.received("in", dg_mem)
    p_in = _wgrad_half(h, dz, core, theirs=False, name="in_proj_wgrad_mine", add=got["w_in"], **in_shape)
    tok = reduce.add_parts("in", {"w_in": p_in})
    tok = reduce.step("ffn", tok)
    tok = reduce.step("mid", tok)
    dh = _matmul(dz, w4["w_in"], mode="nt", tm=512, tn=512, out_dtypes=[F32], name="in_proj_bwd", b_blocks=N_CHIPS,
                 after=tok)
    grad_x, _, dg_mix = _rmsnorm_bwd(dh, xs, small["g_mix"], dx1, "norm_mix_bwd")

    small_grads = {
        "g_mix": dg_mix, "sink": dsink[:, 0], "b_gate": jnp.concatenate([db_a, db_c], axis=1), "g_cross": dg_cross,
        "g_mem": dg_mem, "g_ffn": dg_ffn, "g_final": dg_final, "conv_w": d_conv_w,
    }
    return sq, grad_x, small_grads


def _pair_sum(g4, ra, core, name):
    nb, rs, cs = g4.shape
    rh = rs // 2
    tr = _row_tile(rh, 256)
    per = rh // tr

    def body(c_ref, g_ref, r_ref, o_ref):
        o_ref[...] = (g_ref[...].astype(F32) + r_ref[...].astype(F32)).astype(BF16)

    plain = pl.BlockSpec((None, tr, cs), lambda j, i, c: (j, i, 0))
    return pl.pallas_call(
        body, name=name,
        grid_spec=pltpu.PrefetchScalarGridSpec(
            num_scalar_prefetch=1, grid=(nb, per),
            in_specs=[pl.BlockSpec((None, tr, cs), lambda j, i, c: (j, c[0] * per + i, 0)), plain],
            out_specs=plain),
        out_shape=jax.ShapeDtypeStruct((nb, rh, cs), BF16),
        compiler_params=_params(2),
    )(core, g4, ra)


def _adamw_update(w, g, m, v):
    nm = ADAM_B1 * m + (1.0 - ADAM_B1) * g
    nv = ADAM_B2 * v + (1.0 - ADAM_B2) * (g * g)
    m_hat = nm / ADAM_C1
    v_hat = nv / ADAM_C2
    return -ADAM_LR * (m_hat / (jnp.sqrt(v_hat) + ADAM_EPS) + ADAM_WD * w), nm, nv


def _adamw_own_half(w, m, v, parts, rc, place, name, after=None):
    rows, cols = w.shape
    rh = rows // 2
    tr = _row_tile(rh, 256)
    per = rh // tr

    def body(p_ref, w_ref, m_ref, v_ref, own_ref, r_ref, *rest):
        gx_ref, g_ref, d_ref, nm_ref, nv_ref = rest[-5:]
        g = own_ref[...].astype(F32)
        for j in range(rc.shape[0]):
            g = g + r_ref[j].astype(F32)
        gx_ref[...] = g
        g_ref[...] = g
        d_ref[...], nm_ref[...], nv_ref[...] = _adamw_update(w_ref[...], g, m_ref[...], v_ref[...])

    mine = pl.BlockSpec((tr, cols), lambda i, p: (p[1] * per + i, 0))
    shape = jax.ShapeDtypeStruct((rows, cols), F32)
    return pl.pallas_call(
        body, name=name,
        grid_spec=pltpu.PrefetchScalarGridSpec(
            num_scalar_prefetch=1, grid=(per,),
            in_specs=[mine, mine, mine, pl.BlockSpec((None, tr, cols), lambda i, p: (p[0], i, 0)),
                      pl.BlockSpec((rc.shape[0], tr, cols), lambda i, p: (0, i, 0))] + ([] if after is None else [ANY]),
            out_specs=[mine] * 5),
        out_shape=[shape] * 5,
        compiler_params=_params(1),
    )(place, w, m, v, parts, rc, *([] if after is None else [after]))


def _adamw_other_half(w, m, v, g_exchanged, g, delta, new_m, new_v, place, name, after=None):
    rows, cols = w.shape
    rh = rows // 2
    tr = _row_tile(rh, 256)
    per = rh // tr

    def body(p_ref, w_ref, m_ref, v_ref, gx_ref, *rest):
        g_ref, d_ref, nm_ref, nv_ref = rest[-4:]
        gv = gx_ref[...]
        g_ref[...] = gv
        d_ref[...], nm_ref[...], nv_ref[...] = _adamw_update(w_ref[...], gv, m_ref[...], v_ref[...])

    other = pl.BlockSpec((tr, cols), lambda i, p: ((1 - p[1]) * per + i, 0))
    shape = jax.ShapeDtypeStruct((rows, cols), F32)
    n_after = 0 if after is None else 1
    return pl.pallas_call(
        body, name=name,
        grid_spec=pltpu.PrefetchScalarGridSpec(
            num_scalar_prefetch=1, grid=(per,),
            in_specs=[other] * 4 + [ANY] * (4 + n_after),
            out_specs=[other] * 4),
        out_shape=[shape] * 4,
        input_output_aliases={5: 0, 6: 1, 7: 2, 8: 3},
        compiler_params=_params(1),
    )(place, w, m, v, g_exchanged, g, delta, new_m, new_v, *([] if after is None else [after]))


def _cast_to_slot(w, place, dtype, name, after=None):
    rows, cols = w.shape
    tr = _row_tile(rows, 1024)

    def body(p_ref, w_ref, *rest):
        o_ref = rest[-1]
        o_ref[...] = w_ref[...].astype(dtype)

    return pl.pallas_call(
        body, name=name,
        grid_spec=pltpu.PrefetchScalarGridSpec(
            num_scalar_prefetch=1, grid=(rows // tr,),
            in_specs=[pl.BlockSpec((tr, cols), lambda i, p: (i, 0))] + ([] if after is None else [ANY]),
            out_specs=pl.BlockSpec((None, tr, cols), lambda i, p: (p[0], i, 0))),
        out_shape=jax.ShapeDtypeStruct((N_CHIPS, rows, cols), dtype),
        compiler_params=_params(1),
    )(place, w, *([] if after is None else [after]))


def _adamw(w, g, m, v, name, after=None):
    rows, cols = w.shape
    tr = _row_tile(rows, 256)

    def body(w_ref, g_ref, m_ref, v_ref, *rest):
        go_ref, d_ref, nm_ref, nv_ref = rest[-4:]
        gv = g_ref[...]
        go_ref[...] = gv
        d_ref[...], nm_ref[...], nv_ref[...] = _adamw_update(w_ref[...], gv, m_ref[...], v_ref[...])

    tile = pl.BlockSpec((tr, cols), lambda i: (i, 0))
    shape = jax.ShapeDtypeStruct((rows, cols), F32)
    return pl.pallas_call(
        body, name=name, grid=(rows // tr,),
        in_specs=[tile] * 4 + ([] if after is None else [ANY]), out_specs=[tile] * 4, out_shape=[shape] * 4,
        compiler_params=_params(1),
    )(w, g, m, v, *([] if after is None else [after]))


def _mesh_pos():
    return lax.axis_index("x"), lax.axis_index("y"), lax.axis_index("c")


def _other_chips(x, y):
    return [(1 - x, y), (x, 1 - y), (1 - x, 1 - y)]


def _half_rows(ref, which):
    rh = ref.shape[-2] // 2
    return ref.at[pl.ds(which * rh, rh), :]


def _remote(src, dst, send_sems, recv_sems, sem, to):
    return pltpu.make_async_remote_copy(src_ref=src, dst_ref=dst, send_sem=send_sems.at[sem], recv_sem=recv_sems.at[sem],
                                        device_id=to, device_id_type=MESH)


HBM = pl.BlockSpec(memory_space=pltpu.HBM)
SEM = pl.BlockSpec(memory_space=pltpu.SEMAPHORE)
DATAFLOW_EFFECT = pltpu.SideEffectType.DATAFLOW_SIDE_EFFECTING


def _in_hbm(arrays):
    return [pltpu.with_memory_space_constraint(a, pltpu.HBM) for a in arrays]


def _hbm_like(arrays):
    return [pltpu.HBM(a.shape, a.dtype) for a in arrays]


GATHER_COPIES_PER_ARRAY = {"direct": 2, "forward": 2, "pass_near": 2, "pass_far": 1}


def _gather_copies(kind, refs, x, y, c):
    me, near_x, near_y, far = 2 * x + y, 2 * (1 - x) + y, 2 * x + (1 - y), 2 * (1 - x) + (1 - y)
    to_x, to_y, sibling = (1 - x, y, c), (x, 1 - y, c), (x, y, 1 - c)
    out = []
    for ref in refs:
        rh = ref.shape[1] // 2
        rq = rh // 2

        def half(chip, ref=ref, rh=rh):
            return ref.at[chip, pl.ds(c * rh, rh), :]

        def quarter(chip, q, ref=ref, rh=rh, rq=rq):
            return ref.at[chip, pl.ds(c * rh + q * rq, rq), :]

        if kind == "direct":
            out += [(half(me), half(me), to_x), (half(me), half(me), to_y)]
        elif kind == "forward":
            out += [(quarter(near_x, 0), quarter(near_x, 0), to_y), (quarter(near_y, 1), quarter(near_y, 1), to_x)]
        elif kind == "pass_near":
            out += [(half(near_x), half(near_x), sibling), (half(near_y), half(near_y), sibling)]
        else:
            assert kind == "pass_far"
            out += [(half(far), half(far), sibling)]
    return out


def _gather_step(name, bufs, waits, starts, after):
    nb, nw, ns = len(bufs), len(waits), len(starts)
    after = [] if after is None else list(after) if isinstance(after, (list, tuple)) else [after]
    n_after = len(after)

    def body(*refs):
        ins = refs[:nb]
        wait_sems = refs[nb:nb + 2 * nw]
        start_sems = refs[nb + 2 * nw + n_after:nb + 2 * nw + n_after + 2 * ns]
        token = refs[-1]
        x, y, c = _mesh_pos()
        for j, (kind, idxs, _, _) in enumerate(waits):
            for i, (s_ref, d_ref, to) in enumerate(_gather_copies(kind, [ins[t] for t in idxs], x, y, c)):
                came = _remote(s_ref, d_ref, wait_sems[2 * j], wait_sems[2 * j + 1], i, to)
                came.wait_recv()
                came.wait_send()
        for j, (kind, idxs) in enumerate(starts):
            for i, (s_ref, d_ref, to) in enumerate(_gather_copies(kind, [ins[t] for t in idxs], x, y, c)):
                _remote(s_ref, d_ref, start_sems[2 * j], start_sems[2 * j + 1], i, to).start()
        token[...] = jnp.zeros_like(token)

    sems = []
    for kind, idxs in starts:
        sems += [pltpu.SemaphoreType.DMA((GATHER_COPIES_PER_ARRAY[kind] * len(idxs),))] * 2
    operands = _in_hbm(bufs) + [sem for w in waits for sem in w[2:]] + after
    outs = pl.pallas_call(
        body, name=name,
        in_specs=[HBM] * nb + [SEM] * (2 * nw) + [ANY] * n_after,
        out_specs=[SEM] * (2 * ns) + [HBM] * nb + [pl.BlockSpec(memory_space=pltpu.VMEM)],
        out_shape=sems + _hbm_like(bufs) + [jax.ShapeDtypeStruct((8, 128), F32)],
        input_output_aliases={i: 2 * ns + i for i in range(nb)},
        compiler_params=pltpu.CompilerParams(has_side_effects=DATAFLOW_EFFECT),
    )(*operands)
    return outs[2 * ns:2 * ns + nb], [(outs[2 * j], outs[2 * j + 1]) for j in range(ns)], outs[-1]


class _Gather:
    def __init__(self, groups):
        self.groups = groups
        self.bufs = {}
        self.in_flight = {}

    def put(self, slotted):
        self.bufs.update(slotted)

    def step(self, name, waits, starts, after=None):
        names = []
        for _, group in list(waits) + list(starts):
            names += [n for n in self.groups[group] if n not in names]
        index = {n: i for i, n in enumerate(names)}

        def members(group):
            return [index[n] for n in self.groups[group]]

        wait_args = [(kind, members(group)) + self.in_flight.pop((kind, group)) for kind, group in waits]
        start_args = [(kind, members(group)) for kind, group in starts]
        bufs, sems, token = _gather_step(name, [self.bufs[n] for n in names], wait_args, start_args, after)
        self.bufs.update(zip(names, bufs))
        for (kind, group), pair in zip(starts, sems):
            self.in_flight[(kind, group)] = pair
        return token

    def arrays(self, group):
        return {n: self.bufs[n] for n in self.groups[group]}


def _sibling_halves_copies(srcs, dsts, x, y, c):
    out = []
    for s_ref, d_ref in zip(srcs, dsts, strict=True):
        rh = s_ref.shape[1] // 2
        out.append((s_ref.at[:, pl.ds((1 - c) * rh, rh), :], d_ref, (x, y, 1 - c)))
    return out


def _to_sibling_copies(srcs, dsts, x, y, c):
    return [(s_ref, d_ref, (x, y, 1 - c)) for s_ref, d_ref in zip(srcs, dsts, strict=True)]


def _chip_copies(srcs, dsts, x, y, c):
    out = []
    for s_ref, d_ref in zip(srcs, dsts, strict=True):
        for k, (px, py) in enumerate(_other_chips(x, y)):
            out.append((s_ref.at[2 * px + py], d_ref.at[k], (px, py, c)))
    return out


def _join_copies(srcs, dsts, x, y, c):
    out = []
    for s_ref in srcs:
        mine = _half_rows(s_ref, c)
        out.append((mine, mine, (x, y, 1 - c)))
    return out


def _exchange_start(copies_fn, n_copies, srcs, fresh, after, name):
    ns, nb = len(srcs), len(srcs) + len(fresh)

    def body(*refs):
        bufs, send, recv, token = refs[:nb], refs[nb + 1], refs[nb + 2], refs[-1]
        x, y, c = _mesh_pos()
        for i, (s_ref, d_ref, to) in enumerate(copies_fn(bufs[:ns], bufs[ns:] if fresh else bufs[:ns], x, y, c)):
            _remote(s_ref, d_ref, send, recv, i, to).start()
        token[...] = jnp.zeros_like(token)

    sems = [pltpu.SemaphoreType.DMA((n_copies,))] * 2
    outs = pl.pallas_call(
        body, name=name,
        in_specs=[HBM] * nb + [ANY], out_specs=[SEM, SEM] + [HBM] * nb + [pl.BlockSpec(memory_space=pltpu.VMEM)],
        out_shape=sems + _hbm_like(list(srcs) + list(fresh)) + [jax.ShapeDtypeStruct((8, 128), F32)],
        input_output_aliases={i: 2 + i for i in range(nb)},
        compiler_params=pltpu.CompilerParams(has_side_effects=DATAFLOW_EFFECT),
    )(*_in_hbm(list(srcs) + list(fresh)), after)
    return outs[0], outs[1], outs[2:2 + ns], outs[2 + ns:2 + nb], outs[-1]


def _exchange_done(copies_fn, srcs, fresh, send, recv, after, name):
    ns, nb = len(srcs), len(srcs) + len(fresh)

    def body(*refs):
        bufs, send_in, recv_in = refs[:nb], refs[nb], refs[nb + 1]
        x, y, c = _mesh_pos()
        for i, (s_ref, d_ref, to) in enumerate(copies_fn(bufs[:ns], bufs[ns:] if fresh else bufs[:ns], x, y, c)):
            came = _remote(s_ref, d_ref, send_in, recv_in, i, to)
            came.wait_send()
            came.wait_recv()

    outs = pl.pallas_call(
        body, name=name,
        in_specs=[HBM] * nb + [SEM, SEM, ANY], out_specs=[HBM] * nb,
        out_shape=_hbm_like(list(srcs) + list(fresh)),
        input_output_aliases={i: i for i in range(nb)},
        compiler_params=pltpu.CompilerParams(has_side_effects=DATAFLOW_EFFECT),
    )(*_in_hbm(list(srcs) + list(fresh)), send, recv, after)
    return outs[:ns], outs[ns:]


class _Reduce:
    def __init__(self, place, core, shards, mom_m, mom_v):
        self.place, self.core = place, core
        self.shards, self.mom_m, self.mom_v = shards, mom_m, mom_v
        self.state = {}
        self.results = {}

    def add(self, group, grads, after):
        names = list(grads)
        g4s = [g.reshape((N_CHIPS, -1, g.shape[-1])) if g.ndim == 2 else g for g in grads.values()]
        fresh = [lax.empty((N_CHIPS, g.shape[1] // 2, g.shape[2]), BF16) for g in g4s]
        send, recv, g4s, fresh, token = _exchange_start(_sibling_halves_copies, len(names), g4s, fresh, after,
                                                        "pair_start_" + group)
        self.state[group] = (0, names, send, recv, g4s, fresh)
        return token

    def send(self, group, theirs, after):
        names, srcs = list(theirs), list(theirs.values())
        fresh = [lax.empty(s.shape, BF16) for s in srcs]
        send, recv, srcs, fresh, token = _exchange_start(_to_sibling_copies, len(names), srcs, fresh, after, "pair_start_" + group)
        self.state[group] = ("sent", names, send, recv, srcs, fresh)
        return token

    def received(self, group, after):
        stage, names, send, recv, srcs, fresh = self.state.pop(group)
        assert stage == "sent"
        _, got = _exchange_done(_to_sibling_copies, srcs, fresh, send, recv, after, "pair_done_" + group)
        return dict(zip(names, got))

    def add_parts(self, group, parts):
        names, srcs = list(parts), list(parts.values())
        fresh = [lax.empty((N_CHIPS - 1,) + p.shape[1:], BF16) for p in srcs]
        send, recv, srcs, fresh, token = _exchange_start(_chip_copies, 3 * len(names), srcs, fresh, self.core, "chips_start_" + group)
        self.state[group] = (1, names, send, recv, srcs, fresh)
        return token

    def step(self, group, after, count=None):
        stage, names, send, recv, srcs, fresh = self.state[group]
        if stage == 0:
            g4s, ras = _exchange_done(_sibling_halves_copies, srcs, fresh, send, recv, after, "pair_done_" + group)
            parts = [_pair_sum(g, r, self.core, "pair_sum_" + n) for g, r, n in zip(g4s, ras, names)]
            fresh = [lax.empty((N_CHIPS - 1,) + p.shape[1:], BF16) for p in parts]
            send, recv, parts, fresh, token = _exchange_start(_chip_copies, 3 * len(names), parts, fresh, self.core,
                                                              "chips_start_" + group)
            self.state[group] = (1, names, send, recv, parts, fresh)
            return token
        if stage == 1:
            parts, rcs = _exchange_done(_chip_copies, srcs, fresh, send, recv, after, "chips_done_" + group)
            token = None
            for n, p, r in zip(names, parts, rcs):
                self.results[n] = _adamw_own_half(self.shards[n], self.mom_m[n], self.mom_v[n], p, r, self.place,
                                                  "adamw_own_" + n, after=token)
                token = self.results[n][2]
            wholes = [self.results[n][0] for n in names]
            send, recv, wholes, _, token = _exchange_start(_join_copies, len(names), wholes, [], token, "join_start_" + group)
            self.state[group] = (2, names, send, recv, wholes, [])
            return token
        assert stage in (2, 3)
        if stage == 2:
            srcs, _ = _exchange_done(_join_copies, srcs, [], send, recv, after, "join_done_" + group)
            after = None
        token = after
        count = len(names) if count is None else count
        for n, exchanged in zip(names[:count], srcs):
            _, g, d, nm, nv = self.results[n]
            self.results[n] = _adamw_other_half(self.shards[n], self.mom_m[n], self.mom_v[n], exchanged, g, d, nm, nv,
                                                self.place, "adamw_other_" + n, after=token)
            token = self.results[n][1]
        if count < len(names):
            self.state[group] = (3, names[count:], None, None, srcs[count:], [])
        else:
            del self.state[group]
        return token


N_DEV = 8


def _to_all_copies(srcs, dsts, x, y, c):
    out = []
    for r in range(1, N_DEV):
        fx, fy, fc = (r >> 2) & 1, (r >> 1) & 1, r & 1
        out.append((srcs[0], dsts[0].at[r - 1], (x + fx - 2 * x * fx, y + fy - 2 * y * fy, c + fc - 2 * c * fc)))
    return out


def _all_reduce_small_start(v, after):
    slots = lax.empty((N_DEV - 1,) + v.shape, v.dtype)
    send, recv, (v,), (slots,), token = _exchange_start(_to_all_copies, N_DEV - 1, [v], [slots], after, "small_grads_start")
    return (send, recv, v, slots), token


def _all_reduce_small_done(started, after):
    send, recv, v, slots = started
    (v,), (slots,) = _exchange_done(_to_all_copies, [v], [slots], send, recv, after, "small_grads_done")

    def body(v_ref, slots_ref, o_ref):
        x, y, c = _mesh_pos()
        me = 4 * x + 2 * y + c
        acc = None
        for i in range(N_DEV):
            r = jnp.bitwise_xor(me, i)
            part = jnp.where(r == 0, v_ref[...], slots_ref[jnp.maximum(r - 1, 0)])
            acc = part if acc is None else acc + part
        o_ref[...] = acc

    vm = pl.BlockSpec(memory_space=pltpu.VMEM)
    return pl.pallas_call(body, name="small_grads_sum", in_specs=[vm, vm], out_specs=vm,
                          out_shape=jax.ShapeDtypeStruct(v.shape, v.dtype))(v, slots)


MATRICES = ("w_in", "w_attn_out", "w_conv_out", "w_o", "w_cq", "w_ckv", "w_co", "w_gate", "w_up", "w_down")
VECTORS = ("g_mix", "b_gate", "g_cross", "g_mem", "g_ffn", "g_final", "conv_w", "sink")
WEIGHT_ORDER = ("g_mix", "w_in", "sink", "conv_w", "b_gate", "w_attn_out", "w_conv_out", "w_o", "g_cross", "g_mem", "w_cq",
                "w_ckv", "w_co", "g_ffn", "w_gate", "w_up", "w_down", "g_final")
CONV_PAD_ROWS = 32
SMALL_ROWS = 8


def _pack(pieces):
    flat = jnp.concatenate([p.reshape(-1) for p in pieces])
    lane_group = SMALL_ROWS * 128
    total = -(-flat.shape[0] // lane_group) * lane_group
    flat = jnp.pad(flat, (0, total - flat.shape[0]))
    return flat.reshape(SMALL_ROWS, total // SMALL_ROWS), [p.size for p in pieces]


def _unpack(packed, pieces):
    flat = packed.reshape(-1)
    out, off = [], 0
    for p in pieces:
        out.append(flat[off:off + p.size].reshape(p.shape))
        off += p.size
    return out


def kernel(x, mem, g_mix, w_in, sink, conv_w, b_gate, w_attn_out, w_conv_out, w_o, g_cross, g_mem, w_cq, w_ckv, w_co, g_ffn, w_gate, w_up, w_down, g_final, loss_target, m_g_mix, m_w_in, m_sink, m_conv_w, m_b_gate, m_w_attn_out, m_w_conv_out, m_w_o, m_g_cross, m_g_mem, m_w_cq, m_w_ckv, m_w_co, m_g_ffn, m_w_gate, m_w_up, m_w_down, m_g_final, v_g_mix, v_w_in, v_sink, v_conv_w, v_b_gate, v_w_attn_out, v_w_conv_out, v_w_o, v_g_cross, v_g_mem, v_w_cq, v_w_ckv, v_w_co, v_g_ffn, v_w_gate, v_w_up, v_w_down, v_g_final):
    given = dict(g_mix=g_mix, w_in=w_in, sink=sink, conv_w=conv_w, b_gate=b_gate, w_attn_out=w_attn_out, w_conv_out=w_conv_out,
                 w_o=w_o, g_cross=g_cross, g_mem=g_mem, w_cq=w_cq, w_ckv=w_ckv, w_co=w_co, g_ffn=g_ffn, w_gate=w_gate, w_up=w_up,
                 w_down=w_down, g_final=g_final)
    mom_m = dict(g_mix=m_g_mix, w_in=m_w_in, sink=m_sink, conv_w=m_conv_w, b_gate=m_b_gate, w_attn_out=m_w_attn_out,
                 w_conv_out=m_w_conv_out, w_o=m_w_o, g_cross=m_g_cross, g_mem=m_g_mem, w_cq=m_w_cq, w_ckv=m_w_ckv, w_co=m_w_co,
                 g_ffn=m_g_ffn, w_gate=m_w_gate, w_up=m_w_up, w_down=m_w_down, g_final=m_g_final)
    mom_v = dict(g_mix=v_g_mix, w_in=v_w_in, sink=v_sink, conv_w=v_conv_w, b_gate=v_b_gate, w_attn_out=v_w_attn_out,
                 w_conv_out=v_w_conv_out, w_o=v_w_o, g_cross=v_g_cross, g_mem=v_g_mem, w_cq=v_w_cq, w_ckv=v_w_ckv, w_co=v_w_co,
                 g_ffn=v_g_ffn, w_gate=v_w_gate, w_up=v_w_up, w_down=v_w_down, g_final=v_g_final)
    xs, mems, target = x[0], mem[0], loss_target[0]
    d_model = xs.shape[1]
    chip = 2 * lax.axis_index("x") + lax.axis_index("y")
    core = jnp.reshape(lax.axis_index("c"), (1,)).astype(jnp.int32)
    place = jnp.stack([chip, lax.axis_index("c")]).astype(jnp.int32)

    shards = {n: given[n][0] for n in MATRICES}
    conv_cols = conv_w.shape[2]
    conv_pad = jnp.pad(conv_w[0], ((0, CONV_PAD_ROWS - conv_w.shape[1]), (0, 0)))
    fetch = _Gather(GATHER_GROUPS)
    first = {"w_in": _cast_to_slot(shards["w_in"], place, BF16, "to_slot_w_in"),
             "conv_w": _cast_to_slot(conv_pad, place, F32, "to_slot_conv_w")}
    fetch.put(first)
    tok = fetch.step("gather_start", [], [("direct", "in")])
    fetch.put({n: _cast_to_slot(shards[n], place, BF16, "to_slot_" + n, after=tok) for n in MATRICES if n != "w_in"})
    small = {n: given[n] for n in ("g_mix", "b_gate", "g_cross", "g_mem", "g_ffn")}
    small["g_final"] = g_final[None]
    small["sink"] = sink[0]

    reduce = _Reduce(place, core, shards, {n: mom_m[n][0] for n in MATRICES}, {n: mom_v[n][0] for n in MATRICES})
    sq, grad_x, small_grads = _local_step(xs, mems, target, small, fetch, reduce)

    loss_part = 0.5 * sq[0:1, 0:1] / d_model
    pieces = [small_grads[n] for n in VECTORS] + [loss_part]
    packed, _ = _pack(pieces)
    started, tok = _all_reduce_small_start(packed, core)
    tok = reduce.step("in", tok)
    tok = reduce.step("mid", tok)
    summed = _unpack(_all_reduce_small_done(started, tok), pieces)
    loss = summed[-1][0, 0]
    small_sum = dict(zip(VECTORS, summed[:-1]))
    small_sum["conv_w"] = lax.dynamic_slice_in_dim(small_sum["conv_w"], chip * conv_cols, conv_cols, axis=1)

    grad_out, delta, new_m, new_v = {}, {}, {}, {}
    like = [given[n] for n in VECTORS]
    pw, _ = _pack(like)
    pg, _ = _pack([small_sum[n] for n in VECTORS])
    pm, _ = _pack([mom_m[n] for n in VECTORS])
    pv, _ = _pack([mom_v[n] for n in VECTORS])
    _, pd, pnm, pnv = _adamw(pw, pg, pm, pv, "adamw_small")
    for n, g, d, nm, nv in zip(VECTORS, [small_sum[n] for n in VECTORS], _unpack(pd, like), _unpack(pnm, like), _unpack(pnv, like)):
        grad_out[n] = g.reshape(given[n].shape)
        delta[n], new_m[n], new_v[n] = d, nm, nv
    reduce.step("in", pd)
    for n in MATRICES:
        g, d, nm, nv = reduce.results[n]
        grad_out[n], delta[n], new_m[n], new_v[n] = g[None], d[None], nm[None], nv[None]

    return (loss, grad_x[None], *[grad_out[n] for n in WEIGHT_ORDER], *[delta[n] for n in WEIGHT_ORDER],
            *[new_m[n] for n in WEIGHT_ORDER], *[new_v[n] for n in WEIGHT_ORDER])
```

```python
import jax
import jax.numpy as jnp
from jax import lax
from jax.experimental import pallas as pl
from jax.experimental.pallas import tpu as pltpu

F32 = jnp.float32
BF16 = jnp.bfloat16
MESH = pl.DeviceIdType.MESH
ANY = pl.BlockSpec(memory_space=pl.ANY)

VMEM_LIMIT_BYTES = 56 * 1024 * 1024

N_CHIPS = 4
HEAD_DIM = 128
N_Q_HEADS = 8
N_KV_HEADS = 2
Q_GROUP = N_Q_HEADS // N_KV_HEADS
ATTN_WIDTH = N_Q_HEADS * HEAD_DIM
KV_WIDTH = N_KV_HEADS * HEAD_DIM
WINDOW = 128
BLOCK = 128
BAND = 3 * BLOCK
ROPE_THETA = 10000.0
CONV_WIDTH = 1024
MEM_HEADS = 4
MEM_WIDTH = MEM_HEADS * HEAD_DIM
RMS_EPS = 1e-6
NEG_INF = -1e30
ATTN_SCALE = HEAD_DIM ** -0.5

Q_OFF, K_OFF, V_OFF, CU_OFF, CB_OFF, CC_OFF, GL_OFF = 0, 1024, 1280, 1536, 2560, 3584, 4608

ADAM_LR = 0.001
ADAM_B1 = 0.9
ADAM_B2 = 0.999
ADAM_EPS = 1e-08
ADAM_WD = 0.01
ADAM_STEP = 10
ADAM_C1 = 1.0 - ADAM_B1 ** ADAM_STEP
ADAM_C2 = 1.0 - ADAM_B2 ** ADAM_STEP


def _params(n_grid_axes):
    return pltpu.CompilerParams(dimension_semantics=("arbitrary",) * n_grid_axes, vmem_limit_bytes=VMEM_LIMIT_BYTES)


BF16_SUBLANES = 16


def _row_tile(rows, want):
    if rows <= want:
        return rows
    for t in range(want, 0, -BF16_SUBLANES):
        if rows % t == 0:
            return t
    return rows


def _matmul(a, b, *, mode, tm, tn, out_dtypes, name, extras=(), epilogue=None, b_blocks=1, out_blocks=1, after=None):
    if mode == "tn":
        kdim, m = a.shape
    else:
        m, kdim = a.shape
    if b_blocks > 1:
        nb, brows, bcols = b.shape
        assert nb == b_blocks
        if mode == "nn":
            n = bcols * nb
            assert brows == kdim
        else:
            assert mode == "nt" and bcols * nb == kdim
            n = brows
    else:
        n = b.shape[0] if mode == "nt" else b.shape[1]
    tm, tn = min(tm, m), min(tn, n)
    tk = kdim
    assert m % tm == 0 and n % tn == 0, (name, m, n, tm, tn)
    n_extra, n_out = len(extras), len(out_dtypes)
    n_after = 0 if after is None else 1

    if mode == "tn":
        a_spec = pl.BlockSpec((tk, tm), lambda j, i, k: (k, i))
        dims = (((0,), (0,)), ((), ()))
    else:
        a_spec = pl.BlockSpec((tm, tk), lambda j, i, k: (i, k))
        dims = (((1,), (0,)), ((), ())) if mode == "nn" else (((1,), (1,)), ((), ()))

    if b_blocks > 1 and mode == "nn":
        per = b.shape[2] // tn
        assert b.shape[2] % tn == 0
        b_spec = pl.BlockSpec((None, tk, tn), lambda j, i, k: (j // per, k, j % per))
    elif b_blocks > 1:
        b_spec = pl.BlockSpec((b_blocks, tn, b.shape[2]), lambda j, i, k: (0, j, 0))
    elif mode == "nt":
        b_spec = pl.BlockSpec((tn, tk), lambda j, i, k: (j, k))
    else:
        b_spec = pl.BlockSpec((tk, tn), lambda j, i, k: (k, j))

    tile_spec = pl.BlockSpec((tm, tn), lambda j, i, k: (i, j))
    if out_blocks > 1:
        ncols = n // out_blocks
        assert ncols % tn == 0
        oper = ncols // tn
        out_spec = pl.BlockSpec((None, tm, tn), lambda j, i, k: (j // oper, i, j % oper))
        out_shape = [jax.ShapeDtypeStruct((out_blocks, m, ncols), dt) for dt in out_dtypes]
    else:
        out_spec = tile_spec
        out_shape = [jax.ShapeDtypeStruct((m, n), dt) for dt in out_dtypes]

    def body(a_ref, b_ref, *rest):
        extra_refs = rest[:n_extra]
        out_refs = rest[n_extra + n_after:n_extra + n_after + n_out]
        if mode == "nt" and b_blocks > 1:
            cs = b.shape[2]
            acc = None
            for jb in range(b_blocks):
                prod = lax.dot_general(a_ref[:, jb * cs:(jb + 1) * cs].astype(BF16), b_ref[jb].astype(BF16), dims,
                                       preferred_element_type=F32)
                acc = prod if acc is None else acc + prod
        else:
            acc = lax.dot_general(a_ref[...].astype(BF16), b_ref[...].astype(BF16), dims, preferred_element_type=F32)
        tiles = (acc,) if epilogue is None else epilogue(acc, *[r[...] for r in extra_refs])
        for o_ref, t in zip(out_refs, tiles, strict=True):
            o_ref[...] = t.astype(o_ref.dtype)

    outs = pl.pallas_call(
        body,
        name=name,
        grid=(n // tn, m // tm, 1),
        in_specs=[a_spec, b_spec] + [tile_spec] * n_extra + [ANY] * n_after,
        out_specs=[out_spec] * n_out,
        out_shape=out_shape,
        compiler_params=_params(3),
    )(a, b, *extras, *([] if after is None else [after]))
    return outs[0] if n_out == 1 else outs


def _add_residual(acc, res):
    return (acc + res,)


def _matmul_column_blocks(a, b4, blocks, out, *, tm, name, after=None):
    m, kdim = a.shape
    nb, _, cols = b4.shape
    tm = min(tm, m)
    assert m % tm == 0

    def body(j_ref, a_ref, b_ref, *rest):
        rest[-1][...] = jnp.dot(a_ref[...], b_ref[...], preferred_element_type=F32)

    extra = ([] if out is None else [out]) + ([] if after is None else [after])
    n_blocks = blocks.shape[0]
    return pl.pallas_call(
        body, name=name,
        grid_spec=pltpu.PrefetchScalarGridSpec(
            num_scalar_prefetch=1, grid=(n_blocks, m // tm),
            in_specs=[pl.BlockSpec((tm, kdim), lambda j, i, blk: (i, 0)),
                      pl.BlockSpec((None, kdim, cols), lambda j, i, blk: (blk[j], 0, 0))] + [ANY] * len(extra),
            out_specs=pl.BlockSpec((tm, cols), lambda j, i, blk: (i, blk[j]))),
        out_shape=jax.ShapeDtypeStruct((m, nb * cols), F32),
        input_output_aliases={} if out is None else {3: 0},
        compiler_params=_params(2),
    )(blocks, a, b4, *extra)


def _wgrad_half(a, b, core, *, theirs, row_sharded, tm, tn, name, add=None, after=None):
    kdim, m = a.shape
    n = b.shape[1]
    rs, cs = (m // N_CHIPS, n) if row_sharded else (m, n // N_CHIPS)
    rh = rs // 2
    tm, tn = min(tm, rh), min(tn, cs)
    assert rh % tm == 0 and cs % tn == 0, (name, rh, cs, tm, tn)
    mh, per = rh // tm, cs // tn
    has_add = add is not None

    def half(c):
        return 1 - c[0] if theirs else c[0]

    if row_sharded:
        grid = (n // tn, N_CHIPS * mh)
        a_spec = pl.BlockSpec((kdim, tm), lambda j, r, c: (0, ((r // mh) * 2 + half(c)) * mh + r % mh))
        o_spec = pl.BlockSpec((None, tm, tn), lambda j, r, c: (r // mh, r % mh, j))
    else:
        grid = (n // tn, mh)
        a_spec = pl.BlockSpec((kdim, tm), lambda j, r, c: (0, half(c) * mh + r))
        o_spec = pl.BlockSpec((None, tm, tn), lambda j, r, c: (j // per, r, j % per))
    b_spec = pl.BlockSpec((kdim, tn), lambda j, r, c: (0, j))

    def body(c_ref, a_ref, b_ref, *rest):
        o_ref = rest[-1]
        acc = lax.dot_general(a_ref[...].astype(BF16), b_ref[...].astype(BF16), (((0,), (0,)), ((), ())),
                              preferred_element_type=F32)
        if has_add:
            acc = acc + rest[0][...].astype(F32)
        o_ref[...] = acc.astype(BF16)

    operands = [a, b] + ([add] if has_add else []) + ([] if after is None else [after])
    return pl.pallas_call(
        body, name=name,
        grid_spec=pltpu.PrefetchScalarGridSpec(
            num_scalar_prefetch=1, grid=grid,
            in_specs=[a_spec, b_spec] + ([o_spec] if has_add else []) + ([] if after is None else [ANY]),
            out_specs=o_spec),
        out_shape=jax.ShapeDtypeStruct((N_CHIPS, rh, cs), BF16),
        compiler_params=_params(2),
    )(core, *operands)


def _rstd(x):
    return lax.rsqrt(jnp.mean(x * x, axis=-1, keepdims=True) + RMS_EPS)


def _rmsnorm(x, g, name):
    s, d = x.shape
    tr = _row_tile(s, 512)

    def body(x_ref, g_ref, o_ref):
        xv = x_ref[...]
        o_ref[...] = (xv * _rstd(xv) * g_ref[...]).astype(BF16)

    return pl.pallas_call(
        body, name=name, grid=(s // tr,),
        in_specs=[pl.BlockSpec((tr, d), lambda i: (i, 0)), pl.BlockSpec((1, d), lambda i: (0, 0))],
        out_specs=pl.BlockSpec((tr, d), lambda i: (i, 0)),
        out_shape=jax.ShapeDtypeStruct((s, d), BF16),
        compiler_params=_params(1),
    )(x, g)


def _rmsnorm_bwd(dh, x, g, dres, name):
    s, d = x.shape
    tr = _row_tile(s, 512)
    has_res = dres is not None

    def body(*refs):
        if has_res:
            dh_ref, x_ref, g_ref, res_ref, dx_ref, dxb_ref, dg_ref = refs
        else:
            dh_ref, x_ref, g_ref, dx_ref, dxb_ref, dg_ref = refs
        xv = x_ref[...]
        dhv = dh_ref[...].astype(F32)
        r = _rstd(xv)
        xn = xv * r
        dhg = dhv * g_ref[...]
        dx = r * (dhg - xn * jnp.mean(dhg * xn, axis=-1, keepdims=True))
        if has_res:
            dx = dx + res_ref[...]
        dx_ref[...] = dx
        dxb_ref[...] = dx.astype(BF16)
        part = jnp.sum(dhv * xn, axis=0, keepdims=True)

        @pl.when(pl.program_id(0) == 0)
        def _():
            dg_ref[...] = part

        @pl.when(pl.program_id(0) > 0)
        def _():
            dg_ref[...] += part

    row = pl.BlockSpec((tr, d), lambda i: (i, 0))
    vec = pl.BlockSpec((1, d), lambda i: (0, 0))
    return pl.pallas_call(
        body, name=name, grid=(s // tr,),
        in_specs=[row, row, vec] + ([row] if has_res else []),
        out_specs=[row, row, vec],
        out_shape=[jax.ShapeDtypeStruct((s, d), F32), jax.ShapeDtypeStruct((s, d), BF16), jax.ShapeDtypeStruct((1, d), F32)],
        compiler_params=_params(1),
    )(*([dh, x, g] + ([dres] if has_res else [])))


def _loss_head(x3, g, target):
    s, d = x3.shape
    tr = _row_tile(s, 512)

    def body(x_ref, g_ref, t_ref, dx_ref, dxb_ref, sq_ref, dg_ref):
        xv = x_ref[...]
        gv = g_ref[...]
        r = _rstd(xv)
        xn = xv * r
        err = xn * gv - t_ref[...]
        dy = err * (1.0 / d)
        dyg = dy * gv
        dx = r * (dyg - xn * jnp.mean(dyg * xn, axis=-1, keepdims=True))
        dx_ref[...] = dx
        dxb_ref[...] = dx.astype(BF16)
        sq = jnp.sum(jnp.sum(err * err, axis=1, keepdims=True), axis=0, keepdims=True)
        sq = jnp.broadcast_to(sq, (1, 128))
        part = jnp.sum(dy * xn, axis=0, keepdims=True)

        @pl.when(pl.program_id(0) == 0)
        def _():
            sq_ref[...] = sq
            dg_ref[...] = part

        @pl.when(pl.program_id(0) > 0)
        def _():
            sq_ref[...] += sq
            dg_ref[...] += part

    row = pl.BlockSpec((tr, d), lambda i: (i, 0))
    vec = pl.BlockSpec((1, d), lambda i: (0, 0))
    return pl.pallas_call(
        body, name="loss_head", grid=(s // tr,),
        in_specs=[row, vec, row],
        out_specs=[row, row, pl.BlockSpec((1, 128), lambda i: (0, 0)), vec],
        out_shape=[jax.ShapeDtypeStruct((s, d), F32), jax.ShapeDtypeStruct((s, d), BF16),
                   jax.ShapeDtypeStruct((1, 128), F32), jax.ShapeDtypeStruct((1, d), F32)],
        compiler_params=_params(1),
    )(x3, g, target)


def _rope_tables(s):
    inv = 1.0 / (ROPE_THETA ** (jnp.arange(0, HEAD_DIM, 2, dtype=F32) / HEAD_DIM))
    ang = jnp.arange(s, dtype=F32)[:, None] * inv[None, :]
    cos, sin = jnp.cos(ang), jnp.sin(ang)
    return jnp.concatenate([cos, cos], axis=1), jnp.concatenate([-sin, sin], axis=1)


def _swap_halves(t):
    return pltpu.roll(t, HEAD_DIM // 2, 1)


def _rope_fwd(z, cos_t, sin_t):
    s = z.shape[0]
    tr = _row_tile(s, 256)

    def body(zq_ref, zk_ref, zv_ref, c_ref, s_ref, q_ref, k_ref, v_ref):
        c, sn = c_ref[...], s_ref[...]
        for hd in range(N_Q_HEADS):
            cols = slice(hd * HEAD_DIM, (hd + 1) * HEAD_DIM)
            t = zq_ref[:, cols]
            q_ref[:, cols] = (t * c + _swap_halves(t) * sn).astype(BF16)
        for hd in range(N_KV_HEADS):
            cols = slice(hd * HEAD_DIM, (hd + 1) * HEAD_DIM)
            t = zk_ref[:, cols]
            k_ref[:, cols] = (t * c + _swap_halves(t) * sn).astype(BF16)
        v_ref[...] = zv_ref[...].astype(BF16)

    tab = pl.BlockSpec((tr, HEAD_DIM), lambda i: (i, 0))
    return pl.pallas_call(
        body, name="rope_fwd", grid=(s // tr,),
        in_specs=[pl.BlockSpec((tr, ATTN_WIDTH), lambda i: (i, Q_OFF // ATTN_WIDTH)),
                  pl.BlockSpec((tr, KV_WIDTH), lambda i: (i, K_OFF // KV_WIDTH)),
                  pl.BlockSpec((tr, KV_WIDTH), lambda i: (i, V_OFF // KV_WIDTH)), tab, tab],
        out_specs=[pl.BlockSpec((tr, ATTN_WIDTH), lambda i: (i, 0)), pl.BlockSpec((tr, KV_WIDTH), lambda i: (i, 0)),
                   pl.BlockSpec((tr, KV_WIDTH), lambda i: (i, 0))],
        out_shape=[jax.ShapeDtypeStruct((s, ATTN_WIDTH), BF16), jax.ShapeDtypeStruct((s, KV_WIDTH), BF16),
                   jax.ShapeDtypeStruct((s, KV_WIDTH), BF16)],
        compiler_params=_params(1),
    )(z, z, z, cos_t, sin_t)


def _rope_bwd(dq_rot, dk_rot, dv, cos_t, sin_t, dz):
    s = dq_rot.shape[0]
    tr = _row_tile(s, 256)
    qkv_width = V_OFF + KV_WIDTH

    def body(dq_ref, dk_ref, dv_ref, c_ref, s_ref, dz_in_ref, o_ref):
        c, sn = c_ref[...], s_ref[...]
        for hd in range(N_Q_HEADS):
            t = dq_ref[:, hd * HEAD_DIM:(hd + 1) * HEAD_DIM]
            o_ref[:, Q_OFF + hd * HEAD_DIM:Q_OFF + (hd + 1) * HEAD_DIM] = (t * c + _swap_halves(t * sn)).astype(BF16)
        for hd in range(N_KV_HEADS):
            t = dk_ref[:, hd * HEAD_DIM:(hd + 1) * HEAD_DIM]
            o_ref[:, K_OFF + hd * HEAD_DIM:K_OFF + (hd + 1) * HEAD_DIM] = (t * c + _swap_halves(t * sn)).astype(BF16)
        o_ref[:, V_OFF:V_OFF + KV_WIDTH] = dv_ref[...].astype(BF16)

    tab = pl.BlockSpec((tr, HEAD_DIM), lambda i: (i, 0))
    wide = pl.BlockSpec((tr, ATTN_WIDTH), lambda i: (i, 0))
    narrow = pl.BlockSpec((tr, KV_WIDTH), lambda i: (i, 0))
    return pl.pallas_call(
        body, name="rope_bwd", grid=(s // tr,),
        in_specs=[wide, narrow, narrow, tab, tab, ANY],
        out_specs=pl.BlockSpec((tr, qkv_width), lambda i: (i, 0)),
        out_shape=jax.ShapeDtypeStruct(dz.shape, dz.dtype),
        input_output_aliases={5: 0},
        compiler_params=_params(1),
    )(dq_rot, dk_rot, dv, cos_t, sin_t, dz)


def _swa_band(i, s):
    return pl.multiple_of(jnp.clip((i - 1) * BLOCK, 0, s - BAND), BLOCK)


SWA_HEADS_PER_PASS = Q_GROUP


def _swa_probs(q_ref, k_ref, sink_ref, heads, start, valid):
    kv = heads[0] // Q_GROUP
    cols = slice(kv * HEAD_DIM, (kv + 1) * HEAD_DIM)
    kb = k_ref[pl.ds(start, BAND), cols]
    qg = jnp.concatenate([q_ref[:, hd * HEAD_DIM:(hd + 1) * HEAD_DIM] for hd in heads], axis=0)
    sc = lax.dot_general(qg, kb, (((1,), (1,)), ((), ())), preferred_element_type=F32) * ATTN_SCALE
    sc = jnp.where(valid, sc, NEG_INF)
    sk = jnp.concatenate([jnp.full((BLOCK, 1), sink_ref[hd], F32) for hd in heads], axis=0)
    mx = jnp.maximum(jnp.max(sc, axis=1, keepdims=True), sk)
    e = jnp.exp(sc - mx)
    es = jnp.exp(sk - mx)
    inv = 1.0 / (jnp.sum(e, axis=1, keepdims=True) + es)
    return qg, kb, e * inv, es * inv


def _swa_head_passes():
    return [list(range(h0, h0 + SWA_HEADS_PER_PASS)) for h0 in range(0, N_Q_HEADS, SWA_HEADS_PER_PASS)]


def _swa_valid(i, start):
    q_pos = i * BLOCK + lax.broadcasted_iota(jnp.int32, (BLOCK, 1), 0)
    q_pos = jnp.concatenate([q_pos] * SWA_HEADS_PER_PASS, axis=0)
    k_pos = start + lax.broadcasted_iota(jnp.int32, (1, BAND), 1)
    return jnp.abs(k_pos - q_pos) <= WINDOW


def _swa_fwd(q, k, v, sink):
    s = q.shape[0]
    assert s % BLOCK == 0 and s >= BAND

    def body(sink_ref, q_ref, k_ref, v_ref, o_ref):
        i = pl.program_id(0)
        start = _swa_band(i, s)
        valid = _swa_valid(i, start)
        for heads in _swa_head_passes():
            kv = heads[0] // Q_GROUP
            _, _, p, _ = _swa_probs(q_ref, k_ref, sink_ref, heads, start, valid)
            vb = v_ref[pl.ds(start, BAND), kv * HEAD_DIM:(kv + 1) * HEAD_DIM]
            o = jnp.dot(p.astype(BF16), vb, preferred_element_type=F32)
            for g, hd in enumerate(heads):
                o_ref[:, hd * HEAD_DIM:(hd + 1) * HEAD_DIM] = o[g * BLOCK:(g + 1) * BLOCK].astype(BF16)

    whole = pl.BlockSpec((s, KV_WIDTH), lambda i: (0, 0))
    blk = pl.BlockSpec((BLOCK, ATTN_WIDTH), lambda i: (i, 0))
    return pl.pallas_call(
        body, name="swa_fwd", grid=(s // BLOCK,),
        in_specs=[pl.BlockSpec(memory_space=pltpu.SMEM), blk, whole, whole],
        out_specs=blk,
        out_shape=jax.ShapeDtypeStruct((s, ATTN_WIDTH), BF16),
        compiler_params=_params(1),
    )(sink, q, k, v)


def _swa_bwd(q, k, v, d_out, sink):
    s = q.shape[0]

    def body(sink_ref, q_ref, k_ref, v_ref, do_ref, dq_ref, dk_ref, dv_ref, dsink_ref):
        i = pl.program_id(0)

        @pl.when(i == 0)
        def _():
            dk_ref[...] = jnp.zeros_like(dk_ref)
            dv_ref[...] = jnp.zeros_like(dv_ref)
            dsink_ref[...] = jnp.zeros_like(dsink_ref)

        start = _swa_band(i, s)
        valid = _swa_valid(i, start)
        for heads in _swa_head_passes():
            kv = heads[0] // Q_GROUP
            cols = slice(kv * HEAD_DIM, (kv + 1) * HEAD_DIM)
            qg, kb, p, p_sink = _swa_probs(q_ref, k_ref, sink_ref, heads, start, valid)
            vb = v_ref[pl.ds(start, BAND), cols]
            dog = jnp.concatenate([do_ref[:, hd * HEAD_DIM:(hd + 1) * HEAD_DIM] for hd in heads], axis=0)
            dp = lax.dot_general(dog, vb, (((1,), (1,)), ((), ())), preferred_element_type=F32)
            delta = jnp.sum(p * dp, axis=1, keepdims=True)
            ds = (p * (dp - delta) * ATTN_SCALE).astype(BF16)
            dqg = jnp.dot(ds, kb, preferred_element_type=F32)
            dk_ref[pl.ds(start, BAND), cols] += lax.dot_general(ds, qg, (((0,), (0,)), ((), ())), preferred_element_type=F32)
            dv_ref[pl.ds(start, BAND), cols] += lax.dot_general(p.astype(BF16), dog, (((0,), (0,)), ((), ())),
                                                                 preferred_element_type=F32)
            dsk = p_sink * delta
            for g, hd in enumerate(heads):
                dq_ref[:, hd * HEAD_DIM:(hd + 1) * HEAD_DIM] = dqg[g * BLOCK:(g + 1) * BLOCK]
                tot = jnp.sum(dsk[g * BLOCK:(g + 1) * BLOCK], axis=0, keepdims=True)
                dsink_ref[hd:hd + 1, :] -= jnp.broadcast_to(tot, (1, 128))

    whole = pl.BlockSpec((s, KV_WIDTH), lambda i: (0, 0))
    blk = pl.BlockSpec((BLOCK, ATTN_WIDTH), lambda i: (i, 0))
    return pl.pallas_call(
        body, name="swa_bwd", grid=(s // BLOCK,),
        in_specs=[pl.BlockSpec(memory_space=pltpu.SMEM), blk, whole, whole, blk],
        out_specs=[blk, whole, whole, pl.BlockSpec((N_Q_HEADS, 128), lambda i: (0, 0))],
        out_shape=[jax.ShapeDtypeStruct((s, ATTN_WIDTH), F32), jax.ShapeDtypeStruct((s, KV_WIDTH), F32),
                   jax.ShapeDtypeStruct((s, KV_WIDTH), F32), jax.ShapeDtypeStruct((N_Q_HEADS, 128), F32)],
        compiler_params=_params(1),
    )(sink, q, k, v, d_out)


CONV_CHUNK = 256


def _shift_rows(t, rows, down):
    n = t.shape[0]
    rolled = pltpu.roll(t, 1 if down else n - 1, 0)
    edge = 0 if down else n - 1
    return jnp.where(rows == edge, 0.0, rolled)


def _conv_specs(s):
    def z_spec(off):
        return pl.BlockSpec((s, CONV_CHUNK), lambda j, off=off: (0, off // CONV_CHUNK + j))
    chunk = pl.BlockSpec((s, CONV_CHUNK), lambda j: (0, j))
    w_spec = pl.BlockSpec((3, CONV_CHUNK), lambda j: (0, j))
    return z_spec(CU_OFF), z_spec(CB_OFF), z_spec(CC_OFF), chunk, w_spec


def _conv_fwd(z, conv_w):
    s = z.shape[0]
    cu_spec, cb_spec, cc_spec, chunk, w_spec = _conv_specs(s)

    def body(cu_ref, cb_ref, cc_ref, w_ref, o_ref):
        rows = lax.broadcasted_iota(jnp.int32, (s, 1), 0)
        t = cc_ref[...] * cu_ref[...]
        c3 = _shift_rows(t, rows, True) * w_ref[0:1, :] + t * w_ref[1:2, :] + _shift_rows(t, rows, False) * w_ref[2:3, :]
        o_ref[...] = (cb_ref[...] * c3).astype(BF16)

    return pl.pallas_call(
        body, name="conv_fwd", grid=(CONV_WIDTH // CONV_CHUNK,),
        in_specs=[cu_spec, cb_spec, cc_spec, w_spec],
        out_specs=chunk,
        out_shape=jax.ShapeDtypeStruct((s, CONV_WIDTH), BF16),
        compiler_params=_params(1),
    )(z, z, z, conv_w)


def _conv_bwd(z, conv_w, d_co, dz):
    s = z.shape[0]
    cu_spec, cb_spec, cc_spec, chunk, w_spec = _conv_specs(s)
    n_chunks = CONV_WIDTH // CONV_CHUNK
    offsets = (CU_OFF, CB_OFF, CC_OFF)

    def body(cu_ref, cb_ref, cc_ref, w_ref, d_ref, dz_in_ref, dz_ref, dw_ref, buf, sems):
        j = pl.program_id(0)

        def copies(j_at):
            return [pltpu.make_async_copy(buf.at[h], dz_ref.at[:, pl.ds(off + j_at * CONV_CHUNK, CONV_CHUNK)], sems.at[h])
                    for h, off in enumerate(offsets)]

        rows = lax.broadcasted_iota(jnp.int32, (s, 1), 0)
        cu, cc = cu_ref[...], cc_ref[...]
        t = cc * cu
        t_dn, t_up = _shift_rows(t, rows, True), _shift_rows(t, rows, False)
        c3 = t_dn * w_ref[0:1, :] + t * w_ref[1:2, :] + t_up * w_ref[2:3, :]
        d = d_ref[...]
        dc3 = d * cb_ref[...]
        dw_ref[0:1, :] = jnp.sum(dc3 * t_dn, axis=0, keepdims=True)
        dw_ref[1:2, :] = jnp.sum(dc3 * t, axis=0, keepdims=True)
        dw_ref[2:3, :] = jnp.sum(dc3 * t_up, axis=0, keepdims=True)
        dt = _shift_rows(dc3, rows, False) * w_ref[0:1, :] + dc3 * w_ref[1:2, :] + _shift_rows(dc3, rows, True) * w_ref[2:3, :]

        @pl.when(j > 0)
        def _():
            for cp in copies(j):
                cp.wait()

        buf[0] = (dt * cc).astype(BF16)
        buf[1] = (d * c3).astype(BF16)
        buf[2] = (dt * cu).astype(BF16)
        for cp in copies(j):
            cp.start()

        @pl.when(j == n_chunks - 1)
        def _():
            for cp in copies(j):
                cp.wait()

    return pl.pallas_call(
        body, name="conv_bwd", grid=(n_chunks,),
        in_specs=[cu_spec, cb_spec, cc_spec, w_spec, chunk, ANY],
        out_specs=[ANY, w_spec],
        out_shape=[jax.ShapeDtypeStruct(dz.shape, dz.dtype), jax.ShapeDtypeStruct((3, CONV_WIDTH), F32)],
        input_output_aliases={5: 0},
        scratch_shapes=[pltpu.VMEM((3, s, CONV_CHUNK), BF16), pltpu.SemaphoreType.DMA((3,))],
        compiler_params=_params(1),
    )(z, z, z, conv_w, d_co, dz)


GATE_CHUNK = 512


def _gate_specs(s, d, tr):
    n_chunks = d // GATE_CHUNK
    za = pl.BlockSpec((tr, GATE_CHUNK), lambda j, i: (i, GL_OFF // GATE_CHUNK + j))
    zc = pl.BlockSpec((tr, GATE_CHUNK), lambda j, i: (i, GL_OFF // GATE_CHUNK + n_chunks + j))
    ba = pl.BlockSpec((1, GATE_CHUNK), lambda j, i: (0, j))
    bc = pl.BlockSpec((1, GATE_CHUNK), lambda j, i: (0, n_chunks + j))
    tile = pl.BlockSpec((tr, GATE_CHUNK), lambda j, i: (i, j))
    return za, zc, ba, bc, tile


def _gate_fwd(z, b_gate, ya, yc):
    s, d = ya.shape
    tr = _row_tile(s, 512)
    za, zc, ba, bc, tile = _gate_specs(s, d, tr)

    def body(za_ref, zc_ref, ba_ref, bc_ref, ya_ref, yc_ref, o_ref):
        ga = jax.nn.sigmoid(za_ref[...] + ba_ref[...])
        gc = jax.nn.sigmoid(zc_ref[...] + bc_ref[...])
        o_ref[...] = (ga * ya_ref[...] + gc * yc_ref[...]).astype(BF16)

    return pl.pallas_call(
        body, name="gate_fwd", grid=(d // GATE_CHUNK, s // tr),
        in_specs=[za, zc, ba, bc, tile, tile],
        out_specs=tile,
        out_shape=jax.ShapeDtypeStruct((s, d), BF16),
        compiler_params=_params(2),
    )(z, z, b_gate, b_gate, ya, yc)


def _gate_bwd(z, b_gate, ya, yc, dmix):
    s, d = ya.shape
    tr = _row_tile(s, 512)
    za, zc, ba, bc, tile = _gate_specs(s, d, tr)
    vec = pl.BlockSpec((1, GATE_CHUNK), lambda j, i: (0, j))
    n_rows = s // tr
    in_width = z.shape[1]

    def body(za_ref, zc_ref, ba_ref, bc_ref, ya_ref, yc_ref, dm_ref, dya_ref, dyc_ref, dz_ref, dba_ref, dbc_ref, buf, sems):
        j, i = pl.program_id(0), pl.program_id(1)

        def copies(j_at, i_at):
            rows = pl.ds(i_at * tr, tr)
            return [pltpu.make_async_copy(buf.at[h], dz_ref.at[rows, pl.ds(GL_OFF + h * d + j_at * GATE_CHUNK, GATE_CHUNK)],
                                          sems.at[h]) for h in range(2)]

        ga = jax.nn.sigmoid(za_ref[...] + ba_ref[...])
        gc = jax.nn.sigmoid(zc_ref[...] + bc_ref[...])
        dm = dm_ref[...]
        dya_ref[...] = (dm * ga).astype(BF16)
        dyc_ref[...] = (dm * gc).astype(BF16)
        dla = dm * ya_ref[...] * ga * (1.0 - ga)
        dlc = dm * yc_ref[...] * gc * (1.0 - gc)

        @pl.when(j * n_rows + i > 0)
        def _():
            for cp in copies(j, i):
                cp.wait()

        buf[0] = dla.astype(BF16)
        buf[1] = dlc.astype(BF16)
        for cp in copies(j, i):
            cp.start()

        @pl.when((j == d // GATE_CHUNK - 1) & (i == n_rows - 1))
        def _():
            for cp in copies(j, i):
                cp.wait()

        pa = jnp.sum(dla, axis=0, keepdims=True)
        pc = jnp.sum(dlc, axis=0, keepdims=True)

        @pl.when(i == 0)
        def _():
            dba_ref[...] = pa
            dbc_ref[...] = pc

        @pl.when(i > 0)
        def _():
            dba_ref[...] += pa
            dbc_ref[...] += pc

    big = jax.ShapeDtypeStruct((s, d), BF16)
    small = jax.ShapeDtypeStruct((1, d), F32)
    return pl.pallas_call(
        body, name="gate_bwd", grid=(d // GATE_CHUNK, n_rows),
        in_specs=[za, zc, ba, bc, tile, tile, tile],
        out_specs=[tile, tile, ANY, vec, vec],
        out_shape=[big, big, jax.ShapeDtypeStruct((s, in_width), BF16), small, small],
        scratch_shapes=[pltpu.VMEM((2, tr, GATE_CHUNK), BF16), pltpu.SemaphoreType.DMA((2,))],
        compiler_params=_params(2),
    )(z, z, b_gate, b_gate, ya, yc, dmix)


def _cross_probs(q_ref, kv_ref, hd):
    cols = slice(hd * HEAD_DIM, (hd + 1) * HEAD_DIM)
    qh = q_ref[:, cols]
    kh = kv_ref[:, cols]
    sc = lax.dot_general(qh, kh, (((1,), (1,)), ((), ())), preferred_element_type=F32) * ATTN_SCALE
    e = jnp.exp(sc - jnp.max(sc, axis=1, keepdims=True))
    return qh, kh, e * (1.0 / jnp.sum(e, axis=1, keepdims=True))


def _cross_fwd(qc, kvc):
    s = qc.shape[0]
    n_mem = kvc.shape[0]
    tq = _row_tile(s, 256)

    def body(q_ref, kv_ref, o_ref):
        for hd in range(MEM_HEADS):
            _, _, p = _cross_probs(q_ref, kv_ref, hd)
            vh = kv_ref[:, MEM_WIDTH + hd * HEAD_DIM:MEM_WIDTH + (hd + 1) * HEAD_DIM]
            o_ref[:, hd * HEAD_DIM:(hd + 1) * HEAD_DIM] = jnp.dot(p.astype(BF16), vh, preferred_element_type=F32).astype(BF16)

    return pl.pallas_call(
        body, name="cross_fwd", grid=(s // tq,),
        in_specs=[pl.BlockSpec((tq, MEM_WIDTH), lambda i: (i, 0)), pl.BlockSpec((n_mem, 2 * MEM_WIDTH), lambda i: (0, 0))],
        out_specs=pl.BlockSpec((tq, MEM_WIDTH), lambda i: (i, 0)),
        out_shape=jax.ShapeDtypeStruct((s, MEM_WIDTH), BF16),
        compiler_params=_params(1),
    )(qc, kvc)


def _cross_bwd(qc, kvc, d_out):
    s = qc.shape[0]
    n_mem = kvc.shape[0]
    tq = _row_tile(s, 256)

    def body(q_ref, kv_ref, do_ref, dq_ref, dkv_ref):
        @pl.when(pl.program_id(0) == 0)
        def _():
            dkv_ref[...] = jnp.zeros_like(dkv_ref)

        for hd in range(MEM_HEADS):
            cols = slice(hd * HEAD_DIM, (hd + 1) * HEAD_DIM)
            vcols = slice(MEM_WIDTH + hd * HEAD_DIM, MEM_WIDTH + (hd + 1) * HEAD_DIM)
            qh, kh, p = _cross_probs(q_ref, kv_ref, hd)
            doh = do_ref[:, cols]
            dp = lax.dot_general(doh, kv_ref[:, vcols], (((1,), (1,)), ((), ())), preferred_element_type=F32)
            ds = (p * (dp - jnp.sum(p * dp, axis=1, keepdims=True)) * ATTN_SCALE).astype(BF16)
            dq_ref[:, cols] = jnp.dot(ds, kh, preferred_element_type=F32).astype(BF16)
            dkv_ref[:, cols] += lax.dot_general(ds, qh, (((0,), (0,)), ((), ())), preferred_element_type=F32)
            dkv_ref[:, vcols] += lax.dot_general(p.astype(BF16), doh, (((0,), (0,)), ((), ())), preferred_element_type=F32)

    qspec = pl.BlockSpec((tq, MEM_WIDTH), lambda i: (i, 0))
    kvspec = pl.BlockSpec((n_mem, 2 * MEM_WIDTH), lambda i: (0, 0))
    return pl.pallas_call(
        body, name="cross_bwd", grid=(s // tq,),
        in_specs=[qspec, kvspec, qspec],
        out_specs=[qspec, kvspec],
        out_shape=[jax.ShapeDtypeStruct((s, MEM_WIDTH), BF16), jax.ShapeDtypeStruct((n_mem, 2 * MEM_WIDTH), F32)],
        compiler_params=_params(1),
    )(qc, kvc, d_out)


def _swiglu_fwd(up, gate):
    sg = jax.nn.sigmoid(gate)
    silu = gate * sg
    return silu * up, up * (sg * (1.0 + gate * (1.0 - sg))), silu


def _swiglu_bwd(d_act, dact_dgate, dact_dup):
    return d_act * dact_dgate.astype(F32), d_act * dact_dup.astype(F32)


GATHER_GROUPS = {"in": ("w_in", "conv_w"), "mid": ("w_attn_out", "w_conv_out", "w_o", "w_cq", "w_ckv", "w_co"),
                 "gate": ("w_gate",), "up": ("w_up",), "down": ("w_down",)}


def _local_step(xs, mems, target, small, fetch, reduce):
    s, d = xs.shape
    w4 = {}
    cos_t, sin_t = _rope_tables(s)

    def near(group, done, then, after):
        waits = [("direct", group)] + ([("pass_near", done), ("pass_far", done)] if done else [])
        starts = [("forward", group), ("pass_near", group)] + [("direct", g) for g in then]
        tok = fetch.step("gather_near_" + group, waits, starts, after)
        if done:
            w4.update(fetch.arrays(done))
        return tok

    def far(group, then, after):
        return fetch.step("gather_far_" + group, [("forward", group)], [("pass_far", group)] + [("direct", g) for g in then], after)

    def last(group, after):
        tok = fetch.step("gather_done_" + group, [("pass_near", group), ("pass_far", group)], [], after)
        w4.update(fetch.arrays(group))
        return tok

    h = _rmsnorm(xs, small["g_mix"], "norm_mix")
    slots_filled = [a for g in ("gate", "up", "down") for a in fetch.arrays(g).values()]
    chip_x, chip_y = reduce.place[0] // 2, reduce.place[0] % 2
    own_block = jnp.stack([2 * chip_x + chip_y]).astype(jnp.int32)
    near_blocks = jnp.stack([2 * (1 - chip_x) + chip_y, 2 * chip_x + (1 - chip_y)]).astype(jnp.int32)
    far_block = jnp.stack([2 * (1 - chip_x) + (1 - chip_y)]).astype(jnp.int32)
    z = _matmul_column_blocks(h, fetch.arrays("in")["w_in"], own_block, None, tm=512, name="in_proj_own")
    tok = near("in", None, ["mid"], [z] + slots_filled)
    memn = _rmsnorm(mems, small["g_mem"], "norm_mem")
    tok = fetch.step("gather_near_done_in", [("pass_near", "in")], [], [tok, cos_t, sin_t, memn])
    z = _matmul_column_blocks(h, fetch.arrays("in")["w_in"], near_blocks, z, tm=1024, name="in_proj_near", after=tok)
    tok = far("in", [], z)
    tok = fetch.step("gather_done_in", [("pass_far", "in")], [], tok)
    w4.update(fetch.arrays("in"))
    z = _matmul_column_blocks(h, w4["w_in"], far_block, z, tm=1024, name="in_proj_far", after=tok)
    conv4 = w4["conv_w"]
    conv_w = conv4[:, :3, :].transpose(1, 0, 2).reshape(3, N_CHIPS * conv4.shape[2])
    c_in = w4["w_in"].shape[2]
    tok = near("mid", None, ["gate"], z)
    q_rot, k_rot, v_b = _rope_fwd(z, cos_t, sin_t)
    attn = _swa_fwd(q_rot, k_rot, v_b, small["sink"])
    co = _conv_fwd(z, conv_w)
    tok = far("mid", ["up"], attn)
    tok = last("mid", tok)
    w_o = w4["w_o"].reshape(-1, w4["w_o"].shape[-1])
    c_d = w4["w_attn_out"].shape[2]
    ya = _matmul(attn, w4["w_attn_out"], mode="nn", tm=2048, tn=c_d, out_dtypes=[F32], name="attn_out_proj",
                 b_blocks=N_CHIPS, after=tok)
    yc = _matmul(co, w4["w_conv_out"], mode="nn", tm=2048, tn=c_d, out_dtypes=[F32], name="conv_out_proj",
                 b_blocks=N_CHIPS)
    mix = _gate_fwd(z, small["b_gate"], ya, yc)
    x1 = _matmul(mix, w_o, mode="nn", tm=1024, tn=1024, out_dtypes=[F32], name="mix_out_proj", extras=[xs],
                 epilogue=_add_residual)
    tok = near("gate", None, ["down"], x1)
    w_cq = w4["w_cq"].reshape(-1, w4["w_cq"].shape[-1])
    w_ckv = w4["w_ckv"].reshape(-1, w4["w_ckv"].shape[-1])
    hc = _rmsnorm(x1, small["g_cross"], "norm_cross")
    qc = _matmul(hc, w_cq, mode="nn", tm=2048, tn=MEM_WIDTH, out_dtypes=[BF16], name="cross_q_proj", after=tok)
    kvc = _matmul(memn, w_ckv, mode="nn", tm=256, tn=2 * MEM_WIDTH, out_dtypes=[BF16], name="cross_kv_proj")
    oc = _cross_fwd(qc, kvc)
    tok = far("gate", [], oc)
    x2 = _matmul(oc, w4["w_co"], mode="nn", tm=2048, tn=c_d, out_dtypes=[F32], name="cross_out_proj",
                 extras=[x1], epilogue=_add_residual, b_blocks=N_CHIPS, after=tok)
    hf = _rmsnorm(x2, small["g_ffn"], "norm_ffn")
    tok = near("up", "gate", [], hf)
    c_ff = w4["w_gate"].shape[2]
    gate = _matmul(hf, w4["w_gate"], mode="nn", tm=1024, tn=c_ff, out_dtypes=[F32], name="ffn_gate_proj", b_blocks=N_CHIPS,
                   after=tok)
    tok = far("up", [], gate)
    tok = near("down", "up", [], tok)
    act, dact_dgate, dact_dup = _matmul(hf, w4["w_up"], mode="nn", tm=1024, tn=c_ff, out_dtypes=[BF16, BF16, BF16],
                                        name="ffn_up_proj", extras=[gate], epilogue=_swiglu_fwd, b_blocks=N_CHIPS, after=tok)
    tok = far("down", [], act)
    last("down", tok)
    w_down = w4["w_down"].reshape(-1, w4["w_down"].shape[-1])
    x3 = _matmul(act, w_down, mode="nn", tm=512, tn=512, out_dtypes=[F32], name="ffn_down_proj", extras=[x2],
                 epilogue=_add_residual)
    dx3, dx3b, sq, dg_final = _loss_head(x3, small["g_final"], target)

    da, du = _matmul(dx3b, w_down, mode="nt", tm=1024, tn=c_ff, out_dtypes=[BF16, BF16], name="ffn_down_bwd",
                     extras=[dact_dgate, dact_dup], epilogue=_swiglu_bwd)
    core = reduce.core
    ffn_shape = dict(row_sharded=False, tm=1024, tn=c_ff)
    g_down = _matmul(act, dx3b, mode="tn", tm=c_ff, tn=1024, out_dtypes=[BF16], name="ffn_down_wgrad")
    tok = reduce.add("down", {"w_down": g_down}, da)
    t_gate = _wgrad_half(hf, da, core, theirs=True, name="ffn_gate_wgrad_theirs", after=tok, **ffn_shape)
    tok = reduce.step("down", t_gate)
    t_up = _wgrad_half(hf, du, core, theirs=True, name="ffn_up_wgrad_theirs", after=tok, **ffn_shape)
    tok = reduce.send("ffn", {"w_gate": t_gate, "w_up": t_up}, dx3b)
    dhf = _matmul(da, w4["w_gate"], mode="nt", tm=512, tn=1024, out_dtypes=[F32], name="ffn_gate_bwd", b_blocks=N_CHIPS,
                  after=tok)
    got = reduce.received("ffn", dhf)
    p_gate = _wgrad_half(hf, da, core, theirs=False, name="ffn_gate_wgrad_mine", add=got["w_gate"], **ffn_shape)
    p_up = _wgrad_half(hf, du, core, theirs=False, name="ffn_up_wgrad_mine", add=got["w_up"], **ffn_shape)
    tok = reduce.add_parts("ffn", {"w_gate": p_gate, "w_up": p_up})
    dhf = _matmul(du, w4["w_up"], mode="nt", tm=512, tn=1024, out_dtypes=[F32], name="ffn_up_bwd", extras=[dhf],
                  epilogue=_add_residual, b_blocks=N_CHIPS, after=tok)
    tok = reduce.step("down", dhf)
    dx2, dx2b, dg_ffn = _rmsnorm_bwd(dhf, x2, small["g_ffn"], dx3, "norm_ffn_bwd")

    d_oc = _matmul(dx2b, w4["w_co"], mode="nt", tm=1024, tn=MEM_WIDTH, out_dtypes=[BF16], name="cross_out_bwd",
                   b_blocks=N_CHIPS, after=tok)
    g_co = _matmul(oc, dx2b, mode="tn", tm=MEM_WIDTH, tn=c_d, out_dtypes=[BF16], name="cross_out_wgrad", out_blocks=N_CHIPS)
    tok = reduce.step("down", g_co)
    dqc, dkvc = _cross_bwd(qc, kvc, d_oc)
    g_cq = _matmul(hc, dqc, mode="tn", tm=1024, tn=MEM_WIDTH, out_dtypes=[BF16], name="cross_q_wgrad", after=tok)
    dhc = _matmul(dqc, w_cq, mode="nt", tm=1024, tn=1024, out_dtypes=[F32], name="cross_q_bwd")
    g_ckv = _matmul(memn, dkvc, mode="tn", tm=1024, tn=2 * MEM_WIDTH, out_dtypes=[BF16], name="cross_kv_wgrad")
    dx1, dx1b, dg_cross = _rmsnorm_bwd(dhc, x1, small["g_cross"], dx2, "norm_cross_bwd")

    dmix = _matmul(dx1b, w_o, mode="nt", tm=1024, tn=1024, out_dtypes=[F32], name="mix_out_bwd")
    g_o = _matmul(mix, dx1b, mode="tn", tm=1024, tn=1024, out_dtypes=[BF16], name="mix_out_wgrad")
    dya, dyc, dz, db_a, db_c = _gate_bwd(z, small["b_gate"], ya, yc, dmix)
    d_attn = _matmul(dya, w4["w_attn_out"], mode="nt", tm=1024, tn=ATTN_WIDTH, out_dtypes=[BF16], name="attn_out_bwd",
                     b_blocks=N_CHIPS)
    g_ao = _matmul(attn, dya, mode="tn", tm=ATTN_WIDTH, tn=c_d, out_dtypes=[BF16], name="attn_out_wgrad", out_blocks=N_CHIPS)
    d_co = _matmul(dyc, w4["w_conv_out"], mode="nt", tm=1024, tn=CONV_WIDTH, out_dtypes=[F32], name="conv_out_bwd",
                   b_blocks=N_CHIPS)
    g_cvo = _matmul(co, dyc, mode="tn", tm=CONV_WIDTH, tn=c_d, out_dtypes=[BF16], name="conv_out_wgrad", out_blocks=N_CHIPS)
    tok = reduce.step("ffn", g_cvo)
    tok = reduce.add("mid", {"w_co": g_co, "w_cq": g_cq, "w_ckv": g_ckv, "w_o": g_o, "w_attn_out": g_ao, "w_conv_out": g_cvo}, tok)
    dz, d_conv_w = _conv_bwd(z, conv_w, d_co, dz)
    dq_rot, dk_rot, dv, dsink = _swa_bwd(q_rot, k_rot, v_b, d_attn, small["sink"])
    tok = reduce.step("mid", dq_rot)
    dz = _rope_bwd(dq_rot, dk_rot, dv, cos_t, sin_t, dz)
    in_shape = dict(row_sharded=False, tm=1024, tn=c_in)
    t_in = _wgrad_half(h, dz, core, theirs=True, name="in_proj_wgrad_theirs", after=tok, **in_shape)
    tok = reduce.send("in", {"w_in": t_in}, dk_rot)
    tok = reduce.step("ffn", tok, count=1)
    dmemn = _matmul(dkvc, w_ckv, mode="nt", tm=256, tn=1024, out_dtypes=[F32], name="cross_kv_bwd", after=tok)
    _, _, dg_mem = _rmsnorm_bwd(dmemn, mems, small["g_mem"], None, "norm_mem_bwd")
    got = reduce.received("in", dg_mem)
    p_in = _wgrad_half(h, dz, core, theirs=False, name="in_proj_wgrad_mine", add=got["w_in"], **in_shape)
    tok = reduce.add_parts("in", {"w_in": p_in})
    tok = reduce.step("ffn", tok)
    tok = reduce.step("mid", tok)
    dh = _matmul(dz, w4["w_in"], mode="nt", tm=512, tn=512, out_dtypes=[F32], name="in_proj_bwd", b_blocks=N_CHIPS,
                 after=tok)
    grad_x, _, dg_mix = _rmsnorm_bwd(dh, xs, small["g_mix"], dx1, "norm_mix_bwd")

    small_grads = {
        "g_mix": dg_mix, "sink": dsink[:, 0], "b_gate": jnp.concatenate([db_a, db_c], axis=1), "g_cross": dg_cross,
        "g_mem": dg_mem, "g_ffn": dg_ffn, "g_final": dg_final, "conv_w": d_conv_w,
    }
    return sq, grad_x, small_grads


def _pair_sum(g4, ra, core, name):
    nb, rs, cs = g4.shape
    rh = rs // 2
    tr = _row_tile(rh, 256)
    per = rh // tr

    def body(c_ref, g_ref, r_ref, o_ref):
        o_ref[...] = (g_ref[...].astype(F32) + r_ref[...].astype(F32)).astype(BF16)

    plain = pl.BlockSpec((None, tr, cs), lambda j, i, c: (j, i, 0))
    return pl.pallas_call(
        body, name=name,
        grid_spec=pltpu.PrefetchScalarGridSpec(
            num_scalar_prefetch=1, grid=(nb, per),
            in_specs=[pl.BlockSpec((None, tr, cs), lambda j, i, c: (j, c[0] * per + i, 0)), plain],
            out_specs=plain),
        out_shape=jax.ShapeDtypeStruct((nb, rh, cs), BF16),
        compiler_params=_params(2),
    )(core, g4, ra)


def _adamw_update(w, g, m, v):
    nm = ADAM_B1 * m + (1.0 - ADAM_B1) * g
    nv = ADAM_B2 * v + (1.0 - ADAM_B2) * (g * g)
    m_hat = nm / ADAM_C1
    v_hat = nv / ADAM_C2
    return -ADAM_LR * (m_hat / (jnp.sqrt(v_hat) + ADAM_EPS) + ADAM_WD * w), nm, nv


def _adamw_own_half(w, m, v, parts, rc, place, name, after=None):
    rows, cols = w.shape
    rh = rows // 2
    tr = _row_tile(rh, 256)
    per = rh // tr

    def body(p_ref, w_ref, m_ref, v_ref, own_ref, r_ref, *rest):
        gx_ref, g_ref, d_ref, nm_ref, nv_ref = rest[-5:]
        g = own_ref[...].astype(F32)
        for j in range(rc.shape[0]):
            g = g + r_ref[j].astype(F32)
        gx_ref[...] = g
        g_ref[...] = g
        d_ref[...], nm_ref[...], nv_ref[...] = _adamw_update(w_ref[...], g, m_ref[...], v_ref[...])

    mine = pl.BlockSpec((tr, cols), lambda i, p: (p[1] * per + i, 0))
    shape = jax.ShapeDtypeStruct((rows, cols), F32)
    return pl.pallas_call(
        body, name=name,
        grid_spec=pltpu.PrefetchScalarGridSpec(
            num_scalar_prefetch=1, grid=(per,),
            in_specs=[mine, mine, mine, pl.BlockSpec((None, tr, cols), lambda i, p: (p[0], i, 0)),
                      pl.BlockSpec((rc.shape[0], tr, cols), lambda i, p: (0, i, 0))] + ([] if after is None else [ANY]),
            out_specs=[mine] * 5),
        out_shape=[shape] * 5,
        compiler_params=_params(1),
    )(place, w, m, v, parts, rc, *([] if after is None else [after]))


def _adamw_other_half(w, m, v, g_exchanged, g, delta, new_m, new_v, place, name, after=None):
    rows, cols = w.shape
    rh = rows // 2
    tr = _row_tile(rh, 256)
    per = rh // tr

    def body(p_ref, w_ref, m_ref, v_ref, gx_ref, *rest):
        g_ref, d_ref, nm_ref, nv_ref = rest[-4:]
        gv = gx_ref[...]
        g_ref[...] = gv
        d_ref[...], nm_ref[...], nv_ref[...] = _adamw_update(w_ref[...], gv, m_ref[...], v_ref[...])

    other = pl.BlockSpec((tr, cols), lambda i, p: ((1 - p[1]) * per + i, 0))
    shape = jax.ShapeDtypeStruct((rows, cols), F32)
    n_after = 0 if after is None else 1
    return pl.pallas_call(
        body, name=name,
        grid_spec=pltpu.PrefetchScalarGridSpec(
            num_scalar_prefetch=1, grid=(per,),
            in_specs=[other] * 4 + [ANY] * (4 + n_after),
            out_specs=[other] * 4),
        out_shape=[shape] * 4,
        input_output_aliases={5: 0, 6: 1, 7: 2, 8: 3},
        compiler_params=_params(1),
    )(place, w, m, v, g_exchanged, g, delta, new_m, new_v, *([] if after is None else [after]))


def _cast_to_slot(w, place, dtype, name, after=None):
    rows, cols = w.shape
    tr = _row_tile(rows, 1024)

    def body(p_ref, w_ref, *rest):
        o_ref = rest[-1]
        o_ref[...] = w_ref[...].astype(dtype)

    return pl.pallas_call(
        body, name=name,
        grid_spec=pltpu.PrefetchScalarGridSpec(
            num_scalar_prefetch=1, grid=(rows // tr,),
            in_specs=[pl.BlockSpec((tr, cols), lambda i, p: (i, 0))] + ([] if after is None else [ANY]),
            out_specs=pl.BlockSpec((None, tr, cols), lambda i, p: (p[0], i, 0))),
        out_shape=jax.ShapeDtypeStruct((N_CHIPS, rows, cols), dtype),
        compiler_params=_params(1),
    )(place, w, *([] if after is None else [after]))


def _adamw(w, g, m, v, name, after=None):
    rows, cols = w.shape
    tr = _row_tile(rows, 256)

    def body(w_ref, g_ref, m_ref, v_ref, *rest):
        go_ref, d_ref, nm_ref, nv_ref = rest[-4:]
        gv = g_ref[...]
        go_ref[...] = gv
        d_ref[...], nm_ref[...], nv_ref[...] = _adamw_update(w_ref[...], gv, m_ref[...], v_ref[...])

    tile = pl.BlockSpec((tr, cols), lambda i: (i, 0))
    shape = jax.ShapeDtypeStruct((rows, cols), F32)
    return pl.pallas_call(
        body, name=name, grid=(rows // tr,),
        in_specs=[tile] * 4 + ([] if after is None else [ANY]), out_specs=[tile] * 4, out_shape=[shape] * 4,
        compiler_params=_params(1),
    )(w, g, m, v, *([] if after is None else [after]))


def _mesh_pos():
    return lax.axis_index("x"), lax.axis_index("y"), lax.axis_index("c")


def _other_chips(x, y):
    return [(1 - x, y), (x, 1 - y), (1 - x, 1 - y)]


def _half_rows(ref, which):
    rh = ref.shape[-2] // 2
    return ref.at[pl.ds(which * rh, rh), :]


def _remote(src, dst, send_sems, recv_sems, sem, to):
    return pltpu.make_async_remote_copy(src_ref=src, dst_ref=dst, send_sem=send_sems.at[sem], recv_sem=recv_sems.at[sem],
                                        device_id=to, device_id_type=MESH)


HBM = pl.BlockSpec(memory_space=pltpu.HBM)
SEM = pl.BlockSpec(memory_space=pltpu.SEMAPHORE)
DATAFLOW_EFFECT = pltpu.SideEffectType.DATAFLOW_SIDE_EFFECTING


def _in_hbm(arrays):
    return [pltpu.with_memory_space_constraint(a, pltpu.HBM) for a in arrays]


def _hbm_like(arrays):
    return [pltpu.HBM(a.shape, a.dtype) for a in arrays]


GATHER_COPIES_PER_ARRAY = {"direct": 2, "forward": 2, "pass_near": 2, "pass_far": 1}


def _gather_copies(kind, refs, x, y, c):
    me, near_x, near_y, far = 2 * x + y, 2 * (1 - x) + y, 2 * x + (1 - y), 2 * (1 - x) + (1 - y)
    to_x, to_y, sibling = (1 - x, y, c), (x, 1 - y, c), (x, y, 1 - c)
    out = []
    for ref in refs:
        rh = ref.shape[1] // 2
        rq = rh // 2

        def half(chip, ref=ref, rh=rh):
            return ref.at[chip, pl.ds(c * rh, rh), :]

        def quarter(chip, q, ref=ref, rh=rh, rq=rq):
            return ref.at[chip, pl.ds(c * rh + q * rq, rq), :]

        if kind == "direct":
            out += [(half(me), half(me), to_x), (half(me), half(me), to_y)]
        elif kind == "forward":
            out += [(quarter(near_x, 0), quarter(near_x, 0), to_y), (quarter(near_y, 1), quarter(near_y, 1), to_x)]
        elif kind == "pass_near":
            out += [(half(near_x), half(near_x), sibling), (half(near_y), half(near_y), sibling)]
        else:
            assert kind == "pass_far"
            out += [(half(far), half(far), sibling)]
    return out


def _gather_step(name, bufs, waits, starts, after):
    nb, nw, ns = len(bufs), len(waits), len(starts)
    after = [] if after is None else list(after) if isinstance(after, (list, tuple)) else [after]
    n_after = len(after)

    def body(*refs):
        ins = refs[:nb]
        wait_sems = refs[nb:nb + 2 * nw]
        start_sems = refs[nb + 2 * nw + n_after:nb + 2 * nw + n_after + 2 * ns]
        token = refs[-1]
        x, y, c = _mesh_pos()
        for j, (kind, idxs, _, _) in enumerate(waits):
            for i, (s_ref, d_ref, to) in enumerate(_gather_copies(kind, [ins[t] for t in idxs], x, y, c)):
                came = _remote(s_ref, d_ref, wait_sems[2 * j], wait_sems[2 * j + 1], i, to)
                came.wait_recv()
                came.wait_send()
        for j, (kind, idxs) in enumerate(starts):
            for i, (s_ref, d_ref, to) in enumerate(_gather_copies(kind, [ins[t] for t in idxs], x, y, c)):
                _remote(s_ref, d_ref, start_sems[2 * j], start_sems[2 * j + 1], i, to).start()
        token[...] = jnp.zeros_like(token)

    sems = []
    for kind, idxs in starts:
        sems += [pltpu.SemaphoreType.DMA((GATHER_COPIES_PER_ARRAY[kind] * len(idxs),))] * 2
    operands = _in_hbm(bufs) + [sem for w in waits for sem in w[2:]] + after
    outs = pl.pallas_call(
        body, name=name,
        in_specs=[HBM] * nb + [SEM] * (2 * nw) + [ANY] * n_after,
        out_specs=[SEM] * (2 * ns) + [HBM] * nb + [pl.BlockSpec(memory_space=pltpu.VMEM)],
        out_shape=sems + _hbm_like(bufs) + [jax.ShapeDtypeStruct((8, 128), F32)],
        input_output_aliases={i: 2 * ns + i for i in range(nb)},
        compiler_params=pltpu.CompilerParams(has_side_effects=DATAFLOW_EFFECT),
    )(*operands)
    return outs[2 * ns:2 * ns + nb], [(outs[2 * j], outs[2 * j + 1]) for j in range(ns)], outs[-1]


class _Gather:
    def __init__(self, groups):
        self.groups = groups
        self.bufs = {}
        self.in_flight = {}

    def put(self, slotted):
        self.bufs.update(slotted)

    def step(self, name, waits, starts, after=None):
        names = []
        for _, group in list(waits) + list(starts):
            names += [n for n in self.groups[group] if n not in names]
        index = {n: i for i, n in enumerate(names)}

        def members(group):
            return [index[n] for n in self.groups[group]]

        wait_args = [(kind, members(group)) + self.in_flight.pop((kind, group)) for kind, group in waits]
        start_args = [(kind, members(group)) for kind, group in starts]
        bufs, sems, token = _gather_step(name, [self.bufs[n] for n in names], wait_args, start_args, after)
        self.bufs.update(zip(names, bufs))
        for (kind, group), pair in zip(starts, sems):
            self.in_flight[(kind, group)] = pair
        return token

    def arrays(self, group):
        return {n: self.bufs[n] for n in self.groups[group]}


def _sibling_halves_copies(srcs, dsts, x, y, c):
    out = []
    for s_ref, d_ref in zip(srcs, dsts, strict=True):
        rh = s_ref.shape[1] // 2
        out.append((s_ref.at[:, pl.ds((1 - c) * rh, rh), :], d_ref, (x, y, 1 - c)))
    return out


def _to_sibling_copies(srcs, dsts, x, y, c):
    return [(s_ref, d_ref, (x, y, 1 - c)) for s_ref, d_ref in zip(srcs, dsts, strict=True)]


def _chip_copies(srcs, dsts, x, y, c):
    out = []
    for s_ref, d_ref in zip(srcs, dsts, strict=True):
        for k, (px, py) in enumerate(_other_chips(x, y)):
            out.append((s_ref.at[2 * px + py], d_ref.at[k], (px, py, c)))
    return out


def _join_copies(srcs, dsts, x, y, c):
    out = []
    for s_ref in srcs:
        mine = _half_rows(s_ref, c)
        out.append((mine, mine, (x, y, 1 - c)))
    return out


def _exchange_start(copies_fn, n_copies, srcs, fresh, after, name):
    ns, nb = len(srcs), len(srcs) + len(fresh)

    def body(*refs):
        bufs, send, recv, token = refs[:nb], refs[nb + 1], refs[nb + 2], refs[-1]
        x, y, c = _mesh_pos()
        for i, (s_ref, d_ref, to) in enumerate(copies_fn(bufs[:ns], bufs[ns:] if fresh else bufs[:ns], x, y, c)):
            _remote(s_ref, d_ref, send, recv, i, to).start()
        token[...] = jnp.zeros_like(token)

    sems = [pltpu.SemaphoreType.DMA((n_copies,))] * 2
    outs = pl.pallas_call(
        body, name=name,
        in_specs=[HBM] * nb + [ANY], out_specs=[SEM, SEM] + [HBM] * nb + [pl.BlockSpec(memory_space=pltpu.VMEM)],
        out_shape=sems + _hbm_like(list(srcs) + list(fresh)) + [jax.ShapeDtypeStruct((8, 128), F32)],
        input_output_aliases={i: 2 + i for i in range(nb)},
        compiler_params=pltpu.CompilerParams(has_side_effects=DATAFLOW_EFFECT),
    )(*_in_hbm(list(srcs) + list(fresh)), after)
    return outs[0], outs[1], outs[2:2 + ns], outs[2 + ns:2 + nb], outs[-1]


def _exchange_done(copies_fn, srcs, fresh, send, recv, after, name):
    ns, nb = len(srcs), len(srcs) + len(fresh)

    def body(*refs):
        bufs, send_in, recv_in = refs[:nb], refs[nb], refs[nb + 1]
        x, y, c = _mesh_pos()
        for i, (s_ref, d_ref, to) in enumerate(copies_fn(bufs[:ns], bufs[ns:] if fresh else bufs[:ns], x, y, c)):
            came = _remote(s_ref, d_ref, send_in, recv_in, i, to)
            came.wait_send()
            came.wait_recv()

    outs = pl.pallas_call(
        body, name=name,
        in_specs=[HBM] * nb + [SEM, SEM, ANY], out_specs=[HBM] * nb,
        out_shape=_hbm_like(list(srcs) + list(fresh)),
        input_output_aliases={i: i for i in range(nb)},
        compiler_params=pltpu.CompilerParams(has_side_effects=DATAFLOW_EFFECT),
    )(*_in_hbm(list(srcs) + list(fresh)), send, recv, after)
    return outs[:ns], outs[ns:]


class _Reduce:
    def __init__(self, place, core, shards, mom_m, mom_v):
        self.place, self.core = place, core
        self.shards, self.mom_m, self.mom_v = shards, mom_m, mom_v
        self.state = {}
        self.results = {}

    def add(self, group, grads, after):
        names = list(grads)
        g4s = [g.reshape((N_CHIPS, -1, g.shape[-1])) if g.ndim == 2 else g for g in grads.values()]
        fresh = [lax.empty((N_CHIPS, g.shape[1] // 2, g.shape[2]), BF16) for g in g4s]
        send, recv, g4s, fresh, token = _exchange_start(_sibling_halves_copies, len(names), g4s, fresh, after,
                                                        "pair_start_" + group)
        self.state[group] = (0, names, send, recv, g4s, fresh)
        return token

    def send(self, group, theirs, after):
        names, srcs = list(theirs), list(theirs.values())
        fresh = [lax.empty(s.shape, BF16) for s in srcs]
        send, recv, srcs, fresh, token = _exchange_start(_to_sibling_copies, len(names), srcs, fresh, after, "pair_start_" + group)
        self.state[group] = ("sent", names, send, recv, srcs, fresh)
        return token

    def received(self, group, after):
        stage, names, send, recv, srcs, fresh = self.state.pop(group)
        assert stage == "sent"
        _, got = _exchange_done(_to_sibling_copies, srcs, fresh, send, recv, after, "pair_done_" + group)
        return dict(zip(names, got))

    def add_parts(self, group, parts):
        names, srcs = list(parts), list(parts.values())
        fresh = [lax.empty((N_CHIPS - 1,) + p.shape[1:], BF16) for p in srcs]
        send, recv, srcs, fresh, token = _exchange_start(_chip_copies, 3 * len(names), srcs, fresh, self.core, "chips_start_" + group)
        self.state[group] = (1, names, send, recv, srcs, fresh)
        return token

    def step(self, group, after, count=None):
        stage, names, send, recv, srcs, fresh = self.state[group]
        if stage == 0:
            g4s, ras = _exchange_done(_sibling_halves_copies, srcs, fresh, send, recv, after, "pair_done_" + group)
            parts = [_pair_sum(g, r, self.core, "pair_sum_" + n) for g, r, n in zip(g4s, ras, names)]
            fresh = [lax.empty((N_CHIPS - 1,) + p.shape[1:], BF16) for p in parts]
            send, recv, parts, fresh, token = _exchange_start(_chip_copies, 3 * len(names), parts, fresh, self.core,
                                                              "chips_start_" + group)
            self.state[group] = (1, names, send, recv, parts, fresh)
            return token
        if stage == 1:
            parts, rcs = _exchange_done(_chip_copies, srcs, fresh, send, recv, after, "chips_done_" + group)
            token = None
            for n, p, r in zip(names, parts, rcs):
                self.results[n] = _adamw_own_half(self.shards[n], self.mom_m[n], self.mom_v[n], p, r, self.place,
                                                  "adamw_own_" + n, after=token)
                token = self.results[n][2]
            wholes = [self.results[n][0] for n in names]
            send, recv, wholes, _, token = _exchange_start(_join_copies, len(names), wholes, [], token, "join_start_" + group)
            self.state[group] = (2, names, send, recv, wholes, [])
            return token
        assert stage in (2, 3)
        if stage == 2:
            srcs, _ = _exchange_done(_join_copies, srcs, [], send, recv, after, "join_done_" + group)
            after = None
        token = after
        count = len(names) if count is None else count
        for n, exchanged in zip(names[:count], srcs):
            _, g, d, nm, nv = self.results[n]
            self.results[n] = _adamw_other_half(self.shards[n], self.mom_m[n], self.mom_v[n], exchanged, g, d, nm, nv,
                                                self.place, "adamw_other_" + n, after=token)
            token = self.results[n][1]
        if count < len(names):
            self.state[group] = (3, names[count:], None, None, srcs[count:], [])
        else:
            del self.state[group]
        return token


N_DEV = 8


def _to_all_copies(srcs, dsts, x, y, c):
    out = []
    for r in range(1, N_DEV):
        fx, fy, fc = (r >> 2) & 1, (r >> 1) & 1, r & 1
        out.append((srcs[0], dsts[0].at[r - 1], (x + fx - 2 * x * fx, y + fy - 2 * y * fy, c + fc - 2 * c * fc)))
    return out


def _all_reduce_small_start(v, after):
    slots = lax.empty((N_DEV - 1,) + v.shape, v.dtype)
    send, recv, (v,), (slots,), token = _exchange_start(_to_all_copies, N_DEV - 1, [v], [slots], after, "small_grads_start")
    return (send, recv, v, slots), token


def _all_reduce_small_done(started, after):
    send, recv, v, slots = started
    (v,), (slots,) = _exchange_done(_to_all_copies, [v], [slots], send, recv, after, "small_grads_done")

    def body(v_ref, slots_ref, o_ref):
        x, y, c = _mesh_pos()
        me = 4 * x + 2 * y + c
        acc = None
        for i in range(N_DEV):
            r = jnp.bitwise_xor(me, i)
            part = jnp.where(r == 0, v_ref[...], slots_ref[jnp.maximum(r - 1, 0)])
            acc = part if acc is None else acc + part
        o_ref[...] = acc

    vm = pl.BlockSpec(memory_space=pltpu.VMEM)
    return pl.pallas_call(body, name="small_grads_sum", in_specs=[vm, vm], out_specs=vm,
                          out_shape=jax.ShapeDtypeStruct(v.shape, v.dtype))(v, slots)


MATRICES = ("w_in", "w_attn_out", "w_conv_out", "w_o", "w_cq", "w_ckv", "w_co", "w_gate", "w_up", "w_down")
VECTORS = ("g_mix", "b_gate", "g_cross", "g_mem", "g_ffn", "g_final", "conv_w", "sink")
WEIGHT_ORDER = ("g_mix", "w_in", "sink", "conv_w", "b_gate", "w_attn_out", "w_conv_out", "w_o", "g_cross", "g_mem", "w_cq",
                "w_ckv", "w_co", "g_ffn", "w_gate", "w_up", "w_down", "g_final")
CONV_PAD_ROWS = 32
SMALL_ROWS = 8


def _pack(pieces):
    flat = jnp.concatenate([p.reshape(-1) for p in pieces])
    lane_group = SMALL_ROWS * 128
    total = -(-flat.shape[0] // lane_group) * lane_group
    flat = jnp.pad(flat, (0, total - flat.shape[0]))
    return flat.reshape(SMALL_ROWS, total // SMALL_ROWS), [p.size for p in pieces]


def _unpack(packed, pieces):
    flat = packed.reshape(-1)
    out, off = [], 0
    for p in pieces:
        out.append(flat[off:off + p.size].reshape(p.shape))
        off += p.size
    return out


def kernel(x, mem, g_mix, w_in, sink, conv_w, b_gate, w_attn_out, w_conv_out, w_o, g_cross, g_mem, w_cq, w_ckv, w_co, g_ffn, w_gate, w_up, w_down, g_final, loss_target, m_g_mix, m_w_in, m_sink, m_conv_w, m_b_gate, m_w_attn_out, m_w_conv_out, m_w_o, m_g_cross, m_g_mem, m_w_cq, m_w_ckv, m_w_co, m_g_ffn, m_w_gate, m_w_up, m_w_down, m_g_final, v_g_mix, v_w_in, v_sink, v_conv_w, v_b_gate, v_w_attn_out, v_w_conv_out, v_w_o, v_g_cross, v_g_mem, v_w_cq, v_w_ckv, v_w_co, v_g_ffn, v_w_gate, v_w_up, v_w_down, v_g_final):
    given = dict(g_mix=g_mix, w_in=w_in, sink=sink, conv_w=conv_w, b_gate=b_gate, w_attn_out=w_attn_out, w_conv_out=w_conv_out,
                 w_o=w_o, g_cross=g_cross, g_mem=g_mem, w_cq=w_cq, w_ckv=w_ckv, w_co=w_co, g_ffn=g_ffn, w_gate=w_gate, w_up=w_up,
                 w_down=w_down, g_final=g_final)
    mom_m = dict(g_mix=m_g_mix, w_in=m_w_in, sink=m_sink, conv_w=m_conv_w, b_gate=m_b_gate, w_attn_out=m_w_attn_out,
                 w_conv_out=m_w_conv_out, w_o=m_w_o, g_cross=m_g_cross, g_mem=m_g_mem, w_cq=m_w_cq, w_ckv=m_w_ckv, w_co=m_w_co,
                 g_ffn=m_g_ffn, w_gate=m_w_gate, w_up=m_w_up, w_down=m_w_down, g_final=m_g_final)
    mom_v = dict(g_mix=v_g_mix, w_in=v_w_in, sink=v_sink, conv_w=v_conv_w, b_gate=v_b_gate, w_attn_out=v_w_attn_out,
                 w_conv_out=v_w_conv_out, w_o=v_w_o, g_cross=v_g_cross, g_mem=v_g_mem, w_cq=v_w_cq, w_ckv=v_w_ckv, w_co=v_w_co,
                 g_ffn=v_g_ffn, w_gate=v_w_gate, w_up=v_w_up, w_down=v_w_down, g_final=v_g_final)
    xs, mems, target = x[0], mem[0], loss_target[0]
    d_model = xs.shape[1]
    chip = 2 * lax.axis_index("x") + lax.axis_index("y")
    core = jnp.reshape(lax.axis_index("c"), (1,)).astype(jnp.int32)
    place = jnp.stack([chip, lax.axis_index("c")]).astype(jnp.int32)

    shards = {n: given[n][0] for n in MATRICES}
    conv_cols = conv_w.shape[2]
    conv_pad = jnp.pad(conv_w[0], ((0, CONV_PAD_ROWS - conv_w.shape[1]), (0, 0)))
    fetch = _Gather(GATHER_GROUPS)
    first = {"w_in": _cast_to_slot(shards["w_in"], place, BF16, "to_slot_w_in"),
             "conv_w": _cast_to_slot(conv_pad, place, F32, "to_slot_conv_w")}
    fetch.put(first)
    tok = fetch.step("gather_start", [], [("direct", "in")])
    fetch.put({n: _cast_to_slot(shards[n], place, BF16, "to_slot_" + n, after=tok) for n in MATRICES if n != "w_in"})
    small = {n: given[n] for n in ("g_mix", "b_gate", "g_cross", "g_mem", "g_ffn")}
    small["g_final"] = g_final[None]
    small["sink"] = sink[0]

    reduce = _Reduce(place, core, shards, {n: mom_m[n][0] for n in MATRICES}, {n: mom_v[n][0] for n in MATRICES})
    sq, grad_x, small_grads = _local_step(xs, mems, target, small, fetch, reduce)

    loss_part = 0.5 * sq[0:1, 0:1] / d_model
    pieces = [small_grads[n] for n in VECTORS] + [loss_part]
    packed, _ = _pack(pieces)
    started, tok = _all_reduce_small_start(packed, core)
    tok = reduce.step("in", tok)
    tok = reduce.step("mid", tok)
    summed = _unpack(_all_reduce_small_done(started, tok), pieces)
    loss = summed[-1][0, 0]
    small_sum = dict(zip(VECTORS, summed[:-1]))
    small_sum["conv_w"] = lax.dynamic_slice_in_dim(small_sum["conv_w"], chip * conv_cols, conv_cols, axis=1)

    grad_out, delta, new_m, new_v = {}, {}, {}, {}
    like = [given[n] for n in VECTORS]
    pw, _ = _pack(like)
    pg, _ = _pack([small_sum[n] for n in VECTORS])
    pm, _ = _pack([mom_m[n] for n in VECTORS])
    pv, _ = _pack([mom_v[n] for n in VECTORS])
    _, pd, pnm, pnv = _adamw(pw, pg, pm, pv, "adamw_small")
    for n, g, d, nm, nv in zip(VECTORS, [small_sum[n] for n in VECTORS], _unpack(pd, like), _unpack(pnm, like), _unpack(pnv, like)):
        grad_out[n] = g.reshape(given[n].shape)
        delta[n], new_m[n], new_v[n] = d, nm, nv
    reduce.step("in", pd)
    for n in MATRICES:
        g, d, nm, nv = reduce.results[n]
        grad_out[n], delta[n], new_m[n], new_v[n] = g[None], d[None], nm[None], nv[None]

    return (loss, grad_x[None], *[grad_out[n] for n in WEIGHT_ORDER], *[delta[n] for n in WEIGHT_ORDER],
            *[new_m[n] for n in WEIGHT_ORDER], *[new_v[n] for n in WEIGHT_ORDER])
```

```python
import jax
import jax.numpy as jnp
from jax import lax
from jax.experimental import pallas as pl
from jax.experimental.pallas import tpu as pltpu

F32 = jnp.float32
BF16 = jnp.bfloat16
MESH = pl.DeviceIdType.MESH
ANY = pl.BlockSpec(memory_space=pl.ANY)

VMEM_LIMIT_BYTES = 56 * 1024 * 1024

N_CHIPS = 4
HEAD_DIM = 128
N_Q_HEADS = 8
N_KV_HEADS = 2
Q_GROUP = N_Q_HEADS // N_KV_HEADS
ATTN_WIDTH = N_Q_HEADS * HEAD_DIM
KV_WIDTH = N_KV_HEADS * HEAD_DIM
WINDOW = 128
BLOCK = 128
BAND = 3 * BLOCK
ROPE_THETA = 10000.0
CONV_WIDTH = 1024
MEM_HEADS = 4
MEM_WIDTH = MEM_HEADS * HEAD_DIM
RMS_EPS = 1e-6
NEG_INF = -1e30
ATTN_SCALE = HEAD_DIM ** -0.5

Q_OFF, K_OFF, V_OFF, CU_OFF, CB_OFF, CC_OFF, GL_OFF = 0, 1024, 1280, 1536, 2560, 3584, 4608

ADAM_LR = 0.001
ADAM_B1 = 0.9
ADAM_B2 = 0.999
ADAM_EPS = 1e-08
ADAM_WD = 0.01
ADAM_STEP = 10
ADAM_C1 = 1.0 - ADAM_B1 ** ADAM_STEP
ADAM_C2 = 1.0 - ADAM_B2 ** ADAM_STEP


def _params(n_grid_axes):
    return pltpu.CompilerParams(dimension_semantics=("arbitrary",) * n_grid_axes, vmem_limit_bytes=VMEM_LIMIT_BYTES)


BF16_SUBLANES = 16


def _row_tile(rows, want):
    if rows <= want:
        return rows
    for t in range(want, 0, -BF16_SUBLANES):
        if rows % t == 0:
            return t
    return rows


def _matmul(a, b, *, mode, tm, tn, out_dtypes, name, extras=(), epilogue=None, b_blocks=1, out_blocks=1, after=None):
    if mode == "tn":
        kdim, m = a.shape
    else:
        m, kdim = a.shape
    if b_blocks > 1:
        nb, brows, bcols = b.shape
        assert nb == b_blocks
        if mode == "nn":
            n = bcols * nb
            assert brows == kdim
        else:
            assert mode == "nt" and bcols * nb == kdim
            n = brows
    else:
        n = b.shape[0] if mode == "nt" else b.shape[1]
    tm, tn = min(tm, m), min(tn, n)
    tk = kdim
    assert m % tm == 0 and n % tn == 0, (name, m, n, tm, tn)
    n_extra, n_out = len(extras), len(out_dtypes)
    n_after = 0 if after is None else 1

    if mode == "tn":
        a_spec = pl.BlockSpec((tk, tm), lambda j, i, k: (k, i))
        dims = (((0,), (0,)), ((), ()))
    else:
        a_spec = pl.BlockSpec((tm, tk), lambda j, i, k: (i, k))
        dims = (((1,), (0,)), ((), ())) if mode == "nn" else (((1,), (1,)), ((), ()))

    if b_blocks > 1 and mode == "nn":
        per = b.shape[2] // tn
        assert b.shape[2] % tn == 0
        b_spec = pl.BlockSpec((None, tk, tn), lambda j, i, k: (j // per, k, j % per))
    elif b_blocks > 1:
        b_spec = pl.BlockSpec((b_blocks, tn, b.shape[2]), lambda j, i, k: (0, j, 0))
    elif mode == "nt":
        b_spec = pl.BlockSpec((tn, tk), lambda j, i, k: (j, k))
    else:
        b_spec = pl.BlockSpec((tk, tn), lambda j, i, k: (k, j))

    tile_spec = pl.BlockSpec((tm, tn), lambda j, i, k: (i, j))
    if out_blocks > 1:
        ncols = n // out_blocks
        assert ncols % tn == 0
        oper = ncols // tn
        out_spec = pl.BlockSpec((None, tm, tn), lambda j, i, k: (j // oper, i, j % oper))
        out_shape = [jax.ShapeDtypeStruct((out_blocks, m, ncols), dt) for dt in out_dtypes]
    else:
        out_spec = tile_spec
        out_shape = [jax.ShapeDtypeStruct((m, n), dt) for dt in out_dtypes]

    def body(a_ref, b_ref, *rest):
        extra_refs = rest[:n_extra]
        out_refs = rest[n_extra + n_after:n_extra + n_after + n_out]
        if mode == "nt" and b_blocks > 1:
            cs = b.shape[2]
            acc = None
            for jb in range(b_blocks):
                prod = lax.dot_general(a_ref[:, jb * cs:(jb + 1) * cs].astype(BF16), b_ref[jb].astype(BF16), dims,
                                       preferred_element_type=F32)
                acc = prod if acc is None else acc + prod
        else:
            acc = lax.dot_general(a_ref[...].astype(BF16), b_ref[...].astype(BF16), dims, preferred_element_type=F32)
        tiles = (acc,) if epilogue is None else epilogue(acc, *[r[...] for r in extra_refs])
        for o_ref, t in zip(out_refs, tiles, strict=True):
            o_ref[...] = t.astype(o_ref.dtype)

    outs = pl.pallas_call(
        body,
        name=name,
        grid=(n // tn, m // tm, 1),
        in_specs=[a_spec, b_spec] + [tile_spec] * n_extra + [ANY] * n_after,
        out_specs=[out_spec] * n_out,
        out_shape=out_shape,
        compiler_params=_params(3),
    )(a, b, *extras, *([] if after is None else [after]))
    return outs[0] if n_out == 1 else outs


def _add_residual(acc, res):
    return (acc + res,)


def _matmul_column_blocks(a, b4, blocks, out, *, tm, name, after=None):
    m, kdim = a.shape
    nb, _, cols = b4.shape
    tm = min(tm, m)
    assert m % tm == 0

    def body(j_ref, a_ref, b_ref, *rest):
        rest[-1][...] = jnp.dot(a_ref[...], b_ref[...], preferred_element_type=F32)

    extra = ([] if out is None else [out]) + ([] if after is None else [after])
    n_blocks = blocks.shape[0]
    return pl.pallas_call(
        body, name=name,
        grid_spec=pltpu.PrefetchScalarGridSpec(
            num_scalar_prefetch=1, grid=(n_blocks, m // tm),
            in_specs=[pl.BlockSpec((tm, kdim), lambda j, i, blk: (i, 0)),
                      pl.BlockSpec((None, kdim, cols), lambda j, i, blk: (blk[j], 0, 0))] + [ANY] * len(extra),
            out_specs=pl.BlockSpec((tm, cols), lambda j, i, blk: (i, blk[j]))),
        out_shape=jax.ShapeDtypeStruct((m, nb * cols), F32),
        input_output_aliases={} if out is None else {3: 0},
        compiler_params=_params(2),
    )(blocks, a, b4, *extra)


def _wgrad_half(a, b, core, *, theirs, row_sharded, tm, tn, name, add=None, after=None):
    kdim, m = a.shape
    n = b.shape[1]
    rs, cs = (m // N_CHIPS, n) if row_sharded else (m, n // N_CHIPS)
    rh = rs // 2
    tm, tn = min(tm, rh), min(tn, cs)
    assert rh % tm == 0 and cs % tn == 0, (name, rh, cs, tm, tn)
    mh, per = rh // tm, cs // tn
    has_add = add is not None

    def half(c):
        return 1 - c[0] if theirs else c[0]

    if row_sharded:
        grid = (n // tn, N_CHIPS * mh)
        a_spec = pl.BlockSpec((kdim, tm), lambda j, r, c: (0, ((r // mh) * 2 + half(c)) * mh + r % mh))
        o_spec = pl.BlockSpec((None, tm, tn), lambda j, r, c: (r // mh, r % mh, j))
    else:
        grid = (n // tn, mh)
        a_spec = pl.BlockSpec((kdim, tm), lambda j, r, c: (0, half(c) * mh + r))
        o_spec = pl.BlockSpec((None, tm, tn), lambda j, r, c: (j // per, r, j % per))
    b_spec = pl.BlockSpec((kdim, tn), lambda j, r, c: (0, j))

    def body(c_ref, a_ref, b_ref, *rest):
        o_ref = rest[-1]
        acc = lax.dot_general(a_ref[...].astype(BF16), b_ref[...].astype(BF16), (((0,), (0,)), ((), ())),
                              preferred_element_type=F32)
        if has_add:
            acc = acc + rest[0][...].astype(F32)
        o_ref[...] = acc.astype(BF16)

    operands = [a, b] + ([add] if has_add else []) + ([] if after is None else [after])
    return pl.pallas_call(
        body, name=name,
        grid_spec=pltpu.PrefetchScalarGridSpec(
            num_scalar_prefetch=1, grid=grid,
            in_specs=[a_spec, b_spec] + ([o_spec] if has_add else []) + ([] if after is None else [ANY]),
            out_specs=o_spec),
        out_shape=jax.ShapeDtypeStruct((N_CHIPS, rh, cs), BF16),
        compiler_params=_params(2),
    )(core, *operands)


def _rstd(x):
    return lax.rsqrt(jnp.mean(x * x, axis=-1, keepdims=True) + RMS_EPS)


def _rmsnorm(x, g, name):
    s, d = x.shape
    tr = _row_tile(s, 512)

    def body(x_ref, g_ref, o_ref):
        xv = x_ref[...]
        o_ref[...] = (xv * _rstd(xv) * g_ref[...]).astype(BF16)

    return pl.pallas_call(
        body, name=name, grid=(s // tr,),
        in_specs=[pl.BlockSpec((tr, d), lambda i: (i, 0)), pl.BlockSpec((1, d), lambda i: (0, 0))],
        out_specs=pl.BlockSpec((tr, d), lambda i: (i, 0)),
        out_shape=jax.ShapeDtypeStruct((s, d), BF16),
        compiler_params=_params(1),
    )(x, g)


def _rmsnorm_bwd(dh, x, g, dres, name):
    s, d = x.shape
    tr = _row_tile(s, 512)
    has_res = dres is not None

    def body(*refs):
        if has_res:
            dh_ref, x_ref, g_ref, res_ref, dx_ref, dxb_ref, dg_ref = refs
        else:
            dh_ref, x_ref, g_ref, dx_ref, dxb_ref, dg_ref = refs
        xv = x_ref[...]
        dhv = dh_ref[...].astype(F32)
        r = _rstd(xv)
        xn = xv * r
        dhg = dhv * g_ref[...]
        dx = r * (dhg - xn * jnp.mean(dhg * xn, axis=-1, keepdims=True))
        if has_res:
            dx = dx + res_ref[...]
        dx_ref[...] = dx
        dxb_ref[...] = dx.astype(BF16)
        part = jnp.sum(dhv * xn, axis=0, keepdims=True)

        @pl.when(pl.program_id(0) == 0)
        def _():
            dg_ref[...] = part

        @pl.when(pl.program_id(0) > 0)
        def _():
            dg_ref[...] += part

    row = pl.BlockSpec((tr, d), lambda i: (i, 0))
    vec = pl.BlockSpec((1, d), lambda i: (0, 0))
    return pl.pallas_call(
        body, name=name, grid=(s // tr,),
        in_specs=[row, row, vec] + ([row] if has_res else []),
        out_specs=[row, row, vec],
        out_shape=[jax.ShapeDtypeStruct((s, d), F32), jax.ShapeDtypeStruct((s, d), BF16), jax.ShapeDtypeStruct((1, d), F32)],
        compiler_params=_params(1),
    )(*([dh, x, g] + ([dres] if has_res else [])))


def _loss_head(x3, g, target):
    s, d = x3.shape
    tr = _row_tile(s, 512)

    def body(x_ref, g_ref, t_ref, dx_ref, dxb_ref, sq_ref, dg_ref):
        xv = x_ref[...]
        gv = g_ref[...]
        r = _rstd(xv)
        xn = xv * r
        err = xn * gv - t_ref[...]
        dy = err * (1.0 / d)
        dyg = dy * gv
        dx = r * (dyg - xn * jnp.mean(dyg * xn, axis=-1, keepdims=True))
        dx_ref[...] = dx
        dxb_ref[...] = dx.astype(BF16)
        sq = jnp.sum(jnp.sum(err * err, axis=1, keepdims=True), axis=0, keepdims=True)
        sq = jnp.broadcast_to(sq, (1, 128))
        part = jnp.sum(dy * xn, axis=0, keepdims=True)

        @pl.when(pl.program_id(0) == 0)
        def _():
            sq_ref[...] = sq
            dg_ref[...] = part

        @pl.when(pl.program_id(0) > 0)
        def _():
            sq_ref[...] += sq
            dg_ref[...] += part

    row = pl.BlockSpec((tr, d), lambda i: (i, 0))
    vec = pl.BlockSpec((1, d), lambda i: (0, 0))
    return pl.pallas_call(
        body, name="loss_head", grid=(s // tr,),
        in_specs=[row, vec, row],
        out_specs=[row, row, pl.BlockSpec((1, 128), lambda i: (0, 0)), vec],
        out_shape=[jax.ShapeDtypeStruct((s, d), F32), jax.ShapeDtypeStruct((s, d), BF16),
                   jax.ShapeDtypeStruct((1, 128), F32), jax.ShapeDtypeStruct((1, d), F32)],
        compiler_params=_params(1),
    )(x3, g, target)


def _rope_tables(s):
    inv = 1.0 / (ROPE_THETA ** (jnp.arange(0, HEAD_DIM, 2, dtype=F32) / HEAD_DIM))
    ang = jnp.arange(s, dtype=F32)[:, None] * inv[None, :]
    cos, sin = jnp.cos(ang), jnp.sin(ang)
    return jnp.concatenate([cos, cos], axis=1), jnp.concatenate([-sin, sin], axis=1)


def _swap_halves(t):
    return pltpu.roll(t, HEAD_DIM // 2, 1)


def _rope_fwd(z, cos_t, sin_t):
    s = z.shape[0]
    tr = _row_tile(s, 256)

    def body(zq_ref, zk_ref, zv_ref, c_ref, s_ref, q_ref, k_ref, v_ref):
        c, sn = c_ref[...], s_ref[...]
        for hd in range(N_Q_HEADS):
            cols = slice(hd * HEAD_DIM, (hd + 1) * HEAD_DIM)
            t = zq_ref[:, cols]
            q_ref[:, cols] = (t * c + _swap_halves(t) * sn).astype(BF16)
        for hd in range(N_KV_HEADS):
            cols = slice(hd * HEAD_DIM, (hd + 1) * HEAD_DIM)
            t = zk_ref[:, cols]
            k_ref[:, cols] = (t * c + _swap_halves(t) * sn).astype(BF16)
        v_ref[...] = zv_ref[...].astype(BF16)

    tab = pl.BlockSpec((tr, HEAD_DIM), lambda i: (i, 0))
    return pl.pallas_call(
        body, name="rope_fwd", grid=(s // tr,),
        in_specs=[pl.BlockSpec((tr, ATTN_WIDTH), lambda i: (i, Q_OFF // ATTN_WIDTH)),
                  pl.BlockSpec((tr, KV_WIDTH), lambda i: (i, K_OFF // KV_WIDTH)),
                  pl.BlockSpec((tr, KV_WIDTH), lambda i: (i, V_OFF // KV_WIDTH)), tab, tab],
        out_specs=[pl.BlockSpec((tr, ATTN_WIDTH), lambda i: (i, 0)), pl.BlockSpec((tr, KV_WIDTH), lambda i: (i, 0)),
                   pl.BlockSpec((tr, KV_WIDTH), lambda i: (i, 0))],
        out_shape=[jax.ShapeDtypeStruct((s, ATTN_WIDTH), BF16), jax.ShapeDtypeStruct((s, KV_WIDTH), BF16),
                   jax.ShapeDtypeStruct((s, KV_WIDTH), BF16)],
        compiler_params=_params(1),
    )(z, z, z, cos_t, sin_t)


def _rope_bwd(dq_rot, dk_rot, dv, cos_t, sin_t, dz):
    s = dq_rot.shape[0]
    tr = _row_tile(s, 256)
    qkv_width = V_OFF + KV_WIDTH

    def body(dq_ref, dk_ref, dv_ref, c_ref, s_ref, dz_in_ref, o_ref):
        c, sn = c_ref[...], s_ref[...]
        for hd in range(N_Q_HEADS):
            t = dq_ref[:, hd * HEAD_DIM:(hd + 1) * HEAD_DIM]
            o_ref[:, Q_OFF + hd * HEAD_DIM:Q_OFF + (hd + 1) * HEAD_DIM] = (t * c + _swap_halves(t * sn)).astype(BF16)
        for hd in range(N_KV_HEADS):
            t = dk_ref[:, hd * HEAD_DIM:(hd + 1) * HEAD_DIM]
            o_ref[:, K_OFF + hd * HEAD_DIM:K_OFF + (hd + 1) * HEAD_DIM] = (t * c + _swap_halves(t * sn)).astype(BF16)
        o_ref[:, V_OFF:V_OFF + KV_WIDTH] = dv_ref[...].astype(BF16)

    tab = pl.BlockSpec((tr, HEAD_DIM), lambda i: (i, 0))
    wide = pl.BlockSpec((tr, ATTN_WIDTH), lambda i: (i, 0))
    narrow = pl.BlockSpec((tr, KV_WIDTH), lambda i: (i, 0))
    return pl.pallas_call(
        body, name="rope_bwd", grid=(s // tr,),
        in_specs=[wide, narrow, narrow, tab, tab, ANY],
        out_specs=pl.BlockSpec((tr, qkv_width), lambda i: (i, 0)),
        out_shape=jax.ShapeDtypeStruct(dz.shape, dz.dtype),
        input_output_aliases={5: 0},
        compiler_params=_params(1),
    )(dq_rot, dk_rot, dv, cos_t, sin_t, dz)


def _swa_band(i, s):
    return pl.multiple_of(jnp.clip((i - 1) * BLOCK, 0, s - BAND), BLOCK)


SWA_HEADS_PER_PASS = Q_GROUP


def _swa_probs(q_ref, k_ref, sink_ref, heads, start, valid):
    kv = heads[0] // Q_GROUP
    cols = slice(kv * HEAD_DIM, (kv + 1) * HEAD_DIM)
    kb = k_ref[pl.ds(start, BAND), cols]
    qg = jnp.concatenate([q_ref[:, hd * HEAD_DIM:(hd + 1) * HEAD_DIM] for hd in heads], axis=0)
    sc = lax.dot_general(qg, kb, (((1,), (1,)), ((), ())), preferred_element_type=F32) * ATTN_SCALE
    sc = jnp.where(valid, sc, NEG_INF)
    sk = jnp.concatenate([jnp.full((BLOCK, 1), sink_ref[hd], F32) for hd in heads], axis=0)
    mx = jnp.maximum(jnp.max(sc, axis=1, keepdims=True), sk)
    e = jnp.exp(sc - mx)
    es = jnp.exp(sk - mx)
    inv = 1.0 / (jnp.sum(e, axis=1, keepdims=True) + es)
    return qg, kb, e * inv, es * inv


def _swa_head_passes():
    return [list(range(h0, h0 + SWA_HEADS_PER_PASS)) for h0 in range(0, N_Q_HEADS, SWA_HEADS_PER_PASS)]


def _swa_valid(i, start):
    q_pos = i * BLOCK + lax.broadcasted_iota(jnp.int32, (BLOCK, 1), 0)
    q_pos = jnp.concatenate([q_pos] * SWA_HEADS_PER_PASS, axis=0)
    k_pos = start + lax.broadcasted_iota(jnp.int32, (1, BAND), 1)
    return jnp.abs(k_pos - q_pos) <= WINDOW


def _swa_fwd(q, k, v, sink):
    s = q.shape[0]
    assert s % BLOCK == 0 and s >= BAND

    def body(sink_ref, q_ref, k_ref, v_ref, o_ref):
        i = pl.program_id(0)
        start = _swa_band(i, s)
        valid = _swa_valid(i, start)
        for heads in _swa_head_passes():
            kv = heads[0] // Q_GROUP
            _, _, p, _ = _swa_probs(q_ref, k_ref, sink_ref, heads, start, valid)
            vb = v_ref[pl.ds(start, BAND), kv * HEAD_DIM:(kv + 1) * HEAD_DIM]
            o = jnp.dot(p.astype(BF16), vb, preferred_element_type=F32)
            for g, hd in enumerate(heads):
                o_ref[:, hd * HEAD_DIM:(hd + 1) * HEAD_DIM] = o[g * BLOCK:(g + 1) * BLOCK].astype(BF16)

    whole = pl.BlockSpec((s, KV_WIDTH), lambda i: (0, 0))
    blk = pl.BlockSpec((BLOCK, ATTN_WIDTH), lambda i: (i, 0))
    return pl.pallas_call(
        body, name="swa_fwd", grid=(s // BLOCK,),
        in_specs=[pl.BlockSpec(memory_space=pltpu.SMEM), blk, whole, whole],
        out_specs=blk,
        out_shape=jax.ShapeDtypeStruct((s, ATTN_WIDTH), BF16),
        compiler_params=_params(1),
    )(sink, q, k, v)


def _swa_bwd(q, k, v, d_out, sink):
    s = q.shape[0]

    def body(sink_ref, q_ref, k_ref, v_ref, do_ref, dq_ref, dk_ref, dv_ref, dsink_ref):
        i = pl.program_id(0)

        @pl.when(i == 0)
        def _():
            dk_ref[...] = jnp.zeros_like(dk_ref)
            dv_ref[...] = jnp.zeros_like(dv_ref)
            dsink_ref[...] = jnp.zeros_like(dsink_ref)

        start = _swa_band(i, s)
        valid = _swa_valid(i, start)
        for heads in _swa_head_passes():
            kv = heads[0] // Q_GROUP
            cols = slice(kv * HEAD_DIM, (kv + 1) * HEAD_DIM)
            qg, kb, p, p_sink = _swa_probs(q_ref, k_ref, sink_ref, heads, start, valid)
            vb = v_ref[pl.ds(start, BAND), cols]
            dog = jnp.concatenate([do_ref[:, hd * HEAD_DIM:(hd + 1) * HEAD_DIM] for hd in heads], axis=0)
            dp = lax.dot_general(dog, vb, (((1,), (1,)), ((), ())), preferred_element_type=F32)
            delta = jnp.sum(p * dp, axis=1, keepdims=True)
            ds = (p * (dp - delta) * ATTN_SCALE).astype(BF16)
            dqg = jnp.dot(ds, kb, preferred_element_type=F32)
            dk_ref[pl.ds(start, BAND), cols] += lax.dot_general(ds, qg, (((0,), (0,)), ((), ())), preferred_element_type=F32)
            dv_ref[pl.ds(start, BAND), cols] += lax.dot_general(p.astype(BF16), dog, (((0,), (0,)), ((), ())),
                                                                 preferred_element_type=F32)
            dsk = p_sink * delta
            for g, hd in enumerate(heads):
                dq_ref[:, hd * HEAD_DIM:(hd + 1) * HEAD_DIM] = dqg[g * BLOCK:(g + 1) * BLOCK]
                tot = jnp.sum(dsk[g * BLOCK:(g + 1) * BLOCK], axis=0, keepdims=True)
                dsink_ref[hd:hd + 1, :] -= jnp.broadcast_to(tot, (1, 128))

    whole = pl.BlockSpec((s, KV_WIDTH), lambda i: (0, 0))
    blk = pl.BlockSpec((BLOCK, ATTN_WIDTH), lambda i: (i, 0))
    return pl.pallas_call(
        body, name="swa_bwd", grid=(s // BLOCK,),
        in_specs=[pl.BlockSpec(memory_space=pltpu.SMEM), blk, whole, whole, blk],
        out_specs=[blk, whole, whole, pl.BlockSpec((N_Q_HEADS, 128), lambda i: (0, 0))],
        out_shape=[jax.ShapeDtypeStruct((s, ATTN_WIDTH), F32), jax.ShapeDtypeStruct((s, KV_WIDTH), F32),
                   jax.ShapeDtypeStruct((s, KV_WIDTH), F32), jax.ShapeDtypeStruct((N_Q_HEADS, 128), F32)],
        compiler_params=_params(1),
    )(sink, q, k, v, d_out)


CONV_CHUNK = 256


def _shift_rows(t, rows, down):
    n = t.shape[0]
    rolled = pltpu.roll(t, 1 if down else n - 1, 0)
    edge = 0 if down else n - 1
    return jnp.where(rows == edge, 0.0, rolled)


def _conv_specs(s):
    def z_spec(off):
        return pl.BlockSpec((s, CONV_CHUNK), lambda j, off=off: (0, off // CONV_CHUNK + j))
    chunk = pl.BlockSpec((s, CONV_CHUNK), lambda j: (0, j))
    w_spec = pl.BlockSpec((3, CONV_CHUNK), lambda j: (0, j))
    return z_spec(CU_OFF), z_spec(CB_OFF), z_spec(CC_OFF), chunk, w_spec


def _conv_fwd(z, conv_w):
    s = z.shape[0]
    cu_spec, cb_spec, cc_spec, chunk, w_spec = _conv_specs(s)

    def body(cu_ref, cb_ref, cc_ref, w_ref, o_ref):
        rows = lax.broadcasted_iota(jnp.int32, (s, 1), 0)
        t = cc_ref[...] * cu_ref[...]
        c3 = _shift_rows(t, rows, True) * w_ref[0:1, :] + t * w_ref[1:2, :] + _shift_rows(t, rows, False) * w_ref[2:3, :]
        o_ref[...] = (cb_ref[...] * c3).astype(BF16)

    return pl.pallas_call(
        body, name="conv_fwd", grid=(CONV_WIDTH // CONV_CHUNK,),
        in_specs=[cu_spec, cb_spec, cc_spec, w_spec],
        out_specs=chunk,
        out_shape=jax.ShapeDtypeStruct((s, CONV_WIDTH), BF16),
        compiler_params=_params(1),
    )(z, z, z, conv_w)


def _conv_bwd(z, conv_w, d_co, dz):
    s = z.shape[0]
    cu_spec, cb_spec, cc_spec, chunk, w_spec = _conv_specs(s)
    n_chunks = CONV_WIDTH // CONV_CHUNK
    offsets = (CU_OFF, CB_OFF, CC_OFF)

    def body(cu_ref, cb_ref, cc_ref, w_ref, d_ref, dz_in_ref, dz_ref, dw_ref, buf, sems):
        j = pl.program_id(0)

        def copies(j_at):
            return [pltpu.make_async_copy(buf.at[h], dz_ref.at[:, pl.ds(off + j_at * CONV_CHUNK, CONV_CHUNK)], sems.at[h])
                    for h, off in enumerate(offsets)]

        rows = lax.broadcasted_iota(jnp.int32, (s, 1), 0)
        cu, cc = cu_ref[...], cc_ref[...]
        t = cc * cu
        t_dn, t_up = _shift_rows(t, rows, True), _shift_rows(t, rows, False)
        c3 = t_dn * w_ref[0:1, :] + t * w_ref[1:2, :] + t_up * w_ref[2:3, :]
        d = d_ref[...]
        dc3 = d * cb_ref[...]
        dw_ref[0:1, :] = jnp.sum(dc3 * t_dn, axis=0, keepdims=True)
        dw_ref[1:2, :] = jnp.sum(dc3 * t, axis=0, keepdims=True)
        dw_ref[2:3, :] = jnp.sum(dc3 * t_up, axis=0, keepdims=True)
        dt = _shift_rows(dc3, rows, False) * w_ref[0:1, :] + dc3 * w_ref[1:2, :] + _shift_rows(dc3, rows, True) * w_ref[2:3, :]

        @pl.when(j > 0)
        def _():
            for cp in copies(j):
                cp.wait()

        buf[0] = (dt * cc).astype(BF16)
        buf[1] = (d * c3).astype(BF16)
        buf[2] = (dt * cu).astype(BF16)
        for cp in copies(j):
            cp.start()

        @pl.when(j == n_chunks - 1)
        def _():
            for cp in copies(j):
                cp.wait()

    return pl.pallas_call(
        body, name="conv_bwd", grid=(n_chunks,),
        in_specs=[cu_spec, cb_spec, cc_spec, w_spec, chunk, ANY],
        out_specs=[ANY, w_spec],
        out_shape=[jax.ShapeDtypeStruct(dz.shape, dz.dtype), jax.ShapeDtypeStruct((3, CONV_WIDTH), F32)],
        input_output_aliases={5: 0},
        scratch_shapes=[pltpu.VMEM((3, s, CONV_CHUNK), BF16), pltpu.SemaphoreType.DMA((3,))],
        compiler_params=_params(1),
    )(z, z, z, conv_w, d_co, dz)


GATE_CHUNK = 512


def _gate_specs(s, d, tr):
    n_chunks = d // GATE_CHUNK
    za = pl.BlockSpec((tr, GATE_CHUNK), lambda j, i: (i, GL_OFF // GATE_CHUNK + j))
    zc = pl.BlockSpec((tr, GATE_CHUNK), lambda j, i: (i, GL_OFF // GATE_CHUNK + n_chunks + j))
    ba = pl.BlockSpec((1, GATE_CHUNK), lambda j, i: (0, j))
    bc = pl.BlockSpec((1, GATE_CHUNK), lambda j, i: (0, n_chunks + j))
    tile = pl.BlockSpec((tr, GATE_CHUNK), lambda j, i: (i, j))
    return za, zc, ba, bc, tile


def _gate_fwd(z, b_gate, ya, yc):
    s, d = ya.shape
    tr = _row_tile(s, 512)
    za, zc, ba, bc, tile = _gate_specs(s, d, tr)

    def body(za_ref, zc_ref, ba_ref, bc_ref, ya_ref, yc_ref, o_ref):
        ga = jax.nn.sigmoid(za_ref[...] + ba_ref[...])
        gc = jax.nn.sigmoid(zc_ref[...] + bc_ref[...])
        o_ref[...] = (ga * ya_ref[...] + gc * yc_ref[...]).astype(BF16)

    return pl.pallas_call(
        body, name="gate_fwd", grid=(d // GATE_CHUNK, s // tr),
        in_specs=[za, zc, ba, bc, tile, tile],
        out_specs=tile,
        out_shape=jax.ShapeDtypeStruct((s, d), BF16),
        compiler_params=_params(2),
    )(z, z, b_gate, b_gate, ya, yc)


def _gate_bwd(z, b_gate, ya, yc, dmix):
    s, d = ya.shape
    tr = _row_tile(s, 512)
    za, zc, ba, bc, tile = _gate_specs(s, d, tr)
    vec = pl.BlockSpec((1, GATE_CHUNK), lambda j, i: (0, j))
    n_rows = s // tr
    in_width = z.shape[1]

    def body(za_ref, zc_ref, ba_ref, bc_ref, ya_ref, yc_ref, dm_ref, dya_ref, dyc_ref, dz_ref, dba_ref, dbc_ref, buf, sems):
        j, i = pl.program_id(0), pl.program_id(1)

        def copies(j_at, i_at):
            rows = pl.ds(i_at * tr, tr)
            return [pltpu.make_async_copy(buf.at[h], dz_ref.at[rows, pl.ds(GL_OFF + h * d + j_at * GATE_CHUNK, GATE_CHUNK)],
                                          sems.at[h]) for h in range(2)]

        ga = jax.nn.sigmoid(za_ref[...] + ba_ref[...])
        gc = jax.nn.sigmoid(zc_ref[...] + bc_ref[...])
        dm = dm_ref[...]
        dya_ref[...] = (dm * ga).astype(BF16)
        dyc_ref[...] = (dm * gc).astype(BF16)
        dla = dm * ya_ref[...] * ga * (1.0 - ga)
        dlc = dm * yc_ref[...] * gc * (1.0 - gc)

        @pl.when(j * n_rows + i > 0)
        def _():
            for cp in copies(j, i):
                cp.wait()

        buf[0] = dla.astype(BF16)
        buf[1] = dlc.astype(BF16)
        for cp in copies(j, i):
            cp.start()

        @pl.when((j == d // GATE_CHUNK - 1) & (i == n_rows - 1))
        def _():
            for cp in copies(j, i):
                cp.wait()

        pa = jnp.sum(dla, axis=0, keepdims=True)
        pc = jnp.sum(dlc, axis=0, keepdims=True)

        @pl.when(i == 0)
        def _():
            dba_ref[...] = pa
            dbc_ref[...] = pc

        @pl.when(i > 0)
        def _():
            dba_ref[...] += pa
            dbc_ref[...] += pc

    big = jax.ShapeDtypeStruct((s, d), BF16)
    small = jax.ShapeDtypeStruct((1, d), F32)
    return pl.pallas_call(
        body, name="gate_bwd", grid=(d // GATE_CHUNK, n_rows),
        in_specs=[za, zc, ba, bc, tile, tile, tile],
        out_specs=[tile, tile, ANY, vec, vec],
        out_shape=[big, big, jax.ShapeDtypeStruct((s, in_width), BF16), small, small],
        scratch_shapes=[pltpu.VMEM((2, tr, GATE_CHUNK), BF16), pltpu.SemaphoreType.DMA((2,))],
        compiler_params=_params(2),
    )(z, z, b_gate, b_gate, ya, yc, dmix)


def _cross_probs(q_ref, kv_ref, hd):
    cols = slice(hd * HEAD_DIM, (hd + 1) * HEAD_DIM)
    qh = q_ref[:, cols]
    kh = kv_ref[:, cols]
    sc = lax.dot_general(qh, kh, (((1,), (1,)), ((), ())), preferred_element_type=F32) * ATTN_SCALE
    e = jnp.exp(sc - jnp.max(sc, axis=1, keepdims=True))
    return qh, kh, e * (1.0 / jnp.sum(e, axis=1, keepdims=True))


def _cross_fwd(qc, kvc):
    s = qc.shape[0]
    n_mem = kvc.shape[0]
    tq = _row_tile(s, 256)

    def body(q_ref, kv_ref, o_ref):
        for hd in range(MEM_HEADS):
            _, _, p = _cross_probs(q_ref, kv_ref, hd)
            vh = kv_ref[:, MEM_WIDTH + hd * HEAD_DIM:MEM_WIDTH + (hd + 1) * HEAD_DIM]
            o_ref[:, hd * HEAD_DIM:(hd + 1) * HEAD_DIM] = jnp.dot(p.astype(BF16), vh, preferred_element_type=F32).astype(BF16)

    return pl.pallas_call(
        body, name="cross_fwd", grid=(s // tq,),
        in_specs=[pl.BlockSpec((tq, MEM_WIDTH), lambda i: (i, 0)), pl.BlockSpec((n_mem, 2 * MEM_WIDTH), lambda i: (0, 0))],
        out_specs=pl.BlockSpec((tq, MEM_WIDTH), lambda i: (i, 0)),
        out_shape=jax.ShapeDtypeStruct((s, MEM_WIDTH), BF16),
        compiler_params=_params(1),
    )(qc, kvc)


def _cross_bwd(qc, kvc, d_out):
    s = qc.shape[0]
    n_mem = kvc.shape[0]
    tq = _row_tile(s, 256)

    def body(q_ref, kv_ref, do_ref, dq_ref, dkv_ref):
        @pl.when(pl.program_id(0) == 0)
        def _():
            dkv_ref[...] = jnp.zeros_like(dkv_ref)

        for hd in range(MEM_HEADS):
            cols = slice(hd * HEAD_DIM, (hd + 1) * HEAD_DIM)
            vcols = slice(MEM_WIDTH + hd * HEAD_DIM, MEM_WIDTH + (hd + 1) * HEAD_DIM)
            qh, kh, p = _cross_probs(q_ref, kv_ref, hd)
            doh = do_ref[:, cols]
            dp = lax.dot_general(doh, kv_ref[:, vcols], (((1,), (1,)), ((), ())), preferred_element_type=F32)
            ds = (p * (dp - jnp.sum(p * dp, axis=1, keepdims=True)) * ATTN_SCALE).astype(BF16)
            dq_ref[:, cols] = jnp.dot(ds, kh, preferred_element_type=F32).astype(BF16)
            dkv_ref[:, cols] += lax.dot_general(ds, qh, (((0,), (0,)), ((), ())), preferred_element_type=F32)
            dkv_ref[:, vcols] += lax.dot_general(p.astype(BF16), doh, (((0,), (0,)), ((), ())), preferred_element_type=F32)

    qspec = pl.BlockSpec((tq, MEM_WIDTH), lambda i: (i, 0))
    kvspec = pl.BlockSpec((n_mem, 2 * MEM_WIDTH), lambda i: (0, 0))
    return pl.pallas_call(
        body, name="cross_bwd", grid=(s // tq,),
        in_specs=[qspec, kvspec, qspec],
        out_specs=[qspec, kvspec],
        out_shape=[jax.ShapeDtypeStruct((s, MEM_WIDTH), BF16), jax.ShapeDtypeStruct((n_mem, 2 * MEM_WIDTH), F32)],
        compiler_params=_params(1),
    )(qc, kvc, d_out)


def _swiglu_fwd(up, gate):
    sg = jax.nn.sigmoid(gate)
    silu = gate * sg
    return silu * up, up * (sg * (1.0 + gate * (1.0 - sg))), silu


def _swiglu_bwd(d_act, dact_dgate, dact_dup):
    return d_act * dact_dgate.astype(F32), d_act * dact_dup.astype(F32)


GATHER_GROUPS = {"in": ("w_in", "conv_w"), "mid": ("w_attn_out", "w_conv_out", "w_o", "w_cq", "w_ckv", "w_co"),
                 "gate": ("w_gate",), "up": ("w_up",), "down": ("w_down",)}


def _local_step(xs, mems, target, small, fetch, reduce):
    s, d = xs.shape
    w4 = {}
    cos_t, sin_t = _rope_tables(s)

    def near(group, done, then, after):
        waits = [("direct", group)] + ([("pass_near", done), ("pass_far", done)] if done else [])
        starts = [("forward", group), ("pass_near", group)] + [("direct", g) for g in then]
        tok = fetch.step("gather_near_" + group, waits, starts, after)
        if done:
            w4.update(fetch.arrays(done))
        return tok

    def far(group, then, after):
        return fetch.step("gather_far_" + group, [("forward", group)], [("pass_far", group)] + [("direct", g) for g in then], after)

    def last(group, after):
        tok = fetch.step("gather_done_" + group, [("pass_near", group), ("pass_far", group)], [], after)
        w4.update(fetch.arrays(group))
        return tok

    h = _rmsnorm(xs, small["g_mix"], "norm_mix")
    slots_filled = [a for g in ("gate", "up", "down") for a in fetch.arrays(g).values()]
    chip_x, chip_y = reduce.place[0] // 2, reduce.place[0] % 2
    own_block = jnp.stack([2 * chip_x + chip_y]).astype(jnp.int32)
    near_blocks = jnp.stack([2 * (1 - chip_x) + chip_y, 2 * chip_x + (1 - chip_y)]).astype(jnp.int32)
    far_block = jnp.stack([2 * (1 - chip_x) + (1 - chip_y)]).astype(jnp.int32)
    z = _matmul_column_blocks(h, fetch.arrays("in")["w_in"], own_block, None, tm=512, name="in_proj_own")
    tok = near("in", None, ["mid"], [z] + slots_filled)
    memn = _rmsnorm(mems, small["g_mem"], "norm_mem")
    tok = fetch.step("gather_near_done_in", [("pass_near", "in")], [], [tok, cos_t, sin_t, memn])
    z = _matmul_column_blocks(h, fetch.arrays("in")["w_in"], near_blocks, z, tm=1024, name="in_proj_near", after=tok)
    tok = far("in", [], z)
    tok = fetch.step("gather_done_in", [("pass_far", "in")], [], tok)
    w4.update(fetch.arrays("in"))
    z = _matmul_column_blocks(h, w4["w_in"], far_block, z, tm=1024, name="in_proj_far", after=tok)
    conv4 = w4["conv_w"]
    conv_w = conv4[:, :3, :].transpose(1, 0, 2).reshape(3, N_CHIPS * conv4.shape[2])
    c_in = w4["w_in"].shape[2]
    tok = near("mid", None, ["gate"], z)
    q_rot, k_rot, v_b = _rope_fwd(z, cos_t, sin_t)
    attn = _swa_fwd(q_rot, k_rot, v_b, small["sink"])
    co = _conv_fwd(z, conv_w)
    tok = far("mid", ["up"], attn)
    tok = last("mid", tok)
    w_o = w4["w_o"].reshape(-1, w4["w_o"].shape[-1])
    c_d = w4["w_attn_out"].shape[2]
    ya = _matmul(attn, w4["w_attn_out"], mode="nn", tm=2048, tn=c_d, out_dtypes=[F32], name="attn_out_proj",
                 b_blocks=N_CHIPS, after=tok)
    yc = _matmul(co, w4["w_conv_out"], mode="nn", tm=2048, tn=c_d, out_dtypes=[F32], name="conv_out_proj",
                 b_blocks=N_CHIPS)
    mix = _gate_fwd(z, small["b_gate"], ya, yc)
    x1 = _matmul(mix, w_o, mode="nn", tm=1024, tn=1024, out_dtypes=[F32], name="mix_out_proj", extras=[xs],
                 epilogue=_add_residual)
    tok = near("gate", None, ["down"], x1)
    w_cq = w4["w_cq"].reshape(-1, w4["w_cq"].shape[-1])
    w_ckv = w4["w_ckv"].reshape(-1, w4["w_ckv"].shape[-1])
    hc = _rmsnorm(x1, small["g_cross"], "norm_cross")
    qc = _matmul(hc, w_cq, mode="nn", tm=2048, tn=MEM_WIDTH, out_dtypes=[BF16], name="cross_q_proj", after=tok)
    kvc = _matmul(memn, w_ckv, mode="nn", tm=256, tn=2 * MEM_WIDTH, out_dtypes=[BF16], name="cross_kv_proj")
    oc = _cross_fwd(qc, kvc)
    tok = far("gate", [], oc)
    x2 = _matmul(oc, w4["w_co"], mode="nn", tm=2048, tn=c_d, out_dtypes=[F32], name="cross_out_proj",
                 extras=[x1], epilogue=_add_residual, b_blocks=N_CHIPS, after=tok)
    hf = _rmsnorm(x2, small["g_ffn"], "norm_ffn")
    tok = near("up", "gate", [], hf)
    c_ff = w4["w_gate"].shape[2]
    gate = _matmul(hf, w4["w_gate"], mode="nn", tm=1024, tn=c_ff, out_dtypes=[F32], name="ffn_gate_proj", b_blocks=N_CHIPS,
                   after=tok)
    tok = far("up", [], gate)
    tok = near("down", "up", [], tok)
    act, dact_dgate, dact_dup = _matmul(hf, w4["w_up"], mode="nn", tm=1024, tn=c_ff, out_dtypes=[BF16, BF16, BF16],
                                        name="ffn_up_proj", extras=[gate], epilogue=_swiglu_fwd, b_blocks=N_CHIPS, after=tok)
    tok = far("down", [], act)
    last("down", tok)
    w_down = w4["w_down"].reshape(-1, w4["w_down"].shape[-1])
    x3 = _matmul(act, w_down, mode="nn", tm=512, tn=512, out_dtypes=[F32], name="ffn_down_proj", extras=[x2],
                 epilogue=_add_residual)
    dx3, dx3b, sq, dg_final = _loss_head(x3, small["g_final"], target)

    da, du = _matmul(dx3b, w_down, mode="nt", tm=1024, tn=c_ff, out_dtypes=[BF16, BF16], name="ffn_down_bwd",
                     extras=[dact_dgate, dact_dup], epilogue=_swiglu_bwd)
    core = reduce.core
    ffn_shape = dict(row_sharded=False, tm=1024, tn=c_ff)
    g_down = _matmul(act, dx3b, mode="tn", tm=c_ff, tn=1024, out_dtypes=[BF16], name="ffn_down_wgrad")
    tok = reduce.add("down", {"w_down": g_down}, da)
    t_gate = _wgrad_half(hf, da, core, theirs=True, name="ffn_gate_wgrad_theirs", after=tok, **ffn_shape)
    tok = reduce.step("down", t_gate)
    t_up = _wgrad_half(hf, du, core, theirs=True, name="ffn_up_wgrad_theirs", after=tok, **ffn_shape)
    tok = reduce.send("ffn", {"w_gate": t_gate, "w_up": t_up}, dx3b)
    dhf = _matmul(da, w4["w_gate"], mode="nt", tm=512, tn=1024, out_dtypes=[F32], name="ffn_gate_bwd", b_blocks=N_CHIPS,
                  after=tok)
    got = reduce.received("ffn", dhf)
    p_gate = _wgrad_half(hf, da, core, theirs=False, name="ffn_gate_wgrad_mine", add=got["w_gate"], **ffn_shape)
    p_up = _wgrad_half(hf, du, core, theirs=False, name="ffn_up_wgrad_mine", add=got["w_up"], **ffn_shape)
    tok = reduce.add_parts("ffn", {"w_gate": p_gate, "w_up": p_up})
    dhf = _matmul(du, w4["w_up"], mode="nt", tm=512, tn=1024, out_dtypes=[F32], name="ffn_up_bwd", extras=[dhf],
                  epilogue=_add_residual, b_blocks=N_CHIPS, after=tok)
    tok = reduce.step("down", dhf)
    dx2, dx2b, dg_ffn = _rmsnorm_bwd(dhf, x2, small["g_ffn"], dx3, "norm_ffn_bwd")

    d_oc = _matmul(dx2b, w4["w_co"], mode="nt", tm=1024, tn=MEM_WIDTH, out_dtypes=[BF16], name="cross_out_bwd",
                   b_blocks=N_CHIPS, after=tok)
    g_co = _matmul(oc, dx2b, mode="tn", tm=MEM_WIDTH, tn=c_d, out_dtypes=[BF16], name="cross_out_wgrad", out_blocks=N_CHIPS)
    tok = reduce.step("down", g_co)
    dqc, dkvc = _cross_bwd(qc, kvc, d_oc)
    g_cq = _matmul(hc, dqc, mode="tn", tm=1024, tn=MEM_WIDTH, out_dtypes=[BF16], name="cross_q_wgrad", after=tok)
    dhc = _matmul(dqc, w_cq, mode="nt", tm=1024, tn=1024, out_dtypes=[F32], name="cross_q_bwd")
    g_ckv = _matmul(memn, dkvc, mode="tn", tm=1024, tn=2 * MEM_WIDTH, out_dtypes=[BF16], name="cross_kv_wgrad")
    dx1, dx1b, dg_cross = _rmsnorm_bwd(dhc, x1, small["g_cross"], dx2, "norm_cross_bwd")

    dmix = _matmul(dx1b, w_o, mode="nt", tm=1024, tn=1024, out_dtypes=[F32], name="mix_out_bwd")
    g_o = _matmul(mix, dx1b, mode="tn", tm=1024, tn=1024, out_dtypes=[BF16], name="mix_out_wgrad")
    dya, dyc, dz, db_a, db_c = _gate_bwd(z, small["b_gate"], ya, yc, dmix)
    d_attn = _matmul(dya, w4["w_attn_out"], mode="nt", tm=1024, tn=ATTN_WIDTH, out_dtypes=[BF16], name="attn_out_bwd",
                     b_blocks=N_CHIPS)
    g_ao = _matmul(attn, dya, mode="tn", tm=ATTN_WIDTH, tn=c_d, out_dtypes=[BF16], name="attn_out_wgrad", out_blocks=N_CHIPS)
    d_co = _matmul(dyc, w4["w_conv_out"], mode="nt", tm=1024, tn=CONV_WIDTH, out_dtypes=[F32], name="conv_out_bwd",
                   b_blocks=N_CHIPS)
    g_cvo = _matmul(co, dyc, mode="tn", tm=CONV_WIDTH, tn=c_d, out_dtypes=[BF16], name="conv_out_wgrad", out_blocks=N_CHIPS)
    tok = reduce.step("ffn", g_cvo)
    tok = reduce.add("mid", {"w_co": g_co, "w_cq": g_cq, "w_ckv": g_ckv, "w_o": g_o, "w_attn_out": g_ao, "w_conv_out": g_cvo}, tok)
    dz, d_conv_w = _conv_bwd(z, conv_w, d_co, dz)
    dq_rot, dk_rot, dv, dsink = _swa_bwd(q_rot, k_rot, v_b, d_attn, small["sink"])
    tok = reduce.step("mid", dq_rot)
    dz = _rope_bwd(dq_rot, dk_rot, dv, cos_t, sin_t, dz)
    in_shape = dict(row_sharded=False, tm=1024, tn=c_in)
    t_in = _wgrad_half(h, dz, core, theirs=True, name="in_proj_wgrad_theirs", after=tok, **in_shape)
    tok = reduce.send("in", {"w_in": t_in}, dk_rot)
    tok = reduce.step("ffn", tok, count=1)
    dmemn = _matmul(dkvc, w_ckv, mode="nt", tm=256, tn=1024, out_dtypes=[F32], name="cross_kv_bwd", after=tok)
    _, _, dg_mem = _rmsnorm_bwd(dmemn, mems, small["g_mem"], None, "norm_mem_bwd")
    got = reduce.received("in", dg_mem)
    p_in = _wgrad_half(h, dz, core, theirs=False, name="in_proj_wgrad_mine", add=got["w_in"], **in_shape)
    tok = reduce.add_parts("in", {"w_in": p_in})
    tok = reduce.step("ffn", tok)
    tok = reduce.step("mid", tok)
    dh = _matmul(dz, w4["w_in"], mode="nt", tm=512, tn=512, out_dtypes=[F32], name="in_proj_bwd", b_blocks=N_CHIPS,
                 after=tok)
    grad_x, _, dg_mix = _rmsnorm_bwd(dh, xs, small["g_mix"], dx1, "norm_mix_bwd")

    small_grads = {
        "g_mix": dg_mix, "sink": dsink[:, 0], "b_gate": jnp.concatenate([db_a, db_c], axis=1), "g_cross": dg_cross,
        "g_mem": dg_mem, "g_ffn": dg_ffn, "g_final": dg_final, "conv_w": d_conv_w,
    }
    return sq, grad_x, small_grads


def _pair_sum(g4, ra, core, name):
    nb, rs, cs = g4.shape
    rh = rs // 2
    tr = _row_tile(rh, 256)
    per = rh // tr

    def body(c_ref, g_ref, r_ref, o_ref):
        o_ref[...] = (g_ref[...].astype(F32) + r_ref[...].astype(F32)).astype(BF16)

    plain = pl.BlockSpec((None, tr, cs), lambda j, i, c: (j, i, 0))
    return pl.pallas_call(
        body, name=name,
        grid_spec=pltpu.PrefetchScalarGridSpec(
            num_scalar_prefetch=1, grid=(nb, per),
            in_specs=[pl.BlockSpec((None, tr, cs), lambda j, i, c: (j, c[0] * per + i, 0)), plain],
            out_specs=plain),
        out_shape=jax.ShapeDtypeStruct((nb, rh, cs), BF16),
        compiler_params=_params(2),
    )(core, g4, ra)


def _adamw_update(w, g, m, v):
    nm = ADAM_B1 * m + (1.0 - ADAM_B1) * g
    nv = ADAM_B2 * v + (1.0 - ADAM_B2) * (g * g)
    m_hat = nm / ADAM_C1
    v_hat = nv / ADAM_C2
    return -ADAM_LR * (m_hat / (jnp.sqrt(v_hat) + ADAM_EPS) + ADAM_WD * w), nm, nv


ADAMW_STEPS = 4
ADAMW_BYTES_PER_ELEMENT = 40


def _adamw_calls(names, shards):
    step_bytes = sum(shards[n].size // (2 * ADAMW_STEPS) * ADAMW_BYTES_PER_ELEMENT for n in names)
    return [list(names)] if 2 * step_bytes <= VMEM_LIMIT_BYTES * 3 // 4 else [[n] for n in names]


def _adamw_row_tiles(ws):
    for w in ws:
        assert w.shape[0] % (2 * ADAMW_STEPS * BF16_SUBLANES) == 0, w.shape
    return [w.shape[0] // (2 * ADAMW_STEPS) for w in ws]


def _adamw_own_half(ws, ms, vs, parts, rcs, place, name, after=None):
    n = len(ws)

    def body(p_ref, *refs):
        ins, outs = refs[:5 * n], refs[len(refs) - 5 * n:]
        for k in range(n):
            w_ref, m_ref, v_ref, own_ref, r_ref = ins[5 * k:5 * k + 5]
            gx_ref, g_ref, d_ref, nm_ref, nv_ref = outs[5 * k:5 * k + 5]
            g = own_ref[...].astype(F32)
            for j in range(r_ref.shape[0]):
                g = g + r_ref[j].astype(F32)
            gx_ref[...] = g
            g_ref[...] = g
            d_ref[...], nm_ref[...], nv_ref[...] = _adamw_update(w_ref[...], g, m_ref[...], v_ref[...])

    in_specs, out_specs, out_shape, operands = [], [], [], []
    for w, m, v, p, r, tr in zip(ws, ms, vs, parts, rcs, _adamw_row_tiles(ws), strict=True):
        cols = w.shape[1]
        mine = pl.BlockSpec((tr, cols), lambda i, pos: (pos[1] * ADAMW_STEPS + i, 0))
        in_specs += [mine, mine, mine, pl.BlockSpec((None, tr, cols), lambda i, pos: (pos[0], i, 0)),
                     pl.BlockSpec((r.shape[0], tr, cols), lambda i, pos: (0, i, 0))]
        out_specs += [mine] * 5
        out_shape += [jax.ShapeDtypeStruct(w.shape, F32)] * 5
        operands += [w, m, v, p, r]
    outs = pl.pallas_call(
        body, name=name,
        grid_spec=pltpu.PrefetchScalarGridSpec(
            num_scalar_prefetch=1, grid=(ADAMW_STEPS,),
            in_specs=in_specs + ([] if after is None else [ANY]), out_specs=out_specs),
        out_shape=out_shape,
        compiler_params=_params(1),
    )(place, *operands, *([] if after is None else [after]))
    return [tuple(outs[5 * k:5 * k + 5]) for k in range(n)]


def _adamw_other_half(ws, ms, vs, exchanged, halves, place, name, after=None):
    n = len(ws)

    def body(p_ref, *refs):
        ins, outs = refs[:8 * n], refs[len(refs) - 4 * n:]
        for k in range(n):
            w_ref, m_ref, v_ref, gx_ref = ins[8 * k:8 * k + 4]
            g_ref, d_ref, nm_ref, nv_ref = outs[4 * k:4 * k + 4]
            gv = gx_ref[...]
            g_ref[...] = gv
            d_ref[...], nm_ref[...], nv_ref[...] = _adamw_update(w_ref[...], gv, m_ref[...], v_ref[...])

    in_specs, out_specs, out_shape, operands, aliases = [], [], [], [], {}
    for k, (w, m, v, gx, half, tr) in enumerate(zip(ws, ms, vs, exchanged, halves, _adamw_row_tiles(ws), strict=True)):
        other = pl.BlockSpec((tr, w.shape[1]), lambda i, pos: ((1 - pos[1]) * ADAMW_STEPS + i, 0))
        in_specs += [other] * 4 + [ANY] * 4
        out_specs += [other] * 4
        out_shape += [jax.ShapeDtypeStruct(w.shape, F32)] * 4
        operands += [w, m, v, gx, *half]
        aliases.update({1 + 8 * k + 4 + j: 4 * k + j for j in range(4)})
    outs = pl.pallas_call(
        body, name=name,
        grid_spec=pltpu.PrefetchScalarGridSpec(
            num_scalar_prefetch=1, grid=(ADAMW_STEPS,),
            in_specs=in_specs + ([] if after is None else [ANY]), out_specs=out_specs),
        out_shape=out_shape,
        input_output_aliases=aliases,
        compiler_params=_params(1),
    )(place, *operands, *([] if after is None else [after]))
    return [tuple(outs[4 * k:4 * k + 4]) for k in range(n)]


def _cast_to_slot(w, place, dtype, name, after=None):
    rows, cols = w.shape
    tr = _row_tile(rows, 1024)

    def body(p_ref, w_ref, *rest):
        o_ref = rest[-1]
        o_ref[...] = w_ref[...].astype(dtype)

    return pl.pallas_call(
        body, name=name,
        grid_spec=pltpu.PrefetchScalarGridSpec(
            num_scalar_prefetch=1, grid=(rows // tr,),
            in_specs=[pl.BlockSpec((tr, cols), lambda i, p: (i, 0))] + ([] if after is None else [ANY]),
            out_specs=pl.BlockSpec((None, tr, cols), lambda i, p: (p[0], i, 0))),
        out_shape=jax.ShapeDtypeStruct((N_CHIPS, rows, cols), dtype),
        compiler_params=_params(1),
    )(place, w, *([] if after is None else [after]))


def _adamw(w, g, m, v, name, after=None):
    rows, cols = w.shape
    tr = _row_tile(rows, 256)

    def body(w_ref, g_ref, m_ref, v_ref, *rest):
        go_ref, d_ref, nm_ref, nv_ref = rest[-4:]
        gv = g_ref[...]
        go_ref[...] = gv
        d_ref[...], nm_ref[...], nv_ref[...] = _adamw_update(w_ref[...], gv, m_ref[...], v_ref[...])

    tile = pl.BlockSpec((tr, cols), lambda i: (i, 0))
    shape = jax.ShapeDtypeStruct((rows, cols), F32)
    return pl.pallas_call(
        body, name=name, grid=(rows // tr,),
        in_specs=[tile] * 4 + ([] if after is None else [ANY]), out_specs=[tile] * 4, out_shape=[shape] * 4,
        compiler_params=_params(1),
    )(w, g, m, v, *([] if after is None else [after]))


def _mesh_pos():
    return lax.axis_index("x"), lax.axis_index("y"), lax.axis_index("c")


def _other_chips(x, y):
    return [(1 - x, y), (x, 1 - y), (1 - x, 1 - y)]


def _half_rows(ref, which):
    rh = ref.shape[-2] // 2
    return ref.at[pl.ds(which * rh, rh), :]


def _remote(src, dst, send_sems, recv_sems, sem, to):
    return pltpu.make_async_remote_copy(src_ref=src, dst_ref=dst, send_sem=send_sems.at[sem], recv_sem=recv_sems.at[sem],
                                        device_id=to, device_id_type=MESH)


HBM = pl.BlockSpec(memory_space=pltpu.HBM)
SEM = pl.BlockSpec(memory_space=pltpu.SEMAPHORE)
DATAFLOW_EFFECT = pltpu.SideEffectType.DATAFLOW_SIDE_EFFECTING


def _in_hbm(arrays):
    return [pltpu.with_memory_space_constraint(a, pltpu.HBM) for a in arrays]


def _hbm_like(arrays):
    return [pltpu.HBM(a.shape, a.dtype) for a in arrays]


GATHER_COPIES_PER_ARRAY = {"direct": 2, "forward": 2, "pass_near": 2, "pass_far": 1}


def _gather_copies(kind, refs, x, y, c):
    me, near_x, near_y, far = 2 * x + y, 2 * (1 - x) + y, 2 * x + (1 - y), 2 * (1 - x) + (1 - y)
    to_x, to_y, sibling = (1 - x, y, c), (x, 1 - y, c), (x, y, 1 - c)
    out = []
    for ref in refs:
        rh = ref.shape[1] // 2
        rq = rh // 2

        def half(chip, ref=ref, rh=rh):
            return ref.at[chip, pl.ds(c * rh, rh), :]

        def quarter(chip, q, ref=ref, rh=rh, rq=rq):
            return ref.at[chip, pl.ds(c * rh + q * rq, rq), :]

        if kind == "direct":
            out += [(half(me), half(me), to_x), (half(me), half(me), to_y)]
        elif kind == "forward":
            out += [(quarter(near_x, 0), quarter(near_x, 0), to_y), (quarter(near_y, 1), quarter(near_y, 1), to_x)]
        elif kind == "pass_near":
            out += [(half(near_x), half(near_x), sibling), (half(near_y), half(near_y), sibling)]
        else:
            assert kind == "pass_far"
            out += [(half(far), half(far), sibling)]
    return out


def _gather_step(name, bufs, waits, starts, after):
    nb, nw, ns = len(bufs), len(waits), len(starts)
    after = [] if after is None else list(after) if isinstance(after, (list, tuple)) else [after]
    n_after = len(after)

    def body(*refs):
        ins = refs[:nb]
        wait_sems = refs[nb:nb + 2 * nw]
        start_sems = refs[nb + 2 * nw + n_after:nb + 2 * nw + n_after + 2 * ns]
        token = refs[-1]
        x, y, c = _mesh_pos()
        for j, (kind, idxs, _, _) in enumerate(waits):
            for i, (s_ref, d_ref, to) in enumerate(_gather_copies(kind, [ins[t] for t in idxs], x, y, c)):
                came = _remote(s_ref, d_ref, wait_sems[2 * j], wait_sems[2 * j + 1], i, to)
                came.wait_recv()
                came.wait_send()
        for j, (kind, idxs) in enumerate(starts):
            for i, (s_ref, d_ref, to) in enumerate(_gather_copies(kind, [ins[t] for t in idxs], x, y, c)):
                _remote(s_ref, d_ref, start_sems[2 * j], start_sems[2 * j + 1], i, to).start()
        token[...] = jnp.zeros_like(token)

    sems = []
    for kind, idxs in starts:
        sems += [pltpu.SemaphoreType.DMA((GATHER_COPIES_PER_ARRAY[kind] * len(idxs),))] * 2
    operands = _in_hbm(bufs) + [sem for w in waits for sem in w[2:]] + after
    outs = pl.pallas_call(
        body, name=name,
        in_specs=[HBM] * nb + [SEM] * (2 * nw) + [ANY] * n_after,
        out_specs=[SEM] * (2 * ns) + [HBM] * nb + [pl.BlockSpec(memory_space=pltpu.VMEM)],
        out_shape=sems + _hbm_like(bufs) + [jax.ShapeDtypeStruct((8, 128), F32)],
        input_output_aliases={i: 2 * ns + i for i in range(nb)},
        compiler_params=pltpu.CompilerParams(has_side_effects=DATAFLOW_EFFECT),
    )(*operands)
    return outs[2 * ns:2 * ns + nb], [(outs[2 * j], outs[2 * j + 1]) for j in range(ns)], outs[-1]


class _Gather:
    def __init__(self, groups):
        self.groups = groups
        self.bufs = {}
        self.in_flight = {}

    def put(self, slotted):
        self.bufs.update(slotted)

    def step(self, name, waits, starts, after=None):
        names = []
        for _, group in list(waits) + list(starts):
            names += [n for n in self.groups[group] if n not in names]
        index = {n: i for i, n in enumerate(names)}

        def members(group):
            return [index[n] for n in self.groups[group]]

        wait_args = [(kind, members(group)) + self.in_flight.pop((kind, group)) for kind, group in waits]
        start_args = [(kind, members(group)) for kind, group in starts]
        bufs, sems, token = _gather_step(name, [self.bufs[n] for n in names], wait_args, start_args, after)
        self.bufs.update(zip(names, bufs))
        for (kind, group), pair in zip(starts, sems):
            self.in_flight[(kind, group)] = pair
        return token

    def arrays(self, group):
        return {n: self.bufs[n] for n in self.groups[group]}


def _sibling_halves_copies(srcs, dsts, x, y, c):
    out = []
    for s_ref, d_ref in zip(srcs, dsts, strict=True):
        rh = s_ref.shape[1] // 2
        out.append((s_ref.at[:, pl.ds((1 - c) * rh, rh), :], d_ref, (x, y, 1 - c)))
    return out


def _to_sibling_copies(srcs, dsts, x, y, c):
    return [(s_ref, d_ref, (x, y, 1 - c)) for s_ref, d_ref in zip(srcs, dsts, strict=True)]


def _chip_copies(srcs, dsts, x, y, c):
    out = []
    for s_ref, d_ref in zip(srcs, dsts, strict=True):
        for k, (px, py) in enumerate(_other_chips(x, y)):
            out.append((s_ref.at[2 * px + py], d_ref.at[k], (px, py, c)))
    return out


def _join_copies(srcs, dsts, x, y, c):
    out = []
    for s_ref in srcs:
        mine = _half_rows(s_ref, c)
        out.append((mine, mine, (x, y, 1 - c)))
    return out


def _exchange_start(copies_fn, n_copies, srcs, fresh, after, name):
    ns, nb = len(srcs), len(srcs) + len(fresh)

    def body(*refs):
        bufs, send, recv, token = refs[:nb], refs[nb + 1], refs[nb + 2], refs[-1]
        x, y, c = _mesh_pos()
        for i, (s_ref, d_ref, to) in enumerate(copies_fn(bufs[:ns], bufs[ns:] if fresh else bufs[:ns], x, y, c)):
            _remote(s_ref, d_ref, send, recv, i, to).start()
        token[...] = jnp.zeros_like(token)

    sems = [pltpu.SemaphoreType.DMA((n_copies,))] * 2
    outs = pl.pallas_call(
        body, name=name,
        in_specs=[HBM] * nb + [ANY], out_specs=[SEM, SEM] + [HBM] * nb + [pl.BlockSpec(memory_space=pltpu.VMEM)],
        out_shape=sems + _hbm_like(list(srcs) + list(fresh)) + [jax.ShapeDtypeStruct((8, 128), F32)],
        input_output_aliases={i: 2 + i for i in range(nb)},
        compiler_params=pltpu.CompilerParams(has_side_effects=DATAFLOW_EFFECT),
    )(*_in_hbm(list(srcs) + list(fresh)), after)
    return outs[0], outs[1], outs[2:2 + ns], outs[2 + ns:2 + nb], outs[-1]


def _exchange_done(copies_fn, srcs, fresh, send, recv, after, name):
    ns, nb = len(srcs), len(srcs) + len(fresh)

    def body(*refs):
        bufs, send_in, recv_in = refs[:nb], refs[nb], refs[nb + 1]
        x, y, c = _mesh_pos()
        for i, (s_ref, d_ref, to) in enumerate(copies_fn(bufs[:ns], bufs[ns:] if fresh else bufs[:ns], x, y, c)):
            came = _remote(s_ref, d_ref, send_in, recv_in, i, to)
            came.wait_send()
            came.wait_recv()

    outs = pl.pallas_call(
        body, name=name,
        in_specs=[HBM] * nb + [SEM, SEM, ANY], out_specs=[HBM] * nb,
        out_shape=_hbm_like(list(srcs) + list(fresh)),
        input_output_aliases={i: i for i in range(nb)},
        compiler_params=pltpu.CompilerParams(has_side_effects=DATAFLOW_EFFECT),
    )(*_in_hbm(list(srcs) + list(fresh)), send, recv, after)
    return outs[:ns], outs[ns:]


class _Reduce:
    def __init__(self, place, core, shards, mom_m, mom_v):
        self.place, self.core = place, core
        self.shards, self.mom_m, self.mom_v = shards, mom_m, mom_v
        self.state = {}
        self.results = {}

    def add(self, group, grads, after):
        names = list(grads)
        g4s = [g.reshape((N_CHIPS, -1, g.shape[-1])) if g.ndim == 2 else g for g in grads.values()]
        fresh = [lax.empty((N_CHIPS, g.shape[1] // 2, g.shape[2]), BF16) for g in g4s]
        send, recv, g4s, fresh, token = _exchange_start(_sibling_halves_copies, len(names), g4s, fresh, after,
                                                        "pair_start_" + group)
        self.state[group] = (0, names, send, recv, g4s, fresh)
        return token

    def send(self, group, theirs, after):
        names, srcs = list(theirs), list(theirs.values())
        fresh = [lax.empty(s.shape, BF16) for s in srcs]
        send, recv, srcs, fresh, token = _exchange_start(_to_sibling_copies, len(names), srcs, fresh, after, "pair_start_" + group)
        self.state[group] = ("sent", names, send, recv, srcs, fresh)
        return token

    def received(self, group, after):
        stage, names, send, recv, srcs, fresh = self.state.pop(group)
        assert stage == "sent"
        _, got = _exchange_done(_to_sibling_copies, srcs, fresh, send, recv, after, "pair_done_" + group)
        return dict(zip(names, got))

    def add_parts(self, group, parts):
        names, srcs = list(parts), list(parts.values())
        fresh = [lax.empty((N_CHIPS - 1,) + p.shape[1:], BF16) for p in srcs]
        send, recv, srcs, fresh, token = _exchange_start(_chip_copies, 3 * len(names), srcs, fresh, self.core, "chips_start_" + group)
        self.state[group] = (1, names, send, recv, srcs, fresh)
        return token

    def step(self, group, after, count=None):
        stage, names, send, recv, srcs, fresh = self.state[group]
        if stage == 0:
            g4s, ras = _exchange_done(_sibling_halves_copies, srcs, fresh, send, recv, after, "pair_done_" + group)
            parts = [_pair_sum(g, r, self.core, "pair_sum_" + n) for g, r, n in zip(g4s, ras, names)]
            fresh = [lax.empty((N_CHIPS - 1,) + p.shape[1:], BF16) for p in parts]
            send, recv, parts, fresh, token = _exchange_start(_chip_copies, 3 * len(names), parts, fresh, self.core,
                                                              "chips_start_" + group)
            self.state[group] = (1, names, send, recv, parts, fresh)
            return token
        if stage == 1:
            parts, rcs = _exchange_done(_chip_copies, srcs, fresh, send, recv, after, "chips_done_" + group)
            token = None
            for call in _adamw_calls(names, self.shards):
                at = [names.index(n) for n in call]
                done = _adamw_own_half(*[[held[n] for n in call] for held in (self.shards, self.mom_m, self.mom_v)],
                                       [parts[i] for i in at], [rcs[i] for i in at], self.place,
                                       "adamw_own_" + (call[0] if len(call) == 1 else group), after=token)
                self.results.update(zip(call, done))
                token = done[-1][2]
            wholes = [self.results[n][0] for n in names]
            send, recv, wholes, _, token = _exchange_start(_join_copies, len(names), wholes, [], token, "join_start_" + group)
            self.state[group] = (2, names, send, recv, wholes, [])
            return token
        assert stage in (2, 3)
        if stage == 2:
            srcs, _ = _exchange_done(_join_copies, srcs, [], send, recv, after, "join_done_" + group)
            after = None
        token = after
        count = len(names) if count is None else count
        for call in _adamw_calls(names[:count], self.shards):
            at = [names.index(n) for n in call]
            done = _adamw_other_half(*[[held[n] for n in call] for held in (self.shards, self.mom_m, self.mom_v)],
                                     [srcs[i] for i in at], [self.results[n][1:] for n in call], self.place,
                                     "adamw_other_" + (call[0] if len(call) == 1 else group), after=token)
            self.results.update(zip(call, done))
            token = done[-1][1]
        if count < len(names):
            self.state[group] = (3, names[count:], None, None, srcs[count:], [])
        else:
            del self.state[group]
        return token


N_DEV = 8


def _to_all_copies(srcs, dsts, x, y, c):
    out = []
    for r in range(1, N_DEV):
        fx, fy, fc = (r >> 2) & 1, (r >> 1) & 1, r & 1
        out.append((srcs[0], dsts[0].at[r - 1], (x + fx - 2 * x * fx, y + fy - 2 * y * fy, c + fc - 2 * c * fc)))
    return out


def _all_reduce_small_start(v, after):
    slots = lax.empty((N_DEV - 1,) + v.shape, v.dtype)
    send, recv, (v,), (slots,), token = _exchange_start(_to_all_copies, N_DEV - 1, [v], [slots], after, "small_grads_start")
    return (send, recv, v, slots), token


def _all_reduce_small_done(started, after):
    send, recv, v, slots = started
    (v,), (slots,) = _exchange_done(_to_all_copies, [v], [slots], send, recv, after, "small_grads_done")

    def body(v_ref, slots_ref, o_ref):
        x, y, c = _mesh_pos()
        me = 4 * x + 2 * y + c
        acc = None
        for i in range(N_DEV):
            r = jnp.bitwise_xor(me, i)
            part = jnp.where(r == 0, v_ref[...], slots_ref[jnp.maximum(r - 1, 0)])
            acc = part if acc is None else acc + part
        o_ref[...] = acc

    vm = pl.BlockSpec(memory_space=pltpu.VMEM)
    return pl.pallas_call(body, name="small_grads_sum", in_specs=[vm, vm], out_specs=vm,
                          out_shape=jax.ShapeDtypeStruct(v.shape, v.dtype))(v, slots)


MATRICES = ("w_in", "w_attn_out", "w_conv_out", "w_o", "w_cq", "w_ckv", "w_co", "w_gate", "w_up", "w_down")
VECTORS = ("g_mix", "b_gate", "g_cross", "g_mem", "g_ffn", "g_final", "conv_w", "sink")
WEIGHT_ORDER = ("g_mix", "w_in", "sink", "conv_w", "b_gate", "w_attn_out", "w_conv_out", "w_o", "g_cross", "g_mem", "w_cq",
                "w_ckv", "w_co", "g_ffn", "w_gate", "w_up", "w_down", "g_final")
CONV_PAD_ROWS = 32
SMALL_ROWS = 8


def _pack(pieces):
    flat = jnp.concatenate([p.reshape(-1) for p in pieces])
    lane_group = SMALL_ROWS * 128
    total = -(-flat.shape[0] // lane_group) * lane_group
    flat = jnp.pad(flat, (0, total - flat.shape[0]))
    return flat.reshape(SMALL_ROWS, total // SMALL_ROWS), [p.size for p in pieces]


def _unpack(packed, pieces):
    flat = packed.reshape(-1)
    out, off = [], 0
    for p in pieces:
        out.append(flat[off:off + p.size].reshape(p.shape))
        off += p.size
    return out


def kernel(x, mem, g_mix, w_in, sink, conv_w, b_gate, w_attn_out, w_conv_out, w_o, g_cross, g_mem, w_cq, w_ckv, w_co, g_ffn, w_gate, w_up, w_down, g_final, loss_target, m_g_mix, m_w_in, m_sink, m_conv_w, m_b_gate, m_w_attn_out, m_w_conv_out, m_w_o, m_g_cross, m_g_mem, m_w_cq, m_w_ckv, m_w_co, m_g_ffn, m_w_gate, m_w_up, m_w_down, m_g_final, v_g_mix, v_w_in, v_sink, v_conv_w, v_b_gate, v_w_attn_out, v_w_conv_out, v_w_o, v_g_cross, v_g_mem, v_w_cq, v_w_ckv, v_w_co, v_g_ffn, v_w_gate, v_w_up, v_w_down, v_g_final):
    given = dict(g_mix=g_mix, w_in=w_in, sink=sink, conv_w=conv_w, b_gate=b_gate, w_attn_out=w_attn_out, w_conv_out=w_conv_out,
                 w_o=w_o, g_cross=g_cross, g_mem=g_mem, w_cq=w_cq, w_ckv=w_ckv, w_co=w_co, g_ffn=g_ffn, w_gate=w_gate, w_up=w_up,
                 w_down=w_down, g_final=g_final)
    mom_m = dict(g_mix=m_g_mix, w_in=m_w_in, sink=m_sink, conv_w=m_conv_w, b_gate=m_b_gate, w_attn_out=m_w_attn_out,
                 w_conv_out=m_w_conv_out, w_o=m_w_o, g_cross=m_g_cross, g_mem=m_g_mem, w_cq=m_w_cq, w_ckv=m_w_ckv, w_co=m_w_co,
                 g_ffn=m_g_ffn, w_gate=m_w_gate, w_up=m_w_up, w_down=m_w_down, g_final=m_g_final)
    mom_v = dict(g_mix=v_g_mix, w_in=v_w_in, sink=v_sink, conv_w=v_conv_w, b_gate=v_b_gate, w_attn_out=v_w_attn_out,
                 w_conv_out=v_w_conv_out, w_o=v_w_o, g_cross=v_g_cross, g_mem=v_g_mem, w_cq=v_w_cq, w_ckv=v_w_ckv, w_co=v_w_co,
                 g_ffn=v_g_ffn, w_gate=v_w_gate, w_up=v_w_up, w_down=v_w_down, g_final=v_g_final)
    xs, mems, target = x[0], mem[0], loss_target[0]
    d_model = xs.shape[1]
    chip = 2 * lax.axis_index("x") + lax.axis_index("y")
    core = jnp.reshape(lax.axis_index("c"), (1,)).astype(jnp.int32)
    place = jnp.stack([chip, lax.axis_index("c")]).astype(jnp.int32)

    shards = {n: given[n][0] for n in MATRICES}
    conv_cols = conv_w.shape[2]
    conv_pad = jnp.pad(conv_w[0], ((0, CONV_PAD_ROWS - conv_w.shape[1]), (0, 0)))
    fetch = _Gather(GATHER_GROUPS)
    first = {"w_in": _cast_to_slot(shards["w_in"], place, BF16, "to_slot_w_in"),
             "conv_w": _cast_to_slot(conv_pad, place, F32, "to_slot_conv_w")}
    fetch.put(first)
    tok = fetch.step("gather_start", [], [("direct", "in")])
    fetch.put({n: _cast_to_slot(shards[n], place, BF16, "to_slot_" + n, after=tok) for n in MATRICES if n != "w_in"})
    small = {n: given[n] for n in ("g_mix", "b_gate", "g_cross", "g_mem", "g_ffn")}
    small["g_final"] = g_final[None]
    small["sink"] = sink[0]

    reduce = _Reduce(place, core, shards, {n: mom_m[n][0] for n in MATRICES}, {n: mom_v[n][0] for n in MATRICES})
    sq, grad_x, small_grads = _local_step(xs, mems, target, small, fetch, reduce)

    loss_part = 0.5 * sq[0:1, 0:1] / d_model
    pieces = [small_grads[n] for n in VECTORS] + [loss_part]
    packed, _ = _pack(pieces)
    started, tok = _all_reduce_small_start(packed, core)
    tok = reduce.step("in", tok)
    tok = reduce.step("mid", tok)
    summed = _unpack(_all_reduce_small_done(started, tok), pieces)
    loss = summed[-1][0, 0]
    small_sum = dict(zip(VECTORS, summed[:-1]))
    small_sum["conv_w"] = lax.dynamic_slice_in_dim(small_sum["conv_w"], chip * conv_cols, conv_cols, axis=1)

    grad_out, delta, new_m, new_v = {}, {}, {}, {}
    like = [given[n] for n in VECTORS]
    pw, _ = _pack(like)
    pg, _ = _pack([small_sum[n] for n in VECTORS])
    pm, _ = _pack([mom_m[n] for n in VECTORS])
    pv, _ = _pack([mom_v[n] for n in VECTORS])
    _, pd, pnm, pnv = _adamw(pw, pg, pm, pv, "adamw_small")
    for n, g, d, nm, nv in zip(VECTORS, [small_sum[n] for n in VECTORS], _unpack(pd, like), _unpack(pnm, like), _unpack(pnv, like)):
        grad_out[n] = g.reshape(given[n].shape)
        delta[n], new_m[n], new_v[n] = d, nm, nv
    reduce.step("in", pd)
    for n in MATRICES:
        g, d, nm, nv = reduce.results[n]
        grad_out[n], delta[n], new_m[n], new_v[n] = g[None], d[None], nm[None], nv[None]

    return (loss, grad_x[None], *[grad_out[n] for n in WEIGHT_ORDER], *[delta[n] for n in WEIGHT_ORDER],
            *[new_m[n] for n in WEIGHT_ORDER], *[new_v[n] for n in WEIGHT_ORDER])
```

```python
import jax
import jax.numpy as jnp
from jax import lax
from jax.experimental import pallas as pl
from jax.experimental.pallas import tpu as pltpu

F32 = jnp.float32
BF16 = jnp.bfloat16
MESH = pl.DeviceIdType.MESH
ANY = pl.BlockSpec(memory_space=pl.ANY)

VMEM_LIMIT_BYTES = 56 * 1024 * 1024

N_CHIPS = 4
HEAD_DIM = 128
N_Q_HEADS = 8
N_KV_HEADS = 2
Q_GROUP = N_Q_HEADS // N_KV_HEADS
ATTN_WIDTH = N_Q_HEADS * HEAD_DIM
KV_WIDTH = N_KV_HEADS * HEAD_DIM
WINDOW = 128
BLOCK = 128
BAND = 3 * BLOCK
ROPE_THETA = 10000.0
CONV_WIDTH = 1024
MEM_HEADS = 4
MEM_WIDTH = MEM_HEADS * HEAD_DIM
RMS_EPS = 1e-6
NEG_INF = -1e30
ATTN_SCALE = HEAD_DIM ** -0.5

Q_OFF, K_OFF, V_OFF, CU_OFF, CB_OFF, CC_OFF, GL_OFF = 0, 1024, 1280, 1536, 2560, 3584, 4608

ADAM_LR = 0.001
ADAM_B1 = 0.9
ADAM_B2 = 0.999
ADAM_EPS = 1e-08
ADAM_WD = 0.01
ADAM_STEP = 10
ADAM_C1 = 1.0 - ADAM_B1 ** ADAM_STEP
ADAM_C2 = 1.0 - ADAM_B2 ** ADAM_STEP


def _params(n_grid_axes):
    return pltpu.CompilerParams(dimension_semantics=("arbitrary",) * n_grid_axes, vmem_limit_bytes=VMEM_LIMIT_BYTES)


BF16_SUBLANES = 16


def _row_tile(rows, want):
    if rows <= want:
        return rows
    for t in range(want, 0, -BF16_SUBLANES):
        if rows % t == 0:
            return t
    return rows


def _matmul(a, b, *, mode, tm, tn, out_dtypes, name, extras=(), epilogue=None, b_blocks=1, out_blocks=1, after=None):
    if mode == "tn":
        kdim, m = a.shape
    else:
        m, kdim = a.shape
    if b_blocks > 1:
        nb, brows, bcols = b.shape
        assert nb == b_blocks
        if mode == "nn":
            n = bcols * nb
            assert brows == kdim
        else:
            assert mode == "nt" and bcols * nb == kdim
            n = brows
    else:
        n = b.shape[0] if mode == "nt" else b.shape[1]
    tm, tn = min(tm, m), min(tn, n)
    tk = kdim
    assert m % tm == 0 and n % tn == 0, (name, m, n, tm, tn)
    n_extra, n_out = len(extras), len(out_dtypes)
    n_after = 0 if after is None else 1

    if mode == "tn":
        a_spec = pl.BlockSpec((tk, tm), lambda j, i, k: (k, i))
        dims = (((0,), (0,)), ((), ()))
    else:
        a_spec = pl.BlockSpec((tm, tk), lambda j, i, k: (i, k))
        dims = (((1,), (0,)), ((), ())) if mode == "nn" else (((1,), (1,)), ((), ()))

    if b_blocks > 1 and mode == "nn":
        per = b.shape[2] // tn
        assert b.shape[2] % tn == 0
        b_spec = pl.BlockSpec((None, tk, tn), lambda j, i, k: (j // per, k, j % per))
    elif b_blocks > 1:
        b_spec = pl.BlockSpec((b_blocks, tn, b.shape[2]), lambda j, i, k: (0, j, 0))
    elif mode == "nt":
        b_spec = pl.BlockSpec((tn, tk), lambda j, i, k: (j, k))
    else:
        b_spec = pl.BlockSpec((tk, tn), lambda j, i, k: (k, j))

    tile_spec = pl.BlockSpec((tm, tn), lambda j, i, k: (i, j))
    if out_blocks > 1:
        ncols = n // out_blocks
        assert ncols % tn == 0
        oper = ncols // tn
        out_spec = pl.BlockSpec((None, tm, tn), lambda j, i, k: (j // oper, i, j % oper))
        out_shape = [jax.ShapeDtypeStruct((out_blocks, m, ncols), dt) for dt in out_dtypes]
    else:
        out_spec = tile_spec
        out_shape = [jax.ShapeDtypeStruct((m, n), dt) for dt in out_dtypes]

    def body(a_ref, b_ref, *rest):
        extra_refs = rest[:n_extra]
        out_refs = rest[n_extra + n_after:n_extra + n_after + n_out]
        if mode == "nt" and b_blocks > 1:
            cs = b.shape[2]
            acc = None
            for jb in range(b_blocks):
                prod = lax.dot_general(a_ref[:, jb * cs:(jb + 1) * cs].astype(BF16), b_ref[jb].astype(BF16), dims,
                                       preferred_element_type=F32)
                acc = prod if acc is None else acc + prod
        else:
            acc = lax.dot_general(a_ref[...].astype(BF16), b_ref[...].astype(BF16), dims, preferred_element_type=F32)
        tiles = (acc,) if epilogue is None else epilogue(acc, *[r[...] for r in extra_refs])
        for o_ref, t in zip(out_refs, tiles, strict=True):
            o_ref[...] = t.astype(o_ref.dtype)

    outs = pl.pallas_call(
        body,
        name=name,
        grid=(n // tn, m // tm, 1),
        in_specs=[a_spec, b_spec] + [tile_spec] * n_extra + [ANY] * n_after,
        out_specs=[out_spec] * n_out,
        out_shape=out_shape,
        compiler_params=_params(3),
    )(a, b, *extras, *([] if after is None else [after]))
    return outs[0] if n_out == 1 else outs


def _add_residual(acc, res):
    return (acc + res,)


def _matmul_column_blocks(a, b4, blocks, out, *, tm, name, after=None):
    m, kdim = a.shape
    nb, _, cols = b4.shape
    tm = min(tm, m)
    assert m % tm == 0

    def body(j_ref, a_ref, b_ref, *rest):
        rest[-1][...] = jnp.dot(a_ref[...], b_ref[...], preferred_element_type=F32)

    extra = ([] if out is None else [out]) + ([] if after is None else [after])
    n_blocks = blocks.shape[0]
    return pl.pallas_call(
        body, name=name,
        grid_spec=pltpu.PrefetchScalarGridSpec(
            num_scalar_prefetch=1, grid=(n_blocks, m // tm),
            in_specs=[pl.BlockSpec((tm, kdim), lambda j, i, blk: (i, 0)),
                      pl.BlockSpec((None, kdim, cols), lambda j, i, blk: (blk[j], 0, 0))] + [ANY] * len(extra),
            out_specs=pl.BlockSpec((tm, cols), lambda j, i, blk: (i, blk[j]))),
        out_shape=jax.ShapeDtypeStruct((m, nb * cols), F32),
        input_output_aliases={} if out is None else {3: 0},
        compiler_params=_params(2),
    )(blocks, a, b4, *extra)


def _wgrad_half(a, b, core, *, theirs, row_sharded, tm, tn, name, add=None, after=None):
    kdim, m = a.shape
    n = b.shape[1]
    rs, cs = (m // N_CHIPS, n) if row_sharded else (m, n // N_CHIPS)
    rh = rs // 2
    tm, tn = min(tm, rh), min(tn, cs)
    assert rh % tm == 0 and cs % tn == 0, (name, rh, cs, tm, tn)
    mh, per = rh // tm, cs // tn
    has_add = add is not None

    def half(c):
        return 1 - c[0] if theirs else c[0]

    if row_sharded:
        grid = (n // tn, N_CHIPS * mh)
        a_spec = pl.BlockSpec((kdim, tm), lambda j, r, c: (0, ((r // mh) * 2 + half(c)) * mh + r % mh))
        o_spec = pl.BlockSpec((None, tm, tn), lambda j, r, c: (r // mh, r % mh, j))
    else:
        grid = (n // tn, mh)
        a_spec = pl.BlockSpec((kdim, tm), lambda j, r, c: (0, half(c) * mh + r))
        o_spec = pl.BlockSpec((None, tm, tn), lambda j, r, c: (j // per, r, j % per))
    b_spec = pl.BlockSpec((kdim, tn), lambda j, r, c: (0, j))

    def body(c_ref, a_ref, b_ref, *rest):
        o_ref = rest[-1]
        acc = lax.dot_general(a_ref[...].astype(BF16), b_ref[...].astype(BF16), (((0,), (0,)), ((), ())),
                              preferred_element_type=F32)
        if has_add:
            acc = acc + rest[0][...].astype(F32)
        o_ref[...] = acc.astype(BF16)

    operands = [a, b] + ([add] if has_add else []) + ([] if after is None else [after])
    return pl.pallas_call(
        body, name=name,
        grid_spec=pltpu.PrefetchScalarGridSpec(
            num_scalar_prefetch=1, grid=grid,
            in_specs=[a_spec, b_spec] + ([o_spec] if has_add else []) + ([] if after is None else [ANY]),
            out_specs=o_spec),
        out_shape=jax.ShapeDtypeStruct((N_CHIPS, rh, cs), BF16),
        compiler_params=_params(2),
    )(core, *operands)


def _rstd(x):
    return lax.rsqrt(jnp.mean(x * x, axis=-1, keepdims=True) + RMS_EPS)


def _rmsnorm(x, g, name):
    s, d = x.shape
    tr = _row_tile(s, 512)

    def body(x_ref, g_ref, o_ref):
        xv = x_ref[...]
        o_ref[...] = (xv * _rstd(xv) * g_ref[...]).astype(BF16)

    return pl.pallas_call(
        body, name=name, grid=(s // tr,),
        in_specs=[pl.BlockSpec((tr, d), lambda i: (i, 0)), pl.BlockSpec((1, d), lambda i: (0, 0))],
        out_specs=pl.BlockSpec((tr, d), lambda i: (i, 0)),
        out_shape=jax.ShapeDtypeStruct((s, d), BF16),
        compiler_params=_params(1),
    )(x, g)


def _rmsnorm_bwd(dh, x, g, dres, name):
    s, d = x.shape
    tr = _row_tile(s, 512)
    has_res = dres is not None

    def body(*refs):
        if has_res:
            dh_ref, x_ref, g_ref, res_ref, dx_ref, dxb_ref, dg_ref = refs
        else:
            dh_ref, x_ref, g_ref, dx_ref, dxb_ref, dg_ref = refs
        xv = x_ref[...]
        dhv = dh_ref[...].astype(F32)
        r = _rstd(xv)
        xn = xv * r
        dhg = dhv * g_ref[...]
        dx = r * (dhg - xn * jnp.mean(dhg * xn, axis=-1, keepdims=True))
        if has_res:
            dx = dx + res_ref[...]
        dx_ref[...] = dx
        dxb_ref[...] = dx.astype(BF16)
        part = jnp.sum(dhv * xn, axis=0, keepdims=True)

        @pl.when(pl.program_id(0) == 0)
        def _():
            dg_ref[...] = part

        @pl.when(pl.program_id(0) > 0)
        def _():
            dg_ref[...] += part

    row = pl.BlockSpec((tr, d), lambda i: (i, 0))
    vec = pl.BlockSpec((1, d), lambda i: (0, 0))
    return pl.pallas_call(
        body, name=name, grid=(s // tr,),
        in_specs=[row, row, vec] + ([row] if has_res else []),
        out_specs=[row, row, vec],
        out_shape=[jax.ShapeDtypeStruct((s, d), F32), jax.ShapeDtypeStruct((s, d), BF16), jax.ShapeDtypeStruct((1, d), F32)],
        compiler_params=_params(1),
    )(*([dh, x, g] + ([dres] if has_res else [])))


def _loss_head(x3, g, target):
    s, d = x3.shape
    tr = _row_tile(s, 512)

    def body(x_ref, g_ref, t_ref, dx_ref, dxb_ref, sq_ref, dg_ref):
        xv = x_ref[...]
        gv = g_ref[...]
        r = _rstd(xv)
        xn = xv * r
        err = xn * gv - t_ref[...]
        dy = err * (1.0 / d)
        dyg = dy * gv
        dx = r * (dyg - xn * jnp.mean(dyg * xn, axis=-1, keepdims=True))
        dx_ref[...] = dx
        dxb_ref[...] = dx.astype(BF16)
        sq = jnp.sum(jnp.sum(err * err, axis=1, keepdims=True), axis=0, keepdims=True)
        sq = jnp.broadcast_to(sq, (1, 128))
        part = jnp.sum(dy * xn, axis=0, keepdims=True)

        @pl.when(pl.program_id(0) == 0)
        def _():
            sq_ref[...] = sq
            dg_ref[...] = part

        @pl.when(pl.program_id(0) > 0)
        def _():
            sq_ref[...] += sq
            dg_ref[...] += part

    row = pl.BlockSpec((tr, d), lambda i: (i, 0))
    vec = pl.BlockSpec((1, d), lambda i: (0, 0))
    return pl.pallas_call(
        body, name="loss_head", grid=(s // tr,),
        in_specs=[row, vec, row],
        out_specs=[row, row, pl.BlockSpec((1, 128), lambda i: (0, 0)), vec],
        out_shape=[jax.ShapeDtypeStruct((s, d), F32), jax.ShapeDtypeStruct((s, d), BF16),
                   jax.ShapeDtypeStruct((1, 128), F32), jax.ShapeDtypeStruct((1, d), F32)],
        compiler_params=_params(1),
    )(x3, g, target)


def _rope_tables(s):
    inv = 1.0 / (ROPE_THETA ** (jnp.arange(0, HEAD_DIM, 2, dtype=F32) / HEAD_DIM))
    ang = jnp.arange(s, dtype=F32)[:, None] * inv[None, :]
    cos, sin = jnp.cos(ang), jnp.sin(ang)
    return jnp.concatenate([cos, cos], axis=1), jnp.concatenate([-sin, sin], axis=1)


def _swap_halves(t):
    return pltpu.roll(t, HEAD_DIM // 2, 1)


def _rope_fwd(z, cos_t, sin_t):
    s = z.shape[0]
    tr = _row_tile(s, 256)

    def body(zq_ref, zk_ref, zv_ref, c_ref, s_ref, q_ref, k_ref, v_ref):
        c, sn = c_ref[...], s_ref[...]
        for hd in range(N_Q_HEADS):
            cols = slice(hd * HEAD_DIM, (hd + 1) * HEAD_DIM)
            t = zq_ref[:, cols]
            q_ref[:, cols] = (t * c + _swap_halves(t) * sn).astype(BF16)
        for hd in range(N_KV_HEADS):
            cols = slice(hd * HEAD_DIM, (hd + 1) * HEAD_DIM)
            t = zk_ref[:, cols]
            k_ref[:, cols] = (t * c + _swap_halves(t) * sn).astype(BF16)
        v_ref[...] = zv_ref[...].astype(BF16)

    tab = pl.BlockSpec((tr, HEAD_DIM), lambda i: (i, 0))
    return pl.pallas_call(
        body, name="rope_fwd", grid=(s // tr,),
        in_specs=[pl.BlockSpec((tr, ATTN_WIDTH), lambda i: (i, Q_OFF // ATTN_WIDTH)),
                  pl.BlockSpec((tr, KV_WIDTH), lambda i: (i, K_OFF // KV_WIDTH)),
                  pl.BlockSpec((tr, KV_WIDTH), lambda i: (i, V_OFF // KV_WIDTH)), tab, tab],
        out_specs=[pl.BlockSpec((tr, ATTN_WIDTH), lambda i: (i, 0)), pl.BlockSpec((tr, KV_WIDTH), lambda i: (i, 0)),
                   pl.BlockSpec((tr, KV_WIDTH), lambda i: (i, 0))],
        out_shape=[jax.ShapeDtypeStruct((s, ATTN_WIDTH), BF16), jax.ShapeDtypeStruct((s, KV_WIDTH), BF16),
                   jax.ShapeDtypeStruct((s, KV_WIDTH), BF16)],
        compiler_params=_params(1),
    )(z, z, z, cos_t, sin_t)


def _rope_bwd(dq_rot, dk_rot, dv, cos_t, sin_t, dz):
    s = dq_rot.shape[0]
    tr = _row_tile(s, 256)
    qkv_width = V_OFF + KV_WIDTH

    def body(dq_ref, dk_ref, dv_ref, c_ref, s_ref, dz_in_ref, o_ref):
        c, sn = c_ref[...], s_ref[...]
        for hd in range(N_Q_HEADS):
            t = dq_ref[:, hd * HEAD_DIM:(hd + 1) * HEAD_DIM]
            o_ref[:, Q_OFF + hd * HEAD_DIM:Q_OFF + (hd + 1) * HEAD_DIM] = (t * c + _swap_halves(t * sn)).astype(BF16)
        for hd in range(N_KV_HEADS):
            t = dk_ref[:, hd * HEAD_DIM:(hd + 1) * HEAD_DIM]
            o_ref[:, K_OFF + hd * HEAD_DIM:K_OFF + (hd + 1) * HEAD_DIM] = (t * c + _swap_halves(t * sn)).astype(BF16)
        o_ref[:, V_OFF:V_OFF + KV_WIDTH] = dv_ref[...].astype(BF16)

    tab = pl.BlockSpec((tr, HEAD_DIM), lambda i: (i, 0))
    wide = pl.BlockSpec((tr, ATTN_WIDTH), lambda i: (i, 0))
    narrow = pl.BlockSpec((tr, KV_WIDTH), lambda i: (i, 0))
    return pl.pallas_call(
        body, name="rope_bwd", grid=(s // tr,),
        in_specs=[wide, narrow, narrow, tab, tab, ANY],
        out_specs=pl.BlockSpec((tr, qkv_width), lambda i: (i, 0)),
        out_shape=jax.ShapeDtypeStruct(dz.shape, dz.dtype),
        input_output_aliases={5: 0},
        compiler_params=_params(1),
    )(dq_rot, dk_rot, dv, cos_t, sin_t, dz)


def _swa_band(i, s):
    return pl.multiple_of(jnp.clip((i - 1) * BLOCK, 0, s - BAND), BLOCK)


SWA_HEADS_PER_PASS = Q_GROUP


def _swa_probs(q_ref, k_ref, sink_ref, heads, start, valid):
    kv = heads[0] // Q_GROUP
    cols = slice(kv * HEAD_DIM, (kv + 1) * HEAD_DIM)
    kb = k_ref[pl.ds(start, BAND), cols]
    qg = jnp.concatenate([q_ref[:, hd * HEAD_DIM:(hd + 1) * HEAD_DIM] for hd in heads], axis=0)
    sc = lax.dot_general(qg, kb, (((1,), (1,)), ((), ())), preferred_element_type=F32) * ATTN_SCALE
    sc = jnp.where(valid, sc, NEG_INF)
    sk = jnp.concatenate([jnp.full((BLOCK, 1), sink_ref[hd], F32) for hd in heads], axis=0)
    mx = jnp.maximum(jnp.max(sc, axis=1, keepdims=True), sk)
    e = jnp.exp(sc - mx)
    es = jnp.exp(sk - mx)
    inv = 1.0 / (jnp.sum(e, axis=1, keepdims=True) + es)
    return qg, kb, e * inv, es * inv


def _swa_head_passes():
    return [list(range(h0, h0 + SWA_HEADS_PER_PASS)) for h0 in range(0, N_Q_HEADS, SWA_HEADS_PER_PASS)]


def _swa_valid(i, start):
    q_pos = i * BLOCK + lax.broadcasted_iota(jnp.int32, (BLOCK, 1), 0)
    q_pos = jnp.concatenate([q_pos] * SWA_HEADS_PER_PASS, axis=0)
    k_pos = start + lax.broadcasted_iota(jnp.int32, (1, BAND), 1)
    return jnp.abs(k_pos - q_pos) <= WINDOW


def _swa_fwd(q, k, v, sink):
    s = q.shape[0]
    assert s % BLOCK == 0 and s >= BAND

    def body(sink_ref, q_ref, k_ref, v_ref, o_ref):
        i = pl.program_id(0)
        start = _swa_band(i, s)
        valid = _swa_valid(i, start)
        for heads in _swa_head_passes():
            kv = heads[0] // Q_GROUP
            _, _, p, _ = _swa_probs(q_ref, k_ref, sink_ref, heads, start, valid)
            vb = v_ref[pl.ds(start, BAND), kv * HEAD_DIM:(kv + 1) * HEAD_DIM]
            o = jnp.dot(p.astype(BF16), vb, preferred_element_type=F32)
            for g, hd in enumerate(heads):
                o_ref[:, hd * HEAD_DIM:(hd + 1) * HEAD_DIM] = o[g * BLOCK:(g + 1) * BLOCK].astype(BF16)

    whole = pl.BlockSpec((s, KV_WIDTH), lambda i: (0, 0))
    blk = pl.BlockSpec((BLOCK, ATTN_WIDTH), lambda i: (i, 0))
    return pl.pallas_call(
        body, name="swa_fwd", grid=(s // BLOCK,),
        in_specs=[pl.BlockSpec(memory_space=pltpu.SMEM), blk, whole, whole],
        out_specs=blk,
        out_shape=jax.ShapeDtypeStruct((s, ATTN_WIDTH), BF16),
        compiler_params=_params(1),
    )(sink, q, k, v)


def _swa_bwd(q, k, v, d_out, sink):
    s = q.shape[0]

    def body(sink_ref, q_ref, k_ref, v_ref, do_ref, dq_ref, dk_ref, dv_ref, dsink_ref):
        i = pl.program_id(0)

        @pl.when(i == 0)
        def _():
            dk_ref[...] = jnp.zeros_like(dk_ref)
            dv_ref[...] = jnp.zeros_like(dv_ref)
            dsink_ref[...] = jnp.zeros_like(dsink_ref)

        start = _swa_band(i, s)
        valid = _swa_valid(i, start)
        for heads in _swa_head_passes():
            kv = heads[0] // Q_GROUP
            cols = slice(kv * HEAD_DIM, (kv + 1) * HEAD_DIM)
            qg, kb, p, p_sink = _swa_probs(q_ref, k_ref, sink_ref, heads, start, valid)
            vb = v_ref[pl.ds(start, BAND), cols]
            dog = jnp.concatenate([do_ref[:, hd * HEAD_DIM:(hd + 1) * HEAD_DIM] for hd in heads], axis=0)
            dp = lax.dot_general(dog, vb, (((1,), (1,)), ((), ())), preferred_element_type=F32)
            delta = jnp.sum(p * dp, axis=1, keepdims=True)
            ds = (p * (dp - delta) * ATTN_SCALE).astype(BF16)
            dqg = jnp.dot(ds, kb, preferred_element_type=F32)
            dk_ref[pl.ds(start, BAND), cols] += lax.dot_general(ds, qg, (((0,), (0,)), ((), ())), preferred_element_type=F32)
            dv_ref[pl.ds(start, BAND), cols] += lax.dot_general(p.astype(BF16), dog, (((0,), (0,)), ((), ())),
                                                                 preferred_element_type=F32)
            dsk = p_sink * delta
            for g, hd in enumerate(heads):
                dq_ref[:, hd * HEAD_DIM:(hd + 1) * HEAD_DIM] = dqg[g * BLOCK:(g + 1) * BLOCK]
                tot = jnp.sum(dsk[g * BLOCK:(g + 1) * BLOCK], axis=0, keepdims=True)
                dsink_ref[hd:hd + 1, :] -= jnp.broadcast_to(tot, (1, 128))

    whole = pl.BlockSpec((s, KV_WIDTH), lambda i: (0, 0))
    blk = pl.BlockSpec((BLOCK, ATTN_WIDTH), lambda i: (i, 0))
    return pl.pallas_call(
        body, name="swa_bwd", grid=(s // BLOCK,),
        in_specs=[pl.BlockSpec(memory_space=pltpu.SMEM), blk, whole, whole, blk],
        out_specs=[blk, whole, whole, pl.BlockSpec((N_Q_HEADS, 128), lambda i: (0, 0))],
        out_shape=[jax.ShapeDtypeStruct((s, ATTN_WIDTH), F32), jax.ShapeDtypeStruct((s, KV_WIDTH), F32),
                   jax.ShapeDtypeStruct((s, KV_WIDTH), F32), jax.ShapeDtypeStruct((N_Q_HEADS, 128), F32)],
        compiler_params=_params(1),
    )(sink, q, k, v, d_out)


CONV_CHUNK = 256


def _shift_rows(t, rows, down):
    n = t.shape[0]
    rolled = pltpu.roll(t, 1 if down else n - 1, 0)
    edge = 0 if down else n - 1
    return jnp.where(rows == edge, 0.0, rolled)


def _conv_specs(s):
    def z_spec(off):
        return pl.BlockSpec((s, CONV_CHUNK), lambda j, off=off: (0, off // CONV_CHUNK + j))
    chunk = pl.BlockSpec((s, CONV_CHUNK), lambda j: (0, j))
    w_spec = pl.BlockSpec((3, CONV_CHUNK), lambda j: (0, j))
    return z_spec(CU_OFF), z_spec(CB_OFF), z_spec(CC_OFF), chunk, w_spec


def _conv_fwd(z, conv_w):
    s = z.shape[0]
    cu_spec, cb_spec, cc_spec, chunk, w_spec = _conv_specs(s)

    def body(cu_ref, cb_ref, cc_ref, w_ref, o_ref):
        rows = lax.broadcasted_iota(jnp.int32, (s, 1), 0)
        t = cc_ref[...] * cu_ref[...]
        c3 = _shift_rows(t, rows, True) * w_ref[0:1, :] + t * w_ref[1:2, :] + _shift_rows(t, rows, False) * w_ref[2:3, :]
        o_ref[...] = (cb_ref[...] * c3).astype(BF16)

    return pl.pallas_call(
        body, name="conv_fwd", grid=(CONV_WIDTH // CONV_CHUNK,),
        in_specs=[cu_spec, cb_spec, cc_spec, w_spec],
        out_specs=chunk,
        out_shape=jax.ShapeDtypeStruct((s, CONV_WIDTH), BF16),
        compiler_params=_params(1),
    )(z, z, z, conv_w)


def _conv_bwd(z, conv_w, d_co, dz):
    s = z.shape[0]
    cu_spec, cb_spec, cc_spec, chunk, w_spec = _conv_specs(s)
    n_chunks = CONV_WIDTH // CONV_CHUNK
    offsets = (CU_OFF, CB_OFF, CC_OFF)

    def body(cu_ref, cb_ref, cc_ref, w_ref, d_ref, dz_in_ref, dz_ref, dw_ref, buf, sems):
        j = pl.program_id(0)

        def copies(j_at):
            return [pltpu.make_async_copy(buf.at[h], dz_ref.at[:, pl.ds(off + j_at * CONV_CHUNK, CONV_CHUNK)], sems.at[h])
                    for h, off in enumerate(offsets)]

        rows = lax.broadcasted_iota(jnp.int32, (s, 1), 0)
        cu, cc = cu_ref[...], cc_ref[...]
        t = cc * cu
        t_dn, t_up = _shift_rows(t, rows, True), _shift_rows(t, rows, False)
        c3 = t_dn * w_ref[0:1, :] + t * w_ref[1:2, :] + t_up * w_ref[2:3, :]
        d = d_ref[...]
        dc3 = d * cb_ref[...]
        dw_ref[0:1, :] = jnp.sum(dc3 * t_dn, axis=0, keepdims=True)
        dw_ref[1:2, :] = jnp.sum(dc3 * t, axis=0, keepdims=True)
        dw_ref[2:3, :] = jnp.sum(dc3 * t_up, axis=0, keepdims=True)
        dt = _shift_rows(dc3, rows, False) * w_ref[0:1, :] + dc3 * w_ref[1:2, :] + _shift_rows(dc3, rows, True) * w_ref[2:3, :]

        @pl.when(j > 0)
        def _():
            for cp in copies(j):
                cp.wait()

        buf[0] = (dt * cc).astype(BF16)
        buf[1] = (d * c3).astype(BF16)
        buf[2] = (dt * cu).astype(BF16)
        for cp in copies(j):
            cp.start()

        @pl.when(j == n_chunks - 1)
        def _():
            for cp in copies(j):
                cp.wait()

    return pl.pallas_call(
        body, name="conv_bwd", grid=(n_chunks,),
        in_specs=[cu_spec, cb_spec, cc_spec, w_spec, chunk, ANY],
        out_specs=[ANY, w_spec],
        out_shape=[jax.ShapeDtypeStruct(dz.shape, dz.dtype), jax.ShapeDtypeStruct((3, CONV_WIDTH), F32)],
        input_output_aliases={5: 0},
        scratch_shapes=[pltpu.VMEM((3, s, CONV_CHUNK), BF16), pltpu.SemaphoreType.DMA((3,))],
        compiler_params=_params(1),
    )(z, z, z, conv_w, d_co, dz)


GATE_CHUNK = 512


def _gate_specs(s, d, tr):
    n_chunks = d // GATE_CHUNK
    za = pl.BlockSpec((tr, GATE_CHUNK), lambda j, i: (i, GL_OFF // GATE_CHUNK + j))
    zc = pl.BlockSpec((tr, GATE_CHUNK), lambda j, i: (i, GL_OFF // GATE_CHUNK + n_chunks + j))
    ba = pl.BlockSpec((1, GATE_CHUNK), lambda j, i: (0, j))
    bc = pl.BlockSpec((1, GATE_CHUNK), lambda j, i: (0, n_chunks + j))
    tile = pl.BlockSpec((tr, GATE_CHUNK), lambda j, i: (i, j))
    return za, zc, ba, bc, tile


def _gate_fwd(z, b_gate, ya, yc):
    s, d = ya.shape
    tr = _row_tile(s, 512)
    za, zc, ba, bc, tile = _gate_specs(s, d, tr)

    def body(za_ref, zc_ref, ba_ref, bc_ref, ya_ref, yc_ref, o_ref):
        ga = jax.nn.sigmoid(za_ref[...] + ba_ref[...])
        gc = jax.nn.sigmoid(zc_ref[...] + bc_ref[...])
        o_ref[...] = (ga * ya_ref[...] + gc * yc_ref[...]).astype(BF16)

    return pl.pallas_call(
        body, name="gate_fwd", grid=(d // GATE_CHUNK, s // tr),
        in_specs=[za, zc, ba, bc, tile, tile],
        out_specs=tile,
        out_shape=jax.ShapeDtypeStruct((s, d), BF16),
        compiler_params=_params(2),
    )(z, z, b_gate, b_gate, ya, yc)


def _gate_bwd(z, b_gate, ya, yc, dmix):
    s, d = ya.shape
    tr = _row_tile(s, 512)
    za, zc, ba, bc, tile = _gate_specs(s, d, tr)
    vec = pl.BlockSpec((1, GATE_CHUNK), lambda j, i: (0, j))
    n_rows = s // tr
    in_width = z.shape[1]

    def body(za_ref, zc_ref, ba_ref, bc_ref, ya_ref, yc_ref, dm_ref, dya_ref, dyc_ref, dz_ref, dba_ref, dbc_ref, buf, sems):
        j, i = pl.program_id(0), pl.program_id(1)

        def copies(j_at, i_at):
            rows = pl.ds(i_at * tr, tr)
            return [pltpu.make_async_copy(buf.at[h], dz_ref.at[rows, pl.ds(GL_OFF + h * d + j_at * GATE_CHUNK, GATE_CHUNK)],
                                          sems.at[h]) for h in range(2)]

        ga = jax.nn.sigmoid(za_ref[...] + ba_ref[...])
        gc = jax.nn.sigmoid(zc_ref[...] + bc_ref[...])
        dm = dm_ref[...]
        dya_ref[...] = (dm * ga).astype(BF16)
        dyc_ref[...] = (dm * gc).astype(BF16)
        dla = dm * ya_ref[...] * ga * (1.0 - ga)
        dlc = dm * yc_ref[...] * gc * (1.0 - gc)

        @pl.when(j * n_rows + i > 0)
        def _():
            for cp in copies(j, i):
                cp.wait()

        buf[0] = dla.astype(BF16)
        buf[1] = dlc.astype(BF16)
        for cp in copies(j, i):
            cp.start()

        @pl.when((j == d // GATE_CHUNK - 1) & (i == n_rows - 1))
        def _():
            for cp in copies(j, i):
                cp.wait()

        pa = jnp.sum(dla, axis=0, keepdims=True)
        pc = jnp.sum(dlc, axis=0, keepdims=True)

        @pl.when(i == 0)
        def _():
            dba_ref[...] = pa
            dbc_ref[...] = pc

        @pl.when(i > 0)
        def _():
            dba_ref[...] += pa
            dbc_ref[...] += pc

    big = jax.ShapeDtypeStruct((s, d), BF16)
    small = jax.ShapeDtypeStruct((1, d), F32)
    return pl.pallas_call(
        body, name="gate_bwd", grid=(d // GATE_CHUNK, n_rows),
        in_specs=[za, zc, ba, bc, tile, tile, tile],
        out_specs=[tile, tile, ANY, vec, vec],
        out_shape=[big, big, jax.ShapeDtypeStruct((s, in_width), BF16), small, small],
        scratch_shapes=[pltpu.VMEM((2, tr, GATE_CHUNK), BF16), pltpu.SemaphoreType.DMA((2,))],
        compiler_params=_params(2),
    )(z, z, b_gate, b_gate, ya, yc, dmix)


def _cross_probs(q_ref, kv_ref, hd):
    cols = slice(hd * HEAD_DIM, (hd + 1) * HEAD_DIM)
    qh = q_ref[:, cols]
    kh = kv_ref[:, cols]
    sc = lax.dot_general(qh, kh, (((1,), (1,)), ((), ())), preferred_element_type=F32) * ATTN_SCALE
    e = jnp.exp(sc - jnp.max(sc, axis=1, keepdims=True))
    return qh, kh, e * (1.0 / jnp.sum(e, axis=1, keepdims=True))


def _cross_fwd(qc, kvc):
    s = qc.shape[0]
    n_mem = kvc.shape[0]
    tq = _row_tile(s, 256)

    def body(q_ref, kv_ref, o_ref):
        for hd in range(MEM_HEADS):
            _, _, p = _cross_probs(q_ref, kv_ref, hd)
            vh = kv_ref[:, MEM_WIDTH + hd * HEAD_DIM:MEM_WIDTH + (hd + 1) * HEAD_DIM]
            o_ref[:, hd * HEAD_DIM:(hd + 1) * HEAD_DIM] = jnp.dot(p.astype(BF16), vh, preferred_element_type=F32).astype(BF16)

    return pl.pallas_call(
        body, name="cross_fwd", grid=(s // tq,),
        in_specs=[pl.BlockSpec((tq, MEM_WIDTH), lambda i: (i, 0)), pl.BlockSpec((n_mem, 2 * MEM_WIDTH), lambda i: (0, 0))],
        out_specs=pl.BlockSpec((tq, MEM_WIDTH), lambda i: (i, 0)),
        out_shape=jax.ShapeDtypeStruct((s, MEM_WIDTH), BF16),
        compiler_params=_params(1),
    )(qc, kvc)


def _cross_bwd(qc, kvc, d_out):
    s = qc.shape[0]
    n_mem = kvc.shape[0]
    tq = _row_tile(s, 256)

    def body(q_ref, kv_ref, do_ref, dq_ref, dkv_ref):
        @pl.when(pl.program_id(0) == 0)
        def _():
            dkv_ref[...] = jnp.zeros_like(dkv_ref)

        for hd in range(MEM_HEADS):
            cols = slice(hd * HEAD_DIM, (hd + 1) * HEAD_DIM)
            vcols = slice(MEM_WIDTH + hd * HEAD_DIM, MEM_WIDTH + (hd + 1) * HEAD_DIM)
            qh, kh, p = _cross_probs(q_ref, kv_ref, hd)
            doh = do_ref[:, cols]
            dp = lax.dot_general(doh, kv_ref[:, vcols], (((1,), (1,)), ((), ())), preferred_element_type=F32)
            ds = (p * (dp - jnp.sum(p * dp, axis=1, keepdims=True)) * ATTN_SCALE).astype(BF16)
            dq_ref[:, cols] = jnp.dot(ds, kh, preferred_element_type=F32).astype(BF16)
            dkv_ref[:, cols] += lax.dot_general(ds, qh, (((0,), (0,)), ((), ())), preferred_element_type=F32)
            dkv_ref[:, vcols] += lax.dot_general(p.astype(BF16), doh, (((0,), (0,)), ((), ())), preferred_element_type=F32)

    qspec = pl.BlockSpec((tq, MEM_WIDTH), lambda i: (i, 0))
    kvspec = pl.BlockSpec((n_mem, 2 * MEM_WIDTH), lambda i: (0, 0))
    return pl.pallas_call(
        body, name="cross_bwd", grid=(s // tq,),
        in_specs=[qspec, kvspec, qspec],
        out_specs=[qspec, kvspec],
        out_shape=[jax.ShapeDtypeStruct((s, MEM_WIDTH), BF16), jax.ShapeDtypeStruct((n_mem, 2 * MEM_WIDTH), F32)],
        compiler_params=_params(1),
    )(qc, kvc, d_out)


def _swiglu_fwd(up, gate):
    sg = jax.nn.sigmoid(gate)
    silu = gate * sg
    return silu * up, up * (sg * (1.0 + gate * (1.0 - sg))), silu


def _swiglu_bwd(d_act, dact_dgate, dact_dup):
    return d_act * dact_dgate.astype(F32), d_act * dact_dup.astype(F32)


GATHER_GROUPS = {"in": ("w_in", "conv_w"), "mid": ("w_attn_out", "w_conv_out", "w_o", "w_cq", "w_ckv", "w_co"),
                 "gate": ("w_gate",), "up": ("w_up",), "down": ("w_down",)}


def _local_step(xs, mems, target, small, fetch, reduce):
    s, d = xs.shape
    w4 = {}
    cos_t, sin_t = _rope_tables(s)

    def near(group, done, then, after):
        waits = [("direct", group)] + ([("pass_near", done), ("pass_far", done)] if done else [])
        starts = [("forward", group), ("pass_near", group)] + [("direct", g) for g in then]
        tok = fetch.step("gather_near_" + group, waits, starts, after)
        if done:
            w4.update(fetch.arrays(done))
        return tok

    def far(group, then, after):
        return fetch.step("gather_far_" + group, [("forward", group)], [("pass_far", group)] + [("direct", g) for g in then], after)

    def last(group, after):
        tok = fetch.step("gather_done_" + group, [("pass_near", group), ("pass_far", group)], [], after)
        w4.update(fetch.arrays(group))
        return tok

    h = _rmsnorm(xs, small["g_mix"], "norm_mix")
    slots_filled = [a for g in ("gate", "up", "down") for a in fetch.arrays(g).values()]
    chip_x, chip_y = reduce.place[0] // 2, reduce.place[0] % 2
    own_block = jnp.stack([2 * chip_x + chip_y]).astype(jnp.int32)
    near_blocks = jnp.stack([2 * (1 - chip_x) + chip_y, 2 * chip_x + (1 - chip_y)]).astype(jnp.int32)
    far_block = jnp.stack([2 * (1 - chip_x) + (1 - chip_y)]).astype(jnp.int32)
    z = _matmul_column_blocks(h, fetch.arrays("in")["w_in"], own_block, None, tm=512, name="in_proj_own")
    tok = near("in", None, ["mid"], [z] + slots_filled)
    memn = _rmsnorm(mems, small["g_mem"], "norm_mem")
    tok = fetch.step("gather_near_done_in", [("pass_near", "in")], [], [tok, cos_t, sin_t, memn])
    z = _matmul_column_blocks(h, fetch.arrays("in")["w_in"], near_blocks, z, tm=1024, name="in_proj_near", after=tok)
    tok = far("in", [], z)
    tok = fetch.step("gather_done_in", [("pass_far", "in")], [], tok)
    w4.update(fetch.arrays("in"))
    z = _matmul_column_blocks(h, w4["w_in"], far_block, z, tm=1024, name="in_proj_far", after=tok)
    conv4 = w4["conv_w"]
    conv_w = conv4[:, :3, :].transpose(1, 0, 2).reshape(3, N_CHIPS * conv4.shape[2])
    c_in = w4["w_in"].shape[2]
    tok = near("mid", None, ["gate"], z)
    q_rot, k_rot, v_b = _rope_fwd(z, cos_t, sin_t)
    attn = _swa_fwd(q_rot, k_rot, v_b, small["sink"])
    co = _conv_fwd(z, conv_w)
    tok = far("mid", ["up"], attn)
    tok = last("mid", tok)
    w_o = w4["w_o"].reshape(-1, w4["w_o"].shape[-1])
    c_d = w4["w_attn_out"].shape[2]
    ya = _matmul(attn, w4["w_attn_out"], mode="nn", tm=2048, tn=c_d, out_dtypes=[F32], name="attn_out_proj",
                 b_blocks=N_CHIPS, after=tok)
    yc = _matmul(co, w4["w_conv_out"], mode="nn", tm=2048, tn=c_d, out_dtypes=[F32], name="conv_out_proj",
                 b_blocks=N_CHIPS)
    mix = _gate_fwd(z, small["b_gate"], ya, yc)
    x1 = _matmul(mix, w_o, mode="nn", tm=1024, tn=1024, out_dtypes=[F32], name="mix_out_proj", extras=[xs],
                 epilogue=_add_residual)
    tok = near("gate", None, ["down"], x1)
    w_cq = w4["w_cq"].reshape(-1, w4["w_cq"].shape[-1])
    w_ckv = w4["w_ckv"].reshape(-1, w4["w_ckv"].shape[-1])
    hc = _rmsnorm(x1, small["g_cross"], "norm_cross")
    qc = _matmul(hc, w_cq, mode="nn", tm=2048, tn=MEM_WIDTH, out_dtypes=[BF16], name="cross_q_proj", after=tok)
    kvc = _matmul(memn, w_ckv, mode="nn", tm=256, tn=2 * MEM_WIDTH, out_dtypes=[BF16], name="cross_kv_proj")
    oc = _cross_fwd(qc, kvc)
    tok = far("gate", [], oc)
    x2 = _matmul(oc, w4["w_co"], mode="nn", tm=2048, tn=c_d, out_dtypes=[F32], name="cross_out_proj",
                 extras=[x1], epilogue=_add_residual, b_blocks=N_CHIPS, after=tok)
    hf = _rmsnorm(x2, small["g_ffn"], "norm_ffn")
    tok = near("up", "gate", [], hf)
    c_ff = w4["w_gate"].shape[2]
    gate = _matmul(hf, w4["w_gate"], mode="nn", tm=1024, tn=c_ff, out_dtypes=[F32], name="ffn_gate_proj", b_blocks=N_CHIPS,
                   after=tok)
    tok = far("up", [], gate)
    tok = near("down", "up", [], tok)
    act, dact_dgate, dact_dup = _matmul(hf, w4["w_up"], mode="nn", tm=1024, tn=c_ff, out_dtypes=[BF16, BF16, BF16],
                                        name="ffn_up_proj", extras=[gate], epilogue=_swiglu_fwd, b_blocks=N_CHIPS, after=tok)
    tok = far("down", [], act)
    last("down", tok)
    w_down = w4["w_down"].reshape(-1, w4["w_down"].shape[-1])
    x3 = _matmul(act, w_down, mode="nn", tm=512, tn=512, out_dtypes=[F32], name="ffn_down_proj", extras=[x2],
                 epilogue=_add_residual)
    dx3, dx3b, sq, dg_final = _loss_head(x3, small["g_final"], target)

    da, du = _matmul(dx3b, w_down, mode="nt", tm=1024, tn=c_ff, out_dtypes=[BF16, BF16], name="ffn_down_bwd",
                     extras=[dact_dgate, dact_dup], epilogue=_swiglu_bwd)
    core = reduce.core
    ffn_shape = dict(row_sharded=False, tm=1024, tn=c_ff)
    g_down = _matmul(act, dx3b, mode="tn", tm=c_ff, tn=1024, out_dtypes=[BF16], name="ffn_down_wgrad")
    tok = reduce.add("down", {"w_down": g_down}, da)
    t_gate = _wgrad_half(hf, da, core, theirs=True, name="ffn_gate_wgrad_theirs", after=tok, **ffn_shape)
    tok = reduce.step("down", t_gate)
    t_up = _wgrad_half(hf, du, core, theirs=True, name="ffn_up_wgrad_theirs", after=tok, **ffn_shape)
    tok = reduce.send("ffn", {"w_gate": t_gate, "w_up": t_up}, dx3b)
    dhf = _matmul(da, w4["w_gate"], mode="nt", tm=512, tn=1024, out_dtypes=[F32], name="ffn_gate_bwd", b_blocks=N_CHIPS,
                  after=tok)
    got = reduce.received("ffn", dhf)
    p_gate = _wgrad_half(hf, da, core, theirs=False, name="ffn_gate_wgrad_mine", add=got["w_gate"], **ffn_shape)
    p_up = _wgrad_half(hf, du, core, theirs=False, name="ffn_up_wgrad_mine", add=got["w_up"], **ffn_shape)
    tok = reduce.add_parts("ffn", {"w_gate": p_gate, "w_up": p_up})
    dhf = _matmul(du, w4["w_up"], mode="nt", tm=512, tn=1024, out_dtypes=[F32], name="ffn_up_bwd", extras=[dhf],
                  epilogue=_add_residual, b_blocks=N_CHIPS, after=tok)
    tok = reduce.step("down", dhf)
    dx2, dx2b, dg_ffn = _rmsnorm_bwd(dhf, x2, small["g_ffn"], dx3, "norm_ffn_bwd")

    d_oc = _matmul(dx2b, w4["w_co"], mode="nt", tm=1024, tn=MEM_WIDTH, out_dtypes=[BF16], name="cross_out_bwd",
                   b_blocks=N_CHIPS, after=tok)
    g_co = _matmul(oc, dx2b, mode="tn", tm=MEM_WIDTH, tn=c_d, out_dtypes=[BF16], name="cross_out_wgrad", out_blocks=N_CHIPS)
    tok = reduce.step("down", g_co)
    dqc, dkvc = _cross_bwd(qc, kvc, d_oc)
    g_cq = _matmul(hc, dqc, mode="tn", tm=1024, tn=MEM_WIDTH, out_dtypes=[BF16], name="cross_q_wgrad", after=tok)
    dhc = _matmul(dqc, w_cq, mode="nt", tm=1024, tn=1024, out_dtypes=[F32], name="cross_q_bwd")
    g_ckv = _matmul(memn, dkvc, mode="tn", tm=1024, tn=2 * MEM_WIDTH, out_dtypes=[BF16], name="cross_kv_wgrad")
    dx1, dx1b, dg_cross = _rmsnorm_bwd(dhc, x1, small["g_cross"], dx2, "norm_cross_bwd")

    dmix = _matmul(dx1b, w_o, mode="nt", tm=1024, tn=1024, out_dtypes=[F32], name="mix_out_bwd")
    g_o = _matmul(mix, dx1b, mode="tn", tm=1024, tn=1024, out_dtypes=[BF16], name="mix_out_wgrad")
    dya, dyc, dz, db_a, db_c = _gate_bwd(z, small["b_gate"], ya, yc, dmix)
    d_attn = _matmul(dya, w4["w_attn_out"], mode="nt", tm=1024, tn=ATTN_WIDTH, out_dtypes=[BF16], name="attn_out_bwd",
                     b_blocks=N_CHIPS)
    g_ao = _matmul(attn, dya, mode="tn", tm=ATTN_WIDTH, tn=c_d, out_dtypes=[BF16], name="attn_out_wgrad", out_blocks=N_CHIPS)
    d_co = _matmul(dyc, w4["w_conv_out"], mode="nt", tm=1024, tn=CONV_WIDTH, out_dtypes=[F32], name="conv_out_bwd",
                   b_blocks=N_CHIPS)
    g_cvo = _matmul(co, dyc, mode="tn", tm=CONV_WIDTH, tn=c_d, out_dtypes=[BF16], name="conv_out_wgrad", out_blocks=N_CHIPS)
    tok = reduce.step("ffn", g_cvo)
    tok = reduce.add("mid", {"w_co": g_co, "w_cq": g_cq, "w_ckv": g_ckv, "w_o": g_o, "w_attn_out": g_ao, "w_conv_out": g_cvo}, tok)
    dz, d_conv_w = _conv_bwd(z, conv_w, d_co, dz)
    dq_rot, dk_rot, dv, dsink = _swa_bwd(q_rot, k_rot, v_b, d_attn, small["sink"])
    tok = reduce.step("mid", dq_rot)
    dz = _rope_bwd(dq_rot, dk_rot, dv, cos_t, sin_t, dz)
    in_shape = dict(row_sharded=False, tm=1024, tn=c_in)
    t_in = _wgrad_half(h, dz, core, theirs=True, name="in_proj_wgrad_theirs", after=tok, **in_shape)
    tok = reduce.send("in", {"w_in": t_in}, dk_rot)
    tok = reduce.step("ffn", tok, count=1)
    dmemn = _matmul(dkvc, w_ckv, mode="nt", tm=256, tn=1024, out_dtypes=[F32], name="cross_kv_bwd", after=tok)
    _, _, dg_mem = _rmsnorm_bwd(dmemn, mems, small["g_mem"], None, "norm_mem_bwd")
    got = reduce.received("in", dg_mem)
    p_in = _wgrad_half(h, dz, core, theirs=False, name="in_proj_wgrad_mine", add=got["w_in"], **in_shape)
    tok = reduce.add_parts("in", {"w_in": p_in})
    tok = reduce.step("ffn", tok)
    tok = reduce.step("mid", tok)
    dh = _matmul(dz, w4["w_in"], mode="nt", tm=512, tn=512, out_dtypes=[F32], name="in_proj_bwd", b_blocks=N_CHIPS,
                 after=tok)
    grad_x, _, dg_mix = _rmsnorm_bwd(dh, xs, small["g_mix"], dx1, "norm_mix_bwd")

    small_grads = {
        "g_mix": dg_mix, "sink": dsink[:, 0], "b_gate": jnp.concatenate([db_a, db_c], axis=1), "g_cross": dg_cross,
        "g_mem": dg_mem, "g_ffn": dg_ffn, "g_final": dg_final, "conv_w": d_conv_w,
    }
    return sq, grad_x, small_grads


def _pair_sum(g4, ra, core, name):
    nb, rs, cs = g4.shape
    rh = rs // 2
    tr = _row_tile(rh, 256)
    per = rh // tr

    def body(c_ref, g_ref, r_ref, o_ref):
        o_ref[...] = (g_ref[...].astype(F32) + r_ref[...].astype(F32)).astype(BF16)

    plain = pl.BlockSpec((None, tr, cs), lambda j, i, c: (j, i, 0))
    return pl.pallas_call(
        body, name=name,
        grid_spec=pltpu.PrefetchScalarGridSpec(
            num_scalar_prefetch=1, grid=(nb, per),
            in_specs=[pl.BlockSpec((None, tr, cs), lambda j, i, c: (j, c[0] * per + i, 0)), plain],
            out_specs=plain),
        out_shape=jax.ShapeDtypeStruct((nb, rh, cs), BF16),
        compiler_params=_params(2),
    )(core, g4, ra)


def _adamw_update(w, g, m, v):
    nm = ADAM_B1 * m + (1.0 - ADAM_B1) * g
    nv = ADAM_B2 * v + (1.0 - ADAM_B2) * (g * g)
    m_hat = nm / ADAM_C1
    v_hat = nv / ADAM_C2
    return -ADAM_LR * (m_hat / (jnp.sqrt(v_hat) + ADAM_EPS) + ADAM_WD * w), nm, nv


ADAMW_STEPS = 4
ADAMW_BYTES_PER_ELEMENT = 40


def _adamw_calls(names, shards):
    step_bytes = sum(shards[n].size // (2 * ADAMW_STEPS) * ADAMW_BYTES_PER_ELEMENT for n in names)
    return [list(names)] if 2 * step_bytes <= VMEM_LIMIT_BYTES * 3 // 4 else [[n] for n in names]


def _adamw_row_tiles(ws):
    for w in ws:
        assert w.shape[0] % (2 * ADAMW_STEPS * BF16_SUBLANES) == 0, w.shape
    return [w.shape[0] // (2 * ADAMW_STEPS) for w in ws]


def _adamw_own_half(ws, ms, vs, parts, rcs, place, name, after=None):
    n = len(ws)

    def body(p_ref, *refs):
        ins, outs = refs[:5 * n], refs[len(refs) - 5 * n:]
        for k in range(n):
            w_ref, m_ref, v_ref, own_ref, r_ref = ins[5 * k:5 * k + 5]
            gx_ref, g_ref, d_ref, nm_ref, nv_ref = outs[5 * k:5 * k + 5]
            g = own_ref[...].astype(F32)
            for j in range(r_ref.shape[0]):
                g = g + r_ref[j].astype(F32)
            gx_ref[...] = g
            g_ref[...] = g
            d_ref[...], nm_ref[...], nv_ref[...] = _adamw_update(w_ref[...], g, m_ref[...], v_ref[...])

    in_specs, out_specs, out_shape, operands = [], [], [], []
    for w, m, v, p, r, tr in zip(ws, ms, vs, parts, rcs, _adamw_row_tiles(ws), strict=True):
        cols = w.shape[1]
        mine = pl.BlockSpec((tr, cols), lambda i, pos: (pos[1] * ADAMW_STEPS + i, 0))
        in_specs += [mine, mine, mine, pl.BlockSpec((None, tr, cols), lambda i, pos: (pos[0], i, 0)),
                     pl.BlockSpec((r.shape[0], tr, cols), lambda i, pos: (0, i, 0))]
        out_specs += [mine] * 5
        out_shape += [jax.ShapeDtypeStruct(w.shape, F32)] * 5
        operands += [w, m, v, p, r]
    outs = pl.pallas_call(
        body, name=name,
        grid_spec=pltpu.PrefetchScalarGridSpec(
            num_scalar_prefetch=1, grid=(ADAMW_STEPS,),
            in_specs=in_specs + ([] if after is None else [ANY]), out_specs=out_specs),
        out_shape=out_shape,
        compiler_params=_params(1),
    )(place, *operands, *([] if after is None else [after]))
    return [tuple(outs[5 * k:5 * k + 5]) for k in range(n)]


def _adamw_other_half(ws, ms, vs, exchanged, halves, place, name, after=None):
    n = len(ws)

    def body(p_ref, *refs):
        ins, outs = refs[:8 * n], refs[len(refs) - 4 * n:]
        for k in range(n):
            w_ref, m_ref, v_ref, gx_ref = ins[8 * k:8 * k + 4]
            g_ref, d_ref, nm_ref, nv_ref = outs[4 * k:4 * k + 4]
            gv = gx_ref[...]
            g_ref[...] = gv
            d_ref[...], nm_ref[...], nv_ref[...] = _adamw_update(w_ref[...], gv, m_ref[...], v_ref[...])

    in_specs, out_specs, out_shape, operands, aliases = [], [], [], [], {}
    for k, (w, m, v, gx, half, tr) in enumerate(zip(ws, ms, vs, exchanged, halves, _adamw_row_tiles(ws), strict=True)):
        other = pl.BlockSpec((tr, w.shape[1]), lambda i, pos: ((1 - pos[1]) * ADAMW_STEPS + i, 0))
        in_specs += [other] * 4 + [ANY] * 4
        out_specs += [other] * 4
        out_shape += [jax.ShapeDtypeStruct(w.shape, F32)] * 4
        operands += [w, m, v, gx, *half]
        aliases.update({1 + 8 * k + 4 + j: 4 * k + j for j in range(4)})
    outs = pl.pallas_call(
        body, name=name,
        grid_spec=pltpu.PrefetchScalarGridSpec(
            num_scalar_prefetch=1, grid=(ADAMW_STEPS,),
            in_specs=in_specs + ([] if after is None else [ANY]), out_specs=out_specs),
        out_shape=out_shape,
        input_output_aliases=aliases,
        compiler_params=_params(1),
    )(place, *operands, *([] if after is None else [after]))
    return [tuple(outs[4 * k:4 * k + 4]) for k in range(n)]


def _cast_to_slot(w, place, dtype, name, after=None):
    rows, cols = w.shape
    tr = _row_tile(rows, 1024)

    def body(p_ref, w_ref, *rest):
        o_ref = rest[-1]
        o_ref[...] = w_ref[...].astype(dtype)

    return pl.pallas_call(
        body, name=name,
        grid_spec=pltpu.PrefetchScalarGridSpec(
            num_scalar_prefetch=1, grid=(rows // tr,),
            in_specs=[pl.BlockSpec((tr, cols), lambda i, p: (i, 0))] + ([] if after is None else [ANY]),
            out_specs=pl.BlockSpec((None, tr, cols), lambda i, p: (p[0], i, 0))),
        out_shape=jax.ShapeDtypeStruct((N_CHIPS, rows, cols), dtype),
        compiler_params=_params(1),
    )(place, w, *([] if after is None else [after]))


def _adamw(w, g, m, v, name, after=None):
    rows, cols = w.shape
    tr = _row_tile(rows, 256)

    def body(w_ref, g_ref, m_ref, v_ref, *rest):
        go_ref, d_ref, nm_ref, nv_ref = rest[-4:]
        gv = g_ref[...]
        go_ref[...] = gv
        d_ref[...], nm_ref[...], nv_ref[...] = _adamw_update(w_ref[...], gv, m_ref[...], v_ref[...])

    tile = pl.BlockSpec((tr, cols), lambda i: (i, 0))
    shape = jax.ShapeDtypeStruct((rows, cols), F32)
    return pl.pallas_call(
        body, name=name, grid=(rows // tr,),
        in_specs=[tile] * 4 + ([] if after is None else [ANY]), out_specs=[tile] * 4, out_shape=[shape] * 4,
        compiler_params=_params(1),
    )(w, g, m, v, *([] if after is None else [after]))


def _mesh_pos():
    return lax.axis_index("x"), lax.axis_index("y"), lax.axis_index("c")


def _other_chips(x, y):
    return [(1 - x, y), (x, 1 - y), (1 - x, 1 - y)]


def _half_rows(ref, which):
    rh = ref.shape[-2] // 2
    return ref.at[pl.ds(which * rh, rh), :]


def _remote(src, dst, send_sems, recv_sems, sem, to):
    return pltpu.make_async_remote_copy(src_ref=src, dst_ref=dst, send_sem=send_sems.at[sem], recv_sem=recv_sems.at[sem],
                                        device_id=to, device_id_type=MESH)


HBM = pl.BlockSpec(memory_space=pltpu.HBM)
SEM = pl.BlockSpec(memory_space=pltpu.SEMAPHORE)
DATAFLOW_EFFECT = pltpu.SideEffectType.DATAFLOW_SIDE_EFFECTING


def _in_hbm(arrays):
    return [pltpu.with_memory_space_constraint(a, pltpu.HBM) for a in arrays]


def _hbm_like(arrays):
    return [pltpu.HBM(a.shape, a.dtype) for a in arrays]


GATHER_COPIES_PER_ARRAY = {"direct": 2, "forward": 2, "pass_near": 2, "pass_far": 1}


def _gather_copies(kind, refs, x, y, c):
    me, near_x, near_y, far = 2 * x + y, 2 * (1 - x) + y, 2 * x + (1 - y), 2 * (1 - x) + (1 - y)
    to_x, to_y, sibling = (1 - x, y, c), (x, 1 - y, c), (x, y, 1 - c)
    out = []
    for ref in refs:
        rh = ref.shape[1] // 2
        rq = rh // 2

        def half(chip, ref=ref, rh=rh):
            return ref.at[chip, pl.ds(c * rh, rh), :]

        def quarter(chip, q, ref=ref, rh=rh, rq=rq):
            return ref.at[chip, pl.ds(c * rh + q * rq, rq), :]

        if kind == "direct":
            out += [(half(me), half(me), to_x), (half(me), half(me), to_y)]
        elif kind == "forward":
            out += [(quarter(near_x, 0), quarter(near_x, 0), to_y), (quarter(near_y, 1), quarter(near_y, 1), to_x)]
        elif kind == "pass_near":
            out += [(half(near_x), half(near_x), sibling), (half(near_y), half(near_y), sibling)]
        else:
            assert kind == "pass_far"
            out += [(half(far), half(far), sibling)]
    return out


def _gather_step(name, bufs, waits, starts, after):
    nb, nw, ns = len(bufs), len(waits), len(starts)
    after = [] if after is None else list(after) if isinstance(after, (list, tuple)) else [after]
    n_after = len(after)

    def body(*refs):
        ins = refs[:nb]
        wait_sems = refs[nb:nb + 2 * nw]
        start_sems = refs[nb + 2 * nw + n_after:nb + 2 * nw + n_after + 2 * ns]
        token = refs[-1]
        x, y, c = _mesh_pos()
        for j, (kind, idxs, _, _) in enumerate(waits):
            for i, (s_ref, d_ref, to) in enumerate(_gather_copies(kind, [ins[t] for t in idxs], x, y, c)):
                came = _remote(s_ref, d_ref, wait_sems[2 * j], wait_sems[2 * j + 1], i, to)
                came.wait_recv()
                came.wait_send()
        for j, (kind, idxs) in enumerate(starts):
            for i, (s_ref, d_ref, to) in enumerate(_gather_copies(kind, [ins[t] for t in idxs], x, y, c)):
                _remote(s_ref, d_ref, start_sems[2 * j], start_sems[2 * j + 1], i, to).start()
        token[...] = jnp.zeros_like(token)

    sems = []
    for kind, idxs in starts:
        sems += [pltpu.SemaphoreType.DMA((GATHER_COPIES_PER_ARRAY[kind] * len(idxs),))] * 2
    operands = _in_hbm(bufs) + [sem for w in waits for sem in w[2:]] + after
    outs = pl.pallas_call(
        body, name=name,
        in_specs=[HBM] * nb + [SEM] * (2 * nw) + [ANY] * n_after,
        out_specs=[SEM] * (2 * ns) + [HBM] * nb + [pl.BlockSpec(memory_space=pltpu.VMEM)],
        out_shape=sems + _hbm_like(bufs) + [jax.ShapeDtypeStruct((8, 128), F32)],
        input_output_aliases={i: 2 * ns + i for i in range(nb)},
        compiler_params=pltpu.CompilerParams(has_side_effects=DATAFLOW_EFFECT),
    )(*operands)
    return outs[2 * ns:2 * ns + nb], [(outs[2 * j], outs[2 * j + 1]) for j in range(ns)], outs[-1]


class _Gather:
    def __init__(self, groups):
        self.groups = groups
        self.bufs = {}
        self.in_flight = {}

    def put(self, slotted):
        self.bufs.update(slotted)

    def step(self, name, waits, starts, after=None):
        names = []
        for _, group in list(waits) + list(starts):
            names += [n for n in self.groups[group] if n not in names]
        index = {n: i for i, n in enumerate(names)}

        def members(group):
            return [index[n] for n in self.groups[group]]

        wait_args = [(kind, members(group)) + self.in_flight.pop((kind, group)) for kind, group in waits]
        start_args = [(kind, members(group)) for kind, group in starts]
        bufs, sems, token = _gather_step(name, [self.bufs[n] for n in names], wait_args, start_args, after)
        self.bufs.update(zip(names, bufs))
        for (kind, group), pair in zip(starts, sems):
            self.in_flight[(kind, group)] = pair
        return token

    def arrays(self, group):
        return {n: self.bufs[n] for n in self.groups[group]}


def _sibling_halves_copies(srcs, dsts, x, y, c):
    out = []
    for s_ref, d_ref in zip(srcs, dsts, strict=True):
        rh = s_ref.shape[1] // 2
        out.append((s_ref.at[:, pl.ds((1 - c) * rh, rh), :], d_ref, (x, y, 1 - c)))
    return out


def _to_sibling_copies(srcs, dsts, x, y, c):
    return [(s_ref, d_ref, (x, y, 1 - c)) for s_ref, d_ref in zip(srcs, dsts, strict=True)]


def _chip_copies(srcs, dsts, x, y, c):
    out = []
    for s_ref, d_ref in zip(srcs, dsts, strict=True):
        for k, (px, py) in enumerate(_other_chips(x, y)):
            out.append((s_ref.at[2 * px + py], d_ref.at[k], (px, py, c)))
    return out


def _join_copies(srcs, dsts, x, y, c):
    out = []
    for s_ref in srcs:
        mine = _half_rows(s_ref, c)
        out.append((mine, mine, (x, y, 1 - c)))
    return out


def _exchange_start(copies_fn, n_copies, srcs, fresh, after, name):
    ns, nb = len(srcs), len(srcs) + len(fresh)

    def body(*refs):
        bufs, send, recv, token = refs[:nb], refs[nb + 1], refs[nb + 2], refs[-1]
        x, y, c = _mesh_pos()
        for i, (s_ref, d_ref, to) in enumerate(copies_fn(bufs[:ns], bufs[ns:] if fresh else bufs[:ns], x, y, c)):
            _remote(s_ref, d_ref, send, recv, i, to).start()
        token[...] = jnp.zeros_like(token)

    sems = [pltpu.SemaphoreType.DMA((n_copies,))] * 2
    outs = pl.pallas_call(
        body, name=name,
        in_specs=[HBM] * nb + [ANY], out_specs=[SEM, SEM] + [HBM] * nb + [pl.BlockSpec(memory_space=pltpu.VMEM)],
        out_shape=sems + _hbm_like(list(srcs) + list(fresh)) + [jax.ShapeDtypeStruct((8, 128), F32)],
        input_output_aliases={i: 2 + i for i in range(nb)},
        compiler_params=pltpu.CompilerParams(has_side_effects=DATAFLOW_EFFECT),
    )(*_in_hbm(list(srcs) + list(fresh)), after)
    return outs[0], outs[1], outs[2:2 + ns], outs[2 + ns:2 + nb], outs[-1]


def _exchange_done(copies_fn, srcs, fresh, send, recv, after, name):
    ns, nb = len(srcs), len(srcs) + len(fresh)

    def body(*refs):
        bufs, send_in, recv_in = refs[:nb], refs[nb], refs[nb + 1]
        x, y, c = _mesh_pos()
        for i, (s_ref, d_ref, to) in enumerate(copies_fn(bufs[:ns], bufs[ns:] if fresh else bufs[:ns], x, y, c)):
            came = _remote(s_ref, d_ref, send_in, recv_in, i, to)
            came.wait_send()
            came.wait_recv()

    outs = pl.pallas_call(
        body, name=name,
        in_specs=[HBM] * nb + [SEM, SEM, ANY], out_specs=[HBM] * nb,
        out_shape=_hbm_like(list(srcs) + list(fresh)),
        input_output_aliases={i: i for i in range(nb)},
        compiler_params=pltpu.CompilerParams(has_side_effects=DATAFLOW_EFFECT),
    )(*_in_hbm(list(srcs) + list(fresh)), send, recv, after)
    return outs[:ns], outs[ns:]


class _Reduce:
    def __init__(self, place, core, shards, mom_m, mom_v):
        self.place, self.core = place, core
        self.shards, self.mom_m, self.mom_v = shards, mom_m, mom_v
        self.state = {}
        self.results = {}

    def add(self, group, grads, after):
        names = list(grads)
        g4s = [g.reshape((N_CHIPS, -1, g.shape[-1])) if g.ndim == 2 else g for g in grads.values()]
        fresh = [lax.empty((N_CHIPS, g.shape[1] // 2, g.shape[2]), BF16) for g in g4s]
        send, recv, g4s, fresh, token = _exchange_start(_sibling_halves_copies, len(names), g4s, fresh, after,
                                                        "pair_start_" + group)
        self.state[group] = (0, names, send, recv, g4s, fresh)
        return token

    def send(self, group, theirs, after):
        names, srcs = list(theirs), list(theirs.values())
        fresh = [lax.empty(s.shape, BF16) for s in srcs]
        send, recv, srcs, fresh, token = _exchange_start(_to_sibling_copies, len(names), srcs, fresh, after, "pair_start_" + group)
        self.state[group] = ("sent", names, send, recv, srcs, fresh)
        return token

    def received(self, group, after):
        stage, names, send, recv, srcs, fresh = self.state.pop(group)
        assert stage == "sent"
        _, got = _exchange_done(_to_sibling_copies, srcs, fresh, send, recv, after, "pair_done_" + group)
        return dict(zip(names, got))

    def add_parts(self, group, parts):
        names, srcs = list(parts), list(parts.values())
        fresh = [lax.empty((N_CHIPS - 1,) + p.shape[1:], BF16) for p in srcs]
        send, recv, srcs, fresh, token = _exchange_start(_chip_copies, 3 * len(names), srcs, fresh, self.core, "chips_start_" + group)
        self.state[group] = (1, names, send, recv, srcs, fresh)
        return token

    def step(self, group, after, count=None):
        stage, names, send, recv, srcs, fresh = self.state[group]
        if stage == 0:
            g4s, ras = _exchange_done(_sibling_halves_copies, srcs, fresh, send, recv, after, "pair_done_" + group)
            parts = [_pair_sum(g, r, self.core, "pair_sum_" + n) for g, r, n in zip(g4s, ras, names)]
            fresh = [lax.empty((N_CHIPS - 1,) + p.shape[1:], BF16) for p in parts]
            send, recv, parts, fresh, token = _exchange_start(_chip_copies, 3 * len(names), parts, fresh, self.core,
                                                              "chips_start_" + group)
            self.state[group] = (1, names, send, recv, parts, fresh)
            return token
        if stage == 1:
            parts, rcs = _exchange_done(_chip_copies, srcs, fresh, send, recv, after, "chips_done_" + group)
            token = None
            for call in _adamw_calls(names, self.shards):
                at = [names.index(n) for n in call]
                done = _adamw_own_half(*[[held[n] for n in call] for held in (self.shards, self.mom_m, self.mom_v)],
                                       [parts[i] for i in at], [rcs[i] for i in at], self.place,
                                       "adamw_own_" + (call[0] if len(call) == 1 else group), after=token)
                self.results.update(zip(call, done))
                token = done[-1][2]
            wholes = [self.results[n][0] for n in names]
            send, recv, wholes, _, token = _exchange_start(_join_copies, len(names), wholes, [], token, "join_start_" + group)
            self.state[group] = (2, names, send, recv, wholes, [])
            return token
        assert stage in (2, 3)
        if stage == 2:
            srcs, _ = _exchange_done(_join_copies, srcs, [], send, recv, after, "join_done_" + group)
            after = None
        token = after
        count = len(names) if count is None else count
        for call in _adamw_calls(names[:count], self.shards):
            at = [names.index(n) for n in call]
            done = _adamw_other_half(*[[held[n] for n in call] for held in (self.shards, self.mom_m, self.mom_v)],
                                     [srcs[i] for i in at], [self.results[n][1:] for n in call], self.place,
                                     "adamw_other_" + (call[0] if len(call) == 1 else group), after=token)
            self.results.update(zip(call, done))
            token = done[-1][1]
        if count < len(names):
            self.state[group] = (3, names[count:], None, None, srcs[count:], [])
        else:
            del self.state[group]
        return token


N_DEV = 8


def _to_all_copies(srcs, dsts, x, y, c):
    out = []
    for r in range(1, N_DEV):
        fx, fy, fc = (r >> 2) & 1, (r >> 1) & 1, r & 1
        out.append((srcs[0], dsts[0].at[r - 1], (x + fx - 2 * x * fx, y + fy - 2 * y * fy, c + fc - 2 * c * fc)))
    return out


def _all_reduce_small_start(v, after):
    slots = lax.empty((N_DEV - 1,) + v.shape, v.dtype)
    send, recv, (v,), (slots,), token = _exchange_start(_to_all_copies, N_DEV - 1, [v], [slots], after, "small_grads_start")
    return (send, recv, v, slots), token


def _all_reduce_small_done(started, after):
    send, recv, v, slots = started
    (v,), (slots,) = _exchange_done(_to_all_copies, [v], [slots], send, recv, after, "small_grads_done")

    def body(v_ref, slots_ref, o_ref):
        x, y, c = _mesh_pos()
        me = 4 * x + 2 * y + c
        acc = None
        for i in range(N_DEV):
            r = jnp.bitwise_xor(me, i)
            part = jnp.where(r == 0, v_ref[...], slots_ref[jnp.maximum(r - 1, 0)])
            acc = part if acc is None else acc + part
        o_ref[...] = acc

    vm = pl.BlockSpec(memory_space=pltpu.VMEM)
    return pl.pallas_call(body, name="small_grads_sum", in_specs=[vm, vm], out_specs=vm,
                          out_shape=jax.ShapeDtypeStruct(v.shape, v.dtype))(v, slots)


MATRICES = ("w_in", "w_attn_out", "w_conv_out", "w_o", "w_cq", "w_ckv", "w_co", "w_gate", "w_up", "w_down")
VECTORS = ("g_mix", "b_gate", "g_cross", "g_mem", "g_ffn", "g_final", "conv_w", "sink")
WEIGHT_ORDER = ("g_mix", "w_in", "sink", "conv_w", "b_gate", "w_attn_out", "w_conv_out", "w_o", "g_cross", "g_mem", "w_cq",
                "w_ckv", "w_co", "g_ffn", "w_gate", "w_up", "w_down", "g_final")
CONV_PAD_ROWS = 32
SMALL_ROWS = 8


def _pack(pieces):
    flat = jnp.concatenate([p.reshape(-1) for p in pieces])
    lane_group = SMALL_ROWS * 128
    total = -(-flat.shape[0] // lane_group) * lane_group
    flat = jnp.pad(flat, (0, total - flat.shape[0]))
    return flat.reshape(SMALL_ROWS, total // SMALL_ROWS), [p.size for p in pieces]


def _unpack(packed, pieces):
    flat = packed.reshape(-1)
    out, off = [], 0
    for p in pieces:
        out.append(flat[off:off + p.size].reshape(p.shape))
        off += p.size
    return out


def kernel(x, mem, g_mix, w_in, sink, conv_w, b_gate, w_attn_out, w_conv_out, w_o, g_cross, g_mem, w_cq, w_ckv, w_co, g_ffn, w_gate, w_up, w_down, g_final, loss_target, m_g_mix, m_w_in, m_sink, m_conv_w, m_b_gate, m_w_attn_out, m_w_conv_out, m_w_o, m_g_cross, m_g_mem, m_w_cq, m_w_ckv, m_w_co, m_g_ffn, m_w_gate, m_w_up, m_w_down, m_g_final, v_g_mix, v_w_in, v_sink, v_conv_w, v_b_gate, v_w_attn_out, v_w_conv_out, v_w_o, v_g_cross, v_g_mem, v_w_cq, v_w_ckv, v_w_co, v_g_ffn, v_w_gate, v_w_up, v_w_down, v_g_final):
    given = dict(g_mix=g_mix, w_in=w_in, sink=sink, conv_w=conv_w, b_gate=b_gate, w_attn_out=w_attn_out, w_conv_out=w_conv_out,
                 w_o=w_o, g_cross=g_cross, g_mem=g_mem, w_cq=w_cq, w_ckv=w_ckv, w_co=w_co, g_ffn=g_ffn, w_gate=w_gate, w_up=w_up,
                 w_down=w_down, g_final=g_final)
    mom_m = dict(g_mix=m_g_mix, w_in=m_w_in, sink=m_sink, conv_w=m_conv_w, b_gate=m_b_gate, w_attn_out=m_w_attn_out,
                 w_conv_out=m_w_conv_out, w_o=m_w_o, g_cross=m_g_cross, g_mem=m_g_mem, w_cq=m_w_cq, w_ckv=m_w_ckv, w_co=m_w_co,
                 g_ffn=m_g_ffn, w_gate=m_w_gate, w_up=m_w_up, w_down=m_w_down, g_final=m_g_final)
    mom_v = dict(g_mix=v_g_mix, w_in=v_w_in, sink=v_sink, conv_w=v_conv_w, b_gate=v_b_gate, w_attn_out=v_w_attn_out,
                 w_conv_out=v_w_conv_out, w_o=v_w_o, g_cross=v_g_cross, g_mem=v_g_mem, w_cq=v_w_cq, w_ckv=v_w_ckv, w_co=v_w_co,
                 g_ffn=v_g_ffn, w_gate=v_w_gate, w_up=v_w_up, w_down=v_w_down, g_final=v_g_final)
    xs, mems, target = x[0], mem[0], loss_target[0]
    d_model = xs.shape[1]
    chip = 2 * lax.axis_index("x") + lax.axis_index("y")
    core = jnp.reshape(lax.axis_index("c"), (1,)).astype(jnp.int32)
    place = jnp.stack([chip, lax.axis_index("c")]).astype(jnp.int32)

    shards = {n: given[n][0] for n in MATRICES}
    conv_cols = conv_w.shape[2]
    conv_pad = jnp.pad(conv_w[0], ((0, CONV_PAD_ROWS - conv_w.shape[1]), (0, 0)))
    fetch = _Gather(GATHER_GROUPS)
    first = {"w_in": _cast_to_slot(shards["w_in"], place, BF16, "to_slot_w_in"),
             "conv_w": _cast_to_slot(conv_pad, place, F32, "to_slot_conv_w")}
    fetch.put(first)
    tok = fetch.step("gather_start", [], [("direct", "in")])
    fetch.put({n: _cast_to_slot(shards[n], place, BF16, "to_slot_" + n, after=tok) for n in MATRICES if n != "w_in"})
    small = {n: given[n] for n in ("g_mix", "b_gate", "g_cross", "g_mem", "g_ffn")}
    small["g_final"] = g_final[None]
    small["sink"] = sink[0]

    reduce = _Reduce(place, core, shards, {n: mom_m[n][0] for n in MATRICES}, {n: mom_v[n][0] for n in MATRICES})
    sq, grad_x, small_grads = _local_step(xs, mems, target, small, fetch, reduce)

    loss_part = 0.5 * sq[0:1, 0:1] / d_model
    pieces = [small_grads[n] for n in VECTORS] + [loss_part]
    packed, _ = _pack(pieces)
    started, tok = _all_reduce_small_start(packed, core)
    tok = reduce.step("mid", tok)
    tok = reduce.step("in", tok)
    summed = _unpack(_all_reduce_small_done(started, tok), pieces)
    loss = summed[-1][0, 0]
    small_sum = dict(zip(VECTORS, summed[:-1]))
    small_sum["conv_w"] = lax.dynamic_slice_in_dim(small_sum["conv_w"], chip * conv_cols, conv_cols, axis=1)

    grad_out, delta, new_m, new_v = {}, {}, {}, {}
    like = [given[n] for n in VECTORS]
    pw, _ = _pack(like)
    pg, _ = _pack([small_sum[n] for n in VECTORS])
    pm, _ = _pack([mom_m[n] for n in VECTORS])
    pv, _ = _pack([mom_v[n] for n in VECTORS])
    _, pd, pnm, pnv = _adamw(pw, pg, pm, pv, "adamw_small")
    for n, g, d, nm, nv in zip(VECTORS, [small_sum[n] for n in VECTORS], _unpack(pd, like), _unpack(pnm, like), _unpack(pnv, like)):
        grad_out[n] = g.reshape(given[n].shape)
        delta[n], new_m[n], new_v[n] = d, nm, nv
    reduce.step("in", pd)
    for n in MATRICES:
        g, d, nm, nv = reduce.results[n]
        grad_out[n], delta[n], new_m[n], new_v[n] = g[None], d[None], nm[None], nv[None]

    return (loss, grad_x[None], *[grad_out[n] for n in WEIGHT_ORDER], *[delta[n] for n in WEIGHT_ORDER],
            *[new_m[n] for n in WEIGHT_ORDER], *[new_v[n] for n in WEIGHT_ORDER])
```

```python
import jax
import jax.numpy as jnp
from jax import lax
from jax.experimental import pallas as pl
from jax.experimental.pallas import tpu as pltpu

F32 = jnp.float32
BF16 = jnp.bfloat16
MESH = pl.DeviceIdType.MESH
ANY = pl.BlockSpec(memory_space=pl.ANY)

VMEM_LIMIT_BYTES = 56 * 1024 * 1024

N_CHIPS = 4
HEAD_DIM = 128
N_Q_HEADS = 8
N_KV_HEADS = 2
Q_GROUP = N_Q_HEADS // N_KV_HEADS
ATTN_WIDTH = N_Q_HEADS * HEAD_DIM
KV_WIDTH = N_KV_HEADS * HEAD_DIM
WINDOW = 128
BLOCK = 128
BAND = 3 * BLOCK
ROPE_THETA = 10000.0
CONV_WIDTH = 1024
MEM_HEADS = 4
MEM_WIDTH = MEM_HEADS * HEAD_DIM
RMS_EPS = 1e-6
NEG_INF = -1e30
ATTN_SCALE = HEAD_DIM ** -0.5

Q_OFF, K_OFF, V_OFF, CU_OFF, CB_OFF, CC_OFF, GL_OFF = 0, 1024, 1280, 1536, 2560, 3584, 4608

ADAM_LR = 0.001
ADAM_B1 = 0.9
ADAM_B2 = 0.999
ADAM_EPS = 1e-08
ADAM_WD = 0.01
ADAM_STEP = 10
ADAM_C1 = 1.0 - ADAM_B1 ** ADAM_STEP
ADAM_C2 = 1.0 - ADAM_B2 ** ADAM_STEP


def _params(n_grid_axes):
    return pltpu.CompilerParams(dimension_semantics=("arbitrary",) * n_grid_axes, vmem_limit_bytes=VMEM_LIMIT_BYTES)


BF16_SUBLANES = 16


def _row_tile(rows, want):
    if rows <= want:
        return rows
    for t in range(want, 0, -BF16_SUBLANES):
        if rows % t == 0:
            return t
    return rows


def _matmul(a, b, *, mode, tm, tn, out_dtypes, name, extras=(), epilogue=None, b_blocks=1, out_blocks=1, after=None):
    if mode == "tn":
        kdim, m = a.shape
    else:
        m, kdim = a.shape
    if b_blocks > 1:
        nb, brows, bcols = b.shape
        assert nb == b_blocks
        if mode == "nn":
            n = bcols * nb
            assert brows == kdim
        else:
            assert mode == "nt" and bcols * nb == kdim
            n = brows
    else:
        n = b.shape[0] if mode == "nt" else b.shape[1]
    tm, tn = min(tm, m), min(tn, n)
    tk = kdim
    assert m % tm == 0 and n % tn == 0, (name, m, n, tm, tn)
    n_extra, n_out = len(extras), len(out_dtypes)
    n_after = 0 if after is None else 1

    if mode == "tn":
        a_spec = pl.BlockSpec((tk, tm), lambda j, i, k: (k, i))
        dims = (((0,), (0,)), ((), ()))
    else:
        a_spec = pl.BlockSpec((tm, tk), lambda j, i, k: (i, k))
        dims = (((1,), (0,)), ((), ())) if mode == "nn" else (((1,), (1,)), ((), ()))

    if b_blocks > 1 and mode == "nn":
        per = b.shape[2] // tn
        assert b.shape[2] % tn == 0
        b_spec = pl.BlockSpec((None, tk, tn), lambda j, i, k: (j // per, k, j % per))
    elif b_blocks > 1:
        b_spec = pl.BlockSpec((b_blocks, tn, b.shape[2]), lambda j, i, k: (0, j, 0))
    elif mode == "nt":
        b_spec = pl.BlockSpec((tn, tk), lambda j, i, k: (j, k))
    else:
        b_spec = pl.BlockSpec((tk, tn), lambda j, i, k: (k, j))

    tile_spec = pl.BlockSpec((tm, tn), lambda j, i, k: (i, j))
    if out_blocks > 1:
        ncols = n // out_blocks
        assert ncols % tn == 0
        oper = ncols // tn
        out_spec = pl.BlockSpec((None, tm, tn), lambda j, i, k: (j // oper, i, j % oper))
        out_shape = [jax.ShapeDtypeStruct((out_blocks, m, ncols), dt) for dt in out_dtypes]
    else:
        out_spec = tile_spec
        out_shape = [jax.ShapeDtypeStruct((m, n), dt) for dt in out_dtypes]

    def body(a_ref, b_ref, *rest):
        extra_refs = rest[:n_extra]
        out_refs = rest[n_extra + n_after:n_extra + n_after + n_out]
        if mode == "nt" and b_blocks > 1:
            cs = b.shape[2]
            acc = None
            for jb in range(b_blocks):
                prod = lax.dot_general(a_ref[:, jb * cs:(jb + 1) * cs].astype(BF16), b_ref[jb].astype(BF16), dims,
                                       preferred_element_type=F32)
                acc = prod if acc is None else acc + prod
        else:
            acc = lax.dot_general(a_ref[...].astype(BF16), b_ref[...].astype(BF16), dims, preferred_element_type=F32)
        tiles = (acc,) if epilogue is None else epilogue(acc, *[r[...] for r in extra_refs])
        for o_ref, t in zip(out_refs, tiles, strict=True):
            o_ref[...] = t.astype(o_ref.dtype)

    outs = pl.pallas_call(
        body,
        name=name,
        grid=(n // tn, m // tm, 1),
        in_specs=[a_spec, b_spec] + [tile_spec] * n_extra + [ANY] * n_after,
        out_specs=[out_spec] * n_out,
        out_shape=out_shape,
        compiler_params=_params(3),
    )(a, b, *extras, *([] if after is None else [after]))
    return outs[0] if n_out == 1 else outs


def _add_residual(acc, res):
    return (acc + res,)


def _matmul_column_blocks(a, b4, blocks, out, *, tm, name, after=None):
    m, kdim = a.shape
    nb, _, cols = b4.shape
    tm = min(tm, m)
    assert m % tm == 0

    def body(j_ref, a_ref, b_ref, *rest):
        rest[-1][...] = jnp.dot(a_ref[...], b_ref[...], preferred_element_type=F32)

    extra = ([] if out is None else [out]) + ([] if after is None else [after])
    n_blocks = blocks.shape[0]
    return pl.pallas_call(
        body, name=name,
        grid_spec=pltpu.PrefetchScalarGridSpec(
            num_scalar_prefetch=1, grid=(n_blocks, m // tm),
            in_specs=[pl.BlockSpec((tm, kdim), lambda j, i, blk: (i, 0)),
                      pl.BlockSpec((None, kdim, cols), lambda j, i, blk: (blk[j], 0, 0))] + [ANY] * len(extra),
            out_specs=pl.BlockSpec((tm, cols), lambda j, i, blk: (i, blk[j]))),
        out_shape=jax.ShapeDtypeStruct((m, nb * cols), F32),
        input_output_aliases={} if out is None else {3: 0},
        compiler_params=_params(2),
    )(blocks, a, b4, *extra)


def _wgrad_half(a, b, core, *, theirs, row_sharded, tm, tn, name, add=None, after=None):
    kdim, m = a.shape
    n = b.shape[1]
    rs, cs = (m // N_CHIPS, n) if row_sharded else (m, n // N_CHIPS)
    rh = rs // 2
    tm, tn = min(tm, rh), min(tn, cs)
    assert rh % tm == 0 and cs % tn == 0, (name, rh, cs, tm, tn)
    mh, per = rh // tm, cs // tn
    has_add = add is not None

    def half(c):
        return 1 - c[0] if theirs else c[0]

    if row_sharded:
        grid = (n // tn, N_CHIPS * mh)
        a_spec = pl.BlockSpec((kdim, tm), lambda j, r, c: (0, ((r // mh) * 2 + half(c)) * mh + r % mh))
        o_spec = pl.BlockSpec((None, tm, tn), lambda j, r, c: (r // mh, r % mh, j))
    else:
        grid = (n // tn, mh)
        a_spec = pl.BlockSpec((kdim, tm), lambda j, r, c: (0, half(c) * mh + r))
        o_spec = pl.BlockSpec((None, tm, tn), lambda j, r, c: (j // per, r, j % per))
    b_spec = pl.BlockSpec((kdim, tn), lambda j, r, c: (0, j))

    def body(c_ref, a_ref, b_ref, *rest):
        o_ref = rest[-1]
        acc = lax.dot_general(a_ref[...].astype(BF16), b_ref[...].astype(BF16), (((0,), (0,)), ((), ())),
                              preferred_element_type=F32)
        if has_add:
            acc = acc + rest[0][...].astype(F32)
        o_ref[...] = acc.astype(BF16)

    operands = [a, b] + ([add] if has_add else []) + ([] if after is None else [after])
    return pl.pallas_call(
        body, name=name,
        grid_spec=pltpu.PrefetchScalarGridSpec(
            num_scalar_prefetch=1, grid=grid,
            in_specs=[a_spec, b_spec] + ([o_spec] if has_add else []) + ([] if after is None else [ANY]),
            out_specs=o_spec),
        out_shape=jax.ShapeDtypeStruct((N_CHIPS, rh, cs), BF16),
        compiler_params=_params(2),
    )(core, *operands)


def _rstd(x):
    return lax.rsqrt(jnp.mean(x * x, axis=-1, keepdims=True) + RMS_EPS)


def _rmsnorm(x, g, name):
    s, d = x.shape
    tr = _row_tile(s, 512)

    def body(x_ref, g_ref, o_ref):
        xv = x_ref[...]
        o_ref[...] = (xv * _rstd(xv) * g_ref[...]).astype(BF16)

    return pl.pallas_call(
        body, name=name, grid=(s // tr,),
        in_specs=[pl.BlockSpec((tr, d), lambda i: (i, 0)), pl.BlockSpec((1, d), lambda i: (0, 0))],
        out_specs=pl.BlockSpec((tr, d), lambda i: (i, 0)),
        out_shape=jax.ShapeDtypeStruct((s, d), BF16),
        compiler_params=_params(1),
    )(x, g)


def _rmsnorm_bwd(dh, x, g, dres, name):
    s, d = x.shape
    tr = _row_tile(s, 512)
    has_res = dres is not None

    def body(*refs):
        if has_res:
            dh_ref, x_ref, g_ref, res_ref, dx_ref, dxb_ref, dg_ref = refs
        else:
            dh_ref, x_ref, g_ref, dx_ref, dxb_ref, dg_ref = refs
        xv = x_ref[...]
        dhv = dh_ref[...].astype(F32)
        r = _rstd(xv)
        xn = xv * r
        dhg = dhv * g_ref[...]
        dx = r * (dhg - xn * jnp.mean(dhg * xn, axis=-1, keepdims=True))
        if has_res:
            dx = dx + res_ref[...]
        dx_ref[...] = dx
        dxb_ref[...] = dx.astype(BF16)
        part = jnp.sum(dhv * xn, axis=0, keepdims=True)

        @pl.when(pl.program_id(0) == 0)
        def _():
            dg_ref[...] = part

        @pl.when(pl.program_id(0) > 0)
        def _():
            dg_ref[...] += part

    row = pl.BlockSpec((tr, d), lambda i: (i, 0))
    vec = pl.BlockSpec((1, d), lambda i: (0, 0))
    return pl.pallas_call(
        body, name=name, grid=(s // tr,),
        in_specs=[row, row, vec] + ([row] if has_res else []),
        out_specs=[row, row, vec],
        out_shape=[jax.ShapeDtypeStruct((s, d), F32), jax.ShapeDtypeStruct((s, d), BF16), jax.ShapeDtypeStruct((1, d), F32)],
        compiler_params=_params(1),
    )(*([dh, x, g] + ([dres] if has_res else [])))


def _loss_head(x3, g, target):
    s, d = x3.shape
    tr = _row_tile(s, 512)

    def body(x_ref, g_ref, t_ref, dx_ref, dxb_ref, sq_ref, dg_ref):
        xv = x_ref[...]
        gv = g_ref[...]
        r = _rstd(xv)
        xn = xv * r
        err = xn * gv - t_ref[...]
        dy = err * (1.0 / d)
        dyg = dy * gv
        dx = r * (dyg - xn * jnp.mean(dyg * xn, axis=-1, keepdims=True))
        dx_ref[...] = dx
        dxb_ref[...] = dx.astype(BF16)
        sq = jnp.sum(jnp.sum(err * err, axis=1, keepdims=True), axis=0, keepdims=True)
        sq = jnp.broadcast_to(sq, (1, 128))
        part = jnp.sum(dy * xn, axis=0, keepdims=True)

        @pl.when(pl.program_id(0) == 0)
        def _():
            sq_ref[...] = sq
            dg_ref[...] = part

        @pl.when(pl.program_id(0) > 0)
        def _():
            sq_ref[...] += sq
            dg_ref[...] += part

    row = pl.BlockSpec((tr, d), lambda i: (i, 0))
    vec = pl.BlockSpec((1, d), lambda i: (0, 0))
    return pl.pallas_call(
        body, name="loss_head", grid=(s // tr,),
        in_specs=[row, vec, row],
        out_specs=[row, row, pl.BlockSpec((1, 128), lambda i: (0, 0)), vec],
        out_shape=[jax.ShapeDtypeStruct((s, d), F32), jax.ShapeDtypeStruct((s, d), BF16),
                   jax.ShapeDtypeStruct((1, 128), F32), jax.ShapeDtypeStruct((1, d), F32)],
        compiler_params=_params(1),
    )(x3, g, target)


def _rope_tables(s):
    inv = 1.0 / (ROPE_THETA ** (jnp.arange(0, HEAD_DIM, 2, dtype=F32) / HEAD_DIM))
    ang = jnp.arange(s, dtype=F32)[:, None] * inv[None, :]
    cos, sin = jnp.cos(ang), jnp.sin(ang)
    return jnp.concatenate([cos, cos], axis=1), jnp.concatenate([-sin, sin], axis=1)


def _swap_halves(t):
    return pltpu.roll(t, HEAD_DIM // 2, 1)


def _rope_fwd(z, cos_t, sin_t):
    s = z.shape[0]
    tr = _row_tile(s, 256)

    def body(zq_ref, zk_ref, zv_ref, c_ref, s_ref, q_ref, k_ref, v_ref):
        c, sn = c_ref[...], s_ref[...]
        for hd in range(N_Q_HEADS):
            cols = slice(hd * HEAD_DIM, (hd + 1) * HEAD_DIM)
            t = zq_ref[:, cols]
            q_ref[:, cols] = (t * c + _swap_halves(t) * sn).astype(BF16)
        for hd in range(N_KV_HEADS):
            cols = slice(hd * HEAD_DIM, (hd + 1) * HEAD_DIM)
            t = zk_ref[:, cols]
            k_ref[:, cols] = (t * c + _swap_halves(t) * sn).astype(BF16)
        v_ref[...] = zv_ref[...].astype(BF16)

    tab = pl.BlockSpec((tr, HEAD_DIM), lambda i: (i, 0))
    return pl.pallas_call(
        body, name="rope_fwd", grid=(s // tr,),
        in_specs=[pl.BlockSpec((tr, ATTN_WIDTH), lambda i: (i, Q_OFF // ATTN_WIDTH)),
                  pl.BlockSpec((tr, KV_WIDTH), lambda i: (i, K_OFF // KV_WIDTH)),
                  pl.BlockSpec((tr, KV_WIDTH), lambda i: (i, V_OFF // KV_WIDTH)), tab, tab],
        out_specs=[pl.BlockSpec((tr, ATTN_WIDTH), lambda i: (i, 0)), pl.BlockSpec((tr, KV_WIDTH), lambda i: (i, 0)),
                   pl.BlockSpec((tr, KV_WIDTH), lambda i: (i, 0))],
        out_shape=[jax.ShapeDtypeStruct((s, ATTN_WIDTH), BF16), jax.ShapeDtypeStruct((s, KV_WIDTH), BF16),
                   jax.ShapeDtypeStruct((s, KV_WIDTH), BF16)],
        compiler_params=_params(1),
    )(z, z, z, cos_t, sin_t)


def _rope_bwd(dq_rot, dk_rot, dv, cos_t, sin_t, dz):
    s = dq_rot.shape[0]
    tr = _row_tile(s, 256)
    qkv_width = V_OFF + KV_WIDTH

    def body(dq_ref, dk_ref, dv_ref, c_ref, s_ref, dz_in_ref, o_ref):
        c, sn = c_ref[...], s_ref[...]
        for hd in range(N_Q_HEADS):
            t = dq_ref[:, hd * HEAD_DIM:(hd + 1) * HEAD_DIM]
            o_ref[:, Q_OFF + hd * HEAD_DIM:Q_OFF + (hd + 1) * HEAD_DIM] = (t * c + _swap_halves(t * sn)).astype(BF16)
        for hd in range(N_KV_HEADS):
            t = dk_ref[:, hd * HEAD_DIM:(hd + 1) * HEAD_DIM]
            o_ref[:, K_OFF + hd * HEAD_DIM:K_OFF + (hd + 1) * HEAD_DIM] = (t * c + _swap_halves(t * sn)).astype(BF16)
        o_ref[:, V_OFF:V_OFF + KV_WIDTH] = dv_ref[...].astype(BF16)

    tab = pl.BlockSpec((tr, HEAD_DIM), lambda i: (i, 0))
    wide = pl.BlockSpec((tr, ATTN_WIDTH), lambda i: (i, 0))
    narrow = pl.BlockSpec((tr, KV_WIDTH), lambda i: (i, 0))
    return pl.pallas_call(
        body, name="rope_bwd", grid=(s // tr,),
        in_specs=[wide, narrow, narrow, tab, tab, ANY],
        out_specs=pl.BlockSpec((tr, qkv_width), lambda i: (i, 0)),
        out_shape=jax.ShapeDtypeStruct(dz.shape, dz.dtype),
        input_output_aliases={5: 0},
        compiler_params=_params(1),
    )(dq_rot, dk_rot, dv, cos_t, sin_t, dz)


def _swa_band(i, s):
    return pl.multiple_of(jnp.clip((i - 1) * BLOCK, 0, s - BAND), BLOCK)


SWA_HEADS_PER_PASS = Q_GROUP


def _swa_probs(q_ref, k_ref, sink_ref, heads, start, valid):
    kv = heads[0] // Q_GROUP
    cols = slice(kv * HEAD_DIM, (kv + 1) * HEAD_DIM)
    kb = k_ref[pl.ds(start, BAND), cols]
    qg = jnp.concatenate([q_ref[:, hd * HEAD_DIM:(hd + 1) * HEAD_DIM] for hd in heads], axis=0)
    sc = lax.dot_general(qg, kb, (((1,), (1,)), ((), ())), preferred_element_type=F32) * ATTN_SCALE
    sc = jnp.where(valid, sc, NEG_INF)
    sk = jnp.concatenate([jnp.full((BLOCK, 1), sink_ref[hd], F32) for hd in heads], axis=0)
    mx = jnp.maximum(jnp.max(sc, axis=1, keepdims=True), sk)
    e = jnp.exp(sc - mx)
    es = jnp.exp(sk - mx)
    inv = 1.0 / (jnp.sum(e, axis=1, keepdims=True) + es)
    return qg, kb, e * inv, es * inv


def _swa_head_passes():
    return [list(range(h0, h0 + SWA_HEADS_PER_PASS)) for h0 in range(0, N_Q_HEADS, SWA_HEADS_PER_PASS)]


def _swa_valid(i, start):
    q_pos = i * BLOCK + lax.broadcasted_iota(jnp.int32, (BLOCK, 1), 0)
    q_pos = jnp.concatenate([q_pos] * SWA_HEADS_PER_PASS, axis=0)
    k_pos = start + lax.broadcasted_iota(jnp.int32, (1, BAND), 1)
    return jnp.abs(k_pos - q_pos) <= WINDOW


def _swa_fwd(q, k, v, sink):
    s = q.shape[0]
    assert s % BLOCK == 0 and s >= BAND

    def body(sink_ref, q_ref, k_ref, v_ref, o_ref):
        i = pl.program_id(0)
        start = _swa_band(i, s)
        valid = _swa_valid(i, start)
        for heads in _swa_head_passes():
            kv = heads[0] // Q_GROUP
            _, _, p, _ = _swa_probs(q_ref, k_ref, sink_ref, heads, start, valid)
            vb = v_ref[pl.ds(start, BAND), kv * HEAD_DIM:(kv + 1) * HEAD_DIM]
            o = jnp.dot(p.astype(BF16), vb, preferred_element_type=F32)
            for g, hd in enumerate(heads):
                o_ref[:, hd * HEAD_DIM:(hd + 1) * HEAD_DIM] = o[g * BLOCK:(g + 1) * BLOCK].astype(BF16)

    whole = pl.BlockSpec((s, KV_WIDTH), lambda i: (0, 0))
    blk = pl.BlockSpec((BLOCK, ATTN_WIDTH), lambda i: (i, 0))
    return pl.pallas_call(
        body, name="swa_fwd", grid=(s // BLOCK,),
        in_specs=[pl.BlockSpec(memory_space=pltpu.SMEM), blk, whole, whole],
        out_specs=blk,
        out_shape=jax.ShapeDtypeStruct((s, ATTN_WIDTH), BF16),
        compiler_params=_params(1),
    )(sink, q, k, v)


def _swa_bwd(q, k, v, d_out, sink):
    s = q.shape[0]

    def body(sink_ref, q_ref, k_ref, v_ref, do_ref, dq_ref, dk_ref, dv_ref, dsink_ref):
        i = pl.program_id(0)

        @pl.when(i == 0)
        def _():
            dk_ref[...] = jnp.zeros_like(dk_ref)
            dv_ref[...] = jnp.zeros_like(dv_ref)
            dsink_ref[...] = jnp.zeros_like(dsink_ref)

        start = _swa_band(i, s)
        valid = _swa_valid(i, start)
        for heads in _swa_head_passes():
            kv = heads[0] // Q_GROUP
            cols = slice(kv * HEAD_DIM, (kv + 1) * HEAD_DIM)
            qg, kb, p, p_sink = _swa_probs(q_ref, k_ref, sink_ref, heads, start, valid)
            vb = v_ref[pl.ds(start, BAND), cols]
            dog = jnp.concatenate([do_ref[:, hd * HEAD_DIM:(hd + 1) * HEAD_DIM] for hd in heads], axis=0)
            dp = lax.dot_general(dog, vb, (((1,), (1,)), ((), ())), preferred_element_type=F32)
            delta = jnp.sum(p * dp, axis=1, keepdims=True)
            ds = (p * (dp - delta) * ATTN_SCALE).astype(BF16)
            dqg = jnp.dot(ds, kb, preferred_element_type=F32)
            dk_ref[pl.ds(start, BAND), cols] += lax.dot_general(ds, qg, (((0,), (0,)), ((), ())), preferred_element_type=F32)
            dv_ref[pl.ds(start, BAND), cols] += lax.dot_general(p.astype(BF16), dog, (((0,), (0,)), ((), ())),
                                                                 preferred_element_type=F32)
            dsk = p_sink * delta
            for g, hd in enumerate(heads):
                dq_ref[:, hd * HEAD_DIM:(hd + 1) * HEAD_DIM] = dqg[g * BLOCK:(g + 1) * BLOCK]
                tot = jnp.sum(dsk[g * BLOCK:(g + 1) * BLOCK], axis=0, keepdims=True)
                dsink_ref[hd:hd + 1, :] -= jnp.broadcast_to(tot, (1, 128))

    whole = pl.BlockSpec((s, KV_WIDTH), lambda i: (0, 0))
    blk = pl.BlockSpec((BLOCK, ATTN_WIDTH), lambda i: (i, 0))
    return pl.pallas_call(
        body, name="swa_bwd", grid=(s // BLOCK,),
        in_specs=[pl.BlockSpec(memory_space=pltpu.SMEM), blk, whole, whole, blk],
        out_specs=[blk, whole, whole, pl.BlockSpec((N_Q_HEADS, 128), lambda i: (0, 0))],
        out_shape=[jax.ShapeDtypeStruct((s, ATTN_WIDTH), F32), jax.ShapeDtypeStruct((s, KV_WIDTH), F32),
                   jax.ShapeDtypeStruct((s, KV_WIDTH), F32), jax.ShapeDtypeStruct((N_Q_HEADS, 128), F32)],
        compiler_params=_params(1),
    )(sink, q, k, v, d_out)


CONV_CHUNK = 256


def _shift_rows(t, rows, down):
    n = t.shape[0]
    rolled = pltpu.roll(t, 1 if down else n - 1, 0)
    edge = 0 if down else n - 1
    return jnp.where(rows == edge, 0.0, rolled)


def _conv_specs(s):
    def z_spec(off):
        return pl.BlockSpec((s, CONV_CHUNK), lambda j, off=off: (0, off // CONV_CHUNK + j))
    chunk = pl.BlockSpec((s, CONV_CHUNK), lambda j: (0, j))
    w_spec = pl.BlockSpec((3, CONV_CHUNK), lambda j: (0, j))
    return z_spec(CU_OFF), z_spec(CB_OFF), z_spec(CC_OFF), chunk, w_spec


def _conv_fwd(z, conv_w):
    s = z.shape[0]
    cu_spec, cb_spec, cc_spec, chunk, w_spec = _conv_specs(s)

    def body(cu_ref, cb_ref, cc_ref, w_ref, o_ref):
        rows = lax.broadcasted_iota(jnp.int32, (s, 1), 0)
        t = cc_ref[...] * cu_ref[...]
        c3 = _shift_rows(t, rows, True) * w_ref[0:1, :] + t * w_ref[1:2, :] + _shift_rows(t, rows, False) * w_ref[2:3, :]
        o_ref[...] = (cb_ref[...] * c3).astype(BF16)

    return pl.pallas_call(
        body, name="conv_fwd", grid=(CONV_WIDTH // CONV_CHUNK,),
        in_specs=[cu_spec, cb_spec, cc_spec, w_spec],
        out_specs=chunk,
        out_shape=jax.ShapeDtypeStruct((s, CONV_WIDTH), BF16),
        compiler_params=_params(1),
    )(z, z, z, conv_w)


def _conv_bwd(z, conv_w, d_co, dz):
    s = z.shape[0]
    cu_spec, cb_spec, cc_spec, chunk, w_spec = _conv_specs(s)
    n_chunks = CONV_WIDTH // CONV_CHUNK
    offsets = (CU_OFF, CB_OFF, CC_OFF)

    def body(cu_ref, cb_ref, cc_ref, w_ref, d_ref, dz_in_ref, dz_ref, dw_ref, buf, sems):
        j = pl.program_id(0)

        def copies(j_at):
            return [pltpu.make_async_copy(buf.at[h], dz_ref.at[:, pl.ds(off + j_at * CONV_CHUNK, CONV_CHUNK)], sems.at[h])
                    for h, off in enumerate(offsets)]

        rows = lax.broadcasted_iota(jnp.int32, (s, 1), 0)
        cu, cc = cu_ref[...], cc_ref[...]
        t = cc * cu
        t_dn, t_up = _shift_rows(t, rows, True), _shift_rows(t, rows, False)
        c3 = t_dn * w_ref[0:1, :] + t * w_ref[1:2, :] + t_up * w_ref[2:3, :]
        d = d_ref[...]
        dc3 = d * cb_ref[...]
        dw_ref[0:1, :] = jnp.sum(dc3 * t_dn, axis=0, keepdims=True)
        dw_ref[1:2, :] = jnp.sum(dc3 * t, axis=0, keepdims=True)
        dw_ref[2:3, :] = jnp.sum(dc3 * t_up, axis=0, keepdims=True)
        dt = _shift_rows(dc3, rows, False) * w_ref[0:1, :] + dc3 * w_ref[1:2, :] + _shift_rows(dc3, rows, True) * w_ref[2:3, :]

        @pl.when(j > 0)
        def _():
            for cp in copies(j):
                cp.wait()

        buf[0] = (dt * cc).astype(BF16)
        buf[1] = (d * c3).astype(BF16)
        buf[2] = (dt * cu).astype(BF16)
        for cp in copies(j):
            cp.start()

        @pl.when(j == n_chunks - 1)
        def _():
            for cp in copies(j):
                cp.wait()

    return pl.pallas_call(
        body, name="conv_bwd", grid=(n_chunks,),
        in_specs=[cu_spec, cb_spec, cc_spec, w_spec, chunk, ANY],
        out_specs=[ANY, w_spec],
        out_shape=[jax.ShapeDtypeStruct(dz.shape, dz.dtype), jax.ShapeDtypeStruct((3, CONV_WIDTH), F32)],
        input_output_aliases={5: 0},
        scratch_shapes=[pltpu.VMEM((3, s, CONV_CHUNK), BF16), pltpu.SemaphoreType.DMA((3,))],
        compiler_params=_params(1),
    )(z, z, z, conv_w, d_co, dz)


GATE_CHUNK = 512


def _gate_specs(s, d, tr):
    n_chunks = d // GATE_CHUNK
    za = pl.BlockSpec((tr, GATE_CHUNK), lambda j, i: (i, GL_OFF // GATE_CHUNK + j))
    zc = pl.BlockSpec((tr, GATE_CHUNK), lambda j, i: (i, GL_OFF // GATE_CHUNK + n_chunks + j))
    ba = pl.BlockSpec((1, GATE_CHUNK), lambda j, i: (0, j))
    bc = pl.BlockSpec((1, GATE_CHUNK), lambda j, i: (0, n_chunks + j))
    tile = pl.BlockSpec((tr, GATE_CHUNK), lambda j, i: (i, j))
    return za, zc, ba, bc, tile


def _gate_fwd(z, b_gate, ya, yc):
    s, d = ya.shape
    tr = _row_tile(s, 512)
    za, zc, ba, bc, tile = _gate_specs(s, d, tr)

    def body(za_ref, zc_ref, ba_ref, bc_ref, ya_ref, yc_ref, o_ref):
        ga = jax.nn.sigmoid(za_ref[...] + ba_ref[...])
        gc = jax.nn.sigmoid(zc_ref[...] + bc_ref[...])
        o_ref[...] = (ga * ya_ref[...] + gc * yc_ref[...]).astype(BF16)

    return pl.pallas_call(
        body, name="gate_fwd", grid=(d // GATE_CHUNK, s // tr),
        in_specs=[za, zc, ba, bc, tile, tile],
        out_specs=tile,
        out_shape=jax.ShapeDtypeStruct((s, d), BF16),
        compiler_params=_params(2),
    )(z, z, b_gate, b_gate, ya, yc)


def _gate_bwd(z, b_gate, ya, yc, dmix):
    s, d = ya.shape
    tr = _row_tile(s, 512)
    za, zc, ba, bc, tile = _gate_specs(s, d, tr)
    vec = pl.BlockSpec((1, GATE_CHUNK), lambda j, i: (0, j))
    n_rows = s // tr
    in_width = z.shape[1]

    def body(za_ref, zc_ref, ba_ref, bc_ref, ya_ref, yc_ref, dm_ref, dya_ref, dyc_ref, dz_ref, dba_ref, dbc_ref, buf, sems):
        j, i = pl.program_id(0), pl.program_id(1)

        def copies(j_at, i_at):
            rows = pl.ds(i_at * tr, tr)
            return [pltpu.make_async_copy(buf.at[h], dz_ref.at[rows, pl.ds(GL_OFF + h * d + j_at * GATE_CHUNK, GATE_CHUNK)],
                                          sems.at[h]) for h in range(2)]

        ga = jax.nn.sigmoid(za_ref[...] + ba_ref[...])
        gc = jax.nn.sigmoid(zc_ref[...] + bc_ref[...])
        dm = dm_ref[...]
        dya_ref[...] = (dm * ga).astype(BF16)
        dyc_ref[...] = (dm * gc).astype(BF16)
        dla = dm * ya_ref[...] * ga * (1.0 - ga)
        dlc = dm * yc_ref[...] * gc * (1.0 - gc)

        @pl.when(j * n_rows + i > 0)
        def _():
            for cp in copies(j, i):
                cp.wait()

        buf[0] = dla.astype(BF16)
        buf[1] = dlc.astype(BF16)
        for cp in copies(j, i):
            cp.start()

        @pl.when((j == d // GATE_CHUNK - 1) & (i == n_rows - 1))
        def _():
            for cp in copies(j, i):
                cp.wait()

        pa = jnp.sum(dla, axis=0, keepdims=True)
        pc = jnp.sum(dlc, axis=0, keepdims=True)

        @pl.when(i == 0)
        def _():
            dba_ref[...] = pa
            dbc_ref[...] = pc

        @pl.when(i > 0)
        def _():
            dba_ref[...] += pa
            dbc_ref[...] += pc

    big = jax.ShapeDtypeStruct((s, d), BF16)
    small = jax.ShapeDtypeStruct((1, d), F32)
    return pl.pallas_call(
        body, name="gate_bwd", grid=(d // GATE_CHUNK, n_rows),
        in_specs=[za, zc, ba, bc, tile, tile, tile],
        out_specs=[tile, tile, ANY, vec, vec],
        out_shape=[big, big, jax.ShapeDtypeStruct((s, in_width), BF16), small, small],
        scratch_shapes=[pltpu.VMEM((2, tr, GATE_CHUNK), BF16), pltpu.SemaphoreType.DMA((2,))],
        compiler_params=_params(2),
    )(z, z, b_gate, b_gate, ya, yc, dmix)


def _cross_probs(q_ref, kv_ref, hd):
    cols = slice(hd * HEAD_DIM, (hd + 1) * HEAD_DIM)
    qh = q_ref[:, cols]
    kh = kv_ref[:, cols]
    sc = lax.dot_general(qh, kh, (((1,), (1,)), ((), ())), preferred_element_type=F32) * ATTN_SCALE
    e = jnp.exp(sc - jnp.max(sc, axis=1, keepdims=True))
    return qh, kh, e * (1.0 / jnp.sum(e, axis=1, keepdims=True))


def _cross_fwd(qc, kvc):
    s = qc.shape[0]
    n_mem = kvc.shape[0]
    tq = _row_tile(s, 256)

    def body(q_ref, kv_ref, o_ref):
        for hd in range(MEM_HEADS):
            _, _, p = _cross_probs(q_ref, kv_ref, hd)
            vh = kv_ref[:, MEM_WIDTH + hd * HEAD_DIM:MEM_WIDTH + (hd + 1) * HEAD_DIM]
            o_ref[:, hd * HEAD_DIM:(hd + 1) * HEAD_DIM] = jnp.dot(p.astype(BF16), vh, preferred_element_type=F32).astype(BF16)

    return pl.pallas_call(
        body, name="cross_fwd", grid=(s // tq,),
        in_specs=[pl.BlockSpec((tq, MEM_WIDTH), lambda i: (i, 0)), pl.BlockSpec((n_mem, 2 * MEM_WIDTH), lambda i: (0, 0))],
        out_specs=pl.BlockSpec((tq, MEM_WIDTH), lambda i: (i, 0)),
        out_shape=jax.ShapeDtypeStruct((s, MEM_WIDTH), BF16),
        compiler_params=_params(1),
    )(qc, kvc)


def _cross_bwd(qc, kvc, d_out):
    s = qc.shape[0]
    n_mem = kvc.shape[0]
    tq = _row_tile(s, 256)

    def body(q_ref, kv_ref, do_ref, dq_ref, dkv_ref):
        @pl.when(pl.program_id(0) == 0)
        def _():
            dkv_ref[...] = jnp.zeros_like(dkv_ref)

        for hd in range(MEM_HEADS):
            cols = slice(hd * HEAD_DIM, (hd + 1) * HEAD_DIM)
            vcols = slice(MEM_WIDTH + hd * HEAD_DIM, MEM_WIDTH + (hd + 1) * HEAD_DIM)
            qh, kh, p = _cross_probs(q_ref, kv_ref, hd)
            doh = do_ref[:, cols]
            dp = lax.dot_general(doh, kv_ref[:, vcols], (((1,), (1,)), ((), ())), preferred_element_type=F32)
            ds = (p * (dp - jnp.sum(p * dp, axis=1, keepdims=True)) * ATTN_SCALE).astype(BF16)
            dq_ref[:, cols] = jnp.dot(ds, kh, preferred_element_type=F32).astype(BF16)
            dkv_ref[:, cols] += lax.dot_general(ds, qh, (((0,), (0,)), ((), ())), preferred_element_type=F32)
            dkv_ref[:, vcols] += lax.dot_general(p.astype(BF16), doh, (((0,), (0,)), ((), ())), preferred_element_type=F32)

    qspec = pl.BlockSpec((tq, MEM_WIDTH), lambda i: (i, 0))
    kvspec = pl.BlockSpec((n_mem, 2 * MEM_WIDTH), lambda i: (0, 0))
    return pl.pallas_call(
        body, name="cross_bwd", grid=(s // tq,),
        in_specs=[qspec, kvspec, qspec],
        out_specs=[qspec, kvspec],
        out_shape=[jax.ShapeDtypeStruct((s, MEM_WIDTH), BF16), jax.ShapeDtypeStruct((n_mem, 2 * MEM_WIDTH), F32)],
        compiler_params=_params(1),
    )(qc, kvc, d_out)


def _swiglu_fwd(up, gate):
    sg = jax.nn.sigmoid(gate)
    silu = gate * sg
    return silu * up, up * (sg * (1.0 + gate * (1.0 - sg))), silu


def _swiglu_bwd(d_act, dact_dgate, dact_dup):
    return d_act * dact_dgate.astype(F32), d_act * dact_dup.astype(F32)


GATHER_GROUPS = {"in": ("w_in", "conv_w"), "mid": ("w_attn_out", "w_conv_out", "w_o", "w_cq", "w_ckv", "w_co"),
                 "gate": ("w_gate",), "up": ("w_up",), "down": ("w_down",)}


def _local_step(xs, mems, target, small, fetch, reduce):
    s, d = xs.shape
    w4 = {}
    cos_t, sin_t = _rope_tables(s)

    def near(group, done, then, after):
        waits = [("direct", group)] + ([("pass_near", done), ("pass_far", done)] if done else [])
        starts = [("forward", group), ("pass_near", group)] + [("direct", g) for g in then]
        tok = fetch.step("gather_near_" + group, waits, starts, after)
        if done:
            w4.update(fetch.arrays(done))
        return tok

    def far(group, then, after):
        return fetch.step("gather_far_" + group, [("forward", group)], [("pass_far", group)] + [("direct", g) for g in then], after)

    def last(group, after):
        tok = fetch.step("gather_done_" + group, [("pass_near", group), ("pass_far", group)], [], after)
        w4.update(fetch.arrays(group))
        return tok

    h = _rmsnorm(xs, small["g_mix"], "norm_mix")
    slots_filled = [a for g in ("gate", "up", "down") for a in fetch.arrays(g).values()]
    chip_x, chip_y = reduce.place[0] // 2, reduce.place[0] % 2
    own_block = jnp.stack([2 * chip_x + chip_y]).astype(jnp.int32)
    near_blocks = jnp.stack([2 * (1 - chip_x) + chip_y, 2 * chip_x + (1 - chip_y)]).astype(jnp.int32)
    far_block = jnp.stack([2 * (1 - chip_x) + (1 - chip_y)]).astype(jnp.int32)
    z = _matmul_column_blocks(h, fetch.arrays("in")["w_in"], own_block, None, tm=512, name="in_proj_own")
    tok = near("in", None, ["mid"], [z] + slots_filled)
    memn = _rmsnorm(mems, small["g_mem"], "norm_mem")
    tok = fetch.step("gather_near_done_in", [("pass_near", "in")], [], [tok, cos_t, sin_t, memn])
    z = _matmul_column_blocks(h, fetch.arrays("in")["w_in"], near_blocks, z, tm=1024, name="in_proj_near", after=tok)
    tok = far("in", [], z)
    tok = fetch.step("gather_done_in", [("pass_far", "in")], [], tok)
    w4.update(fetch.arrays("in"))
    z = _matmul_column_blocks(h, w4["w_in"], far_block, z, tm=1024, name="in_proj_far", after=tok)
    conv4 = w4["conv_w"]
    conv_w = conv4[:, :3, :].transpose(1, 0, 2).reshape(3, N_CHIPS * conv4.shape[2])
    c_in = w4["w_in"].shape[2]
    tok = near("mid", None, ["gate"], z)
    q_rot, k_rot, v_b = _rope_fwd(z, cos_t, sin_t)
    attn = _swa_fwd(q_rot, k_rot, v_b, small["sink"])
    co = _conv_fwd(z, conv_w)
    tok = far("mid", ["up"], attn)
    tok = last("mid", tok)
    w_o = w4["w_o"].reshape(-1, w4["w_o"].shape[-1])
    c_d = w4["w_attn_out"].shape[2]
    ya = _matmul(attn, w4["w_attn_out"], mode="nn", tm=2048, tn=c_d, out_dtypes=[F32], name="attn_out_proj",
                 b_blocks=N_CHIPS, after=tok)
    yc = _matmul(co, w4["w_conv_out"], mode="nn", tm=2048, tn=c_d, out_dtypes=[F32], name="conv_out_proj",
                 b_blocks=N_CHIPS)
    mix = _gate_fwd(z, small["b_gate"], ya, yc)
    x1 = _matmul(mix, w_o, mode="nn", tm=1024, tn=1024, out_dtypes=[F32], name="mix_out_proj", extras=[xs],
                 epilogue=_add_residual)
    tok = near("gate", None, ["down"], x1)
    w_cq = w4["w_cq"].reshape(-1, w4["w_cq"].shape[-1])
    w_ckv = w4["w_ckv"].reshape(-1, w4["w_ckv"].shape[-1])
    hc = _rmsnorm(x1, small["g_cross"], "norm_cross")
    qc = _matmul(hc, w_cq, mode="nn", tm=2048, tn=MEM_WIDTH, out_dtypes=[BF16], name="cross_q_proj", after=tok)
    kvc = _matmul(memn, w_ckv, mode="nn", tm=256, tn=2 * MEM_WIDTH, out_dtypes=[BF16], name="cross_kv_proj")
    oc = _cross_fwd(qc, kvc)
    tok = far("gate", [], oc)
    x2 = _matmul(oc, w4["w_co"], mode="nn", tm=2048, tn=c_d, out_dtypes=[F32], name="cross_out_proj",
                 extras=[x1], epilogue=_add_residual, b_blocks=N_CHIPS, after=tok)
    hf = _rmsnorm(x2, small["g_ffn"], "norm_ffn")
    tok = near("up", "gate", [], hf)
    c_ff = w4["w_gate"].shape[2]
    gate = _matmul(hf, w4["w_gate"], mode="nn", tm=1024, tn=c_ff, out_dtypes=[F32], name="ffn_gate_proj", b_blocks=N_CHIPS,
                   after=tok)
    tok = far("up", [], gate)
    tok = near("down", "up", [], tok)
    act, dact_dgate, dact_dup = _matmul(hf, w4["w_up"], mode="nn", tm=1024, tn=c_ff, out_dtypes=[BF16, BF16, BF16],
                                        name="ffn_up_proj", extras=[gate], epilogue=_swiglu_fwd, b_blocks=N_CHIPS, after=tok)
    tok = far("down", [], act)
    last("down", tok)
    w_down = w4["w_down"].reshape(-1, w4["w_down"].shape[-1])
    x3 = _matmul(act, w_down, mode="nn", tm=512, tn=512, out_dtypes=[F32], name="ffn_down_proj", extras=[x2],
                 epilogue=_add_residual)
    dx3, dx3b, sq, dg_final = _loss_head(x3, small["g_final"], target)

    da, du = _matmul(dx3b, w_down, mode="nt", tm=1024, tn=c_ff, out_dtypes=[BF16, BF16], name="ffn_down_bwd",
                     extras=[dact_dgate, dact_dup], epilogue=_swiglu_bwd)
    core = reduce.core
    ffn_shape = dict(row_sharded=False, tm=1024, tn=c_ff)
    g_down = _matmul(act, dx3b, mode="tn", tm=c_ff, tn=1024, out_dtypes=[BF16], name="ffn_down_wgrad")
    tok = reduce.add("down", {"w_down": g_down}, da)
    t_gate = _wgrad_half(hf, da, core, theirs=True, name="ffn_gate_wgrad_theirs", after=tok, **ffn_shape)
    tok = reduce.step("down", t_gate)
    t_up = _wgrad_half(hf, du, core, theirs=True, name="ffn_up_wgrad_theirs", after=tok, **ffn_shape)
    tok = reduce.send("ffn", {"w_gate": t_gate, "w_up": t_up}, dx3b)
    dhf = _matmul(da, w4["w_gate"], mode="nt", tm=512, tn=1024, out_dtypes=[F32], name="ffn_gate_bwd", b_blocks=N_CHIPS,
                  after=tok)
    got = reduce.received("ffn", dhf)
    p_gate = _wgrad_half(hf, da, core, theirs=False, name="ffn_gate_wgrad_mine", add=got["w_gate"], **ffn_shape)
    p_up = _wgrad_half(hf, du, core, theirs=False, name="ffn_up_wgrad_mine", add=got["w_up"], **ffn_shape)
    tok = reduce.add_parts("ffn", {"w_gate": p_gate, "w_up": p_up})
    dhf = _matmul(du, w4["w_up"], mode="nt", tm=512, tn=1024, out_dtypes=[F32], name="ffn_up_bwd", extras=[dhf],
                  epilogue=_add_residual, b_blocks=N_CHIPS, after=tok)
    tok = reduce.step("down", dhf)
    dx2, dx2b, dg_ffn = _rmsnorm_bwd(dhf, x2, small["g_ffn"], dx3, "norm_ffn_bwd")

    d_oc = _matmul(dx2b, w4["w_co"], mode="nt", tm=1024, tn=MEM_WIDTH, out_dtypes=[BF16], name="cross_out_bwd",
                   b_blocks=N_CHIPS, after=tok)
    g_co = _matmul(oc, dx2b, mode="tn", tm=MEM_WIDTH, tn=c_d, out_dtypes=[BF16], name="cross_out_wgrad", out_blocks=N_CHIPS)
    tok = reduce.step("down", g_co)
    dqc, dkvc = _cross_bwd(qc, kvc, d_oc)
    g_cq = _matmul(hc, dqc, mode="tn", tm=1024, tn=MEM_WIDTH, out_dtypes=[BF16], name="cross_q_wgrad", after=tok)
    dhc = _matmul(dqc, w_cq, mode="nt", tm=1024, tn=1024, out_dtypes=[F32], name="cross_q_bwd")
    g_ckv = _matmul(memn, dkvc, mode="tn", tm=1024, tn=2 * MEM_WIDTH, out_dtypes=[BF16], name="cross_kv_wgrad")
    dx1, dx1b, dg_cross = _rmsnorm_bwd(dhc, x1, small["g_cross"], dx2, "norm_cross_bwd")

    dmix = _matmul(dx1b, w_o, mode="nt", tm=1024, tn=1024, out_dtypes=[F32], name="mix_out_bwd")
    g_o = _matmul(mix, dx1b, mode="tn", tm=1024, tn=1024, out_dtypes=[BF16], name="mix_out_wgrad")
    dya, dyc, dz, db_a, db_c = _gate_bwd(z, small["b_gate"], ya, yc, dmix)
    d_attn = _matmul(dya, w4["w_attn_out"], mode="nt", tm=1024, tn=ATTN_WIDTH, out_dtypes=[BF16], name="attn_out_bwd",
                     b_blocks=N_CHIPS)
    g_ao = _matmul(attn, dya, mode="tn", tm=ATTN_WIDTH, tn=c_d, out_dtypes=[BF16], name="attn_out_wgrad", out_blocks=N_CHIPS)
    d_co = _matmul(dyc, w4["w_conv_out"], mode="nt", tm=1024, tn=CONV_WIDTH, out_dtypes=[F32], name="conv_out_bwd",
                   b_blocks=N_CHIPS)
    g_cvo = _matmul(co, dyc, mode="tn", tm=CONV_WIDTH, tn=c_d, out_dtypes=[BF16], name="conv_out_wgrad", out_blocks=N_CHIPS)
    tok = reduce.step("ffn", g_cvo)
    tok = reduce.add("mid", {"w_co": g_co, "w_cq": g_cq, "w_ckv": g_ckv, "w_o": g_o, "w_attn_out": g_ao, "w_conv_out": g_cvo}, tok)
    dz, d_conv_w = _conv_bwd(z, conv_w, d_co, dz)
    dq_rot, dk_rot, dv, dsink = _swa_bwd(q_rot, k_rot, v_b, d_attn, small["sink"])
    tok = reduce.step("mid", dq_rot)
    dz = _rope_bwd(dq_rot, dk_rot, dv, cos_t, sin_t, dz)
    in_shape = dict(row_sharded=False, tm=1024, tn=c_in)
    t_in = _wgrad_half(h, dz, core, theirs=True, name="in_proj_wgrad_theirs", after=tok, **in_shape)
    tok = reduce.send("in", {"w_in": t_in}, dk_rot)
    tok = reduce.step("ffn", tok, count=1)
    dmemn = _matmul(dkvc, w_ckv, mode="nt", tm=256, tn=1024, out_dtypes=[F32], name="cross_kv_bwd", after=tok)
    _, _, dg_mem = _rmsnorm_bwd(dmemn, mems, small["g_mem"], None, "norm_mem_bwd")
    got = reduce.received("in", dg_mem)
    p_in = _wgrad_half(h, dz, core, theirs=False, name="in_proj_wgrad_mine", add=got["w_in"], **in_shape)
    tok = reduce.add_parts("in", {"w_in": p_in})
    tok = reduce.step("ffn", tok)
    tok = reduce.step("mid", tok)
    dh = _matmul(dz, w4["w_in"], mode="nt", tm=512, tn=512, out_dtypes=[F32], name="in_proj_bwd", b_blocks=N_CHIPS,
                 after=tok)
    grad_x, _, dg_mix = _rmsnorm_bwd(dh, xs, small["g_mix"], dx1, "norm_mix_bwd")

    small_grads = {
        "g_mix": dg_mix, "sink": dsink[:, 0], "b_gate": jnp.concatenate([db_a, db_c], axis=1), "g_cross": dg_cross,
        "g_mem": dg_mem, "g_ffn": dg_ffn, "g_final": dg_final, "conv_w": d_conv_w,
    }
    return sq, grad_x, small_grads


def _pair_sum(g4s, ras, core, name):
    n = len(g4s)

    def body(c_ref, *refs):
        for k in range(n):
            g_ref, r_ref, o_ref = refs[2 * k], refs[2 * k + 1], refs[2 * n + k]
            o_ref[...] = (g_ref[...].astype(F32) + r_ref[...].astype(F32)).astype(BF16)

    in_specs, out_specs, out_shape, operands = [], [], [], []
    for g4, ra in zip(g4s, ras, strict=True):
        nb, rs, cs = g4.shape
        rh = rs // 2
        assert nb == N_CHIPS and ra.shape == (nb, rh, cs) and rh % BF16_SUBLANES == 0, (g4.shape, ra.shape)
        plain = pl.BlockSpec((None, rh, cs), lambda j, c: (j, 0, 0))
        in_specs += [pl.BlockSpec((None, rh, cs), lambda j, c: (j, c[0], 0)), plain]
        out_specs.append(plain)
        out_shape.append(jax.ShapeDtypeStruct((nb, rh, cs), BF16))
        operands += [g4, ra]
    return pl.pallas_call(
        body, name=name,
        grid_spec=pltpu.PrefetchScalarGridSpec(num_scalar_prefetch=1, grid=(N_CHIPS,), in_specs=in_specs, out_specs=out_specs),
        out_shape=out_shape,
        compiler_params=_params(1),
    )(core, *operands)


def _adamw_update(w, g, m, v):
    nm = ADAM_B1 * m + (1.0 - ADAM_B1) * g
    nv = ADAM_B2 * v + (1.0 - ADAM_B2) * (g * g)
    m_hat = nm / ADAM_C1
    v_hat = nv / ADAM_C2
    return -ADAM_LR * (m_hat / (jnp.sqrt(v_hat) + ADAM_EPS) + ADAM_WD * w), nm, nv


ADAMW_STEPS = 4
ADAMW_BYTES_PER_ELEMENT = 40


def _adamw_calls(names, shards):
    step_bytes = sum(shards[n].size // (2 * ADAMW_STEPS) * ADAMW_BYTES_PER_ELEMENT for n in names)
    return [list(names)] if 2 * step_bytes <= VMEM_LIMIT_BYTES * 3 // 4 else [[n] for n in names]


def _adamw_row_tiles(ws):
    for w in ws:
        assert w.shape[0] % (2 * ADAMW_STEPS * BF16_SUBLANES) == 0, w.shape
    return [w.shape[0] // (2 * ADAMW_STEPS) for w in ws]


def _adamw_own_half(ws, ms, vs, parts, rcs, place, name, after=None):
    n = len(ws)

    def body(p_ref, *refs):
        ins, outs = refs[:5 * n], refs[len(refs) - 5 * n:]
        for k in range(n):
            w_ref, m_ref, v_ref, own_ref, r_ref = ins[5 * k:5 * k + 5]
            gx_ref, g_ref, d_ref, nm_ref, nv_ref = outs[5 * k:5 * k + 5]
            g = own_ref[...].astype(F32)
            for j in range(r_ref.shape[0]):
                g = g + r_ref[j].astype(F32)
            gx_ref[...] = g
            g_ref[...] = g
            d_ref[...], nm_ref[...], nv_ref[...] = _adamw_update(w_ref[...], g, m_ref[...], v_ref[...])

    in_specs, out_specs, out_shape, operands = [], [], [], []
    for w, m, v, p, r, tr in zip(ws, ms, vs, parts, rcs, _adamw_row_tiles(ws), strict=True):
        cols = w.shape[1]
        mine = pl.BlockSpec((tr, cols), lambda i, pos: (pos[1] * ADAMW_STEPS + i, 0))
        in_specs += [mine, mine, mine, pl.BlockSpec((None, tr, cols), lambda i, pos: (pos[0], i, 0)),
                     pl.BlockSpec((r.shape[0], tr, cols), lambda i, pos: (0, i, 0))]
        out_specs += [mine] * 5
        out_shape += [jax.ShapeDtypeStruct(w.shape, F32)] * 5
        operands += [w, m, v, p, r]
    outs = pl.pallas_call(
        body, name=name,
        grid_spec=pltpu.PrefetchScalarGridSpec(
            num_scalar_prefetch=1, grid=(ADAMW_STEPS,),
            in_specs=in_specs + ([] if after is None else [ANY]), out_specs=out_specs),
        out_shape=out_shape,
        compiler_params=_params(1),
    )(place, *operands, *([] if after is None else [after]))
    return [tuple(outs[5 * k:5 * k + 5]) for k in range(n)]


def _adamw_other_half(ws, ms, vs, exchanged, halves, place, name, after=None):
    n = len(ws)

    def body(p_ref, *refs):
        ins, outs = refs[:8 * n], refs[len(refs) - 4 * n:]
        for k in range(n):
            w_ref, m_ref, v_ref, gx_ref = ins[8 * k:8 * k + 4]
            g_ref, d_ref, nm_ref, nv_ref = outs[4 * k:4 * k + 4]
            gv = gx_ref[...]
            g_ref[...] = gv
            d_ref[...], nm_ref[...], nv_ref[...] = _adamw_update(w_ref[...], gv, m_ref[...], v_ref[...])

    in_specs, out_specs, out_shape, operands, aliases = [], [], [], [], {}
    for k, (w, m, v, gx, half, tr) in enumerate(zip(ws, ms, vs, exchanged, halves, _adamw_row_tiles(ws), strict=True)):
        other = pl.BlockSpec((tr, w.shape[1]), lambda i, pos: ((1 - pos[1]) * ADAMW_STEPS + i, 0))
        in_specs += [other] * 4 + [ANY] * 4
        out_specs += [other] * 4
        out_shape += [jax.ShapeDtypeStruct(w.shape, F32)] * 4
        operands += [w, m, v, gx, *half]
        aliases.update({1 + 8 * k + 4 + j: 4 * k + j for j in range(4)})
    outs = pl.pallas_call(
        body, name=name,
        grid_spec=pltpu.PrefetchScalarGridSpec(
            num_scalar_prefetch=1, grid=(ADAMW_STEPS,),
            in_specs=in_specs + ([] if after is None else [ANY]), out_specs=out_specs),
        out_shape=out_shape,
        input_output_aliases=aliases,
        compiler_params=_params(1),
    )(place, *operands, *([] if after is None else [after]))
    return [tuple(outs[4 * k:4 * k + 4]) for k in range(n)]


def _cast_to_slot(w, place, dtype, name, after=None):
    rows, cols = w.shape
    tr = _row_tile(rows, 1024)

    def body(p_ref, w_ref, *rest):
        o_ref = rest[-1]
        o_ref[...] = w_ref[...].astype(dtype)

    return pl.pallas_call(
        body, name=name,
        grid_spec=pltpu.PrefetchScalarGridSpec(
            num_scalar_prefetch=1, grid=(rows // tr,),
            in_specs=[pl.BlockSpec((tr, cols), lambda i, p: (i, 0))] + ([] if after is None else [ANY]),
            out_specs=pl.BlockSpec((None, tr, cols), lambda i, p: (p[0], i, 0))),
        out_shape=jax.ShapeDtypeStruct((N_CHIPS, rows, cols), dtype),
        compiler_params=_params(1),
    )(place, w, *([] if after is None else [after]))


def _adamw(w, g, m, v, name, after=None):
    rows, cols = w.shape
    tr = _row_tile(rows, 256)

    def body(w_ref, g_ref, m_ref, v_ref, *rest):
        go_ref, d_ref, nm_ref, nv_ref = rest[-4:]
        gv = g_ref[...]
        go_ref[...] = gv
        d_ref[...], nm_ref[...], nv_ref[...] = _adamw_update(w_ref[...], gv, m_ref[...], v_ref[...])

    tile = pl.BlockSpec((tr, cols), lambda i: (i, 0))
    shape = jax.ShapeDtypeStruct((rows, cols), F32)
    return pl.pallas_call(
        body, name=name, grid=(rows // tr,),
        in_specs=[tile] * 4 + ([] if after is None else [ANY]), out_specs=[tile] * 4, out_shape=[shape] * 4,
        compiler_params=_params(1),
    )(w, g, m, v, *([] if after is None else [after]))


def _mesh_pos():
    return lax.axis_index("x"), lax.axis_index("y"), lax.axis_index("c")


def _other_chips(x, y):
    return [(1 - x, y), (x, 1 - y), (1 - x, 1 - y)]


def _half_rows(ref, which):
    rh = ref.shape[-2] // 2
    return ref.at[pl.ds(which * rh, rh), :]


def _remote(src, dst, send_sems, recv_sems, sem, to):
    return pltpu.make_async_remote_copy(src_ref=src, dst_ref=dst, send_sem=send_sems.at[sem], recv_sem=recv_sems.at[sem],
                                        device_id=to, device_id_type=MESH)


HBM = pl.BlockSpec(memory_space=pltpu.HBM)
SEM = pl.BlockSpec(memory_space=pltpu.SEMAPHORE)
DATAFLOW_EFFECT = pltpu.SideEffectType.DATAFLOW_SIDE_EFFECTING


def _in_hbm(arrays):
    return [pltpu.with_memory_space_constraint(a, pltpu.HBM) for a in arrays]


def _hbm_like(arrays):
    return [pltpu.HBM(a.shape, a.dtype) for a in arrays]


GATHER_COPIES_PER_ARRAY = {"direct": 2, "forward": 2, "pass_near": 2, "pass_far": 1}


def _gather_copies(kind, refs, x, y, c):
    me, near_x, near_y, far = 2 * x + y, 2 * (1 - x) + y, 2 * x + (1 - y), 2 * (1 - x) + (1 - y)
    to_x, to_y, sibling = (1 - x, y, c), (x, 1 - y, c), (x, y, 1 - c)
    out = []
    for ref in refs:
        rh = ref.shape[1] // 2
        rq = rh // 2

        def half(chip, ref=ref, rh=rh):
            return ref.at[chip, pl.ds(c * rh, rh), :]

        def quarter(chip, q, ref=ref, rh=rh, rq=rq):
            return ref.at[chip, pl.ds(c * rh + q * rq, rq), :]

        if kind == "direct":
            out += [(half(me), half(me), to_x), (half(me), half(me), to_y)]
        elif kind == "forward":
            out += [(quarter(near_x, 0), quarter(near_x, 0), to_y), (quarter(near_y, 1), quarter(near_y, 1), to_x)]
        elif kind == "pass_near":
            out += [(half(near_x), half(near_x), sibling), (half(near_y), half(near_y), sibling)]
        else:
            assert kind == "pass_far"
            out += [(half(far), half(far), sibling)]
    return out


def _gather_step(name, bufs, waits, starts, after):
    nb, nw, ns = len(bufs), len(waits), len(starts)
    after = [] if after is None else list(after) if isinstance(after, (list, tuple)) else [after]
    n_after = len(after)

    def body(*refs):
        ins = refs[:nb]
        wait_sems = refs[nb:nb + 2 * nw]
        start_sems = refs[nb + 2 * nw + n_after:nb + 2 * nw + n_after + 2 * ns]
        token = refs[-1]
        x, y, c = _mesh_pos()
        for j, (kind, idxs, _, _) in enumerate(waits):
            for i, (s_ref, d_ref, to) in enumerate(_gather_copies(kind, [ins[t] for t in idxs], x, y, c)):
                came = _remote(s_ref, d_ref, wait_sems[2 * j], wait_sems[2 * j + 1], i, to)
                came.wait_recv()
                came.wait_send()
        for j, (kind, idxs) in enumerate(starts):
            for i, (s_ref, d_ref, to) in enumerate(_gather_copies(kind, [ins[t] for t in idxs], x, y, c)):
                _remote(s_ref, d_ref, start_sems[2 * j], start_sems[2 * j + 1], i, to).start()
        token[...] = jnp.zeros_like(token)

    sems = []
    for kind, idxs in starts:
        sems += [pltpu.SemaphoreType.DMA((GATHER_COPIES_PER_ARRAY[kind] * len(idxs),))] * 2
    operands = _in_hbm(bufs) + [sem for w in waits for sem in w[2:]] + after
    outs = pl.pallas_call(
        body, name=name,
        in_specs=[HBM] * nb + [SEM] * (2 * nw) + [ANY] * n_after,
        out_specs=[SEM] * (2 * ns) + [HBM] * nb + [pl.BlockSpec(memory_space=pltpu.VMEM)],
        out_shape=sems + _hbm_like(bufs) + [jax.ShapeDtypeStruct((8, 128), F32)],
        input_output_aliases={i: 2 * ns + i for i in range(nb)},
        compiler_params=pltpu.CompilerParams(has_side_effects=DATAFLOW_EFFECT),
    )(*operands)
    return outs[2 * ns:2 * ns + nb], [(outs[2 * j], outs[2 * j + 1]) for j in range(ns)], outs[-1]


class _Gather:
    def __init__(self, groups):
        self.groups = groups
        self.bufs = {}
        self.in_flight = {}

    def put(self, slotted):
        self.bufs.update(slotted)

    def step(self, name, waits, starts, after=None):
        names = []
        for _, group in list(waits) + list(starts):
            names += [n for n in self.groups[group] if n not in names]
        index = {n: i for i, n in enumerate(names)}

        def members(group):
            return [index[n] for n in self.groups[group]]

        wait_args = [(kind, members(group)) + self.in_flight.pop((kind, group)) for kind, group in waits]
        start_args = [(kind, members(group)) for kind, group in starts]
        bufs, sems, token = _gather_step(name, [self.bufs[n] for n in names], wait_args, start_args, after)
        self.bufs.update(zip(names, bufs))
        for (kind, group), pair in zip(starts, sems):
            self.in_flight[(kind, group)] = pair
        return token

    def arrays(self, group):
        return {n: self.bufs[n] for n in self.groups[group]}


def _sibling_halves_copies(srcs, dsts, x, y, c):
    out = []
    for s_ref, d_ref in zip(srcs, dsts, strict=True):
        rh = s_ref.shape[1] // 2
        out.append((s_ref.at[:, pl.ds((1 - c) * rh, rh), :], d_ref, (x, y, 1 - c)))
    return out


def _to_sibling_copies(srcs, dsts, x, y, c):
    return [(s_ref, d_ref, (x, y, 1 - c)) for s_ref, d_ref in zip(srcs, dsts, strict=True)]


def _chip_copies(srcs, dsts, x, y, c):
    out = []
    for s_ref, d_ref in zip(srcs, dsts, strict=True):
        for k, (px, py) in enumerate(_other_chips(x, y)):
            out.append((s_ref.at[2 * px + py], d_ref.at[k], (px, py, c)))
    return out


def _join_copies(srcs, dsts, x, y, c):
    out = []
    for s_ref in srcs:
        mine = _half_rows(s_ref, c)
        out.append((mine, mine, (x, y, 1 - c)))
    return out


def _exchange_start(copies_fn, n_copies, srcs, fresh, after, name):
    ns, nb = len(srcs), len(srcs) + len(fresh)

    def body(*refs):
        bufs, send, recv, token = refs[:nb], refs[nb + 1], refs[nb + 2], refs[-1]
        x, y, c = _mesh_pos()
        for i, (s_ref, d_ref, to) in enumerate(copies_fn(bufs[:ns], bufs[ns:] if fresh else bufs[:ns], x, y, c)):
            _remote(s_ref, d_ref, send, recv, i, to).start()
        token[...] = jnp.zeros_like(token)

    sems = [pltpu.SemaphoreType.DMA((n_copies,))] * 2
    outs = pl.pallas_call(
        body, name=name,
        in_specs=[HBM] * nb + [ANY], out_specs=[SEM, SEM] + [HBM] * nb + [pl.BlockSpec(memory_space=pltpu.VMEM)],
        out_shape=sems + _hbm_like(list(srcs) + list(fresh)) + [jax.ShapeDtypeStruct((8, 128), F32)],
        input_output_aliases={i: 2 + i for i in range(nb)},
        compiler_params=pltpu.CompilerParams(has_side_effects=DATAFLOW_EFFECT),
    )(*_in_hbm(list(srcs) + list(fresh)), after)
    return outs[0], outs[1], outs[2:2 + ns], outs[2 + ns:2 + nb], outs[-1]


def _exchange_done(copies_fn, srcs, fresh, send, recv, after, name):
    ns, nb = len(srcs), len(srcs) + len(fresh)

    def body(*refs):
        bufs, send_in, recv_in = refs[:nb], refs[nb], refs[nb + 1]
        x, y, c = _mesh_pos()
        for i, (s_ref, d_ref, to) in enumerate(copies_fn(bufs[:ns], bufs[ns:] if fresh else bufs[:ns], x, y, c)):
            came = _remote(s_ref, d_ref, send_in, recv_in, i, to)
            came.wait_send()
            came.wait_recv()

    outs = pl.pallas_call(
        body, name=name,
        in_specs=[HBM] * nb + [SEM, SEM, ANY], out_specs=[HBM] * nb,
        out_shape=_hbm_like(list(srcs) + list(fresh)),
        input_output_aliases={i: i for i in range(nb)},
        compiler_params=pltpu.CompilerParams(has_side_effects=DATAFLOW_EFFECT),
    )(*_in_hbm(list(srcs) + list(fresh)), send, recv, after)
    return outs[:ns], outs[ns:]


class _Reduce:
    def __init__(self, place, core, shards, mom_m, mom_v):
        self.place, self.core = place, core
        self.shards, self.mom_m, self.mom_v = shards, mom_m, mom_v
        self.state = {}
        self.results = {}

    def add(self, group, grads, after):
        names = list(grads)
        g4s = [g.reshape((N_CHIPS, -1, g.shape[-1])) if g.ndim == 2 else g for g in grads.values()]
        fresh = [lax.empty((N_CHIPS, g.shape[1] // 2, g.shape[2]), BF16) for g in g4s]
        send, recv, g4s, fresh, token = _exchange_start(_sibling_halves_copies, len(names), g4s, fresh, after,
                                                        "pair_start_" + group)
        self.state[group] = (0, names, send, recv, g4s, fresh)
        return token

    def send(self, group, theirs, after):
        names, srcs = list(theirs), list(theirs.values())
        fresh = [lax.empty(s.shape, BF16) for s in srcs]
        send, recv, srcs, fresh, token = _exchange_start(_to_sibling_copies, len(names), srcs, fresh, after, "pair_start_" + group)
        self.state[group] = ("sent", names, send, recv, srcs, fresh)
        return token

    def received(self, group, after):
        stage, names, send, recv, srcs, fresh = self.state.pop(group)
        assert stage == "sent"
        _, got = _exchange_done(_to_sibling_copies, srcs, fresh, send, recv, after, "pair_done_" + group)
        return dict(zip(names, got))

    def add_parts(self, group, parts):
        names, srcs = list(parts), list(parts.values())
        fresh = [lax.empty((N_CHIPS - 1,) + p.shape[1:], BF16) for p in srcs]
        send, recv, srcs, fresh, token = _exchange_start(_chip_copies, 3 * len(names), srcs, fresh, self.core, "chips_start_" + group)
        self.state[group] = (1, names, send, recv, srcs, fresh)
        return token

    def step(self, group, after, count=None):
        stage, names, send, recv, srcs, fresh = self.state[group]
        if stage == 0:
            g4s, ras = _exchange_done(_sibling_halves_copies, srcs, fresh, send, recv, after, "pair_done_" + group)
            parts = _pair_sum(g4s, ras, self.core, "pair_sum_" + group)
            fresh = [lax.empty((N_CHIPS - 1,) + p.shape[1:], BF16) for p in parts]
            send, recv, parts, fresh, token = _exchange_start(_chip_copies, 3 * len(names), parts, fresh, self.core,
                                                              "chips_start_" + group)
            self.state[group] = (1, names, send, recv, parts, fresh)
            return token
        if stage == 1:
            parts, rcs = _exchange_done(_chip_copies, srcs, fresh, send, recv, after, "chips_done_" + group)
            token = None
            for call in _adamw_calls(names, self.shards):
                at = [names.index(n) for n in call]
                done = _adamw_own_half(*[[held[n] for n in call] for held in (self.shards, self.mom_m, self.mom_v)],
                                       [parts[i] for i in at], [rcs[i] for i in at], self.place,
                                       "adamw_own_" + (call[0] if len(call) == 1 else group), after=token)
                self.results.update(zip(call, done))
                token = done[-1][2]
            wholes = [self.results[n][0] for n in names]
            send, recv, wholes, _, token = _exchange_start(_join_copies, len(names), wholes, [], token, "join_start_" + group)
            self.state[group] = (2, names, send, recv, wholes, [])
            return token
        assert stage in (2, 3)
        if stage == 2:
            srcs, _ = _exchange_done(_join_copies, srcs, [], send, recv, after, "join_done_" + group)
            after = None
        token = after
        count = len(names) if count is None else count
        for call in _adamw_calls(names[:count], self.shards):
            at = [names.index(n) for n in call]
            done = _adamw_other_half(*[[held[n] for n in call] for held in (self.shards, self.mom_m, self.mom_v)],
                                     [srcs[i] for i in at], [self.results[n][1:] for n in call], self.place,
                                     "adamw_other_" + (call[0] if len(call) == 1 else group), after=token)
            self.results.update(zip(call, done))
            token = done[-1][1]
        if count < len(names):
            self.state[group] = (3, names[count:], None, None, srcs[count:], [])
        else:
            del self.state[group]
        return token


N_DEV = 8


def _to_all_copies(srcs, dsts, x, y, c):
    out = []
    for r in range(1, N_DEV):
        fx, fy, fc = (r >> 2) & 1, (r >> 1) & 1, r & 1
        out.append((srcs[0], dsts[0].at[r - 1], (x + fx - 2 * x * fx, y + fy - 2 * y * fy, c + fc - 2 * c * fc)))
    return out


def _all_reduce_small_start(v, after):
    slots = lax.empty((N_DEV - 1,) + v.shape, v.dtype)
    send, recv, (v,), (slots,), token = _exchange_start(_to_all_copies, N_DEV - 1, [v], [slots], after, "small_grads_start")
    return (send, recv, v, slots), token


def _all_reduce_small_done(started, after):
    send, recv, v, slots = started
    (v,), (slots,) = _exchange_done(_to_all_copies, [v], [slots], send, recv, after, "small_grads_done")

    def body(v_ref, slots_ref, o_ref):
        x, y, c = _mesh_pos()
        me = 4 * x + 2 * y + c
        acc = None
        for i in range(N_DEV):
            r = jnp.bitwise_xor(me, i)
            part = jnp.where(r == 0, v_ref[...], slots_ref[jnp.maximum(r - 1, 0)])
            acc = part if acc is None else acc + part
        o_ref[...] = acc

    vm = pl.BlockSpec(memory_space=pltpu.VMEM)
    return pl.pallas_call(body, name="small_grads_sum", in_specs=[vm, vm], out_specs=vm,
                          out_shape=jax.ShapeDtypeStruct(v.shape, v.dtype))(v, slots)


MATRICES = ("w_in", "w_attn_out", "w_conv_out", "w_o", "w_cq", "w_ckv", "w_co", "w_gate", "w_up", "w_down")
VECTORS = ("g_mix", "b_gate", "g_cross", "g_mem", "g_ffn", "g_final", "conv_w", "sink")
WEIGHT_ORDER = ("g_mix", "w_in", "sink", "conv_w", "b_gate", "w_attn_out", "w_conv_out", "w_o", "g_cross", "g_mem", "w_cq",
                "w_ckv", "w_co", "g_ffn", "w_gate", "w_up", "w_down", "g_final")
CONV_PAD_ROWS = 32
SMALL_ROWS = 8


def _pack(pieces):
    flat = jnp.concatenate([p.reshape(-1) for p in pieces])
    lane_group = SMALL_ROWS * 128
    total = -(-flat.shape[0] // lane_group) * lane_group
    flat = jnp.pad(flat, (0, total - flat.shape[0]))
    return flat.reshape(SMALL_ROWS, total // SMALL_ROWS), [p.size for p in pieces]


def _unpack(packed, pieces):
    flat = packed.reshape(-1)
    out, off = [], 0
    for p in pieces:
        out.append(flat[off:off + p.size].reshape(p.shape))
        off += p.size
    return out


def kernel(x, mem, g_mix, w_in, sink, conv_w, b_gate, w_attn_out, w_conv_out, w_o, g_cross, g_mem, w_cq, w_ckv, w_co, g_ffn, w_gate, w_up, w_down, g_final, loss_target, m_g_mix, m_w_in, m_sink, m_conv_w, m_b_gate, m_w_attn_out, m_w_conv_out, m_w_o, m_g_cross, m_g_mem, m_w_cq, m_w_ckv, m_w_co, m_g_ffn, m_w_gate, m_w_up, m_w_down, m_g_final, v_g_mix, v_w_in, v_sink, v_conv_w, v_b_gate, v_w_attn_out, v_w_conv_out, v_w_o, v_g_cross, v_g_mem, v_w_cq, v_w_ckv, v_w_co, v_g_ffn, v_w_gate, v_w_up, v_w_down, v_g_final):
    given = dict(g_mix=g_mix, w_in=w_in, sink=sink, conv_w=conv_w, b_gate=b_gate, w_attn_out=w_attn_out, w_conv_out=w_conv_out,
                 w_o=w_o, g_cross=g_cross, g_mem=g_mem, w_cq=w_cq, w_ckv=w_ckv, w_co=w_co, g_ffn=g_ffn, w_gate=w_gate, w_up=w_up,
                 w_down=w_down, g_final=g_final)
    mom_m = dict(g_mix=m_g_mix, w_in=m_w_in, sink=m_sink, conv_w=m_conv_w, b_gate=m_b_gate, w_attn_out=m_w_attn_out,
                 w_conv_out=m_w_conv_out, w_o=m_w_o, g_cross=m_g_cross, g_mem=m_g_mem, w_cq=m_w_cq, w_ckv=m_w_ckv, w_co=m_w_co,
                 g_ffn=m_g_ffn, w_gate=m_w_gate, w_up=m_w_up, w_down=m_w_down, g_final=m_g_final)
    mom_v = dict(g_mix=v_g_mix, w_in=v_w_in, sink=v_sink, conv_w=v_conv_w, b_gate=v_b_gate, w_attn_out=v_w_attn_out,
                 w_conv_out=v_w_conv_out, w_o=v_w_o, g_cross=v_g_cross, g_mem=v_g_mem, w_cq=v_w_cq, w_ckv=v_w_ckv, w_co=v_w_co,
                 g_ffn=v_g_ffn, w_gate=v_w_gate, w_up=v_w_up, w_down=v_w_down, g_final=v_g_final)
    xs, mems, target = x[0], mem[0], loss_target[0]
    d_model = xs.shape[1]
    chip = 2 * lax.axis_index("x") + lax.axis_index("y")
    core = jnp.reshape(lax.axis_index("c"), (1,)).astype(jnp.int32)
    place = jnp.stack([chip, lax.axis_index("c")]).astype(jnp.int32)

    shards = {n: given[n][0] for n in MATRICES}
    conv_cols = conv_w.shape[2]
    conv_pad = jnp.pad(conv_w[0], ((0, CONV_PAD_ROWS - conv_w.shape[1]), (0, 0)))
    fetch = _Gather(GATHER_GROUPS)
    first = {"w_in": _cast_to_slot(shards["w_in"], place, BF16, "to_slot_w_in"),
             "conv_w": _cast_to_slot(conv_pad, place, F32, "to_slot_conv_w")}
    fetch.put(first)
    tok = fetch.step("gather_start", [], [("direct", "in")])
    fetch.put({n: _cast_to_slot(shards[n], place, BF16, "to_slot_" + n, after=tok) for n in MATRICES if n != "w_in"})
    small = {n: given[n] for n in ("g_mix", "b_gate", "g_cross", "g_mem", "g_ffn")}
    small["g_final"] = g_final[None]
    small["sink"] = sink[0]

    reduce = _Reduce(place, core, shards, {n: mom_m[n][0] for n in MATRICES}, {n: mom_v[n][0] for n in MATRICES})
    sq, grad_x, small_grads = _local_step(xs, mems, target, small, fetch, reduce)

    loss_part = 0.5 * sq[0:1, 0:1] / d_model
    pieces = [small_grads[n] for n in VECTORS] + [loss_part]
    packed, _ = _pack(pieces)
    started, tok = _all_reduce_small_start(packed, core)
    tok = reduce.step("mid", tok)
    tok = reduce.step("in", tok)
    summed = _unpack(_all_reduce_small_done(started, tok), pieces)
    loss = summed[-1][0, 0]
    small_sum = dict(zip(VECTORS, summed[:-1]))
    small_sum["conv_w"] = lax.dynamic_slice_in_dim(small_sum["conv_w"], chip * conv_cols, conv_cols, axis=1)

    grad_out, delta, new_m, new_v = {}, {}, {}, {}
    like = [given[n] for n in VECTORS]
    pw, _ = _pack(like)
    pg, _ = _pack([small_sum[n] for n in VECTORS])
    pm, _ = _pack([mom_m[n] for n in VECTORS])
    pv, _ = _pack([mom_v[n] for n in VECTORS])
    _, pd, pnm, pnv = _adamw(pw, pg, pm, pv, "adamw_small")
    for n, g, d, nm, nv in zip(VECTORS, [small_sum[n] for n in VECTORS], _unpack(pd, like), _unpack(pnm, like), _unpack(pnv, like)):
        grad_out[n] = g.reshape(given[n].shape)
        delta[n], new_m[n], new_v[n] = d, nm, nv
    reduce.step("in", pd)
    for n in MATRICES:
        g, d, nm, nv = reduce.results[n]
        grad_out[n], delta[n], new_m[n], new_v[n] = g[None], d[None], nm[None], nv[None]

    return (loss, grad_x[None], *[grad_out[n] for n in WEIGHT_ORDER], *[delta[n] for n in WEIGHT_ORDER],
            *[new_m[n] for n in WEIGHT_ORDER], *[new_v[n] for n in WEIGHT_ORDER])
```

```python
import jax
import jax.numpy as jnp
from jax import lax
from jax.experimental import pallas as pl
from jax.experimental.pallas import tpu as pltpu

F32 = jnp.float32
BF16 = jnp.bfloat16
MESH = pl.DeviceIdType.MESH
ANY = pl.BlockSpec(memory_space=pl.ANY)

VMEM_LIMIT_BYTES = 56 * 1024 * 1024

N_CHIPS = 4
HEAD_DIM = 128
N_Q_HEADS = 8
N_KV_HEADS = 2
Q_GROUP = N_Q_HEADS // N_KV_HEADS
ATTN_WIDTH = N_Q_HEADS * HEAD_DIM
KV_WIDTH = N_KV_HEADS * HEAD_DIM
WINDOW = 128
BLOCK = 128
BAND = 3 * BLOCK
ROPE_THETA = 10000.0
CONV_WIDTH = 1024
MEM_HEADS = 4
MEM_WIDTH = MEM_HEADS * HEAD_DIM
RMS_EPS = 1e-6
NEG_INF = -1e30
ATTN_SCALE = HEAD_DIM ** -0.5

Q_OFF, K_OFF, V_OFF, CU_OFF, CB_OFF, CC_OFF, GL_OFF = 0, 1024, 1280, 1536, 2560, 3584, 4608

ADAM_LR = 0.001
ADAM_B1 = 0.9
ADAM_B2 = 0.999
ADAM_EPS = 1e-08
ADAM_WD = 0.01
ADAM_STEP = 10
ADAM_C1 = 1.0 - ADAM_B1 ** ADAM_STEP
ADAM_C2 = 1.0 - ADAM_B2 ** ADAM_STEP


def _params(n_grid_axes):
    return pltpu.CompilerParams(dimension_semantics=("arbitrary",) * n_grid_axes, vmem_limit_bytes=VMEM_LIMIT_BYTES)


BF16_SUBLANES = 16


def _row_tile(rows, want):
    if rows <= want:
        return rows
    for t in range(want, 0, -BF16_SUBLANES):
        if rows % t == 0:
            return t
    return rows


def _matmul(a, b, *, mode, tm, tn, out_dtypes, name, extras=(), epilogue=None, b_blocks=1, out_blocks=1, after=None):
    if mode == "tn":
        kdim, m = a.shape
    else:
        m, kdim = a.shape
    if b_blocks > 1:
        nb, brows, bcols = b.shape
        assert nb == b_blocks
        if mode == "nn":
            n = bcols * nb
            assert brows == kdim
        else:
            assert mode == "nt" and bcols * nb == kdim
            n = brows
    else:
        n = b.shape[0] if mode == "nt" else b.shape[1]
    tm, tn = min(tm, m), min(tn, n)
    tk = kdim
    assert m % tm == 0 and n % tn == 0, (name, m, n, tm, tn)
    n_extra, n_out = len(extras), len(out_dtypes)
    n_after = 0 if after is None else 1

    if mode == "tn":
        a_spec = pl.BlockSpec((tk, tm), lambda j, i, k: (k, i))
        dims = (((0,), (0,)), ((), ()))
    else:
        a_spec = pl.BlockSpec((tm, tk), lambda j, i, k: (i, k))
        dims = (((1,), (0,)), ((), ())) if mode == "nn" else (((1,), (1,)), ((), ()))

    if b_blocks > 1 and mode == "nn":
        per = b.shape[2] // tn
        assert b.shape[2] % tn == 0
        b_spec = pl.BlockSpec((None, tk, tn), lambda j, i, k: (j // per, k, j % per))
    elif b_blocks > 1:
        b_spec = pl.BlockSpec((b_blocks, tn, b.shape[2]), lambda j, i, k: (0, j, 0))
    elif mode == "nt":
        b_spec = pl.BlockSpec((tn, tk), lambda j, i, k: (j, k))
    else:
        b_spec = pl.BlockSpec((tk, tn), lambda j, i, k: (k, j))

    tile_spec = pl.BlockSpec((tm, tn), lambda j, i, k: (i, j))
    if out_blocks > 1:
        ncols = n // out_blocks
        assert ncols % tn == 0
        oper = ncols // tn
        out_spec = pl.BlockSpec((None, tm, tn), lambda j, i, k: (j // oper, i, j % oper))
        out_shape = [jax.ShapeDtypeStruct((out_blocks, m, ncols), dt) for dt in out_dtypes]
    else:
        out_spec = tile_spec
        out_shape = [jax.ShapeDtypeStruct((m, n), dt) for dt in out_dtypes]

    def body(a_ref, b_ref, *rest):
        extra_refs = rest[:n_extra]
        out_refs = rest[n_extra + n_after:n_extra + n_after + n_out]
        if mode == "nt" and b_blocks > 1:
            cs = b.shape[2]
            acc = None
            for jb in range(b_blocks):
                prod = lax.dot_general(a_ref[:, jb * cs:(jb + 1) * cs].astype(BF16), b_ref[jb].astype(BF16), dims,
                                       preferred_element_type=F32)
                acc = prod if acc is None else acc + prod
        else:
            acc = lax.dot_general(a_ref[...].astype(BF16), b_ref[...].astype(BF16), dims, preferred_element_type=F32)
        tiles = (acc,) if epilogue is None else epilogue(acc, *[r[...] for r in extra_refs])
        for o_ref, t in zip(out_refs, tiles, strict=True):
            o_ref[...] = t.astype(o_ref.dtype)

    outs = pl.pallas_call(
        body,
        name=name,
        grid=(n // tn, m // tm, 1),
        in_specs=[a_spec, b_spec] + [tile_spec] * n_extra + [ANY] * n_after,
        out_specs=[out_spec] * n_out,
        out_shape=out_shape,
        compiler_params=_params(3),
    )(a, b, *extras, *([] if after is None else [after]))
    return outs[0] if n_out == 1 else outs


def _add_residual(acc, res):
    return (acc + res,)


def _matmul_column_blocks(a, b4, blocks, out, *, tm, name, after=None):
    m, kdim = a.shape
    nb, _, cols = b4.shape
    tm = min(tm, m)
    assert m % tm == 0

    def body(j_ref, a_ref, b_ref, *rest):
        rest[-1][...] = jnp.dot(a_ref[...], b_ref[...], preferred_element_type=F32)

    extra = ([] if out is None else [out]) + ([] if after is None else [after])
    n_blocks = blocks.shape[0]
    return pl.pallas_call(
        body, name=name,
        grid_spec=pltpu.PrefetchScalarGridSpec(
            num_scalar_prefetch=1, grid=(n_blocks, m // tm),
            in_specs=[pl.BlockSpec((tm, kdim), lambda j, i, blk: (i, 0)),
                      pl.BlockSpec((None, kdim, cols), lambda j, i, blk: (blk[j], 0, 0))] + [ANY] * len(extra),
            out_specs=pl.BlockSpec((tm, cols), lambda j, i, blk: (i, blk[j]))),
        out_shape=jax.ShapeDtypeStruct((m, nb * cols), F32),
        input_output_aliases={} if out is None else {3: 0},
        compiler_params=_params(2),
    )(blocks, a, b4, *extra)


def _wgrad_half(a, b, core, *, theirs, row_sharded, tm, tn, name, add=None, after=None):
    kdim, m = a.shape
    n = b.shape[1]
    rs, cs = (m // N_CHIPS, n) if row_sharded else (m, n // N_CHIPS)
    rh = rs // 2
    tm, tn = min(tm, rh), min(tn, cs)
    assert rh % tm == 0 and cs % tn == 0, (name, rh, cs, tm, tn)
    mh, per = rh // tm, cs // tn
    has_add = add is not None

    def half(c):
        return 1 - c[0] if theirs else c[0]

    if row_sharded:
        grid = (n // tn, N_CHIPS * mh)
        a_spec = pl.BlockSpec((kdim, tm), lambda j, r, c: (0, ((r // mh) * 2 + half(c)) * mh + r % mh))
        o_spec = pl.BlockSpec((None, tm, tn), lambda j, r, c: (r // mh, r % mh, j))
    else:
        grid = (n // tn, mh)
        a_spec = pl.BlockSpec((kdim, tm), lambda j, r, c: (0, half(c) * mh + r))
        o_spec = pl.BlockSpec((None, tm, tn), lambda j, r, c: (j // per, r, j % per))
    b_spec = pl.BlockSpec((kdim, tn), lambda j, r, c: (0, j))

    def body(c_ref, a_ref, b_ref, *rest):
        o_ref = rest[-1]
        acc = lax.dot_general(a_ref[...].astype(BF16), b_ref[...].astype(BF16), (((0,), (0,)), ((), ())),
                              preferred_element_type=F32)
        if has_add:
            acc = acc + rest[0][...].astype(F32)
        o_ref[...] = acc.astype(BF16)

    operands = [a, b] + ([add] if has_add else []) + ([] if after is None else [after])
    return pl.pallas_call(
        body, name=name,
        grid_spec=pltpu.PrefetchScalarGridSpec(
            num_scalar_prefetch=1, grid=grid,
            in_specs=[a_spec, b_spec] + ([o_spec] if has_add else []) + ([] if after is None else [ANY]),
            out_specs=o_spec),
        out_shape=jax.ShapeDtypeStruct((N_CHIPS, rh, cs), BF16),
        compiler_params=_params(2),
    )(core, *operands)


def _rstd(x):
    return lax.rsqrt(jnp.mean(x * x, axis=-1, keepdims=True) + RMS_EPS)


def _rmsnorm(x, g, name):
    s, d = x.shape
    tr = _row_tile(s, 512)

    def body(x_ref, g_ref, o_ref):
        xv = x_ref[...]
        o_ref[...] = (xv * _rstd(xv) * g_ref[...]).astype(BF16)

    return pl.pallas_call(
        body, name=name, grid=(s // tr,),
        in_specs=[pl.BlockSpec((tr, d), lambda i: (i, 0)), pl.BlockSpec((1, d), lambda i: (0, 0))],
        out_specs=pl.BlockSpec((tr, d), lambda i: (i, 0)),
        out_shape=jax.ShapeDtypeStruct((s, d), BF16),
        compiler_params=_params(1),
    )(x, g)


def _rmsnorm_bwd(dh, x, g, dres, name):
    s, d = x.shape
    tr = _row_tile(s, 512)
    has_res = dres is not None

    def body(*refs):
        if has_res:
            dh_ref, x_ref, g_ref, res_ref, dx_ref, dxb_ref, dg_ref = refs
        else:
            dh_ref, x_ref, g_ref, dx_ref, dxb_ref, dg_ref = refs
        xv = x_ref[...]
        dhv = dh_ref[...].astype(F32)
        r = _rstd(xv)
        xn = xv * r
        dhg = dhv * g_ref[...]
        dx = r * (dhg - xn * jnp.mean(dhg * xn, axis=-1, keepdims=True))
        if has_res:
            dx = dx + res_ref[...]
        dx_ref[...] = dx
        dxb_ref[...] = dx.astype(BF16)
        part = jnp.sum(dhv * xn, axis=0, keepdims=True)

        @pl.when(pl.program_id(0) == 0)
        def _():
            dg_ref[...] = part

        @pl.when(pl.program_id(0) > 0)
        def _():
            dg_ref[...] += part

    row = pl.BlockSpec((tr, d), lambda i: (i, 0))
    vec = pl.BlockSpec((1, d), lambda i: (0, 0))
    return pl.pallas_call(
        body, name=name, grid=(s // tr,),
        in_specs=[row, row, vec] + ([row] if has_res else []),
        out_specs=[row, row, vec],
        out_shape=[jax.ShapeDtypeStruct((s, d), F32), jax.ShapeDtypeStruct((s, d), BF16), jax.ShapeDtypeStruct((1, d), F32)],
        compiler_params=_params(1),
    )(*([dh, x, g] + ([dres] if has_res else [])))


def _loss_head(x3, g, target):
    s, d = x3.shape
    tr = _row_tile(s, 512)

    def body(x_ref, g_ref, t_ref, dx_ref, dxb_ref, sq_ref, dg_ref):
        xv = x_ref[...]
        gv = g_ref[...]
        r = _rstd(xv)
        xn = xv * r
        err = xn * gv - t_ref[...]
        dy = err * (1.0 / d)
        dyg = dy * gv
        dx = r * (dyg - xn * jnp.mean(dyg * xn, axis=-1, keepdims=True))
        dx_ref[...] = dx
        dxb_ref[...] = dx.astype(BF16)
        sq = jnp.sum(jnp.sum(err * err, axis=1, keepdims=True), axis=0, keepdims=True)
        sq = jnp.broadcast_to(sq, (1, 128))
        part = jnp.sum(dy * xn, axis=0, keepdims=True)

        @pl.when(pl.program_id(0) == 0)
        def _():
            sq_ref[...] = sq
            dg_ref[...] = part

        @pl.when(pl.program_id(0) > 0)
        def _():
            sq_ref[...] += sq
            dg_ref[...] += part

    row = pl.BlockSpec((tr, d), lambda i: (i, 0))
    vec = pl.BlockSpec((1, d), lambda i: (0, 0))
    return pl.pallas_call(
        body, name="loss_head", grid=(s // tr,),
        in_specs=[row, vec, row],
        out_specs=[row, row, pl.BlockSpec((1, 128), lambda i: (0, 0)), vec],
        out_shape=[jax.ShapeDtypeStruct((s, d), F32), jax.ShapeDtypeStruct((s, d), BF16),
                   jax.ShapeDtypeStruct((1, 128), F32), jax.ShapeDtypeStruct((1, d), F32)],
        compiler_params=_params(1),
    )(x3, g, target)


def _rope_tables(s):
    inv = 1.0 / (ROPE_THETA ** (jnp.arange(0, HEAD_DIM, 2, dtype=F32) / HEAD_DIM))
    ang = jnp.arange(s, dtype=F32)[:, None] * inv[None, :]
    cos, sin = jnp.cos(ang), jnp.sin(ang)
    return jnp.concatenate([cos, cos], axis=1), jnp.concatenate([-sin, sin], axis=1)


def _swap_halves(t):
    return pltpu.roll(t, HEAD_DIM // 2, 1)


def _rope_fwd(z, cos_t, sin_t):
    s = z.shape[0]
    tr = _row_tile(s, 256)

    def body(zq_ref, zk_ref, zv_ref, c_ref, s_ref, q_ref, k_ref, v_ref):
        c, sn = c_ref[...], s_ref[...]
        for hd in range(N_Q_HEADS):
            cols = slice(hd * HEAD_DIM, (hd + 1) * HEAD_DIM)
            t = zq_ref[:, cols]
            q_ref[:, cols] = (t * c + _swap_halves(t) * sn).astype(BF16)
        for hd in range(N_KV_HEADS):
            cols = slice(hd * HEAD_DIM, (hd + 1) * HEAD_DIM)
            t = zk_ref[:, cols]
            k_ref[:, cols] = (t * c + _swap_halves(t) * sn).astype(BF16)
        v_ref[...] = zv_ref[...].astype(BF16)

    tab = pl.BlockSpec((tr, HEAD_DIM), lambda i: (i, 0))
    return pl.pallas_call(
        body, name="rope_fwd", grid=(s // tr,),
        in_specs=[pl.BlockSpec((tr, ATTN_WIDTH), lambda i: (i, Q_OFF // ATTN_WIDTH)),
                  pl.BlockSpec((tr, KV_WIDTH), lambda i: (i, K_OFF // KV_WIDTH)),
                  pl.BlockSpec((tr, KV_WIDTH), lambda i: (i, V_OFF // KV_WIDTH)), tab, tab],
        out_specs=[pl.BlockSpec((tr, ATTN_WIDTH), lambda i: (i, 0)), pl.BlockSpec((tr, KV_WIDTH), lambda i: (i, 0)),
                   pl.BlockSpec((tr, KV_WIDTH), lambda i: (i, 0))],
        out_shape=[jax.ShapeDtypeStruct((s, ATTN_WIDTH), BF16), jax.ShapeDtypeStruct((s, KV_WIDTH), BF16),
                   jax.ShapeDtypeStruct((s, KV_WIDTH), BF16)],
        compiler_params=_params(1),
    )(z, z, z, cos_t, sin_t)


def _rope_bwd(dq_rot, dk_rot, dv, cos_t, sin_t, dz):
    s = dq_rot.shape[0]
    tr = _row_tile(s, 256)
    qkv_width = V_OFF + KV_WIDTH

    def body(dq_ref, dk_ref, dv_ref, c_ref, s_ref, dz_in_ref, o_ref):
        c, sn = c_ref[...], s_ref[...]
        for hd in range(N_Q_HEADS):
            t = dq_ref[:, hd * HEAD_DIM:(hd + 1) * HEAD_DIM]
            o_ref[:, Q_OFF + hd * HEAD_DIM:Q_OFF + (hd + 1) * HEAD_DIM] = (t * c + _swap_halves(t * sn)).astype(BF16)
        for hd in range(N_KV_HEADS):
            t = dk_ref[:, hd * HEAD_DIM:(hd + 1) * HEAD_DIM]
            o_ref[:, K_OFF + hd * HEAD_DIM:K_OFF + (hd + 1) * HEAD_DIM] = (t * c + _swap_halves(t * sn)).astype(BF16)
        o_ref[:, V_OFF:V_OFF + KV_WIDTH] = dv_ref[...].astype(BF16)

    tab = pl.BlockSpec((tr, HEAD_DIM), lambda i: (i, 0))
    wide = pl.BlockSpec((tr, ATTN_WIDTH), lambda i: (i, 0))
    narrow = pl.BlockSpec((tr, KV_WIDTH), lambda i: (i, 0))
    return pl.pallas_call(
        body, name="rope_bwd", grid=(s // tr,),
        in_specs=[wide, narrow, narrow, tab, tab, ANY],
        out_specs=pl.BlockSpec((tr, qkv_width), lambda i: (i, 0)),
        out_shape=jax.ShapeDtypeStruct(dz.shape, dz.dtype),
        input_output_aliases={5: 0},
        compiler_params=_params(1),
    )(dq_rot, dk_rot, dv, cos_t, sin_t, dz)


def _swa_band(i, s):
    return pl.multiple_of(jnp.clip((i - 1) * BLOCK, 0, s - BAND), BLOCK)


SWA_HEADS_PER_PASS = Q_GROUP


def _swa_probs(q_ref, k_ref, sink_ref, heads, start, valid):
    kv = heads[0] // Q_GROUP
    cols = slice(kv * HEAD_DIM, (kv + 1) * HEAD_DIM)
    kb = k_ref[pl.ds(start, BAND), cols]
    qg = jnp.concatenate([q_ref[:, hd * HEAD_DIM:(hd + 1) * HEAD_DIM] for hd in heads], axis=0)
    sc = lax.dot_general(qg, kb, (((1,), (1,)), ((), ())), preferred_element_type=F32) * ATTN_SCALE
    sc = jnp.where(valid, sc, NEG_INF)
    sk = jnp.concatenate([jnp.full((BLOCK, 1), sink_ref[hd], F32) for hd in heads], axis=0)
    mx = jnp.maximum(jnp.max(sc, axis=1, keepdims=True), sk)
    e = jnp.exp(sc - mx)
    es = jnp.exp(sk - mx)
    inv = 1.0 / (jnp.sum(e, axis=1, keepdims=True) + es)
    return qg, kb, e * inv, es * inv


def _swa_head_passes():
    return [list(range(h0, h0 + SWA_HEADS_PER_PASS)) for h0 in range(0, N_Q_HEADS, SWA_HEADS_PER_PASS)]


def _swa_valid(i, start):
    q_pos = i * BLOCK + lax.broadcasted_iota(jnp.int32, (BLOCK, 1), 0)
    q_pos = jnp.concatenate([q_pos] * SWA_HEADS_PER_PASS, axis=0)
    k_pos = start + lax.broadcasted_iota(jnp.int32, (1, BAND), 1)
    return jnp.abs(k_pos - q_pos) <= WINDOW


def _swa_fwd(q, k, v, sink):
    s = q.shape[0]
    assert s % BLOCK == 0 and s >= BAND

    def body(sink_ref, q_ref, k_ref, v_ref, o_ref):
        i = pl.program_id(0)
        start = _swa_band(i, s)
        valid = _swa_valid(i, start)
        for heads in _swa_head_passes():
            kv = heads[0] // Q_GROUP
            _, _, p, _ = _swa_probs(q_ref, k_ref, sink_ref, heads, start, valid)
            vb = v_ref[pl.ds(start, BAND), kv * HEAD_DIM:(kv + 1) * HEAD_DIM]
            o = jnp.dot(p.astype(BF16), vb, preferred_element_type=F32)
            for g, hd in enumerate(heads):
                o_ref[:, hd * HEAD_DIM:(hd + 1) * HEAD_DIM] = o[g * BLOCK:(g + 1) * BLOCK].astype(BF16)

    whole = pl.BlockSpec((s, KV_WIDTH), lambda i: (0, 0))
    blk = pl.BlockSpec((BLOCK, ATTN_WIDTH), lambda i: (i, 0))
    return pl.pallas_call(
        body, name="swa_fwd", grid=(s // BLOCK,),
        in_specs=[pl.BlockSpec(memory_space=pltpu.SMEM), blk, whole, whole],
        out_specs=blk,
        out_shape=jax.ShapeDtypeStruct((s, ATTN_WIDTH), BF16),
        compiler_params=_params(1),
    )(sink, q, k, v)


def _swa_bwd(q, k, v, d_out, sink):
    s = q.shape[0]

    def body(sink_ref, q_ref, k_ref, v_ref, do_ref, dq_ref, dk_ref, dv_ref, dsink_ref):
        i = pl.program_id(0)

        @pl.when(i == 0)
        def _():
            dk_ref[...] = jnp.zeros_like(dk_ref)
            dv_ref[...] = jnp.zeros_like(dv_ref)
            dsink_ref[...] = jnp.zeros_like(dsink_ref)

        start = _swa_band(i, s)
        valid = _swa_valid(i, start)
        for heads in _swa_head_passes():
            kv = heads[0] // Q_GROUP
            cols = slice(kv * HEAD_DIM, (kv + 1) * HEAD_DIM)
            qg, kb, p, p_sink = _swa_probs(q_ref, k_ref, sink_ref, heads, start, valid)
            vb = v_ref[pl.ds(start, BAND), cols]
            dog = jnp.concatenate([do_ref[:, hd * HEAD_DIM:(hd + 1) * HEAD_DIM] for hd in heads], axis=0)
            dp = lax.dot_general(dog, vb, (((1,), (1,)), ((), ())), preferred_element_type=F32)
            delta = jnp.sum(p * dp, axis=1, keepdims=True)
            ds = (p * (dp - delta) * ATTN_SCALE).astype(BF16)
            dqg = jnp.dot(ds, kb, preferred_element_type=F32)
            dk_ref[pl.ds(start, BAND), cols] += lax.dot_general(ds, qg, (((0,), (0,)), ((), ())), preferred_element_type=F32)
            dv_ref[pl.ds(start, BAND), cols] += lax.dot_general(p.astype(BF16), dog, (((0,), (0,)), ((), ())),
                                                                 preferred_element_type=F32)
            dsk = p_sink * delta
            for g, hd in enumerate(heads):
                dq_ref[:, hd * HEAD_DIM:(hd + 1) * HEAD_DIM] = dqg[g * BLOCK:(g + 1) * BLOCK]
                tot = jnp.sum(dsk[g * BLOCK:(g + 1) * BLOCK], axis=0, keepdims=True)
                dsink_ref[hd:hd + 1, :] -= jnp.broadcast_to(tot, (1, 128))

    whole = pl.BlockSpec((s, KV_WIDTH), lambda i: (0, 0))
    blk = pl.BlockSpec((BLOCK, ATTN_WIDTH), lambda i: (i, 0))
    return pl.pallas_call(
        body, name="swa_bwd", grid=(s // BLOCK,),
        in_specs=[pl.BlockSpec(memory_space=pltpu.SMEM), blk, whole, whole, blk],
        out_specs=[blk, whole, whole, pl.BlockSpec((N_Q_HEADS, 128), lambda i: (0, 0))],
        out_shape=[jax.ShapeDtypeStruct((s, ATTN_WIDTH), F32), jax.ShapeDtypeStruct((s, KV_WIDTH), F32),
                   jax.ShapeDtypeStruct((s, KV_WIDTH), F32), jax.ShapeDtypeStruct((N_Q_HEADS, 128), F32)],
        compiler_params=_params(1),
    )(sink, q, k, v, d_out)


CONV_CHUNK = 256


def _shift_rows(t, rows, down):
    n = t.shape[0]
    rolled = pltpu.roll(t, 1 if down else n - 1, 0)
    edge = 0 if down else n - 1
    return jnp.where(rows == edge, 0.0, rolled)


def _conv_specs(s):
    def z_spec(off):
        return pl.BlockSpec((s, CONV_CHUNK), lambda j, off=off: (0, off // CONV_CHUNK + j))
    chunk = pl.BlockSpec((s, CONV_CHUNK), lambda j: (0, j))
    w_spec = pl.BlockSpec((3, CONV_CHUNK), lambda j: (0, j))
    return z_spec(CU_OFF), z_spec(CB_OFF), z_spec(CC_OFF), chunk, w_spec


def _conv_fwd(z, conv_w):
    s = z.shape[0]
    cu_spec, cb_spec, cc_spec, chunk, w_spec = _conv_specs(s)

    def body(cu_ref, cb_ref, cc_ref, w_ref, o_ref):
        rows = lax.broadcasted_iota(jnp.int32, (s, 1), 0)
        t = cc_ref[...] * cu_ref[...]
        c3 = _shift_rows(t, rows, True) * w_ref[0:1, :] + t * w_ref[1:2, :] + _shift_rows(t, rows, False) * w_ref[2:3, :]
        o_ref[...] = (cb_ref[...] * c3).astype(BF16)

    return pl.pallas_call(
        body, name="conv_fwd", grid=(CONV_WIDTH // CONV_CHUNK,),
        in_specs=[cu_spec, cb_spec, cc_spec, w_spec],
        out_specs=chunk,
        out_shape=jax.ShapeDtypeStruct((s, CONV_WIDTH), BF16),
        compiler_params=_params(1),
    )(z, z, z, conv_w)


def _conv_bwd(z, conv_w, d_co, dz):
    s = z.shape[0]
    cu_spec, cb_spec, cc_spec, chunk, w_spec = _conv_specs(s)
    n_chunks = CONV_WIDTH // CONV_CHUNK
    offsets = (CU_OFF, CB_OFF, CC_OFF)

    def body(cu_ref, cb_ref, cc_ref, w_ref, d_ref, dz_in_ref, dz_ref, dw_ref, buf, sems):
        j = pl.program_id(0)

        def copies(j_at):
            return [pltpu.make_async_copy(buf.at[h], dz_ref.at[:, pl.ds(off + j_at * CONV_CHUNK, CONV_CHUNK)], sems.at[h])
                    for h, off in enumerate(offsets)]

        rows = lax.broadcasted_iota(jnp.int32, (s, 1), 0)
        cu, cc = cu_ref[...], cc_ref[...]
        t = cc * cu
        t_dn, t_up = _shift_rows(t, rows, True), _shift_rows(t, rows, False)
        c3 = t_dn * w_ref[0:1, :] + t * w_ref[1:2, :] + t_up * w_ref[2:3, :]
        d = d_ref[...]
        dc3 = d * cb_ref[...]
        dw_ref[0:1, :] = jnp.sum(dc3 * t_dn, axis=0, keepdims=True)
        dw_ref[1:2, :] = jnp.sum(dc3 * t, axis=0, keepdims=True)
        dw_ref[2:3, :] = jnp.sum(dc3 * t_up, axis=0, keepdims=True)
        dt = _shift_rows(dc3, rows, False) * w_ref[0:1, :] + dc3 * w_ref[1:2, :] + _shift_rows(dc3, rows, True) * w_ref[2:3, :]

        @pl.when(j > 0)
        def _():
            for cp in copies(j):
                cp.wait()

        buf[0] = (dt * cc).astype(BF16)
        buf[1] = (d * c3).astype(BF16)
        buf[2] = (dt * cu).astype(BF16)
        for cp in copies(j):
            cp.start()

        @pl.when(j == n_chunks - 1)
        def _():
            for cp in copies(j):
                cp.wait()

    return pl.pallas_call(
        body, name="conv_bwd", grid=(n_chunks,),
        in_specs=[cu_spec, cb_spec, cc_spec, w_spec, chunk, ANY],
        out_specs=[ANY, w_spec],
        out_shape=[jax.ShapeDtypeStruct(dz.shape, dz.dtype), jax.ShapeDtypeStruct((3, CONV_WIDTH), F32)],
        input_output_aliases={5: 0},
        scratch_shapes=[pltpu.VMEM((3, s, CONV_CHUNK), BF16), pltpu.SemaphoreType.DMA((3,))],
        compiler_params=_params(1),
    )(z, z, z, conv_w, d_co, dz)


GATE_CHUNK = 512


def _gate_specs(s, d, tr):
    n_chunks = d // GATE_CHUNK
    za = pl.BlockSpec((tr, GATE_CHUNK), lambda j, i: (i, GL_OFF // GATE_CHUNK + j))
    zc = pl.BlockSpec((tr, GATE_CHUNK), lambda j, i: (i, GL_OFF // GATE_CHUNK + n_chunks + j))
    ba = pl.BlockSpec((1, GATE_CHUNK), lambda j, i: (0, j))
    bc = pl.BlockSpec((1, GATE_CHUNK), lambda j, i: (0, n_chunks + j))
    tile = pl.BlockSpec((tr, GATE_CHUNK), lambda j, i: (i, j))
    return za, zc, ba, bc, tile


def _gate_fwd(z, b_gate, ya, yc):
    s, d = ya.shape
    tr = _row_tile(s, 512)
    za, zc, ba, bc, tile = _gate_specs(s, d, tr)

    def body(za_ref, zc_ref, ba_ref, bc_ref, ya_ref, yc_ref, o_ref):
        ga = jax.nn.sigmoid(za_ref[...] + ba_ref[...])
        gc = jax.nn.sigmoid(zc_ref[...] + bc_ref[...])
        o_ref[...] = (ga * ya_ref[...] + gc * yc_ref[...]).astype(BF16)

    return pl.pallas_call(
        body, name="gate_fwd", grid=(d // GATE_CHUNK, s // tr),
        in_specs=[za, zc, ba, bc, tile, tile],
        out_specs=tile,
        out_shape=jax.ShapeDtypeStruct((s, d), BF16),
        compiler_params=_params(2),
    )(z, z, b_gate, b_gate, ya, yc)


def _gate_bwd(z, b_gate, ya, yc, dmix):
    s, d = ya.shape
    tr = _row_tile(s, 512)
    za, zc, ba, bc, tile = _gate_specs(s, d, tr)
    vec = pl.BlockSpec((1, GATE_CHUNK), lambda j, i: (0, j))
    n_rows = s // tr
    in_width = z.shape[1]

    def body(za_ref, zc_ref, ba_ref, bc_ref, ya_ref, yc_ref, dm_ref, dya_ref, dyc_ref, dz_ref, dba_ref, dbc_ref, buf, sems):
        j, i = pl.program_id(0), pl.program_id(1)

        def copies(j_at, i_at):
            rows = pl.ds(i_at * tr, tr)
            return [pltpu.make_async_copy(buf.at[h], dz_ref.at[rows, pl.ds(GL_OFF + h * d + j_at * GATE_CHUNK, GATE_CHUNK)],
                                          sems.at[h]) for h in range(2)]

        ga = jax.nn.sigmoid(za_ref[...] + ba_ref[...])
        gc = jax.nn.sigmoid(zc_ref[...] + bc_ref[...])
        dm = dm_ref[...]
        dya_ref[...] = (dm * ga).astype(BF16)
        dyc_ref[...] = (dm * gc).astype(BF16)
        dla = dm * ya_ref[...] * ga * (1.0 - ga)
        dlc = dm * yc_ref[...] * gc * (1.0 - gc)

        @pl.when(j * n_rows + i > 0)
        def _():
            for cp in copies(j, i):
                cp.wait()

        buf[0] = dla.astype(BF16)
        buf[1] = dlc.astype(BF16)
        for cp in copies(j, i):
            cp.start()

        @pl.when((j == d // GATE_CHUNK - 1) & (i == n_rows - 1))
        def _():
            for cp in copies(j, i):
                cp.wait()

        pa = jnp.sum(dla, axis=0, keepdims=True)
        pc = jnp.sum(dlc, axis=0, keepdims=True)

        @pl.when(i == 0)
        def _():
            dba_ref[...] = pa
            dbc_ref[...] = pc

        @pl.when(i > 0)
        def _():
            dba_ref[...] += pa
            dbc_ref[...] += pc

    big = jax.ShapeDtypeStruct((s, d), BF16)
    small = jax.ShapeDtypeStruct((1, d), F32)
    return pl.pallas_call(
        body, name="gate_bwd", grid=(d // GATE_CHUNK, n_rows),
        in_specs=[za, zc, ba, bc, tile, tile, tile],
        out_specs=[tile, tile, ANY, vec, vec],
        out_shape=[big, big, jax.ShapeDtypeStruct((s, in_width), BF16), small, small],
        scratch_shapes=[pltpu.VMEM((2, tr, GATE_CHUNK), BF16), pltpu.SemaphoreType.DMA((2,))],
        compiler_params=_params(2),
    )(z, z, b_gate, b_gate, ya, yc, dmix)


def _cross_probs(q_ref, kv_ref, hd):
    cols = slice(hd * HEAD_DIM, (hd + 1) * HEAD_DIM)
    qh = q_ref[:, cols]
    kh = kv_ref[:, cols]
    sc = lax.dot_general(qh, kh, (((1,), (1,)), ((), ())), preferred_element_type=F32) * ATTN_SCALE
    e = jnp.exp(sc - jnp.max(sc, axis=1, keepdims=True))
    return qh, kh, e * (1.0 / jnp.sum(e, axis=1, keepdims=True))


def _cross_fwd(qc, kvc):
    s = qc.shape[0]
    n_mem = kvc.shape[0]
    tq = _row_tile(s, 256)

    def body(q_ref, kv_ref, o_ref):
        for hd in range(MEM_HEADS):
            _, _, p = _cross_probs(q_ref, kv_ref, hd)
            vh = kv_ref[:, MEM_WIDTH + hd * HEAD_DIM:MEM_WIDTH + (hd + 1) * HEAD_DIM]
            o_ref[:, hd * HEAD_DIM:(hd + 1) * HEAD_DIM] = jnp.dot(p.astype(BF16), vh, preferred_element_type=F32).astype(BF16)

    return pl.pallas_call(
        body, name="cross_fwd", grid=(s // tq,),
        in_specs=[pl.BlockSpec((tq, MEM_WIDTH), lambda i: (i, 0)), pl.BlockSpec((n_mem, 2 * MEM_WIDTH), lambda i: (0, 0))],
        out_specs=pl.BlockSpec((tq, MEM_WIDTH), lambda i: (i, 0)),
        out_shape=jax.ShapeDtypeStruct((s, MEM_WIDTH), BF16),
        compiler_params=_params(1),
    )(qc, kvc)


def _cross_bwd(qc, kvc, d_out):
    s = qc.shape[0]
    n_mem = kvc.shape[0]
    tq = _row_tile(s, 256)

    def body(q_ref, kv_ref, do_ref, dq_ref, dkv_ref):
        @pl.when(pl.program_id(0) == 0)
        def _():
            dkv_ref[...] = jnp.zeros_like(dkv_ref)

        for hd in range(MEM_HEADS):
            cols = slice(hd * HEAD_DIM, (hd + 1) * HEAD_DIM)
            vcols = slice(MEM_WIDTH + hd * HEAD_DIM, MEM_WIDTH + (hd + 1) * HEAD_DIM)
            qh, kh, p = _cross_probs(q_ref, kv_ref, hd)
            doh = do_ref[:, cols]
            dp = lax.dot_general(doh, kv_ref[:, vcols], (((1,), (1,)), ((), ())), preferred_element_type=F32)
            ds = (p * (dp - jnp.sum(p * dp, axis=1, keepdims=True)) * ATTN_SCALE).astype(BF16)
            dq_ref[:, cols] = jnp.dot(ds, kh, preferred_element_type=F32).astype(BF16)
            dkv_ref[:, cols] += lax.dot_general(ds, qh, (((0,), (0,)), ((), ())), preferred_element_type=F32)
            dkv_ref[:, vcols] += lax.dot_general(p.astype(BF16), doh, (((0,), (0,)), ((), ())), preferred_element_type=F32)

    qspec = pl.BlockSpec((tq, MEM_WIDTH), lambda i: (i, 0))
    kvspec = pl.BlockSpec((n_mem, 2 * MEM_WIDTH), lambda i: (0, 0))
    return pl.pallas_call(
        body, name="cross_bwd", grid=(s // tq,),
        in_specs=[qspec, kvspec, qspec],
        out_specs=[qspec, kvspec],
        out_shape=[jax.ShapeDtypeStruct((s, MEM_WIDTH), BF16), jax.ShapeDtypeStruct((n_mem, 2 * MEM_WIDTH), F32)],
        compiler_params=_params(1),
    )(qc, kvc, d_out)


def _swiglu_fwd(up, gate):
    sg = jax.nn.sigmoid(gate)
    silu = gate * sg
    return silu * up, up * (sg * (1.0 + gate * (1.0 - sg))), silu


def _swiglu_bwd(d_act, dact_dgate, dact_dup):
    return d_act * dact_dgate.astype(F32), d_act * dact_dup.astype(F32)


GATHER_GROUPS = {"in": ("w_in", "conv_w"), "mid": ("w_attn_out", "w_conv_out", "w_o", "w_cq", "w_ckv", "w_co"),
                 "gate": ("w_gate",), "up": ("w_up",), "down": ("w_down",)}


def _local_step(xs, mems, target, small, fetch, reduce):
    s, d = xs.shape
    w4 = {}
    cos_t, sin_t = _rope_tables(s)

    def near(group, done, then, after):
        waits = [("direct", group)] + ([("pass_near", done), ("pass_far", done)] if done else [])
        starts = [("forward", group), ("pass_near", group)] + [("direct", g) for g in then]
        tok = fetch.step("gather_near_" + group, waits, starts, after)
        if done:
            w4.update(fetch.arrays(done))
        return tok

    def far(group, then, after):
        return fetch.step("gather_far_" + group, [("forward", group)], [("pass_far", group)] + [("direct", g) for g in then], after)

    def last(group, after):
        tok = fetch.step("gather_done_" + group, [("pass_near", group), ("pass_far", group)], [], after)
        w4.update(fetch.arrays(group))
        return tok

    h = _rmsnorm(xs, small["g_mix"], "norm_mix")
    slots_filled = [a for g in ("gate", "up", "down") for a in fetch.arrays(g).values()]
    chip_x, chip_y = reduce.place[0] // 2, reduce.place[0] % 2
    own_block = jnp.stack([2 * chip_x + chip_y]).astype(jnp.int32)
    near_blocks = jnp.stack([2 * (1 - chip_x) + chip_y, 2 * chip_x + (1 - chip_y)]).astype(jnp.int32)
    far_block = jnp.stack([2 * (1 - chip_x) + (1 - chip_y)]).astype(jnp.int32)
    z = _matmul_column_blocks(h, fetch.arrays("in")["w_in"], own_block, None, tm=512, name="in_proj_own")
    tok = near("in", None, ["mid"], [z] + slots_filled)
    memn = _rmsnorm(mems, small["g_mem"], "norm_mem")
    tok = fetch.step("gather_near_done_in", [("pass_near", "in")], [], [tok, cos_t, sin_t, memn])
    z = _matmul_column_blocks(h, fetch.arrays("in")["w_in"], near_blocks, z, tm=1024, name="in_proj_near", after=tok)
    tok = far("in", [], z)
    tok = fetch.step("gather_done_in", [("pass_far", "in")], [], tok)
    w4.update(fetch.arrays("in"))
    z = _matmul_column_blocks(h, w4["w_in"], far_block, z, tm=1024, name="in_proj_far", after=tok)
    conv4 = w4["conv_w"]
    conv_w = conv4[:, :3, :].transpose(1, 0, 2).reshape(3, N_CHIPS * conv4.shape[2])
    c_in = w4["w_in"].shape[2]
    tok = near("mid", None, ["gate"], z)
    q_rot, k_rot, v_b = _rope_fwd(z, cos_t, sin_t)
    attn = _swa_fwd(q_rot, k_rot, v_b, small["sink"])
    co = _conv_fwd(z, conv_w)
    tok = far("mid", ["up"], attn)
    tok = last("mid", tok)
    w_o = w4["w_o"].reshape(-1, w4["w_o"].shape[-1])
    c_d = w4["w_attn_out"].shape[2]
    ya = _matmul(attn, w4["w_attn_out"], mode="nn", tm=2048, tn=c_d, out_dtypes=[F32], name="attn_out_proj",
                 b_blocks=N_CHIPS, after=tok)
    yc = _matmul(co, w4["w_conv_out"], mode="nn", tm=2048, tn=c_d, out_dtypes=[F32], name="conv_out_proj",
                 b_blocks=N_CHIPS)
    mix = _gate_fwd(z, small["b_gate"], ya, yc)
    x1 = _matmul(mix, w_o, mode="nn", tm=1024, tn=1024, out_dtypes=[F32], name="mix_out_proj", extras=[xs],
                 epilogue=_add_residual)
    tok = near("gate", None, ["down"], x1)
    w_cq = w4["w_cq"].reshape(-1, w4["w_cq"].shape[-1])
    w_ckv = w4["w_ckv"].reshape(-1, w4["w_ckv"].shape[-1])
    hc = _rmsnorm(x1, small["g_cross"], "norm_cross")
    qc = _matmul(hc, w_cq, mode="nn", tm=2048, tn=MEM_WIDTH, out_dtypes=[BF16], name="cross_q_proj", after=tok)
    kvc = _matmul(memn, w_ckv, mode="nn", tm=256, tn=2 * MEM_WIDTH, out_dtypes=[BF16], name="cross_kv_proj")
    oc = _cross_fwd(qc, kvc)
    tok = far("gate", [], oc)
    x2 = _matmul(oc, w4["w_co"], mode="nn", tm=2048, tn=c_d, out_dtypes=[F32], name="cross_out_proj",
                 extras=[x1], epilogue=_add_residual, b_blocks=N_CHIPS, after=tok)
    hf = _rmsnorm(x2, small["g_ffn"], "norm_ffn")
    tok = near("up", "gate", [], hf)
    c_ff = w4["w_gate"].shape[2]
    gate = _matmul(hf, w4["w_gate"], mode="nn", tm=1024, tn=c_ff, out_dtypes=[F32], name="ffn_gate_proj", b_blocks=N_CHIPS,
                   after=tok)
    tok = far("up", [], gate)
    tok = near("down", "up", [], tok)
    act, dact_dgate, dact_dup = _matmul(hf, w4["w_up"], mode="nn", tm=1024, tn=c_ff, out_dtypes=[BF16, BF16, BF16],
                                        name="ffn_up_proj", extras=[gate], epilogue=_swiglu_fwd, b_blocks=N_CHIPS, after=tok)
    tok = far("down", [], act)
    last("down", tok)
    w_down = w4["w_down"].reshape(-1, w4["w_down"].shape[-1])
    x3 = _matmul(act, w_down, mode="nn", tm=512, tn=512, out_dtypes=[F32], name="ffn_down_proj", extras=[x2],
                 epilogue=_add_residual)
    dx3, dx3b, sq, dg_final = _loss_head(x3, small["g_final"], target)

    da, du = _matmul(dx3b, w_down, mode="nt", tm=1024, tn=c_ff, out_dtypes=[BF16, BF16], name="ffn_down_bwd",
                     extras=[dact_dgate, dact_dup], epilogue=_swiglu_bwd)
    core = reduce.core
    ffn_shape = dict(row_sharded=False, tm=1024, tn=c_ff)
    g_down = _matmul(act, dx3b, mode="tn", tm=c_ff, tn=1024, out_dtypes=[BF16], name="ffn_down_wgrad")
    tok = reduce.add("down", {"w_down": g_down}, da)
    t_gate = _wgrad_half(hf, da, core, theirs=True, name="ffn_gate_wgrad_theirs", after=tok, **ffn_shape)
    tok = reduce.step("down", t_gate)
    t_up = _wgrad_half(hf, du, core, theirs=True, name="ffn_up_wgrad_theirs", after=tok, **ffn_shape)
    tok = reduce.send("ffn", {"w_gate": t_gate, "w_up": t_up}, dx3b)
    dhf = _matmul(da, w4["w_gate"], mode="nt", tm=512, tn=1024, out_dtypes=[F32], name="ffn_gate_bwd", b_blocks=N_CHIPS,
                  after=tok)
    got = reduce.received("ffn", dhf)
    p_gate = _wgrad_half(hf, da, core, theirs=False, name="ffn_gate_wgrad_mine", add=got["w_gate"], **ffn_shape)
    p_up = _wgrad_half(hf, du, core, theirs=False, name="ffn_up_wgrad_mine", add=got["w_up"], **ffn_shape)
    tok = reduce.add_parts("ffn", {"w_gate": p_gate, "w_up": p_up})
    dhf = _matmul(du, w4["w_up"], mode="nt", tm=512, tn=1024, out_dtypes=[F32], name="ffn_up_bwd", extras=[dhf],
                  epilogue=_add_residual, b_blocks=N_CHIPS, after=tok)
    tok = reduce.step("down", dhf)
    dx2, dx2b, dg_ffn = _rmsnorm_bwd(dhf, x2, small["g_ffn"], dx3, "norm_ffn_bwd")

    d_oc = _matmul(dx2b, w4["w_co"], mode="nt", tm=1024, tn=MEM_WIDTH, out_dtypes=[BF16], name="cross_out_bwd",
                   b_blocks=N_CHIPS, after=tok)
    g_co = _matmul(oc, dx2b, mode="tn", tm=MEM_WIDTH, tn=c_d, out_dtypes=[BF16], name="cross_out_wgrad", out_blocks=N_CHIPS)
    tok = reduce.step("down", g_co)
    dqc, dkvc = _cross_bwd(qc, kvc, d_oc)
    g_cq = _matmul(hc, dqc, mode="tn", tm=1024, tn=MEM_WIDTH, out_dtypes=[BF16], name="cross_q_wgrad", after=tok)
    dhc = _matmul(dqc, w_cq, mode="nt", tm=1024, tn=1024, out_dtypes=[F32], name="cross_q_bwd")
    g_ckv = _matmul(memn, dkvc, mode="tn", tm=1024, tn=2 * MEM_WIDTH, out_dtypes=[BF16], name="cross_kv_wgrad")
    dx1, dx1b, dg_cross = _rmsnorm_bwd(dhc, x1, small["g_cross"], dx2, "norm_cross_bwd")

    dmix = _matmul(dx1b, w_o, mode="nt", tm=1024, tn=1024, out_dtypes=[F32], name="mix_out_bwd")
    g_o = _matmul(mix, dx1b, mode="tn", tm=1024, tn=1024, out_dtypes=[BF16], name="mix_out_wgrad")
    dya, dyc, dz, db_a, db_c = _gate_bwd(z, small["b_gate"], ya, yc, dmix)
    d_attn = _matmul(dya, w4["w_attn_out"], mode="nt", tm=1024, tn=ATTN_WIDTH, out_dtypes=[BF16], name="attn_out_bwd",
                     b_blocks=N_CHIPS)
    g_ao = _matmul(attn, dya, mode="tn", tm=ATTN_WIDTH, tn=c_d, out_dtypes=[BF16], name="attn_out_wgrad", out_blocks=N_CHIPS)
    d_co = _matmul(dyc, w4["w_conv_out"], mode="nt", tm=1024, tn=CONV_WIDTH, out_dtypes=[F32], name="conv_out_bwd",
                   b_blocks=N_CHIPS)
    g_cvo = _matmul(co, dyc, mode="tn", tm=CONV_WIDTH, tn=c_d, out_dtypes=[BF16], name="conv_out_wgrad", out_blocks=N_CHIPS)
    tok = reduce.step("ffn", g_cvo)
    tok = reduce.add("mid", {"w_co": g_co, "w_cq": g_cq, "w_ckv": g_ckv, "w_o": g_o, "w_attn_out": g_ao, "w_conv_out": g_cvo}, tok)
    dz, d_conv_w = _conv_bwd(z, conv_w, d_co, dz)
    dq_rot, dk_rot, dv, dsink = _swa_bwd(q_rot, k_rot, v_b, d_attn, small["sink"])
    tok = reduce.step("mid", dq_rot)
    dz = _rope_bwd(dq_rot, dk_rot, dv, cos_t, sin_t, dz)
    in_shape = dict(row_sharded=False, tm=1024, tn=c_in)
    t_in = _wgrad_half(h, dz, core, theirs=True, name="in_proj_wgrad_theirs", after=tok, **in_shape)
    tok = reduce.send("in", {"w_in": t_in}, dk_rot)
    tok = reduce.step("ffn", tok, count=1)
    dmemn = _matmul(dkvc, w_ckv, mode="nt", tm=256, tn=1024, out_dtypes=[F32], name="cross_kv_bwd", after=tok)
    _, _, dg_mem = _rmsnorm_bwd(dmemn, mems, small["g_mem"], None, "norm_mem_bwd")
    got = reduce.received("in", dg_mem)
    p_in = _wgrad_half(h, dz, core, theirs=False, name="in_proj_wgrad_mine", add=got["w_in"], **in_shape)
    tok = reduce.add_parts("in", {"w_in": p_in})
    tok = reduce.step("ffn", tok)
    tok = reduce.step("mid", tok)
    dh = _matmul(dz, w4["w_in"], mode="nt", tm=512, tn=512, out_dtypes=[F32], name="in_proj_bwd", b_blocks=N_CHIPS,
                 after=tok)
    grad_x, _, dg_mix = _rmsnorm_bwd(dh, xs, small["g_mix"], dx1, "norm_mix_bwd")

    small_grads = {
        "g_mix": dg_mix, "sink": dsink[:, 0], "b_gate": jnp.concatenate([db_a, db_c], axis=1), "g_cross": dg_cross,
        "g_mem": dg_mem, "g_ffn": dg_ffn, "g_final": dg_final, "conv_w": d_conv_w,
    }
    return sq, grad_x, small_grads


def _pair_sum(g4s, ras, core, name):
    n = len(g4s)

    def body(c_ref, *refs):
        for k in range(n):
            g_ref, r_ref, o_ref = refs[2 * k], refs[2 * k + 1], refs[2 * n + k]
            o_ref[...] = (g_ref[...].astype(F32) + r_ref[...].astype(F32)).astype(BF16)

    in_specs, out_specs, out_shape, operands = [], [], [], []
    for g4, ra in zip(g4s, ras, strict=True):
        nb, rs, cs = g4.shape
        rh = rs // 2
        assert nb == N_CHIPS and ra.shape == (nb, rh, cs) and rh % BF16_SUBLANES == 0, (g4.shape, ra.shape)
        plain = pl.BlockSpec((None, rh, cs), lambda j, c: (j, 0, 0))
        in_specs += [pl.BlockSpec((None, rh, cs), lambda j, c: (j, c[0], 0)), plain]
        out_specs.append(plain)
        out_shape.append(jax.ShapeDtypeStruct((nb, rh, cs), BF16))
        operands += [g4, ra]
    return pl.pallas_call(
        body, name=name,
        grid_spec=pltpu.PrefetchScalarGridSpec(num_scalar_prefetch=1, grid=(N_CHIPS,), in_specs=in_specs, out_specs=out_specs),
        out_shape=out_shape,
        compiler_params=_params(1),
    )(core, *operands)


def _adamw_update(w, g, m, v):
    nm = ADAM_B1 * m + (1.0 - ADAM_B1) * g
    nv = ADAM_B2 * v + (1.0 - ADAM_B2) * (g * g)
    m_hat = nm / ADAM_C1
    v_hat = nv / ADAM_C2
    return -ADAM_LR * (m_hat / (jnp.sqrt(v_hat) + ADAM_EPS) + ADAM_WD * w), nm, nv


ADAMW_STEPS = 4
ADAMW_BYTES_PER_ELEMENT = 40


def _adamw_calls(names, shards):
    step_bytes = sum(shards[n].size // (2 * ADAMW_STEPS) * ADAMW_BYTES_PER_ELEMENT for n in names)
    return [list(names)] if 2 * step_bytes <= VMEM_LIMIT_BYTES * 3 // 4 else [[n] for n in names]


def _adamw_row_tiles(ws):
    for w in ws:
        assert w.shape[0] % (2 * ADAMW_STEPS * BF16_SUBLANES) == 0, w.shape
    return [w.shape[0] // (2 * ADAMW_STEPS) for w in ws]


def _adamw_own_half(ws, ms, vs, parts, rcs, place, name, after=None):
    n = len(ws)

    def body(p_ref, *refs):
        ins, outs = refs[:5 * n], refs[len(refs) - 5 * n:]
        for k in range(n):
            w_ref, m_ref, v_ref, own_ref, r_ref = ins[5 * k:5 * k + 5]
            gx_ref, g_ref, d_ref, nm_ref, nv_ref = outs[5 * k:5 * k + 5]
            g = own_ref[...].astype(F32)
            for j in range(r_ref.shape[0]):
                g = g + r_ref[j].astype(F32)
            gx_ref[...] = g
            g_ref[...] = g
            d_ref[...], nm_ref[...], nv_ref[...] = _adamw_update(w_ref[...], g, m_ref[...], v_ref[...])

    in_specs, out_specs, out_shape, operands = [], [], [], []
    for w, m, v, p, r, tr in zip(ws, ms, vs, parts, rcs, _adamw_row_tiles(ws), strict=True):
        cols = w.shape[1]
        mine = pl.BlockSpec((tr, cols), lambda i, pos: (pos[1] * ADAMW_STEPS + i, 0))
        in_specs += [mine, mine, mine, pl.BlockSpec((None, tr, cols), lambda i, pos: (pos[0], i, 0)),
                     pl.BlockSpec((r.shape[0], tr, cols), lambda i, pos: (0, i, 0))]
        out_specs += [mine] * 5
        out_shape += [jax.ShapeDtypeStruct(w.shape, F32)] * 5
        operands += [w, m, v, p, r]
    outs = pl.pallas_call(
        body, name=name,
        grid_spec=pltpu.PrefetchScalarGridSpec(
            num_scalar_prefetch=1, grid=(ADAMW_STEPS,),
            in_specs=in_specs + ([] if after is None else [ANY]), out_specs=out_specs),
        out_shape=out_shape,
        compiler_params=_params(1),
    )(place, *operands, *([] if after is None else [after]))
    return [tuple(outs[5 * k:5 * k + 5]) for k in range(n)]


def _adamw_other_half(ws, ms, vs, exchanged, halves, place, name, after=None):
    n = len(ws)

    def body(p_ref, *refs):
        ins, outs = refs[:8 * n], refs[len(refs) - 4 * n:]
        for k in range(n):
            w_ref, m_ref, v_ref, gx_ref = ins[8 * k:8 * k + 4]
            g_ref, d_ref, nm_ref, nv_ref = outs[4 * k:4 * k + 4]
            gv = gx_ref[...]
            g_ref[...] = gv
            d_ref[...], nm_ref[...], nv_ref[...] = _adamw_update(w_ref[...], gv, m_ref[...], v_ref[...])

    in_specs, out_specs, out_shape, operands, aliases = [], [], [], [], {}
    for k, (w, m, v, gx, half, tr) in enumerate(zip(ws, ms, vs, exchanged, halves, _adamw_row_tiles(ws), strict=True)):
        other = pl.BlockSpec((tr, w.shape[1]), lambda i, pos: ((1 - pos[1]) * ADAMW_STEPS + i, 0))
        in_specs += [other] * 4 + [ANY] * 4
        out_specs += [other] * 4
        out_shape += [jax.ShapeDtypeStruct(w.shape, F32)] * 4
        operands += [w, m, v, gx, *half]
        aliases.update({1 + 8 * k + 4 + j: 4 * k + j for j in range(4)})
    outs = pl.pallas_call(
        body, name=name,
        grid_spec=pltpu.PrefetchScalarGridSpec(
            num_scalar_prefetch=1, grid=(ADAMW_STEPS,),
            in_specs=in_specs + ([] if after is None else [ANY]), out_specs=out_specs),
        out_shape=out_shape,
        input_output_aliases=aliases,
        compiler_params=_params(1),
    )(place, *operands, *([] if after is None else [after]))
    return [tuple(outs[4 * k:4 * k + 4]) for k in range(n)]


def _cast_to_slot(w, place, dtype, name, after=None):
    rows, cols = w.shape
    tr = _row_tile(rows, 1024)

    def body(p_ref, w_ref, *rest):
        o_ref = rest[-1]
        o_ref[...] = w_ref[...].astype(dtype)

    return pl.pallas_call(
        body, name=name,
        grid_spec=pltpu.PrefetchScalarGridSpec(
            num_scalar_prefetch=1, grid=(rows // tr,),
            in_specs=[pl.BlockSpec((tr, cols), lambda i, p: (i, 0))] + ([] if after is None else [ANY]),
            out_specs=pl.BlockSpec((None, tr, cols), lambda i, p: (p[0], i, 0))),
        out_shape=jax.ShapeDtypeStruct((N_CHIPS, rows, cols), dtype),
        compiler_params=_params(1),
    )(place, w, *([] if after is None else [after]))


def _adamw_vectors(ws, gs, ms, vs, name):
    n = len(ws)

    def body(*refs):
        for k in range(n):
            w_ref, g_ref, m_ref, v_ref = refs[4 * k:4 * k + 4]
            d_ref, nm_ref, nv_ref = refs[4 * n + 3 * k:4 * n + 3 * k + 3]
            d_ref[...], nm_ref[...], nv_ref[...] = _adamw_update(w_ref[...], g_ref[...], m_ref[...], v_ref[...])

    operands = []
    for w, g, m, v in zip(ws, gs, ms, vs, strict=True):
        assert g.shape == w.shape == m.shape == v.shape, (w.shape, g.shape, m.shape, v.shape)
        operands += [w, g, m, v]
    vm = pl.BlockSpec(memory_space=pltpu.VMEM)
    outs = pl.pallas_call(
        body, name=name, in_specs=[vm] * (4 * n), out_specs=[vm] * (3 * n),
        out_shape=[jax.ShapeDtypeStruct(w.shape, F32) for w in ws for _ in range(3)],
    )(*operands)
    return [tuple(outs[3 * k:3 * k + 3]) for k in range(n)]


def _mesh_pos():
    return lax.axis_index("x"), lax.axis_index("y"), lax.axis_index("c")


def _other_chips(x, y):
    return [(1 - x, y), (x, 1 - y), (1 - x, 1 - y)]


def _half_rows(ref, which):
    rh = ref.shape[-2] // 2
    return ref.at[pl.ds(which * rh, rh), :]


def _remote(src, dst, send_sems, recv_sems, sem, to):
    return pltpu.make_async_remote_copy(src_ref=src, dst_ref=dst, send_sem=send_sems.at[sem], recv_sem=recv_sems.at[sem],
                                        device_id=to, device_id_type=MESH)


HBM = pl.BlockSpec(memory_space=pltpu.HBM)
SEM = pl.BlockSpec(memory_space=pltpu.SEMAPHORE)
DATAFLOW_EFFECT = pltpu.SideEffectType.DATAFLOW_SIDE_EFFECTING


def _in_hbm(arrays):
    return [pltpu.with_memory_space_constraint(a, pltpu.HBM) for a in arrays]


def _hbm_like(arrays):
    return [pltpu.HBM(a.shape, a.dtype) for a in arrays]


GATHER_COPIES_PER_ARRAY = {"direct": 2, "forward": 2, "pass_near": 2, "pass_far": 1}


def _gather_copies(kind, refs, x, y, c):
    me, near_x, near_y, far = 2 * x + y, 2 * (1 - x) + y, 2 * x + (1 - y), 2 * (1 - x) + (1 - y)
    to_x, to_y, sibling = (1 - x, y, c), (x, 1 - y, c), (x, y, 1 - c)
    out = []
    for ref in refs:
        rh = ref.shape[1] // 2
        rq = rh // 2

        def half(chip, ref=ref, rh=rh):
            return ref.at[chip, pl.ds(c * rh, rh), :]

        def quarter(chip, q, ref=ref, rh=rh, rq=rq):
            return ref.at[chip, pl.ds(c * rh + q * rq, rq), :]

        if kind == "direct":
            out += [(half(me), half(me), to_x), (half(me), half(me), to_y)]
        elif kind == "forward":
            out += [(quarter(near_x, 0), quarter(near_x, 0), to_y), (quarter(near_y, 1), quarter(near_y, 1), to_x)]
        elif kind == "pass_near":
            out += [(half(near_x), half(near_x), sibling), (half(near_y), half(near_y), sibling)]
        else:
            assert kind == "pass_far"
            out += [(half(far), half(far), sibling)]
    return out


def _gather_step(name, bufs, waits, starts, after):
    nb, nw, ns = len(bufs), len(waits), len(starts)
    after = [] if after is None else list(after) if isinstance(after, (list, tuple)) else [after]
    n_after = len(after)

    def body(*refs):
        ins = refs[:nb]
        wait_sems = refs[nb:nb + 2 * nw]
        start_sems = refs[nb + 2 * nw + n_after:nb + 2 * nw + n_after + 2 * ns]
        token = refs[-1]
        x, y, c = _mesh_pos()
        for j, (kind, idxs, _, _) in enumerate(waits):
            for i, (s_ref, d_ref, to) in enumerate(_gather_copies(kind, [ins[t] for t in idxs], x, y, c)):
                came = _remote(s_ref, d_ref, wait_sems[2 * j], wait_sems[2 * j + 1], i, to)
                came.wait_recv()
                came.wait_send()
        for j, (kind, idxs) in enumerate(starts):
            for i, (s_ref, d_ref, to) in enumerate(_gather_copies(kind, [ins[t] for t in idxs], x, y, c)):
                _remote(s_ref, d_ref, start_sems[2 * j], start_sems[2 * j + 1], i, to).start()
        token[...] = jnp.zeros_like(token)

    sems = []
    for kind, idxs in starts:
        sems += [pltpu.SemaphoreType.DMA((GATHER_COPIES_PER_ARRAY[kind] * len(idxs),))] * 2
    operands = _in_hbm(bufs) + [sem for w in waits for sem in w[2:]] + after
    outs = pl.pallas_call(
        body, name=name,
        in_specs=[HBM] * nb + [SEM] * (2 * nw) + [ANY] * n_after,
        out_specs=[SEM] * (2 * ns) + [HBM] * nb + [pl.BlockSpec(memory_space=pltpu.VMEM)],
        out_shape=sems + _hbm_like(bufs) + [jax.ShapeDtypeStruct((8, 128), F32)],
        input_output_aliases={i: 2 * ns + i for i in range(nb)},
        compiler_params=pltpu.CompilerParams(has_side_effects=DATAFLOW_EFFECT),
    )(*operands)
    return outs[2 * ns:2 * ns + nb], [(outs[2 * j], outs[2 * j + 1]) for j in range(ns)], outs[-1]


class _Gather:
    def __init__(self, groups):
        self.groups = groups
        self.bufs = {}
        self.in_flight = {}

    def put(self, slotted):
        self.bufs.update(slotted)

    def step(self, name, waits, starts, after=None):
        names = []
        for _, group in list(waits) + list(starts):
            names += [n for n in self.groups[group] if n not in names]
        index = {n: i for i, n in enumerate(names)}

        def members(group):
            return [index[n] for n in self.groups[group]]

        wait_args = [(kind, members(group)) + self.in_flight.pop((kind, group)) for kind, group in waits]
        start_args = [(kind, members(group)) for kind, group in starts]
        bufs, sems, token = _gather_step(name, [self.bufs[n] for n in names], wait_args, start_args, after)
        self.bufs.update(zip(names, bufs))
        for (kind, group), pair in zip(starts, sems):
            self.in_flight[(kind, group)] = pair
        return token

    def arrays(self, group):
        return {n: self.bufs[n] for n in self.groups[group]}


def _sibling_halves_copies(srcs, dsts, x, y, c):
    out = []
    for s_ref, d_ref in zip(srcs, dsts, strict=True):
        rh = s_ref.shape[1] // 2
        out.append((s_ref.at[:, pl.ds((1 - c) * rh, rh), :], d_ref, (x, y, 1 - c)))
    return out


def _to_sibling_copies(srcs, dsts, x, y, c):
    return [(s_ref, d_ref, (x, y, 1 - c)) for s_ref, d_ref in zip(srcs, dsts, strict=True)]


def _chip_copies(srcs, dsts, x, y, c):
    out = []
    for s_ref, d_ref in zip(srcs, dsts, strict=True):
        for k, (px, py) in enumerate(_other_chips(x, y)):
            out.append((s_ref.at[2 * px + py], d_ref.at[k], (px, py, c)))
    return out


def _join_copies(srcs, dsts, x, y, c):
    out = []
    for s_ref in srcs:
        mine = _half_rows(s_ref, c)
        out.append((mine, mine, (x, y, 1 - c)))
    return out


def _exchange_start(copies_fn, n_copies, srcs, fresh, after, name):
    ns, nb = len(srcs), len(srcs) + len(fresh)

    def body(*refs):
        bufs, send, recv, token = refs[:nb], refs[nb + 1], refs[nb + 2], refs[-1]
        x, y, c = _mesh_pos()
        for i, (s_ref, d_ref, to) in enumerate(copies_fn(bufs[:ns], bufs[ns:] if fresh else bufs[:ns], x, y, c)):
            _remote(s_ref, d_ref, send, recv, i, to).start()
        token[...] = jnp.zeros_like(token)

    sems = [pltpu.SemaphoreType.DMA((n_copies,))] * 2
    outs = pl.pallas_call(
        body, name=name,
        in_specs=[HBM] * nb + [ANY], out_specs=[SEM, SEM] + [HBM] * nb + [pl.BlockSpec(memory_space=pltpu.VMEM)],
        out_shape=sems + _hbm_like(list(srcs) + list(fresh)) + [jax.ShapeDtypeStruct((8, 128), F32)],
        input_output_aliases={i: 2 + i for i in range(nb)},
        compiler_params=pltpu.CompilerParams(has_side_effects=DATAFLOW_EFFECT),
    )(*_in_hbm(list(srcs) + list(fresh)), after)
    return outs[0], outs[1], outs[2:2 + ns], outs[2 + ns:2 + nb], outs[-1]


def _exchange_done(copies_fn, srcs, fresh, send, recv, after, name):
    ns, nb = len(srcs), len(srcs) + len(fresh)

    def body(*refs):
        bufs, send_in, recv_in = refs[:nb], refs[nb], refs[nb + 1]
        x, y, c = _mesh_pos()
        for i, (s_ref, d_ref, to) in enumerate(copies_fn(bufs[:ns], bufs[ns:] if fresh else bufs[:ns], x, y, c)):
            came = _remote(s_ref, d_ref, send_in, recv_in, i, to)
            came.wait_send()
            came.wait_recv()

    outs = pl.pallas_call(
        body, name=name,
        in_specs=[HBM] * nb + [SEM, SEM, ANY], out_specs=[HBM] * nb,
        out_shape=_hbm_like(list(srcs) + list(fresh)),
        input_output_aliases={i: i for i in range(nb)},
        compiler_params=pltpu.CompilerParams(has_side_effects=DATAFLOW_EFFECT),
    )(*_in_hbm(list(srcs) + list(fresh)), send, recv, after)
    return outs[:ns], outs[ns:]


class _Reduce:
    def __init__(self, place, core, shards, mom_m, mom_v):
        self.place, self.core = place, core
        self.shards, self.mom_m, self.mom_v = shards, mom_m, mom_v
        self.state = {}
        self.results = {}

    def add(self, group, grads, after):
        names = list(grads)
        g4s = [g.reshape((N_CHIPS, -1, g.shape[-1])) if g.ndim == 2 else g for g in grads.values()]
        fresh = [lax.empty((N_CHIPS, g.shape[1] // 2, g.shape[2]), BF16) for g in g4s]
        send, recv, g4s, fresh, token = _exchange_start(_sibling_halves_copies, len(names), g4s, fresh, after,
                                                        "pair_start_" + group)
        self.state[group] = (0, names, send, recv, g4s, fresh)
        return token

    def send(self, group, theirs, after):
        names, srcs = list(theirs), list(theirs.values())
        fresh = [lax.empty(s.shape, BF16) for s in srcs]
        send, recv, srcs, fresh, token = _exchange_start(_to_sibling_copies, len(names), srcs, fresh, after, "pair_start_" + group)
        self.state[group] = ("sent", names, send, recv, srcs, fresh)
        return token

    def received(self, group, after):
        stage, names, send, recv, srcs, fresh = self.state.pop(group)
        assert stage == "sent"
        _, got = _exchange_done(_to_sibling_copies, srcs, fresh, send, recv, after, "pair_done_" + group)
        return dict(zip(names, got))

    def add_parts(self, group, parts):
        names, srcs = list(parts), list(parts.values())
        fresh = [lax.empty((N_CHIPS - 1,) + p.shape[1:], BF16) for p in srcs]
        send, recv, srcs, fresh, token = _exchange_start(_chip_copies, 3 * len(names), srcs, fresh, self.core, "chips_start_" + group)
        self.state[group] = (1, names, send, recv, srcs, fresh)
        return token

    def step(self, group, after, count=None):
        stage, names, send, recv, srcs, fresh = self.state[group]
        if stage == 0:
            g4s, ras = _exchange_done(_sibling_halves_copies, srcs, fresh, send, recv, after, "pair_done_" + group)
            parts = _pair_sum(g4s, ras, self.core, "pair_sum_" + group)
            fresh = [lax.empty((N_CHIPS - 1,) + p.shape[1:], BF16) for p in parts]
            send, recv, parts, fresh, token = _exchange_start(_chip_copies, 3 * len(names), parts, fresh, self.core,
                                                              "chips_start_" + group)
            self.state[group] = (1, names, send, recv, parts, fresh)
            return token
        if stage == 1:
            parts, rcs = _exchange_done(_chip_copies, srcs, fresh, send, recv, after, "chips_done_" + group)
            token = None
            for call in _adamw_calls(names, self.shards):
                at = [names.index(n) for n in call]
                done = _adamw_own_half(*[[held[n] for n in call] for held in (self.shards, self.mom_m, self.mom_v)],
                                       [parts[i] for i in at], [rcs[i] for i in at], self.place,
                                       "adamw_own_" + (call[0] if len(call) == 1 else group), after=token)
                self.results.update(zip(call, done))
                token = done[-1][2]
            wholes = [self.results[n][0] for n in names]
            send, recv, wholes, _, token = _exchange_start(_join_copies, len(names), wholes, [], token, "join_start_" + group)
            self.state[group] = (2, names, send, recv, wholes, [])
            return token
        assert stage in (2, 3)
        if stage == 2:
            srcs, _ = _exchange_done(_join_copies, srcs, [], send, recv, after, "join_done_" + group)
            after = None
        token = after
        count = len(names) if count is None else count
        for call in _adamw_calls(names[:count], self.shards):
            at = [names.index(n) for n in call]
            done = _adamw_other_half(*[[held[n] for n in call] for held in (self.shards, self.mom_m, self.mom_v)],
                                     [srcs[i] for i in at], [self.results[n][1:] for n in call], self.place,
                                     "adamw_other_" + (call[0] if len(call) == 1 else group), after=token)
            self.results.update(zip(call, done))
            token = done[-1][1]
        if count < len(names):
            self.state[group] = (3, names[count:], None, None, srcs[count:], [])
        else:
            del self.state[group]
        return token


N_DEV = 8


def _to_all_copies(srcs, dsts, x, y, c):
    out = []
    for r in range(1, N_DEV):
        fx, fy, fc = (r >> 2) & 1, (r >> 1) & 1, r & 1
        out.append((srcs[0], dsts[0].at[r - 1], (x + fx - 2 * x * fx, y + fy - 2 * y * fy, c + fc - 2 * c * fc)))
    return out


def _all_reduce_small_start(v, after):
    slots = lax.empty((N_DEV - 1,) + v.shape, v.dtype)
    send, recv, (v,), (slots,), token = _exchange_start(_to_all_copies, N_DEV - 1, [v], [slots], after, "small_grads_start")
    return (send, recv, v, slots), token


def _all_reduce_small_done(started, after):
    send, recv, v, slots = started
    (v,), (slots,) = _exchange_done(_to_all_copies, [v], [slots], send, recv, after, "small_grads_done")

    def body(v_ref, slots_ref, o_ref):
        x, y, c = _mesh_pos()
        me = 4 * x + 2 * y + c
        acc = None
        for i in range(N_DEV):
            r = jnp.bitwise_xor(me, i)
            part = jnp.where(r == 0, v_ref[...], slots_ref[jnp.maximum(r - 1, 0)])
            acc = part if acc is None else acc + part
        o_ref[...] = acc

    vm = pl.BlockSpec(memory_space=pltpu.VMEM)
    return pl.pallas_call(body, name="small_grads_sum", in_specs=[vm, vm], out_specs=vm,
                          out_shape=jax.ShapeDtypeStruct(v.shape, v.dtype))(v, slots)


MATRICES = ("w_in", "w_attn_out", "w_conv_out", "w_o", "w_cq", "w_ckv", "w_co", "w_gate", "w_up", "w_down")
VECTORS = ("g_mix", "b_gate", "g_cross", "g_mem", "g_ffn", "g_final", "conv_w", "sink")
WEIGHT_ORDER = ("g_mix", "w_in", "sink", "conv_w", "b_gate", "w_attn_out", "w_conv_out", "w_o", "g_cross", "g_mem", "w_cq",
                "w_ckv", "w_co", "g_ffn", "w_gate", "w_up", "w_down", "g_final")
CONV_PAD_ROWS = 32
SMALL_ROWS = 8


def _pack(pieces):
    flat = jnp.concatenate([p.reshape(-1) for p in pieces])
    lane_group = SMALL_ROWS * 128
    total = -(-flat.shape[0] // lane_group) * lane_group
    flat = jnp.pad(flat, (0, total - flat.shape[0]))
    return flat.reshape(SMALL_ROWS, total // SMALL_ROWS), [p.size for p in pieces]


def _unpack(packed, pieces):
    flat = packed.reshape(-1)
    out, off = [], 0
    for p in pieces:
        out.append(flat[off:off + p.size].reshape(p.shape))
        off += p.size
    return out


def kernel(x, mem, g_mix, w_in, sink, conv_w, b_gate, w_attn_out, w_conv_out, w_o, g_cross, g_mem, w_cq, w_ckv, w_co, g_ffn, w_gate, w_up, w_down, g_final, loss_target, m_g_mix, m_w_in, m_sink, m_conv_w, m_b_gate, m_w_attn_out, m_w_conv_out, m_w_o, m_g_cross, m_g_mem, m_w_cq, m_w_ckv, m_w_co, m_g_ffn, m_w_gate, m_w_up, m_w_down, m_g_final, v_g_mix, v_w_in, v_sink, v_conv_w, v_b_gate, v_w_attn_out, v_w_conv_out, v_w_o, v_g_cross, v_g_mem, v_w_cq, v_w_ckv, v_w_co, v_g_ffn, v_w_gate, v_w_up, v_w_down, v_g_final):
    given = dict(g_mix=g_mix, w_in=w_in, sink=sink, conv_w=conv_w, b_gate=b_gate, w_attn_out=w_attn_out, w_conv_out=w_conv_out,
                 w_o=w_o, g_cross=g_cross, g_mem=g_mem, w_cq=w_cq, w_ckv=w_ckv, w_co=w_co, g_ffn=g_ffn, w_gate=w_gate, w_up=w_up,
                 w_down=w_down, g_final=g_final)
    mom_m = dict(g_mix=m_g_mix, w_in=m_w_in, sink=m_sink, conv_w=m_conv_w, b_gate=m_b_gate, w_attn_out=m_w_attn_out,
                 w_conv_out=m_w_conv_out, w_o=m_w_o, g_cross=m_g_cross, g_mem=m_g_mem, w_cq=m_w_cq, w_ckv=m_w_ckv, w_co=m_w_co,
                 g_ffn=m_g_ffn, w_gate=m_w_gate, w_up=m_w_up, w_down=m_w_down, g_final=m_g_final)
    mom_v = dict(g_mix=v_g_mix, w_in=v_w_in, sink=v_sink, conv_w=v_conv_w, b_gate=v_b_gate, w_attn_out=v_w_attn_out,
                 w_conv_out=v_w_conv_out, w_o=v_w_o, g_cross=v_g_cross, g_mem=v_g_mem, w_cq=v_w_cq, w_ckv=v_w_ckv, w_co=v_w_co,
                 g_ffn=v_g_ffn, w_gate=v_w_gate, w_up=v_w_up, w_down=v_w_down, g_final=v_g_final)
    xs, mems, target = x[0], mem[0], loss_target[0]
    d_model = xs.shape[1]
    chip = 2 * lax.axis_index("x") + lax.axis_index("y")
    core = jnp.reshape(lax.axis_index("c"), (1,)).astype(jnp.int32)
    place = jnp.stack([chip, lax.axis_index("c")]).astype(jnp.int32)

    shards = {n: given[n][0] for n in MATRICES}
    conv_cols = conv_w.shape[2]
    conv_pad = jnp.pad(conv_w[0], ((0, CONV_PAD_ROWS - conv_w.shape[1]), (0, 0)))
    fetch = _Gather(GATHER_GROUPS)
    first = {"w_in": _cast_to_slot(shards["w_in"], place, BF16, "to_slot_w_in"),
             "conv_w": _cast_to_slot(conv_pad, place, F32, "to_slot_conv_w")}
    fetch.put(first)
    tok = fetch.step("gather_start", [], [("direct", "in")])
    fetch.put({n: _cast_to_slot(shards[n], place, BF16, "to_slot_" + n, after=tok) for n in MATRICES if n != "w_in"})
    small = {n: given[n] for n in ("g_mix", "b_gate", "g_cross", "g_mem", "g_ffn")}
    small["g_final"] = g_final[None]
    small["sink"] = sink[0]

    reduce = _Reduce(place, core, shards, {n: mom_m[n][0] for n in MATRICES}, {n: mom_v[n][0] for n in MATRICES})
    sq, grad_x, small_grads = _local_step(xs, mems, target, small, fetch, reduce)

    loss_part = 0.5 * sq[0:1, 0:1] / d_model
    pieces = [small_grads[n] for n in VECTORS] + [loss_part]
    packed, _ = _pack(pieces)
    started, tok = _all_reduce_small_start(packed, core)
    tok = reduce.step("mid", tok)
    tok = reduce.step("in", tok)
    summed = _unpack(_all_reduce_small_done(started, tok), pieces)
    loss = summed[-1][0, 0]
    small_sum = dict(zip(VECTORS, summed[:-1]))
    small_sum["conv_w"] = lax.dynamic_slice_in_dim(small_sum["conv_w"], chip * conv_cols, conv_cols, axis=1)

    grad_out, delta, new_m, new_v = {}, {}, {}, {}
    for n in VECTORS:
        grad_out[n] = small_sum[n].reshape(given[n].shape)
    stepped = _adamw_vectors(*[[held[n] for n in VECTORS] for held in (given, grad_out, mom_m, mom_v)], "adamw_small")
    for n, (d, nm, nv) in zip(VECTORS, stepped):
        delta[n], new_m[n], new_v[n] = d, nm, nv
    reduce.step("in", stepped[-1][0])
    for n in MATRICES:
        g, d, nm, nv = reduce.results[n]
        grad_out[n], delta[n], new_m[n], new_v[n] = g[None], d[None], nm[None], nv[None]

    return (loss, grad_x[None], *[grad_out[n] for n in WEIGHT_ORDER], *[delta[n] for n in WEIGHT_ORDER],
            *[new_m[n] for n in WEIGHT_ORDER], *[new_v[n] for n in WEIGHT_ORDER])
```

```python
import jax
import jax.numpy as jnp
from jax import lax
from jax.experimental import pallas as pl
from jax.experimental.pallas import tpu as pltpu

F32 = jnp.float32
BF16 = jnp.bfloat16
MESH = pl.DeviceIdType.MESH
ANY = pl.BlockSpec(memory_space=pl.ANY)

VMEM_LIMIT_BYTES = 56 * 1024 * 1024

N_CHIPS = 4
HEAD_DIM = 128
N_Q_HEADS = 8
N_KV_HEADS = 2
Q_GROUP = N_Q_HEADS // N_KV_HEADS
ATTN_WIDTH = N_Q_HEADS * HEAD_DIM
KV_WIDTH = N_KV_HEADS * HEAD_DIM
WINDOW = 128
BLOCK = 128
BAND = 3 * BLOCK
ROPE_THETA = 10000.0
CONV_WIDTH = 1024
MEM_HEADS = 4
MEM_WIDTH = MEM_HEADS * HEAD_DIM
RMS_EPS = 1e-6
NEG_INF = -1e30
ATTN_SCALE = HEAD_DIM ** -0.5

Q_OFF, K_OFF, V_OFF, CU_OFF, CB_OFF, CC_OFF, GL_OFF = 0, 1024, 1280, 1536, 2560, 3584, 4608

ADAM_LR = 0.001
ADAM_B1 = 0.9
ADAM_B2 = 0.999
ADAM_EPS = 1e-08
ADAM_WD = 0.01
ADAM_STEP = 10
ADAM_C1 = 1.0 - ADAM_B1 ** ADAM_STEP
ADAM_C2 = 1.0 - ADAM_B2 ** ADAM_STEP


def _params(n_grid_axes):
    return pltpu.CompilerParams(dimension_semantics=("arbitrary",) * n_grid_axes, vmem_limit_bytes=VMEM_LIMIT_BYTES)


BF16_SUBLANES = 16


def _row_tile(rows, want):
    if rows <= want:
        return rows
    for t in range(want, 0, -BF16_SUBLANES):
        if rows % t == 0:
            return t
    return rows


def _matmul(a, b, *, mode, tm, tn, out_dtypes, name, extras=(), epilogue=None, b_blocks=1, out_blocks=1, after=None):
    if mode == "tn":
        kdim, m = a.shape
    else:
        m, kdim = a.shape
    if b_blocks > 1:
        nb, brows, bcols = b.shape
        assert nb == b_blocks
        if mode == "nn":
            n = bcols * nb
            assert brows == kdim
        else:
            assert mode == "nt" and bcols * nb == kdim
            n = brows
    else:
        n = b.shape[0] if mode == "nt" else b.shape[1]
    tm, tn = min(tm, m), min(tn, n)
    tk = kdim
    assert m % tm == 0 and n % tn == 0, (name, m, n, tm, tn)
    n_extra, n_out = len(extras), len(out_dtypes)
    n_after = 0 if after is None else 1

    if mode == "tn":
        a_spec = pl.BlockSpec((tk, tm), lambda j, i, k: (k, i))
        dims = (((0,), (0,)), ((), ()))
    else:
        a_spec = pl.BlockSpec((tm, tk), lambda j, i, k: (i, k))
        dims = (((1,), (0,)), ((), ())) if mode == "nn" else (((1,), (1,)), ((), ()))

    if b_blocks > 1 and mode == "nn":
        per = b.shape[2] // tn
        assert b.shape[2] % tn == 0
        b_spec = pl.BlockSpec((None, tk, tn), lambda j, i, k: (j // per, k, j % per))
    elif b_blocks > 1:
        b_spec = pl.BlockSpec((b_blocks, tn, b.shape[2]), lambda j, i, k: (0, j, 0))
    elif mode == "nt":
        b_spec = pl.BlockSpec((tn, tk), lambda j, i, k: (j, k))
    else:
        b_spec = pl.BlockSpec((tk, tn), lambda j, i, k: (k, j))

    tile_spec = pl.BlockSpec((tm, tn), lambda j, i, k: (i, j))
    if out_blocks > 1:
        ncols = n // out_blocks
        assert ncols % tn == 0
        oper = ncols // tn
        out_spec = pl.BlockSpec((None, tm, tn), lambda j, i, k: (j // oper, i, j % oper))
        out_shape = [jax.ShapeDtypeStruct((out_blocks, m, ncols), dt) for dt in out_dtypes]
    else:
        out_spec = tile_spec
        out_shape = [jax.ShapeDtypeStruct((m, n), dt) for dt in out_dtypes]

    def body(a_ref, b_ref, *rest):
        extra_refs = rest[:n_extra]
        out_refs = rest[n_extra + n_after:n_extra + n_after + n_out]
        if mode == "nt" and b_blocks > 1:
            cs = b.shape[2]
            acc = None
            for jb in range(b_blocks):
                prod = lax.dot_general(a_ref[:, jb * cs:(jb + 1) * cs].astype(BF16), b_ref[jb].astype(BF16), dims,
                                       preferred_element_type=F32)
                acc = prod if acc is None else acc + prod
        else:
            acc = lax.dot_general(a_ref[...].astype(BF16), b_ref[...].astype(BF16), dims, preferred_element_type=F32)
        tiles = (acc,) if epilogue is None else epilogue(acc, *[r[...] for r in extra_refs])
        for o_ref, t in zip(out_refs, tiles, strict=True):
            o_ref[...] = t.astype(o_ref.dtype)

    outs = pl.pallas_call(
        body,
        name=name,
        grid=(n // tn, m // tm, 1),
        in_specs=[a_spec, b_spec] + [tile_spec] * n_extra + [ANY] * n_after,
        out_specs=[out_spec] * n_out,
        out_shape=out_shape,
        compiler_params=_params(3),
    )(a, b, *extras, *([] if after is None else [after]))
    return outs[0] if n_out == 1 else outs


def _add_residual(acc, res):
    return (acc + res,)


def _matmul_column_blocks(a, b4, blocks, out, *, tm, name, after=None):
    m, kdim = a.shape
    nb, _, cols = b4.shape
    tm = min(tm, m)
    assert m % tm == 0

    def body(j_ref, a_ref, b_ref, *rest):
        rest[-1][...] = jnp.dot(a_ref[...], b_ref[...], preferred_element_type=F32)

    extra = ([] if out is None else [out]) + ([] if after is None else [after])
    n_blocks = blocks.shape[0]
    return pl.pallas_call(
        body, name=name,
        grid_spec=pltpu.PrefetchScalarGridSpec(
            num_scalar_prefetch=1, grid=(n_blocks, m // tm),
            in_specs=[pl.BlockSpec((tm, kdim), lambda j, i, blk: (i, 0)),
                      pl.BlockSpec((None, kdim, cols), lambda j, i, blk: (blk[j], 0, 0))] + [ANY] * len(extra),
            out_specs=pl.BlockSpec((tm, cols), lambda j, i, blk: (i, blk[j]))),
        out_shape=jax.ShapeDtypeStruct((m, nb * cols), F32),
        input_output_aliases={} if out is None else {3: 0},
        compiler_params=_params(2),
    )(blocks, a, b4, *extra)


def _wgrad_half(a, b, core, *, theirs, row_sharded, tm, tn, name, add=None, after=None):
    kdim, m = a.shape
    n = b.shape[1]
    rs, cs = (m // N_CHIPS, n) if row_sharded else (m, n // N_CHIPS)
    rh = rs // 2
    tm, tn = min(tm, rh), min(tn, cs)
    assert rh % tm == 0 and cs % tn == 0, (name, rh, cs, tm, tn)
    mh, per = rh // tm, cs // tn
    has_add = add is not None

    def half(c):
        return 1 - c[0] if theirs else c[0]

    if row_sharded:
        grid = (n // tn, N_CHIPS * mh)
        a_spec = pl.BlockSpec((kdim, tm), lambda j, r, c: (0, ((r // mh) * 2 + half(c)) * mh + r % mh))
        o_spec = pl.BlockSpec((None, tm, tn), lambda j, r, c: (r // mh, r % mh, j))
    else:
        grid = (n // tn, mh)
        a_spec = pl.BlockSpec((kdim, tm), lambda j, r, c: (0, half(c) * mh + r))
        o_spec = pl.BlockSpec((None, tm, tn), lambda j, r, c: (j // per, r, j % per))
    b_spec = pl.BlockSpec((kdim, tn), lambda j, r, c: (0, j))

    def body(c_ref, a_ref, b_ref, *rest):
        o_ref = rest[-1]
        acc = lax.dot_general(a_ref[...].astype(BF16), b_ref[...].astype(BF16), (((0,), (0,)), ((), ())),
                              preferred_element_type=F32)
        if has_add:
            acc = acc + rest[0][...].astype(F32)
        o_ref[...] = acc.astype(BF16)

    operands = [a, b] + ([add] if has_add else []) + ([] if after is None else [after])
    return pl.pallas_call(
        body, name=name,
        grid_spec=pltpu.PrefetchScalarGridSpec(
            num_scalar_prefetch=1, grid=grid,
            in_specs=[a_spec, b_spec] + ([o_spec] if has_add else []) + ([] if after is None else [ANY]),
            out_specs=o_spec),
        out_shape=jax.ShapeDtypeStruct((N_CHIPS, rh, cs), BF16),
        compiler_params=_params(2),
    )(core, *operands)


def _rstd(x):
    return lax.rsqrt(jnp.mean(x * x, axis=-1, keepdims=True) + RMS_EPS)


def _rmsnorm(x, g, name):
    s, d = x.shape
    tr = _row_tile(s, 512)

    def body(x_ref, g_ref, o_ref):
        xv = x_ref[...]
        o_ref[...] = (xv * _rstd(xv) * g_ref[...]).astype(BF16)

    return pl.pallas_call(
        body, name=name, grid=(s // tr,),
        in_specs=[pl.BlockSpec((tr, d), lambda i: (i, 0)), pl.BlockSpec((1, d), lambda i: (0, 0))],
        out_specs=pl.BlockSpec((tr, d), lambda i: (i, 0)),
        out_shape=jax.ShapeDtypeStruct((s, d), BF16),
        compiler_params=_params(1),
    )(x, g)


def _rmsnorm_bwd(dh, x, g, dres, name):
    s, d = x.shape
    tr = _row_tile(s, 512)
    has_res = dres is not None

    def body(*refs):
        if has_res:
            dh_ref, x_ref, g_ref, res_ref, dx_ref, dxb_ref, dg_ref = refs
        else:
            dh_ref, x_ref, g_ref, dx_ref, dxb_ref, dg_ref = refs
        xv = x_ref[...]
        dhv = dh_ref[...].astype(F32)
        r = _rstd(xv)
        xn = xv * r
        dhg = dhv * g_ref[...]
        dx = r * (dhg - xn * jnp.mean(dhg * xn, axis=-1, keepdims=True))
        if has_res:
            dx = dx + res_ref[...]
        dx_ref[...] = dx
        dxb_ref[...] = dx.astype(BF16)
        part = jnp.sum(dhv * xn, axis=0, keepdims=True)

        @pl.when(pl.program_id(0) == 0)
        def _():
            dg_ref[...] = part

        @pl.when(pl.program_id(0) > 0)
        def _():
            dg_ref[...] += part

    row = pl.BlockSpec((tr, d), lambda i: (i, 0))
    vec = pl.BlockSpec((1, d), lambda i: (0, 0))
    return pl.pallas_call(
        body, name=name, grid=(s // tr,),
        in_specs=[row, row, vec] + ([row] if has_res else []),
        out_specs=[row, row, vec],
        out_shape=[jax.ShapeDtypeStruct((s, d), F32), jax.ShapeDtypeStruct((s, d), BF16), jax.ShapeDtypeStruct((1, d), F32)],
        compiler_params=_params(1),
    )(*([dh, x, g] + ([dres] if has_res else [])))


def _loss_head(x3, g, target):
    s, d = x3.shape
    tr = _row_tile(s, 512)

    def body(x_ref, g_ref, t_ref, dx_ref, dxb_ref, sq_ref, dg_ref):
        xv = x_ref[...]
        gv = g_ref[...]
        r = _rstd(xv)
        xn = xv * r
        err = xn * gv - t_ref[...]
        dy = err * (1.0 / d)
        dyg = dy * gv
        dx = r * (dyg - xn * jnp.mean(dyg * xn, axis=-1, keepdims=True))
        dx_ref[...] = dx
        dxb_ref[...] = dx.astype(BF16)
        sq = jnp.sum(jnp.sum(err * err, axis=1, keepdims=True), axis=0, keepdims=True)
        sq = jnp.broadcast_to(sq, (1, 128))
        part = jnp.sum(dy * xn, axis=0, keepdims=True)

        @pl.when(pl.program_id(0) == 0)
        def _():
            sq_ref[...] = sq
            dg_ref[...] = part

        @pl.when(pl.program_id(0) > 0)
        def _():
            sq_ref[...] += sq
            dg_ref[...] += part

    row = pl.BlockSpec((tr, d), lambda i: (i, 0))
    vec = pl.BlockSpec((1, d), lambda i: (0, 0))
    return pl.pallas_call(
        body, name="loss_head", grid=(s // tr,),
        in_specs=[row, vec, row],
        out_specs=[row, row, pl.BlockSpec((1, 128), lambda i: (0, 0)), vec],
        out_shape=[jax.ShapeDtypeStruct((s, d), F32), jax.ShapeDtypeStruct((s, d), BF16),
                   jax.ShapeDtypeStruct((1, 128), F32), jax.ShapeDtypeStruct((1, d), F32)],
        compiler_params=_params(1),
    )(x3, g, target)


def _rope_tables(s):
    inv = 1.0 / (ROPE_THETA ** (jnp.arange(0, HEAD_DIM, 2, dtype=F32) / HEAD_DIM))
    ang = jnp.arange(s, dtype=F32)[:, None] * inv[None, :]
    cos, sin = jnp.cos(ang), jnp.sin(ang)
    return jnp.concatenate([cos, cos], axis=1), jnp.concatenate([-sin, sin], axis=1)


def _swap_halves(t):
    return pltpu.roll(t, HEAD_DIM // 2, 1)


def _rope_fwd(z, cos_t, sin_t):
    s = z.shape[0]
    tr = _row_tile(s, 256)

    def body(zq_ref, zk_ref, zv_ref, c_ref, s_ref, q_ref, k_ref, v_ref):
        c, sn = c_ref[...], s_ref[...]
        for hd in range(N_Q_HEADS):
            cols = slice(hd * HEAD_DIM, (hd + 1) * HEAD_DIM)
            t = zq_ref[:, cols]
            q_ref[:, cols] = (t * c + _swap_halves(t) * sn).astype(BF16)
        for hd in range(N_KV_HEADS):
            cols = slice(hd * HEAD_DIM, (hd + 1) * HEAD_DIM)
            t = zk_ref[:, cols]
            k_ref[:, cols] = (t * c + _swap_halves(t) * sn).astype(BF16)
        v_ref[...] = zv_ref[...].astype(BF16)

    tab = pl.BlockSpec((tr, HEAD_DIM), lambda i: (i, 0))
    return pl.pallas_call(
        body, name="rope_fwd", grid=(s // tr,),
        in_specs=[pl.BlockSpec((tr, ATTN_WIDTH), lambda i: (i, Q_OFF // ATTN_WIDTH)),
                  pl.BlockSpec((tr, KV_WIDTH), lambda i: (i, K_OFF // KV_WIDTH)),
                  pl.BlockSpec((tr, KV_WIDTH), lambda i: (i, V_OFF // KV_WIDTH)), tab, tab],
        out_specs=[pl.BlockSpec((tr, ATTN_WIDTH), lambda i: (i, 0)), pl.BlockSpec((tr, KV_WIDTH), lambda i: (i, 0)),
                   pl.BlockSpec((tr, KV_WIDTH), lambda i: (i, 0))],
        out_shape=[jax.ShapeDtypeStruct((s, ATTN_WIDTH), BF16), jax.ShapeDtypeStruct((s, KV_WIDTH), BF16),
                   jax.ShapeDtypeStruct((s, KV_WIDTH), BF16)],
        compiler_params=_params(1),
    )(z, z, z, cos_t, sin_t)


def _rope_bwd(dq_rot, dk_rot, dv, cos_t, sin_t, dz):
    s = dq_rot.shape[0]
    tr = _row_tile(s, 256)
    qkv_width = V_OFF + KV_WIDTH

    def body(dq_ref, dk_ref, dv_ref, c_ref, s_ref, dz_in_ref, o_ref):
        c, sn = c_ref[...], s_ref[...]
        for hd in range(N_Q_HEADS):
            t = dq_ref[:, hd * HEAD_DIM:(hd + 1) * HEAD_DIM]
            o_ref[:, Q_OFF + hd * HEAD_DIM:Q_OFF + (hd + 1) * HEAD_DIM] = (t * c + _swap_halves(t * sn)).astype(BF16)
        for hd in range(N_KV_HEADS):
            t = dk_ref[:, hd * HEAD_DIM:(hd + 1) * HEAD_DIM]
            o_ref[:, K_OFF + hd * HEAD_DIM:K_OFF + (hd + 1) * HEAD_DIM] = (t * c + _swap_halves(t * sn)).astype(BF16)
        o_ref[:, V_OFF:V_OFF + KV_WIDTH] = dv_ref[...].astype(BF16)

    tab = pl.BlockSpec((tr, HEAD_DIM), lambda i: (i, 0))
    wide = pl.BlockSpec((tr, ATTN_WIDTH), lambda i: (i, 0))
    narrow = pl.BlockSpec((tr, KV_WIDTH), lambda i: (i, 0))
    return pl.pallas_call(
        body, name="rope_bwd", grid=(s // tr,),
        in_specs=[wide, narrow, narrow, tab, tab, ANY],
        out_specs=pl.BlockSpec((tr, qkv_width), lambda i: (i, 0)),
        out_shape=jax.ShapeDtypeStruct(dz.shape, dz.dtype),
        input_output_aliases={5: 0},
        compiler_params=_params(1),
    )(dq_rot, dk_rot, dv, cos_t, sin_t, dz)


def _swa_band(i, s):
    return pl.multiple_of(jnp.clip((i - 1) * BLOCK, 0, s - BAND), BLOCK)


SWA_HEADS_PER_PASS = Q_GROUP


def _swa_probs(q_ref, k_ref, sink_ref, heads, start, valid):
    kv = heads[0] // Q_GROUP
    cols = slice(kv * HEAD_DIM, (kv + 1) * HEAD_DIM)
    kb = k_ref[pl.ds(start, BAND), cols]
    qg = jnp.concatenate([q_ref[:, hd * HEAD_DIM:(hd + 1) * HEAD_DIM] for hd in heads], axis=0)
    sc = lax.dot_general(qg, kb, (((1,), (1,)), ((), ())), preferred_element_type=F32) * ATTN_SCALE
    sc = jnp.where(valid, sc, NEG_INF)
    sk = jnp.concatenate([jnp.full((BLOCK, 1), sink_ref[hd], F32) for hd in heads], axis=0)
    mx = jnp.maximum(jnp.max(sc, axis=1, keepdims=True), sk)
    e = jnp.exp(sc - mx)
    es = jnp.exp(sk - mx)
    inv = 1.0 / (jnp.sum(e, axis=1, keepdims=True) + es)
    return qg, kb, e * inv, es * inv


def _swa_head_passes():
    return [list(range(h0, h0 + SWA_HEADS_PER_PASS)) for h0 in range(0, N_Q_HEADS, SWA_HEADS_PER_PASS)]


def _swa_valid(i, start):
    q_pos = i * BLOCK + lax.broadcasted_iota(jnp.int32, (BLOCK, 1), 0)
    q_pos = jnp.concatenate([q_pos] * SWA_HEADS_PER_PASS, axis=0)
    k_pos = start + lax.broadcasted_iota(jnp.int32, (1, BAND), 1)
    return jnp.abs(k_pos - q_pos) <= WINDOW


def _swa_fwd(q, k, v, sink):
    s = q.shape[0]
    assert s % BLOCK == 0 and s >= BAND

    def body(sink_ref, q_ref, k_ref, v_ref, o_ref):
        i = pl.program_id(0)
        start = _swa_band(i, s)
        valid = _swa_valid(i, start)
        for heads in _swa_head_passes():
            kv = heads[0] // Q_GROUP
            _, _, p, _ = _swa_probs(q_ref, k_ref, sink_ref, heads, start, valid)
            vb = v_ref[pl.ds(start, BAND), kv * HEAD_DIM:(kv + 1) * HEAD_DIM]
            o = jnp.dot(p.astype(BF16), vb, preferred_element_type=F32)
            for g, hd in enumerate(heads):
                o_ref[:, hd * HEAD_DIM:(hd + 1) * HEAD_DIM] = o[g * BLOCK:(g + 1) * BLOCK].astype(BF16)

    whole = pl.BlockSpec((s, KV_WIDTH), lambda i: (0, 0))
    blk = pl.BlockSpec((BLOCK, ATTN_WIDTH), lambda i: (i, 0))
    return pl.pallas_call(
        body, name="swa_fwd", grid=(s // BLOCK,),
        in_specs=[pl.BlockSpec(memory_space=pltpu.SMEM), blk, whole, whole],
        out_specs=blk,
        out_shape=jax.ShapeDtypeStruct((s, ATTN_WIDTH), BF16),
        compiler_params=_params(1),
    )(sink, q, k, v)


def _swa_bwd(q, k, v, d_out, sink):
    s = q.shape[0]

    def body(sink_ref, q_ref, k_ref, v_ref, do_ref, dq_ref, dk_ref, dv_ref, dsink_ref):
        i = pl.program_id(0)

        @pl.when(i == 0)
        def _():
            dk_ref[...] = jnp.zeros_like(dk_ref)
            dv_ref[...] = jnp.zeros_like(dv_ref)
            dsink_ref[...] = jnp.zeros_like(dsink_ref)

        start = _swa_band(i, s)
        valid = _swa_valid(i, start)
        for heads in _swa_head_passes():
            kv = heads[0] // Q_GROUP
            cols = slice(kv * HEAD_DIM, (kv + 1) * HEAD_DIM)
            qg, kb, p, p_sink = _swa_probs(q_ref, k_ref, sink_ref, heads, start, valid)
            vb = v_ref[pl.ds(start, BAND), cols]
            dog = jnp.concatenate([do_ref[:, hd * HEAD_DIM:(hd + 1) * HEAD_DIM] for hd in heads], axis=0)
            dp = lax.dot_general(dog, vb, (((1,), (1,)), ((), ())), preferred_element_type=F32)
            delta = jnp.sum(p * dp, axis=1, keepdims=True)
            ds = (p * (dp - delta) * ATTN_SCALE).astype(BF16)
            dqg = jnp.dot(ds, kb, preferred_element_type=F32)
            dk_ref[pl.ds(start, BAND), cols] += lax.dot_general(ds, qg, (((0,), (0,)), ((), ())), preferred_element_type=F32)
            dv_ref[pl.ds(start, BAND), cols] += lax.dot_general(p.astype(BF16), dog, (((0,), (0,)), ((), ())),
                                                                 preferred_element_type=F32)
            dsk = p_sink * delta
            for g, hd in enumerate(heads):
                dq_ref[:, hd * HEAD_DIM:(hd + 1) * HEAD_DIM] = dqg[g * BLOCK:(g + 1) * BLOCK]
                tot = jnp.sum(dsk[g * BLOCK:(g + 1) * BLOCK], axis=0, keepdims=True)
                dsink_ref[hd:hd + 1, :] -= jnp.broadcast_to(tot, (1, 128))

    whole = pl.BlockSpec((s, KV_WIDTH), lambda i: (0, 0))
    blk = pl.BlockSpec((BLOCK, ATTN_WIDTH), lambda i: (i, 0))
    return pl.pallas_call(
        body, name="swa_bwd", grid=(s // BLOCK,),
        in_specs=[pl.BlockSpec(memory_space=pltpu.SMEM), blk, whole, whole, blk],
        out_specs=[blk, whole, whole, pl.BlockSpec((N_Q_HEADS, 128), lambda i: (0, 0))],
        out_shape=[jax.ShapeDtypeStruct((s, ATTN_WIDTH), F32), jax.ShapeDtypeStruct((s, KV_WIDTH), F32),
                   jax.ShapeDtypeStruct((s, KV_WIDTH), F32), jax.ShapeDtypeStruct((N_Q_HEADS, 128), F32)],
        compiler_params=_params(1),
    )(sink, q, k, v, d_out)


CONV_CHUNK = 256


def _shift_rows(t, rows, down):
    n = t.shape[0]
    rolled = pltpu.roll(t, 1 if down else n - 1, 0)
    edge = 0 if down else n - 1
    return jnp.where(rows == edge, 0.0, rolled)


def _conv_specs(s):
    def z_spec(off):
        return pl.BlockSpec((s, CONV_CHUNK), lambda j, off=off: (0, off // CONV_CHUNK + j))
    chunk = pl.BlockSpec((s, CONV_CHUNK), lambda j: (0, j))
    w_spec = pl.BlockSpec((3, CONV_CHUNK), lambda j: (0, j))
    return z_spec(CU_OFF), z_spec(CB_OFF), z_spec(CC_OFF), chunk, w_spec


def _conv_fwd(z, conv_w):
    s = z.shape[0]
    cu_spec, cb_spec, cc_spec, chunk, w_spec = _conv_specs(s)

    def body(cu_ref, cb_ref, cc_ref, w_ref, o_ref):
        rows = lax.broadcasted_iota(jnp.int32, (s, 1), 0)
        t = cc_ref[...] * cu_ref[...]
        c3 = _shift_rows(t, rows, True) * w_ref[0:1, :] + t * w_ref[1:2, :] + _shift_rows(t, rows, False) * w_ref[2:3, :]
        o_ref[...] = (cb_ref[...] * c3).astype(BF16)

    return pl.pallas_call(
        body, name="conv_fwd", grid=(CONV_WIDTH // CONV_CHUNK,),
        in_specs=[cu_spec, cb_spec, cc_spec, w_spec],
        out_specs=chunk,
        out_shape=jax.ShapeDtypeStruct((s, CONV_WIDTH), BF16),
        compiler_params=_params(1),
    )(z, z, z, conv_w)


def _conv_bwd(z, conv_w, d_co, dz):
    s = z.shape[0]
    cu_spec, cb_spec, cc_spec, chunk, w_spec = _conv_specs(s)
    n_chunks = CONV_WIDTH // CONV_CHUNK
    offsets = (CU_OFF, CB_OFF, CC_OFF)

    def body(cu_ref, cb_ref, cc_ref, w_ref, d_ref, dz_in_ref, dz_ref, dw_ref, buf, sems):
        j = pl.program_id(0)

        def copies(j_at):
            return [pltpu.make_async_copy(buf.at[h], dz_ref.at[:, pl.ds(off + j_at * CONV_CHUNK, CONV_CHUNK)], sems.at[h])
                    for h, off in enumerate(offsets)]

        rows = lax.broadcasted_iota(jnp.int32, (s, 1), 0)
        cu, cc = cu_ref[...], cc_ref[...]
        t = cc * cu
        t_dn, t_up = _shift_rows(t, rows, True), _shift_rows(t, rows, False)
        c3 = t_dn * w_ref[0:1, :] + t * w_ref[1:2, :] + t_up * w_ref[2:3, :]
        d = d_ref[...]
        dc3 = d * cb_ref[...]
        dw_ref[0:1, :] = jnp.sum(dc3 * t_dn, axis=0, keepdims=True)
        dw_ref[1:2, :] = jnp.sum(dc3 * t, axis=0, keepdims=True)
        dw_ref[2:3, :] = jnp.sum(dc3 * t_up, axis=0, keepdims=True)
        dt = _shift_rows(dc3, rows, False) * w_ref[0:1, :] + dc3 * w_ref[1:2, :] + _shift_rows(dc3, rows, True) * w_ref[2:3, :]

        @pl.when(j > 0)
        def _():
            for cp in copies(j):
                cp.wait()

        buf[0] = (dt * cc).astype(BF16)
        buf[1] = (d * c3).astype(BF16)
        buf[2] = (dt * cu).astype(BF16)
        for cp in copies(j):
            cp.start()

        @pl.when(j == n_chunks - 1)
        def _():
            for cp in copies(j):
                cp.wait()

    return pl.pallas_call(
        body, name="conv_bwd", grid=(n_chunks,),
        in_specs=[cu_spec, cb_spec, cc_spec, w_spec, chunk, ANY],
        out_specs=[ANY, w_spec],
        out_shape=[jax.ShapeDtypeStruct(dz.shape, dz.dtype), jax.ShapeDtypeStruct((3, CONV_WIDTH), F32)],
        input_output_aliases={5: 0},
        scratch_shapes=[pltpu.VMEM((3, s, CONV_CHUNK), BF16), pltpu.SemaphoreType.DMA((3,))],
        compiler_params=_params(1),
    )(z, z, z, conv_w, d_co, dz)


GATE_CHUNK = 512


def _gate_specs(s, d, tr):
    n_chunks = d // GATE_CHUNK
    za = pl.BlockSpec((tr, GATE_CHUNK), lambda j, i: (i, GL_OFF // GATE_CHUNK + j))
    zc = pl.BlockSpec((tr, GATE_CHUNK), lambda j, i: (i, GL_OFF // GATE_CHUNK + n_chunks + j))
    ba = pl.BlockSpec((1, GATE_CHUNK), lambda j, i: (0, j))
    bc = pl.BlockSpec((1, GATE_CHUNK), lambda j, i: (0, n_chunks + j))
    tile = pl.BlockSpec((tr, GATE_CHUNK), lambda j, i: (i, j))
    return za, zc, ba, bc, tile


def _gate_fwd(z, b_gate, ya, yc):
    s, d = ya.shape
    tr = _row_tile(s, 512)
    za, zc, ba, bc, tile = _gate_specs(s, d, tr)

    def body(za_ref, zc_ref, ba_ref, bc_ref, ya_ref, yc_ref, o_ref):
        ga = jax.nn.sigmoid(za_ref[...] + ba_ref[...])
        gc = jax.nn.sigmoid(zc_ref[...] + bc_ref[...])
        o_ref[...] = (ga * ya_ref[...] + gc * yc_ref[...]).astype(BF16)

    return pl.pallas_call(
        body, name="gate_fwd", grid=(d // GATE_CHUNK, s // tr),
        in_specs=[za, zc, ba, bc, tile, tile],
        out_specs=tile,
        out_shape=jax.ShapeDtypeStruct((s, d), BF16),
        compiler_params=_params(2),
    )(z, z, b_gate, b_gate, ya, yc)


def _gate_bwd(z, b_gate, ya, yc, dmix):
    s, d = ya.shape
    tr = _row_tile(s, 512)
    za, zc, ba, bc, tile = _gate_specs(s, d, tr)
    vec = pl.BlockSpec((1, GATE_CHUNK), lambda j, i: (0, j))
    n_rows = s // tr
    in_width = z.shape[1]

    def body(za_ref, zc_ref, ba_ref, bc_ref, ya_ref, yc_ref, dm_ref, dya_ref, dyc_ref, dz_ref, dba_ref, dbc_ref, buf, sems):
        j, i = pl.program_id(0), pl.program_id(1)

        def copies(j_at, i_at):
            rows = pl.ds(i_at * tr, tr)
            return [pltpu.make_async_copy(buf.at[h], dz_ref.at[rows, pl.ds(GL_OFF + h * d + j_at * GATE_CHUNK, GATE_CHUNK)],
                                          sems.at[h]) for h in range(2)]

        ga = jax.nn.sigmoid(za_ref[...] + ba_ref[...])
        gc = jax.nn.sigmoid(zc_ref[...] + bc_ref[...])
        dm = dm_ref[...]
        dya_ref[...] = (dm * ga).astype(BF16)
        dyc_ref[...] = (dm * gc).astype(BF16)
        dla = dm * ya_ref[...] * ga * (1.0 - ga)
        dlc = dm * yc_ref[...] * gc * (1.0 - gc)

        @pl.when(j * n_rows + i > 0)
        def _():
            for cp in copies(j, i):
                cp.wait()

        buf[0] = dla.astype(BF16)
        buf[1] = dlc.astype(BF16)
        for cp in copies(j, i):
            cp.start()

        @pl.when((j == d // GATE_CHUNK - 1) & (i == n_rows - 1))
        def _():
            for cp in copies(j, i):
                cp.wait()

        pa = jnp.sum(dla, axis=0, keepdims=True)
        pc = jnp.sum(dlc, axis=0, keepdims=True)

        @pl.when(i == 0)
        def _():
            dba_ref[...] = pa
            dbc_ref[...] = pc

        @pl.when(i > 0)
        def _():
            dba_ref[...] += pa
            dbc_ref[...] += pc

    big = jax.ShapeDtypeStruct((s, d), BF16)
    small = jax.ShapeDtypeStruct((1, d), F32)
    return pl.pallas_call(
        body, name="gate_bwd", grid=(d // GATE_CHUNK, n_rows),
        in_specs=[za, zc, ba, bc, tile, tile, tile],
        out_specs=[tile, tile, ANY, vec, vec],
        out_shape=[big, big, jax.ShapeDtypeStruct((s, in_width), BF16), small, small],
        scratch_shapes=[pltpu.VMEM((2, tr, GATE_CHUNK), BF16), pltpu.SemaphoreType.DMA((2,))],
        compiler_params=_params(2),
    )(z, z, b_gate, b_gate, ya, yc, dmix)


def _cross_probs(q_ref, kv_ref, hd):
    cols = slice(hd * HEAD_DIM, (hd + 1) * HEAD_DIM)
    qh = q_ref[:, cols]
    kh = kv_ref[:, cols]
    sc = lax.dot_general(qh, kh, (((1,), (1,)), ((), ())), preferred_element_type=F32) * ATTN_SCALE
    e = jnp.exp(sc - jnp.max(sc, axis=1, keepdims=True))
    return qh, kh, e * (1.0 / jnp.sum(e, axis=1, keepdims=True))


def _cross_fwd(qc, kvc):
    s = qc.shape[0]
    n_mem = kvc.shape[0]
    tq = _row_tile(s, 256)

    def body(q_ref, kv_ref, o_ref):
        for hd in range(MEM_HEADS):
            _, _, p = _cross_probs(q_ref, kv_ref, hd)
            vh = kv_ref[:, MEM_WIDTH + hd * HEAD_DIM:MEM_WIDTH + (hd + 1) * HEAD_DIM]
            o_ref[:, hd * HEAD_DIM:(hd + 1) * HEAD_DIM] = jnp.dot(p.astype(BF16), vh, preferred_element_type=F32).astype(BF16)

    return pl.pallas_call(
        body, name="cross_fwd", grid=(s // tq,),
        in_specs=[pl.BlockSpec((tq, MEM_WIDTH), lambda i: (i, 0)), pl.BlockSpec((n_mem, 2 * MEM_WIDTH), lambda i: (0, 0))],
        out_specs=pl.BlockSpec((tq, MEM_WIDTH), lambda i: (i, 0)),
        out_shape=jax.ShapeDtypeStruct((s, MEM_WIDTH), BF16),
        compiler_params=_params(1),
    )(qc, kvc)


def _cross_bwd(qc, kvc, d_out):
    s = qc.shape[0]
    n_mem = kvc.shape[0]
    tq = _row_tile(s, 256)

    def body(q_ref, kv_ref, do_ref, dq_ref, dkv_ref):
        @pl.when(pl.program_id(0) == 0)
        def _():
            dkv_ref[...] = jnp.zeros_like(dkv_ref)

        for hd in range(MEM_HEADS):
            cols = slice(hd * HEAD_DIM, (hd + 1) * HEAD_DIM)
            vcols = slice(MEM_WIDTH + hd * HEAD_DIM, MEM_WIDTH + (hd + 1) * HEAD_DIM)
            qh, kh, p = _cross_probs(q_ref, kv_ref, hd)
            doh = do_ref[:, cols]
            dp = lax.dot_general(doh, kv_ref[:, vcols], (((1,), (1,)), ((), ())), preferred_element_type=F32)
            ds = (p * (dp - jnp.sum(p * dp, axis=1, keepdims=True)) * ATTN_SCALE).astype(BF16)
            dq_ref[:, cols] = jnp.dot(ds, kh, preferred_element_type=F32).astype(BF16)
            dkv_ref[:, cols] += lax.dot_general(ds, qh, (((0,), (0,)), ((), ())), preferred_element_type=F32)
            dkv_ref[:, vcols] += lax.dot_general(p.astype(BF16), doh, (((0,), (0,)), ((), ())), preferred_element_type=F32)

    qspec = pl.BlockSpec((tq, MEM_WIDTH), lambda i: (i, 0))
    kvspec = pl.BlockSpec((n_mem, 2 * MEM_WIDTH), lambda i: (0, 0))
    return pl.pallas_call(
        body, name="cross_bwd", grid=(s // tq,),
        in_specs=[qspec, kvspec, qspec],
        out_specs=[qspec, kvspec],
        out_shape=[jax.ShapeDtypeStruct((s, MEM_WIDTH), BF16), jax.ShapeDtypeStruct((n_mem, 2 * MEM_WIDTH), F32)],
        compiler_params=_params(1),
    )(qc, kvc, d_out)


def _swiglu_fwd(up, gate):
    sg = jax.nn.sigmoid(gate)
    silu = gate * sg
    return silu * up, up * (sg * (1.0 + gate * (1.0 - sg))), silu


def _swiglu_bwd(d_act, dact_dgate, dact_dup):
    return d_act * dact_dgate.astype(F32), d_act * dact_dup.astype(F32)


GATHER_GROUPS = {"in": ("w_in", "conv_w"), "mid": ("w_attn_out", "w_conv_out", "w_o", "w_cq", "w_ckv", "w_co"),
                 "gate": ("w_gate",), "up": ("w_up",), "down": ("w_down",)}


def _local_step(xs, mems, target, small, fetch, reduce):
    s, d = xs.shape
    w4 = {}
    cos_t, sin_t = _rope_tables(s)

    def near(group, done, then, after):
        waits = [("direct", group)] + ([("pass_near", done), ("pass_far", done)] if done else [])
        starts = [("forward", group), ("pass_near", group)] + [("direct", g) for g in then]
        tok = fetch.step("gather_near_" + group, waits, starts, after)
        if done:
            w4.update(fetch.arrays(done))
        return tok

    def far(group, then, after):
        return fetch.step("gather_far_" + group, [("forward", group)], [("pass_far", group)] + [("direct", g) for g in then], after)

    def last(group, after):
        tok = fetch.step("gather_done_" + group, [("pass_near", group), ("pass_far", group)], [], after)
        w4.update(fetch.arrays(group))
        return tok

    h = _rmsnorm(xs, small["g_mix"], "norm_mix")
    slots_filled = [a for g in ("gate", "up", "down") for a in fetch.arrays(g).values()]
    chip_x, chip_y = reduce.place[0] // 2, reduce.place[0] % 2
    own_block = jnp.stack([2 * chip_x + chip_y]).astype(jnp.int32)
    near_blocks = jnp.stack([2 * (1 - chip_x) + chip_y, 2 * chip_x + (1 - chip_y)]).astype(jnp.int32)
    far_block = jnp.stack([2 * (1 - chip_x) + (1 - chip_y)]).astype(jnp.int32)
    z = _matmul_column_blocks(h, fetch.arrays("in")["w_in"], own_block, None, tm=512, name="in_proj_own")
    tok = near("in", None, ["mid"], [z] + slots_filled)
    memn = _rmsnorm(mems, small["g_mem"], "norm_mem")
    tok = fetch.step("gather_near_done_in", [("pass_near", "in")], [], [tok, cos_t, sin_t, memn])
    z = _matmul_column_blocks(h, fetch.arrays("in")["w_in"], near_blocks, z, tm=1024, name="in_proj_near", after=tok)
    tok = far("in", [], z)
    tok = fetch.step("gather_done_in", [("pass_far", "in")], [], tok)
    w4.update(fetch.arrays("in"))
    z = _matmul_column_blocks(h, w4["w_in"], far_block, z, tm=1024, name="in_proj_far", after=tok)
    conv4 = w4["conv_w"]
    conv_w = conv4[:, :3, :].transpose(1, 0, 2).reshape(3, N_CHIPS * conv4.shape[2])
    c_in = w4["w_in"].shape[2]
    tok = near("mid", None, ["gate"], z)
    q_rot, k_rot, v_b = _rope_fwd(z, cos_t, sin_t)
    attn = _swa_fwd(q_rot, k_rot, v_b, small["sink"])
    co = _conv_fwd(z, conv_w)
    tok = far("mid", ["up"], attn)
    tok = last("mid", tok)
    w_o = w4["w_o"].reshape(-1, w4["w_o"].shape[-1])
    c_d = w4["w_attn_out"].shape[2]
    ya = _matmul(attn, w4["w_attn_out"], mode="nn", tm=2048, tn=c_d, out_dtypes=[F32], name="attn_out_proj",
                 b_blocks=N_CHIPS, after=tok)
    yc = _matmul(co, w4["w_conv_out"], mode="nn", tm=2048, tn=c_d, out_dtypes=[F32], name="conv_out_proj",
                 b_blocks=N_CHIPS)
    mix = _gate_fwd(z, small["b_gate"], ya, yc)
    x1 = _matmul(mix, w_o, mode="nn", tm=1024, tn=1024, out_dtypes=[F32], name="mix_out_proj", extras=[xs],
                 epilogue=_add_residual)
    tok = near("gate", None, ["down"], x1)
    w_cq = w4["w_cq"].reshape(-1, w4["w_cq"].shape[-1])
    w_ckv = w4["w_ckv"].reshape(-1, w4["w_ckv"].shape[-1])
    hc = _rmsnorm(x1, small["g_cross"], "norm_cross")
    qc = _matmul(hc, w_cq, mode="nn", tm=2048, tn=MEM_WIDTH, out_dtypes=[BF16], name="cross_q_proj", after=tok)
    kvc = _matmul(memn, w_ckv, mode="nn", tm=256, tn=2 * MEM_WIDTH, out_dtypes=[BF16], name="cross_kv_proj")
    oc = _cross_fwd(qc, kvc)
    tok = far("gate", [], oc)
    x2 = _matmul(oc, w4["w_co"], mode="nn", tm=2048, tn=c_d, out_dtypes=[F32], name="cross_out_proj",
                 extras=[x1], epilogue=_add_residual, b_blocks=N_CHIPS, after=tok)
    hf = _rmsnorm(x2, small["g_ffn"], "norm_ffn")
    tok = near("up", "gate", [], hf)
    c_ff = w4["w_gate"].shape[2]
    gate = _matmul(hf, w4["w_gate"], mode="nn", tm=1024, tn=c_ff, out_dtypes=[F32], name="ffn_gate_proj", b_blocks=N_CHIPS,
                   after=tok)
    tok = far("up", [], gate)
    tok = near("down", "up", [], tok)
    act, dact_dgate, dact_dup = _matmul(hf, w4["w_up"], mode="nn", tm=1024, tn=c_ff, out_dtypes=[BF16, BF16, BF16],
                                        name="ffn_up_proj", extras=[gate], epilogue=_swiglu_fwd, b_blocks=N_CHIPS, after=tok)
    tok = far("down", [], act)
    last("down", tok)
    w_down = w4["w_down"].reshape(-1, w4["w_down"].shape[-1])
    x3 = _matmul(act, w_down, mode="nn", tm=512, tn=512, out_dtypes=[F32], name="ffn_down_proj", extras=[x2],
                 epilogue=_add_residual)
    dx3, dx3b, sq, dg_final = _loss_head(x3, small["g_final"], target)

    da, du = _matmul(dx3b, w_down, mode="nt", tm=1024, tn=c_ff, out_dtypes=[BF16, BF16], name="ffn_down_bwd",
                     extras=[dact_dgate, dact_dup], epilogue=_swiglu_bwd)
    core = reduce.core
    ffn_shape = dict(row_sharded=False, tm=1024, tn=c_ff)
    g_down = _matmul(act, dx3b, mode="tn", tm=c_ff, tn=1024, out_dtypes=[BF16], name="ffn_down_wgrad")
    tok = reduce.add("down", {"w_down": g_down}, da)
    t_gate = _wgrad_half(hf, da, core, theirs=True, name="ffn_gate_wgrad_theirs", after=tok, **ffn_shape)
    tok = reduce.step("down", t_gate)
    t_up = _wgrad_half(hf, du, core, theirs=True, name="ffn_up_wgrad_theirs", after=tok, **ffn_shape)
    tok = reduce.send("ffn", {"w_gate": t_gate, "w_up": t_up}, dx3b)
    dhf = _matmul(da, w4["w_gate"], mode="nt", tm=512, tn=1024, out_dtypes=[F32], name="ffn_gate_bwd", b_blocks=N_CHIPS,
                  after=tok)
    got = reduce.received("ffn", dhf)
    p_gate = _wgrad_half(hf, da, core, theirs=False, name="ffn_gate_wgrad_mine", add=got["w_gate"], **ffn_shape)
    p_up = _wgrad_half(hf, du, core, theirs=False, name="ffn_up_wgrad_mine", add=got["w_up"], **ffn_shape)
    tok = reduce.add_parts("ffn", {"w_gate": p_gate, "w_up": p_up})
    dhf = _matmul(du, w4["w_up"], mode="nt", tm=512, tn=1024, out_dtypes=[F32], name="ffn_up_bwd", extras=[dhf],
                  epilogue=_add_residual, b_blocks=N_CHIPS, after=tok)
    tok = reduce.step("down", dhf)
    dx2, dx2b, dg_ffn = _rmsnorm_bwd(dhf, x2, small["g_ffn"], dx3, "norm_ffn_bwd")

    d_oc = _matmul(dx2b, w4["w_co"], mode="nt", tm=1024, tn=MEM_WIDTH, out_dtypes=[BF16], name="cross_out_bwd",
                   b_blocks=N_CHIPS, after=tok)
    g_co = _matmul(oc, dx2b, mode="tn", tm=MEM_WIDTH, tn=c_d, out_dtypes=[BF16], name="cross_out_wgrad", out_blocks=N_CHIPS)
    tok = reduce.step("down", g_co)
    dqc, dkvc = _cross_bwd(qc, kvc, d_oc)
    g_cq = _matmul(hc, dqc, mode="tn", tm=1024, tn=MEM_WIDTH, out_dtypes=[BF16], name="cross_q_wgrad", after=tok)
    dhc = _matmul(dqc, w_cq, mode="nt", tm=1024, tn=1024, out_dtypes=[F32], name="cross_q_bwd")
    g_ckv = _matmul(memn, dkvc, mode="tn", tm=1024, tn=2 * MEM_WIDTH, out_dtypes=[BF16], name="cross_kv_wgrad")
    dx1, dx1b, dg_cross = _rmsnorm_bwd(dhc, x1, small["g_cross"], dx2, "norm_cross_bwd")

    dmix = _matmul(dx1b, w_o, mode="nt", tm=1024, tn=1024, out_dtypes=[F32], name="mix_out_bwd")
    g_o = _matmul(mix, dx1b, mode="tn", tm=1024, tn=1024, out_dtypes=[BF16], name="mix_out_wgrad")
    dya, dyc, dz, db_a, db_c = _gate_bwd(z, small["b_gate"], ya, yc, dmix)
    d_attn = _matmul(dya, w4["w_attn_out"], mode="nt", tm=1024, tn=ATTN_WIDTH, out_dtypes=[BF16], name="attn_out_bwd",
                     b_blocks=N_CHIPS)
    g_ao = _matmul(attn, dya, mode="tn", tm=ATTN_WIDTH, tn=c_d, out_dtypes=[BF16], name="attn_out_wgrad", out_blocks=N_CHIPS)
    d_co = _matmul(dyc, w4["w_conv_out"], mode="nt", tm=1024, tn=CONV_WIDTH, out_dtypes=[F32], name="conv_out_bwd",
                   b_blocks=N_CHIPS)
    g_cvo = _matmul(co, dyc, mode="tn", tm=CONV_WIDTH, tn=c_d, out_dtypes=[BF16], name="conv_out_wgrad", out_blocks=N_CHIPS)
    tok = reduce.step("ffn", g_cvo)
    tok = reduce.add("mid", {"w_co": g_co, "w_cq": g_cq, "w_ckv": g_ckv, "w_o": g_o, "w_attn_out": g_ao, "w_conv_out": g_cvo}, tok)
    dz, d_conv_w = _conv_bwd(z, conv_w, d_co, dz)
    dq_rot, dk_rot, dv, dsink = _swa_bwd(q_rot, k_rot, v_b, d_attn, small["sink"])
    tok = reduce.step("mid", dq_rot)
    dz = _rope_bwd(dq_rot, dk_rot, dv, cos_t, sin_t, dz)
    in_shape = dict(row_sharded=False, tm=1024, tn=c_in)
    t_in = _wgrad_half(h, dz, core, theirs=True, name="in_proj_wgrad_theirs", after=tok, **in_shape)
    tok = reduce.send("in", {"w_in": t_in}, dk_rot)
    tok = reduce.step("ffn", tok, count=1)
    dmemn = _matmul(dkvc, w_ckv, mode="nt", tm=256, tn=1024, out_dtypes=[F32], name="cross_kv_bwd", after=tok)
    _, _, dg_mem = _rmsnorm_bwd(dmemn, mems, small["g_mem"], None, "norm_mem_bwd")
    got = reduce.received("in", dg_mem)
    p_in = _wgrad_half(h, dz, core, theirs=False, name="in_proj_wgrad_mine", add=got["w_in"], **in_shape)
    tok = reduce.add_parts("in", {"w_in": p_in})
    tok = reduce.step("ffn", tok)
    tok = reduce.step("mid", tok)
    dh = _matmul(dz, w4["w_in"], mode="nt", tm=512, tn=512, out_dtypes=[F32], name="in_proj_bwd", b_blocks=N_CHIPS,
                 after=tok)
    grad_x, _, dg_mix = _rmsnorm_bwd(dh, xs, small["g_mix"], dx1, "norm_mix_bwd")

    small_grads = {
        "g_mix": dg_mix, "sink": dsink[:, 0], "b_gate": jnp.concatenate([db_a, db_c], axis=1), "g_cross": dg_cross,
        "g_mem": dg_mem, "g_ffn": dg_ffn, "g_final": dg_final, "conv_w": d_conv_w,
    }
    return sq, grad_x, small_grads


def _pair_sum(g4s, ras, core, name):
    n = len(g4s)

    def body(c_ref, *refs):
        for k in range(n):
            g_ref, r_ref, o_ref = refs[2 * k], refs[2 * k + 1], refs[2 * n + k]
            o_ref[...] = (g_ref[...].astype(F32) + r_ref[...].astype(F32)).astype(BF16)

    in_specs, out_specs, out_shape, operands = [], [], [], []
    for g4, ra in zip(g4s, ras, strict=True):
        nb, rs, cs = g4.shape
        rh = rs // 2
        assert nb == N_CHIPS and ra.shape == (nb, rh, cs) and rh % BF16_SUBLANES == 0, (g4.shape, ra.shape)
        plain = pl.BlockSpec((None, rh, cs), lambda j, c: (j, 0, 0))
        in_specs += [pl.BlockSpec((None, rh, cs), lambda j, c: (j, c[0], 0)), plain]
        out_specs.append(plain)
        out_shape.append(jax.ShapeDtypeStruct((nb, rh, cs), BF16))
        operands += [g4, ra]
    return pl.pallas_call(
        body, name=name,
        grid_spec=pltpu.PrefetchScalarGridSpec(num_scalar_prefetch=1, grid=(N_CHIPS,), in_specs=in_specs, out_specs=out_specs),
        out_shape=out_shape,
        compiler_params=_params(1),
    )(core, *operands)


def _adamw_update(w, g, m, v):
    nm = ADAM_B1 * m + (1.0 - ADAM_B1) * g
    nv = ADAM_B2 * v + (1.0 - ADAM_B2) * (g * g)
    m_hat = nm / ADAM_C1
    v_hat = nv / ADAM_C2
    return -ADAM_LR * (m_hat / (jnp.sqrt(v_hat) + ADAM_EPS) + ADAM_WD * w), nm, nv


ADAMW_STEPS = 4
ADAMW_BYTES_PER_ELEMENT = 40


def _adamw_calls(names, shards):
    step_bytes = sum(shards[n].size // (2 * ADAMW_STEPS) * ADAMW_BYTES_PER_ELEMENT for n in names)
    return [list(names)] if 2 * step_bytes <= VMEM_LIMIT_BYTES * 3 // 4 else [[n] for n in names]


def _adamw_row_tiles(ws):
    for w in ws:
        assert w.shape[0] % (2 * ADAMW_STEPS * BF16_SUBLANES) == 0, w.shape
    return [w.shape[0] // (2 * ADAMW_STEPS) for w in ws]


def _adamw_own_half(ws, ms, vs, parts, rcs, place, name, after=None):
    n = len(ws)

    def body(p_ref, *refs):
        ins, outs = refs[:5 * n], refs[len(refs) - 5 * n:]
        for k in range(n):
            w_ref, m_ref, v_ref, own_ref, r_ref = ins[5 * k:5 * k + 5]
            gx_ref, g_ref, d_ref, nm_ref, nv_ref = outs[5 * k:5 * k + 5]
            g = own_ref[...].astype(F32)
            for j in range(r_ref.shape[0]):
                g = g + r_ref[j].astype(F32)
            gx_ref[...] = g
            g_ref[...] = g
            d_ref[...], nm_ref[...], nv_ref[...] = _adamw_update(w_ref[...], g, m_ref[...], v_ref[...])

    in_specs, out_specs, out_shape, operands = [], [], [], []
    for w, m, v, p, r, tr in zip(ws, ms, vs, parts, rcs, _adamw_row_tiles(ws), strict=True):
        cols = w.shape[1]
        mine = pl.BlockSpec((tr, cols), lambda i, pos: (pos[1] * ADAMW_STEPS + i, 0))
        in_specs += [mine, mine, mine, pl.BlockSpec((None, tr, cols), lambda i, pos: (pos[0], i, 0)),
                     pl.BlockSpec((r.shape[0], tr, cols), lambda i, pos: (0, i, 0))]
        out_specs += [mine] * 5
        out_shape += [jax.ShapeDtypeStruct(w.shape, F32)] * 5
        operands += [w, m, v, p, r]
    outs = pl.pallas_call(
        body, name=name,
        grid_spec=pltpu.PrefetchScalarGridSpec(
            num_scalar_prefetch=1, grid=(ADAMW_STEPS,),
            in_specs=in_specs + ([] if after is None else [ANY]), out_specs=out_specs),
        out_shape=out_shape,
        compiler_params=_params(1),
    )(place, *operands, *([] if after is None else [after]))
    return [tuple(outs[5 * k:5 * k + 5]) for k in range(n)]


def _adamw_other_half(ws, ms, vs, exchanged, halves, place, name, after=None):
    n = len(ws)

    def body(p_ref, *refs):
        ins, outs = refs[:8 * n], refs[len(refs) - 4 * n:]
        for k in range(n):
            w_ref, m_ref, v_ref, gx_ref = ins[8 * k:8 * k + 4]
            g_ref, d_ref, nm_ref, nv_ref = outs[4 * k:4 * k + 4]
            gv = gx_ref[...]
            g_ref[...] = gv
            d_ref[...], nm_ref[...], nv_ref[...] = _adamw_update(w_ref[...], gv, m_ref[...], v_ref[...])

    in_specs, out_specs, out_shape, operands, aliases = [], [], [], [], {}
    for k, (w, m, v, gx, half, tr) in enumerate(zip(ws, ms, vs, exchanged, halves, _adamw_row_tiles(ws), strict=True)):
        other = pl.BlockSpec((tr, w.shape[1]), lambda i, pos: ((1 - pos[1]) * ADAMW_STEPS + i, 0))
        in_specs += [other] * 4 + [ANY] * 4
        out_specs += [other] * 4
        out_shape += [jax.ShapeDtypeStruct(w.shape, F32)] * 4
        operands += [w, m, v, gx, *half]
        aliases.update({1 + 8 * k + 4 + j: 4 * k + j for j in range(4)})
    outs = pl.pallas_call(
        body, name=name,
        grid_spec=pltpu.PrefetchScalarGridSpec(
            num_scalar_prefetch=1, grid=(ADAMW_STEPS,),
            in_specs=in_specs + ([] if after is None else [ANY]), out_specs=out_specs),
        out_shape=out_shape,
        input_output_aliases=aliases,
        compiler_params=_params(1),
    )(place, *operands, *([] if after is None else [after]))
    return [tuple(outs[4 * k:4 * k + 4]) for k in range(n)]


def _cast_to_slot(w, place, dtype, name, after=None):
    rows, cols = w.shape
    tr = _row_tile(rows, 1024)

    def body(p_ref, w_ref, *rest):
        o_ref = rest[-1]
        o_ref[...] = w_ref[...].astype(dtype)

    return pl.pallas_call(
        body, name=name,
        grid_spec=pltpu.PrefetchScalarGridSpec(
            num_scalar_prefetch=1, grid=(rows // tr,),
            in_specs=[pl.BlockSpec((tr, cols), lambda i, p: (i, 0))] + ([] if after is None else [ANY]),
            out_specs=pl.BlockSpec((None, tr, cols), lambda i, p: (p[0], i, 0))),
        out_shape=jax.ShapeDtypeStruct((N_CHIPS, rows, cols), dtype),
        compiler_params=_params(1),
    )(place, w, *([] if after is None else [after]))


def _adamw_vectors(ws, gs, ms, vs, name):
    n = len(ws)

    def body(*refs):
        for k in range(n):
            w_ref, g_ref, m_ref, v_ref = refs[4 * k:4 * k + 4]
            d_ref, nm_ref, nv_ref = refs[4 * n + 3 * k:4 * n + 3 * k + 3]
            d_ref[...], nm_ref[...], nv_ref[...] = _adamw_update(w_ref[...], g_ref[...], m_ref[...], v_ref[...])

    operands = []
    for w, g, m, v in zip(ws, gs, ms, vs, strict=True):
        assert g.shape == w.shape == m.shape == v.shape, (w.shape, g.shape, m.shape, v.shape)
        operands += [a.reshape(-1, a.shape[-1]) for a in (w, g, m, v)]
    vm = pl.BlockSpec(memory_space=pltpu.VMEM)
    outs = pl.pallas_call(
        body, name=name, in_specs=[vm] * (4 * n), out_specs=[vm] * (3 * n),
        out_shape=[jax.ShapeDtypeStruct(a.shape, F32) for a in operands[::4] for _ in range(3)],
    )(*operands)
    return [tuple(o.reshape(w.shape) for o in outs[3 * k:3 * k + 3]) for k, w in enumerate(ws)]


def _mesh_pos():
    return lax.axis_index("x"), lax.axis_index("y"), lax.axis_index("c")


def _other_chips(x, y):
    return [(1 - x, y), (x, 1 - y), (1 - x, 1 - y)]


def _half_rows(ref, which):
    rh = ref.shape[-2] // 2
    return ref.at[pl.ds(which * rh, rh), :]


def _remote(src, dst, send_sems, recv_sems, sem, to):
    return pltpu.make_async_remote_copy(src_ref=src, dst_ref=dst, send_sem=send_sems.at[sem], recv_sem=recv_sems.at[sem],
                                        device_id=to, device_id_type=MESH)


HBM = pl.BlockSpec(memory_space=pltpu.HBM)
SEM = pl.BlockSpec(memory_space=pltpu.SEMAPHORE)
DATAFLOW_EFFECT = pltpu.SideEffectType.DATAFLOW_SIDE_EFFECTING


def _in_hbm(arrays):
    return [pltpu.with_memory_space_constraint(a, pltpu.HBM) for a in arrays]


def _hbm_like(arrays):
    return [pltpu.HBM(a.shape, a.dtype) for a in arrays]


GATHER_COPIES_PER_ARRAY = {"direct": 2, "forward": 2, "pass_near": 2, "pass_far": 1}


def _gather_copies(kind, refs, x, y, c):
    me, near_x, near_y, far = 2 * x + y, 2 * (1 - x) + y, 2 * x + (1 - y), 2 * (1 - x) + (1 - y)
    to_x, to_y, sibling = (1 - x, y, c), (x, 1 - y, c), (x, y, 1 - c)
    out = []
    for ref in refs:
        rh = ref.shape[1] // 2
        rq = rh // 2

        def half(chip, ref=ref, rh=rh):
            return ref.at[chip, pl.ds(c * rh, rh), :]

        def quarter(chip, q, ref=ref, rh=rh, rq=rq):
            return ref.at[chip, pl.ds(c * rh + q * rq, rq), :]

        if kind == "direct":
            out += [(half(me), half(me), to_x), (half(me), half(me), to_y)]
        elif kind == "forward":
            out += [(quarter(near_x, 0), quarter(near_x, 0), to_y), (quarter(near_y, 1), quarter(near_y, 1), to_x)]
        elif kind == "pass_near":
            out += [(half(near_x), half(near_x), sibling), (half(near_y), half(near_y), sibling)]
        else:
            assert kind == "pass_far"
            out += [(half(far), half(far), sibling)]
    return out


def _gather_step(name, bufs, waits, starts, after):
    nb, nw, ns = len(bufs), len(waits), len(starts)
    after = [] if after is None else list(after) if isinstance(after, (list, tuple)) else [after]
    n_after = len(after)

    def body(*refs):
        ins = refs[:nb]
        wait_sems = refs[nb:nb + 2 * nw]
        start_sems = refs[nb + 2 * nw + n_after:nb + 2 * nw + n_after + 2 * ns]
        token = refs[-1]
        x, y, c = _mesh_pos()
        for j, (kind, idxs, _, _) in enumerate(waits):
            for i, (s_ref, d_ref, to) in enumerate(_gather_copies(kind, [ins[t] for t in idxs], x, y, c)):
                came = _remote(s_ref, d_ref, wait_sems[2 * j], wait_sems[2 * j + 1], i, to)
                came.wait_recv()
                came.wait_send()
        for j, (kind, idxs) in enumerate(starts):
            for i, (s_ref, d_ref, to) in enumerate(_gather_copies(kind, [ins[t] for t in idxs], x, y, c)):
                _remote(s_ref, d_ref, start_sems[2 * j], start_sems[2 * j + 1], i, to).start()
        token[...] = jnp.zeros_like(token)

    sems = []
    for kind, idxs in starts:
        sems += [pltpu.SemaphoreType.DMA((GATHER_COPIES_PER_ARRAY[kind] * len(idxs),))] * 2
    operands = _in_hbm(bufs) + [sem for w in waits for sem in w[2:]] + after
    outs = pl.pallas_call(
        body, name=name,
        in_specs=[HBM] * nb + [SEM] * (2 * nw) + [ANY] * n_after,
        out_specs=[SEM] * (2 * ns) + [HBM] * nb + [pl.BlockSpec(memory_space=pltpu.VMEM)],
        out_shape=sems + _hbm_like(bufs) + [jax.ShapeDtypeStruct((8, 128), F32)],
        input_output_aliases={i: 2 * ns + i for i in range(nb)},
        compiler_params=pltpu.CompilerParams(has_side_effects=DATAFLOW_EFFECT),
    )(*operands)
    return outs[2 * ns:2 * ns + nb], [(outs[2 * j], outs[2 * j + 1]) for j in range(ns)], outs[-1]


class _Gather:
    def __init__(self, groups):
        self.groups = groups
        self.bufs = {}
        self.in_flight = {}

    def put(self, slotted):
        self.bufs.update(slotted)

    def step(self, name, waits, starts, after=None):
        names = []
        for _, group in list(waits) + list(starts):
            names += [n for n in self.groups[group] if n not in names]
        index = {n: i for i, n in enumerate(names)}

        def members(group):
            return [index[n] for n in self.groups[group]]

        wait_args = [(kind, members(group)) + self.in_flight.pop((kind, group)) for kind, group in waits]
        start_args = [(kind, members(group)) for kind, group in starts]
        bufs, sems, token = _gather_step(name, [self.bufs[n] for n in names], wait_args, start_args, after)
        self.bufs.update(zip(names, bufs))
        for (kind, group), pair in zip(starts, sems):
            self.in_flight[(kind, group)] = pair
        return token

    def arrays(self, group):
        return {n: self.bufs[n] for n in self.groups[group]}


def _sibling_halves_copies(srcs, dsts, x, y, c):
    out = []
    for s_ref, d_ref in zip(srcs, dsts, strict=True):
        rh = s_ref.shape[1] // 2
        out.append((s_ref.at[:, pl.ds((1 - c) * rh, rh), :], d_ref, (x, y, 1 - c)))
    return out


def _to_sibling_copies(srcs, dsts, x, y, c):
    return [(s_ref, d_ref, (x, y, 1 - c)) for s_ref, d_ref in zip(srcs, dsts, strict=True)]


def _chip_copies(srcs, dsts, x, y, c):
    out = []
    for s_ref, d_ref in zip(srcs, dsts, strict=True):
        for k, (px, py) in enumerate(_other_chips(x, y)):
            out.append((s_ref.at[2 * px + py], d_ref.at[k], (px, py, c)))
    return out


def _join_copies(srcs, dsts, x, y, c):
    out = []
    for s_ref in srcs:
        mine = _half_rows(s_ref, c)
        out.append((mine, mine, (x, y, 1 - c)))
    return out


def _exchange_start(copies_fn, n_copies, srcs, fresh, after, name):
    ns, nb = len(srcs), len(srcs) + len(fresh)

    def body(*refs):
        bufs, send, recv, token = refs[:nb], refs[nb + 1], refs[nb + 2], refs[-1]
        x, y, c = _mesh_pos()
        for i, (s_ref, d_ref, to) in enumerate(copies_fn(bufs[:ns], bufs[ns:] if fresh else bufs[:ns], x, y, c)):
            _remote(s_ref, d_ref, send, recv, i, to).start()
        token[...] = jnp.zeros_like(token)

    sems = [pltpu.SemaphoreType.DMA((n_copies,))] * 2
    outs = pl.pallas_call(
        body, name=name,
        in_specs=[HBM] * nb + [ANY], out_specs=[SEM, SEM] + [HBM] * nb + [pl.BlockSpec(memory_space=pltpu.VMEM)],
        out_shape=sems + _hbm_like(list(srcs) + list(fresh)) + [jax.ShapeDtypeStruct((8, 128), F32)],
        input_output_aliases={i: 2 + i for i in range(nb)},
        compiler_params=pltpu.CompilerParams(has_side_effects=DATAFLOW_EFFECT),
    )(*_in_hbm(list(srcs) + list(fresh)), after)
    return outs[0], outs[1], outs[2:2 + ns], outs[2 + ns:2 + nb], outs[-1]


def _exchange_done(copies_fn, srcs, fresh, send, recv, after, name):
    ns, nb = len(srcs), len(srcs) + len(fresh)

    def body(*refs):
        bufs, send_in, recv_in = refs[:nb], refs[nb], refs[nb + 1]
        x, y, c = _mesh_pos()
        for i, (s_ref, d_ref, to) in enumerate(copies_fn(bufs[:ns], bufs[ns:] if fresh else bufs[:ns], x, y, c)):
            came = _remote(s_ref, d_ref, send_in, recv_in, i, to)
            came.wait_send()
            came.wait_recv()

    outs = pl.pallas_call(
        body, name=name,
        in_specs=[HBM] * nb + [SEM, SEM, ANY], out_specs=[HBM] * nb,
        out_shape=_hbm_like(list(srcs) + list(fresh)),
        input_output_aliases={i: i for i in range(nb)},
        compiler_params=pltpu.CompilerParams(has_side_effects=DATAFLOW_EFFECT),
    )(*_in_hbm(list(srcs) + list(fresh)), send, recv, after)
    return outs[:ns], outs[ns:]


class _Reduce:
    def __init__(self, place, core, shards, mom_m, mom_v):
        self.place, self.core = place, core
        self.shards, self.mom_m, self.mom_v = shards, mom_m, mom_v
        self.state = {}
        self.results = {}

    def add(self, group, grads, after):
        names = list(grads)
        g4s = [g.reshape((N_CHIPS, -1, g.shape[-1])) if g.ndim == 2 else g for g in grads.values()]
        fresh = [lax.empty((N_CHIPS, g.shape[1] // 2, g.shape[2]), BF16) for g in g4s]
        send, recv, g4s, fresh, token = _exchange_start(_sibling_halves_copies, len(names), g4s, fresh, after,
                                                        "pair_start_" + group)
        self.state[group] = (0, names, send, recv, g4s, fresh)
        return token

    def send(self, group, theirs, after):
        names, srcs = list(theirs), list(theirs.values())
        fresh = [lax.empty(s.shape, BF16) for s in srcs]
        send, recv, srcs, fresh, token = _exchange_start(_to_sibling_copies, len(names), srcs, fresh, after, "pair_start_" + group)
        self.state[group] = ("sent", names, send, recv, srcs, fresh)
        return token

    def received(self, group, after):
        stage, names, send, recv, srcs, fresh = self.state.pop(group)
        assert stage == "sent"
        _, got = _exchange_done(_to_sibling_copies, srcs, fresh, send, recv, after, "pair_done_" + group)
        return dict(zip(names, got))

    def add_parts(self, group, parts):
        names, srcs = list(parts), list(parts.values())
        fresh = [lax.empty((N_CHIPS - 1,) + p.shape[1:], BF16) for p in srcs]
        send, recv, srcs, fresh, token = _exchange_start(_chip_copies, 3 * len(names), srcs, fresh, self.core, "chips_start_" + group)
        self.state[group] = (1, names, send, recv, srcs, fresh)
        return token

    def step(self, group, after, count=None):
        stage, names, send, recv, srcs, fresh = self.state[group]
        if stage == 0:
            g4s, ras = _exchange_done(_sibling_halves_copies, srcs, fresh, send, recv, after, "pair_done_" + group)
            parts = _pair_sum(g4s, ras, self.core, "pair_sum_" + group)
            fresh = [lax.empty((N_CHIPS - 1,) + p.shape[1:], BF16) for p in parts]
            send, recv, parts, fresh, token = _exchange_start(_chip_copies, 3 * len(names), parts, fresh, self.core,
                                                              "chips_start_" + group)
            self.state[group] = (1, names, send, recv, parts, fresh)
            return token
        if stage == 1:
            parts, rcs = _exchange_done(_chip_copies, srcs, fresh, send, recv, after, "chips_done_" + group)
            token = None
            for call in _adamw_calls(names, self.shards):
                at = [names.index(n) for n in call]
                done = _adamw_own_half(*[[held[n] for n in call] for held in (self.shards, self.mom_m, self.mom_v)],
                                       [parts[i] for i in at], [rcs[i] for i in at], self.place,
                                       "adamw_own_" + (call[0] if len(call) == 1 else group), after=token)
                self.results.update(zip(call, done))
                token = done[-1][2]
            wholes = [self.results[n][0] for n in names]
            send, recv, wholes, _, token = _exchange_start(_join_copies, len(names), wholes, [], token, "join_start_" + group)
            self.state[group] = (2, names, send, recv, wholes, [])
            return token
        assert stage in (2, 3)
        if stage == 2:
            srcs, _ = _exchange_done(_join_copies, srcs, [], send, recv, after, "join_done_" + group)
            after = None
        token = after
        count = len(names) if count is None else count
        for call in _adamw_calls(names[:count], self.shards):
            at = [names.index(n) for n in call]
            done = _adamw_other_half(*[[held[n] for n in call] for held in (self.shards, self.mom_m, self.mom_v)],
                                     [srcs[i] for i in at], [self.results[n][1:] for n in call], self.place,
                                     "adamw_other_" + (call[0] if len(call) == 1 else group), after=token)
            self.results.update(zip(call, done))
            token = done[-1][1]
        if count < len(names):
            self.state[group] = (3, names[count:], None, None, srcs[count:], [])
        else:
            del self.state[group]
        return token


N_DEV = 8


def _to_all_copies(srcs, dsts, x, y, c):
    out = []
    for r in range(1, N_DEV):
        fx, fy, fc = (r >> 2) & 1, (r >> 1) & 1, r & 1
        out.append((srcs[0], dsts[0].at[r - 1], (x + fx - 2 * x * fx, y + fy - 2 * y * fy, c + fc - 2 * c * fc)))
    return out


def _all_reduce_small_start(v, after):
    slots = lax.empty((N_DEV - 1,) + v.shape, v.dtype)
    send, recv, (v,), (slots,), token = _exchange_start(_to_all_copies, N_DEV - 1, [v], [slots], after, "small_grads_start")
    return (send, recv, v, slots), token


def _all_reduce_small_done(started, after):
    send, recv, v, slots = started
    (v,), (slots,) = _exchange_done(_to_all_copies, [v], [slots], send, recv, after, "small_grads_done")

    def body(v_ref, slots_ref, o_ref):
        x, y, c = _mesh_pos()
        me = 4 * x + 2 * y + c
        acc = None
        for i in range(N_DEV):
            r = jnp.bitwise_xor(me, i)
            part = jnp.where(r == 0, v_ref[...], slots_ref[jnp.maximum(r - 1, 0)])
            acc = part if acc is None else acc + part
        o_ref[...] = acc

    vm = pl.BlockSpec(memory_space=pltpu.VMEM)
    return pl.pallas_call(body, name="small_grads_sum", in_specs=[vm, vm], out_specs=vm,
                          out_shape=jax.ShapeDtypeStruct(v.shape, v.dtype))(v, slots)


MATRICES = ("w_in", "w_attn_out", "w_conv_out", "w_o", "w_cq", "w_ckv", "w_co", "w_gate", "w_up", "w_down")
VECTORS = ("g_mix", "b_gate", "g_cross", "g_mem", "g_ffn", "g_final", "conv_w", "sink")
WEIGHT_ORDER = ("g_mix", "w_in", "sink", "conv_w", "b_gate", "w_attn_out", "w_conv_out", "w_o", "g_cross", "g_mem", "w_cq",
                "w_ckv", "w_co", "g_ffn", "w_gate", "w_up", "w_down", "g_final")
CONV_PAD_ROWS = 32
SMALL_ROWS = 8


def _pack(pieces):
    flat = jnp.concatenate([p.reshape(-1) for p in pieces])
    lane_group = SMALL_ROWS * 128
    total = -(-flat.shape[0] // lane_group) * lane_group
    flat = jnp.pad(flat, (0, total - flat.shape[0]))
    return flat.reshape(SMALL_ROWS, total // SMALL_ROWS)


def _unpack(packed, pieces):
    flat = packed.reshape(-1)
    out, off = [], 0
    for p in pieces:
        out.append(flat[off:off + p.size].reshape(p.shape))
        off += p.size
    return out


def kernel(x, mem, g_mix, w_in, sink, conv_w, b_gate, w_attn_out, w_conv_out, w_o, g_cross, g_mem, w_cq, w_ckv, w_co, g_ffn, w_gate, w_up, w_down, g_final, loss_target, m_g_mix, m_w_in, m_sink, m_conv_w, m_b_gate, m_w_attn_out, m_w_conv_out, m_w_o, m_g_cross, m_g_mem, m_w_cq, m_w_ckv, m_w_co, m_g_ffn, m_w_gate, m_w_up, m_w_down, m_g_final, v_g_mix, v_w_in, v_sink, v_conv_w, v_b_gate, v_w_attn_out, v_w_conv_out, v_w_o, v_g_cross, v_g_mem, v_w_cq, v_w_ckv, v_w_co, v_g_ffn, v_w_gate, v_w_up, v_w_down, v_g_final):
    given = dict(g_mix=g_mix, w_in=w_in, sink=sink, conv_w=conv_w, b_gate=b_gate, w_attn_out=w_attn_out, w_conv_out=w_conv_out,
                 w_o=w_o, g_cross=g_cross, g_mem=g_mem, w_cq=w_cq, w_ckv=w_ckv, w_co=w_co, g_ffn=g_ffn, w_gate=w_gate, w_up=w_up,
                 w_down=w_down, g_final=g_final)
    mom_m = dict(g_mix=m_g_mix, w_in=m_w_in, sink=m_sink, conv_w=m_conv_w, b_gate=m_b_gate, w_attn_out=m_w_attn_out,
                 w_conv_out=m_w_conv_out, w_o=m_w_o, g_cross=m_g_cross, g_mem=m_g_mem, w_cq=m_w_cq, w_ckv=m_w_ckv, w_co=m_w_co,
                 g_ffn=m_g_ffn, w_gate=m_w_gate, w_up=m_w_up, w_down=m_w_down, g_final=m_g_final)
    mom_v = dict(g_mix=v_g_mix, w_in=v_w_in, sink=v_sink, conv_w=v_conv_w, b_gate=v_b_gate, w_attn_out=v_w_attn_out,
                 w_conv_out=v_w_conv_out, w_o=v_w_o, g_cross=v_g_cross, g_mem=v_g_mem, w_cq=v_w_cq, w_ckv=v_w_ckv, w_co=v_w_co,
                 g_ffn=v_g_ffn, w_gate=v_w_gate, w_up=v_w_up, w_down=v_w_down, g_final=v_g_final)
    xs, mems, target = x[0], mem[0], loss_target[0]
    d_model = xs.shape[1]
    chip = 2 * lax.axis_index("x") + lax.axis_index("y")
    core = jnp.reshape(lax.axis_index("c"), (1,)).astype(jnp.int32)
    place = jnp.stack([chip, lax.axis_index("c")]).astype(jnp.int32)

    shards = {n: given[n][0] for n in MATRICES}
    conv_cols = conv_w.shape[2]
    conv_pad = jnp.pad(conv_w[0], ((0, CONV_PAD_ROWS - conv_w.shape[1]), (0, 0)))
    fetch = _Gather(GATHER_GROUPS)
    first = {"w_in": _cast_to_slot(shards["w_in"], place, BF16, "to_slot_w_in"),
             "conv_w": _cast_to_slot(conv_pad, place, F32, "to_slot_conv_w")}
    fetch.put(first)
    tok = fetch.step("gather_start", [], [("direct", "in")])
    fetch.put({n: _cast_to_slot(shards[n], place, BF16, "to_slot_" + n, after=tok) for n in MATRICES if n != "w_in"})
    small = {n: given[n] for n in ("g_mix", "b_gate", "g_cross", "g_mem", "g_ffn")}
    small["g_final"] = g_final[None]
    small["sink"] = sink[0]

    reduce = _Reduce(place, core, shards, {n: mom_m[n][0] for n in MATRICES}, {n: mom_v[n][0] for n in MATRICES})
    sq, grad_x, small_grads = _local_step(xs, mems, target, small, fetch, reduce)

    loss_part = 0.5 * sq[0:1, 0:1] / d_model
    pieces = [small_grads[n] for n in VECTORS] + [loss_part]
    started, tok = _all_reduce_small_start(_pack(pieces), core)
    tok = reduce.step("mid", tok)
    tok = reduce.step("in", tok)
    summed = _unpack(_all_reduce_small_done(started, tok), pieces)
    loss = summed[-1][0, 0]
    small_sum = dict(zip(VECTORS, summed[:-1]))
    small_sum["conv_w"] = lax.dynamic_slice_in_dim(small_sum["conv_w"], chip * conv_cols, conv_cols, axis=1)

    grad_out, delta, new_m, new_v = {}, {}, {}, {}
    for n in VECTORS:
        grad_out[n] = small_sum[n].reshape(given[n].shape)
    stepped = _adamw_vectors(*[[held[n] for n in VECTORS] for held in (given, grad_out, mom_m, mom_v)], "adamw_small")
    for n, (d, nm, nv) in zip(VECTORS, stepped):
        delta[n], new_m[n], new_v[n] = d, nm, nv
    reduce.step("in", stepped[-1][0])
    for n in MATRICES:
        g, d, nm, nv = reduce.results[n]
        grad_out[n], delta[n], new_m[n], new_v[n] = g[None], d[None], nm[None], nv[None]

    return (loss, grad_x[None], *[grad_out[n] for n in WEIGHT_ORDER], *[delta[n] for n in WEIGHT_ORDER],
            *[new_m[n] for n in WEIGHT_ORDER], *[new_v[n] for n in WEIGHT_ORDER])
```

```python
import jax
import jax.numpy as jnp
from jax import lax
from jax.experimental import pallas as pl
from jax.experimental.pallas import tpu as pltpu

F32 = jnp.float32
BF16 = jnp.bfloat16
MESH = pl.DeviceIdType.MESH
ANY = pl.BlockSpec(memory_space=pl.ANY)

VMEM_LIMIT_BYTES = 56 * 1024 * 1024

N_CHIPS = 4
HEAD_DIM = 128
N_Q_HEADS = 8
N_KV_HEADS = 2
Q_GROUP = N_Q_HEADS // N_KV_HEADS
ATTN_WIDTH = N_Q_HEADS * HEAD_DIM
KV_WIDTH = N_KV_HEADS * HEAD_DIM
WINDOW = 128
BLOCK = 128
BAND = 3 * BLOCK
ROPE_THETA = 10000.0
CONV_WIDTH = 1024
MEM_HEADS = 4
MEM_WIDTH = MEM_HEADS * HEAD_DIM
RMS_EPS = 1e-6
NEG_INF = -1e30
ATTN_SCALE = HEAD_DIM ** -0.5

Q_OFF, K_OFF, V_OFF, CU_OFF, CB_OFF, CC_OFF, GL_OFF = 0, 1024, 1280, 1536, 2560, 3584, 4608

ADAM_LR = 0.001
ADAM_B1 = 0.9
ADAM_B2 = 0.999
ADAM_EPS = 1e-08
ADAM_WD = 0.01
ADAM_STEP = 10
ADAM_C1 = 1.0 - ADAM_B1 ** ADAM_STEP
ADAM_C2 = 1.0 - ADAM_B2 ** ADAM_STEP


def _params(n_grid_axes):
    return pltpu.CompilerParams(dimension_semantics=("arbitrary",) * n_grid_axes, vmem_limit_bytes=VMEM_LIMIT_BYTES)


BF16_SUBLANES = 16


def _row_tile(rows, want):
    if rows <= want:
        return rows
    for t in range(want, 0, -BF16_SUBLANES):
        if rows % t == 0:
            return t
    return rows


def _matmul(a, b, *, mode, tm, tn, out_dtypes, name, extras=(), epilogue=None, b_blocks=1, out_blocks=1, after=None,
            second=None):
    if mode == "tn":
        kdim, m = a.shape
    else:
        m, kdim = a.shape
    if b_blocks > 1:
        nb, brows, bcols = b.shape
        assert nb == b_blocks
        if mode == "nn":
            n = bcols * nb
            assert brows == kdim
        else:
            assert mode == "nt" and bcols * nb == kdim
            n = brows
    else:
        n = b.shape[0] if mode == "nt" else b.shape[1]
    tm, tn = min(tm, m), min(tn, n)
    tk = kdim
    assert m % tm == 0 and n % tn == 0, (name, m, n, tm, tn)
    n_extra, n_out = len(extras), len(out_dtypes)
    n_after = 0 if after is None else 1
    n_second = 0 if second is None else 1
    assert second is None or (second[0].shape == a.shape and second[1].shape == b.shape), name

    if mode == "tn":
        a_spec = pl.BlockSpec((tk, tm), lambda j, i, k: (k, i))
        dims = (((0,), (0,)), ((), ()))
    else:
        a_spec = pl.BlockSpec((tm, tk), lambda j, i, k: (i, k))
        dims = (((1,), (0,)), ((), ())) if mode == "nn" else (((1,), (1,)), ((), ()))

    if b_blocks > 1 and mode == "nn":
        per = b.shape[2] // tn
        assert b.shape[2] % tn == 0
        b_spec = pl.BlockSpec((None, tk, tn), lambda j, i, k: (j // per, k, j % per))
    elif b_blocks > 1:
        b_spec = pl.BlockSpec((b_blocks, tn, b.shape[2]), lambda j, i, k: (0, j, 0))
    elif mode == "nt":
        b_spec = pl.BlockSpec((tn, tk), lambda j, i, k: (j, k))
    else:
        b_spec = pl.BlockSpec((tk, tn), lambda j, i, k: (k, j))

    tile_spec = pl.BlockSpec((tm, tn), lambda j, i, k: (i, j))
    if out_blocks > 1:
        ncols = n // out_blocks
        assert ncols % tn == 0
        oper = ncols // tn
        out_spec = pl.BlockSpec((None, tm, tn), lambda j, i, k: (j // oper, i, j % oper))
        out_shape = [jax.ShapeDtypeStruct((out_blocks, m, ncols), dt) for dt in out_dtypes]
    else:
        out_spec = tile_spec
        out_shape = [jax.ShapeDtypeStruct((m, n), dt) for dt in out_dtypes]
    if second is not None:
        out_shape.append(jax.ShapeDtypeStruct(out_shape[0].shape, second[2]))

    def product(a_ref, b_ref):
        if mode == "nt" and b_blocks > 1:
            cs = b.shape[2]
            acc = None
            for jb in range(b_blocks):
                prod = lax.dot_general(a_ref[:, jb * cs:(jb + 1) * cs].astype(BF16), b_ref[jb].astype(BF16), dims,
                                       preferred_element_type=F32)
                acc = prod if acc is None else acc + prod
            return acc
        return lax.dot_general(a_ref[...].astype(BF16), b_ref[...].astype(BF16), dims, preferred_element_type=F32)

    def body(a_ref, b_ref, *rest):
        extra_refs = rest[:n_extra]
        second_refs = rest[n_extra:n_extra + 2 * n_second]
        out_refs = rest[n_extra + 2 * n_second + n_after:]
        acc = product(a_ref, b_ref)
        tiles = (acc,) if epilogue is None else epilogue(acc, *[r[...] for r in extra_refs])
        if second is not None:
            tiles = (*tiles, product(*second_refs))
        for o_ref, t in zip(out_refs, tiles, strict=True):
            o_ref[...] = t.astype(o_ref.dtype)

    outs = pl.pallas_call(
        body,
        name=name,
        grid=(n // tn, m // tm, 1),
        in_specs=[a_spec, b_spec] + [tile_spec] * n_extra + [a_spec, b_spec] * n_second + [ANY] * n_after,
        out_specs=[out_spec] * (n_out + n_second),
        out_shape=out_shape,
        compiler_params=_params(3),
    )(a, b, *extras, *([] if second is None else second[:2]), *([] if after is None else [after]))
    return outs[0] if n_out + n_second == 1 else outs


def _add_residual(acc, res):
    return (acc + res,)


def _matmul_column_blocks(a, b4, blocks, out, *, tm, name, after=None):
    m, kdim = a.shape
    nb, _, cols = b4.shape
    tm = min(tm, m)
    assert m % tm == 0

    def body(j_ref, a_ref, b_ref, *rest):
        rest[-1][...] = jnp.dot(a_ref[...], b_ref[...], preferred_element_type=F32)

    extra = ([] if out is None else [out]) + ([] if after is None else [after])
    n_blocks = blocks.shape[0]
    return pl.pallas_call(
        body, name=name,
        grid_spec=pltpu.PrefetchScalarGridSpec(
            num_scalar_prefetch=1, grid=(n_blocks, m // tm),
            in_specs=[pl.BlockSpec((tm, kdim), lambda j, i, blk: (i, 0)),
                      pl.BlockSpec((None, kdim, cols), lambda j, i, blk: (blk[j], 0, 0))] + [ANY] * len(extra),
            out_specs=pl.BlockSpec((tm, cols), lambda j, i, blk: (i, blk[j]))),
        out_shape=jax.ShapeDtypeStruct((m, nb * cols), F32),
        input_output_aliases={} if out is None else {3: 0},
        compiler_params=_params(2),
    )(blocks, a, b4, *extra)


def _wgrad_half(a, b, core, *, theirs, row_sharded, tm, tn, name, add=None, after=None):
    kdim, m = a.shape
    n = b.shape[1]
    rs, cs = (m // N_CHIPS, n) if row_sharded else (m, n // N_CHIPS)
    rh = rs // 2
    tm, tn = min(tm, rh), min(tn, cs)
    assert rh % tm == 0 and cs % tn == 0, (name, rh, cs, tm, tn)
    mh, per = rh // tm, cs // tn
    has_add = add is not None

    def half(c):
        return 1 - c[0] if theirs else c[0]

    if row_sharded:
        grid = (n // tn, N_CHIPS * mh)
        a_spec = pl.BlockSpec((kdim, tm), lambda j, r, c: (0, ((r // mh) * 2 + half(c)) * mh + r % mh))
        o_spec = pl.BlockSpec((None, tm, tn), lambda j, r, c: (r // mh, r % mh, j))
    else:
        grid = (n // tn, mh)
        a_spec = pl.BlockSpec((kdim, tm), lambda j, r, c: (0, half(c) * mh + r))
        o_spec = pl.BlockSpec((None, tm, tn), lambda j, r, c: (j // per, r, j % per))
    b_spec = pl.BlockSpec((kdim, tn), lambda j, r, c: (0, j))

    def body(c_ref, a_ref, b_ref, *rest):
        o_ref = rest[-1]
        acc = lax.dot_general(a_ref[...].astype(BF16), b_ref[...].astype(BF16), (((0,), (0,)), ((), ())),
                              preferred_element_type=F32)
        if has_add:
            acc = acc + rest[0][...].astype(F32)
        o_ref[...] = acc.astype(BF16)

    operands = [a, b] + ([add] if has_add else []) + ([] if after is None else [after])
    return pl.pallas_call(
        body, name=name,
        grid_spec=pltpu.PrefetchScalarGridSpec(
            num_scalar_prefetch=1, grid=grid,
            in_specs=[a_spec, b_spec] + ([o_spec] if has_add else []) + ([] if after is None else [ANY]),
            out_specs=o_spec),
        out_shape=jax.ShapeDtypeStruct((N_CHIPS, rh, cs), BF16),
        compiler_params=_params(2),
    )(core, *operands)


def _rstd(x):
    return lax.rsqrt(jnp.mean(x * x, axis=-1, keepdims=True) + RMS_EPS)


def _rmsnorm(x, g, name):
    s, d = x.shape
    tr = _row_tile(s, 512)

    def body(x_ref, g_ref, o_ref):
        xv = x_ref[...]
        o_ref[...] = (xv * _rstd(xv) * g_ref[...]).astype(BF16)

    return pl.pallas_call(
        body, name=name, grid=(s // tr,),
        in_specs=[pl.BlockSpec((tr, d), lambda i: (i, 0)), pl.BlockSpec((1, d), lambda i: (0, 0))],
        out_specs=pl.BlockSpec((tr, d), lambda i: (i, 0)),
        out_shape=jax.ShapeDtypeStruct((s, d), BF16),
        compiler_params=_params(1),
    )(x, g)


def _rmsnorm_bwd(dh, x, g, dres, name):
    s, d = x.shape
    tr = _row_tile(s, 512)
    has_res = dres is not None

    def body(*refs):
        if has_res:
            dh_ref, x_ref, g_ref, res_ref, dx_ref, dxb_ref, dg_ref = refs
        else:
            dh_ref, x_ref, g_ref, dx_ref, dxb_ref, dg_ref = refs
        xv = x_ref[...]
        dhv = dh_ref[...].astype(F32)
        r = _rstd(xv)
        xn = xv * r
        dhg = dhv * g_ref[...]
        dx = r * (dhg - xn * jnp.mean(dhg * xn, axis=-1, keepdims=True))
        if has_res:
            dx = dx + res_ref[...]
        dx_ref[...] = dx
        dxb_ref[...] = dx.astype(BF16)
        part = jnp.sum(dhv * xn, axis=0, keepdims=True)

        @pl.when(pl.program_id(0) == 0)
        def _():
            dg_ref[...] = part

        @pl.when(pl.program_id(0) > 0)
        def _():
            dg_ref[...] += part

    row = pl.BlockSpec((tr, d), lambda i: (i, 0))
    vec = pl.BlockSpec((1, d), lambda i: (0, 0))
    return pl.pallas_call(
        body, name=name, grid=(s // tr,),
        in_specs=[row, row, vec] + ([row] if has_res else []),
        out_specs=[row, row, vec],
        out_shape=[jax.ShapeDtypeStruct((s, d), F32), jax.ShapeDtypeStruct((s, d), BF16), jax.ShapeDtypeStruct((1, d), F32)],
        compiler_params=_params(1),
    )(*([dh, x, g] + ([dres] if has_res else [])))


def _loss_head(x3, g, target):
    s, d = x3.shape
    tr = _row_tile(s, 512)

    def body(x_ref, g_ref, t_ref, dx_ref, dxb_ref, sq_ref, dg_ref):
        xv = x_ref[...]
        gv = g_ref[...]
        r = _rstd(xv)
        xn = xv * r
        err = xn * gv - t_ref[...]
        dy = err * (1.0 / d)
        dyg = dy * gv
        dx = r * (dyg - xn * jnp.mean(dyg * xn, axis=-1, keepdims=True))
        dx_ref[...] = dx
        dxb_ref[...] = dx.astype(BF16)
        sq = jnp.sum(jnp.sum(err * err, axis=1, keepdims=True), axis=0, keepdims=True)
        sq = jnp.broadcast_to(sq, (1, 128))
        part = jnp.sum(dy * xn, axis=0, keepdims=True)

        @pl.when(pl.program_id(0) == 0)
        def _():
            sq_ref[...] = sq
            dg_ref[...] = part

        @pl.when(pl.program_id(0) > 0)
        def _():
            sq_ref[...] += sq
            dg_ref[...] += part

    row = pl.BlockSpec((tr, d), lambda i: (i, 0))
    vec = pl.BlockSpec((1, d), lambda i: (0, 0))
    return pl.pallas_call(
        body, name="loss_head", grid=(s // tr,),
        in_specs=[row, vec, row],
        out_specs=[row, row, pl.BlockSpec((1, 128), lambda i: (0, 0)), vec],
        out_shape=[jax.ShapeDtypeStruct((s, d), F32), jax.ShapeDtypeStruct((s, d), BF16),
                   jax.ShapeDtypeStruct((1, 128), F32), jax.ShapeDtypeStruct((1, d), F32)],
        compiler_params=_params(1),
    )(x3, g, target)


def _rope_tables(s):
    inv = 1.0 / (ROPE_THETA ** (jnp.arange(0, HEAD_DIM, 2, dtype=F32) / HEAD_DIM))
    ang = jnp.arange(s, dtype=F32)[:, None] * inv[None, :]
    cos, sin = jnp.cos(ang), jnp.sin(ang)
    return jnp.concatenate([cos, cos], axis=1), jnp.concatenate([-sin, sin], axis=1)


def _swap_halves(t):
    return pltpu.roll(t, HEAD_DIM // 2, 1)


def _rope_fwd(z, cos_t, sin_t):
    s = z.shape[0]
    tr = _row_tile(s, 256)

    def body(zq_ref, zk_ref, zv_ref, c_ref, s_ref, q_ref, k_ref, v_ref):
        c, sn = c_ref[...], s_ref[...]
        for hd in range(N_Q_HEADS):
            cols = slice(hd * HEAD_DIM, (hd + 1) * HEAD_DIM)
            t = zq_ref[:, cols]
            q_ref[:, cols] = (t * c + _swap_halves(t) * sn).astype(BF16)
        for hd in range(N_KV_HEADS):
            cols = slice(hd * HEAD_DIM, (hd + 1) * HEAD_DIM)
            t = zk_ref[:, cols]
            k_ref[:, cols] = (t * c + _swap_halves(t) * sn).astype(BF16)
        v_ref[...] = zv_ref[...].astype(BF16)

    tab = pl.BlockSpec((tr, HEAD_DIM), lambda i: (i, 0))
    return pl.pallas_call(
        body, name="rope_fwd", grid=(s // tr,),
        in_specs=[pl.BlockSpec((tr, ATTN_WIDTH), lambda i: (i, Q_OFF // ATTN_WIDTH)),
                  pl.BlockSpec((tr, KV_WIDTH), lambda i: (i, K_OFF // KV_WIDTH)),
                  pl.BlockSpec((tr, KV_WIDTH), lambda i: (i, V_OFF // KV_WIDTH)), tab, tab],
        out_specs=[pl.BlockSpec((tr, ATTN_WIDTH), lambda i: (i, 0)), pl.BlockSpec((tr, KV_WIDTH), lambda i: (i, 0)),
                   pl.BlockSpec((tr, KV_WIDTH), lambda i: (i, 0))],
        out_shape=[jax.ShapeDtypeStruct((s, ATTN_WIDTH), BF16), jax.ShapeDtypeStruct((s, KV_WIDTH), BF16),
                   jax.ShapeDtypeStruct((s, KV_WIDTH), BF16)],
        compiler_params=_params(1),
    )(z, z, z, cos_t, sin_t)


def _rope_bwd(dq_rot, dk_rot, dv, cos_t, sin_t, dz):
    s = dq_rot.shape[0]
    tr = _row_tile(s, 256)
    qkv_width = V_OFF + KV_WIDTH

    def body(dq_ref, dk_ref, dv_ref, c_ref, s_ref, dz_in_ref, o_ref):
        c, sn = c_ref[...], s_ref[...]
        for hd in range(N_Q_HEADS):
            t = dq_ref[:, hd * HEAD_DIM:(hd + 1) * HEAD_DIM]
            o_ref[:, Q_OFF + hd * HEAD_DIM:Q_OFF + (hd + 1) * HEAD_DIM] = (t * c + _swap_halves(t * sn)).astype(BF16)
        for hd in range(N_KV_HEADS):
            t = dk_ref[:, hd * HEAD_DIM:(hd + 1) * HEAD_DIM]
            o_ref[:, K_OFF + hd * HEAD_DIM:K_OFF + (hd + 1) * HEAD_DIM] = (t * c + _swap_halves(t * sn)).astype(BF16)
        o_ref[:, V_OFF:V_OFF + KV_WIDTH] = dv_ref[...].astype(BF16)

    tab = pl.BlockSpec((tr, HEAD_DIM), lambda i: (i, 0))
    wide = pl.BlockSpec((tr, ATTN_WIDTH), lambda i: (i, 0))
    narrow = pl.BlockSpec((tr, KV_WIDTH), lambda i: (i, 0))
    return pl.pallas_call(
        body, name="rope_bwd", grid=(s // tr,),
        in_specs=[wide, narrow, narrow, tab, tab, ANY],
        out_specs=pl.BlockSpec((tr, qkv_width), lambda i: (i, 0)),
        out_shape=jax.ShapeDtypeStruct(dz.shape, dz.dtype),
        input_output_aliases={5: 0},
        compiler_params=_params(1),
    )(dq_rot, dk_rot, dv, cos_t, sin_t, dz)


def _swa_band(i, s):
    return pl.multiple_of(jnp.clip((i - 1) * BLOCK, 0, s - BAND), BLOCK)


SWA_HEADS_PER_PASS = Q_GROUP


def _swa_probs(q_ref, k_ref, sink_ref, heads, start, valid):
    kv = heads[0] // Q_GROUP
    cols = slice(kv * HEAD_DIM, (kv + 1) * HEAD_DIM)
    kb = k_ref[pl.ds(start, BAND), cols]
    qg = jnp.concatenate([q_ref[:, hd * HEAD_DIM:(hd + 1) * HEAD_DIM] for hd in heads], axis=0)
    sc = lax.dot_general(qg, kb, (((1,), (1,)), ((), ())), preferred_element_type=F32) * ATTN_SCALE
    sc = jnp.where(valid, sc, NEG_INF)
    sk = jnp.concatenate([jnp.full((BLOCK, 1), sink_ref[hd], F32) for hd in heads], axis=0)
    mx = jnp.maximum(jnp.max(sc, axis=1, keepdims=True), sk)
    e = jnp.exp(sc - mx)
    es = jnp.exp(sk - mx)
    inv = 1.0 / (jnp.sum(e, axis=1, keepdims=True) + es)
    return qg, kb, e * inv, es * inv


def _swa_head_passes():
    return [list(range(h0, h0 + SWA_HEADS_PER_PASS)) for h0 in range(0, N_Q_HEADS, SWA_HEADS_PER_PASS)]


def _swa_valid(i, start):
    q_pos = i * BLOCK + lax.broadcasted_iota(jnp.int32, (BLOCK, 1), 0)
    q_pos = jnp.concatenate([q_pos] * SWA_HEADS_PER_PASS, axis=0)
    k_pos = start + lax.broadcasted_iota(jnp.int32, (1, BAND), 1)
    return jnp.abs(k_pos - q_pos) <= WINDOW


def _swa_fwd(q, k, v, sink):
    s = q.shape[0]
    assert s % BLOCK == 0 and s >= BAND

    def body(sink_ref, q_ref, k_ref, v_ref, o_ref):
        i = pl.program_id(0)
        start = _swa_band(i, s)
        valid = _swa_valid(i, start)
        for heads in _swa_head_passes():
            kv = heads[0] // Q_GROUP
            _, _, p, _ = _swa_probs(q_ref, k_ref, sink_ref, heads, start, valid)
            vb = v_ref[pl.ds(start, BAND), kv * HEAD_DIM:(kv + 1) * HEAD_DIM]
            o = jnp.dot(p.astype(BF16), vb, preferred_element_type=F32)
            for g, hd in enumerate(heads):
                o_ref[:, hd * HEAD_DIM:(hd + 1) * HEAD_DIM] = o[g * BLOCK:(g + 1) * BLOCK].astype(BF16)

    whole = pl.BlockSpec((s, KV_WIDTH), lambda i: (0, 0))
    blk = pl.BlockSpec((BLOCK, ATTN_WIDTH), lambda i: (i, 0))
    return pl.pallas_call(
        body, name="swa_fwd", grid=(s // BLOCK,),
        in_specs=[pl.BlockSpec(memory_space=pltpu.SMEM), blk, whole, whole],
        out_specs=blk,
        out_shape=jax.ShapeDtypeStruct((s, ATTN_WIDTH), BF16),
        compiler_params=_params(1),
    )(sink, q, k, v)


def _swa_bwd(q, k, v, d_out, sink):
    s = q.shape[0]

    def body(sink_ref, q_ref, k_ref, v_ref, do_ref, dq_ref, dk_ref, dv_ref, dsink_ref):
        i = pl.program_id(0)

        @pl.when(i == 0)
        def _():
            dk_ref[...] = jnp.zeros_like(dk_ref)
            dv_ref[...] = jnp.zeros_like(dv_ref)
            dsink_ref[...] = jnp.zeros_like(dsink_ref)

        start = _swa_band(i, s)
        valid = _swa_valid(i, start)
        for heads in _swa_head_passes():
            kv = heads[0] // Q_GROUP
            cols = slice(kv * HEAD_DIM, (kv + 1) * HEAD_DIM)
            qg, kb, p, p_sink = _swa_probs(q_ref, k_ref, sink_ref, heads, start, valid)
            vb = v_ref[pl.ds(start, BAND), cols]
            dog = jnp.concatenate([do_ref[:, hd * HEAD_DIM:(hd + 1) * HEAD_DIM] for hd in heads], axis=0)
            dp = lax.dot_general(dog, vb, (((1,), (1,)), ((), ())), preferred_element_type=F32)
            delta = jnp.sum(p * dp, axis=1, keepdims=True)
            ds = (p * (dp - delta) * ATTN_SCALE).astype(BF16)
            dqg = jnp.dot(ds, kb, preferred_element_type=F32)
            dk_ref[pl.ds(start, BAND), cols] += lax.dot_general(ds, qg, (((0,), (0,)), ((), ())), preferred_element_type=F32)
            dv_ref[pl.ds(start, BAND), cols] += lax.dot_general(p.astype(BF16), dog, (((0,), (0,)), ((), ())),
                                                                 preferred_element_type=F32)
            dsk = p_sink * delta
            for g, hd in enumerate(heads):
                dq_ref[:, hd * HEAD_DIM:(hd + 1) * HEAD_DIM] = dqg[g * BLOCK:(g + 1) * BLOCK]
                tot = jnp.sum(dsk[g * BLOCK:(g + 1) * BLOCK], axis=0, keepdims=True)
                dsink_ref[hd:hd + 1, :] -= jnp.broadcast_to(tot, (1, 128))

    whole = pl.BlockSpec((s, KV_WIDTH), lambda i: (0, 0))
    blk = pl.BlockSpec((BLOCK, ATTN_WIDTH), lambda i: (i, 0))
    return pl.pallas_call(
        body, name="swa_bwd", grid=(s // BLOCK,),
        in_specs=[pl.BlockSpec(memory_space=pltpu.SMEM), blk, whole, whole, blk],
        out_specs=[blk, whole, whole, pl.BlockSpec((N_Q_HEADS, 128), lambda i: (0, 0))],
        out_shape=[jax.ShapeDtypeStruct((s, ATTN_WIDTH), F32), jax.ShapeDtypeStruct((s, KV_WIDTH), F32),
                   jax.ShapeDtypeStruct((s, KV_WIDTH), F32), jax.ShapeDtypeStruct((N_Q_HEADS, 128), F32)],
        compiler_params=_params(1),
    )(sink, q, k, v, d_out)


CONV_CHUNK = 256


def _shift_rows(t, rows, down):
    n = t.shape[0]
    rolled = pltpu.roll(t, 1 if down else n - 1, 0)
    edge = 0 if down else n - 1
    return jnp.where(rows == edge, 0.0, rolled)


def _conv_specs(s):
    def z_spec(off):
        return pl.BlockSpec((s, CONV_CHUNK), lambda j, off=off: (0, off // CONV_CHUNK + j))
    chunk = pl.BlockSpec((s, CONV_CHUNK), lambda j: (0, j))
    w_spec = pl.BlockSpec((3, CONV_CHUNK), lambda j: (0, j))
    return z_spec(CU_OFF), z_spec(CB_OFF), z_spec(CC_OFF), chunk, w_spec


def _conv_fwd(z, conv_w):
    s = z.shape[0]
    cu_spec, cb_spec, cc_spec, chunk, w_spec = _conv_specs(s)

    def body(cu_ref, cb_ref, cc_ref, w_ref, o_ref):
        rows = lax.broadcasted_iota(jnp.int32, (s, 1), 0)
        t = cc_ref[...] * cu_ref[...]
        c3 = _shift_rows(t, rows, True) * w_ref[0:1, :] + t * w_ref[1:2, :] + _shift_rows(t, rows, False) * w_ref[2:3, :]
        o_ref[...] = (cb_ref[...] * c3).astype(BF16)

    return pl.pallas_call(
        body, name="conv_fwd", grid=(CONV_WIDTH // CONV_CHUNK,),
        in_specs=[cu_spec, cb_spec, cc_spec, w_spec],
        out_specs=chunk,
        out_shape=jax.ShapeDtypeStruct((s, CONV_WIDTH), BF16),
        compiler_params=_params(1),
    )(z, z, z, conv_w)


def _conv_bwd(z, conv_w, d_co, dz):
    s = z.shape[0]
    cu_spec, cb_spec, cc_spec, chunk, w_spec = _conv_specs(s)
    n_chunks = CONV_WIDTH // CONV_CHUNK
    offsets = (CU_OFF, CB_OFF, CC_OFF)

    def body(cu_ref, cb_ref, cc_ref, w_ref, d_ref, dz_in_ref, dz_ref, dw_ref, buf, sems):
        j = pl.program_id(0)

        def copies(j_at):
            return [pltpu.make_async_copy(buf.at[h], dz_ref.at[:, pl.ds(off + j_at * CONV_CHUNK, CONV_CHUNK)], sems.at[h])
                    for h, off in enumerate(offsets)]

        rows = lax.broadcasted_iota(jnp.int32, (s, 1), 0)
        cu, cc = cu_ref[...], cc_ref[...]
        t = cc * cu
        t_dn, t_up = _shift_rows(t, rows, True), _shift_rows(t, rows, False)
        c3 = t_dn * w_ref[0:1, :] + t * w_ref[1:2, :] + t_up * w_ref[2:3, :]
        d = d_ref[...]
        dc3 = d * cb_ref[...]
        dw_ref[0:1, :] = jnp.sum(dc3 * t_dn, axis=0, keepdims=True)
        dw_ref[1:2, :] = jnp.sum(dc3 * t, axis=0, keepdims=True)
        dw_ref[2:3, :] = jnp.sum(dc3 * t_up, axis=0, keepdims=True)
        dt = _shift_rows(dc3, rows, False) * w_ref[0:1, :] + dc3 * w_ref[1:2, :] + _shift_rows(dc3, rows, True) * w_ref[2:3, :]

        @pl.when(j > 0)
        def _():
            for cp in copies(j):
                cp.wait()

        buf[0] = (dt * cc).astype(BF16)
        buf[1] = (d * c3).astype(BF16)
        buf[2] = (dt * cu).astype(BF16)
        for cp in copies(j):
            cp.start()

        @pl.when(j == n_chunks - 1)
        def _():
            for cp in copies(j):
                cp.wait()

    return pl.pallas_call(
        body, name="conv_bwd", grid=(n_chunks,),
        in_specs=[cu_spec, cb_spec, cc_spec, w_spec, chunk, ANY],
        out_specs=[ANY, w_spec],
        out_shape=[jax.ShapeDtypeStruct(dz.shape, dz.dtype), jax.ShapeDtypeStruct((3, CONV_WIDTH), F32)],
        input_output_aliases={5: 0},
        scratch_shapes=[pltpu.VMEM((3, s, CONV_CHUNK), BF16), pltpu.SemaphoreType.DMA((3,))],
        compiler_params=_params(1),
    )(z, z, z, conv_w, d_co, dz)


GATE_CHUNK = 512


def _gate_specs(s, d, tr):
    n_chunks = d // GATE_CHUNK
    za = pl.BlockSpec((tr, GATE_CHUNK), lambda j, i: (i, GL_OFF // GATE_CHUNK + j))
    zc = pl.BlockSpec((tr, GATE_CHUNK), lambda j, i: (i, GL_OFF // GATE_CHUNK + n_chunks + j))
    ba = pl.BlockSpec((1, GATE_CHUNK), lambda j, i: (0, j))
    bc = pl.BlockSpec((1, GATE_CHUNK), lambda j, i: (0, n_chunks + j))
    tile = pl.BlockSpec((tr, GATE_CHUNK), lambda j, i: (i, j))
    return za, zc, ba, bc, tile


def _gate_fwd(z, b_gate, ya, yc):
    s, d = ya.shape
    tr = _row_tile(s, 512)
    za, zc, ba, bc, tile = _gate_specs(s, d, tr)

    def body(za_ref, zc_ref, ba_ref, bc_ref, ya_ref, yc_ref, o_ref):
        ga = jax.nn.sigmoid(za_ref[...] + ba_ref[...])
        gc = jax.nn.sigmoid(zc_ref[...] + bc_ref[...])
        o_ref[...] = (ga * ya_ref[...] + gc * yc_ref[...]).astype(BF16)

    return pl.pallas_call(
        body, name="gate_fwd", grid=(d // GATE_CHUNK, s // tr),
        in_specs=[za, zc, ba, bc, tile, tile],
        out_specs=tile,
        out_shape=jax.ShapeDtypeStruct((s, d), BF16),
        compiler_params=_params(2),
    )(z, z, b_gate, b_gate, ya, yc)


def _gate_bwd(z, b_gate, ya, yc, dmix):
    s, d = ya.shape
    tr = _row_tile(s, 512)
    za, zc, ba, bc, tile = _gate_specs(s, d, tr)
    vec = pl.BlockSpec((1, GATE_CHUNK), lambda j, i: (0, j))
    n_rows = s // tr
    in_width = z.shape[1]

    def body(za_ref, zc_ref, ba_ref, bc_ref, ya_ref, yc_ref, dm_ref, dya_ref, dyc_ref, dz_ref, dba_ref, dbc_ref, buf, sems):
        j, i = pl.program_id(0), pl.program_id(1)

        def copies(j_at, i_at):
            rows = pl.ds(i_at * tr, tr)
            return [pltpu.make_async_copy(buf.at[h], dz_ref.at[rows, pl.ds(GL_OFF + h * d + j_at * GATE_CHUNK, GATE_CHUNK)],
                                          sems.at[h]) for h in range(2)]

        ga = jax.nn.sigmoid(za_ref[...] + ba_ref[...])
        gc = jax.nn.sigmoid(zc_ref[...] + bc_ref[...])
        dm = dm_ref[...]
        dya_ref[...] = (dm * ga).astype(BF16)
        dyc_ref[...] = (dm * gc).astype(BF16)
        dla = dm * ya_ref[...] * ga * (1.0 - ga)
        dlc = dm * yc_ref[...] * gc * (1.0 - gc)

        @pl.when(j * n_rows + i > 0)
        def _():
            for cp in copies(j, i):
                cp.wait()

        buf[0] = dla.astype(BF16)
        buf[1] = dlc.astype(BF16)
        for cp in copies(j, i):
            cp.start()

        @pl.when((j == d // GATE_CHUNK - 1) & (i == n_rows - 1))
        def _():
            for cp in copies(j, i):
                cp.wait()

        pa = jnp.sum(dla, axis=0, keepdims=True)
        pc = jnp.sum(dlc, axis=0, keepdims=True)

        @pl.when(i == 0)
        def _():
            dba_ref[...] = pa
            dbc_ref[...] = pc

        @pl.when(i > 0)
        def _():
            dba_ref[...] += pa
            dbc_ref[...] += pc

    big = jax.ShapeDtypeStruct((s, d), BF16)
    small = jax.ShapeDtypeStruct((1, d), F32)
    return pl.pallas_call(
        body, name="gate_bwd", grid=(d // GATE_CHUNK, n_rows),
        in_specs=[za, zc, ba, bc, tile, tile, tile],
        out_specs=[tile, tile, ANY, vec, vec],
        out_shape=[big, big, jax.ShapeDtypeStruct((s, in_width), BF16), small, small],
        scratch_shapes=[pltpu.VMEM((2, tr, GATE_CHUNK), BF16), pltpu.SemaphoreType.DMA((2,))],
        compiler_params=_params(2),
    )(z, z, b_gate, b_gate, ya, yc, dmix)


def _cross_probs(q_ref, kv_ref, hd):
    cols = slice(hd * HEAD_DIM, (hd + 1) * HEAD_DIM)
    qh = q_ref[:, cols]
    kh = kv_ref[:, cols]
    sc = lax.dot_general(qh, kh, (((1,), (1,)), ((), ())), preferred_element_type=F32) * ATTN_SCALE
    e = jnp.exp(sc - jnp.max(sc, axis=1, keepdims=True))
    return qh, kh, e * (1.0 / jnp.sum(e, axis=1, keepdims=True))


def _cross_fwd(qc, kvc):
    s = qc.shape[0]
    n_mem = kvc.shape[0]
    tq = _row_tile(s, 256)

    def body(q_ref, kv_ref, o_ref):
        for hd in range(MEM_HEADS):
            _, _, p = _cross_probs(q_ref, kv_ref, hd)
            vh = kv_ref[:, MEM_WIDTH + hd * HEAD_DIM:MEM_WIDTH + (hd + 1) * HEAD_DIM]
            o_ref[:, hd * HEAD_DIM:(hd + 1) * HEAD_DIM] = jnp.dot(p.astype(BF16), vh, preferred_element_type=F32).astype(BF16)

    return pl.pallas_call(
        body, name="cross_fwd", grid=(s // tq,),
        in_specs=[pl.BlockSpec((tq, MEM_WIDTH), lambda i: (i, 0)), pl.BlockSpec((n_mem, 2 * MEM_WIDTH), lambda i: (0, 0))],
        out_specs=pl.BlockSpec((tq, MEM_WIDTH), lambda i: (i, 0)),
        out_shape=jax.ShapeDtypeStruct((s, MEM_WIDTH), BF16),
        compiler_params=_params(1),
    )(qc, kvc)


def _cross_bwd(qc, kvc, d_out):
    s = qc.shape[0]
    n_mem = kvc.shape[0]
    tq = _row_tile(s, 256)

    def body(q_ref, kv_ref, do_ref, dq_ref, dkv_ref):
        @pl.when(pl.program_id(0) == 0)
        def _():
            dkv_ref[...] = jnp.zeros_like(dkv_ref)

        for hd in range(MEM_HEADS):
            cols = slice(hd * HEAD_DIM, (hd + 1) * HEAD_DIM)
            vcols = slice(MEM_WIDTH + hd * HEAD_DIM, MEM_WIDTH + (hd + 1) * HEAD_DIM)
            qh, kh, p = _cross_probs(q_ref, kv_ref, hd)
            doh = do_ref[:, cols]
            dp = lax.dot_general(doh, kv_ref[:, vcols], (((1,), (1,)), ((), ())), preferred_element_type=F32)
            ds = (p * (dp - jnp.sum(p * dp, axis=1, keepdims=True)) * ATTN_SCALE).astype(BF16)
            dq_ref[:, cols] = jnp.dot(ds, kh, preferred_element_type=F32).astype(BF16)
            dkv_ref[:, cols] += lax.dot_general(ds, qh, (((0,), (0,)), ((), ())), preferred_element_type=F32)
            dkv_ref[:, vcols] += lax.dot_general(p.astype(BF16), doh, (((0,), (0,)), ((), ())), preferred_element_type=F32)

    qspec = pl.BlockSpec((tq, MEM_WIDTH), lambda i: (i, 0))
    kvspec = pl.BlockSpec((n_mem, 2 * MEM_WIDTH), lambda i: (0, 0))
    return pl.pallas_call(
        body, name="cross_bwd", grid=(s // tq,),
        in_specs=[qspec, kvspec, qspec],
        out_specs=[qspec, kvspec],
        out_shape=[jax.ShapeDtypeStruct((s, MEM_WIDTH), BF16), jax.ShapeDtypeStruct((n_mem, 2 * MEM_WIDTH), F32)],
        compiler_params=_params(1),
    )(qc, kvc, d_out)


def _swiglu_fwd(up, gate):
    sg = jax.nn.sigmoid(gate)
    silu = gate * sg
    return silu * up, up * (sg * (1.0 + gate * (1.0 - sg))), silu


def _swiglu_bwd(d_act, dact_dgate, dact_dup):
    return d_act * dact_dgate.astype(F32), d_act * dact_dup.astype(F32)


GATHER_GROUPS = {"in": ("w_in", "conv_w"), "mid": ("w_attn_out", "w_conv_out", "w_o", "w_cq", "w_ckv", "w_co"),
                 "gate": ("w_gate",), "up": ("w_up",), "down": ("w_down",)}


def _local_step(xs, mems, target, small, fetch, reduce):
    s, d = xs.shape
    w4 = {}
    cos_t, sin_t = _rope_tables(s)

    def near(group, done, then, after):
        waits = [("direct", group)] + ([("pass_near", done), ("pass_far", done)] if done else [])
        starts = [("forward", group), ("pass_near", group)] + [("direct", g) for g in then]
        tok = fetch.step("gather_near_" + group, waits, starts, after)
        if done:
            w4.update(fetch.arrays(done))
        return tok

    def far(group, then, after):
        return fetch.step("gather_far_" + group, [("forward", group)], [("pass_far", group)] + [("direct", g) for g in then], after)

    def last(group, after):
        tok = fetch.step("gather_done_" + group, [("pass_near", group), ("pass_far", group)], [], after)
        w4.update(fetch.arrays(group))
        return tok

    h = _rmsnorm(xs, small["g_mix"], "norm_mix")
    slots_filled = [a for g in ("gate", "up", "down") for a in fetch.arrays(g).values()]
    chip_x, chip_y = reduce.place[0] // 2, reduce.place[0] % 2
    own_block = jnp.stack([2 * chip_x + chip_y]).astype(jnp.int32)
    near_blocks = jnp.stack([2 * (1 - chip_x) + chip_y, 2 * chip_x + (1 - chip_y)]).astype(jnp.int32)
    far_block = jnp.stack([2 * (1 - chip_x) + (1 - chip_y)]).astype(jnp.int32)
    z = _matmul_column_blocks(h, fetch.arrays("in")["w_in"], own_block, None, tm=512, name="in_proj_own")
    tok = near("in", None, ["mid"], [z] + slots_filled)
    memn = _rmsnorm(mems, small["g_mem"], "norm_mem")
    tok = fetch.step("gather_near_done_in", [("pass_near", "in")], [], [tok, cos_t, sin_t, memn])
    z = _matmul_column_blocks(h, fetch.arrays("in")["w_in"], near_blocks, z, tm=1024, name="in_proj_near", after=tok)
    tok = far("in", [], z)
    tok = fetch.step("gather_done_in", [("pass_far", "in")], [], tok)
    w4.update(fetch.arrays("in"))
    z = _matmul_column_blocks(h, w4["w_in"], far_block, z, tm=1024, name="in_proj_far", after=tok)
    conv4 = w4["conv_w"]
    conv_w = conv4[:, :3, :].transpose(1, 0, 2).reshape(3, N_CHIPS * conv4.shape[2])
    c_in = w4["w_in"].shape[2]
    tok = near("mid", None, ["gate"], z)
    q_rot, k_rot, v_b = _rope_fwd(z, cos_t, sin_t)
    attn = _swa_fwd(q_rot, k_rot, v_b, small["sink"])
    co = _conv_fwd(z, conv_w)
    tok = far("mid", ["up"], attn)
    tok = last("mid", tok)
    w_o = w4["w_o"].reshape(-1, w4["w_o"].shape[-1])
    c_d = w4["w_attn_out"].shape[2]
    ya, yc = _matmul(attn, w4["w_attn_out"], mode="nn", tm=2048, tn=c_d, out_dtypes=[F32], name="branch_out_proj",
                     b_blocks=N_CHIPS, after=tok, second=(co, w4["w_conv_out"], F32))
    mix = _gate_fwd(z, small["b_gate"], ya, yc)
    x1 = _matmul(mix, w_o, mode="nn", tm=1024, tn=1024, out_dtypes=[F32], name="mix_out_proj", extras=[xs],
                 epilogue=_add_residual)
    tok = near("gate", None, ["down"], x1)
    w_cq = w4["w_cq"].reshape(-1, w4["w_cq"].shape[-1])
    w_ckv = w4["w_ckv"].reshape(-1, w4["w_ckv"].shape[-1])
    hc = _rmsnorm(x1, small["g_cross"], "norm_cross")
    qc = _matmul(hc, w_cq, mode="nn", tm=2048, tn=MEM_WIDTH, out_dtypes=[BF16], name="cross_q_proj", after=tok)
    kvc = _matmul(memn, w_ckv, mode="nn", tm=256, tn=2 * MEM_WIDTH, out_dtypes=[BF16], name="cross_kv_proj")
    oc = _cross_fwd(qc, kvc)
    tok = far("gate", [], oc)
    x2 = _matmul(oc, w4["w_co"], mode="nn", tm=2048, tn=c_d, out_dtypes=[F32], name="cross_out_proj",
                 extras=[x1], epilogue=_add_residual, b_blocks=N_CHIPS, after=tok)
    hf = _rmsnorm(x2, small["g_ffn"], "norm_ffn")
    tok = near("up", "gate", [], hf)
    c_ff = w4["w_gate"].shape[2]
    gate = _matmul(hf, w4["w_gate"], mode="nn", tm=1024, tn=c_ff, out_dtypes=[F32], name="ffn_gate_proj", b_blocks=N_CHIPS,
                   after=tok)
    tok = far("up", [], gate)
    tok = near("down", "up", [], tok)
    act, dact_dgate, dact_dup = _matmul(hf, w4["w_up"], mode="nn", tm=1024, tn=c_ff, out_dtypes=[BF16, BF16, BF16],
                                        name="ffn_up_proj", extras=[gate], epilogue=_swiglu_fwd, b_blocks=N_CHIPS, after=tok)
    tok = far("down", [], act)
    last("down", tok)
    w_down = w4["w_down"].reshape(-1, w4["w_down"].shape[-1])
    x3 = _matmul(act, w_down, mode="nn", tm=512, tn=512, out_dtypes=[F32], name="ffn_down_proj", extras=[x2],
                 epilogue=_add_residual)
    dx3, dx3b, sq, dg_final = _loss_head(x3, small["g_final"], target)

    da, du = _matmul(dx3b, w_down, mode="nt", tm=1024, tn=c_ff, out_dtypes=[BF16, BF16], name="ffn_down_bwd",
                     extras=[dact_dgate, dact_dup], epilogue=_swiglu_bwd)
    core = reduce.core
    ffn_shape = dict(row_sharded=False, tm=1024, tn=c_ff)
    g_down = _matmul(act, dx3b, mode="tn", tm=c_ff, tn=1024, out_dtypes=[BF16], name="ffn_down_wgrad")
    tok = reduce.add("down", {"w_down": g_down}, da)
    t_gate = _wgrad_half(hf, da, core, theirs=True, name="ffn_gate_wgrad_theirs", after=tok, **ffn_shape)
    tok = reduce.step("down", t_gate)
    t_up = _wgrad_half(hf, du, core, theirs=True, name="ffn_up_wgrad_theirs", after=tok, **ffn_shape)
    tok = reduce.send("ffn", {"w_gate": t_gate, "w_up": t_up}, dx3b)
    dhf = _matmul(da, w4["w_gate"], mode="nt", tm=512, tn=1024, out_dtypes=[F32], name="ffn_gate_bwd", b_blocks=N_CHIPS,
                  after=tok)
    got = reduce.received("ffn", dhf)
    p_gate = _wgrad_half(hf, da, core, theirs=False, name="ffn_gate_wgrad_mine", add=got["w_gate"], **ffn_shape)
    p_up = _wgrad_half(hf, du, core, theirs=False, name="ffn_up_wgrad_mine", add=got["w_up"], **ffn_shape)
    tok = reduce.add_parts("ffn", {"w_gate": p_gate, "w_up": p_up})
    dhf = _matmul(du, w4["w_up"], mode="nt", tm=512, tn=1024, out_dtypes=[F32], name="ffn_up_bwd", extras=[dhf],
                  epilogue=_add_residual, b_blocks=N_CHIPS, after=tok)
    tok = reduce.step("down", dhf)
    dx2, dx2b, dg_ffn = _rmsnorm_bwd(dhf, x2, small["g_ffn"], dx3, "norm_ffn_bwd")

    d_oc = _matmul(dx2b, w4["w_co"], mode="nt", tm=1024, tn=MEM_WIDTH, out_dtypes=[BF16], name="cross_out_bwd",
                   b_blocks=N_CHIPS, after=tok)
    g_co = _matmul(oc, dx2b, mode="tn", tm=MEM_WIDTH, tn=c_d, out_dtypes=[BF16], name="cross_out_wgrad", out_blocks=N_CHIPS)
    tok = reduce.step("down", g_co)
    dqc, dkvc = _cross_bwd(qc, kvc, d_oc)
    g_cq = _matmul(hc, dqc, mode="tn", tm=1024, tn=MEM_WIDTH, out_dtypes=[BF16], name="cross_q_wgrad", after=tok)
    dhc = _matmul(dqc, w_cq, mode="nt", tm=1024, tn=1024, out_dtypes=[F32], name="cross_q_bwd")
    g_ckv = _matmul(memn, dkvc, mode="tn", tm=1024, tn=2 * MEM_WIDTH, out_dtypes=[BF16], name="cross_kv_wgrad")
    dx1, dx1b, dg_cross = _rmsnorm_bwd(dhc, x1, small["g_cross"], dx2, "norm_cross_bwd")

    dmix = _matmul(dx1b, w_o, mode="nt", tm=1024, tn=1024, out_dtypes=[F32], name="mix_out_bwd")
    g_o = _matmul(mix, dx1b, mode="tn", tm=1024, tn=1024, out_dtypes=[BF16], name="mix_out_wgrad")
    dya, dyc, dz, db_a, db_c = _gate_bwd(z, small["b_gate"], ya, yc, dmix)
    d_attn, d_co = _matmul(dya, w4["w_attn_out"], mode="nt", tm=1024, tn=ATTN_WIDTH, out_dtypes=[BF16], name="branch_out_bwd",
                           b_blocks=N_CHIPS, second=(dyc, w4["w_conv_out"], F32))
    g_ao, g_cvo = _matmul(attn, dya, mode="tn", tm=ATTN_WIDTH, tn=c_d, out_dtypes=[BF16], name="branch_out_wgrad",
                          out_blocks=N_CHIPS, second=(co, dyc, BF16))
    tok = reduce.step("ffn", g_cvo)
    tok = reduce.add("mid", {"w_co": g_co, "w_cq": g_cq, "w_ckv": g_ckv, "w_o": g_o, "w_attn_out": g_ao, "w_conv_out": g_cvo}, tok)
    dz, d_conv_w = _conv_bwd(z, conv_w, d_co, dz)
    dq_rot, dk_rot, dv, dsink = _swa_bwd(q_rot, k_rot, v_b, d_attn, small["sink"])
    tok = reduce.step("mid", dq_rot)
    dz = _rope_bwd(dq_rot, dk_rot, dv, cos_t, sin_t, dz)
    in_shape = dict(row_sharded=False, tm=1024, tn=c_in)
    t_in = _wgrad_half(h, dz, core, theirs=True, name="in_proj_wgrad_theirs", after=tok, **in_shape)
    tok = reduce.send("in", {"w_in": t_in}, dk_rot)
    tok = reduce.step("ffn", tok, count=1)
    dmemn = _matmul(dkvc, w_ckv, mode="nt", tm=256, tn=1024, out_dtypes=[F32], name="cross_kv_bwd", after=tok)
    _, _, dg_mem = _rmsnorm_bwd(dmemn, mems, small["g_mem"], None, "norm_mem_bwd")
    got = reduce.received("in", dg_mem)
    p_in = _wgrad_half(h, dz, core, theirs=False, name="in_proj_wgrad_mine", add=got["w_in"], **in_shape)
    tok = reduce.add_parts("in", {"w_in": p_in})
    tok = reduce.step("ffn", tok)
    tok = reduce.step("mid", tok)
    dh = _matmul(dz, w4["w_in"], mode="nt", tm=512, tn=512, out_dtypes=[F32], name="in_proj_bwd", b_blocks=N_CHIPS,
                 after=tok)
    grad_x, _, dg_mix = _rmsnorm_bwd(dh, xs, small["g_mix"], dx1, "norm_mix_bwd")

    small_grads = {
        "g_mix": dg_mix, "sink": dsink[:, 0], "b_gate": jnp.concatenate([db_a, db_c], axis=1), "g_cross": dg_cross,
        "g_mem": dg_mem, "g_ffn": dg_ffn, "g_final": dg_final, "conv_w": d_conv_w,
    }
    return sq, grad_x, small_grads


def _pair_sum(g4s, ras, core, name):
    n = len(g4s)

    def body(c_ref, *refs):
        for k in range(n):
            g_ref, r_ref, o_ref = refs[2 * k], refs[2 * k + 1], refs[2 * n + k]
            o_ref[...] = (g_ref[...].astype(F32) + r_ref[...].astype(F32)).astype(BF16)

    in_specs, out_specs, out_shape, operands = [], [], [], []
    for g4, ra in zip(g4s, ras, strict=True):
        nb, rs, cs = g4.shape
        rh = rs // 2
        assert nb == N_CHIPS and ra.shape == (nb, rh, cs) and rh % BF16_SUBLANES == 0, (g4.shape, ra.shape)
        plain = pl.BlockSpec((None, rh, cs), lambda j, c: (j, 0, 0))
        in_specs += [pl.BlockSpec((None, rh, cs), lambda j, c: (j, c[0], 0)), plain]
        out_specs.append(plain)
        out_shape.append(jax.ShapeDtypeStruct((nb, rh, cs), BF16))
        operands += [g4, ra]
    return pl.pallas_call(
        body, name=name,
        grid_spec=pltpu.PrefetchScalarGridSpec(num_scalar_prefetch=1, grid=(N_CHIPS,), in_specs=in_specs, out_specs=out_specs),
        out_shape=out_shape,
        compiler_params=_params(1),
    )(core, *operands)


def _adamw_update(w, g, m, v):
    nm = ADAM_B1 * m + (1.0 - ADAM_B1) * g
    nv = ADAM_B2 * v + (1.0 - ADAM_B2) * (g * g)
    m_hat = nm / ADAM_C1
    v_hat = nv / ADAM_C2
    return -ADAM_LR * (m_hat / (jnp.sqrt(v_hat) + ADAM_EPS) + ADAM_WD * w), nm, nv


ADAMW_STEPS = 4
ADAMW_BYTES_PER_ELEMENT = 40


def _adamw_calls(names, shards):
    step_bytes = sum(shards[n].size // (2 * ADAMW_STEPS) * ADAMW_BYTES_PER_ELEMENT for n in names)
    return [list(names)] if 2 * step_bytes <= VMEM_LIMIT_BYTES * 3 // 4 else [[n] for n in names]


def _adamw_row_tiles(ws):
    for w in ws:
        assert w.shape[0] % (2 * ADAMW_STEPS * BF16_SUBLANES) == 0, w.shape
    return [w.shape[0] // (2 * ADAMW_STEPS) for w in ws]


def _adamw_own_half(ws, ms, vs, parts, rcs, place, name, after=None):
    n = len(ws)

    def body(p_ref, *refs):
        ins, outs = refs[:5 * n], refs[len(refs) - 5 * n:]
        for k in range(n):
            w_ref, m_ref, v_ref, own_ref, r_ref = ins[5 * k:5 * k + 5]
            gx_ref, g_ref, d_ref, nm_ref, nv_ref = outs[5 * k:5 * k + 5]
            g = own_ref[...].astype(F32)
            for j in range(r_ref.shape[0]):
                g = g + r_ref[j].astype(F32)
            gx_ref[...] = g
            g_ref[...] = g
            d_ref[...], nm_ref[...], nv_ref[...] = _adamw_update(w_ref[...], g, m_ref[...], v_ref[...])

    in_specs, out_specs, out_shape, operands = [], [], [], []
    for w, m, v, p, r, tr in zip(ws, ms, vs, parts, rcs, _adamw_row_tiles(ws), strict=True):
        cols = w.shape[1]
        mine = pl.BlockSpec((tr, cols), lambda i, pos: (pos[1] * ADAMW_STEPS + i, 0))
        in_specs += [mine, mine, mine, pl.BlockSpec((None, tr, cols), lambda i, pos: (pos[0], i, 0)),
                     pl.BlockSpec((r.shape[0], tr, cols), lambda i, pos: (0, i, 0))]
        out_specs += [mine] * 5
        out_shape += [jax.ShapeDtypeStruct(w.shape, F32)] * 5
        operands += [w, m, v, p, r]
    outs = pl.pallas_call(
        body, name=name,
        grid_spec=pltpu.PrefetchScalarGridSpec(
            num_scalar_prefetch=1, grid=(ADAMW_STEPS,),
            in_specs=in_specs + ([] if after is None else [ANY]), out_specs=out_specs),
        out_shape=out_shape,
        compiler_params=_params(1),
    )(place, *operands, *([] if after is None else [after]))
    return [tuple(outs[5 * k:5 * k + 5]) for k in range(n)]


def _adamw_other_half(ws, ms, vs, exchanged, halves, place, name, after=None):
    n = len(ws)

    def body(p_ref, *refs):
        ins, outs = refs[:8 * n], refs[len(refs) - 4 * n:]
        for k in range(n):
            w_ref, m_ref, v_ref, gx_ref = ins[8 * k:8 * k + 4]
            g_ref, d_ref, nm_ref, nv_ref = outs[4 * k:4 * k + 4]
            gv = gx_ref[...]
            g_ref[...] = gv
            d_ref[...], nm_ref[...], nv_ref[...] = _adamw_update(w_ref[...], gv, m_ref[...], v_ref[...])

    in_specs, out_specs, out_shape, operands, aliases = [], [], [], [], {}
    for k, (w, m, v, gx, half, tr) in enumerate(zip(ws, ms, vs, exchanged, halves, _adamw_row_tiles(ws), strict=True)):
        other = pl.BlockSpec((tr, w.shape[1]), lambda i, pos: ((1 - pos[1]) * ADAMW_STEPS + i, 0))
        in_specs += [other] * 4 + [ANY] * 4
        out_specs += [other] * 4
        out_shape += [jax.ShapeDtypeStruct(w.shape, F32)] * 4
        operands += [w, m, v, gx, *half]
        aliases.update({1 + 8 * k + 4 + j: 4 * k + j for j in range(4)})
    outs = pl.pallas_call(
        body, name=name,
        grid_spec=pltpu.PrefetchScalarGridSpec(
            num_scalar_prefetch=1, grid=(ADAMW_STEPS,),
            in_specs=in_specs + ([] if after is None else [ANY]), out_specs=out_specs),
        out_shape=out_shape,
        input_output_aliases=aliases,
        compiler_params=_params(1),
    )(place, *operands, *([] if after is None else [after]))
    return [tuple(outs[4 * k:4 * k + 4]) for k in range(n)]


def _cast_to_slot(w, place, dtype, name, after=None):
    rows, cols = w.shape
    tr = _row_tile(rows, 1024)

    def body(p_ref, w_ref, *rest):
        o_ref = rest[-1]
        o_ref[...] = w_ref[...].astype(dtype)

    return pl.pallas_call(
        body, name=name,
        grid_spec=pltpu.PrefetchScalarGridSpec(
            num_scalar_prefetch=1, grid=(rows // tr,),
            in_specs=[pl.BlockSpec((tr, cols), lambda i, p: (i, 0))] + ([] if after is None else [ANY]),
            out_specs=pl.BlockSpec((None, tr, cols), lambda i, p: (p[0], i, 0))),
        out_shape=jax.ShapeDtypeStruct((N_CHIPS, rows, cols), dtype),
        compiler_params=_params(1),
    )(place, w, *([] if after is None else [after]))


def _adamw_vectors(ws, gs, ms, vs, name):
    n = len(ws)

    def body(*refs):
        for k in range(n):
            w_ref, g_ref, m_ref, v_ref = refs[4 * k:4 * k + 4]
            d_ref, nm_ref, nv_ref = refs[4 * n + 3 * k:4 * n + 3 * k + 3]
            d_ref[...], nm_ref[...], nv_ref[...] = _adamw_update(w_ref[...], g_ref[...], m_ref[...], v_ref[...])

    operands = []
    for w, g, m, v in zip(ws, gs, ms, vs, strict=True):
        assert g.shape == w.shape == m.shape == v.shape, (w.shape, g.shape, m.shape, v.shape)
        operands += [a.reshape(-1, a.shape[-1]) for a in (w, g, m, v)]
    vm = pl.BlockSpec(memory_space=pltpu.VMEM)
    outs = pl.pallas_call(
        body, name=name, in_specs=[vm] * (4 * n), out_specs=[vm] * (3 * n),
        out_shape=[jax.ShapeDtypeStruct(a.shape, F32) for a in operands[::4] for _ in range(3)],
    )(*operands)
    return [tuple(o.reshape(w.shape) for o in outs[3 * k:3 * k + 3]) for k, w in enumerate(ws)]


def _mesh_pos():
    return lax.axis_index("x"), lax.axis_index("y"), lax.axis_index("c")


def _other_chips(x, y):
    return [(1 - x, y), (x, 1 - y), (1 - x, 1 - y)]


def _half_rows(ref, which):
    rh = ref.shape[-2] // 2
    return ref.at[pl.ds(which * rh, rh), :]


def _remote(src, dst, send_sems, recv_sems, sem, to):
    return pltpu.make_async_remote_copy(src_ref=src, dst_ref=dst, send_sem=send_sems.at[sem], recv_sem=recv_sems.at[sem],
                                        device_id=to, device_id_type=MESH)


HBM = pl.BlockSpec(memory_space=pltpu.HBM)
SEM = pl.BlockSpec(memory_space=pltpu.SEMAPHORE)
DATAFLOW_EFFECT = pltpu.SideEffectType.DATAFLOW_SIDE_EFFECTING


def _in_hbm(arrays):
    return [pltpu.with_memory_space_constraint(a, pltpu.HBM) for a in arrays]


def _hbm_like(arrays):
    return [pltpu.HBM(a.shape, a.dtype) for a in arrays]


GATHER_COPIES_PER_ARRAY = {"direct": 2, "forward": 2, "pass_near": 2, "pass_far": 1}


def _gather_copies(kind, refs, x, y, c):
    me, near_x, near_y, far = 2 * x + y, 2 * (1 - x) + y, 2 * x + (1 - y), 2 * (1 - x) + (1 - y)
    to_x, to_y, sibling = (1 - x, y, c), (x, 1 - y, c), (x, y, 1 - c)
    out = []
    for ref in refs:
        rh = ref.shape[1] // 2
        rq = rh // 2

        def half(chip, ref=ref, rh=rh):
            return ref.at[chip, pl.ds(c * rh, rh), :]

        def quarter(chip, q, ref=ref, rh=rh, rq=rq):
            return ref.at[chip, pl.ds(c * rh + q * rq, rq), :]

        if kind == "direct":
            out += [(half(me), half(me), to_x), (half(me), half(me), to_y)]
        elif kind == "forward":
            out += [(quarter(near_x, 0), quarter(near_x, 0), to_y), (quarter(near_y, 1), quarter(near_y, 1), to_x)]
        elif kind == "pass_near":
            out += [(half(near_x), half(near_x), sibling), (half(near_y), half(near_y), sibling)]
        else:
            assert kind == "pass_far"
            out += [(half(far), half(far), sibling)]
    return out


def _gather_step(name, bufs, waits, starts, after):
    nb, nw, ns = len(bufs), len(waits), len(starts)
    after = [] if after is None else list(after) if isinstance(after, (list, tuple)) else [after]
    n_after = len(after)

    def body(*refs):
        ins = refs[:nb]
        wait_sems = refs[nb:nb + 2 * nw]
        start_sems = refs[nb + 2 * nw + n_after:nb + 2 * nw + n_after + 2 * ns]
        token = refs[-1]
        x, y, c = _mesh_pos()
        for j, (kind, idxs, _, _) in enumerate(waits):
            for i, (s_ref, d_ref, to) in enumerate(_gather_copies(kind, [ins[t] for t in idxs], x, y, c)):
                came = _remote(s_ref, d_ref, wait_sems[2 * j], wait_sems[2 * j + 1], i, to)
                came.wait_recv()
                came.wait_send()
        for j, (kind, idxs) in enumerate(starts):
            for i, (s_ref, d_ref, to) in enumerate(_gather_copies(kind, [ins[t] for t in idxs], x, y, c)):
                _remote(s_ref, d_ref, start_sems[2 * j], start_sems[2 * j + 1], i, to).start()
        token[...] = jnp.zeros_like(token)

    sems = []
    for kind, idxs in starts:
        sems += [pltpu.SemaphoreType.DMA((GATHER_COPIES_PER_ARRAY[kind] * len(idxs),))] * 2
    operands = _in_hbm(bufs) + [sem for w in waits for sem in w[2:]] + after
    outs = pl.pallas_call(
        body, name=name,
        in_specs=[HBM] * nb + [SEM] * (2 * nw) + [ANY] * n_after,
        out_specs=[SEM] * (2 * ns) + [HBM] * nb + [pl.BlockSpec(memory_space=pltpu.VMEM)],
        out_shape=sems + _hbm_like(bufs) + [jax.ShapeDtypeStruct((8, 128), F32)],
        input_output_aliases={i: 2 * ns + i for i in range(nb)},
        compiler_params=pltpu.CompilerParams(has_side_effects=DATAFLOW_EFFECT),
    )(*operands)
    return outs[2 * ns:2 * ns + nb], [(outs[2 * j], outs[2 * j + 1]) for j in range(ns)], outs[-1]


class _Gather:
    def __init__(self, groups):
        self.groups = groups
        self.bufs = {}
        self.in_flight = {}

    def put(self, slotted):
        self.bufs.update(slotted)

    def step(self, name, waits, starts, after=None):
        names = []
        for _, group in list(waits) + list(starts):
            names += [n for n in self.groups[group] if n not in names]
        index = {n: i for i, n in enumerate(names)}

        def members(group):
            return [index[n] for n in self.groups[group]]

        wait_args = [(kind, members(group)) + self.in_flight.pop((kind, group)) for kind, group in waits]
        start_args = [(kind, members(group)) for kind, group in starts]
        bufs, sems, token = _gather_step(name, [self.bufs[n] for n in names], wait_args, start_args, after)
        self.bufs.update(zip(names, bufs))
        for (kind, group), pair in zip(starts, sems):
            self.in_flight[(kind, group)] = pair
        return token

    def arrays(self, group):
        return {n: self.bufs[n] for n in self.groups[group]}


def _sibling_halves_copies(srcs, dsts, x, y, c):
    out = []
    for s_ref, d_ref in zip(srcs, dsts, strict=True):
        rh = s_ref.shape[1] // 2
        out.append((s_ref.at[:, pl.ds((1 - c) * rh, rh), :], d_ref, (x, y, 1 - c)))
    return out


def _to_sibling_copies(srcs, dsts, x, y, c):
    return [(s_ref, d_ref, (x, y, 1 - c)) for s_ref, d_ref in zip(srcs, dsts, strict=True)]


def _chip_copies(srcs, dsts, x, y, c):
    out = []
    for s_ref, d_ref in zip(srcs, dsts, strict=True):
        for k, (px, py) in enumerate(_other_chips(x, y)):
            out.append((s_ref.at[2 * px + py], d_ref.at[k], (px, py, c)))
    return out


def _join_copies(srcs, dsts, x, y, c):
    out = []
    for s_ref in srcs:
        mine = _half_rows(s_ref, c)
        out.append((mine, mine, (x, y, 1 - c)))
    return out


def _exchange_start(copies_fn, n_copies, srcs, fresh, after, name):
    ns, nb = len(srcs), len(srcs) + len(fresh)

    def body(*refs):
        bufs, send, recv, token = refs[:nb], refs[nb + 1], refs[nb + 2], refs[-1]
        x, y, c = _mesh_pos()
        for i, (s_ref, d_ref, to) in enumerate(copies_fn(bufs[:ns], bufs[ns:] if fresh else bufs[:ns], x, y, c)):
            _remote(s_ref, d_ref, send, recv, i, to).start()
        token[...] = jnp.zeros_like(token)

    sems = [pltpu.SemaphoreType.DMA((n_copies,))] * 2
    outs = pl.pallas_call(
        body, name=name,
        in_specs=[HBM] * nb + [ANY], out_specs=[SEM, SEM] + [HBM] * nb + [pl.BlockSpec(memory_space=pltpu.VMEM)],
        out_shape=sems + _hbm_like(list(srcs) + list(fresh)) + [jax.ShapeDtypeStruct((8, 128), F32)],
        input_output_aliases={i: 2 + i for i in range(nb)},
        compiler_params=pltpu.CompilerParams(has_side_effects=DATAFLOW_EFFECT),
    )(*_in_hbm(list(srcs) + list(fresh)), after)
    return outs[0], outs[1], outs[2:2 + ns], outs[2 + ns:2 + nb], outs[-1]


def _exchange_done(copies_fn, srcs, fresh, send, recv, after, name):
    ns, nb = len(srcs), len(srcs) + len(fresh)

    def body(*refs):
        bufs, send_in, recv_in = refs[:nb], refs[nb], refs[nb + 1]
        x, y, c = _mesh_pos()
        for i, (s_ref, d_ref, to) in enumerate(copies_fn(bufs[:ns], bufs[ns:] if fresh else bufs[:ns], x, y, c)):
            came = _remote(s_ref, d_ref, send_in, recv_in, i, to)
            came.wait_send()
            came.wait_recv()

    outs = pl.pallas_call(
        body, name=name,
        in_specs=[HBM] * nb + [SEM, SEM, ANY], out_specs=[HBM] * nb,
        out_shape=_hbm_like(list(srcs) + list(fresh)),
        input_output_aliases={i: i for i in range(nb)},
        compiler_params=pltpu.CompilerParams(has_side_effects=DATAFLOW_EFFECT),
    )(*_in_hbm(list(srcs) + list(fresh)), send, recv, after)
    return outs[:ns], outs[ns:]


class _Reduce:
    def __init__(self, place, core, shards, mom_m, mom_v):
        self.place, self.core = place, core
        self.shards, self.mom_m, self.mom_v = shards, mom_m, mom_v
        self.state = {}
        self.results = {}

    def add(self, group, grads, after):
        names = list(grads)
        g4s = [g.reshape((N_CHIPS, -1, g.shape[-1])) if g.ndim == 2 else g for g in grads.values()]
        fresh = [lax.empty((N_CHIPS, g.shape[1] // 2, g.shape[2]), BF16) for g in g4s]
        send, recv, g4s, fresh, token = _exchange_start(_sibling_halves_copies, len(names), g4s, fresh, after,
                                                        "pair_start_" + group)
        self.state[group] = (0, names, send, recv, g4s, fresh)
        return token

    def send(self, group, theirs, after):
        names, srcs = list(theirs), list(theirs.values())
        fresh = [lax.empty(s.shape, BF16) for s in srcs]
        send, recv, srcs, fresh, token = _exchange_start(_to_sibling_copies, len(names), srcs, fresh, after, "pair_start_" + group)
        self.state[group] = ("sent", names, send, recv, srcs, fresh)
        return token

    def received(self, group, after):
        stage, names, send, recv, srcs, fresh = self.state.pop(group)
        assert stage == "sent"
        _, got = _exchange_done(_to_sibling_copies, srcs, fresh, send, recv, after, "pair_done_" + group)
        return dict(zip(names, got))

    def add_parts(self, group, parts):
        names, srcs = list(parts), list(parts.values())
        fresh = [lax.empty((N_CHIPS - 1,) + p.shape[1:], BF16) for p in srcs]
        send, recv, srcs, fresh, token = _exchange_start(_chip_copies, 3 * len(names), srcs, fresh, self.core, "chips_start_" + group)
        self.state[group] = (1, names, send, recv, srcs, fresh)
        return token

    def step(self, group, after, count=None):
        stage, names, send, recv, srcs, fresh = self.state[group]
        if stage == 0:
            g4s, ras = _exchange_done(_sibling_halves_copies, srcs, fresh, send, recv, after, "pair_done_" + group)
            parts = _pair_sum(g4s, ras, self.core, "pair_sum_" + group)
            fresh = [lax.empty((N_CHIPS - 1,) + p.shape[1:], BF16) for p in parts]
            send, recv, parts, fresh, token = _exchange_start(_chip_copies, 3 * len(names), parts, fresh, self.core,
                                                              "chips_start_" + group)
            self.state[group] = (1, names, send, recv, parts, fresh)
            return token
        if stage == 1:
            parts, rcs = _exchange_done(_chip_copies, srcs, fresh, send, recv, after, "chips_done_" + group)
            token = None
            for call in _adamw_calls(names, self.shards):
                at = [names.index(n) for n in call]
                done = _adamw_own_half(*[[held[n] for n in call] for held in (self.shards, self.mom_m, self.mom_v)],
                                       [parts[i] for i in at], [rcs[i] for i in at], self.place,
                                       "adamw_own_" + (call[0] if len(call) == 1 else group), after=token)
                self.results.update(zip(call, done))
                token = done[-1][2]
            wholes = [self.results[n][0] for n in names]
            send, recv, wholes, _, token = _exchange_start(_join_copies, len(names), wholes, [], token, "join_start_" + group)
            self.state[group] = (2, names, send, recv, wholes, [])
            return token
        assert stage in (2, 3)
        if stage == 2:
            srcs, _ = _exchange_done(_join_copies, srcs, [], send, recv, after, "join_done_" + group)
            after = None
        token = after
        count = len(names) if count is None else count
        for call in _adamw_calls(names[:count], self.shards):
            at = [names.index(n) for n in call]
            done = _adamw_other_half(*[[held[n] for n in call] for held in (self.shards, self.mom_m, self.mom_v)],
                                     [srcs[i] for i in at], [self.results[n][1:] for n in call], self.place,
                                     "adamw_other_" + (call[0] if len(call) == 1 else group), after=token)
            self.results.update(zip(call, done))
            token = done[-1][1]
        if count < len(names):
            self.state[group] = (3, names[count:], None, None, srcs[count:], [])
        else:
            del self.state[group]
        return token


N_DEV = 8


def _to_all_copies(srcs, dsts, x, y, c):
    out = []
    for r in range(1, N_DEV):
        fx, fy, fc = (r >> 2) & 1, (r >> 1) & 1, r & 1
        out.append((srcs[0], dsts[0].at[r - 1], (x + fx - 2 * x * fx, y + fy - 2 * y * fy, c + fc - 2 * c * fc)))
    return out


def _all_reduce_small_start(v, after):
    slots = lax.empty((N_DEV - 1,) + v.shape, v.dtype)
    send, recv, (v,), (slots,), token = _exchange_start(_to_all_copies, N_DEV - 1, [v], [slots], after, "small_grads_start")
    return (send, recv, v, slots), token


def _all_reduce_small_done(started, after):
    send, recv, v, slots = started
    (v,), (slots,) = _exchange_done(_to_all_copies, [v], [slots], send, recv, after, "small_grads_done")

    def body(v_ref, slots_ref, o_ref):
        x, y, c = _mesh_pos()
        me = 4 * x + 2 * y + c
        acc = None
        for i in range(N_DEV):
            r = jnp.bitwise_xor(me, i)
            part = jnp.where(r == 0, v_ref[...], slots_ref[jnp.maximum(r - 1, 0)])
            acc = part if acc is None else acc + part
        o_ref[...] = acc

    vm = pl.BlockSpec(memory_space=pltpu.VMEM)
    return pl.pallas_call(body, name="small_grads_sum", in_specs=[vm, vm], out_specs=vm,
                          out_shape=jax.ShapeDtypeStruct(v.shape, v.dtype))(v, slots)


MATRICES = ("w_in", "w_attn_out", "w_conv_out", "w_o", "w_cq", "w_ckv", "w_co", "w_gate", "w_up", "w_down")
VECTORS = ("g_mix", "b_gate", "g_cross", "g_mem", "g_ffn", "g_final", "conv_w", "sink")
WEIGHT_ORDER = ("g_mix", "w_in", "sink", "conv_w", "b_gate", "w_attn_out", "w_conv_out", "w_o", "g_cross", "g_mem", "w_cq",
                "w_ckv", "w_co", "g_ffn", "w_gate", "w_up", "w_down", "g_final")
CONV_PAD_ROWS = 32
SMALL_ROWS = 8


def _pack(pieces):
    flat = jnp.concatenate([p.reshape(-1) for p in pieces])
    lane_group = SMALL_ROWS * 128
    total = -(-flat.shape[0] // lane_group) * lane_group
    flat = jnp.pad(flat, (0, total - flat.shape[0]))
    return flat.reshape(SMALL_ROWS, total // SMALL_ROWS)


def _unpack(packed, pieces):
    flat = packed.reshape(-1)
    out, off = [], 0
    for p in pieces:
        out.append(flat[off:off + p.size].reshape(p.shape))
        off += p.size
    return out


def kernel(x, mem, g_mix, w_in, sink, conv_w, b_gate, w_attn_out, w_conv_out, w_o, g_cross, g_mem, w_cq, w_ckv, w_co, g_ffn, w_gate, w_up, w_down, g_final, loss_target, m_g_mix, m_w_in, m_sink, m_conv_w, m_b_gate, m_w_attn_out, m_w_conv_out, m_w_o, m_g_cross, m_g_mem, m_w_cq, m_w_ckv, m_w_co, m_g_ffn, m_w_gate, m_w_up, m_w_down, m_g_final, v_g_mix, v_w_in, v_sink, v_conv_w, v_b_gate, v_w_attn_out, v_w_conv_out, v_w_o, v_g_cross, v_g_mem, v_w_cq, v_w_ckv, v_w_co, v_g_ffn, v_w_gate, v_w_up, v_w_down, v_g_final):
    given = dict(g_mix=g_mix, w_in=w_in, sink=sink, conv_w=conv_w, b_gate=b_gate, w_attn_out=w_attn_out, w_conv_out=w_conv_out,
                 w_o=w_o, g_cross=g_cross, g_mem=g_mem, w_cq=w_cq, w_ckv=w_ckv, w_co=w_co, g_ffn=g_ffn, w_gate=w_gate, w_up=w_up,
                 w_down=w_down, g_final=g_final)
    mom_m = dict(g_mix=m_g_mix, w_in=m_w_in, sink=m_sink, conv_w=m_conv_w, b_gate=m_b_gate, w_attn_out=m_w_attn_out,
                 w_conv_out=m_w_conv_out, w_o=m_w_o, g_cross=m_g_cross, g_mem=m_g_mem, w_cq=m_w_cq, w_ckv=m_w_ckv, w_co=m_w_co,
                 g_ffn=m_g_ffn, w_gate=m_w_gate, w_up=m_w_up, w_down=m_w_down, g_final=m_g_final)
    mom_v = dict(g_mix=v_g_mix, w_in=v_w_in, sink=v_sink, conv_w=v_conv_w, b_gate=v_b_gate, w_attn_out=v_w_attn_out,
                 w_conv_out=v_w_conv_out, w_o=v_w_o, g_cross=v_g_cross, g_mem=v_g_mem, w_cq=v_w_cq, w_ckv=v_w_ckv, w_co=v_w_co,
                 g_ffn=v_g_ffn, w_gate=v_w_gate, w_up=v_w_up, w_down=v_w_down, g_final=v_g_final)
    xs, mems, target = x[0], mem[0], loss_target[0]
    d_model = xs.shape[1]
    chip = 2 * lax.axis_index("x") + lax.axis_index("y")
    core = jnp.reshape(lax.axis_index("c"), (1,)).astype(jnp.int32)
    place = jnp.stack([chip, lax.axis_index("c")]).astype(jnp.int32)

    shards = {n: given[n][0] for n in MATRICES}
    conv_cols = conv_w.shape[2]
    conv_pad = jnp.pad(conv_w[0], ((0, CONV_PAD_ROWS - conv_w.shape[1]), (0, 0)))
    fetch = _Gather(GATHER_GROUPS)
    first = {"w_in": _cast_to_slot(shards["w_in"], place, BF16, "to_slot_w_in"),
             "conv_w": _cast_to_slot(conv_pad, place, F32, "to_slot_conv_w")}
    fetch.put(first)
    tok = fetch.step("gather_start", [], [("direct", "in")])
    fetch.put({n: _cast_to_slot(shards[n], place, BF16, "to_slot_" + n, after=tok) for n in MATRICES if n != "w_in"})
    small = {n: given[n] for n in ("g_mix", "b_gate", "g_cross", "g_mem", "g_ffn")}
    small["g_final"] = g_final[None]
    small["sink"] = sink[0]

    reduce = _Reduce(place, core, shards, {n: mom_m[n][0] for n in MATRICES}, {n: mom_v[n][0] for n in MATRICES})
    sq, grad_x, small_grads = _local_step(xs, mems, target, small, fetch, reduce)

    loss_part = 0.5 * sq[0:1, 0:1] / d_model
    pieces = [small_grads[n] for n in VECTORS] + [loss_part]
    started, tok = _all_reduce_small_start(_pack(pieces), core)
    tok = reduce.step("mid", tok)
    tok = reduce.step("in", tok)
    summed = _unpack(_all_reduce_small_done(started, tok), pieces)
    loss = summed[-1][0, 0]
    small_sum = dict(zip(VECTORS, summed[:-1]))
    small_sum["conv_w"] = lax.dynamic_slice_in_dim(small_sum["conv_w"], chip * conv_cols, conv_cols, axis=1)

    grad_out, delta, new_m, new_v = {}, {}, {}, {}
    for n in VECTORS:
        grad_out[n] = small_sum[n].reshape(given[n].shape)
    stepped = _adamw_vectors(*[[held[n] for n in VECTORS] for held in (given, grad_out, mom_m, mom_v)], "adamw_small")
    for n, (d, nm, nv) in zip(VECTORS, stepped):
        delta[n], new_m[n], new_v[n] = d, nm, nv
    reduce.step("in", stepped[-1][0])
    for n in MATRICES:
        g, d, nm, nv = reduce.results[n]
        grad_out[n], delta[n], new_m[n], new_v[n] = g[None], d[None], nm[None], nv[None]

    return (loss, grad_x[None], *[grad_out[n] for n in WEIGHT_ORDER], *[delta[n] for n in WEIGHT_ORDER],
            *[new_m[n] for n in WEIGHT_ORDER], *[new_v[n] for n in WEIGHT_ORDER])
```
